```python
import jax, jax.numpy as jnp
from jax import lax
import numpy as np

D_MODEL = 1024
BATCH = 8
SEQ = 4096
DEPTH = 1

CTX_LEN = 256
GRID_W = 64
CONV_WIDTH = D_MODEL
CONV_K = 3
RET_HEADS = 8
RET_DV = D_MODEL // RET_HEADS
RET_DK = RET_DV // 2
RET_QK_WIDTH = RET_HEADS * RET_DK
RET_V_WIDTH = RET_HEADS * RET_DV
CHUNK = 128
ROPE_BASE = 10000.0
EPS = 1e-6
IN_WIDTHS = (CONV_WIDTH, CONV_WIDTH, CONV_WIDTH, CONV_WIDTH,
             RET_QK_WIDTH, RET_QK_WIDTH, RET_V_WIDTH, RET_V_WIDTH, D_MODEL, D_MODEL)
SPLIT_POINTS = tuple(int(s) for s in np.cumsum(IN_WIDTHS)[:-1])
IN_WIDTH = int(sum(IN_WIDTHS))

kernel_name = "hybrid_conv_retention_dit_block"


def rmsnorm(x, w):
    x32 = x.astype(jnp.float32)
    y = x32 * lax.rsqrt(jnp.mean(x32 * x32, axis=-1, keepdims=True) + EPS)
    return (y * w.astype(jnp.float32)).astype(x.dtype)


def dwconv_centred(u, w, b):
    L = u.shape[1]
    pad = CONV_K // 2
    up = jnp.pad(u, ((0, 0), (pad, pad), (0, 0)))
    return sum(up[:, j:j + L] * w[j] for j in range(CONV_K)) + b


def to_heads(t, d):
    B_, L, _ = t.shape
    return t.reshape(B_, L, RET_HEADS, d).transpose(0, 2, 1, 3)


def rope2d(t):
    L = t.shape[2]
    rows = L // GRID_W
    row = jnp.repeat(jnp.arange(rows), GRID_W).astype(jnp.float32)
    col = jnp.tile(jnp.arange(GRID_W), rows).astype(jnp.float32)
    nf = RET_DK // 4
    inv = ROPE_BASE ** (-jnp.arange(nf, dtype=jnp.float32) / nf)
    ang = jnp.concatenate([row[:, None] * inv, col[:, None] * inv], axis=-1)
    cos = jnp.cos(ang).astype(t.dtype)
    sin = jnp.sin(ang).astype(t.dtype)
    half = RET_DK // 2
    t1, t2 = t[..., :half], t[..., half:]
    return jnp.concatenate([t1 * cos - t2 * sin, t1 * sin + t2 * cos], axis=-1)


def retention_scan(q, k, v, log_gamma, s0):
    B_, H, L, dk = q.shape
    dv = v.shape[-1]
    n = L // CHUNK
    qc = q.astype(jnp.float32).reshape(B_, H, n, CHUNK, dk)
    kc = k.astype(jnp.float32).reshape(B_, H, n, CHUNK, dk)
    vc = v.astype(jnp.float32).reshape(B_, H, n, CHUNK, dv)
    idx = jnp.arange(CHUNK, dtype=jnp.float32)
    diff = idx[:, None] - idx[None, :]
    dmask = jnp.where(diff >= 0, jnp.exp(log_gamma[:, None, None] * jnp.maximum(diff, 0.0)), 0.0)
    scores = jnp.einsum('bhnid,bhnjd->bhnij', qc, kc) * dmask[None, :, None]
    inner = jnp.einsum('bhnij,bhnjv->bhniv', scores, vc)
    k_dec = jnp.exp(log_gamma[:, None] * (CHUNK - 1 - idx))
    kv = jnp.einsum('bhnjd,hj,bhnjv->bhndv', kc, k_dec, vc)
    chunk_decay = jnp.exp(log_gamma * CHUNK)[None, :, None, None]

    def step(s, kv_c):
        return chunk_decay * s + kv_c, s

    _, s_prev = lax.scan(step, s0.astype(jnp.float32), jnp.moveaxis(kv, 2, 0))
    s_prev = jnp.moveaxis(s_prev, 0, 2)
    q_dec = jnp.exp(log_gamma[:, None] * (idx + 1.0))
    cross = jnp.einsum('bhnid,hi,bhndv->bhniv', qc, q_dec, s_prev)
    return (inner + cross).reshape(B_, H, L, dv)


def bidir_retention(q, k, v, lg_f, lg_b, s0_f, s0_b):
    o_f = retention_scan(q, k, v, lg_f, s0_f)
    flip = lambda t: t[:, :, ::-1]
    o_b = retention_scan(flip(q), flip(k), flip(v), lg_b, s0_b)
    return o_f + flip(o_b)


def ctx_final_states(k, v, lg_f, lg_b):
    Lc = k.shape[2]
    m = jnp.arange(Lc, dtype=jnp.float32)
    k32 = k.astype(jnp.float32)
    v32 = v.astype(jnp.float32)
    dec_f = jnp.exp(lg_f[:, None] * (Lc - 1 - m))
    dec_b = jnp.exp(lg_b[:, None] * m)
    s_f = jnp.einsum('bhmd,hm,bhmv->bhdv', k32, dec_f, v32)
    s_b = jnp.einsum('bhmd,hm,bhmv->bhdv', k32, dec_b, v32)
    return s_f, s_b


def retention_groupnorm(ret, gn_w, dtype):
    mu = jnp.mean(ret, axis=-1, keepdims=True)
    var = jnp.mean(jnp.square(ret - mu), axis=-1, keepdims=True)
    rn = (ret - mu) * lax.rsqrt(var + EPS)
    B_, H, L, dv = rn.shape
    rn = rn.transpose(0, 2, 1, 3).reshape(B_, L, H * dv)
    return (rn * gn_w.astype(jnp.float32)).astype(dtype)


def split_heads(proj, rotary):
    h, bg, cg, za, q, k, v, zb, ga, gb = jnp.split(proj, SPLIT_POINTS, axis=-1)
    q = to_heads(q, RET_DK)
    k = to_heads(k, RET_DK) * (RET_DK ** -0.5)
    v = to_heads(v, RET_DV)
    if rotary:
        q, k = rope2d(q), rope2d(k)
    return (h, bg, cg, za, zb, ga, gb), (q, k, v)


def merge_branches(parts, ret, conv_w, conv_b, gn_w, w_a, w_b, w_out):
    h, bg, cg, za, zb, ga, gb = parts
    conv_out = dwconv_centred(cg * h, conv_w, conv_b)
    y_a = (jax.nn.silu(za) * bg * conv_out) @ w_a
    ret_n = retention_groupnorm(ret, gn_w, h.dtype)
    y_b = (jax.nn.silu(zb) * ret_n) @ w_b
    return (jax.nn.sigmoid(ga) * y_a + jax.nn.sigmoid(gb) * y_b) @ w_out


def _fwd_setup_inputs(seed: int = 0) -> dict:
    key = jax.random.key(seed)
    ks = jax.random.split(key, 20)
    f32 = jnp.float32
    nrm = lambda k, shape, s: jax.random.normal(k, shape, f32) * s
    base_gamma = 1.0 - 2.0 ** (-5.0 - np.arange(RET_HEADS, dtype=np.float32))
    base_logit = jnp.asarray(np.log(base_gamma / (1.0 - base_gamma)), f32)
    decay_logit = base_logit[None, None, :] + nrm(ks[10], (DEPTH, 2, RET_HEADS), 0.1)
    return {
        "x": nrm(ks[0], (BATCH, SEQ, D_MODEL), 1.0),
        "c": nrm(ks[1], (BATCH, D_MODEL), 1.0),
        "ctx": nrm(ks[2], (BATCH, CTX_LEN, D_MODEL), 1.0),
        "c_ctx": nrm(ks[3], (D_MODEL,), 1.0),
        "norm_w": 1.0 + nrm(ks[4], (DEPTH, D_MODEL), 0.02),
        "ada_w": nrm(ks[5], (DEPTH, D_MODEL, 3 * D_MODEL), 0.5 * D_MODEL ** -0.5),
        "ada_b": nrm(ks[6], (DEPTH, 3 * D_MODEL), 0.02),
        "w_in": nrm(ks[7], (DEPTH, D_MODEL, IN_WIDTH), D_MODEL ** -0.5),
        "conv_w": nrm(ks[8], (DEPTH, CONV_K, CONV_WIDTH), CONV_K ** -0.5),
        "conv_b": nrm(ks[9], (DEPTH, CONV_WIDTH), 0.02),
        "decay_logit": decay_logit,
        "gn_w": 1.0 + nrm(ks[11], (DEPTH, RET_V_WIDTH), 0.02),
        "w_a": nrm(ks[12], (DEPTH, CONV_WIDTH, D_MODEL), CONV_WIDTH ** -0.5),
        "w_b": nrm(ks[13], (DEPTH, RET_V_WIDTH, D_MODEL), RET_V_WIDTH ** -0.5),
        "w_out": nrm(ks[14], (DEPTH, D_MODEL, D_MODEL), D_MODEL ** -0.5),
        "final_norm_w": 1.0 + nrm(ks[15], (D_MODEL,), 0.02),
    }


def _fwd_reference(x, c, ctx, c_ctx, norm_w, ada_w, ada_b, w_in, conv_w, conv_b,
              decay_logit, gn_w, w_a, w_b, w_out, final_norm_w):
    for l in range(DEPTH):
        mod_x = jax.nn.silu(c) @ ada_w[l] + ada_b[l]
        sh_x, sc_x, g_x = jnp.split(mod_x[:, None, :], 3, axis=-1)
        mod_c = jax.nn.silu(c_ctx) @ ada_w[l] + ada_b[l]
        sh_c, sc_c, g_c = jnp.split(mod_c, 3, axis=-1)
        xm = rmsnorm(x, norm_w[l]) * (1.0 + sc_x) + sh_x
        cm = rmsnorm(ctx, norm_w[l]) * (1.0 + sc_c) + sh_c
        parts_x, (qx, kx, vx) = split_heads(xm @ w_in[l], rotary=True)
        parts_c, (qc, kc, vc) = split_heads(cm @ w_in[l], rotary=False)
        lg = jax.nn.log_sigmoid(decay_logit[l].astype(jnp.float32))
        s_f, s_b = ctx_final_states(kc, vc, lg[0], lg[1])
        ret_x = bidir_retention(qx, kx, vx, lg[0], lg[1], s_f, s_b)
        y_x = merge_branches(parts_x, ret_x, conv_w[l], conv_b[l], gn_w[l], w_a[l], w_b[l], w_out[l])
        if l < DEPTH - 1:
            zeros = jnp.zeros_like(s_f)
            ret_c = bidir_retention(qc, kc, vc, lg[0], lg[1], zeros, zeros)
            y_c = merge_branches(parts_c, ret_c, conv_w[l], conv_b[l], gn_w[l], w_a[l], w_b[l], w_out[l])
            ctx = ctx + g_c * y_c
        x = x + g_x * y_x
    return rmsnorm(x, final_norm_w)


import jax as _jax
import jax.numpy as _jnp

TWIN_FORMAT = 'train_step'
FWD_PARAMS = ['x', 'c', 'ctx', 'c_ctx', 'norm_w', 'ada_w', 'ada_b', 'w_in', 'conv_w', 'conv_b', 'decay_logit', 'gn_w', 'w_a', 'w_b', 'w_out', 'final_norm_w']
TWIN_WEIGHTS = ['c_ctx', 'norm_w', 'ada_w', 'ada_b', 'w_in', 'conv_w', 'conv_b', 'decay_logit', 'gn_w', 'w_a', 'w_b', 'w_out', 'final_norm_w']
TWIN_DIFF_INPUT = 'x'
TWIN_INPUTS = ['x', 'c', 'ctx', 'c_ctx', 'norm_w', 'ada_w', 'ada_b', 'w_in', 'conv_w', 'conv_b', 'decay_logit', 'gn_w', 'w_a', 'w_b', 'w_out', 'final_norm_w', 'loss_target', 'm_c_ctx', 'm_norm_w', 'm_ada_w', 'm_ada_b', 'm_w_in', 'm_conv_w', 'm_conv_b', 'm_decay_logit', 'm_gn_w', 'm_w_a', 'm_w_b', 'm_w_out', 'm_final_norm_w', 'v_c_ctx', 'v_norm_w', 'v_ada_w', 'v_ada_b', 'v_w_in', 'v_conv_w', 'v_conv_b', 'v_decay_logit', 'v_gn_w', 'v_w_a', 'v_w_b', 'v_w_out', 'v_final_norm_w']
TWIN_OUTPUTS = ['loss', 'grad_x', 'grad_c_ctx', 'grad_norm_w', 'grad_ada_w', 'grad_ada_b', 'grad_w_in', 'grad_conv_w', 'grad_conv_b', 'grad_decay_logit', 'grad_gn_w', 'grad_w_a', 'grad_w_b', 'grad_w_out', 'grad_final_norm_w', 'delta_c_ctx', 'delta_norm_w', 'delta_ada_w', 'delta_ada_b', 'delta_w_in', 'delta_conv_w', 'delta_conv_b', 'delta_decay_logit', 'delta_gn_w', 'delta_w_a', 'delta_w_b', 'delta_w_out', 'delta_final_norm_w', 'new_m_c_ctx', 'new_m_norm_w', 'new_m_ada_w', 'new_m_ada_b', 'new_m_w_in', 'new_m_conv_w', 'new_m_conv_b', 'new_m_decay_logit', 'new_m_gn_w', 'new_m_w_a', 'new_m_w_b', 'new_m_w_out', 'new_m_final_norm_w', 'new_v_c_ctx', 'new_v_norm_w', 'new_v_ada_w', 'new_v_ada_b', 'new_v_w_in', 'new_v_conv_w', 'new_v_conv_b', 'new_v_decay_logit', 'new_v_gn_w', 'new_v_w_a', 'new_v_w_b', 'new_v_w_out', 'new_v_final_norm_w']
TWIN_LEAF_KINDS = {'loss': 'loss', 'grad_x': 'grad_x', 'grad_c_ctx': 'grad_w', 'grad_norm_w': 'grad_w', 'grad_ada_w': 'grad_w', 'grad_ada_b': 'grad_w', 'grad_w_in': 'grad_w', 'grad_conv_w': 'grad_w', 'grad_conv_b': 'grad_w', 'grad_decay_logit': 'grad_w', 'grad_gn_w': 'grad_w', 'grad_w_a': 'grad_w', 'grad_w_b': 'grad_w', 'grad_w_out': 'grad_w', 'grad_final_norm_w': 'grad_w', 'delta_c_ctx': 'delta_w', 'delta_norm_w': 'delta_w', 'delta_ada_w': 'delta_w', 'delta_ada_b': 'delta_w', 'delta_w_in': 'delta_w', 'delta_conv_w': 'delta_w', 'delta_conv_b': 'delta_w', 'delta_decay_logit': 'delta_w', 'delta_gn_w': 'delta_w', 'delta_w_a': 'delta_w', 'delta_w_b': 'delta_w', 'delta_w_out': 'delta_w', 'delta_final_norm_w': 'delta_w', 'new_m_c_ctx': 'new_m', 'new_m_norm_w': 'new_m', 'new_m_ada_w': 'new_m', 'new_m_ada_b': 'new_m', 'new_m_w_in': 'new_m', 'new_m_conv_w': 'new_m', 'new_m_conv_b': 'new_m', 'new_m_decay_logit': 'new_m', 'new_m_gn_w': 'new_m', 'new_m_w_a': 'new_m', 'new_m_w_b': 'new_m', 'new_m_w_out': 'new_m', 'new_m_final_norm_w': 'new_m', 'new_v_c_ctx': 'new_v', 'new_v_norm_w': 'new_v', 'new_v_ada_w': 'new_v', 'new_v_ada_b': 'new_v', 'new_v_w_in': 'new_v', 'new_v_conv_w': 'new_v', 'new_v_conv_b': 'new_v', 'new_v_decay_logit': 'new_v', 'new_v_gn_w': 'new_v', 'new_v_w_a': 'new_v', 'new_v_w_b': 'new_v', 'new_v_w_out': 'new_v', 'new_v_final_norm_w': 'new_v'}


def _forward(args):
    return _fwd_reference(*[args[k] for k in FWD_PARAMS])


def _output_shape():
    def fwd():
        inp = _fwd_setup_inputs(0)
        return _fwd_reference(*[inp[k] for k in FWD_PARAMS])
    out = _jax.eval_shape(fwd)
    return out.shape, out.dtype

N_MICROBATCH = 1
ADAM_LR = 0.001
ADAM_B1 = 0.9
ADAM_B2 = 0.999
ADAM_EPS = 1e-08
ADAM_WD = 0.01
ADAM_STEP = 10
PER_EXAMPLE_BATCH_AXIS = {'x': 0, 'c': 0, 'ctx': 0, 'loss_target': 0}
SHARED_INPUTS = []
_WEIGHT_DTYPES = {'c_ctx': _jnp.float32, 'norm_w': _jnp.float32, 'ada_w': _jnp.float32, 'ada_b': _jnp.float32, 'w_in': _jnp.float32, 'conv_w': _jnp.float32, 'conv_b': _jnp.float32, 'decay_logit': _jnp.float32, 'gn_w': _jnp.float32, 'w_a': _jnp.float32, 'w_b': _jnp.float32, 'w_out': _jnp.float32, 'final_norm_w': _jnp.float32}
MOMENT_SCALE = {'c_ctx': 1.409210e-02, 'norm_w': 5.906941e-02, 'ada_w': 5.134411e-02, 'ada_b': 8.444308e-02, 'w_in': 2.125826e-02, 'conv_w': 2.455606e-02, 'conv_b': 2.063886e-02, 'decay_logit': 5.089706e-02, 'gn_w': 2.030872e-02, 'w_a': 2.376215e-02, 'w_b': 1.876660e-02, 'w_out': 3.020748e-02, 'final_norm_w': 3.199398e+01}


def _to_microbatches(a, axis):
    t = _jnp.moveaxis(a, axis, 0)
    t = t.reshape((N_MICROBATCH, t.shape[0] // N_MICROBATCH) + t.shape[1:])
    return _jnp.moveaxis(t, 1, axis + 1)


def setup_inputs(seed: int = 0) -> dict:
    inp = _fwd_setup_inputs(seed)
    key = _jax.random.fold_in(_jax.random.key(seed), 7919)
    shape, _ = _output_shape()
    out = dict(inp)
    out["loss_target"] = _jax.random.normal(_jax.random.fold_in(key, 0), shape, _jnp.float32)
    for i, name in enumerate(TWIN_WEIGHTS):
        w = inp[name].astype(_jnp.float32)
        if MOMENT_SCALE is None:
            s = _jnp.sqrt(_jnp.mean(_jnp.square(w)) + 1e-30)
        else:
            s = MOMENT_SCALE[name]
        km, kv = _jax.random.split(_jax.random.fold_in(key, i + 1))
        out[name] = w
        out["m_" + name] = s * _jax.random.normal(km, w.shape, _jnp.float32)
        out["v_" + name] = (s * s) * _jax.random.uniform(kv, w.shape, _jnp.float32, 0.5, 1.5)
    if N_MICROBATCH > 1:
        for name, axis in PER_EXAMPLE_BATCH_AXIS.items():
            out[name] = _to_microbatches(out[name], axis)
    return {'x': out['x'], 'c': out['c'], 'ctx': out['ctx'], 'c_ctx': out['c_ctx'], 'norm_w': out['norm_w'], 'ada_w': out['ada_w'], 'ada_b': out['ada_b'], 'w_in': out['w_in'], 'conv_w': out['conv_w'], 'conv_b': out['conv_b'], 'decay_logit': out['decay_logit'], 'gn_w': out['gn_w'], 'w_a': out['w_a'], 'w_b': out['w_b'], 'w_out': out['w_out'], 'final_norm_w': out['final_norm_w'], 'loss_target': out['loss_target'], 'm_c_ctx': out['m_c_ctx'], 'm_norm_w': out['m_norm_w'], 'm_ada_w': out['m_ada_w'], 'm_ada_b': out['m_ada_b'], 'm_w_in': out['m_w_in'], 'm_conv_w': out['m_conv_w'], 'm_conv_b': out['m_conv_b'], 'm_decay_logit': out['m_decay_logit'], 'm_gn_w': out['m_gn_w'], 'm_w_a': out['m_w_a'], 'm_w_b': out['m_w_b'], 'm_w_out': out['m_w_out'], 'm_final_norm_w': out['m_final_norm_w'], 'v_c_ctx': out['v_c_ctx'], 'v_norm_w': out['v_norm_w'], 'v_ada_w': out['v_ada_w'], 'v_ada_b': out['v_ada_b'], 'v_w_in': out['v_w_in'], 'v_conv_w': out['v_conv_w'], 'v_conv_b': out['v_conv_b'], 'v_decay_logit': out['v_decay_logit'], 'v_gn_w': out['v_gn_w'], 'v_w_a': out['v_w_a'], 'v_w_b': out['v_w_b'], 'v_w_out': out['v_w_out'], 'v_final_norm_w': out['v_final_norm_w']}


def _loss(weights, diff, rest, loss_target):
    with _jax.named_scope("forward"):
        args = {**rest, TWIN_DIFF_INPUT: diff, **{k: w.astype(_WEIGHT_DTYPES[k]) for k, w in weights.items()}}
        y = _forward(args)
    with _jax.named_scope("loss_head"):
        err = _jnp.square(y.astype(_jnp.float32) - loss_target)
        return 0.5 * _jnp.sum(_jnp.mean(err, axis=-1)) if err.ndim else 0.5 * err


def _adamw(w, g, m, v):
    m = ADAM_B1 * m + (1.0 - ADAM_B1) * g
    v = ADAM_B2 * v + (1.0 - ADAM_B2) * _jnp.square(g)
    m_hat = m / (1.0 - ADAM_B1 ** ADAM_STEP)
    v_hat = v / (1.0 - ADAM_B2 ** ADAM_STEP)
    delta = -ADAM_LR * (m_hat / (_jnp.sqrt(v_hat) + ADAM_EPS) + ADAM_WD * w)
    return delta, m, v


def reference(x, c, ctx, c_ctx, norm_w, ada_w, ada_b, w_in, conv_w, conv_b, decay_logit, gn_w, w_a, w_b, w_out, final_norm_w, loss_target, m_c_ctx, m_norm_w, m_ada_w, m_ada_b, m_w_in, m_conv_w, m_conv_b, m_decay_logit, m_gn_w, m_w_a, m_w_b, m_w_out, m_final_norm_w, v_c_ctx, v_norm_w, v_ada_w, v_ada_b, v_w_in, v_conv_w, v_conv_b, v_decay_logit, v_gn_w, v_w_a, v_w_b, v_w_out, v_final_norm_w):
    given = dict(x=x, c=c, ctx=ctx, c_ctx=c_ctx, norm_w=norm_w, ada_w=ada_w, ada_b=ada_b, w_in=w_in, conv_w=conv_w, conv_b=conv_b, decay_logit=decay_logit, gn_w=gn_w, w_a=w_a, w_b=w_b, w_out=w_out, final_norm_w=final_norm_w, loss_target=loss_target, m_c_ctx=m_c_ctx, m_norm_w=m_norm_w, m_ada_w=m_ada_w, m_ada_b=m_ada_b, m_w_in=m_w_in, m_conv_w=m_conv_w, m_conv_b=m_conv_b, m_decay_logit=m_decay_logit, m_gn_w=m_gn_w, m_w_a=m_w_a, m_w_b=m_w_b, m_w_out=m_w_out, m_final_norm_w=m_final_norm_w, v_c_ctx=v_c_ctx, v_norm_w=v_norm_w, v_ada_w=v_ada_w, v_ada_b=v_ada_b, v_w_in=v_w_in, v_conv_w=v_conv_w, v_conv_b=v_conv_b, v_decay_logit=v_decay_logit, v_gn_w=v_gn_w, v_w_a=v_w_a, v_w_b=v_w_b, v_w_out=v_w_out, v_final_norm_w=v_final_norm_w)
    weights = {n: given[n] for n in TWIN_WEIGHTS}
    shared = {n: given[n] for n in SHARED_INPUTS}
    per_example = {n: given[n] for n in ['x', 'c', 'ctx']}
    grad_fn = _jax.value_and_grad(_loss, argnums=(0, 1))

    def one_microbatch(ex, loss_target):
        ex = dict(ex)
        diff = ex.pop(TWIN_DIFF_INPUT)
        return grad_fn(weights, diff, {**shared, **ex}, loss_target)

    if N_MICROBATCH == 1:
        loss, (grad_w, grad_x) = one_microbatch(per_example, given["loss_target"])
    else:
        def body(carry, xs):
            loss_sum, grad_sum = carry
            l_k, (gw_k, gx_k) = one_microbatch(xs[0], xs[1])
            with _jax.named_scope("update"):
                return (loss_sum + l_k, _jax.tree.map(_jnp.add, grad_sum, gw_k)), gx_k

        init = (_jnp.zeros((), _jnp.float32), _jax.tree.map(_jnp.zeros_like, weights))
        (loss, grad_w), grad_x = _jax.lax.scan(body, init, (per_example, given["loss_target"]))
    with _jax.named_scope("update"):
        delta_w, new_m, new_v = {}, {}, {}
        for n in TWIN_WEIGHTS:
            delta_w[n], new_m[n], new_v[n] = _adamw(weights[n], grad_w[n], given["m_" + n], given["v_" + n])
    return (loss, grad_x, *[grad_w[n] for n in TWIN_WEIGHTS], *[delta_w[n] for n in TWIN_WEIGHTS],
            *[new_m[n] for n in TWIN_WEIGHTS], *[new_v[n] for n in TWIN_WEIGHTS])
```

```python
import functools

import jax
import jax.numpy as jnp
from jax import lax
from jax.experimental import pallas as pl
from jax.experimental.pallas import tpu as pltpu

F32 = jnp.float32
BF16 = jnp.bfloat16
MESH = pl.DeviceIdType.MESH

CHUNK = 128
DV = 128
DK = 64
GRID_W = 64
ROPE_BASE = 10000.0
EPS = 1e-6
K_SCALE = DK ** -0.5
N_SHARD = 4
N_DEV = 8

ADAM_LR = 0.001
ADAM_B1 = 0.9
ADAM_B2 = 0.999
ADAM_EPS = 1e-08
ADAM_WD = 0.01
ADAM_STEP = 10

VMEM_LIMIT = 56 * 1024 * 1024


def _cparams(sem=None, vmem=None):
    kw = {}
    if sem is not None:
        kw["dimension_semantics"] = sem
    if vmem is not None:
        kw["vmem_limit_bytes"] = vmem
    return pltpu.CompilerParams(**kw)


def _dot(a, b):
    return jnp.dot(a, b, preferred_element_type=F32)


def _dot_nt(a, b):
    return lax.dot_general(a, b, (((1,), (1,)), ((), ())), preferred_element_type=F32)


def _dot_tn(a, b):
    return lax.dot_general(a, b, (((0,), (0,)), ((), ())), preferred_element_type=F32)


def _sigmoid(x):
    return 1.0 / (1.0 + jnp.exp(-x))


def _sum_all(x):
    return jnp.sum(jnp.sum(x, axis=1, keepdims=True), axis=0, keepdims=True)


def _swap_halves(t):
    n = t.shape[1]
    lane = lax.broadcasted_iota(jnp.int32, t.shape, 1)
    low = (lane & 32) == 0
    return jnp.where(low, pltpu.roll(t, n - 32, 1), pltpu.roll(t, 32, 1))


def _vec_spec(d):
    return pl.BlockSpec((1, d), lambda *a: (0, 0))


def _norm_mod(x, nw, sc, sh, name):
    L, D = x.shape
    tl = min(256, L)

    def body(x_ref, nw_ref, sc_ref, sh_ref, xm_ref, xmt_ref):
        xv = x_ref[...]
        r = lax.rsqrt(jnp.mean(xv * xv, axis=-1, keepdims=True) + EPS)
        xm = (xv * r * nw_ref[...]) * (1.0 + sc_ref[...]) + sh_ref[...]
        xm_ref[...] = xm.astype(BF16)
        xmt_ref[...] = xm.T.astype(BF16)

    return pl.pallas_call(
        body, name=name, grid=(L // tl,),
        in_specs=[pl.BlockSpec((tl, D), lambda i: (i, 0)), _vec_spec(D), _vec_spec(D), _vec_spec(D)],
        out_specs=[pl.BlockSpec((tl, D), lambda i: (i, 0)), pl.BlockSpec((D, tl), lambda i: (0, i))],
        out_shape=[jax.ShapeDtypeStruct((L, D), BF16), jax.ShapeDtypeStruct((D, L), BF16)],
        compiler_params=_cparams(("parallel",)),
    )(x, nw, sc, sh)


def _in_proj(xm, w, name):
    M, D = xm.shape
    N = w.shape[1]
    tm = min(1024, M)
    tn = 3 * D // 4

    def body(a_ref, b_ref, o_ref):
        o_ref[...] = _dot(a_ref[...], b_ref[...]).astype(o_ref.dtype)

    return pl.pallas_call(
        body, name=name, grid=(M // tm, N // tn),
        in_specs=[pl.BlockSpec((tm, D), lambda i, j: (i, 0)), pl.BlockSpec((D, tn), lambda i, j: (0, j))],
        out_specs=pl.BlockSpec((tm, tn), lambda i, j: (i, j)),
        out_shape=jax.ShapeDtypeStruct((M, N), BF16),
        compiler_params=_cparams(("parallel", "parallel")),
    )(xm, w)


def _halo_specs(tl, L, D, col):
    hb = tl // 16
    last = L // 16 - 1
    prev = pl.BlockSpec((16, D), lambda i: (jnp.maximum(i * hb - 1, 0), col))
    nxt = pl.BlockSpec((16, D), lambda i: (jnp.minimum((i + 1) * hb, last), col))
    return prev, nxt


def _shift_rows(u, above, below):
    tl = u.shape[0]
    row = lax.broadcasted_iota(jnp.int32, u.shape, 0)
    dn = jnp.where(row == 0, above, pltpu.roll(u, 1, 0))
    up = jnp.where(row == tl - 1, below, pltpu.roll(u, tl - 1, 0))
    return dn, up


def _conv_gate_fwd(p, conv_w, conv_b, D):
    L = p.shape[0]
    tl = min(256, L)
    nt = L // tl

    def body(h_ref, bg_ref, cg_ref, za_ref, hp_ref, hn_ref, cp_ref, cn_ref, w_ref, b_ref, o_ref):
        i = pl.program_id(0)
        u = cg_ref[...].astype(F32) * h_ref[...].astype(F32)
        above = cp_ref[15:16, :].astype(F32) * hp_ref[15:16, :].astype(F32)
        below = cn_ref[0:1, :].astype(F32) * hn_ref[0:1, :].astype(F32)
        above = jnp.where(i == 0, 0.0, above)
        below = jnp.where(i == nt - 1, 0.0, below)
        dn, up = _shift_rows(u, above, below)
        co = w_ref[0:1, :] * dn + w_ref[1:2, :] * u + w_ref[2:3, :] * up + b_ref[...]
        za = za_ref[...].astype(F32)
        o_ref[...] = (za * _sigmoid(za) * bg_ref[...].astype(F32) * co).astype(BF16)

    main = lambda col: pl.BlockSpec((tl, D), lambda i: (i, col))
    hp, hn = _halo_specs(tl, L, D, 0)
    cp, cn = _halo_specs(tl, L, D, 2)
    return pl.pallas_call(
        body, name="conv_gate_fwd", grid=(nt,),
        in_specs=[main(0), main(1), main(2), main(3), hp, hn, cp, cn,
                  pl.BlockSpec((8, D), lambda i: (0, 0)), _vec_spec(D)],
        out_specs=pl.BlockSpec((tl, D), lambda i: (i, 0)),
        out_shape=jax.ShapeDtypeStruct((L, D), BF16),
        compiler_params=_cparams(("parallel",)),
    )(p, p, p, p, p, p, p, p, conv_w, conv_b)


def _rope_tables(L, D):
    pos = jnp.arange(L)
    row = (pos // GRID_W).astype(F32)
    col = (pos % GRID_W).astype(F32)
    nf = DK // 4
    inv = ROPE_BASE ** (-jnp.arange(nf, dtype=F32) / nf)
    ang = jnp.concatenate([row[:, None] * inv, col[:, None] * inv], axis=-1)
    cos, sin = jnp.cos(ang), jnp.sin(ang)
    heads = D // DV
    c2 = jnp.tile(jnp.concatenate([cos, cos], axis=-1), (1, heads))
    s2 = jnp.tile(jnp.concatenate([-sin, sin], axis=-1), (1, heads))
    return c2, s2


def _rope_fwd(p, c2, s2, D):
    L = p.shape[0]
    W = D // 2
    tl = min(256, L)

    def body(q_ref, k_ref, c_ref, s_ref, qo_ref, ko_ref):
        c, s = c_ref[...], s_ref[...]
        q = q_ref[...].astype(F32)
        k = k_ref[...].astype(F32) * K_SCALE
        qo_ref[...] = (q * c + _swap_halves(q) * s).astype(BF16)
        ko_ref[...] = (k * c + _swap_halves(k) * s).astype(BF16)

    blk = lambda col: pl.BlockSpec((tl, W), lambda i: (i, col))
    return pl.pallas_call(
        body, name="rope_fwd", grid=(L // tl,),
        in_specs=[blk(8), blk(9), blk(0), blk(0)],
        out_specs=[blk(0), blk(0)],
        out_shape=[jax.ShapeDtypeStruct((L, W), BF16)] * 2,
        compiler_params=_cparams(("parallel",)),
    )(p, p, c2, s2)


def _smem_spec():
    return pl.BlockSpec(memory_space=pltpu.SMEM)


def _pair_select(e0, e1):
    row = lax.broadcasted_iota(jnp.int32, e0.shape, 0)
    return jnp.where(row < DK, e0, e1)


def _head_lane_mask(shape, e):
    lane = lax.broadcasted_iota(jnp.int32, shape, 1)
    return (lane < DK) if e == 0 else (lane >= DK)


def _ctx_states(pc, lg, D):
    Lc = pc.shape[0]
    H = D // DV

    def body(lg_ref, k_ref, v_ref, s_ref):
        m = lax.broadcasted_iota(jnp.int32, (Lc, DV), 0).astype(F32)
        for pr in range(H // 2):
            k2 = k_ref[:, pr * 128:(pr + 1) * 128].astype(F32) * K_SCALE
            res = [[None, None], [None, None]]
            for e in range(2):
                h = 2 * pr + e
                v = v_ref[:, h * DV:(h + 1) * DV]
                dec_f = jnp.exp(lg_ref[0, h] * (Lc - 1.0 - m))
                dec_b = jnp.exp(lg_ref[1, h] * m)
                res[0][e] = _dot_tn((k2 * dec_f).astype(BF16), v)
                res[1][e] = _dot_tn((k2 * dec_b).astype(BF16), v)
            s_ref[0, pr] = _pair_select(res[0][0], res[0][1])
            s_ref[1, pr] = _pair_select(res[1][0], res[1][1])

    return pl.pallas_call(
        body, name="ctx_states", grid=(1,),
        in_specs=[_smem_spec(), pl.BlockSpec((Lc, D // 2), lambda i: (0, 9)), pl.BlockSpec((Lc, D), lambda i: (0, 5))],
        out_specs=pl.BlockSpec((2, H // 2, 128, 128), lambda i: (0, 0, 0, 0)),
        out_shape=jax.ShapeDtypeStruct((2, H // 2, 128, 128), F32),
    )(lg, pc, pc)


def _ret_states(kr, p, s0, lg, D):
    L = kr.shape[0]
    H = D // DV
    N = L // CHUNK
    HP = H // 2

    def body(lg_ref, kf_ref, kb_ref, vf_ref, vb_ref, s0_ref, sf_out, sb_out, sf, sb):
        n = pl.program_id(0)

        @pl.when(n == 0)
        def _():
            sf[...] = s0_ref[0]
            sb[...] = s0_ref[1]

        sf_out[0] = sf[...]
        sb_out[0] = sb[...]
        j = lax.broadcasted_iota(jnp.int32, (CHUNK, 128), 0).astype(F32)
        full = jnp.full((128, 128), float(CHUNK), F32)
        for pr in range(HP):
            kf2 = kf_ref[:, pr * 128:(pr + 1) * 128].astype(F32)
            kb2 = kb_ref[:, pr * 128:(pr + 1) * 128].astype(F32)
            inc_f, inc_b, gf, gb = [], [], [], []
            for e in range(2):
                h = 2 * pr + e
                lgf, lgb = lg_ref[0, h], lg_ref[1, h]
                inc_f.append(_dot_tn((kf2 * jnp.exp(lgf * (CHUNK - 1.0 - j))).astype(BF16), vf_ref[:, h * DV:(h + 1) * DV]))
                inc_b.append(_dot_tn((kb2 * jnp.exp(lgb * j)).astype(BF16), vb_ref[:, h * DV:(h + 1) * DV]))
                gf.append(jnp.exp(lgf * full))
                gb.append(jnp.exp(lgb * full))
            sf[pr] = _pair_select(gf[0], gf[1]) * sf[pr] + _pair_select(inc_f[0], inc_f[1])
            sb[pr] = _pair_select(gb[0], gb[1]) * sb[pr] + _pair_select(inc_b[0], inc_b[1])

    st = jax.ShapeDtypeStruct((N, HP, 128, 128), F32)
    return pl.pallas_call(
        body, name="ret_states", grid=(N,),
        in_specs=[_smem_spec(),
                  pl.BlockSpec((CHUNK, D // 2), lambda n: (n, 0)),
                  pl.BlockSpec((CHUNK, D // 2), lambda n: (N - 1 - n, 0)),
                  pl.BlockSpec((CHUNK, D), lambda n: (n, 5)),
                  pl.BlockSpec((CHUNK, D), lambda n: (N - 1 - n, 5)),
                  pl.BlockSpec((2, HP, 128, 128), lambda n: (0, 0, 0, 0))],
        out_specs=[pl.BlockSpec((1, HP, 128, 128), lambda n: (n, 0, 0, 0)),
                   pl.BlockSpec((1, HP, 128, 128), lambda n: (N - 1 - n, 0, 0, 0))],
        out_shape=[st, st],
        scratch_shapes=[pltpu.VMEM((HP, 128, 128), F32), pltpu.VMEM((HP, 128, 128), F32)],
        compiler_params=_cparams(("arbitrary",)),
    )(lg, kr, kr, p, p, s0)


def _decay_masks(lgf, lgb):
    i = lax.broadcasted_iota(jnp.int32, (CHUNK, CHUNK), 0).astype(F32)
    j = lax.broadcasted_iota(jnp.int32, (CHUNK, CHUNK), 1).astype(F32)
    d = i - j
    mf = jnp.where(d > 0, jnp.exp(lgf * jnp.maximum(d, 0.0)), 0.0)
    mb = jnp.where(d < 0, jnp.exp(lgb * jnp.maximum(-d, 0.0)), 0.0)
    m = mf + mb + jnp.where(d == 0, 2.0, 0.0)
    return m, mf * d, mb * (-d)


def _ret_out(qr, kr, p, sf_prev, sb_prev, gn_w, lg, D):
    L = qr.shape[0]
    H = D // DV
    N = L // CHUNK
    HP = H // 2

    def body(lg_ref, q_ref, k_ref, v_ref, zb_ref, sf_ref, sb_ref, gn_ref, o_ref, yb_ref):
        i = lax.broadcasted_iota(jnp.int32, (CHUNK, 128), 0).astype(F32)
        for pr in range(HP):
            q2 = q_ref[:, pr * 128:(pr + 1) * 128]
            k2 = k_ref[:, pr * 128:(pr + 1) * 128]
            sfp = sf_ref[0, pr].astype(BF16)
            sbp = sb_ref[0, pr].astype(BF16)
            for e in range(2):
                h = 2 * pr + e
                sl = slice(h * DV, (h + 1) * DV)
                lgf, lgb = lg_ref[0, h], lg_ref[1, h]
                qm = jnp.where(_head_lane_mask(q2.shape, e), q2, jnp.zeros_like(q2))
                m, _, _ = _decay_masks(lgf, lgb)
                a = (_dot_nt(qm, k2) * m).astype(BF16)
                qf = qm.astype(F32)
                o = _dot(a, v_ref[:, sl])
                o += _dot((qf * jnp.exp(lgf * (i + 1.0))).astype(BF16), sfp)
                o += _dot((qf * jnp.exp(lgb * (CHUNK - i))).astype(BF16), sbp)
                o_ref[:, sl] = o
                mu = jnp.mean(o, axis=-1, keepdims=True)
                oc = o - mu
                rstd = lax.rsqrt(jnp.mean(oc * oc, axis=-1, keepdims=True) + EPS)
                zb = zb_ref[:, sl].astype(F32)
                yb_ref[:, sl] = (zb * _sigmoid(zb) * (oc * rstd * gn_ref[:, sl])).astype(BF16)

    return pl.pallas_call(
        body, name="ret_out", grid=(N,),
        in_specs=[_smem_spec(),
                  pl.BlockSpec((CHUNK, D // 2), lambda n: (n, 0)),
                  pl.BlockSpec((CHUNK, D // 2), lambda n: (n, 0)),
                  pl.BlockSpec((CHUNK, D), lambda n: (n, 5)),
                  pl.BlockSpec((CHUNK, D), lambda n: (n, 6)),
                  pl.BlockSpec((1, HP, 128, 128), lambda n: (n, 0, 0, 0)),
                  pl.BlockSpec((1, HP, 128, 128), lambda n: (n, 0, 0, 0)),
                  _vec_spec(D)],
        out_specs=[pl.BlockSpec((CHUNK, D), lambda n: (n, 0)), pl.BlockSpec((CHUNK, D), lambda n: (n, 0))],
        out_shape=[jax.ShapeDtypeStruct((L, D), F32), jax.ShapeDtypeStruct((L, D), BF16)],
        compiler_params=_cparams(("parallel",)),
    )(lg, qr, kr, p, p, sf_prev, sb_prev, gn_w)


def _mid(ya, yb, p, x, tgt, w3, g, fw, D):
    L = x.shape[0]
    tm = min(256, L)
    nt = L // tm

    def body(ya_ref, yb_ref, ga_ref, gb_ref, x_ref, t_ref, w_hbm, g_ref, fw_ref,
             dx1_ref, dya_ref, dyb_ref, dga_ref, dgb_ref, dw_hbm, st_ref, w_vm, dw_acc, sem):
        i = pl.program_id(0)

        @pl.when(i == 0)
        def _():
            cp = pltpu.make_async_copy(w_hbm, w_vm, sem)
            cp.start()
            dw_acc[...] = jnp.zeros_like(dw_acc)
            st_ref[...] = jnp.zeros_like(st_ref)
            cp.wait()

        ya_b, yb_b = ya_ref[...], yb_ref[...]
        y_a = _dot(ya_b, w_vm[0])
        y_b = _dot(yb_b, w_vm[1])
        sga = _sigmoid(ga_ref[...].astype(F32))
        sgb = _sigmoid(gb_ref[...].astype(F32))
        mix_b = (sga * y_a + sgb * y_b).astype(BF16)
        y_x = _dot(mix_b, w_vm[2])
        gvec, fwv = g_ref[...], fw_ref[...]
        x1 = x_ref[...] + gvec * y_x
        r1 = lax.rsqrt(jnp.mean(x1 * x1, axis=-1, keepdims=True) + EPS)
        xh = x1 * r1
        diff = xh * fwv - t_ref[...]
        dout = diff * (1.0 / D)
        dxh = dout * fwv
        dx1 = r1 * (dxh - xh * jnp.mean(dxh * xh, axis=-1, keepdims=True))
        dx1_ref[...] = dx1
        st_ref[0:1, :] += jnp.sum(dout * xh, axis=0, keepdims=True)
        st_ref[1:2, :] += jnp.sum(dx1 * y_x, axis=0, keepdims=True)
        st_ref[2:3, :] += jnp.sum(diff * diff, axis=0, keepdims=True)
        dyx_b = (dx1 * gvec).astype(BF16)
        dmix = _dot_nt(dyx_b, w_vm[2])
        dw_acc[2] += _dot_tn(mix_b, dyx_b)
        dya_b = (dmix * sga).astype(BF16)
        dyb_b = (dmix * sgb).astype(BF16)
        dga_ref[...] = (dmix * y_a * sga * (1.0 - sga)).astype(BF16)
        dgb_ref[...] = (dmix * y_b * sgb * (1.0 - sgb)).astype(BF16)
        dya_ref[...] = _dot_nt(dya_b, w_vm[0]).astype(BF16)
        dyb_ref[...] = _dot_nt(dyb_b, w_vm[1]).astype(BF16)
        dw_acc[0] += _dot_tn(ya_b, dya_b)
        dw_acc[1] += _dot_tn(yb_b, dyb_b)

        @pl.when(i == nt - 1)
        def _():
            out = pltpu.make_async_copy(dw_acc, dw_hbm, sem)
            out.start()
            out.wait()

    row = lambda col: pl.BlockSpec((tm, D), lambda i: (i, col))
    any_spec = pl.BlockSpec(memory_space=pl.ANY)
    bfo = jax.ShapeDtypeStruct((L, D), BF16)
    return pl.pallas_call(
        body, name="mid", grid=(nt,),
        in_specs=[row(0), row(0), row(7), row(8), row(0), row(0), any_spec, _vec_spec(D), _vec_spec(D)],
        out_specs=[row(0), row(0), row(0), row(0), row(0), any_spec, pl.BlockSpec((8, D), lambda i: (0, 0))],
        out_shape=[jax.ShapeDtypeStruct((L, D), F32), bfo, bfo, bfo, bfo,
                   jax.ShapeDtypeStruct((3, D, D), F32), jax.ShapeDtypeStruct((8, D), F32)],
        scratch_shapes=[pltpu.VMEM((3, D, D), BF16), pltpu.VMEM((3, D, D), F32), pltpu.SemaphoreType.DMA],
        compiler_params=_cparams(("arbitrary",), VMEM_LIMIT),
    )(ya, yb, p, p, x, tgt, w3, g, fw)


def _conv_bwd(dya, p, conv_w, conv_b, D):
    L = p.shape[0]
    tl = min(256, L)
    nt = L // tl

    def body(d_ref, h_ref, bg_ref, cg_ref, za_ref,
             dp_ref, dn_ref, hp_ref, hn_ref, bp_ref, bn_ref, cp_ref, cn_ref, zp_ref, zn_ref,
             w_ref, b_ref, dh_ref, dbg_ref, dcg_ref, dza_ref, st_ref):
        i = pl.program_id(0)

        @pl.when(i == 0)
        def _():
            st_ref[...] = jnp.zeros_like(st_ref)

        first, last = i == 0, i == nt - 1
        h = h_ref[...].astype(F32)
        cg = cg_ref[...].astype(F32)
        bg = bg_ref[...].astype(F32)
        za = za_ref[...].astype(F32)
        dy = d_ref[...].astype(F32)
        u = cg * h
        u_above = jnp.where(first, 0.0, cp_ref[15:16, :].astype(F32) * hp_ref[15:16, :].astype(F32))
        u_below = jnp.where(last, 0.0, cn_ref[0:1, :].astype(F32) * hn_ref[0:1, :].astype(F32))
        u_dn, u_up = _shift_rows(u, u_above, u_below)
        w0, w1, w2 = w_ref[0:1, :], w_ref[1:2, :], w_ref[2:3, :]
        co = w0 * u_dn + w1 * u + w2 * u_up + b_ref[...]
        sz = _sigmoid(za)
        silu = za * sz
        dza_ref[...] = (dy * bg * co * (sz * (1.0 + za * (1.0 - sz)))).astype(BF16)
        dbg_ref[...] = (dy * silu * co).astype(BF16)
        dco = dy * silu * bg

        def edge(dr, zr, br, r):
            z = zr[r:r + 1, :].astype(F32)
            return dr[r:r + 1, :].astype(F32) * (z * _sigmoid(z)) * br[r:r + 1, :].astype(F32)

        dco_above = jnp.where(first, 0.0, edge(dp_ref, zp_ref, bp_ref, 15))
        dco_below = jnp.where(last, 0.0, edge(dn_ref, zn_ref, bn_ref, 0))
        dco_dn, dco_up = _shift_rows(dco, dco_above, dco_below)
        du = w0 * dco_up + w1 * dco + w2 * dco_dn
        dcg_ref[...] = (du * h).astype(BF16)
        dh_ref[...] = (du * cg).astype(BF16)
        st_ref[0:1, :] += jnp.sum(dco * u_dn, axis=0, keepdims=True)
        st_ref[1:2, :] += jnp.sum(dco * u, axis=0, keepdims=True)
        st_ref[2:3, :] += jnp.sum(dco * u_up, axis=0, keepdims=True)
        st_ref[3:4, :] += jnp.sum(dco, axis=0, keepdims=True)

    main = lambda col: pl.BlockSpec((tl, D), lambda i: (i, col))
    halos = []
    for col in (0, 0, 1, 2, 3):
        halos.extend(_halo_specs(tl, L, D, col))
    bfo = jax.ShapeDtypeStruct((L, D), BF16)
    return pl.pallas_call(
        body, name="conv_bwd", grid=(nt,),
        in_specs=[main(0), main(0), main(1), main(2), main(3)] + halos
                 + [pl.BlockSpec((8, D), lambda i: (0, 0)), _vec_spec(D)],
        out_specs=[main(0)] * 4 + [pl.BlockSpec((8, D), lambda i: (0, 0))],
        out_shape=[bfo, bfo, bfo, bfo, jax.ShapeDtypeStruct((8, D), F32)],
        compiler_params=_cparams(("arbitrary",)),
    )(dya, p, p, p, p, dya, dya, p, p, p, p, p, p, p, p, conv_w, conv_b)


def _ret_bwd_pre(dyb, p, o, gn_w, D):
    L = o.shape[0]
    H = D // DV
    tl = min(256, L)

    def body(d_ref, zb_ref, o_ref, gn_ref, do_ref, dzb_ref, st_ref):
        @pl.when(pl.program_id(0) == 0)
        def _():
            st_ref[...] = jnp.zeros_like(st_ref)

        for h in range(H):
            sl = slice(h * DV, (h + 1) * DV)
            ov = o_ref[:, sl]
            mu = jnp.mean(ov, axis=-1, keepdims=True)
            oc = ov - mu
            rstd = lax.rsqrt(jnp.mean(oc * oc, axis=-1, keepdims=True) + EPS)
            rn = oc * rstd
            gw = gn_ref[:, sl]
            zb = zb_ref[:, sl].astype(F32)
            sz = _sigmoid(zb)
            dy = d_ref[:, sl].astype(F32)
            dzb_ref[:, sl] = (dy * (rn * gw) * (sz * (1.0 + zb * (1.0 - sz)))).astype(BF16)
            dretn = dy * (zb * sz)
            st_ref[0:1, sl] += jnp.sum(dretn * rn, axis=0, keepdims=True)
            drn = dretn * gw
            do = rstd * (drn - jnp.mean(drn, axis=-1, keepdims=True)
                         - rn * jnp.mean(drn * rn, axis=-1, keepdims=True))
            do_ref[:, sl] = do.astype(BF16)

    main = lambda col: pl.BlockSpec((tl, D), lambda i: (i, col))
    bfo = jax.ShapeDtypeStruct((L, D), BF16)
    return pl.pallas_call(
        body, name="ret_bwd_pre", grid=(L // tl,),
        in_specs=[main(0), main(6), main(0), _vec_spec(D)],
        out_specs=[main(0), main(0), pl.BlockSpec((8, D), lambda i: (0, 0))],
        out_shape=[bfo, bfo, jax.ShapeDtypeStruct((8, D), F32)],
        compiler_params=_cparams(("arbitrary",)),
    )(dyb, p, o, gn_w)


def _ret_bwd_states(qr, do, lg, D):
    L = qr.shape[0]
    H = D // DV
    N = L // CHUNK
    HP = H // 2

    def body(lg_ref, qf_ref, qb_ref, dof_ref, dob_ref, dsf_out, dsb_out, ds0_out, dsf, dsb):
        n = pl.program_id(0)

        @pl.when(n == 0)
        def _():
            dsf[...] = jnp.zeros_like(dsf)
            dsb[...] = jnp.zeros_like(dsb)

        dsf_out[0] = dsf[...]
        dsb_out[0] = dsb[...]
        i = lax.broadcasted_iota(jnp.int32, (CHUNK, 128), 0).astype(F32)
        full = jnp.full((128, 128), float(CHUNK), F32)
        for pr in range(HP):
            qf2 = qf_ref[:, pr * 128:(pr + 1) * 128].astype(F32)
            qb2 = qb_ref[:, pr * 128:(pr + 1) * 128].astype(F32)
            inc_f, inc_b, gf, gb = [], [], [], []
            for e in range(2):
                h = 2 * pr + e
                lgf, lgb = lg_ref[0, h], lg_ref[1, h]
                inc_f.append(_dot_tn((qf2 * jnp.exp(lgf * (i + 1.0))).astype(BF16), dof_ref[:, h * DV:(h + 1) * DV]))
                inc_b.append(_dot_tn((qb2 * jnp.exp(lgb * (CHUNK - i))).astype(BF16), dob_ref[:, h * DV:(h + 1) * DV]))
                gf.append(jnp.exp(lgf * full))
                gb.append(jnp.exp(lgb * full))
            dsf[pr] = _pair_select(gf[0], gf[1]) * dsf[pr] + _pair_select(inc_f[0], inc_f[1])
            dsb[pr] = _pair_select(gb[0], gb[1]) * dsb[pr] + _pair_select(inc_b[0], inc_b[1])

        @pl.when(n == N - 1)
        def _():
            ds0_out[0] = dsf[...]
            ds0_out[1] = dsb[...]

    st = jax.ShapeDtypeStruct((N, HP, 128, 128), F32)
    return pl.pallas_call(
        body, name="ret_bwd_states", grid=(N,),
        in_specs=[_smem_spec(),
                  pl.BlockSpec((CHUNK, D // 2), lambda n: (N - 1 - n, 0)),
                  pl.BlockSpec((CHUNK, D // 2), lambda n: (n, 0)),
                  pl.BlockSpec((CHUNK, D), lambda n: (N - 1 - n, 0)),
                  pl.BlockSpec((CHUNK, D), lambda n: (n, 0))],
        out_specs=[pl.BlockSpec((1, HP, 128, 128), lambda n: (N - 1 - n, 0, 0, 0)),
                   pl.BlockSpec((1, HP, 128, 128), lambda n: (n, 0, 0, 0)),
                   pl.BlockSpec((2, HP, 128, 128), lambda n: (0, 0, 0, 0))],
        out_shape=[st, st, jax.ShapeDtypeStruct((2, HP, 128, 128), F32)],
        scratch_shapes=[pltpu.VMEM((HP, 128, 128), F32), pltpu.VMEM((HP, 128, 128), F32)],
        compiler_params=_cparams(("arbitrary",)),
    )(lg, qr, qr, do, do)


def _ret_bwd_main(qr, kr, p, do, sf_prev, sb_prev, dsf, dsb, c2, s2, lg, D):
    L = qr.shape[0]
    H = D // DV
    N = L // CHUNK
    HP = H // 2
    W = D // 2

    def body(lg_ref, q_ref, k_ref, v_ref, do_ref, sf_ref, sb_ref, dsf_ref, dsb_ref, c_ref, s_ref,
             dqk_ref, dv_ref, st_ref):
        @pl.when(pl.program_id(0) == 0)
        def _():
            st_ref[...] = jnp.zeros_like(st_ref)

        i = lax.broadcasted_iota(jnp.int32, (CHUNK, 128), 0).astype(F32)
        lane = lax.broadcasted_iota(jnp.int32, (1, 128), 1)
        rowid = lax.broadcasted_iota(jnp.int32, (128, 128), 0)
        full = jnp.full((1, 1), float(CHUNK), F32)
        acc_f = jnp.zeros((1, 128), F32)
        acc_b = jnp.zeros((1, 128), F32)
        for pr in range(HP):
            ps = slice(pr * 128, (pr + 1) * 128)
            q2, k2 = q_ref[:, ps], k_ref[:, ps]
            sf32, sb32 = sf_ref[0, pr], sb_ref[0, pr]
            dsf32, dsb32 = dsf_ref[0, pr], dsb_ref[0, pr]
            sfp, sbp = sf32.astype(BF16), sb32.astype(BF16)
            dsfp, dsbp = dsf32.astype(BF16), dsb32.astype(BF16)
            dq2 = jnp.zeros((CHUNK, 128), F32)
            dk2 = jnp.zeros((CHUNK, 128), F32)
            for e in range(2):
                h = 2 * pr + e
                sl = slice(h * DV, (h + 1) * DV)
                lgf, lgb = lg_ref[0, h], lg_ref[1, h]
                hm = _head_lane_mask(q2.shape, e)
                qm = jnp.where(hm, q2, jnp.zeros_like(q2))
                km = jnp.where(hm, k2, jnp.zeros_like(k2))
                qf, kf = qm.astype(F32), km.astype(F32)
                v, do = v_ref[:, sl], do_ref[:, sl]
                vf, dof = v.astype(F32), do.astype(F32)
                m, mf1, mb1 = _decay_masks(lgf, lgb)
                m_t, _, _ = _decay_masks(lgb, lgf)
                sc = _dot_nt(qm, k2)
                dpm = _dot_nt(do, v)
                dsc = (dpm * m).astype(BF16)
                a_t = (_dot_nt(km, q2) * m_t).astype(BF16)
                dsc_t = (_dot_nt(v, do) * m_t).astype(BF16)
                dq_f, dq_b = jnp.exp(lgf * (i + 1.0)), jnp.exp(lgb * (CHUNK - i))
                dk_f, dk_b = jnp.exp(lgf * (CHUNK - 1.0 - i)), jnp.exp(lgb * i)
                dq = _dot(dsc, km)
                dq += jnp.where(hm, dq_f * _dot_nt(do, sfp) + dq_b * _dot_nt(do, sbp), 0.0)
                dk = _dot(dsc_t, qm)
                dk += jnp.where(hm, dk_f * _dot_nt(v, dsfp) + dk_b * _dot_nt(v, dsbp), 0.0)
                kdf = _dot((kf * dk_f).astype(BF16), dsfp)
                kdb = _dot((kf * dk_b).astype(BF16), dsbp)
                dv_ref[:, sl] = (_dot(a_t, do) + kdf + kdb).astype(BF16)
                dq2 += dq
                dk2 += dk
                xf = _dot((qf * dq_f).astype(BF16), sfp)
                xb = _dot((qf * dq_b).astype(BF16), sbp)
                pair = (rowid < DK) if e == 0 else (rowid >= DK)
                gcf, gcb = jnp.exp(lgf * full), jnp.exp(lgb * full)
                tf = _sum_all(sc * dpm * mf1) + _sum_all(xf * dof * (i + 1.0)) \
                    + _sum_all(kdf * vf * (CHUNK - 1.0 - i)) \
                    + CHUNK * gcf * _sum_all(jnp.where(pair, dsf32 * sf32, 0.0))
                tb = _sum_all(sc * dpm * mb1) + _sum_all(xb * dof * (CHUNK - i)) \
                    + _sum_all(kdb * vf * i) \
                    + CHUNK * gcb * _sum_all(jnp.where(pair, dsb32 * sb32, 0.0))
                acc_f += jnp.where(lane == h, tf, 0.0)
                acc_b += jnp.where(lane == h, tb, 0.0)
            c, s = c_ref[:, ps], s_ref[:, ps]
            dqk_ref[:, ps] = (dq2 * c - _swap_halves(dq2) * s).astype(BF16)
            dqk_ref[:, W + pr * 128:W + (pr + 1) * 128] = ((dk2 * c - _swap_halves(dk2) * s) * K_SCALE).astype(BF16)
        st_ref[0:1, :] += acc_f
        st_ref[1:2, :] += acc_b

    st_spec = pl.BlockSpec((1, HP, 128, 128), lambda n: (n, 0, 0, 0))
    half = pl.BlockSpec((CHUNK, W), lambda n: (n, 0))
    return pl.pallas_call(
        body, name="ret_bwd_main", grid=(N,),
        in_specs=[_smem_spec(), half, half,
                  pl.BlockSpec((CHUNK, D), lambda n: (n, 5)),
                  pl.BlockSpec((CHUNK, D), lambda n: (n, 0)),
                  st_spec, st_spec, st_spec, st_spec, half, half],
        out_specs=[pl.BlockSpec((CHUNK, D), lambda n: (n, 0)),
                   pl.BlockSpec((CHUNK, D), lambda n: (n, 0)),
                   pl.BlockSpec((8, 128), lambda n: (0, 0))],
        out_shape=[jax.ShapeDtypeStruct((L, D), BF16), jax.ShapeDtypeStruct((L, D), BF16),
                   jax.ShapeDtypeStruct((8, 128), F32)],
        compiler_params=_cparams(("arbitrary",)),
    )(lg, qr, kr, p, do, sf_prev, sb_prev, dsf, dsb, c2, s2)


def _ctx_bwd(pc, ds0, lg, D):
    Lc = pc.shape[0]
    H = D // DV
    HP = H // 2
    W = D // 2

    def body(lg_ref, k_ref, v_ref, ds_ref, dqk_ref, dv_ref, st_ref):
        m = lax.broadcasted_iota(jnp.int32, (Lc, 128), 0).astype(F32)
        lane = lax.broadcasted_iota(jnp.int32, (1, 128), 1)
        acc_f = jnp.zeros((1, 128), F32)
        acc_b = jnp.zeros((1, 128), F32)
        dqk_ref[:, 0:W] = jnp.zeros((Lc, W), BF16)
        for pr in range(HP):
            ps = slice(pr * 128, (pr + 1) * 128)
            k2 = k_ref[:, ps].astype(F32) * K_SCALE
            dsfp, dsbp = ds_ref[0, pr].astype(BF16), ds_ref[1, pr].astype(BF16)
            dk2 = jnp.zeros((Lc, 128), F32)
            for e in range(2):
                h = 2 * pr + e
                sl = slice(h * DV, (h + 1) * DV)
                hm = _head_lane_mask(k2.shape, e)
                km = jnp.where(hm, k2, 0.0)
                v = v_ref[:, sl]
                vf = v.astype(F32)
                dec_f = jnp.exp(lg_ref[0, h] * (Lc - 1.0 - m))
                dec_b = jnp.exp(lg_ref[1, h] * m)
                kdf = _dot((km * dec_f).astype(BF16), dsfp)
                kdb = _dot((km * dec_b).astype(BF16), dsbp)
                dv_ref[:, sl] = (kdf + kdb).astype(BF16)
                dk2 += jnp.where(hm, dec_f * _dot_nt(v, dsfp) + dec_b * _dot_nt(v, dsbp), 0.0)
                acc_f += jnp.where(lane == h, _sum_all(kdf * vf * (Lc - 1.0 - m)), 0.0)
                acc_b += jnp.where(lane == h, _sum_all(kdb * vf * m), 0.0)
            dqk_ref[:, W + pr * 128:W + (pr + 1) * 128] = (dk2 * K_SCALE).astype(BF16)
        st_ref[...] = jnp.zeros_like(st_ref)
        st_ref[0:1, :] = acc_f
        st_ref[1:2, :] = acc_b

    return pl.pallas_call(
        body, name="ctx_bwd", grid=(1,),
        in_specs=[_smem_spec(), pl.BlockSpec((Lc, W), lambda i: (0, 9)), pl.BlockSpec((Lc, D), lambda i: (0, 5)),
                  pl.BlockSpec((2, HP, 128, 128), lambda i: (0, 0, 0, 0))],
        out_specs=[pl.BlockSpec((Lc, D), lambda i: (0, 0)), pl.BlockSpec((Lc, D), lambda i: (0, 0)),
                   pl.BlockSpec((8, 128), lambda i: (0, 0))],
        out_shape=[jax.ShapeDtypeStruct((Lc, D), BF16), jax.ShapeDtypeStruct((Lc, D), BF16),
                   jax.ShapeDtypeStruct((8, 128), F32)],
    )(lg, pc, pc, ds0)


def _dxm(pieces, cols, w, x, nw, sc, dx1, name):
    L, D = x.shape
    tm = min(512, L)
    nk = len(pieces)
    with_dx = dx1 is not None

    def body(*refs):
        piece_refs = refs[:nk]
        w_ref, x_ref, nw_ref, sc_ref = refs[nk:nk + 4]
        rest = refs[nk + 4:]
        if with_dx:
            dx1_ref, gx_ref, st_ref, acc = rest
        else:
            st_ref, acc = rest
        i, k = pl.program_id(0), pl.program_id(1)

        @pl.when((i == 0) & (k == 0))
        def _():
            st_ref[...] = jnp.zeros_like(st_ref)

        for kk in range(nk):
            @pl.when(k == kk)
            def _(kk=kk):
                part = _dot_nt(piece_refs[kk][...], w_ref[...])
                if kk == 0:
                    acc[...] = part
                else:
                    acc[...] += part

        @pl.when(k == nk - 1)
        def _():
            dxm = acc[...]
            xv = x_ref[...]
            r = lax.rsqrt(jnp.mean(xv * xv, axis=-1, keepdims=True) + EPS)
            xh = xv * r
            nwv = nw_ref[...]
            dxn = dxm * (1.0 + sc_ref[...])
            st_ref[0:1, :] += jnp.sum(dxm, axis=0, keepdims=True)
            st_ref[1:2, :] += jnp.sum(dxm * (xh * nwv), axis=0, keepdims=True)
            st_ref[2:3, :] += jnp.sum(dxn * xh, axis=0, keepdims=True)
            if with_dx:
                dxh = dxn * nwv
                gx_ref[...] = dx1_ref[...] + r * (dxh - xh * jnp.mean(dxh * xh, axis=-1, keepdims=True))

    row = pl.BlockSpec((tm, D), lambda i, k: (i, 0))
    w_spec = pl.BlockSpec((D, D), lambda i, k: (0, _select(k, cols)))
    in_specs = [row] * nk + [w_spec, row, _vec_spec(D), _vec_spec(D)]
    out_specs = [pl.BlockSpec((8, D), lambda i, k: (0, 0))]
    out_shape = [jax.ShapeDtypeStruct((8, D), F32)]
    args = list(pieces) + [w, x, nw, sc]
    if with_dx:
        in_specs.append(row)
        out_specs.insert(0, row)
        out_shape.insert(0, jax.ShapeDtypeStruct((L, D), F32))
        args.append(dx1)
    res = pl.pallas_call(
        body, name=name, grid=(L // tm, nk),
        in_specs=in_specs, out_specs=out_specs, out_shape=out_shape,
        scratch_shapes=[pltpu.VMEM((tm, D), F32)],
        compiler_params=_cparams(("arbitrary", "arbitrary"), VMEM_LIMIT),
    )(*args)
    return (res[0], res[1]) if with_dx else (None, res[0])


def _select(k, values):
    out = values[-1]
    for idx in range(len(values) - 2, -1, -1):
        out = jnp.where(k == idx, values[idx], out)
    return out


def _dw_in(xmt, pieces, cmt, dqk_c, dv_c, D):
    L = xmt.shape[1]
    Lc = cmt.shape[1]
    tk = min(512, L)
    nk = L // tk
    nj = len(pieces)

    def body(*refs):
        xt_ref = refs[0]
        piece_refs = refs[1:1 + nj]
        ct_ref, dqkc_ref, dvc_ref, o_ref = refs[1 + nj:]
        j, k = pl.program_id(0), pl.program_id(1)
        for jj in range(nj):
            @pl.when(j == jj)
            def _(jj=jj):
                part = _dot(xt_ref[...], piece_refs[jj][...])

                @pl.when(k == 0)
                def _():
                    o_ref[...] = part

                @pl.when(k > 0)
                def _():
                    o_ref[...] += part

        @pl.when((k == nk - 1) & (j == 4))
        def _():
            o_ref[...] += _dot(ct_ref[...], dqkc_ref[...])

        @pl.when((k == nk - 1) & (j == 5))
        def _():
            o_ref[...] += _dot(ct_ref[...], dvc_ref[...])

    def piece_spec(jj):
        return pl.BlockSpec((tk, D), lambda j, k: (jnp.where(j == jj, k, 0), 0))

    full_c = pl.BlockSpec((Lc, D), lambda j, k: (0, 0))
    return pl.pallas_call(
        body, name="dw_in", grid=(nj, nk),
        in_specs=[pl.BlockSpec((D, tk), lambda j, k: (0, k))] + [piece_spec(jj) for jj in range(nj)]
                 + [pl.BlockSpec((D, Lc), lambda j, k: (0, 0)), full_c, full_c],
        out_specs=pl.BlockSpec((D, D), lambda j, k: (0, j)),
        out_shape=jax.ShapeDtypeStruct((D, nj * D), F32),
        compiler_params=_cparams(("arbitrary", "arbitrary"), VMEM_LIMIT),
    )(xmt, *pieces, cmt, dqk_c, dv_c)


def _local_step(x, ctx, tgt, mod_x, mod_c, norm_w, conv_w8, conv_b, lg, gn_w, fw, w_in, w3):
    L, D = x.shape
    sh_x, sc_x, g_x = mod_x[0:1], mod_x[1:2], mod_x[2:3]
    sh_c, sc_c = mod_c[0:1], mod_c[1:2]
    c2, s2 = _rope_tables(L, D)

    xm, xmt = _norm_mod(x, norm_w, sc_x, sh_x, "norm_mod_x")
    cm, cmt = _norm_mod(ctx, norm_w, sc_c, sh_c, "norm_mod_ctx")
    p = _in_proj(xm, w_in, "in_proj_x")
    pc = _in_proj(cm, w_in, "in_proj_ctx")
    ya = _conv_gate_fwd(p, conv_w8, conv_b, D)
    qr, kr = _rope_fwd(p, c2, s2, D)
    s0 = _ctx_states(pc, lg, D)
    sf_prev, sb_prev = _ret_states(kr, p, s0, lg, D)
    o, yb = _ret_out(qr, kr, p, sf_prev, sb_prev, gn_w, lg, D)
    dx1, dya, dyb, dga, dgb, dw3, st_mid = _mid(ya, yb, p, x, tgt, w3, g_x, fw, D)
    dh, dbg, dcg, dza, st_conv = _conv_bwd(dya, p, conv_w8, conv_b, D)
    do, dzb, st_gn = _ret_bwd_pre(dyb, p, o, gn_w, D)
    dsf, dsb, ds0 = _ret_bwd_states(qr, do, lg, D)
    dqk, dv, st_lg = _ret_bwd_main(qr, kr, p, do, sf_prev, sb_prev, dsf, dsb, c2, s2, lg, D)
    dqk_c, dv_c, st_lgc = _ctx_bwd(pc, ds0, lg, D)
    pieces = (dh, dbg, dcg, dza, dqk, dv, dzb, dga, dgb)
    grad_x, st_x = _dxm(pieces, tuple(range(9)), w_in, x, norm_w, sc_x, dx1, "dxm_x")
    _, st_c = _dxm((dqk_c, dv_c), (4, 5), w_in, ctx, norm_w, sc_c, None, "dxm_ctx")
    dw_in = _dw_in(xmt, pieces, cmt, dqk_c, dv_c, D)
    return grad_x, dw_in, dw3, (st_mid, st_conv, st_gn, st_lg, st_lgc, st_x, st_c)


CHIP_FLIPS = (4, 2, 6)
ANY = pl.BlockSpec(memory_space=pl.ANY)
VMEM_FULL = pl.BlockSpec(memory_space=pltpu.VMEM)


def _position():
    return lax.axis_index("x"), lax.axis_index("y"), lax.axis_index("c")


def _peer(pos, k):
    x, y, c = pos
    return (1 - x if k & 4 else x, 1 - y if k & 2 else y, 1 - c if k & 1 else c)


def _dev_id(pos):
    return 4 * pos[0] + 2 * pos[1] + pos[2]


def _shard_of(pos):
    return 2 * pos[0] + pos[1]


def _remote(src, dst, send_sems, recv_sems, idx, to):
    return pltpu.make_async_remote_copy(src_ref=src, dst_ref=dst, send_sem=send_sems.at[idx],
                                        recv_sem=recv_sems.at[idx], device_id=to, device_id_type=MESH)


def _dot_f32(a, b):
    return jnp.dot(a, b, precision=lax.Precision.HIGHEST, preferred_element_type=F32)


def _silu(x):
    return x * _sigmoid(x)


def _fwd_small(c8, cctx8, ada_w, ada_b, conv_w8):
    D = c8.shape[1]
    Wm = ada_w.shape[1]
    Dq = conv_w8.shape[1]

    def body(c_ref, cc_ref, aw_ref, ab_ref, cw_ref, act_ref, mod_ref, cwf_ref,
             cbuf, pmine, pbuf, wbuf, s_c, r_c, s_p, r_p, s_w, r_w):
        pos = _position()
        me, s = _dev_id(pos), _shard_of(pos)
        cbuf[me] = c_ref[...]
        wbuf[s] = cw_ref[...]
        sends = [_remote(c_ref, cbuf.at[me], s_c, r_c, k - 1, _peer(pos, k)) for k in range(1, 8)]
        sends += [_remote(cw_ref, wbuf.at[s], s_w, r_w, j, _peer(pos, k)) for j, k in enumerate(CHIP_FLIPS)]
        for cp in sends:
            cp.start()
        for k in range(1, 8):
            _remote(c_ref, cbuf.at[_dev_id(_peer(pos, k))], s_c, r_c, k - 1, _peer(pos, k)).wait_recv()
        for d in range(N_DEV):
            act_ref[d:d + 1, :] = _silu(cbuf[d, 0:1, :])
        act_ref[8:9, :] = _silu(cc_ref[0:1, :])
        act_ref[9:16, :] = jnp.zeros((7, D), F32)
        part = _dot_f32(act_ref[...], aw_ref[...])
        pmine[...] = part
        pbuf[s] = part
        psend = [_remote(pmine, pbuf.at[s], s_p, r_p, j, _peer(pos, k)) for j, k in enumerate(CHIP_FLIPS)]
        for cp in psend:
            cp.start()
        for j, k in enumerate(CHIP_FLIPS):
            t = _shard_of(_peer(pos, k))
            _remote(pmine, pbuf.at[t], s_p, r_p, j, _peer(pos, k)).wait_recv()
            _remote(cw_ref, wbuf.at[t], s_w, r_w, j, _peer(pos, k)).wait_recv()
        for t in range(N_SHARD):
            mod_ref[:, t * Wm:(t + 1) * Wm] = pbuf[t] + ab_ref[:, t * Wm:(t + 1) * Wm]
            cwf_ref[:, t * Dq:(t + 1) * Dq] = wbuf[t]
        for cp in sends + psend:
            cp.wait_send()

    return pl.pallas_call(
        body, name="fwd_small",
        in_specs=[VMEM_FULL] * 5, out_specs=[VMEM_FULL] * 3,
        out_shape=[jax.ShapeDtypeStruct((16, D), F32), jax.ShapeDtypeStruct((16, 3 * D), F32),
                   jax.ShapeDtypeStruct((8, D), F32)],
        scratch_shapes=[pltpu.VMEM((N_DEV, 8, D), F32), pltpu.VMEM((16, Wm), F32),
                        pltpu.VMEM((N_SHARD, 16, Wm), F32), pltpu.VMEM((N_SHARD, 8, Dq), F32),
                        pltpu.SemaphoreType.DMA((7,)), pltpu.SemaphoreType.DMA((7,)),
                        pltpu.SemaphoreType.DMA((3,)), pltpu.SemaphoreType.DMA((3,)),
                        pltpu.SemaphoreType.DMA((3,)), pltpu.SemaphoreType.DMA((3,))],
        compiler_params=_cparams(None, VMEM_LIMIT),
    )(c8, cctx8, ada_w, ada_b, conv_w8)


def _cols(t, w):
    return pl.ds(pl.multiple_of(t * w, 128), w)


def _ag_weights(w_in_s, w3_s):
    D, Wc = w_in_s.shape
    Dh = D // 2
    Do = w3_s.shape[2]

    def body(wi_ref, w3_ref, fi_ref, f3_ref, si, s3, send, recv, loc):
        pos = _position()
        c = pos[2]
        s = _shard_of(pos)
        sib = _peer(pos, 1)

        def cast_rows(r, carry):
            rows = pl.ds(pl.multiple_of(r * 64, 64), 64)
            si[rows, :] = wi_ref[rows, :].astype(BF16)
            return carry

        lax.fori_loop(0, D // 64, cast_rows, 0)
        for a in range(3):
            s3[a] = w3_ref[a].astype(BF16)

        def in_half(t, hf):
            return fi_ref.at[pl.ds(hf * Dh, Dh), _cols(t, Wc)]

        def w3_half(t, hf):
            return f3_ref.at[:, t, hf]

        own = [pltpu.make_async_copy(si, fi_ref.at[:, _cols(s, Wc)], loc.at[0]),
               pltpu.make_async_copy(s3, f3_ref.at[:, s], loc.at[1])]
        first = []
        for j, k in enumerate(CHIP_FLIPS):
            to = _peer(pos, k)
            first.append(_remote(si.at[pl.ds(c * Dh, Dh), :], in_half(s, c), send, recv, j, to))
            first.append(_remote(s3.at[:, c], w3_half(s, c), send, recv, 3 + j, to))
        for cp in own + first:
            cp.start()
        passed = []
        for j, k in enumerate(CHIP_FLIPS):
            t = _shard_of(_peer(pos, k))
            _remote(in_half(t, c), in_half(t, c), send, recv, j, sib).wait_recv()
            fwd_i = _remote(in_half(t, c), in_half(t, c), send, recv, 6 + j, sib)
            fwd_i.start()
            _remote(w3_half(t, c), w3_half(t, c), send, recv, 3 + j, sib).wait_recv()
            fwd_3 = _remote(w3_half(t, c), w3_half(t, c), send, recv, 9 + j, sib)
            fwd_3.start()
            passed += [fwd_i, fwd_3]
        for j, k in enumerate(CHIP_FLIPS):
            t = _shard_of(_peer(pos, k))
            _remote(in_half(t, 1 - c), in_half(t, 1 - c), send, recv, 6 + j, sib).wait_recv()
            _remote(w3_half(t, 1 - c), w3_half(t, 1 - c), send, recv, 9 + j, sib).wait_recv()
        for cp in first + passed:
            cp.wait_send()
        for cp in own:
            cp.wait()

    return pl.pallas_call(
        body, name="ag_weights",
        in_specs=[VMEM_FULL, VMEM_FULL], out_specs=[ANY, ANY],
        out_shape=[jax.ShapeDtypeStruct((D, N_SHARD * Wc), BF16), jax.ShapeDtypeStruct((3, N_SHARD, 2, Do, D), BF16)],
        scratch_shapes=[pltpu.VMEM((D, Wc), BF16), pltpu.VMEM((3, 2, Do, D), BF16),
                        pltpu.SemaphoreType.DMA((12,)), pltpu.SemaphoreType.DMA((12,)), pltpu.SemaphoreType.DMA((2,))],
        compiler_params=_cparams(None, VMEM_LIMIT),
    )(w_in_s, w3_s)


def _rs_pair(dw_in, dw3):
    _, Dh, Wf = dw_in.shape
    _, _, _, Do, D = dw3.shape

    def body(gi_ref, g3_ref, ri_ref, r3_ref, send, recv):
        pos = _position()
        c = pos[2]
        sib = _peer(pos, 1)
        cps = [_remote(gi_ref.at[1 - c], ri_ref, send, recv, 0, sib),
               _remote(g3_ref.at[:, :, 1 - c], r3_ref, send, recv, 1, sib)]
        for cp in cps:
            cp.start()
        for cp in cps:
            cp.wait()

    return pl.pallas_call(
        body, name="rs_pair", in_specs=[ANY, ANY], out_specs=[ANY, ANY],
        out_shape=[jax.ShapeDtypeStruct((Dh, Wf), F32), jax.ShapeDtypeStruct((3, N_SHARD, Do, D), F32)],
        scratch_shapes=[pltpu.SemaphoreType.DMA((2,)), pltpu.SemaphoreType.DMA((2,))],
    )(dw_in, dw3)


def _sum_pair(cidx, dw_in, ri, dw3, r3):
    _, Dh, Wf = dw_in.shape
    Wc = Wf // N_SHARD
    _, _, _, Do, D = dw3.shape
    tr = min(256, Dh)

    def body_i(c_ref, a_ref, b_ref, o_ref):
        o_ref[...] = (a_ref[...] + b_ref[...]).astype(BF16)

    sum_i = pl.pallas_call(
        body_i, name="sum_pair_in",
        grid_spec=pltpu.PrefetchScalarGridSpec(
            num_scalar_prefetch=1, grid=(Dh // tr, N_SHARD),
            in_specs=[pl.BlockSpec((None, tr, Wc), lambda i, t, c: (c[0], i, t)),
                      pl.BlockSpec((tr, Wc), lambda i, t, c: (i, t))],
            out_specs=pl.BlockSpec((None, tr, Wc), lambda i, t, c: (t, i, 0))),
        out_shape=jax.ShapeDtypeStruct((N_SHARD, Dh, Wc), BF16),
        compiler_params=_cparams(("parallel", "parallel")),
    )(cidx, dw_in, ri)

    def body_3(c_ref, a_ref, b_ref, o_ref):
        o_ref[...] = (a_ref[...] + b_ref[...]).astype(BF16)

    sum_3 = pl.pallas_call(
        body_3, name="sum_pair_w3",
        grid_spec=pltpu.PrefetchScalarGridSpec(
            num_scalar_prefetch=1, grid=(3, N_SHARD),
            in_specs=[pl.BlockSpec((None, None, None, Do, D), lambda a, t, c: (a, t, c[0], 0, 0)),
                      pl.BlockSpec((None, None, Do, D), lambda a, t, c: (a, t, 0, 0))],
            out_specs=pl.BlockSpec((None, None, Do, D), lambda a, t, c: (a, t, 0, 0))),
        out_shape=jax.ShapeDtypeStruct((3, N_SHARD, Do, D), BF16),
        compiler_params=_cparams(("parallel", "parallel")),
    )(cidx, dw3, r3)
    return sum_i, sum_3


def _rs_chips(cs_in, cs_3):
    _, Dh, Wc = cs_in.shape
    _, _, Do, D = cs_3.shape

    def body(ci_ref, c3_ref, ri_ref, r3_ref, send, recv):
        pos = _position()
        cps = []
        for j, k in enumerate(CHIP_FLIPS):
            to = _peer(pos, k)
            t = _shard_of(to)
            cps.append(_remote(ci_ref.at[t], ri_ref.at[j], send, recv, j, to))
            cps.append(_remote(c3_ref.at[:, t], r3_ref.at[j], send, recv, 3 + j, to))
        for cp in cps:
            cp.start()
        for cp in cps:
            cp.wait()

    return pl.pallas_call(
        body, name="rs_chips", in_specs=[ANY, ANY], out_specs=[ANY, ANY],
        out_shape=[jax.ShapeDtypeStruct((3, Dh, Wc), BF16), jax.ShapeDtypeStruct((3, 3, Do, D), BF16)],
        scratch_shapes=[pltpu.SemaphoreType.DMA((6,)), pltpu.SemaphoreType.DMA((6,))],
    )(cs_in, cs_3)


def _sum_chips(sidx, cs_in, rb_in, cs_3, rb_3):
    _, Dh, Wc = cs_in.shape
    _, _, Do, D = cs_3.shape
    tr = min(256, Dh)

    def body_i(s_ref, a_ref, b_ref, o_ref):
        acc = a_ref[...].astype(F32)
        for j in range(3):
            acc = acc + b_ref[j].astype(F32)
        o_ref[...] = acc

    g_in = pl.pallas_call(
        body_i, name="sum_chips_in",
        grid_spec=pltpu.PrefetchScalarGridSpec(
            num_scalar_prefetch=1, grid=(Dh // tr,),
            in_specs=[pl.BlockSpec((None, tr, Wc), lambda i, s: (s[0], i, 0)),
                      pl.BlockSpec((3, tr, Wc), lambda i, s: (0, i, 0))],
            out_specs=pl.BlockSpec((tr, Wc), lambda i, s: (i, 0))),
        out_shape=jax.ShapeDtypeStruct((Dh, Wc), F32),
        compiler_params=_cparams(("parallel",)),
    )(sidx, cs_in, rb_in)

    def body_3(s_ref, a_ref, b_ref, o_ref):
        acc = a_ref[...].astype(F32)
        for j in range(3):
            acc = acc + b_ref[j].astype(F32)
        o_ref[...] = acc

    g_3 = pl.pallas_call(
        body_3, name="sum_chips_w3",
        grid_spec=pltpu.PrefetchScalarGridSpec(
            num_scalar_prefetch=1, grid=(3,),
            in_specs=[pl.BlockSpec((None, None, Do, D), lambda a, s: (a, s[0], 0, 0)),
                      pl.BlockSpec((3, None, Do, D), lambda a, s: (0, a, 0, 0))],
            out_specs=pl.BlockSpec((None, Do, D), lambda a, s: (a, 0, 0))),
        out_shape=jax.ShapeDtypeStruct((3, Do, D), F32),
        compiler_params=_cparams(("parallel",)),
    )(sidx, cs_3, rb_3)
    return g_in, g_3


def _rs_final(gh_in, gh_3):
    Dh, Wc = gh_in.shape
    _, Do, D = gh_3.shape

    def body(hi_ref, h3_ref, gi_ref, g3_ref, send, recv, loc):
        pos = _position()
        c = pos[2]
        sib = _peer(pos, 1)
        own = [pltpu.make_async_copy(hi_ref, gi_ref.at[c], loc.at[0]),
               pltpu.make_async_copy(h3_ref, g3_ref.at[:, c], loc.at[1])]
        cps = [_remote(hi_ref, gi_ref.at[c], send, recv, 0, sib),
               _remote(h3_ref, g3_ref.at[:, c], send, recv, 1, sib)]
        for cp in own + cps:
            cp.start()
        _remote(hi_ref, gi_ref.at[1 - c], send, recv, 0, sib).wait_recv()
        _remote(h3_ref, g3_ref.at[:, 1 - c], send, recv, 1, sib).wait_recv()
        for cp in cps:
            cp.wait_send()
        for cp in own:
            cp.wait()

    return pl.pallas_call(
        body, name="rs_final", in_specs=[ANY, ANY], out_specs=[ANY, ANY],
        out_shape=[jax.ShapeDtypeStruct((2, Dh, Wc), F32), jax.ShapeDtypeStruct((3, 2, Do, D), F32)],
        scratch_shapes=[pltpu.SemaphoreType.DMA((2,)), pltpu.SemaphoreType.DMA((2,)), pltpu.SemaphoreType.DMA((2,))],
    )(gh_in, gh_3)


def _adam_math(w, g, m, v):
    m = ADAM_B1 * m + (1.0 - ADAM_B1) * g
    v = ADAM_B2 * v + (1.0 - ADAM_B2) * (g * g)
    m_hat = m / (1.0 - ADAM_B1 ** ADAM_STEP)
    v_hat = v / (1.0 - ADAM_B2 ** ADAM_STEP)
    delta = -ADAM_LR * (m_hat / (jnp.sqrt(v_hat) + ADAM_EPS) + ADAM_WD * w)
    return delta, m, v


def _adamw(w, g, m, v, name):
    R, C = w.shape
    tr = min(128, R)

    def body(w_ref, g_ref, m_ref, v_ref, d_ref, nm_ref, nv_ref):
        d_ref[...], nm_ref[...], nv_ref[...] = _adam_math(w_ref[...], g_ref[...], m_ref[...], v_ref[...])

    blk = pl.BlockSpec((tr, C), lambda i: (i, 0))
    return pl.pallas_call(
        body, name=name, grid=(R // tr,), in_specs=[blk] * 4, out_specs=[blk] * 3,
        out_shape=[jax.ShapeDtypeStruct((R, C), F32)] * 3,
        compiler_params=_cparams(("parallel",), VMEM_LIMIT),
    )(w, g, m, v)


def _bwd_small(vec, act, ada_w, cctx8, w_sm, m_sm, v_sm, w_ab, m_ab, v_ab, w_cw, m_cw, v_cw, w_dl, m_dl, v_dl):
    D = vec.shape[1]
    Wm = ada_w.shape[1]
    Dq = w_cw.shape[1]

    def body(vec_ref, act_ref, aw_ref, cc_ref, wsm, msm, vsm, wab, mab, vab, wcw, mcw, vcw, wdl, mdl, vdl,
             gaw_ref, loss_ref, o_sm, o_ab, o_cw, o_dl,
             vbuf, dm, dm_sh, gcw, amine, abuf, s_v, r_v, s_a, r_a):
        pos = _position()
        me, s = _dev_id(pos), _shard_of(pos)
        vbuf[me] = vec_ref[...]
        sends = [_remote(vec_ref, vbuf.at[me], s_v, r_v, k - 1, _peer(pos, k)) for k in range(1, 8)]
        for cp in sends:
            cp.start()
        for k in range(1, 8):
            _remote(vec_ref, vbuf.at[_dev_id(_peer(pos, k))], s_v, r_v, k - 1, _peer(pos, k)).wait_recv()
        tot = vbuf[0]
        for d in range(1, N_DEV):
            tot = tot + vbuf[d]
        loss_ref[...] = jnp.zeros((8, 128), F32) + (0.5 / D) * _sum_all(tot[14:15, :])
        dm[...] = jnp.zeros_like(dm)
        for d in range(N_DEV):
            for r in range(3):
                dm[d:d + 1, r * D:(r + 1) * D] = vbuf[d, r:r + 1, :]
        dm[8:9, 0:D] = tot[3:4, :]
        dm[8:9, D:2 * D] = tot[4:5, :]
        for t in range(N_SHARD):
            @pl.when(s == t)
            def _(t=t):
                dm_sh[...] = dm[:, t * Wm:(t + 1) * Wm]
                gcw[...] = jnp.zeros_like(gcw)
                gcw[0:3, :] = tot[9:12, t * Dq:(t + 1) * Dq]
        gaw_ref[...] = lax.dot_general(act_ref[...], dm_sh[...], (((0,), (0,)), ((), ())),
                                       precision=lax.Precision.HIGHEST, preferred_element_type=F32)
        part = lax.dot_general(dm_sh[8:16, :], aw_ref[...], (((1,), (1,)), ((), ())),
                               precision=lax.Precision.HIGHEST, preferred_element_type=F32)
        amine[...] = part
        abuf[s] = part
        asend = [_remote(amine, abuf.at[s], s_a, r_a, j, _peer(pos, k)) for j, k in enumerate(CHIP_FLIPS)]
        for cp in asend:
            cp.start()
        for j, k in enumerate(CHIP_FLIPS):
            _remote(amine, abuf.at[_shard_of(_peer(pos, k))], s_a, r_a, j, _peer(pos, k)).wait_recv()
        da = abuf[0]
        for t in range(1, N_SHARD):
            da = da + abuf[t]
        cc = cc_ref[0:1, :]
        sg = _sigmoid(cc)
        g_cctx = da[0:1, :] * (sg * (1.0 + cc * (1.0 - sg)))

        def emit(o_ref, w, g, m, v):
            o_ref[0] = g
            o_ref[1], o_ref[2], o_ref[3] = _adam_math(w, g, m, v)

        g_sm = jnp.concatenate([g_cctx, tot[5:9, :], jnp.zeros((3, D), F32)], axis=0)
        emit(o_sm, wsm[...], g_sm, msm[...], vsm[...])
        g_ab = jnp.concatenate([tot[0:1, :] + tot[3:4, :], tot[1:2, :] + tot[4:5, :], tot[2:3, :],
                                jnp.zeros((5, D), F32)], axis=0)
        emit(o_ab, wab[...], g_ab, mab[...], vab[...])
        emit(o_cw, wcw[...], gcw[...], mcw[...], vcw[...])
        g_dl = jnp.concatenate([tot[12:14, 0:128] * _sigmoid(-wdl[0:2, :]), jnp.zeros((6, 128), F32)], axis=0)
        emit(o_dl, wdl[...], g_dl, mdl[...], vdl[...])
        for cp in sends + asend:
            cp.wait_send()

    return pl.pallas_call(
        body, name="bwd_small",
        in_specs=[VMEM_FULL] * 16, out_specs=[VMEM_FULL] * 6,
        out_shape=[jax.ShapeDtypeStruct((D, Wm), F32), jax.ShapeDtypeStruct((8, 128), F32),
                   jax.ShapeDtypeStruct((4, 8, D), F32), jax.ShapeDtypeStruct((4, 8, D), F32),
                   jax.ShapeDtypeStruct((4, 8, Dq), F32), jax.ShapeDtypeStruct((4, 8, 128), F32)],
        scratch_shapes=[pltpu.VMEM((N_DEV, 16, D), F32), pltpu.VMEM((16, 3 * D), F32), pltpu.VMEM((16, Wm), F32),
                        pltpu.VMEM((8, Dq), F32), pltpu.VMEM((8, D), F32), pltpu.VMEM((N_SHARD, 8, D), F32),
                        pltpu.SemaphoreType.DMA((7,)), pltpu.SemaphoreType.DMA((7,)),
                        pltpu.SemaphoreType.DMA((3,)), pltpu.SemaphoreType.DMA((3,))],
        compiler_params=_cparams(None, VMEM_LIMIT),
    )(vec, act, ada_w, cctx8, w_sm, m_sm, v_sm, w_ab, m_ab, v_ab, w_cw, m_cw, v_cw, w_dl, m_dl, v_dl)


def _pad_rows(a, rows=8):
    return jnp.pad(a, ((0, rows - a.shape[0]), (0, 0)))


def kernel(x, c, ctx, c_ctx, norm_w, ada_w, ada_b, w_in, conv_w, conv_b, decay_logit, gn_w, w_a, w_b, w_out, final_norm_w, loss_target, m_c_ctx, m_norm_w, m_ada_w, m_ada_b, m_w_in, m_conv_w, m_conv_b, m_decay_logit, m_gn_w, m_w_a, m_w_b, m_w_out, m_final_norm_w, v_c_ctx, v_norm_w, v_ada_w, v_ada_b, v_w_in, v_conv_w, v_conv_b, v_decay_logit, v_gn_w, v_w_a, v_w_b, v_w_out, v_final_norm_w):
    L, D = x.shape[1], x.shape[2]
    H = D // DV
    Wc = w_in.shape[2]
    Do = D // 8
    pos = _position()
    me = _dev_id(pos)
    cidx = jnp.reshape(pos[2], (1,)).astype(jnp.int32)
    sidx = jnp.reshape(_shard_of(pos), (1,)).astype(jnp.int32)

    act, mod, conv_w8 = _fwd_small(_pad_rows(c), _pad_rows(c_ctx[None]), ada_w[0], ada_b, _pad_rows(conv_w[0]))
    mod_x = lax.dynamic_slice_in_dim(mod, me, 1, axis=0).reshape(3, D)
    mod_c = mod[8].reshape(3, D)
    lg = jax.nn.log_sigmoid(decay_logit[0])

    w3_s = jnp.stack([w_a[0], w_b[0], w_out[0]]).reshape(3, 2, Do, D)
    w_in_full, w3_full = _ag_weights(w_in[0], w3_s)
    w3_full = w3_full.reshape(3, D, D)

    grad_x, dw_in, dw3, sts = _local_step(x[0], ctx[0], loss_target[0], mod_x, mod_c, norm_w, conv_w8, conv_b, lg,
                                          gn_w, final_norm_w[None], w_in_full, w3_full)
    st_mid, st_conv, st_gn, st_lg, st_lgc, st_x, st_c = sts

    dw_in2 = dw_in.reshape(2, D // 2, N_SHARD * Wc)
    dw3_5 = dw3.reshape(3, N_SHARD, 2, Do, D)
    ra_in, ra_3 = _rs_pair(dw_in2, dw3_5)
    cs_in, cs_3 = _sum_pair(cidx, dw_in2, ra_in, dw3_5, ra_3)
    rb_in, rb_3 = _rs_chips(cs_in, cs_3)
    gh_in, gh_3 = _sum_chips(sidx, cs_in, rb_in, cs_3, rb_3)
    g_in, g_3 = _rs_final(gh_in, gh_3)
    g_w_in = g_in.reshape(D, Wc)
    g_3 = g_3.reshape(3, D // 4, D)

    lanes = lambda a: jnp.pad(a, ((0, 0), (0, D - a.shape[1])))
    vec = jnp.concatenate([
        st_x[0:2], st_mid[1:2], st_c[0:2], st_x[2:3] + st_c[2:3], st_conv[3:4], st_gn[0:1], st_mid[0:1],
        st_conv[0:3], lanes(st_lg[0:2] + st_lgc[0:2]), st_mid[2:3], jnp.zeros((1, D), F32)], axis=0)
    small = lambda a, b_, c_, d_, e_: _pad_rows(jnp.concatenate([a[None], b_, c_, d_, e_[None]], axis=0))
    dl = lambda a: jnp.pad(a[0], ((0, 6), (0, 128 - H)))
    g_ada_w, loss_t, o_sm, o_ab, o_cw, o_dl = _bwd_small(
        vec, act, ada_w[0], _pad_rows(c_ctx[None]),
        small(c_ctx, norm_w, conv_b, gn_w, final_norm_w), small(m_c_ctx, m_norm_w, m_conv_b, m_gn_w, m_final_norm_w),
        small(v_c_ctx, v_norm_w, v_conv_b, v_gn_w, v_final_norm_w),
        _pad_rows(ada_b.reshape(3, D)), _pad_rows(m_ada_b.reshape(3, D)), _pad_rows(v_ada_b.reshape(3, D)),
        _pad_rows(conv_w[0]), _pad_rows(m_conv_w[0]), _pad_rows(v_conv_w[0]),
        dl(decay_logit), dl(m_decay_logit), dl(v_decay_logit))

    upd_in = _adamw(w_in[0], g_w_in, m_w_in[0], v_w_in[0], "adamw_w_in")
    upd_ada = _adamw(ada_w[0], g_ada_w, m_ada_w[0], v_ada_w[0], "adamw_ada_w")
    upd_a = _adamw(w_a[0], g_3[0], m_w_a[0], v_w_a[0], "adamw_w_a")
    upd_b = _adamw(w_b[0], g_3[1], m_w_b[0], v_w_b[0], "adamw_w_b")
    upd_o = _adamw(w_out[0], g_3[2], m_w_out[0], v_w_out[0], "adamw_w_out")

    def leaves(q):
        big = lambda g, upd: (g if q == 0 else upd[q - 1])[None]
        sm = o_sm[q]
        return [sm[0], sm[1:2], big(g_ada_w, upd_ada), o_ab[q][0:3].reshape(1, 3 * D), big(g_w_in, upd_in),
                o_cw[q][0:3][None], sm[2:3], o_dl[q][0:2, 0:H][None], sm[3:4],
                big(g_3[0], upd_a), big(g_3[1], upd_b), big(g_3[2], upd_o), sm[4]]

    loss = loss_t[0, 0]
    return (loss, grad_x[None], *leaves(0), *leaves(1), *leaves(2), *leaves(3))
```

```python
import functools

import jax
import jax.numpy as jnp
from jax import lax
from jax.experimental import pallas as pl
from jax.experimental.pallas import tpu as pltpu

F32 = jnp.float32
BF16 = jnp.bfloat16
MESH = pl.DeviceIdType.MESH

CHUNK = 128
DV = 128
DK = 64
GRID_W = 64
ROPE_BASE = 10000.0
EPS = 1e-6
K_SCALE = DK ** -0.5
N_SHARD = 4
N_DEV = 8

ADAM_LR = 0.001
ADAM_B1 = 0.9
ADAM_B2 = 0.999
ADAM_EPS = 1e-08
ADAM_WD = 0.01
ADAM_STEP = 10

VMEM_LIMIT = 56 * 1024 * 1024


def _cparams(sem=None, vmem=None):
    kw = {}
    if sem is not None:
        kw["dimension_semantics"] = sem
    if vmem is not None:
        kw["vmem_limit_bytes"] = vmem
    return pltpu.CompilerParams(**kw)


def _dot(a, b):
    return jnp.dot(a, b, preferred_element_type=F32)


def _dot_nt(a, b):
    return lax.dot_general(a, b, (((1,), (1,)), ((), ())), preferred_element_type=F32)


def _dot_tn(a, b):
    return lax.dot_general(a, b, (((0,), (0,)), ((), ())), preferred_element_type=F32)


def _sigmoid(x):
    return 1.0 / (1.0 + jnp.exp(-x))


def _sum_all(x):
    return jnp.sum(jnp.sum(x, axis=1, keepdims=True), axis=0, keepdims=True)


def _swap_halves(t):
    n = t.shape[1]
    lane = lax.broadcasted_iota(jnp.int32, t.shape, 1)
    low = (lane & 32) == 0
    return jnp.where(low, pltpu.roll(t, n - 32, 1), pltpu.roll(t, 32, 1))


def _vec_spec(d):
    return pl.BlockSpec((1, d), lambda *a: (0, 0))


def _norm_mod(x, nw, sc, sh, name):
    L, D = x.shape
    tl = min(256, L)

    def body(x_ref, nw_ref, sc_ref, sh_ref, xm_ref, xmt_ref):
        xv = x_ref[...]
        r = lax.rsqrt(jnp.mean(xv * xv, axis=-1, keepdims=True) + EPS)
        xm = (xv * r * nw_ref[...]) * (1.0 + sc_ref[...]) + sh_ref[...]
        xm_ref[...] = xm.astype(BF16)
        xmt_ref[...] = xm.T.astype(BF16)

    return pl.pallas_call(
        body, name=name, grid=(L // tl,),
        in_specs=[pl.BlockSpec((tl, D), lambda i: (i, 0)), _vec_spec(D), _vec_spec(D), _vec_spec(D)],
        out_specs=[pl.BlockSpec((tl, D), lambda i: (i, 0)), pl.BlockSpec((D, tl), lambda i: (0, i))],
        out_shape=[jax.ShapeDtypeStruct((L, D), BF16), jax.ShapeDtypeStruct((D, L), BF16)],
        compiler_params=_cparams(("parallel",)),
    )(x, nw, sc, sh)


def _in_proj(xm, w, name):
    M, D = xm.shape
    N = w.shape[1]
    tm = min(1024, M)
    tn = 3 * D // 4

    def body(a_ref, b_ref, o_ref):
        o_ref[...] = _dot(a_ref[...], b_ref[...]).astype(o_ref.dtype)

    return pl.pallas_call(
        body, name=name, grid=(M // tm, N // tn),
        in_specs=[pl.BlockSpec((tm, D), lambda i, j: (i, 0)), pl.BlockSpec((D, tn), lambda i, j: (0, j))],
        out_specs=pl.BlockSpec((tm, tn), lambda i, j: (i, j)),
        out_shape=jax.ShapeDtypeStruct((M, N), BF16),
        compiler_params=_cparams(("parallel", "parallel")),
    )(xm, w)


def _halo_specs(tl, L, D, col):
    hb = tl // 16
    last = L // 16 - 1
    prev = pl.BlockSpec((16, D), lambda i: (jnp.maximum(i * hb - 1, 0), col))
    nxt = pl.BlockSpec((16, D), lambda i: (jnp.minimum((i + 1) * hb, last), col))
    return prev, nxt


def _shift_rows(u, above, below):
    tl = u.shape[0]
    row = lax.broadcasted_iota(jnp.int32, u.shape, 0)
    dn = jnp.where(row == 0, above, pltpu.roll(u, 1, 0))
    up = jnp.where(row == tl - 1, below, pltpu.roll(u, tl - 1, 0))
    return dn, up


def _conv_gate_fwd(p, conv_w, conv_b, D):
    L = p.shape[0]
    tl = min(256, L)
    nt = L // tl

    def body(h_ref, bg_ref, cg_ref, za_ref, hp_ref, hn_ref, cp_ref, cn_ref, w_ref, b_ref, o_ref):
        i = pl.program_id(0)
        u = cg_ref[...].astype(F32) * h_ref[...].astype(F32)
        above = cp_ref[15:16, :].astype(F32) * hp_ref[15:16, :].astype(F32)
        below = cn_ref[0:1, :].astype(F32) * hn_ref[0:1, :].astype(F32)
        above = jnp.where(i == 0, 0.0, above)
        below = jnp.where(i == nt - 1, 0.0, below)
        dn, up = _shift_rows(u, above, below)
        co = w_ref[0:1, :] * dn + w_ref[1:2, :] * u + w_ref[2:3, :] * up + b_ref[...]
        za = za_ref[...].astype(F32)
        o_ref[...] = (za * _sigmoid(za) * bg_ref[...].astype(F32) * co).astype(BF16)

    main = lambda col: pl.BlockSpec((tl, D), lambda i: (i, col))
    hp, hn = _halo_specs(tl, L, D, 0)
    cp, cn = _halo_specs(tl, L, D, 2)
    return pl.pallas_call(
        body, name="conv_gate_fwd", grid=(nt,),
        in_specs=[main(0), main(1), main(2), main(3), hp, hn, cp, cn,
                  pl.BlockSpec((8, D), lambda i: (0, 0)), _vec_spec(D)],
        out_specs=pl.BlockSpec((tl, D), lambda i: (i, 0)),
        out_shape=jax.ShapeDtypeStruct((L, D), BF16),
        compiler_params=_cparams(("parallel",)),
    )(p, p, p, p, p, p, p, p, conv_w, conv_b)


def _rope_tables(L, D):
    pos = jnp.arange(L)
    row = (pos // GRID_W).astype(F32)
    col = (pos % GRID_W).astype(F32)
    nf = DK // 4
    inv = ROPE_BASE ** (-jnp.arange(nf, dtype=F32) / nf)
    ang = jnp.concatenate([row[:, None] * inv, col[:, None] * inv], axis=-1)
    cos, sin = jnp.cos(ang), jnp.sin(ang)
    heads = D // DV
    c2 = jnp.tile(jnp.concatenate([cos, cos], axis=-1), (1, heads))
    s2 = jnp.tile(jnp.concatenate([-sin, sin], axis=-1), (1, heads))
    return c2, s2


def _rope_fwd(p, c2, s2, D):
    L = p.shape[0]
    W = D // 2
    tl = min(256, L)

    def body(q_ref, k_ref, c_ref, s_ref, qo_ref, ko_ref):
        c, s = c_ref[...], s_ref[...]
        q = q_ref[...].astype(F32)
        k = k_ref[...].astype(F32) * K_SCALE
        qo_ref[...] = (q * c + _swap_halves(q) * s).astype(BF16)
        ko_ref[...] = (k * c + _swap_halves(k) * s).astype(BF16)

    blk = lambda col: pl.BlockSpec((tl, W), lambda i: (i, col))
    return pl.pallas_call(
        body, name="rope_fwd", grid=(L // tl,),
        in_specs=[blk(8), blk(9), blk(0), blk(0)],
        out_specs=[blk(0), blk(0)],
        out_shape=[jax.ShapeDtypeStruct((L, W), BF16)] * 2,
        compiler_params=_cparams(("parallel",)),
    )(p, p, c2, s2)


def _smem_spec():
    return pl.BlockSpec(memory_space=pltpu.SMEM)


def _pair_select(e0, e1):
    row = lax.broadcasted_iota(jnp.int32, e0.shape, 0)
    return jnp.where(row < DK, e0, e1)


def _head_lane_mask(shape, e):
    lane = lax.broadcasted_iota(jnp.int32, shape, 1)
    return (lane < DK) if e == 0 else (lane >= DK)


def _ctx_states(pc, lg, D):
    Lc = pc.shape[0]
    H = D // DV

    def body(lg_ref, k_ref, v_ref, s_ref):
        m = lax.broadcasted_iota(jnp.int32, (Lc, DV), 0).astype(F32)
        for pr in range(H // 2):
            k2 = k_ref[:, pr * 128:(pr + 1) * 128].astype(F32) * K_SCALE
            res = [[None, None], [None, None]]
            for e in range(2):
                h = 2 * pr + e
                v = v_ref[:, h * DV:(h + 1) * DV]
                dec_f = jnp.exp(lg_ref[0, h] * (Lc - 1.0 - m))
                dec_b = jnp.exp(lg_ref[1, h] * m)
                res[0][e] = _dot_tn((k2 * dec_f).astype(BF16), v)
                res[1][e] = _dot_tn((k2 * dec_b).astype(BF16), v)
            s_ref[0, pr] = _pair_select(res[0][0], res[0][1])
            s_ref[1, pr] = _pair_select(res[1][0], res[1][1])

    return pl.pallas_call(
        body, name="ctx_states", grid=(1,),
        in_specs=[_smem_spec(), pl.BlockSpec((Lc, D // 2), lambda i: (0, 9)), pl.BlockSpec((Lc, D), lambda i: (0, 5))],
        out_specs=pl.BlockSpec((2, H // 2, 128, 128), lambda i: (0, 0, 0, 0)),
        out_shape=jax.ShapeDtypeStruct((2, H // 2, 128, 128), F32),
    )(lg, pc, pc)


def _ret_states(kr, p, s0, lg, D):
    L = kr.shape[0]
    H = D // DV
    N = L // CHUNK
    HP = H // 2

    def body(lg_ref, kf_ref, kb_ref, vf_ref, vb_ref, s0_ref, sf_out, sb_out, sf, sb):
        n = pl.program_id(0)

        @pl.when(n == 0)
        def _():
            sf[...] = s0_ref[0]
            sb[...] = s0_ref[1]

        sf_out[0] = sf[...]
        sb_out[0] = sb[...]
        j = lax.broadcasted_iota(jnp.int32, (CHUNK, 128), 0).astype(F32)
        full = jnp.full((128, 128), float(CHUNK), F32)
        for pr in range(HP):
            kf2 = kf_ref[:, pr * 128:(pr + 1) * 128].astype(F32)
            kb2 = kb_ref[:, pr * 128:(pr + 1) * 128].astype(F32)
            inc_f, inc_b, gf, gb = [], [], [], []
            for e in range(2):
                h = 2 * pr + e
                lgf, lgb = lg_ref[0, h], lg_ref[1, h]
                inc_f.append(_dot_tn((kf2 * jnp.exp(lgf * (CHUNK - 1.0 - j))).astype(BF16), vf_ref[:, h * DV:(h + 1) * DV]))
                inc_b.append(_dot_tn((kb2 * jnp.exp(lgb * j)).astype(BF16), vb_ref[:, h * DV:(h + 1) * DV]))
                gf.append(jnp.exp(lgf * full))
                gb.append(jnp.exp(lgb * full))
            sf[pr] = _pair_select(gf[0], gf[1]) * sf[pr] + _pair_select(inc_f[0], inc_f[1])
            sb[pr] = _pair_select(gb[0], gb[1]) * sb[pr] + _pair_select(inc_b[0], inc_b[1])

    st = jax.ShapeDtypeStruct((N, HP, 128, 128), F32)
    return pl.pallas_call(
        body, name="ret_states", grid=(N,),
        in_specs=[_smem_spec(),
                  pl.BlockSpec((CHUNK, D // 2), lambda n: (n, 0)),
                  pl.BlockSpec((CHUNK, D // 2), lambda n: (N - 1 - n, 0)),
                  pl.BlockSpec((CHUNK, D), lambda n: (n, 5)),
                  pl.BlockSpec((CHUNK, D), lambda n: (N - 1 - n, 5)),
                  pl.BlockSpec((2, HP, 128, 128), lambda n: (0, 0, 0, 0))],
        out_specs=[pl.BlockSpec((1, HP, 128, 128), lambda n: (n, 0, 0, 0)),
                   pl.BlockSpec((1, HP, 128, 128), lambda n: (N - 1 - n, 0, 0, 0))],
        out_shape=[st, st],
        scratch_shapes=[pltpu.VMEM((HP, 128, 128), F32), pltpu.VMEM((HP, 128, 128), F32)],
        compiler_params=_cparams(("arbitrary",)),
    )(lg, kr, kr, p, p, s0)


def _decay_masks(lgf, lgb):
    i = lax.broadcasted_iota(jnp.int32, (CHUNK, CHUNK), 0).astype(F32)
    j = lax.broadcasted_iota(jnp.int32, (CHUNK, CHUNK), 1).astype(F32)
    d = i - j
    mf = jnp.where(d > 0, jnp.exp(lgf * jnp.maximum(d, 0.0)), 0.0)
    mb = jnp.where(d < 0, jnp.exp(lgb * jnp.maximum(-d, 0.0)), 0.0)
    m = mf + mb + jnp.where(d == 0, 2.0, 0.0)
    return m, mf * d, mb * (-d)


def _ret_out(qr, kr, p, sf_prev, sb_prev, gn_w, lg, D):
    L = qr.shape[0]
    H = D // DV
    N = L // CHUNK
    HP = H // 2

    def body(lg_ref, q_ref, k_ref, v_ref, zb_ref, sf_ref, sb_ref, gn_ref, o_ref, yb_ref):
        i = lax.broadcasted_iota(jnp.int32, (CHUNK, 128), 0).astype(F32)
        for pr in range(HP):
            q2 = q_ref[:, pr * 128:(pr + 1) * 128]
            k2 = k_ref[:, pr * 128:(pr + 1) * 128]
            sfp = sf_ref[0, pr].astype(BF16)
            sbp = sb_ref[0, pr].astype(BF16)
            for e in range(2):
                h = 2 * pr + e
                sl = slice(h * DV, (h + 1) * DV)
                lgf, lgb = lg_ref[0, h], lg_ref[1, h]
                qm = jnp.where(_head_lane_mask(q2.shape, e), q2, jnp.zeros_like(q2))
                m, _, _ = _decay_masks(lgf, lgb)
                a = (_dot_nt(qm, k2) * m).astype(BF16)
                qf = qm.astype(F32)
                o = _dot(a, v_ref[:, sl])
                o += _dot((qf * jnp.exp(lgf * (i + 1.0))).astype(BF16), sfp)
                o += _dot((qf * jnp.exp(lgb * (CHUNK - i))).astype(BF16), sbp)
                o_ref[:, sl] = o
                mu = jnp.mean(o, axis=-1, keepdims=True)
                oc = o - mu
                rstd = lax.rsqrt(jnp.mean(oc * oc, axis=-1, keepdims=True) + EPS)
                zb = zb_ref[:, sl].astype(F32)
                yb_ref[:, sl] = (zb * _sigmoid(zb) * (oc * rstd * gn_ref[:, sl])).astype(BF16)

    return pl.pallas_call(
        body, name="ret_out", grid=(N,),
        in_specs=[_smem_spec(),
                  pl.BlockSpec((CHUNK, D // 2), lambda n: (n, 0)),
                  pl.BlockSpec((CHUNK, D // 2), lambda n: (n, 0)),
                  pl.BlockSpec((CHUNK, D), lambda n: (n, 5)),
                  pl.BlockSpec((CHUNK, D), lambda n: (n, 6)),
                  pl.BlockSpec((1, HP, 128, 128), lambda n: (n, 0, 0, 0)),
                  pl.BlockSpec((1, HP, 128, 128), lambda n: (n, 0, 0, 0)),
                  _vec_spec(D)],
        out_specs=[pl.BlockSpec((CHUNK, D), lambda n: (n, 0)), pl.BlockSpec((CHUNK, D), lambda n: (n, 0))],
        out_shape=[jax.ShapeDtypeStruct((L, D), F32), jax.ShapeDtypeStruct((L, D), BF16)],
        compiler_params=_cparams(("parallel",)),
    )(lg, qr, kr, p, p, sf_prev, sb_prev, gn_w)


def _mid(ya, yb, p, x, tgt, w3, g, fw, D):
    L = x.shape[0]
    tm = min(256, L)
    nt = L // tm

    def body(ya_ref, yb_ref, ga_ref, gb_ref, x_ref, t_ref, w_hbm, g_ref, fw_ref,
             dx1_ref, dya_ref, dyb_ref, dga_ref, dgb_ref, dw_hbm, st_ref, w_vm, dw_acc, sem):
        i = pl.program_id(0)

        @pl.when(i == 0)
        def _():
            cp = pltpu.make_async_copy(w_hbm, w_vm, sem)
            cp.start()
            dw_acc[...] = jnp.zeros_like(dw_acc)
            st_ref[...] = jnp.zeros_like(st_ref)
            cp.wait()

        ya_b, yb_b = ya_ref[...], yb_ref[...]
        y_a = _dot(ya_b, w_vm[0])
        y_b = _dot(yb_b, w_vm[1])
        sga = _sigmoid(ga_ref[...].astype(F32))
        sgb = _sigmoid(gb_ref[...].astype(F32))
        mix_b = (sga * y_a + sgb * y_b).astype(BF16)
        y_x = _dot(mix_b, w_vm[2])
        gvec, fwv = g_ref[...], fw_ref[...]
        x1 = x_ref[...] + gvec * y_x
        r1 = lax.rsqrt(jnp.mean(x1 * x1, axis=-1, keepdims=True) + EPS)
        xh = x1 * r1
        diff = xh * fwv - t_ref[...]
        dout = diff * (1.0 / D)
        dxh = dout * fwv
        dx1 = r1 * (dxh - xh * jnp.mean(dxh * xh, axis=-1, keepdims=True))
        dx1_ref[...] = dx1
        st_ref[0:1, :] += jnp.sum(dout * xh, axis=0, keepdims=True)
        st_ref[1:2, :] += jnp.sum(dx1 * y_x, axis=0, keepdims=True)
        st_ref[2:3, :] += jnp.sum(diff * diff, axis=0, keepdims=True)
        dyx_b = (dx1 * gvec).astype(BF16)
        dmix = _dot_nt(dyx_b, w_vm[2])
        dw_acc[2] += _dot_tn(mix_b, dyx_b)
        dya_b = (dmix * sga).astype(BF16)
        dyb_b = (dmix * sgb).astype(BF16)
        dga_ref[...] = (dmix * y_a * sga * (1.0 - sga)).astype(BF16)
        dgb_ref[...] = (dmix * y_b * sgb * (1.0 - sgb)).astype(BF16)
        dya_ref[...] = _dot_nt(dya_b, w_vm[0]).astype(BF16)
        dyb_ref[...] = _dot_nt(dyb_b, w_vm[1]).astype(BF16)
        dw_acc[0] += _dot_tn(ya_b, dya_b)
        dw_acc[1] += _dot_tn(yb_b, dyb_b)

        @pl.when(i == nt - 1)
        def _():
            out = pltpu.make_async_copy(dw_acc, dw_hbm, sem)
            out.start()
            out.wait()

    row = lambda col: pl.BlockSpec((tm, D), lambda i: (i, col))
    any_spec = pl.BlockSpec(memory_space=pl.ANY)
    bfo = jax.ShapeDtypeStruct((L, D), BF16)
    return pl.pallas_call(
        body, name="mid", grid=(nt,),
        in_specs=[row(0), row(0), row(7), row(8), row(0), row(0), any_spec, _vec_spec(D), _vec_spec(D)],
        out_specs=[row(0), row(0), row(0), row(0), row(0), any_spec, pl.BlockSpec((8, D), lambda i: (0, 0))],
        out_shape=[jax.ShapeDtypeStruct((L, D), F32), bfo, bfo, bfo, bfo,
                   jax.ShapeDtypeStruct((3, D, D), F32), jax.ShapeDtypeStruct((8, D), F32)],
        scratch_shapes=[pltpu.VMEM((3, D, D), BF16), pltpu.VMEM((3, D, D), F32), pltpu.SemaphoreType.DMA],
        compiler_params=_cparams(("arbitrary",), VMEM_LIMIT),
    )(ya, yb, p, p, x, tgt, w3, g, fw)


def _conv_bwd(dya, p, conv_w, conv_b, D):
    L = p.shape[0]
    tl = min(256, L)
    nt = L // tl

    def body(d_ref, h_ref, bg_ref, cg_ref, za_ref,
             dp_ref, dn_ref, hp_ref, hn_ref, bp_ref, bn_ref, cp_ref, cn_ref, zp_ref, zn_ref,
             w_ref, b_ref, dh_ref, dbg_ref, dcg_ref, dza_ref, st_ref):
        i = pl.program_id(0)

        @pl.when(i == 0)
        def _():
            st_ref[...] = jnp.zeros_like(st_ref)

        first, last = i == 0, i == nt - 1
        h = h_ref[...].astype(F32)
        cg = cg_ref[...].astype(F32)
        bg = bg_ref[...].astype(F32)
        za = za_ref[...].astype(F32)
        dy = d_ref[...].astype(F32)
        u = cg * h
        u_above = jnp.where(first, 0.0, cp_ref[15:16, :].astype(F32) * hp_ref[15:16, :].astype(F32))
        u_below = jnp.where(last, 0.0, cn_ref[0:1, :].astype(F32) * hn_ref[0:1, :].astype(F32))
        u_dn, u_up = _shift_rows(u, u_above, u_below)
        w0, w1, w2 = w_ref[0:1, :], w_ref[1:2, :], w_ref[2:3, :]
        co = w0 * u_dn + w1 * u + w2 * u_up + b_ref[...]
        sz = _sigmoid(za)
        silu = za * sz
        dza_ref[...] = (dy * bg * co * (sz * (1.0 + za * (1.0 - sz)))).astype(BF16)
        dbg_ref[...] = (dy * silu * co).astype(BF16)
        dco = dy * silu * bg

        def edge(dr, zr, br, r):
            z = zr[r:r + 1, :].astype(F32)
            return dr[r:r + 1, :].astype(F32) * (z * _sigmoid(z)) * br[r:r + 1, :].astype(F32)

        dco_above = jnp.where(first, 0.0, edge(dp_ref, zp_ref, bp_ref, 15))
        dco_below = jnp.where(last, 0.0, edge(dn_ref, zn_ref, bn_ref, 0))
        dco_dn, dco_up = _shift_rows(dco, dco_above, dco_below)
        du = w0 * dco_up + w1 * dco + w2 * dco_dn
        dcg_ref[...] = (du * h).astype(BF16)
        dh_ref[...] = (du * cg).astype(BF16)
        st_ref[0:1, :] += jnp.sum(dco * u_dn, axis=0, keepdims=True)
        st_ref[1:2, :] += jnp.sum(dco * u, axis=0, keepdims=True)
        st_ref[2:3, :] += jnp.sum(dco * u_up, axis=0, keepdims=True)
        st_ref[3:4, :] += jnp.sum(dco, axis=0, keepdims=True)

    main = lambda col: pl.BlockSpec((tl, D), lambda i: (i, col))
    halos = []
    for col in (0, 0, 1, 2, 3):
        halos.extend(_halo_specs(tl, L, D, col))
    bfo = jax.ShapeDtypeStruct((L, D), BF16)
    return pl.pallas_call(
        body, name="conv_bwd", grid=(nt,),
        in_specs=[main(0), main(0), main(1), main(2), main(3)] + halos
                 + [pl.BlockSpec((8, D), lambda i: (0, 0)), _vec_spec(D)],
        out_specs=[main(0)] * 4 + [pl.BlockSpec((8, D), lambda i: (0, 0))],
        out_shape=[bfo, bfo, bfo, bfo, jax.ShapeDtypeStruct((8, D), F32)],
        compiler_params=_cparams(("arbitrary",)),
    )(dya, p, p, p, p, dya, dya, p, p, p, p, p, p, p, p, conv_w, conv_b)


def _ret_bwd_pre(dyb, p, o, gn_w, D):
    L = o.shape[0]
    H = D // DV
    tl = min(256, L)

    def body(d_ref, zb_ref, o_ref, gn_ref, do_ref, dzb_ref, st_ref):
        @pl.when(pl.program_id(0) == 0)
        def _():
            st_ref[...] = jnp.zeros_like(st_ref)

        for h in range(H):
            sl = slice(h * DV, (h + 1) * DV)
            ov = o_ref[:, sl]
            mu = jnp.mean(ov, axis=-1, keepdims=True)
            oc = ov - mu
            rstd = lax.rsqrt(jnp.mean(oc * oc, axis=-1, keepdims=True) + EPS)
            rn = oc * rstd
            gw = gn_ref[:, sl]
            zb = zb_ref[:, sl].astype(F32)
            sz = _sigmoid(zb)
            dy = d_ref[:, sl].astype(F32)
            dzb_ref[:, sl] = (dy * (rn * gw) * (sz * (1.0 + zb * (1.0 - sz)))).astype(BF16)
            dretn = dy * (zb * sz)
            st_ref[0:1, sl] += jnp.sum(dretn * rn, axis=0, keepdims=True)
            drn = dretn * gw
            do = rstd * (drn - jnp.mean(drn, axis=-1, keepdims=True)
                         - rn * jnp.mean(drn * rn, axis=-1, keepdims=True))
            do_ref[:, sl] = do.astype(BF16)

    main = lambda col: pl.BlockSpec((tl, D), lambda i: (i, col))
    bfo = jax.ShapeDtypeStruct((L, D), BF16)
    return pl.pallas_call(
        body, name="ret_bwd_pre", grid=(L // tl,),
        in_specs=[main(0), main(6), main(0), _vec_spec(D)],
        out_specs=[main(0), main(0), pl.BlockSpec((8, D), lambda i: (0, 0))],
        out_shape=[bfo, bfo, jax.ShapeDtypeStruct((8, D), F32)],
        compiler_params=_cparams(("arbitrary",)),
    )(dyb, p, o, gn_w)


def _ret_bwd_states(qr, do, lg, D):
    L = qr.shape[0]
    H = D // DV
    N = L // CHUNK
    HP = H // 2

    def body(lg_ref, qf_ref, qb_ref, dof_ref, dob_ref, dsf_out, dsb_out, ds0_out, dsf, dsb):
        n = pl.program_id(0)

        @pl.when(n == 0)
        def _():
            dsf[...] = jnp.zeros_like(dsf)
            dsb[...] = jnp.zeros_like(dsb)

        dsf_out[0] = dsf[...]
        dsb_out[0] = dsb[...]
        i = lax.broadcasted_iota(jnp.int32, (CHUNK, 128), 0).astype(F32)
        full = jnp.full((128, 128), float(CHUNK), F32)
        for pr in range(HP):
            qf2 = qf_ref[:, pr * 128:(pr + 1) * 128].astype(F32)
            qb2 = qb_ref[:, pr * 128:(pr + 1) * 128].astype(F32)
            inc_f, inc_b, gf, gb = [], [], [], []
            for e in range(2):
                h = 2 * pr + e
                lgf, lgb = lg_ref[0, h], lg_ref[1, h]
                inc_f.append(_dot_tn((qf2 * jnp.exp(lgf * (i + 1.0))).astype(BF16), dof_ref[:, h * DV:(h + 1) * DV]))
                inc_b.append(_dot_tn((qb2 * jnp.exp(lgb * (CHUNK - i))).astype(BF16), dob_ref[:, h * DV:(h + 1) * DV]))
                gf.append(jnp.exp(lgf * full))
                gb.append(jnp.exp(lgb * full))
            dsf[pr] = _pair_select(gf[0], gf[1]) * dsf[pr] + _pair_select(inc_f[0], inc_f[1])
            dsb[pr] = _pair_select(gb[0], gb[1]) * dsb[pr] + _pair_select(inc_b[0], inc_b[1])

        @pl.when(n == N - 1)
        def _():
            ds0_out[0] = dsf[...]
            ds0_out[1] = dsb[...]

    st = jax.ShapeDtypeStruct((N, HP, 128, 128), F32)
    return pl.pallas_call(
        body, name="ret_bwd_states", grid=(N,),
        in_specs=[_smem_spec(),
                  pl.BlockSpec((CHUNK, D // 2), lambda n: (N - 1 - n, 0)),
                  pl.BlockSpec((CHUNK, D // 2), lambda n: (n, 0)),
                  pl.BlockSpec((CHUNK, D), lambda n: (N - 1 - n, 0)),
                  pl.BlockSpec((CHUNK, D), lambda n: (n, 0))],
        out_specs=[pl.BlockSpec((1, HP, 128, 128), lambda n: (N - 1 - n, 0, 0, 0)),
                   pl.BlockSpec((1, HP, 128, 128), lambda n: (n, 0, 0, 0)),
                   pl.BlockSpec((2, HP, 128, 128), lambda n: (0, 0, 0, 0))],
        out_shape=[st, st, jax.ShapeDtypeStruct((2, HP, 128, 128), F32)],
        scratch_shapes=[pltpu.VMEM((HP, 128, 128), F32), pltpu.VMEM((HP, 128, 128), F32)],
        compiler_params=_cparams(("arbitrary",)),
    )(lg, qr, qr, do, do)


def _ret_bwd_main(qr, kr, p, do, sf_prev, sb_prev, dsf, dsb, c2, s2, lg, D):
    L = qr.shape[0]
    H = D // DV
    N = L // CHUNK
    HP = H // 2
    W = D // 2

    def body(lg_ref, q_ref, k_ref, v_ref, do_ref, sf_ref, sb_ref, dsf_ref, dsb_ref, c_ref, s_ref,
             dqk_ref, dv_ref, st_ref):
        @pl.when(pl.program_id(0) == 0)
        def _():
            st_ref[...] = jnp.zeros_like(st_ref)

        i = lax.broadcasted_iota(jnp.int32, (CHUNK, 128), 0).astype(F32)
        lane = lax.broadcasted_iota(jnp.int32, (1, 128), 1)
        rowid = lax.broadcasted_iota(jnp.int32, (128, 128), 0)
        full = jnp.full((1, 1), float(CHUNK), F32)
        acc_f = jnp.zeros((1, 128), F32)
        acc_b = jnp.zeros((1, 128), F32)
        for pr in range(HP):
            ps = slice(pr * 128, (pr + 1) * 128)
            q2, k2 = q_ref[:, ps], k_ref[:, ps]
            sf32, sb32 = sf_ref[0, pr], sb_ref[0, pr]
            dsf32, dsb32 = dsf_ref[0, pr], dsb_ref[0, pr]
            sfp, sbp = sf32.astype(BF16), sb32.astype(BF16)
            dsfp, dsbp = dsf32.astype(BF16), dsb32.astype(BF16)
            dq2 = jnp.zeros((CHUNK, 128), F32)
            dk2 = jnp.zeros((CHUNK, 128), F32)
            for e in range(2):
                h = 2 * pr + e
                sl = slice(h * DV, (h + 1) * DV)
                lgf, lgb = lg_ref[0, h], lg_ref[1, h]
                hm = _head_lane_mask(q2.shape, e)
                qm = jnp.where(hm, q2, jnp.zeros_like(q2))
                km = jnp.where(hm, k2, jnp.zeros_like(k2))
                qf, kf = qm.astype(F32), km.astype(F32)
                v, do = v_ref[:, sl], do_ref[:, sl]
                vf, dof = v.astype(F32), do.astype(F32)
                m, mf1, mb1 = _decay_masks(lgf, lgb)
                m_t, _, _ = _decay_masks(lgb, lgf)
                sc = _dot_nt(qm, k2)
                dpm = _dot_nt(do, v)
                dsc = (dpm * m).astype(BF16)
                a_t = (_dot_nt(km, q2) * m_t).astype(BF16)
                dsc_t = (_dot_nt(v, do) * m_t).astype(BF16)
                dq_f, dq_b = jnp.exp(lgf * (i + 1.0)), jnp.exp(lgb * (CHUNK - i))
                dk_f, dk_b = jnp.exp(lgf * (CHUNK - 1.0 - i)), jnp.exp(lgb * i)
                dq = _dot(dsc, km)
                dq += jnp.where(hm, dq_f * _dot_nt(do, sfp) + dq_b * _dot_nt(do, sbp), 0.0)
                dk = _dot(dsc_t, qm)
                dk += jnp.where(hm, dk_f * _dot_nt(v, dsfp) + dk_b * _dot_nt(v, dsbp), 0.0)
                kdf = _dot((kf * dk_f).astype(BF16), dsfp)
                kdb = _dot((kf * dk_b).astype(BF16), dsbp)
                dv_ref[:, sl] = (_dot(a_t, do) + kdf + kdb).astype(BF16)
                dq2 += dq
                dk2 += dk
                xf = _dot((qf * dq_f).astype(BF16), sfp)
                xb = _dot((qf * dq_b).astype(BF16), sbp)
                pair = (rowid < DK) if e == 0 else (rowid >= DK)
                gcf, gcb = jnp.exp(lgf * full), jnp.exp(lgb * full)
                tf = _sum_all(sc * dpm * mf1) + _sum_all(xf * dof * (i + 1.0)) \
                    + _sum_all(kdf * vf * (CHUNK - 1.0 - i)) \
                    + CHUNK * gcf * _sum_all(jnp.where(pair, dsf32 * sf32, 0.0))
                tb = _sum_all(sc * dpm * mb1) + _sum_all(xb * dof * (CHUNK - i)) \
                    + _sum_all(kdb * vf * i) \
                    + CHUNK * gcb * _sum_all(jnp.where(pair, dsb32 * sb32, 0.0))
                acc_f += jnp.where(lane == h, tf, 0.0)
                acc_b += jnp.where(lane == h, tb, 0.0)
            c, s = c_ref[:, ps], s_ref[:, ps]
            dqk_ref[:, ps] = (dq2 * c - _swap_halves(dq2) * s).astype(BF16)
            dqk_ref[:, W + pr * 128:W + (pr + 1) * 128] = ((dk2 * c - _swap_halves(dk2) * s) * K_SCALE).astype(BF16)
        st_ref[0:1, :] += acc_f
        st_ref[1:2, :] += acc_b

    st_spec = pl.BlockSpec((1, HP, 128, 128), lambda n: (n, 0, 0, 0))
    half = pl.BlockSpec((CHUNK, W), lambda n: (n, 0))
    return pl.pallas_call(
        body, name="ret_bwd_main", grid=(N,),
        in_specs=[_smem_spec(), half, half,
                  pl.BlockSpec((CHUNK, D), lambda n: (n, 5)),
                  pl.BlockSpec((CHUNK, D), lambda n: (n, 0)),
                  st_spec, st_spec, st_spec, st_spec, half, half],
        out_specs=[pl.BlockSpec((CHUNK, D), lambda n: (n, 0)),
                   pl.BlockSpec((CHUNK, D), lambda n: (n, 0)),
                   pl.BlockSpec((8, 128), lambda n: (0, 0))],
        out_shape=[jax.ShapeDtypeStruct((L, D), BF16), jax.ShapeDtypeStruct((L, D), BF16),
                   jax.ShapeDtypeStruct((8, 128), F32)],
        compiler_params=_cparams(("arbitrary",)),
    )(lg, qr, kr, p, do, sf_prev, sb_prev, dsf, dsb, c2, s2)


def _ctx_bwd(pc, ds0, lg, D):
    Lc = pc.shape[0]
    H = D // DV
    HP = H // 2
    W = D // 2

    def body(lg_ref, k_ref, v_ref, ds_ref, dqk_ref, dv_ref, st_ref):
        m = lax.broadcasted_iota(jnp.int32, (Lc, 128), 0).astype(F32)
        lane = lax.broadcasted_iota(jnp.int32, (1, 128), 1)
        acc_f = jnp.zeros((1, 128), F32)
        acc_b = jnp.zeros((1, 128), F32)
        dqk_ref[:, 0:W] = jnp.zeros((Lc, W), BF16)
        for pr in range(HP):
            ps = slice(pr * 128, (pr + 1) * 128)
            k2 = k_ref[:, ps].astype(F32) * K_SCALE
            dsfp, dsbp = ds_ref[0, pr].astype(BF16), ds_ref[1, pr].astype(BF16)
            dk2 = jnp.zeros((Lc, 128), F32)
            for e in range(2):
                h = 2 * pr + e
                sl = slice(h * DV, (h + 1) * DV)
                hm = _head_lane_mask(k2.shape, e)
                km = jnp.where(hm, k2, 0.0)
                v = v_ref[:, sl]
                vf = v.astype(F32)
                dec_f = jnp.exp(lg_ref[0, h] * (Lc - 1.0 - m))
                dec_b = jnp.exp(lg_ref[1, h] * m)
                kdf = _dot((km * dec_f).astype(BF16), dsfp)
                kdb = _dot((km * dec_b).astype(BF16), dsbp)
                dv_ref[:, sl] = (kdf + kdb).astype(BF16)
                dk2 += jnp.where(hm, dec_f * _dot_nt(v, dsfp) + dec_b * _dot_nt(v, dsbp), 0.0)
                acc_f += jnp.where(lane == h, _sum_all(kdf * vf * (Lc - 1.0 - m)), 0.0)
                acc_b += jnp.where(lane == h, _sum_all(kdb * vf * m), 0.0)
            dqk_ref[:, W + pr * 128:W + (pr + 1) * 128] = (dk2 * K_SCALE).astype(BF16)
        st_ref[...] = jnp.zeros_like(st_ref)
        st_ref[0:1, :] = acc_f
        st_ref[1:2, :] = acc_b

    return pl.pallas_call(
        body, name="ctx_bwd", grid=(1,),
        in_specs=[_smem_spec(), pl.BlockSpec((Lc, W), lambda i: (0, 9)), pl.BlockSpec((Lc, D), lambda i: (0, 5)),
                  pl.BlockSpec((2, HP, 128, 128), lambda i: (0, 0, 0, 0))],
        out_specs=[pl.BlockSpec((Lc, D), lambda i: (0, 0)), pl.BlockSpec((Lc, D), lambda i: (0, 0)),
                   pl.BlockSpec((8, 128), lambda i: (0, 0))],
        out_shape=[jax.ShapeDtypeStruct((Lc, D), BF16), jax.ShapeDtypeStruct((Lc, D), BF16),
                   jax.ShapeDtypeStruct((8, 128), F32)],
    )(lg, pc, pc, ds0)


def _dxm(pieces, cols, w, x, nw, sc, dx1, name):
    L, D = x.shape
    tm = min(512, L)
    nk = len(pieces)
    with_dx = dx1 is not None

    def body(*refs):
        piece_refs = refs[:nk]
        w_ref, x_ref, nw_ref, sc_ref = refs[nk:nk + 4]
        rest = refs[nk + 4:]
        if with_dx:
            dx1_ref, gx_ref, st_ref, acc = rest
        else:
            st_ref, acc = rest
        i, k = pl.program_id(0), pl.program_id(1)

        @pl.when((i == 0) & (k == 0))
        def _():
            st_ref[...] = jnp.zeros_like(st_ref)

        for kk in range(nk):
            @pl.when(k == kk)
            def _(kk=kk):
                part = _dot_nt(piece_refs[kk][...], w_ref[...])
                if kk == 0:
                    acc[...] = part
                else:
                    acc[...] += part

        @pl.when(k == nk - 1)
        def _():
            dxm = acc[...]
            xv = x_ref[...]
            r = lax.rsqrt(jnp.mean(xv * xv, axis=-1, keepdims=True) + EPS)
            xh = xv * r
            nwv = nw_ref[...]
            dxn = dxm * (1.0 + sc_ref[...])
            st_ref[0:1, :] += jnp.sum(dxm, axis=0, keepdims=True)
            st_ref[1:2, :] += jnp.sum(dxm * (xh * nwv), axis=0, keepdims=True)
            st_ref[2:3, :] += jnp.sum(dxn * xh, axis=0, keepdims=True)
            if with_dx:
                dxh = dxn * nwv
                gx_ref[...] = dx1_ref[...] + r * (dxh - xh * jnp.mean(dxh * xh, axis=-1, keepdims=True))

    row = pl.BlockSpec((tm, D), lambda i, k: (i, 0))
    w_spec = pl.BlockSpec((D, D), lambda i, k: (0, _select(k, cols)))
    in_specs = [row] * nk + [w_spec, row, _vec_spec(D), _vec_spec(D)]
    out_specs = [pl.BlockSpec((8, D), lambda i, k: (0, 0))]
    out_shape = [jax.ShapeDtypeStruct((8, D), F32)]
    args = list(pieces) + [w, x, nw, sc]
    if with_dx:
        in_specs.append(row)
        out_specs.insert(0, row)
        out_shape.insert(0, jax.ShapeDtypeStruct((L, D), F32))
        args.append(dx1)
    res = pl.pallas_call(
        body, name=name, grid=(L // tm, nk),
        in_specs=in_specs, out_specs=out_specs, out_shape=out_shape,
        scratch_shapes=[pltpu.VMEM((tm, D), F32)],
        compiler_params=_cparams(("arbitrary", "arbitrary"), VMEM_LIMIT),
    )(*args)
    return (res[0], res[1]) if with_dx else (None, res[0])


def _select(k, values):
    out = values[-1]
    for idx in range(len(values) - 2, -1, -1):
        out = jnp.where(k == idx, values[idx], out)
    return out


def _dw_in(xmt, pieces, cmt, dqk_c, dv_c, D):
    L = xmt.shape[1]
    Lc = cmt.shape[1]
    tk = min(512, L)
    nk = L // tk
    nj = len(pieces)

    def body(*refs):
        xt_ref = refs[0]
        piece_refs = refs[1:1 + nj]
        ct_ref, dqkc_ref, dvc_ref, o_ref = refs[1 + nj:]
        j, k = pl.program_id(0), pl.program_id(1)
        for jj in range(nj):
            @pl.when(j == jj)
            def _(jj=jj):
                part = _dot(xt_ref[...], piece_refs[jj][...])

                @pl.when(k == 0)
                def _():
                    o_ref[...] = part

                @pl.when(k > 0)
                def _():
                    o_ref[...] += part

        @pl.when((k == nk - 1) & (j == 4))
        def _():
            o_ref[...] += _dot(ct_ref[...], dqkc_ref[...])

        @pl.when((k == nk - 1) & (j == 5))
        def _():
            o_ref[...] += _dot(ct_ref[...], dvc_ref[...])

    def piece_spec(jj):
        return pl.BlockSpec((tk, D), lambda j, k: (jnp.where(j == jj, k, 0), 0))

    full_c = pl.BlockSpec((Lc, D), lambda j, k: (0, 0))
    return pl.pallas_call(
        body, name="dw_in", grid=(nj, nk),
        in_specs=[pl.BlockSpec((D, tk), lambda j, k: (0, k))] + [piece_spec(jj) for jj in range(nj)]
                 + [pl.BlockSpec((D, Lc), lambda j, k: (0, 0)), full_c, full_c],
        out_specs=pl.BlockSpec((D, D), lambda j, k: (0, j)),
        out_shape=jax.ShapeDtypeStruct((D, nj * D), F32),
        compiler_params=_cparams(("arbitrary", "arbitrary"), VMEM_LIMIT),
    )(xmt, *pieces, cmt, dqk_c, dv_c)


def _local_step(x, ctx, tgt, mod_x, mod_c, norm_w, conv_w8, conv_b, lg, gn_w, fw, w_in, w3):
    L, D = x.shape
    sh_x, sc_x, g_x = mod_x[0:1], mod_x[1:2], mod_x[2:3]
    sh_c, sc_c = mod_c[0:1], mod_c[1:2]
    c2, s2 = _rope_tables(L, D)

    xm, xmt = _norm_mod(x, norm_w, sc_x, sh_x, "norm_mod_x")
    cm, cmt = _norm_mod(ctx, norm_w, sc_c, sh_c, "norm_mod_ctx")
    p = _in_proj(xm, w_in, "in_proj_x")
    pc = _in_proj(cm, w_in, "in_proj_ctx")
    ya = _conv_gate_fwd(p, conv_w8, conv_b, D)
    qr, kr = _rope_fwd(p, c2, s2, D)
    s0 = _ctx_states(pc, lg, D)
    sf_prev, sb_prev = _ret_states(kr, p, s0, lg, D)
    o, yb = _ret_out(qr, kr, p, sf_prev, sb_prev, gn_w, lg, D)
    dx1, dya, dyb, dga, dgb, dw3, st_mid = _mid(ya, yb, p, x, tgt, w3, g_x, fw, D)
    dh, dbg, dcg, dza, st_conv = _conv_bwd(dya, p, conv_w8, conv_b, D)
    do, dzb, st_gn = _ret_bwd_pre(dyb, p, o, gn_w, D)
    dsf, dsb, ds0 = _ret_bwd_states(qr, do, lg, D)
    dqk, dv, st_lg = _ret_bwd_main(qr, kr, p, do, sf_prev, sb_prev, dsf, dsb, c2, s2, lg, D)
    dqk_c, dv_c, st_lgc = _ctx_bwd(pc, ds0, lg, D)
    pieces = (dh, dbg, dcg, dza, dqk, dv, dzb, dga, dgb)
    grad_x, st_x = _dxm(pieces, tuple(range(9)), w_in, x, norm_w, sc_x, dx1, "dxm_x")
    _, st_c = _dxm((dqk_c, dv_c), (4, 5), w_in, ctx, norm_w, sc_c, None, "dxm_ctx")
    dw_in = _dw_in(xmt, pieces, cmt, dqk_c, dv_c, D)
    return grad_x, dw_in, dw3, (st_mid, st_conv, st_gn, st_lg, st_lgc, st_x, st_c)


CHIP_FLIPS = (4, 2, 6)
ANY = pl.BlockSpec(memory_space=pl.ANY)
VMEM_FULL = pl.BlockSpec(memory_space=pltpu.VMEM)


def _position():
    return lax.axis_index("x"), lax.axis_index("y"), lax.axis_index("c")


def _peer(pos, k):
    x, y, c = pos
    return (1 - x if k & 4 else x, 1 - y if k & 2 else y, 1 - c if k & 1 else c)


def _dev_id(pos):
    return 4 * pos[0] + 2 * pos[1] + pos[2]


def _shard_of(pos):
    return 2 * pos[0] + pos[1]


def _remote(src, dst, send_sems, recv_sems, idx, to):
    return pltpu.make_async_remote_copy(src_ref=src, dst_ref=dst, send_sem=send_sems.at[idx],
                                        recv_sem=recv_sems.at[idx], device_id=to, device_id_type=MESH)


def _dot_f32(a, b):
    return jnp.dot(a, b, precision=lax.Precision.HIGHEST, preferred_element_type=F32)


def _silu(x):
    return x * _sigmoid(x)


def _fwd_small(c8, cctx8, ada_w, ada_b, conv_w8):
    D = c8.shape[1]
    Wm = ada_w.shape[1]
    Dq = conv_w8.shape[1]

    def body(c_ref, cc_ref, aw_ref, ab_ref, cw_ref, act_ref, mod_ref, cwf_ref,
             cbuf, pmine, pbuf, wbuf, s_c, r_c, s_p, r_p, s_w, r_w):
        pos = _position()
        me, s = _dev_id(pos), _shard_of(pos)
        cbuf[me] = c_ref[...]
        wbuf[s] = cw_ref[...]
        sends = [_remote(c_ref, cbuf.at[me], s_c, r_c, k - 1, _peer(pos, k)) for k in range(1, 8)]
        sends += [_remote(cw_ref, wbuf.at[s], s_w, r_w, j, _peer(pos, k)) for j, k in enumerate(CHIP_FLIPS)]
        for cp in sends:
            cp.start()
        for k in range(1, 8):
            _remote(c_ref, cbuf.at[_dev_id(_peer(pos, k))], s_c, r_c, k - 1, _peer(pos, k)).wait_recv()
        for d in range(N_DEV):
            act_ref[d:d + 1, :] = _silu(cbuf[d, 0:1, :])
        act_ref[8:9, :] = _silu(cc_ref[0:1, :])
        act_ref[9:16, :] = jnp.zeros((7, D), F32)
        part = _dot_f32(act_ref[...], aw_ref[...])
        pmine[...] = part
        pbuf[s] = part
        psend = [_remote(pmine, pbuf.at[s], s_p, r_p, j, _peer(pos, k)) for j, k in enumerate(CHIP_FLIPS)]
        for cp in psend:
            cp.start()
        for j, k in enumerate(CHIP_FLIPS):
            t = _shard_of(_peer(pos, k))
            _remote(pmine, pbuf.at[t], s_p, r_p, j, _peer(pos, k)).wait_recv()
            _remote(cw_ref, wbuf.at[t], s_w, r_w, j, _peer(pos, k)).wait_recv()
        for t in range(N_SHARD):
            mod_ref[:, t * Wm:(t + 1) * Wm] = pbuf[t] + ab_ref[:, t * Wm:(t + 1) * Wm]
            cwf_ref[:, t * Dq:(t + 1) * Dq] = wbuf[t]
        for cp in sends + psend:
            cp.wait_send()

    return pl.pallas_call(
        body, name="fwd_small",
        in_specs=[VMEM_FULL] * 5, out_specs=[VMEM_FULL] * 3,
        out_shape=[jax.ShapeDtypeStruct((16, D), F32), jax.ShapeDtypeStruct((16, 3 * D), F32),
                   jax.ShapeDtypeStruct((8, D), F32)],
        scratch_shapes=[pltpu.VMEM((N_DEV, 8, D), F32), pltpu.VMEM((16, Wm), F32),
                        pltpu.VMEM((N_SHARD, 16, Wm), F32), pltpu.VMEM((N_SHARD, 8, Dq), F32),
                        pltpu.SemaphoreType.DMA((7,)), pltpu.SemaphoreType.DMA((7,)),
                        pltpu.SemaphoreType.DMA((3,)), pltpu.SemaphoreType.DMA((3,)),
                        pltpu.SemaphoreType.DMA((3,)), pltpu.SemaphoreType.DMA((3,))],
        compiler_params=_cparams(None, VMEM_LIMIT),
    )(c8, cctx8, ada_w, ada_b, conv_w8)


def _cols(t, w):
    return pl.ds(pl.multiple_of(t * w, 128), w)


def _ag_weights(w_in_s, w3_s):
    D, Wc = w_in_s.shape
    Dh = D // 2
    Do = w3_s.shape[2]

    def body(wi_ref, w3_ref, fi_ref, f3_ref, si, s3, send, recv, loc):
        pos = _position()
        c = pos[2]
        s = _shard_of(pos)
        sib = _peer(pos, 1)

        def cast_rows(r, carry):
            rows = pl.ds(pl.multiple_of(r * 64, 64), 64)
            si[rows, :] = wi_ref[rows, :].astype(BF16)
            return carry

        lax.fori_loop(0, D // 64, cast_rows, 0)
        for a in range(3):
            s3[a] = w3_ref[a].astype(BF16)

        def in_half(t, hf):
            return fi_ref.at[pl.ds(hf * Dh, Dh), _cols(t, Wc)]

        def w3_half(t, hf):
            return f3_ref.at[:, t, hf]

        own = [pltpu.make_async_copy(si, fi_ref.at[:, _cols(s, Wc)], loc.at[0]),
               pltpu.make_async_copy(s3, f3_ref.at[:, s], loc.at[1])]
        first = []
        for j, k in enumerate(CHIP_FLIPS):
            to = _peer(pos, k)
            first.append(_remote(si.at[pl.ds(c * Dh, Dh), :], in_half(s, c), send, recv, j, to))
            first.append(_remote(s3.at[:, c], w3_half(s, c), send, recv, 3 + j, to))
        for cp in own + first:
            cp.start()
        passed = []
        for j, k in enumerate(CHIP_FLIPS):
            t = _shard_of(_peer(pos, k))
            _remote(in_half(t, c), in_half(t, c), send, recv, j, sib).wait_recv()
            fwd_i = _remote(in_half(t, c), in_half(t, c), send, recv, 6 + j, sib)
            fwd_i.start()
            _remote(w3_half(t, c), w3_half(t, c), send, recv, 3 + j, sib).wait_recv()
            fwd_3 = _remote(w3_half(t, c), w3_half(t, c), send, recv, 9 + j, sib)
            fwd_3.start()
            passed += [fwd_i, fwd_3]
        for j, k in enumerate(CHIP_FLIPS):
            t = _shard_of(_peer(pos, k))
            _remote(in_half(t, 1 - c), in_half(t, 1 - c), send, recv, 6 + j, sib).wait_recv()
            _remote(w3_half(t, 1 - c), w3_half(t, 1 - c), send, recv, 9 + j, sib).wait_recv()
        for cp in first + passed:
            cp.wait_send()
        for cp in own:
            cp.wait()

    return pl.pallas_call(
        body, name="ag_weights",
        in_specs=[VMEM_FULL, VMEM_FULL], out_specs=[ANY, ANY],
        out_shape=[jax.ShapeDtypeStruct((D, N_SHARD * Wc), BF16), jax.ShapeDtypeStruct((3, N_SHARD, 2, Do, D), BF16)],
        scratch_shapes=[pltpu.VMEM((D, Wc), BF16), pltpu.VMEM((3, 2, Do, D), BF16),
                        pltpu.SemaphoreType.DMA((12,)), pltpu.SemaphoreType.DMA((12,)), pltpu.SemaphoreType.DMA((2,))],
        compiler_params=_cparams(None, VMEM_LIMIT),
    )(w_in_s, w3_s)


def _rs_pair(dw_in, dw3):
    _, Dh, Wf = dw_in.shape
    _, _, _, Do, D = dw3.shape

    def body(gi_ref, g3_ref, ri_ref, r3_ref, send, recv):
        pos = _position()
        c = pos[2]
        sib = _peer(pos, 1)
        cps = [_remote(gi_ref.at[1 - c], ri_ref, send, recv, 0, sib),
               _remote(g3_ref.at[:, :, 1 - c], r3_ref, send, recv, 1, sib)]
        for cp in cps:
            cp.start()
        for cp in cps:
            cp.wait()

    return pl.pallas_call(
        body, name="rs_pair", in_specs=[ANY, ANY], out_specs=[ANY, ANY],
        out_shape=[jax.ShapeDtypeStruct((Dh, Wf), F32), jax.ShapeDtypeStruct((3, N_SHARD, Do, D), F32)],
        scratch_shapes=[pltpu.SemaphoreType.DMA((2,)), pltpu.SemaphoreType.DMA((2,))],
    )(dw_in, dw3)


def _sum_pair(cidx, dw_in, ri, dw3, r3):
    _, Dh, Wf = dw_in.shape
    Wc = Wf // N_SHARD
    _, _, _, Do, D = dw3.shape
    tr = min(256, Dh)

    def body_i(c_ref, a_ref, b_ref, o_ref):
        o_ref[...] = (a_ref[...] + b_ref[...]).astype(BF16)

    sum_i = pl.pallas_call(
        body_i, name="sum_pair_in",
        grid_spec=pltpu.PrefetchScalarGridSpec(
            num_scalar_prefetch=1, grid=(Dh // tr, N_SHARD),
            in_specs=[pl.BlockSpec((None, tr, Wc), lambda i, t, c: (c[0], i, t)),
                      pl.BlockSpec((tr, Wc), lambda i, t, c: (i, t))],
            out_specs=pl.BlockSpec((None, tr, Wc), lambda i, t, c: (t, i, 0))),
        out_shape=jax.ShapeDtypeStruct((N_SHARD, Dh, Wc), BF16),
        compiler_params=_cparams(("parallel", "parallel")),
    )(cidx, dw_in, ri)

    def body_3(c_ref, a_ref, b_ref, o_ref):
        o_ref[...] = (a_ref[...] + b_ref[...]).astype(BF16)

    sum_3 = pl.pallas_call(
        body_3, name="sum_pair_w3",
        grid_spec=pltpu.PrefetchScalarGridSpec(
            num_scalar_prefetch=1, grid=(3, N_SHARD),
            in_specs=[pl.BlockSpec((None, None, None, Do, D), lambda a, t, c: (a, t, c[0], 0, 0)),
                      pl.BlockSpec((None, None, Do, D), lambda a, t, c: (a, t, 0, 0))],
            out_specs=pl.BlockSpec((None, None, Do, D), lambda a, t, c: (a, t, 0, 0))),
        out_shape=jax.ShapeDtypeStruct((3, N_SHARD, Do, D), BF16),
        compiler_params=_cparams(("parallel", "parallel")),
    )(cidx, dw3, r3)
    return sum_i, sum_3


def _rs_chips(cs_in, cs_3):
    _, Dh, Wc = cs_in.shape
    _, _, Do, D = cs_3.shape

    def body(ci_ref, c3_ref, ri_ref, r3_ref, send, recv):
        pos = _position()
        cps = []
        for j, k in enumerate(CHIP_FLIPS):
            to = _peer(pos, k)
            t = _shard_of(to)
            cps.append(_remote(ci_ref.at[t], ri_ref.at[j], send, recv, j, to))
            cps.append(_remote(c3_ref.at[:, t], r3_ref.at[j], send, recv, 3 + j, to))
        for cp in cps:
            cp.start()
        for cp in cps:
            cp.wait()

    return pl.pallas_call(
        body, name="rs_chips", in_specs=[ANY, ANY], out_specs=[ANY, ANY],
        out_shape=[jax.ShapeDtypeStruct((3, Dh, Wc), BF16), jax.ShapeDtypeStruct((3, 3, Do, D), BF16)],
        scratch_shapes=[pltpu.SemaphoreType.DMA((6,)), pltpu.SemaphoreType.DMA((6,))],
    )(cs_in, cs_3)


def _sum_chips(csidx, cs_in, rb_in, cs_3, rb_3):
    _, Dh, Wc = cs_in.shape
    _, _, Do, D = cs_3.shape
    tr = min(256, Dh)

    def body_i(s_ref, a_ref, b_ref, o_ref):
        acc = a_ref[...].astype(F32)
        for j in range(3):
            acc = acc + b_ref[j].astype(F32)
        o_ref[...] = acc

    g_in = pl.pallas_call(
        body_i, name="sum_chips_in",
        grid_spec=pltpu.PrefetchScalarGridSpec(
            num_scalar_prefetch=1, grid=(Dh // tr,),
            in_specs=[pl.BlockSpec((None, tr, Wc), lambda i, s: (s[1], i, 0)),
                      pl.BlockSpec((3, tr, Wc), lambda i, s: (0, i, 0))],
            out_specs=pl.BlockSpec((None, tr, Wc), lambda i, s: (s[0], i, 0))),
        out_shape=jax.ShapeDtypeStruct((2, Dh, Wc), F32),
        compiler_params=_cparams(("parallel",)),
    )(csidx, cs_in, rb_in)

    def body_3(s_ref, a_ref, b_ref, o_ref):
        acc = a_ref[...].astype(F32)
        for j in range(3):
            acc = acc + b_ref[j].astype(F32)
        o_ref[...] = acc

    g_3 = pl.pallas_call(
        body_3, name="sum_chips_w3",
        grid_spec=pltpu.PrefetchScalarGridSpec(
            num_scalar_prefetch=1, grid=(3,),
            in_specs=[pl.BlockSpec((None, None, Do, D), lambda a, s: (a, s[1], 0, 0)),
                      pl.BlockSpec((3, None, Do, D), lambda a, s: (0, a, 0, 0))],
            out_specs=pl.BlockSpec((None, None, Do, D), lambda a, s: (a, s[0], 0, 0))),
        out_shape=jax.ShapeDtypeStruct((3, 2, Do, D), F32),
        compiler_params=_cparams(("parallel",)),
    )(csidx, cs_3, rb_3)
    return g_in, g_3


def _rs_final(g_in, g_3):
    def body(hi_ref, h3_ref, gi_ref, g3_ref, send, recv):
        pos = _position()
        c = pos[2]
        sib = _peer(pos, 1)
        cps = [_remote(hi_ref.at[c], gi_ref.at[c], send, recv, 0, sib),
               _remote(h3_ref.at[:, c], g3_ref.at[:, c], send, recv, 1, sib)]
        for cp in cps:
            cp.start()
        _remote(hi_ref.at[1 - c], gi_ref.at[1 - c], send, recv, 0, sib).wait_recv()
        _remote(h3_ref.at[:, 1 - c], g3_ref.at[:, 1 - c], send, recv, 1, sib).wait_recv()
        for cp in cps:
            cp.wait_send()

    return pl.pallas_call(
        body, name="rs_final", in_specs=[ANY, ANY], out_specs=[ANY, ANY],
        out_shape=[jax.ShapeDtypeStruct(g_in.shape, F32), jax.ShapeDtypeStruct(g_3.shape, F32)],
        input_output_aliases={0: 0, 1: 1},
        scratch_shapes=[pltpu.SemaphoreType.DMA((2,)), pltpu.SemaphoreType.DMA((2,))],
    )(g_in, g_3)


def _adam_math(w, g, m, v):
    m = ADAM_B1 * m + (1.0 - ADAM_B1) * g
    v = ADAM_B2 * v + (1.0 - ADAM_B2) * (g * g)
    m_hat = m / (1.0 - ADAM_B1 ** ADAM_STEP)
    v_hat = v / (1.0 - ADAM_B2 ** ADAM_STEP)
    delta = -ADAM_LR * (m_hat / (jnp.sqrt(v_hat) + ADAM_EPS) + ADAM_WD * w)
    return delta, m, v


def _adamw(w, g, m, v, name):
    R, C = w.shape
    tr = min(128, R)

    def body(w_ref, g_ref, m_ref, v_ref, d_ref, nm_ref, nv_ref):
        d_ref[...], nm_ref[...], nv_ref[...] = _adam_math(w_ref[...], g_ref[...], m_ref[...], v_ref[...])

    blk = pl.BlockSpec((tr, C), lambda i: (i, 0))
    return pl.pallas_call(
        body, name=name, grid=(R // tr,), in_specs=[blk] * 4, out_specs=[blk] * 3,
        out_shape=[jax.ShapeDtypeStruct((R, C), F32)] * 3,
        compiler_params=_cparams(("parallel",), VMEM_LIMIT),
    )(w, g, m, v)


def _bwd_small(vec, act, ada_w, cctx8, w_sm, m_sm, v_sm, w_ab, m_ab, v_ab, w_cw, m_cw, v_cw, w_dl, m_dl, v_dl):
    D = vec.shape[1]
    Wm = ada_w.shape[1]
    Dq = w_cw.shape[1]

    def body(vec_ref, act_ref, aw_ref, cc_ref, wsm, msm, vsm, wab, mab, vab, wcw, mcw, vcw, wdl, mdl, vdl,
             gaw_ref, loss_ref, o_sm, o_ab, o_cw, o_dl,
             vbuf, dm, dm_sh, gcw, amine, abuf, s_v, r_v, s_a, r_a):
        pos = _position()
        me, s = _dev_id(pos), _shard_of(pos)
        vbuf[me] = vec_ref[...]
        sends = [_remote(vec_ref, vbuf.at[me], s_v, r_v, k - 1, _peer(pos, k)) for k in range(1, 8)]
        for cp in sends:
            cp.start()
        for k in range(1, 8):
            _remote(vec_ref, vbuf.at[_dev_id(_peer(pos, k))], s_v, r_v, k - 1, _peer(pos, k)).wait_recv()
        tot = vbuf[0]
        for d in range(1, N_DEV):
            tot = tot + vbuf[d]
        loss_ref[...] = jnp.zeros((8, 128), F32) + (0.5 / D) * _sum_all(tot[14:15, :])
        dm[...] = jnp.zeros_like(dm)
        for d in range(N_DEV):
            for r in range(3):
                dm[d:d + 1, r * D:(r + 1) * D] = vbuf[d, r:r + 1, :]
        dm[8:9, 0:D] = tot[3:4, :]
        dm[8:9, D:2 * D] = tot[4:5, :]
        for t in range(N_SHARD):
            @pl.when(s == t)
            def _(t=t):
                dm_sh[...] = dm[:, t * Wm:(t + 1) * Wm]
                gcw[...] = jnp.zeros_like(gcw)
                gcw[0:3, :] = tot[9:12, t * Dq:(t + 1) * Dq]
        gaw_ref[...] = lax.dot_general(act_ref[...], dm_sh[...], (((0,), (0,)), ((), ())),
                                       precision=lax.Precision.HIGHEST, preferred_element_type=F32)
        part = lax.dot_general(dm_sh[8:16, :], aw_ref[...], (((1,), (1,)), ((), ())),
                               precision=lax.Precision.HIGHEST, preferred_element_type=F32)
        amine[...] = part
        abuf[s] = part
        asend = [_remote(amine, abuf.at[s], s_a, r_a, j, _peer(pos, k)) for j, k in enumerate(CHIP_FLIPS)]
        for cp in asend:
            cp.start()
        for j, k in enumerate(CHIP_FLIPS):
            _remote(amine, abuf.at[_shard_of(_peer(pos, k))], s_a, r_a, j, _peer(pos, k)).wait_recv()
        da = abuf[0]
        for t in range(1, N_SHARD):
            da = da + abuf[t]
        cc = cc_ref[0:1, :]
        sg = _sigmoid(cc)
        g_cctx = da[0:1, :] * (sg * (1.0 + cc * (1.0 - sg)))

        def emit(o_ref, w, g, m, v):
            o_ref[0] = g
            o_ref[1], o_ref[2], o_ref[3] = _adam_math(w, g, m, v)

        g_sm = jnp.concatenate([g_cctx, tot[5:9, :], jnp.zeros((3, D), F32)], axis=0)
        emit(o_sm, wsm[...], g_sm, msm[...], vsm[...])
        g_ab = jnp.concatenate([tot[0:1, :] + tot[3:4, :], tot[1:2, :] + tot[4:5, :], tot[2:3, :],
                                jnp.zeros((5, D), F32)], axis=0)
        emit(o_ab, wab[...], g_ab, mab[...], vab[...])
        emit(o_cw, wcw[...], gcw[...], mcw[...], vcw[...])
        g_dl = jnp.concatenate([tot[12:14, 0:128] * _sigmoid(-wdl[0:2, :]), jnp.zeros((6, 128), F32)], axis=0)
        emit(o_dl, wdl[...], g_dl, mdl[...], vdl[...])
        for cp in sends + asend:
            cp.wait_send()

    return pl.pallas_call(
        body, name="bwd_small",
        in_specs=[VMEM_FULL] * 16, out_specs=[VMEM_FULL] * 6,
        out_shape=[jax.ShapeDtypeStruct((D, Wm), F32), jax.ShapeDtypeStruct((8, 128), F32),
                   jax.ShapeDtypeStruct((4, 8, D), F32), jax.ShapeDtypeStruct((4, 8, D), F32),
                   jax.ShapeDtypeStruct((4, 8, Dq), F32), jax.ShapeDtypeStruct((4, 8, 128), F32)],
        scratch_shapes=[pltpu.VMEM((N_DEV, 16, D), F32), pltpu.VMEM((16, 3 * D), F32), pltpu.VMEM((16, Wm), F32),
                        pltpu.VMEM((8, Dq), F32), pltpu.VMEM((8, D), F32), pltpu.VMEM((N_SHARD, 8, D), F32),
                        pltpu.SemaphoreType.DMA((7,)), pltpu.SemaphoreType.DMA((7,)),
                        pltpu.SemaphoreType.DMA((3,)), pltpu.SemaphoreType.DMA((3,))],
        compiler_params=_cparams(None, VMEM_LIMIT),
    )(vec, act, ada_w, cctx8, w_sm, m_sm, v_sm, w_ab, m_ab, v_ab, w_cw, m_cw, v_cw, w_dl, m_dl, v_dl)


def _pad_rows(a, rows=8):
    return jnp.pad(a, ((0, rows - a.shape[0]), (0, 0)))


def kernel(x, c, ctx, c_ctx, norm_w, ada_w, ada_b, w_in, conv_w, conv_b, decay_logit, gn_w, w_a, w_b, w_out, final_norm_w, loss_target, m_c_ctx, m_norm_w, m_ada_w, m_ada_b, m_w_in, m_conv_w, m_conv_b, m_decay_logit, m_gn_w, m_w_a, m_w_b, m_w_out, m_final_norm_w, v_c_ctx, v_norm_w, v_ada_w, v_ada_b, v_w_in, v_conv_w, v_conv_b, v_decay_logit, v_gn_w, v_w_a, v_w_b, v_w_out, v_final_norm_w):
    L, D = x.shape[1], x.shape[2]
    H = D // DV
    Wc = w_in.shape[2]
    Do = D // 8
    pos = _position()
    me = _dev_id(pos)
    cidx = jnp.reshape(pos[2], (1,)).astype(jnp.int32)
    sidx = jnp.reshape(_shard_of(pos), (1,)).astype(jnp.int32)

    act, mod, conv_w8 = _fwd_small(_pad_rows(c), _pad_rows(c_ctx[None]), ada_w[0], ada_b, _pad_rows(conv_w[0]))
    mod_x = lax.dynamic_slice_in_dim(mod, me, 1, axis=0).reshape(3, D)
    mod_c = mod[8].reshape(3, D)
    lg = jax.nn.log_sigmoid(decay_logit[0])

    w3_s = jnp.stack([w_a[0], w_b[0], w_out[0]]).reshape(3, 2, Do, D)
    w_in_full, w3_full = _ag_weights(w_in[0], w3_s)
    w3_full = w3_full.reshape(3, D, D)

    grad_x, dw_in, dw3, sts = _local_step(x[0], ctx[0], loss_target[0], mod_x, mod_c, norm_w, conv_w8, conv_b, lg,
                                          gn_w, final_norm_w[None], w_in_full, w3_full)
    st_mid, st_conv, st_gn, st_lg, st_lgc, st_x, st_c = sts

    dw_in2 = dw_in.reshape(2, D // 2, N_SHARD * Wc)
    dw3_5 = dw3.reshape(3, N_SHARD, 2, Do, D)
    ra_in, ra_3 = _rs_pair(dw_in2, dw3_5)
    cs_in, cs_3 = _sum_pair(cidx, dw_in2, ra_in, dw3_5, ra_3)
    rb_in, rb_3 = _rs_chips(cs_in, cs_3)
    gh_in, gh_3 = _sum_chips(jnp.concatenate([cidx, sidx]), cs_in, rb_in, cs_3, rb_3)
    g_in, g_3 = _rs_final(gh_in, gh_3)
    g_w_in = g_in.reshape(D, Wc)
    g_3 = g_3.reshape(3, D // 4, D)

    lanes = lambda a: jnp.pad(a, ((0, 0), (0, D - a.shape[1])))
    vec = jnp.concatenate([
        st_x[0:2], st_mid[1:2], st_c[0:2], st_x[2:3] + st_c[2:3], st_conv[3:4], st_gn[0:1], st_mid[0:1],
        st_conv[0:3], lanes(st_lg[0:2] + st_lgc[0:2]), st_mid[2:3], jnp.zeros((1, D), F32)], axis=0)
    small = lambda a, b_, c_, d_, e_: _pad_rows(jnp.concatenate([a[None], b_, c_, d_, e_[None]], axis=0))
    dl = lambda a: jnp.pad(a[0], ((0, 6), (0, 128 - H)))
    g_ada_w, loss_t, o_sm, o_ab, o_cw, o_dl = _bwd_small(
        vec, act, ada_w[0], _pad_rows(c_ctx[None]),
        small(c_ctx, norm_w, conv_b, gn_w, final_norm_w), small(m_c_ctx, m_norm_w, m_conv_b, m_gn_w, m_final_norm_w),
        small(v_c_ctx, v_norm_w, v_conv_b, v_gn_w, v_final_norm_w),
        _pad_rows(ada_b.reshape(3, D)), _pad_rows(m_ada_b.reshape(3, D)), _pad_rows(v_ada_b.reshape(3, D)),
        _pad_rows(conv_w[0]), _pad_rows(m_conv_w[0]), _pad_rows(v_conv_w[0]),
        dl(decay_logit), dl(m_decay_logit), dl(v_decay_logit))

    upd_in = _adamw(w_in[0], g_w_in, m_w_in[0], v_w_in[0], "adamw_w_in")
    upd_ada = _adamw(ada_w[0], g_ada_w, m_ada_w[0], v_ada_w[0], "adamw_ada_w")
    upd_a = _adamw(w_a[0], g_3[0], m_w_a[0], v_w_a[0], "adamw_w_a")
    upd_b = _adamw(w_b[0], g_3[1], m_w_b[0], v_w_b[0], "adamw_w_b")
    upd_o = _adamw(w_out[0], g_3[2], m_w_out[0], v_w_out[0], "adamw_w_out")

    def leaves(q):
        big = lambda g, upd: (g if q == 0 else upd[q - 1])[None]
        sm = o_sm[q]
        return [sm[0], sm[1:2], big(g_ada_w, upd_ada), o_ab[q][0:3].reshape(1, 3 * D), big(g_w_in, upd_in),
                o_cw[q][0:3][None], sm[2:3], o_dl[q][0:2, 0:H][None], sm[3:4],
                big(g_3[0], upd_a), big(g_3[1], upd_b), big(g_3[2], upd_o), sm[4]]

    loss = loss_t[0, 0]
    return (loss, grad_x[None], *leaves(0), *leaves(1), *leaves(2), *leaves(3))
```

```python
import functools

import jax
import jax.numpy as jnp
from jax import lax
from jax.experimental import pallas as pl
from jax.experimental.pallas import tpu as pltpu

F32 = jnp.float32
BF16 = jnp.bfloat16
MESH = pl.DeviceIdType.MESH

CHUNK = 128
DV = 128
DK = 64
GRID_W = 64
ROPE_BASE = 10000.0
EPS = 1e-6
K_SCALE = DK ** -0.5
N_SHARD = 4
N_DEV = 8

ADAM_LR = 0.001
ADAM_B1 = 0.9
ADAM_B2 = 0.999
ADAM_EPS = 1e-08
ADAM_WD = 0.01
ADAM_STEP = 10

VMEM_LIMIT = 56 * 1024 * 1024


def _cparams(sem=None, vmem=None):
    kw = {}
    if sem is not None:
        kw["dimension_semantics"] = sem
    if vmem is not None:
        kw["vmem_limit_bytes"] = vmem
    return pltpu.CompilerParams(**kw)


def _dot(a, b):
    return jnp.dot(a, b, preferred_element_type=F32)


def _dot_nt(a, b):
    return lax.dot_general(a, b, (((1,), (1,)), ((), ())), preferred_element_type=F32)


def _dot_tn(a, b):
    return lax.dot_general(a, b, (((0,), (0,)), ((), ())), preferred_element_type=F32)


def _sigmoid(x):
    return 1.0 / (1.0 + jnp.exp(-x))


def _sum_all(x):
    return jnp.sum(jnp.sum(x, axis=1, keepdims=True), axis=0, keepdims=True)


def _swap_halves(t):
    n = t.shape[1]
    lane = lax.broadcasted_iota(jnp.int32, t.shape, 1)
    low = (lane & 32) == 0
    return jnp.where(low, pltpu.roll(t, n - 32, 1), pltpu.roll(t, 32, 1))


def _vec_spec(d):
    return pl.BlockSpec((1, d), lambda *a: (0, 0))


def _norm_mod(x, nw, sc, sh, name):
    L, D = x.shape
    tl = min(256, L)

    def body(x_ref, nw_ref, sc_ref, sh_ref, xm_ref, xmt_ref):
        xv = x_ref[...]
        r = lax.rsqrt(jnp.mean(xv * xv, axis=-1, keepdims=True) + EPS)
        xm = (xv * r * nw_ref[...]) * (1.0 + sc_ref[...]) + sh_ref[...]
        xm_ref[...] = xm.astype(BF16)
        xmt_ref[...] = xm.T.astype(BF16)

    return pl.pallas_call(
        body, name=name, grid=(L // tl,),
        in_specs=[pl.BlockSpec((tl, D), lambda i: (i, 0)), _vec_spec(D), _vec_spec(D), _vec_spec(D)],
        out_specs=[pl.BlockSpec((tl, D), lambda i: (i, 0)), pl.BlockSpec((D, tl), lambda i: (0, i))],
        out_shape=[jax.ShapeDtypeStruct((L, D), BF16), jax.ShapeDtypeStruct((D, L), BF16)],
        compiler_params=_cparams(("parallel",)),
    )(x, nw, sc, sh)


def _in_proj(xm, w, name):
    M, D = xm.shape
    N = w.shape[1]
    tm = min(1024, M)

    def body(a_ref, b_ref, o_ref, qk_ref):
        acc = _dot(a_ref[...], b_ref[...])
        o_ref[...] = acc.astype(o_ref.dtype)

        @pl.when(pl.program_id(1) == 4)
        def _():
            qk_ref[...] = acc

    return pl.pallas_call(
        body, name=name, grid=(M // tm, N // D),
        in_specs=[pl.BlockSpec((tm, D), lambda i, j: (i, 0)), pl.BlockSpec((D, D), lambda i, j: (0, j))],
        out_specs=[pl.BlockSpec((tm, D), lambda i, j: (i, j)), pl.BlockSpec((tm, D), lambda i, j: (i, 0))],
        out_shape=[jax.ShapeDtypeStruct((M, N), BF16), jax.ShapeDtypeStruct((M, D), F32)],
        compiler_params=_cparams(("parallel", "arbitrary")),
    )(xm, w)


def _halo_specs(tl, L, D, col):
    hb = tl // 16
    last = L // 16 - 1
    prev = pl.BlockSpec((16, D), lambda i: (jnp.maximum(i * hb - 1, 0), col))
    nxt = pl.BlockSpec((16, D), lambda i: (jnp.minimum((i + 1) * hb, last), col))
    return prev, nxt


def _shift_rows(u, above, below):
    tl = u.shape[0]
    row = lax.broadcasted_iota(jnp.int32, u.shape, 0)
    dn = jnp.where(row == 0, above, pltpu.roll(u, 1, 0))
    up = jnp.where(row == tl - 1, below, pltpu.roll(u, tl - 1, 0))
    return dn, up


def _conv_gate_fwd(p, conv_w, conv_b, D):
    L = p.shape[0]
    tl = min(256, L)
    nt = L // tl

    def body(h_ref, bg_ref, cg_ref, za_ref, hp_ref, hn_ref, cp_ref, cn_ref, w_ref, b_ref, o_ref):
        i = pl.program_id(0)
        u = cg_ref[...].astype(F32) * h_ref[...].astype(F32)
        above = cp_ref[15:16, :].astype(F32) * hp_ref[15:16, :].astype(F32)
        below = cn_ref[0:1, :].astype(F32) * hn_ref[0:1, :].astype(F32)
        above = jnp.where(i == 0, 0.0, above)
        below = jnp.where(i == nt - 1, 0.0, below)
        dn, up = _shift_rows(u, above, below)
        co = w_ref[0:1, :] * dn + w_ref[1:2, :] * u + w_ref[2:3, :] * up + b_ref[...]
        za = za_ref[...].astype(F32)
        o_ref[...] = (za * _sigmoid(za) * bg_ref[...].astype(F32) * co).astype(BF16)

    main = lambda col: pl.BlockSpec((tl, D), lambda i: (i, col))
    hp, hn = _halo_specs(tl, L, D, 0)
    cp, cn = _halo_specs(tl, L, D, 2)
    return pl.pallas_call(
        body, name="conv_gate_fwd", grid=(nt,),
        in_specs=[main(0), main(1), main(2), main(3), hp, hn, cp, cn,
                  pl.BlockSpec((8, D), lambda i: (0, 0)), _vec_spec(D)],
        out_specs=pl.BlockSpec((tl, D), lambda i: (i, 0)),
        out_shape=jax.ShapeDtypeStruct((L, D), BF16),
        compiler_params=_cparams(("parallel",)),
    )(p, p, p, p, p, p, p, p, conv_w, conv_b)


def _rope_tables(L, D):
    pos = jnp.arange(L)
    row = (pos // GRID_W).astype(F32)
    col = (pos % GRID_W).astype(F32)
    nf = DK // 4
    inv = ROPE_BASE ** (-jnp.arange(nf, dtype=F32) / nf)
    ang = jnp.concatenate([row[:, None] * inv, col[:, None] * inv], axis=-1)
    cos, sin = jnp.cos(ang), jnp.sin(ang)
    heads = D // DV
    c2 = jnp.tile(jnp.concatenate([cos, cos], axis=-1), (1, heads))
    s2 = jnp.tile(jnp.concatenate([-sin, sin], axis=-1), (1, heads))
    return c2, s2


def _rope_fwd(pqk, c2, s2, D):
    L = pqk.shape[0]
    W = D // 2
    tl = min(256, L)

    def body(q_ref, k_ref, c_ref, s_ref, qo_ref, ko_ref):
        c, s = c_ref[...], s_ref[...]
        q = q_ref[...].astype(F32)
        k = k_ref[...].astype(F32) * K_SCALE
        qo_ref[...] = (q * c + _swap_halves(q) * s).astype(BF16)
        ko_ref[...] = (k * c + _swap_halves(k) * s).astype(BF16)

    blk = lambda col: pl.BlockSpec((tl, W), lambda i: (i, col))
    return pl.pallas_call(
        body, name="rope_fwd", grid=(L // tl,),
        in_specs=[blk(0), blk(1), blk(0), blk(0)],
        out_specs=[blk(0), blk(0)],
        out_shape=[jax.ShapeDtypeStruct((L, W), BF16)] * 2,
        compiler_params=_cparams(("parallel",)),
    )(pqk, pqk, c2, s2)


def _smem_spec():
    return pl.BlockSpec(memory_space=pltpu.SMEM)


def _pair_select(e0, e1):
    row = lax.broadcasted_iota(jnp.int32, e0.shape, 0)
    return jnp.where(row < DK, e0, e1)


def _head_lane_mask(shape, e):
    lane = lax.broadcasted_iota(jnp.int32, shape, 1)
    return (lane < DK) if e == 0 else (lane >= DK)


def _ctx_states(pc, pqk_c, lg, D):
    Lc = pc.shape[0]
    H = D // DV

    def body(lg_ref, k_ref, v_ref, s_ref):
        m = lax.broadcasted_iota(jnp.int32, (Lc, DV), 0).astype(F32)
        for pr in range(H // 2):
            k2 = k_ref[:, pr * 128:(pr + 1) * 128].astype(F32) * K_SCALE
            res = [[None, None], [None, None]]
            for e in range(2):
                h = 2 * pr + e
                v = v_ref[:, h * DV:(h + 1) * DV]
                dec_f = jnp.exp(lg_ref[0, h] * (Lc - 1.0 - m))
                dec_b = jnp.exp(lg_ref[1, h] * m)
                res[0][e] = _dot_tn((k2 * dec_f).astype(BF16), v)
                res[1][e] = _dot_tn((k2 * dec_b).astype(BF16), v)
            s_ref[0, pr] = _pair_select(res[0][0], res[0][1])
            s_ref[1, pr] = _pair_select(res[1][0], res[1][1])

    return pl.pallas_call(
        body, name="ctx_states", grid=(1,),
        in_specs=[_smem_spec(), pl.BlockSpec((Lc, D // 2), lambda i: (0, 1)), pl.BlockSpec((Lc, D), lambda i: (0, 5))],
        out_specs=pl.BlockSpec((2, H // 2, 128, 128), lambda i: (0, 0, 0, 0)),
        out_shape=jax.ShapeDtypeStruct((2, H // 2, 128, 128), F32),
    )(lg, pqk_c, pc)


def _ret_states(kr, p, s0, lg, D):
    L = kr.shape[0]
    H = D // DV
    N = L // CHUNK
    HP = H // 2

    def body(lg_ref, kf_ref, kb_ref, vf_ref, vb_ref, s0_ref, sf_out, sb_out, sf, sb):
        n = pl.program_id(0)

        @pl.when(n == 0)
        def _():
            sf[...] = s0_ref[0]
            sb[...] = s0_ref[1]

        sf_out[0] = sf[...]
        sb_out[0] = sb[...]
        j = lax.broadcasted_iota(jnp.int32, (CHUNK, 128), 0).astype(F32)
        full = jnp.full((128, 128), float(CHUNK), F32)
        for pr in range(HP):
            kf2 = kf_ref[:, pr * 128:(pr + 1) * 128].astype(F32)
            kb2 = kb_ref[:, pr * 128:(pr + 1) * 128].astype(F32)
            inc_f, inc_b, gf, gb = [], [], [], []
            for e in range(2):
                h = 2 * pr + e
                lgf, lgb = lg_ref[0, h], lg_ref[1, h]
                inc_f.append(_dot_tn((kf2 * jnp.exp(lgf * (CHUNK - 1.0 - j))).astype(BF16), vf_ref[:, h * DV:(h + 1) * DV]))
                inc_b.append(_dot_tn((kb2 * jnp.exp(lgb * j)).astype(BF16), vb_ref[:, h * DV:(h + 1) * DV]))
                gf.append(jnp.exp(lgf * full))
                gb.append(jnp.exp(lgb * full))
            sf[pr] = _pair_select(gf[0], gf[1]) * sf[pr] + _pair_select(inc_f[0], inc_f[1])
            sb[pr] = _pair_select(gb[0], gb[1]) * sb[pr] + _pair_select(inc_b[0], inc_b[1])

    st = jax.ShapeDtypeStruct((N, HP, 128, 128), F32)
    return pl.pallas_call(
        body, name="ret_states", grid=(N,),
        in_specs=[_smem_spec(),
                  pl.BlockSpec((CHUNK, D // 2), lambda n: (n, 0)),
                  pl.BlockSpec((CHUNK, D // 2), lambda n: (N - 1 - n, 0)),
                  pl.BlockSpec((CHUNK, D), lambda n: (n, 5)),
                  pl.BlockSpec((CHUNK, D), lambda n: (N - 1 - n, 5)),
                  pl.BlockSpec((2, HP, 128, 128), lambda n: (0, 0, 0, 0))],
        out_specs=[pl.BlockSpec((1, HP, 128, 128), lambda n: (n, 0, 0, 0)),
                   pl.BlockSpec((1, HP, 128, 128), lambda n: (N - 1 - n, 0, 0, 0))],
        out_shape=[st, st],
        scratch_shapes=[pltpu.VMEM((HP, 128, 128), F32), pltpu.VMEM((HP, 128, 128), F32)],
        compiler_params=_cparams(("arbitrary",)),
    )(lg, kr, kr, p, p, s0)


def _decay_masks(lgf, lgb):
    i = lax.broadcasted_iota(jnp.int32, (CHUNK, CHUNK), 0).astype(F32)
    j = lax.broadcasted_iota(jnp.int32, (CHUNK, CHUNK), 1).astype(F32)
    d = i - j
    mf = jnp.where(d > 0, jnp.exp(lgf * jnp.maximum(d, 0.0)), 0.0)
    mb = jnp.where(d < 0, jnp.exp(lgb * jnp.maximum(-d, 0.0)), 0.0)
    m = mf + mb + jnp.where(d == 0, 2.0, 0.0)
    return m, mf * d, mb * (-d)


def _ret_out(qr, kr, p, sf_prev, sb_prev, gn_w, lg, D):
    L = qr.shape[0]
    H = D // DV
    N = L // CHUNK
    HP = H // 2

    def body(lg_ref, q_ref, k_ref, v_ref, zb_ref, sf_ref, sb_ref, gn_ref, o_ref, yb_ref):
        i = lax.broadcasted_iota(jnp.int32, (CHUNK, 128), 0).astype(F32)
        for pr in range(HP):
            q2 = q_ref[:, pr * 128:(pr + 1) * 128]
            k2 = k_ref[:, pr * 128:(pr + 1) * 128]
            sfp = sf_ref[0, pr].astype(BF16)
            sbp = sb_ref[0, pr].astype(BF16)
            for e in range(2):
                h = 2 * pr + e
                sl = slice(h * DV, (h + 1) * DV)
                lgf, lgb = lg_ref[0, h], lg_ref[1, h]
                qm = jnp.where(_head_lane_mask(q2.shape, e), q2, jnp.zeros_like(q2))
                m, _, _ = _decay_masks(lgf, lgb)
                a = (_dot_nt(qm, k2) * m).astype(BF16)
                qf = qm.astype(F32)
                o = _dot(a, v_ref[:, sl])
                o += _dot((qf * jnp.exp(lgf * (i + 1.0))).astype(BF16), sfp)
                o += _dot((qf * jnp.exp(lgb * (CHUNK - i))).astype(BF16), sbp)
                o_ref[:, sl] = o
                mu = jnp.mean(o, axis=-1, keepdims=True)
                oc = o - mu
                rstd = lax.rsqrt(jnp.mean(oc * oc, axis=-1, keepdims=True) + EPS)
                zb = zb_ref[:, sl].astype(F32)
                yb_ref[:, sl] = (zb * _sigmoid(zb) * (oc * rstd * gn_ref[:, sl])).astype(BF16)

    return pl.pallas_call(
        body, name="ret_out", grid=(N,),
        in_specs=[_smem_spec(),
                  pl.BlockSpec((CHUNK, D // 2), lambda n: (n, 0)),
                  pl.BlockSpec((CHUNK, D // 2), lambda n: (n, 0)),
                  pl.BlockSpec((CHUNK, D), lambda n: (n, 5)),
                  pl.BlockSpec((CHUNK, D), lambda n: (n, 6)),
                  pl.BlockSpec((1, HP, 128, 128), lambda n: (n, 0, 0, 0)),
                  pl.BlockSpec((1, HP, 128, 128), lambda n: (n, 0, 0, 0)),
                  _vec_spec(D)],
        out_specs=[pl.BlockSpec((CHUNK, D), lambda n: (n, 0)), pl.BlockSpec((CHUNK, D), lambda n: (n, 0))],
        out_shape=[jax.ShapeDtypeStruct((L, D), F32), jax.ShapeDtypeStruct((L, D), BF16)],
        compiler_params=_cparams(("parallel",)),
    )(lg, qr, kr, p, p, sf_prev, sb_prev, gn_w)


def _mid(ya, yb, p, x, tgt, w3, g, fw, D):
    L = x.shape[0]
    tm = min(256, L)
    nt = L // tm

    def body(ya_ref, yb_ref, ga_ref, gb_ref, x_ref, t_ref, w_hbm, g_ref, fw_ref,
             dx1_ref, dya_ref, dyb_ref, dgab_ref, dw_hbm, st_ref, w_vm, dw_acc, sem):
        i = pl.program_id(0)

        @pl.when(i == 0)
        def _():
            cp = pltpu.make_async_copy(w_hbm, w_vm, sem)
            cp.start()
            dw_acc[...] = jnp.zeros_like(dw_acc)
            st_ref[...] = jnp.zeros_like(st_ref)
            cp.wait()

        ya_b, yb_b = ya_ref[...], yb_ref[...]
        y_a = _dot(ya_b, w_vm[0])
        y_b = _dot(yb_b, w_vm[1])
        sga = _sigmoid(ga_ref[...].astype(F32))
        sgb = _sigmoid(gb_ref[...].astype(F32))
        mix_b = (sga * y_a + sgb * y_b).astype(BF16)
        y_x = _dot(mix_b, w_vm[2])
        gvec, fwv = g_ref[...], fw_ref[...]
        x1 = x_ref[...] + gvec * y_x
        r1 = lax.rsqrt(jnp.mean(x1 * x1, axis=-1, keepdims=True) + EPS)
        xh = x1 * r1
        diff = xh * fwv - t_ref[...]
        dout = diff * (1.0 / D)
        dxh = dout * fwv
        dx1 = r1 * (dxh - xh * jnp.mean(dxh * xh, axis=-1, keepdims=True))
        dx1_ref[...] = dx1
        st_ref[0:1, :] += jnp.sum(dout * xh, axis=0, keepdims=True)
        st_ref[1:2, :] += jnp.sum(dx1 * y_x, axis=0, keepdims=True)
        st_ref[2:3, :] += jnp.sum(diff * diff, axis=0, keepdims=True)
        dyx_b = (dx1 * gvec).astype(BF16)
        dmix = _dot_nt(dyx_b, w_vm[2])
        dw_acc[2] += _dot_tn(mix_b, dyx_b)
        dya_b = (dmix * sga).astype(BF16)
        dyb_b = (dmix * sgb).astype(BF16)
        dgab_ref[:, 0:D] = (dmix * y_a * sga * (1.0 - sga)).astype(BF16)
        dgab_ref[:, D:2 * D] = (dmix * y_b * sgb * (1.0 - sgb)).astype(BF16)
        dya_ref[...] = _dot_nt(dya_b, w_vm[0])
        dyb_ref[...] = _dot_nt(dyb_b, w_vm[1])
        dw_acc[0] += _dot_tn(ya_b, dya_b)
        dw_acc[1] += _dot_tn(yb_b, dyb_b)

        @pl.when(i == nt - 1)
        def _():
            out = pltpu.make_async_copy(dw_acc, dw_hbm, sem)
            out.start()
            out.wait()

    row = lambda col: pl.BlockSpec((tm, D), lambda i: (i, col))
    any_spec = pl.BlockSpec(memory_space=pl.ANY)
    f32o = jax.ShapeDtypeStruct((L, D), F32)
    return pl.pallas_call(
        body, name="mid", grid=(nt,),
        in_specs=[row(0), row(0), row(7), row(8), row(0), row(0), any_spec, _vec_spec(D), _vec_spec(D)],
        out_specs=[row(0), row(0), row(0), pl.BlockSpec((tm, 2 * D), lambda i: (i, 0)), any_spec,
                   pl.BlockSpec((8, D), lambda i: (0, 0))],
        out_shape=[f32o, f32o, f32o, jax.ShapeDtypeStruct((L, 2 * D), BF16),
                   jax.ShapeDtypeStruct((3, D, D), F32), jax.ShapeDtypeStruct((8, D), F32)],
        scratch_shapes=[pltpu.VMEM((3, D, D), BF16), pltpu.VMEM((3, D, D), F32), pltpu.SemaphoreType.DMA],
        compiler_params=_cparams(("arbitrary",), VMEM_LIMIT),
    )(ya, yb, p, p, x, tgt, w3, g, fw)


def _conv_bwd(dya, p, conv_w, conv_b, D):
    L = p.shape[0]
    tl = min(256, L)
    nt = L // tl

    def body(d_ref, h_ref, bg_ref, cg_ref, za_ref,
             dp_ref, dn_ref, hp_ref, hn_ref, bp_ref, bn_ref, cp_ref, cn_ref, zp_ref, zn_ref,
             w_ref, b_ref, dc_ref, st_ref):
        i = pl.program_id(0)

        @pl.when(i == 0)
        def _():
            st_ref[...] = jnp.zeros_like(st_ref)

        first, last = i == 0, i == nt - 1
        h = h_ref[...].astype(F32)
        cg = cg_ref[...].astype(F32)
        bg = bg_ref[...].astype(F32)
        za = za_ref[...].astype(F32)
        dy = d_ref[...].astype(F32)
        u = cg * h
        u_above = jnp.where(first, 0.0, cp_ref[15:16, :].astype(F32) * hp_ref[15:16, :].astype(F32))
        u_below = jnp.where(last, 0.0, cn_ref[0:1, :].astype(F32) * hn_ref[0:1, :].astype(F32))
        u_dn, u_up = _shift_rows(u, u_above, u_below)
        w0, w1, w2 = w_ref[0:1, :], w_ref[1:2, :], w_ref[2:3, :]
        co = w0 * u_dn + w1 * u + w2 * u_up + b_ref[...]
        sz = _sigmoid(za)
        silu = za * sz
        dc_ref[:, 3 * D:4 * D] = (dy * bg * co * (sz * (1.0 + za * (1.0 - sz)))).astype(BF16)
        dc_ref[:, D:2 * D] = (dy * silu * co).astype(BF16)
        dco = dy * silu * bg

        def edge(dr, zr, br, r):
            z = zr[r:r + 1, :].astype(F32)
            return dr[r:r + 1, :].astype(F32) * (z * _sigmoid(z)) * br[r:r + 1, :].astype(F32)

        dco_above = jnp.where(first, 0.0, edge(dp_ref, zp_ref, bp_ref, 15))
        dco_below = jnp.where(last, 0.0, edge(dn_ref, zn_ref, bn_ref, 0))
        dco_dn, dco_up = _shift_rows(dco, dco_above, dco_below)
        du = w0 * dco_up + w1 * dco + w2 * dco_dn
        dc_ref[:, 2 * D:3 * D] = (du * h).astype(BF16)
        dc_ref[:, 0:D] = (du * cg).astype(BF16)
        st_ref[0:1, :] += jnp.sum(dco * u_dn, axis=0, keepdims=True)
        st_ref[1:2, :] += jnp.sum(dco * u, axis=0, keepdims=True)
        st_ref[2:3, :] += jnp.sum(dco * u_up, axis=0, keepdims=True)
        st_ref[3:4, :] += jnp.sum(dco, axis=0, keepdims=True)

    main = lambda col: pl.BlockSpec((tl, D), lambda i: (i, col))
    halos = []
    for col in (0, 0, 1, 2, 3):
        halos.extend(_halo_specs(tl, L, D, col))
    return pl.pallas_call(
        body, name="conv_bwd", grid=(nt,),
        in_specs=[main(0), main(0), main(1), main(2), main(3)] + halos
                 + [pl.BlockSpec((8, D), lambda i: (0, 0)), _vec_spec(D)],
        out_specs=[pl.BlockSpec((tl, 4 * D), lambda i: (i, 0)), pl.BlockSpec((8, D), lambda i: (0, 0))],
        out_shape=[jax.ShapeDtypeStruct((L, 4 * D), BF16), jax.ShapeDtypeStruct((8, D), F32)],
        compiler_params=_cparams(("arbitrary",)),
    )(dya, p, p, p, p, dya, dya, p, p, p, p, p, p, p, p, conv_w, conv_b)


def _ret_bwd_pre(dyb, p, o, gn_w, D):
    L = o.shape[0]
    H = D // DV
    tl = min(256, L)

    def body(d_ref, zb_ref, o_ref, gn_ref, do_ref, dzb_ref, st_ref):
        @pl.when(pl.program_id(0) == 0)
        def _():
            st_ref[...] = jnp.zeros_like(st_ref)

        for h in range(H):
            sl = slice(h * DV, (h + 1) * DV)
            ov = o_ref[:, sl]
            mu = jnp.mean(ov, axis=-1, keepdims=True)
            oc = ov - mu
            rstd = lax.rsqrt(jnp.mean(oc * oc, axis=-1, keepdims=True) + EPS)
            rn = oc * rstd
            gw = gn_ref[:, sl]
            zb = zb_ref[:, sl].astype(F32)
            sz = _sigmoid(zb)
            dy = d_ref[:, sl].astype(F32)
            dzb_ref[:, sl] = (dy * (rn * gw) * (sz * (1.0 + zb * (1.0 - sz)))).astype(BF16)
            dretn = dy * (zb * sz)
            st_ref[0:1, sl] += jnp.sum(dretn * rn, axis=0, keepdims=True)
            drn = dretn * gw
            do = rstd * (drn - jnp.mean(drn, axis=-1, keepdims=True)
                         - rn * jnp.mean(drn * rn, axis=-1, keepdims=True))
            do_ref[:, sl] = do.astype(BF16)

    main = lambda col: pl.BlockSpec((tl, D), lambda i: (i, col))
    bfo = jax.ShapeDtypeStruct((L, D), BF16)
    return pl.pallas_call(
        body, name="ret_bwd_pre", grid=(L // tl,),
        in_specs=[main(0), main(6), main(0), _vec_spec(D)],
        out_specs=[main(0), main(0), pl.BlockSpec((8, D), lambda i: (0, 0))],
        out_shape=[bfo, bfo, jax.ShapeDtypeStruct((8, D), F32)],
        compiler_params=_cparams(("arbitrary",)),
    )(dyb, p, o, gn_w)


def _ret_bwd_states(qr, do, lg, D):
    L = qr.shape[0]
    H = D // DV
    N = L // CHUNK
    HP = H // 2

    def body(lg_ref, qf_ref, qb_ref, dof_ref, dob_ref, dsf_out, dsb_out, ds0_out, dsf, dsb):
        n = pl.program_id(0)

        @pl.when(n == 0)
        def _():
            dsf[...] = jnp.zeros_like(dsf)
            dsb[...] = jnp.zeros_like(dsb)

        dsf_out[0] = dsf[...]
        dsb_out[0] = dsb[...]
        i = lax.broadcasted_iota(jnp.int32, (CHUNK, 128), 0).astype(F32)
        full = jnp.full((128, 128), float(CHUNK), F32)
        for pr in range(HP):
            qf2 = qf_ref[:, pr * 128:(pr + 1) * 128].astype(F32)
            qb2 = qb_ref[:, pr * 128:(pr + 1) * 128].astype(F32)
            inc_f, inc_b, gf, gb = [], [], [], []
            for e in range(2):
                h = 2 * pr + e
                lgf, lgb = lg_ref[0, h], lg_ref[1, h]
                inc_f.append(_dot_tn((qf2 * jnp.exp(lgf * (i + 1.0))).astype(BF16), dof_ref[:, h * DV:(h + 1) * DV]))
                inc_b.append(_dot_tn((qb2 * jnp.exp(lgb * (CHUNK - i))).astype(BF16), dob_ref[:, h * DV:(h + 1) * DV]))
                gf.append(jnp.exp(lgf * full))
                gb.append(jnp.exp(lgb * full))
            dsf[pr] = _pair_select(gf[0], gf[1]) * dsf[pr] + _pair_select(inc_f[0], inc_f[1])
            dsb[pr] = _pair_select(gb[0], gb[1]) * dsb[pr] + _pair_select(inc_b[0], inc_b[1])

        @pl.when(n == N - 1)
        def _():
            ds0_out[0] = dsf[...]
            ds0_out[1] = dsb[...]

    st = jax.ShapeDtypeStruct((N, HP, 128, 128), F32)
    return pl.pallas_call(
        body, name="ret_bwd_states", grid=(N,),
        in_specs=[_smem_spec(),
                  pl.BlockSpec((CHUNK, D // 2), lambda n: (N - 1 - n, 0)),
                  pl.BlockSpec((CHUNK, D // 2), lambda n: (n, 0)),
                  pl.BlockSpec((CHUNK, D), lambda n: (N - 1 - n, 0)),
                  pl.BlockSpec((CHUNK, D), lambda n: (n, 0))],
        out_specs=[pl.BlockSpec((1, HP, 128, 128), lambda n: (N - 1 - n, 0, 0, 0)),
                   pl.BlockSpec((1, HP, 128, 128), lambda n: (n, 0, 0, 0)),
                   pl.BlockSpec((2, HP, 128, 128), lambda n: (0, 0, 0, 0))],
        out_shape=[st, st, jax.ShapeDtypeStruct((2, HP, 128, 128), F32)],
        scratch_shapes=[pltpu.VMEM((HP, 128, 128), F32), pltpu.VMEM((HP, 128, 128), F32)],
        compiler_params=_cparams(("arbitrary",)),
    )(lg, qr, qr, do, do)


def _ret_bwd_main(qr, kr, p, do, sf_prev, sb_prev, dsf, dsb, c2, s2, lg, D):
    L = qr.shape[0]
    H = D // DV
    N = L // CHUNK
    HP = H // 2
    W = D // 2

    def body(lg_ref, q_ref, k_ref, v_ref, do_ref, sf_ref, sb_ref, dsf_ref, dsb_ref, c_ref, s_ref,
             dr_ref, st_ref):
        @pl.when(pl.program_id(0) == 0)
        def _():
            st_ref[...] = jnp.zeros_like(st_ref)

        dqk_ref = dr_ref.at[:, 0:D]
        dv_ref = dr_ref.at[:, D:2 * D]
        i = lax.broadcasted_iota(jnp.int32, (CHUNK, 128), 0).astype(F32)
        lane = lax.broadcasted_iota(jnp.int32, (1, 128), 1)
        rowid = lax.broadcasted_iota(jnp.int32, (128, 128), 0)
        full = jnp.full((1, 1), float(CHUNK), F32)
        acc_f = jnp.zeros((1, 128), F32)
        acc_b = jnp.zeros((1, 128), F32)
        for pr in range(HP):
            ps = slice(pr * 128, (pr + 1) * 128)
            q2, k2 = q_ref[:, ps], k_ref[:, ps]
            sf32, sb32 = sf_ref[0, pr], sb_ref[0, pr]
            dsf32, dsb32 = dsf_ref[0, pr], dsb_ref[0, pr]
            sfp, sbp = sf32.astype(BF16), sb32.astype(BF16)
            dsfp, dsbp = dsf32.astype(BF16), dsb32.astype(BF16)
            dq2 = jnp.zeros((CHUNK, 128), F32)
            dk2 = jnp.zeros((CHUNK, 128), F32)
            for e in range(2):
                h = 2 * pr + e
                sl = slice(h * DV, (h + 1) * DV)
                lgf, lgb = lg_ref[0, h], lg_ref[1, h]
                hm = _head_lane_mask(q2.shape, e)
                qm = jnp.where(hm, q2, jnp.zeros_like(q2))
                km = jnp.where(hm, k2, jnp.zeros_like(k2))
                qf, kf = qm.astype(F32), km.astype(F32)
                v, do = v_ref[:, sl], do_ref[:, sl]
                vf, dof = v.astype(F32), do.astype(F32)
                m, mf1, mb1 = _decay_masks(lgf, lgb)
                m_t, _, _ = _decay_masks(lgb, lgf)
                sc = _dot_nt(qm, k2)
                dpm = _dot_nt(do, v)
                dsc = (dpm * m).astype(BF16)
                a_t = (_dot_nt(km, q2) * m_t).astype(BF16)
                dsc_t = (_dot_nt(v, do) * m_t).astype(BF16)
                dq_f, dq_b = jnp.exp(lgf * (i + 1.0)), jnp.exp(lgb * (CHUNK - i))
                dk_f, dk_b = jnp.exp(lgf * (CHUNK - 1.0 - i)), jnp.exp(lgb * i)
                dq = _dot(dsc, km)
                dq += jnp.where(hm, dq_f * _dot_nt(do, sfp) + dq_b * _dot_nt(do, sbp), 0.0)
                dk = _dot(dsc_t, qm)
                dk += jnp.where(hm, dk_f * _dot_nt(v, dsfp) + dk_b * _dot_nt(v, dsbp), 0.0)
                kdf = _dot((kf * dk_f).astype(BF16), dsfp)
                kdb = _dot((kf * dk_b).astype(BF16), dsbp)
                dv_ref[:, sl] = (_dot(a_t, do) + kdf + kdb).astype(BF16)
                dq2 += dq
                dk2 += dk
                xf = _dot((qf * dq_f).astype(BF16), sfp)
                xb = _dot((qf * dq_b).astype(BF16), sbp)
                pair = (rowid < DK) if e == 0 else (rowid >= DK)
                gcf, gcb = jnp.exp(lgf * full), jnp.exp(lgb * full)
                tf = _sum_all(sc * dpm * mf1) + _sum_all(xf * dof * (i + 1.0)) \
                    + _sum_all(kdf * vf * (CHUNK - 1.0 - i)) \
                    + CHUNK * gcf * _sum_all(jnp.where(pair, dsf32 * sf32, 0.0))
                tb = _sum_all(sc * dpm * mb1) + _sum_all(xb * dof * (CHUNK - i)) \
                    + _sum_all(kdb * vf * i) \
                    + CHUNK * gcb * _sum_all(jnp.where(pair, dsb32 * sb32, 0.0))
                acc_f += jnp.where(lane == h, tf, 0.0)
                acc_b += jnp.where(lane == h, tb, 0.0)
            c, s = c_ref[:, ps], s_ref[:, ps]
            dqk_ref[:, ps] = (dq2 * c - _swap_halves(dq2) * s).astype(BF16)
            dqk_ref[:, W + pr * 128:W + (pr + 1) * 128] = ((dk2 * c - _swap_halves(dk2) * s) * K_SCALE).astype(BF16)
        st_ref[0:1, :] += acc_f
        st_ref[1:2, :] += acc_b

    st_spec = pl.BlockSpec((1, HP, 128, 128), lambda n: (n, 0, 0, 0))
    half = pl.BlockSpec((CHUNK, W), lambda n: (n, 0))
    return pl.pallas_call(
        body, name="ret_bwd_main", grid=(N,),
        in_specs=[_smem_spec(), half, half,
                  pl.BlockSpec((CHUNK, D), lambda n: (n, 5)),
                  pl.BlockSpec((CHUNK, D), lambda n: (n, 0)),
                  st_spec, st_spec, st_spec, st_spec, half, half],
        out_specs=[pl.BlockSpec((CHUNK, 2 * D), lambda n: (n, 0)),
                   pl.BlockSpec((8, 128), lambda n: (0, 0))],
        out_shape=[jax.ShapeDtypeStruct((L, 2 * D), BF16), jax.ShapeDtypeStruct((8, 128), F32)],
        compiler_params=_cparams(("arbitrary",)),
    )(lg, qr, kr, p, do, sf_prev, sb_prev, dsf, dsb, c2, s2)


def _ctx_bwd(pc, pqk_c, ds0, lg, D):
    Lc = pc.shape[0]
    H = D // DV
    HP = H // 2
    W = D // 2

    def body(lg_ref, k_ref, v_ref, ds_ref, dr_ref, st_ref):
        dqk_ref = dr_ref.at[:, 0:D]
        dv_ref = dr_ref.at[:, D:2 * D]
        m = lax.broadcasted_iota(jnp.int32, (Lc, 128), 0).astype(F32)
        lane = lax.broadcasted_iota(jnp.int32, (1, 128), 1)
        acc_f = jnp.zeros((1, 128), F32)
        acc_b = jnp.zeros((1, 128), F32)
        dqk_ref[:, 0:W] = jnp.zeros((Lc, W), BF16)
        for pr in range(HP):
            ps = slice(pr * 128, (pr + 1) * 128)
            k2 = k_ref[:, ps].astype(F32) * K_SCALE
            dsfp, dsbp = ds_ref[0, pr].astype(BF16), ds_ref[1, pr].astype(BF16)
            dk2 = jnp.zeros((Lc, 128), F32)
            for e in range(2):
                h = 2 * pr + e
                sl = slice(h * DV, (h + 1) * DV)
                hm = _head_lane_mask(k2.shape, e)
                km = jnp.where(hm, k2, 0.0)
                v = v_ref[:, sl]
                vf = v.astype(F32)
                dec_f = jnp.exp(lg_ref[0, h] * (Lc - 1.0 - m))
                dec_b = jnp.exp(lg_ref[1, h] * m)
                kdf = _dot((km * dec_f).astype(BF16), dsfp)
                kdb = _dot((km * dec_b).astype(BF16), dsbp)
                dv_ref[:, sl] = (kdf + kdb).astype(BF16)
                dk2 += jnp.where(hm, dec_f * _dot_nt(v, dsfp) + dec_b * _dot_nt(v, dsbp), 0.0)
                acc_f += jnp.where(lane == h, _sum_all(kdf * vf * (Lc - 1.0 - m)), 0.0)
                acc_b += jnp.where(lane == h, _sum_all(kdb * vf * m), 0.0)
            dqk_ref[:, W + pr * 128:W + (pr + 1) * 128] = (dk2 * K_SCALE).astype(BF16)
        st_ref[...] = jnp.zeros_like(st_ref)
        st_ref[0:1, :] = acc_f
        st_ref[1:2, :] = acc_b

    return pl.pallas_call(
        body, name="ctx_bwd", grid=(1,),
        in_specs=[_smem_spec(), pl.BlockSpec((Lc, W), lambda i: (0, 1)), pl.BlockSpec((Lc, D), lambda i: (0, 5)),
                  pl.BlockSpec((2, HP, 128, 128), lambda i: (0, 0, 0, 0))],
        out_specs=[pl.BlockSpec((Lc, 2 * D), lambda i: (0, 0)), pl.BlockSpec((8, 128), lambda i: (0, 0))],
        out_shape=[jax.ShapeDtypeStruct((Lc, 2 * D), BF16), jax.ShapeDtypeStruct((8, 128), F32)],
    )(lg, pqk_c, pc, ds0)


def _dxm(groups, col0, w, x, nw, sc, dx1, name):
    L, D = x.shape
    tm = min(256, L)
    ng = len(groups)
    widths = [g.shape[1] for g in groups]
    wtot = sum(widths)
    with_dx = dx1 is not None

    def body(*refs):
        group_refs = refs[:ng]
        w_hbm, x_ref, nw_ref, sc_ref = refs[ng:ng + 4]
        rest = refs[ng + 4:]
        if with_dx:
            dx1_ref, gx_ref, st_ref, w_vm, sem = rest
        else:
            st_ref, w_vm, sem = rest

        @pl.when(pl.program_id(0) == 0)
        def _():
            cp = pltpu.make_async_copy(w_hbm.at[:, col0 * D:col0 * D + wtot], w_vm, sem)
            cp.start()
            st_ref[...] = jnp.zeros_like(st_ref)
            cp.wait()

        dxm, off = None, 0
        for g_ref, wd in zip(group_refs, widths):
            part = _dot_nt(g_ref[...], w_vm[:, off:off + wd])
            dxm = part if dxm is None else dxm + part
            off += wd

        xv = x_ref[...]
        r = lax.rsqrt(jnp.mean(xv * xv, axis=-1, keepdims=True) + EPS)
        xh = xv * r
        nwv = nw_ref[...]
        dxn = dxm * (1.0 + sc_ref[...])
        st_ref[0:1, :] += jnp.sum(dxm, axis=0, keepdims=True)
        st_ref[1:2, :] += jnp.sum(dxm * (xh * nwv), axis=0, keepdims=True)
        st_ref[2:3, :] += jnp.sum(dxn * xh, axis=0, keepdims=True)
        if with_dx:
            dxh = dxn * nwv
            gx_ref[...] = dx1_ref[...] + r * (dxh - xh * jnp.mean(dxh * xh, axis=-1, keepdims=True))

    row = pl.BlockSpec((tm, D), lambda i: (i, 0))
    in_specs = [pl.BlockSpec((tm, wd), lambda i: (i, 0)) for wd in widths] + [ANY, row, _vec_spec(D), _vec_spec(D)]
    out_specs = [pl.BlockSpec((8, D), lambda i: (0, 0))]
    out_shape = [jax.ShapeDtypeStruct((8, D), F32)]
    args = list(groups) + [w, x, nw, sc]
    if with_dx:
        in_specs.append(row)
        out_specs.insert(0, row)
        out_shape.insert(0, jax.ShapeDtypeStruct((L, D), F32))
        args.append(dx1)
    res = pl.pallas_call(
        body, name=name, grid=(L // tm,),
        in_specs=in_specs, out_specs=out_specs, out_shape=out_shape,
        scratch_shapes=[pltpu.VMEM((D, wtot), BF16), pltpu.SemaphoreType.DMA],
        compiler_params=_cparams(("arbitrary",), VMEM_LIMIT),
    )(*args)
    return (res[0], res[1]) if with_dx else (None, res[0])


DW_TN = 256


def _dw_in(xmt, groups, cmt, dr_c, D):
    L = xmt.shape[1]
    Lc = cmt.shape[1]
    tn = min(DW_TN, D)
    nblk = [g.shape[1] // tn for g in groups]
    starts = [sum(nblk[:g]) for g in range(len(groups))]
    ng = len(groups)

    def body(*refs):
        xt_hbm = refs[0]
        group_refs = refs[1:1 + ng]
        ct_ref, drc_ref, o_ref, xt_vm, sem = refs[1 + ng:]
        j = pl.program_id(0)

        @pl.when(j == 0)
        def _():
            cp = pltpu.make_async_copy(xt_hbm, xt_vm, sem)
            cp.start()
            cp.wait()

        for g in range(ng):
            @pl.when((j >= starts[g]) & (j < starts[g] + nblk[g]))
            def _(g=g):
                acc = _dot(xt_vm[...], group_refs[g][...])
                if g == 1:
                    acc += _dot(ct_ref[...], drc_ref[...])
                o_ref[...] = acc

    def group_spec(g, rows):
        return pl.BlockSpec((rows, tn), lambda j: (0, jnp.clip(j - starts[g], 0, nblk[g] - 1)))

    return pl.pallas_call(
        body, name="dw_in", grid=(sum(nblk),),
        in_specs=[ANY] + [group_spec(g, L) for g in range(ng)]
                 + [pl.BlockSpec((D, Lc), lambda j: (0, 0)), group_spec(1, Lc)],
        out_specs=pl.BlockSpec((D, tn), lambda j: (0, j)),
        out_shape=jax.ShapeDtypeStruct((D, sum(nblk) * tn), F32),
        scratch_shapes=[pltpu.VMEM((D, L), BF16), pltpu.SemaphoreType.DMA],
        compiler_params=_cparams(("arbitrary",), VMEM_LIMIT),
    )(xmt, *groups, cmt, dr_c)


def _local_step(x, ctx, tgt, mod_x, mod_c, norm_w, conv_w8, conv_b, lg, gn_w, fw, w_in, w3):
    L, D = x.shape
    sh_x, sc_x, g_x = mod_x[0:1], mod_x[1:2], mod_x[2:3]
    sh_c, sc_c = mod_c[0:1], mod_c[1:2]
    c2, s2 = _rope_tables(L, D)

    xm, xmt = _norm_mod(x, norm_w, sc_x, sh_x, "norm_mod_x")
    cm, cmt = _norm_mod(ctx, norm_w, sc_c, sh_c, "norm_mod_ctx")
    p, pqk = _in_proj(xm, w_in, "in_proj_x")
    pc, pqk_c = _in_proj(cm, w_in, "in_proj_ctx")
    ya = _conv_gate_fwd(p, conv_w8, conv_b, D)
    qr, kr = _rope_fwd(pqk, c2, s2, D)
    s0 = _ctx_states(pc, pqk_c, lg, D)
    sf_prev, sb_prev = _ret_states(kr, p, s0, lg, D)
    o, yb = _ret_out(qr, kr, p, sf_prev, sb_prev, gn_w, lg, D)
    dx1, dya, dyb, dgab, dw3, st_mid = _mid(ya, yb, p, x, tgt, w3, g_x, fw, D)
    dconv, st_conv = _conv_bwd(dya, p, conv_w8, conv_b, D)
    do, dzb, st_gn = _ret_bwd_pre(dyb, p, o, gn_w, D)
    dsf, dsb, ds0 = _ret_bwd_states(qr, do, lg, D)
    dret, st_lg = _ret_bwd_main(qr, kr, p, do, sf_prev, sb_prev, dsf, dsb, c2, s2, lg, D)
    dret_c, st_lgc = _ctx_bwd(pc, pqk_c, ds0, lg, D)
    groups = (dconv, dret, dzb, dgab)
    grad_x, st_x = _dxm(groups, 0, w_in, x, norm_w, sc_x, dx1, "dxm_x")
    _, st_c = _dxm((dret_c,), 4, w_in, ctx, norm_w, sc_c, None, "dxm_ctx")
    dw_in = _dw_in(xmt, groups, cmt, dret_c, D)
    return grad_x, dw_in, dw3, (st_mid, st_conv, st_gn, st_lg, st_lgc, st_x, st_c)


CHIP_FLIPS = (4, 2, 6)
ANY = pl.BlockSpec(memory_space=pl.ANY)
VMEM_FULL = pl.BlockSpec(memory_space=pltpu.VMEM)


def _position():
    return lax.axis_index("x"), lax.axis_index("y"), lax.axis_index("c")


def _peer(pos, k):
    x, y, c = pos
    return (1 - x if k & 4 else x, 1 - y if k & 2 else y, 1 - c if k & 1 else c)


def _dev_id(pos):
    return 4 * pos[0] + 2 * pos[1] + pos[2]


def _shard_of(pos):
    return 2 * pos[0] + pos[1]


def _remote(src, dst, send_sems, recv_sems, idx, to):
    return pltpu.make_async_remote_copy(src_ref=src, dst_ref=dst, send_sem=send_sems.at[idx],
                                        recv_sem=recv_sems.at[idx], device_id=to, device_id_type=MESH)


def _dot_f32(a, b):
    return jnp.dot(a, b, precision=lax.Precision.HIGHEST, preferred_element_type=F32)


def _silu(x):
    return x * _sigmoid(x)


def _fwd_small(c8, cctx8, ada_w, ada_b, conv_w8):
    D = c8.shape[1]
    Wm = ada_w.shape[1]
    Dq = conv_w8.shape[1]

    def body(c_ref, cc_ref, aw_ref, ab_ref, cw_ref, act_ref, mod_ref, cwf_ref,
             cbuf, pmine, pbuf, wbuf, s_c, r_c, s_p, r_p, s_w, r_w):
        pos = _position()
        me, s = _dev_id(pos), _shard_of(pos)
        cbuf[me] = c_ref[...]
        wbuf[s] = cw_ref[...]
        sends = [_remote(c_ref, cbuf.at[me], s_c, r_c, k - 1, _peer(pos, k)) for k in range(1, 8)]
        sends += [_remote(cw_ref, wbuf.at[s], s_w, r_w, j, _peer(pos, k)) for j, k in enumerate(CHIP_FLIPS)]
        for cp in sends:
            cp.start()
        for k in range(1, 8):
            _remote(c_ref, cbuf.at[_dev_id(_peer(pos, k))], s_c, r_c, k - 1, _peer(pos, k)).wait_recv()
        for d in range(N_DEV):
            act_ref[d:d + 1, :] = _silu(cbuf[d, 0:1, :])
        act_ref[8:9, :] = _silu(cc_ref[0:1, :])
        act_ref[9:16, :] = jnp.zeros((7, D), F32)
        part = _dot_f32(act_ref[...], aw_ref[...])
        pmine[...] = part
        pbuf[s] = part
        psend = [_remote(pmine, pbuf.at[s], s_p, r_p, j, _peer(pos, k)) for j, k in enumerate(CHIP_FLIPS)]
        for cp in psend:
            cp.start()
        for j, k in enumerate(CHIP_FLIPS):
            t = _shard_of(_peer(pos, k))
            _remote(pmine, pbuf.at[t], s_p, r_p, j, _peer(pos, k)).wait_recv()
            _remote(cw_ref, wbuf.at[t], s_w, r_w, j, _peer(pos, k)).wait_recv()
        for t in range(N_SHARD):
            mod_ref[:, t * Wm:(t + 1) * Wm] = pbuf[t] + ab_ref[:, t * Wm:(t + 1) * Wm]
            cwf_ref[:, t * Dq:(t + 1) * Dq] = wbuf[t]
        for cp in sends + psend:
            cp.wait_send()

    return pl.pallas_call(
        body, name="fwd_small",
        in_specs=[VMEM_FULL] * 5, out_specs=[VMEM_FULL] * 3,
        out_shape=[jax.ShapeDtypeStruct((16, D), F32), jax.ShapeDtypeStruct((16, 3 * D), F32),
                   jax.ShapeDtypeStruct((8, D), F32)],
        scratch_shapes=[pltpu.VMEM((N_DEV, 8, D), F32), pltpu.VMEM((16, Wm), F32),
                        pltpu.VMEM((N_SHARD, 16, Wm), F32), pltpu.VMEM((N_SHARD, 8, Dq), F32),
                        pltpu.SemaphoreType.DMA((7,)), pltpu.SemaphoreType.DMA((7,)),
                        pltpu.SemaphoreType.DMA((3,)), pltpu.SemaphoreType.DMA((3,)),
                        pltpu.SemaphoreType.DMA((3,)), pltpu.SemaphoreType.DMA((3,))],
        compiler_params=_cparams(None, VMEM_LIMIT),
    )(c8, cctx8, ada_w, ada_b, conv_w8)


def _cols(t, w):
    return pl.ds(pl.multiple_of(t * w, 128), w)


def _ag_weights(w_in_s, w3_s):
    D, Wc = w_in_s.shape
    Dh = D // 2
    Do = w3_s.shape[2]

    def body(wi_ref, w3_ref, fi_ref, f3_ref, si, s3, send, recv, loc):
        pos = _position()
        c = pos[2]
        s = _shard_of(pos)
        sib = _peer(pos, 1)

        def cast_rows(r, carry):
            rows = pl.ds(pl.multiple_of(r * 64, 64), 64)
            si[rows, :] = wi_ref[rows, :].astype(BF16)
            return carry

        lax.fori_loop(0, D // 64, cast_rows, 0)
        for a in range(3):
            s3[a] = w3_ref[a].astype(BF16)

        def in_half(t, hf):
            return fi_ref.at[pl.ds(hf * Dh, Dh), _cols(t, Wc)]

        def w3_half(t, hf):
            return f3_ref.at[:, t, hf]

        own = [pltpu.make_async_copy(si, fi_ref.at[:, _cols(s, Wc)], loc.at[0]),
               pltpu.make_async_copy(s3, f3_ref.at[:, s], loc.at[1])]
        first = []
        for j, k in enumerate(CHIP_FLIPS):
            to = _peer(pos, k)
            first.append(_remote(si.at[pl.ds(c * Dh, Dh), :], in_half(s, c), send, recv, j, to))
            first.append(_remote(s3.at[:, c], w3_half(s, c), send, recv, 3 + j, to))
        for cp in own + first:
            cp.start()
        passed = []
        for j, k in enumerate(CHIP_FLIPS):
            t = _shard_of(_peer(pos, k))
            _remote(in_half(t, c), in_half(t, c), send, recv, j, sib).wait_recv()
            fwd_i = _remote(in_half(t, c), in_half(t, c), send, recv, 6 + j, sib)
            fwd_i.start()
            _remote(w3_half(t, c), w3_half(t, c), send, recv, 3 + j, sib).wait_recv()
            fwd_3 = _remote(w3_half(t, c), w3_half(t, c), send, recv, 9 + j, sib)
            fwd_3.start()
            passed += [fwd_i, fwd_3]
        for j, k in enumerate(CHIP_FLIPS):
            t = _shard_of(_peer(pos, k))
            _remote(in_half(t, 1 - c), in_half(t, 1 - c), send, recv, 6 + j, sib).wait_recv()
            _remote(w3_half(t, 1 - c), w3_half(t, 1 - c), send, recv, 9 + j, sib).wait_recv()
        for cp in first + passed:
            cp.wait_send()
        for cp in own:
            cp.wait()

    return pl.pallas_call(
        body, name="ag_weights",
        in_specs=[VMEM_FULL, VMEM_FULL], out_specs=[ANY, ANY],
        out_shape=[jax.ShapeDtypeStruct((D, N_SHARD * Wc), BF16), jax.ShapeDtypeStruct((3, N_SHARD, 2, Do, D), BF16)],
        scratch_shapes=[pltpu.VMEM((D, Wc), BF16), pltpu.VMEM((3, 2, Do, D), BF16),
                        pltpu.SemaphoreType.DMA((12,)), pltpu.SemaphoreType.DMA((12,)), pltpu.SemaphoreType.DMA((2,))],
        compiler_params=_cparams(None, VMEM_LIMIT),
    )(w_in_s, w3_s)


def _rs_pair(dw_in, dw3):
    _, Dh, Wf = dw_in.shape
    _, _, _, Do, D = dw3.shape

    def body(gi_ref, g3_ref, ri_ref, r3_ref, send, recv):
        pos = _position()
        c = pos[2]
        sib = _peer(pos, 1)
        cps = [_remote(gi_ref.at[1 - c], ri_ref, send, recv, 0, sib),
               _remote(g3_ref.at[:, :, 1 - c], r3_ref, send, recv, 1, sib)]
        for cp in cps:
            cp.start()
        for cp in cps:
            cp.wait()

    return pl.pallas_call(
        body, name="rs_pair", in_specs=[ANY, ANY], out_specs=[ANY, ANY],
        out_shape=[jax.ShapeDtypeStruct((Dh, Wf), F32), jax.ShapeDtypeStruct((3, N_SHARD, Do, D), F32)],
        scratch_shapes=[pltpu.SemaphoreType.DMA((2,)), pltpu.SemaphoreType.DMA((2,))],
    )(dw_in, dw3)


def _sum_pair(cidx, dw_in, ri, dw3, r3):
    _, Dh, Wf = dw_in.shape
    Wc = Wf // N_SHARD
    _, _, _, Do, D = dw3.shape
    tr = min(256, Dh)

    def body_i(c_ref, a_ref, b_ref, o_ref):
        o_ref[...] = (a_ref[...] + b_ref[...]).astype(BF16)

    sum_i = pl.pallas_call(
        body_i, name="sum_pair_in",
        grid_spec=pltpu.PrefetchScalarGridSpec(
            num_scalar_prefetch=1, grid=(Dh // tr, N_SHARD),
            in_specs=[pl.BlockSpec((None, tr, Wc), lambda i, t, c: (c[0], i, t)),
                      pl.BlockSpec((tr, Wc), lambda i, t, c: (i, t))],
            out_specs=pl.BlockSpec((None, tr, Wc), lambda i, t, c: (t, i, 0))),
        out_shape=jax.ShapeDtypeStruct((N_SHARD, Dh, Wc), BF16),
        compiler_params=_cparams(("parallel", "parallel")),
    )(cidx, dw_in, ri)

    def body_3(c_ref, a_ref, b_ref, o_ref):
        o_ref[...] = (a_ref[...] + b_ref[...]).astype(BF16)

    sum_3 = pl.pallas_call(
        body_3, name="sum_pair_w3",
        grid_spec=pltpu.PrefetchScalarGridSpec(
            num_scalar_prefetch=1, grid=(3, N_SHARD),
            in_specs=[pl.BlockSpec((None, None, None, Do, D), lambda a, t, c: (a, t, c[0], 0, 0)),
                      pl.BlockSpec((None, None, Do, D), lambda a, t, c: (a, t, 0, 0))],
            out_specs=pl.BlockSpec((None, None, Do, D), lambda a, t, c: (a, t, 0, 0))),
        out_shape=jax.ShapeDtypeStruct((3, N_SHARD, Do, D), BF16),
        compiler_params=_cparams(("parallel", "parallel")),
    )(cidx, dw3, r3)
    return sum_i, sum_3


def _rs_chips(cs_in, cs_3):
    _, Dh, Wc = cs_in.shape
    _, _, Do, D = cs_3.shape

    def body(ci_ref, c3_ref, ri_ref, r3_ref, send, recv):
        pos = _position()
        cps = []
        for j, k in enumerate(CHIP_FLIPS):
            to = _peer(pos, k)
            t = _shard_of(to)
            cps.append(_remote(ci_ref.at[t], ri_ref.at[j], send, recv, j, to))
            cps.append(_remote(c3_ref.at[:, t], r3_ref.at[j], send, recv, 3 + j, to))
        for cp in cps:
            cp.start()
        for cp in cps:
            cp.wait()

    return pl.pallas_call(
        body, name="rs_chips", in_specs=[ANY, ANY], out_specs=[ANY, ANY],
        out_shape=[jax.ShapeDtypeStruct((3, Dh, Wc), BF16), jax.ShapeDtypeStruct((3, 3, Do, D), BF16)],
        scratch_shapes=[pltpu.SemaphoreType.DMA((6,)), pltpu.SemaphoreType.DMA((6,))],
    )(cs_in, cs_3)


def _sum_chips(csidx, cs_in, rb_in, cs_3, rb_3):
    _, Dh, Wc = cs_in.shape
    _, _, Do, D = cs_3.shape
    tr = min(256, Dh)

    def body_i(s_ref, a_ref, b_ref, o_ref):
        acc = a_ref[...].astype(F32)
        for j in range(3):
            acc = acc + b_ref[j].astype(F32)
        o_ref[...] = acc

    g_in = pl.pallas_call(
        body_i, name="sum_chips_in",
        grid_spec=pltpu.PrefetchScalarGridSpec(
            num_scalar_prefetch=1, grid=(Dh // tr,),
            in_specs=[pl.BlockSpec((None, tr, Wc), lambda i, s: (s[1], i, 0)),
                      pl.BlockSpec((3, tr, Wc), lambda i, s: (0, i, 0))],
            out_specs=pl.BlockSpec((None, tr, Wc), lambda i, s: (s[0], i, 0))),
        out_shape=jax.ShapeDtypeStruct((2, Dh, Wc), F32),
        compiler_params=_cparams(("parallel",)),
    )(csidx, cs_in, rb_in)

    def body_3(s_ref, a_ref, b_ref, o_ref):
        acc = a_ref[...].astype(F32)
        for j in range(3):
            acc = acc + b_ref[j].astype(F32)
        o_ref[...] = acc

    g_3 = pl.pallas_call(
        body_3, name="sum_chips_w3",
        grid_spec=pltpu.PrefetchScalarGridSpec(
            num_scalar_prefetch=1, grid=(3,),
            in_specs=[pl.BlockSpec((None, None, Do, D), lambda a, s: (a, s[1], 0, 0)),
                      pl.BlockSpec((3, None, Do, D), lambda a, s: (0, a, 0, 0))],
            out_specs=pl.BlockSpec((None, None, Do, D), lambda a, s: (a, s[0], 0, 0))),
        out_shape=jax.ShapeDtypeStruct((3, 2, Do, D), F32),
        compiler_params=_cparams(("parallel",)),
    )(csidx, cs_3, rb_3)
    return g_in, g_3


def _rs_final(g_in, g_3):
    def body(hi_ref, h3_ref, gi_ref, g3_ref, send, recv):
        pos = _position()
        c = pos[2]
        sib = _peer(pos, 1)
        cps = [_remote(hi_ref.at[c], gi_ref.at[c], send, recv, 0, sib),
               _remote(h3_ref.at[:, c], g3_ref.at[:, c], send, recv, 1, sib)]
        for cp in cps:
            cp.start()
        _remote(hi_ref.at[1 - c], gi_ref.at[1 - c], send, recv, 0, sib).wait_recv()
        _remote(h3_ref.at[:, 1 - c], g3_ref.at[:, 1 - c], send, recv, 1, sib).wait_recv()
        for cp in cps:
            cp.wait_send()

    return pl.pallas_call(
        body, name="rs_final", in_specs=[ANY, ANY], out_specs=[ANY, ANY],
        out_shape=[jax.ShapeDtypeStruct(g_in.shape, F32), jax.ShapeDtypeStruct(g_3.shape, F32)],
        input_output_aliases={0: 0, 1: 1},
        scratch_shapes=[pltpu.SemaphoreType.DMA((2,)), pltpu.SemaphoreType.DMA((2,))],
    )(g_in, g_3)


def _adam_math(w, g, m, v):
    m = ADAM_B1 * m + (1.0 - ADAM_B1) * g
    v = ADAM_B2 * v + (1.0 - ADAM_B2) * (g * g)
    m_hat = m / (1.0 - ADAM_B1 ** ADAM_STEP)
    v_hat = v / (1.0 - ADAM_B2 ** ADAM_STEP)
    delta = -ADAM_LR * (m_hat / (jnp.sqrt(v_hat) + ADAM_EPS) + ADAM_WD * w)
    return delta, m, v


def _adamw(w, g, m, v, name):
    R, C = w.shape
    tr = min(128, R)

    def body(w_ref, g_ref, m_ref, v_ref, d_ref, nm_ref, nv_ref):
        d_ref[...], nm_ref[...], nv_ref[...] = _adam_math(w_ref[...], g_ref[...], m_ref[...], v_ref[...])

    blk = pl.BlockSpec((tr, C), lambda i: (i, 0))
    return pl.pallas_call(
        body, name=name, grid=(R // tr,), in_specs=[blk] * 4, out_specs=[blk] * 3,
        out_shape=[jax.ShapeDtypeStruct((R, C), F32)] * 3,
        compiler_params=_cparams(("parallel",), VMEM_LIMIT),
    )(w, g, m, v)


def _bwd_small(vec, act, ada_w, cctx8, w_sm, m_sm, v_sm, w_ab, m_ab, v_ab, w_cw, m_cw, v_cw, w_dl, m_dl, v_dl):
    D = vec.shape[1]
    Wm = ada_w.shape[1]
    Dq = w_cw.shape[1]

    def body(vec_ref, act_ref, aw_ref, cc_ref, wsm, msm, vsm, wab, mab, vab, wcw, mcw, vcw, wdl, mdl, vdl,
             gaw_ref, loss_ref, o_sm, o_ab, o_cw, o_dl,
             vbuf, dm, dm_sh, gcw, amine, abuf, s_v, r_v, s_a, r_a):
        pos = _position()
        me, s = _dev_id(pos), _shard_of(pos)
        vbuf[me] = vec_ref[...]
        sends = [_remote(vec_ref, vbuf.at[me], s_v, r_v, k - 1, _peer(pos, k)) for k in range(1, 8)]
        for cp in sends:
            cp.start()
        for k in range(1, 8):
            _remote(vec_ref, vbuf.at[_dev_id(_peer(pos, k))], s_v, r_v, k - 1, _peer(pos, k)).wait_recv()
        tot = vbuf[0]
        for d in range(1, N_DEV):
            tot = tot + vbuf[d]
        loss_ref[...] = jnp.zeros((8, 128), F32) + (0.5 / D) * _sum_all(tot[14:15, :])
        dm[...] = jnp.zeros_like(dm)
        for d in range(N_DEV):
            for r in range(3):
                dm[d:d + 1, r * D:(r + 1) * D] = vbuf[d, r:r + 1, :]
        dm[8:9, 0:D] = tot[3:4, :]
        dm[8:9, D:2 * D] = tot[4:5, :]
        for t in range(N_SHARD):
            @pl.when(s == t)
            def _(t=t):
                dm_sh[...] = dm[:, t * Wm:(t + 1) * Wm]
                gcw[...] = jnp.zeros_like(gcw)
                gcw[0:3, :] = tot[9:12, t * Dq:(t + 1) * Dq]
        gaw_ref[...] = lax.dot_general(act_ref[...], dm_sh[...], (((0,), (0,)), ((), ())),
                                       precision=lax.Precision.HIGHEST, preferred_element_type=F32)
        part = lax.dot_general(dm_sh[8:16, :], aw_ref[...], (((1,), (1,)), ((), ())),
                               precision=lax.Precision.HIGHEST, preferred_element_type=F32)
        amine[...] = part
        abuf[s] = part
        asend = [_remote(amine, abuf.at[s], s_a, r_a, j, _peer(pos, k)) for j, k in enumerate(CHIP_FLIPS)]
        for cp in asend:
            cp.start()
        for j, k in enumerate(CHIP_FLIPS):
            _remote(amine, abuf.at[_shard_of(_peer(pos, k))], s_a, r_a, j, _peer(pos, k)).wait_recv()
        da = abuf[0]
        for t in range(1, N_SHARD):
            da = da + abuf[t]
        cc = cc_ref[0:1, :]
        sg = _sigmoid(cc)
        g_cctx = da[0:1, :] * (sg * (1.0 + cc * (1.0 - sg)))

        def emit(o_ref, w, g, m, v):
            o_ref[0] = g
            o_ref[1], o_ref[2], o_ref[3] = _adam_math(w, g, m, v)

        g_sm = jnp.concatenate([g_cctx, tot[5:9, :], jnp.zeros((3, D), F32)], axis=0)
        emit(o_sm, wsm[...], g_sm, msm[...], vsm[...])
        g_ab = jnp.concatenate([tot[0:1, :] + tot[3:4, :], tot[1:2, :] + tot[4:5, :], tot[2:3, :],
                                jnp.zeros((5, D), F32)], axis=0)
        emit(o_ab, wab[...], g_ab, mab[...], vab[...])
        emit(o_cw, wcw[...], gcw[...], mcw[...], vcw[...])
        g_dl = jnp.concatenate([tot[12:14, 0:128] * _sigmoid(-wdl[0:2, :]), jnp.zeros((6, 128), F32)], axis=0)
        emit(o_dl, wdl[...], g_dl, mdl[...], vdl[...])
        for cp in sends + asend:
            cp.wait_send()

    return pl.pallas_call(
        body, name="bwd_small",
        in_specs=[VMEM_FULL] * 16, out_specs=[VMEM_FULL] * 6,
        out_shape=[jax.ShapeDtypeStruct((D, Wm), F32), jax.ShapeDtypeStruct((8, 128), F32),
                   jax.ShapeDtypeStruct((4, 8, D), F32), jax.ShapeDtypeStruct((4, 8, D), F32),
                   jax.ShapeDtypeStruct((4, 8, Dq), F32), jax.ShapeDtypeStruct((4, 8, 128), F32)],
        scratch_shapes=[pltpu.VMEM((N_DEV, 16, D), F32), pltpu.VMEM((16, 3 * D), F32), pltpu.VMEM((16, Wm), F32),
                        pltpu.VMEM((8, Dq), F32), pltpu.VMEM((8, D), F32), pltpu.VMEM((N_SHARD, 8, D), F32),
                        pltpu.SemaphoreType.DMA((7,)), pltpu.SemaphoreType.DMA((7,)),
                        pltpu.SemaphoreType.DMA((3,)), pltpu.SemaphoreType.DMA((3,))],
        compiler_params=_cparams(None, VMEM_LIMIT),
    )(vec, act, ada_w, cctx8, w_sm, m_sm, v_sm, w_ab, m_ab, v_ab, w_cw, m_cw, v_cw, w_dl, m_dl, v_dl)


def _pad_rows(a, rows=8):
    return jnp.pad(a, ((0, rows - a.shape[0]), (0, 0)))


def kernel(x, c, ctx, c_ctx, norm_w, ada_w, ada_b, w_in, conv_w, conv_b, decay_logit, gn_w, w_a, w_b, w_out, final_norm_w, loss_target, m_c_ctx, m_norm_w, m_ada_w, m_ada_b, m_w_in, m_conv_w, m_conv_b, m_decay_logit, m_gn_w, m_w_a, m_w_b, m_w_out, m_final_norm_w, v_c_ctx, v_norm_w, v_ada_w, v_ada_b, v_w_in, v_conv_w, v_conv_b, v_decay_logit, v_gn_w, v_w_a, v_w_b, v_w_out, v_final_norm_w):
    L, D = x.shape[1], x.shape[2]
    H = D // DV
    Wc = w_in.shape[2]
    Do = D // 8
    pos = _position()
    me = _dev_id(pos)
    cidx = jnp.reshape(pos[2], (1,)).astype(jnp.int32)
    sidx = jnp.reshape(_shard_of(pos), (1,)).astype(jnp.int32)

    act, mod, conv_w8 = _fwd_small(_pad_rows(c), _pad_rows(c_ctx[None]), ada_w[0], ada_b, _pad_rows(conv_w[0]))
    mod_x = lax.dynamic_slice_in_dim(mod, me, 1, axis=0).reshape(3, D)
    mod_c = mod[8].reshape(3, D)
    lg = jax.nn.log_sigmoid(decay_logit[0])

    w3_s = jnp.stack([w_a[0], w_b[0], w_out[0]]).reshape(3, 2, Do, D)
    w_in_full, w3_full = _ag_weights(w_in[0], w3_s)
    w3_full = w3_full.reshape(3, D, D)

    grad_x, dw_in, dw3, sts = _local_step(x[0], ctx[0], loss_target[0], mod_x, mod_c, norm_w, conv_w8, conv_b, lg,
                                          gn_w, final_norm_w[None], w_in_full, w3_full)
    st_mid, st_conv, st_gn, st_lg, st_lgc, st_x, st_c = sts

    dw_in2 = dw_in.reshape(2, D // 2, N_SHARD * Wc)
    dw3_5 = dw3.reshape(3, N_SHARD, 2, Do, D)
    ra_in, ra_3 = _rs_pair(dw_in2, dw3_5)
    cs_in, cs_3 = _sum_pair(cidx, dw_in2, ra_in, dw3_5, ra_3)
    rb_in, rb_3 = _rs_chips(cs_in, cs_3)
    gh_in, gh_3 = _sum_chips(jnp.concatenate([cidx, sidx]), cs_in, rb_in, cs_3, rb_3)
    g_in, g_3 = _rs_final(gh_in, gh_3)
    g_w_in = g_in.reshape(D, Wc)
    g_3 = g_3.reshape(3, D // 4, D)

    lanes = lambda a: jnp.pad(a, ((0, 0), (0, D - a.shape[1])))
    vec = jnp.concatenate([
        st_x[0:2], st_mid[1:2], st_c[0:2], st_x[2:3] + st_c[2:3], st_conv[3:4], st_gn[0:1], st_mid[0:1],
        st_conv[0:3], lanes(st_lg[0:2] + st_lgc[0:2]), st_mid[2:3], jnp.zeros((1, D), F32)], axis=0)
    small = lambda a, b_, c_, d_, e_: _pad_rows(jnp.concatenate([a[None], b_, c_, d_, e_[None]], axis=0))
    dl = lambda a: jnp.pad(a[0], ((0, 6), (0, 128 - H)))
    g_ada_w, loss_t, o_sm, o_ab, o_cw, o_dl = _bwd_small(
        vec, act, ada_w[0], _pad_rows(c_ctx[None]),
        small(c_ctx, norm_w, conv_b, gn_w, final_norm_w), small(m_c_ctx, m_norm_w, m_conv_b, m_gn_w, m_final_norm_w),
        small(v_c_ctx, v_norm_w, v_conv_b, v_gn_w, v_final_norm_w),
        _pad_rows(ada_b.reshape(3, D)), _pad_rows(m_ada_b.reshape(3, D)), _pad_rows(v_ada_b.reshape(3, D)),
        _pad_rows(conv_w[0]), _pad_rows(m_conv_w[0]), _pad_rows(v_conv_w[0]),
        dl(decay_logit), dl(m_decay_logit), dl(v_decay_logit))

    upd_in = _adamw(w_in[0], g_w_in, m_w_in[0], v_w_in[0], "adamw_w_in")
    upd_ada = _adamw(ada_w[0], g_ada_w, m_ada_w[0], v_ada_w[0], "adamw_ada_w")
    upd_a = _adamw(w_a[0], g_3[0], m_w_a[0], v_w_a[0], "adamw_w_a")
    upd_b = _adamw(w_b[0], g_3[1], m_w_b[0], v_w_b[0], "adamw_w_b")
    upd_o = _adamw(w_out[0], g_3[2], m_w_out[0], v_w_out[0], "adamw_w_out")

    def leaves(q):
        big = lambda g, upd: (g if q == 0 else upd[q - 1])[None]
        sm = o_sm[q]
        return [sm[0], sm[1:2], big(g_ada_w, upd_ada), o_ab[q][0:3].reshape(1, 3 * D), big(g_w_in, upd_in),
                o_cw[q][0:3][None], sm[2:3], o_dl[q][0:2, 0:H][None], sm[3:4],
                big(g_3[0], upd_a), big(g_3[1], upd_b), big(g_3[2], upd_o), sm[4]]

    loss = loss_t[0, 0]
    return (loss, grad_x[None], *leaves(0), *leaves(1), *leaves(2), *leaves(3))
```

```python
from typing import Callable, NamedTuple

import jax
import jax.numpy as jnp
from jax import lax
from jax.experimental import pallas as pl
from jax.experimental.pallas import tpu as pltpu

F32 = jnp.float32
BF16 = jnp.bfloat16
MESH = pl.DeviceIdType.MESH

CHUNK = 128
DV = 128
DK = 64
GRID_W = 64
ROPE_BASE = 10000.0
EPS = 1e-6
K_SCALE = DK ** -0.5
N_SHARD = 4
N_DEV = 8

ADAM_LR = 0.001
ADAM_B1 = 0.9
ADAM_B2 = 0.999
ADAM_EPS = 1e-08
ADAM_WD = 0.01
ADAM_STEP = 10

VMEM_LIMIT = 56 * 1024 * 1024


def _cparams(sem=None, vmem=None):
    kw = {}
    if sem is not None:
        kw["dimension_semantics"] = sem
    if vmem is not None:
        kw["vmem_limit_bytes"] = vmem
    return pltpu.CompilerParams(**kw)


def _dot(a, b):
    return jnp.dot(a, b, preferred_element_type=F32)


def _dot_nt(a, b):
    return lax.dot_general(a, b, (((1,), (1,)), ((), ())), preferred_element_type=F32)


def _dot_tn(a, b):
    return lax.dot_general(a, b, (((0,), (0,)), ((), ())), preferred_element_type=F32)


def _sigmoid(x):
    return 1.0 / (1.0 + jnp.exp(-x))


def _sum_all(x):
    return jnp.sum(jnp.sum(x, axis=1, keepdims=True), axis=0, keepdims=True)


def _swap_halves(t):
    n = t.shape[1]
    lane = lax.broadcasted_iota(jnp.int32, t.shape, 1)
    low = (lane & 32) == 0
    return jnp.where(low, pltpu.roll(t, n - 32, 1), pltpu.roll(t, 32, 1))


def _vec_spec(d):
    return pl.BlockSpec((1, d), lambda *a: (0, 0))


def _norm_mod(x, nw, sc, sh, name):
    L, D = x.shape
    tl = min(256, L)

    def body(x_ref, nw_ref, sc_ref, sh_ref, xm_ref, xmt_ref):
        xv = x_ref[...]
        r = lax.rsqrt(jnp.mean(xv * xv, axis=-1, keepdims=True) + EPS)
        xm = (xv * r * nw_ref[...]) * (1.0 + sc_ref[...]) + sh_ref[...]
        xm_ref[...] = xm.astype(BF16)
        xmt_ref[...] = xm.T.astype(BF16)

    return pl.pallas_call(
        body, name=name, grid=(L // tl,),
        in_specs=[pl.BlockSpec((tl, D), lambda i: (i, 0)), _vec_spec(D), _vec_spec(D), _vec_spec(D)],
        out_specs=[pl.BlockSpec((tl, D), lambda i: (i, 0)), pl.BlockSpec((D, tl), lambda i: (0, i))],
        out_shape=[jax.ShapeDtypeStruct((L, D), BF16), jax.ShapeDtypeStruct((D, L), BF16)],
        compiler_params=_cparams(("parallel",)),
    )(x, nw, sc, sh)


def _in_proj(xm, w, name):
    M, D = xm.shape
    N = w.shape[1]
    tm = min(1024, M)

    def body(a_ref, b_ref, o_ref, qk_ref):
        acc = _dot(a_ref[...], b_ref[...])
        o_ref[...] = acc.astype(o_ref.dtype)

        @pl.when(pl.program_id(1) == 4)
        def _():
            qk_ref[...] = acc

    return pl.pallas_call(
        body, name=name, grid=(M // tm, N // D),
        in_specs=[pl.BlockSpec((tm, D), lambda i, j: (i, 0)), pl.BlockSpec((D, D), lambda i, j: (0, j))],
        out_specs=[pl.BlockSpec((tm, D), lambda i, j: (i, j)), pl.BlockSpec((tm, D), lambda i, j: (i, 0))],
        out_shape=[jax.ShapeDtypeStruct((M, N), BF16), jax.ShapeDtypeStruct((M, D), F32)],
        compiler_params=_cparams(("parallel", "arbitrary")),
    )(xm, w)


def _halo_specs(tl, L, D, col):
    hb = tl // 16
    last = L // 16 - 1
    prev = pl.BlockSpec((16, D), lambda i: (jnp.maximum(i * hb - 1, 0), col))
    nxt = pl.BlockSpec((16, D), lambda i: (jnp.minimum((i + 1) * hb, last), col))
    return prev, nxt


def _shift_rows(u, above, below):
    tl = u.shape[0]
    row = lax.broadcasted_iota(jnp.int32, u.shape, 0)
    dn = jnp.where(row == 0, above, pltpu.roll(u, 1, 0))
    up = jnp.where(row == tl - 1, below, pltpu.roll(u, tl - 1, 0))
    return dn, up


def _conv_gate_fwd(p, conv_w, conv_b, D):
    L = p.shape[0]
    tl = min(256, L)
    nt = L // tl

    def body(h_ref, bg_ref, cg_ref, za_ref, hp_ref, hn_ref, cp_ref, cn_ref, w_ref, b_ref, o_ref):
        i = pl.program_id(0)
        u = cg_ref[...].astype(F32) * h_ref[...].astype(F32)
        above = cp_ref[15:16, :].astype(F32) * hp_ref[15:16, :].astype(F32)
        below = cn_ref[0:1, :].astype(F32) * hn_ref[0:1, :].astype(F32)
        above = jnp.where(i == 0, 0.0, above)
        below = jnp.where(i == nt - 1, 0.0, below)
        dn, up = _shift_rows(u, above, below)
        co = w_ref[0:1, :] * dn + w_ref[1:2, :] * u + w_ref[2:3, :] * up + b_ref[...]
        za = za_ref[...].astype(F32)
        o_ref[...] = (za * _sigmoid(za) * bg_ref[...].astype(F32) * co).astype(BF16)

    main = lambda col: pl.BlockSpec((tl, D), lambda i: (i, col))
    hp, hn = _halo_specs(tl, L, D, 0)
    cp, cn = _halo_specs(tl, L, D, 2)
    return pl.pallas_call(
        body, name="conv_gate_fwd", grid=(nt,),
        in_specs=[main(0), main(1), main(2), main(3), hp, hn, cp, cn,
                  pl.BlockSpec((8, D), lambda i: (0, 0)), _vec_spec(D)],
        out_specs=pl.BlockSpec((tl, D), lambda i: (i, 0)),
        out_shape=jax.ShapeDtypeStruct((L, D), BF16),
        compiler_params=_cparams(("parallel",)),
    )(p, p, p, p, p, p, p, p, conv_w, conv_b)


def _rope_tables(L, D):
    pos = jnp.arange(L)
    row = (pos // GRID_W).astype(F32)
    col = (pos % GRID_W).astype(F32)
    nf = DK // 4
    inv = ROPE_BASE ** (-jnp.arange(nf, dtype=F32) / nf)
    ang = jnp.concatenate([row[:, None] * inv, col[:, None] * inv], axis=-1)
    cos, sin = jnp.cos(ang), jnp.sin(ang)
    heads = D // DV
    c2 = jnp.tile(jnp.concatenate([cos, cos], axis=-1), (1, heads))
    s2 = jnp.tile(jnp.concatenate([-sin, sin], axis=-1), (1, heads))
    return c2, s2


def _rope_fwd(pqk, c2, s2, D):
    L = pqk.shape[0]
    W = D // 2
    tl = min(256, L)

    def body(q_ref, k_ref, c_ref, s_ref, qo_ref, ko_ref):
        c, s = c_ref[...], s_ref[...]
        q = q_ref[...].astype(F32)
        k = k_ref[...].astype(F32) * K_SCALE
        qo_ref[...] = (q * c + _swap_halves(q) * s).astype(BF16)
        ko_ref[...] = (k * c + _swap_halves(k) * s).astype(BF16)

    blk = lambda col: pl.BlockSpec((tl, W), lambda i: (i, col))
    return pl.pallas_call(
        body, name="rope_fwd", grid=(L // tl,),
        in_specs=[blk(0), blk(1), blk(0), blk(0)],
        out_specs=[blk(0), blk(0)],
        out_shape=[jax.ShapeDtypeStruct((L, W), BF16)] * 2,
        compiler_params=_cparams(("parallel",)),
    )(pqk, pqk, c2, s2)


def _smem_spec():
    return pl.BlockSpec(memory_space=pltpu.SMEM)


def _pair_select(e0, e1):
    row = lax.broadcasted_iota(jnp.int32, e0.shape, 0)
    return jnp.where(row < DK, e0, e1)


def _head_lane_mask(shape, e):
    lane = lax.broadcasted_iota(jnp.int32, shape, 1)
    return (lane < DK) if e == 0 else (lane >= DK)


def _ctx_states(pc, pqk_c, lg, D):
    Lc = pc.shape[0]
    H = D // DV

    def body(lg_ref, k_ref, v_ref, s_ref):
        m = lax.broadcasted_iota(jnp.int32, (Lc, DV), 0).astype(F32)
        for pr in range(H // 2):
            k2 = k_ref[:, pr * 128:(pr + 1) * 128].astype(F32) * K_SCALE
            res = [[None, None], [None, None]]
            for e in range(2):
                h = 2 * pr + e
                v = v_ref[:, h * DV:(h + 1) * DV]
                dec_f = jnp.exp(lg_ref[0, h] * (Lc - 1.0 - m))
                dec_b = jnp.exp(lg_ref[1, h] * m)
                res[0][e] = _dot_tn((k2 * dec_f).astype(BF16), v)
                res[1][e] = _dot_tn((k2 * dec_b).astype(BF16), v)
            s_ref[0, pr] = _pair_select(res[0][0], res[0][1])
            s_ref[1, pr] = _pair_select(res[1][0], res[1][1])

    return pl.pallas_call(
        body, name="ctx_states", grid=(1,),
        in_specs=[_smem_spec(), pl.BlockSpec((Lc, D // 2), lambda i: (0, 1)), pl.BlockSpec((Lc, D), lambda i: (0, 5))],
        out_specs=pl.BlockSpec((2, H // 2, 128, 128), lambda i: (0, 0, 0, 0)),
        out_shape=jax.ShapeDtypeStruct((2, H // 2, 128, 128), F32),
    )(lg, pqk_c, pc)


def _ret_states(kr, p, s0, lg, D):
    L = kr.shape[0]
    H = D // DV
    N = L // CHUNK
    HP = H // 2

    def body(lg_ref, kf_ref, kb_ref, vf_ref, vb_ref, s0_ref, sf_out, sb_out, sf, sb):
        n = pl.program_id(0)

        @pl.when(n == 0)
        def _():
            sf[...] = s0_ref[0]
            sb[...] = s0_ref[1]

        sf_out[0] = sf[...]
        sb_out[0] = sb[...]
        j = lax.broadcasted_iota(jnp.int32, (CHUNK, 128), 0).astype(F32)
        full = jnp.full((128, 128), float(CHUNK), F32)
        for pr in range(HP):
            kf2 = kf_ref[:, pr * 128:(pr + 1) * 128].astype(F32)
            kb2 = kb_ref[:, pr * 128:(pr + 1) * 128].astype(F32)
            inc_f, inc_b, gf, gb = [], [], [], []
            for e in range(2):
                h = 2 * pr + e
                lgf, lgb = lg_ref[0, h], lg_ref[1, h]
                inc_f.append(_dot_tn((kf2 * jnp.exp(lgf * (CHUNK - 1.0 - j))).astype(BF16), vf_ref[:, h * DV:(h + 1) * DV]))
                inc_b.append(_dot_tn((kb2 * jnp.exp(lgb * j)).astype(BF16), vb_ref[:, h * DV:(h + 1) * DV]))
                gf.append(jnp.exp(lgf * full))
                gb.append(jnp.exp(lgb * full))
            sf[pr] = _pair_select(gf[0], gf[1]) * sf[pr] + _pair_select(inc_f[0], inc_f[1])
            sb[pr] = _pair_select(gb[0], gb[1]) * sb[pr] + _pair_select(inc_b[0], inc_b[1])

    st = jax.ShapeDtypeStruct((N, HP, 128, 128), F32)
    return pl.pallas_call(
        body, name="ret_states", grid=(N,),
        in_specs=[_smem_spec(),
                  pl.BlockSpec((CHUNK, D // 2), lambda n: (n, 0)),
                  pl.BlockSpec((CHUNK, D // 2), lambda n: (N - 1 - n, 0)),
                  pl.BlockSpec((CHUNK, D), lambda n: (n, 5)),
                  pl.BlockSpec((CHUNK, D), lambda n: (N - 1 - n, 5)),
                  pl.BlockSpec((2, HP, 128, 128), lambda n: (0, 0, 0, 0))],
        out_specs=[pl.BlockSpec((1, HP, 128, 128), lambda n: (n, 0, 0, 0)),
                   pl.BlockSpec((1, HP, 128, 128), lambda n: (N - 1 - n, 0, 0, 0))],
        out_shape=[st, st],
        scratch_shapes=[pltpu.VMEM((HP, 128, 128), F32), pltpu.VMEM((HP, 128, 128), F32)],
        compiler_params=_cparams(("arbitrary",)),
    )(lg, kr, kr, p, p, s0)


def _decay_masks(lgf, lgb):
    i = lax.broadcasted_iota(jnp.int32, (CHUNK, CHUNK), 0).astype(F32)
    j = lax.broadcasted_iota(jnp.int32, (CHUNK, CHUNK), 1).astype(F32)
    d = i - j
    mf = jnp.where(d > 0, jnp.exp(lgf * jnp.maximum(d, 0.0)), 0.0)
    mb = jnp.where(d < 0, jnp.exp(lgb * jnp.maximum(-d, 0.0)), 0.0)
    m = mf + mb + jnp.where(d == 0, 2.0, 0.0)
    return m, mf * d, mb * (-d)


def _ret_out(qr, kr, p, sf_prev, sb_prev, gn_w, lg, D):
    L = qr.shape[0]
    H = D // DV
    N = L // CHUNK
    HP = H // 2

    def body(lg_ref, q_ref, k_ref, v_ref, zb_ref, sf_ref, sb_ref, gn_ref, o_ref, yb_ref):
        i = lax.broadcasted_iota(jnp.int32, (CHUNK, 128), 0).astype(F32)
        for pr in range(HP):
            q2 = q_ref[:, pr * 128:(pr + 1) * 128]
            k2 = k_ref[:, pr * 128:(pr + 1) * 128]
            sfp = sf_ref[0, pr].astype(BF16)
            sbp = sb_ref[0, pr].astype(BF16)
            for e in range(2):
                h = 2 * pr + e
                sl = slice(h * DV, (h + 1) * DV)
                lgf, lgb = lg_ref[0, h], lg_ref[1, h]
                qm = jnp.where(_head_lane_mask(q2.shape, e), q2, jnp.zeros_like(q2))
                m, _, _ = _decay_masks(lgf, lgb)
                a = (_dot_nt(qm, k2) * m).astype(BF16)
                qf = qm.astype(F32)
                o = _dot(a, v_ref[:, sl])
                o += _dot((qf * jnp.exp(lgf * (i + 1.0))).astype(BF16), sfp)
                o += _dot((qf * jnp.exp(lgb * (CHUNK - i))).astype(BF16), sbp)
                o_ref[:, sl] = o
                mu = jnp.mean(o, axis=-1, keepdims=True)
                oc = o - mu
                rstd = lax.rsqrt(jnp.mean(oc * oc, axis=-1, keepdims=True) + EPS)
                zb = zb_ref[:, sl].astype(F32)
                yb_ref[:, sl] = (zb * _sigmoid(zb) * (oc * rstd * gn_ref[:, sl])).astype(BF16)

    return pl.pallas_call(
        body, name="ret_out", grid=(N,),
        in_specs=[_smem_spec(),
                  pl.BlockSpec((CHUNK, D // 2), lambda n: (n, 0)),
                  pl.BlockSpec((CHUNK, D // 2), lambda n: (n, 0)),
                  pl.BlockSpec((CHUNK, D), lambda n: (n, 5)),
                  pl.BlockSpec((CHUNK, D), lambda n: (n, 6)),
                  pl.BlockSpec((1, HP, 128, 128), lambda n: (n, 0, 0, 0)),
                  pl.BlockSpec((1, HP, 128, 128), lambda n: (n, 0, 0, 0)),
                  _vec_spec(D)],
        out_specs=[pl.BlockSpec((CHUNK, D), lambda n: (n, 0)), pl.BlockSpec((CHUNK, D), lambda n: (n, 0))],
        out_shape=[jax.ShapeDtypeStruct((L, D), F32), jax.ShapeDtypeStruct((L, D), BF16)],
        compiler_params=_cparams(("parallel",)),
    )(lg, qr, kr, p, p, sf_prev, sb_prev, gn_w)


def _mid(ya, yb, p, x, tgt, w3, g, fw, D):
    L = x.shape[0]
    tm = min(256, L)
    nt = L // tm

    def body(ya_ref, yb_ref, ga_ref, gb_ref, x_ref, t_ref, w_hbm, g_ref, fw_ref,
             dx1_ref, dya_ref, dyb_ref, dgab_ref, dw_hbm, st_ref, w_vm, dw_acc, sem):
        i = pl.program_id(0)

        @pl.when(i == 0)
        def _():
            cp = pltpu.make_async_copy(w_hbm, w_vm, sem)
            cp.start()
            dw_acc[...] = jnp.zeros_like(dw_acc)
            st_ref[...] = jnp.zeros_like(st_ref)
            cp.wait()

        ya_b, yb_b = ya_ref[...], yb_ref[...]
        y_a = _dot(ya_b, w_vm[0])
        y_b = _dot(yb_b, w_vm[1])
        sga = _sigmoid(ga_ref[...].astype(F32))
        sgb = _sigmoid(gb_ref[...].astype(F32))
        mix_b = (sga * y_a + sgb * y_b).astype(BF16)
        y_x = _dot(mix_b, w_vm[2])
        gvec, fwv = g_ref[...], fw_ref[...]
        x1 = x_ref[...] + gvec * y_x
        r1 = lax.rsqrt(jnp.mean(x1 * x1, axis=-1, keepdims=True) + EPS)
        xh = x1 * r1
        diff = xh * fwv - t_ref[...]
        dout = diff * (1.0 / D)
        dxh = dout * fwv
        dx1 = r1 * (dxh - xh * jnp.mean(dxh * xh, axis=-1, keepdims=True))
        dx1_ref[...] = dx1
        st_ref[0:1, :] += jnp.sum(dout * xh, axis=0, keepdims=True)
        st_ref[1:2, :] += jnp.sum(dx1 * y_x, axis=0, keepdims=True)
        st_ref[2:3, :] += jnp.sum(diff * diff, axis=0, keepdims=True)
        dyx_b = (dx1 * gvec).astype(BF16)
        dmix = _dot_nt(dyx_b, w_vm[2])
        dw_acc[2] += _dot_tn(mix_b, dyx_b)
        dya_b = (dmix * sga).astype(BF16)
        dyb_b = (dmix * sgb).astype(BF16)
        dgab_ref[:, 0:D] = (dmix * y_a * sga * (1.0 - sga)).astype(BF16)
        dgab_ref[:, D:2 * D] = (dmix * y_b * sgb * (1.0 - sgb)).astype(BF16)
        dya_ref[...] = _dot_nt(dya_b, w_vm[0])
        dyb_ref[...] = _dot_nt(dyb_b, w_vm[1])
        dw_acc[0] += _dot_tn(ya_b, dya_b)
        dw_acc[1] += _dot_tn(yb_b, dyb_b)

        @pl.when(i == nt - 1)
        def _():
            out = pltpu.make_async_copy(dw_acc, dw_hbm, sem)
            out.start()
            out.wait()

    row = lambda col: pl.BlockSpec((tm, D), lambda i: (i, col))
    any_spec = pl.BlockSpec(memory_space=pl.ANY)
    f32o = jax.ShapeDtypeStruct((L, D), F32)
    return pl.pallas_call(
        body, name="mid", grid=(nt,),
        in_specs=[row(0), row(0), row(7), row(8), row(0), row(0), any_spec, _vec_spec(D), _vec_spec(D)],
        out_specs=[row(0), row(0), row(0), pl.BlockSpec((tm, 2 * D), lambda i: (i, 0)), any_spec,
                   pl.BlockSpec((8, D), lambda i: (0, 0))],
        out_shape=[f32o, f32o, f32o, jax.ShapeDtypeStruct((L, 2 * D), BF16),
                   jax.ShapeDtypeStruct((3, D, D), F32), jax.ShapeDtypeStruct((8, D), F32)],
        scratch_shapes=[pltpu.VMEM((3, D, D), BF16), pltpu.VMEM((3, D, D), F32), pltpu.SemaphoreType.DMA],
        compiler_params=_cparams(("arbitrary",), VMEM_LIMIT),
    )(ya, yb, p, p, x, tgt, w3, g, fw)


def _conv_bwd(dya, p, conv_w, conv_b, D):
    L = p.shape[0]
    tl = min(256, L)
    nt = L // tl

    def body(d_ref, h_ref, bg_ref, cg_ref, za_ref,
             dp_ref, dn_ref, hp_ref, hn_ref, bp_ref, bn_ref, cp_ref, cn_ref, zp_ref, zn_ref,
             w_ref, b_ref, dc_ref, st_ref):
        i = pl.program_id(0)

        @pl.when(i == 0)
        def _():
            st_ref[...] = jnp.zeros_like(st_ref)

        first, last = i == 0, i == nt - 1
        h = h_ref[...].astype(F32)
        cg = cg_ref[...].astype(F32)
        bg = bg_ref[...].astype(F32)
        za = za_ref[...].astype(F32)
        dy = d_ref[...].astype(F32)
        u = cg * h
        u_above = jnp.where(first, 0.0, cp_ref[15:16, :].astype(F32) * hp_ref[15:16, :].astype(F32))
        u_below = jnp.where(last, 0.0, cn_ref[0:1, :].astype(F32) * hn_ref[0:1, :].astype(F32))
        u_dn, u_up = _shift_rows(u, u_above, u_below)
        w0, w1, w2 = w_ref[0:1, :], w_ref[1:2, :], w_ref[2:3, :]
        co = w0 * u_dn + w1 * u + w2 * u_up + b_ref[...]
        sz = _sigmoid(za)
        silu = za * sz
        dc_ref[:, 3 * D:4 * D] = (dy * bg * co * (sz * (1.0 + za * (1.0 - sz)))).astype(BF16)
        dc_ref[:, D:2 * D] = (dy * silu * co).astype(BF16)
        dco = dy * silu * bg

        def edge(dr, zr, br, r):
            z = zr[r:r + 1, :].astype(F32)
            return dr[r:r + 1, :].astype(F32) * (z * _sigmoid(z)) * br[r:r + 1, :].astype(F32)

        dco_above = jnp.where(first, 0.0, edge(dp_ref, zp_ref, bp_ref, 15))
        dco_below = jnp.where(last, 0.0, edge(dn_ref, zn_ref, bn_ref, 0))
        dco_dn, dco_up = _shift_rows(dco, dco_above, dco_below)
        du = w0 * dco_up + w1 * dco + w2 * dco_dn
        dc_ref[:, 2 * D:3 * D] = (du * h).astype(BF16)
        dc_ref[:, 0:D] = (du * cg).astype(BF16)
        st_ref[0:1, :] += jnp.sum(dco * u_dn, axis=0, keepdims=True)
        st_ref[1:2, :] += jnp.sum(dco * u, axis=0, keepdims=True)
        st_ref[2:3, :] += jnp.sum(dco * u_up, axis=0, keepdims=True)
        st_ref[3:4, :] += jnp.sum(dco, axis=0, keepdims=True)

    main = lambda col: pl.BlockSpec((tl, D), lambda i: (i, col))
    halos = []
    for col in (0, 0, 1, 2, 3):
        halos.extend(_halo_specs(tl, L, D, col))
    return pl.pallas_call(
        body, name="conv_bwd", grid=(nt,),
        in_specs=[main(0), main(0), main(1), main(2), main(3)] + halos
                 + [pl.BlockSpec((8, D), lambda i: (0, 0)), _vec_spec(D)],
        out_specs=[pl.BlockSpec((tl, 4 * D), lambda i: (i, 0)), pl.BlockSpec((8, D), lambda i: (0, 0))],
        out_shape=[jax.ShapeDtypeStruct((L, 4 * D), BF16), jax.ShapeDtypeStruct((8, D), F32)],
        compiler_params=_cparams(("arbitrary",)),
    )(dya, p, p, p, p, dya, dya, p, p, p, p, p, p, p, p, conv_w, conv_b)


def _ret_bwd_pre(dyb, p, o, gn_w, D):
    L = o.shape[0]
    H = D // DV
    tl = min(256, L)

    def body(d_ref, zb_ref, o_ref, gn_ref, do_ref, dzb_ref, st_ref):
        @pl.when(pl.program_id(0) == 0)
        def _():
            st_ref[...] = jnp.zeros_like(st_ref)

        for h in range(H):
            sl = slice(h * DV, (h + 1) * DV)
            ov = o_ref[:, sl]
            mu = jnp.mean(ov, axis=-1, keepdims=True)
            oc = ov - mu
            rstd = lax.rsqrt(jnp.mean(oc * oc, axis=-1, keepdims=True) + EPS)
            rn = oc * rstd
            gw = gn_ref[:, sl]
            zb = zb_ref[:, sl].astype(F32)
            sz = _sigmoid(zb)
            dy = d_ref[:, sl].astype(F32)
            dzb_ref[:, sl] = (dy * (rn * gw) * (sz * (1.0 + zb * (1.0 - sz)))).astype(BF16)
            dretn = dy * (zb * sz)
            st_ref[0:1, sl] += jnp.sum(dretn * rn, axis=0, keepdims=True)
            drn = dretn * gw
            do = rstd * (drn - jnp.mean(drn, axis=-1, keepdims=True)
                         - rn * jnp.mean(drn * rn, axis=-1, keepdims=True))
            do_ref[:, sl] = do.astype(BF16)

    main = lambda col: pl.BlockSpec((tl, D), lambda i: (i, col))
    bfo = jax.ShapeDtypeStruct((L, D), BF16)
    return pl.pallas_call(
        body, name="ret_bwd_pre", grid=(L // tl,),
        in_specs=[main(0), main(6), main(0), _vec_spec(D)],
        out_specs=[main(0), main(0), pl.BlockSpec((8, D), lambda i: (0, 0))],
        out_shape=[bfo, bfo, jax.ShapeDtypeStruct((8, D), F32)],
        compiler_params=_cparams(("arbitrary",)),
    )(dyb, p, o, gn_w)


def _ret_bwd_states(qr, do, lg, D):
    L = qr.shape[0]
    H = D // DV
    N = L // CHUNK
    HP = H // 2

    def body(lg_ref, qf_ref, qb_ref, dof_ref, dob_ref, dsf_out, dsb_out, ds0_out, dsf, dsb):
        n = pl.program_id(0)

        @pl.when(n == 0)
        def _():
            dsf[...] = jnp.zeros_like(dsf)
            dsb[...] = jnp.zeros_like(dsb)

        dsf_out[0] = dsf[...]
        dsb_out[0] = dsb[...]
        i = lax.broadcasted_iota(jnp.int32, (CHUNK, 128), 0).astype(F32)
        full = jnp.full((128, 128), float(CHUNK), F32)
        for pr in range(HP):
            qf2 = qf_ref[:, pr * 128:(pr + 1) * 128].astype(F32)
            qb2 = qb_ref[:, pr * 128:(pr + 1) * 128].astype(F32)
            inc_f, inc_b, gf, gb = [], [], [], []
            for e in range(2):
                h = 2 * pr + e
                lgf, lgb = lg_ref[0, h], lg_ref[1, h]
                inc_f.append(_dot_tn((qf2 * jnp.exp(lgf * (i + 1.0))).astype(BF16), dof_ref[:, h * DV:(h + 1) * DV]))
                inc_b.append(_dot_tn((qb2 * jnp.exp(lgb * (CHUNK - i))).astype(BF16), dob_ref[:, h * DV:(h + 1) * DV]))
                gf.append(jnp.exp(lgf * full))
                gb.append(jnp.exp(lgb * full))
            dsf[pr] = _pair_select(gf[0], gf[1]) * dsf[pr] + _pair_select(inc_f[0], inc_f[1])
            dsb[pr] = _pair_select(gb[0], gb[1]) * dsb[pr] + _pair_select(inc_b[0], inc_b[1])

        @pl.when(n == N - 1)
        def _():
            ds0_out[0] = dsf[...]
            ds0_out[1] = dsb[...]

    st = jax.ShapeDtypeStruct((N, HP, 128, 128), F32)
    return pl.pallas_call(
        body, name="ret_bwd_states", grid=(N,),
        in_specs=[_smem_spec(),
                  pl.BlockSpec((CHUNK, D // 2), lambda n: (N - 1 - n, 0)),
                  pl.BlockSpec((CHUNK, D // 2), lambda n: (n, 0)),
                  pl.BlockSpec((CHUNK, D), lambda n: (N - 1 - n, 0)),
                  pl.BlockSpec((CHUNK, D), lambda n: (n, 0))],
        out_specs=[pl.BlockSpec((1, HP, 128, 128), lambda n: (N - 1 - n, 0, 0, 0)),
                   pl.BlockSpec((1, HP, 128, 128), lambda n: (n, 0, 0, 0)),
                   pl.BlockSpec((2, HP, 128, 128), lambda n: (0, 0, 0, 0))],
        out_shape=[st, st, jax.ShapeDtypeStruct((2, HP, 128, 128), F32)],
        scratch_shapes=[pltpu.VMEM((HP, 128, 128), F32), pltpu.VMEM((HP, 128, 128), F32)],
        compiler_params=_cparams(("arbitrary",)),
    )(lg, qr, qr, do, do)


def _ret_bwd_main(qr, kr, p, do, sf_prev, sb_prev, dsf, dsb, c2, s2, lg, D):
    L = qr.shape[0]
    H = D // DV
    N = L // CHUNK
    HP = H // 2
    W = D // 2

    def body(lg_ref, q_ref, k_ref, v_ref, do_ref, sf_ref, sb_ref, dsf_ref, dsb_ref, c_ref, s_ref,
             dr_ref, st_ref):
        @pl.when(pl.program_id(0) == 0)
        def _():
            st_ref[...] = jnp.zeros_like(st_ref)

        dqk_ref = dr_ref.at[:, 0:D]
        dv_ref = dr_ref.at[:, D:2 * D]
        i = lax.broadcasted_iota(jnp.int32, (CHUNK, 128), 0).astype(F32)
        lane = lax.broadcasted_iota(jnp.int32, (1, 128), 1)
        rowid = lax.broadcasted_iota(jnp.int32, (128, 128), 0)
        full = jnp.full((1, 1), float(CHUNK), F32)
        acc_f = jnp.zeros((1, 128), F32)
        acc_b = jnp.zeros((1, 128), F32)
        for pr in range(HP):
            ps = slice(pr * 128, (pr + 1) * 128)
            q2, k2 = q_ref[:, ps], k_ref[:, ps]
            sf32, sb32 = sf_ref[0, pr], sb_ref[0, pr]
            dsf32, dsb32 = dsf_ref[0, pr], dsb_ref[0, pr]
            sfp, sbp = sf32.astype(BF16), sb32.astype(BF16)
            dsfp, dsbp = dsf32.astype(BF16), dsb32.astype(BF16)
            dq2 = jnp.zeros((CHUNK, 128), F32)
            dk2 = jnp.zeros((CHUNK, 128), F32)
            for e in range(2):
                h = 2 * pr + e
                sl = slice(h * DV, (h + 1) * DV)
                lgf, lgb = lg_ref[0, h], lg_ref[1, h]
                hm = _head_lane_mask(q2.shape, e)
                qm = jnp.where(hm, q2, jnp.zeros_like(q2))
                km = jnp.where(hm, k2, jnp.zeros_like(k2))
                qf, kf = qm.astype(F32), km.astype(F32)
                v, do = v_ref[:, sl], do_ref[:, sl]
                vf, dof = v.astype(F32), do.astype(F32)
                m, mf1, mb1 = _decay_masks(lgf, lgb)
                m_t, _, _ = _decay_masks(lgb, lgf)
                sc = _dot_nt(qm, k2)
                dpm = _dot_nt(do, v)
                dsc = (dpm * m).astype(BF16)
                a_t = (_dot_nt(km, q2) * m_t).astype(BF16)
                dsc_t = (_dot_nt(v, do) * m_t).astype(BF16)
                dq_f, dq_b = jnp.exp(lgf * (i + 1.0)), jnp.exp(lgb * (CHUNK - i))
                dk_f, dk_b = jnp.exp(lgf * (CHUNK - 1.0 - i)), jnp.exp(lgb * i)
                dq = _dot(dsc, km)
                dq += jnp.where(hm, dq_f * _dot_nt(do, sfp) + dq_b * _dot_nt(do, sbp), 0.0)
                dk = _dot(dsc_t, qm)
                dk += jnp.where(hm, dk_f * _dot_nt(v, dsfp) + dk_b * _dot_nt(v, dsbp), 0.0)
                kdf = _dot((kf * dk_f).astype(BF16), dsfp)
                kdb = _dot((kf * dk_b).astype(BF16), dsbp)
                dv_ref[:, sl] = (_dot(a_t, do) + kdf + kdb).astype(BF16)
                dq2 += dq
                dk2 += dk
                xf = _dot((qf * dq_f).astype(BF16), sfp)
                xb = _dot((qf * dq_b).astype(BF16), sbp)
                pair = (rowid < DK) if e == 0 else (rowid >= DK)
                gcf, gcb = jnp.exp(lgf * full), jnp.exp(lgb * full)
                tf = _sum_all(sc * dpm * mf1) + _sum_all(xf * dof * (i + 1.0)) \
                    + _sum_all(kdf * vf * (CHUNK - 1.0 - i)) \
                    + CHUNK * gcf * _sum_all(jnp.where(pair, dsf32 * sf32, 0.0))
                tb = _sum_all(sc * dpm * mb1) + _sum_all(xb * dof * (CHUNK - i)) \
                    + _sum_all(kdb * vf * i) \
                    + CHUNK * gcb * _sum_all(jnp.where(pair, dsb32 * sb32, 0.0))
                acc_f += jnp.where(lane == h, tf, 0.0)
                acc_b += jnp.where(lane == h, tb, 0.0)
            c, s = c_ref[:, ps], s_ref[:, ps]
            dqk_ref[:, ps] = (dq2 * c - _swap_halves(dq2) * s).astype(BF16)
            dqk_ref[:, W + pr * 128:W + (pr + 1) * 128] = ((dk2 * c - _swap_halves(dk2) * s) * K_SCALE).astype(BF16)
        st_ref[0:1, :] += acc_f
        st_ref[1:2, :] += acc_b

    st_spec = pl.BlockSpec((1, HP, 128, 128), lambda n: (n, 0, 0, 0))
    half = pl.BlockSpec((CHUNK, W), lambda n: (n, 0))
    return pl.pallas_call(
        body, name="ret_bwd_main", grid=(N,),
        in_specs=[_smem_spec(), half, half,
                  pl.BlockSpec((CHUNK, D), lambda n: (n, 5)),
                  pl.BlockSpec((CHUNK, D), lambda n: (n, 0)),
                  st_spec, st_spec, st_spec, st_spec, half, half],
        out_specs=[pl.BlockSpec((CHUNK, 2 * D), lambda n: (n, 0)),
                   pl.BlockSpec((8, 128), lambda n: (0, 0))],
        out_shape=[jax.ShapeDtypeStruct((L, 2 * D), BF16), jax.ShapeDtypeStruct((8, 128), F32)],
        compiler_params=_cparams(("arbitrary",)),
    )(lg, qr, kr, p, do, sf_prev, sb_prev, dsf, dsb, c2, s2)


def _ctx_bwd(pc, pqk_c, ds0, lg, D):
    Lc = pc.shape[0]
    H = D // DV
    HP = H // 2
    W = D // 2

    def body(lg_ref, k_ref, v_ref, ds_ref, dr_ref, st_ref):
        dqk_ref = dr_ref.at[:, 0:D]
        dv_ref = dr_ref.at[:, D:2 * D]
        m = lax.broadcasted_iota(jnp.int32, (Lc, 128), 0).astype(F32)
        lane = lax.broadcasted_iota(jnp.int32, (1, 128), 1)
        acc_f = jnp.zeros((1, 128), F32)
        acc_b = jnp.zeros((1, 128), F32)
        dqk_ref[:, 0:W] = jnp.zeros((Lc, W), BF16)
        for pr in range(HP):
            ps = slice(pr * 128, (pr + 1) * 128)
            k2 = k_ref[:, ps].astype(F32) * K_SCALE
            dsfp, dsbp = ds_ref[0, pr].astype(BF16), ds_ref[1, pr].astype(BF16)
            dk2 = jnp.zeros((Lc, 128), F32)
            for e in range(2):
                h = 2 * pr + e
                sl = slice(h * DV, (h + 1) * DV)
                hm = _head_lane_mask(k2.shape, e)
                km = jnp.where(hm, k2, 0.0)
                v = v_ref[:, sl]
                vf = v.astype(F32)
                dec_f = jnp.exp(lg_ref[0, h] * (Lc - 1.0 - m))
                dec_b = jnp.exp(lg_ref[1, h] * m)
                kdf = _dot((km * dec_f).astype(BF16), dsfp)
                kdb = _dot((km * dec_b).astype(BF16), dsbp)
                dv_ref[:, sl] = (kdf + kdb).astype(BF16)
                dk2 += jnp.where(hm, dec_f * _dot_nt(v, dsfp) + dec_b * _dot_nt(v, dsbp), 0.0)
                acc_f += jnp.where(lane == h, _sum_all(kdf * vf * (Lc - 1.0 - m)), 0.0)
                acc_b += jnp.where(lane == h, _sum_all(kdb * vf * m), 0.0)
            dqk_ref[:, W + pr * 128:W + (pr + 1) * 128] = (dk2 * K_SCALE).astype(BF16)
        st_ref[...] = jnp.zeros_like(st_ref)
        st_ref[0:1, :] = acc_f
        st_ref[1:2, :] = acc_b

    return pl.pallas_call(
        body, name="ctx_bwd", grid=(1,),
        in_specs=[_smem_spec(), pl.BlockSpec((Lc, W), lambda i: (0, 1)), pl.BlockSpec((Lc, D), lambda i: (0, 5)),
                  pl.BlockSpec((2, HP, 128, 128), lambda i: (0, 0, 0, 0))],
        out_specs=[pl.BlockSpec((Lc, 2 * D), lambda i: (0, 0)), pl.BlockSpec((8, 128), lambda i: (0, 0))],
        out_shape=[jax.ShapeDtypeStruct((Lc, 2 * D), BF16), jax.ShapeDtypeStruct((8, 128), F32)],
    )(lg, pqk_c, pc, ds0)


class _Exchange(NamedTuple):
    inputs: tuple
    out_shapes: tuple
    n_copies: int
    build: Callable


def _exchange_parts(exchange):
    if exchange is None:
        return [], [], [], [], []
    n = exchange.n_copies
    return (list(exchange.inputs), [ANY] * len(exchange.inputs), list(exchange.out_shapes),
            [ANY] * len(exchange.out_shapes), [pltpu.SemaphoreType.DMA((n,)), pltpu.SemaphoreType.DMA((n,))])


def _dxm(groups, col0, w, x, nw, sc, dx1, name, exchange=None):
    L, D = x.shape
    tm = min(256, L)
    nt = L // tm
    ng = len(groups)
    widths = [g.shape[1] for g in groups]
    wtot = sum(widths)
    with_dx = dx1 is not None
    ex_args, ex_in_specs, ex_shapes, ex_out_specs, ex_scratch = _exchange_parts(exchange)
    n_in = ng + 4 + (1 if with_dx else 0)
    n_out = 2 if with_dx else 1

    def body(*refs):
        group_refs = refs[:ng]
        w_hbm, x_ref, nw_ref, sc_ref = refs[ng:ng + 4]
        ex_in = refs[n_in:n_in + len(ex_args)]
        outs = refs[n_in + len(ex_args):]
        if with_dx:
            dx1_ref, gx_ref, st_ref = refs[ng + 4], outs[0], outs[1]
        else:
            st_ref = outs[0]
        ex_out = outs[n_out:n_out + len(ex_shapes)]
        w_vm, sem = outs[n_out + len(ex_shapes):n_out + len(ex_shapes) + 2]
        ex_sems = outs[n_out + len(ex_shapes) + 2:]
        i = pl.program_id(0)

        @pl.when(i == 0)
        def _():
            cp = pltpu.make_async_copy(w_hbm.at[:, col0 * D:col0 * D + wtot], w_vm, sem)
            cp.start()
            if exchange is not None:
                for rc in exchange.build(ex_in, ex_out, *ex_sems):
                    rc.start()
            st_ref[...] = jnp.zeros_like(st_ref)
            cp.wait()

        dxm, off = None, 0
        for g_ref, wd in zip(group_refs, widths):
            part = _dot_nt(g_ref[...], w_vm[:, off:off + wd])
            dxm = part if dxm is None else dxm + part
            off += wd

        xv = x_ref[...]
        r = lax.rsqrt(jnp.mean(xv * xv, axis=-1, keepdims=True) + EPS)
        xh = xv * r
        nwv = nw_ref[...]
        dxn = dxm * (1.0 + sc_ref[...])
        st_ref[0:1, :] += jnp.sum(dxm, axis=0, keepdims=True)
        st_ref[1:2, :] += jnp.sum(dxm * (xh * nwv), axis=0, keepdims=True)
        st_ref[2:3, :] += jnp.sum(dxn * xh, axis=0, keepdims=True)
        if with_dx:
            dxh = dxn * nwv
            gx_ref[...] = dx1_ref[...] + r * (dxh - xh * jnp.mean(dxh * xh, axis=-1, keepdims=True))

        if exchange is not None:
            @pl.when(i == nt - 1)
            def _():
                for rc in exchange.build(ex_in, ex_out, *ex_sems):
                    rc.wait()

    row = pl.BlockSpec((tm, D), lambda i: (i, 0))
    in_specs = [pl.BlockSpec((tm, wd), lambda i: (i, 0)) for wd in widths] + [ANY, row, _vec_spec(D), _vec_spec(D)]
    out_specs = [pl.BlockSpec((8, D), lambda i: (0, 0))]
    out_shape = [jax.ShapeDtypeStruct((8, D), F32)]
    args = list(groups) + [w, x, nw, sc]
    if with_dx:
        in_specs.append(row)
        out_specs.insert(0, row)
        out_shape.insert(0, jax.ShapeDtypeStruct((L, D), F32))
        args.append(dx1)
    res = pl.pallas_call(
        body, name=name, grid=(nt,),
        in_specs=in_specs + ex_in_specs, out_specs=out_specs + ex_out_specs, out_shape=out_shape + ex_shapes,
        scratch_shapes=[pltpu.VMEM((D, wtot), BF16), pltpu.SemaphoreType.DMA] + ex_scratch,
        compiler_params=_cparams(("arbitrary",), VMEM_LIMIT),
    )(*args, *ex_args)
    gx = res[0] if with_dx else None
    return (gx, res[n_out - 1], *res[n_out:])


DW_TN = 512


def _dw_in(hidx, xmt, groups, cmt, dr_c, D, name, exchange=None):
    L = xmt.shape[1]
    Lc = cmt.shape[1]
    Dh = D // 2
    tn = min(DW_TN, D)
    nblk = [g.shape[1] // tn for g in groups]
    starts = [sum(nblk[:g]) for g in range(len(groups))]
    ng = len(groups)
    nj = sum(nblk)
    ex_args, ex_in_specs, ex_shapes, ex_out_specs, ex_scratch = _exchange_parts(exchange)

    def body(*refs):
        h_ref, xt_hbm = refs[0], refs[1]
        group_refs = refs[2:2 + ng]
        ct_ref, drc_ref = refs[2 + ng:4 + ng]
        ex_in = refs[4 + ng:4 + ng + len(ex_args)]
        outs = refs[4 + ng + len(ex_args):]
        o_ref = outs[0]
        ex_out = outs[1:1 + len(ex_shapes)]
        xt_vm, sem = outs[1 + len(ex_shapes):3 + len(ex_shapes)]
        ex_sems = outs[3 + len(ex_shapes):]
        j = pl.program_id(0)

        @pl.when(j == 0)
        def _():
            rows = pl.ds(pl.multiple_of(h_ref[0] * Dh, Dh), Dh)
            cp = pltpu.make_async_copy(xt_hbm.at[rows, :], xt_vm, sem)
            cp.start()
            if exchange is not None:
                for rc in exchange.build(ex_in, ex_out, *ex_sems):
                    rc.start()
            cp.wait()

        for g in range(ng):
            @pl.when((j >= starts[g]) & (j < starts[g] + nblk[g]))
            def _(g=g):
                acc = _dot(xt_vm[...], group_refs[g][...])
                if g == 1:
                    acc += _dot(ct_ref[...], drc_ref[...])
                o_ref[...] = acc

        if exchange is not None:
            @pl.when(j == nj - 1)
            def _():
                for rc in exchange.build(ex_in, ex_out, *ex_sems):
                    rc.wait()

    def group_spec(g, rows):
        return pl.BlockSpec((rows, tn), lambda j, h: (0, jnp.clip(j - starts[g], 0, nblk[g] - 1)))

    res = pl.pallas_call(
        body, name=name,
        grid_spec=pltpu.PrefetchScalarGridSpec(
            num_scalar_prefetch=1, grid=(nj,),
            in_specs=[ANY] + [group_spec(g, L) for g in range(ng)]
                     + [pl.BlockSpec((Dh, Lc), lambda j, h: (h[0], 0)), group_spec(1, Lc)] + ex_in_specs,
            out_specs=[pl.BlockSpec((Dh, tn), lambda j, h: (0, j))] + ex_out_specs,
            scratch_shapes=[pltpu.VMEM((Dh, L), BF16), pltpu.SemaphoreType.DMA] + ex_scratch),
        out_shape=[jax.ShapeDtypeStruct((Dh, nj * tn), F32)] + ex_shapes,
        compiler_params=_cparams(("arbitrary",), VMEM_LIMIT),
    )(hidx, xmt, *groups, cmt, dr_c, *ex_args)
    return tuple(res)


def _local_step(x, ctx, tgt, mod_x, mod_c, norm_w, conv_w8, conv_b, lg, gn_w, fw, w_in, w3):
    L, D = x.shape
    sh_x, sc_x, g_x = mod_x[0:1], mod_x[1:2], mod_x[2:3]
    sh_c, sc_c = mod_c[0:1], mod_c[1:2]
    c2, s2 = _rope_tables(L, D)

    xm, xmt = _norm_mod(x, norm_w, sc_x, sh_x, "norm_mod_x")
    cm, cmt = _norm_mod(ctx, norm_w, sc_c, sh_c, "norm_mod_ctx")
    p, pqk = _in_proj(xm, w_in, "in_proj_x")
    pc, pqk_c = _in_proj(cm, w_in, "in_proj_ctx")
    ya = _conv_gate_fwd(p, conv_w8, conv_b, D)
    qr, kr = _rope_fwd(pqk, c2, s2, D)
    s0 = _ctx_states(pc, pqk_c, lg, D)
    sf_prev, sb_prev = _ret_states(kr, p, s0, lg, D)
    o, yb = _ret_out(qr, kr, p, sf_prev, sb_prev, gn_w, lg, D)
    dx1, dya, dyb, dgab, dw3, st_mid = _mid(ya, yb, p, x, tgt, w3, g_x, fw, D)
    dconv, st_conv = _conv_bwd(dya, p, conv_w8, conv_b, D)
    do, dzb, st_gn = _ret_bwd_pre(dyb, p, o, gn_w, D)
    dsf, dsb, ds0 = _ret_bwd_states(qr, do, lg, D)
    dret, st_lg = _ret_bwd_main(qr, kr, p, do, sf_prev, sb_prev, dsf, dsb, c2, s2, lg, D)
    dret_c, st_lgc = _ctx_bwd(pc, pqk_c, ds0, lg, D)
    groups = (dconv, dret, dzb, dgab)
    _, st_c = _dxm((dret_c,), 4, w_in, ctx, norm_w, sc_c, None, "dxm_ctx")
    return groups, dret_c, xmt, cmt, dx1, sc_x, dw3, (st_mid, st_conv, st_gn, st_lg, st_lgc, st_c)


CHIP_FLIPS = (4, 2, 6)
ANY = pl.BlockSpec(memory_space=pl.ANY)
VMEM_FULL = pl.BlockSpec(memory_space=pltpu.VMEM)


def _position():
    return lax.axis_index("x"), lax.axis_index("y"), lax.axis_index("c")


def _peer(pos, k):
    x, y, c = pos
    return (1 - x if k & 4 else x, 1 - y if k & 2 else y, 1 - c if k & 1 else c)


def _dev_id(pos):
    return 4 * pos[0] + 2 * pos[1] + pos[2]


def _shard_of(pos):
    return 2 * pos[0] + pos[1]


def _remote(src, dst, send_sems, recv_sems, idx, to):
    return pltpu.make_async_remote_copy(src_ref=src, dst_ref=dst, send_sem=send_sems.at[idx],
                                        recv_sem=recv_sems.at[idx], device_id=to, device_id_type=MESH)


def _dot_f32(a, b):
    return jnp.dot(a, b, precision=lax.Precision.HIGHEST, preferred_element_type=F32)


def _silu(x):
    return x * _sigmoid(x)


def _fwd_small(c8, cctx8, ada_w, ada_b, conv_w8):
    D = c8.shape[1]
    Wm = ada_w.shape[1]
    Dq = conv_w8.shape[1]

    def body(c_ref, cc_ref, aw_ref, ab_ref, cw_ref, act_ref, mod_ref, cwf_ref,
             cbuf, pmine, pbuf, wbuf, s_c, r_c, s_p, r_p, s_w, r_w):
        pos = _position()
        me, s = _dev_id(pos), _shard_of(pos)
        cbuf[me] = c_ref[...]
        wbuf[s] = cw_ref[...]
        sends = [_remote(c_ref, cbuf.at[me], s_c, r_c, k - 1, _peer(pos, k)) for k in range(1, 8)]
        sends += [_remote(cw_ref, wbuf.at[s], s_w, r_w, j, _peer(pos, k)) for j, k in enumerate(CHIP_FLIPS)]
        for cp in sends:
            cp.start()
        for k in range(1, 8):
            _remote(c_ref, cbuf.at[_dev_id(_peer(pos, k))], s_c, r_c, k - 1, _peer(pos, k)).wait_recv()
        for d in range(N_DEV):
            act_ref[d:d + 1, :] = _silu(cbuf[d, 0:1, :])
        act_ref[8:9, :] = _silu(cc_ref[0:1, :])
        act_ref[9:16, :] = jnp.zeros((7, D), F32)
        part = _dot_f32(act_ref[...], aw_ref[...])
        pmine[...] = part
        pbuf[s] = part
        psend = [_remote(pmine, pbuf.at[s], s_p, r_p, j, _peer(pos, k)) for j, k in enumerate(CHIP_FLIPS)]
        for cp in psend:
            cp.start()
        for j, k in enumerate(CHIP_FLIPS):
            t = _shard_of(_peer(pos, k))
            _remote(pmine, pbuf.at[t], s_p, r_p, j, _peer(pos, k)).wait_recv()
            _remote(cw_ref, wbuf.at[t], s_w, r_w, j, _peer(pos, k)).wait_recv()
        for t in range(N_SHARD):
            mod_ref[:, t * Wm:(t + 1) * Wm] = pbuf[t] + ab_ref[:, t * Wm:(t + 1) * Wm]
            cwf_ref[:, t * Dq:(t + 1) * Dq] = wbuf[t]
        for cp in sends + psend:
            cp.wait_send()

    return pl.pallas_call(
        body, name="fwd_small",
        in_specs=[VMEM_FULL] * 5, out_specs=[VMEM_FULL] * 3,
        out_shape=[jax.ShapeDtypeStruct((16, D), F32), jax.ShapeDtypeStruct((16, 3 * D), F32),
                   jax.ShapeDtypeStruct((8, D), F32)],
        scratch_shapes=[pltpu.VMEM((N_DEV, 8, D), F32), pltpu.VMEM((16, Wm), F32),
                        pltpu.VMEM((N_SHARD, 16, Wm), F32), pltpu.VMEM((N_SHARD, 8, Dq), F32),
                        pltpu.SemaphoreType.DMA((7,)), pltpu.SemaphoreType.DMA((7,)),
                        pltpu.SemaphoreType.DMA((3,)), pltpu.SemaphoreType.DMA((3,)),
                        pltpu.SemaphoreType.DMA((3,)), pltpu.SemaphoreType.DMA((3,))],
        compiler_params=_cparams(None, VMEM_LIMIT),
    )(c8, cctx8, ada_w, ada_b, conv_w8)


def _cols(t, w):
    return pl.ds(pl.multiple_of(t * w, 128), w)


def _ag_weights(w_in_s, w3_s):
    D, Wc = w_in_s.shape
    Dh = D // 2
    Do = w3_s.shape[2]

    def body(wi_ref, w3_ref, fi_ref, f3_ref, si, s3, send, recv, loc):
        pos = _position()
        c = pos[2]
        s = _shard_of(pos)
        sib = _peer(pos, 1)

        def cast_rows(r, carry):
            rows = pl.ds(pl.multiple_of(r * 64, 64), 64)
            si[rows, :] = wi_ref[rows, :].astype(BF16)
            return carry

        lax.fori_loop(0, D // 64, cast_rows, 0)
        for a in range(3):
            s3[a] = w3_ref[a].astype(BF16)

        def in_half(t, hf):
            return fi_ref.at[pl.ds(hf * Dh, Dh), _cols(t, Wc)]

        def w3_half(t, hf):
            return f3_ref.at[:, t, hf]

        own = [pltpu.make_async_copy(si, fi_ref.at[:, _cols(s, Wc)], loc.at[0]),
               pltpu.make_async_copy(s3, f3_ref.at[:, s], loc.at[1])]
        first = []
        for j, k in enumerate(CHIP_FLIPS):
            to = _peer(pos, k)
            first.append(_remote(si.at[pl.ds(c * Dh, Dh), :], in_half(s, c), send, recv, j, to))
            first.append(_remote(s3.at[:, c], w3_half(s, c), send, recv, 3 + j, to))
        for cp in own + first:
            cp.start()
        passed = []
        for j, k in enumerate(CHIP_FLIPS):
            t = _shard_of(_peer(pos, k))
            _remote(in_half(t, c), in_half(t, c), send, recv, j, sib).wait_recv()
            fwd_i = _remote(in_half(t, c), in_half(t, c), send, recv, 6 + j, sib)
            fwd_i.start()
            _remote(w3_half(t, c), w3_half(t, c), send, recv, 3 + j, sib).wait_recv()
            fwd_3 = _remote(w3_half(t, c), w3_half(t, c), send, recv, 9 + j, sib)
            fwd_3.start()
            passed += [fwd_i, fwd_3]
        for j, k in enumerate(CHIP_FLIPS):
            t = _shard_of(_peer(pos, k))
            _remote(in_half(t, 1 - c), in_half(t, 1 - c), send, recv, 6 + j, sib).wait_recv()
            _remote(w3_half(t, 1 - c), w3_half(t, 1 - c), send, recv, 9 + j, sib).wait_recv()
        for cp in first + passed:
            cp.wait_send()
        for cp in own:
            cp.wait()

    return pl.pallas_call(
        body, name="ag_weights",
        in_specs=[VMEM_FULL, VMEM_FULL], out_specs=[ANY, ANY],
        out_shape=[jax.ShapeDtypeStruct((D, N_SHARD * Wc), BF16), jax.ShapeDtypeStruct((3, N_SHARD, 2, Do, D), BF16)],
        scratch_shapes=[pltpu.VMEM((D, Wc), BF16), pltpu.VMEM((3, 2, Do, D), BF16),
                        pltpu.SemaphoreType.DMA((12,)), pltpu.SemaphoreType.DMA((12,)), pltpu.SemaphoreType.DMA((2,))],
        compiler_params=_cparams(None, VMEM_LIMIT),
    )(w_in_s, w3_s)


def _pair_exchange(dw_other, dw3):
    _, _, _, Do, D = dw3.shape

    def build(ins, outs, send, recv):
        pos = _position()
        sib = _peer(pos, 1)
        return [_remote(ins[0], outs[0], send, recv, 0, sib),
                _remote(ins[1].at[:, :, 1 - pos[2]], outs[1], send, recv, 1, sib)]

    shapes = (jax.ShapeDtypeStruct(dw_other.shape, F32), jax.ShapeDtypeStruct((3, N_SHARD, Do, D), F32))
    return _Exchange((dw_other, dw3), shapes, 2, build)


def _sum_pair(cidx, dw_mine, ri, dw3, r3):
    Dh, Wf = dw_mine.shape
    Wc = Wf // N_SHARD
    _, _, _, Do, D = dw3.shape
    tr = min(256, Dh)

    def body_i(a_ref, b_ref, o_ref):
        o_ref[...] = (a_ref[...] + b_ref[...]).astype(BF16)

    sum_i = pl.pallas_call(
        body_i, name="sum_pair_in", grid=(Dh // tr, N_SHARD),
        in_specs=[pl.BlockSpec((tr, Wc), lambda i, t: (i, t)), pl.BlockSpec((tr, Wc), lambda i, t: (i, t))],
        out_specs=pl.BlockSpec((None, tr, Wc), lambda i, t: (t, i, 0)),
        out_shape=jax.ShapeDtypeStruct((N_SHARD, Dh, Wc), BF16),
        compiler_params=_cparams(("parallel", "parallel")),
    )(dw_mine, ri)

    def body_3(c_ref, a_ref, b_ref, o_ref):
        o_ref[...] = (a_ref[...] + b_ref[...]).astype(BF16)

    sum_3 = pl.pallas_call(
        body_3, name="sum_pair_w3",
        grid_spec=pltpu.PrefetchScalarGridSpec(
            num_scalar_prefetch=1, grid=(3, N_SHARD),
            in_specs=[pl.BlockSpec((None, None, None, Do, D), lambda a, t, c: (a, t, c[0], 0, 0)),
                      pl.BlockSpec((None, None, Do, D), lambda a, t, c: (a, t, 0, 0))],
            out_specs=pl.BlockSpec((None, None, Do, D), lambda a, t, c: (a, t, 0, 0))),
        out_shape=jax.ShapeDtypeStruct((3, N_SHARD, Do, D), BF16),
        compiler_params=_cparams(("parallel", "parallel")),
    )(cidx, dw3, r3)
    return sum_i, sum_3


def _chips_exchange(cs_in, cs_3):
    _, Dh, Wc = cs_in.shape
    _, _, Do, D = cs_3.shape

    def build(ins, outs, send, recv):
        pos = _position()
        cps = []
        for j, k in enumerate(CHIP_FLIPS):
            to = _peer(pos, k)
            t = _shard_of(to)
            cps.append(_remote(ins[0].at[t], outs[0].at[j], send, recv, j, to))
            cps.append(_remote(ins[1].at[:, t], outs[1].at[j], send, recv, 3 + j, to))
        return cps

    shapes = (jax.ShapeDtypeStruct((3, Dh, Wc), BF16), jax.ShapeDtypeStruct((3, 3, Do, D), BF16))
    return _Exchange((cs_in, cs_3), shapes, 6, build)


def _sum_chips(csidx, cs_in, rb_in, cs_3, rb_3):
    _, Dh, Wc = cs_in.shape
    _, _, Do, D = cs_3.shape
    tr = min(256, Dh)

    def body_i(s_ref, a_ref, b_ref, o_ref):
        acc = a_ref[...].astype(F32)
        for j in range(3):
            acc = acc + b_ref[j].astype(F32)
        o_ref[...] = acc

    g_in = pl.pallas_call(
        body_i, name="sum_chips_in",
        grid_spec=pltpu.PrefetchScalarGridSpec(
            num_scalar_prefetch=1, grid=(Dh // tr,),
            in_specs=[pl.BlockSpec((None, tr, Wc), lambda i, s: (s[1], i, 0)),
                      pl.BlockSpec((3, tr, Wc), lambda i, s: (0, i, 0))],
            out_specs=pl.BlockSpec((None, tr, Wc), lambda i, s: (s[0], i, 0))),
        out_shape=jax.ShapeDtypeStruct((2, Dh, Wc), F32),
        compiler_params=_cparams(("parallel",)),
    )(csidx, cs_in, rb_in)

    def body_3(s_ref, a_ref, b_ref, o_ref):
        acc = a_ref[...].astype(F32)
        for j in range(3):
            acc = acc + b_ref[j].astype(F32)
        o_ref[...] = acc

    g_3 = pl.pallas_call(
        body_3, name="sum_chips_w3",
        grid_spec=pltpu.PrefetchScalarGridSpec(
            num_scalar_prefetch=1, grid=(3,),
            in_specs=[pl.BlockSpec((None, None, Do, D), lambda a, s: (a, s[1], 0, 0)),
                      pl.BlockSpec((3, None, Do, D), lambda a, s: (0, a, 0, 0))],
            out_specs=pl.BlockSpec((None, None, Do, D), lambda a, s: (a, s[0], 0, 0))),
        out_shape=jax.ShapeDtypeStruct((3, 2, Do, D), F32),
        compiler_params=_cparams(("parallel",)),
    )(csidx, cs_3, rb_3)
    return g_in, g_3


def _rs_final(g_in, g_3):
    def body(hi_ref, h3_ref, gi_ref, g3_ref, send, recv):
        pos = _position()
        c = pos[2]
        sib = _peer(pos, 1)
        cps = [_remote(hi_ref.at[c], gi_ref.at[c], send, recv, 0, sib),
               _remote(h3_ref.at[:, c], g3_ref.at[:, c], send, recv, 1, sib)]
        for cp in cps:
            cp.start()
        _remote(hi_ref.at[1 - c], gi_ref.at[1 - c], send, recv, 0, sib).wait_recv()
        _remote(h3_ref.at[:, 1 - c], g3_ref.at[:, 1 - c], send, recv, 1, sib).wait_recv()
        for cp in cps:
            cp.wait_send()

    return pl.pallas_call(
        body, name="rs_final", in_specs=[ANY, ANY], out_specs=[ANY, ANY],
        out_shape=[jax.ShapeDtypeStruct(g_in.shape, F32), jax.ShapeDtypeStruct(g_3.shape, F32)],
        input_output_aliases={0: 0, 1: 1},
        scratch_shapes=[pltpu.SemaphoreType.DMA((2,)), pltpu.SemaphoreType.DMA((2,))],
    )(g_in, g_3)


def _adam_math(w, g, m, v):
    m = ADAM_B1 * m + (1.0 - ADAM_B1) * g
    v = ADAM_B2 * v + (1.0 - ADAM_B2) * (g * g)
    m_hat = m / (1.0 - ADAM_B1 ** ADAM_STEP)
    v_hat = v / (1.0 - ADAM_B2 ** ADAM_STEP)
    delta = -ADAM_LR * (m_hat / (jnp.sqrt(v_hat) + ADAM_EPS) + ADAM_WD * w)
    return delta, m, v


def _adamw(w, g, m, v, name):
    R, C = w.shape
    tr = min(128, R)

    def body(w_ref, g_ref, m_ref, v_ref, d_ref, nm_ref, nv_ref):
        d_ref[...], nm_ref[...], nv_ref[...] = _adam_math(w_ref[...], g_ref[...], m_ref[...], v_ref[...])

    blk = pl.BlockSpec((tr, C), lambda i: (i, 0))
    return pl.pallas_call(
        body, name=name, grid=(R // tr,), in_specs=[blk] * 4, out_specs=[blk] * 3,
        out_shape=[jax.ShapeDtypeStruct((R, C), F32)] * 3,
        compiler_params=_cparams(("parallel",), VMEM_LIMIT),
    )(w, g, m, v)


def _bwd_small(vec, act, ada_w, cctx8, w_sm, m_sm, v_sm, w_ab, m_ab, v_ab, w_cw, m_cw, v_cw, w_dl, m_dl, v_dl):
    D = vec.shape[1]
    Wm = ada_w.shape[1]
    Dq = w_cw.shape[1]

    def body(vec_ref, act_ref, aw_ref, cc_ref, wsm, msm, vsm, wab, mab, vab, wcw, mcw, vcw, wdl, mdl, vdl,
             gaw_ref, loss_ref, o_sm, o_ab, o_cw, o_dl,
             vbuf, dm, dm_sh, gcw, amine, abuf, s_v, r_v, s_a, r_a):
        pos = _position()
        me, s = _dev_id(pos), _shard_of(pos)
        vbuf[me] = vec_ref[...]
        sends = [_remote(vec_ref, vbuf.at[me], s_v, r_v, k - 1, _peer(pos, k)) for k in range(1, 8)]
        for cp in sends:
            cp.start()
        for k in range(1, 8):
            _remote(vec_ref, vbuf.at[_dev_id(_peer(pos, k))], s_v, r_v, k - 1, _peer(pos, k)).wait_recv()
        tot = vbuf[0]
        for d in range(1, N_DEV):
            tot = tot + vbuf[d]
        loss_ref[...] = jnp.zeros((8, 128), F32) + (0.5 / D) * _sum_all(tot[14:15, :])
        dm[...] = jnp.zeros_like(dm)
        for d in range(N_DEV):
            for r in range(3):
                dm[d:d + 1, r * D:(r + 1) * D] = vbuf[d, r:r + 1, :]
        dm[8:9, 0:D] = tot[3:4, :]
        dm[8:9, D:2 * D] = tot[4:5, :]
        for t in range(N_SHARD):
            @pl.when(s == t)
            def _(t=t):
                dm_sh[...] = dm[:, t * Wm:(t + 1) * Wm]
                gcw[...] = jnp.zeros_like(gcw)
                gcw[0:3, :] = tot[9:12, t * Dq:(t + 1) * Dq]
        gaw_ref[...] = lax.dot_general(act_ref[...], dm_sh[...], (((0,), (0,)), ((), ())),
                                       precision=lax.Precision.HIGHEST, preferred_element_type=F32)
        part = lax.dot_general(dm_sh[8:16, :], aw_ref[...], (((1,), (1,)), ((), ())),
                               precision=lax.Precision.HIGHEST, preferred_element_type=F32)
        amine[...] = part
        abuf[s] = part
        asend = [_remote(amine, abuf.at[s], s_a, r_a, j, _peer(pos, k)) for j, k in enumerate(CHIP_FLIPS)]
        for cp in asend:
            cp.start()
        for j, k in enumerate(CHIP_FLIPS):
            _remote(amine, abuf.at[_shard_of(_peer(pos, k))], s_a, r_a, j, _peer(pos, k)).wait_recv()
        da = abuf[0]
        for t in range(1, N_SHARD):
            da = da + abuf[t]
        cc = cc_ref[0:1, :]
        sg = _sigmoid(cc)
        g_cctx = da[0:1, :] * (sg * (1.0 + cc * (1.0 - sg)))

        def emit(o_ref, w, g, m, v):
            o_ref[0] = g
            o_ref[1], o_ref[2], o_ref[3] = _adam_math(w, g, m, v)

        g_sm = jnp.concatenate([g_cctx, tot[5:9, :], jnp.zeros((3, D), F32)], axis=0)
        emit(o_sm, wsm[...], g_sm, msm[...], vsm[...])
        g_ab = jnp.concatenate([tot[0:1, :] + tot[3:4, :], tot[1:2, :] + tot[4:5, :], tot[2:3, :],
                                jnp.zeros((5, D), F32)], axis=0)
        emit(o_ab, wab[...], g_ab, mab[...], vab[...])
        emit(o_cw, wcw[...], gcw[...], mcw[...], vcw[...])
        g_dl = jnp.concatenate([tot[12:14, 0:128] * _sigmoid(-wdl[0:2, :]), jnp.zeros((6, 128), F32)], axis=0)
        emit(o_dl, wdl[...], g_dl, mdl[...], vdl[...])
        for cp in sends + asend:
            cp.wait_send()

    return pl.pallas_call(
        body, name="bwd_small",
        in_specs=[VMEM_FULL] * 16, out_specs=[VMEM_FULL] * 6,
        out_shape=[jax.ShapeDtypeStruct((D, Wm), F32), jax.ShapeDtypeStruct((8, 128), F32),
                   jax.ShapeDtypeStruct((4, 8, D), F32), jax.ShapeDtypeStruct((4, 8, D), F32),
                   jax.ShapeDtypeStruct((4, 8, Dq), F32), jax.ShapeDtypeStruct((4, 8, 128), F32)],
        scratch_shapes=[pltpu.VMEM((N_DEV, 16, D), F32), pltpu.VMEM((16, 3 * D), F32), pltpu.VMEM((16, Wm), F32),
                        pltpu.VMEM((8, Dq), F32), pltpu.VMEM((8, D), F32), pltpu.VMEM((N_SHARD, 8, D), F32),
                        pltpu.SemaphoreType.DMA((7,)), pltpu.SemaphoreType.DMA((7,)),
                        pltpu.SemaphoreType.DMA((3,)), pltpu.SemaphoreType.DMA((3,))],
        compiler_params=_cparams(None, VMEM_LIMIT),
    )(vec, act, ada_w, cctx8, w_sm, m_sm, v_sm, w_ab, m_ab, v_ab, w_cw, m_cw, v_cw, w_dl, m_dl, v_dl)


def _pad_rows(a, rows=8):
    return jnp.pad(a, ((0, rows - a.shape[0]), (0, 0)))


def kernel(x, c, ctx, c_ctx, norm_w, ada_w, ada_b, w_in, conv_w, conv_b, decay_logit, gn_w, w_a, w_b, w_out, final_norm_w, loss_target, m_c_ctx, m_norm_w, m_ada_w, m_ada_b, m_w_in, m_conv_w, m_conv_b, m_decay_logit, m_gn_w, m_w_a, m_w_b, m_w_out, m_final_norm_w, v_c_ctx, v_norm_w, v_ada_w, v_ada_b, v_w_in, v_conv_w, v_conv_b, v_decay_logit, v_gn_w, v_w_a, v_w_b, v_w_out, v_final_norm_w):
    L, D = x.shape[1], x.shape[2]
    H = D // DV
    Wc = w_in.shape[2]
    Do = D // 8
    pos = _position()
    me = _dev_id(pos)
    cidx = jnp.reshape(pos[2], (1,)).astype(jnp.int32)
    sidx = jnp.reshape(_shard_of(pos), (1,)).astype(jnp.int32)

    act, mod, conv_w8 = _fwd_small(_pad_rows(c), _pad_rows(c_ctx[None]), ada_w[0], ada_b, _pad_rows(conv_w[0]))
    mod_x = lax.dynamic_slice_in_dim(mod, me, 1, axis=0).reshape(3, D)
    mod_c = mod[8].reshape(3, D)
    lg = jax.nn.log_sigmoid(decay_logit[0])

    w3_s = jnp.stack([w_a[0], w_b[0], w_out[0]]).reshape(3, 2, Do, D)
    w_in_full, w3_full = _ag_weights(w_in[0], w3_s)
    w3_full = w3_full.reshape(3, D, D)

    groups, dret_c, xmt, cmt, dx1, sc_x, dw3, sts = _local_step(
        x[0], ctx[0], loss_target[0], mod_x, mod_c, norm_w, conv_w8, conv_b, lg, gn_w, final_norm_w[None],
        w_in_full, w3_full)
    st_mid, st_conv, st_gn, st_lg, st_lgc, st_c = sts

    dw3_5 = dw3.reshape(3, N_SHARD, 2, Do, D)
    (dw_other,) = _dw_in(1 - cidx, xmt, groups, cmt, dret_c, D, "dw_in_other")
    dw_mine, ra_in, ra_3 = _dw_in(cidx, xmt, groups, cmt, dret_c, D, "dw_in_mine", _pair_exchange(dw_other, dw3_5))
    cs_in, cs_3 = _sum_pair(cidx, dw_mine, ra_in, dw3_5, ra_3)
    grad_x, st_x, rb_in, rb_3 = _dxm(groups, 0, w_in_full, x[0], norm_w, sc_x, dx1, "dxm_x",
                                     _chips_exchange(cs_in, cs_3))
    gh_in, gh_3 = _sum_chips(jnp.concatenate([cidx, sidx]), cs_in, rb_in, cs_3, rb_3)
    g_in, g_3 = _rs_final(gh_in, gh_3)
    g_w_in = g_in.reshape(D, Wc)
    g_3 = g_3.reshape(3, D // 4, D)

    lanes = lambda a: jnp.pad(a, ((0, 0), (0, D - a.shape[1])))
    vec = jnp.concatenate([
        st_x[0:2], st_mid[1:2], st_c[0:2], st_x[2:3] + st_c[2:3], st_conv[3:4], st_gn[0:1], st_mid[0:1],
        st_conv[0:3], lanes(st_lg[0:2] + st_lgc[0:2]), st_mid[2:3], jnp.zeros((1, D), F32)], axis=0)
    small = lambda a, b_, c_, d_, e_: _pad_rows(jnp.concatenate([a[None], b_, c_, d_, e_[None]], axis=0))
    dl = lambda a: jnp.pad(a[0], ((0, 6), (0, 128 - H)))
    g_ada_w, loss_t, o_sm, o_ab, o_cw, o_dl = _bwd_small(
        vec, act, ada_w[0], _pad_rows(c_ctx[None]),
        small(c_ctx, norm_w, conv_b, gn_w, final_norm_w), small(m_c_ctx, m_norm_w, m_conv_b, m_gn_w, m_final_norm_w),
        small(v_c_ctx, v_norm_w, v_conv_b, v_gn_w, v_final_norm_w),
        _pad_rows(ada_b.reshape(3, D)), _pad_rows(m_ada_b.reshape(3, D)), _pad_rows(v_ada_b.reshape(3, D)),
        _pad_rows(conv_w[0]), _pad_rows(m_conv_w[0]), _pad_rows(v_conv_w[0]),
        dl(decay_logit), dl(m_decay_logit), dl(v_decay_logit))

    upd_in = _adamw(w_in[0], g_w_in, m_w_in[0], v_w_in[0], "adamw_w_in")
    upd_ada = _adamw(ada_w[0], g_ada_w, m_ada_w[0], v_ada_w[0], "adamw_ada_w")
    upd_a = _adamw(w_a[0], g_3[0], m_w_a[0], v_w_a[0], "adamw_w_a")
    upd_b = _adamw(w_b[0], g_3[1], m_w_b[0], v_w_b[0], "adamw_w_b")
    upd_o = _adamw(w_out[0], g_3[2], m_w_out[0], v_w_out[0], "adamw_w_out")

    def leaves(q):
        big = lambda g, upd: (g if q == 0 else upd[q - 1])[None]
        sm = o_sm[q]
        return [sm[0], sm[1:2], big(g_ada_w, upd_ada), o_ab[q][0:3].reshape(1, 3 * D), big(g_w_in, upd_in),
                o_cw[q][0:3][None], sm[2:3], o_dl[q][0:2, 0:H][None], sm[3:4],
                big(g_3[0], upd_a), big(g_3[1], upd_b), big(g_3[2], upd_o), sm[4]]

    loss = loss_t[0, 0]
    return (loss, grad_x[None], *leaves(0), *leaves(1), *leaves(2), *leaves(3))
```

```python
from typing import Callable, NamedTuple

import jax
import jax.numpy as jnp
from jax import lax
from jax.experimental import pallas as pl
from jax.experimental.pallas import tpu as pltpu

F32 = jnp.float32
BF16 = jnp.bfloat16
MESH = pl.DeviceIdType.MESH

CHUNK = 128
DV = 128
DK = 64
GRID_W = 64
ROPE_BASE = 10000.0
EPS = 1e-6
K_SCALE = DK ** -0.5
N_SHARD = 4
N_DEV = 8

ADAM_LR = 0.001
ADAM_B1 = 0.9
ADAM_B2 = 0.999
ADAM_EPS = 1e-08
ADAM_WD = 0.01
ADAM_STEP = 10

VMEM_LIMIT = 56 * 1024 * 1024


def _cparams(sem=None, vmem=None):
    kw = {}
    if sem is not None:
        kw["dimension_semantics"] = sem
    if vmem is not None:
        kw["vmem_limit_bytes"] = vmem
    return pltpu.CompilerParams(**kw)


def _dot(a, b):
    return jnp.dot(a, b, preferred_element_type=F32)


def _dot_nt(a, b):
    return lax.dot_general(a, b, (((1,), (1,)), ((), ())), preferred_element_type=F32)


def _dot_tn(a, b):
    return lax.dot_general(a, b, (((0,), (0,)), ((), ())), preferred_element_type=F32)


def _sigmoid(x):
    return 1.0 / (1.0 + jnp.exp(-x))


def _sum_all(x):
    return jnp.sum(jnp.sum(x, axis=1, keepdims=True), axis=0, keepdims=True)


def _swap_halves(t):
    n = t.shape[1]
    lane = lax.broadcasted_iota(jnp.int32, t.shape, 1)
    low = (lane & 32) == 0
    return jnp.where(low, pltpu.roll(t, n - 32, 1), pltpu.roll(t, 32, 1))


def _vec_spec(d):
    return pl.BlockSpec((1, d), lambda *a: (0, 0))


def _norm_mod(x, nw, sc, sh, name):
    L, D = x.shape
    tl = min(256, L)

    def body(x_ref, nw_ref, sc_ref, sh_ref, xm_ref, xmt_ref):
        xv = x_ref[...]
        r = lax.rsqrt(jnp.mean(xv * xv, axis=-1, keepdims=True) + EPS)
        xm = (xv * r * nw_ref[...]) * (1.0 + sc_ref[...]) + sh_ref[...]
        xm_ref[...] = xm.astype(BF16)
        xmt_ref[...] = xm.T.astype(BF16)

    return pl.pallas_call(
        body, name=name, grid=(L // tl,),
        in_specs=[pl.BlockSpec((tl, D), lambda i: (i, 0)), _vec_spec(D), _vec_spec(D), _vec_spec(D)],
        out_specs=[pl.BlockSpec((tl, D), lambda i: (i, 0)), pl.BlockSpec((D, tl), lambda i: (0, i))],
        out_shape=[jax.ShapeDtypeStruct((L, D), BF16), jax.ShapeDtypeStruct((D, L), BF16)],
        compiler_params=_cparams(("parallel",)),
    )(x, nw, sc, sh)


def _in_proj(xm, w, name):
    M, D = xm.shape
    N = w.shape[1]
    tm = min(1024, M)

    def body(a_ref, b_ref, o_ref, qk_ref):
        acc = _dot(a_ref[...], b_ref[...])
        o_ref[...] = acc.astype(o_ref.dtype)

        @pl.when(pl.program_id(1) == 4)
        def _():
            qk_ref[...] = acc

    return pl.pallas_call(
        body, name=name, grid=(M // tm, N // D),
        in_specs=[pl.BlockSpec((tm, D), lambda i, j: (i, 0)), pl.BlockSpec((D, D), lambda i, j: (0, j))],
        out_specs=[pl.BlockSpec((tm, D), lambda i, j: (i, j)), pl.BlockSpec((tm, D), lambda i, j: (i, 0))],
        out_shape=[jax.ShapeDtypeStruct((M, N), BF16), jax.ShapeDtypeStruct((M, D), F32)],
        compiler_params=_cparams(("parallel", "arbitrary")),
    )(xm, w)


def _halo_specs(tl, L, D, col):
    hb = tl // 16
    last = L // 16 - 1
    prev = pl.BlockSpec((16, D), lambda i: (jnp.maximum(i * hb - 1, 0), col))
    nxt = pl.BlockSpec((16, D), lambda i: (jnp.minimum((i + 1) * hb, last), col))
    return prev, nxt


def _shift_rows(u, above, below):
    tl = u.shape[0]
    row = lax.broadcasted_iota(jnp.int32, u.shape, 0)
    dn = jnp.where(row == 0, above, pltpu.roll(u, 1, 0))
    up = jnp.where(row == tl - 1, below, pltpu.roll(u, tl - 1, 0))
    return dn, up


def _conv_gate_fwd(p, conv_w, conv_b, D):
    L = p.shape[0]
    tl = min(256, L)
    nt = L // tl

    def body(h_ref, bg_ref, cg_ref, za_ref, hp_ref, hn_ref, cp_ref, cn_ref, w_ref, b_ref, o_ref):
        i = pl.program_id(0)
        u = cg_ref[...].astype(F32) * h_ref[...].astype(F32)
        above = cp_ref[15:16, :].astype(F32) * hp_ref[15:16, :].astype(F32)
        below = cn_ref[0:1, :].astype(F32) * hn_ref[0:1, :].astype(F32)
        above = jnp.where(i == 0, 0.0, above)
        below = jnp.where(i == nt - 1, 0.0, below)
        dn, up = _shift_rows(u, above, below)
        co = w_ref[0:1, :] * dn + w_ref[1:2, :] * u + w_ref[2:3, :] * up + b_ref[...]
        za = za_ref[...].astype(F32)
        o_ref[...] = (za * _sigmoid(za) * bg_ref[...].astype(F32) * co).astype(BF16)

    main = lambda col: pl.BlockSpec((tl, D), lambda i: (i, col))
    hp, hn = _halo_specs(tl, L, D, 0)
    cp, cn = _halo_specs(tl, L, D, 2)
    return pl.pallas_call(
        body, name="conv_gate_fwd", grid=(nt,),
        in_specs=[main(0), main(1), main(2), main(3), hp, hn, cp, cn,
                  pl.BlockSpec((8, D), lambda i: (0, 0)), _vec_spec(D)],
        out_specs=pl.BlockSpec((tl, D), lambda i: (i, 0)),
        out_shape=jax.ShapeDtypeStruct((L, D), BF16),
        compiler_params=_cparams(("parallel",)),
    )(p, p, p, p, p, p, p, p, conv_w, conv_b)


def _rope_tables(L, D):
    pos = jnp.arange(L)
    row = (pos // GRID_W).astype(F32)
    col = (pos % GRID_W).astype(F32)
    nf = DK // 4
    inv = ROPE_BASE ** (-jnp.arange(nf, dtype=F32) / nf)
    ang = jnp.concatenate([row[:, None] * inv, col[:, None] * inv], axis=-1)
    cos, sin = jnp.cos(ang), jnp.sin(ang)
    heads = D // DV
    c2 = jnp.tile(jnp.concatenate([cos, cos], axis=-1), (1, heads))
    s2 = jnp.tile(jnp.concatenate([-sin, sin], axis=-1), (1, heads))
    return c2, s2


def _rope_fwd(pqk, c2, s2, D):
    L = pqk.shape[0]
    W = D // 2
    tl = min(256, L)

    def body(q_ref, k_ref, c_ref, s_ref, qo_ref, ko_ref):
        c, s = c_ref[...], s_ref[...]
        q = q_ref[...].astype(F32)
        k = k_ref[...].astype(F32) * K_SCALE
        qo_ref[...] = (q * c + _swap_halves(q) * s).astype(BF16)
        ko_ref[...] = (k * c + _swap_halves(k) * s).astype(BF16)

    blk = lambda col: pl.BlockSpec((tl, W), lambda i: (i, col))
    return pl.pallas_call(
        body, name="rope_fwd", grid=(L // tl,),
        in_specs=[blk(0), blk(1), blk(0), blk(0)],
        out_specs=[blk(0), blk(0)],
        out_shape=[jax.ShapeDtypeStruct((L, W), BF16)] * 2,
        compiler_params=_cparams(("parallel",)),
    )(pqk, pqk, c2, s2)


def _smem_spec():
    return pl.BlockSpec(memory_space=pltpu.SMEM)


def _pair_select(e0, e1):
    row = lax.broadcasted_iota(jnp.int32, e0.shape, 0)
    return jnp.where(row < DK, e0, e1)


def _head_lane_mask(shape, e):
    lane = lax.broadcasted_iota(jnp.int32, shape, 1)
    return (lane < DK) if e == 0 else (lane >= DK)


def _ctx_states(pc, pqk_c, lg, D):
    Lc = pc.shape[0]
    H = D // DV

    def body(lg_ref, k_ref, v_ref, s_ref):
        m = lax.broadcasted_iota(jnp.int32, (Lc, DV), 0).astype(F32)
        for pr in range(H // 2):
            k2 = k_ref[:, pr * 128:(pr + 1) * 128].astype(F32) * K_SCALE
            res = [[None, None], [None, None]]
            for e in range(2):
                h = 2 * pr + e
                v = v_ref[:, h * DV:(h + 1) * DV]
                dec_f = jnp.exp(lg_ref[0, h] * (Lc - 1.0 - m))
                dec_b = jnp.exp(lg_ref[1, h] * m)
                res[0][e] = _dot_tn((k2 * dec_f).astype(BF16), v)
                res[1][e] = _dot_tn((k2 * dec_b).astype(BF16), v)
            s_ref[0, pr] = _pair_select(res[0][0], res[0][1])
            s_ref[1, pr] = _pair_select(res[1][0], res[1][1])

    return pl.pallas_call(
        body, name="ctx_states", grid=(1,),
        in_specs=[_smem_spec(), pl.BlockSpec((Lc, D // 2), lambda i: (0, 1)), pl.BlockSpec((Lc, D), lambda i: (0, 5))],
        out_specs=pl.BlockSpec((2, H // 2, 128, 128), lambda i: (0, 0, 0, 0)),
        out_shape=jax.ShapeDtypeStruct((2, H // 2, 128, 128), F32),
    )(lg, pqk_c, pc)


def _ret_states(kr, p, s0, lg, D):
    L = kr.shape[0]
    H = D // DV
    N = L // CHUNK
    HP = H // 2

    def body(lg_ref, kf_ref, kb_ref, vf_ref, vb_ref, s0_ref, sf_out, sb_out, sf, sb):
        n = pl.program_id(0)

        @pl.when(n == 0)
        def _():
            sf[...] = s0_ref[0]
            sb[...] = s0_ref[1]

        sf_out[0] = sf[...]
        sb_out[0] = sb[...]
        j = lax.broadcasted_iota(jnp.int32, (CHUNK, 128), 0).astype(F32)
        full = jnp.full((128, 128), float(CHUNK), F32)
        for pr in range(HP):
            kf2 = kf_ref[:, pr * 128:(pr + 1) * 128].astype(F32)
            kb2 = kb_ref[:, pr * 128:(pr + 1) * 128].astype(F32)
            inc_f, inc_b, gf, gb = [], [], [], []
            for e in range(2):
                h = 2 * pr + e
                lgf, lgb = lg_ref[0, h], lg_ref[1, h]
                inc_f.append(_dot_tn((kf2 * jnp.exp(lgf * (CHUNK - 1.0 - j))).astype(BF16), vf_ref[:, h * DV:(h + 1) * DV]))
                inc_b.append(_dot_tn((kb2 * jnp.exp(lgb * j)).astype(BF16), vb_ref[:, h * DV:(h + 1) * DV]))
                gf.append(jnp.exp(lgf * full))
                gb.append(jnp.exp(lgb * full))
            sf[pr] = _pair_select(gf[0], gf[1]) * sf[pr] + _pair_select(inc_f[0], inc_f[1])
            sb[pr] = _pair_select(gb[0], gb[1]) * sb[pr] + _pair_select(inc_b[0], inc_b[1])

    st = jax.ShapeDtypeStruct((N, HP, 128, 128), F32)
    return pl.pallas_call(
        body, name="ret_states", grid=(N,),
        in_specs=[_smem_spec(),
                  pl.BlockSpec((CHUNK, D // 2), lambda n: (n, 0)),
                  pl.BlockSpec((CHUNK, D // 2), lambda n: (N - 1 - n, 0)),
                  pl.BlockSpec((CHUNK, D), lambda n: (n, 5)),
                  pl.BlockSpec((CHUNK, D), lambda n: (N - 1 - n, 5)),
                  pl.BlockSpec((2, HP, 128, 128), lambda n: (0, 0, 0, 0))],
        out_specs=[pl.BlockSpec((1, HP, 128, 128), lambda n: (n, 0, 0, 0)),
                   pl.BlockSpec((1, HP, 128, 128), lambda n: (N - 1 - n, 0, 0, 0))],
        out_shape=[st, st],
        scratch_shapes=[pltpu.VMEM((HP, 128, 128), F32), pltpu.VMEM((HP, 128, 128), F32)],
        compiler_params=_cparams(("arbitrary",)),
    )(lg, kr, kr, p, p, s0)


def _decay_masks(lgf, lgb):
    i = lax.broadcasted_iota(jnp.int32, (CHUNK, CHUNK), 0).astype(F32)
    j = lax.broadcasted_iota(jnp.int32, (CHUNK, CHUNK), 1).astype(F32)
    d = i - j
    mf = jnp.where(d > 0, jnp.exp(lgf * jnp.maximum(d, 0.0)), 0.0)
    mb = jnp.where(d < 0, jnp.exp(lgb * jnp.maximum(-d, 0.0)), 0.0)
    m = mf + mb + jnp.where(d == 0, 2.0, 0.0)
    return m, mf * d, mb * (-d)


def _ret_out(qr, kr, p, sf_prev, sb_prev, gn_w, lg, D):
    L = qr.shape[0]
    H = D // DV
    N = L // CHUNK
    HP = H // 2

    def body(lg_ref, q_ref, k_ref, v_ref, zb_ref, sf_ref, sb_ref, gn_ref, o_ref, yb_ref):
        i = lax.broadcasted_iota(jnp.int32, (CHUNK, 128), 0).astype(F32)
        for pr in range(HP):
            q2 = q_ref[:, pr * 128:(pr + 1) * 128]
            k2 = k_ref[:, pr * 128:(pr + 1) * 128]
            sfp = sf_ref[0, pr].astype(BF16)
            sbp = sb_ref[0, pr].astype(BF16)
            for e in range(2):
                h = 2 * pr + e
                sl = slice(h * DV, (h + 1) * DV)
                lgf, lgb = lg_ref[0, h], lg_ref[1, h]
                qm = jnp.where(_head_lane_mask(q2.shape, e), q2, jnp.zeros_like(q2))
                m, _, _ = _decay_masks(lgf, lgb)
                a = (_dot_nt(qm, k2) * m).astype(BF16)
                qf = qm.astype(F32)
                o = _dot(a, v_ref[:, sl])
                o += _dot((qf * jnp.exp(lgf * (i + 1.0))).astype(BF16), sfp)
                o += _dot((qf * jnp.exp(lgb * (CHUNK - i))).astype(BF16), sbp)
                o_ref[:, sl] = o
                mu = jnp.mean(o, axis=-1, keepdims=True)
                oc = o - mu
                rstd = lax.rsqrt(jnp.mean(oc * oc, axis=-1, keepdims=True) + EPS)
                zb = zb_ref[:, sl].astype(F32)
                yb_ref[:, sl] = (zb * _sigmoid(zb) * (oc * rstd * gn_ref[:, sl])).astype(BF16)

    return pl.pallas_call(
        body, name="ret_out", grid=(N,),
        in_specs=[_smem_spec(),
                  pl.BlockSpec((CHUNK, D // 2), lambda n: (n, 0)),
                  pl.BlockSpec((CHUNK, D // 2), lambda n: (n, 0)),
                  pl.BlockSpec((CHUNK, D), lambda n: (n, 5)),
                  pl.BlockSpec((CHUNK, D), lambda n: (n, 6)),
                  pl.BlockSpec((1, HP, 128, 128), lambda n: (n, 0, 0, 0)),
                  pl.BlockSpec((1, HP, 128, 128), lambda n: (n, 0, 0, 0)),
                  _vec_spec(D)],
        out_specs=[pl.BlockSpec((CHUNK, D), lambda n: (n, 0)), pl.BlockSpec((CHUNK, D), lambda n: (n, 0))],
        out_shape=[jax.ShapeDtypeStruct((L, D), F32), jax.ShapeDtypeStruct((L, D), BF16)],
        compiler_params=_cparams(("parallel",)),
    )(lg, qr, kr, p, p, sf_prev, sb_prev, gn_w)


def _mid(ya, yb, p, x, tgt, w3, g, fw, D):
    L = x.shape[0]
    tm = min(256, L)
    nt = L // tm

    def body(ya_ref, yb_ref, ga_ref, gb_ref, x_ref, t_ref, w_hbm, g_ref, fw_ref,
             dx1_ref, dya_ref, dyb_ref, dgab_ref, dw_hbm, st_ref, w_vm, dw_acc, sem):
        i = pl.program_id(0)

        @pl.when(i == 0)
        def _():
            cp = pltpu.make_async_copy(w_hbm, w_vm, sem)
            cp.start()
            dw_acc[...] = jnp.zeros_like(dw_acc)
            st_ref[...] = jnp.zeros_like(st_ref)
            cp.wait()

        ya_b, yb_b = ya_ref[...], yb_ref[...]
        y_a = _dot(ya_b, w_vm[0])
        y_b = _dot(yb_b, w_vm[1])
        sga = _sigmoid(ga_ref[...].astype(F32))
        sgb = _sigmoid(gb_ref[...].astype(F32))
        mix_b = (sga * y_a + sgb * y_b).astype(BF16)
        y_x = _dot(mix_b, w_vm[2])
        gvec, fwv = g_ref[...], fw_ref[...]
        x1 = x_ref[...] + gvec * y_x
        r1 = lax.rsqrt(jnp.mean(x1 * x1, axis=-1, keepdims=True) + EPS)
        xh = x1 * r1
        diff = xh * fwv - t_ref[...]
        dout = diff * (1.0 / D)
        dxh = dout * fwv
        dx1 = r1 * (dxh - xh * jnp.mean(dxh * xh, axis=-1, keepdims=True))
        dx1_ref[...] = dx1
        st_ref[0:1, :] += jnp.sum(dout * xh, axis=0, keepdims=True)
        st_ref[1:2, :] += jnp.sum(dx1 * y_x, axis=0, keepdims=True)
        st_ref[2:3, :] += jnp.sum(diff * diff, axis=0, keepdims=True)
        dyx_b = (dx1 * gvec).astype(BF16)
        dmix = _dot_nt(dyx_b, w_vm[2])
        dw_acc[2] += _dot_tn(mix_b, dyx_b)
        dya_b = (dmix * sga).astype(BF16)
        dyb_b = (dmix * sgb).astype(BF16)
        dgab_ref[:, 0:D] = (dmix * y_a * sga * (1.0 - sga)).astype(BF16)
        dgab_ref[:, D:2 * D] = (dmix * y_b * sgb * (1.0 - sgb)).astype(BF16)
        dya_ref[...] = _dot_nt(dya_b, w_vm[0])
        dyb_ref[...] = _dot_nt(dyb_b, w_vm[1])
        dw_acc[0] += _dot_tn(ya_b, dya_b)
        dw_acc[1] += _dot_tn(yb_b, dyb_b)

        @pl.when(i == nt - 1)
        def _():
            out = pltpu.make_async_copy(dw_acc, dw_hbm, sem)
            out.start()
            out.wait()

    row = lambda col: pl.BlockSpec((tm, D), lambda i: (i, col))
    any_spec = pl.BlockSpec(memory_space=pl.ANY)
    f32o = jax.ShapeDtypeStruct((L, D), F32)
    return pl.pallas_call(
        body, name="mid", grid=(nt,),
        in_specs=[row(0), row(0), row(7), row(8), row(0), row(0), any_spec, _vec_spec(D), _vec_spec(D)],
        out_specs=[row(0), row(0), row(0), pl.BlockSpec((tm, 2 * D), lambda i: (i, 0)), any_spec,
                   pl.BlockSpec((8, D), lambda i: (0, 0))],
        out_shape=[f32o, f32o, f32o, jax.ShapeDtypeStruct((L, 2 * D), BF16),
                   jax.ShapeDtypeStruct((3, D, D), F32), jax.ShapeDtypeStruct((8, D), F32)],
        scratch_shapes=[pltpu.VMEM((3, D, D), BF16), pltpu.VMEM((3, D, D), F32), pltpu.SemaphoreType.DMA],
        compiler_params=_cparams(("arbitrary",), VMEM_LIMIT),
    )(ya, yb, p, p, x, tgt, w3, g, fw)


def _conv_bwd(dya, p, conv_w, conv_b, D, exchange=None):
    L = p.shape[0]
    tl = min(256, L)
    nt = L // tl

    def body(d_ref, h_ref, bg_ref, cg_ref, za_ref,
             dp_ref, dn_ref, hp_ref, hn_ref, bp_ref, bn_ref, cp_ref, cn_ref, zp_ref, zn_ref,
             w_ref, b_ref, dc_ref, st_ref):
        i = pl.program_id(0)

        @pl.when(i == 0)
        def _():
            st_ref[...] = jnp.zeros_like(st_ref)

        first, last = i == 0, i == nt - 1
        h = h_ref[...].astype(F32)
        cg = cg_ref[...].astype(F32)
        bg = bg_ref[...].astype(F32)
        za = za_ref[...].astype(F32)
        dy = d_ref[...].astype(F32)
        u = cg * h
        u_above = jnp.where(first, 0.0, cp_ref[15:16, :].astype(F32) * hp_ref[15:16, :].astype(F32))
        u_below = jnp.where(last, 0.0, cn_ref[0:1, :].astype(F32) * hn_ref[0:1, :].astype(F32))
        u_dn, u_up = _shift_rows(u, u_above, u_below)
        w0, w1, w2 = w_ref[0:1, :], w_ref[1:2, :], w_ref[2:3, :]
        co = w0 * u_dn + w1 * u + w2 * u_up + b_ref[...]
        sz = _sigmoid(za)
        silu = za * sz
        dc_ref[:, 3 * D:4 * D] = (dy * bg * co * (sz * (1.0 + za * (1.0 - sz)))).astype(BF16)
        dc_ref[:, D:2 * D] = (dy * silu * co).astype(BF16)
        dco = dy * silu * bg

        def edge(dr, zr, br, r):
            z = zr[r:r + 1, :].astype(F32)
            return dr[r:r + 1, :].astype(F32) * (z * _sigmoid(z)) * br[r:r + 1, :].astype(F32)

        dco_above = jnp.where(first, 0.0, edge(dp_ref, zp_ref, bp_ref, 15))
        dco_below = jnp.where(last, 0.0, edge(dn_ref, zn_ref, bn_ref, 0))
        dco_dn, dco_up = _shift_rows(dco, dco_above, dco_below)
        du = w0 * dco_up + w1 * dco + w2 * dco_dn
        dc_ref[:, 2 * D:3 * D] = (du * h).astype(BF16)
        dc_ref[:, 0:D] = (du * cg).astype(BF16)
        st_ref[0:1, :] += jnp.sum(dco * u_dn, axis=0, keepdims=True)
        st_ref[1:2, :] += jnp.sum(dco * u, axis=0, keepdims=True)
        st_ref[2:3, :] += jnp.sum(dco * u_up, axis=0, keepdims=True)
        st_ref[3:4, :] += jnp.sum(dco, axis=0, keepdims=True)

    main = lambda col: pl.BlockSpec((tl, D), lambda i: (i, col))
    halos = []
    for col in (0, 0, 1, 2, 3):
        halos.extend(_halo_specs(tl, L, D, col))
    return _riding_call(
        body, exchange, nt, name="conv_bwd",
        args=(dya, p, p, p, p, dya, dya, p, p, p, p, p, p, p, p, conv_w, conv_b),
        in_specs=[main(0), main(0), main(1), main(2), main(3)] + halos
                 + [pl.BlockSpec((8, D), lambda i: (0, 0)), _vec_spec(D)],
        out_specs=[pl.BlockSpec((tl, 4 * D), lambda i: (i, 0)), pl.BlockSpec((8, D), lambda i: (0, 0))],
        out_shape=[jax.ShapeDtypeStruct((L, 4 * D), BF16), jax.ShapeDtypeStruct((8, D), F32)],
        cparams=_cparams(("arbitrary",)))


def _ret_bwd_pre(dyb, p, o, gn_w, D):
    L = o.shape[0]
    H = D // DV
    tl = min(256, L)

    def body(d_ref, zb_ref, o_ref, gn_ref, do_ref, dzb_ref, st_ref):
        @pl.when(pl.program_id(0) == 0)
        def _():
            st_ref[...] = jnp.zeros_like(st_ref)

        for h in range(H):
            sl = slice(h * DV, (h + 1) * DV)
            ov = o_ref[:, sl]
            mu = jnp.mean(ov, axis=-1, keepdims=True)
            oc = ov - mu
            rstd = lax.rsqrt(jnp.mean(oc * oc, axis=-1, keepdims=True) + EPS)
            rn = oc * rstd
            gw = gn_ref[:, sl]
            zb = zb_ref[:, sl].astype(F32)
            sz = _sigmoid(zb)
            dy = d_ref[:, sl].astype(F32)
            dzb_ref[:, sl] = (dy * (rn * gw) * (sz * (1.0 + zb * (1.0 - sz)))).astype(BF16)
            dretn = dy * (zb * sz)
            st_ref[0:1, sl] += jnp.sum(dretn * rn, axis=0, keepdims=True)
            drn = dretn * gw
            do = rstd * (drn - jnp.mean(drn, axis=-1, keepdims=True)
                         - rn * jnp.mean(drn * rn, axis=-1, keepdims=True))
            do_ref[:, sl] = do.astype(BF16)

    main = lambda col: pl.BlockSpec((tl, D), lambda i: (i, col))
    bfo = jax.ShapeDtypeStruct((L, D), BF16)
    return pl.pallas_call(
        body, name="ret_bwd_pre", grid=(L // tl,),
        in_specs=[main(0), main(6), main(0), _vec_spec(D)],
        out_specs=[main(0), main(0), pl.BlockSpec((8, D), lambda i: (0, 0))],
        out_shape=[bfo, bfo, jax.ShapeDtypeStruct((8, D), F32)],
        compiler_params=_cparams(("arbitrary",)),
    )(dyb, p, o, gn_w)


def _ret_bwd_states(qr, do, lg, D):
    L = qr.shape[0]
    H = D // DV
    N = L // CHUNK
    HP = H // 2

    def body(lg_ref, qf_ref, qb_ref, dof_ref, dob_ref, dsf_out, dsb_out, ds0_out, dsf, dsb):
        n = pl.program_id(0)

        @pl.when(n == 0)
        def _():
            dsf[...] = jnp.zeros_like(dsf)
            dsb[...] = jnp.zeros_like(dsb)

        dsf_out[0] = dsf[...]
        dsb_out[0] = dsb[...]
        i = lax.broadcasted_iota(jnp.int32, (CHUNK, 128), 0).astype(F32)
        full = jnp.full((128, 128), float(CHUNK), F32)
        for pr in range(HP):
            qf2 = qf_ref[:, pr * 128:(pr + 1) * 128].astype(F32)
            qb2 = qb_ref[:, pr * 128:(pr + 1) * 128].astype(F32)
            inc_f, inc_b, gf, gb = [], [], [], []
            for e in range(2):
                h = 2 * pr + e
                lgf, lgb = lg_ref[0, h], lg_ref[1, h]
                inc_f.append(_dot_tn((qf2 * jnp.exp(lgf * (i + 1.0))).astype(BF16), dof_ref[:, h * DV:(h + 1) * DV]))
                inc_b.append(_dot_tn((qb2 * jnp.exp(lgb * (CHUNK - i))).astype(BF16), dob_ref[:, h * DV:(h + 1) * DV]))
                gf.append(jnp.exp(lgf * full))
                gb.append(jnp.exp(lgb * full))
            dsf[pr] = _pair_select(gf[0], gf[1]) * dsf[pr] + _pair_select(inc_f[0], inc_f[1])
            dsb[pr] = _pair_select(gb[0], gb[1]) * dsb[pr] + _pair_select(inc_b[0], inc_b[1])

        @pl.when(n == N - 1)
        def _():
            ds0_out[0] = dsf[...]
            ds0_out[1] = dsb[...]

    st = jax.ShapeDtypeStruct((N, HP, 128, 128), F32)
    return pl.pallas_call(
        body, name="ret_bwd_states", grid=(N,),
        in_specs=[_smem_spec(),
                  pl.BlockSpec((CHUNK, D // 2), lambda n: (N - 1 - n, 0)),
                  pl.BlockSpec((CHUNK, D // 2), lambda n: (n, 0)),
                  pl.BlockSpec((CHUNK, D), lambda n: (N - 1 - n, 0)),
                  pl.BlockSpec((CHUNK, D), lambda n: (n, 0))],
        out_specs=[pl.BlockSpec((1, HP, 128, 128), lambda n: (N - 1 - n, 0, 0, 0)),
                   pl.BlockSpec((1, HP, 128, 128), lambda n: (n, 0, 0, 0)),
                   pl.BlockSpec((2, HP, 128, 128), lambda n: (0, 0, 0, 0))],
        out_shape=[st, st, jax.ShapeDtypeStruct((2, HP, 128, 128), F32)],
        scratch_shapes=[pltpu.VMEM((HP, 128, 128), F32), pltpu.VMEM((HP, 128, 128), F32)],
        compiler_params=_cparams(("arbitrary",)),
    )(lg, qr, qr, do, do)


def _ret_bwd_main(qr, kr, p, do, sf_prev, sb_prev, dsf, dsb, c2, s2, lg, D, exchange=None):
    L = qr.shape[0]
    H = D // DV
    N = L // CHUNK
    HP = H // 2
    W = D // 2

    def body(lg_ref, q_ref, k_ref, v_ref, do_ref, sf_ref, sb_ref, dsf_ref, dsb_ref, c_ref, s_ref,
             dr_ref, st_ref):
        @pl.when(pl.program_id(0) == 0)
        def _():
            st_ref[...] = jnp.zeros_like(st_ref)

        dqk_ref = dr_ref.at[:, 0:D]
        dv_ref = dr_ref.at[:, D:2 * D]
        i = lax.broadcasted_iota(jnp.int32, (CHUNK, 128), 0).astype(F32)
        lane = lax.broadcasted_iota(jnp.int32, (1, 128), 1)
        rowid = lax.broadcasted_iota(jnp.int32, (128, 128), 0)
        full = jnp.full((1, 1), float(CHUNK), F32)
        acc_f = jnp.zeros((1, 128), F32)
        acc_b = jnp.zeros((1, 128), F32)
        for pr in range(HP):
            ps = slice(pr * 128, (pr + 1) * 128)
            q2, k2 = q_ref[:, ps], k_ref[:, ps]
            sf32, sb32 = sf_ref[0, pr], sb_ref[0, pr]
            dsf32, dsb32 = dsf_ref[0, pr], dsb_ref[0, pr]
            sfp, sbp = sf32.astype(BF16), sb32.astype(BF16)
            dsfp, dsbp = dsf32.astype(BF16), dsb32.astype(BF16)
            dq2 = jnp.zeros((CHUNK, 128), F32)
            dk2 = jnp.zeros((CHUNK, 128), F32)
            for e in range(2):
                h = 2 * pr + e
                sl = slice(h * DV, (h + 1) * DV)
                lgf, lgb = lg_ref[0, h], lg_ref[1, h]
                hm = _head_lane_mask(q2.shape, e)
                qm = jnp.where(hm, q2, jnp.zeros_like(q2))
                km = jnp.where(hm, k2, jnp.zeros_like(k2))
                qf, kf = qm.astype(F32), km.astype(F32)
                v, do = v_ref[:, sl], do_ref[:, sl]
                vf, dof = v.astype(F32), do.astype(F32)
                m, mf1, mb1 = _decay_masks(lgf, lgb)
                m_t, _, _ = _decay_masks(lgb, lgf)
                sc = _dot_nt(qm, k2)
                dpm = _dot_nt(do, v)
                dsc = (dpm * m).astype(BF16)
                a_t = (_dot_nt(km, q2) * m_t).astype(BF16)
                dsc_t = (_dot_nt(v, do) * m_t).astype(BF16)
                dq_f, dq_b = jnp.exp(lgf * (i + 1.0)), jnp.exp(lgb * (CHUNK - i))
                dk_f, dk_b = jnp.exp(lgf * (CHUNK - 1.0 - i)), jnp.exp(lgb * i)
                dq = _dot(dsc, km)
                dq += jnp.where(hm, dq_f * _dot_nt(do, sfp) + dq_b * _dot_nt(do, sbp), 0.0)
                dk = _dot(dsc_t, qm)
                dk += jnp.where(hm, dk_f * _dot_nt(v, dsfp) + dk_b * _dot_nt(v, dsbp), 0.0)
                kdf = _dot((kf * dk_f).astype(BF16), dsfp)
                kdb = _dot((kf * dk_b).astype(BF16), dsbp)
                dv_ref[:, sl] = (_dot(a_t, do) + kdf + kdb).astype(BF16)
                dq2 += dq
                dk2 += dk
                xf = _dot((qf * dq_f).astype(BF16), sfp)
                xb = _dot((qf * dq_b).astype(BF16), sbp)
                pair = (rowid < DK) if e == 0 else (rowid >= DK)
                gcf, gcb = jnp.exp(lgf * full), jnp.exp(lgb * full)
                tf = _sum_all(sc * dpm * mf1) + _sum_all(xf * dof * (i + 1.0)) \
                    + _sum_all(kdf * vf * (CHUNK - 1.0 - i)) \
                    + CHUNK * gcf * _sum_all(jnp.where(pair, dsf32 * sf32, 0.0))
                tb = _sum_all(sc * dpm * mb1) + _sum_all(xb * dof * (CHUNK - i)) \
                    + _sum_all(kdb * vf * i) \
                    + CHUNK * gcb * _sum_all(jnp.where(pair, dsb32 * sb32, 0.0))
                acc_f += jnp.where(lane == h, tf, 0.0)
                acc_b += jnp.where(lane == h, tb, 0.0)
            c, s = c_ref[:, ps], s_ref[:, ps]
            dqk_ref[:, ps] = (dq2 * c - _swap_halves(dq2) * s).astype(BF16)
            dqk_ref[:, W + pr * 128:W + (pr + 1) * 128] = ((dk2 * c - _swap_halves(dk2) * s) * K_SCALE).astype(BF16)
        st_ref[0:1, :] += acc_f
        st_ref[1:2, :] += acc_b

    st_spec = pl.BlockSpec((1, HP, 128, 128), lambda n: (n, 0, 0, 0))
    half = pl.BlockSpec((CHUNK, W), lambda n: (n, 0))
    return _riding_call(
        body, exchange, N, name="ret_bwd_main",
        args=(lg, qr, kr, p, do, sf_prev, sb_prev, dsf, dsb, c2, s2),
        in_specs=[_smem_spec(), half, half,
                  pl.BlockSpec((CHUNK, D), lambda n: (n, 5)),
                  pl.BlockSpec((CHUNK, D), lambda n: (n, 0)),
                  st_spec, st_spec, st_spec, st_spec, half, half],
        out_specs=[pl.BlockSpec((CHUNK, 2 * D), lambda n: (n, 0)),
                   pl.BlockSpec((8, 128), lambda n: (0, 0))],
        out_shape=[jax.ShapeDtypeStruct((L, 2 * D), BF16), jax.ShapeDtypeStruct((8, 128), F32)],
        cparams=_cparams(("arbitrary",)))


def _ctx_bwd(pc, pqk_c, ds0, lg, D):
    Lc = pc.shape[0]
    H = D // DV
    HP = H // 2
    W = D // 2

    def body(lg_ref, k_ref, v_ref, ds_ref, dr_ref, st_ref):
        dqk_ref = dr_ref.at[:, 0:D]
        dv_ref = dr_ref.at[:, D:2 * D]
        m = lax.broadcasted_iota(jnp.int32, (Lc, 128), 0).astype(F32)
        lane = lax.broadcasted_iota(jnp.int32, (1, 128), 1)
        acc_f = jnp.zeros((1, 128), F32)
        acc_b = jnp.zeros((1, 128), F32)
        dqk_ref[:, 0:W] = jnp.zeros((Lc, W), BF16)
        for pr in range(HP):
            ps = slice(pr * 128, (pr + 1) * 128)
            k2 = k_ref[:, ps].astype(F32) * K_SCALE
            dsfp, dsbp = ds_ref[0, pr].astype(BF16), ds_ref[1, pr].astype(BF16)
            dk2 = jnp.zeros((Lc, 128), F32)
            for e in range(2):
                h = 2 * pr + e
                sl = slice(h * DV, (h + 1) * DV)
                hm = _head_lane_mask(k2.shape, e)
                km = jnp.where(hm, k2, 0.0)
                v = v_ref[:, sl]
                vf = v.astype(F32)
                dec_f = jnp.exp(lg_ref[0, h] * (Lc - 1.0 - m))
                dec_b = jnp.exp(lg_ref[1, h] * m)
                kdf = _dot((km * dec_f).astype(BF16), dsfp)
                kdb = _dot((km * dec_b).astype(BF16), dsbp)
                dv_ref[:, sl] = (kdf + kdb).astype(BF16)
                dk2 += jnp.where(hm, dec_f * _dot_nt(v, dsfp) + dec_b * _dot_nt(v, dsbp), 0.0)
                acc_f += jnp.where(lane == h, _sum_all(kdf * vf * (Lc - 1.0 - m)), 0.0)
                acc_b += jnp.where(lane == h, _sum_all(kdb * vf * m), 0.0)
            dqk_ref[:, W + pr * 128:W + (pr + 1) * 128] = (dk2 * K_SCALE).astype(BF16)
        st_ref[...] = jnp.zeros_like(st_ref)
        st_ref[0:1, :] = acc_f
        st_ref[1:2, :] = acc_b

    return pl.pallas_call(
        body, name="ctx_bwd", grid=(1,),
        in_specs=[_smem_spec(), pl.BlockSpec((Lc, W), lambda i: (0, 1)), pl.BlockSpec((Lc, D), lambda i: (0, 5)),
                  pl.BlockSpec((2, HP, 128, 128), lambda i: (0, 0, 0, 0))],
        out_specs=[pl.BlockSpec((Lc, 2 * D), lambda i: (0, 0)), pl.BlockSpec((8, 128), lambda i: (0, 0))],
        out_shape=[jax.ShapeDtypeStruct((Lc, 2 * D), BF16), jax.ShapeDtypeStruct((8, 128), F32)],
    )(lg, pqk_c, pc, ds0)


class _Exchange(NamedTuple):
    inputs: tuple
    out_shapes: tuple
    n_copies: int
    build: Callable


def _exchange_parts(exchange):
    if exchange is None:
        return [], [], [], [], []
    n = exchange.n_copies
    return (list(exchange.inputs), [ANY] * len(exchange.inputs), list(exchange.out_shapes),
            [ANY] * len(exchange.out_shapes), [pltpu.SemaphoreType.DMA((n,)), pltpu.SemaphoreType.DMA((n,))])


def _riding_call(body, exchange, n_steps, *, args, in_specs, out_specs, out_shape, name, cparams, scratch=()):
    ex_args, ex_in_specs, ex_shapes, ex_out_specs, ex_scratch = _exchange_parts(exchange)
    n_in, n_out, n_sc = len(args), len(out_shape), len(scratch)

    def riding(*refs):
        k = n_in + len(ex_args)
        ins, ex_in = refs[:n_in], refs[n_in:k]
        outs, ex_out = refs[k:k + n_out], refs[k + n_out:k + n_out + len(ex_shapes)]
        k += n_out + len(ex_shapes)
        own_scratch, ex_sems = refs[k:k + n_sc], refs[k + n_sc:]
        step = pl.program_id(0)
        if exchange is not None:
            @pl.when(step == 0)
            def _():
                for rc in exchange.build(ex_in, ex_out, *ex_sems):
                    rc.start()
        body(*ins, *outs, *own_scratch)
        if exchange is not None:
            @pl.when(step == n_steps - 1)
            def _():
                for rc in exchange.build(ex_in, ex_out, *ex_sems):
                    rc.wait()

    return tuple(pl.pallas_call(
        riding, name=name, grid=(n_steps,),
        in_specs=list(in_specs) + ex_in_specs, out_specs=list(out_specs) + ex_out_specs,
        out_shape=list(out_shape) + ex_shapes, scratch_shapes=list(scratch) + ex_scratch,
        compiler_params=cparams,
    )(*args, *ex_args))


def _dxm(groups, col0, w, x, nw, sc, dx1, name, exchange=None):
    L, D = x.shape
    tm = min(256, L)
    nt = L // tm
    ng = len(groups)
    widths = [g.shape[1] for g in groups]
    wtot = sum(widths)
    with_dx = dx1 is not None
    ex_args, ex_in_specs, ex_shapes, ex_out_specs, ex_scratch = _exchange_parts(exchange)
    n_in = ng + 4 + (1 if with_dx else 0)
    n_out = 2 if with_dx else 1

    def body(*refs):
        group_refs = refs[:ng]
        w_hbm, x_ref, nw_ref, sc_ref = refs[ng:ng + 4]
        ex_in = refs[n_in:n_in + len(ex_args)]
        outs = refs[n_in + len(ex_args):]
        if with_dx:
            dx1_ref, gx_ref, st_ref = refs[ng + 4], outs[0], outs[1]
        else:
            st_ref = outs[0]
        ex_out = outs[n_out:n_out + len(ex_shapes)]
        w_vm, sem = outs[n_out + len(ex_shapes):n_out + len(ex_shapes) + 2]
        ex_sems = outs[n_out + len(ex_shapes) + 2:]
        i = pl.program_id(0)

        @pl.when(i == 0)
        def _():
            cp = pltpu.make_async_copy(w_hbm.at[:, col0 * D:col0 * D + wtot], w_vm, sem)
            cp.start()
            if exchange is not None:
                for rc in exchange.build(ex_in, ex_out, *ex_sems):
                    rc.start()
            st_ref[...] = jnp.zeros_like(st_ref)
            cp.wait()

        dxm, off = None, 0
        for g_ref, wd in zip(group_refs, widths):
            part = _dot_nt(g_ref[...], w_vm[:, off:off + wd])
            dxm = part if dxm is None else dxm + part
            off += wd

        xv = x_ref[...]
        r = lax.rsqrt(jnp.mean(xv * xv, axis=-1, keepdims=True) + EPS)
        xh = xv * r
        nwv = nw_ref[...]
        dxn = dxm * (1.0 + sc_ref[...])
        st_ref[0:1, :] += jnp.sum(dxm, axis=0, keepdims=True)
        st_ref[1:2, :] += jnp.sum(dxm * (xh * nwv), axis=0, keepdims=True)
        st_ref[2:3, :] += jnp.sum(dxn * xh, axis=0, keepdims=True)
        if with_dx:
            dxh = dxn * nwv
            gx_ref[...] = dx1_ref[...] + r * (dxh - xh * jnp.mean(dxh * xh, axis=-1, keepdims=True))

        if exchange is not None:
            @pl.when(i == nt - 1)
            def _():
                for rc in exchange.build(ex_in, ex_out, *ex_sems):
                    rc.wait()

    row = pl.BlockSpec((tm, D), lambda i: (i, 0))
    in_specs = [pl.BlockSpec((tm, wd), lambda i: (i, 0)) for wd in widths] + [ANY, row, _vec_spec(D), _vec_spec(D)]
    out_specs = [pl.BlockSpec((8, D), lambda i: (0, 0))]
    out_shape = [jax.ShapeDtypeStruct((8, D), F32)]
    args = list(groups) + [w, x, nw, sc]
    if with_dx:
        in_specs.append(row)
        out_specs.insert(0, row)
        out_shape.insert(0, jax.ShapeDtypeStruct((L, D), F32))
        args.append(dx1)
    res = pl.pallas_call(
        body, name=name, grid=(nt,),
        in_specs=in_specs + ex_in_specs, out_specs=out_specs + ex_out_specs, out_shape=out_shape + ex_shapes,
        scratch_shapes=[pltpu.VMEM((D, wtot), BF16), pltpu.SemaphoreType.DMA] + ex_scratch,
        compiler_params=_cparams(("arbitrary",), VMEM_LIMIT),
    )(*args, *ex_args)
    gx = res[0] if with_dx else None
    return (gx, res[n_out - 1], *res[n_out:])


DW_TN = 512


def _dw_in(hidx, xmt, groups, cmt, dr_c, D, name, exchange=None):
    L = xmt.shape[1]
    Lc = cmt.shape[1]
    Dh = D // 2
    tn = min(DW_TN, D)
    nblk = [g.shape[1] // tn for g in groups]
    starts = [sum(nblk[:g]) for g in range(len(groups))]
    ng = len(groups)
    nj = sum(nblk)
    ex_args, ex_in_specs, ex_shapes, ex_out_specs, ex_scratch = _exchange_parts(exchange)

    def body(*refs):
        h_ref, xt_hbm = refs[0], refs[1]
        group_refs = refs[2:2 + ng]
        ct_ref, drc_ref = refs[2 + ng:4 + ng]
        ex_in = refs[4 + ng:4 + ng + len(ex_args)]
        outs = refs[4 + ng + len(ex_args):]
        o_ref = outs[0]
        ex_out = outs[1:1 + len(ex_shapes)]
        xt_vm, sem = outs[1 + len(ex_shapes):3 + len(ex_shapes)]
        ex_sems = outs[3 + len(ex_shapes):]
        j = pl.program_id(0)

        @pl.when(j == 0)
        def _():
            rows = pl.ds(pl.multiple_of(h_ref[0] * Dh, Dh), Dh)
            cp = pltpu.make_async_copy(xt_hbm.at[rows, :], xt_vm, sem)
            cp.start()
            if exchange is not None:
                for rc in exchange.build(ex_in, ex_out, *ex_sems):
                    rc.start()
            cp.wait()

        for g in range(ng):
            @pl.when((j >= starts[g]) & (j < starts[g] + nblk[g]))
            def _(g=g):
                acc = _dot(xt_vm[...], group_refs[g][...])
                if g == 1:
                    acc += _dot(ct_ref[...], drc_ref[...])
                o_ref[...] = acc

        if exchange is not None:
            @pl.when(j == nj - 1)
            def _():
                for rc in exchange.build(ex_in, ex_out, *ex_sems):
                    rc.wait()

    def group_spec(g, rows):
        return pl.BlockSpec((rows, tn), lambda j, h: (0, jnp.clip(j - starts[g], 0, nblk[g] - 1)))

    res = pl.pallas_call(
        body, name=name,
        grid_spec=pltpu.PrefetchScalarGridSpec(
            num_scalar_prefetch=1, grid=(nj,),
            in_specs=[ANY] + [group_spec(g, L) for g in range(ng)]
                     + [pl.BlockSpec((Dh, Lc), lambda j, h: (h[0], 0)), group_spec(1, Lc)] + ex_in_specs,
            out_specs=[pl.BlockSpec((Dh, tn), lambda j, h: (0, j))] + ex_out_specs,
            scratch_shapes=[pltpu.VMEM((Dh, L), BF16), pltpu.SemaphoreType.DMA] + ex_scratch),
        out_shape=[jax.ShapeDtypeStruct((Dh, nj * tn), F32)] + ex_shapes,
        compiler_params=_cparams(("arbitrary",), VMEM_LIMIT),
    )(hidx, xmt, *groups, cmt, dr_c, *ex_args)
    return tuple(res)


def _local_step(x, ctx, tgt, mod_x, mod_c, norm_w, conv_w8, conv_b, lg, gn_w, fw, w_in, w3, csidx=None):
    L, D = x.shape
    sh_x, sc_x, g_x = mod_x[0:1], mod_x[1:2], mod_x[2:3]
    sh_c, sc_c = mod_c[0:1], mod_c[1:2]
    c2, s2 = _rope_tables(L, D)

    xm, xmt = _norm_mod(x, norm_w, sc_x, sh_x, "norm_mod_x")
    cm, cmt = _norm_mod(ctx, norm_w, sc_c, sh_c, "norm_mod_ctx")
    p, pqk = _in_proj(xm, w_in, "in_proj_x")
    pc, pqk_c = _in_proj(cm, w_in, "in_proj_ctx")
    ya = _conv_gate_fwd(p, conv_w8, conv_b, D)
    qr, kr = _rope_fwd(pqk, c2, s2, D)
    s0 = _ctx_states(pc, pqk_c, lg, D)
    sf_prev, sb_prev = _ret_states(kr, p, s0, lg, D)
    o, yb = _ret_out(qr, kr, p, sf_prev, sb_prev, gn_w, lg, D)
    dx1, dya, dyb, dgab, dw3, st_mid = _mid(ya, yb, p, x, tgt, w3, g_x, fw, D)
    reduce = csidx is not None
    dw3_5 = dw3.reshape(3, N_SHARD, 2, D // 8, D)
    dconv, st_conv, *ra_3 = _conv_bwd(dya, p, conv_w8, conv_b, D, _pair_exchange_w3(dw3_5) if reduce else None)
    do, dzb, st_gn = _ret_bwd_pre(dyb, p, o, gn_w, D)
    dsf, dsb, ds0 = _ret_bwd_states(qr, do, lg, D)
    cs_3 = _sum_pair_w3(csidx[0:1], dw3_5, ra_3[0]) if reduce else None
    dret, st_lg, *rb_3 = _ret_bwd_main(qr, kr, p, do, sf_prev, sb_prev, dsf, dsb, c2, s2, lg, D,
                                       _chips_exchange_w3(cs_3) if reduce else None)
    g_3 = _sum_chips_w3(csidx, cs_3, rb_3[0]) if reduce else dw3
    dret_c, st_lgc = _ctx_bwd(pc, pqk_c, ds0, lg, D)
    groups = (dconv, dret, dzb, dgab)
    _, st_c = _dxm((dret_c,), 4, w_in, ctx, norm_w, sc_c, None, "dxm_ctx")
    return groups, dret_c, xmt, cmt, dx1, sc_x, g_3, (st_mid, st_conv, st_gn, st_lg, st_lgc, st_c)


CHIP_FLIPS = (4, 2, 6)
ANY = pl.BlockSpec(memory_space=pl.ANY)
VMEM_FULL = pl.BlockSpec(memory_space=pltpu.VMEM)


def _position():
    return lax.axis_index("x"), lax.axis_index("y"), lax.axis_index("c")


def _peer(pos, k):
    x, y, c = pos
    return (1 - x if k & 4 else x, 1 - y if k & 2 else y, 1 - c if k & 1 else c)


def _dev_id(pos):
    return 4 * pos[0] + 2 * pos[1] + pos[2]


def _shard_of(pos):
    return 2 * pos[0] + pos[1]


def _remote(src, dst, send_sems, recv_sems, idx, to):
    return pltpu.make_async_remote_copy(src_ref=src, dst_ref=dst, send_sem=send_sems.at[idx],
                                        recv_sem=recv_sems.at[idx], device_id=to, device_id_type=MESH)


def _dot_f32(a, b):
    return jnp.dot(a, b, precision=lax.Precision.HIGHEST, preferred_element_type=F32)


def _silu(x):
    return x * _sigmoid(x)


def _fwd_small(c8, cctx8, ada_w, ada_b, conv_w8):
    D = c8.shape[1]
    Wm = ada_w.shape[1]
    Dq = conv_w8.shape[1]

    def body(c_ref, cc_ref, aw_ref, ab_ref, cw_ref, act_ref, mod_ref, cwf_ref,
             cbuf, pmine, pbuf, wbuf, s_c, r_c, s_p, r_p, s_w, r_w):
        pos = _position()
        me, s = _dev_id(pos), _shard_of(pos)
        cbuf[me] = c_ref[...]
        wbuf[s] = cw_ref[...]
        sends = [_remote(c_ref, cbuf.at[me], s_c, r_c, k - 1, _peer(pos, k)) for k in range(1, 8)]
        sends += [_remote(cw_ref, wbuf.at[s], s_w, r_w, j, _peer(pos, k)) for j, k in enumerate(CHIP_FLIPS)]
        for cp in sends:
            cp.start()
        for k in range(1, 8):
            _remote(c_ref, cbuf.at[_dev_id(_peer(pos, k))], s_c, r_c, k - 1, _peer(pos, k)).wait_recv()
        for d in range(N_DEV):
            act_ref[d:d + 1, :] = _silu(cbuf[d, 0:1, :])
        act_ref[8:9, :] = _silu(cc_ref[0:1, :])
        act_ref[9:16, :] = jnp.zeros((7, D), F32)
        part = _dot_f32(act_ref[...], aw_ref[...])
        pmine[...] = part
        pbuf[s] = part
        psend = [_remote(pmine, pbuf.at[s], s_p, r_p, j, _peer(pos, k)) for j, k in enumerate(CHIP_FLIPS)]
        for cp in psend:
            cp.start()
        for j, k in enumerate(CHIP_FLIPS):
            t = _shard_of(_peer(pos, k))
            _remote(pmine, pbuf.at[t], s_p, r_p, j, _peer(pos, k)).wait_recv()
            _remote(cw_ref, wbuf.at[t], s_w, r_w, j, _peer(pos, k)).wait_recv()
        for t in range(N_SHARD):
            mod_ref[:, t * Wm:(t + 1) * Wm] = pbuf[t] + ab_ref[:, t * Wm:(t + 1) * Wm]
            cwf_ref[:, t * Dq:(t + 1) * Dq] = wbuf[t]
        for cp in sends + psend:
            cp.wait_send()

    return pl.pallas_call(
        body, name="fwd_small",
        in_specs=[VMEM_FULL] * 5, out_specs=[VMEM_FULL] * 3,
        out_shape=[jax.ShapeDtypeStruct((16, D), F32), jax.ShapeDtypeStruct((16, 3 * D), F32),
                   jax.ShapeDtypeStruct((8, D), F32)],
        scratch_shapes=[pltpu.VMEM((N_DEV, 8, D), F32), pltpu.VMEM((16, Wm), F32),
                        pltpu.VMEM((N_SHARD, 16, Wm), F32), pltpu.VMEM((N_SHARD, 8, Dq), F32),
                        pltpu.SemaphoreType.DMA((7,)), pltpu.SemaphoreType.DMA((7,)),
                        pltpu.SemaphoreType.DMA((3,)), pltpu.SemaphoreType.DMA((3,)),
                        pltpu.SemaphoreType.DMA((3,)), pltpu.SemaphoreType.DMA((3,))],
        compiler_params=_cparams(None, VMEM_LIMIT),
    )(c8, cctx8, ada_w, ada_b, conv_w8)


def _cols(t, w):
    return pl.ds(pl.multiple_of(t * w, 128), w)


def _ag_weights(w_in_s, w3_s):
    D, Wc = w_in_s.shape
    Dh = D // 2
    Do = w3_s.shape[2]

    def body(wi_ref, w3_ref, fi_ref, f3_ref, si, s3, send, recv, loc):
        pos = _position()
        c = pos[2]
        s = _shard_of(pos)
        sib = _peer(pos, 1)

        def cast_rows(r, carry):
            rows = pl.ds(pl.multiple_of(r * 64, 64), 64)
            si[rows, :] = wi_ref[rows, :].astype(BF16)
            return carry

        lax.fori_loop(0, D // 64, cast_rows, 0)
        for a in range(3):
            s3[a] = w3_ref[a].astype(BF16)

        def in_half(t, hf):
            return fi_ref.at[pl.ds(hf * Dh, Dh), _cols(t, Wc)]

        def w3_half(t, hf):
            return f3_ref.at[:, t, hf]

        own = [pltpu.make_async_copy(si, fi_ref.at[:, _cols(s, Wc)], loc.at[0]),
               pltpu.make_async_copy(s3, f3_ref.at[:, s], loc.at[1])]
        first = []
        for j, k in enumerate(CHIP_FLIPS):
            to = _peer(pos, k)
            first.append(_remote(si.at[pl.ds(c * Dh, Dh), :], in_half(s, c), send, recv, j, to))
            first.append(_remote(s3.at[:, c], w3_half(s, c), send, recv, 3 + j, to))
        for cp in own + first:
            cp.start()
        passed = []
        for j, k in enumerate(CHIP_FLIPS):
            t = _shard_of(_peer(pos, k))
            _remote(in_half(t, c), in_half(t, c), send, recv, j, sib).wait_recv()
            fwd_i = _remote(in_half(t, c), in_half(t, c), send, recv, 6 + j, sib)
            fwd_i.start()
            _remote(w3_half(t, c), w3_half(t, c), send, recv, 3 + j, sib).wait_recv()
            fwd_3 = _remote(w3_half(t, c), w3_half(t, c), send, recv, 9 + j, sib)
            fwd_3.start()
            passed += [fwd_i, fwd_3]
        for j, k in enumerate(CHIP_FLIPS):
            t = _shard_of(_peer(pos, k))
            _remote(in_half(t, 1 - c), in_half(t, 1 - c), send, recv, 6 + j, sib).wait_recv()
            _remote(w3_half(t, 1 - c), w3_half(t, 1 - c), send, recv, 9 + j, sib).wait_recv()
        for cp in first + passed:
            cp.wait_send()
        for cp in own:
            cp.wait()

    return pl.pallas_call(
        body, name="ag_weights",
        in_specs=[VMEM_FULL, VMEM_FULL], out_specs=[ANY, ANY],
        out_shape=[jax.ShapeDtypeStruct((D, N_SHARD * Wc), BF16), jax.ShapeDtypeStruct((3, N_SHARD, 2, Do, D), BF16)],
        scratch_shapes=[pltpu.VMEM((D, Wc), BF16), pltpu.VMEM((3, 2, Do, D), BF16),
                        pltpu.SemaphoreType.DMA((12,)), pltpu.SemaphoreType.DMA((12,)), pltpu.SemaphoreType.DMA((2,))],
        compiler_params=_cparams(None, VMEM_LIMIT),
    )(w_in_s, w3_s)


def _pair_exchange_in(dw_other):
    def build(ins, outs, send, recv):
        return [_remote(ins[0], outs[0], send, recv, 0, _peer(_position(), 1))]

    return _Exchange((dw_other,), (jax.ShapeDtypeStruct(dw_other.shape, F32),), 1, build)


def _pair_exchange_w3(dw3):
    _, _, _, Do, D = dw3.shape

    def build(ins, outs, send, recv):
        pos = _position()
        return [_remote(ins[0].at[:, :, 1 - pos[2]], outs[0], send, recv, 0, _peer(pos, 1))]

    return _Exchange((dw3,), (jax.ShapeDtypeStruct((3, N_SHARD, Do, D), F32),), 1, build)


def _sum_pair_in(dw_mine, ri):
    Dh, Wf = dw_mine.shape
    Wc = Wf // N_SHARD
    tr = min(256, Dh)

    def body(a_ref, b_ref, o_ref):
        o_ref[...] = (a_ref[...] + b_ref[...]).astype(BF16)

    return pl.pallas_call(
        body, name="sum_pair_in", grid=(Dh // tr, N_SHARD),
        in_specs=[pl.BlockSpec((tr, Wc), lambda i, t: (i, t)), pl.BlockSpec((tr, Wc), lambda i, t: (i, t))],
        out_specs=pl.BlockSpec((None, tr, Wc), lambda i, t: (t, i, 0)),
        out_shape=jax.ShapeDtypeStruct((N_SHARD, Dh, Wc), BF16),
        compiler_params=_cparams(("parallel", "parallel")),
    )(dw_mine, ri)


def _sum_pair_w3(cidx, dw3, r3):
    _, _, _, Do, D = dw3.shape

    def body(c_ref, a_ref, b_ref, o_ref):
        o_ref[...] = (a_ref[...] + b_ref[...]).astype(BF16)

    return pl.pallas_call(
        body, name="sum_pair_w3",
        grid_spec=pltpu.PrefetchScalarGridSpec(
            num_scalar_prefetch=1, grid=(3, N_SHARD),
            in_specs=[pl.BlockSpec((None, None, None, Do, D), lambda a, t, c: (a, t, c[0], 0, 0)),
                      pl.BlockSpec((None, None, Do, D), lambda a, t, c: (a, t, 0, 0))],
            out_specs=pl.BlockSpec((None, None, Do, D), lambda a, t, c: (a, t, 0, 0))),
        out_shape=jax.ShapeDtypeStruct((3, N_SHARD, Do, D), BF16),
        compiler_params=_cparams(("parallel", "parallel")),
    )(cidx, dw3, r3)


def _chips_exchange_in(cs_in):
    _, Dh, Wc = cs_in.shape

    def build(ins, outs, send, recv):
        pos = _position()
        return [_remote(ins[0].at[_shard_of(_peer(pos, k))], outs[0].at[j], send, recv, j, _peer(pos, k))
                for j, k in enumerate(CHIP_FLIPS)]

    return _Exchange((cs_in,), (jax.ShapeDtypeStruct((3, Dh, Wc), BF16),), 3, build)


def _chips_exchange_w3(cs_3):
    _, _, Do, D = cs_3.shape

    def build(ins, outs, send, recv):
        pos = _position()
        return [_remote(ins[0].at[:, _shard_of(_peer(pos, k))], outs[0].at[j], send, recv, j, _peer(pos, k))
                for j, k in enumerate(CHIP_FLIPS)]

    return _Exchange((cs_3,), (jax.ShapeDtypeStruct((3, 3, Do, D), BF16),), 3, build)


def _sum_chips_in(csidx, cs_in, rb_in):
    _, Dh, Wc = cs_in.shape
    tr = min(256, Dh)

    def body(s_ref, a_ref, b_ref, o_ref):
        acc = a_ref[...].astype(F32)
        for j in range(3):
            acc = acc + b_ref[j].astype(F32)
        o_ref[...] = acc

    return pl.pallas_call(
        body, name="sum_chips_in",
        grid_spec=pltpu.PrefetchScalarGridSpec(
            num_scalar_prefetch=1, grid=(Dh // tr,),
            in_specs=[pl.BlockSpec((None, tr, Wc), lambda i, s: (s[1], i, 0)),
                      pl.BlockSpec((3, tr, Wc), lambda i, s: (0, i, 0))],
            out_specs=pl.BlockSpec((None, tr, Wc), lambda i, s: (s[0], i, 0))),
        out_shape=jax.ShapeDtypeStruct((2, Dh, Wc), F32),
        compiler_params=_cparams(("parallel",)),
    )(csidx, cs_in, rb_in)


def _sum_chips_w3(csidx, cs_3, rb_3):
    _, _, Do, D = cs_3.shape

    def body(s_ref, a_ref, b_ref, o_ref):
        acc = a_ref[...].astype(F32)
        for j in range(3):
            acc = acc + b_ref[j].astype(F32)
        o_ref[...] = acc

    return pl.pallas_call(
        body, name="sum_chips_w3",
        grid_spec=pltpu.PrefetchScalarGridSpec(
            num_scalar_prefetch=1, grid=(3,),
            in_specs=[pl.BlockSpec((None, None, Do, D), lambda a, s: (a, s[1], 0, 0)),
                      pl.BlockSpec((3, None, Do, D), lambda a, s: (0, a, 0, 0))],
            out_specs=pl.BlockSpec((None, None, Do, D), lambda a, s: (a, s[0], 0, 0))),
        out_shape=jax.ShapeDtypeStruct((3, 2, Do, D), F32),
        compiler_params=_cparams(("parallel",)),
    )(csidx, cs_3, rb_3)


def _rs_final(g_in, g_3):
    def body(hi_ref, h3_ref, gi_ref, g3_ref, send, recv):
        pos = _position()
        c = pos[2]
        sib = _peer(pos, 1)
        cps = [_remote(hi_ref.at[c], gi_ref.at[c], send, recv, 0, sib),
               _remote(h3_ref.at[:, c], g3_ref.at[:, c], send, recv, 1, sib)]
        for cp in cps:
            cp.start()
        _remote(hi_ref.at[1 - c], gi_ref.at[1 - c], send, recv, 0, sib).wait_recv()
        _remote(h3_ref.at[:, 1 - c], g3_ref.at[:, 1 - c], send, recv, 1, sib).wait_recv()
        for cp in cps:
            cp.wait_send()

    return pl.pallas_call(
        body, name="rs_final", in_specs=[ANY, ANY], out_specs=[ANY, ANY],
        out_shape=[jax.ShapeDtypeStruct(g_in.shape, F32), jax.ShapeDtypeStruct(g_3.shape, F32)],
        input_output_aliases={0: 0, 1: 1},
        scratch_shapes=[pltpu.SemaphoreType.DMA((2,)), pltpu.SemaphoreType.DMA((2,))],
    )(g_in, g_3)


def _adam_math(w, g, m, v):
    m = ADAM_B1 * m + (1.0 - ADAM_B1) * g
    v = ADAM_B2 * v + (1.0 - ADAM_B2) * (g * g)
    m_hat = m / (1.0 - ADAM_B1 ** ADAM_STEP)
    v_hat = v / (1.0 - ADAM_B2 ** ADAM_STEP)
    delta = -ADAM_LR * (m_hat / (jnp.sqrt(v_hat) + ADAM_EPS) + ADAM_WD * w)
    return delta, m, v


def _adamw(w, g, m, v, name):
    R, C = w.shape
    tr = min(128, R)

    def body(w_ref, g_ref, m_ref, v_ref, d_ref, nm_ref, nv_ref):
        d_ref[...], nm_ref[...], nv_ref[...] = _adam_math(w_ref[...], g_ref[...], m_ref[...], v_ref[...])

    blk = pl.BlockSpec((tr, C), lambda i: (i, 0))
    return pl.pallas_call(
        body, name=name, grid=(R // tr,), in_specs=[blk] * 4, out_specs=[blk] * 3,
        out_shape=[jax.ShapeDtypeStruct((R, C), F32)] * 3,
        compiler_params=_cparams(("parallel",), VMEM_LIMIT),
    )(w, g, m, v)


def _bwd_small(vec, act, ada_w, cctx8, w_sm, m_sm, v_sm, w_ab, m_ab, v_ab, w_cw, m_cw, v_cw, w_dl, m_dl, v_dl):
    D = vec.shape[1]
    Wm = ada_w.shape[1]
    Dq = w_cw.shape[1]

    def body(vec_ref, act_ref, aw_ref, cc_ref, wsm, msm, vsm, wab, mab, vab, wcw, mcw, vcw, wdl, mdl, vdl,
             gaw_ref, loss_ref, o_sm, o_ab, o_cw, o_dl,
             vbuf, dm, dm_sh, gcw, amine, abuf, s_v, r_v, s_a, r_a):
        pos = _position()
        me, s = _dev_id(pos), _shard_of(pos)
        vbuf[me] = vec_ref[...]
        sends = [_remote(vec_ref, vbuf.at[me], s_v, r_v, k - 1, _peer(pos, k)) for k in range(1, 8)]
        for cp in sends:
            cp.start()
        for k in range(1, 8):
            _remote(vec_ref, vbuf.at[_dev_id(_peer(pos, k))], s_v, r_v, k - 1, _peer(pos, k)).wait_recv()
        tot = vbuf[0]
        for d in range(1, N_DEV):
            tot = tot + vbuf[d]
        loss_ref[...] = jnp.zeros((8, 128), F32) + (0.5 / D) * _sum_all(tot[14:15, :])
        dm[...] = jnp.zeros_like(dm)
        for d in range(N_DEV):
            for r in range(3):
                dm[d:d + 1, r * D:(r + 1) * D] = vbuf[d, r:r + 1, :]
        dm[8:9, 0:D] = tot[3:4, :]
        dm[8:9, D:2 * D] = tot[4:5, :]
        for t in range(N_SHARD):
            @pl.when(s == t)
            def _(t=t):
                dm_sh[...] = dm[:, t * Wm:(t + 1) * Wm]
                gcw[...] = jnp.zeros_like(gcw)
                gcw[0:3, :] = tot[9:12, t * Dq:(t + 1) * Dq]
        gaw_ref[...] = lax.dot_general(act_ref[...], dm_sh[...], (((0,), (0,)), ((), ())),
                                       precision=lax.Precision.HIGHEST, preferred_element_type=F32)
        part = lax.dot_general(dm_sh[8:16, :], aw_ref[...], (((1,), (1,)), ((), ())),
                               precision=lax.Precision.HIGHEST, preferred_element_type=F32)
        amine[...] = part
        abuf[s] = part
        asend = [_remote(amine, abuf.at[s], s_a, r_a, j, _peer(pos, k)) for j, k in enumerate(CHIP_FLIPS)]
        for cp in asend:
            cp.start()
        for j, k in enumerate(CHIP_FLIPS):
            _remote(amine, abuf.at[_shard_of(_peer(pos, k))], s_a, r_a, j, _peer(pos, k)).wait_recv()
        da = abuf[0]
        for t in range(1, N_SHARD):
            da = da + abuf[t]
        cc = cc_ref[0:1, :]
        sg = _sigmoid(cc)
        g_cctx = da[0:1, :] * (sg * (1.0 + cc * (1.0 - sg)))

        def emit(o_ref, w, g, m, v):
            o_ref[0] = g
            o_ref[1], o_ref[2], o_ref[3] = _adam_math(w, g, m, v)

        g_sm = jnp.concatenate([g_cctx, tot[5:9, :], jnp.zeros((3, D), F32)], axis=0)
        emit(o_sm, wsm[...], g_sm, msm[...], vsm[...])
        g_ab = jnp.concatenate([tot[0:1, :] + tot[3:4, :], tot[1:2, :] + tot[4:5, :], tot[2:3, :],
                                jnp.zeros((5, D), F32)], axis=0)
        emit(o_ab, wab[...], g_ab, mab[...], vab[...])
        emit(o_cw, wcw[...], gcw[...], mcw[...], vcw[...])
        g_dl = jnp.concatenate([tot[12:14, 0:128] * _sigmoid(-wdl[0:2, :]), jnp.zeros((6, 128), F32)], axis=0)
        emit(o_dl, wdl[...], g_dl, mdl[...], vdl[...])
        for cp in sends + asend:
            cp.wait_send()

    return pl.pallas_call(
        body, name="bwd_small",
        in_specs=[VMEM_FULL] * 16, out_specs=[VMEM_FULL] * 6,
        out_shape=[jax.ShapeDtypeStruct((D, Wm), F32), jax.ShapeDtypeStruct((8, 128), F32),
                   jax.ShapeDtypeStruct((4, 8, D), F32), jax.ShapeDtypeStruct((4, 8, D), F32),
                   jax.ShapeDtypeStruct((4, 8, Dq), F32), jax.ShapeDtypeStruct((4, 8, 128), F32)],
        scratch_shapes=[pltpu.VMEM((N_DEV, 16, D), F32), pltpu.VMEM((16, 3 * D), F32), pltpu.VMEM((16, Wm), F32),
                        pltpu.VMEM((8, Dq), F32), pltpu.VMEM((8, D), F32), pltpu.VMEM((N_SHARD, 8, D), F32),
                        pltpu.SemaphoreType.DMA((7,)), pltpu.SemaphoreType.DMA((7,)),
                        pltpu.SemaphoreType.DMA((3,)), pltpu.SemaphoreType.DMA((3,))],
        compiler_params=_cparams(None, VMEM_LIMIT),
    )(vec, act, ada_w, cctx8, w_sm, m_sm, v_sm, w_ab, m_ab, v_ab, w_cw, m_cw, v_cw, w_dl, m_dl, v_dl)


def _pad_rows(a, rows=8):
    return jnp.pad(a, ((0, rows - a.shape[0]), (0, 0)))


def kernel(x, c, ctx, c_ctx, norm_w, ada_w, ada_b, w_in, conv_w, conv_b, decay_logit, gn_w, w_a, w_b, w_out, final_norm_w, loss_target, m_c_ctx, m_norm_w, m_ada_w, m_ada_b, m_w_in, m_conv_w, m_conv_b, m_decay_logit, m_gn_w, m_w_a, m_w_b, m_w_out, m_final_norm_w, v_c_ctx, v_norm_w, v_ada_w, v_ada_b, v_w_in, v_conv_w, v_conv_b, v_decay_logit, v_gn_w, v_w_a, v_w_b, v_w_out, v_final_norm_w):
    L, D = x.shape[1], x.shape[2]
    H = D // DV
    Wc = w_in.shape[2]
    Do = D // 8
    pos = _position()
    me = _dev_id(pos)
    cidx = jnp.reshape(pos[2], (1,)).astype(jnp.int32)
    sidx = jnp.reshape(_shard_of(pos), (1,)).astype(jnp.int32)

    act, mod, conv_w8 = _fwd_small(_pad_rows(c), _pad_rows(c_ctx[None]), ada_w[0], ada_b, _pad_rows(conv_w[0]))
    mod_x = lax.dynamic_slice_in_dim(mod, me, 1, axis=0).reshape(3, D)
    mod_c = mod[8].reshape(3, D)
    lg = jax.nn.log_sigmoid(decay_logit[0])

    w3_s = jnp.stack([w_a[0], w_b[0], w_out[0]]).reshape(3, 2, Do, D)
    w_in_full, w3_full = _ag_weights(w_in[0], w3_s)
    w3_full = w3_full.reshape(3, D, D)

    csidx = jnp.concatenate([cidx, sidx])
    groups, dret_c, xmt, cmt, dx1, sc_x, gh_3, sts = _local_step(
        x[0], ctx[0], loss_target[0], mod_x, mod_c, norm_w, conv_w8, conv_b, lg, gn_w, final_norm_w[None],
        w_in_full, w3_full, csidx)
    st_mid, st_conv, st_gn, st_lg, st_lgc, st_c = sts

    (dw_other,) = _dw_in(1 - cidx, xmt, groups, cmt, dret_c, D, "dw_in_other")
    dw_mine, ra_in = _dw_in(cidx, xmt, groups, cmt, dret_c, D, "dw_in_mine", _pair_exchange_in(dw_other))
    cs_in = _sum_pair_in(dw_mine, ra_in)
    grad_x, st_x, rb_in = _dxm(groups, 0, w_in_full, x[0], norm_w, sc_x, dx1, "dxm_x", _chips_exchange_in(cs_in))
    gh_in = _sum_chips_in(csidx, cs_in, rb_in)
    g_in, g_3 = _rs_final(gh_in, gh_3)
    g_w_in = g_in.reshape(D, Wc)
    g_3 = g_3.reshape(3, D // 4, D)

    lanes = lambda a: jnp.pad(a, ((0, 0), (0, D - a.shape[1])))
    vec = jnp.concatenate([
        st_x[0:2], st_mid[1:2], st_c[0:2], st_x[2:3] + st_c[2:3], st_conv[3:4], st_gn[0:1], st_mid[0:1],
        st_conv[0:3], lanes(st_lg[0:2] + st_lgc[0:2]), st_mid[2:3], jnp.zeros((1, D), F32)], axis=0)
    small = lambda a, b_, c_, d_, e_: _pad_rows(jnp.concatenate([a[None], b_, c_, d_, e_[None]], axis=0))
    dl = lambda a: jnp.pad(a[0], ((0, 6), (0, 128 - H)))
    g_ada_w, loss_t, o_sm, o_ab, o_cw, o_dl = _bwd_small(
        vec, act, ada_w[0], _pad_rows(c_ctx[None]),
        small(c_ctx, norm_w, conv_b, gn_w, final_norm_w), small(m_c_ctx, m_norm_w, m_conv_b, m_gn_w, m_final_norm_w),
        small(v_c_ctx, v_norm_w, v_conv_b, v_gn_w, v_final_norm_w),
        _pad_rows(ada_b.reshape(3, D)), _pad_rows(m_ada_b.reshape(3, D)), _pad_rows(v_ada_b.reshape(3, D)),
        _pad_rows(conv_w[0]), _pad_rows(m_conv_w[0]), _pad_rows(v_conv_w[0]),
        dl(decay_logit), dl(m_decay_logit), dl(v_decay_logit))

    upd_in = _adamw(w_in[0], g_w_in, m_w_in[0], v_w_in[0], "adamw_w_in")
    upd_ada = _adamw(ada_w[0], g_ada_w, m_ada_w[0], v_ada_w[0], "adamw_ada_w")
    upd_a = _adamw(w_a[0], g_3[0], m_w_a[0], v_w_a[0], "adamw_w_a")
    upd_b = _adamw(w_b[0], g_3[1], m_w_b[0], v_w_b[0], "adamw_w_b")
    upd_o = _adamw(w_out[0], g_3[2], m_w_out[0], v_w_out[0], "adamw_w_out")

    def leaves(q):
        big = lambda g, upd: (g if q == 0 else upd[q - 1])[None]
        sm = o_sm[q]
        return [sm[0], sm[1:2], big(g_ada_w, upd_ada), o_ab[q][0:3].reshape(1, 3 * D), big(g_w_in, upd_in),
                o_cw[q][0:3][None], sm[2:3], o_dl[q][0:2, 0:H][None], sm[3:4],
                big(g_3[0], upd_a), big(g_3[1], upd_b), big(g_3[2], upd_o), sm[4]]

    loss = loss_t[0, 0]
    return (loss, grad_x[None], *leaves(0), *leaves(1), *leaves(2), *leaves(3))
```

```python
from typing import Callable, NamedTuple

import jax
import jax.numpy as jnp
from jax import lax
from jax.experimental import pallas as pl
from jax.experimental.pallas import tpu as pltpu

F32 = jnp.float32
BF16 = jnp.bfloat16
MESH = pl.DeviceIdType.MESH

CHUNK = 128
DV = 128
DK = 64
GRID_W = 64
ROPE_BASE = 10000.0
EPS = 1e-6
K_SCALE = DK ** -0.5
N_SHARD = 4
N_DEV = 8

ADAM_LR = 0.001
ADAM_B1 = 0.9
ADAM_B2 = 0.999
ADAM_EPS = 1e-08
ADAM_WD = 0.01
ADAM_STEP = 10

VMEM_LIMIT = 56 * 1024 * 1024


def _cparams(sem=None, vmem=None):
    kw = {}
    if sem is not None:
        kw["dimension_semantics"] = sem
    if vmem is not None:
        kw["vmem_limit_bytes"] = vmem
    return pltpu.CompilerParams(**kw)


def _dot(a, b):
    return jnp.dot(a, b, preferred_element_type=F32)


def _dot_nt(a, b):
    return lax.dot_general(a, b, (((1,), (1,)), ((), ())), preferred_element_type=F32)


def _dot_tn(a, b):
    return lax.dot_general(a, b, (((0,), (0,)), ((), ())), preferred_element_type=F32)


def _sigmoid(x):
    return 1.0 / (1.0 + jnp.exp(-x))


def _sum_all(x):
    return jnp.sum(jnp.sum(x, axis=1, keepdims=True), axis=0, keepdims=True)


def _swap_halves(t):
    n = t.shape[1]
    lane = lax.broadcasted_iota(jnp.int32, t.shape, 1)
    low = (lane & 32) == 0
    return jnp.where(low, pltpu.roll(t, n - 32, 1), pltpu.roll(t, 32, 1))


def _vec_spec(d):
    return pl.BlockSpec((1, d), lambda *a: (0, 0))


def _norm_mod(x, nw, sc, sh, name):
    L, D = x.shape
    tl = min(256, L)

    def body(x_ref, nw_ref, sc_ref, sh_ref, xm_ref, xmt_ref):
        xv = x_ref[...]
        r = lax.rsqrt(jnp.mean(xv * xv, axis=-1, keepdims=True) + EPS)
        xm = (xv * r * nw_ref[...]) * (1.0 + sc_ref[...]) + sh_ref[...]
        xm_ref[...] = xm.astype(BF16)
        xmt_ref[...] = xm.T.astype(BF16)

    return pl.pallas_call(
        body, name=name, grid=(L // tl,),
        in_specs=[pl.BlockSpec((tl, D), lambda i: (i, 0)), _vec_spec(D), _vec_spec(D), _vec_spec(D)],
        out_specs=[pl.BlockSpec((tl, D), lambda i: (i, 0)), pl.BlockSpec((D, tl), lambda i: (0, i))],
        out_shape=[jax.ShapeDtypeStruct((L, D), BF16), jax.ShapeDtypeStruct((D, L), BF16)],
        compiler_params=_cparams(("parallel",)),
    )(x, nw, sc, sh)


def _in_proj(xm, w, name):
    M, D = xm.shape
    N = w.shape[1]
    tm = min(1024, M)

    def body(a_ref, b_ref, o_ref, qk_ref):
        acc = _dot(a_ref[...], b_ref[...])
        o_ref[...] = acc.astype(o_ref.dtype)

        @pl.when(pl.program_id(1) == 4)
        def _():
            qk_ref[...] = acc

    return pl.pallas_call(
        body, name=name, grid=(M // tm, N // D),
        in_specs=[pl.BlockSpec((tm, D), lambda i, j: (i, 0)), pl.BlockSpec((D, D), lambda i, j: (0, j))],
        out_specs=[pl.BlockSpec((tm, D), lambda i, j: (i, j)), pl.BlockSpec((tm, D), lambda i, j: (i, 0))],
        out_shape=[jax.ShapeDtypeStruct((M, N), BF16), jax.ShapeDtypeStruct((M, D), F32)],
        compiler_params=_cparams(("parallel", "arbitrary")),
    )(xm, w)


def _halo_specs(tl, L, D, col):
    hb = tl // 16
    last = L // 16 - 1
    prev = pl.BlockSpec((16, D), lambda i: (jnp.maximum(i * hb - 1, 0), col))
    nxt = pl.BlockSpec((16, D), lambda i: (jnp.minimum((i + 1) * hb, last), col))
    return prev, nxt


def _shift_rows(u, above, below):
    tl = u.shape[0]
    row = lax.broadcasted_iota(jnp.int32, u.shape, 0)
    dn = jnp.where(row == 0, above, pltpu.roll(u, 1, 0))
    up = jnp.where(row == tl - 1, below, pltpu.roll(u, tl - 1, 0))
    return dn, up


def _conv_gate_fwd(p, conv_w, conv_b, D):
    L = p.shape[0]
    tl = min(256, L)
    nt = L // tl

    def body(h_ref, bg_ref, cg_ref, za_ref, hp_ref, hn_ref, cp_ref, cn_ref, w_ref, b_ref, o_ref):
        i = pl.program_id(0)
        u = cg_ref[...].astype(F32) * h_ref[...].astype(F32)
        above = cp_ref[15:16, :].astype(F32) * hp_ref[15:16, :].astype(F32)
        below = cn_ref[0:1, :].astype(F32) * hn_ref[0:1, :].astype(F32)
        above = jnp.where(i == 0, 0.0, above)
        below = jnp.where(i == nt - 1, 0.0, below)
        dn, up = _shift_rows(u, above, below)
        co = w_ref[0:1, :] * dn + w_ref[1:2, :] * u + w_ref[2:3, :] * up + b_ref[...]
        za = za_ref[...].astype(F32)
        o_ref[...] = (za * _sigmoid(za) * bg_ref[...].astype(F32) * co).astype(BF16)

    main = lambda col: pl.BlockSpec((tl, D), lambda i: (i, col))
    hp, hn = _halo_specs(tl, L, D, 0)
    cp, cn = _halo_specs(tl, L, D, 2)
    return pl.pallas_call(
        body, name="conv_gate_fwd", grid=(nt,),
        in_specs=[main(0), main(1), main(2), main(3), hp, hn, cp, cn,
                  pl.BlockSpec((8, D), lambda i: (0, 0)), _vec_spec(D)],
        out_specs=pl.BlockSpec((tl, D), lambda i: (i, 0)),
        out_shape=jax.ShapeDtypeStruct((L, D), BF16),
        compiler_params=_cparams(("parallel",)),
    )(p, p, p, p, p, p, p, p, conv_w, conv_b)


def _rope_tables(L):
    pos = jnp.arange(L)
    row = (pos // GRID_W).astype(F32)
    col = (pos % GRID_W).astype(F32)
    nf = DK // 4
    inv = ROPE_BASE ** (-jnp.arange(nf, dtype=F32) / nf)
    ang = jnp.concatenate([row[:, None] * inv, col[:, None] * inv], axis=-1)
    cos, sin = jnp.cos(ang), jnp.sin(ang)
    return jnp.concatenate([cos, cos, cos, cos], axis=-1), jnp.concatenate([-sin, sin, -sin, sin], axis=-1)


def _rope_fwd(pqk, c2, s2, D):
    L = pqk.shape[0]
    W = D // 2
    tl = min(256, L)

    def body(q_ref, k_ref, c_ref, s_ref, qo_ref, ko_ref):
        c, s = c_ref[...], s_ref[...]
        for pr in range(W // 128):
            ps = slice(pr * 128, (pr + 1) * 128)
            q = q_ref[:, ps]
            k = k_ref[:, ps] * K_SCALE
            qo_ref[:, ps] = (q * c + _swap_halves(q) * s).astype(BF16)
            ko_ref[:, ps] = (k * c + _swap_halves(k) * s).astype(BF16)

    blk = lambda col: pl.BlockSpec((tl, W), lambda i: (i, col))
    tab = pl.BlockSpec((tl, 128), lambda i: (i, 0))
    return pl.pallas_call(
        body, name="rope_fwd", grid=(L // tl,),
        in_specs=[blk(0), blk(1), tab, tab],
        out_specs=[blk(0), blk(0)],
        out_shape=[jax.ShapeDtypeStruct((L, W), BF16)] * 2,
        compiler_params=_cparams(("parallel",)),
    )(pqk, pqk, c2, s2)


def _smem_spec():
    return pl.BlockSpec(memory_space=pltpu.SMEM)


def _pair_select(e0, e1):
    row = lax.broadcasted_iota(jnp.int32, e0.shape, 0)
    return jnp.where(row < DK, e0, e1)


def _head_lane_mask(shape, e):
    lane = lax.broadcasted_iota(jnp.int32, shape, 1)
    return (lane < DK) if e == 0 else (lane >= DK)


def _ctx_states(pc, pqk_c, lg, D):
    Lc = pc.shape[0]
    H = D // DV

    def body(lg_ref, k_ref, v_ref, s_ref):
        m = lax.broadcasted_iota(jnp.int32, (Lc, DV), 0).astype(F32)
        for pr in range(H // 2):
            k2 = k_ref[:, pr * 128:(pr + 1) * 128].astype(F32) * K_SCALE
            res = [[None, None], [None, None]]
            for e in range(2):
                h = 2 * pr + e
                v = v_ref[:, h * DV:(h + 1) * DV]
                dec_f = jnp.exp(lg_ref[0, h] * (Lc - 1.0 - m))
                dec_b = jnp.exp(lg_ref[1, h] * m)
                res[0][e] = _dot_tn((k2 * dec_f).astype(BF16), v)
                res[1][e] = _dot_tn((k2 * dec_b).astype(BF16), v)
            s_ref[0, pr] = _pair_select(res[0][0], res[0][1])
            s_ref[1, pr] = _pair_select(res[1][0], res[1][1])

    return pl.pallas_call(
        body, name="ctx_states", grid=(1,),
        in_specs=[_smem_spec(), pl.BlockSpec((Lc, D // 2), lambda i: (0, 1)), pl.BlockSpec((Lc, D), lambda i: (0, 5))],
        out_specs=pl.BlockSpec((2, H // 2, 128, 128), lambda i: (0, 0, 0, 0)),
        out_shape=jax.ShapeDtypeStruct((2, H // 2, 128, 128), F32),
    )(lg, pqk_c, pc)


T_M, T_MT = 0, 1
T_MF1, T_MB1 = 2, 3
T_QF, T_QB = 4, 5
T_KF, T_KB = 6, 7


def _decay_tables(lg, H):
    def body(lg_ref, t_ref):
        h = pl.program_id(0)
        lgf, lgb = lg_ref[0, h], lg_ref[1, h]
        i = lax.broadcasted_iota(jnp.int32, (CHUNK, CHUNK), 0).astype(F32)
        j = lax.broadcasted_iota(jnp.int32, (CHUNK, CHUNK), 1).astype(F32)
        d = i - j
        mf = jnp.where(d > 0, jnp.exp(lgf * jnp.maximum(d, 0.0)), 0.0)
        mb = jnp.where(d < 0, jnp.exp(lgb * jnp.maximum(-d, 0.0)), 0.0)
        mf_t = jnp.where(d < 0, jnp.exp(lgf * jnp.maximum(-d, 0.0)), 0.0)
        mb_t = jnp.where(d > 0, jnp.exp(lgb * jnp.maximum(d, 0.0)), 0.0)
        diag = jnp.where(d == 0, 2.0, 0.0)
        t_ref[0, T_M] = mf + mb + diag
        t_ref[0, T_MT] = mf_t + mb_t + diag
        t_ref[0, T_MF1] = mf * d
        t_ref[0, T_MB1] = mb * (-d)
        t_ref[0, T_QF] = jnp.exp(lgf * (i + 1.0))
        t_ref[0, T_QB] = jnp.exp(lgb * (CHUNK - i))
        t_ref[0, T_KF] = jnp.exp(lgf * (CHUNK - 1.0 - i))
        t_ref[0, T_KB] = jnp.exp(lgb * i)

    return pl.pallas_call(
        body, name="decay_tables", grid=(H,), in_specs=[_smem_spec()],
        out_specs=pl.BlockSpec((1, 8, CHUNK, CHUNK), lambda h: (h, 0, 0, 0)),
        out_shape=jax.ShapeDtypeStruct((H, 8, CHUNK, CHUNK), F32),
    )(lg)


def _tab_spec(H):
    return pl.BlockSpec((H, 8, CHUNK, CHUNK), lambda n: (0, 0, 0, 0))


def _chunk_decay(tab_ref, h):
    return tab_ref[h, T_QF, CHUNK - 1:CHUNK, :], tab_ref[h, T_QB, 0:1, :]


def _ret_states(kr, p, s0, tab, D):
    L = kr.shape[0]
    H = D // DV
    N = L // CHUNK
    HP = H // 2

    def body(tab_ref, kf_ref, kb_ref, vf_ref, vb_ref, s0_ref, sf_out, sb_out, sf, sb):
        n = pl.program_id(0)

        @pl.when(n == 0)
        def _():
            sf[...] = s0_ref[0]
            sb[...] = s0_ref[1]

        sf_out[0] = sf[...]
        sb_out[0] = sb[...]
        for pr in range(HP):
            kf2 = kf_ref[:, pr * 128:(pr + 1) * 128].astype(F32)
            kb2 = kb_ref[:, pr * 128:(pr + 1) * 128].astype(F32)
            inc_f, inc_b, gf, gb = [], [], [], []
            for e in range(2):
                h = 2 * pr + e
                inc_f.append(_dot_tn((kf2 * tab_ref[h, T_KF]).astype(BF16), vf_ref[:, h * DV:(h + 1) * DV]))
                inc_b.append(_dot_tn((kb2 * tab_ref[h, T_KB]).astype(BF16), vb_ref[:, h * DV:(h + 1) * DV]))
                cf, cb = _chunk_decay(tab_ref, h)
                gf.append(jnp.broadcast_to(cf, (128, 128)))
                gb.append(jnp.broadcast_to(cb, (128, 128)))
            sf[pr] = _pair_select(gf[0], gf[1]) * sf[pr] + _pair_select(inc_f[0], inc_f[1])
            sb[pr] = _pair_select(gb[0], gb[1]) * sb[pr] + _pair_select(inc_b[0], inc_b[1])

    st = jax.ShapeDtypeStruct((N, HP, 128, 128), F32)
    return pl.pallas_call(
        body, name="ret_states", grid=(N,),
        in_specs=[_tab_spec(H),
                  pl.BlockSpec((CHUNK, D // 2), lambda n: (n, 0)),
                  pl.BlockSpec((CHUNK, D // 2), lambda n: (N - 1 - n, 0)),
                  pl.BlockSpec((CHUNK, D), lambda n: (n, 5)),
                  pl.BlockSpec((CHUNK, D), lambda n: (N - 1 - n, 5)),
                  pl.BlockSpec((2, HP, 128, 128), lambda n: (0, 0, 0, 0))],
        out_specs=[pl.BlockSpec((1, HP, 128, 128), lambda n: (n, 0, 0, 0)),
                   pl.BlockSpec((1, HP, 128, 128), lambda n: (N - 1 - n, 0, 0, 0))],
        out_shape=[st, st],
        scratch_shapes=[pltpu.VMEM((HP, 128, 128), F32), pltpu.VMEM((HP, 128, 128), F32)],
        compiler_params=_cparams(("arbitrary",)),
    )(tab, kr, kr, p, p, s0)


def _ret_out(qr, kr, p, sf_prev, sb_prev, gn_w, tab, D):
    L = qr.shape[0]
    H = D // DV
    N = L // CHUNK
    HP = H // 2

    def body(tab_ref, q_ref, k_ref, v_ref, zb_ref, sf_ref, sb_ref, gn_ref, o_ref, yb_ref):
        for pr in range(HP):
            q2 = q_ref[:, pr * 128:(pr + 1) * 128]
            k2 = k_ref[:, pr * 128:(pr + 1) * 128]
            sfp = sf_ref[0, pr].astype(BF16)
            sbp = sb_ref[0, pr].astype(BF16)
            for e in range(2):
                h = 2 * pr + e
                sl = slice(h * DV, (h + 1) * DV)
                qm = jnp.where(_head_lane_mask(q2.shape, e), q2, jnp.zeros_like(q2))
                a = (_dot_nt(qm, k2) * tab_ref[h, T_M]).astype(BF16)
                qf = qm.astype(F32)
                o = _dot(a, v_ref[:, sl])
                o += _dot((qf * tab_ref[h, T_QF]).astype(BF16), sfp)
                o += _dot((qf * tab_ref[h, T_QB]).astype(BF16), sbp)
                o_ref[:, sl] = o
                mu = jnp.mean(o, axis=-1, keepdims=True)
                oc = o - mu
                rstd = lax.rsqrt(jnp.mean(oc * oc, axis=-1, keepdims=True) + EPS)
                zb = zb_ref[:, sl].astype(F32)
                yb_ref[:, sl] = (zb * _sigmoid(zb) * (oc * rstd * gn_ref[:, sl])).astype(BF16)

    return pl.pallas_call(
        body, name="ret_out", grid=(N,),
        in_specs=[_tab_spec(H),
                  pl.BlockSpec((CHUNK, D // 2), lambda n: (n, 0)),
                  pl.BlockSpec((CHUNK, D // 2), lambda n: (n, 0)),
                  pl.BlockSpec((CHUNK, D), lambda n: (n, 5)),
                  pl.BlockSpec((CHUNK, D), lambda n: (n, 6)),
                  pl.BlockSpec((1, HP, 128, 128), lambda n: (n, 0, 0, 0)),
                  pl.BlockSpec((1, HP, 128, 128), lambda n: (n, 0, 0, 0)),
                  _vec_spec(D)],
        out_specs=[pl.BlockSpec((CHUNK, D), lambda n: (n, 0)), pl.BlockSpec((CHUNK, D), lambda n: (n, 0))],
        out_shape=[jax.ShapeDtypeStruct((L, D), F32), jax.ShapeDtypeStruct((L, D), BF16)],
        compiler_params=_cparams(("parallel",)),
    )(tab, qr, kr, p, p, sf_prev, sb_prev, gn_w)


def _mid(ya, yb, p, x, tgt, w3, g, fw, D):
    L = x.shape[0]
    tm = min(256, L)
    nt = L // tm

    def body(ya_ref, yb_ref, ga_ref, gb_ref, x_ref, t_ref, w_hbm, g_ref, fw_ref,
             dx1_ref, dya_ref, dyb_ref, dgab_ref, dw_hbm, st_ref, w_vm, dw_acc, sem):
        i = pl.program_id(0)

        @pl.when(i == 0)
        def _():
            cp = pltpu.make_async_copy(w_hbm, w_vm, sem)
            cp.start()
            dw_acc[...] = jnp.zeros_like(dw_acc)
            st_ref[...] = jnp.zeros_like(st_ref)
            cp.wait()

        ya_b, yb_b = ya_ref[...], yb_ref[...]
        y_a = _dot(ya_b, w_vm[0])
        y_b = _dot(yb_b, w_vm[1])
        sga = _sigmoid(ga_ref[...].astype(F32))
        sgb = _sigmoid(gb_ref[...].astype(F32))
        mix_b = (sga * y_a + sgb * y_b).astype(BF16)
        y_x = _dot(mix_b, w_vm[2])
        gvec, fwv = g_ref[...], fw_ref[...]
        x1 = x_ref[...] + gvec * y_x
        r1 = lax.rsqrt(jnp.mean(x1 * x1, axis=-1, keepdims=True) + EPS)
        xh = x1 * r1
        diff = xh * fwv - t_ref[...]
        dout = diff * (1.0 / D)
        dxh = dout * fwv
        dx1 = r1 * (dxh - xh * jnp.mean(dxh * xh, axis=-1, keepdims=True))
        dx1_ref[...] = dx1
        st_ref[0:1, :] += jnp.sum(dout * xh, axis=0, keepdims=True)
        st_ref[1:2, :] += jnp.sum(dx1 * y_x, axis=0, keepdims=True)
        st_ref[2:3, :] += jnp.sum(diff * diff, axis=0, keepdims=True)
        dyx_b = (dx1 * gvec).astype(BF16)
        dmix = _dot_nt(dyx_b, w_vm[2])
        dw_acc[2] += _dot_tn(mix_b, dyx_b)
        dya_b = (dmix * sga).astype(BF16)
        dyb_b = (dmix * sgb).astype(BF16)
        dgab_ref[:, 0:D] = (dmix * y_a * sga * (1.0 - sga)).astype(BF16)
        dgab_ref[:, D:2 * D] = (dmix * y_b * sgb * (1.0 - sgb)).astype(BF16)
        dya_ref[...] = _dot_nt(dya_b, w_vm[0])
        dyb_ref[...] = _dot_nt(dyb_b, w_vm[1])
        dw_acc[0] += _dot_tn(ya_b, dya_b)
        dw_acc[1] += _dot_tn(yb_b, dyb_b)

        @pl.when(i == nt - 1)
        def _():
            out = pltpu.make_async_copy(dw_acc, dw_hbm, sem)
            out.start()
            out.wait()

    row = lambda col: pl.BlockSpec((tm, D), lambda i: (i, col))
    any_spec = pl.BlockSpec(memory_space=pl.ANY)
    f32o = jax.ShapeDtypeStruct((L, D), F32)
    return pl.pallas_call(
        body, name="mid", grid=(nt,),
        in_specs=[row(0), row(0), row(7), row(8), row(0), row(0), any_spec, _vec_spec(D), _vec_spec(D)],
        out_specs=[row(0), row(0), row(0), pl.BlockSpec((tm, 2 * D), lambda i: (i, 0)), any_spec,
                   pl.BlockSpec((8, D), lambda i: (0, 0))],
        out_shape=[f32o, f32o, f32o, jax.ShapeDtypeStruct((L, 2 * D), BF16),
                   jax.ShapeDtypeStruct((3, D, D), F32), jax.ShapeDtypeStruct((8, D), F32)],
        scratch_shapes=[pltpu.VMEM((3, D, D), BF16), pltpu.VMEM((3, D, D), F32), pltpu.SemaphoreType.DMA],
        compiler_params=_cparams(("arbitrary",), VMEM_LIMIT),
    )(ya, yb, p, p, x, tgt, w3, g, fw)


def _conv_bwd(dya, p, conv_w, conv_b, D, exchange=None):
    L = p.shape[0]
    tl = min(256, L)
    nt = L // tl

    def body(d_ref, h_ref, bg_ref, cg_ref, za_ref,
             dp_ref, dn_ref, hp_ref, hn_ref, bp_ref, bn_ref, cp_ref, cn_ref, zp_ref, zn_ref,
             w_ref, b_ref, dc_ref, st_ref):
        i = pl.program_id(0)

        @pl.when(i == 0)
        def _():
            st_ref[...] = jnp.zeros_like(st_ref)

        first, last = i == 0, i == nt - 1
        h = h_ref[...].astype(F32)
        cg = cg_ref[...].astype(F32)
        bg = bg_ref[...].astype(F32)
        za = za_ref[...].astype(F32)
        dy = d_ref[...].astype(F32)
        u = cg * h
        u_above = jnp.where(first, 0.0, cp_ref[15:16, :].astype(F32) * hp_ref[15:16, :].astype(F32))
        u_below = jnp.where(last, 0.0, cn_ref[0:1, :].astype(F32) * hn_ref[0:1, :].astype(F32))
        u_dn, u_up = _shift_rows(u, u_above, u_below)
        w0, w1, w2 = w_ref[0:1, :], w_ref[1:2, :], w_ref[2:3, :]
        co = w0 * u_dn + w1 * u + w2 * u_up + b_ref[...]
        sz = _sigmoid(za)
        silu = za * sz
        dc_ref[:, 3 * D:4 * D] = (dy * bg * co * (sz * (1.0 + za * (1.0 - sz)))).astype(BF16)
        dc_ref[:, D:2 * D] = (dy * silu * co).astype(BF16)
        dco = dy * silu * bg

        def edge(dr, zr, br, r):
            z = zr[r:r + 1, :].astype(F32)
            return dr[r:r + 1, :].astype(F32) * (z * _sigmoid(z)) * br[r:r + 1, :].astype(F32)

        dco_above = jnp.where(first, 0.0, edge(dp_ref, zp_ref, bp_ref, 15))
        dco_below = jnp.where(last, 0.0, edge(dn_ref, zn_ref, bn_ref, 0))
        dco_dn, dco_up = _shift_rows(dco, dco_above, dco_below)
        du = w0 * dco_up + w1 * dco + w2 * dco_dn
        dc_ref[:, 2 * D:3 * D] = (du * h).astype(BF16)
        dc_ref[:, 0:D] = (du * cg).astype(BF16)
        st_ref[0:1, :] += jnp.sum(dco * u_dn, axis=0, keepdims=True)
        st_ref[1:2, :] += jnp.sum(dco * u, axis=0, keepdims=True)
        st_ref[2:3, :] += jnp.sum(dco * u_up, axis=0, keepdims=True)
        st_ref[3:4, :] += jnp.sum(dco, axis=0, keepdims=True)

    main = lambda col: pl.BlockSpec((tl, D), lambda i: (i, col))
    halos = []
    for col in (0, 0, 1, 2, 3):
        halos.extend(_halo_specs(tl, L, D, col))
    return _riding_call(
        body, exchange, nt, name="conv_bwd",
        args=(dya, p, p, p, p, dya, dya, p, p, p, p, p, p, p, p, conv_w, conv_b),
        in_specs=[main(0), main(0), main(1), main(2), main(3)] + halos
                 + [pl.BlockSpec((8, D), lambda i: (0, 0)), _vec_spec(D)],
        out_specs=[pl.BlockSpec((tl, 4 * D), lambda i: (i, 0)), pl.BlockSpec((8, D), lambda i: (0, 0))],
        out_shape=[jax.ShapeDtypeStruct((L, 4 * D), BF16), jax.ShapeDtypeStruct((8, D), F32)],
        cparams=_cparams(("arbitrary",)))


def _ret_bwd_pre(dyb, p, o, gn_w, D):
    L = o.shape[0]
    H = D // DV
    tl = min(256, L)

    def body(d_ref, zb_ref, o_ref, gn_ref, do_ref, dzb_ref, st_ref):
        @pl.when(pl.program_id(0) == 0)
        def _():
            st_ref[...] = jnp.zeros_like(st_ref)

        for h in range(H):
            sl = slice(h * DV, (h + 1) * DV)
            ov = o_ref[:, sl]
            mu = jnp.mean(ov, axis=-1, keepdims=True)
            oc = ov - mu
            rstd = lax.rsqrt(jnp.mean(oc * oc, axis=-1, keepdims=True) + EPS)
            rn = oc * rstd
            gw = gn_ref[:, sl]
            zb = zb_ref[:, sl].astype(F32)
            sz = _sigmoid(zb)
            dy = d_ref[:, sl].astype(F32)
            dzb_ref[:, sl] = (dy * (rn * gw) * (sz * (1.0 + zb * (1.0 - sz)))).astype(BF16)
            dretn = dy * (zb * sz)
            st_ref[0:1, sl] += jnp.sum(dretn * rn, axis=0, keepdims=True)
            drn = dretn * gw
            do = rstd * (drn - jnp.mean(drn, axis=-1, keepdims=True)
                         - rn * jnp.mean(drn * rn, axis=-1, keepdims=True))
            do_ref[:, sl] = do.astype(BF16)

    main = lambda col: pl.BlockSpec((tl, D), lambda i: (i, col))
    bfo = jax.ShapeDtypeStruct((L, D), BF16)
    return pl.pallas_call(
        body, name="ret_bwd_pre", grid=(L // tl,),
        in_specs=[main(0), main(6), main(0), _vec_spec(D)],
        out_specs=[main(0), main(0), pl.BlockSpec((8, D), lambda i: (0, 0))],
        out_shape=[bfo, bfo, jax.ShapeDtypeStruct((8, D), F32)],
        compiler_params=_cparams(("arbitrary",)),
    )(dyb, p, o, gn_w)


def _ret_bwd_states(qr, do, tab, D):
    L = qr.shape[0]
    H = D // DV
    N = L // CHUNK
    HP = H // 2

    def body(tab_ref, qf_ref, qb_ref, dof_ref, dob_ref, dsf_out, dsb_out, ds0_out, dsf, dsb):
        n = pl.program_id(0)

        @pl.when(n == 0)
        def _():
            dsf[...] = jnp.zeros_like(dsf)
            dsb[...] = jnp.zeros_like(dsb)

        dsf_out[0] = dsf[...]
        dsb_out[0] = dsb[...]
        for pr in range(HP):
            qf2 = qf_ref[:, pr * 128:(pr + 1) * 128].astype(F32)
            qb2 = qb_ref[:, pr * 128:(pr + 1) * 128].astype(F32)
            inc_f, inc_b, gf, gb = [], [], [], []
            for e in range(2):
                h = 2 * pr + e
                inc_f.append(_dot_tn((qf2 * tab_ref[h, T_QF]).astype(BF16), dof_ref[:, h * DV:(h + 1) * DV]))
                inc_b.append(_dot_tn((qb2 * tab_ref[h, T_QB]).astype(BF16), dob_ref[:, h * DV:(h + 1) * DV]))
                cf, cb = _chunk_decay(tab_ref, h)
                gf.append(jnp.broadcast_to(cf, (128, 128)))
                gb.append(jnp.broadcast_to(cb, (128, 128)))
            dsf[pr] = _pair_select(gf[0], gf[1]) * dsf[pr] + _pair_select(inc_f[0], inc_f[1])
            dsb[pr] = _pair_select(gb[0], gb[1]) * dsb[pr] + _pair_select(inc_b[0], inc_b[1])

        @pl.when(n == N - 1)
        def _():
            ds0_out[0] = dsf[...]
            ds0_out[1] = dsb[...]

    st = jax.ShapeDtypeStruct((N, HP, 128, 128), F32)
    return pl.pallas_call(
        body, name="ret_bwd_states", grid=(N,),
        in_specs=[_tab_spec(H),
                  pl.BlockSpec((CHUNK, D // 2), lambda n: (N - 1 - n, 0)),
                  pl.BlockSpec((CHUNK, D // 2), lambda n: (n, 0)),
                  pl.BlockSpec((CHUNK, D), lambda n: (N - 1 - n, 0)),
                  pl.BlockSpec((CHUNK, D), lambda n: (n, 0))],
        out_specs=[pl.BlockSpec((1, HP, 128, 128), lambda n: (N - 1 - n, 0, 0, 0)),
                   pl.BlockSpec((1, HP, 128, 128), lambda n: (n, 0, 0, 0)),
                   pl.BlockSpec((2, HP, 128, 128), lambda n: (0, 0, 0, 0))],
        out_shape=[st, st, jax.ShapeDtypeStruct((2, HP, 128, 128), F32)],
        scratch_shapes=[pltpu.VMEM((HP, 128, 128), F32), pltpu.VMEM((HP, 128, 128), F32)],
        compiler_params=_cparams(("arbitrary",)),
    )(tab, qr, qr, do, do)


def _ret_bwd_main(qr, kr, p, do, sf_prev, sb_prev, dsf, dsb, c2, s2, tab, D, exchange=None):
    L = qr.shape[0]
    H = D // DV
    N = L // CHUNK
    HP = H // 2
    W = D // 2

    def body(tab_ref, q_ref, k_ref, v_ref, do_ref, sf_ref, sb_ref, dsf_ref, dsb_ref, c_ref, s_ref,
             dr_ref, st_ref):
        @pl.when(pl.program_id(0) == 0)
        def _():
            st_ref[...] = jnp.zeros_like(st_ref)

        dqk_ref = dr_ref.at[:, 0:D]
        dv_ref = dr_ref.at[:, D:2 * D]
        i = lax.broadcasted_iota(jnp.int32, (CHUNK, 128), 0).astype(F32)
        lane = lax.broadcasted_iota(jnp.int32, (1, 128), 1)
        rowid = lax.broadcasted_iota(jnp.int32, (128, 128), 0)
        c, s = c_ref[...], s_ref[...]
        acc_f = jnp.zeros((1, 128), F32)
        acc_b = jnp.zeros((1, 128), F32)
        for pr in range(HP):
            ps = slice(pr * 128, (pr + 1) * 128)
            q2, k2 = q_ref[:, ps], k_ref[:, ps]
            sf32, sb32 = sf_ref[0, pr], sb_ref[0, pr]
            dsf32, dsb32 = dsf_ref[0, pr], dsb_ref[0, pr]
            sfp, sbp = sf32.astype(BF16), sb32.astype(BF16)
            dsfp, dsbp = dsf32.astype(BF16), dsb32.astype(BF16)
            dq2 = jnp.zeros((CHUNK, 128), F32)
            dk2 = jnp.zeros((CHUNK, 128), F32)
            for e in range(2):
                h = 2 * pr + e
                sl = slice(h * DV, (h + 1) * DV)
                hm = _head_lane_mask(q2.shape, e)
                qm = jnp.where(hm, q2, jnp.zeros_like(q2))
                km = jnp.where(hm, k2, jnp.zeros_like(k2))
                qf, kf = qm.astype(F32), km.astype(F32)
                v, do = v_ref[:, sl], do_ref[:, sl]
                vf, dof = v.astype(F32), do.astype(F32)
                m_t = tab_ref[h, T_MT]
                sc = _dot_nt(qm, k2)
                dpm = _dot_nt(do, v)
                dsc = (dpm * tab_ref[h, T_M]).astype(BF16)
                a_t = (_dot_nt(km, q2) * m_t).astype(BF16)
                dsc_t = (_dot_nt(v, do) * m_t).astype(BF16)
                dq_f, dq_b = tab_ref[h, T_QF], tab_ref[h, T_QB]
                dk_f, dk_b = tab_ref[h, T_KF], tab_ref[h, T_KB]
                dq = _dot(dsc, km)
                dq += jnp.where(hm, dq_f * _dot_nt(do, sfp) + dq_b * _dot_nt(do, sbp), 0.0)
                dk = _dot(dsc_t, qm)
                dk += jnp.where(hm, dk_f * _dot_nt(v, dsfp) + dk_b * _dot_nt(v, dsbp), 0.0)
                kdf = _dot((kf * dk_f).astype(BF16), dsfp)
                kdb = _dot((kf * dk_b).astype(BF16), dsbp)
                dv_ref[:, sl] = (_dot(a_t, do) + kdf + kdb).astype(BF16)
                dq2 += dq
                dk2 += dk
                xf = _dot((qf * dq_f).astype(BF16), sfp)
                xb = _dot((qf * dq_b).astype(BF16), sbp)
                pair = (rowid < DK) if e == 0 else (rowid >= DK)
                gcf, gcb = tab_ref[h, T_QF, CHUNK - 1:CHUNK, 0:1], tab_ref[h, T_QB, 0:1, 0:1]
                scdp = sc * dpm
                tf = _sum_all(scdp * tab_ref[h, T_MF1]) + _sum_all(xf * dof * (i + 1.0)) \
                    + _sum_all(kdf * vf * (CHUNK - 1.0 - i)) \
                    + CHUNK * gcf * _sum_all(jnp.where(pair, dsf32 * sf32, 0.0))
                tb = _sum_all(scdp * tab_ref[h, T_MB1]) + _sum_all(xb * dof * (CHUNK - i)) \
                    + _sum_all(kdb * vf * i) \
                    + CHUNK * gcb * _sum_all(jnp.where(pair, dsb32 * sb32, 0.0))
                acc_f += jnp.where(lane == h, tf, 0.0)
                acc_b += jnp.where(lane == h, tb, 0.0)
            dqk_ref[:, ps] = (dq2 * c - _swap_halves(dq2) * s).astype(BF16)
            dqk_ref[:, W + pr * 128:W + (pr + 1) * 128] = ((dk2 * c - _swap_halves(dk2) * s) * K_SCALE).astype(BF16)
        st_ref[0:1, :] += acc_f
        st_ref[1:2, :] += acc_b

    st_spec = pl.BlockSpec((1, HP, 128, 128), lambda n: (n, 0, 0, 0))
    half = pl.BlockSpec((CHUNK, W), lambda n: (n, 0))
    rope = pl.BlockSpec((CHUNK, 128), lambda n: (n, 0))
    return _riding_call(
        body, exchange, N, name="ret_bwd_main",
        args=(tab, qr, kr, p, do, sf_prev, sb_prev, dsf, dsb, c2, s2),
        in_specs=[_tab_spec(H), half, half,
                  pl.BlockSpec((CHUNK, D), lambda n: (n, 5)),
                  pl.BlockSpec((CHUNK, D), lambda n: (n, 0)),
                  st_spec, st_spec, st_spec, st_spec, rope, rope],
        out_specs=[pl.BlockSpec((CHUNK, 2 * D), lambda n: (n, 0)),
                   pl.BlockSpec((8, 128), lambda n: (0, 0))],
        out_shape=[jax.ShapeDtypeStruct((L, 2 * D), BF16), jax.ShapeDtypeStruct((8, 128), F32)],
        cparams=_cparams(("arbitrary",)))


def _ctx_bwd(pc, pqk_c, ds0, lg, D):
    Lc = pc.shape[0]
    H = D // DV
    HP = H // 2
    W = D // 2

    def body(lg_ref, k_ref, v_ref, ds_ref, dr_ref, st_ref):
        dqk_ref = dr_ref.at[:, 0:D]
        dv_ref = dr_ref.at[:, D:2 * D]
        m = lax.broadcasted_iota(jnp.int32, (Lc, 128), 0).astype(F32)
        lane = lax.broadcasted_iota(jnp.int32, (1, 128), 1)
        acc_f = jnp.zeros((1, 128), F32)
        acc_b = jnp.zeros((1, 128), F32)
        dqk_ref[:, 0:W] = jnp.zeros((Lc, W), BF16)
        for pr in range(HP):
            ps = slice(pr * 128, (pr + 1) * 128)
            k2 = k_ref[:, ps].astype(F32) * K_SCALE
            dsfp, dsbp = ds_ref[0, pr].astype(BF16), ds_ref[1, pr].astype(BF16)
            dk2 = jnp.zeros((Lc, 128), F32)
            for e in range(2):
                h = 2 * pr + e
                sl = slice(h * DV, (h + 1) * DV)
                hm = _head_lane_mask(k2.shape, e)
                km = jnp.where(hm, k2, 0.0)
                v = v_ref[:, sl]
                vf = v.astype(F32)
                dec_f = jnp.exp(lg_ref[0, h] * (Lc - 1.0 - m))
                dec_b = jnp.exp(lg_ref[1, h] * m)
                kdf = _dot((km * dec_f).astype(BF16), dsfp)
                kdb = _dot((km * dec_b).astype(BF16), dsbp)
                dv_ref[:, sl] = (kdf + kdb).astype(BF16)
                dk2 += jnp.where(hm, dec_f * _dot_nt(v, dsfp) + dec_b * _dot_nt(v, dsbp), 0.0)
                acc_f += jnp.where(lane == h, _sum_all(kdf * vf * (Lc - 1.0 - m)), 0.0)
                acc_b += jnp.where(lane == h, _sum_all(kdb * vf * m), 0.0)
            dqk_ref[:, W + pr * 128:W + (pr + 1) * 128] = (dk2 * K_SCALE).astype(BF16)
        st_ref[...] = jnp.zeros_like(st_ref)
        st_ref[0:1, :] = acc_f
        st_ref[1:2, :] = acc_b

    return pl.pallas_call(
        body, name="ctx_bwd", grid=(1,),
        in_specs=[_smem_spec(), pl.BlockSpec((Lc, W), lambda i: (0, 1)), pl.BlockSpec((Lc, D), lambda i: (0, 5)),
                  pl.BlockSpec((2, HP, 128, 128), lambda i: (0, 0, 0, 0))],
        out_specs=[pl.BlockSpec((Lc, 2 * D), lambda i: (0, 0)), pl.BlockSpec((8, 128), lambda i: (0, 0))],
        out_shape=[jax.ShapeDtypeStruct((Lc, 2 * D), BF16), jax.ShapeDtypeStruct((8, 128), F32)],
    )(lg, pqk_c, pc, ds0)


class _Exchange(NamedTuple):
    inputs: tuple
    out_shapes: tuple
    n_copies: int
    build: Callable


def _exchange_parts(exchange):
    if exchange is None:
        return [], [], [], [], []
    n = exchange.n_copies
    return (list(exchange.inputs), [ANY] * len(exchange.inputs), list(exchange.out_shapes),
            [ANY] * len(exchange.out_shapes), [pltpu.SemaphoreType.DMA((n,)), pltpu.SemaphoreType.DMA((n,))])


def _riding_call(body, exchange, n_steps, *, args, in_specs, out_specs, out_shape, name, cparams, scratch=()):
    ex_args, ex_in_specs, ex_shapes, ex_out_specs, ex_scratch = _exchange_parts(exchange)
    n_in, n_out, n_sc = len(args), len(out_shape), len(scratch)

    def riding(*refs):
        k = n_in + len(ex_args)
        ins, ex_in = refs[:n_in], refs[n_in:k]
        outs, ex_out = refs[k:k + n_out], refs[k + n_out:k + n_out + len(ex_shapes)]
        k += n_out + len(ex_shapes)
        own_scratch, ex_sems = refs[k:k + n_sc], refs[k + n_sc:]
        step = pl.program_id(0)
        if exchange is not None:
            @pl.when(step == 0)
            def _():
                for rc in exchange.build(ex_in, ex_out, *ex_sems):
                    rc.start()
        body(*ins, *outs, *own_scratch)
        if exchange is not None:
            @pl.when(step == n_steps - 1)
            def _():
                for rc in exchange.build(ex_in, ex_out, *ex_sems):
                    rc.wait()

    return tuple(pl.pallas_call(
        riding, name=name, grid=(n_steps,),
        in_specs=list(in_specs) + ex_in_specs, out_specs=list(out_specs) + ex_out_specs,
        out_shape=list(out_shape) + ex_shapes, scratch_shapes=list(scratch) + ex_scratch,
        compiler_params=cparams,
    )(*args, *ex_args))


def _dxm(groups, col0, w, x, nw, sc, dx1, name, exchange=None):
    L, D = x.shape
    tm = min(256, L)
    nt = L // tm
    ng = len(groups)
    widths = [g.shape[1] for g in groups]
    wtot = sum(widths)
    with_dx = dx1 is not None
    ex_args, ex_in_specs, ex_shapes, ex_out_specs, ex_scratch = _exchange_parts(exchange)
    n_in = ng + 4 + (1 if with_dx else 0)
    n_out = 2 if with_dx else 1

    def body(*refs):
        group_refs = refs[:ng]
        w_hbm, x_ref, nw_ref, sc_ref = refs[ng:ng + 4]
        ex_in = refs[n_in:n_in + len(ex_args)]
        outs = refs[n_in + len(ex_args):]
        if with_dx:
            dx1_ref, gx_ref, st_ref = refs[ng + 4], outs[0], outs[1]
        else:
            st_ref = outs[0]
        ex_out = outs[n_out:n_out + len(ex_shapes)]
        w_vm, sem = outs[n_out + len(ex_shapes):n_out + len(ex_shapes) + 2]
        ex_sems = outs[n_out + len(ex_shapes) + 2:]
        i = pl.program_id(0)

        @pl.when(i == 0)
        def _():
            cp = pltpu.make_async_copy(w_hbm.at[:, col0 * D:col0 * D + wtot], w_vm, sem)
            cp.start()
            if exchange is not None:
                for rc in exchange.build(ex_in, ex_out, *ex_sems):
                    rc.start()
            st_ref[...] = jnp.zeros_like(st_ref)
            cp.wait()

        dxm, off = None, 0
        for g_ref, wd in zip(group_refs, widths):
            part = _dot_nt(g_ref[...], w_vm[:, off:off + wd])
            dxm = part if dxm is None else dxm + part
            off += wd

        xv = x_ref[...]
        r = lax.rsqrt(jnp.mean(xv * xv, axis=-1, keepdims=True) + EPS)
        xh = xv * r
        nwv = nw_ref[...]
        dxn = dxm * (1.0 + sc_ref[...])
        st_ref[0:1, :] += jnp.sum(dxm, axis=0, keepdims=True)
        st_ref[1:2, :] += jnp.sum(dxm * (xh * nwv), axis=0, keepdims=True)
        st_ref[2:3, :] += jnp.sum(dxn * xh, axis=0, keepdims=True)
        if with_dx:
            dxh = dxn * nwv
            gx_ref[...] = dx1_ref[...] + r * (dxh - xh * jnp.mean(dxh * xh, axis=-1, keepdims=True))

        if exchange is not None:
            @pl.when(i == nt - 1)
            def _():
                for rc in exchange.build(ex_in, ex_out, *ex_sems):
                    rc.wait()

    row = pl.BlockSpec((tm, D), lambda i: (i, 0))
    in_specs = [pl.BlockSpec((tm, wd), lambda i: (i, 0)) for wd in widths] + [ANY, row, _vec_spec(D), _vec_spec(D)]
    out_specs = [pl.BlockSpec((8, D), lambda i: (0, 0))]
    out_shape = [jax.ShapeDtypeStruct((8, D), F32)]
    args = list(groups) + [w, x, nw, sc]
    if with_dx:
        in_specs.append(row)
        out_specs.insert(0, row)
        out_shape.insert(0, jax.ShapeDtypeStruct((L, D), F32))
        args.append(dx1)
    res = pl.pallas_call(
        body, name=name, grid=(nt,),
        in_specs=in_specs + ex_in_specs, out_specs=out_specs + ex_out_specs, out_shape=out_shape + ex_shapes,
        scratch_shapes=[pltpu.VMEM((D, wtot), BF16), pltpu.SemaphoreType.DMA] + ex_scratch,
        compiler_params=_cparams(("arbitrary",), VMEM_LIMIT),
    )(*args, *ex_args)
    gx = res[0] if with_dx else None
    return (gx, res[n_out - 1], *res[n_out:])


DW_TN = 512


def _dw_in(hidx, xmt, groups, cmt, dr_c, D, name, exchange=None):
    L = xmt.shape[1]
    Lc = cmt.shape[1]
    Dh = D // 2
    tn = min(DW_TN, D)
    nblk = [g.shape[1] // tn for g in groups]
    starts = [sum(nblk[:g]) for g in range(len(groups))]
    ng = len(groups)
    nj = sum(nblk)
    ex_args, ex_in_specs, ex_shapes, ex_out_specs, ex_scratch = _exchange_parts(exchange)

    def body(*refs):
        h_ref, xt_hbm = refs[0], refs[1]
        group_refs = refs[2:2 + ng]
        ct_ref, drc_ref = refs[2 + ng:4 + ng]
        ex_in = refs[4 + ng:4 + ng + len(ex_args)]
        outs = refs[4 + ng + len(ex_args):]
        o_ref = outs[0]
        ex_out = outs[1:1 + len(ex_shapes)]
        xt_vm, sem = outs[1 + len(ex_shapes):3 + len(ex_shapes)]
        ex_sems = outs[3 + len(ex_shapes):]
        j = pl.program_id(0)

        @pl.when(j == 0)
        def _():
            rows = pl.ds(pl.multiple_of(h_ref[0] * Dh, Dh), Dh)
            cp = pltpu.make_async_copy(xt_hbm.at[rows, :], xt_vm, sem)
            cp.start()
            if exchange is not None:
                for rc in exchange.build(ex_in, ex_out, *ex_sems):
                    rc.start()
            cp.wait()

        for g in range(ng):
            @pl.when((j >= starts[g]) & (j < starts[g] + nblk[g]))
            def _(g=g):
                acc = _dot(xt_vm[...], group_refs[g][...])
                if g == 1:
                    acc += _dot(ct_ref[...], drc_ref[...])
                o_ref[...] = acc

        if exchange is not None:
            @pl.when(j == nj - 1)
            def _():
                for rc in exchange.build(ex_in, ex_out, *ex_sems):
                    rc.wait()

    def group_spec(g, rows):
        return pl.BlockSpec((rows, tn), lambda j, h: (0, jnp.clip(j - starts[g], 0, nblk[g] - 1)))

    res = pl.pallas_call(
        body, name=name,
        grid_spec=pltpu.PrefetchScalarGridSpec(
            num_scalar_prefetch=1, grid=(nj,),
            in_specs=[ANY] + [group_spec(g, L) for g in range(ng)]
                     + [pl.BlockSpec((Dh, Lc), lambda j, h: (h[0], 0)), group_spec(1, Lc)] + ex_in_specs,
            out_specs=[pl.BlockSpec((Dh, tn), lambda j, h: (0, j))] + ex_out_specs,
            scratch_shapes=[pltpu.VMEM((Dh, L), BF16), pltpu.SemaphoreType.DMA] + ex_scratch),
        out_shape=[jax.ShapeDtypeStruct((Dh, nj * tn), F32)] + ex_shapes,
        compiler_params=_cparams(("arbitrary",), VMEM_LIMIT),
    )(hidx, xmt, *groups, cmt, dr_c, *ex_args)
    return tuple(res)


def _local_step(x, ctx, tgt, mod_x, mod_c, norm_w, conv_w8, conv_b, lg, gn_w, fw, w_in, w3, csidx=None):
    L, D = x.shape
    sh_x, sc_x, g_x = mod_x[0:1], mod_x[1:2], mod_x[2:3]
    sh_c, sc_c = mod_c[0:1], mod_c[1:2]
    c2, s2 = _rope_tables(L)
    tab = _decay_tables(lg, D // DV)

    xm, xmt = _norm_mod(x, norm_w, sc_x, sh_x, "norm_mod_x")
    cm, cmt = _norm_mod(ctx, norm_w, sc_c, sh_c, "norm_mod_ctx")
    p, pqk = _in_proj(xm, w_in, "in_proj_x")
    pc, pqk_c = _in_proj(cm, w_in, "in_proj_ctx")
    ya = _conv_gate_fwd(p, conv_w8, conv_b, D)
    qr, kr = _rope_fwd(pqk, c2, s2, D)
    s0 = _ctx_states(pc, pqk_c, lg, D)
    sf_prev, sb_prev = _ret_states(kr, p, s0, tab, D)
    o, yb = _ret_out(qr, kr, p, sf_prev, sb_prev, gn_w, tab, D)
    dx1, dya, dyb, dgab, dw3, st_mid = _mid(ya, yb, p, x, tgt, w3, g_x, fw, D)
    reduce = csidx is not None
    dw3_5 = dw3.reshape(3, N_SHARD, 2, D // 8, D)
    dconv, st_conv, *ra_3 = _conv_bwd(dya, p, conv_w8, conv_b, D, _pair_exchange_w3(dw3_5) if reduce else None)
    do, dzb, st_gn = _ret_bwd_pre(dyb, p, o, gn_w, D)
    dsf, dsb, ds0 = _ret_bwd_states(qr, do, tab, D)
    cs_3 = _sum_pair_w3(csidx[0:1], dw3_5, ra_3[0]) if reduce else None
    dret, st_lg, *rb_3 = _ret_bwd_main(qr, kr, p, do, sf_prev, sb_prev, dsf, dsb, c2, s2, tab, D,
                                       _chips_exchange_w3(cs_3) if reduce else None)
    g_3 = _sum_chips_w3(csidx, cs_3, rb_3[0]) if reduce else dw3
    dret_c, st_lgc = _ctx_bwd(pc, pqk_c, ds0, lg, D)
    groups = (dconv, dret, dzb, dgab)
    _, st_c = _dxm((dret_c,), 4, w_in, ctx, norm_w, sc_c, None, "dxm_ctx")
    return groups, dret_c, xmt, cmt, dx1, sc_x, g_3, (st_mid, st_conv, st_gn, st_lg, st_lgc, st_c)


CHIP_FLIPS = (4, 2, 6)
ANY = pl.BlockSpec(memory_space=pl.ANY)
VMEM_FULL = pl.BlockSpec(memory_space=pltpu.VMEM)


def _position():
    return lax.axis_index("x"), lax.axis_index("y"), lax.axis_index("c")


def _peer(pos, k):
    x, y, c = pos
    return (1 - x if k & 4 else x, 1 - y if k & 2 else y, 1 - c if k & 1 else c)


def _dev_id(pos):
    return 4 * pos[0] + 2 * pos[1] + pos[2]


def _shard_of(pos):
    return 2 * pos[0] + pos[1]


def _remote(src, dst, send_sems, recv_sems, idx, to):
    return pltpu.make_async_remote_copy(src_ref=src, dst_ref=dst, send_sem=send_sems.at[idx],
                                        recv_sem=recv_sems.at[idx], device_id=to, device_id_type=MESH)


def _dot_f32(a, b):
    return jnp.dot(a, b, precision=lax.Precision.HIGHEST, preferred_element_type=F32)


def _silu(x):
    return x * _sigmoid(x)


def _fwd_small(c8, cctx8, ada_w, ada_b, conv_w8):
    D = c8.shape[1]
    Wm = ada_w.shape[1]
    Dq = conv_w8.shape[1]

    def body(c_ref, cc_ref, aw_ref, ab_ref, cw_ref, act_ref, mod_ref, cwf_ref,
             cbuf, pmine, pbuf, wbuf, s_c, r_c, s_p, r_p, s_w, r_w):
        pos = _position()
        me, s = _dev_id(pos), _shard_of(pos)
        cbuf[me] = c_ref[...]
        wbuf[s] = cw_ref[...]
        sends = [_remote(c_ref, cbuf.at[me], s_c, r_c, k - 1, _peer(pos, k)) for k in range(1, 8)]
        sends += [_remote(cw_ref, wbuf.at[s], s_w, r_w, j, _peer(pos, k)) for j, k in enumerate(CHIP_FLIPS)]
        for cp in sends:
            cp.start()
        for k in range(1, 8):
            _remote(c_ref, cbuf.at[_dev_id(_peer(pos, k))], s_c, r_c, k - 1, _peer(pos, k)).wait_recv()
        for d in range(N_DEV):
            act_ref[d:d + 1, :] = _silu(cbuf[d, 0:1, :])
        act_ref[8:9, :] = _silu(cc_ref[0:1, :])
        act_ref[9:16, :] = jnp.zeros((7, D), F32)
        part = _dot_f32(act_ref[...], aw_ref[...])
        pmine[...] = part
        pbuf[s] = part
        psend = [_remote(pmine, pbuf.at[s], s_p, r_p, j, _peer(pos, k)) for j, k in enumerate(CHIP_FLIPS)]
        for cp in psend:
            cp.start()
        for j, k in enumerate(CHIP_FLIPS):
            t = _shard_of(_peer(pos, k))
            _remote(pmine, pbuf.at[t], s_p, r_p, j, _peer(pos, k)).wait_recv()
            _remote(cw_ref, wbuf.at[t], s_w, r_w, j, _peer(pos, k)).wait_recv()
        for t in range(N_SHARD):
            mod_ref[:, t * Wm:(t + 1) * Wm] = pbuf[t] + ab_ref[:, t * Wm:(t + 1) * Wm]
            cwf_ref[:, t * Dq:(t + 1) * Dq] = wbuf[t]
        for cp in sends + psend:
            cp.wait_send()

    return pl.pallas_call(
        body, name="fwd_small",
        in_specs=[VMEM_FULL] * 5, out_specs=[VMEM_FULL] * 3,
        out_shape=[jax.ShapeDtypeStruct((16, D), F32), jax.ShapeDtypeStruct((16, 3 * D), F32),
                   jax.ShapeDtypeStruct((8, D), F32)],
        scratch_shapes=[pltpu.VMEM((N_DEV, 8, D), F32), pltpu.VMEM((16, Wm), F32),
                        pltpu.VMEM((N_SHARD, 16, Wm), F32), pltpu.VMEM((N_SHARD, 8, Dq), F32),
                        pltpu.SemaphoreType.DMA((7,)), pltpu.SemaphoreType.DMA((7,)),
                        pltpu.SemaphoreType.DMA((3,)), pltpu.SemaphoreType.DMA((3,)),
                        pltpu.SemaphoreType.DMA((3,)), pltpu.SemaphoreType.DMA((3,))],
        compiler_params=_cparams(None, VMEM_LIMIT),
    )(c8, cctx8, ada_w, ada_b, conv_w8)


def _cols(t, w):
    return pl.ds(pl.multiple_of(t * w, 128), w)


def _ag_weights(w_in_s, w3_s):
    D, Wc = w_in_s.shape
    Dh = D // 2
    Do = w3_s.shape[2]

    def body(wi_ref, w3_ref, fi_ref, f3_ref, si, s3, send, recv, loc):
        pos = _position()
        c = pos[2]
        s = _shard_of(pos)
        sib = _peer(pos, 1)

        def cast_rows(r, carry):
            rows = pl.ds(pl.multiple_of(r * 64, 64), 64)
            si[rows, :] = wi_ref[rows, :].astype(BF16)
            return carry

        lax.fori_loop(0, D // 64, cast_rows, 0)
        for a in range(3):
            s3[a] = w3_ref[a].astype(BF16)

        def in_half(t, hf):
            return fi_ref.at[pl.ds(hf * Dh, Dh), _cols(t, Wc)]

        def w3_half(t, hf):
            return f3_ref.at[:, t, hf]

        own = [pltpu.make_async_copy(si, fi_ref.at[:, _cols(s, Wc)], loc.at[0]),
               pltpu.make_async_copy(s3, f3_ref.at[:, s], loc.at[1])]
        first = []
        for j, k in enumerate(CHIP_FLIPS):
            to = _peer(pos, k)
            first.append(_remote(si.at[pl.ds(c * Dh, Dh), :], in_half(s, c), send, recv, j, to))
            first.append(_remote(s3.at[:, c], w3_half(s, c), send, recv, 3 + j, to))
        for cp in own + first:
            cp.start()
        passed = []
        for j, k in enumerate(CHIP_FLIPS):
            t = _shard_of(_peer(pos, k))
            _remote(in_half(t, c), in_half(t, c), send, recv, j, sib).wait_recv()
            fwd_i = _remote(in_half(t, c), in_half(t, c), send, recv, 6 + j, sib)
            fwd_i.start()
            _remote(w3_half(t, c), w3_half(t, c), send, recv, 3 + j, sib).wait_recv()
            fwd_3 = _remote(w3_half(t, c), w3_half(t, c), send, recv, 9 + j, sib)
            fwd_3.start()
            passed += [fwd_i, fwd_3]
        for j, k in enumerate(CHIP_FLIPS):
            t = _shard_of(_peer(pos, k))
            _remote(in_half(t, 1 - c), in_half(t, 1 - c), send, recv, 6 + j, sib).wait_recv()
            _remote(w3_half(t, 1 - c), w3_half(t, 1 - c), send, recv, 9 + j, sib).wait_recv()
        for cp in first + passed:
            cp.wait_send()
        for cp in own:
            cp.wait()

    return pl.pallas_call(
        body, name="ag_weights",
        in_specs=[VMEM_FULL, VMEM_FULL], out_specs=[ANY, ANY],
        out_shape=[jax.ShapeDtypeStruct((D, N_SHARD * Wc), BF16), jax.ShapeDtypeStruct((3, N_SHARD, 2, Do, D), BF16)],
        scratch_shapes=[pltpu.VMEM((D, Wc), BF16), pltpu.VMEM((3, 2, Do, D), BF16),
                        pltpu.SemaphoreType.DMA((12,)), pltpu.SemaphoreType.DMA((12,)), pltpu.SemaphoreType.DMA((2,))],
        compiler_params=_cparams(None, VMEM_LIMIT),
    )(w_in_s, w3_s)


def _pair_exchange_in(dw_other):
    def build(ins, outs, send, recv):
        return [_remote(ins[0], outs[0], send, recv, 0, _peer(_position(), 1))]

    return _Exchange((dw_other,), (jax.ShapeDtypeStruct(dw_other.shape, F32),), 1, build)


def _pair_exchange_w3(dw3):
    _, _, _, Do, D = dw3.shape

    def build(ins, outs, send, recv):
        pos = _position()
        return [_remote(ins[0].at[:, :, 1 - pos[2]], outs[0], send, recv, 0, _peer(pos, 1))]

    return _Exchange((dw3,), (jax.ShapeDtypeStruct((3, N_SHARD, Do, D), F32),), 1, build)


def _sum_pair_in(dw_mine, ri):
    Dh, Wf = dw_mine.shape
    Wc = Wf // N_SHARD
    tr = min(256, Dh)

    def body(a_ref, b_ref, o_ref):
        o_ref[...] = (a_ref[...] + b_ref[...]).astype(BF16)

    return pl.pallas_call(
        body, name="sum_pair_in", grid=(Dh // tr, N_SHARD),
        in_specs=[pl.BlockSpec((tr, Wc), lambda i, t: (i, t)), pl.BlockSpec((tr, Wc), lambda i, t: (i, t))],
        out_specs=pl.BlockSpec((None, tr, Wc), lambda i, t: (t, i, 0)),
        out_shape=jax.ShapeDtypeStruct((N_SHARD, Dh, Wc), BF16),
        compiler_params=_cparams(("parallel", "parallel")),
    )(dw_mine, ri)


def _sum_pair_w3(cidx, dw3, r3):
    _, _, _, Do, D = dw3.shape

    def body(c_ref, a_ref, b_ref, o_ref):
        o_ref[...] = (a_ref[...] + b_ref[...]).astype(BF16)

    return pl.pallas_call(
        body, name="sum_pair_w3",
        grid_spec=pltpu.PrefetchScalarGridSpec(
            num_scalar_prefetch=1, grid=(3, N_SHARD),
            in_specs=[pl.BlockSpec((None, None, None, Do, D), lambda a, t, c: (a, t, c[0], 0, 0)),
                      pl.BlockSpec((None, None, Do, D), lambda a, t, c: (a, t, 0, 0))],
            out_specs=pl.BlockSpec((None, None, Do, D), lambda a, t, c: (a, t, 0, 0))),
        out_shape=jax.ShapeDtypeStruct((3, N_SHARD, Do, D), BF16),
        compiler_params=_cparams(("parallel", "parallel")),
    )(cidx, dw3, r3)


def _chips_exchange_in(cs_in):
    _, Dh, Wc = cs_in.shape

    def build(ins, outs, send, recv):
        pos = _position()
        return [_remote(ins[0].at[_shard_of(_peer(pos, k))], outs[0].at[j], send, recv, j, _peer(pos, k))
                for j, k in enumerate(CHIP_FLIPS)]

    return _Exchange((cs_in,), (jax.ShapeDtypeStruct((3, Dh, Wc), BF16),), 3, build)


def _chips_exchange_w3(cs_3):
    _, _, Do, D = cs_3.shape

    def build(ins, outs, send, recv):
        pos = _position()
        return [_remote(ins[0].at[:, _shard_of(_peer(pos, k))], outs[0].at[j], send, recv, j, _peer(pos, k))
                for j, k in enumerate(CHIP_FLIPS)]

    return _Exchange((cs_3,), (jax.ShapeDtypeStruct((3, 3, Do, D), BF16),), 3, build)


def _sum_chips_in(csidx, cs_in, rb_in):
    _, Dh, Wc = cs_in.shape
    tr = min(256, Dh)

    def body(s_ref, a_ref, b_ref, o_ref):
        acc = a_ref[...].astype(F32)
        for j in range(3):
            acc = acc + b_ref[j].astype(F32)
        o_ref[...] = acc

    return pl.pallas_call(
        body, name="sum_chips_in",
        grid_spec=pltpu.PrefetchScalarGridSpec(
            num_scalar_prefetch=1, grid=(Dh // tr,),
            in_specs=[pl.BlockSpec((None, tr, Wc), lambda i, s: (s[1], i, 0)),
                      pl.BlockSpec((3, tr, Wc), lambda i, s: (0, i, 0))],
            out_specs=pl.BlockSpec((None, tr, Wc), lambda i, s: (s[0], i, 0))),
        out_shape=jax.ShapeDtypeStruct((2, Dh, Wc), F32),
        compiler_params=_cparams(("parallel",)),
    )(csidx, cs_in, rb_in)


def _sum_chips_w3(csidx, cs_3, rb_3):
    _, _, Do, D = cs_3.shape

    def body(s_ref, a_ref, b_ref, o_ref):
        acc = a_ref[...].astype(F32)
        for j in range(3):
            acc = acc + b_ref[j].astype(F32)
        o_ref[...] = acc

    return pl.pallas_call(
        body, name="sum_chips_w3",
        grid_spec=pltpu.PrefetchScalarGridSpec(
            num_scalar_prefetch=1, grid=(3,),
            in_specs=[pl.BlockSpec((None, None, Do, D), lambda a, s: (a, s[1], 0, 0)),
                      pl.BlockSpec((3, None, Do, D), lambda a, s: (0, a, 0, 0))],
            out_specs=pl.BlockSpec((None, None, Do, D), lambda a, s: (a, s[0], 0, 0))),
        out_shape=jax.ShapeDtypeStruct((3, 2, Do, D), F32),
        compiler_params=_cparams(("parallel",)),
    )(csidx, cs_3, rb_3)


def _rs_final(g_in, g_3):
    def body(hi_ref, h3_ref, gi_ref, g3_ref, send, recv):
        pos = _position()
        c = pos[2]
        sib = _peer(pos, 1)
        cps = [_remote(hi_ref.at[c], gi_ref.at[c], send, recv, 0, sib),
               _remote(h3_ref.at[:, c], g3_ref.at[:, c], send, recv, 1, sib)]
        for cp in cps:
            cp.start()
        _remote(hi_ref.at[1 - c], gi_ref.at[1 - c], send, recv, 0, sib).wait_recv()
        _remote(h3_ref.at[:, 1 - c], g3_ref.at[:, 1 - c], send, recv, 1, sib).wait_recv()
        for cp in cps:
            cp.wait_send()

    return pl.pallas_call(
        body, name="rs_final", in_specs=[ANY, ANY], out_specs=[ANY, ANY],
        out_shape=[jax.ShapeDtypeStruct(g_in.shape, F32), jax.ShapeDtypeStruct(g_3.shape, F32)],
        input_output_aliases={0: 0, 1: 1},
        scratch_shapes=[pltpu.SemaphoreType.DMA((2,)), pltpu.SemaphoreType.DMA((2,))],
    )(g_in, g_3)


def _adam_math(w, g, m, v):
    m = ADAM_B1 * m + (1.0 - ADAM_B1) * g
    v = ADAM_B2 * v + (1.0 - ADAM_B2) * (g * g)
    m_hat = m / (1.0 - ADAM_B1 ** ADAM_STEP)
    v_hat = v / (1.0 - ADAM_B2 ** ADAM_STEP)
    delta = -ADAM_LR * (m_hat / (jnp.sqrt(v_hat) + ADAM_EPS) + ADAM_WD * w)
    return delta, m, v


def _adamw(w, g, m, v, name):
    R, C = w.shape
    tr = min(128, R)

    def body(w_ref, g_ref, m_ref, v_ref, d_ref, nm_ref, nv_ref):
        d_ref[...], nm_ref[...], nv_ref[...] = _adam_math(w_ref[...], g_ref[...], m_ref[...], v_ref[...])

    blk = pl.BlockSpec((tr, C), lambda i: (i, 0))
    return pl.pallas_call(
        body, name=name, grid=(R // tr,), in_specs=[blk] * 4, out_specs=[blk] * 3,
        out_shape=[jax.ShapeDtypeStruct((R, C), F32)] * 3,
        compiler_params=_cparams(("parallel",), VMEM_LIMIT),
    )(w, g, m, v)


def _bwd_small(vec, act, ada_w, cctx8, w_sm, m_sm, v_sm, w_ab, m_ab, v_ab, w_cw, m_cw, v_cw, w_dl, m_dl, v_dl):
    D = vec.shape[1]
    Wm = ada_w.shape[1]
    Dq = w_cw.shape[1]

    def body(vec_ref, act_ref, aw_ref, cc_ref, wsm, msm, vsm, wab, mab, vab, wcw, mcw, vcw, wdl, mdl, vdl,
             gaw_ref, loss_ref, o_sm, o_ab, o_cw, o_dl,
             vbuf, dm, dm_sh, gcw, amine, abuf, s_v, r_v, s_a, r_a):
        pos = _position()
        me, s = _dev_id(pos), _shard_of(pos)
        vbuf[me] = vec_ref[...]
        sends = [_remote(vec_ref, vbuf.at[me], s_v, r_v, k - 1, _peer(pos, k)) for k in range(1, 8)]
        for cp in sends:
            cp.start()
        for k in range(1, 8):
            _remote(vec_ref, vbuf.at[_dev_id(_peer(pos, k))], s_v, r_v, k - 1, _peer(pos, k)).wait_recv()
        tot = vbuf[0]
        for d in range(1, N_DEV):
            tot = tot + vbuf[d]
        loss_ref[...] = jnp.zeros((8, 128), F32) + (0.5 / D) * _sum_all(tot[14:15, :])
        dm[...] = jnp.zeros_like(dm)
        for d in range(N_DEV):
            for r in range(3):
                dm[d:d + 1, r * D:(r + 1) * D] = vbuf[d, r:r + 1, :]
        dm[8:9, 0:D] = tot[3:4, :]
        dm[8:9, D:2 * D] = tot[4:5, :]
        for t in range(N_SHARD):
            @pl.when(s == t)
            def _(t=t):
                dm_sh[...] = dm[:, t * Wm:(t + 1) * Wm]
                gcw[...] = jnp.zeros_like(gcw)
                gcw[0:3, :] = tot[9:12, t * Dq:(t + 1) * Dq]
        gaw_ref[...] = lax.dot_general(act_ref[...], dm_sh[...], (((0,), (0,)), ((), ())),
                                       precision=lax.Precision.HIGHEST, preferred_element_type=F32)
        part = lax.dot_general(dm_sh[8:16, :], aw_ref[...], (((1,), (1,)), ((), ())),
                               precision=lax.Precision.HIGHEST, preferred_element_type=F32)
        amine[...] = part
        abuf[s] = part
        asend = [_remote(amine, abuf.at[s], s_a, r_a, j, _peer(pos, k)) for j, k in enumerate(CHIP_FLIPS)]
        for cp in asend:
            cp.start()
        for j, k in enumerate(CHIP_FLIPS):
            _remote(amine, abuf.at[_shard_of(_peer(pos, k))], s_a, r_a, j, _peer(pos, k)).wait_recv()
        da = abuf[0]
        for t in range(1, N_SHARD):
            da = da + abuf[t]
        cc = cc_ref[0:1, :]
        sg = _sigmoid(cc)
        g_cctx = da[0:1, :] * (sg * (1.0 + cc * (1.0 - sg)))

        def emit(o_ref, w, g, m, v):
            o_ref[0] = g
            o_ref[1], o_ref[2], o_ref[3] = _adam_math(w, g, m, v)

        g_sm = jnp.concatenate([g_cctx, tot[5:9, :], jnp.zeros((3, D), F32)], axis=0)
        emit(o_sm, wsm[...], g_sm, msm[...], vsm[...])
        g_ab = jnp.concatenate([tot[0:1, :] + tot[3:4, :], tot[1:2, :] + tot[4:5, :], tot[2:3, :],
                                jnp.zeros((5, D), F32)], axis=0)
        emit(o_ab, wab[...], g_ab, mab[...], vab[...])
        emit(o_cw, wcw[...], gcw[...], mcw[...], vcw[...])
        g_dl = jnp.concatenate([tot[12:14, 0:128] * _sigmoid(-wdl[0:2, :]), jnp.zeros((6, 128), F32)], axis=0)
        emit(o_dl, wdl[...], g_dl, mdl[...], vdl[...])
        for cp in sends + asend:
            cp.wait_send()

    return pl.pallas_call(
        body, name="bwd_small",
        in_specs=[VMEM_FULL] * 16, out_specs=[VMEM_FULL] * 6,
        out_shape=[jax.ShapeDtypeStruct((D, Wm), F32), jax.ShapeDtypeStruct((8, 128), F32),
                   jax.ShapeDtypeStruct((4, 8, D), F32), jax.ShapeDtypeStruct((4, 8, D), F32),
                   jax.ShapeDtypeStruct((4, 8, Dq), F32), jax.ShapeDtypeStruct((4, 8, 128), F32)],
        scratch_shapes=[pltpu.VMEM((N_DEV, 16, D), F32), pltpu.VMEM((16, 3 * D), F32), pltpu.VMEM((16, Wm), F32),
                        pltpu.VMEM((8, Dq), F32), pltpu.VMEM((8, D), F32), pltpu.VMEM((N_SHARD, 8, D), F32),
                        pltpu.SemaphoreType.DMA((7,)), pltpu.SemaphoreType.DMA((7,)),
                        pltpu.SemaphoreType.DMA((3,)), pltpu.SemaphoreType.DMA((3,))],
        compiler_params=_cparams(None, VMEM_LIMIT),
    )(vec, act, ada_w, cctx8, w_sm, m_sm, v_sm, w_ab, m_ab, v_ab, w_cw, m_cw, v_cw, w_dl, m_dl, v_dl)


def _pad_rows(a, rows=8):
    return jnp.pad(a, ((0, rows - a.shape[0]), (0, 0)))


def kernel(x, c, ctx, c_ctx, norm_w, ada_w, ada_b, w_in, conv_w, conv_b, decay_logit, gn_w, w_a, w_b, w_out, final_norm_w, loss_target, m_c_ctx, m_norm_w, m_ada_w, m_ada_b, m_w_in, m_conv_w, m_conv_b, m_decay_logit, m_gn_w, m_w_a, m_w_b, m_w_out, m_final_norm_w, v_c_ctx, v_norm_w, v_ada_w, v_ada_b, v_w_in, v_conv_w, v_conv_b, v_decay_logit, v_gn_w, v_w_a, v_w_b, v_w_out, v_final_norm_w):
    L, D = x.shape[1], x.shape[2]
    H = D // DV
    Wc = w_in.shape[2]
    Do = D // 8
    pos = _position()
    me = _dev_id(pos)
    cidx = jnp.reshape(pos[2], (1,)).astype(jnp.int32)
    sidx = jnp.reshape(_shard_of(pos), (1,)).astype(jnp.int32)

    act, mod, conv_w8 = _fwd_small(_pad_rows(c), _pad_rows(c_ctx[None]), ada_w[0], ada_b, _pad_rows(conv_w[0]))
    mod_x = lax.dynamic_slice_in_dim(mod, me, 1, axis=0).reshape(3, D)
    mod_c = mod[8].reshape(3, D)
    lg = jax.nn.log_sigmoid(decay_logit[0])

    w3_s = jnp.stack([w_a[0], w_b[0], w_out[0]]).reshape(3, 2, Do, D)
    w_in_full, w3_full = _ag_weights(w_in[0], w3_s)
    w3_full = w3_full.reshape(3, D, D)

    csidx = jnp.concatenate([cidx, sidx])
    groups, dret_c, xmt, cmt, dx1, sc_x, gh_3, sts = _local_step(
        x[0], ctx[0], loss_target[0], mod_x, mod_c, norm_w, conv_w8, conv_b, lg, gn_w, final_norm_w[None],
        w_in_full, w3_full, csidx)
    st_mid, st_conv, st_gn, st_lg, st_lgc, st_c = sts

    (dw_other,) = _dw_in(1 - cidx, xmt, groups, cmt, dret_c, D, "dw_in_other")
    dw_mine, ra_in = _dw_in(cidx, xmt, groups, cmt, dret_c, D, "dw_in_mine", _pair_exchange_in(dw_other))
    cs_in = _sum_pair_in(dw_mine, ra_in)
    grad_x, st_x, rb_in = _dxm(groups, 0, w_in_full, x[0], norm_w, sc_x, dx1, "dxm_x", _chips_exchange_in(cs_in))
    gh_in = _sum_chips_in(csidx, cs_in, rb_in)
    g_in, g_3 = _rs_final(gh_in, gh_3)
    g_w_in = g_in.reshape(D, Wc)
    g_3 = g_3.reshape(3, D // 4, D)

    lanes = lambda a: jnp.pad(a, ((0, 0), (0, D - a.shape[1])))
    vec = jnp.concatenate([
        st_x[0:2], st_mid[1:2], st_c[0:2], st_x[2:3] + st_c[2:3], st_conv[3:4], st_gn[0:1], st_mid[0:1],
        st_conv[0:3], lanes(st_lg[0:2] + st_lgc[0:2]), st_mid[2:3], jnp.zeros((1, D), F32)], axis=0)
    small = lambda a, b_, c_, d_, e_: _pad_rows(jnp.concatenate([a[None], b_, c_, d_, e_[None]], axis=0))
    dl = lambda a: jnp.pad(a[0], ((0, 6), (0, 128 - H)))
    g_ada_w, loss_t, o_sm, o_ab, o_cw, o_dl = _bwd_small(
        vec, act, ada_w[0], _pad_rows(c_ctx[None]),
        small(c_ctx, norm_w, conv_b, gn_w, final_norm_w), small(m_c_ctx, m_norm_w, m_conv_b, m_gn_w, m_final_norm_w),
        small(v_c_ctx, v_norm_w, v_conv_b, v_gn_w, v_final_norm_w),
        _pad_rows(ada_b.reshape(3, D)), _pad_rows(m_ada_b.reshape(3, D)), _pad_rows(v_ada_b.reshape(3, D)),
        _pad_rows(conv_w[0]), _pad_rows(m_conv_w[0]), _pad_rows(v_conv_w[0]),
        dl(decay_logit), dl(m_decay_logit), dl(v_decay_logit))

    upd_in = _adamw(w_in[0], g_w_in, m_w_in[0], v_w_in[0], "adamw_w_in")
    upd_ada = _adamw(ada_w[0], g_ada_w, m_ada_w[0], v_ada_w[0], "adamw_ada_w")
    upd_a = _adamw(w_a[0], g_3[0], m_w_a[0], v_w_a[0], "adamw_w_a")
    upd_b = _adamw(w_b[0], g_3[1], m_w_b[0], v_w_b[0], "adamw_w_b")
    upd_o = _adamw(w_out[0], g_3[2], m_w_out[0], v_w_out[0], "adamw_w_out")

    def leaves(q):
        big = lambda g, upd: (g if q == 0 else upd[q - 1])[None]
        sm = o_sm[q]
        return [sm[0], sm[1:2], big(g_ada_w, upd_ada), o_ab[q][0:3].reshape(1, 3 * D), big(g_w_in, upd_in),
                o_cw[q][0:3][None], sm[2:3], o_dl[q][0:2, 0:H][None], sm[3:4],
                big(g_3[0], upd_a), big(g_3[1], upd_b), big(g_3[2], upd_o), sm[4]]

    loss = loss_t[0, 0]
    return (loss, grad_x[None], *leaves(0), *leaves(1), *leaves(2), *leaves(3))
```

```python
from typing import Callable, NamedTuple

import jax
import jax.numpy as jnp
from jax import lax
from jax.experimental import pallas as pl
from jax.experimental.pallas import tpu as pltpu

F32 = jnp.float32
BF16 = jnp.bfloat16
MESH = pl.DeviceIdType.MESH

CHUNK = 128
RET_CPB = 4
DV = 128
DK = 64
GRID_W = 64
ROPE_BASE = 10000.0
EPS = 1e-6
K_SCALE = DK ** -0.5
N_SHARD = 4
N_DEV = 8

ADAM_LR = 0.001
ADAM_B1 = 0.9
ADAM_B2 = 0.999
ADAM_EPS = 1e-08
ADAM_WD = 0.01
ADAM_STEP = 10

VMEM_LIMIT = 56 * 1024 * 1024


def _cparams(sem=None, vmem=None):
    kw = {}
    if sem is not None:
        kw["dimension_semantics"] = sem
    if vmem is not None:
        kw["vmem_limit_bytes"] = vmem
    return pltpu.CompilerParams(**kw)


def _dot(a, b):
    return jnp.dot(a, b, preferred_element_type=F32)


def _dot_nt(a, b):
    return lax.dot_general(a, b, (((1,), (1,)), ((), ())), preferred_element_type=F32)


def _dot_tn(a, b):
    return lax.dot_general(a, b, (((0,), (0,)), ((), ())), preferred_element_type=F32)


def _sigmoid(x):
    return 1.0 / (1.0 + jnp.exp(-x))


def _sum_all(x):
    return jnp.sum(jnp.sum(x, axis=1, keepdims=True), axis=0, keepdims=True)


def _swap_halves(t):
    n = t.shape[1]
    lane = lax.broadcasted_iota(jnp.int32, t.shape, 1)
    low = (lane & 32) == 0
    return jnp.where(low, pltpu.roll(t, n - 32, 1), pltpu.roll(t, 32, 1))


def _vec_spec(d):
    return pl.BlockSpec((1, d), lambda *a: (0, 0))


def _norm_mod(x, nw, sc, sh, name):
    L, D = x.shape
    tl = min(256, L)

    def body(x_ref, nw_ref, sc_ref, sh_ref, xm_ref, xmt_ref):
        xv = x_ref[...]
        r = lax.rsqrt(jnp.mean(xv * xv, axis=-1, keepdims=True) + EPS)
        xm = (xv * r * nw_ref[...]) * (1.0 + sc_ref[...]) + sh_ref[...]
        xm_ref[...] = xm.astype(BF16)
        xmt_ref[...] = xm.T.astype(BF16)

    return pl.pallas_call(
        body, name=name, grid=(L // tl,),
        in_specs=[pl.BlockSpec((tl, D), lambda i: (i, 0)), _vec_spec(D), _vec_spec(D), _vec_spec(D)],
        out_specs=[pl.BlockSpec((tl, D), lambda i: (i, 0)), pl.BlockSpec((D, tl), lambda i: (0, i))],
        out_shape=[jax.ShapeDtypeStruct((L, D), BF16), jax.ShapeDtypeStruct((D, L), BF16)],
        compiler_params=_cparams(("parallel",)),
    )(x, nw, sc, sh)


def _in_proj(xm, w, name):
    M, D = xm.shape
    N = w.shape[1]
    tm = min(1024, M)

    def body(a_ref, b_ref, o_ref, qk_ref):
        acc = _dot(a_ref[...], b_ref[...])
        o_ref[...] = acc.astype(o_ref.dtype)

        @pl.when(pl.program_id(1) == 4)
        def _():
            qk_ref[...] = acc

    return pl.pallas_call(
        body, name=name, grid=(M // tm, N // D),
        in_specs=[pl.BlockSpec((tm, D), lambda i, j: (i, 0)), pl.BlockSpec((D, D), lambda i, j: (0, j))],
        out_specs=[pl.BlockSpec((tm, D), lambda i, j: (i, j)), pl.BlockSpec((tm, D), lambda i, j: (i, 0))],
        out_shape=[jax.ShapeDtypeStruct((M, N), BF16), jax.ShapeDtypeStruct((M, D), F32)],
        compiler_params=_cparams(("parallel", "arbitrary")),
    )(xm, w)


def _halo_specs(tl, L, D, col):
    hb = tl // 16
    last = L // 16 - 1
    prev = pl.BlockSpec((16, D), lambda i: (jnp.maximum(i * hb - 1, 0), col))
    nxt = pl.BlockSpec((16, D), lambda i: (jnp.minimum((i + 1) * hb, last), col))
    return prev, nxt


def _shift_rows(u, above, below):
    tl = u.shape[0]
    row = lax.broadcasted_iota(jnp.int32, u.shape, 0)
    dn = jnp.where(row == 0, above, pltpu.roll(u, 1, 0))
    up = jnp.where(row == tl - 1, below, pltpu.roll(u, tl - 1, 0))
    return dn, up


def _conv_gate_fwd(p, conv_w, conv_b, D):
    L = p.shape[0]
    tl = min(256, L)
    nt = L // tl

    def body(h_ref, bg_ref, cg_ref, za_ref, hp_ref, hn_ref, cp_ref, cn_ref, w_ref, b_ref, o_ref):
        i = pl.program_id(0)
        u = cg_ref[...].astype(F32) * h_ref[...].astype(F32)
        above = cp_ref[15:16, :].astype(F32) * hp_ref[15:16, :].astype(F32)
        below = cn_ref[0:1, :].astype(F32) * hn_ref[0:1, :].astype(F32)
        above = jnp.where(i == 0, 0.0, above)
        below = jnp.where(i == nt - 1, 0.0, below)
        dn, up = _shift_rows(u, above, below)
        co = w_ref[0:1, :] * dn + w_ref[1:2, :] * u + w_ref[2:3, :] * up + b_ref[...]
        za = za_ref[...].astype(F32)
        o_ref[...] = (za * _sigmoid(za) * bg_ref[...].astype(F32) * co).astype(BF16)

    main = lambda col: pl.BlockSpec((tl, D), lambda i: (i, col))
    hp, hn = _halo_specs(tl, L, D, 0)
    cp, cn = _halo_specs(tl, L, D, 2)
    return pl.pallas_call(
        body, name="conv_gate_fwd", grid=(nt,),
        in_specs=[main(0), main(1), main(2), main(3), hp, hn, cp, cn,
                  pl.BlockSpec((8, D), lambda i: (0, 0)), _vec_spec(D)],
        out_specs=pl.BlockSpec((tl, D), lambda i: (i, 0)),
        out_shape=jax.ShapeDtypeStruct((L, D), BF16),
        compiler_params=_cparams(("parallel",)),
    )(p, p, p, p, p, p, p, p, conv_w, conv_b)


def _rope_tables(L):
    pos = jnp.arange(L)
    row = (pos // GRID_W).astype(F32)
    col = (pos % GRID_W).astype(F32)
    nf = DK // 4
    inv = ROPE_BASE ** (-jnp.arange(nf, dtype=F32) / nf)
    ang = jnp.concatenate([row[:, None] * inv, col[:, None] * inv], axis=-1)
    cos, sin = jnp.cos(ang), jnp.sin(ang)
    return jnp.concatenate([cos, cos, cos, cos], axis=-1), jnp.concatenate([-sin, sin, -sin, sin], axis=-1)


def _rope_fwd(pqk, c2, s2, D):
    L = pqk.shape[0]
    W = D // 2
    tl = min(256, L)

    def body(q_ref, k_ref, c_ref, s_ref, qo_ref, ko_ref):
        c, s = c_ref[...], s_ref[...]
        for pr in range(W // 128):
            ps = slice(pr * 128, (pr + 1) * 128)
            q = q_ref[:, ps]
            k = k_ref[:, ps] * K_SCALE
            qo_ref[:, ps] = (q * c + _swap_halves(q) * s).astype(BF16)
            ko_ref[:, ps] = (k * c + _swap_halves(k) * s).astype(BF16)

    blk = lambda col: pl.BlockSpec((tl, W), lambda i: (i, col))
    tab = pl.BlockSpec((tl, 128), lambda i: (i, 0))
    return pl.pallas_call(
        body, name="rope_fwd", grid=(L // tl,),
        in_specs=[blk(0), blk(1), tab, tab],
        out_specs=[blk(0), blk(0)],
        out_shape=[jax.ShapeDtypeStruct((L, W), BF16)] * 2,
        compiler_params=_cparams(("parallel",)),
    )(pqk, pqk, c2, s2)


def _smem_spec():
    return pl.BlockSpec(memory_space=pltpu.SMEM)


def _pair_select(e0, e1):
    row = lax.broadcasted_iota(jnp.int32, e0.shape, 0)
    return jnp.where(row < DK, e0, e1)


def _head_lane_mask(shape, e):
    lane = lax.broadcasted_iota(jnp.int32, shape, 1)
    return (lane < DK) if e == 0 else (lane >= DK)


def _ctx_states(pc, pqk_c, lg, D):
    Lc = pc.shape[0]
    H = D // DV

    def body(lg_ref, k_ref, v_ref, s_ref):
        m = lax.broadcasted_iota(jnp.int32, (Lc, DV), 0).astype(F32)
        for pr in range(H // 2):
            k2 = k_ref[:, pr * 128:(pr + 1) * 128].astype(F32) * K_SCALE
            res = [[None, None], [None, None]]
            for e in range(2):
                h = 2 * pr + e
                v = v_ref[:, h * DV:(h + 1) * DV]
                dec_f = jnp.exp(lg_ref[0, h] * (Lc - 1.0 - m))
                dec_b = jnp.exp(lg_ref[1, h] * m)
                res[0][e] = _dot_tn((k2 * dec_f).astype(BF16), v)
                res[1][e] = _dot_tn((k2 * dec_b).astype(BF16), v)
            s_ref[0, pr] = _pair_select(res[0][0], res[0][1])
            s_ref[1, pr] = _pair_select(res[1][0], res[1][1])

    return pl.pallas_call(
        body, name="ctx_states", grid=(1,),
        in_specs=[_smem_spec(), pl.BlockSpec((Lc, D // 2), lambda i: (0, 1)), pl.BlockSpec((Lc, D), lambda i: (0, 5))],
        out_specs=pl.BlockSpec((2, H // 2, 128, 128), lambda i: (0, 0, 0, 0)),
        out_shape=jax.ShapeDtypeStruct((2, H // 2, 128, 128), F32),
    )(lg, pqk_c, pc)


T_M, T_MT = 0, 1
T_MF1, T_MB1 = 2, 3
T_QF, T_QB = 4, 5
T_KF, T_KB = 6, 7


def _decay_tables(lg, H):
    def body(lg_ref, t_ref):
        h = pl.program_id(0)
        lgf, lgb = lg_ref[0, h], lg_ref[1, h]
        i = lax.broadcasted_iota(jnp.int32, (CHUNK, CHUNK), 0).astype(F32)
        j = lax.broadcasted_iota(jnp.int32, (CHUNK, CHUNK), 1).astype(F32)
        d = i - j
        mf = jnp.where(d > 0, jnp.exp(lgf * jnp.maximum(d, 0.0)), 0.0)
        mb = jnp.where(d < 0, jnp.exp(lgb * jnp.maximum(-d, 0.0)), 0.0)
        mf_t = jnp.where(d < 0, jnp.exp(lgf * jnp.maximum(-d, 0.0)), 0.0)
        mb_t = jnp.where(d > 0, jnp.exp(lgb * jnp.maximum(d, 0.0)), 0.0)
        diag = jnp.where(d == 0, 2.0, 0.0)
        t_ref[0, T_M] = mf + mb + diag
        t_ref[0, T_MT] = mf_t + mb_t + diag
        t_ref[0, T_MF1] = mf * d
        t_ref[0, T_MB1] = mb * (-d)
        t_ref[0, T_QF] = jnp.exp(lgf * (i + 1.0))
        t_ref[0, T_QB] = jnp.exp(lgb * (CHUNK - i))
        t_ref[0, T_KF] = jnp.exp(lgf * (CHUNK - 1.0 - i))
        t_ref[0, T_KB] = jnp.exp(lgb * i)

    return pl.pallas_call(
        body, name="decay_tables", grid=(H,), in_specs=[_smem_spec()],
        out_specs=pl.BlockSpec((1, 8, CHUNK, CHUNK), lambda h: (h, 0, 0, 0)),
        out_shape=jax.ShapeDtypeStruct((H, 8, CHUNK, CHUNK), F32),
    )(lg)


def _tab_spec(H):
    return pl.BlockSpec((H, 8, CHUNK, CHUNK), lambda n: (0, 0, 0, 0))


def _chunk_decay(tab_ref, h):
    return tab_ref[h, T_QF, CHUNK - 1:CHUNK, :], tab_ref[h, T_QB, 0:1, :]


def _ret_states(kr, p, s0, tab, D):
    L = kr.shape[0]
    H = D // DV
    N = L // CHUNK
    HP = H // 2

    def body(tab_ref, kf_ref, kb_ref, vf_ref, vb_ref, s0_ref, sf_out, sb_out, sf, sb):
        n = pl.program_id(0)

        @pl.when(n == 0)
        def _():
            sf[...] = s0_ref[0]
            sb[...] = s0_ref[1]

        for cc in range(RET_CPB):
            cf_, cb_ = cc, RET_CPB - 1 - cc
            rf, rb = slice(cf_ * CHUNK, (cf_ + 1) * CHUNK), slice(cb_ * CHUNK, (cb_ + 1) * CHUNK)
            sf_out[cf_] = sf[...]
            sb_out[cb_] = sb[...]
            for pr in range(HP):
                kf2 = kf_ref[rf, pr * 128:(pr + 1) * 128].astype(F32)
                kb2 = kb_ref[rb, pr * 128:(pr + 1) * 128].astype(F32)
                inc_f, inc_b, gf, gb = [], [], [], []
                for e in range(2):
                    h = 2 * pr + e
                    inc_f.append(_dot_tn((kf2 * tab_ref[h, T_KF]).astype(BF16), vf_ref[rf, h * DV:(h + 1) * DV]))
                    inc_b.append(_dot_tn((kb2 * tab_ref[h, T_KB]).astype(BF16), vb_ref[rb, h * DV:(h + 1) * DV]))
                    cf, cb = _chunk_decay(tab_ref, h)
                    gf.append(jnp.broadcast_to(cf, (128, 128)))
                    gb.append(jnp.broadcast_to(cb, (128, 128)))
                sf[pr] = _pair_select(gf[0], gf[1]) * sf[pr] + _pair_select(inc_f[0], inc_f[1])
                sb[pr] = _pair_select(gb[0], gb[1]) * sb[pr] + _pair_select(inc_b[0], inc_b[1])

    st = jax.ShapeDtypeStruct((N, HP, 128, 128), F32)
    R = RET_CPB * CHUNK
    NB = N // RET_CPB
    return pl.pallas_call(
        body, name="ret_states", grid=(NB,),
        in_specs=[_tab_spec(H),
                  pl.BlockSpec((R, D // 2), lambda n: (n, 0)),
                  pl.BlockSpec((R, D // 2), lambda n: (NB - 1 - n, 0)),
                  pl.BlockSpec((R, D), lambda n: (n, 5)),
                  pl.BlockSpec((R, D), lambda n: (NB - 1 - n, 5)),
                  pl.BlockSpec((2, HP, 128, 128), lambda n: (0, 0, 0, 0))],
        out_specs=[pl.BlockSpec((RET_CPB, HP, 128, 128), lambda n: (n, 0, 0, 0)),
                   pl.BlockSpec((RET_CPB, HP, 128, 128), lambda n: (NB - 1 - n, 0, 0, 0))],
        out_shape=[st, st],
        scratch_shapes=[pltpu.VMEM((HP, 128, 128), F32), pltpu.VMEM((HP, 128, 128), F32)],
        compiler_params=_cparams(("arbitrary",)),
    )(tab, kr, kr, p, p, s0)


def _ret_out(qr, kr, p, sf_prev, sb_prev, gn_w, tab, D):
    L = qr.shape[0]
    H = D // DV
    N = L // CHUNK
    HP = H // 2

    def body(tab_ref, q_ref, k_ref, v_ref, zb_ref, sf_ref, sb_ref, gn_ref, o_ref, yb_ref):
        def chunk(cc, carry):
            rows = pl.ds(pl.multiple_of(cc * CHUNK, CHUNK), CHUNK)
            for pr in range(HP):
                q2 = q_ref[rows, pr * 128:(pr + 1) * 128]
                k2 = k_ref[rows, pr * 128:(pr + 1) * 128]
                sfp = sf_ref[cc, pr].astype(BF16)
                sbp = sb_ref[cc, pr].astype(BF16)
                for e in range(2):
                    h = 2 * pr + e
                    sl = slice(h * DV, (h + 1) * DV)
                    qm = jnp.where(_head_lane_mask(q2.shape, e), q2, jnp.zeros_like(q2))
                    a = (_dot_nt(qm, k2) * tab_ref[h, T_M]).astype(BF16)
                    qf = qm.astype(F32)
                    o = _dot(a, v_ref[rows, sl])
                    o += _dot((qf * tab_ref[h, T_QF]).astype(BF16), sfp)
                    o += _dot((qf * tab_ref[h, T_QB]).astype(BF16), sbp)
                    o_ref[rows, sl] = o
                    mu = jnp.mean(o, axis=-1, keepdims=True)
                    oc = o - mu
                    rstd = lax.rsqrt(jnp.mean(oc * oc, axis=-1, keepdims=True) + EPS)
                    zb = zb_ref[rows, sl].astype(F32)
                    yb_ref[rows, sl] = (zb * _sigmoid(zb) * (oc * rstd * gn_ref[:, sl])).astype(BF16)
            return carry

        lax.fori_loop(0, RET_CPB, chunk, 0)

    R = RET_CPB * CHUNK
    return pl.pallas_call(
        body, name="ret_out", grid=(N // RET_CPB,),
        in_specs=[_tab_spec(H),
                  pl.BlockSpec((R, D // 2), lambda n: (n, 0)),
                  pl.BlockSpec((R, D // 2), lambda n: (n, 0)),
                  pl.BlockSpec((R, D), lambda n: (n, 5)),
                  pl.BlockSpec((R, D), lambda n: (n, 6)),
                  pl.BlockSpec((RET_CPB, HP, 128, 128), lambda n: (n, 0, 0, 0)),
                  pl.BlockSpec((RET_CPB, HP, 128, 128), lambda n: (n, 0, 0, 0)),
                  _vec_spec(D)],
        out_specs=[pl.BlockSpec((R, D), lambda n: (n, 0)), pl.BlockSpec((R, D), lambda n: (n, 0))],
        out_shape=[jax.ShapeDtypeStruct((L, D), F32), jax.ShapeDtypeStruct((L, D), BF16)],
        compiler_params=_cparams(("parallel",)),
    )(tab, qr, kr, p, p, sf_prev, sb_prev, gn_w)


def _mid(ya, yb, p, x, tgt, w3, g, fw, D):
    L = x.shape[0]
    tm = min(256, L)
    nt = L // tm

    def body(ya_ref, yb_ref, ga_ref, gb_ref, x_ref, t_ref, w_hbm, g_ref, fw_ref,
             dx1_ref, dya_ref, dyb_ref, dgab_ref, dw_hbm, st_ref, w_vm, dw_acc, sem):
        i = pl.program_id(0)

        @pl.when(i == 0)
        def _():
            cp = pltpu.make_async_copy(w_hbm, w_vm, sem)
            cp.start()
            dw_acc[...] = jnp.zeros_like(dw_acc)
            st_ref[...] = jnp.zeros_like(st_ref)
            cp.wait()

        ya_b, yb_b = ya_ref[...], yb_ref[...]
        y_a = _dot(ya_b, w_vm[0])
        y_b = _dot(yb_b, w_vm[1])
        sga = _sigmoid(ga_ref[...].astype(F32))
        sgb = _sigmoid(gb_ref[...].astype(F32))
        mix_b = (sga * y_a + sgb * y_b).astype(BF16)
        y_x = _dot(mix_b, w_vm[2])
        gvec, fwv = g_ref[...], fw_ref[...]
        x1 = x_ref[...] + gvec * y_x
        r1 = lax.rsqrt(jnp.mean(x1 * x1, axis=-1, keepdims=True) + EPS)
        xh = x1 * r1
        diff = xh * fwv - t_ref[...]
        dout = diff * (1.0 / D)
        dxh = dout * fwv
        dx1 = r1 * (dxh - xh * jnp.mean(dxh * xh, axis=-1, keepdims=True))
        dx1_ref[...] = dx1
        st_ref[0:1, :] += jnp.sum(dout * xh, axis=0, keepdims=True)
        st_ref[1:2, :] += jnp.sum(dx1 * y_x, axis=0, keepdims=True)
        st_ref[2:3, :] += jnp.sum(diff * diff, axis=0, keepdims=True)
        dyx_b = (dx1 * gvec).astype(BF16)
        dmix = _dot_nt(dyx_b, w_vm[2])
        dw_acc[2] += _dot_tn(mix_b, dyx_b)
        dya_b = (dmix * sga).astype(BF16)
        dyb_b = (dmix * sgb).astype(BF16)
        dgab_ref[:, 0:D] = (dmix * y_a * sga * (1.0 - sga)).astype(BF16)
        dgab_ref[:, D:2 * D] = (dmix * y_b * sgb * (1.0 - sgb)).astype(BF16)
        dya_ref[...] = _dot_nt(dya_b, w_vm[0])
        dyb_ref[...] = _dot_nt(dyb_b, w_vm[1])
        dw_acc[0] += _dot_tn(ya_b, dya_b)
        dw_acc[1] += _dot_tn(yb_b, dyb_b)

        @pl.when(i == nt - 1)
        def _():
            out = pltpu.make_async_copy(dw_acc, dw_hbm, sem)
            out.start()
            out.wait()

    row = lambda col: pl.BlockSpec((tm, D), lambda i: (i, col))
    any_spec = pl.BlockSpec(memory_space=pl.ANY)
    f32o = jax.ShapeDtypeStruct((L, D), F32)
    return pl.pallas_call(
        body, name="mid", grid=(nt,),
        in_specs=[row(0), row(0), row(7), row(8), row(0), row(0), any_spec, _vec_spec(D), _vec_spec(D)],
        out_specs=[row(0), row(0), row(0), pl.BlockSpec((tm, 2 * D), lambda i: (i, 0)), any_spec,
                   pl.BlockSpec((8, D), lambda i: (0, 0))],
        out_shape=[f32o, f32o, f32o, jax.ShapeDtypeStruct((L, 2 * D), BF16),
                   jax.ShapeDtypeStruct((3, D, D), F32), jax.ShapeDtypeStruct((8, D), F32)],
        scratch_shapes=[pltpu.VMEM((3, D, D), BF16), pltpu.VMEM((3, D, D), F32), pltpu.SemaphoreType.DMA],
        compiler_params=_cparams(("arbitrary",), VMEM_LIMIT),
    )(ya, yb, p, p, x, tgt, w3, g, fw)


def _conv_bwd(dya, p, conv_w, conv_b, D, exchange=None):
    L = p.shape[0]
    tl = min(256, L)
    nt = L // tl

    def body(d_ref, h_ref, bg_ref, cg_ref, za_ref,
             dp_ref, dn_ref, hp_ref, hn_ref, bp_ref, bn_ref, cp_ref, cn_ref, zp_ref, zn_ref,
             w_ref, b_ref, dc_ref, st_ref):
        i = pl.program_id(0)

        @pl.when(i == 0)
        def _():
            st_ref[...] = jnp.zeros_like(st_ref)

        first, last = i == 0, i == nt - 1
        h = h_ref[...].astype(F32)
        cg = cg_ref[...].astype(F32)
        bg = bg_ref[...].astype(F32)
        za = za_ref[...].astype(F32)
        dy = d_ref[...].astype(F32)
        u = cg * h
        u_above = jnp.where(first, 0.0, cp_ref[15:16, :].astype(F32) * hp_ref[15:16, :].astype(F32))
        u_below = jnp.where(last, 0.0, cn_ref[0:1, :].astype(F32) * hn_ref[0:1, :].astype(F32))
        u_dn, u_up = _shift_rows(u, u_above, u_below)
        w0, w1, w2 = w_ref[0:1, :], w_ref[1:2, :], w_ref[2:3, :]
        co = w0 * u_dn + w1 * u + w2 * u_up + b_ref[...]
        sz = _sigmoid(za)
        silu = za * sz
        dc_ref[:, 3 * D:4 * D] = (dy * bg * co * (sz * (1.0 + za * (1.0 - sz)))).astype(BF16)
        dc_ref[:, D:2 * D] = (dy * silu * co).astype(BF16)
        dco = dy * silu * bg

        def edge(dr, zr, br, r):
            z = zr[r:r + 1, :].astype(F32)
            return dr[r:r + 1, :].astype(F32) * (z * _sigmoid(z)) * br[r:r + 1, :].astype(F32)

        dco_above = jnp.where(first, 0.0, edge(dp_ref, zp_ref, bp_ref, 15))
        dco_below = jnp.where(last, 0.0, edge(dn_ref, zn_ref, bn_ref, 0))
        dco_dn, dco_up = _shift_rows(dco, dco_above, dco_below)
        du = w0 * dco_up + w1 * dco + w2 * dco_dn
        dc_ref[:, 2 * D:3 * D] = (du * h).astype(BF16)
        dc_ref[:, 0:D] = (du * cg).astype(BF16)
        st_ref[0:1, :] += jnp.sum(dco * u_dn, axis=0, keepdims=True)
        st_ref[1:2, :] += jnp.sum(dco * u, axis=0, keepdims=True)
        st_ref[2:3, :] += jnp.sum(dco * u_up, axis=0, keepdims=True)
        st_ref[3:4, :] += jnp.sum(dco, axis=0, keepdims=True)

    main = lambda col: pl.BlockSpec((tl, D), lambda i: (i, col))
    halos = []
    for col in (0, 0, 1, 2, 3):
        halos.extend(_halo_specs(tl, L, D, col))
    return _riding_call(
        body, exchange, nt, name="conv_bwd",
        args=(dya, p, p, p, p, dya, dya, p, p, p, p, p, p, p, p, conv_w, conv_b),
        in_specs=[main(0), main(0), main(1), main(2), main(3)] + halos
                 + [pl.BlockSpec((8, D), lambda i: (0, 0)), _vec_spec(D)],
        out_specs=[pl.BlockSpec((tl, 4 * D), lambda i: (i, 0)), pl.BlockSpec((8, D), lambda i: (0, 0))],
        out_shape=[jax.ShapeDtypeStruct((L, 4 * D), BF16), jax.ShapeDtypeStruct((8, D), F32)],
        cparams=_cparams(("arbitrary",)))


def _ret_bwd_pre(dyb, p, o, gn_w, D):
    L = o.shape[0]
    H = D // DV
    tl = min(256, L)

    def body(d_ref, zb_ref, o_ref, gn_ref, do_ref, dzb_ref, st_ref):
        @pl.when(pl.program_id(0) == 0)
        def _():
            st_ref[...] = jnp.zeros_like(st_ref)

        for h in range(H):
            sl = slice(h * DV, (h + 1) * DV)
            ov = o_ref[:, sl]
            mu = jnp.mean(ov, axis=-1, keepdims=True)
            oc = ov - mu
            rstd = lax.rsqrt(jnp.mean(oc * oc, axis=-1, keepdims=True) + EPS)
            rn = oc * rstd
            gw = gn_ref[:, sl]
            zb = zb_ref[:, sl].astype(F32)
            sz = _sigmoid(zb)
            dy = d_ref[:, sl].astype(F32)
            dzb_ref[:, sl] = (dy * (rn * gw) * (sz * (1.0 + zb * (1.0 - sz)))).astype(BF16)
            dretn = dy * (zb * sz)
            st_ref[0:1, sl] += jnp.sum(dretn * rn, axis=0, keepdims=True)
            drn = dretn * gw
            do = rstd * (drn - jnp.mean(drn, axis=-1, keepdims=True)
                         - rn * jnp.mean(drn * rn, axis=-1, keepdims=True))
            do_ref[:, sl] = do.astype(BF16)

    main = lambda col: pl.BlockSpec((tl, D), lambda i: (i, col))
    bfo = jax.ShapeDtypeStruct((L, D), BF16)
    return pl.pallas_call(
        body, name="ret_bwd_pre", grid=(L // tl,),
        in_specs=[main(0), main(6), main(0), _vec_spec(D)],
        out_specs=[main(0), main(0), pl.BlockSpec((8, D), lambda i: (0, 0))],
        out_shape=[bfo, bfo, jax.ShapeDtypeStruct((8, D), F32)],
        compiler_params=_cparams(("arbitrary",)),
    )(dyb, p, o, gn_w)


def _ret_bwd_states(qr, do, tab, D):
    L = qr.shape[0]
    H = D // DV
    N = L // CHUNK
    HP = H // 2

    def body(tab_ref, qf_ref, qb_ref, dof_ref, dob_ref, dsf_out, dsb_out, ds0_out, dsf, dsb):
        n = pl.program_id(0)

        @pl.when(n == 0)
        def _():
            dsf[...] = jnp.zeros_like(dsf)
            dsb[...] = jnp.zeros_like(dsb)

        for cc in range(RET_CPB):
            cf_, cb_ = RET_CPB - 1 - cc, cc
            rf, rb = slice(cf_ * CHUNK, (cf_ + 1) * CHUNK), slice(cb_ * CHUNK, (cb_ + 1) * CHUNK)
            dsf_out[cf_] = dsf[...]
            dsb_out[cb_] = dsb[...]
            for pr in range(HP):
                qf2 = qf_ref[rf, pr * 128:(pr + 1) * 128].astype(F32)
                qb2 = qb_ref[rb, pr * 128:(pr + 1) * 128].astype(F32)
                inc_f, inc_b, gf, gb = [], [], [], []
                for e in range(2):
                    h = 2 * pr + e
                    inc_f.append(_dot_tn((qf2 * tab_ref[h, T_QF]).astype(BF16), dof_ref[rf, h * DV:(h + 1) * DV]))
                    inc_b.append(_dot_tn((qb2 * tab_ref[h, T_QB]).astype(BF16), dob_ref[rb, h * DV:(h + 1) * DV]))
                    cf, cb = _chunk_decay(tab_ref, h)
                    gf.append(jnp.broadcast_to(cf, (128, 128)))
                    gb.append(jnp.broadcast_to(cb, (128, 128)))
                dsf[pr] = _pair_select(gf[0], gf[1]) * dsf[pr] + _pair_select(inc_f[0], inc_f[1])
                dsb[pr] = _pair_select(gb[0], gb[1]) * dsb[pr] + _pair_select(inc_b[0], inc_b[1])

        @pl.when(n == NB - 1)
        def _():
            ds0_out[0] = dsf[...]
            ds0_out[1] = dsb[...]

    st = jax.ShapeDtypeStruct((N, HP, 128, 128), F32)
    R = RET_CPB * CHUNK
    NB = N // RET_CPB
    return pl.pallas_call(
        body, name="ret_bwd_states", grid=(NB,),
        in_specs=[_tab_spec(H),
                  pl.BlockSpec((R, D // 2), lambda n: (NB - 1 - n, 0)),
                  pl.BlockSpec((R, D // 2), lambda n: (n, 0)),
                  pl.BlockSpec((R, D), lambda n: (NB - 1 - n, 0)),
                  pl.BlockSpec((R, D), lambda n: (n, 0))],
        out_specs=[pl.BlockSpec((RET_CPB, HP, 128, 128), lambda n: (NB - 1 - n, 0, 0, 0)),
                   pl.BlockSpec((RET_CPB, HP, 128, 128), lambda n: (n, 0, 0, 0)),
                   pl.BlockSpec((2, HP, 128, 128), lambda n: (0, 0, 0, 0))],
        out_shape=[st, st, jax.ShapeDtypeStruct((2, HP, 128, 128), F32)],
        scratch_shapes=[pltpu.VMEM((HP, 128, 128), F32), pltpu.VMEM((HP, 128, 128), F32)],
        compiler_params=_cparams(("arbitrary",)),
    )(tab, qr, qr, do, do)


def _ret_bwd_main(qr, kr, p, do, sf_prev, sb_prev, dsf, dsb, c2, s2, tab, D, exchange=None):
    L = qr.shape[0]
    H = D // DV
    N = L // CHUNK
    HP = H // 2
    W = D // 2

    def body(tab_ref, q_ref, k_ref, v_ref, do_ref, sf_ref, sb_ref, dsf_ref, dsb_ref, c_ref, s_ref,
             dr_ref, st_ref):
        @pl.when(pl.program_id(0) == 0)
        def _():
            st_ref[...] = jnp.zeros_like(st_ref)

        i = lax.broadcasted_iota(jnp.int32, (CHUNK, 128), 0).astype(F32)
        lane = lax.broadcasted_iota(jnp.int32, (1, 128), 1)
        rowid = lax.broadcasted_iota(jnp.int32, (128, 128), 0)

        def chunk(cc, carry):
            acc_f, acc_b = carry
            rows = pl.ds(pl.multiple_of(cc * CHUNK, CHUNK), CHUNK)
            c, s = c_ref[rows, :], s_ref[rows, :]
            for pr in range(HP):
                ps = slice(pr * 128, (pr + 1) * 128)
                q2, k2 = q_ref[rows, ps], k_ref[rows, ps]
                sf32, sb32 = sf_ref[cc, pr], sb_ref[cc, pr]
                dsf32, dsb32 = dsf_ref[cc, pr], dsb_ref[cc, pr]
                sfp, sbp = sf32.astype(BF16), sb32.astype(BF16)
                dsfp, dsbp = dsf32.astype(BF16), dsb32.astype(BF16)
                dq2 = jnp.zeros((CHUNK, 128), F32)
                dk2 = jnp.zeros((CHUNK, 128), F32)
                for e in range(2):
                    h = 2 * pr + e
                    sl = slice(h * DV, (h + 1) * DV)
                    hm = _head_lane_mask(q2.shape, e)
                    qm = jnp.where(hm, q2, jnp.zeros_like(q2))
                    km = jnp.where(hm, k2, jnp.zeros_like(k2))
                    qf, kf = qm.astype(F32), km.astype(F32)
                    v, do = v_ref[rows, sl], do_ref[rows, sl]
                    vf, dof = v.astype(F32), do.astype(F32)
                    m_t = tab_ref[h, T_MT]
                    sc = _dot_nt(qm, k2)
                    dpm = _dot_nt(do, v)
                    dsc = (dpm * tab_ref[h, T_M]).astype(BF16)
                    a_t = (_dot_nt(km, q2) * m_t).astype(BF16)
                    dsc_t = (_dot_nt(v, do) * m_t).astype(BF16)
                    dq_f, dq_b = tab_ref[h, T_QF], tab_ref[h, T_QB]
                    dk_f, dk_b = tab_ref[h, T_KF], tab_ref[h, T_KB]
                    dq = _dot(dsc, km)
                    dq += jnp.where(hm, dq_f * _dot_nt(do, sfp) + dq_b * _dot_nt(do, sbp), 0.0)
                    dk = _dot(dsc_t, qm)
                    dk += jnp.where(hm, dk_f * _dot_nt(v, dsfp) + dk_b * _dot_nt(v, dsbp), 0.0)
                    kdf = _dot((kf * dk_f).astype(BF16), dsfp)
                    kdb = _dot((kf * dk_b).astype(BF16), dsbp)
                    dr_ref[rows, D + h * DV:D + (h + 1) * DV] = (_dot(a_t, do) + kdf + kdb).astype(BF16)
                    dq2 += dq
                    dk2 += dk
                    xf = _dot((qf * dq_f).astype(BF16), sfp)
                    xb = _dot((qf * dq_b).astype(BF16), sbp)
                    pair = (rowid < DK) if e == 0 else (rowid >= DK)
                    gcf, gcb = tab_ref[h, T_QF, CHUNK - 1:CHUNK, 0:1], tab_ref[h, T_QB, 0:1, 0:1]
                    scdp = sc * dpm
                    tf = _sum_all(scdp * tab_ref[h, T_MF1]) + _sum_all(xf * dof * (i + 1.0)) \
                        + _sum_all(kdf * vf * (CHUNK - 1.0 - i)) \
                        + CHUNK * gcf * _sum_all(jnp.where(pair, dsf32 * sf32, 0.0))
                    tb = _sum_all(scdp * tab_ref[h, T_MB1]) + _sum_all(xb * dof * (CHUNK - i)) \
                        + _sum_all(kdb * vf * i) \
                        + CHUNK * gcb * _sum_all(jnp.where(pair, dsb32 * sb32, 0.0))
                    acc_f += jnp.where(lane == h, tf, 0.0)
                    acc_b += jnp.where(lane == h, tb, 0.0)
                dr_ref[rows, ps] = (dq2 * c - _swap_halves(dq2) * s).astype(BF16)
                dr_ref[rows, W + pr * 128:W + (pr + 1) * 128] = \
                    ((dk2 * c - _swap_halves(dk2) * s) * K_SCALE).astype(BF16)
            return acc_f, acc_b

        zero = jnp.zeros((1, 128), F32)
        acc_f, acc_b = lax.fori_loop(0, RET_CPB, chunk, (zero, zero))
        st_ref[0:1, :] += acc_f
        st_ref[1:2, :] += acc_b

    R = RET_CPB * CHUNK
    st_spec = pl.BlockSpec((RET_CPB, HP, 128, 128), lambda n: (n, 0, 0, 0))
    half = pl.BlockSpec((R, W), lambda n: (n, 0))
    rope = pl.BlockSpec((R, 128), lambda n: (n, 0))
    return _riding_call(
        body, exchange, N // RET_CPB, name="ret_bwd_main",
        args=(tab, qr, kr, p, do, sf_prev, sb_prev, dsf, dsb, c2, s2),
        in_specs=[_tab_spec(H), half, half,
                  pl.BlockSpec((R, D), lambda n: (n, 5)),
                  pl.BlockSpec((R, D), lambda n: (n, 0)),
                  st_spec, st_spec, st_spec, st_spec, rope, rope],
        out_specs=[pl.BlockSpec((R, 2 * D), lambda n: (n, 0)),
                   pl.BlockSpec((8, 128), lambda n: (0, 0))],
        out_shape=[jax.ShapeDtypeStruct((L, 2 * D), BF16), jax.ShapeDtypeStruct((8, 128), F32)],
        cparams=_cparams(("arbitrary",)))


def _ctx_bwd(pc, pqk_c, ds0, lg, D):
    Lc = pc.shape[0]
    H = D // DV
    HP = H // 2
    W = D // 2

    def body(lg_ref, k_ref, v_ref, ds_ref, dr_ref, st_ref):
        dqk_ref = dr_ref.at[:, 0:D]
        dv_ref = dr_ref.at[:, D:2 * D]
        m = lax.broadcasted_iota(jnp.int32, (Lc, 128), 0).astype(F32)
        lane = lax.broadcasted_iota(jnp.int32, (1, 128), 1)
        acc_f = jnp.zeros((1, 128), F32)
        acc_b = jnp.zeros((1, 128), F32)
        dqk_ref[:, 0:W] = jnp.zeros((Lc, W), BF16)
        for pr in range(HP):
            ps = slice(pr * 128, (pr + 1) * 128)
            k2 = k_ref[:, ps].astype(F32) * K_SCALE
            dsfp, dsbp = ds_ref[0, pr].astype(BF16), ds_ref[1, pr].astype(BF16)
            dk2 = jnp.zeros((Lc, 128), F32)
            for e in range(2):
                h = 2 * pr + e
                sl = slice(h * DV, (h + 1) * DV)
                hm = _head_lane_mask(k2.shape, e)
                km = jnp.where(hm, k2, 0.0)
                v = v_ref[:, sl]
                vf = v.astype(F32)
                dec_f = jnp.exp(lg_ref[0, h] * (Lc - 1.0 - m))
                dec_b = jnp.exp(lg_ref[1, h] * m)
                kdf = _dot((km * dec_f).astype(BF16), dsfp)
                kdb = _dot((km * dec_b).astype(BF16), dsbp)
                dv_ref[:, sl] = (kdf + kdb).astype(BF16)
                dk2 += jnp.where(hm, dec_f * _dot_nt(v, dsfp) + dec_b * _dot_nt(v, dsbp), 0.0)
                acc_f += jnp.where(lane == h, _sum_all(kdf * vf * (Lc - 1.0 - m)), 0.0)
                acc_b += jnp.where(lane == h, _sum_all(kdb * vf * m), 0.0)
            dqk_ref[:, W + pr * 128:W + (pr + 1) * 128] = (dk2 * K_SCALE).astype(BF16)
        st_ref[...] = jnp.zeros_like(st_ref)
        st_ref[0:1, :] = acc_f
        st_ref[1:2, :] = acc_b

    return pl.pallas_call(
        body, name="ctx_bwd", grid=(1,),
        in_specs=[_smem_spec(), pl.BlockSpec((Lc, W), lambda i: (0, 1)), pl.BlockSpec((Lc, D), lambda i: (0, 5)),
                  pl.BlockSpec((2, HP, 128, 128), lambda i: (0, 0, 0, 0))],
        out_specs=[pl.BlockSpec((Lc, 2 * D), lambda i: (0, 0)), pl.BlockSpec((8, 128), lambda i: (0, 0))],
        out_shape=[jax.ShapeDtypeStruct((Lc, 2 * D), BF16), jax.ShapeDtypeStruct((8, 128), F32)],
    )(lg, pqk_c, pc, ds0)


class _Exchange(NamedTuple):
    inputs: tuple
    out_shapes: tuple
    n_copies: int
    build: Callable


def _exchange_parts(exchange):
    if exchange is None:
        return [], [], [], [], []
    n = exchange.n_copies
    return (list(exchange.inputs), [ANY] * len(exchange.inputs), list(exchange.out_shapes),
            [ANY] * len(exchange.out_shapes), [pltpu.SemaphoreType.DMA((n,)), pltpu.SemaphoreType.DMA((n,))])


def _riding_call(body, exchange, n_steps, *, args, in_specs, out_specs, out_shape, name, cparams, scratch=()):
    ex_args, ex_in_specs, ex_shapes, ex_out_specs, ex_scratch = _exchange_parts(exchange)
    n_in, n_out, n_sc = len(args), len(out_shape), len(scratch)

    def riding(*refs):
        k = n_in + len(ex_args)
        ins, ex_in = refs[:n_in], refs[n_in:k]
        outs, ex_out = refs[k:k + n_out], refs[k + n_out:k + n_out + len(ex_shapes)]
        k += n_out + len(ex_shapes)
        own_scratch, ex_sems = refs[k:k + n_sc], refs[k + n_sc:]
        step = pl.program_id(0)
        if exchange is not None:
            @pl.when(step == 0)
            def _():
                for rc in exchange.build(ex_in, ex_out, *ex_sems):
                    rc.start()
        body(*ins, *outs, *own_scratch)
        if exchange is not None:
            @pl.when(step == n_steps - 1)
            def _():
                for rc in exchange.build(ex_in, ex_out, *ex_sems):
                    rc.wait()

    return tuple(pl.pallas_call(
        riding, name=name, grid=(n_steps,),
        in_specs=list(in_specs) + ex_in_specs, out_specs=list(out_specs) + ex_out_specs,
        out_shape=list(out_shape) + ex_shapes, scratch_shapes=list(scratch) + ex_scratch,
        compiler_params=cparams,
    )(*args, *ex_args))


def _dxm(groups, col0, w, x, nw, sc, dx1, name, exchange=None):
    L, D = x.shape
    tm = min(256, L)
    nt = L // tm
    ng = len(groups)
    widths = [g.shape[1] for g in groups]
    wtot = sum(widths)
    with_dx = dx1 is not None
    ex_args, ex_in_specs, ex_shapes, ex_out_specs, ex_scratch = _exchange_parts(exchange)
    n_in = ng + 4 + (1 if with_dx else 0)
    n_out = 2 if with_dx else 1

    def body(*refs):
        group_refs = refs[:ng]
        w_hbm, x_ref, nw_ref, sc_ref = refs[ng:ng + 4]
        ex_in = refs[n_in:n_in + len(ex_args)]
        outs = refs[n_in + len(ex_args):]
        if with_dx:
            dx1_ref, gx_ref, st_ref = refs[ng + 4], outs[0], outs[1]
        else:
            st_ref = outs[0]
        ex_out = outs[n_out:n_out + len(ex_shapes)]
        w_vm, sem = outs[n_out + len(ex_shapes):n_out + len(ex_shapes) + 2]
        ex_sems = outs[n_out + len(ex_shapes) + 2:]
        i = pl.program_id(0)

        @pl.when(i == 0)
        def _():
            cp = pltpu.make_async_copy(w_hbm.at[:, col0 * D:col0 * D + wtot], w_vm, sem)
            cp.start()
            if exchange is not None:
                for rc in exchange.build(ex_in, ex_out, *ex_sems):
                    rc.start()
            st_ref[...] = jnp.zeros_like(st_ref)
            cp.wait()

        dxm, off = None, 0
        for g_ref, wd in zip(group_refs, widths):
            part = _dot_nt(g_ref[...], w_vm[:, off:off + wd])
            dxm = part if dxm is None else dxm + part
            off += wd

        xv = x_ref[...]
        r = lax.rsqrt(jnp.mean(xv * xv, axis=-1, keepdims=True) + EPS)
        xh = xv * r
        nwv = nw_ref[...]
        dxn = dxm * (1.0 + sc_ref[...])
        st_ref[0:1, :] += jnp.sum(dxm, axis=0, keepdims=True)
        st_ref[1:2, :] += jnp.sum(dxm * (xh * nwv), axis=0, keepdims=True)
        st_ref[2:3, :] += jnp.sum(dxn * xh, axis=0, keepdims=True)
        if with_dx:
            dxh = dxn * nwv
            gx_ref[...] = dx1_ref[...] + r * (dxh - xh * jnp.mean(dxh * xh, axis=-1, keepdims=True))

        if exchange is not None:
            @pl.when(i == nt - 1)
            def _():
                for rc in exchange.build(ex_in, ex_out, *ex_sems):
                    rc.wait()

    row = pl.BlockSpec((tm, D), lambda i: (i, 0))
    in_specs = [pl.BlockSpec((tm, wd), lambda i: (i, 0)) for wd in widths] + [ANY, row, _vec_spec(D), _vec_spec(D)]
    out_specs = [pl.BlockSpec((8, D), lambda i: (0, 0))]
    out_shape = [jax.ShapeDtypeStruct((8, D), F32)]
    args = list(groups) + [w, x, nw, sc]
    if with_dx:
        in_specs.append(row)
        out_specs.insert(0, row)
        out_shape.insert(0, jax.ShapeDtypeStruct((L, D), F32))
        args.append(dx1)
    res = pl.pallas_call(
        body, name=name, grid=(nt,),
        in_specs=in_specs + ex_in_specs, out_specs=out_specs + ex_out_specs, out_shape=out_shape + ex_shapes,
        scratch_shapes=[pltpu.VMEM((D, wtot), BF16), pltpu.SemaphoreType.DMA] + ex_scratch,
        compiler_params=_cparams(("arbitrary",), VMEM_LIMIT),
    )(*args, *ex_args)
    gx = res[0] if with_dx else None
    return (gx, res[n_out - 1], *res[n_out:])


DW_TN = 512


def _dw_in(hidx, xmt, groups, cmt, dr_c, D, name, exchange=None):
    L = xmt.shape[1]
    Lc = cmt.shape[1]
    Dh = D // 2
    tn = min(DW_TN, D)
    nblk = [g.shape[1] // tn for g in groups]
    starts = [sum(nblk[:g]) for g in range(len(groups))]
    ng = len(groups)
    nj = sum(nblk)
    ex_args, ex_in_specs, ex_shapes, ex_out_specs, ex_scratch = _exchange_parts(exchange)

    def body(*refs):
        h_ref, xt_hbm = refs[0], refs[1]
        group_refs = refs[2:2 + ng]
        ct_ref, drc_ref = refs[2 + ng:4 + ng]
        ex_in = refs[4 + ng:4 + ng + len(ex_args)]
        outs = refs[4 + ng + len(ex_args):]
        o_ref = outs[0]
        ex_out = outs[1:1 + len(ex_shapes)]
        xt_vm, sem = outs[1 + len(ex_shapes):3 + len(ex_shapes)]
        ex_sems = outs[3 + len(ex_shapes):]
        j = pl.program_id(0)

        @pl.when(j == 0)
        def _():
            rows = pl.ds(pl.multiple_of(h_ref[0] * Dh, Dh), Dh)
            cp = pltpu.make_async_copy(xt_hbm.at[rows, :], xt_vm, sem)
            cp.start()
            if exchange is not None:
                for rc in exchange.build(ex_in, ex_out, *ex_sems):
                    rc.start()
            cp.wait()

        for g in range(ng):
            @pl.when((j >= starts[g]) & (j < starts[g] + nblk[g]))
            def _(g=g):
                acc = _dot(xt_vm[...], group_refs[g][...])
                if g == 1:
                    acc += _dot(ct_ref[...], drc_ref[...])
                o_ref[...] = acc

        if exchange is not None:
            @pl.when(j == nj - 1)
            def _():
                for rc in exchange.build(ex_in, ex_out, *ex_sems):
                    rc.wait()

    def group_spec(g, rows):
        return pl.BlockSpec((rows, tn), lambda j, h: (0, jnp.clip(j - starts[g], 0, nblk[g] - 1)))

    res = pl.pallas_call(
        body, name=name,
        grid_spec=pltpu.PrefetchScalarGridSpec(
            num_scalar_prefetch=1, grid=(nj,),
            in_specs=[ANY] + [group_spec(g, L) for g in range(ng)]
                     + [pl.BlockSpec((Dh, Lc), lambda j, h: (h[0], 0)), group_spec(1, Lc)] + ex_in_specs,
            out_specs=[pl.BlockSpec((Dh, tn), lambda j, h: (0, j))] + ex_out_specs,
            scratch_shapes=[pltpu.VMEM((Dh, L), BF16), pltpu.SemaphoreType.DMA] + ex_scratch),
        out_shape=[jax.ShapeDtypeStruct((Dh, nj * tn), F32)] + ex_shapes,
        compiler_params=_cparams(("arbitrary",), VMEM_LIMIT),
    )(hidx, xmt, *groups, cmt, dr_c, *ex_args)
    return tuple(res)


def _local_step(x, ctx, tgt, mod_x, mod_c, norm_w, conv_w8, conv_b, lg, gn_w, fw, w_in, w3, csidx=None):
    L, D = x.shape
    sh_x, sc_x, g_x = mod_x[0:1], mod_x[1:2], mod_x[2:3]
    sh_c, sc_c = mod_c[0:1], mod_c[1:2]
    c2, s2 = _rope_tables(L)
    tab = _decay_tables(lg, D // DV)

    xm, xmt = _norm_mod(x, norm_w, sc_x, sh_x, "norm_mod_x")
    cm, cmt = _norm_mod(ctx, norm_w, sc_c, sh_c, "norm_mod_ctx")
    p, pqk = _in_proj(xm, w_in, "in_proj_x")
    pc, pqk_c = _in_proj(cm, w_in, "in_proj_ctx")
    ya = _conv_gate_fwd(p, conv_w8, conv_b, D)
    qr, kr = _rope_fwd(pqk, c2, s2, D)
    s0 = _ctx_states(pc, pqk_c, lg, D)
    sf_prev, sb_prev = _ret_states(kr, p, s0, tab, D)
    o, yb = _ret_out(qr, kr, p, sf_prev, sb_prev, gn_w, tab, D)
    dx1, dya, dyb, dgab, dw3, st_mid = _mid(ya, yb, p, x, tgt, w3, g_x, fw, D)
    reduce = csidx is not None
    dw3_5 = dw3.reshape(3, N_SHARD, 2, D // 8, D)
    dconv, st_conv, *ra_3 = _conv_bwd(dya, p, conv_w8, conv_b, D, _pair_exchange_w3(dw3_5) if reduce else None)
    do, dzb, st_gn = _ret_bwd_pre(dyb, p, o, gn_w, D)
    dsf, dsb, ds0 = _ret_bwd_states(qr, do, tab, D)
    cs_3 = _sum_pair_w3(csidx[0:1], dw3_5, ra_3[0]) if reduce else None
    dret, st_lg, *rb_3 = _ret_bwd_main(qr, kr, p, do, sf_prev, sb_prev, dsf, dsb, c2, s2, tab, D,
                                       _chips_exchange_w3(cs_3) if reduce else None)
    g_3 = _sum_chips_w3(csidx, cs_3, rb_3[0]) if reduce else dw3
    dret_c, st_lgc = _ctx_bwd(pc, pqk_c, ds0, lg, D)
    groups = (dconv, dret, dzb, dgab)
    _, st_c = _dxm((dret_c,), 4, w_in, ctx, norm_w, sc_c, None, "dxm_ctx")
    return groups, dret_c, xmt, cmt, dx1, sc_x, g_3, (st_mid, st_conv, st_gn, st_lg, st_lgc, st_c)


CHIP_FLIPS = (4, 2, 6)
ANY = pl.BlockSpec(memory_space=pl.ANY)
VMEM_FULL = pl.BlockSpec(memory_space=pltpu.VMEM)


def _position():
    return lax.axis_index("x"), lax.axis_index("y"), lax.axis_index("c")


def _peer(pos, k):
    x, y, c = pos
    return (1 - x if k & 4 else x, 1 - y if k & 2 else y, 1 - c if k & 1 else c)


def _dev_id(pos):
    return 4 * pos[0] + 2 * pos[1] + pos[2]


def _shard_of(pos):
    return 2 * pos[0] + pos[1]


def _remote(src, dst, send_sems, recv_sems, idx, to):
    return pltpu.make_async_remote_copy(src_ref=src, dst_ref=dst, send_sem=send_sems.at[idx],
                                        recv_sem=recv_sems.at[idx], device_id=to, device_id_type=MESH)


def _dot_f32(a, b):
    return jnp.dot(a, b, precision=lax.Precision.HIGHEST, preferred_element_type=F32)


def _silu(x):
    return x * _sigmoid(x)


def _fwd_small(c8, cctx8, ada_w, ada_b, conv_w8):
    D = c8.shape[1]
    Wm = ada_w.shape[1]
    Dq = conv_w8.shape[1]

    def body(c_ref, cc_ref, aw_ref, ab_ref, cw_ref, act_ref, mod_ref, cwf_ref,
             cbuf, pmine, pbuf, wbuf, s_c, r_c, s_p, r_p, s_w, r_w):
        pos = _position()
        me, s = _dev_id(pos), _shard_of(pos)
        cbuf[me] = c_ref[...]
        wbuf[s] = cw_ref[...]
        sends = [_remote(c_ref, cbuf.at[me], s_c, r_c, k - 1, _peer(pos, k)) for k in range(1, 8)]
        sends += [_remote(cw_ref, wbuf.at[s], s_w, r_w, j, _peer(pos, k)) for j, k in enumerate(CHIP_FLIPS)]
        for cp in sends:
            cp.start()
        for k in range(1, 8):
            _remote(c_ref, cbuf.at[_dev_id(_peer(pos, k))], s_c, r_c, k - 1, _peer(pos, k)).wait_recv()
        for d in range(N_DEV):
            act_ref[d:d + 1, :] = _silu(cbuf[d, 0:1, :])
        act_ref[8:9, :] = _silu(cc_ref[0:1, :])
        act_ref[9:16, :] = jnp.zeros((7, D), F32)
        part = _dot_f32(act_ref[...], aw_ref[...])
        pmine[...] = part
        pbuf[s] = part
        psend = [_remote(pmine, pbuf.at[s], s_p, r_p, j, _peer(pos, k)) for j, k in enumerate(CHIP_FLIPS)]
        for cp in psend:
            cp.start()
        for j, k in enumerate(CHIP_FLIPS):
            t = _shard_of(_peer(pos, k))
            _remote(pmine, pbuf.at[t], s_p, r_p, j, _peer(pos, k)).wait_recv()
            _remote(cw_ref, wbuf.at[t], s_w, r_w, j, _peer(pos, k)).wait_recv()
        for t in range(N_SHARD):
            mod_ref[:, t * Wm:(t + 1) * Wm] = pbuf[t] + ab_ref[:, t * Wm:(t + 1) * Wm]
            cwf_ref[:, t * Dq:(t + 1) * Dq] = wbuf[t]
        for cp in sends + psend:
            cp.wait_send()

    return pl.pallas_call(
        body, name="fwd_small",
        in_specs=[VMEM_FULL] * 5, out_specs=[VMEM_FULL] * 3,
        out_shape=[jax.ShapeDtypeStruct((16, D), F32), jax.ShapeDtypeStruct((16, 3 * D), F32),
                   jax.ShapeDtypeStruct((8, D), F32)],
        scratch_shapes=[pltpu.VMEM((N_DEV, 8, D), F32), pltpu.VMEM((16, Wm), F32),
                        pltpu.VMEM((N_SHARD, 16, Wm), F32), pltpu.VMEM((N_SHARD, 8, Dq), F32),
                        pltpu.SemaphoreType.DMA((7,)), pltpu.SemaphoreType.DMA((7,)),
                        pltpu.SemaphoreType.DMA((3,)), pltpu.SemaphoreType.DMA((3,)),
                        pltpu.SemaphoreType.DMA((3,)), pltpu.SemaphoreType.DMA((3,))],
        compiler_params=_cparams(None, VMEM_LIMIT),
    )(c8, cctx8, ada_w, ada_b, conv_w8)


def _cols(t, w):
    return pl.ds(pl.multiple_of(t * w, 128), w)


def _ag_weights(w_in_s, w3_s):
    D, Wc = w_in_s.shape
    Dh = D // 2
    Do = w3_s.shape[2]

    def body(wi_ref, w3_ref, fi_ref, f3_ref, si, s3, send, recv, loc):
        pos = _position()
        c = pos[2]
        s = _shard_of(pos)
        sib = _peer(pos, 1)

        def cast_rows(r, carry):
            rows = pl.ds(pl.multiple_of(r * 64, 64), 64)
            si[rows, :] = wi_ref[rows, :].astype(BF16)
            return carry

        lax.fori_loop(0, D // 64, cast_rows, 0)
        for a in range(3):
            s3[a] = w3_ref[a].astype(BF16)

        def in_half(t, hf):
            return fi_ref.at[pl.ds(hf * Dh, Dh), _cols(t, Wc)]

        def w3_half(t, hf):
            return f3_ref.at[:, t, hf]

        own = [pltpu.make_async_copy(si, fi_ref.at[:, _cols(s, Wc)], loc.at[0]),
               pltpu.make_async_copy(s3, f3_ref.at[:, s], loc.at[1])]
        first = []
        for j, k in enumerate(CHIP_FLIPS):
            to = _peer(pos, k)
            first.append(_remote(si.at[pl.ds(c * Dh, Dh), :], in_half(s, c), send, recv, j, to))
            first.append(_remote(s3.at[:, c], w3_half(s, c), send, recv, 3 + j, to))
        for cp in own + first:
            cp.start()
        passed = []
        for j, k in enumerate(CHIP_FLIPS):
            t = _shard_of(_peer(pos, k))
            _remote(in_half(t, c), in_half(t, c), send, recv, j, sib).wait_recv()
            fwd_i = _remote(in_half(t, c), in_half(t, c), send, recv, 6 + j, sib)
            fwd_i.start()
            _remote(w3_half(t, c), w3_half(t, c), send, recv, 3 + j, sib).wait_recv()
            fwd_3 = _remote(w3_half(t, c), w3_half(t, c), send, recv, 9 + j, sib)
            fwd_3.start()
            passed += [fwd_i, fwd_3]
        for j, k in enumerate(CHIP_FLIPS):
            t = _shard_of(_peer(pos, k))
            _remote(in_half(t, 1 - c), in_half(t, 1 - c), send, recv, 6 + j, sib).wait_recv()
            _remote(w3_half(t, 1 - c), w3_half(t, 1 - c), send, recv, 9 + j, sib).wait_recv()
        for cp in first + passed:
            cp.wait_send()
        for cp in own:
            cp.wait()

    return pl.pallas_call(
        body, name="ag_weights",
        in_specs=[VMEM_FULL, VMEM_FULL], out_specs=[ANY, ANY],
        out_shape=[jax.ShapeDtypeStruct((D, N_SHARD * Wc), BF16), jax.ShapeDtypeStruct((3, N_SHARD, 2, Do, D), BF16)],
        scratch_shapes=[pltpu.VMEM((D, Wc), BF16), pltpu.VMEM((3, 2, Do, D), BF16),
                        pltpu.SemaphoreType.DMA((12,)), pltpu.SemaphoreType.DMA((12,)), pltpu.SemaphoreType.DMA((2,))],
        compiler_params=_cparams(None, VMEM_LIMIT),
    )(w_in_s, w3_s)


def _pair_exchange_in(dw_other):
    def build(ins, outs, send, recv):
        return [_remote(ins[0], outs[0], send, recv, 0, _peer(_position(), 1))]

    return _Exchange((dw_other,), (jax.ShapeDtypeStruct(dw_other.shape, F32),), 1, build)


def _pair_exchange_w3(dw3):
    _, _, _, Do, D = dw3.shape

    def build(ins, outs, send, recv):
        pos = _position()
        return [_remote(ins[0].at[:, :, 1 - pos[2]], outs[0], send, recv, 0, _peer(pos, 1))]

    return _Exchange((dw3,), (jax.ShapeDtypeStruct((3, N_SHARD, Do, D), F32),), 1, build)


def _sum_pair_in(dw_mine, ri):
    Dh, Wf = dw_mine.shape
    Wc = Wf // N_SHARD
    tr = min(256, Dh)

    def body(a_ref, b_ref, o_ref):
        o_ref[...] = (a_ref[...] + b_ref[...]).astype(BF16)

    return pl.pallas_call(
        body, name="sum_pair_in", grid=(Dh // tr, N_SHARD),
        in_specs=[pl.BlockSpec((tr, Wc), lambda i, t: (i, t)), pl.BlockSpec((tr, Wc), lambda i, t: (i, t))],
        out_specs=pl.BlockSpec((None, tr, Wc), lambda i, t: (t, i, 0)),
        out_shape=jax.ShapeDtypeStruct((N_SHARD, Dh, Wc), BF16),
        compiler_params=_cparams(("parallel", "parallel")),
    )(dw_mine, ri)


def _sum_pair_w3(cidx, dw3, r3):
    _, _, _, Do, D = dw3.shape

    def body(c_ref, a_ref, b_ref, o_ref):
        o_ref[...] = (a_ref[...] + b_ref[...]).astype(BF16)

    return pl.pallas_call(
        body, name="sum_pair_w3",
        grid_spec=pltpu.PrefetchScalarGridSpec(
            num_scalar_prefetch=1, grid=(3, N_SHARD),
            in_specs=[pl.BlockSpec((None, None, None, Do, D), lambda a, t, c: (a, t, c[0], 0, 0)),
                      pl.BlockSpec((None, None, Do, D), lambda a, t, c: (a, t, 0, 0))],
            out_specs=pl.BlockSpec((None, None, Do, D), lambda a, t, c: (a, t, 0, 0))),
        out_shape=jax.ShapeDtypeStruct((3, N_SHARD, Do, D), BF16),
        compiler_params=_cparams(("parallel", "parallel")),
    )(cidx, dw3, r3)


def _chips_exchange_in(cs_in):
    _, Dh, Wc = cs_in.shape

    def build(ins, outs, send, recv):
        pos = _position()
        return [_remote(ins[0].at[_shard_of(_peer(pos, k))], outs[0].at[j], send, recv, j, _peer(pos, k))
                for j, k in enumerate(CHIP_FLIPS)]

    return _Exchange((cs_in,), (jax.ShapeDtypeStruct((3, Dh, Wc), BF16),), 3, build)


def _chips_exchange_w3(cs_3):
    _, _, Do, D = cs_3.shape

    def build(ins, outs, send, recv):
        pos = _position()
        return [_remote(ins[0].at[:, _shard_of(_peer(pos, k))], outs[0].at[j], send, recv, j, _peer(pos, k))
                for j, k in enumerate(CHIP_FLIPS)]

    return _Exchange((cs_3,), (jax.ShapeDtypeStruct((3, 3, Do, D), BF16),), 3, build)


def _sum_chips_in(csidx, cs_in, rb_in):
    _, Dh, Wc = cs_in.shape
    tr = min(256, Dh)

    def body(s_ref, a_ref, b_ref, o_ref):
        acc = a_ref[...].astype(F32)
        for j in range(3):
            acc = acc + b_ref[j].astype(F32)
        o_ref[...] = acc

    return pl.pallas_call(
        body, name="sum_chips_in",
        grid_spec=pltpu.PrefetchScalarGridSpec(
            num_scalar_prefetch=1, grid=(Dh // tr,),
            in_specs=[pl.BlockSpec((None, tr, Wc), lambda i, s: (s[1], i, 0)),
                      pl.BlockSpec((3, tr, Wc), lambda i, s: (0, i, 0))],
            out_specs=pl.BlockSpec((None, tr, Wc), lambda i, s: (s[0], i, 0))),
        out_shape=jax.ShapeDtypeStruct((2, Dh, Wc), F32),
        compiler_params=_cparams(("parallel",)),
    )(csidx, cs_in, rb_in)


def _sum_chips_w3(csidx, cs_3, rb_3):
    _, _, Do, D = cs_3.shape

    def body(s_ref, a_ref, b_ref, o_ref):
        acc = a_ref[...].astype(F32)
        for j in range(3):
            acc = acc + b_ref[j].astype(F32)
        o_ref[...] = acc

    return pl.pallas_call(
        body, name="sum_chips_w3",
        grid_spec=pltpu.PrefetchScalarGridSpec(
            num_scalar_prefetch=1, grid=(3,),
            in_specs=[pl.BlockSpec((None, None, Do, D), lambda a, s: (a, s[1], 0, 0)),
                      pl.BlockSpec((3, None, Do, D), lambda a, s: (0, a, 0, 0))],
            out_specs=pl.BlockSpec((None, None, Do, D), lambda a, s: (a, s[0], 0, 0))),
        out_shape=jax.ShapeDtypeStruct((3, 2, Do, D), F32),
        compiler_params=_cparams(("parallel",)),
    )(csidx, cs_3, rb_3)


def _rs_final(g_in, g_3):
    def body(hi_ref, h3_ref, gi_ref, g3_ref, send, recv):
        pos = _position()
        c = pos[2]
        sib = _peer(pos, 1)
        cps = [_remote(hi_ref.at[c], gi_ref.at[c], send, recv, 0, sib),
               _remote(h3_ref.at[:, c], g3_ref.at[:, c], send, recv, 1, sib)]
        for cp in cps:
            cp.start()
        _remote(hi_ref.at[1 - c], gi_ref.at[1 - c], send, recv, 0, sib).wait_recv()
        _remote(h3_ref.at[:, 1 - c], g3_ref.at[:, 1 - c], send, recv, 1, sib).wait_recv()
        for cp in cps:
            cp.wait_send()

    return pl.pallas_call(
        body, name="rs_final", in_specs=[ANY, ANY], out_specs=[ANY, ANY],
        out_shape=[jax.ShapeDtypeStruct(g_in.shape, F32), jax.ShapeDtypeStruct(g_3.shape, F32)],
        input_output_aliases={0: 0, 1: 1},
        scratch_shapes=[pltpu.SemaphoreType.DMA((2,)), pltpu.SemaphoreType.DMA((2,))],
    )(g_in, g_3)


def _adam_math(w, g, m, v):
    m = ADAM_B1 * m + (1.0 - ADAM_B1) * g
    v = ADAM_B2 * v + (1.0 - ADAM_B2) * (g * g)
    m_hat = m / (1.0 - ADAM_B1 ** ADAM_STEP)
    v_hat = v / (1.0 - ADAM_B2 ** ADAM_STEP)
    delta = -ADAM_LR * (m_hat / (jnp.sqrt(v_hat) + ADAM_EPS) + ADAM_WD * w)
    return delta, m, v


def _adamw(w, g, m, v, name):
    R, C = w.shape
    tr = min(128, R)

    def body(w_ref, g_ref, m_ref, v_ref, d_ref, nm_ref, nv_ref):
        d_ref[...], nm_ref[...], nv_ref[...] = _adam_math(w_ref[...], g_ref[...], m_ref[...], v_ref[...])

    blk = pl.BlockSpec((tr, C), lambda i: (i, 0))
    return pl.pallas_call(
        body, name=name, grid=(R // tr,), in_specs=[blk] * 4, out_specs=[blk] * 3,
        out_shape=[jax.ShapeDtypeStruct((R, C), F32)] * 3,
        compiler_params=_cparams(("parallel",), VMEM_LIMIT),
    )(w, g, m, v)


SMALL_ROWS = ("c_ctx", "norm_w", "conv_b", "gn_w", "final_norm_w")


def _bwd_small(stats, act, ada_w, p_row, p_ab, p_cw, p_dl):
    D = act.shape[1]
    Wm = ada_w.shape[1]
    Dq = p_cw.shape[2]
    n_out = 2 + 4 * 8

    def body(stx, stm, stc, stv, stg, stl, stlc, act_ref, aw_ref, prow, pab, pcw, pdl, *rest):
        gaw_ref, loss_ref = rest[0:2]
        o_q = [rest[2 + 8 * q:2 + 8 * (q + 1)] for q in range(4)]
        vec_ref, vbuf, dm, dm_sh, gcw, amine, abuf, s_v, r_v, s_a, r_a = rest[n_out:]
        pos = _position()
        me, s = _dev_id(pos), _shard_of(pos)
        vec_ref[...] = jnp.zeros_like(vec_ref)
        vec_ref[0:2, :] = stx[0:2, :]
        vec_ref[2:3, :] = stm[1:2, :]
        vec_ref[3:5, :] = stc[0:2, :]
        vec_ref[5:6, :] = stx[2:3, :] + stc[2:3, :]
        vec_ref[6:7, :] = stv[3:4, :]
        vec_ref[7:8, :] = stg[0:1, :]
        vec_ref[8:9, :] = stm[0:1, :]
        vec_ref[9:12, :] = stv[0:3, :]
        vec_ref[12:14, 0:128] = stl[0:2, :] + stlc[0:2, :]
        vec_ref[14:15, :] = stm[2:3, :]
        vbuf[me] = vec_ref[...]
        sends = [_remote(vec_ref, vbuf.at[me], s_v, r_v, k - 1, _peer(pos, k)) for k in range(1, 8)]
        for cp in sends:
            cp.start()
        for k in range(1, 8):
            _remote(vec_ref, vbuf.at[_dev_id(_peer(pos, k))], s_v, r_v, k - 1, _peer(pos, k)).wait_recv()
        tot = vbuf[0]
        for d in range(1, N_DEV):
            tot = tot + vbuf[d]
        loss_ref[...] = jnp.zeros((8, 128), F32) + (0.5 / D) * _sum_all(tot[14:15, :])
        dm[...] = jnp.zeros_like(dm)
        for d in range(N_DEV):
            for r in range(3):
                dm[d:d + 1, r * D:(r + 1) * D] = vbuf[d, r:r + 1, :]
        dm[8:9, 0:D] = tot[3:4, :]
        dm[8:9, D:2 * D] = tot[4:5, :]
        for t in range(N_SHARD):
            @pl.when(s == t)
            def _(t=t):
                dm_sh[...] = dm[:, t * Wm:(t + 1) * Wm]
                gcw[...] = tot[9:12, t * Dq:(t + 1) * Dq]
        gaw_ref[...] = lax.dot_general(act_ref[...], dm_sh[...], (((0,), (0,)), ((), ())),
                                       precision=lax.Precision.HIGHEST, preferred_element_type=F32)
        part = lax.dot_general(dm_sh[8:16, :], aw_ref[...], (((1,), (1,)), ((), ())),
                               precision=lax.Precision.HIGHEST, preferred_element_type=F32)
        amine[...] = part
        abuf[s] = part
        asend = [_remote(amine, abuf.at[s], s_a, r_a, j, _peer(pos, k)) for j, k in enumerate(CHIP_FLIPS)]
        for cp in asend:
            cp.start()
        for j, k in enumerate(CHIP_FLIPS):
            _remote(amine, abuf.at[_shard_of(_peer(pos, k))], s_a, r_a, j, _peer(pos, k)).wait_recv()
        da = abuf[0]
        for t in range(1, N_SHARD):
            da = da + abuf[t]
        cc = prow[0, 0:1, :]
        sg = _sigmoid(cc)
        g_cctx = da[0:1, :] * (sg * (1.0 + cc * (1.0 - sg)))

        def place_all(o, val):
            o[...] = val

        def emit(k, w, g, m, v, place=place_all):
            for q, val in enumerate((g,) + _adam_math(w, g, m, v)):
                place(o_q[q][k], val)

        g_rows = [g_cctx, tot[5:6, :], tot[6:7, :], tot[7:8, :], tot[8:9, :]]
        for k, g in enumerate(g_rows):
            emit(k, prow[0, k:k + 1, :], g, prow[1, k:k + 1, :], prow[2, k:k + 1, :])

        def place_ab(o, val):
            for r in range(3):
                o[0:1, r * D:(r + 1) * D] = val[r:r + 1, :]

        g_ab = jnp.concatenate([tot[0:1, :] + tot[3:4, :], tot[1:2, :] + tot[4:5, :], tot[2:3, :]], axis=0)
        emit(5, pab[0], g_ab, pab[1], pab[2], place_ab)
        emit(6, pcw[0], gcw[...], pcw[1], pcw[2])
        g_dl = jnp.concatenate([tot[12:14, 0:128] * _sigmoid(-pdl[0, 0:2, :]), jnp.zeros((6, 128), F32)], axis=0)
        emit(7, pdl[0], g_dl, pdl[1], pdl[2])
        for cp in sends + asend:
            cp.wait_send()

    row, f32 = (lambda *shape: jax.ShapeDtypeStruct(shape, F32)), F32
    per_q = [row(1, D)] * 5 + [row(1, 3 * D), row(3, Dq), row(8, 128)]
    res = pl.pallas_call(
        body, name="bwd_small",
        in_specs=[VMEM_FULL] * 13, out_specs=[VMEM_FULL] * n_out,
        out_shape=[row(D, Wm), row(8, 128)] + per_q * 4,
        scratch_shapes=[pltpu.VMEM((16, D), f32), pltpu.VMEM((N_DEV, 16, D), f32), pltpu.VMEM((16, 3 * D), f32),
                        pltpu.VMEM((16, Wm), f32), pltpu.VMEM((3, Dq), f32), pltpu.VMEM((8, D), f32),
                        pltpu.VMEM((N_SHARD, 8, D), f32),
                        pltpu.SemaphoreType.DMA((7,)), pltpu.SemaphoreType.DMA((7,)),
                        pltpu.SemaphoreType.DMA((3,)), pltpu.SemaphoreType.DMA((3,))],
        compiler_params=_cparams(None, VMEM_LIMIT),
    )(*stats, act, ada_w, p_row, p_ab, p_cw, p_dl)
    return res[0], res[1], [res[2 + 8 * q:2 + 8 * (q + 1)] for q in range(4)]


def _pad_rows(a, rows=8):
    return jnp.pad(a, ((0, rows - a.shape[0]), (0, 0)))


def kernel(x, c, ctx, c_ctx, norm_w, ada_w, ada_b, w_in, conv_w, conv_b, decay_logit, gn_w, w_a, w_b, w_out, final_norm_w, loss_target, m_c_ctx, m_norm_w, m_ada_w, m_ada_b, m_w_in, m_conv_w, m_conv_b, m_decay_logit, m_gn_w, m_w_a, m_w_b, m_w_out, m_final_norm_w, v_c_ctx, v_norm_w, v_ada_w, v_ada_b, v_w_in, v_conv_w, v_conv_b, v_decay_logit, v_gn_w, v_w_a, v_w_b, v_w_out, v_final_norm_w):
    L, D = x.shape[1], x.shape[2]
    H = D // DV
    Wc = w_in.shape[2]
    Do = D // 8
    pos = _position()
    me = _dev_id(pos)
    cidx = jnp.reshape(pos[2], (1,)).astype(jnp.int32)
    sidx = jnp.reshape(_shard_of(pos), (1,)).astype(jnp.int32)

    act, mod, conv_w8 = _fwd_small(_pad_rows(c), _pad_rows(c_ctx[None]), ada_w[0], ada_b, _pad_rows(conv_w[0]))
    mod_x = lax.dynamic_slice_in_dim(mod, me, 1, axis=0).reshape(3, D)
    mod_c = mod[8].reshape(3, D)
    lg = jax.nn.log_sigmoid(decay_logit[0])

    w3_s = jnp.stack([w_a[0], w_b[0], w_out[0]]).reshape(3, 2, Do, D)
    w_in_full, w3_full = _ag_weights(w_in[0], w3_s)
    w3_full = w3_full.reshape(3, D, D)

    csidx = jnp.concatenate([cidx, sidx])
    groups, dret_c, xmt, cmt, dx1, sc_x, gh_3, sts = _local_step(
        x[0], ctx[0], loss_target[0], mod_x, mod_c, norm_w, conv_w8, conv_b, lg, gn_w, final_norm_w[None],
        w_in_full, w3_full, csidx)
    st_mid, st_conv, st_gn, st_lg, st_lgc, st_c = sts

    (dw_other,) = _dw_in(1 - cidx, xmt, groups, cmt, dret_c, D, "dw_in_other")
    dw_mine, ra_in = _dw_in(cidx, xmt, groups, cmt, dret_c, D, "dw_in_mine", _pair_exchange_in(dw_other))
    cs_in = _sum_pair_in(dw_mine, ra_in)
    grad_x, st_x, rb_in = _dxm(groups, 0, w_in_full, x[0], norm_w, sc_x, dx1, "dxm_x", _chips_exchange_in(cs_in))
    gh_in = _sum_chips_in(csidx, cs_in, rb_in)
    g_in, g_3 = _rs_final(gh_in, gh_3)
    g_w_in = g_in.reshape(D, Wc)
    g_3 = g_3.reshape(3, D // 4, D)

    zeros3 = jnp.zeros((3, D), F32)
    p_row = jnp.concatenate(
        [r for t in ((c_ctx[None], norm_w, conv_b, gn_w, final_norm_w[None], zeros3),
                     (m_c_ctx[None], m_norm_w, m_conv_b, m_gn_w, m_final_norm_w[None], zeros3),
                     (v_c_ctx[None], v_norm_w, v_conv_b, v_gn_w, v_final_norm_w[None], zeros3)) for r in t],
        axis=0).reshape(3, 8, D)
    p_ab = jnp.concatenate([ada_b, m_ada_b, v_ada_b], axis=0).reshape(3, 3, D)
    p_cw = jnp.concatenate([conv_w, m_conv_w, v_conv_w], axis=0)
    p_dl = jnp.pad(jnp.concatenate([decay_logit, m_decay_logit, v_decay_logit], axis=0), ((0, 0), (0, 6), (0, 128 - H)))
    g_ada_w, loss_t, small = _bwd_small((st_x, st_mid, st_c, st_conv, st_gn, st_lg, st_lgc), act, ada_w[0],
                                        p_row, p_ab, p_cw, p_dl)

    upd_in = _adamw(w_in[0], g_w_in, m_w_in[0], v_w_in[0], "adamw_w_in")
    upd_ada = _adamw(ada_w[0], g_ada_w, m_ada_w[0], v_ada_w[0], "adamw_ada_w")
    upd_a = _adamw(w_a[0], g_3[0], m_w_a[0], v_w_a[0], "adamw_w_a")
    upd_b = _adamw(w_b[0], g_3[1], m_w_b[0], v_w_b[0], "adamw_w_b")
    upd_o = _adamw(w_out[0], g_3[2], m_w_out[0], v_w_out[0], "adamw_w_out")

    def leaves(q):
        big = lambda g, upd: (g if q == 0 else upd[q - 1])[None]
        r_cctx, r_norm, r_convb, r_gn, r_fnorm, r_ab, r_cw, r_dl = small[q]
        return [r_cctx.reshape(D), r_norm, big(g_ada_w, upd_ada), r_ab, big(g_w_in, upd_in),
                r_cw[None], r_convb, r_dl[0:2, 0:H][None], r_gn,
                big(g_3[0], upd_a), big(g_3[1], upd_b), big(g_3[2], upd_o), r_fnorm.reshape(D)]

    loss = loss_t[0, 0]
    return (loss, grad_x[None], *leaves(0), *leaves(1), *leaves(2), *leaves(3))
```

```python
from typing import Callable, NamedTuple

import jax
import jax.numpy as jnp
from jax import lax
from jax.experimental import pallas as pl
from jax.experimental.pallas import tpu as pltpu

F32 = jnp.float32
BF16 = jnp.bfloat16
MESH = pl.DeviceIdType.MESH

CHUNK = 128
RET_CPB = 4
DV = 128
DK = 64
GRID_W = 64
ROPE_BASE = 10000.0
EPS = 1e-6
K_SCALE = DK ** -0.5
N_SHARD = 4
N_DEV = 8

ADAM_LR = 0.001
ADAM_B1 = 0.9
ADAM_B2 = 0.999
ADAM_EPS = 1e-08
ADAM_WD = 0.01
ADAM_STEP = 10

VMEM_LIMIT = 56 * 1024 * 1024


def _cparams(sem=None, vmem=None):
    kw = {}
    if sem is not None:
        kw["dimension_semantics"] = sem
    if vmem is not None:
        kw["vmem_limit_bytes"] = vmem
    return pltpu.CompilerParams(**kw)


def _dot(a, b):
    return jnp.dot(a, b, preferred_element_type=F32)


def _dot_nt(a, b):
    return lax.dot_general(a, b, (((1,), (1,)), ((), ())), preferred_element_type=F32)


def _dot_tn(a, b):
    return lax.dot_general(a, b, (((0,), (0,)), ((), ())), preferred_element_type=F32)


def _sigmoid(x):
    return 1.0 / (1.0 + jnp.exp(-x))


def _sum_all(x):
    return jnp.sum(jnp.sum(x, axis=1, keepdims=True), axis=0, keepdims=True)


def _swap_halves(t):
    n = t.shape[1]
    lane = lax.broadcasted_iota(jnp.int32, t.shape, 1)
    low = (lane & 32) == 0
    return jnp.where(low, pltpu.roll(t, n - 32, 1), pltpu.roll(t, 32, 1))


def _vec_spec(d):
    return pl.BlockSpec((1, d), lambda *a: (0, 0))


def _norm_mod(x, nw, sc, sh, name):
    L, D = x.shape
    tl = min(256, L)

    def body(x_ref, nw_ref, sc_ref, sh_ref, xm_ref, xmt_ref):
        xv = x_ref[...]
        r = lax.rsqrt(jnp.mean(xv * xv, axis=-1, keepdims=True) + EPS)
        xm = (xv * r * nw_ref[...]) * (1.0 + sc_ref[...]) + sh_ref[...]
        xm_ref[...] = xm.astype(BF16)
        xmt_ref[...] = xm.T.astype(BF16)

    return pl.pallas_call(
        body, name=name, grid=(L // tl,),
        in_specs=[pl.BlockSpec((tl, D), lambda i: (i, 0)), _vec_spec(D), _vec_spec(D), _vec_spec(D)],
        out_specs=[pl.BlockSpec((tl, D), lambda i: (i, 0)), pl.BlockSpec((D, tl), lambda i: (0, i))],
        out_shape=[jax.ShapeDtypeStruct((L, D), BF16), jax.ShapeDtypeStruct((D, L), BF16)],
        compiler_params=_cparams(("parallel",)),
    )(x, nw, sc, sh)


def _in_proj(xm, w, name):
    M, D = xm.shape
    N = w.shape[1]
    tm = min(1024, M)

    def body(a_ref, b_ref, o_ref, qk_ref):
        acc = _dot(a_ref[...], b_ref[...])
        o_ref[...] = acc.astype(o_ref.dtype)

        @pl.when(pl.program_id(1) == 4)
        def _():
            qk_ref[...] = acc

    return pl.pallas_call(
        body, name=name, grid=(M // tm, N // D),
        in_specs=[pl.BlockSpec((tm, D), lambda i, j: (i, 0)), pl.BlockSpec((D, D), lambda i, j: (0, j))],
        out_specs=[pl.BlockSpec((tm, D), lambda i, j: (i, j)), pl.BlockSpec((tm, D), lambda i, j: (i, 0))],
        out_shape=[jax.ShapeDtypeStruct((M, N), BF16), jax.ShapeDtypeStruct((M, D), F32)],
        compiler_params=_cparams(("parallel", "arbitrary")),
    )(xm, w)


def _halo_specs(tl, L, D, col):
    hb = tl // 16
    last = L // 16 - 1
    prev = pl.BlockSpec((16, D), lambda i: (jnp.maximum(i * hb - 1, 0), col))
    nxt = pl.BlockSpec((16, D), lambda i: (jnp.minimum((i + 1) * hb, last), col))
    return prev, nxt


def _shift_rows(u, above, below):
    tl = u.shape[0]
    row = lax.broadcasted_iota(jnp.int32, u.shape, 0)
    dn = jnp.where(row == 0, above, pltpu.roll(u, 1, 0))
    up = jnp.where(row == tl - 1, below, pltpu.roll(u, tl - 1, 0))
    return dn, up


def _conv_gate_fwd(p, conv_w, conv_b, D):
    L = p.shape[0]
    tl = min(256, L)
    nt = L // tl

    def body(h_ref, bg_ref, cg_ref, za_ref, hp_ref, hn_ref, cp_ref, cn_ref, w_ref, b_ref, o_ref):
        i = pl.program_id(0)
        u = cg_ref[...].astype(F32) * h_ref[...].astype(F32)
        above = cp_ref[15:16, :].astype(F32) * hp_ref[15:16, :].astype(F32)
        below = cn_ref[0:1, :].astype(F32) * hn_ref[0:1, :].astype(F32)
        above = jnp.where(i == 0, 0.0, above)
        below = jnp.where(i == nt - 1, 0.0, below)
        dn, up = _shift_rows(u, above, below)
        co = w_ref[0:1, :] * dn + w_ref[1:2, :] * u + w_ref[2:3, :] * up + b_ref[...]
        za = za_ref[...].astype(F32)
        o_ref[...] = (za * _sigmoid(za) * bg_ref[...].astype(F32) * co).astype(BF16)

    main = lambda col: pl.BlockSpec((tl, D), lambda i: (i, col))
    hp, hn = _halo_specs(tl, L, D, 0)
    cp, cn = _halo_specs(tl, L, D, 2)
    return pl.pallas_call(
        body, name="conv_gate_fwd", grid=(nt,),
        in_specs=[main(0), main(1), main(2), main(3), hp, hn, cp, cn,
                  pl.BlockSpec((8, D), lambda i: (0, 0)), _vec_spec(D)],
        out_specs=pl.BlockSpec((tl, D), lambda i: (i, 0)),
        out_shape=jax.ShapeDtypeStruct((L, D), BF16),
        compiler_params=_cparams(("parallel",)),
    )(p, p, p, p, p, p, p, p, conv_w, conv_b)


def _rope_tables(L):
    pos = jnp.arange(L)
    row = (pos // GRID_W).astype(F32)
    col = (pos % GRID_W).astype(F32)
    nf = DK // 4
    inv = ROPE_BASE ** (-jnp.arange(nf, dtype=F32) / nf)
    ang = jnp.concatenate([row[:, None] * inv, col[:, None] * inv], axis=-1)
    cos, sin = jnp.cos(ang), jnp.sin(ang)
    return jnp.concatenate([cos, cos, cos, cos], axis=-1), jnp.concatenate([-sin, sin, -sin, sin], axis=-1)


def _rope_fwd(pqk, c2, s2, D):
    L = pqk.shape[0]
    W = D // 2
    tl = min(256, L)

    def body(q_ref, k_ref, c_ref, s_ref, qo_ref, ko_ref):
        c, s = c_ref[...], s_ref[...]
        for pr in range(W // 128):
            ps = slice(pr * 128, (pr + 1) * 128)
            q = q_ref[:, ps]
            k = k_ref[:, ps] * K_SCALE
            qo_ref[:, ps] = (q * c + _swap_halves(q) * s).astype(BF16)
            ko_ref[:, ps] = (k * c + _swap_halves(k) * s).astype(BF16)

    blk = lambda col: pl.BlockSpec((tl, W), lambda i: (i, col))
    tab = pl.BlockSpec((tl, 128), lambda i: (i, 0))
    return pl.pallas_call(
        body, name="rope_fwd", grid=(L // tl,),
        in_specs=[blk(0), blk(1), tab, tab],
        out_specs=[blk(0), blk(0)],
        out_shape=[jax.ShapeDtypeStruct((L, W), BF16)] * 2,
        compiler_params=_cparams(("parallel",)),
    )(pqk, pqk, c2, s2)


def _smem_spec():
    return pl.BlockSpec(memory_space=pltpu.SMEM)


def _pair_select(e0, e1):
    row = lax.broadcasted_iota(jnp.int32, e0.shape, 0)
    return jnp.where(row < DK, e0, e1)


def _head_lane_mask(shape, e):
    lane = lax.broadcasted_iota(jnp.int32, shape, 1)
    return (lane < DK) if e == 0 else (lane >= DK)


def _ctx_states(pc, pqk_c, lg, D):
    Lc = pc.shape[0]
    H = D // DV

    def body(lg_ref, k_ref, v_ref, s_ref):
        m = lax.broadcasted_iota(jnp.int32, (Lc, DV), 0).astype(F32)
        for pr in range(H // 2):
            k2 = k_ref[:, pr * 128:(pr + 1) * 128].astype(F32) * K_SCALE
            res = [[None, None], [None, None]]
            for e in range(2):
                h = 2 * pr + e
                v = v_ref[:, h * DV:(h + 1) * DV]
                dec_f = jnp.exp(lg_ref[0, h] * (Lc - 1.0 - m))
                dec_b = jnp.exp(lg_ref[1, h] * m)
                res[0][e] = _dot_tn((k2 * dec_f).astype(BF16), v)
                res[1][e] = _dot_tn((k2 * dec_b).astype(BF16), v)
            s_ref[0, pr] = _pair_select(res[0][0], res[0][1])
            s_ref[1, pr] = _pair_select(res[1][0], res[1][1])

    return pl.pallas_call(
        body, name="ctx_states", grid=(1,),
        in_specs=[_smem_spec(), pl.BlockSpec((Lc, D // 2), lambda i: (0, 1)), pl.BlockSpec((Lc, D), lambda i: (0, 5))],
        out_specs=pl.BlockSpec((2, H // 2, 128, 128), lambda i: (0, 0, 0, 0)),
        out_shape=jax.ShapeDtypeStruct((2, H // 2, 128, 128), F32),
    )(lg, pqk_c, pc)


T_M, T_MT = 0, 1
T_MF1, T_MB1 = 2, 3
T_QF, T_QB = 4, 5
T_KF, T_KB = 6, 7


def _decay_tables(lg, H):
    def body(lg_ref, t_ref):
        h = pl.program_id(0)
        lgf, lgb = lg_ref[0, h], lg_ref[1, h]
        i = lax.broadcasted_iota(jnp.int32, (CHUNK, CHUNK), 0).astype(F32)
        j = lax.broadcasted_iota(jnp.int32, (CHUNK, CHUNK), 1).astype(F32)
        d = i - j
        mf = jnp.where(d > 0, jnp.exp(lgf * jnp.maximum(d, 0.0)), 0.0)
        mb = jnp.where(d < 0, jnp.exp(lgb * jnp.maximum(-d, 0.0)), 0.0)
        mf_t = jnp.where(d < 0, jnp.exp(lgf * jnp.maximum(-d, 0.0)), 0.0)
        mb_t = jnp.where(d > 0, jnp.exp(lgb * jnp.maximum(d, 0.0)), 0.0)
        diag = jnp.where(d == 0, 2.0, 0.0)
        t_ref[0, T_M] = mf + mb + diag
        t_ref[0, T_MT] = mf_t + mb_t + diag
        t_ref[0, T_MF1] = mf * d
        t_ref[0, T_MB1] = mb * (-d)
        t_ref[0, T_QF] = jnp.exp(lgf * (i + 1.0))
        t_ref[0, T_QB] = jnp.exp(lgb * (CHUNK - i))
        t_ref[0, T_KF] = jnp.exp(lgf * (CHUNK - 1.0 - i))
        t_ref[0, T_KB] = jnp.exp(lgb * i)

    return pl.pallas_call(
        body, name="decay_tables", grid=(H,), in_specs=[_smem_spec()],
        out_specs=pl.BlockSpec((1, 8, CHUNK, CHUNK), lambda h: (h, 0, 0, 0)),
        out_shape=jax.ShapeDtypeStruct((H, 8, CHUNK, CHUNK), F32),
    )(lg)


def _tab_spec(H):
    return pl.BlockSpec((H, 8, CHUNK, CHUNK), lambda n: (0, 0, 0, 0))


def _chunk_decay(tab_ref, h):
    return tab_ref[h, T_QF, CHUNK - 1:CHUNK, :], tab_ref[h, T_QB, 0:1, :]


def _ret_states(kr, p, s0, tab, D):
    L = kr.shape[0]
    H = D // DV
    N = L // CHUNK
    HP = H // 2

    def body(tab_ref, kf_ref, kb_ref, vf_ref, vb_ref, s0_ref, sf_out, sb_out, sf, sb):
        n = pl.program_id(0)

        @pl.when(n == 0)
        def _():
            sf[...] = s0_ref[0]
            sb[...] = s0_ref[1]

        for cc in range(RET_CPB):
            cf_, cb_ = cc, RET_CPB - 1 - cc
            rf, rb = slice(cf_ * CHUNK, (cf_ + 1) * CHUNK), slice(cb_ * CHUNK, (cb_ + 1) * CHUNK)
            sf_out[cf_] = sf[...]
            sb_out[cb_] = sb[...]
            for pr in range(HP):
                kf2 = kf_ref[rf, pr * 128:(pr + 1) * 128].astype(F32)
                kb2 = kb_ref[rb, pr * 128:(pr + 1) * 128].astype(F32)
                inc_f, inc_b, gf, gb = [], [], [], []
                for e in range(2):
                    h = 2 * pr + e
                    inc_f.append(_dot_tn((kf2 * tab_ref[h, T_KF]).astype(BF16), vf_ref[rf, h * DV:(h + 1) * DV]))
                    inc_b.append(_dot_tn((kb2 * tab_ref[h, T_KB]).astype(BF16), vb_ref[rb, h * DV:(h + 1) * DV]))
                    cf, cb = _chunk_decay(tab_ref, h)
                    gf.append(jnp.broadcast_to(cf, (128, 128)))
                    gb.append(jnp.broadcast_to(cb, (128, 128)))
                sf[pr] = _pair_select(gf[0], gf[1]) * sf[pr] + _pair_select(inc_f[0], inc_f[1])
                sb[pr] = _pair_select(gb[0], gb[1]) * sb[pr] + _pair_select(inc_b[0], inc_b[1])

    st = jax.ShapeDtypeStruct((N, HP, 128, 128), F32)
    R = RET_CPB * CHUNK
    NB = N // RET_CPB
    return pl.pallas_call(
        body, name="ret_states", grid=(NB,),
        in_specs=[_tab_spec(H),
                  pl.BlockSpec((R, D // 2), lambda n: (n, 0)),
                  pl.BlockSpec((R, D // 2), lambda n: (NB - 1 - n, 0)),
                  pl.BlockSpec((R, D), lambda n: (n, 5)),
                  pl.BlockSpec((R, D), lambda n: (NB - 1 - n, 5)),
                  pl.BlockSpec((2, HP, 128, 128), lambda n: (0, 0, 0, 0))],
        out_specs=[pl.BlockSpec((RET_CPB, HP, 128, 128), lambda n: (n, 0, 0, 0)),
                   pl.BlockSpec((RET_CPB, HP, 128, 128), lambda n: (NB - 1 - n, 0, 0, 0))],
        out_shape=[st, st],
        scratch_shapes=[pltpu.VMEM((HP, 128, 128), F32), pltpu.VMEM((HP, 128, 128), F32)],
        compiler_params=_cparams(("arbitrary",)),
    )(tab, kr, kr, p, p, s0)


def _ret_out(qr, kr, p, sf_prev, sb_prev, gn_w, tab, D):
    L = qr.shape[0]
    H = D // DV
    N = L // CHUNK
    HP = H // 2

    def body(tab_ref, q_ref, k_ref, v_ref, zb_ref, sf_ref, sb_ref, gn_ref, o_ref, yb_ref):
        def chunk(cc, carry):
            rows = pl.ds(pl.multiple_of(cc * CHUNK, CHUNK), CHUNK)
            for pr in range(HP):
                q2 = q_ref[rows, pr * 128:(pr + 1) * 128]
                k2 = k_ref[rows, pr * 128:(pr + 1) * 128]
                sfp = sf_ref[cc, pr].astype(BF16)
                sbp = sb_ref[cc, pr].astype(BF16)
                for e in range(2):
                    h = 2 * pr + e
                    sl = slice(h * DV, (h + 1) * DV)
                    qm = jnp.where(_head_lane_mask(q2.shape, e), q2, jnp.zeros_like(q2))
                    a = (_dot_nt(qm, k2) * tab_ref[h, T_M]).astype(BF16)
                    qf = qm.astype(F32)
                    o = _dot(a, v_ref[rows, sl])
                    o += _dot((qf * tab_ref[h, T_QF]).astype(BF16), sfp)
                    o += _dot((qf * tab_ref[h, T_QB]).astype(BF16), sbp)
                    o_ref[rows, sl] = o
                    mu = jnp.mean(o, axis=-1, keepdims=True)
                    oc = o - mu
                    rstd = lax.rsqrt(jnp.mean(oc * oc, axis=-1, keepdims=True) + EPS)
                    zb = zb_ref[rows, sl].astype(F32)
                    yb_ref[rows, sl] = (zb * _sigmoid(zb) * (oc * rstd * gn_ref[:, sl])).astype(BF16)
            return carry

        lax.fori_loop(0, RET_CPB, chunk, 0)

    R = RET_CPB * CHUNK
    return pl.pallas_call(
        body, name="ret_out", grid=(N // RET_CPB,),
        in_specs=[_tab_spec(H),
                  pl.BlockSpec((R, D // 2), lambda n: (n, 0)),
                  pl.BlockSpec((R, D // 2), lambda n: (n, 0)),
                  pl.BlockSpec((R, D), lambda n: (n, 5)),
                  pl.BlockSpec((R, D), lambda n: (n, 6)),
                  pl.BlockSpec((RET_CPB, HP, 128, 128), lambda n: (n, 0, 0, 0)),
                  pl.BlockSpec((RET_CPB, HP, 128, 128), lambda n: (n, 0, 0, 0)),
                  _vec_spec(D)],
        out_specs=[pl.BlockSpec((R, D), lambda n: (n, 0)), pl.BlockSpec((R, D), lambda n: (n, 0))],
        out_shape=[jax.ShapeDtypeStruct((L, D), F32), jax.ShapeDtypeStruct((L, D), BF16)],
        compiler_params=_cparams(("parallel",)),
    )(tab, qr, kr, p, p, sf_prev, sb_prev, gn_w)


def _mid(ya, yb, p, x, tgt, w3, g, fw, D):
    L = x.shape[0]
    tm = min(256, L)
    nt = L // tm

    def body(ya_ref, yb_ref, ga_ref, gb_ref, x_ref, t_ref, w_hbm, g_ref, fw_ref,
             dx1_ref, dya_ref, dyb_ref, dgab_ref, dw_hbm, st_ref, w_vm, dw_acc, sem):
        i = pl.program_id(0)

        @pl.when(i == 0)
        def _():
            cp = pltpu.make_async_copy(w_hbm, w_vm, sem)
            cp.start()
            dw_acc[...] = jnp.zeros_like(dw_acc)
            st_ref[...] = jnp.zeros_like(st_ref)
            cp.wait()

        ya_b, yb_b = ya_ref[...], yb_ref[...]
        y_a = _dot(ya_b, w_vm[0])
        y_b = _dot(yb_b, w_vm[1])
        sga = _sigmoid(ga_ref[...].astype(F32))
        sgb = _sigmoid(gb_ref[...].astype(F32))
        mix_b = (sga * y_a + sgb * y_b).astype(BF16)
        y_x = _dot(mix_b, w_vm[2])
        gvec, fwv = g_ref[...], fw_ref[...]
        x1 = x_ref[...] + gvec * y_x
        r1 = lax.rsqrt(jnp.mean(x1 * x1, axis=-1, keepdims=True) + EPS)
        xh = x1 * r1
        diff = xh * fwv - t_ref[...]
        dout = diff * (1.0 / D)
        dxh = dout * fwv
        dx1 = r1 * (dxh - xh * jnp.mean(dxh * xh, axis=-1, keepdims=True))
        dx1_ref[...] = dx1
        st_ref[0:1, :] += jnp.sum(dout * xh, axis=0, keepdims=True)
        st_ref[1:2, :] += jnp.sum(dx1 * y_x, axis=0, keepdims=True)
        st_ref[2:3, :] += jnp.sum(diff * diff, axis=0, keepdims=True)
        dyx_b = (dx1 * gvec).astype(BF16)
        dmix = _dot_nt(dyx_b, w_vm[2])
        dw_acc[2] += _dot_tn(mix_b, dyx_b)
        dya_b = (dmix * sga).astype(BF16)
        dyb_b = (dmix * sgb).astype(BF16)
        dgab_ref[:, 0:D] = (dmix * y_a * sga * (1.0 - sga)).astype(BF16)
        dgab_ref[:, D:2 * D] = (dmix * y_b * sgb * (1.0 - sgb)).astype(BF16)
        dya_ref[...] = _dot_nt(dya_b, w_vm[0])
        dyb_ref[...] = _dot_nt(dyb_b, w_vm[1])
        dw_acc[0] += _dot_tn(ya_b, dya_b)
        dw_acc[1] += _dot_tn(yb_b, dyb_b)

        @pl.when(i == nt - 1)
        def _():
            out = pltpu.make_async_copy(dw_acc, dw_hbm, sem)
            out.start()
            out.wait()

    row = lambda col: pl.BlockSpec((tm, D), lambda i: (i, col))
    any_spec = pl.BlockSpec(memory_space=pl.ANY)
    f32o = jax.ShapeDtypeStruct((L, D), F32)
    return pl.pallas_call(
        body, name="mid", grid=(nt,),
        in_specs=[row(0), row(0), row(7), row(8), row(0), row(0), any_spec, _vec_spec(D), _vec_spec(D)],
        out_specs=[row(0), row(0), row(0), pl.BlockSpec((tm, 2 * D), lambda i: (i, 0)), any_spec,
                   pl.BlockSpec((8, D), lambda i: (0, 0))],
        out_shape=[f32o, f32o, f32o, jax.ShapeDtypeStruct((L, 2 * D), BF16),
                   jax.ShapeDtypeStruct((3, D, D), F32), jax.ShapeDtypeStruct((8, D), F32)],
        scratch_shapes=[pltpu.VMEM((3, D, D), BF16), pltpu.VMEM((3, D, D), F32), pltpu.SemaphoreType.DMA],
        compiler_params=_cparams(("arbitrary",), VMEM_LIMIT),
    )(ya, yb, p, p, x, tgt, w3, g, fw)


def _conv_bwd(dya, p, conv_w, conv_b, D, exchange=None):
    L = p.shape[0]
    tl = min(256, L)
    nt = L // tl

    def body(d_ref, h_ref, bg_ref, cg_ref, za_ref,
             dp_ref, dn_ref, hp_ref, hn_ref, bp_ref, bn_ref, cp_ref, cn_ref, zp_ref, zn_ref,
             w_ref, b_ref, dc_ref, st_ref):
        i = pl.program_id(0)

        @pl.when(i == 0)
        def _():
            st_ref[...] = jnp.zeros_like(st_ref)

        first, last = i == 0, i == nt - 1
        h = h_ref[...].astype(F32)
        cg = cg_ref[...].astype(F32)
        bg = bg_ref[...].astype(F32)
        za = za_ref[...].astype(F32)
        dy = d_ref[...].astype(F32)
        u = cg * h
        u_above = jnp.where(first, 0.0, cp_ref[15:16, :].astype(F32) * hp_ref[15:16, :].astype(F32))
        u_below = jnp.where(last, 0.0, cn_ref[0:1, :].astype(F32) * hn_ref[0:1, :].astype(F32))
        u_dn, u_up = _shift_rows(u, u_above, u_below)
        w0, w1, w2 = w_ref[0:1, :], w_ref[1:2, :], w_ref[2:3, :]
        co = w0 * u_dn + w1 * u + w2 * u_up + b_ref[...]
        sz = _sigmoid(za)
        silu = za * sz
        dc_ref[:, 3 * D:4 * D] = (dy * bg * co * (sz * (1.0 + za * (1.0 - sz)))).astype(BF16)
        dc_ref[:, D:2 * D] = (dy * silu * co).astype(BF16)
        dco = dy * silu * bg

        def edge(dr, zr, br, r):
            z = zr[r:r + 1, :].astype(F32)
            return dr[r:r + 1, :].astype(F32) * (z * _sigmoid(z)) * br[r:r + 1, :].astype(F32)

        dco_above = jnp.where(first, 0.0, edge(dp_ref, zp_ref, bp_ref, 15))
        dco_below = jnp.where(last, 0.0, edge(dn_ref, zn_ref, bn_ref, 0))
        dco_dn, dco_up = _shift_rows(dco, dco_above, dco_below)
        du = w0 * dco_up + w1 * dco + w2 * dco_dn
        dc_ref[:, 2 * D:3 * D] = (du * h).astype(BF16)
        dc_ref[:, 0:D] = (du * cg).astype(BF16)
        st_ref[0:1, :] += jnp.sum(dco * u_dn, axis=0, keepdims=True)
        st_ref[1:2, :] += jnp.sum(dco * u, axis=0, keepdims=True)
        st_ref[2:3, :] += jnp.sum(dco * u_up, axis=0, keepdims=True)
        st_ref[3:4, :] += jnp.sum(dco, axis=0, keepdims=True)

    main = lambda col: pl.BlockSpec((tl, D), lambda i: (i, col))
    halos = []
    for col in (0, 0, 1, 2, 3):
        halos.extend(_halo_specs(tl, L, D, col))
    return _riding_call(
        body, exchange, nt, name="conv_bwd",
        args=(dya, p, p, p, p, dya, dya, p, p, p, p, p, p, p, p, conv_w, conv_b),
        in_specs=[main(0), main(0), main(1), main(2), main(3)] + halos
                 + [pl.BlockSpec((8, D), lambda i: (0, 0)), _vec_spec(D)],
        out_specs=[pl.BlockSpec((tl, 4 * D), lambda i: (i, 0)), pl.BlockSpec((8, D), lambda i: (0, 0))],
        out_shape=[jax.ShapeDtypeStruct((L, 4 * D), BF16), jax.ShapeDtypeStruct((8, D), F32)],
        cparams=_cparams(("arbitrary",)))


def _ret_bwd_pre(dyb, p, o, gn_w, D):
    L = o.shape[0]
    H = D // DV
    tl = min(256, L)

    def body(d_ref, zb_ref, o_ref, gn_ref, do_ref, dzb_ref, st_ref):
        @pl.when(pl.program_id(0) == 0)
        def _():
            st_ref[...] = jnp.zeros_like(st_ref)

        for h in range(H):
            sl = slice(h * DV, (h + 1) * DV)
            ov = o_ref[:, sl]
            mu = jnp.mean(ov, axis=-1, keepdims=True)
            oc = ov - mu
            rstd = lax.rsqrt(jnp.mean(oc * oc, axis=-1, keepdims=True) + EPS)
            rn = oc * rstd
            gw = gn_ref[:, sl]
            zb = zb_ref[:, sl].astype(F32)
            sz = _sigmoid(zb)
            dy = d_ref[:, sl].astype(F32)
            dzb_ref[:, sl] = (dy * (rn * gw) * (sz * (1.0 + zb * (1.0 - sz)))).astype(BF16)
            dretn = dy * (zb * sz)
            st_ref[0:1, sl] += jnp.sum(dretn * rn, axis=0, keepdims=True)
            drn = dretn * gw
            do = rstd * (drn - jnp.mean(drn, axis=-1, keepdims=True)
                         - rn * jnp.mean(drn * rn, axis=-1, keepdims=True))
            do_ref[:, sl] = do.astype(BF16)

    main = lambda col: pl.BlockSpec((tl, D), lambda i: (i, col))
    bfo = jax.ShapeDtypeStruct((L, D), BF16)
    return pl.pallas_call(
        body, name="ret_bwd_pre", grid=(L // tl,),
        in_specs=[main(0), main(6), main(0), _vec_spec(D)],
        out_specs=[main(0), main(0), pl.BlockSpec((8, D), lambda i: (0, 0))],
        out_shape=[bfo, bfo, jax.ShapeDtypeStruct((8, D), F32)],
        compiler_params=_cparams(("arbitrary",)),
    )(dyb, p, o, gn_w)


def _ret_bwd_states(qr, do, tab, D):
    L = qr.shape[0]
    H = D // DV
    N = L // CHUNK
    HP = H // 2

    def body(tab_ref, qf_ref, qb_ref, dof_ref, dob_ref, dsf_out, dsb_out, ds0_out, dsf, dsb):
        n = pl.program_id(0)

        @pl.when(n == 0)
        def _():
            dsf[...] = jnp.zeros_like(dsf)
            dsb[...] = jnp.zeros_like(dsb)

        for cc in range(RET_CPB):
            cf_, cb_ = RET_CPB - 1 - cc, cc
            rf, rb = slice(cf_ * CHUNK, (cf_ + 1) * CHUNK), slice(cb_ * CHUNK, (cb_ + 1) * CHUNK)
            dsf_out[cf_] = dsf[...]
            dsb_out[cb_] = dsb[...]
            for pr in range(HP):
                qf2 = qf_ref[rf, pr * 128:(pr + 1) * 128].astype(F32)
                qb2 = qb_ref[rb, pr * 128:(pr + 1) * 128].astype(F32)
                inc_f, inc_b, gf, gb = [], [], [], []
                for e in range(2):
                    h = 2 * pr + e
                    inc_f.append(_dot_tn((qf2 * tab_ref[h, T_QF]).astype(BF16), dof_ref[rf, h * DV:(h + 1) * DV]))
                    inc_b.append(_dot_tn((qb2 * tab_ref[h, T_QB]).astype(BF16), dob_ref[rb, h * DV:(h + 1) * DV]))
                    cf, cb = _chunk_decay(tab_ref, h)
                    gf.append(jnp.broadcast_to(cf, (128, 128)))
                    gb.append(jnp.broadcast_to(cb, (128, 128)))
                dsf[pr] = _pair_select(gf[0], gf[1]) * dsf[pr] + _pair_select(inc_f[0], inc_f[1])
                dsb[pr] = _pair_select(gb[0], gb[1]) * dsb[pr] + _pair_select(inc_b[0], inc_b[1])

        @pl.when(n == NB - 1)
        def _():
            ds0_out[0] = dsf[...]
            ds0_out[1] = dsb[...]

    st = jax.ShapeDtypeStruct((N, HP, 128, 128), F32)
    R = RET_CPB * CHUNK
    NB = N // RET_CPB
    return pl.pallas_call(
        body, name="ret_bwd_states", grid=(NB,),
        in_specs=[_tab_spec(H),
                  pl.BlockSpec((R, D // 2), lambda n: (NB - 1 - n, 0)),
                  pl.BlockSpec((R, D // 2), lambda n: (n, 0)),
                  pl.BlockSpec((R, D), lambda n: (NB - 1 - n, 0)),
                  pl.BlockSpec((R, D), lambda n: (n, 0))],
        out_specs=[pl.BlockSpec((RET_CPB, HP, 128, 128), lambda n: (NB - 1 - n, 0, 0, 0)),
                   pl.BlockSpec((RET_CPB, HP, 128, 128), lambda n: (n, 0, 0, 0)),
                   pl.BlockSpec((2, HP, 128, 128), lambda n: (0, 0, 0, 0))],
        out_shape=[st, st, jax.ShapeDtypeStruct((2, HP, 128, 128), F32)],
        scratch_shapes=[pltpu.VMEM((HP, 128, 128), F32), pltpu.VMEM((HP, 128, 128), F32)],
        compiler_params=_cparams(("arbitrary",)),
    )(tab, qr, qr, do, do)


def _ret_bwd_main(qr, kr, p, do, sf_prev, sb_prev, dsf, dsb, c2, s2, tab, D, exchange=None):
    L = qr.shape[0]
    H = D // DV
    N = L // CHUNK
    HP = H // 2
    W = D // 2

    def body(tab_ref, q_ref, k_ref, v_ref, do_ref, sf_ref, sb_ref, dsf_ref, dsb_ref, c_ref, s_ref,
             dr_ref, st_ref):
        @pl.when(pl.program_id(0) == 0)
        def _():
            st_ref[...] = jnp.zeros_like(st_ref)

        i = lax.broadcasted_iota(jnp.int32, (CHUNK, 128), 0).astype(F32)
        lane = lax.broadcasted_iota(jnp.int32, (1, 128), 1)
        rowid = lax.broadcasted_iota(jnp.int32, (128, 128), 0)

        def chunk(cc, carry):
            acc_f, acc_b = carry
            rows = pl.ds(pl.multiple_of(cc * CHUNK, CHUNK), CHUNK)
            c, s = c_ref[rows, :], s_ref[rows, :]
            for pr in range(HP):
                ps = slice(pr * 128, (pr + 1) * 128)
                q2, k2 = q_ref[rows, ps], k_ref[rows, ps]
                sf32, sb32 = sf_ref[cc, pr], sb_ref[cc, pr]
                dsf32, dsb32 = dsf_ref[cc, pr], dsb_ref[cc, pr]
                sfp, sbp = sf32.astype(BF16), sb32.astype(BF16)
                dsfp, dsbp = dsf32.astype(BF16), dsb32.astype(BF16)
                dq2 = jnp.zeros((CHUNK, 128), F32)
                dk2 = jnp.zeros((CHUNK, 128), F32)
                for e in range(2):
                    h = 2 * pr + e
                    sl = slice(h * DV, (h + 1) * DV)
                    hm = _head_lane_mask(q2.shape, e)
                    qm = jnp.where(hm, q2, jnp.zeros_like(q2))
                    km = jnp.where(hm, k2, jnp.zeros_like(k2))
                    qf, kf = qm.astype(F32), km.astype(F32)
                    v, do = v_ref[rows, sl], do_ref[rows, sl]
                    vf, dof = v.astype(F32), do.astype(F32)
                    m_t = tab_ref[h, T_MT]
                    sc = _dot_nt(qm, k2)
                    dpm = _dot_nt(do, v)
                    dsc = (dpm * tab_ref[h, T_M]).astype(BF16)
                    a_t = (_dot_nt(km, q2) * m_t).astype(BF16)
                    dsc_t = (_dot_nt(v, do) * m_t).astype(BF16)
                    dq_f, dq_b = tab_ref[h, T_QF], tab_ref[h, T_QB]
                    dk_f, dk_b = tab_ref[h, T_KF], tab_ref[h, T_KB]
                    dq = _dot(dsc, km)
                    dq += jnp.where(hm, dq_f * _dot_nt(do, sfp) + dq_b * _dot_nt(do, sbp), 0.0)
                    dk = _dot(dsc_t, qm)
                    dk += jnp.where(hm, dk_f * _dot_nt(v, dsfp) + dk_b * _dot_nt(v, dsbp), 0.0)
                    kdf = _dot((kf * dk_f).astype(BF16), dsfp)
                    kdb = _dot((kf * dk_b).astype(BF16), dsbp)
                    dr_ref[rows, D + h * DV:D + (h + 1) * DV] = (_dot(a_t, do) + kdf + kdb).astype(BF16)
                    dq2 += dq
                    dk2 += dk
                    xf = _dot((qf * dq_f).astype(BF16), sfp)
                    xb = _dot((qf * dq_b).astype(BF16), sbp)
                    pair = (rowid < DK) if e == 0 else (rowid >= DK)
                    gcf, gcb = tab_ref[h, T_QF, CHUNK - 1:CHUNK, 0:1], tab_ref[h, T_QB, 0:1, 0:1]
                    scdp = sc * dpm
                    tf = _sum_all(scdp * tab_ref[h, T_MF1]) + _sum_all(xf * dof * (i + 1.0)) \
                        + _sum_all(kdf * vf * (CHUNK - 1.0 - i)) \
                        + CHUNK * gcf * _sum_all(jnp.where(pair, dsf32 * sf32, 0.0))
                    tb = _sum_all(scdp * tab_ref[h, T_MB1]) + _sum_all(xb * dof * (CHUNK - i)) \
                        + _sum_all(kdb * vf * i) \
                        + CHUNK * gcb * _sum_all(jnp.where(pair, dsb32 * sb32, 0.0))
                    acc_f += jnp.where(lane == h, tf, 0.0)
                    acc_b += jnp.where(lane == h, tb, 0.0)
                dr_ref[rows, ps] = (dq2 * c - _swap_halves(dq2) * s).astype(BF16)
                dr_ref[rows, W + pr * 128:W + (pr + 1) * 128] = \
                    ((dk2 * c - _swap_halves(dk2) * s) * K_SCALE).astype(BF16)
            return acc_f, acc_b

        zero = jnp.zeros((1, 128), F32)
        acc_f, acc_b = lax.fori_loop(0, RET_CPB, chunk, (zero, zero))
        st_ref[0:1, :] += acc_f
        st_ref[1:2, :] += acc_b

    R = RET_CPB * CHUNK
    st_spec = pl.BlockSpec((RET_CPB, HP, 128, 128), lambda n: (n, 0, 0, 0))
    half = pl.BlockSpec((R, W), lambda n: (n, 0))
    rope = pl.BlockSpec((R, 128), lambda n: (n, 0))
    return _riding_call(
        body, exchange, N // RET_CPB, name="ret_bwd_main",
        args=(tab, qr, kr, p, do, sf_prev, sb_prev, dsf, dsb, c2, s2),
        in_specs=[_tab_spec(H), half, half,
                  pl.BlockSpec((R, D), lambda n: (n, 5)),
                  pl.BlockSpec((R, D), lambda n: (n, 0)),
                  st_spec, st_spec, st_spec, st_spec, rope, rope],
        out_specs=[pl.BlockSpec((R, 2 * D), lambda n: (n, 0)),
                   pl.BlockSpec((8, 128), lambda n: (0, 0))],
        out_shape=[jax.ShapeDtypeStruct((L, 2 * D), BF16), jax.ShapeDtypeStruct((8, 128), F32)],
        cparams=_cparams(("arbitrary",)))


def _ctx_bwd(pc, pqk_c, ds0, lg, D):
    Lc = pc.shape[0]
    H = D // DV
    HP = H // 2
    W = D // 2

    def body(lg_ref, k_ref, v_ref, ds_ref, dr_ref, st_ref):
        dqk_ref = dr_ref.at[:, 0:D]
        dv_ref = dr_ref.at[:, D:2 * D]
        m = lax.broadcasted_iota(jnp.int32, (Lc, 128), 0).astype(F32)
        lane = lax.broadcasted_iota(jnp.int32, (1, 128), 1)
        acc_f = jnp.zeros((1, 128), F32)
        acc_b = jnp.zeros((1, 128), F32)
        dqk_ref[:, 0:W] = jnp.zeros((Lc, W), BF16)
        for pr in range(HP):
            ps = slice(pr * 128, (pr + 1) * 128)
            k2 = k_ref[:, ps].astype(F32) * K_SCALE
            dsfp, dsbp = ds_ref[0, pr].astype(BF16), ds_ref[1, pr].astype(BF16)
            dk2 = jnp.zeros((Lc, 128), F32)
            for e in range(2):
                h = 2 * pr + e
                sl = slice(h * DV, (h + 1) * DV)
                hm = _head_lane_mask(k2.shape, e)
                km = jnp.where(hm, k2, 0.0)
                v = v_ref[:, sl]
                vf = v.astype(F32)
                dec_f = jnp.exp(lg_ref[0, h] * (Lc - 1.0 - m))
                dec_b = jnp.exp(lg_ref[1, h] * m)
                kdf = _dot((km * dec_f).astype(BF16), dsfp)
                kdb = _dot((km * dec_b).astype(BF16), dsbp)
                dv_ref[:, sl] = (kdf + kdb).astype(BF16)
                dk2 += jnp.where(hm, dec_f * _dot_nt(v, dsfp) + dec_b * _dot_nt(v, dsbp), 0.0)
                acc_f += jnp.where(lane == h, _sum_all(kdf * vf * (Lc - 1.0 - m)), 0.0)
                acc_b += jnp.where(lane == h, _sum_all(kdb * vf * m), 0.0)
            dqk_ref[:, W + pr * 128:W + (pr + 1) * 128] = (dk2 * K_SCALE).astype(BF16)
        st_ref[...] = jnp.zeros_like(st_ref)
        st_ref[0:1, :] = acc_f
        st_ref[1:2, :] = acc_b

    return pl.pallas_call(
        body, name="ctx_bwd", grid=(1,),
        in_specs=[_smem_spec(), pl.BlockSpec((Lc, W), lambda i: (0, 1)), pl.BlockSpec((Lc, D), lambda i: (0, 5)),
                  pl.BlockSpec((2, HP, 128, 128), lambda i: (0, 0, 0, 0))],
        out_specs=[pl.BlockSpec((Lc, 2 * D), lambda i: (0, 0)), pl.BlockSpec((8, 128), lambda i: (0, 0))],
        out_shape=[jax.ShapeDtypeStruct((Lc, 2 * D), BF16), jax.ShapeDtypeStruct((8, 128), F32)],
    )(lg, pqk_c, pc, ds0)


class _Exchange(NamedTuple):
    inputs: tuple
    out_shapes: tuple
    n_copies: int
    build: Callable


def _exchange_parts(exchange):
    if exchange is None:
        return [], [], [], [], []
    n = exchange.n_copies
    return (list(exchange.inputs), [ANY] * len(exchange.inputs), list(exchange.out_shapes),
            [ANY] * len(exchange.out_shapes), [pltpu.SemaphoreType.DMA((n,)), pltpu.SemaphoreType.DMA((n,))])


def _riding_call(body, exchange, n_steps, *, args, in_specs, out_specs, out_shape, name, cparams, scratch=()):
    ex_args, ex_in_specs, ex_shapes, ex_out_specs, ex_scratch = _exchange_parts(exchange)
    n_in, n_out, n_sc = len(args), len(out_shape), len(scratch)

    def riding(*refs):
        k = n_in + len(ex_args)
        ins, ex_in = refs[:n_in], refs[n_in:k]
        outs, ex_out = refs[k:k + n_out], refs[k + n_out:k + n_out + len(ex_shapes)]
        k += n_out + len(ex_shapes)
        own_scratch, ex_sems = refs[k:k + n_sc], refs[k + n_sc:]
        step = pl.program_id(0)
        if exchange is not None:
            @pl.when(step == 0)
            def _():
                for rc in exchange.build(ex_in, ex_out, *ex_sems):
                    rc.start()
        body(*ins, *outs, *own_scratch)
        if exchange is not None:
            @pl.when(step == n_steps - 1)
            def _():
                for rc in exchange.build(ex_in, ex_out, *ex_sems):
                    rc.wait()

    return tuple(pl.pallas_call(
        riding, name=name, grid=(n_steps,),
        in_specs=list(in_specs) + ex_in_specs, out_specs=list(out_specs) + ex_out_specs,
        out_shape=list(out_shape) + ex_shapes, scratch_shapes=list(scratch) + ex_scratch,
        compiler_params=cparams,
    )(*args, *ex_args))


def _dxm(groups, col0, w, x, nw, sc, dx1, name, exchange=None):
    L, D = x.shape
    tm = min(256, L)
    nt = L // tm
    ng = len(groups)
    widths = [g.shape[1] for g in groups]
    wtot = sum(widths)
    with_dx = dx1 is not None
    ex_args, ex_in_specs, ex_shapes, ex_out_specs, ex_scratch = _exchange_parts(exchange)
    n_in = ng + 4 + (1 if with_dx else 0)
    n_out = 2 if with_dx else 1

    def body(*refs):
        group_refs = refs[:ng]
        w_hbm, x_ref, nw_ref, sc_ref = refs[ng:ng + 4]
        ex_in = refs[n_in:n_in + len(ex_args)]
        outs = refs[n_in + len(ex_args):]
        if with_dx:
            dx1_ref, gx_ref, st_ref = refs[ng + 4], outs[0], outs[1]
        else:
            st_ref = outs[0]
        ex_out = outs[n_out:n_out + len(ex_shapes)]
        w_vm, sem = outs[n_out + len(ex_shapes):n_out + len(ex_shapes) + 2]
        ex_sems = outs[n_out + len(ex_shapes) + 2:]
        i = pl.program_id(0)

        @pl.when(i == 0)
        def _():
            cp = pltpu.make_async_copy(w_hbm.at[:, col0 * D:col0 * D + wtot], w_vm, sem)
            cp.start()
            if exchange is not None:
                for rc in exchange.build(ex_in, ex_out, *ex_sems):
                    rc.start()
            st_ref[...] = jnp.zeros_like(st_ref)
            cp.wait()

        dxm, off = None, 0
        for g_ref, wd in zip(group_refs, widths):
            part = _dot_nt(g_ref[...], w_vm[:, off:off + wd])
            dxm = part if dxm is None else dxm + part
            off += wd

        xv = x_ref[...]
        r = lax.rsqrt(jnp.mean(xv * xv, axis=-1, keepdims=True) + EPS)
        xh = xv * r
        nwv = nw_ref[...]
        dxn = dxm * (1.0 + sc_ref[...])
        st_ref[0:1, :] += jnp.sum(dxm, axis=0, keepdims=True)
        st_ref[1:2, :] += jnp.sum(dxm * (xh * nwv), axis=0, keepdims=True)
        st_ref[2:3, :] += jnp.sum(dxn * xh, axis=0, keepdims=True)
        if with_dx:
            dxh = dxn * nwv
            gx_ref[...] = dx1_ref[...] + r * (dxh - xh * jnp.mean(dxh * xh, axis=-1, keepdims=True))

        if exchange is not None:
            @pl.when(i == nt - 1)
            def _():
                for rc in exchange.build(ex_in, ex_out, *ex_sems):
                    rc.wait()

    row = pl.BlockSpec((tm, D), lambda i: (i, 0))
    in_specs = [pl.BlockSpec((tm, wd), lambda i: (i, 0)) for wd in widths] + [ANY, row, _vec_spec(D), _vec_spec(D)]
    out_specs = [pl.BlockSpec((8, D), lambda i: (0, 0))]
    out_shape = [jax.ShapeDtypeStruct((8, D), F32)]
    args = list(groups) + [w, x, nw, sc]
    if with_dx:
        in_specs.append(row)
        out_specs.insert(0, row)
        out_shape.insert(0, jax.ShapeDtypeStruct((L, D), F32))
        args.append(dx1)
    res = pl.pallas_call(
        body, name=name, grid=(nt,),
        in_specs=in_specs + ex_in_specs, out_specs=out_specs + ex_out_specs, out_shape=out_shape + ex_shapes,
        scratch_shapes=[pltpu.VMEM((D, wtot), BF16), pltpu.SemaphoreType.DMA] + ex_scratch,
        compiler_params=_cparams(("arbitrary",), VMEM_LIMIT),
    )(*args, *ex_args)
    gx = res[0] if with_dx else None
    return (gx, res[n_out - 1], *res[n_out:])


DW_TN = 512


def _dw_in(hidx, xmt, groups, cmt, dr_c, D, name, exchange=None):
    L = xmt.shape[1]
    Lc = cmt.shape[1]
    Dh = D // 2
    tn = min(DW_TN, D)
    nblk = [g.shape[1] // tn for g in groups]
    starts = [sum(nblk[:g]) for g in range(len(groups))]
    ng = len(groups)
    nj = sum(nblk)
    ex_args, ex_in_specs, ex_shapes, ex_out_specs, ex_scratch = _exchange_parts(exchange)

    def body(*refs):
        h_ref, xt_hbm = refs[0], refs[1]
        group_refs = refs[2:2 + ng]
        ct_ref, drc_ref = refs[2 + ng:4 + ng]
        ex_in = refs[4 + ng:4 + ng + len(ex_args)]
        outs = refs[4 + ng + len(ex_args):]
        o_ref = outs[0]
        ex_out = outs[1:1 + len(ex_shapes)]
        xt_vm, sem = outs[1 + len(ex_shapes):3 + len(ex_shapes)]
        ex_sems = outs[3 + len(ex_shapes):]
        j = pl.program_id(0)

        @pl.when(j == 0)
        def _():
            rows = pl.ds(pl.multiple_of(h_ref[0] * Dh, Dh), Dh)
            cp = pltpu.make_async_copy(xt_hbm.at[rows, :], xt_vm, sem)
            cp.start()
            if exchange is not None:
                for rc in exchange.build(ex_in, ex_out, *ex_sems):
                    rc.start()
            cp.wait()

        for g in range(ng):
            @pl.when((j >= starts[g]) & (j < starts[g] + nblk[g]))
            def _(g=g):
                acc = _dot(xt_vm[...], group_refs[g][...])
                if g == 1:
                    acc += _dot(ct_ref[...], drc_ref[...])
                o_ref[...] = acc

        if exchange is not None:
            @pl.when(j == nj - 1)
            def _():
                for rc in exchange.build(ex_in, ex_out, *ex_sems):
                    rc.wait()

    def group_spec(g, rows):
        return pl.BlockSpec((rows, tn), lambda j, h: (0, jnp.clip(j - starts[g], 0, nblk[g] - 1)))

    res = pl.pallas_call(
        body, name=name,
        grid_spec=pltpu.PrefetchScalarGridSpec(
            num_scalar_prefetch=1, grid=(nj,),
            in_specs=[ANY] + [group_spec(g, L) for g in range(ng)]
                     + [pl.BlockSpec((Dh, Lc), lambda j, h: (h[0], 0)), group_spec(1, Lc)] + ex_in_specs,
            out_specs=[pl.BlockSpec((Dh, tn), lambda j, h: (0, j))] + ex_out_specs,
            scratch_shapes=[pltpu.VMEM((Dh, L), BF16), pltpu.SemaphoreType.DMA] + ex_scratch),
        out_shape=[jax.ShapeDtypeStruct((Dh, nj * tn), F32)] + ex_shapes,
        compiler_params=_cparams(("arbitrary",), VMEM_LIMIT),
    )(hidx, xmt, *groups, cmt, dr_c, *ex_args)
    return tuple(res)


def _local_step(x, ctx, tgt, mod_x, mod_c, norm_w, conv_w8, conv_b, lg, gn_w, fw, project, csidx=None):
    L, D = x.shape
    sh_x, sc_x, g_x = mod_x[0:1], mod_x[1:2], mod_x[2:3]
    sh_c, sc_c = mod_c[0:1], mod_c[1:2]
    c2, s2 = _rope_tables(L)
    tab = _decay_tables(lg, D // DV)

    xm, xmt = _norm_mod(x, norm_w, sc_x, sh_x, "norm_mod_x")
    cm, cmt = _norm_mod(ctx, norm_w, sc_c, sh_c, "norm_mod_ctx")
    p, pqk, w_in, w3 = project(xm)
    pc, pqk_c = _in_proj(cm, w_in, "in_proj_ctx")
    ya = _conv_gate_fwd(p, conv_w8, conv_b, D)
    qr, kr = _rope_fwd(pqk, c2, s2, D)
    s0 = _ctx_states(pc, pqk_c, lg, D)
    sf_prev, sb_prev = _ret_states(kr, p, s0, tab, D)
    o, yb = _ret_out(qr, kr, p, sf_prev, sb_prev, gn_w, tab, D)
    dx1, dya, dyb, dgab, dw3, st_mid = _mid(ya, yb, p, x, tgt, w3, g_x, fw, D)
    reduce = csidx is not None
    dw3_5 = dw3.reshape(3, N_SHARD, 2, D // 8, D)
    dconv, st_conv, *ra_3 = _conv_bwd(dya, p, conv_w8, conv_b, D, _pair_exchange_w3(dw3_5) if reduce else None)
    do, dzb, st_gn = _ret_bwd_pre(dyb, p, o, gn_w, D)
    dsf, dsb, ds0 = _ret_bwd_states(qr, do, tab, D)
    cs_3 = _sum_pair_w3(csidx[0:1], dw3_5, ra_3[0]) if reduce else None
    dret, st_lg, *rb_3 = _ret_bwd_main(qr, kr, p, do, sf_prev, sb_prev, dsf, dsb, c2, s2, tab, D,
                                       _chips_exchange_w3(cs_3) if reduce else None)
    g_3 = _sum_chips_w3(csidx, cs_3, rb_3[0]) if reduce else dw3
    dret_c, st_lgc = _ctx_bwd(pc, pqk_c, ds0, lg, D)
    groups = (dconv, dret, dzb, dgab)
    _, st_c = _dxm((dret_c,), 4, w_in, ctx, norm_w, sc_c, None, "dxm_ctx")
    return groups, dret_c, xmt, cmt, dx1, sc_x, w_in, g_3, (st_mid, st_conv, st_gn, st_lg, st_lgc, st_c)


CHIP_FLIPS = (4, 2, 6)
ANY = pl.BlockSpec(memory_space=pl.ANY)
VMEM_FULL = pl.BlockSpec(memory_space=pltpu.VMEM)


def _position():
    return lax.axis_index("x"), lax.axis_index("y"), lax.axis_index("c")


def _peer(pos, k):
    x, y, c = pos
    return (1 - x if k & 4 else x, 1 - y if k & 2 else y, 1 - c if k & 1 else c)


def _dev_id(pos):
    return 4 * pos[0] + 2 * pos[1] + pos[2]


def _shard_of(pos):
    return 2 * pos[0] + pos[1]


def _remote(src, dst, send_sems, recv_sems, idx, to):
    return pltpu.make_async_remote_copy(src_ref=src, dst_ref=dst, send_sem=send_sems.at[idx],
                                        recv_sem=recv_sems.at[idx], device_id=to, device_id_type=MESH)


def _dot_f32(a, b):
    return jnp.dot(a, b, precision=lax.Precision.HIGHEST, preferred_element_type=F32)


def _silu(x):
    return x * _sigmoid(x)


def _fwd_small(c8, cctx8, ada_w, ada_b, conv_w8):
    D = c8.shape[1]
    Wm = ada_w.shape[1]
    Dq = conv_w8.shape[1]

    def body(c_ref, cc_ref, aw_ref, ab_ref, cw_ref, act_ref, mod_ref, cwf_ref,
             cbuf, pmine, pbuf, wbuf, s_c, r_c, s_p, r_p, s_w, r_w):
        pos = _position()
        me, s = _dev_id(pos), _shard_of(pos)
        cbuf[me] = c_ref[...]
        wbuf[s] = cw_ref[...]
        sends = [_remote(c_ref, cbuf.at[me], s_c, r_c, k - 1, _peer(pos, k)) for k in range(1, 8)]
        sends += [_remote(cw_ref, wbuf.at[s], s_w, r_w, j, _peer(pos, k)) for j, k in enumerate(CHIP_FLIPS)]
        for cp in sends:
            cp.start()
        for k in range(1, 8):
            _remote(c_ref, cbuf.at[_dev_id(_peer(pos, k))], s_c, r_c, k - 1, _peer(pos, k)).wait_recv()
        for d in range(N_DEV):
            act_ref[d:d + 1, :] = _silu(cbuf[d, 0:1, :])
        act_ref[8:9, :] = _silu(cc_ref[0:1, :])
        act_ref[9:16, :] = jnp.zeros((7, D), F32)
        part = _dot_f32(act_ref[...], aw_ref[...])
        pmine[...] = part
        pbuf[s] = part
        psend = [_remote(pmine, pbuf.at[s], s_p, r_p, j, _peer(pos, k)) for j, k in enumerate(CHIP_FLIPS)]
        for cp in psend:
            cp.start()
        for j, k in enumerate(CHIP_FLIPS):
            t = _shard_of(_peer(pos, k))
            _remote(pmine, pbuf.at[t], s_p, r_p, j, _peer(pos, k)).wait_recv()
            _remote(cw_ref, wbuf.at[t], s_w, r_w, j, _peer(pos, k)).wait_recv()
        for t in range(N_SHARD):
            mod_ref[:, t * Wm:(t + 1) * Wm] = pbuf[t] + ab_ref[:, t * Wm:(t + 1) * Wm]
            cwf_ref[:, t * Dq:(t + 1) * Dq] = wbuf[t]
        for cp in sends + psend:
            cp.wait_send()

    return pl.pallas_call(
        body, name="fwd_small",
        in_specs=[VMEM_FULL] * 5, out_specs=[VMEM_FULL] * 3,
        out_shape=[jax.ShapeDtypeStruct((16, D), F32), jax.ShapeDtypeStruct((16, 3 * D), F32),
                   jax.ShapeDtypeStruct((8, D), F32)],
        scratch_shapes=[pltpu.VMEM((N_DEV, 8, D), F32), pltpu.VMEM((16, Wm), F32),
                        pltpu.VMEM((N_SHARD, 16, Wm), F32), pltpu.VMEM((N_SHARD, 8, Dq), F32),
                        pltpu.SemaphoreType.DMA((7,)), pltpu.SemaphoreType.DMA((7,)),
                        pltpu.SemaphoreType.DMA((3,)), pltpu.SemaphoreType.DMA((3,)),
                        pltpu.SemaphoreType.DMA((3,)), pltpu.SemaphoreType.DMA((3,))],
        compiler_params=_cparams(None, VMEM_LIMIT),
    )(c8, cctx8, ada_w, ada_b, conv_w8)


def _cols(t, w):
    return pl.ds(pl.multiple_of(t * w, 128), w)


AG_CHUNKS = 3


def _ag_in_proj(xm, w_in_s, w3_s):
    L, D = xm.shape
    Wc = w_in_s.shape[1]
    Wq = Wc // AG_CHUNKS
    Dh = D // 2
    Do = w3_s.shape[2]
    TM = min(1024, L // 4)
    NT = L // TM
    RC = min(128, Dh)
    NQ = AG_CHUNKS

    def body(xm_ref, wi_hbm, w3_ref, p_hbm, pqk_hbm, fi_hbm, f3_hbm,
             w_vm, cast_buf, s3, stage, qk_stage, ici_s, ici_r, d2d_s, d2d_r, w3_s_, w3_r_, loc, out_sem, qk_sem):
        pos = _position()
        c = pos[2]
        s = _shard_of(pos)
        sib = _peer(pos, 1)
        mine = pl.ds(pl.multiple_of(c * Dh, Dh), Dh)
        other = pl.ds(pl.multiple_of((1 - c) * Dh, Dh), Dh)

        def cast_half(hf):
            def step(i, carry):
                rows = pl.ds(pl.multiple_of(hf * Dh + i * RC, RC), RC)
                cp = pltpu.make_async_copy(wi_hbm.at[rows, :], cast_buf, loc.at[0])
                cp.start()
                cp.wait()
                for q in range(NQ):
                    w_vm[0, q, rows, :] = cast_buf[:, q * Wq:(q + 1) * Wq].astype(BF16)
                return carry
            lax.fori_loop(0, Dh // RC, step, 0)

        def abs_col(t, q):
            return pl.ds(pl.multiple_of(t * Wc + q * Wq, 128), Wq)

        cast_half(c)
        for a in range(3):
            s3[a] = w3_ref[a].astype(BF16)
        sends = []
        for q in range(NQ):
            for j, k in enumerate(CHIP_FLIPS):
                sends.append(_remote(w_vm.at[0, q, mine, :], w_vm.at[1 + j, q, mine, :], ici_s, ici_r,
                                     q * 3 + j, _peer(pos, k)))
        for j, k in enumerate(CHIP_FLIPS):
            sends.append(_remote(s3.at[:, c], f3_hbm.at[:, s, c], w3_s_, w3_r_, j, _peer(pos, k)))
        for cp in sends:
            cp.start()
        cast_half(1 - c)
        local = [pltpu.make_async_copy(s3, f3_hbm.at[:, s], loc.at[1])]
        local += [pltpu.make_async_copy(w_vm.at[0, q], fi_hbm.at[:, abs_col(s, q)], loc.at[2 + q]) for q in range(NQ)]
        for cp in local:
            cp.start()

        def out_copy(slot, rows, cols):
            return pltpu.make_async_copy(stage.at[slot], p_hbm.at[rows, cols], out_sem.at[slot])

        def block(r, q, t, first):
            cols = abs_col(t, q)

            def row_tile(rt, carry):
                rows = pl.ds(pl.multiple_of(rt * TM, TM), TM)
                acc = _dot(xm_ref[rows, :], w_vm[r, q])
                slot = lax.rem(rt, 2)

                @pl.when(rt >= 2 if first else rt >= 0)
                def _():
                    out_copy(slot, rows, cols).wait()

                stage[slot] = acc.astype(BF16)
                out_copy(slot, rows, cols).start()

                def keep_f32(lo, dst_col):
                    qk_stage[...] = acc[:, lo:lo + Dh]
                    cp = pltpu.make_async_copy(qk_stage, pqk_hbm.at[rows, dst_col:dst_col + Dh], qk_sem)
                    cp.start()
                    cp.wait()

                if q == NQ - 1:
                    @pl.when(t == 1)
                    def _():
                        keep_f32(Wq - Dh, 0)
                if q == 0:
                    @pl.when(t == 2)
                    def _():
                        keep_f32(0, Dh)
                return carry

            lax.fori_loop(0, NT, row_tile, 0)

        for q in range(NQ):
            block(0, q, s, q == 0)
        passed = []
        for q in range(NQ):
            for j, k in enumerate(CHIP_FLIPS):
                r, idx = 1 + j, q * 3 + j
                t = _shard_of(_peer(pos, k))
                _remote(w_vm.at[r, q, mine, :], w_vm.at[r, q, mine, :], ici_s, ici_r, idx, sib).wait_recv()
                fwd = _remote(w_vm.at[r, q, mine, :], w_vm.at[r, q, mine, :], d2d_s, d2d_r, idx, sib)
                fwd.start()
                passed.append(fwd)
                _remote(w_vm.at[r, q, other, :], w_vm.at[r, q, other, :], d2d_s, d2d_r, idx, sib).wait_recv()
                block(r, q, t, False)
                cp = pltpu.make_async_copy(w_vm.at[r, q], fi_hbm.at[:, abs_col(t, q)], loc.at[2 + NQ + idx])
                cp.start()
                local.append(cp)
        for j, k in enumerate(CHIP_FLIPS):
            t = _shard_of(_peer(pos, k))
            _remote(s3.at[:, c], f3_hbm.at[:, t, c], w3_s_, w3_r_, j, sib).wait_recv()
            fwd = _remote(f3_hbm.at[:, t, c], f3_hbm.at[:, t, c], w3_s_, w3_r_, 3 + j, sib)
            fwd.start()
            passed.append(fwd)
        for j, k in enumerate(CHIP_FLIPS):
            t = _shard_of(_peer(pos, k))
            _remote(s3.at[:, c], f3_hbm.at[:, t, 1 - c], w3_s_, w3_r_, 3 + j, sib).wait_recv()
        for cp in sends + passed:
            cp.wait_send()
        for cp in local:
            cp.wait()
        for slot in range(2):
            out_copy(slot, pl.ds(0, TM), abs_col(s, 0)).wait()

    n_loc = 2 + NQ + 3 * NQ
    return pl.pallas_call(
        body, name="ag_in_proj",
        in_specs=[VMEM_FULL, ANY, VMEM_FULL], out_specs=[ANY, ANY, ANY, ANY],
        out_shape=[jax.ShapeDtypeStruct((L, N_SHARD * Wc), BF16), jax.ShapeDtypeStruct((L, D), F32),
                   jax.ShapeDtypeStruct((D, N_SHARD * Wc), BF16), jax.ShapeDtypeStruct((3, N_SHARD, 2, Do, D), BF16)],
        scratch_shapes=[pltpu.VMEM((N_SHARD, NQ, D, Wq), BF16), pltpu.VMEM((RC, Wc), F32),
                        pltpu.VMEM((3, 2, Do, D), BF16), pltpu.VMEM((2, TM, Wq), BF16), pltpu.VMEM((TM, Dh), F32),
                        pltpu.SemaphoreType.DMA((3 * NQ,)), pltpu.SemaphoreType.DMA((3 * NQ,)),
                        pltpu.SemaphoreType.DMA((3 * NQ,)), pltpu.SemaphoreType.DMA((3 * NQ,)),
                        pltpu.SemaphoreType.DMA((6,)), pltpu.SemaphoreType.DMA((6,)),
                        pltpu.SemaphoreType.DMA((n_loc,)), pltpu.SemaphoreType.DMA((2,)), pltpu.SemaphoreType.DMA],
        compiler_params=_cparams(None, VMEM_LIMIT),
    )(xm, w_in_s, w3_s)


def _pair_exchange_in(dw_other):
    def build(ins, outs, send, recv):
        return [_remote(ins[0], outs[0], send, recv, 0, _peer(_position(), 1))]

    return _Exchange((dw_other,), (jax.ShapeDtypeStruct(dw_other.shape, F32),), 1, build)


def _pair_exchange_w3(dw3):
    _, _, _, Do, D = dw3.shape

    def build(ins, outs, send, recv):
        pos = _position()
        return [_remote(ins[0].at[:, :, 1 - pos[2]], outs[0], send, recv, 0, _peer(pos, 1))]

    return _Exchange((dw3,), (jax.ShapeDtypeStruct((3, N_SHARD, Do, D), F32),), 1, build)


def _sum_pair_in(dw_mine, ri):
    Dh, Wf = dw_mine.shape
    Wc = Wf // N_SHARD
    tr = min(256, Dh)

    def body(a_ref, b_ref, o_ref):
        o_ref[...] = (a_ref[...] + b_ref[...]).astype(BF16)

    return pl.pallas_call(
        body, name="sum_pair_in", grid=(Dh // tr, N_SHARD),
        in_specs=[pl.BlockSpec((tr, Wc), lambda i, t: (i, t)), pl.BlockSpec((tr, Wc), lambda i, t: (i, t))],
        out_specs=pl.BlockSpec((None, tr, Wc), lambda i, t: (t, i, 0)),
        out_shape=jax.ShapeDtypeStruct((N_SHARD, Dh, Wc), BF16),
        compiler_params=_cparams(("parallel", "parallel")),
    )(dw_mine, ri)


def _sum_pair_w3(cidx, dw3, r3):
    _, _, _, Do, D = dw3.shape

    def body(c_ref, a_ref, b_ref, o_ref):
        o_ref[...] = (a_ref[...] + b_ref[...]).astype(BF16)

    return pl.pallas_call(
        body, name="sum_pair_w3",
        grid_spec=pltpu.PrefetchScalarGridSpec(
            num_scalar_prefetch=1, grid=(3, N_SHARD),
            in_specs=[pl.BlockSpec((None, None, None, Do, D), lambda a, t, c: (a, t, c[0], 0, 0)),
                      pl.BlockSpec((None, None, Do, D), lambda a, t, c: (a, t, 0, 0))],
            out_specs=pl.BlockSpec((None, None, Do, D), lambda a, t, c: (a, t, 0, 0))),
        out_shape=jax.ShapeDtypeStruct((3, N_SHARD, Do, D), BF16),
        compiler_params=_cparams(("parallel", "parallel")),
    )(cidx, dw3, r3)


def _chips_exchange_in(cs_in):
    _, Dh, Wc = cs_in.shape

    def build(ins, outs, send, recv):
        pos = _position()
        return [_remote(ins[0].at[_shard_of(_peer(pos, k))], outs[0].at[j], send, recv, j, _peer(pos, k))
                for j, k in enumerate(CHIP_FLIPS)]

    return _Exchange((cs_in,), (jax.ShapeDtypeStruct((3, Dh, Wc), BF16),), 3, build)


def _chips_exchange_w3(cs_3):
    _, _, Do, D = cs_3.shape

    def build(ins, outs, send, recv):
        pos = _position()
        return [_remote(ins[0].at[:, _shard_of(_peer(pos, k))], outs[0].at[j], send, recv, j, _peer(pos, k))
                for j, k in enumerate(CHIP_FLIPS)]

    return _Exchange((cs_3,), (jax.ShapeDtypeStruct((3, 3, Do, D), BF16),), 3, build)


def _sum_chips_in(csidx, cs_in, rb_in):
    _, Dh, Wc = cs_in.shape
    tr = min(256, Dh)

    def body(s_ref, a_ref, b_ref, o_ref):
        acc = a_ref[...].astype(F32)
        for j in range(3):
            acc = acc + b_ref[j].astype(F32)
        o_ref[...] = acc

    return pl.pallas_call(
        body, name="sum_chips_in",
        grid_spec=pltpu.PrefetchScalarGridSpec(
            num_scalar_prefetch=1, grid=(Dh // tr,),
            in_specs=[pl.BlockSpec((None, tr, Wc), lambda i, s: (s[1], i, 0)),
                      pl.BlockSpec((3, tr, Wc), lambda i, s: (0, i, 0))],
            out_specs=pl.BlockSpec((None, tr, Wc), lambda i, s: (s[0], i, 0))),
        out_shape=jax.ShapeDtypeStruct((2, Dh, Wc), F32),
        compiler_params=_cparams(("parallel",)),
    )(csidx, cs_in, rb_in)


def _sum_chips_w3(csidx, cs_3, rb_3):
    _, _, Do, D = cs_3.shape

    def body(s_ref, a_ref, b_ref, o_ref):
        acc = a_ref[...].astype(F32)
        for j in range(3):
            acc = acc + b_ref[j].astype(F32)
        o_ref[...] = acc

    return pl.pallas_call(
        body, name="sum_chips_w3",
        grid_spec=pltpu.PrefetchScalarGridSpec(
            num_scalar_prefetch=1, grid=(3,),
            in_specs=[pl.BlockSpec((None, None, Do, D), lambda a, s: (a, s[1], 0, 0)),
                      pl.BlockSpec((3, None, Do, D), lambda a, s: (0, a, 0, 0))],
            out_specs=pl.BlockSpec((None, None, Do, D), lambda a, s: (a, s[0], 0, 0))),
        out_shape=jax.ShapeDtypeStruct((3, 2, Do, D), F32),
        compiler_params=_cparams(("parallel",)),
    )(csidx, cs_3, rb_3)


def _rs_final(g_in, g_3):
    def body(hi_ref, h3_ref, gi_ref, g3_ref, send, recv):
        pos = _position()
        c = pos[2]
        sib = _peer(pos, 1)
        cps = [_remote(hi_ref.at[c], gi_ref.at[c], send, recv, 0, sib),
               _remote(h3_ref.at[:, c], g3_ref.at[:, c], send, recv, 1, sib)]
        for cp in cps:
            cp.start()
        _remote(hi_ref.at[1 - c], gi_ref.at[1 - c], send, recv, 0, sib).wait_recv()
        _remote(h3_ref.at[:, 1 - c], g3_ref.at[:, 1 - c], send, recv, 1, sib).wait_recv()
        for cp in cps:
            cp.wait_send()

    return pl.pallas_call(
        body, name="rs_final", in_specs=[ANY, ANY], out_specs=[ANY, ANY],
        out_shape=[jax.ShapeDtypeStruct(g_in.shape, F32), jax.ShapeDtypeStruct(g_3.shape, F32)],
        input_output_aliases={0: 0, 1: 1},
        scratch_shapes=[pltpu.SemaphoreType.DMA((2,)), pltpu.SemaphoreType.DMA((2,))],
    )(g_in, g_3)


def _adam_math(w, g, m, v):
    m = ADAM_B1 * m + (1.0 - ADAM_B1) * g
    v = ADAM_B2 * v + (1.0 - ADAM_B2) * (g * g)
    m_hat = m / (1.0 - ADAM_B1 ** ADAM_STEP)
    v_hat = v / (1.0 - ADAM_B2 ** ADAM_STEP)
    delta = -ADAM_LR * (m_hat / (jnp.sqrt(v_hat) + ADAM_EPS) + ADAM_WD * w)
    return delta, m, v


def _adamw(w, g, m, v, name):
    R, C = w.shape
    tr = min(128, R)

    def body(w_ref, g_ref, m_ref, v_ref, d_ref, nm_ref, nv_ref):
        d_ref[...], nm_ref[...], nv_ref[...] = _adam_math(w_ref[...], g_ref[...], m_ref[...], v_ref[...])

    blk = pl.BlockSpec((tr, C), lambda i: (i, 0))
    return pl.pallas_call(
        body, name=name, grid=(R // tr,), in_specs=[blk] * 4, out_specs=[blk] * 3,
        out_shape=[jax.ShapeDtypeStruct((R, C), F32)] * 3,
        compiler_params=_cparams(("parallel",), VMEM_LIMIT),
    )(w, g, m, v)


SMALL_ROWS = ("c_ctx", "norm_w", "conv_b", "gn_w", "final_norm_w")


def _bwd_small(stats, act, ada_w, p_row, p_ab, p_cw, p_dl):
    D = act.shape[1]
    Wm = ada_w.shape[1]
    Dq = p_cw.shape[2]
    n_out = 2 + 4 * 8

    def body(stx, stm, stc, stv, stg, stl, stlc, act_ref, aw_ref, prow, pab, pcw, pdl, *rest):
        gaw_ref, loss_ref = rest[0:2]
        o_q = [rest[2 + 8 * q:2 + 8 * (q + 1)] for q in range(4)]
        vec_ref, vbuf, dm, dm_sh, gcw, amine, abuf, s_v, r_v, s_a, r_a = rest[n_out:]
        pos = _position()
        me, s = _dev_id(pos), _shard_of(pos)
        vec_ref[...] = jnp.zeros_like(vec_ref)
        vec_ref[0:2, :] = stx[0:2, :]
        vec_ref[2:3, :] = stm[1:2, :]
        vec_ref[3:5, :] = stc[0:2, :]
        vec_ref[5:6, :] = stx[2:3, :] + stc[2:3, :]
        vec_ref[6:7, :] = stv[3:4, :]
        vec_ref[7:8, :] = stg[0:1, :]
        vec_ref[8:9, :] = stm[0:1, :]
        vec_ref[9:12, :] = stv[0:3, :]
        vec_ref[12:14, 0:128] = stl[0:2, :] + stlc[0:2, :]
        vec_ref[14:15, :] = stm[2:3, :]
        vbuf[me] = vec_ref[...]
        sends = [_remote(vec_ref, vbuf.at[me], s_v, r_v, k - 1, _peer(pos, k)) for k in range(1, 8)]
        for cp in sends:
            cp.start()
        for k in range(1, 8):
            _remote(vec_ref, vbuf.at[_dev_id(_peer(pos, k))], s_v, r_v, k - 1, _peer(pos, k)).wait_recv()
        tot = vbuf[0]
        for d in range(1, N_DEV):
            tot = tot + vbuf[d]
        loss_ref[...] = jnp.zeros((8, 128), F32) + (0.5 / D) * _sum_all(tot[14:15, :])
        dm[...] = jnp.zeros_like(dm)
        for d in range(N_DEV):
            for r in range(3):
                dm[d:d + 1, r * D:(r + 1) * D] = vbuf[d, r:r + 1, :]
        dm[8:9, 0:D] = tot[3:4, :]
        dm[8:9, D:2 * D] = tot[4:5, :]
        for t in range(N_SHARD):
            @pl.when(s == t)
            def _(t=t):
                dm_sh[...] = dm[:, t * Wm:(t + 1) * Wm]
                gcw[...] = tot[9:12, t * Dq:(t + 1) * Dq]
        gaw_ref[...] = lax.dot_general(act_ref[...], dm_sh[...], (((0,), (0,)), ((), ())),
                                       precision=lax.Precision.HIGHEST, preferred_element_type=F32)
        part = lax.dot_general(dm_sh[8:16, :], aw_ref[...], (((1,), (1,)), ((), ())),
                               precision=lax.Precision.HIGHEST, preferred_element_type=F32)
        amine[...] = part
        abuf[s] = part
        asend = [_remote(amine, abuf.at[s], s_a, r_a, j, _peer(pos, k)) for j, k in enumerate(CHIP_FLIPS)]
        for cp in asend:
            cp.start()
        for j, k in enumerate(CHIP_FLIPS):
            _remote(amine, abuf.at[_shard_of(_peer(pos, k))], s_a, r_a, j, _peer(pos, k)).wait_recv()
        da = abuf[0]
        for t in range(1, N_SHARD):
            da = da + abuf[t]
        cc = prow[0, 0:1, :]
        sg = _sigmoid(cc)
        g_cctx = da[0:1, :] * (sg * (1.0 + cc * (1.0 - sg)))

        def place_all(o, val):
            o[...] = val

        def emit(k, w, g, m, v, place=place_all):
            for q, val in enumerate((g,) + _adam_math(w, g, m, v)):
                place(o_q[q][k], val)

        g_rows = [g_cctx, tot[5:6, :], tot[6:7, :], tot[7:8, :], tot[8:9, :]]
        for k, g in enumerate(g_rows):
            emit(k, prow[0, k:k + 1, :], g, prow[1, k:k + 1, :], prow[2, k:k + 1, :])

        def place_ab(o, val):
            for r in range(3):
                o[0:1, r * D:(r + 1) * D] = val[r:r + 1, :]

        g_ab = jnp.concatenate([tot[0:1, :] + tot[3:4, :], tot[1:2, :] + tot[4:5, :], tot[2:3, :]], axis=0)
        emit(5, pab[0], g_ab, pab[1], pab[2], place_ab)
        emit(6, pcw[0], gcw[...], pcw[1], pcw[2])
        g_dl = jnp.concatenate([tot[12:14, 0:128] * _sigmoid(-pdl[0, 0:2, :]), jnp.zeros((6, 128), F32)], axis=0)
        emit(7, pdl[0], g_dl, pdl[1], pdl[2])
        for cp in sends + asend:
            cp.wait_send()

    row, f32 = (lambda *shape: jax.ShapeDtypeStruct(shape, F32)), F32
    per_q = [row(1, D)] * 5 + [row(1, 3 * D), row(3, Dq), row(8, 128)]
    res = pl.pallas_call(
        body, name="bwd_small",
        in_specs=[VMEM_FULL] * 13, out_specs=[VMEM_FULL] * n_out,
        out_shape=[row(D, Wm), row(8, 128)] + per_q * 4,
        scratch_shapes=[pltpu.VMEM((16, D), f32), pltpu.VMEM((N_DEV, 16, D), f32), pltpu.VMEM((16, 3 * D), f32),
                        pltpu.VMEM((16, Wm), f32), pltpu.VMEM((3, Dq), f32), pltpu.VMEM((8, D), f32),
                        pltpu.VMEM((N_SHARD, 8, D), f32),
                        pltpu.SemaphoreType.DMA((7,)), pltpu.SemaphoreType.DMA((7,)),
                        pltpu.SemaphoreType.DMA((3,)), pltpu.SemaphoreType.DMA((3,))],
        compiler_params=_cparams(None, VMEM_LIMIT),
    )(*stats, act, ada_w, p_row, p_ab, p_cw, p_dl)
    return res[0], res[1], [res[2 + 8 * q:2 + 8 * (q + 1)] for q in range(4)]


def _pad_rows(a, rows=8):
    return jnp.pad(a, ((0, rows - a.shape[0]), (0, 0)))


def kernel(x, c, ctx, c_ctx, norm_w, ada_w, ada_b, w_in, conv_w, conv_b, decay_logit, gn_w, w_a, w_b, w_out, final_norm_w, loss_target, m_c_ctx, m_norm_w, m_ada_w, m_ada_b, m_w_in, m_conv_w, m_conv_b, m_decay_logit, m_gn_w, m_w_a, m_w_b, m_w_out, m_final_norm_w, v_c_ctx, v_norm_w, v_ada_w, v_ada_b, v_w_in, v_conv_w, v_conv_b, v_decay_logit, v_gn_w, v_w_a, v_w_b, v_w_out, v_final_norm_w):
    L, D = x.shape[1], x.shape[2]
    H = D // DV
    Wc = w_in.shape[2]
    Do = D // 8
    pos = _position()
    me = _dev_id(pos)
    cidx = jnp.reshape(pos[2], (1,)).astype(jnp.int32)
    sidx = jnp.reshape(_shard_of(pos), (1,)).astype(jnp.int32)

    act, mod, conv_w8 = _fwd_small(_pad_rows(c), _pad_rows(c_ctx[None]), ada_w[0], ada_b, _pad_rows(conv_w[0]))
    mod_x = lax.dynamic_slice_in_dim(mod, me, 1, axis=0).reshape(3, D)
    mod_c = mod[8].reshape(3, D)
    lg = jax.nn.log_sigmoid(decay_logit[0])

    w3_s = jnp.stack([w_a[0], w_b[0], w_out[0]]).reshape(3, 2, Do, D)

    def project(xm):
        p, pqk, w_in_full, w3_full = _ag_in_proj(xm, w_in[0], w3_s)
        return p, pqk, w_in_full, w3_full.reshape(3, D, D)

    csidx = jnp.concatenate([cidx, sidx])
    groups, dret_c, xmt, cmt, dx1, sc_x, w_in_full, gh_3, sts = _local_step(
        x[0], ctx[0], loss_target[0], mod_x, mod_c, norm_w, conv_w8, conv_b, lg, gn_w, final_norm_w[None],
        project, csidx)
    st_mid, st_conv, st_gn, st_lg, st_lgc, st_c = sts

    (dw_other,) = _dw_in(1 - cidx, xmt, groups, cmt, dret_c, D, "dw_in_other")
    dw_mine, ra_in = _dw_in(cidx, xmt, groups, cmt, dret_c, D, "dw_in_mine", _pair_exchange_in(dw_other))
    cs_in = _sum_pair_in(dw_mine, ra_in)
    grad_x, st_x, rb_in = _dxm(groups, 0, w_in_full, x[0], norm_w, sc_x, dx1, "dxm_x", _chips_exchange_in(cs_in))
    gh_in = _sum_chips_in(csidx, cs_in, rb_in)
    g_in, g_3 = _rs_final(gh_in, gh_3)
    g_w_in = g_in.reshape(D, Wc)
    g_3 = g_3.reshape(3, D // 4, D)

    zeros3 = jnp.zeros((3, D), F32)
    p_row = jnp.concatenate(
        [r for t in ((c_ctx[None], norm_w, conv_b, gn_w, final_norm_w[None], zeros3),
                     (m_c_ctx[None], m_norm_w, m_conv_b, m_gn_w, m_final_norm_w[None], zeros3),
                     (v_c_ctx[None], v_norm_w, v_conv_b, v_gn_w, v_final_norm_w[None], zeros3)) for r in t],
        axis=0).reshape(3, 8, D)
    p_ab = jnp.concatenate([ada_b, m_ada_b, v_ada_b], axis=0).reshape(3, 3, D)
    p_cw = jnp.concatenate([conv_w, m_conv_w, v_conv_w], axis=0)
    p_dl = jnp.pad(jnp.concatenate([decay_logit, m_decay_logit, v_decay_logit], axis=0), ((0, 0), (0, 6), (0, 128 - H)))
    g_ada_w, loss_t, small = _bwd_small((st_x, st_mid, st_c, st_conv, st_gn, st_lg, st_lgc), act, ada_w[0],
                                        p_row, p_ab, p_cw, p_dl)

    upd_in = _adamw(w_in[0], g_w_in, m_w_in[0], v_w_in[0], "adamw_w_in")
    upd_ada = _adamw(ada_w[0], g_ada_w, m_ada_w[0], v_ada_w[0], "adamw_ada_w")
    upd_a = _adamw(w_a[0], g_3[0], m_w_a[0], v_w_a[0], "adamw_w_a")
    upd_b = _adamw(w_b[0], g_3[1], m_w_b[0], v_w_b[0], "adamw_w_b")
    upd_o = _adamw(w_out[0], g_3[2], m_w_out[0], v_w_out[0], "adamw_w_out")

    def leaves(q):
        big = lambda g, upd: (g if q == 0 else upd[q - 1])[None]
        r_cctx, r_norm, r_convb, r_gn, r_fnorm, r_ab, r_cw, r_dl = small[q]
        return [r_cctx.reshape(D), r_norm, big(g_ada_w, upd_ada), r_ab, big(g_w_in, upd_in),
                r_cw[None], r_convb, r_dl[0:2, 0:H][None], r_gn,
                big(g_3[0], upd_a), big(g_3[1], upd_b), big(g_3[2], upd_o), r_fnorm.reshape(D)]

    loss = loss_t[0, 0]
    return (loss, grad_x[None], *leaves(0), *leaves(1), *leaves(2), *leaves(3))
```

```python
from typing import Callable, NamedTuple

import jax
import jax.numpy as jnp
from jax import lax
from jax.experimental import pallas as pl
from jax.experimental.pallas import tpu as pltpu

F32 = jnp.float32
BF16 = jnp.bfloat16
MESH = pl.DeviceIdType.MESH

CHUNK = 128
RET_CPB = 4
DV = 128
DK = 64
GRID_W = 64
ROPE_BASE = 10000.0
EPS = 1e-6
K_SCALE = DK ** -0.5
N_SHARD = 4
N_DEV = 8

ADAM_LR = 0.001
ADAM_B1 = 0.9
ADAM_B2 = 0.999
ADAM_EPS = 1e-08
ADAM_WD = 0.01
ADAM_STEP = 10

VMEM_LIMIT = 56 * 1024 * 1024


def _cparams(sem=None, vmem=None):
    kw = {}
    if sem is not None:
        kw["dimension_semantics"] = sem
    if vmem is not None:
        kw["vmem_limit_bytes"] = vmem
    return pltpu.CompilerParams(**kw)


def _dot(a, b):
    return jnp.dot(a, b, preferred_element_type=F32)


def _dot_nt(a, b):
    return lax.dot_general(a, b, (((1,), (1,)), ((), ())), preferred_element_type=F32)


def _dot_tn(a, b):
    return lax.dot_general(a, b, (((0,), (0,)), ((), ())), preferred_element_type=F32)


def _sigmoid(x):
    return 1.0 / (1.0 + jnp.exp(-x))


def _sum_all(x):
    return jnp.sum(jnp.sum(x, axis=1, keepdims=True), axis=0, keepdims=True)


def _swap_halves(t):
    n = t.shape[1]
    lane = lax.broadcasted_iota(jnp.int32, t.shape, 1)
    low = (lane & 32) == 0
    return jnp.where(low, pltpu.roll(t, n - 32, 1), pltpu.roll(t, 32, 1))


def _vec_spec(d):
    return pl.BlockSpec((1, d), lambda *a: (0, 0))


def _norm_mod(x, nw, sc, sh, name):
    L, D = x.shape
    tl = min(256, L)

    def body(x_ref, nw_ref, sc_ref, sh_ref, xm_ref, xmt_ref):
        xv = x_ref[...]
        r = lax.rsqrt(jnp.mean(xv * xv, axis=-1, keepdims=True) + EPS)
        xm = (xv * r * nw_ref[...]) * (1.0 + sc_ref[...]) + sh_ref[...]
        xm_ref[...] = xm.astype(BF16)
        xmt_ref[...] = xm.T.astype(BF16)

    return pl.pallas_call(
        body, name=name, grid=(L // tl,),
        in_specs=[pl.BlockSpec((tl, D), lambda i: (i, 0)), _vec_spec(D), _vec_spec(D), _vec_spec(D)],
        out_specs=[pl.BlockSpec((tl, D), lambda i: (i, 0)), pl.BlockSpec((D, tl), lambda i: (0, i))],
        out_shape=[jax.ShapeDtypeStruct((L, D), BF16), jax.ShapeDtypeStruct((D, L), BF16)],
        compiler_params=_cparams(("parallel",)),
    )(x, nw, sc, sh)


QK_BLOCK, V_BLOCK = 4, 5


def _in_proj(xm, w, name, first=0, count=None):
    M, D = xm.shape
    count = w.shape[1] // D if count is None else count
    tm = min(1024, M)

    def body(a_ref, b_ref, o_ref, qk_ref):
        acc = _dot(a_ref[...], b_ref[...])
        o_ref[...] = acc.astype(o_ref.dtype)

        @pl.when(pl.program_id(1) == QK_BLOCK - first)
        def _():
            qk_ref[...] = acc

    return pl.pallas_call(
        body, name=name, grid=(M // tm, count),
        in_specs=[pl.BlockSpec((tm, D), lambda i, j: (i, 0)), pl.BlockSpec((D, D), lambda i, j: (0, first + j))],
        out_specs=[pl.BlockSpec((tm, D), lambda i, j: (i, j)), pl.BlockSpec((tm, D), lambda i, j: (i, 0))],
        out_shape=[jax.ShapeDtypeStruct((M, count * D), BF16), jax.ShapeDtypeStruct((M, D), F32)],
        compiler_params=_cparams(("parallel", "arbitrary")),
    )(xm, w)


def _halo_specs(tl, L, D, col):
    hb = tl // 16
    last = L // 16 - 1
    prev = pl.BlockSpec((16, D), lambda i: (jnp.maximum(i * hb - 1, 0), col))
    nxt = pl.BlockSpec((16, D), lambda i: (jnp.minimum((i + 1) * hb, last), col))
    return prev, nxt


def _shift_rows(u, above, below):
    tl = u.shape[0]
    row = lax.broadcasted_iota(jnp.int32, u.shape, 0)
    dn = jnp.where(row == 0, above, pltpu.roll(u, 1, 0))
    up = jnp.where(row == tl - 1, below, pltpu.roll(u, tl - 1, 0))
    return dn, up


def _conv_gate_fwd(p, conv_w, conv_b, D):
    L = p.shape[0]
    tl = min(256, L)
    nt = L // tl

    def body(h_ref, bg_ref, cg_ref, za_ref, hp_ref, hn_ref, cp_ref, cn_ref, w_ref, b_ref, o_ref):
        i = pl.program_id(0)
        u = cg_ref[...].astype(F32) * h_ref[...].astype(F32)
        above = cp_ref[15:16, :].astype(F32) * hp_ref[15:16, :].astype(F32)
        below = cn_ref[0:1, :].astype(F32) * hn_ref[0:1, :].astype(F32)
        above = jnp.where(i == 0, 0.0, above)
        below = jnp.where(i == nt - 1, 0.0, below)
        dn, up = _shift_rows(u, above, below)
        co = w_ref[0:1, :] * dn + w_ref[1:2, :] * u + w_ref[2:3, :] * up + b_ref[...]
        za = za_ref[...].astype(F32)
        o_ref[...] = (za * _sigmoid(za) * bg_ref[...].astype(F32) * co).astype(BF16)

    main = lambda col: pl.BlockSpec((tl, D), lambda i: (i, col))
    hp, hn = _halo_specs(tl, L, D, 0)
    cp, cn = _halo_specs(tl, L, D, 2)
    return pl.pallas_call(
        body, name="conv_gate_fwd", grid=(nt,),
        in_specs=[main(0), main(1), main(2), main(3), hp, hn, cp, cn,
                  pl.BlockSpec((8, D), lambda i: (0, 0)), _vec_spec(D)],
        out_specs=pl.BlockSpec((tl, D), lambda i: (i, 0)),
        out_shape=jax.ShapeDtypeStruct((L, D), BF16),
        compiler_params=_cparams(("parallel",)),
    )(p, p, p, p, p, p, p, p, conv_w, conv_b)


def _rope_tables(L):
    pos = jnp.arange(L)
    row = (pos // GRID_W).astype(F32)
    col = (pos % GRID_W).astype(F32)
    nf = DK // 4
    inv = ROPE_BASE ** (-jnp.arange(nf, dtype=F32) / nf)
    ang = jnp.concatenate([row[:, None] * inv, col[:, None] * inv], axis=-1)
    cos, sin = jnp.cos(ang), jnp.sin(ang)
    return jnp.concatenate([cos, cos, cos, cos], axis=-1), jnp.concatenate([-sin, sin, -sin, sin], axis=-1)


def _rope_fwd(pqk, c2, s2, D):
    L = pqk.shape[0]
    W = D // 2
    tl = min(256, L)

    def body(q_ref, k_ref, c_ref, s_ref, qo_ref, ko_ref):
        c, s = c_ref[...], s_ref[...]
        for pr in range(W // 128):
            ps = slice(pr * 128, (pr + 1) * 128)
            q = q_ref[:, ps]
            k = k_ref[:, ps] * K_SCALE
            qo_ref[:, ps] = (q * c + _swap_halves(q) * s).astype(BF16)
            ko_ref[:, ps] = (k * c + _swap_halves(k) * s).astype(BF16)

    blk = lambda col: pl.BlockSpec((tl, W), lambda i: (i, col))
    tab = pl.BlockSpec((tl, 128), lambda i: (i, 0))
    return pl.pallas_call(
        body, name="rope_fwd", grid=(L // tl,),
        in_specs=[blk(0), blk(1), tab, tab],
        out_specs=[blk(0), blk(0)],
        out_shape=[jax.ShapeDtypeStruct((L, W), BF16)] * 2,
        compiler_params=_cparams(("parallel",)),
    )(pqk, pqk, c2, s2)


def _smem_spec():
    return pl.BlockSpec(memory_space=pltpu.SMEM)


def _pair_select(e0, e1):
    row = lax.broadcasted_iota(jnp.int32, e0.shape, 0)
    return jnp.where(row < DK, e0, e1)


def _head_lane_mask(shape, e):
    lane = lax.broadcasted_iota(jnp.int32, shape, 1)
    return (lane < DK) if e == 0 else (lane >= DK)


def _ctx_states(pc, pqk_c, lg, D):
    Lc = pc.shape[0]
    H = D // DV

    def body(lg_ref, k_ref, v_ref, s_ref):
        m = lax.broadcasted_iota(jnp.int32, (Lc, DV), 0).astype(F32)
        for pr in range(H // 2):
            k2 = k_ref[:, pr * 128:(pr + 1) * 128].astype(F32) * K_SCALE
            res = [[None, None], [None, None]]
            for e in range(2):
                h = 2 * pr + e
                v = v_ref[:, h * DV:(h + 1) * DV]
                dec_f = jnp.exp(lg_ref[0, h] * (Lc - 1.0 - m))
                dec_b = jnp.exp(lg_ref[1, h] * m)
                res[0][e] = _dot_tn((k2 * dec_f).astype(BF16), v)
                res[1][e] = _dot_tn((k2 * dec_b).astype(BF16), v)
            s_ref[0, pr] = _pair_select(res[0][0], res[0][1])
            s_ref[1, pr] = _pair_select(res[1][0], res[1][1])

    return pl.pallas_call(
        body, name="ctx_states", grid=(1,),
        in_specs=[_smem_spec(), pl.BlockSpec((Lc, D // 2), lambda i: (0, 1)), pl.BlockSpec((Lc, D), lambda i: (0, 1))],
        out_specs=pl.BlockSpec((2, H // 2, 128, 128), lambda i: (0, 0, 0, 0)),
        out_shape=jax.ShapeDtypeStruct((2, H // 2, 128, 128), F32),
    )(lg, pqk_c, pc)


T_M, T_MT = 0, 1
T_MF1, T_MB1 = 2, 3
T_QF, T_QB = 4, 5
T_KF, T_KB = 6, 7


def _decay_tables(lg, H):
    def body(lg_ref, t_ref):
        h = pl.program_id(0)
        lgf, lgb = lg_ref[0, h], lg_ref[1, h]
        i = lax.broadcasted_iota(jnp.int32, (CHUNK, CHUNK), 0).astype(F32)
        j = lax.broadcasted_iota(jnp.int32, (CHUNK, CHUNK), 1).astype(F32)
        d = i - j
        mf = jnp.where(d > 0, jnp.exp(lgf * jnp.maximum(d, 0.0)), 0.0)
        mb = jnp.where(d < 0, jnp.exp(lgb * jnp.maximum(-d, 0.0)), 0.0)
        mf_t = jnp.where(d < 0, jnp.exp(lgf * jnp.maximum(-d, 0.0)), 0.0)
        mb_t = jnp.where(d > 0, jnp.exp(lgb * jnp.maximum(d, 0.0)), 0.0)
        diag = jnp.where(d == 0, 2.0, 0.0)
        t_ref[0, T_M] = mf + mb + diag
        t_ref[0, T_MT] = mf_t + mb_t + diag
        t_ref[0, T_MF1] = mf * d
        t_ref[0, T_MB1] = mb * (-d)
        t_ref[0, T_QF] = jnp.exp(lgf * (i + 1.0))
        t_ref[0, T_QB] = jnp.exp(lgb * (CHUNK - i))
        t_ref[0, T_KF] = jnp.exp(lgf * (CHUNK - 1.0 - i))
        t_ref[0, T_KB] = jnp.exp(lgb * i)

    return pl.pallas_call(
        body, name="decay_tables", grid=(H,), in_specs=[_smem_spec()],
        out_specs=pl.BlockSpec((1, 8, CHUNK, CHUNK), lambda h: (h, 0, 0, 0)),
        out_shape=jax.ShapeDtypeStruct((H, 8, CHUNK, CHUNK), F32),
    )(lg)


def _tab_spec(H):
    return pl.BlockSpec((H, 8, CHUNK, CHUNK), lambda n: (0, 0, 0, 0))


def _chunk_decay(tab_ref, h):
    return tab_ref[h, T_QF, CHUNK - 1:CHUNK, :], tab_ref[h, T_QB, 0:1, :]


def _ret_states(kr, p, s0, tab, D):
    L = kr.shape[0]
    H = D // DV
    N = L // CHUNK
    HP = H // 2

    def body(tab_ref, kf_ref, kb_ref, vf_ref, vb_ref, s0_ref, sf_out, sb_out, sf, sb):
        n = pl.program_id(0)

        @pl.when(n == 0)
        def _():
            sf[...] = s0_ref[0]
            sb[...] = s0_ref[1]

        for cc in range(RET_CPB):
            cf_, cb_ = cc, RET_CPB - 1 - cc
            rf, rb = slice(cf_ * CHUNK, (cf_ + 1) * CHUNK), slice(cb_ * CHUNK, (cb_ + 1) * CHUNK)
            sf_out[cf_] = sf[...]
            sb_out[cb_] = sb[...]
            for pr in range(HP):
                kf2 = kf_ref[rf, pr * 128:(pr + 1) * 128].astype(F32)
                kb2 = kb_ref[rb, pr * 128:(pr + 1) * 128].astype(F32)
                inc_f, inc_b, gf, gb = [], [], [], []
                for e in range(2):
                    h = 2 * pr + e
                    inc_f.append(_dot_tn((kf2 * tab_ref[h, T_KF]).astype(BF16), vf_ref[rf, h * DV:(h + 1) * DV]))
                    inc_b.append(_dot_tn((kb2 * tab_ref[h, T_KB]).astype(BF16), vb_ref[rb, h * DV:(h + 1) * DV]))
                    cf, cb = _chunk_decay(tab_ref, h)
                    gf.append(jnp.broadcast_to(cf, (128, 128)))
                    gb.append(jnp.broadcast_to(cb, (128, 128)))
                sf[pr] = _pair_select(gf[0], gf[1]) * sf[pr] + _pair_select(inc_f[0], inc_f[1])
                sb[pr] = _pair_select(gb[0], gb[1]) * sb[pr] + _pair_select(inc_b[0], inc_b[1])

    st = jax.ShapeDtypeStruct((N, HP, 128, 128), F32)
    R = RET_CPB * CHUNK
    NB = N // RET_CPB
    return pl.pallas_call(
        body, name="ret_states", grid=(NB,),
        in_specs=[_tab_spec(H),
                  pl.BlockSpec((R, D // 2), lambda n: (n, 0)),
                  pl.BlockSpec((R, D // 2), lambda n: (NB - 1 - n, 0)),
                  pl.BlockSpec((R, D), lambda n: (n, 5)),
                  pl.BlockSpec((R, D), lambda n: (NB - 1 - n, 5)),
                  pl.BlockSpec((2, HP, 128, 128), lambda n: (0, 0, 0, 0))],
        out_specs=[pl.BlockSpec((RET_CPB, HP, 128, 128), lambda n: (n, 0, 0, 0)),
                   pl.BlockSpec((RET_CPB, HP, 128, 128), lambda n: (NB - 1 - n, 0, 0, 0))],
        out_shape=[st, st],
        scratch_shapes=[pltpu.VMEM((HP, 128, 128), F32), pltpu.VMEM((HP, 128, 128), F32)],
        compiler_params=_cparams(("arbitrary",)),
    )(tab, kr, kr, p, p, s0)


def _ret_out(qr, kr, p, sf_prev, sb_prev, gn_w, tab, D):
    L = qr.shape[0]
    H = D // DV
    N = L // CHUNK
    HP = H // 2

    def body(tab_ref, q_ref, k_ref, v_ref, zb_ref, sf_ref, sb_ref, gn_ref, o_ref, yb_ref):
        def chunk(cc, carry):
            rows = pl.ds(pl.multiple_of(cc * CHUNK, CHUNK), CHUNK)
            for pr in range(HP):
                q2 = q_ref[rows, pr * 128:(pr + 1) * 128]
                k2 = k_ref[rows, pr * 128:(pr + 1) * 128]
                sfp = sf_ref[cc, pr].astype(BF16)
                sbp = sb_ref[cc, pr].astype(BF16)
                for e in range(2):
                    h = 2 * pr + e
                    sl = slice(h * DV, (h + 1) * DV)
                    qm = jnp.where(_head_lane_mask(q2.shape, e), q2, jnp.zeros_like(q2))
                    a = (_dot_nt(qm, k2) * tab_ref[h, T_M]).astype(BF16)
                    qf = qm.astype(F32)
                    o = _dot(a, v_ref[rows, sl])
                    o += _dot((qf * tab_ref[h, T_QF]).astype(BF16), sfp)
                    o += _dot((qf * tab_ref[h, T_QB]).astype(BF16), sbp)
                    o_ref[rows, sl] = o
                    mu = jnp.mean(o, axis=-1, keepdims=True)
                    oc = o - mu
                    rstd = lax.rsqrt(jnp.mean(oc * oc, axis=-1, keepdims=True) + EPS)
                    zb = zb_ref[rows, sl].astype(F32)
                    yb_ref[rows, sl] = (zb * _sigmoid(zb) * (oc * rstd * gn_ref[:, sl])).astype(BF16)
            return carry

        lax.fori_loop(0, RET_CPB, chunk, 0)

    R = RET_CPB * CHUNK
    return pl.pallas_call(
        body, name="ret_out", grid=(N // RET_CPB,),
        in_specs=[_tab_spec(H),
                  pl.BlockSpec((R, D // 2), lambda n: (n, 0)),
                  pl.BlockSpec((R, D // 2), lambda n: (n, 0)),
                  pl.BlockSpec((R, D), lambda n: (n, 5)),
                  pl.BlockSpec((R, D), lambda n: (n, 6)),
                  pl.BlockSpec((RET_CPB, HP, 128, 128), lambda n: (n, 0, 0, 0)),
                  pl.BlockSpec((RET_CPB, HP, 128, 128), lambda n: (n, 0, 0, 0)),
                  _vec_spec(D)],
        out_specs=[pl.BlockSpec((R, D), lambda n: (n, 0)), pl.BlockSpec((R, D), lambda n: (n, 0))],
        out_shape=[jax.ShapeDtypeStruct((L, D), F32), jax.ShapeDtypeStruct((L, D), BF16)],
        compiler_params=_cparams(("parallel",)),
    )(tab, qr, kr, p, p, sf_prev, sb_prev, gn_w)


def _mid(ya, yb, p, x, tgt, w3, g, fw, D):
    L = x.shape[0]
    tm = min(256, L)
    nt = L // tm

    def body(ya_ref, yb_ref, ga_ref, gb_ref, x_ref, t_ref, w_hbm, g_ref, fw_ref,
             dx1_ref, dya_ref, dyb_ref, dgab_ref, dw_hbm, st_ref, w_vm, dw_acc, sem):
        i = pl.program_id(0)

        @pl.when(i == 0)
        def _():
            cp = pltpu.make_async_copy(w_hbm, w_vm, sem)
            cp.start()
            dw_acc[...] = jnp.zeros_like(dw_acc)
            st_ref[...] = jnp.zeros_like(st_ref)
            cp.wait()

        ya_b, yb_b = ya_ref[...], yb_ref[...]
        y_a = _dot(ya_b, w_vm[0])
        y_b = _dot(yb_b, w_vm[1])
        sga = _sigmoid(ga_ref[...].astype(F32))
        sgb = _sigmoid(gb_ref[...].astype(F32))
        mix_b = (sga * y_a + sgb * y_b).astype(BF16)
        y_x = _dot(mix_b, w_vm[2])
        gvec, fwv = g_ref[...], fw_ref[...]
        x1 = x_ref[...] + gvec * y_x
        r1 = lax.rsqrt(jnp.mean(x1 * x1, axis=-1, keepdims=True) + EPS)
        xh = x1 * r1
        diff = xh * fwv - t_ref[...]
        dout = diff * (1.0 / D)
        dxh = dout * fwv
        dx1 = r1 * (dxh - xh * jnp.mean(dxh * xh, axis=-1, keepdims=True))
        dx1_ref[...] = dx1
        st_ref[0:1, :] += jnp.sum(dout * xh, axis=0, keepdims=True)
        st_ref[1:2, :] += jnp.sum(dx1 * y_x, axis=0, keepdims=True)
        st_ref[2:3, :] += jnp.sum(diff * diff, axis=0, keepdims=True)
        dyx_b = (dx1 * gvec).astype(BF16)
        dmix = _dot_nt(dyx_b, w_vm[2])
        dw_acc[2] += _dot_tn(mix_b, dyx_b)
        dya_b = (dmix * sga).astype(BF16)
        dyb_b = (dmix * sgb).astype(BF16)
        dgab_ref[:, 0:D] = (dmix * y_a * sga * (1.0 - sga)).astype(BF16)
        dgab_ref[:, D:2 * D] = (dmix * y_b * sgb * (1.0 - sgb)).astype(BF16)
        dya_ref[...] = _dot_nt(dya_b, w_vm[0])
        dyb_ref[...] = _dot_nt(dyb_b, w_vm[1])
        dw_acc[0] += _dot_tn(ya_b, dya_b)
        dw_acc[1] += _dot_tn(yb_b, dyb_b)

        @pl.when(i == nt - 1)
        def _():
            out = pltpu.make_async_copy(dw_acc, dw_hbm, sem)
            out.start()
            out.wait()

    row = lambda col: pl.BlockSpec((tm, D), lambda i: (i, col))
    any_spec = pl.BlockSpec(memory_space=pl.ANY)
    f32o = jax.ShapeDtypeStruct((L, D), F32)
    return pl.pallas_call(
        body, name="mid", grid=(nt,),
        in_specs=[row(0), row(0), row(7), row(8), row(0), row(0), any_spec, _vec_spec(D), _vec_spec(D)],
        out_specs=[row(0), row(0), row(0), pl.BlockSpec((tm, 2 * D), lambda i: (i, 0)), any_spec,
                   pl.BlockSpec((8, D), lambda i: (0, 0))],
        out_shape=[f32o, f32o, f32o, jax.ShapeDtypeStruct((L, 2 * D), BF16),
                   jax.ShapeDtypeStruct((3, D, D), F32), jax.ShapeDtypeStruct((8, D), F32)],
        scratch_shapes=[pltpu.VMEM((3, D, D), BF16), pltpu.VMEM((3, D, D), F32), pltpu.SemaphoreType.DMA],
        compiler_params=_cparams(("arbitrary",), VMEM_LIMIT),
    )(ya, yb, p, p, x, tgt, w3, g, fw)


def _conv_bwd(dya, p, conv_w, conv_b, D, exchange=None):
    L = p.shape[0]
    tl = min(256, L)
    nt = L // tl

    def body(d_ref, h_ref, bg_ref, cg_ref, za_ref,
             dp_ref, dn_ref, hp_ref, hn_ref, bp_ref, bn_ref, cp_ref, cn_ref, zp_ref, zn_ref,
             w_ref, b_ref, dc_ref, st_ref):
        i = pl.program_id(0)

        @pl.when(i == 0)
        def _():
            st_ref[...] = jnp.zeros_like(st_ref)

        first, last = i == 0, i == nt - 1
        h = h_ref[...].astype(F32)
        cg = cg_ref[...].astype(F32)
        bg = bg_ref[...].astype(F32)
        za = za_ref[...].astype(F32)
        dy = d_ref[...].astype(F32)
        u = cg * h
        u_above = jnp.where(first, 0.0, cp_ref[15:16, :].astype(F32) * hp_ref[15:16, :].astype(F32))
        u_below = jnp.where(last, 0.0, cn_ref[0:1, :].astype(F32) * hn_ref[0:1, :].astype(F32))
        u_dn, u_up = _shift_rows(u, u_above, u_below)
        w0, w1, w2 = w_ref[0:1, :], w_ref[1:2, :], w_ref[2:3, :]
        co = w0 * u_dn + w1 * u + w2 * u_up + b_ref[...]
        sz = _sigmoid(za)
        silu = za * sz
        dc_ref[:, 3 * D:4 * D] = (dy * bg * co * (sz * (1.0 + za * (1.0 - sz)))).astype(BF16)
        dc_ref[:, D:2 * D] = (dy * silu * co).astype(BF16)
        dco = dy * silu * bg

        def edge(dr, zr, br, r):
            z = zr[r:r + 1, :].astype(F32)
            return dr[r:r + 1, :].astype(F32) * (z * _sigmoid(z)) * br[r:r + 1, :].astype(F32)

        dco_above = jnp.where(first, 0.0, edge(dp_ref, zp_ref, bp_ref, 15))
        dco_below = jnp.where(last, 0.0, edge(dn_ref, zn_ref, bn_ref, 0))
        dco_dn, dco_up = _shift_rows(dco, dco_above, dco_below)
        du = w0 * dco_up + w1 * dco + w2 * dco_dn
        dc_ref[:, 2 * D:3 * D] = (du * h).astype(BF16)
        dc_ref[:, 0:D] = (du * cg).astype(BF16)
        st_ref[0:1, :] += jnp.sum(dco * u_dn, axis=0, keepdims=True)
        st_ref[1:2, :] += jnp.sum(dco * u, axis=0, keepdims=True)
        st_ref[2:3, :] += jnp.sum(dco * u_up, axis=0, keepdims=True)
        st_ref[3:4, :] += jnp.sum(dco, axis=0, keepdims=True)

    main = lambda col: pl.BlockSpec((tl, D), lambda i: (i, col))
    halos = []
    for col in (0, 0, 1, 2, 3):
        halos.extend(_halo_specs(tl, L, D, col))
    return _riding_call(
        body, exchange, nt, name="conv_bwd",
        args=(dya, p, p, p, p, dya, dya, p, p, p, p, p, p, p, p, conv_w, conv_b),
        in_specs=[main(0), main(0), main(1), main(2), main(3)] + halos
                 + [pl.BlockSpec((8, D), lambda i: (0, 0)), _vec_spec(D)],
        out_specs=[pl.BlockSpec((tl, 4 * D), lambda i: (i, 0)), pl.BlockSpec((8, D), lambda i: (0, 0))],
        out_shape=[jax.ShapeDtypeStruct((L, 4 * D), BF16), jax.ShapeDtypeStruct((8, D), F32)],
        cparams=_cparams(("arbitrary",)))


def _ret_bwd_pre(dyb, p, o, gn_w, D):
    L = o.shape[0]
    H = D // DV
    tl = min(256, L)

    def body(d_ref, zb_ref, o_ref, gn_ref, do_ref, dzb_ref, st_ref):
        @pl.when(pl.program_id(0) == 0)
        def _():
            st_ref[...] = jnp.zeros_like(st_ref)

        for h in range(H):
            sl = slice(h * DV, (h + 1) * DV)
            ov = o_ref[:, sl]
            mu = jnp.mean(ov, axis=-1, keepdims=True)
            oc = ov - mu
            rstd = lax.rsqrt(jnp.mean(oc * oc, axis=-1, keepdims=True) + EPS)
            rn = oc * rstd
            gw = gn_ref[:, sl]
            zb = zb_ref[:, sl].astype(F32)
            sz = _sigmoid(zb)
            dy = d_ref[:, sl].astype(F32)
            dzb_ref[:, sl] = (dy * (rn * gw) * (sz * (1.0 + zb * (1.0 - sz)))).astype(BF16)
            dretn = dy * (zb * sz)
            st_ref[0:1, sl] += jnp.sum(dretn * rn, axis=0, keepdims=True)
            drn = dretn * gw
            do = rstd * (drn - jnp.mean(drn, axis=-1, keepdims=True)
                         - rn * jnp.mean(drn * rn, axis=-1, keepdims=True))
            do_ref[:, sl] = do.astype(BF16)

    main = lambda col: pl.BlockSpec((tl, D), lambda i: (i, col))
    bfo = jax.ShapeDtypeStruct((L, D), BF16)
    return pl.pallas_call(
        body, name="ret_bwd_pre", grid=(L // tl,),
        in_specs=[main(0), main(6), main(0), _vec_spec(D)],
        out_specs=[main(0), main(0), pl.BlockSpec((8, D), lambda i: (0, 0))],
        out_shape=[bfo, bfo, jax.ShapeDtypeStruct((8, D), F32)],
        compiler_params=_cparams(("arbitrary",)),
    )(dyb, p, o, gn_w)


def _ret_bwd_states(qr, do, tab, D):
    L = qr.shape[0]
    H = D // DV
    N = L // CHUNK
    HP = H // 2

    def body(tab_ref, qf_ref, qb_ref, dof_ref, dob_ref, dsf_out, dsb_out, ds0_out, dsf, dsb):
        n = pl.program_id(0)

        @pl.when(n == 0)
        def _():
            dsf[...] = jnp.zeros_like(dsf)
            dsb[...] = jnp.zeros_like(dsb)

        for cc in range(RET_CPB):
            cf_, cb_ = RET_CPB - 1 - cc, cc
            rf, rb = slice(cf_ * CHUNK, (cf_ + 1) * CHUNK), slice(cb_ * CHUNK, (cb_ + 1) * CHUNK)
            dsf_out[cf_] = dsf[...]
            dsb_out[cb_] = dsb[...]
            for pr in range(HP):
                qf2 = qf_ref[rf, pr * 128:(pr + 1) * 128].astype(F32)
                qb2 = qb_ref[rb, pr * 128:(pr + 1) * 128].astype(F32)
                inc_f, inc_b, gf, gb = [], [], [], []
                for e in range(2):
                    h = 2 * pr + e
                    inc_f.append(_dot_tn((qf2 * tab_ref[h, T_QF]).astype(BF16), dof_ref[rf, h * DV:(h + 1) * DV]))
                    inc_b.append(_dot_tn((qb2 * tab_ref[h, T_QB]).astype(BF16), dob_ref[rb, h * DV:(h + 1) * DV]))
                    cf, cb = _chunk_decay(tab_ref, h)
                    gf.append(jnp.broadcast_to(cf, (128, 128)))
                    gb.append(jnp.broadcast_to(cb, (128, 128)))
                dsf[pr] = _pair_select(gf[0], gf[1]) * dsf[pr] + _pair_select(inc_f[0], inc_f[1])
                dsb[pr] = _pair_select(gb[0], gb[1]) * dsb[pr] + _pair_select(inc_b[0], inc_b[1])

        @pl.when(n == NB - 1)
        def _():
            ds0_out[0] = dsf[...]
            ds0_out[1] = dsb[...]

    st = jax.ShapeDtypeStruct((N, HP, 128, 128), F32)
    R = RET_CPB * CHUNK
    NB = N // RET_CPB
    return pl.pallas_call(
        body, name="ret_bwd_states", grid=(NB,),
        in_specs=[_tab_spec(H),
                  pl.BlockSpec((R, D // 2), lambda n: (NB - 1 - n, 0)),
                  pl.BlockSpec((R, D // 2), lambda n: (n, 0)),
                  pl.BlockSpec((R, D), lambda n: (NB - 1 - n, 0)),
                  pl.BlockSpec((R, D), lambda n: (n, 0))],
        out_specs=[pl.BlockSpec((RET_CPB, HP, 128, 128), lambda n: (NB - 1 - n, 0, 0, 0)),
                   pl.BlockSpec((RET_CPB, HP, 128, 128), lambda n: (n, 0, 0, 0)),
                   pl.BlockSpec((2, HP, 128, 128), lambda n: (0, 0, 0, 0))],
        out_shape=[st, st, jax.ShapeDtypeStruct((2, HP, 128, 128), F32)],
        scratch_shapes=[pltpu.VMEM((HP, 128, 128), F32), pltpu.VMEM((HP, 128, 128), F32)],
        compiler_params=_cparams(("arbitrary",)),
    )(tab, qr, qr, do, do)


def _ret_bwd_main(qr, kr, p, do, sf_prev, sb_prev, dsf, dsb, c2, s2, tab, D, exchange=None):
    L = qr.shape[0]
    H = D // DV
    N = L // CHUNK
    HP = H // 2
    W = D // 2

    def body(tab_ref, q_ref, k_ref, v_ref, do_ref, sf_ref, sb_ref, dsf_ref, dsb_ref, c_ref, s_ref,
             dr_ref, st_ref, dl_acc):
        @pl.when(pl.program_id(0) == 0)
        def _():
            dl_acc[...] = jnp.zeros_like(dl_acc)

        i = lax.broadcasted_iota(jnp.int32, (CHUNK, 128), 0).astype(F32)
        rowid = lax.broadcasted_iota(jnp.int32, (128, 128), 0)

        def chunk(cc, carry):
            rows = pl.ds(pl.multiple_of(cc * CHUNK, CHUNK), CHUNK)
            c, s = c_ref[rows, :], s_ref[rows, :]
            for pr in range(HP):
                ps = slice(pr * 128, (pr + 1) * 128)
                q2, k2 = q_ref[rows, ps], k_ref[rows, ps]
                sf32, sb32 = sf_ref[cc, pr], sb_ref[cc, pr]
                dsf32, dsb32 = dsf_ref[cc, pr], dsb_ref[cc, pr]
                sfp, sbp = sf32.astype(BF16), sb32.astype(BF16)
                dsfp, dsbp = dsf32.astype(BF16), dsb32.astype(BF16)
                dq2 = jnp.zeros((CHUNK, 128), F32)
                dk2 = jnp.zeros((CHUNK, 128), F32)
                for e in range(2):
                    h = 2 * pr + e
                    sl = slice(h * DV, (h + 1) * DV)
                    hm = _head_lane_mask(q2.shape, e)
                    qm = jnp.where(hm, q2, jnp.zeros_like(q2))
                    km = jnp.where(hm, k2, jnp.zeros_like(k2))
                    qf, kf = qm.astype(F32), km.astype(F32)
                    v, do = v_ref[rows, sl], do_ref[rows, sl]
                    vf, dof = v.astype(F32), do.astype(F32)
                    m_t = tab_ref[h, T_MT]
                    sc = _dot_nt(qm, k2)
                    dpm = _dot_nt(do, v)
                    dsc = (dpm * tab_ref[h, T_M]).astype(BF16)
                    a_t = (_dot_nt(km, q2) * m_t).astype(BF16)
                    dsc_t = (_dot_nt(v, do) * m_t).astype(BF16)
                    dq_f, dq_b = tab_ref[h, T_QF], tab_ref[h, T_QB]
                    dk_f, dk_b = tab_ref[h, T_KF], tab_ref[h, T_KB]
                    dq = _dot(dsc, km)
                    dq += jnp.where(hm, dq_f * _dot_nt(do, sfp) + dq_b * _dot_nt(do, sbp), 0.0)
                    dk = _dot(dsc_t, qm)
                    dk += jnp.where(hm, dk_f * _dot_nt(v, dsfp) + dk_b * _dot_nt(v, dsbp), 0.0)
                    kdf = _dot((kf * dk_f).astype(BF16), dsfp)
                    kdb = _dot((kf * dk_b).astype(BF16), dsbp)
                    dr_ref[rows, D + h * DV:D + (h + 1) * DV] = (_dot(a_t, do) + kdf + kdb).astype(BF16)
                    dq2 += dq
                    dk2 += dk
                    xf = _dot((qf * dq_f).astype(BF16), sfp)
                    xb = _dot((qf * dq_b).astype(BF16), sbp)
                    pair = (rowid < DK) if e == 0 else (rowid >= DK)
                    gcf, gcb = tab_ref[h, T_QF, CHUNK - 1:CHUNK, 0:1], tab_ref[h, T_QB, 0:1, 0:1]
                    scdp = sc * dpm
                    dl_acc[h, 0] += scdp * tab_ref[h, T_MF1] + xf * dof * (i + 1.0) \
                        + kdf * vf * (CHUNK - 1.0 - i) + (CHUNK * gcf) * jnp.where(pair, dsf32 * sf32, 0.0)
                    dl_acc[h, 1] += scdp * tab_ref[h, T_MB1] + xb * dof * (CHUNK - i) \
                        + kdb * vf * i + (CHUNK * gcb) * jnp.where(pair, dsb32 * sb32, 0.0)
                dr_ref[rows, ps] = (dq2 * c - _swap_halves(dq2) * s).astype(BF16)
                dr_ref[rows, W + pr * 128:W + (pr + 1) * 128] = \
                    ((dk2 * c - _swap_halves(dk2) * s) * K_SCALE).astype(BF16)
            return carry

        lax.fori_loop(0, RET_CPB, chunk, 0)

        @pl.when(pl.program_id(0) == N // RET_CPB - 1)
        def _():
            lane = lax.broadcasted_iota(jnp.int32, (1, 128), 1)
            acc = [jnp.zeros((1, 128), F32), jnp.zeros((1, 128), F32)]
            for h in range(H):
                for b in range(2):
                    acc[b] += jnp.where(lane == h, _sum_all(dl_acc[h, b]), 0.0)
            st_ref[...] = jnp.zeros_like(st_ref)
            st_ref[0:1, :] = acc[0]
            st_ref[1:2, :] = acc[1]

    R = RET_CPB * CHUNK
    st_spec = pl.BlockSpec((RET_CPB, HP, 128, 128), lambda n: (n, 0, 0, 0))
    half = pl.BlockSpec((R, W), lambda n: (n, 0))
    rope = pl.BlockSpec((R, 128), lambda n: (n, 0))
    return _riding_call(
        body, exchange, N // RET_CPB, name="ret_bwd_main",
        args=(tab, qr, kr, p, do, sf_prev, sb_prev, dsf, dsb, c2, s2),
        in_specs=[_tab_spec(H), half, half,
                  pl.BlockSpec((R, D), lambda n: (n, 5)),
                  pl.BlockSpec((R, D), lambda n: (n, 0)),
                  st_spec, st_spec, st_spec, st_spec, rope, rope],
        out_specs=[pl.BlockSpec((R, 2 * D), lambda n: (n, 0)),
                   pl.BlockSpec((8, 128), lambda n: (0, 0))],
        out_shape=[jax.ShapeDtypeStruct((L, 2 * D), BF16), jax.ShapeDtypeStruct((8, 128), F32)],
        scratch=[pltpu.VMEM((H, 2, CHUNK, 128), F32)],
        cparams=_cparams(("arbitrary",)))


def _ctx_bwd(pc, pqk_c, ds0, lg, D):
    Lc = pc.shape[0]
    H = D // DV
    HP = H // 2
    W = D // 2

    def body(lg_ref, k_ref, v_ref, ds_ref, dr_ref, st_ref):
        dqk_ref = dr_ref.at[:, 0:D]
        dv_ref = dr_ref.at[:, D:2 * D]
        m = lax.broadcasted_iota(jnp.int32, (Lc, 128), 0).astype(F32)
        lane = lax.broadcasted_iota(jnp.int32, (1, 128), 1)
        acc_f = jnp.zeros((1, 128), F32)
        acc_b = jnp.zeros((1, 128), F32)
        dqk_ref[:, 0:W] = jnp.zeros((Lc, W), BF16)
        for pr in range(HP):
            ps = slice(pr * 128, (pr + 1) * 128)
            k2 = k_ref[:, ps].astype(F32) * K_SCALE
            dsfp, dsbp = ds_ref[0, pr].astype(BF16), ds_ref[1, pr].astype(BF16)
            dk2 = jnp.zeros((Lc, 128), F32)
            for e in range(2):
                h = 2 * pr + e
                sl = slice(h * DV, (h + 1) * DV)
                hm = _head_lane_mask(k2.shape, e)
                km = jnp.where(hm, k2, 0.0)
                v = v_ref[:, sl]
                vf = v.astype(F32)
                dec_f = jnp.exp(lg_ref[0, h] * (Lc - 1.0 - m))
                dec_b = jnp.exp(lg_ref[1, h] * m)
                kdf = _dot((km * dec_f).astype(BF16), dsfp)
                kdb = _dot((km * dec_b).astype(BF16), dsbp)
                dv_ref[:, sl] = (kdf + kdb).astype(BF16)
                dk2 += jnp.where(hm, dec_f * _dot_nt(v, dsfp) + dec_b * _dot_nt(v, dsbp), 0.0)
                acc_f += jnp.where(lane == h, _sum_all(kdf * vf * (Lc - 1.0 - m)), 0.0)
                acc_b += jnp.where(lane == h, _sum_all(kdb * vf * m), 0.0)
            dqk_ref[:, W + pr * 128:W + (pr + 1) * 128] = (dk2 * K_SCALE).astype(BF16)
        st_ref[...] = jnp.zeros_like(st_ref)
        st_ref[0:1, :] = acc_f
        st_ref[1:2, :] = acc_b

    return pl.pallas_call(
        body, name="ctx_bwd", grid=(1,),
        in_specs=[_smem_spec(), pl.BlockSpec((Lc, W), lambda i: (0, 1)), pl.BlockSpec((Lc, D), lambda i: (0, 1)),
                  pl.BlockSpec((2, HP, 128, 128), lambda i: (0, 0, 0, 0))],
        out_specs=[pl.BlockSpec((Lc, 2 * D), lambda i: (0, 0)), pl.BlockSpec((8, 128), lambda i: (0, 0))],
        out_shape=[jax.ShapeDtypeStruct((Lc, 2 * D), BF16), jax.ShapeDtypeStruct((8, 128), F32)],
    )(lg, pqk_c, pc, ds0)


class _Exchange(NamedTuple):
    inputs: tuple
    out_shapes: tuple
    n_copies: int
    build: Callable


def _exchange_parts(exchange):
    if exchange is None:
        return [], [], [], [], []
    n = exchange.n_copies
    return (list(exchange.inputs), [ANY] * len(exchange.inputs), list(exchange.out_shapes),
            [ANY] * len(exchange.out_shapes), [pltpu.SemaphoreType.DMA((n,)), pltpu.SemaphoreType.DMA((n,))])


def _riding_call(body, exchange, n_steps, *, args, in_specs, out_specs, out_shape, name, cparams, scratch=()):
    ex_args, ex_in_specs, ex_shapes, ex_out_specs, ex_scratch = _exchange_parts(exchange)
    n_in, n_out, n_sc = len(args), len(out_shape), len(scratch)

    def riding(*refs):
        k = n_in + len(ex_args)
        ins, ex_in = refs[:n_in], refs[n_in:k]
        outs, ex_out = refs[k:k + n_out], refs[k + n_out:k + n_out + len(ex_shapes)]
        k += n_out + len(ex_shapes)
        own_scratch, ex_sems = refs[k:k + n_sc], refs[k + n_sc:]
        step = pl.program_id(0)
        if exchange is not None:
            @pl.when(step == 0)
            def _():
                for rc in exchange.build(ex_in, ex_out, *ex_sems):
                    rc.start()
        body(*ins, *outs, *own_scratch)
        if exchange is not None:
            @pl.when(step == n_steps - 1)
            def _():
                for rc in exchange.build(ex_in, ex_out, *ex_sems):
                    rc.wait()

    return tuple(pl.pallas_call(
        riding, name=name, grid=(n_steps,),
        in_specs=list(in_specs) + ex_in_specs, out_specs=list(out_specs) + ex_out_specs,
        out_shape=list(out_shape) + ex_shapes, scratch_shapes=list(scratch) + ex_scratch,
        compiler_params=cparams,
    )(*args, *ex_args))


def _dxm(groups, col0, w, x, nw, sc, dx1, name, exchange=None):
    L, D = x.shape
    tm = min(256, L)
    nt = L // tm
    ng = len(groups)
    widths = [g.shape[1] for g in groups]
    wtot = sum(widths)
    with_dx = dx1 is not None
    ex_args, ex_in_specs, ex_shapes, ex_out_specs, ex_scratch = _exchange_parts(exchange)
    n_in = ng + 4 + (1 if with_dx else 0)
    n_out = 2 if with_dx else 1

    def body(*refs):
        group_refs = refs[:ng]
        w_hbm, x_ref, nw_ref, sc_ref = refs[ng:ng + 4]
        ex_in = refs[n_in:n_in + len(ex_args)]
        outs = refs[n_in + len(ex_args):]
        if with_dx:
            dx1_ref, gx_ref, st_ref = refs[ng + 4], outs[0], outs[1]
        else:
            st_ref = outs[0]
        ex_out = outs[n_out:n_out + len(ex_shapes)]
        w_vm, sem = outs[n_out + len(ex_shapes):n_out + len(ex_shapes) + 2]
        ex_sems = outs[n_out + len(ex_shapes) + 2:]
        i = pl.program_id(0)

        @pl.when(i == 0)
        def _():
            cp = pltpu.make_async_copy(w_hbm.at[:, col0 * D:col0 * D + wtot], w_vm, sem)
            cp.start()
            if exchange is not None:
                for rc in exchange.build(ex_in, ex_out, *ex_sems):
                    rc.start()
            st_ref[...] = jnp.zeros_like(st_ref)
            cp.wait()

        dxm, off = None, 0
        for g_ref, wd in zip(group_refs, widths):
            part = _dot_nt(g_ref[...], w_vm[:, off:off + wd])
            dxm = part if dxm is None else dxm + part
            off += wd

        xv = x_ref[...]
        r = lax.rsqrt(jnp.mean(xv * xv, axis=-1, keepdims=True) + EPS)
        xh = xv * r
        nwv = nw_ref[...]
        dxn = dxm * (1.0 + sc_ref[...])
        st_ref[0:1, :] += jnp.sum(dxm, axis=0, keepdims=True)
        st_ref[1:2, :] += jnp.sum(dxm * (xh * nwv), axis=0, keepdims=True)
        st_ref[2:3, :] += jnp.sum(dxn * xh, axis=0, keepdims=True)
        if with_dx:
            dxh = dxn * nwv
            gx_ref[...] = dx1_ref[...] + r * (dxh - xh * jnp.mean(dxh * xh, axis=-1, keepdims=True))

        if exchange is not None:
            @pl.when(i == nt - 1)
            def _():
                for rc in exchange.build(ex_in, ex_out, *ex_sems):
                    rc.wait()

    row = pl.BlockSpec((tm, D), lambda i: (i, 0))
    in_specs = [pl.BlockSpec((tm, wd), lambda i: (i, 0)) for wd in widths] + [ANY, row, _vec_spec(D), _vec_spec(D)]
    out_specs = [pl.BlockSpec((8, D), lambda i: (0, 0))]
    out_shape = [jax.ShapeDtypeStruct((8, D), F32)]
    args = list(groups) + [w, x, nw, sc]
    if with_dx:
        in_specs.append(row)
        out_specs.insert(0, row)
        out_shape.insert(0, jax.ShapeDtypeStruct((L, D), F32))
        args.append(dx1)
    res = pl.pallas_call(
        body, name=name, grid=(nt,),
        in_specs=in_specs + ex_in_specs, out_specs=out_specs + ex_out_specs, out_shape=out_shape + ex_shapes,
        scratch_shapes=[pltpu.VMEM((D, wtot), BF16), pltpu.SemaphoreType.DMA] + ex_scratch,
        compiler_params=_cparams(("arbitrary",), VMEM_LIMIT),
    )(*args, *ex_args)
    gx = res[0] if with_dx else None
    return (gx, res[n_out - 1], *res[n_out:])


DW_TN = 512


def _dw_in(hidx, xmt, groups, cmt, dr_c, D, name, exchange=None):
    L = xmt.shape[1]
    Lc = cmt.shape[1]
    Dh = D // 2
    tn = min(DW_TN, D)
    nblk = [g.shape[1] // tn for g in groups]
    starts = [sum(nblk[:g]) for g in range(len(groups))]
    ng = len(groups)
    nj = sum(nblk)
    ex_args, ex_in_specs, ex_shapes, ex_out_specs, ex_scratch = _exchange_parts(exchange)

    def body(*refs):
        h_ref, xt_hbm = refs[0], refs[1]
        group_refs = refs[2:2 + ng]
        ct_ref, drc_ref = refs[2 + ng:4 + ng]
        ex_in = refs[4 + ng:4 + ng + len(ex_args)]
        outs = refs[4 + ng + len(ex_args):]
        o_ref = outs[0]
        ex_out = outs[1:1 + len(ex_shapes)]
        xt_vm, sem = outs[1 + len(ex_shapes):3 + len(ex_shapes)]
        ex_sems = outs[3 + len(ex_shapes):]
        j = pl.program_id(0)

        @pl.when(j == 0)
        def _():
            rows = pl.ds(pl.multiple_of(h_ref[0] * Dh, Dh), Dh)
            cp = pltpu.make_async_copy(xt_hbm.at[rows, :], xt_vm, sem)
            cp.start()
            if exchange is not None:
                for rc in exchange.build(ex_in, ex_out, *ex_sems):
                    rc.start()
            cp.wait()

        for g in range(ng):
            @pl.when((j >= starts[g]) & (j < starts[g] + nblk[g]))
            def _(g=g):
                acc = _dot(xt_vm[...], group_refs[g][...])
                if g == 1:
                    acc += _dot(ct_ref[...], drc_ref[...])
                o_ref[...] = acc

        if exchange is not None:
            @pl.when(j == nj - 1)
            def _():
                for rc in exchange.build(ex_in, ex_out, *ex_sems):
                    rc.wait()

    def group_spec(g, rows):
        return pl.BlockSpec((rows, tn), lambda j, h: (0, jnp.clip(j - starts[g], 0, nblk[g] - 1)))

    res = pl.pallas_call(
        body, name=name,
        grid_spec=pltpu.PrefetchScalarGridSpec(
            num_scalar_prefetch=1, grid=(nj,),
            in_specs=[ANY] + [group_spec(g, L) for g in range(ng)]
                     + [pl.BlockSpec((Dh, Lc), lambda j, h: (h[0], 0)), group_spec(1, Lc)] + ex_in_specs,
            out_specs=[pl.BlockSpec((Dh, tn), lambda j, h: (0, j))] + ex_out_specs,
            scratch_shapes=[pltpu.VMEM((Dh, L), BF16), pltpu.SemaphoreType.DMA] + ex_scratch),
        out_shape=[jax.ShapeDtypeStruct((Dh, nj * tn), F32)] + ex_shapes,
        compiler_params=_cparams(("arbitrary",), VMEM_LIMIT),
    )(hidx, xmt, *groups, cmt, dr_c, *ex_args)
    return tuple(res)


def _local_step(x, ctx, tgt, mod_x, mod_c, norm_w, conv_w8, conv_b, lg, gn_w, fw, project, csidx=None):
    L, D = x.shape
    sh_x, sc_x, g_x = mod_x[0:1], mod_x[1:2], mod_x[2:3]
    sh_c, sc_c = mod_c[0:1], mod_c[1:2]
    c2, s2 = _rope_tables(L)
    tab = _decay_tables(lg, D // DV)

    xm, xmt = _norm_mod(x, norm_w, sc_x, sh_x, "norm_mod_x")
    cm, cmt = _norm_mod(ctx, norm_w, sc_c, sh_c, "norm_mod_ctx")
    p, pqk, w_in, w3 = project(xm)
    pc, pqk_c = _in_proj(cm, w_in, "in_proj_ctx", QK_BLOCK, 2)
    ya = _conv_gate_fwd(p, conv_w8, conv_b, D)
    qr, kr = _rope_fwd(pqk, c2, s2, D)
    s0 = _ctx_states(pc, pqk_c, lg, D)
    sf_prev, sb_prev = _ret_states(kr, p, s0, tab, D)
    o, yb = _ret_out(qr, kr, p, sf_prev, sb_prev, gn_w, tab, D)
    dx1, dya, dyb, dgab, dw3, st_mid = _mid(ya, yb, p, x, tgt, w3, g_x, fw, D)
    reduce = csidx is not None
    dw3_5 = dw3.reshape(3, N_SHARD, 2, D // 8, D)
    dconv, st_conv, *ra_3 = _conv_bwd(dya, p, conv_w8, conv_b, D, _pair_exchange_w3(dw3_5) if reduce else None)
    do, dzb, st_gn = _ret_bwd_pre(dyb, p, o, gn_w, D)
    dsf, dsb, ds0 = _ret_bwd_states(qr, do, tab, D)
    cs_3 = _sum_pair_w3(csidx[0:1], dw3_5, ra_3[0]) if reduce else None
    dret, st_lg, *rb_3 = _ret_bwd_main(qr, kr, p, do, sf_prev, sb_prev, dsf, dsb, c2, s2, tab, D,
                                       _chips_exchange_w3(cs_3) if reduce else None)
    g_3 = _sum_chips_w3(csidx, cs_3, rb_3[0]) if reduce else dw3
    dret_c, st_lgc = _ctx_bwd(pc, pqk_c, ds0, lg, D)
    groups = (dconv, dret, dzb, dgab)
    _, st_c = _dxm((dret_c,), 4, w_in, ctx, norm_w, sc_c, None, "dxm_ctx")
    return groups, dret_c, xmt, cmt, dx1, sc_x, w_in, g_3, (st_mid, st_conv, st_gn, st_lg, st_lgc, st_c)


CHIP_FLIPS = (4, 2, 6)
ANY = pl.BlockSpec(memory_space=pl.ANY)
VMEM_FULL = pl.BlockSpec(memory_space=pltpu.VMEM)


def _position():
    return lax.axis_index("x"), lax.axis_index("y"), lax.axis_index("c")


def _peer(pos, k):
    x, y, c = pos
    return (1 - x if k & 4 else x, 1 - y if k & 2 else y, 1 - c if k & 1 else c)


def _dev_id(pos):
    return 4 * pos[0] + 2 * pos[1] + pos[2]


def _shard_of(pos):
    return 2 * pos[0] + pos[1]


def _remote(src, dst, send_sems, recv_sems, idx, to):
    return pltpu.make_async_remote_copy(src_ref=src, dst_ref=dst, send_sem=send_sems.at[idx],
                                        recv_sem=recv_sems.at[idx], device_id=to, device_id_type=MESH)


def _dot_f32(a, b):
    return jnp.dot(a, b, precision=lax.Precision.HIGHEST, preferred_element_type=F32)


def _silu(x):
    return x * _sigmoid(x)


def _fwd_small(c8, cctx8, ada_w, ada_b, conv_w8):
    D = c8.shape[1]
    Wm = ada_w.shape[1]
    Dq = conv_w8.shape[1]

    def body(c_ref, cc_ref, aw_ref, ab_ref, cw_ref, act_ref, mod_ref, cwf_ref,
             cbuf, pmine, pbuf, wbuf, s_c, r_c, s_p, r_p, s_w, r_w):
        pos = _position()
        me, s = _dev_id(pos), _shard_of(pos)
        cbuf[me] = c_ref[...]
        wbuf[s] = cw_ref[...]
        sends = [_remote(c_ref, cbuf.at[me], s_c, r_c, k - 1, _peer(pos, k)) for k in range(1, 8)]
        sends += [_remote(cw_ref, wbuf.at[s], s_w, r_w, j, _peer(pos, k)) for j, k in enumerate(CHIP_FLIPS)]
        for cp in sends:
            cp.start()
        for k in range(1, 8):
            _remote(c_ref, cbuf.at[_dev_id(_peer(pos, k))], s_c, r_c, k - 1, _peer(pos, k)).wait_recv()
        for d in range(N_DEV):
            act_ref[d:d + 1, :] = _silu(cbuf[d, 0:1, :])
        act_ref[8:9, :] = _silu(cc_ref[0:1, :])
        act_ref[9:16, :] = jnp.zeros((7, D), F32)
        part = _dot_f32(act_ref[...], aw_ref[...])
        pmine[...] = part
        pbuf[s] = part
        psend = [_remote(pmine, pbuf.at[s], s_p, r_p, j, _peer(pos, k)) for j, k in enumerate(CHIP_FLIPS)]
        for cp in psend:
            cp.start()
        for j, k in enumerate(CHIP_FLIPS):
            t = _shard_of(_peer(pos, k))
            _remote(pmine, pbuf.at[t], s_p, r_p, j, _peer(pos, k)).wait_recv()
            _remote(cw_ref, wbuf.at[t], s_w, r_w, j, _peer(pos, k)).wait_recv()
        for t in range(N_SHARD):
            mod_ref[:, t * Wm:(t + 1) * Wm] = pbuf[t] + ab_ref[:, t * Wm:(t + 1) * Wm]
            cwf_ref[:, t * Dq:(t + 1) * Dq] = wbuf[t]
        for cp in sends + psend:
            cp.wait_send()

    return pl.pallas_call(
        body, name="fwd_small",
        in_specs=[VMEM_FULL] * 5, out_specs=[VMEM_FULL] * 3,
        out_shape=[jax.ShapeDtypeStruct((16, D), F32), jax.ShapeDtypeStruct((16, 3 * D), F32),
                   jax.ShapeDtypeStruct((8, D), F32)],
        scratch_shapes=[pltpu.VMEM((N_DEV, 8, D), F32), pltpu.VMEM((16, Wm), F32),
                        pltpu.VMEM((N_SHARD, 16, Wm), F32), pltpu.VMEM((N_SHARD, 8, Dq), F32),
                        pltpu.SemaphoreType.DMA((7,)), pltpu.SemaphoreType.DMA((7,)),
                        pltpu.SemaphoreType.DMA((3,)), pltpu.SemaphoreType.DMA((3,)),
                        pltpu.SemaphoreType.DMA((3,)), pltpu.SemaphoreType.DMA((3,))],
        compiler_params=_cparams(None, VMEM_LIMIT),
    )(c8, cctx8, ada_w, ada_b, conv_w8)


AG_CHUNKS = 3


def _ag_in_proj(xm, w_in_s, w3_s):
    L, D = xm.shape
    Wc = w_in_s.shape[1]
    Wq = Wc // AG_CHUNKS
    Dh = D // 2
    Do = w3_s[0].shape[1]
    TM = min(1024, L // 4)
    NT = L // TM
    RC = min(128, Dh)
    NQ = AG_CHUNKS

    def body(xm_ref, wi_hbm, wa_ref, wb_ref, wo_ref, p_hbm, pqk_hbm, fi_hbm, f3_hbm,
             w_vm, cast_buf, s3, stage, qk_stage, ici_s, ici_r, d2d_s, d2d_r, w3_s_, w3_r_, loc, out_sem, qk_sem):
        pos = _position()
        c = pos[2]
        s = _shard_of(pos)
        sib = _peer(pos, 1)
        mine = pl.ds(pl.multiple_of(c * Dh, Dh), Dh)
        other = pl.ds(pl.multiple_of((1 - c) * Dh, Dh), Dh)

        def cast_half(hf):
            def step(i, carry):
                rows = pl.ds(pl.multiple_of(hf * Dh + i * RC, RC), RC)
                cp = pltpu.make_async_copy(wi_hbm.at[rows, :], cast_buf, loc.at[0])
                cp.start()
                cp.wait()
                for q in range(NQ):
                    w_vm[0, q, rows, :] = cast_buf[:, q * Wq:(q + 1) * Wq].astype(BF16)
                return carry
            lax.fori_loop(0, Dh // RC, step, 0)

        def abs_col(t, q):
            return pl.ds(pl.multiple_of(t * Wc + q * Wq, 128), Wq)

        cast_half(c)
        for a, w_ref in enumerate((wa_ref, wb_ref, wo_ref)):
            s3[a] = w_ref[...].astype(BF16)
        sends = []
        for q in range(NQ):
            for j, k in enumerate(CHIP_FLIPS):
                sends.append(_remote(w_vm.at[0, q, mine, :], w_vm.at[1 + j, q, mine, :], ici_s, ici_r,
                                     q * 3 + j, _peer(pos, k)))
        for j, k in enumerate(CHIP_FLIPS):
            sends.append(_remote(s3.at[:, c], f3_hbm.at[:, s, c], w3_s_, w3_r_, j, _peer(pos, k)))
        for cp in sends:
            cp.start()
        cast_half(1 - c)
        local = [pltpu.make_async_copy(s3, f3_hbm.at[:, s], loc.at[1])]
        local += [pltpu.make_async_copy(w_vm.at[0, q], fi_hbm.at[:, abs_col(s, q)], loc.at[2 + q]) for q in range(NQ)]
        for cp in local:
            cp.start()

        def out_copy(slot, rows, cols):
            return pltpu.make_async_copy(stage.at[slot], p_hbm.at[rows, cols], out_sem.at[slot])

        def block(r, q, t, first):
            cols = abs_col(t, q)

            def row_tile(rt, carry):
                rows = pl.ds(pl.multiple_of(rt * TM, TM), TM)
                acc = _dot(xm_ref[rows, :], w_vm[r, q])
                slot = lax.rem(rt, 2)

                @pl.when(rt >= 2 if first else rt >= 0)
                def _():
                    out_copy(slot, rows, cols).wait()

                stage[slot] = acc.astype(BF16)
                out_copy(slot, rows, cols).start()

                def keep_f32(lo, dst_col):
                    qk_stage[...] = acc[:, lo:lo + Dh]
                    cp = pltpu.make_async_copy(qk_stage, pqk_hbm.at[rows, dst_col:dst_col + Dh], qk_sem)
                    cp.start()
                    cp.wait()

                if q == NQ - 1:
                    @pl.when(t == 1)
                    def _():
                        keep_f32(Wq - Dh, 0)
                if q == 0:
                    @pl.when(t == 2)
                    def _():
                        keep_f32(0, Dh)
                return carry

            lax.fori_loop(0, NT, row_tile, 0)

        for q in range(NQ):
            block(0, q, s, q == 0)
        passed = []
        for q in range(NQ):
            for j, k in enumerate(CHIP_FLIPS):
                r, idx = 1 + j, q * 3 + j
                t = _shard_of(_peer(pos, k))
                _remote(w_vm.at[r, q, mine, :], w_vm.at[r, q, mine, :], ici_s, ici_r, idx, sib).wait_recv()
                fwd = _remote(w_vm.at[r, q, mine, :], w_vm.at[r, q, mine, :], d2d_s, d2d_r, idx, sib)
                fwd.start()
                passed.append(fwd)
                _remote(w_vm.at[r, q, other, :], w_vm.at[r, q, other, :], d2d_s, d2d_r, idx, sib).wait_recv()
                block(r, q, t, False)
                cp = pltpu.make_async_copy(w_vm.at[r, q], fi_hbm.at[:, abs_col(t, q)], loc.at[2 + NQ + idx])
                cp.start()
                local.append(cp)
        for j, k in enumerate(CHIP_FLIPS):
            t = _shard_of(_peer(pos, k))
            _remote(s3.at[:, c], f3_hbm.at[:, t, c], w3_s_, w3_r_, j, sib).wait_recv()
            fwd = _remote(f3_hbm.at[:, t, c], f3_hbm.at[:, t, c], w3_s_, w3_r_, 3 + j, sib)
            fwd.start()
            passed.append(fwd)
        for j, k in enumerate(CHIP_FLIPS):
            t = _shard_of(_peer(pos, k))
            _remote(s3.at[:, c], f3_hbm.at[:, t, 1 - c], w3_s_, w3_r_, 3 + j, sib).wait_recv()
        for cp in sends + passed:
            cp.wait_send()
        for cp in local:
            cp.wait()
        for slot in range(2):
            out_copy(slot, pl.ds(0, TM), abs_col(s, 0)).wait()

    n_loc = 2 + NQ + 3 * NQ
    return pl.pallas_call(
        body, name="ag_in_proj",
        in_specs=[VMEM_FULL, ANY, VMEM_FULL, VMEM_FULL, VMEM_FULL], out_specs=[ANY, ANY, ANY, ANY],
        out_shape=[jax.ShapeDtypeStruct((L, N_SHARD * Wc), BF16), jax.ShapeDtypeStruct((L, D), F32),
                   jax.ShapeDtypeStruct((D, N_SHARD * Wc), BF16), jax.ShapeDtypeStruct((3, N_SHARD, 2, Do, D), BF16)],
        scratch_shapes=[pltpu.VMEM((N_SHARD, NQ, D, Wq), BF16), pltpu.VMEM((RC, Wc), F32),
                        pltpu.VMEM((3, 2, Do, D), BF16), pltpu.VMEM((2, TM, Wq), BF16), pltpu.VMEM((TM, Dh), F32),
                        pltpu.SemaphoreType.DMA((3 * NQ,)), pltpu.SemaphoreType.DMA((3 * NQ,)),
                        pltpu.SemaphoreType.DMA((3 * NQ,)), pltpu.SemaphoreType.DMA((3 * NQ,)),
                        pltpu.SemaphoreType.DMA((6,)), pltpu.SemaphoreType.DMA((6,)),
                        pltpu.SemaphoreType.DMA((n_loc,)), pltpu.SemaphoreType.DMA((2,)), pltpu.SemaphoreType.DMA],
        compiler_params=_cparams(None, VMEM_LIMIT),
    )(xm, w_in_s, *w3_s)


def _pair_exchange_in(dw_other):
    def build(ins, outs, send, recv):
        return [_remote(ins[0], outs[0], send, recv, 0, _peer(_position(), 1))]

    return _Exchange((dw_other,), (jax.ShapeDtypeStruct(dw_other.shape, F32),), 1, build)


def _pair_exchange_w3(dw3):
    _, _, _, Do, D = dw3.shape

    def build(ins, outs, send, recv):
        pos = _position()
        return [_remote(ins[0].at[:, :, 1 - pos[2]], outs[0], send, recv, 0, _peer(pos, 1))]

    return _Exchange((dw3,), (jax.ShapeDtypeStruct((3, N_SHARD, Do, D), F32),), 1, build)


def _sum_pair_in(dw_mine, ri):
    Dh, Wf = dw_mine.shape
    Wc = Wf // N_SHARD
    tr = min(256, Dh)

    def body(a_ref, b_ref, o_ref):
        o_ref[...] = (a_ref[...] + b_ref[...]).astype(BF16)

    return pl.pallas_call(
        body, name="sum_pair_in", grid=(Dh // tr, N_SHARD),
        in_specs=[pl.BlockSpec((tr, Wc), lambda i, t: (i, t)), pl.BlockSpec((tr, Wc), lambda i, t: (i, t))],
        out_specs=pl.BlockSpec((None, tr, Wc), lambda i, t: (t, i, 0)),
        out_shape=jax.ShapeDtypeStruct((N_SHARD, Dh, Wc), BF16),
        compiler_params=_cparams(("parallel", "parallel")),
    )(dw_mine, ri)


def _sum_pair_w3(cidx, dw3, r3):
    _, _, _, Do, D = dw3.shape

    def body(c_ref, a_ref, b_ref, o_ref):
        o_ref[...] = (a_ref[...] + b_ref[...]).astype(BF16)

    return pl.pallas_call(
        body, name="sum_pair_w3",
        grid_spec=pltpu.PrefetchScalarGridSpec(
            num_scalar_prefetch=1, grid=(3,),
            in_specs=[pl.BlockSpec((None, N_SHARD, None, Do, D), lambda a, c: (a, 0, c[0], 0, 0)),
                      pl.BlockSpec((None, N_SHARD, Do, D), lambda a, c: (a, 0, 0, 0))],
            out_specs=pl.BlockSpec((None, N_SHARD, Do, D), lambda a, c: (a, 0, 0, 0))),
        out_shape=jax.ShapeDtypeStruct((3, N_SHARD, Do, D), BF16),
        compiler_params=_cparams(("parallel",)),
    )(cidx, dw3, r3)


def _chips_exchange_in(cs_in):
    _, Dh, Wc = cs_in.shape

    def build(ins, outs, send, recv):
        pos = _position()
        return [_remote(ins[0].at[_shard_of(_peer(pos, k))], outs[0].at[j], send, recv, j, _peer(pos, k))
                for j, k in enumerate(CHIP_FLIPS)]

    return _Exchange((cs_in,), (jax.ShapeDtypeStruct((3, Dh, Wc), BF16),), 3, build)


def _chips_exchange_w3(cs_3):
    _, _, Do, D = cs_3.shape

    def build(ins, outs, send, recv):
        pos = _position()
        return [_remote(ins[0].at[:, _shard_of(_peer(pos, k))], outs[0].at[j], send, recv, j, _peer(pos, k))
                for j, k in enumerate(CHIP_FLIPS)]

    return _Exchange((cs_3,), (jax.ShapeDtypeStruct((3, 3, Do, D), BF16),), 3, build)


def _sum_chips_in(csidx, cs_in, rb_in):
    _, Dh, Wc = cs_in.shape
    tr = min(256, Dh)

    def body(s_ref, a_ref, b_ref, o_ref):
        acc = a_ref[...].astype(F32)
        for j in range(3):
            acc = acc + b_ref[j].astype(F32)
        o_ref[...] = acc

    return pl.pallas_call(
        body, name="sum_chips_in",
        grid_spec=pltpu.PrefetchScalarGridSpec(
            num_scalar_prefetch=1, grid=(Dh // tr,),
            in_specs=[pl.BlockSpec((None, tr, Wc), lambda i, s: (s[1], i, 0)),
                      pl.BlockSpec((3, tr, Wc), lambda i, s: (0, i, 0))],
            out_specs=pl.BlockSpec((None, tr, Wc), lambda i, s: (s[0], i, 0))),
        out_shape=jax.ShapeDtypeStruct((2, Dh, Wc), F32),
        compiler_params=_cparams(("parallel",)),
    )(csidx, cs_in, rb_in)


def _sum_chips_w3(csidx, cs_3, rb_3):
    _, _, Do, D = cs_3.shape

    def body(s_ref, a_ref, b_ref, o_ref):
        acc = a_ref[...].astype(F32)
        for j in range(3):
            acc = acc + b_ref[j].astype(F32)
        o_ref[...] = acc

    return pl.pallas_call(
        body, name="sum_chips_w3",
        grid_spec=pltpu.PrefetchScalarGridSpec(
            num_scalar_prefetch=1, grid=(3,),
            in_specs=[pl.BlockSpec((None, None, Do, D), lambda a, s: (a, s[1], 0, 0)),
                      pl.BlockSpec((3, None, Do, D), lambda a, s: (0, a, 0, 0))],
            out_specs=pl.BlockSpec((None, None, Do, D), lambda a, s: (a, s[0], 0, 0))),
        out_shape=jax.ShapeDtypeStruct((3, 2, Do, D), F32),
        compiler_params=_cparams(("parallel",)),
    )(csidx, cs_3, rb_3)


def _rs_final(g_in, g_3):
    def body(hi_ref, h3_ref, gi_ref, g3_ref, send, recv):
        pos = _position()
        c = pos[2]
        sib = _peer(pos, 1)
        cps = [_remote(hi_ref.at[c], gi_ref.at[c], send, recv, 0, sib),
               _remote(h3_ref.at[:, c], g3_ref.at[:, c], send, recv, 1, sib)]
        for cp in cps:
            cp.start()
        _remote(hi_ref.at[1 - c], gi_ref.at[1 - c], send, recv, 0, sib).wait_recv()
        _remote(h3_ref.at[:, 1 - c], g3_ref.at[:, 1 - c], send, recv, 1, sib).wait_recv()
        for cp in cps:
            cp.wait_send()

    return pl.pallas_call(
        body, name="rs_final", in_specs=[ANY, ANY], out_specs=[ANY, ANY],
        out_shape=[jax.ShapeDtypeStruct(g_in.shape, F32), jax.ShapeDtypeStruct(g_3.shape, F32)],
        input_output_aliases={0: 0, 1: 1},
        scratch_shapes=[pltpu.SemaphoreType.DMA((2,)), pltpu.SemaphoreType.DMA((2,))],
    )(g_in, g_3)


def _adam_math(w, g, m, v):
    m = ADAM_B1 * m + (1.0 - ADAM_B1) * g
    v = ADAM_B2 * v + (1.0 - ADAM_B2) * (g * g)
    m_hat = m / (1.0 - ADAM_B1 ** ADAM_STEP)
    v_hat = v / (1.0 - ADAM_B2 ** ADAM_STEP)
    delta = -ADAM_LR * (m_hat / (jnp.sqrt(v_hat) + ADAM_EPS) + ADAM_WD * w)
    return delta, m, v


def _adamw(w, g, m, v, name):
    R, C = w.shape
    tr = min(128, R)

    def body(w_ref, g_ref, m_ref, v_ref, d_ref, nm_ref, nv_ref):
        d_ref[...], nm_ref[...], nv_ref[...] = _adam_math(w_ref[...], g_ref[...], m_ref[...], v_ref[...])

    blk = pl.BlockSpec((tr, C), lambda i: (i, 0))
    return pl.pallas_call(
        body, name=name, grid=(R // tr,), in_specs=[blk] * 4, out_specs=[blk] * 3,
        out_shape=[jax.ShapeDtypeStruct((R, C), F32)] * 3,
        compiler_params=_cparams(("parallel",), VMEM_LIMIT),
    )(w, g, m, v)


SMALL_ROWS = ("c_ctx", "norm_w", "conv_b", "gn_w", "final_norm_w")


def _bwd_small(stats, ada_w, Dq):
    D = stats[0].shape[1]
    Wm = ada_w.shape[1]

    def body(stx, stm, stc, stv, stg, stl, stlc, aw_ref, tot_ref, dm_sh, gcw, da_ref, loss_ref,
             vec_ref, vbuf, dm, amine, abuf, s_v, r_v, s_a, r_a):
        pos = _position()
        me, s = _dev_id(pos), _shard_of(pos)
        vec_ref[...] = jnp.zeros_like(vec_ref)
        vec_ref[0:2, :] = stx[0:2, :]
        vec_ref[2:3, :] = stm[1:2, :]
        vec_ref[3:5, :] = stc[0:2, :]
        vec_ref[5:6, :] = stx[2:3, :] + stc[2:3, :]
        vec_ref[6:7, :] = stv[3:4, :]
        vec_ref[7:8, :] = stg[0:1, :]
        vec_ref[8:9, :] = stm[0:1, :]
        vec_ref[9:12, :] = stv[0:3, :]
        vec_ref[12:14, 0:128] = stl[0:2, :] + stlc[0:2, :]
        vec_ref[14:15, :] = stm[2:3, :]
        vbuf[me] = vec_ref[...]
        sends = [_remote(vec_ref, vbuf.at[me], s_v, r_v, k - 1, _peer(pos, k)) for k in range(1, 8)]
        for cp in sends:
            cp.start()
        for k in range(1, 8):
            _remote(vec_ref, vbuf.at[_dev_id(_peer(pos, k))], s_v, r_v, k - 1, _peer(pos, k)).wait_recv()
        tot = vbuf[0]
        for d in range(1, N_DEV):
            tot = tot + vbuf[d]
        loss_ref[...] = jnp.zeros((8, 128), F32) + (0.5 / D) * _sum_all(tot[14:15, :])
        dm[...] = jnp.zeros_like(dm)
        for d in range(N_DEV):
            for r in range(3):
                dm[d:d + 1, r * D:(r + 1) * D] = vbuf[d, r:r + 1, :]
        dm[8:9, 0:D] = tot[3:4, :]
        dm[8:9, D:2 * D] = tot[4:5, :]
        for t in range(N_SHARD):
            @pl.when(s == t)
            def _(t=t):
                dm_sh[...] = dm[:, t * Wm:(t + 1) * Wm]
                gcw[...] = tot[9:12, t * Dq:(t + 1) * Dq]
        tot_ref[...] = tot
        part = lax.dot_general(dm_sh[8:16, :], aw_ref[...], (((1,), (1,)), ((), ())),
                               precision=lax.Precision.HIGHEST, preferred_element_type=F32)
        amine[...] = part
        abuf[s] = part
        asend = [_remote(amine, abuf.at[s], s_a, r_a, j, _peer(pos, k)) for j, k in enumerate(CHIP_FLIPS)]
        for cp in asend:
            cp.start()
        for j, k in enumerate(CHIP_FLIPS):
            _remote(amine, abuf.at[_shard_of(_peer(pos, k))], s_a, r_a, j, _peer(pos, k)).wait_recv()
        da = abuf[0]
        for t in range(1, N_SHARD):
            da = da + abuf[t]
        da_ref[...] = da
        for cp in sends + asend:
            cp.wait_send()

    row = lambda *shape: jax.ShapeDtypeStruct(shape, F32)
    return pl.pallas_call(
        body, name="bwd_small",
        in_specs=[VMEM_FULL] * 8, out_specs=[VMEM_FULL] * 5,
        out_shape=[row(16, D), row(16, Wm), row(3, Dq), row(8, D), row(8, 128)],
        scratch_shapes=[pltpu.VMEM((16, D), F32), pltpu.VMEM((N_DEV, 16, D), F32), pltpu.VMEM((16, 3 * D), F32),
                        pltpu.VMEM((8, D), F32), pltpu.VMEM((N_SHARD, 8, D), F32),
                        pltpu.SemaphoreType.DMA((7,)), pltpu.SemaphoreType.DMA((7,)),
                        pltpu.SemaphoreType.DMA((3,)), pltpu.SemaphoreType.DMA((3,))],
        compiler_params=_cparams(None, VMEM_LIMIT),
    )(*stats, ada_w)


def _small_update(tot, dm_sh, gcw, da, act, p_row, p_ab, p_cw, p_dl):
    D = act.shape[1]
    Wm = dm_sh.shape[1]
    Dq = gcw.shape[1]

    def body(tot_ref, dm_ref, gcw_ref, da_ref, act_ref, prow, pab, pcw, pdl, gaw_ref, *outs):
        o_q = [outs[8 * q:8 * (q + 1)] for q in range(4)]
        tot = tot_ref[...]
        gaw_ref[...] = lax.dot_general(act_ref[...], dm_ref[...], (((0,), (0,)), ((), ())),
                                       precision=lax.Precision.HIGHEST, preferred_element_type=F32)
        cc = prow[0, 0:1, :]
        sg = _sigmoid(cc)
        g_cctx = da_ref[0:1, :] * (sg * (1.0 + cc * (1.0 - sg)))

        def place_all(o, val):
            o[...] = val

        def emit(k, w, g, m, v, place=place_all):
            for q, val in enumerate((g,) + _adam_math(w, g, m, v)):
                place(o_q[q][k], val)

        g_rows = [g_cctx, tot[5:6, :], tot[6:7, :], tot[7:8, :], tot[8:9, :]]
        for k, g in enumerate(g_rows):
            emit(k, prow[0, k:k + 1, :], g, prow[1, k:k + 1, :], prow[2, k:k + 1, :])

        def place_ab(o, val):
            for r in range(3):
                o[0:1, r * D:(r + 1) * D] = val[r:r + 1, :]

        g_ab = jnp.concatenate([tot[0:1, :] + tot[3:4, :], tot[1:2, :] + tot[4:5, :], tot[2:3, :]], axis=0)
        emit(5, pab[0], g_ab, pab[1], pab[2], place_ab)
        emit(6, pcw[0], gcw_ref[...], pcw[1], pcw[2])
        g_dl = jnp.concatenate([tot[12:14, 0:128] * _sigmoid(-pdl[0, 0:2, :]), jnp.zeros((6, 128), F32)], axis=0)
        emit(7, pdl[0], g_dl, pdl[1], pdl[2])

    row = lambda *shape: jax.ShapeDtypeStruct(shape, F32)
    per_q = [row(1, D)] * 5 + [row(1, 3 * D), row(3, Dq), row(8, 128)]
    res = pl.pallas_call(
        body, name="small_update",
        in_specs=[VMEM_FULL] * 9, out_specs=[VMEM_FULL] * 33,
        out_shape=[row(D, Wm)] + per_q * 4,
        compiler_params=_cparams(None, VMEM_LIMIT),
    )(tot, dm_sh, gcw, da, act, p_row, p_ab, p_cw, p_dl)
    return res[0], [res[1 + 8 * q:1 + 8 * (q + 1)] for q in range(4)]


def _pad_rows(a, rows=8):
    return jnp.pad(a, ((0, rows - a.shape[0]), (0, 0)))


def kernel(x, c, ctx, c_ctx, norm_w, ada_w, ada_b, w_in, conv_w, conv_b, decay_logit, gn_w, w_a, w_b, w_out, final_norm_w, loss_target, m_c_ctx, m_norm_w, m_ada_w, m_ada_b, m_w_in, m_conv_w, m_conv_b, m_decay_logit, m_gn_w, m_w_a, m_w_b, m_w_out, m_final_norm_w, v_c_ctx, v_norm_w, v_ada_w, v_ada_b, v_w_in, v_conv_w, v_conv_b, v_decay_logit, v_gn_w, v_w_a, v_w_b, v_w_out, v_final_norm_w):
    L, D = x.shape[1], x.shape[2]
    H = D // DV
    Wc = w_in.shape[2]
    Do = D // 8
    pos = _position()
    me = _dev_id(pos)
    cidx = jnp.reshape(pos[2], (1,)).astype(jnp.int32)
    sidx = jnp.reshape(_shard_of(pos), (1,)).astype(jnp.int32)

    act, mod, conv_w8 = _fwd_small(_pad_rows(c), _pad_rows(c_ctx[None]), ada_w[0], ada_b, _pad_rows(conv_w[0]))
    mod_x = lax.dynamic_slice_in_dim(mod, me, 1, axis=0).reshape(3, D)
    mod_c = mod[8].reshape(3, D)
    lg = jax.nn.log_sigmoid(decay_logit[0])

    w3_s = tuple(w[0].reshape(2, Do, D) for w in (w_a, w_b, w_out))

    def project(xm):
        p, pqk, w_in_full, w3_full = _ag_in_proj(xm, w_in[0], w3_s)
        return p, pqk, w_in_full, w3_full.reshape(3, D, D)

    csidx = jnp.concatenate([cidx, sidx])
    groups, dret_c, xmt, cmt, dx1, sc_x, w_in_full, gh_3, sts = _local_step(
        x[0], ctx[0], loss_target[0], mod_x, mod_c, norm_w, conv_w8, conv_b, lg, gn_w, final_norm_w[None],
        project, csidx)
    st_mid, st_conv, st_gn, st_lg, st_lgc, st_c = sts

    (dw_other,) = _dw_in(1 - cidx, xmt, groups, cmt, dret_c, D, "dw_in_other")
    dw_mine, ra_in = _dw_in(cidx, xmt, groups, cmt, dret_c, D, "dw_in_mine", _pair_exchange_in(dw_other))
    cs_in = _sum_pair_in(dw_mine, ra_in)
    grad_x, st_x, rb_in = _dxm(groups, 0, w_in_full, x[0], norm_w, sc_x, dx1, "dxm_x", _chips_exchange_in(cs_in))
    gh_in = _sum_chips_in(csidx, cs_in, rb_in)
    g_in, g_3 = _rs_final(gh_in, gh_3)
    g_w_in = g_in.reshape(D, Wc)
    g_3 = g_3.reshape(3, D // 4, D)

    zeros3 = jnp.zeros((3, D), F32)
    p_row = jnp.concatenate(
        [r for t in ((c_ctx[None], norm_w, conv_b, gn_w, final_norm_w[None], zeros3),
                     (m_c_ctx[None], m_norm_w, m_conv_b, m_gn_w, m_final_norm_w[None], zeros3),
                     (v_c_ctx[None], v_norm_w, v_conv_b, v_gn_w, v_final_norm_w[None], zeros3)) for r in t],
        axis=0).reshape(3, 8, D)
    p_ab = jnp.concatenate([ada_b, m_ada_b, v_ada_b], axis=0).reshape(3, 3, D)
    p_cw = jnp.concatenate([conv_w, m_conv_w, v_conv_w], axis=0)
    p_dl = jnp.pad(jnp.concatenate([decay_logit, m_decay_logit, v_decay_logit], axis=0), ((0, 0), (0, 6), (0, 128 - H)))
    tot, dm_sh, gcw, da, loss_t = _bwd_small((st_x, st_mid, st_c, st_conv, st_gn, st_lg, st_lgc), ada_w[0],
                                             conv_w.shape[2])
    g_ada_w, small = _small_update(tot, dm_sh, gcw, da, act, p_row, p_ab, p_cw, p_dl)

    upd_in = _adamw(w_in[0], g_w_in, m_w_in[0], v_w_in[0], "adamw_w_in")
    upd_ada = _adamw(ada_w[0], g_ada_w, m_ada_w[0], v_ada_w[0], "adamw_ada_w")
    upd_a = _adamw(w_a[0], g_3[0], m_w_a[0], v_w_a[0], "adamw_w_a")
    upd_b = _adamw(w_b[0], g_3[1], m_w_b[0], v_w_b[0], "adamw_w_b")
    upd_o = _adamw(w_out[0], g_3[2], m_w_out[0], v_w_out[0], "adamw_w_out")

    def leaves(q):
        big = lambda g, upd: (g if q == 0 else upd[q - 1])[None]
        r_cctx, r_norm, r_convb, r_gn, r_fnorm, r_ab, r_cw, r_dl = small[q]
        return [r_cctx.reshape(D), r_norm, big(g_ada_w, upd_ada), r_ab, big(g_w_in, upd_in),
                r_cw[None], r_convb, r_dl[0:2, 0:H][None], r_gn,
                big(g_3[0], upd_a), big(g_3[1], upd_b), big(g_3[2], upd_o), r_fnorm.reshape(D)]

    loss = loss_t[0, 0]
    return (loss, grad_x[None], *leaves(0), *leaves(1), *leaves(2), *leaves(3))
```

```python
from typing import Callable, NamedTuple

import jax
import jax.numpy as jnp
from jax import lax
from jax.experimental import pallas as pl
from jax.experimental.pallas import tpu as pltpu

F32 = jnp.float32
BF16 = jnp.bfloat16
MESH = pl.DeviceIdType.MESH

CHUNK = 128
RET_CPB = 4
DV = 128
DK = 64
GRID_W = 64
ROPE_BASE = 10000.0
EPS = 1e-6
K_SCALE = DK ** -0.5
N_SHARD = 4
N_DEV = 8

ADAM_LR = 0.001
ADAM_B1 = 0.9
ADAM_B2 = 0.999
ADAM_EPS = 1e-08
ADAM_WD = 0.01
ADAM_STEP = 10

VMEM_LIMIT = 56 * 1024 * 1024


def _cparams(sem=None, vmem=None):
    kw = {}
    if sem is not None:
        kw["dimension_semantics"] = sem
    if vmem is not None:
        kw["vmem_limit_bytes"] = vmem
    return pltpu.CompilerParams(**kw)


def _dot(a, b):
    return jnp.dot(a, b, preferred_element_type=F32)


def _dot_nt(a, b):
    return lax.dot_general(a, b, (((1,), (1,)), ((), ())), preferred_element_type=F32)


def _dot_tn(a, b):
    return lax.dot_general(a, b, (((0,), (0,)), ((), ())), preferred_element_type=F32)


def _sigmoid(x):
    return 1.0 / (1.0 + jnp.exp(-x))


def _sum_all(x):
    return jnp.sum(jnp.sum(x, axis=1, keepdims=True), axis=0, keepdims=True)


def _swap_halves(t):
    n = t.shape[1]
    lane = lax.broadcasted_iota(jnp.int32, t.shape, 1)
    low = (lane & 32) == 0
    return jnp.where(low, pltpu.roll(t, n - 32, 1), pltpu.roll(t, 32, 1))


def _vec_spec(d):
    return pl.BlockSpec((1, d), lambda *a: (0, 0))


def _norm_mod(x, nw, sc, sh, name):
    L, D = x.shape
    tl = min(256, L)

    def body(x_ref, nw_ref, sc_ref, sh_ref, xm_ref, xmt_ref):
        xv = x_ref[...]
        r = lax.rsqrt(jnp.mean(xv * xv, axis=-1, keepdims=True) + EPS)
        xm = (xv * r * nw_ref[...]) * (1.0 + sc_ref[...]) + sh_ref[...]
        xm_ref[...] = xm.astype(BF16)
        xmt_ref[...] = xm.T.astype(BF16)

    return pl.pallas_call(
        body, name=name, grid=(L // tl,),
        in_specs=[pl.BlockSpec((tl, D), lambda i: (i, 0)), _vec_spec(D), _vec_spec(D), _vec_spec(D)],
        out_specs=[pl.BlockSpec((tl, D), lambda i: (i, 0)), pl.BlockSpec((D, tl), lambda i: (0, i))],
        out_shape=[jax.ShapeDtypeStruct((L, D), BF16), jax.ShapeDtypeStruct((D, L), BF16)],
        compiler_params=_cparams(("parallel",)),
    )(x, nw, sc, sh)


QK_BLOCK, V_BLOCK = 4, 5


def _in_proj(xm, w, name, first=0, count=None):
    M, D = xm.shape
    count = w.shape[1] // D if count is None else count
    tm = min(1024, M)

    def body(a_ref, b_ref, o_ref, qk_ref):
        acc = _dot(a_ref[...], b_ref[...])
        o_ref[...] = acc.astype(o_ref.dtype)

        @pl.when(pl.program_id(1) == QK_BLOCK - first)
        def _():
            qk_ref[...] = acc

    return pl.pallas_call(
        body, name=name, grid=(M // tm, count),
        in_specs=[pl.BlockSpec((tm, D), lambda i, j: (i, 0)), pl.BlockSpec((D, D), lambda i, j: (0, first + j))],
        out_specs=[pl.BlockSpec((tm, D), lambda i, j: (i, j)), pl.BlockSpec((tm, D), lambda i, j: (i, 0))],
        out_shape=[jax.ShapeDtypeStruct((M, count * D), BF16), jax.ShapeDtypeStruct((M, D), F32)],
        compiler_params=_cparams(("parallel", "arbitrary")),
    )(xm, w)


def _halo_specs(tl, L, D, col):
    hb = tl // 16
    last = L // 16 - 1
    prev = pl.BlockSpec((16, D), lambda i: (jnp.maximum(i * hb - 1, 0), col))
    nxt = pl.BlockSpec((16, D), lambda i: (jnp.minimum((i + 1) * hb, last), col))
    return prev, nxt


def _shift_rows(u, above, below):
    tl = u.shape[0]
    row = lax.broadcasted_iota(jnp.int32, u.shape, 0)
    dn = jnp.where(row == 0, above, pltpu.roll(u, 1, 0))
    up = jnp.where(row == tl - 1, below, pltpu.roll(u, tl - 1, 0))
    return dn, up


def _conv_gate_fwd(p, conv_w, conv_b, D):
    L = p.shape[0]
    tl = min(256, L)
    nt = L // tl

    def body(h_ref, bg_ref, cg_ref, za_ref, hp_ref, hn_ref, cp_ref, cn_ref, w_ref, b_ref, o_ref):
        i = pl.program_id(0)
        u = cg_ref[...].astype(F32) * h_ref[...].astype(F32)
        above = cp_ref[15:16, :].astype(F32) * hp_ref[15:16, :].astype(F32)
        below = cn_ref[0:1, :].astype(F32) * hn_ref[0:1, :].astype(F32)
        above = jnp.where(i == 0, 0.0, above)
        below = jnp.where(i == nt - 1, 0.0, below)
        dn, up = _shift_rows(u, above, below)
        co = w_ref[0:1, :] * dn + w_ref[1:2, :] * u + w_ref[2:3, :] * up + b_ref[...]
        za = za_ref[...].astype(F32)
        o_ref[...] = (za * _sigmoid(za) * bg_ref[...].astype(F32) * co).astype(BF16)

    main = lambda col: pl.BlockSpec((tl, D), lambda i: (i, col))
    hp, hn = _halo_specs(tl, L, D, 0)
    cp, cn = _halo_specs(tl, L, D, 2)
    return pl.pallas_call(
        body, name="conv_gate_fwd", grid=(nt,),
        in_specs=[main(0), main(1), main(2), main(3), hp, hn, cp, cn,
                  pl.BlockSpec((8, D), lambda i: (0, 0)), _vec_spec(D)],
        out_specs=pl.BlockSpec((tl, D), lambda i: (i, 0)),
        out_shape=jax.ShapeDtypeStruct((L, D), BF16),
        compiler_params=_cparams(("parallel",)),
    )(p, p, p, p, p, p, p, p, conv_w, conv_b)


def _rope_tables(L):
    pos = jnp.arange(L)
    row = (pos // GRID_W).astype(F32)
    col = (pos % GRID_W).astype(F32)
    nf = DK // 4
    inv = ROPE_BASE ** (-jnp.arange(nf, dtype=F32) / nf)
    ang = jnp.concatenate([row[:, None] * inv, col[:, None] * inv], axis=-1)
    cos, sin = jnp.cos(ang), jnp.sin(ang)
    return jnp.concatenate([cos, cos, cos, cos], axis=-1), jnp.concatenate([-sin, sin, -sin, sin], axis=-1)


def _smem_spec():
    return pl.BlockSpec(memory_space=pltpu.SMEM)


def _pair_select(e0, e1):
    row = lax.broadcasted_iota(jnp.int32, e0.shape, 0)
    return jnp.where(row < DK, e0, e1)


def _head_lane_mask(shape, e):
    lane = lax.broadcasted_iota(jnp.int32, shape, 1)
    return (lane < DK) if e == 0 else (lane >= DK)


def _ctx_states(pc, pqk_c, lg, D):
    Lc = pc.shape[0]
    H = D // DV

    def body(lg_ref, k_ref, v_ref, s_ref):
        m = lax.broadcasted_iota(jnp.int32, (Lc, DV), 0).astype(F32)
        for pr in range(H // 2):
            k2 = k_ref[:, pr * 128:(pr + 1) * 128].astype(F32) * K_SCALE
            res = [[None, None], [None, None]]
            for e in range(2):
                h = 2 * pr + e
                v = v_ref[:, h * DV:(h + 1) * DV]
                dec_f = jnp.exp(lg_ref[0, h] * (Lc - 1.0 - m))
                dec_b = jnp.exp(lg_ref[1, h] * m)
                res[0][e] = _dot_tn((k2 * dec_f).astype(BF16), v)
                res[1][e] = _dot_tn((k2 * dec_b).astype(BF16), v)
            s_ref[0, pr] = _pair_select(res[0][0], res[0][1])
            s_ref[1, pr] = _pair_select(res[1][0], res[1][1])

    return pl.pallas_call(
        body, name="ctx_states", grid=(1,),
        in_specs=[_smem_spec(), pl.BlockSpec((Lc, D // 2), lambda i: (0, 1)), pl.BlockSpec((Lc, D), lambda i: (0, 1))],
        out_specs=pl.BlockSpec((2, H // 2, 128, 128), lambda i: (0, 0, 0, 0)),
        out_shape=jax.ShapeDtypeStruct((2, H // 2, 128, 128), F32),
    )(lg, pqk_c, pc)


T_M, T_MT = 0, 1
T_MF1, T_MB1 = 2, 3
T_QF, T_QB = 4, 5
T_KF, T_KB = 6, 7


def _decay_tables(lg, H):
    def body(lg_ref, t_ref):
        h = pl.program_id(0)
        lgf, lgb = lg_ref[0, h], lg_ref[1, h]
        i = lax.broadcasted_iota(jnp.int32, (CHUNK, CHUNK), 0).astype(F32)
        j = lax.broadcasted_iota(jnp.int32, (CHUNK, CHUNK), 1).astype(F32)
        d = i - j
        mf = jnp.where(d > 0, jnp.exp(lgf * jnp.maximum(d, 0.0)), 0.0)
        mb = jnp.where(d < 0, jnp.exp(lgb * jnp.maximum(-d, 0.0)), 0.0)
        mf_t = jnp.where(d < 0, jnp.exp(lgf * jnp.maximum(-d, 0.0)), 0.0)
        mb_t = jnp.where(d > 0, jnp.exp(lgb * jnp.maximum(d, 0.0)), 0.0)
        diag = jnp.where(d == 0, 2.0, 0.0)
        t_ref[0, T_M] = mf + mb + diag
        t_ref[0, T_MT] = mf_t + mb_t + diag
        t_ref[0, T_MF1] = mf * d
        t_ref[0, T_MB1] = mb * (-d)
        t_ref[0, T_QF] = jnp.exp(lgf * (i + 1.0))
        t_ref[0, T_QB] = jnp.exp(lgb * (CHUNK - i))
        t_ref[0, T_KF] = jnp.exp(lgf * (CHUNK - 1.0 - i))
        t_ref[0, T_KB] = jnp.exp(lgb * i)

    return pl.pallas_call(
        body, name="decay_tables", grid=(H,), in_specs=[_smem_spec()],
        out_specs=pl.BlockSpec((1, 8, CHUNK, CHUNK), lambda h: (h, 0, 0, 0)),
        out_shape=jax.ShapeDtypeStruct((H, 8, CHUNK, CHUNK), F32),
    )(lg)


def _tab_spec(H):
    return pl.BlockSpec((H, 8, CHUNK, CHUNK), lambda n: (0, 0, 0, 0))


def _chunk_decay(tab_ref, h):
    return tab_ref[h, T_QF, CHUNK - 1:CHUNK, :], tab_ref[h, T_QB, 0:1, :]


def _ret_states(kr, p, s0, tab, D):
    L = kr.shape[0]
    H = D // DV
    N = L // CHUNK
    HP = H // 2

    def body(tab_ref, kf_ref, kb_ref, vf_ref, vb_ref, s0_ref, sf_out, sb_out, sf, sb):
        n = pl.program_id(0)

        @pl.when(n == 0)
        def _():
            sf[...] = s0_ref[0]
            sb[...] = s0_ref[1]

        for cc in range(RET_CPB):
            cf_, cb_ = cc, RET_CPB - 1 - cc
            rf, rb = slice(cf_ * CHUNK, (cf_ + 1) * CHUNK), slice(cb_ * CHUNK, (cb_ + 1) * CHUNK)
            sf_out[cf_] = sf[...]
            sb_out[cb_] = sb[...]
            for pr in range(HP):
                kf2 = kf_ref[rf, pr * 128:(pr + 1) * 128].astype(F32)
                kb2 = kb_ref[rb, pr * 128:(pr + 1) * 128].astype(F32)
                inc_f, inc_b, gf, gb = [], [], [], []
                for e in range(2):
                    h = 2 * pr + e
                    inc_f.append(_dot_tn((kf2 * tab_ref[h, T_KF]).astype(BF16), vf_ref[rf, h * DV:(h + 1) * DV]))
                    inc_b.append(_dot_tn((kb2 * tab_ref[h, T_KB]).astype(BF16), vb_ref[rb, h * DV:(h + 1) * DV]))
                    cf, cb = _chunk_decay(tab_ref, h)
                    gf.append(jnp.broadcast_to(cf, (128, 128)))
                    gb.append(jnp.broadcast_to(cb, (128, 128)))
                sf[pr] = _pair_select(gf[0], gf[1]) * sf[pr] + _pair_select(inc_f[0], inc_f[1])
                sb[pr] = _pair_select(gb[0], gb[1]) * sb[pr] + _pair_select(inc_b[0], inc_b[1])

    st = jax.ShapeDtypeStruct((N, HP, 128, 128), F32)
    R = RET_CPB * CHUNK
    NB = N // RET_CPB
    return pl.pallas_call(
        body, name="ret_states", grid=(NB,),
        in_specs=[_tab_spec(H),
                  pl.BlockSpec((R, D // 2), lambda n: (n, 0)),
                  pl.BlockSpec((R, D // 2), lambda n: (NB - 1 - n, 0)),
                  pl.BlockSpec((R, D), lambda n: (n, 5)),
                  pl.BlockSpec((R, D), lambda n: (NB - 1 - n, 5)),
                  pl.BlockSpec((2, HP, 128, 128), lambda n: (0, 0, 0, 0))],
        out_specs=[pl.BlockSpec((RET_CPB, HP, 128, 128), lambda n: (n, 0, 0, 0)),
                   pl.BlockSpec((RET_CPB, HP, 128, 128), lambda n: (NB - 1 - n, 0, 0, 0))],
        out_shape=[st, st],
        scratch_shapes=[pltpu.VMEM((HP, 128, 128), F32), pltpu.VMEM((HP, 128, 128), F32)],
        compiler_params=_cparams(("arbitrary",)),
    )(tab, kr, kr, p, p, s0)


def _ret_out(qr, kr, p, sf_prev, sb_prev, gn_w, tab, D):
    L = qr.shape[0]
    H = D // DV
    N = L // CHUNK
    HP = H // 2

    def body(tab_ref, q_ref, k_ref, v_ref, zb_ref, sf_ref, sb_ref, gn_ref, o_ref, yb_ref):
        def chunk(cc, carry):
            rows = pl.ds(pl.multiple_of(cc * CHUNK, CHUNK), CHUNK)
            for pr in range(HP):
                q2 = q_ref[rows, pr * 128:(pr + 1) * 128]
                k2 = k_ref[rows, pr * 128:(pr + 1) * 128]
                sfp = sf_ref[cc, pr].astype(BF16)
                sbp = sb_ref[cc, pr].astype(BF16)
                for e in range(2):
                    h = 2 * pr + e
                    sl = slice(h * DV, (h + 1) * DV)
                    qm = jnp.where(_head_lane_mask(q2.shape, e), q2, jnp.zeros_like(q2))
                    a = (_dot_nt(qm, k2) * tab_ref[h, T_M]).astype(BF16)
                    qf = qm.astype(F32)
                    o = _dot(a, v_ref[rows, sl])
                    o += _dot((qf * tab_ref[h, T_QF]).astype(BF16), sfp)
                    o += _dot((qf * tab_ref[h, T_QB]).astype(BF16), sbp)
                    o_ref[rows, sl] = o
                    mu = jnp.mean(o, axis=-1, keepdims=True)
                    oc = o - mu
                    rstd = lax.rsqrt(jnp.mean(oc * oc, axis=-1, keepdims=True) + EPS)
                    zb = zb_ref[rows, sl].astype(F32)
                    yb_ref[rows, sl] = (zb * _sigmoid(zb) * (oc * rstd * gn_ref[:, sl])).astype(BF16)
            return carry

        lax.fori_loop(0, RET_CPB, chunk, 0)

    R = RET_CPB * CHUNK
    return pl.pallas_call(
        body, name="ret_out", grid=(N // RET_CPB,),
        in_specs=[_tab_spec(H),
                  pl.BlockSpec((R, D // 2), lambda n: (n, 0)),
                  pl.BlockSpec((R, D // 2), lambda n: (n, 0)),
                  pl.BlockSpec((R, D), lambda n: (n, 5)),
                  pl.BlockSpec((R, D), lambda n: (n, 6)),
                  pl.BlockSpec((RET_CPB, HP, 128, 128), lambda n: (n, 0, 0, 0)),
                  pl.BlockSpec((RET_CPB, HP, 128, 128), lambda n: (n, 0, 0, 0)),
                  _vec_spec(D)],
        out_specs=[pl.BlockSpec((R, D), lambda n: (n, 0)), pl.BlockSpec((R, D), lambda n: (n, 0))],
        out_shape=[jax.ShapeDtypeStruct((L, D), F32), jax.ShapeDtypeStruct((L, D), BF16)],
        compiler_params=_cparams(("parallel",)),
    )(tab, qr, kr, p, p, sf_prev, sb_prev, gn_w)


def _mid(ya, yb, p, x, tgt, w3, g, fw, D):
    L = x.shape[0]
    tm = min(256, L)
    nt = L // tm

    def body(ya_ref, yb_ref, ga_ref, gb_ref, x_ref, t_ref, w_hbm, g_ref, fw_ref,
             dx1_ref, dya_ref, dyb_ref, dgab_ref, dw_hbm, st_ref, w_vm, dw_acc, sem):
        i = pl.program_id(0)

        @pl.when(i == 0)
        def _():
            cp = pltpu.make_async_copy(w_hbm, w_vm, sem)
            cp.start()
            dw_acc[...] = jnp.zeros_like(dw_acc)
            st_ref[...] = jnp.zeros_like(st_ref)
            cp.wait()

        ya_b, yb_b = ya_ref[...], yb_ref[...]
        y_a = _dot(ya_b, w_vm[0])
        y_b = _dot(yb_b, w_vm[1])
        sga = _sigmoid(ga_ref[...].astype(F32))
        sgb = _sigmoid(gb_ref[...].astype(F32))
        mix_b = (sga * y_a + sgb * y_b).astype(BF16)
        y_x = _dot(mix_b, w_vm[2])
        gvec, fwv = g_ref[...], fw_ref[...]
        x1 = x_ref[...] + gvec * y_x
        r1 = lax.rsqrt(jnp.mean(x1 * x1, axis=-1, keepdims=True) + EPS)
        xh = x1 * r1
        diff = xh * fwv - t_ref[...]
        dout = diff * (1.0 / D)
        dxh = dout * fwv
        dx1 = r1 * (dxh - xh * jnp.mean(dxh * xh, axis=-1, keepdims=True))
        dx1_ref[...] = dx1
        st_ref[0:1, :] += jnp.sum(dout * xh, axis=0, keepdims=True)
        st_ref[1:2, :] += jnp.sum(dx1 * y_x, axis=0, keepdims=True)
        st_ref[2:3, :] += jnp.sum(diff * diff, axis=0, keepdims=True)
        dyx_b = (dx1 * gvec).astype(BF16)
        dmix = _dot_nt(dyx_b, w_vm[2])
        dw_acc[2] += _dot_tn(mix_b, dyx_b)
        dya_b = (dmix * sga).astype(BF16)
        dyb_b = (dmix * sgb).astype(BF16)
        dgab_ref[:, 0:D] = (dmix * y_a * sga * (1.0 - sga)).astype(BF16)
        dgab_ref[:, D:2 * D] = (dmix * y_b * sgb * (1.0 - sgb)).astype(BF16)
        dya_ref[...] = _dot_nt(dya_b, w_vm[0])
        dyb_ref[...] = _dot_nt(dyb_b, w_vm[1])
        dw_acc[0] += _dot_tn(ya_b, dya_b)
        dw_acc[1] += _dot_tn(yb_b, dyb_b)

        @pl.when(i == nt - 1)
        def _():
            out = pltpu.make_async_copy(dw_acc, dw_hbm, sem)
            out.start()
            out.wait()

    row = lambda col: pl.BlockSpec((tm, D), lambda i: (i, col))
    any_spec = pl.BlockSpec(memory_space=pl.ANY)
    f32o = jax.ShapeDtypeStruct((L, D), F32)
    return pl.pallas_call(
        body, name="mid", grid=(nt,),
        in_specs=[row(0), row(0), row(7), row(8), row(0), row(0), any_spec, _vec_spec(D), _vec_spec(D)],
        out_specs=[row(0), row(0), row(0), pl.BlockSpec((tm, 2 * D), lambda i: (i, 0)), any_spec,
                   pl.BlockSpec((8, D), lambda i: (0, 0))],
        out_shape=[f32o, f32o, f32o, jax.ShapeDtypeStruct((L, 2 * D), BF16),
                   jax.ShapeDtypeStruct((3, D, D), F32), jax.ShapeDtypeStruct((8, D), F32)],
        scratch_shapes=[pltpu.VMEM((3, D, D), BF16), pltpu.VMEM((3, D, D), F32), pltpu.SemaphoreType.DMA],
        compiler_params=_cparams(("arbitrary",), VMEM_LIMIT),
    )(ya, yb, p, p, x, tgt, w3, g, fw)


def _conv_bwd(dya, p, conv_w, conv_b, D, exchange=None):
    L = p.shape[0]
    tl = min(256, L)
    nt = L // tl

    def body(d_ref, h_ref, bg_ref, cg_ref, za_ref,
             dp_ref, dn_ref, hp_ref, hn_ref, bp_ref, bn_ref, cp_ref, cn_ref, zp_ref, zn_ref,
             w_ref, b_ref, dc_ref, st_ref):
        i = pl.program_id(0)

        @pl.when(i == 0)
        def _():
            st_ref[...] = jnp.zeros_like(st_ref)

        first, last = i == 0, i == nt - 1
        h = h_ref[...].astype(F32)
        cg = cg_ref[...].astype(F32)
        bg = bg_ref[...].astype(F32)
        za = za_ref[...].astype(F32)
        dy = d_ref[...].astype(F32)
        u = cg * h
        u_above = jnp.where(first, 0.0, cp_ref[15:16, :].astype(F32) * hp_ref[15:16, :].astype(F32))
        u_below = jnp.where(last, 0.0, cn_ref[0:1, :].astype(F32) * hn_ref[0:1, :].astype(F32))
        u_dn, u_up = _shift_rows(u, u_above, u_below)
        w0, w1, w2 = w_ref[0:1, :], w_ref[1:2, :], w_ref[2:3, :]
        co = w0 * u_dn + w1 * u + w2 * u_up + b_ref[...]
        sz = _sigmoid(za)
        silu = za * sz
        dc_ref[:, 3 * D:4 * D] = (dy * bg * co * (sz * (1.0 + za * (1.0 - sz)))).astype(BF16)
        dc_ref[:, D:2 * D] = (dy * silu * co).astype(BF16)
        dco = dy * silu * bg

        def edge(dr, zr, br, r):
            z = zr[r:r + 1, :].astype(F32)
            return dr[r:r + 1, :].astype(F32) * (z * _sigmoid(z)) * br[r:r + 1, :].astype(F32)

        dco_above = jnp.where(first, 0.0, edge(dp_ref, zp_ref, bp_ref, 15))
        dco_below = jnp.where(last, 0.0, edge(dn_ref, zn_ref, bn_ref, 0))
        dco_dn, dco_up = _shift_rows(dco, dco_above, dco_below)
        du = w0 * dco_up + w1 * dco + w2 * dco_dn
        dc_ref[:, 2 * D:3 * D] = (du * h).astype(BF16)
        dc_ref[:, 0:D] = (du * cg).astype(BF16)
        st_ref[0:1, :] += jnp.sum(dco * u_dn, axis=0, keepdims=True)
        st_ref[1:2, :] += jnp.sum(dco * u, axis=0, keepdims=True)
        st_ref[2:3, :] += jnp.sum(dco * u_up, axis=0, keepdims=True)
        st_ref[3:4, :] += jnp.sum(dco, axis=0, keepdims=True)

    main = lambda col: pl.BlockSpec((tl, D), lambda i: (i, col))
    halos = []
    for col in (0, 0, 1, 2, 3):
        halos.extend(_halo_specs(tl, L, D, col))
    return _riding_call(
        body, exchange, nt, name="conv_bwd",
        args=(dya, p, p, p, p, dya, dya, p, p, p, p, p, p, p, p, conv_w, conv_b),
        in_specs=[main(0), main(0), main(1), main(2), main(3)] + halos
                 + [pl.BlockSpec((8, D), lambda i: (0, 0)), _vec_spec(D)],
        out_specs=[pl.BlockSpec((tl, 4 * D), lambda i: (i, 0)), pl.BlockSpec((8, D), lambda i: (0, 0))],
        out_shape=[jax.ShapeDtypeStruct((L, 4 * D), BF16), jax.ShapeDtypeStruct((8, D), F32)],
        cparams=_cparams(("arbitrary",)))


def _ret_bwd_pre(dyb, p, o, gn_w, D):
    L = o.shape[0]
    H = D // DV
    tl = min(256, L)

    def body(d_ref, zb_ref, o_ref, gn_ref, do_ref, dzb_ref, st_ref):
        @pl.when(pl.program_id(0) == 0)
        def _():
            st_ref[...] = jnp.zeros_like(st_ref)

        for h in range(H):
            sl = slice(h * DV, (h + 1) * DV)
            ov = o_ref[:, sl]
            mu = jnp.mean(ov, axis=-1, keepdims=True)
            oc = ov - mu
            rstd = lax.rsqrt(jnp.mean(oc * oc, axis=-1, keepdims=True) + EPS)
            rn = oc * rstd
            gw = gn_ref[:, sl]
            zb = zb_ref[:, sl].astype(F32)
            sz = _sigmoid(zb)
            dy = d_ref[:, sl].astype(F32)
            dzb_ref[:, sl] = (dy * (rn * gw) * (sz * (1.0 + zb * (1.0 - sz)))).astype(BF16)
            dretn = dy * (zb * sz)
            st_ref[0:1, sl] += jnp.sum(dretn * rn, axis=0, keepdims=True)
            drn = dretn * gw
            do = rstd * (drn - jnp.mean(drn, axis=-1, keepdims=True)
                         - rn * jnp.mean(drn * rn, axis=-1, keepdims=True))
            do_ref[:, sl] = do.astype(BF16)

    main = lambda col: pl.BlockSpec((tl, D), lambda i: (i, col))
    bfo = jax.ShapeDtypeStruct((L, D), BF16)
    return pl.pallas_call(
        body, name="ret_bwd_pre", grid=(L // tl,),
        in_specs=[main(0), main(6), main(0), _vec_spec(D)],
        out_specs=[main(0), main(0), pl.BlockSpec((8, D), lambda i: (0, 0))],
        out_shape=[bfo, bfo, jax.ShapeDtypeStruct((8, D), F32)],
        compiler_params=_cparams(("arbitrary",)),
    )(dyb, p, o, gn_w)


def _ret_bwd_states(qr, do, tab, D):
    L = qr.shape[0]
    H = D // DV
    N = L // CHUNK
    HP = H // 2

    def body(tab_ref, qf_ref, qb_ref, dof_ref, dob_ref, dsf_out, dsb_out, ds0_out, dsf, dsb):
        n = pl.program_id(0)

        @pl.when(n == 0)
        def _():
            dsf[...] = jnp.zeros_like(dsf)
            dsb[...] = jnp.zeros_like(dsb)

        for cc in range(RET_CPB):
            cf_, cb_ = RET_CPB - 1 - cc, cc
            rf, rb = slice(cf_ * CHUNK, (cf_ + 1) * CHUNK), slice(cb_ * CHUNK, (cb_ + 1) * CHUNK)
            dsf_out[cf_] = dsf[...]
            dsb_out[cb_] = dsb[...]
            for pr in range(HP):
                qf2 = qf_ref[rf, pr * 128:(pr + 1) * 128].astype(F32)
                qb2 = qb_ref[rb, pr * 128:(pr + 1) * 128].astype(F32)
                inc_f, inc_b, gf, gb = [], [], [], []
                for e in range(2):
                    h = 2 * pr + e
                    inc_f.append(_dot_tn((qf2 * tab_ref[h, T_QF]).astype(BF16), dof_ref[rf, h * DV:(h + 1) * DV]))
                    inc_b.append(_dot_tn((qb2 * tab_ref[h, T_QB]).astype(BF16), dob_ref[rb, h * DV:(h + 1) * DV]))
                    cf, cb = _chunk_decay(tab_ref, h)
                    gf.append(jnp.broadcast_to(cf, (128, 128)))
                    gb.append(jnp.broadcast_to(cb, (128, 128)))
                dsf[pr] = _pair_select(gf[0], gf[1]) * dsf[pr] + _pair_select(inc_f[0], inc_f[1])
                dsb[pr] = _pair_select(gb[0], gb[1]) * dsb[pr] + _pair_select(inc_b[0], inc_b[1])

        @pl.when(n == NB - 1)
        def _():
            ds0_out[0] = dsf[...]
            ds0_out[1] = dsb[...]

    st = jax.ShapeDtypeStruct((N, HP, 128, 128), F32)
    R = RET_CPB * CHUNK
    NB = N // RET_CPB
    return pl.pallas_call(
        body, name="ret_bwd_states", grid=(NB,),
        in_specs=[_tab_spec(H),
                  pl.BlockSpec((R, D // 2), lambda n: (NB - 1 - n, 0)),
                  pl.BlockSpec((R, D // 2), lambda n: (n, 0)),
                  pl.BlockSpec((R, D), lambda n: (NB - 1 - n, 0)),
                  pl.BlockSpec((R, D), lambda n: (n, 0))],
        out_specs=[pl.BlockSpec((RET_CPB, HP, 128, 128), lambda n: (NB - 1 - n, 0, 0, 0)),
                   pl.BlockSpec((RET_CPB, HP, 128, 128), lambda n: (n, 0, 0, 0)),
                   pl.BlockSpec((2, HP, 128, 128), lambda n: (0, 0, 0, 0))],
        out_shape=[st, st, jax.ShapeDtypeStruct((2, HP, 128, 128), F32)],
        scratch_shapes=[pltpu.VMEM((HP, 128, 128), F32), pltpu.VMEM((HP, 128, 128), F32)],
        compiler_params=_cparams(("arbitrary",)),
    )(tab, qr, qr, do, do)


def _ret_bwd_main(qr, kr, p, do, sf_prev, sb_prev, dsf, dsb, c2, s2, tab, D, exchange=None):
    L = qr.shape[0]
    H = D // DV
    N = L // CHUNK
    HP = H // 2
    W = D // 2

    def body(tab_ref, q_ref, k_ref, v_ref, do_ref, sf_ref, sb_ref, dsf_ref, dsb_ref, c_ref, s_ref,
             dr_ref, st_ref, dl_acc):
        @pl.when(pl.program_id(0) == 0)
        def _():
            dl_acc[...] = jnp.zeros_like(dl_acc)

        i = lax.broadcasted_iota(jnp.int32, (CHUNK, 128), 0).astype(F32)
        rowid = lax.broadcasted_iota(jnp.int32, (128, 128), 0)

        def chunk(cc, carry):
            rows = pl.ds(pl.multiple_of(cc * CHUNK, CHUNK), CHUNK)
            c, s = c_ref[rows, :], s_ref[rows, :]
            for pr in range(HP):
                ps = slice(pr * 128, (pr + 1) * 128)
                q2, k2 = q_ref[rows, ps], k_ref[rows, ps]
                sf32, sb32 = sf_ref[cc, pr], sb_ref[cc, pr]
                dsf32, dsb32 = dsf_ref[cc, pr], dsb_ref[cc, pr]
                sfp, sbp = sf32.astype(BF16), sb32.astype(BF16)
                dsfp, dsbp = dsf32.astype(BF16), dsb32.astype(BF16)
                dq2 = jnp.zeros((CHUNK, 128), F32)
                dk2 = jnp.zeros((CHUNK, 128), F32)
                for e in range(2):
                    h = 2 * pr + e
                    sl = slice(h * DV, (h + 1) * DV)
                    hm = _head_lane_mask(q2.shape, e)
                    qm = jnp.where(hm, q2, jnp.zeros_like(q2))
                    km = jnp.where(hm, k2, jnp.zeros_like(k2))
                    qf, kf = qm.astype(F32), km.astype(F32)
                    v, do = v_ref[rows, sl], do_ref[rows, sl]
                    vf, dof = v.astype(F32), do.astype(F32)
                    m_t = tab_ref[h, T_MT]
                    sc = _dot_nt(qm, k2)
                    dpm = _dot_nt(do, v)
                    dsc = (dpm * tab_ref[h, T_M]).astype(BF16)
                    a_t = (_dot_nt(km, q2) * m_t).astype(BF16)
                    dsc_t = (_dot_nt(v, do) * m_t).astype(BF16)
                    dq_f, dq_b = tab_ref[h, T_QF], tab_ref[h, T_QB]
                    dk_f, dk_b = tab_ref[h, T_KF], tab_ref[h, T_KB]
                    dq = _dot(dsc, km)
                    dq += jnp.where(hm, dq_f * _dot_nt(do, sfp) + dq_b * _dot_nt(do, sbp), 0.0)
                    dk = _dot(dsc_t, qm)
                    dk += jnp.where(hm, dk_f * _dot_nt(v, dsfp) + dk_b * _dot_nt(v, dsbp), 0.0)
                    kdf = _dot((kf * dk_f).astype(BF16), dsfp)
                    kdb = _dot((kf * dk_b).astype(BF16), dsbp)
                    dr_ref[rows, D + h * DV:D + (h + 1) * DV] = (_dot(a_t, do) + kdf + kdb).astype(BF16)
                    dq2 += dq
                    dk2 += dk
                    xf = _dot((qf * dq_f).astype(BF16), sfp)
                    xb = _dot((qf * dq_b).astype(BF16), sbp)
                    pair = (rowid < DK) if e == 0 else (rowid >= DK)
                    gcf, gcb = tab_ref[h, T_QF, CHUNK - 1:CHUNK, 0:1], tab_ref[h, T_QB, 0:1, 0:1]
                    scdp = sc * dpm
                    dl_acc[h, 0] += scdp * tab_ref[h, T_MF1] + xf * dof * (i + 1.0) \
                        + kdf * vf * (CHUNK - 1.0 - i) + (CHUNK * gcf) * jnp.where(pair, dsf32 * sf32, 0.0)
                    dl_acc[h, 1] += scdp * tab_ref[h, T_MB1] + xb * dof * (CHUNK - i) \
                        + kdb * vf * i + (CHUNK * gcb) * jnp.where(pair, dsb32 * sb32, 0.0)
                dr_ref[rows, ps] = (dq2 * c - _swap_halves(dq2) * s).astype(BF16)
                dr_ref[rows, W + pr * 128:W + (pr + 1) * 128] = \
                    ((dk2 * c - _swap_halves(dk2) * s) * K_SCALE).astype(BF16)
            return carry

        lax.fori_loop(0, RET_CPB, chunk, 0)

        @pl.when(pl.program_id(0) == N // RET_CPB - 1)
        def _():
            lane = lax.broadcasted_iota(jnp.int32, (1, 128), 1)
            acc = [jnp.zeros((1, 128), F32), jnp.zeros((1, 128), F32)]
            for h in range(H):
                for b in range(2):
                    acc[b] += jnp.where(lane == h, _sum_all(dl_acc[h, b]), 0.0)
            st_ref[...] = jnp.zeros_like(st_ref)
            st_ref[0:1, :] = acc[0]
            st_ref[1:2, :] = acc[1]

    R = RET_CPB * CHUNK
    st_spec = pl.BlockSpec((RET_CPB, HP, 128, 128), lambda n: (n, 0, 0, 0))
    half = pl.BlockSpec((R, W), lambda n: (n, 0))
    rope = pl.BlockSpec((R, 128), lambda n: (n, 0))
    return _riding_call(
        body, exchange, N // RET_CPB, name="ret_bwd_main",
        args=(tab, qr, kr, p, do, sf_prev, sb_prev, dsf, dsb, c2, s2),
        in_specs=[_tab_spec(H), half, half,
                  pl.BlockSpec((R, D), lambda n: (n, 5)),
                  pl.BlockSpec((R, D), lambda n: (n, 0)),
                  st_spec, st_spec, st_spec, st_spec, rope, rope],
        out_specs=[pl.BlockSpec((R, 2 * D), lambda n: (n, 0)),
                   pl.BlockSpec((8, 128), lambda n: (0, 0))],
        out_shape=[jax.ShapeDtypeStruct((L, 2 * D), BF16), jax.ShapeDtypeStruct((8, 128), F32)],
        scratch=[pltpu.VMEM((H, 2, CHUNK, 128), F32)],
        cparams=_cparams(("arbitrary",)))


def _ctx_bwd(pc, pqk_c, ds0, lg, D):
    Lc = pc.shape[0]
    H = D // DV
    HP = H // 2
    W = D // 2

    def body(lg_ref, k_ref, v_ref, ds_ref, dr_ref, st_ref):
        dqk_ref = dr_ref.at[:, 0:D]
        dv_ref = dr_ref.at[:, D:2 * D]
        m = lax.broadcasted_iota(jnp.int32, (Lc, 128), 0).astype(F32)
        lane = lax.broadcasted_iota(jnp.int32, (1, 128), 1)
        acc_f = jnp.zeros((1, 128), F32)
        acc_b = jnp.zeros((1, 128), F32)
        dqk_ref[:, 0:W] = jnp.zeros((Lc, W), BF16)
        for pr in range(HP):
            ps = slice(pr * 128, (pr + 1) * 128)
            k2 = k_ref[:, ps].astype(F32) * K_SCALE
            dsfp, dsbp = ds_ref[0, pr].astype(BF16), ds_ref[1, pr].astype(BF16)
            dk2 = jnp.zeros((Lc, 128), F32)
            for e in range(2):
                h = 2 * pr + e
                sl = slice(h * DV, (h + 1) * DV)
                hm = _head_lane_mask(k2.shape, e)
                km = jnp.where(hm, k2, 0.0)
                v = v_ref[:, sl]
                vf = v.astype(F32)
                dec_f = jnp.exp(lg_ref[0, h] * (Lc - 1.0 - m))
                dec_b = jnp.exp(lg_ref[1, h] * m)
                kdf = _dot((km * dec_f).astype(BF16), dsfp)
                kdb = _dot((km * dec_b).astype(BF16), dsbp)
                dv_ref[:, sl] = (kdf + kdb).astype(BF16)
                dk2 += jnp.where(hm, dec_f * _dot_nt(v, dsfp) + dec_b * _dot_nt(v, dsbp), 0.0)
                acc_f += jnp.where(lane == h, _sum_all(kdf * vf * (Lc - 1.0 - m)), 0.0)
                acc_b += jnp.where(lane == h, _sum_all(kdb * vf * m), 0.0)
            dqk_ref[:, W + pr * 128:W + (pr + 1) * 128] = (dk2 * K_SCALE).astype(BF16)
        st_ref[...] = jnp.zeros_like(st_ref)
        st_ref[0:1, :] = acc_f
        st_ref[1:2, :] = acc_b

    return pl.pallas_call(
        body, name="ctx_bwd", grid=(1,),
        in_specs=[_smem_spec(), pl.BlockSpec((Lc, W), lambda i: (0, 1)), pl.BlockSpec((Lc, D), lambda i: (0, 1)),
                  pl.BlockSpec((2, HP, 128, 128), lambda i: (0, 0, 0, 0))],
        out_specs=[pl.BlockSpec((Lc, 2 * D), lambda i: (0, 0)), pl.BlockSpec((8, 128), lambda i: (0, 0))],
        out_shape=[jax.ShapeDtypeStruct((Lc, 2 * D), BF16), jax.ShapeDtypeStruct((8, 128), F32)],
    )(lg, pqk_c, pc, ds0)


class _Exchange(NamedTuple):
    inputs: tuple
    out_shapes: tuple
    n_copies: int
    build: Callable


def _exchange_parts(exchange):
    if exchange is None:
        return [], [], [], [], []
    n = exchange.n_copies
    return (list(exchange.inputs), [ANY] * len(exchange.inputs), list(exchange.out_shapes),
            [ANY] * len(exchange.out_shapes), [pltpu.SemaphoreType.DMA((n,)), pltpu.SemaphoreType.DMA((n,))])


def _riding_call(body, exchange, n_steps, *, args, in_specs, out_specs, out_shape, name, cparams, scratch=()):
    ex_args, ex_in_specs, ex_shapes, ex_out_specs, ex_scratch = _exchange_parts(exchange)
    n_in, n_out, n_sc = len(args), len(out_shape), len(scratch)

    def riding(*refs):
        k = n_in + len(ex_args)
        ins, ex_in = refs[:n_in], refs[n_in:k]
        outs, ex_out = refs[k:k + n_out], refs[k + n_out:k + n_out + len(ex_shapes)]
        k += n_out + len(ex_shapes)
        own_scratch, ex_sems = refs[k:k + n_sc], refs[k + n_sc:]
        step = pl.program_id(0)
        if exchange is not None:
            @pl.when(step == 0)
            def _():
                for rc in exchange.build(ex_in, ex_out, *ex_sems):
                    rc.start()
        body(*ins, *outs, *own_scratch)
        if exchange is not None:
            @pl.when(step == n_steps - 1)
            def _():
                for rc in exchange.build(ex_in, ex_out, *ex_sems):
                    rc.wait()

    return tuple(pl.pallas_call(
        riding, name=name, grid=(n_steps,),
        in_specs=list(in_specs) + ex_in_specs, out_specs=list(out_specs) + ex_out_specs,
        out_shape=list(out_shape) + ex_shapes, scratch_shapes=list(scratch) + ex_scratch,
        compiler_params=cparams,
    )(*args, *ex_args))


def _dxm(groups, col0, w, x, nw, sc, dx1, name, exchange=None):
    L, D = x.shape
    tm = min(256, L)
    nt = L // tm
    ng = len(groups)
    widths = [g.shape[1] for g in groups]
    wtot = sum(widths)
    with_dx = dx1 is not None
    ex_args, ex_in_specs, ex_shapes, ex_out_specs, ex_scratch = _exchange_parts(exchange)
    n_in = ng + 4 + (1 if with_dx else 0)
    n_out = 2 if with_dx else 1

    def body(*refs):
        group_refs = refs[:ng]
        w_hbm, x_ref, nw_ref, sc_ref = refs[ng:ng + 4]
        ex_in = refs[n_in:n_in + len(ex_args)]
        outs = refs[n_in + len(ex_args):]
        if with_dx:
            dx1_ref, gx_ref, st_ref = refs[ng + 4], outs[0], outs[1]
        else:
            st_ref = outs[0]
        ex_out = outs[n_out:n_out + len(ex_shapes)]
        w_vm, sem = outs[n_out + len(ex_shapes):n_out + len(ex_shapes) + 2]
        ex_sems = outs[n_out + len(ex_shapes) + 2:]
        i = pl.program_id(0)

        @pl.when(i == 0)
        def _():
            cp = pltpu.make_async_copy(w_hbm.at[:, col0 * D:col0 * D + wtot], w_vm, sem)
            cp.start()
            if exchange is not None:
                for rc in exchange.build(ex_in, ex_out, *ex_sems):
                    rc.start()
            st_ref[...] = jnp.zeros_like(st_ref)
            cp.wait()

        dxm, off = None, 0
        for g_ref, wd in zip(group_refs, widths):
            part = _dot_nt(g_ref[...], w_vm[:, off:off + wd])
            dxm = part if dxm is None else dxm + part
            off += wd

        xv = x_ref[...]
        r = lax.rsqrt(jnp.mean(xv * xv, axis=-1, keepdims=True) + EPS)
        xh = xv * r
        nwv = nw_ref[...]
        dxn = dxm * (1.0 + sc_ref[...])
        st_ref[0:1, :] += jnp.sum(dxm, axis=0, keepdims=True)
        st_ref[1:2, :] += jnp.sum(dxm * (xh * nwv), axis=0, keepdims=True)
        st_ref[2:3, :] += jnp.sum(dxn * xh, axis=0, keepdims=True)
        if with_dx:
            dxh = dxn * nwv
            gx_ref[...] = dx1_ref[...] + r * (dxh - xh * jnp.mean(dxh * xh, axis=-1, keepdims=True))

        if exchange is not None:
            @pl.when(i == nt - 1)
            def _():
                for rc in exchange.build(ex_in, ex_out, *ex_sems):
                    rc.wait()

    row = pl.BlockSpec((tm, D), lambda i: (i, 0))
    in_specs = [pl.BlockSpec((tm, wd), lambda i: (i, 0)) for wd in widths] + [ANY, row, _vec_spec(D), _vec_spec(D)]
    out_specs = [pl.BlockSpec((8, D), lambda i: (0, 0))]
    out_shape = [jax.ShapeDtypeStruct((8, D), F32)]
    args = list(groups) + [w, x, nw, sc]
    if with_dx:
        in_specs.append(row)
        out_specs.insert(0, row)
        out_shape.insert(0, jax.ShapeDtypeStruct((L, D), F32))
        args.append(dx1)
    res = pl.pallas_call(
        body, name=name, grid=(nt,),
        in_specs=in_specs + ex_in_specs, out_specs=out_specs + ex_out_specs, out_shape=out_shape + ex_shapes,
        scratch_shapes=[pltpu.VMEM((D, wtot), BF16), pltpu.SemaphoreType.DMA] + ex_scratch,
        compiler_params=_cparams(("arbitrary",), VMEM_LIMIT),
    )(*args, *ex_args)
    gx = res[0] if with_dx else None
    return (gx, res[n_out - 1], *res[n_out:])


DW_TN = 256
DW_RING = 4


def _dw_in(xmt, groups, cmt, dr_c, D, pair):
    L = xmt.shape[1]
    Lc = cmt.shape[1]
    Dh = D // 2
    tn = min(DW_TN, D)
    nblk = [g.shape[1] // tn for g in groups]
    starts = [sum(nblk[:g]) for g in range(len(groups))]
    ng = len(groups)
    nj = sum(nblk)
    rows_out = Dh if pair else D

    def body(*refs):
        xt_hbm = refs[0]
        group_refs = refs[1:1 + ng]
        ct_hbm, drc_ref, o_ref = refs[1 + ng:4 + ng]
        rest = refs[4 + ng:]
        if pair:
            ra_hbm, xt_vm, ct_vm, loc, ring, s_send, s_recv = rest
            pos = _position()
            sib = _peer(pos, 1)
        else:
            xt_vm, ct_vm, loc = rest
        j = pl.program_id(0)

        @pl.when(j == 0)
        def _():
            if pair:
                c = pos[2]
                other = pl.ds(pl.multiple_of((1 - c) * Dh, Dh), Dh)
                mine = pl.ds(pl.multiple_of(c * Dh, Dh), Dh)
                cps = [pltpu.make_async_copy(xt_hbm.at[other, :], xt_vm.at[0:Dh, :], loc.at[0]),
                       pltpu.make_async_copy(xt_hbm.at[mine, :], xt_vm.at[Dh:D, :], loc.at[1]),
                       pltpu.make_async_copy(ct_hbm.at[other, :], ct_vm.at[0:Dh, :], loc.at[2]),
                       pltpu.make_async_copy(ct_hbm.at[mine, :], ct_vm.at[Dh:D, :], loc.at[3])]
            else:
                cps = [pltpu.make_async_copy(xt_hbm, xt_vm, loc.at[0]), pltpu.make_async_copy(ct_hbm, ct_vm, loc.at[1])]
            for cp in cps:
                cp.start()
            for cp in cps:
                cp.wait()

        def send(slot):
            cols = pl.ds(pl.multiple_of(j * tn, 128), tn)
            return pltpu.make_async_remote_copy(src_ref=ring.at[slot], dst_ref=ra_hbm.at[:, cols],
                                                send_sem=s_send.at[slot], recv_sem=s_recv,
                                                device_id=sib, device_id_type=MESH)

        for g in range(ng):
            @pl.when((j >= starts[g]) & (j < starts[g] + nblk[g]))
            def _(g=g):
                acc = _dot(xt_vm[...], group_refs[g][...])
                if g == 1:
                    acc += _dot(ct_vm[...], drc_ref[...])
                if not pair:
                    o_ref[...] = acc
                    return
                o_ref[...] = acc[Dh:, :]
                slot = lax.rem(j, DW_RING)

                @pl.when(j >= DW_RING)
                def _():
                    send(slot).wait_send()

                ring[slot] = acc[0:Dh, :]
                send(slot).start()

        if pair:
            @pl.when(j == nj - 1)
            def _():
                pltpu.make_async_remote_copy(src_ref=ra_hbm, dst_ref=ra_hbm, send_sem=s_send.at[0], recv_sem=s_recv,
                                             device_id=sib, device_id_type=MESH).wait_recv()
                for slot in range(DW_RING):
                    send(slot).wait_send()

    def group_spec(g, rows):
        return pl.BlockSpec((rows, tn), lambda j: (0, jnp.clip(j - starts[g], 0, nblk[g] - 1)))

    out_specs = [pl.BlockSpec((rows_out, tn), lambda j: (0, j))]
    out_shape = [jax.ShapeDtypeStruct((rows_out, nj * tn), F32)]
    scratch = [pltpu.VMEM((D, L), BF16), pltpu.VMEM((D, Lc), BF16), pltpu.SemaphoreType.DMA((4,))]
    if pair:
        out_specs.append(ANY)
        out_shape.append(jax.ShapeDtypeStruct((Dh, nj * tn), F32))
        scratch += [pltpu.VMEM((DW_RING, Dh, tn), F32), pltpu.SemaphoreType.DMA((DW_RING,)), pltpu.SemaphoreType.DMA]
    return tuple(pl.pallas_call(
        body, name="dw_in", grid=(nj,),
        in_specs=[ANY] + [group_spec(g, L) for g in range(ng)] + [ANY, group_spec(1, Lc)],
        out_specs=out_specs, out_shape=out_shape, scratch_shapes=scratch,
        compiler_params=_cparams(("arbitrary",), VMEM_LIMIT),
    )(xmt, *groups, cmt, dr_c))


def _local_step(x, ctx, tgt, mod_x, mod_c, norm_w, conv_w8, conv_b, lg, gn_w, fw, project, csidx=None):
    L, D = x.shape
    sh_x, sc_x, g_x = mod_x[0:1], mod_x[1:2], mod_x[2:3]
    sh_c, sc_c = mod_c[0:1], mod_c[1:2]
    c2, s2 = _rope_tables(L)
    tab = _decay_tables(lg, D // DV)

    xm, xmt = _norm_mod(x, norm_w, sc_x, sh_x, "norm_mod_x")
    cm, cmt = _norm_mod(ctx, norm_w, sc_c, sh_c, "norm_mod_ctx")
    p, qr, kr, w_in, w3 = project(xm, c2, s2)
    pc, pqk_c = _in_proj(cm, w_in, "in_proj_ctx", QK_BLOCK, 2)
    ya = _conv_gate_fwd(p, conv_w8, conv_b, D)
    s0 = _ctx_states(pc, pqk_c, lg, D)
    sf_prev, sb_prev = _ret_states(kr, p, s0, tab, D)
    o, yb = _ret_out(qr, kr, p, sf_prev, sb_prev, gn_w, tab, D)
    dx1, dya, dyb, dgab, dw3, st_mid = _mid(ya, yb, p, x, tgt, w3, g_x, fw, D)
    reduce = csidx is not None
    dw3_5 = dw3.reshape(3, N_SHARD, 2, D // 8, D)
    dconv, st_conv, *ra_3 = _conv_bwd(dya, p, conv_w8, conv_b, D, _pair_exchange_w3(dw3_5) if reduce else None)
    do, dzb, st_gn = _ret_bwd_pre(dyb, p, o, gn_w, D)
    dsf, dsb, ds0 = _ret_bwd_states(qr, do, tab, D)
    cs_3 = _sum_pair_w3(csidx[0:1], dw3_5, ra_3[0]) if reduce else None
    dret, st_lg, *rb_3 = _ret_bwd_main(qr, kr, p, do, sf_prev, sb_prev, dsf, dsb, c2, s2, tab, D,
                                       _chips_exchange_w3(cs_3) if reduce else None)
    g_3 = _sum_chips_w3(csidx, cs_3, rb_3[0]) if reduce else dw3
    dret_c, st_lgc = _ctx_bwd(pc, pqk_c, ds0, lg, D)
    groups = (dconv, dret, dzb, dgab)
    _, st_c = _dxm((dret_c,), 4, w_in, ctx, norm_w, sc_c, None, "dxm_ctx")
    return groups, dret_c, xmt, cmt, dx1, sc_x, w_in, g_3, (st_mid, st_conv, st_gn, st_lg, st_lgc, st_c)


CHIP_FLIPS = (4, 2, 6)
ANY = pl.BlockSpec(memory_space=pl.ANY)
VMEM_FULL = pl.BlockSpec(memory_space=pltpu.VMEM)


def _position():
    return lax.axis_index("x"), lax.axis_index("y"), lax.axis_index("c")


def _peer(pos, k):
    x, y, c = pos
    return (1 - x if k & 4 else x, 1 - y if k & 2 else y, 1 - c if k & 1 else c)


def _dev_id(pos):
    return 4 * pos[0] + 2 * pos[1] + pos[2]


def _shard_of(pos):
    return 2 * pos[0] + pos[1]


def _remote(src, dst, send_sems, recv_sems, idx, to):
    return pltpu.make_async_remote_copy(src_ref=src, dst_ref=dst, send_sem=send_sems.at[idx],
                                        recv_sem=recv_sems.at[idx], device_id=to, device_id_type=MESH)


def _dot_f32(a, b):
    return jnp.dot(a, b, precision=lax.Precision.HIGHEST, preferred_element_type=F32)


def _silu(x):
    return x * _sigmoid(x)


def _fwd_small(c8, cctx8, ada_w, ada_b, conv_w8):
    D = c8.shape[1]
    Wm = ada_w.shape[1]
    Dq = conv_w8.shape[1]

    def body(c_ref, cc_ref, aw_ref, ab_ref, cw_ref, act_ref, mod_ref, cwf_ref,
             cbuf, pmine, pbuf, wbuf, s_c, r_c, s_p, r_p, s_w, r_w):
        pos = _position()
        me, s = _dev_id(pos), _shard_of(pos)
        cbuf[me] = c_ref[...]
        wbuf[s] = cw_ref[...]
        sends = [_remote(c_ref, cbuf.at[me], s_c, r_c, k - 1, _peer(pos, k)) for k in range(1, 8)]
        sends += [_remote(cw_ref, wbuf.at[s], s_w, r_w, j, _peer(pos, k)) for j, k in enumerate(CHIP_FLIPS)]
        for cp in sends:
            cp.start()
        for k in range(1, 8):
            _remote(c_ref, cbuf.at[_dev_id(_peer(pos, k))], s_c, r_c, k - 1, _peer(pos, k)).wait_recv()
        for d in range(N_DEV):
            act_ref[d:d + 1, :] = _silu(cbuf[d, 0:1, :])
        act_ref[8:9, :] = _silu(cc_ref[0:1, :])
        act_ref[9:16, :] = jnp.zeros((7, D), F32)
        part = _dot_f32(act_ref[...], aw_ref[...])
        pmine[...] = part
        pbuf[s] = part
        psend = [_remote(pmine, pbuf.at[s], s_p, r_p, j, _peer(pos, k)) for j, k in enumerate(CHIP_FLIPS)]
        for cp in psend:
            cp.start()
        for j, k in enumerate(CHIP_FLIPS):
            t = _shard_of(_peer(pos, k))
            _remote(pmine, pbuf.at[t], s_p, r_p, j, _peer(pos, k)).wait_recv()
            _remote(cw_ref, wbuf.at[t], s_w, r_w, j, _peer(pos, k)).wait_recv()
        for t in range(N_SHARD):
            mod_ref[:, t * Wm:(t + 1) * Wm] = pbuf[t] + ab_ref[:, t * Wm:(t + 1) * Wm]
            cwf_ref[:, t * Dq:(t + 1) * Dq] = wbuf[t]
        for cp in sends + psend:
            cp.wait_send()

    return pl.pallas_call(
        body, name="fwd_small",
        in_specs=[VMEM_FULL] * 5, out_specs=[VMEM_FULL] * 3,
        out_shape=[jax.ShapeDtypeStruct((16, D), F32), jax.ShapeDtypeStruct((16, 3 * D), F32),
                   jax.ShapeDtypeStruct((8, D), F32)],
        scratch_shapes=[pltpu.VMEM((N_DEV, 8, D), F32), pltpu.VMEM((16, Wm), F32),
                        pltpu.VMEM((N_SHARD, 16, Wm), F32), pltpu.VMEM((N_SHARD, 8, Dq), F32),
                        pltpu.SemaphoreType.DMA((7,)), pltpu.SemaphoreType.DMA((7,)),
                        pltpu.SemaphoreType.DMA((3,)), pltpu.SemaphoreType.DMA((3,)),
                        pltpu.SemaphoreType.DMA((3,)), pltpu.SemaphoreType.DMA((3,))],
        compiler_params=_cparams(None, VMEM_LIMIT),
    )(c8, cctx8, ada_w, ada_b, conv_w8)


AG_CHUNKS = 3


def _ag_in_proj(xm, w_in_s, w3_s, c2, s2):
    L, D = xm.shape
    Wc = w_in_s.shape[1]
    Wq = Wc // AG_CHUNKS
    Dh = D // 2
    Do = w3_s[0].shape[1]
    TM = min(1024, L // 4)
    NT = L // TM
    RC = min(128, Dh)
    NQ = AG_CHUNKS

    def body(xm_ref, wi_hbm, wa_ref, wb_ref, wo_ref, c_ref, s_ref, p_hbm, qr_hbm, kr_hbm, fi_hbm, f3_hbm,
             w_vm, cast_buf, s3, stage, qk_stage, ici_s, ici_r, d2d_s, d2d_r, w3_s_, w3_r_, loc, out_sem, qk_sem):
        pos = _position()
        c = pos[2]
        s = _shard_of(pos)
        sib = _peer(pos, 1)
        mine = pl.ds(pl.multiple_of(c * Dh, Dh), Dh)
        other = pl.ds(pl.multiple_of((1 - c) * Dh, Dh), Dh)

        def cast_half(hf):
            def step(i, carry):
                rows = pl.ds(pl.multiple_of(hf * Dh + i * RC, RC), RC)
                cp = pltpu.make_async_copy(wi_hbm.at[rows, :], cast_buf, loc.at[0])
                cp.start()
                cp.wait()
                for q in range(NQ):
                    w_vm[0, q, rows, :] = cast_buf[:, q * Wq:(q + 1) * Wq].astype(BF16)
                return carry
            lax.fori_loop(0, Dh // RC, step, 0)

        def abs_col(t, q):
            return pl.ds(pl.multiple_of(t * Wc + q * Wq, 128), Wq)

        cast_half(c)
        for a, w_ref in enumerate((wa_ref, wb_ref, wo_ref)):
            s3[a] = w_ref[...].astype(BF16)
        sends = []
        for q in range(NQ):
            for j, k in enumerate(CHIP_FLIPS):
                sends.append(_remote(w_vm.at[0, q, mine, :], w_vm.at[1 + j, q, mine, :], ici_s, ici_r,
                                     q * 3 + j, _peer(pos, k)))
        for j, k in enumerate(CHIP_FLIPS):
            sends.append(_remote(s3.at[:, c], f3_hbm.at[:, s, c], w3_s_, w3_r_, j, _peer(pos, k)))
        for cp in sends:
            cp.start()
        cast_half(1 - c)
        local = [pltpu.make_async_copy(s3, f3_hbm.at[:, s], loc.at[1])]
        local += [pltpu.make_async_copy(w_vm.at[0, q], fi_hbm.at[:, abs_col(s, q)], loc.at[2 + q]) for q in range(NQ)]
        for cp in local:
            cp.start()

        def out_copy(slot, rows, cols):
            return pltpu.make_async_copy(stage.at[slot], p_hbm.at[rows, cols], out_sem.at[slot])

        def block(r, q, t, first):
            cols = abs_col(t, q)

            def row_tile(rt, carry):
                rows = pl.ds(pl.multiple_of(rt * TM, TM), TM)
                acc = _dot(xm_ref[rows, :], w_vm[r, q])
                slot = lax.rem(rt, 2)

                @pl.when(rt >= 2 if first else rt >= 0)
                def _():
                    out_copy(slot, rows, cols).wait()

                stage[slot] = acc.astype(BF16)
                out_copy(slot, rows, cols).start()

                def rotary(lo, scale, dst_hbm):
                    c, s = c_ref[rows, :], s_ref[rows, :]
                    for pr in range(Dh // 128):
                        tq = acc[:, lo + pr * 128:lo + (pr + 1) * 128] * scale
                        qk_stage[:, pr * 128:(pr + 1) * 128] = (tq * c + _swap_halves(tq) * s).astype(BF16)
                    cp = pltpu.make_async_copy(qk_stage, dst_hbm.at[rows, :], qk_sem)
                    cp.start()
                    cp.wait()

                if q == NQ - 1:
                    @pl.when(t == 1)
                    def _():
                        rotary(Wq - Dh, 1.0, qr_hbm)
                if q == 0:
                    @pl.when(t == 2)
                    def _():
                        rotary(0, K_SCALE, kr_hbm)
                return carry

            lax.fori_loop(0, NT, row_tile, 0)

        for q in range(NQ):
            block(0, q, s, q == 0)
        passed = []
        for q in range(NQ):
            for j, k in enumerate(CHIP_FLIPS):
                r, idx = 1 + j, q * 3 + j
                t = _shard_of(_peer(pos, k))
                _remote(w_vm.at[r, q, mine, :], w_vm.at[r, q, mine, :], ici_s, ici_r, idx, sib).wait_recv()
                fwd = _remote(w_vm.at[r, q, mine, :], w_vm.at[r, q, mine, :], d2d_s, d2d_r, idx, sib)
                fwd.start()
                passed.append(fwd)
                _remote(w_vm.at[r, q, other, :], w_vm.at[r, q, other, :], d2d_s, d2d_r, idx, sib).wait_recv()
                block(r, q, t, False)
                cp = pltpu.make_async_copy(w_vm.at[r, q], fi_hbm.at[:, abs_col(t, q)], loc.at[2 + NQ + idx])
                cp.start()
                local.append(cp)
        for j, k in enumerate(CHIP_FLIPS):
            t = _shard_of(_peer(pos, k))
            _remote(s3.at[:, c], f3_hbm.at[:, t, c], w3_s_, w3_r_, j, sib).wait_recv()
            fwd = _remote(f3_hbm.at[:, t, c], f3_hbm.at[:, t, c], w3_s_, w3_r_, 3 + j, sib)
            fwd.start()
            passed.append(fwd)
        for j, k in enumerate(CHIP_FLIPS):
            t = _shard_of(_peer(pos, k))
            _remote(s3.at[:, c], f3_hbm.at[:, t, 1 - c], w3_s_, w3_r_, 3 + j, sib).wait_recv()
        for cp in sends + passed:
            cp.wait_send()
        for cp in local:
            cp.wait()
        for slot in range(2):
            out_copy(slot, pl.ds(0, TM), abs_col(s, 0)).wait()

    n_loc = 2 + NQ + 3 * NQ
    return pl.pallas_call(
        body, name="ag_in_proj",
        in_specs=[VMEM_FULL, ANY, VMEM_FULL, VMEM_FULL, VMEM_FULL, VMEM_FULL, VMEM_FULL], out_specs=[ANY] * 5,
        out_shape=[jax.ShapeDtypeStruct((L, N_SHARD * Wc), BF16),
                   jax.ShapeDtypeStruct((L, Dh), BF16), jax.ShapeDtypeStruct((L, Dh), BF16),
                   jax.ShapeDtypeStruct((D, N_SHARD * Wc), BF16), jax.ShapeDtypeStruct((3, N_SHARD, 2, Do, D), BF16)],
        scratch_shapes=[pltpu.VMEM((N_SHARD, NQ, D, Wq), BF16), pltpu.VMEM((RC, Wc), F32),
                        pltpu.VMEM((3, 2, Do, D), BF16), pltpu.VMEM((2, TM, Wq), BF16), pltpu.VMEM((TM, Dh), BF16),
                        pltpu.SemaphoreType.DMA((3 * NQ,)), pltpu.SemaphoreType.DMA((3 * NQ,)),
                        pltpu.SemaphoreType.DMA((3 * NQ,)), pltpu.SemaphoreType.DMA((3 * NQ,)),
                        pltpu.SemaphoreType.DMA((6,)), pltpu.SemaphoreType.DMA((6,)),
                        pltpu.SemaphoreType.DMA((n_loc,)), pltpu.SemaphoreType.DMA((2,)), pltpu.SemaphoreType.DMA],
        compiler_params=_cparams(None, VMEM_LIMIT),
    )(xm, w_in_s, *w3_s, c2, s2)


def _pair_exchange_w3(dw3):
    _, _, _, Do, D = dw3.shape

    def build(ins, outs, send, recv):
        pos = _position()
        return [_remote(ins[0].at[:, :, 1 - pos[2]], outs[0], send, recv, 0, _peer(pos, 1))]

    return _Exchange((dw3,), (jax.ShapeDtypeStruct((3, N_SHARD, Do, D), F32),), 1, build)


def _sum_pair_in(dw_mine, ri):
    Dh, Wf = dw_mine.shape
    Wc = Wf // N_SHARD
    tr = min(256, Dh)

    def body(a_ref, b_ref, o_ref):
        o_ref[...] = (a_ref[...] + b_ref[...]).astype(BF16)

    return pl.pallas_call(
        body, name="sum_pair_in", grid=(Dh // tr, N_SHARD),
        in_specs=[pl.BlockSpec((tr, Wc), lambda i, t: (i, t)), pl.BlockSpec((tr, Wc), lambda i, t: (i, t))],
        out_specs=pl.BlockSpec((None, tr, Wc), lambda i, t: (t, i, 0)),
        out_shape=jax.ShapeDtypeStruct((N_SHARD, Dh, Wc), BF16),
        compiler_params=_cparams(("parallel", "parallel")),
    )(dw_mine, ri)


def _sum_pair_w3(cidx, dw3, r3):
    _, _, _, Do, D = dw3.shape

    def body(c_ref, a_ref, b_ref, o_ref):
        o_ref[...] = (a_ref[...] + b_ref[...]).astype(BF16)

    return pl.pallas_call(
        body, name="sum_pair_w3",
        grid_spec=pltpu.PrefetchScalarGridSpec(
            num_scalar_prefetch=1, grid=(3,),
            in_specs=[pl.BlockSpec((None, N_SHARD, None, Do, D), lambda a, c: (a, 0, c[0], 0, 0)),
                      pl.BlockSpec((None, N_SHARD, Do, D), lambda a, c: (a, 0, 0, 0))],
            out_specs=pl.BlockSpec((None, N_SHARD, Do, D), lambda a, c: (a, 0, 0, 0))),
        out_shape=jax.ShapeDtypeStruct((3, N_SHARD, Do, D), BF16),
        compiler_params=_cparams(("parallel",)),
    )(cidx, dw3, r3)


def _chips_exchange_in(cs_in):
    _, Dh, Wc = cs_in.shape

    def build(ins, outs, send, recv):
        pos = _position()
        return [_remote(ins[0].at[_shard_of(_peer(pos, k))], outs[0].at[j], send, recv, j, _peer(pos, k))
                for j, k in enumerate(CHIP_FLIPS)]

    return _Exchange((cs_in,), (jax.ShapeDtypeStruct((3, Dh, Wc), BF16),), 3, build)


def _chips_exchange_w3(cs_3):
    _, _, Do, D = cs_3.shape

    def build(ins, outs, send, recv):
        pos = _position()
        return [_remote(ins[0].at[:, _shard_of(_peer(pos, k))], outs[0].at[j], send, recv, j, _peer(pos, k))
                for j, k in enumerate(CHIP_FLIPS)]

    return _Exchange((cs_3,), (jax.ShapeDtypeStruct((3, 3, Do, D), BF16),), 3, build)


def _sum_chips_in(csidx, cs_in, rb_in):
    _, Dh, Wc = cs_in.shape
    tr = min(256, Dh)

    def body(s_ref, a_ref, b_ref, o_ref):
        acc = a_ref[...].astype(F32)
        for j in range(3):
            acc = acc + b_ref[j].astype(F32)
        o_ref[...] = acc

    return pl.pallas_call(
        body, name="sum_chips_in",
        grid_spec=pltpu.PrefetchScalarGridSpec(
            num_scalar_prefetch=1, grid=(Dh // tr,),
            in_specs=[pl.BlockSpec((None, tr, Wc), lambda i, s: (s[1], i, 0)),
                      pl.BlockSpec((3, tr, Wc), lambda i, s: (0, i, 0))],
            out_specs=pl.BlockSpec((None, tr, Wc), lambda i, s: (s[0], i, 0))),
        out_shape=jax.ShapeDtypeStruct((2, Dh, Wc), F32),
        compiler_params=_cparams(("parallel",)),
    )(csidx, cs_in, rb_in)


def _sum_chips_w3(csidx, cs_3, rb_3):
    _, _, Do, D = cs_3.shape

    def body(s_ref, a_ref, b_ref, o_ref):
        acc = a_ref[...].astype(F32)
        for j in range(3):
            acc = acc + b_ref[j].astype(F32)
        o_ref[...] = acc

    return pl.pallas_call(
        body, name="sum_chips_w3",
        grid_spec=pltpu.PrefetchScalarGridSpec(
            num_scalar_prefetch=1, grid=(3,),
            in_specs=[pl.BlockSpec((None, None, Do, D), lambda a, s: (a, s[1], 0, 0)),
                      pl.BlockSpec((3, None, Do, D), lambda a, s: (0, a, 0, 0))],
            out_specs=pl.BlockSpec((None, None, Do, D), lambda a, s: (a, s[0], 0, 0))),
        out_shape=jax.ShapeDtypeStruct((3, 2, Do, D), F32),
        compiler_params=_cparams(("parallel",)),
    )(csidx, cs_3, rb_3)


def _rs_final(g_in, g_3):
    def body(hi_ref, h3_ref, gi_ref, g3_ref, send, recv):
        pos = _position()
        c = pos[2]
        sib = _peer(pos, 1)
        cps = [_remote(hi_ref.at[c], gi_ref.at[c], send, recv, 0, sib),
               _remote(h3_ref.at[:, c], g3_ref.at[:, c], send, recv, 1, sib)]
        for cp in cps:
            cp.start()
        _remote(hi_ref.at[1 - c], gi_ref.at[1 - c], send, recv, 0, sib).wait_recv()
        _remote(h3_ref.at[:, 1 - c], g3_ref.at[:, 1 - c], send, recv, 1, sib).wait_recv()
        for cp in cps:
            cp.wait_send()

    return pl.pallas_call(
        body, name="rs_final", in_specs=[ANY, ANY], out_specs=[ANY, ANY],
        out_shape=[jax.ShapeDtypeStruct(g_in.shape, F32), jax.ShapeDtypeStruct(g_3.shape, F32)],
        input_output_aliases={0: 0, 1: 1},
        scratch_shapes=[pltpu.SemaphoreType.DMA((2,)), pltpu.SemaphoreType.DMA((2,))],
    )(g_in, g_3)


def _adam_math(w, g, m, v):
    m = ADAM_B1 * m + (1.0 - ADAM_B1) * g
    v = ADAM_B2 * v + (1.0 - ADAM_B2) * (g * g)
    m_hat = m / (1.0 - ADAM_B1 ** ADAM_STEP)
    v_hat = v / (1.0 - ADAM_B2 ** ADAM_STEP)
    delta = -ADAM_LR * (m_hat / (jnp.sqrt(v_hat) + ADAM_EPS) + ADAM_WD * w)
    return delta, m, v


def _adamw(w, g, m, v, name):
    R, C = w.shape
    tr = min(128, R)

    def body(w_ref, g_ref, m_ref, v_ref, d_ref, nm_ref, nv_ref):
        d_ref[...], nm_ref[...], nv_ref[...] = _adam_math(w_ref[...], g_ref[...], m_ref[...], v_ref[...])

    blk = pl.BlockSpec((tr, C), lambda i: (i, 0))
    return pl.pallas_call(
        body, name=name, grid=(R // tr,), in_specs=[blk] * 4, out_specs=[blk] * 3,
        out_shape=[jax.ShapeDtypeStruct((R, C), F32)] * 3,
        compiler_params=_cparams(("parallel",), VMEM_LIMIT),
    )(w, g, m, v)


SMALL_ROWS = ("c_ctx", "norm_w", "conv_b", "gn_w", "final_norm_w")


def _bwd_small(stats, ada_w, Dq):
    D = stats[0].shape[1]
    Wm = ada_w.shape[1]

    def body(stx, stm, stc, stv, stg, stl, stlc, aw_ref, tot_ref, dm_sh, gcw, da_ref, loss_ref,
             vec_ref, vbuf, dm, amine, abuf, s_v, r_v, s_a, r_a):
        pos = _position()
        me, s = _dev_id(pos), _shard_of(pos)
        vec_ref[...] = jnp.zeros_like(vec_ref)
        vec_ref[0:2, :] = stx[0:2, :]
        vec_ref[2:3, :] = stm[1:2, :]
        vec_ref[3:5, :] = stc[0:2, :]
        vec_ref[5:6, :] = stx[2:3, :] + stc[2:3, :]
        vec_ref[6:7, :] = stv[3:4, :]
        vec_ref[7:8, :] = stg[0:1, :]
        vec_ref[8:9, :] = stm[0:1, :]
        vec_ref[9:12, :] = stv[0:3, :]
        vec_ref[12:14, 0:128] = stl[0:2, :] + stlc[0:2, :]
        vec_ref[14:15, :] = stm[2:3, :]
        vbuf[me] = vec_ref[...]
        sends = [_remote(vec_ref, vbuf.at[me], s_v, r_v, k - 1, _peer(pos, k)) for k in range(1, 8)]
        for cp in sends:
            cp.start()
        for k in range(1, 8):
            _remote(vec_ref, vbuf.at[_dev_id(_peer(pos, k))], s_v, r_v, k - 1, _peer(pos, k)).wait_recv()
        tot = vbuf[0]
        for d in range(1, N_DEV):
            tot = tot + vbuf[d]
        loss_ref[...] = jnp.zeros((8, 128), F32) + (0.5 / D) * _sum_all(tot[14:15, :])
        dm[...] = jnp.zeros_like(dm)
        for d in range(N_DEV):
            for r in range(3):
                dm[d:d + 1, r * D:(r + 1) * D] = vbuf[d, r:r + 1, :]
        dm[8:9, 0:D] = tot[3:4, :]
        dm[8:9, D:2 * D] = tot[4:5, :]
        for t in range(N_SHARD):
            @pl.when(s == t)
            def _(t=t):
                dm_sh[...] = dm[:, t * Wm:(t + 1) * Wm]
                gcw[...] = tot[9:12, t * Dq:(t + 1) * Dq]
        tot_ref[...] = tot
        part = lax.dot_general(dm_sh[8:16, :], aw_ref[...], (((1,), (1,)), ((), ())),
                               precision=lax.Precision.HIGHEST, preferred_element_type=F32)
        amine[...] = part
        abuf[s] = part
        asend = [_remote(amine, abuf.at[s], s_a, r_a, j, _peer(pos, k)) for j, k in enumerate(CHIP_FLIPS)]
        for cp in asend:
            cp.start()
        for j, k in enumerate(CHIP_FLIPS):
            _remote(amine, abuf.at[_shard_of(_peer(pos, k))], s_a, r_a, j, _peer(pos, k)).wait_recv()
        da = abuf[0]
        for t in range(1, N_SHARD):
            da = da + abuf[t]
        da_ref[...] = da
        for cp in sends + asend:
            cp.wait_send()

    row = lambda *shape: jax.ShapeDtypeStruct(shape, F32)
    return pl.pallas_call(
        body, name="bwd_small",
        in_specs=[VMEM_FULL] * 8, out_specs=[VMEM_FULL] * 5,
        out_shape=[row(16, D), row(16, Wm), row(3, Dq), row(8, D), row(8, 128)],
        scratch_shapes=[pltpu.VMEM((16, D), F32), pltpu.VMEM((N_DEV, 16, D), F32), pltpu.VMEM((16, 3 * D), F32),
                        pltpu.VMEM((8, D), F32), pltpu.VMEM((N_SHARD, 8, D), F32),
                        pltpu.SemaphoreType.DMA((7,)), pltpu.SemaphoreType.DMA((7,)),
                        pltpu.SemaphoreType.DMA((3,)), pltpu.SemaphoreType.DMA((3,))],
        compiler_params=_cparams(None, VMEM_LIMIT),
    )(*stats, ada_w)


def _small_update(tot, dm_sh, gcw, da, act, p_row, p_ab, p_cw, p_dl):
    D = act.shape[1]
    Wm = dm_sh.shape[1]
    Dq = gcw.shape[1]

    def body(tot_ref, dm_ref, gcw_ref, da_ref, act_ref, prow, pab, pcw, pdl, gaw_ref, *outs):
        o_q = [outs[8 * q:8 * (q + 1)] for q in range(4)]
        tot = tot_ref[...]
        gaw_ref[...] = lax.dot_general(act_ref[...], dm_ref[...], (((0,), (0,)), ((), ())),
                                       precision=lax.Precision.HIGHEST, preferred_element_type=F32)
        cc = prow[0, 0:1, :]
        sg = _sigmoid(cc)
        g_cctx = da_ref[0:1, :] * (sg * (1.0 + cc * (1.0 - sg)))

        def place_all(o, val):
            o[...] = val

        def emit(k, w, g, m, v, place=place_all):
            for q, val in enumerate((g,) + _adam_math(w, g, m, v)):
                place(o_q[q][k], val)

        g_rows = [g_cctx, tot[5:6, :], tot[6:7, :], tot[7:8, :], tot[8:9, :]]
        for k, g in enumerate(g_rows):
            emit(k, prow[0, k:k + 1, :], g, prow[1, k:k + 1, :], prow[2, k:k + 1, :])

        def place_ab(o, val):
            for r in range(3):
                o[0:1, r * D:(r + 1) * D] = val[r:r + 1, :]

        g_ab = jnp.concatenate([tot[0:1, :] + tot[3:4, :], tot[1:2, :] + tot[4:5, :], tot[2:3, :]], axis=0)
        emit(5, pab[0], g_ab, pab[1], pab[2], place_ab)
        emit(6, pcw[0], gcw_ref[...], pcw[1], pcw[2])
        g_dl = jnp.concatenate([tot[12:14, 0:128] * _sigmoid(-pdl[0, 0:2, :]), jnp.zeros((6, 128), F32)], axis=0)
        emit(7, pdl[0], g_dl, pdl[1], pdl[2])

    row = lambda *shape: jax.ShapeDtypeStruct(shape, F32)
    per_q = [row(1, D)] * 5 + [row(1, 3 * D), row(3, Dq), row(8, 128)]
    res = pl.pallas_call(
        body, name="small_update",
        in_specs=[VMEM_FULL] * 9, out_specs=[VMEM_FULL] * 33,
        out_shape=[row(D, Wm)] + per_q * 4,
        compiler_params=_cparams(None, VMEM_LIMIT),
    )(tot, dm_sh, gcw, da, act, p_row, p_ab, p_cw, p_dl)
    return res[0], [res[1 + 8 * q:1 + 8 * (q + 1)] for q in range(4)]


def _pad_rows(a, rows=8):
    return jnp.pad(a, ((0, rows - a.shape[0]), (0, 0)))


def kernel(x, c, ctx, c_ctx, norm_w, ada_w, ada_b, w_in, conv_w, conv_b, decay_logit, gn_w, w_a, w_b, w_out, final_norm_w, loss_target, m_c_ctx, m_norm_w, m_ada_w, m_ada_b, m_w_in, m_conv_w, m_conv_b, m_decay_logit, m_gn_w, m_w_a, m_w_b, m_w_out, m_final_norm_w, v_c_ctx, v_norm_w, v_ada_w, v_ada_b, v_w_in, v_conv_w, v_conv_b, v_decay_logit, v_gn_w, v_w_a, v_w_b, v_w_out, v_final_norm_w):
    L, D = x.shape[1], x.shape[2]
    H = D // DV
    Wc = w_in.shape[2]
    Do = D // 8
    pos = _position()
    me = _dev_id(pos)
    cidx = jnp.reshape(pos[2], (1,)).astype(jnp.int32)
    sidx = jnp.reshape(_shard_of(pos), (1,)).astype(jnp.int32)

    act, mod, conv_w8 = _fwd_small(_pad_rows(c), _pad_rows(c_ctx[None]), ada_w[0], ada_b, _pad_rows(conv_w[0]))
    mod_x = lax.dynamic_slice_in_dim(mod, me, 1, axis=0).reshape(3, D)
    mod_c = mod[8].reshape(3, D)
    lg = jax.nn.log_sigmoid(decay_logit[0])

    w3_s = tuple(w[0].reshape(2, Do, D) for w in (w_a, w_b, w_out))

    def project(xm, c2, s2):
        p, qr, kr, w_in_full, w3_full = _ag_in_proj(xm, w_in[0], w3_s, c2, s2)
        return p, qr, kr, w_in_full, w3_full.reshape(3, D, D)

    csidx = jnp.concatenate([cidx, sidx])
    groups, dret_c, xmt, cmt, dx1, sc_x, w_in_full, gh_3, sts = _local_step(
        x[0], ctx[0], loss_target[0], mod_x, mod_c, norm_w, conv_w8, conv_b, lg, gn_w, final_norm_w[None],
        project, csidx)
    st_mid, st_conv, st_gn, st_lg, st_lgc, st_c = sts

    dw_mine, ra_in = _dw_in(xmt, groups, cmt, dret_c, D, True)
    cs_in = _sum_pair_in(dw_mine, ra_in)
    grad_x, st_x, rb_in = _dxm(groups, 0, w_in_full, x[0], norm_w, sc_x, dx1, "dxm_x", _chips_exchange_in(cs_in))
    gh_in = _sum_chips_in(csidx, cs_in, rb_in)
    g_in, g_3 = _rs_final(gh_in, gh_3)
    g_w_in = g_in.reshape(D, Wc)
    g_3 = g_3.reshape(3, D // 4, D)

    zeros3 = jnp.zeros((3, D), F32)
    p_row = jnp.concatenate(
        [r for t in ((c_ctx[None], norm_w, conv_b, gn_w, final_norm_w[None], zeros3),
                     (m_c_ctx[None], m_norm_w, m_conv_b, m_gn_w, m_final_norm_w[None], zeros3),
                     (v_c_ctx[None], v_norm_w, v_conv_b, v_gn_w, v_final_norm_w[None], zeros3)) for r in t],
        axis=0).reshape(3, 8, D)
    p_ab = jnp.concatenate([ada_b, m_ada_b, v_ada_b], axis=0).reshape(3, 3, D)
    p_cw = jnp.concatenate([conv_w, m_conv_w, v_conv_w], axis=0)
    p_dl = jnp.pad(jnp.concatenate([decay_logit, m_decay_logit, v_decay_logit], axis=0), ((0, 0), (0, 6), (0, 128 - H)))
    tot, dm_sh, gcw, da, loss_t = _bwd_small((st_x, st_mid, st_c, st_conv, st_gn, st_lg, st_lgc), ada_w[0],
                                             conv_w.shape[2])
    g_ada_w, small = _small_update(tot, dm_sh, gcw, da, act, p_row, p_ab, p_cw, p_dl)

    upd_in = _adamw(w_in[0], g_w_in, m_w_in[0], v_w_in[0], "adamw_w_in")
    upd_ada = _adamw(ada_w[0], g_ada_w, m_ada_w[0], v_ada_w[0], "adamw_ada_w")
    upd_a = _adamw(w_a[0], g_3[0], m_w_a[0], v_w_a[0], "adamw_w_a")
    upd_b = _adamw(w_b[0], g_3[1], m_w_b[0], v_w_b[0], "adamw_w_b")
    upd_o = _adamw(w_out[0], g_3[2], m_w_out[0], v_w_out[0], "adamw_w_out")

    def leaves(q):
        big = lambda g, upd: (g if q == 0 else upd[q - 1])[None]
        r_cctx, r_norm, r_convb, r_gn, r_fnorm, r_ab, r_cw, r_dl = small[q]
        return [r_cctx.reshape(D), r_norm, big(g_ada_w, upd_ada), r_ab, big(g_w_in, upd_in),
                r_cw[None], r_convb, r_dl[0:2, 0:H][None], r_gn,
                big(g_3[0], upd_a), big(g_3[1], upd_b), big(g_3[2], upd_o), r_fnorm.reshape(D)]

    loss = loss_t[0, 0]
    return (loss, grad_x[None], *leaves(0), *leaves(1), *leaves(2), *leaves(3))
```

```python
from typing import Callable, NamedTuple

import jax
import jax.numpy as jnp
from jax import lax
from jax.experimental import pallas as pl
from jax.experimental.pallas import tpu as pltpu

F32 = jnp.float32
BF16 = jnp.bfloat16
MESH = pl.DeviceIdType.MESH

CHUNK = 128
RET_CPB = 4
DV = 128
DK = 64
GRID_W = 64
ROPE_BASE = 10000.0
EPS = 1e-6
K_SCALE = DK ** -0.5
N_SHARD = 4
N_DEV = 8

ADAM_LR = 0.001
ADAM_B1 = 0.9
ADAM_B2 = 0.999
ADAM_EPS = 1e-08
ADAM_WD = 0.01
ADAM_STEP = 10

VMEM_LIMIT = 56 * 1024 * 1024


def _cparams(sem=None, vmem=None):
    kw = {}
    if sem is not None:
        kw["dimension_semantics"] = sem
    if vmem is not None:
        kw["vmem_limit_bytes"] = vmem
    return pltpu.CompilerParams(**kw)


def _dot(a, b):
    return jnp.dot(a, b, preferred_element_type=F32)


def _dot_nt(a, b):
    return lax.dot_general(a, b, (((1,), (1,)), ((), ())), preferred_element_type=F32)


def _dot_tn(a, b):
    return lax.dot_general(a, b, (((0,), (0,)), ((), ())), preferred_element_type=F32)


def _sigmoid(x):
    return 1.0 / (1.0 + jnp.exp(-x))


def _sum_all(x):
    return jnp.sum(jnp.sum(x, axis=1, keepdims=True), axis=0, keepdims=True)


def _swap_halves(t):
    n = t.shape[1]
    lane = lax.broadcasted_iota(jnp.int32, t.shape, 1)
    low = (lane & 32) == 0
    return jnp.where(low, pltpu.roll(t, n - 32, 1), pltpu.roll(t, 32, 1))


def _vec_spec(d):
    return pl.BlockSpec((1, d), lambda *a: (0, 0))


def _norm_mod(x, nw, sc, sh, name):
    L, D = x.shape
    tl = min(256, L)

    def body(x_ref, nw_ref, sc_ref, sh_ref, xm_ref, xmt_ref):
        xv = x_ref[...]
        r = lax.rsqrt(jnp.mean(xv * xv, axis=-1, keepdims=True) + EPS)
        xm = (xv * r * nw_ref[...]) * (1.0 + sc_ref[...]) + sh_ref[...]
        xm_ref[...] = xm.astype(BF16)
        xmt_ref[...] = xm.T.astype(BF16)

    return pl.pallas_call(
        body, name=name, grid=(L // tl,),
        in_specs=[pl.BlockSpec((tl, D), lambda i: (i, 0)), _vec_spec(D), _vec_spec(D), _vec_spec(D)],
        out_specs=[pl.BlockSpec((tl, D), lambda i: (i, 0)), pl.BlockSpec((D, tl), lambda i: (0, i))],
        out_shape=[jax.ShapeDtypeStruct((L, D), BF16), jax.ShapeDtypeStruct((D, L), BF16)],
        compiler_params=_cparams(("parallel",)),
    )(x, nw, sc, sh)


QK_BLOCK, V_BLOCK = 4, 5


def _in_proj(xm, w, name, first=0, count=None):
    M, D = xm.shape
    count = w.shape[1] // D if count is None else count
    tm = min(1024, M)

    def body(a_ref, b_ref, o_ref, qk_ref):
        acc = _dot(a_ref[...], b_ref[...])
        o_ref[...] = acc.astype(o_ref.dtype)

        @pl.when(pl.program_id(1) == QK_BLOCK - first)
        def _():
            qk_ref[...] = acc

    return pl.pallas_call(
        body, name=name, grid=(M // tm, count),
        in_specs=[pl.BlockSpec((tm, D), lambda i, j: (i, 0)), pl.BlockSpec((D, D), lambda i, j: (0, first + j))],
        out_specs=[pl.BlockSpec((tm, D), lambda i, j: (i, j)), pl.BlockSpec((tm, D), lambda i, j: (i, 0))],
        out_shape=[jax.ShapeDtypeStruct((M, count * D), BF16), jax.ShapeDtypeStruct((M, D), F32)],
        compiler_params=_cparams(("parallel", "arbitrary")),
    )(xm, w)


def _halo_specs(tl, L, D, col):
    hb = tl // 16
    last = L // 16 - 1
    prev = pl.BlockSpec((16, D), lambda i: (jnp.maximum(i * hb - 1, 0), col))
    nxt = pl.BlockSpec((16, D), lambda i: (jnp.minimum((i + 1) * hb, last), col))
    return prev, nxt


def _shift_rows(u, above, below):
    tl = u.shape[0]
    row = lax.broadcasted_iota(jnp.int32, u.shape, 0)
    dn = jnp.where(row == 0, above, pltpu.roll(u, 1, 0))
    up = jnp.where(row == tl - 1, below, pltpu.roll(u, tl - 1, 0))
    return dn, up


def _rope_tables(L):
    pos = jnp.arange(L)
    row = (pos // GRID_W).astype(F32)
    col = (pos % GRID_W).astype(F32)
    nf = DK // 4
    inv = ROPE_BASE ** (-jnp.arange(nf, dtype=F32) / nf)
    ang = jnp.concatenate([row[:, None] * inv, col[:, None] * inv], axis=-1)
    cos, sin = jnp.cos(ang), jnp.sin(ang)
    return jnp.concatenate([cos, cos, cos, cos], axis=-1), jnp.concatenate([-sin, sin, -sin, sin], axis=-1)


def _smem_spec():
    return pl.BlockSpec(memory_space=pltpu.SMEM)


def _pair_select(e0, e1):
    row = lax.broadcasted_iota(jnp.int32, e0.shape, 0)
    return jnp.where(row < DK, e0, e1)


def _head_lane_mask(shape, e):
    lane = lax.broadcasted_iota(jnp.int32, shape, 1)
    return (lane < DK) if e == 0 else (lane >= DK)


def _ctx_states(pc, pqk_c, lg, D):
    Lc = pc.shape[0]
    H = D // DV

    def body(lg_ref, k_ref, v_ref, s_ref):
        m = lax.broadcasted_iota(jnp.int32, (Lc, DV), 0).astype(F32)
        for pr in range(H // 2):
            k2 = k_ref[:, pr * 128:(pr + 1) * 128].astype(F32) * K_SCALE
            res = [[None, None], [None, None]]
            for e in range(2):
                h = 2 * pr + e
                v = v_ref[:, h * DV:(h + 1) * DV]
                dec_f = jnp.exp(lg_ref[0, h] * (Lc - 1.0 - m))
                dec_b = jnp.exp(lg_ref[1, h] * m)
                res[0][e] = _dot_tn((k2 * dec_f).astype(BF16), v)
                res[1][e] = _dot_tn((k2 * dec_b).astype(BF16), v)
            s_ref[0, pr] = _pair_select(res[0][0], res[0][1])
            s_ref[1, pr] = _pair_select(res[1][0], res[1][1])

    return pl.pallas_call(
        body, name="ctx_states", grid=(1,),
        in_specs=[_smem_spec(), pl.BlockSpec((Lc, D // 2), lambda i: (0, 1)), pl.BlockSpec((Lc, D), lambda i: (0, 1))],
        out_specs=pl.BlockSpec((2, H // 2, 128, 128), lambda i: (0, 0, 0, 0)),
        out_shape=jax.ShapeDtypeStruct((2, H // 2, 128, 128), F32),
    )(lg, pqk_c, pc)


T_M, T_MT = 0, 1
T_MF1, T_MB1 = 2, 3
T_QF, T_QB = 4, 5
T_KF, T_KB = 6, 7


def _decay_tables(lg, H):
    def body(lg_ref, t_ref):
        h = pl.program_id(0)
        lgf, lgb = lg_ref[0, h], lg_ref[1, h]
        i = lax.broadcasted_iota(jnp.int32, (CHUNK, CHUNK), 0).astype(F32)
        j = lax.broadcasted_iota(jnp.int32, (CHUNK, CHUNK), 1).astype(F32)
        d = i - j
        mf = jnp.where(d > 0, jnp.exp(lgf * jnp.maximum(d, 0.0)), 0.0)
        mb = jnp.where(d < 0, jnp.exp(lgb * jnp.maximum(-d, 0.0)), 0.0)
        mf_t = jnp.where(d < 0, jnp.exp(lgf * jnp.maximum(-d, 0.0)), 0.0)
        mb_t = jnp.where(d > 0, jnp.exp(lgb * jnp.maximum(d, 0.0)), 0.0)
        diag = jnp.where(d == 0, 2.0, 0.0)
        t_ref[0, T_M] = mf + mb + diag
        t_ref[0, T_MT] = mf_t + mb_t + diag
        t_ref[0, T_MF1] = mf * d
        t_ref[0, T_MB1] = mb * (-d)
        t_ref[0, T_QF] = jnp.exp(lgf * (i + 1.0))
        t_ref[0, T_QB] = jnp.exp(lgb * (CHUNK - i))
        t_ref[0, T_KF] = jnp.exp(lgf * (CHUNK - 1.0 - i))
        t_ref[0, T_KB] = jnp.exp(lgb * i)

    return pl.pallas_call(
        body, name="decay_tables", grid=(H,), in_specs=[_smem_spec()],
        out_specs=pl.BlockSpec((1, 8, CHUNK, CHUNK), lambda h: (h, 0, 0, 0)),
        out_shape=jax.ShapeDtypeStruct((H, 8, CHUNK, CHUNK), F32),
    )(lg)


def _tab_spec(H):
    return pl.BlockSpec((H, 8, CHUNK, CHUNK), lambda n: (0, 0, 0, 0))


def _chunk_decay(tab_ref, h):
    return tab_ref[h, T_QF, CHUNK - 1:CHUNK, :], tab_ref[h, T_QB, 0:1, :]


def _ret_states(kr, p, s0, tab, D):
    L = kr.shape[0]
    H = D // DV
    N = L // CHUNK
    HP = H // 2

    def body(tab_ref, kf_ref, kb_ref, vf_ref, vb_ref, s0_ref, sf_out, sb_out, sf, sb):
        n = pl.program_id(0)

        @pl.when(n == 0)
        def _():
            sf[...] = s0_ref[0]
            sb[...] = s0_ref[1]

        for cc in range(RET_CPB):
            cf_, cb_ = cc, RET_CPB - 1 - cc
            rf, rb = slice(cf_ * CHUNK, (cf_ + 1) * CHUNK), slice(cb_ * CHUNK, (cb_ + 1) * CHUNK)
            sf_out[cf_] = sf[...]
            sb_out[cb_] = sb[...]
            for pr in range(HP):
                kf2 = kf_ref[rf, pr * 128:(pr + 1) * 128].astype(F32)
                kb2 = kb_ref[rb, pr * 128:(pr + 1) * 128].astype(F32)
                inc_f, inc_b, gf, gb = [], [], [], []
                for e in range(2):
                    h = 2 * pr + e
                    inc_f.append(_dot_tn((kf2 * tab_ref[h, T_KF]).astype(BF16), vf_ref[rf, h * DV:(h + 1) * DV]))
                    inc_b.append(_dot_tn((kb2 * tab_ref[h, T_KB]).astype(BF16), vb_ref[rb, h * DV:(h + 1) * DV]))
                    cf, cb = _chunk_decay(tab_ref, h)
                    gf.append(jnp.broadcast_to(cf, (128, 128)))
                    gb.append(jnp.broadcast_to(cb, (128, 128)))
                sf[pr] = _pair_select(gf[0], gf[1]) * sf[pr] + _pair_select(inc_f[0], inc_f[1])
                sb[pr] = _pair_select(gb[0], gb[1]) * sb[pr] + _pair_select(inc_b[0], inc_b[1])

    st = jax.ShapeDtypeStruct((N, HP, 128, 128), F32)
    R = RET_CPB * CHUNK
    NB = N // RET_CPB
    return pl.pallas_call(
        body, name="ret_states", grid=(NB,),
        in_specs=[_tab_spec(H),
                  pl.BlockSpec((R, D // 2), lambda n: (n, 0)),
                  pl.BlockSpec((R, D // 2), lambda n: (NB - 1 - n, 0)),
                  pl.BlockSpec((R, D), lambda n: (n, 5)),
                  pl.BlockSpec((R, D), lambda n: (NB - 1 - n, 5)),
                  pl.BlockSpec((2, HP, 128, 128), lambda n: (0, 0, 0, 0))],
        out_specs=[pl.BlockSpec((RET_CPB, HP, 128, 128), lambda n: (n, 0, 0, 0)),
                   pl.BlockSpec((RET_CPB, HP, 128, 128), lambda n: (NB - 1 - n, 0, 0, 0))],
        out_shape=[st, st],
        scratch_shapes=[pltpu.VMEM((HP, 128, 128), F32), pltpu.VMEM((HP, 128, 128), F32)],
        compiler_params=_cparams(("arbitrary",)),
    )(tab, kr, kr, p, p, s0)


def _ret_out(qr, kr, p, sf_prev, sb_prev, gn_w, tab, D):
    L = qr.shape[0]
    H = D // DV
    N = L // CHUNK
    HP = H // 2

    def body(tab_ref, q_ref, k_ref, v_ref, zb_ref, sf_ref, sb_ref, gn_ref, o_ref, yb_ref):
        def chunk(cc, carry):
            rows = pl.ds(pl.multiple_of(cc * CHUNK, CHUNK), CHUNK)
            for pr in range(HP):
                q2 = q_ref[rows, pr * 128:(pr + 1) * 128]
                k2 = k_ref[rows, pr * 128:(pr + 1) * 128]
                sfp = sf_ref[cc, pr].astype(BF16)
                sbp = sb_ref[cc, pr].astype(BF16)
                for e in range(2):
                    h = 2 * pr + e
                    sl = slice(h * DV, (h + 1) * DV)
                    qm = jnp.where(_head_lane_mask(q2.shape, e), q2, jnp.zeros_like(q2))
                    a = (_dot_nt(qm, k2) * tab_ref[h, T_M]).astype(BF16)
                    qf = qm.astype(F32)
                    o = _dot(a, v_ref[rows, sl])
                    o += _dot((qf * tab_ref[h, T_QF]).astype(BF16), sfp)
                    o += _dot((qf * tab_ref[h, T_QB]).astype(BF16), sbp)
                    o_ref[rows, sl] = o
                    mu = jnp.mean(o, axis=-1, keepdims=True)
                    oc = o - mu
                    rstd = lax.rsqrt(jnp.mean(oc * oc, axis=-1, keepdims=True) + EPS)
                    zb = zb_ref[rows, sl].astype(F32)
                    yb_ref[rows, sl] = (zb * _sigmoid(zb) * (oc * rstd * gn_ref[:, sl])).astype(BF16)
            return carry

        lax.fori_loop(0, RET_CPB, chunk, 0)

    R = RET_CPB * CHUNK
    return pl.pallas_call(
        body, name="ret_out", grid=(N // RET_CPB,),
        in_specs=[_tab_spec(H),
                  pl.BlockSpec((R, D // 2), lambda n: (n, 0)),
                  pl.BlockSpec((R, D // 2), lambda n: (n, 0)),
                  pl.BlockSpec((R, D), lambda n: (n, 5)),
                  pl.BlockSpec((R, D), lambda n: (n, 6)),
                  pl.BlockSpec((RET_CPB, HP, 128, 128), lambda n: (n, 0, 0, 0)),
                  pl.BlockSpec((RET_CPB, HP, 128, 128), lambda n: (n, 0, 0, 0)),
                  _vec_spec(D)],
        out_specs=[pl.BlockSpec((R, D), lambda n: (n, 0)), pl.BlockSpec((R, D), lambda n: (n, 0))],
        out_shape=[jax.ShapeDtypeStruct((L, D), F32), jax.ShapeDtypeStruct((L, D), BF16)],
        compiler_params=_cparams(("parallel",)),
    )(tab, qr, kr, p, p, sf_prev, sb_prev, gn_w)


def _mid(p, yb, o, x, tgt, w3, g, fw, conv_w, conv_b, gn_w, D):
    L = x.shape[0]
    H = D // DV
    tm = min(256, L)
    nt = L // tm

    def body(h_ref, bg_ref, cg_ref, za_ref, hp_ref, hn_ref, cp_ref, cn_ref, yb_ref, ga_ref, gb_ref, zb_ref, o_ref,
             x_ref, t_ref, w_hbm, g_ref, fw_ref, cw_ref, cb_ref, gn_ref,
             dx1_ref, dya_ref, do_ref, dzb_ref, dgab_ref, dw_hbm, st_ref, w_vm, dw_acc, sem):
        i = pl.program_id(0)

        @pl.when(i == 0)
        def _():
            cp = pltpu.make_async_copy(w_hbm, w_vm, sem)
            cp.start()
            dw_acc[...] = jnp.zeros_like(dw_acc)
            st_ref[...] = jnp.zeros_like(st_ref)
            cp.wait()

        u = cg_ref[...].astype(F32) * h_ref[...].astype(F32)
        above = jnp.where(i == 0, 0.0, cp_ref[15:16, :].astype(F32) * hp_ref[15:16, :].astype(F32))
        below = jnp.where(i == nt - 1, 0.0, cn_ref[0:1, :].astype(F32) * hn_ref[0:1, :].astype(F32))
        dn, up = _shift_rows(u, above, below)
        co = cw_ref[0:1, :] * dn + cw_ref[1:2, :] * u + cw_ref[2:3, :] * up + cb_ref[...]
        za = za_ref[...].astype(F32)
        ya_b = (za * _sigmoid(za) * bg_ref[...].astype(F32) * co).astype(BF16)
        yb_b = yb_ref[...]
        y_a = _dot(ya_b, w_vm[0])
        y_b = _dot(yb_b, w_vm[1])
        sga = _sigmoid(ga_ref[...].astype(F32))
        sgb = _sigmoid(gb_ref[...].astype(F32))
        mix_b = (sga * y_a + sgb * y_b).astype(BF16)
        y_x = _dot(mix_b, w_vm[2])
        gvec, fwv = g_ref[...], fw_ref[...]
        x1 = x_ref[...] + gvec * y_x
        r1 = lax.rsqrt(jnp.mean(x1 * x1, axis=-1, keepdims=True) + EPS)
        xh = x1 * r1
        diff = xh * fwv - t_ref[...]
        dout = diff * (1.0 / D)
        dxh = dout * fwv
        dx1 = r1 * (dxh - xh * jnp.mean(dxh * xh, axis=-1, keepdims=True))
        dx1_ref[...] = dx1
        st_ref[0:1, :] += jnp.sum(dout * xh, axis=0, keepdims=True)
        st_ref[1:2, :] += jnp.sum(dx1 * y_x, axis=0, keepdims=True)
        st_ref[2:3, :] += jnp.sum(diff * diff, axis=0, keepdims=True)
        dyx_b = (dx1 * gvec).astype(BF16)
        dmix = _dot_nt(dyx_b, w_vm[2])
        dw_acc[2] += _dot_tn(mix_b, dyx_b)
        dya_b = (dmix * sga).astype(BF16)
        dyb_b = (dmix * sgb).astype(BF16)
        dgab_ref[:, 0:D] = (dmix * y_a * sga * (1.0 - sga)).astype(BF16)
        dgab_ref[:, D:2 * D] = (dmix * y_b * sgb * (1.0 - sgb)).astype(BF16)
        dya_ref[...] = _dot_nt(dya_b, w_vm[0])
        dyb = _dot_nt(dyb_b, w_vm[1])
        dw_acc[0] += _dot_tn(ya_b, dya_b)
        dw_acc[1] += _dot_tn(yb_b, dyb_b)

        for h in range(H):
            sl = slice(h * DV, (h + 1) * DV)
            ov = o_ref[:, sl]
            oc = ov - jnp.mean(ov, axis=-1, keepdims=True)
            rstd = lax.rsqrt(jnp.mean(oc * oc, axis=-1, keepdims=True) + EPS)
            rn = oc * rstd
            gw = gn_ref[:, sl]
            zb = zb_ref[:, sl].astype(F32)
            sz = _sigmoid(zb)
            dy = dyb[:, sl]
            dzb_ref[:, sl] = (dy * (rn * gw) * (sz * (1.0 + zb * (1.0 - sz)))).astype(BF16)
            dretn = dy * (zb * sz)
            st_ref[3:4, sl] += jnp.sum(dretn * rn, axis=0, keepdims=True)
            drn = dretn * gw
            do_ref[:, sl] = (rstd * (drn - jnp.mean(drn, axis=-1, keepdims=True)
                                     - rn * jnp.mean(drn * rn, axis=-1, keepdims=True))).astype(BF16)

        @pl.when(i == nt - 1)
        def _():
            out = pltpu.make_async_copy(dw_acc, dw_hbm, sem)
            out.start()
            out.wait()

    row = lambda col: pl.BlockSpec((tm, D), lambda i: (i, col))
    any_spec = pl.BlockSpec(memory_space=pl.ANY)
    f32o = jax.ShapeDtypeStruct((L, D), F32)
    bf16o = jax.ShapeDtypeStruct((L, D), BF16)
    hp, hn = _halo_specs(tm, L, D, 0)
    cp, cn = _halo_specs(tm, L, D, 2)
    return pl.pallas_call(
        body, name="mid", grid=(nt,),
        in_specs=[row(0), row(1), row(2), row(3), hp, hn, cp, cn, row(0), row(7), row(8), row(6), row(0),
                  row(0), row(0), any_spec, _vec_spec(D), _vec_spec(D),
                  pl.BlockSpec((8, D), lambda i: (0, 0)), _vec_spec(D), _vec_spec(D)],
        out_specs=[row(0), row(0), row(0), row(0), pl.BlockSpec((tm, 2 * D), lambda i: (i, 0)), any_spec,
                   pl.BlockSpec((8, D), lambda i: (0, 0))],
        out_shape=[f32o, f32o, bf16o, bf16o, jax.ShapeDtypeStruct((L, 2 * D), BF16),
                   jax.ShapeDtypeStruct((3, D, D), F32), jax.ShapeDtypeStruct((8, D), F32)],
        scratch_shapes=[pltpu.VMEM((3, D, D), BF16), pltpu.VMEM((3, D, D), F32), pltpu.SemaphoreType.DMA],
        compiler_params=_cparams(("arbitrary",), VMEM_LIMIT),
    )(p, p, p, p, p, p, p, p, yb, p, p, p, o, x, tgt, w3, g, fw, conv_w, conv_b, gn_w)


def _conv_bwd(dya, p, conv_w, conv_b, D, exchange=None):
    L = p.shape[0]
    tl = min(256, L)
    nt = L // tl

    def body(d_ref, h_ref, bg_ref, cg_ref, za_ref,
             dp_ref, dn_ref, hp_ref, hn_ref, bp_ref, bn_ref, cp_ref, cn_ref, zp_ref, zn_ref,
             w_ref, b_ref, dc_ref, st_ref):
        i = pl.program_id(0)

        @pl.when(i == 0)
        def _():
            st_ref[...] = jnp.zeros_like(st_ref)

        first, last = i == 0, i == nt - 1
        h = h_ref[...].astype(F32)
        cg = cg_ref[...].astype(F32)
        bg = bg_ref[...].astype(F32)
        za = za_ref[...].astype(F32)
        dy = d_ref[...].astype(F32)
        u = cg * h
        u_above = jnp.where(first, 0.0, cp_ref[15:16, :].astype(F32) * hp_ref[15:16, :].astype(F32))
        u_below = jnp.where(last, 0.0, cn_ref[0:1, :].astype(F32) * hn_ref[0:1, :].astype(F32))
        u_dn, u_up = _shift_rows(u, u_above, u_below)
        w0, w1, w2 = w_ref[0:1, :], w_ref[1:2, :], w_ref[2:3, :]
        co = w0 * u_dn + w1 * u + w2 * u_up + b_ref[...]
        sz = _sigmoid(za)
        silu = za * sz
        dc_ref[:, 3 * D:4 * D] = (dy * bg * co * (sz * (1.0 + za * (1.0 - sz)))).astype(BF16)
        dc_ref[:, D:2 * D] = (dy * silu * co).astype(BF16)
        dco = dy * silu * bg

        def edge(dr, zr, br, r):
            z = zr[r:r + 1, :].astype(F32)
            return dr[r:r + 1, :].astype(F32) * (z * _sigmoid(z)) * br[r:r + 1, :].astype(F32)

        dco_above = jnp.where(first, 0.0, edge(dp_ref, zp_ref, bp_ref, 15))
        dco_below = jnp.where(last, 0.0, edge(dn_ref, zn_ref, bn_ref, 0))
        dco_dn, dco_up = _shift_rows(dco, dco_above, dco_below)
        du = w0 * dco_up + w1 * dco + w2 * dco_dn
        dc_ref[:, 2 * D:3 * D] = (du * h).astype(BF16)
        dc_ref[:, 0:D] = (du * cg).astype(BF16)
        st_ref[0:1, :] += jnp.sum(dco * u_dn, axis=0, keepdims=True)
        st_ref[1:2, :] += jnp.sum(dco * u, axis=0, keepdims=True)
        st_ref[2:3, :] += jnp.sum(dco * u_up, axis=0, keepdims=True)
        st_ref[3:4, :] += jnp.sum(dco, axis=0, keepdims=True)

    main = lambda col: pl.BlockSpec((tl, D), lambda i: (i, col))
    halos = []
    for col in (0, 0, 1, 2, 3):
        halos.extend(_halo_specs(tl, L, D, col))
    return _riding_call(
        body, exchange, nt, name="conv_bwd",
        args=(dya, p, p, p, p, dya, dya, p, p, p, p, p, p, p, p, conv_w, conv_b),
        in_specs=[main(0), main(0), main(1), main(2), main(3)] + halos
                 + [pl.BlockSpec((8, D), lambda i: (0, 0)), _vec_spec(D)],
        out_specs=[pl.BlockSpec((tl, 4 * D), lambda i: (i, 0)), pl.BlockSpec((8, D), lambda i: (0, 0))],
        out_shape=[jax.ShapeDtypeStruct((L, 4 * D), BF16), jax.ShapeDtypeStruct((8, D), F32)],
        cparams=_cparams(("arbitrary",)))


def _ret_bwd_states(qr, do, tab, D):
    L = qr.shape[0]
    H = D // DV
    N = L // CHUNK
    HP = H // 2

    def body(tab_ref, qf_ref, qb_ref, dof_ref, dob_ref, dsf_out, dsb_out, ds0_out, dsf, dsb):
        n = pl.program_id(0)

        @pl.when(n == 0)
        def _():
            dsf[...] = jnp.zeros_like(dsf)
            dsb[...] = jnp.zeros_like(dsb)

        for cc in range(RET_CPB):
            cf_, cb_ = RET_CPB - 1 - cc, cc
            rf, rb = slice(cf_ * CHUNK, (cf_ + 1) * CHUNK), slice(cb_ * CHUNK, (cb_ + 1) * CHUNK)
            dsf_out[cf_] = dsf[...]
            dsb_out[cb_] = dsb[...]
            for pr in range(HP):
                qf2 = qf_ref[rf, pr * 128:(pr + 1) * 128].astype(F32)
                qb2 = qb_ref[rb, pr * 128:(pr + 1) * 128].astype(F32)
                inc_f, inc_b, gf, gb = [], [], [], []
                for e in range(2):
                    h = 2 * pr + e
                    inc_f.append(_dot_tn((qf2 * tab_ref[h, T_QF]).astype(BF16), dof_ref[rf, h * DV:(h + 1) * DV]))
                    inc_b.append(_dot_tn((qb2 * tab_ref[h, T_QB]).astype(BF16), dob_ref[rb, h * DV:(h + 1) * DV]))
                    cf, cb = _chunk_decay(tab_ref, h)
                    gf.append(jnp.broadcast_to(cf, (128, 128)))
                    gb.append(jnp.broadcast_to(cb, (128, 128)))
                dsf[pr] = _pair_select(gf[0], gf[1]) * dsf[pr] + _pair_select(inc_f[0], inc_f[1])
                dsb[pr] = _pair_select(gb[0], gb[1]) * dsb[pr] + _pair_select(inc_b[0], inc_b[1])

        @pl.when(n == NB - 1)
        def _():
            ds0_out[0] = dsf[...]
            ds0_out[1] = dsb[...]

    st = jax.ShapeDtypeStruct((N, HP, 128, 128), F32)
    R = RET_CPB * CHUNK
    NB = N // RET_CPB
    return pl.pallas_call(
        body, name="ret_bwd_states", grid=(NB,),
        in_specs=[_tab_spec(H),
                  pl.BlockSpec((R, D // 2), lambda n: (NB - 1 - n, 0)),
                  pl.BlockSpec((R, D // 2), lambda n: (n, 0)),
                  pl.BlockSpec((R, D), lambda n: (NB - 1 - n, 0)),
                  pl.BlockSpec((R, D), lambda n: (n, 0))],
        out_specs=[pl.BlockSpec((RET_CPB, HP, 128, 128), lambda n: (NB - 1 - n, 0, 0, 0)),
                   pl.BlockSpec((RET_CPB, HP, 128, 128), lambda n: (n, 0, 0, 0)),
                   pl.BlockSpec((2, HP, 128, 128), lambda n: (0, 0, 0, 0))],
        out_shape=[st, st, jax.ShapeDtypeStruct((2, HP, 128, 128), F32)],
        scratch_shapes=[pltpu.VMEM((HP, 128, 128), F32), pltpu.VMEM((HP, 128, 128), F32)],
        compiler_params=_cparams(("arbitrary",)),
    )(tab, qr, qr, do, do)


def _ret_bwd_main(qr, kr, p, do, sf_prev, sb_prev, dsf, dsb, c2, s2, tab, D, exchange=None):
    L = qr.shape[0]
    H = D // DV
    N = L // CHUNK
    HP = H // 2
    W = D // 2

    def body(tab_ref, q_ref, k_ref, v_ref, do_ref, sf_ref, sb_ref, dsf_ref, dsb_ref, c_ref, s_ref,
             dr_ref, st_ref, dl_acc):
        @pl.when(pl.program_id(0) == 0)
        def _():
            dl_acc[...] = jnp.zeros_like(dl_acc)

        i = lax.broadcasted_iota(jnp.int32, (CHUNK, 128), 0).astype(F32)
        rowid = lax.broadcasted_iota(jnp.int32, (128, 128), 0)

        def chunk(cc, carry):
            rows = pl.ds(pl.multiple_of(cc * CHUNK, CHUNK), CHUNK)
            c, s = c_ref[rows, :], s_ref[rows, :]
            for pr in range(HP):
                ps = slice(pr * 128, (pr + 1) * 128)
                q2, k2 = q_ref[rows, ps], k_ref[rows, ps]
                sf32, sb32 = sf_ref[cc, pr], sb_ref[cc, pr]
                dsf32, dsb32 = dsf_ref[cc, pr], dsb_ref[cc, pr]
                sfp, sbp = sf32.astype(BF16), sb32.astype(BF16)
                dsfp, dsbp = dsf32.astype(BF16), dsb32.astype(BF16)
                dq2 = jnp.zeros((CHUNK, 128), F32)
                dk2 = jnp.zeros((CHUNK, 128), F32)
                for e in range(2):
                    h = 2 * pr + e
                    sl = slice(h * DV, (h + 1) * DV)
                    hm = _head_lane_mask(q2.shape, e)
                    qm = jnp.where(hm, q2, jnp.zeros_like(q2))
                    km = jnp.where(hm, k2, jnp.zeros_like(k2))
                    qf, kf = qm.astype(F32), km.astype(F32)
                    v, do = v_ref[rows, sl], do_ref[rows, sl]
                    vf, dof = v.astype(F32), do.astype(F32)
                    m_t = tab_ref[h, T_MT]
                    sc = _dot_nt(qm, k2)
                    dpm = _dot_nt(do, v)
                    dsc = (dpm * tab_ref[h, T_M]).astype(BF16)
                    a_t = (_dot_nt(km, q2) * m_t).astype(BF16)
                    dsc_t = (_dot_nt(v, do) * m_t).astype(BF16)
                    dq_f, dq_b = tab_ref[h, T_QF], tab_ref[h, T_QB]
                    dk_f, dk_b = tab_ref[h, T_KF], tab_ref[h, T_KB]
                    dq = _dot(dsc, km)
                    dq += jnp.where(hm, dq_f * _dot_nt(do, sfp) + dq_b * _dot_nt(do, sbp), 0.0)
                    dk = _dot(dsc_t, qm)
                    dk += jnp.where(hm, dk_f * _dot_nt(v, dsfp) + dk_b * _dot_nt(v, dsbp), 0.0)
                    kdf = _dot((kf * dk_f).astype(BF16), dsfp)
                    kdb = _dot((kf * dk_b).astype(BF16), dsbp)
                    dr_ref[rows, D + h * DV:D + (h + 1) * DV] = (_dot(a_t, do) + kdf + kdb).astype(BF16)
                    dq2 += dq
                    dk2 += dk
                    xf = _dot((qf * dq_f).astype(BF16), sfp)
                    xb = _dot((qf * dq_b).astype(BF16), sbp)
                    pair = (rowid < DK) if e == 0 else (rowid >= DK)
                    gcf, gcb = tab_ref[h, T_QF, CHUNK - 1:CHUNK, 0:1], tab_ref[h, T_QB, 0:1, 0:1]
                    scdp = sc * dpm
                    dl_acc[h, 0] += scdp * tab_ref[h, T_MF1] + xf * dof * (i + 1.0) \
                        + kdf * vf * (CHUNK - 1.0 - i) + (CHUNK * gcf) * jnp.where(pair, dsf32 * sf32, 0.0)
                    dl_acc[h, 1] += scdp * tab_ref[h, T_MB1] + xb * dof * (CHUNK - i) \
                        + kdb * vf * i + (CHUNK * gcb) * jnp.where(pair, dsb32 * sb32, 0.0)
                dr_ref[rows, ps] = (dq2 * c - _swap_halves(dq2) * s).astype(BF16)
                dr_ref[rows, W + pr * 128:W + (pr + 1) * 128] = \
                    ((dk2 * c - _swap_halves(dk2) * s) * K_SCALE).astype(BF16)
            return carry

        lax.fori_loop(0, RET_CPB, chunk, 0)

        @pl.when(pl.program_id(0) == N // RET_CPB - 1)
        def _():
            lane = lax.broadcasted_iota(jnp.int32, (1, 128), 1)
            acc = [jnp.zeros((1, 128), F32), jnp.zeros((1, 128), F32)]
            for h in range(H):
                for b in range(2):
                    acc[b] += jnp.where(lane == h, _sum_all(dl_acc[h, b]), 0.0)
            st_ref[...] = jnp.zeros_like(st_ref)
            st_ref[0:1, :] = acc[0]
            st_ref[1:2, :] = acc[1]

    R = RET_CPB * CHUNK
    st_spec = pl.BlockSpec((RET_CPB, HP, 128, 128), lambda n: (n, 0, 0, 0))
    half = pl.BlockSpec((R, W), lambda n: (n, 0))
    rope = pl.BlockSpec((R, 128), lambda n: (n, 0))
    return _riding_call(
        body, exchange, N // RET_CPB, name="ret_bwd_main",
        args=(tab, qr, kr, p, do, sf_prev, sb_prev, dsf, dsb, c2, s2),
        in_specs=[_tab_spec(H), half, half,
                  pl.BlockSpec((R, D), lambda n: (n, 5)),
                  pl.BlockSpec((R, D), lambda n: (n, 0)),
                  st_spec, st_spec, st_spec, st_spec, rope, rope],
        out_specs=[pl.BlockSpec((R, 2 * D), lambda n: (n, 0)),
                   pl.BlockSpec((8, 128), lambda n: (0, 0))],
        out_shape=[jax.ShapeDtypeStruct((L, 2 * D), BF16), jax.ShapeDtypeStruct((8, 128), F32)],
        scratch=[pltpu.VMEM((H, 2, CHUNK, 128), F32)],
        cparams=_cparams(("arbitrary",)))


def _ctx_bwd(pc, pqk_c, ds0, lg, D):
    Lc = pc.shape[0]
    H = D // DV
    HP = H // 2
    W = D // 2

    def body(lg_ref, k_ref, v_ref, ds_ref, dr_ref, st_ref):
        dqk_ref = dr_ref.at[:, 0:D]
        dv_ref = dr_ref.at[:, D:2 * D]
        m = lax.broadcasted_iota(jnp.int32, (Lc, 128), 0).astype(F32)
        lane = lax.broadcasted_iota(jnp.int32, (1, 128), 1)
        acc_f = jnp.zeros((1, 128), F32)
        acc_b = jnp.zeros((1, 128), F32)
        dqk_ref[:, 0:W] = jnp.zeros((Lc, W), BF16)
        for pr in range(HP):
            ps = slice(pr * 128, (pr + 1) * 128)
            k2 = k_ref[:, ps].astype(F32) * K_SCALE
            dsfp, dsbp = ds_ref[0, pr].astype(BF16), ds_ref[1, pr].astype(BF16)
            dk2 = jnp.zeros((Lc, 128), F32)
            for e in range(2):
                h = 2 * pr + e
                sl = slice(h * DV, (h + 1) * DV)
                hm = _head_lane_mask(k2.shape, e)
                km = jnp.where(hm, k2, 0.0)
                v = v_ref[:, sl]
                vf = v.astype(F32)
                dec_f = jnp.exp(lg_ref[0, h] * (Lc - 1.0 - m))
                dec_b = jnp.exp(lg_ref[1, h] * m)
                kdf = _dot((km * dec_f).astype(BF16), dsfp)
                kdb = _dot((km * dec_b).astype(BF16), dsbp)
                dv_ref[:, sl] = (kdf + kdb).astype(BF16)
                dk2 += jnp.where(hm, dec_f * _dot_nt(v, dsfp) + dec_b * _dot_nt(v, dsbp), 0.0)
                acc_f += jnp.where(lane == h, _sum_all(kdf * vf * (Lc - 1.0 - m)), 0.0)
                acc_b += jnp.where(lane == h, _sum_all(kdb * vf * m), 0.0)
            dqk_ref[:, W + pr * 128:W + (pr + 1) * 128] = (dk2 * K_SCALE).astype(BF16)
        st_ref[...] = jnp.zeros_like(st_ref)
        st_ref[0:1, :] = acc_f
        st_ref[1:2, :] = acc_b

    return pl.pallas_call(
        body, name="ctx_bwd", grid=(1,),
        in_specs=[_smem_spec(), pl.BlockSpec((Lc, W), lambda i: (0, 1)), pl.BlockSpec((Lc, D), lambda i: (0, 1)),
                  pl.BlockSpec((2, HP, 128, 128), lambda i: (0, 0, 0, 0))],
        out_specs=[pl.BlockSpec((Lc, 2 * D), lambda i: (0, 0)), pl.BlockSpec((8, 128), lambda i: (0, 0))],
        out_shape=[jax.ShapeDtypeStruct((Lc, 2 * D), BF16), jax.ShapeDtypeStruct((8, 128), F32)],
    )(lg, pqk_c, pc, ds0)


class _Exchange(NamedTuple):
    inputs: tuple
    out_shapes: tuple
    n_copies: int
    build: Callable


def _exchange_parts(exchange):
    if exchange is None:
        return [], [], [], [], []
    n = exchange.n_copies
    return (list(exchange.inputs), [ANY] * len(exchange.inputs), list(exchange.out_shapes),
            [ANY] * len(exchange.out_shapes), [pltpu.SemaphoreType.DMA((n,)), pltpu.SemaphoreType.DMA((n,))])


def _riding_call(body, exchange, n_steps, *, args, in_specs, out_specs, out_shape, name, cparams, scratch=()):
    ex_args, ex_in_specs, ex_shapes, ex_out_specs, ex_scratch = _exchange_parts(exchange)
    n_in, n_out, n_sc = len(args), len(out_shape), len(scratch)

    def riding(*refs):
        k = n_in + len(ex_args)
        ins, ex_in = refs[:n_in], refs[n_in:k]
        outs, ex_out = refs[k:k + n_out], refs[k + n_out:k + n_out + len(ex_shapes)]
        k += n_out + len(ex_shapes)
        own_scratch, ex_sems = refs[k:k + n_sc], refs[k + n_sc:]
        step = pl.program_id(0)
        if exchange is not None:
            @pl.when(step == 0)
            def _():
                for rc in exchange.build(ex_in, ex_out, *ex_sems):
                    rc.start()
        body(*ins, *outs, *own_scratch)
        if exchange is not None:
            @pl.when(step == n_steps - 1)
            def _():
                for rc in exchange.build(ex_in, ex_out, *ex_sems):
                    rc.wait()

    return tuple(pl.pallas_call(
        riding, name=name, grid=(n_steps,),
        in_specs=list(in_specs) + ex_in_specs, out_specs=list(out_specs) + ex_out_specs,
        out_shape=list(out_shape) + ex_shapes, scratch_shapes=list(scratch) + ex_scratch,
        compiler_params=cparams,
    )(*args, *ex_args))


def _dxm(groups, col0, w, x, nw, sc, dx1, name, exchange=None):
    L, D = x.shape
    tm = min(256, L)
    nt = L // tm
    ng = len(groups)
    widths = [g.shape[1] for g in groups]
    wtot = sum(widths)
    with_dx = dx1 is not None
    ex_args, ex_in_specs, ex_shapes, ex_out_specs, ex_scratch = _exchange_parts(exchange)
    n_in = ng + 4 + (1 if with_dx else 0)
    n_out = 2 if with_dx else 1

    def body(*refs):
        group_refs = refs[:ng]
        w_hbm, x_ref, nw_ref, sc_ref = refs[ng:ng + 4]
        ex_in = refs[n_in:n_in + len(ex_args)]
        outs = refs[n_in + len(ex_args):]
        if with_dx:
            dx1_ref, gx_ref, st_ref = refs[ng + 4], outs[0], outs[1]
        else:
            st_ref = outs[0]
        ex_out = outs[n_out:n_out + len(ex_shapes)]
        w_vm, sem = outs[n_out + len(ex_shapes):n_out + len(ex_shapes) + 2]
        ex_sems = outs[n_out + len(ex_shapes) + 2:]
        i = pl.program_id(0)

        @pl.when(i == 0)
        def _():
            cp = pltpu.make_async_copy(w_hbm.at[:, col0 * D:col0 * D + wtot], w_vm, sem)
            cp.start()
            if exchange is not None:
                for rc in exchange.build(ex_in, ex_out, *ex_sems):
                    rc.start()
            st_ref[...] = jnp.zeros_like(st_ref)
            cp.wait()

        dxm, off = None, 0
        for g_ref, wd in zip(group_refs, widths):
            part = _dot_nt(g_ref[...], w_vm[:, off:off + wd])
            dxm = part if dxm is None else dxm + part
            off += wd

        xv = x_ref[...]
        r = lax.rsqrt(jnp.mean(xv * xv, axis=-1, keepdims=True) + EPS)
        xh = xv * r
        nwv = nw_ref[...]
        dxn = dxm * (1.0 + sc_ref[...])
        st_ref[0:1, :] += jnp.sum(dxm, axis=0, keepdims=True)
        st_ref[1:2, :] += jnp.sum(dxm * (xh * nwv), axis=0, keepdims=True)
        st_ref[2:3, :] += jnp.sum(dxn * xh, axis=0, keepdims=True)
        if with_dx:
            dxh = dxn * nwv
            gx_ref[...] = dx1_ref[...] + r * (dxh - xh * jnp.mean(dxh * xh, axis=-1, keepdims=True))

        if exchange is not None:
            @pl.when(i == nt - 1)
            def _():
                for rc in exchange.build(ex_in, ex_out, *ex_sems):
                    rc.wait()

    row = pl.BlockSpec((tm, D), lambda i: (i, 0))
    in_specs = [pl.BlockSpec((tm, wd), lambda i: (i, 0)) for wd in widths] + [ANY, row, _vec_spec(D), _vec_spec(D)]
    out_specs = [pl.BlockSpec((8, D), lambda i: (0, 0))]
    out_shape = [jax.ShapeDtypeStruct((8, D), F32)]
    args = list(groups) + [w, x, nw, sc]
    if with_dx:
        in_specs.append(row)
        out_specs.insert(0, row)
        out_shape.insert(0, jax.ShapeDtypeStruct((L, D), F32))
        args.append(dx1)
    res = pl.pallas_call(
        body, name=name, grid=(nt,),
        in_specs=in_specs + ex_in_specs, out_specs=out_specs + ex_out_specs, out_shape=out_shape + ex_shapes,
        scratch_shapes=[pltpu.VMEM((D, wtot), BF16), pltpu.SemaphoreType.DMA] + ex_scratch,
        compiler_params=_cparams(("arbitrary",), VMEM_LIMIT),
    )(*args, *ex_args)
    gx = res[0] if with_dx else None
    return (gx, res[n_out - 1], *res[n_out:])


DW_TN = 256
DW_RING = 4


def _dw_in(xmt, groups, cmt, dr_c, D, pair):
    L = xmt.shape[1]
    Lc = cmt.shape[1]
    Dh = D // 2
    tn = min(DW_TN, D)
    nblk = [g.shape[1] // tn for g in groups]
    starts = [sum(nblk[:g]) for g in range(len(groups))]
    ng = len(groups)
    nj = sum(nblk)
    rows_out = Dh if pair else D

    def body(*refs):
        xt_hbm = refs[0]
        group_refs = refs[1:1 + ng]
        ct_hbm, drc_ref, o_ref = refs[1 + ng:4 + ng]
        rest = refs[4 + ng:]
        if pair:
            ra_hbm, xt_vm, ct_vm, loc, ring, s_send, s_recv = rest
            pos = _position()
            sib = _peer(pos, 1)
        else:
            xt_vm, ct_vm, loc = rest
        j = pl.program_id(0)

        @pl.when(j == 0)
        def _():
            if pair:
                c = pos[2]
                other = pl.ds(pl.multiple_of((1 - c) * Dh, Dh), Dh)
                mine = pl.ds(pl.multiple_of(c * Dh, Dh), Dh)
                cps = [pltpu.make_async_copy(xt_hbm.at[other, :], xt_vm.at[0:Dh, :], loc.at[0]),
                       pltpu.make_async_copy(xt_hbm.at[mine, :], xt_vm.at[Dh:D, :], loc.at[1]),
                       pltpu.make_async_copy(ct_hbm.at[other, :], ct_vm.at[0:Dh, :], loc.at[2]),
                       pltpu.make_async_copy(ct_hbm.at[mine, :], ct_vm.at[Dh:D, :], loc.at[3])]
            else:
                cps = [pltpu.make_async_copy(xt_hbm, xt_vm, loc.at[0]), pltpu.make_async_copy(ct_hbm, ct_vm, loc.at[1])]
            for cp in cps:
                cp.start()
            for cp in cps:
                cp.wait()

        def send(slot):
            cols = pl.ds(pl.multiple_of(j * tn, 128), tn)
            return pltpu.make_async_remote_copy(src_ref=ring.at[slot], dst_ref=ra_hbm.at[:, cols],
                                                send_sem=s_send.at[slot], recv_sem=s_recv,
                                                device_id=sib, device_id_type=MESH)

        for g in range(ng):
            @pl.when((j >= starts[g]) & (j < starts[g] + nblk[g]))
            def _(g=g):
                acc = _dot(xt_vm[...], group_refs[g][...])
                if g == 1:
                    acc += _dot(ct_vm[...], drc_ref[...])
                if not pair:
                    o_ref[...] = acc
                    return
                o_ref[...] = acc[Dh:, :]
                slot = lax.rem(j, DW_RING)

                @pl.when(j >= DW_RING)
                def _():
                    send(slot).wait_send()

                ring[slot] = acc[0:Dh, :]
                send(slot).start()

        if pair:
            @pl.when(j == nj - 1)
            def _():
                pltpu.make_async_remote_copy(src_ref=ra_hbm, dst_ref=ra_hbm, send_sem=s_send.at[0], recv_sem=s_recv,
                                             device_id=sib, device_id_type=MESH).wait_recv()
                for slot in range(DW_RING):
                    send(slot).wait_send()

    def group_spec(g, rows):
        return pl.BlockSpec((rows, tn), lambda j: (0, jnp.clip(j - starts[g], 0, nblk[g] - 1)))

    out_specs = [pl.BlockSpec((rows_out, tn), lambda j: (0, j))]
    out_shape = [jax.ShapeDtypeStruct((rows_out, nj * tn), F32)]
    scratch = [pltpu.VMEM((D, L), BF16), pltpu.VMEM((D, Lc), BF16), pltpu.SemaphoreType.DMA((4,))]
    if pair:
        out_specs.append(ANY)
        out_shape.append(jax.ShapeDtypeStruct((Dh, nj * tn), F32))
        scratch += [pltpu.VMEM((DW_RING, Dh, tn), F32), pltpu.SemaphoreType.DMA((DW_RING,)), pltpu.SemaphoreType.DMA]
    return tuple(pl.pallas_call(
        body, name="dw_in", grid=(nj,),
        in_specs=[ANY] + [group_spec(g, L) for g in range(ng)] + [ANY, group_spec(1, Lc)],
        out_specs=out_specs, out_shape=out_shape, scratch_shapes=scratch,
        compiler_params=_cparams(("arbitrary",), VMEM_LIMIT),
    )(xmt, *groups, cmt, dr_c))


def _local_step(x, ctx, tgt, mod_x, mod_c, norm_w, conv_w8, conv_b, lg, gn_w, fw, project, csidx=None):
    L, D = x.shape
    sh_x, sc_x, g_x = mod_x[0:1], mod_x[1:2], mod_x[2:3]
    sh_c, sc_c = mod_c[0:1], mod_c[1:2]
    c2, s2 = _rope_tables(L)
    tab = _decay_tables(lg, D // DV)

    xm, xmt = _norm_mod(x, norm_w, sc_x, sh_x, "norm_mod_x")
    cm, cmt = _norm_mod(ctx, norm_w, sc_c, sh_c, "norm_mod_ctx")
    p, qr, kr, w_in, w3 = project(xm, c2, s2)
    pc, pqk_c = _in_proj(cm, w_in, "in_proj_ctx", QK_BLOCK, 2)
    s0 = _ctx_states(pc, pqk_c, lg, D)
    sf_prev, sb_prev = _ret_states(kr, p, s0, tab, D)
    o, yb = _ret_out(qr, kr, p, sf_prev, sb_prev, gn_w, tab, D)
    dx1, dya, do, dzb, dgab, dw3, st_mid = _mid(p, yb, o, x, tgt, w3, g_x, fw, conv_w8, conv_b, gn_w, D)
    reduce = csidx is not None
    dw3_5 = dw3.reshape(3, N_SHARD, 2, D // 8, D)
    dconv, st_conv, *ra_3 = _conv_bwd(dya, p, conv_w8, conv_b, D, _pair_exchange_w3(dw3_5) if reduce else None)
    dsf, dsb, ds0 = _ret_bwd_states(qr, do, tab, D)
    cs_3 = _sum_pair_w3(csidx[0:1], dw3_5, ra_3[0]) if reduce else None
    dret, st_lg, *rb_3 = _ret_bwd_main(qr, kr, p, do, sf_prev, sb_prev, dsf, dsb, c2, s2, tab, D,
                                       _chips_exchange_w3(cs_3) if reduce else None)
    g_3 = _sum_chips_w3(csidx, cs_3, rb_3[0]) if reduce else dw3
    dret_c, st_lgc = _ctx_bwd(pc, pqk_c, ds0, lg, D)
    groups = (dconv, dret, dzb, dgab)
    _, st_c = _dxm((dret_c,), 4, w_in, ctx, norm_w, sc_c, None, "dxm_ctx")
    return groups, dret_c, xmt, cmt, dx1, sc_x, w_in, g_3, (st_mid, st_conv, st_lg, st_lgc, st_c)


CHIP_FLIPS = (4, 2, 6)
ANY = pl.BlockSpec(memory_space=pl.ANY)
VMEM_FULL = pl.BlockSpec(memory_space=pltpu.VMEM)


def _position():
    return lax.axis_index("x"), lax.axis_index("y"), lax.axis_index("c")


def _peer(pos, k):
    x, y, c = pos
    return (1 - x if k & 4 else x, 1 - y if k & 2 else y, 1 - c if k & 1 else c)


def _dev_id(pos):
    return 4 * pos[0] + 2 * pos[1] + pos[2]


def _shard_of(pos):
    return 2 * pos[0] + pos[1]


def _remote(src, dst, send_sems, recv_sems, idx, to):
    return pltpu.make_async_remote_copy(src_ref=src, dst_ref=dst, send_sem=send_sems.at[idx],
                                        recv_sem=recv_sems.at[idx], device_id=to, device_id_type=MESH)


def _dot_f32(a, b):
    return jnp.dot(a, b, precision=lax.Precision.HIGHEST, preferred_element_type=F32)


def _silu(x):
    return x * _sigmoid(x)


def _fwd_small(c8, cctx8, ada_w, ada_b, conv_w8):
    D = c8.shape[1]
    Wm = ada_w.shape[1]
    Dq = conv_w8.shape[1]

    def body(c_ref, cc_ref, aw_ref, ab_ref, cw_ref, act_ref, mod_ref, cwf_ref,
             cbuf, pmine, pbuf, wbuf, s_c, r_c, s_p, r_p, s_w, r_w):
        pos = _position()
        me, s = _dev_id(pos), _shard_of(pos)
        cbuf[me] = c_ref[...]
        wbuf[s] = cw_ref[...]
        sends = [_remote(c_ref, cbuf.at[me], s_c, r_c, k - 1, _peer(pos, k)) for k in range(1, 8)]
        sends += [_remote(cw_ref, wbuf.at[s], s_w, r_w, j, _peer(pos, k)) for j, k in enumerate(CHIP_FLIPS)]
        for cp in sends:
            cp.start()
        for k in range(1, 8):
            _remote(c_ref, cbuf.at[_dev_id(_peer(pos, k))], s_c, r_c, k - 1, _peer(pos, k)).wait_recv()
        for d in range(N_DEV):
            act_ref[d:d + 1, :] = _silu(cbuf[d, 0:1, :])
        act_ref[8:9, :] = _silu(cc_ref[0:1, :])
        act_ref[9:16, :] = jnp.zeros((7, D), F32)
        part = _dot_f32(act_ref[...], aw_ref[...])
        pmine[...] = part
        pbuf[s] = part
        psend = [_remote(pmine, pbuf.at[s], s_p, r_p, j, _peer(pos, k)) for j, k in enumerate(CHIP_FLIPS)]
        for cp in psend:
            cp.start()
        for j, k in enumerate(CHIP_FLIPS):
            t = _shard_of(_peer(pos, k))
            _remote(pmine, pbuf.at[t], s_p, r_p, j, _peer(pos, k)).wait_recv()
            _remote(cw_ref, wbuf.at[t], s_w, r_w, j, _peer(pos, k)).wait_recv()
        for t in range(N_SHARD):
            mod_ref[:, t * Wm:(t + 1) * Wm] = pbuf[t] + ab_ref[:, t * Wm:(t + 1) * Wm]
            cwf_ref[:, t * Dq:(t + 1) * Dq] = wbuf[t]
        for cp in sends + psend:
            cp.wait_send()

    return pl.pallas_call(
        body, name="fwd_small",
        in_specs=[VMEM_FULL] * 5, out_specs=[VMEM_FULL] * 3,
        out_shape=[jax.ShapeDtypeStruct((16, D), F32), jax.ShapeDtypeStruct((16, 3 * D), F32),
                   jax.ShapeDtypeStruct((8, D), F32)],
        scratch_shapes=[pltpu.VMEM((N_DEV, 8, D), F32), pltpu.VMEM((16, Wm), F32),
                        pltpu.VMEM((N_SHARD, 16, Wm), F32), pltpu.VMEM((N_SHARD, 8, Dq), F32),
                        pltpu.SemaphoreType.DMA((7,)), pltpu.SemaphoreType.DMA((7,)),
                        pltpu.SemaphoreType.DMA((3,)), pltpu.SemaphoreType.DMA((3,)),
                        pltpu.SemaphoreType.DMA((3,)), pltpu.SemaphoreType.DMA((3,))],
        compiler_params=_cparams(None, VMEM_LIMIT),
    )(c8, cctx8, ada_w, ada_b, conv_w8)


AG_CHUNKS = 3


def _ag_in_proj(xm, w_in_s, w3_s, c2, s2):
    L, D = xm.shape
    Wc = w_in_s.shape[1]
    Wq = Wc // AG_CHUNKS
    Dh = D // 2
    Do = w3_s[0].shape[1]
    TM = min(1024, L // 4)
    NT = L // TM
    RC = min(128, Dh)
    NQ = AG_CHUNKS

    def body(xm_ref, wi_hbm, wa_ref, wb_ref, wo_ref, c_ref, s_ref, p_hbm, qr_hbm, kr_hbm, fi_hbm, f3_hbm,
             w_vm, cast_buf, s3, stage, qk_stage, ici_s, ici_r, d2d_s, d2d_r, w3_s_, w3_r_, loc, out_sem, qk_sem):
        pos = _position()
        c = pos[2]
        s = _shard_of(pos)
        sib = _peer(pos, 1)
        mine = pl.ds(pl.multiple_of(c * Dh, Dh), Dh)
        other = pl.ds(pl.multiple_of((1 - c) * Dh, Dh), Dh)

        def cast_half(hf):
            def step(i, carry):
                rows = pl.ds(pl.multiple_of(hf * Dh + i * RC, RC), RC)
                cp = pltpu.make_async_copy(wi_hbm.at[rows, :], cast_buf, loc.at[0])
                cp.start()
                cp.wait()
                for q in range(NQ):
                    w_vm[0, q, rows, :] = cast_buf[:, q * Wq:(q + 1) * Wq].astype(BF16)
                return carry
            lax.fori_loop(0, Dh // RC, step, 0)

        def abs_col(t, q):
            return pl.ds(pl.multiple_of(t * Wc + q * Wq, 128), Wq)

        cast_half(c)
        for a, w_ref in enumerate((wa_ref, wb_ref, wo_ref)):
            s3[a] = w_ref[...].astype(BF16)
        sends = []
        for q in range(NQ):
            for j, k in enumerate(CHIP_FLIPS):
                sends.append(_remote(w_vm.at[0, q, mine, :], w_vm.at[1 + j, q, mine, :], ici_s, ici_r,
                                     q * 3 + j, _peer(pos, k)))
        for j, k in enumerate(CHIP_FLIPS):
            sends.append(_remote(s3.at[:, c], f3_hbm.at[:, s, c], w3_s_, w3_r_, j, _peer(pos, k)))
        for cp in sends:
            cp.start()
        cast_half(1 - c)
        local = [pltpu.make_async_copy(s3, f3_hbm.at[:, s], loc.at[1])]
        local += [pltpu.make_async_copy(w_vm.at[0, q], fi_hbm.at[:, abs_col(s, q)], loc.at[2 + q]) for q in range(NQ)]
        for cp in local:
            cp.start()

        def out_copy(slot, rows, cols):
            return pltpu.make_async_copy(stage.at[slot], p_hbm.at[rows, cols], out_sem.at[slot])

        def block(r, q, t, first):
            cols = abs_col(t, q)

            def row_tile(rt, carry):
                rows = pl.ds(pl.multiple_of(rt * TM, TM), TM)
                acc = _dot(xm_ref[rows, :], w_vm[r, q])
                slot = lax.rem(rt, 2)

                @pl.when(rt >= 2 if first else rt >= 0)
                def _():
                    out_copy(slot, rows, cols).wait()

                stage[slot] = acc.astype(BF16)
                out_copy(slot, rows, cols).start()

                def rotary(lo, scale, dst_hbm):
                    c, s = c_ref[rows, :], s_ref[rows, :]
                    for pr in range(Dh // 128):
                        tq = acc[:, lo + pr * 128:lo + (pr + 1) * 128] * scale
                        qk_stage[:, pr * 128:(pr + 1) * 128] = (tq * c + _swap_halves(tq) * s).astype(BF16)
                    cp = pltpu.make_async_copy(qk_stage, dst_hbm.at[rows, :], qk_sem)
                    cp.start()
                    cp.wait()

                if q == NQ - 1:
                    @pl.when(t == 1)
                    def _():
                        rotary(Wq - Dh, 1.0, qr_hbm)
                if q == 0:
                    @pl.when(t == 2)
                    def _():
                        rotary(0, K_SCALE, kr_hbm)
                return carry

            lax.fori_loop(0, NT, row_tile, 0)

        for q in range(NQ):
            block(0, q, s, q == 0)
        passed = []
        for q in range(NQ):
            for j, k in enumerate(CHIP_FLIPS):
                r, idx = 1 + j, q * 3 + j
                t = _shard_of(_peer(pos, k))
                _remote(w_vm.at[r, q, mine, :], w_vm.at[r, q, mine, :], ici_s, ici_r, idx, sib).wait_recv()
                fwd = _remote(w_vm.at[r, q, mine, :], w_vm.at[r, q, mine, :], d2d_s, d2d_r, idx, sib)
                fwd.start()
                passed.append(fwd)
                _remote(w_vm.at[r, q, other, :], w_vm.at[r, q, other, :], d2d_s, d2d_r, idx, sib).wait_recv()
                block(r, q, t, False)
                cp = pltpu.make_async_copy(w_vm.at[r, q], fi_hbm.at[:, abs_col(t, q)], loc.at[2 + NQ + idx])
                cp.start()
                local.append(cp)
        for j, k in enumerate(CHIP_FLIPS):
            t = _shard_of(_peer(pos, k))
            _remote(s3.at[:, c], f3_hbm.at[:, t, c], w3_s_, w3_r_, j, sib).wait_recv()
            fwd = _remote(f3_hbm.at[:, t, c], f3_hbm.at[:, t, c], w3_s_, w3_r_, 3 + j, sib)
            fwd.start()
            passed.append(fwd)
        for j, k in enumerate(CHIP_FLIPS):
            t = _shard_of(_peer(pos, k))
            _remote(s3.at[:, c], f3_hbm.at[:, t, 1 - c], w3_s_, w3_r_, 3 + j, sib).wait_recv()
        for cp in sends + passed:
            cp.wait_send()
        for cp in local:
            cp.wait()
        for slot in range(2):
            out_copy(slot, pl.ds(0, TM), abs_col(s, 0)).wait()

    n_loc = 2 + NQ + 3 * NQ
    return pl.pallas_call(
        body, name="ag_in_proj",
        in_specs=[VMEM_FULL, ANY, VMEM_FULL, VMEM_FULL, VMEM_FULL, VMEM_FULL, VMEM_FULL], out_specs=[ANY] * 5,
        out_shape=[jax.ShapeDtypeStruct((L, N_SHARD * Wc), BF16),
                   jax.ShapeDtypeStruct((L, Dh), BF16), jax.ShapeDtypeStruct((L, Dh), BF16),
                   jax.ShapeDtypeStruct((D, N_SHARD * Wc), BF16), jax.ShapeDtypeStruct((3, N_SHARD, 2, Do, D), BF16)],
        scratch_shapes=[pltpu.VMEM((N_SHARD, NQ, D, Wq), BF16), pltpu.VMEM((RC, Wc), F32),
                        pltpu.VMEM((3, 2, Do, D), BF16), pltpu.VMEM((2, TM, Wq), BF16), pltpu.VMEM((TM, Dh), BF16),
                        pltpu.SemaphoreType.DMA((3 * NQ,)), pltpu.SemaphoreType.DMA((3 * NQ,)),
                        pltpu.SemaphoreType.DMA((3 * NQ,)), pltpu.SemaphoreType.DMA((3 * NQ,)),
                        pltpu.SemaphoreType.DMA((6,)), pltpu.SemaphoreType.DMA((6,)),
                        pltpu.SemaphoreType.DMA((n_loc,)), pltpu.SemaphoreType.DMA((2,)), pltpu.SemaphoreType.DMA],
        compiler_params=_cparams(None, VMEM_LIMIT),
    )(xm, w_in_s, *w3_s, c2, s2)


def _pair_exchange_w3(dw3):
    _, _, _, Do, D = dw3.shape

    def build(ins, outs, send, recv):
        pos = _position()
        return [_remote(ins[0].at[:, :, 1 - pos[2]], outs[0], send, recv, 0, _peer(pos, 1))]

    return _Exchange((dw3,), (jax.ShapeDtypeStruct((3, N_SHARD, Do, D), F32),), 1, build)


def _sum_pair_in(dw_mine, ri):
    Dh, Wf = dw_mine.shape
    Wc = Wf // N_SHARD
    tr = min(256, Dh)

    def body(a_ref, b_ref, o_ref):
        o_ref[...] = (a_ref[...] + b_ref[...]).astype(BF16)

    return pl.pallas_call(
        body, name="sum_pair_in", grid=(Dh // tr, N_SHARD),
        in_specs=[pl.BlockSpec((tr, Wc), lambda i, t: (i, t)), pl.BlockSpec((tr, Wc), lambda i, t: (i, t))],
        out_specs=pl.BlockSpec((None, tr, Wc), lambda i, t: (t, i, 0)),
        out_shape=jax.ShapeDtypeStruct((N_SHARD, Dh, Wc), BF16),
        compiler_params=_cparams(("parallel", "parallel")),
    )(dw_mine, ri)


def _sum_pair_w3(cidx, dw3, r3):
    _, _, _, Do, D = dw3.shape

    def body(c_ref, a_ref, b_ref, o_ref):
        o_ref[...] = (a_ref[...] + b_ref[...]).astype(BF16)

    return pl.pallas_call(
        body, name="sum_pair_w3",
        grid_spec=pltpu.PrefetchScalarGridSpec(
            num_scalar_prefetch=1, grid=(3,),
            in_specs=[pl.BlockSpec((None, N_SHARD, None, Do, D), lambda a, c: (a, 0, c[0], 0, 0)),
                      pl.BlockSpec((None, N_SHARD, Do, D), lambda a, c: (a, 0, 0, 0))],
            out_specs=pl.BlockSpec((None, N_SHARD, Do, D), lambda a, c: (a, 0, 0, 0))),
        out_shape=jax.ShapeDtypeStruct((3, N_SHARD, Do, D), BF16),
        compiler_params=_cparams(("parallel",)),
    )(cidx, dw3, r3)


def _chips_exchange_in(cs_in):
    _, Dh, Wc = cs_in.shape

    def build(ins, outs, send, recv):
        pos = _position()
        return [_remote(ins[0].at[_shard_of(_peer(pos, k))], outs[0].at[j], send, recv, j, _peer(pos, k))
                for j, k in enumerate(CHIP_FLIPS)]

    return _Exchange((cs_in,), (jax.ShapeDtypeStruct((3, Dh, Wc), BF16),), 3, build)


def _chips_exchange_w3(cs_3):
    _, _, Do, D = cs_3.shape

    def build(ins, outs, send, recv):
        pos = _position()
        return [_remote(ins[0].at[:, _shard_of(_peer(pos, k))], outs[0].at[j], send, recv, j, _peer(pos, k))
                for j, k in enumerate(CHIP_FLIPS)]

    return _Exchange((cs_3,), (jax.ShapeDtypeStruct((3, 3, Do, D), BF16),), 3, build)


def _sum_chips_in(csidx, cs_in, rb_in):
    _, Dh, Wc = cs_in.shape
    tr = min(256, Dh)

    def body(s_ref, a_ref, b_ref, o_ref):
        acc = a_ref[...].astype(F32)
        for j in range(3):
            acc = acc + b_ref[j].astype(F32)
        o_ref[...] = acc

    return pl.pallas_call(
        body, name="sum_chips_in",
        grid_spec=pltpu.PrefetchScalarGridSpec(
            num_scalar_prefetch=1, grid=(Dh // tr,),
            in_specs=[pl.BlockSpec((None, tr, Wc), lambda i, s: (s[1], i, 0)),
                      pl.BlockSpec((3, tr, Wc), lambda i, s: (0, i, 0))],
            out_specs=pl.BlockSpec((None, tr, Wc), lambda i, s: (s[0], i, 0))),
        out_shape=jax.ShapeDtypeStruct((2, Dh, Wc), F32),
        compiler_params=_cparams(("parallel",)),
    )(csidx, cs_in, rb_in)


def _sum_chips_w3(csidx, cs_3, rb_3):
    _, _, Do, D = cs_3.shape

    def body(s_ref, a_ref, b_ref, o_ref):
        acc = a_ref[...].astype(F32)
        for j in range(3):
            acc = acc + b_ref[j].astype(F32)
        o_ref[...] = acc

    return pl.pallas_call(
        body, name="sum_chips_w3",
        grid_spec=pltpu.PrefetchScalarGridSpec(
            num_scalar_prefetch=1, grid=(3,),
            in_specs=[pl.BlockSpec((None, None, Do, D), lambda a, s: (a, s[1], 0, 0)),
                      pl.BlockSpec((3, None, Do, D), lambda a, s: (0, a, 0, 0))],
            out_specs=pl.BlockSpec((None, None, Do, D), lambda a, s: (a, s[0], 0, 0))),
        out_shape=jax.ShapeDtypeStruct((3, 2, Do, D), F32),
        compiler_params=_cparams(("parallel",)),
    )(csidx, cs_3, rb_3)


def _rs_final(g_in, g_3):
    def body(hi_ref, h3_ref, gi_ref, g3_ref, send, recv):
        pos = _position()
        c = pos[2]
        sib = _peer(pos, 1)
        cps = [_remote(hi_ref.at[c], gi_ref.at[c], send, recv, 0, sib),
               _remote(h3_ref.at[:, c], g3_ref.at[:, c], send, recv, 1, sib)]
        for cp in cps:
            cp.start()
        _remote(hi_ref.at[1 - c], gi_ref.at[1 - c], send, recv, 0, sib).wait_recv()
        _remote(h3_ref.at[:, 1 - c], g3_ref.at[:, 1 - c], send, recv, 1, sib).wait_recv()
        for cp in cps:
            cp.wait_send()

    return pl.pallas_call(
        body, name="rs_final", in_specs=[ANY, ANY], out_specs=[ANY, ANY],
        out_shape=[jax.ShapeDtypeStruct(g_in.shape, F32), jax.ShapeDtypeStruct(g_3.shape, F32)],
        input_output_aliases={0: 0, 1: 1},
        scratch_shapes=[pltpu.SemaphoreType.DMA((2,)), pltpu.SemaphoreType.DMA((2,))],
    )(g_in, g_3)


def _adam_math(w, g, m, v):
    m = ADAM_B1 * m + (1.0 - ADAM_B1) * g
    v = ADAM_B2 * v + (1.0 - ADAM_B2) * (g * g)
    m_hat = m / (1.0 - ADAM_B1 ** ADAM_STEP)
    v_hat = v / (1.0 - ADAM_B2 ** ADAM_STEP)
    delta = -ADAM_LR * (m_hat / (jnp.sqrt(v_hat) + ADAM_EPS) + ADAM_WD * w)
    return delta, m, v


def _adamw(w, g, m, v, name):
    R, C = w.shape
    tr = min(128, R)

    def body(w_ref, g_ref, m_ref, v_ref, d_ref, nm_ref, nv_ref):
        d_ref[...], nm_ref[...], nv_ref[...] = _adam_math(w_ref[...], g_ref[...], m_ref[...], v_ref[...])

    blk = pl.BlockSpec((tr, C), lambda i: (i, 0))
    return pl.pallas_call(
        body, name=name, grid=(R // tr,), in_specs=[blk] * 4, out_specs=[blk] * 3,
        out_shape=[jax.ShapeDtypeStruct((R, C), F32)] * 3,
        compiler_params=_cparams(("parallel",), VMEM_LIMIT),
    )(w, g, m, v)


SMALL_ROWS = ("c_ctx", "norm_w", "conv_b", "gn_w", "final_norm_w")


def _bwd_small(stats, ada_w, Dq):
    D = stats[0].shape[1]
    Wm = ada_w.shape[1]

    def body(stx, stm, stc, stv, stl, stlc, aw_ref, tot_ref, dm_sh, gcw, da_ref, loss_ref,
             vec_ref, vbuf, dm, amine, abuf, s_v, r_v, s_a, r_a):
        pos = _position()
        me, s = _dev_id(pos), _shard_of(pos)
        vec_ref[...] = jnp.zeros_like(vec_ref)
        vec_ref[0:2, :] = stx[0:2, :]
        vec_ref[2:3, :] = stm[1:2, :]
        vec_ref[3:5, :] = stc[0:2, :]
        vec_ref[5:6, :] = stx[2:3, :] + stc[2:3, :]
        vec_ref[6:7, :] = stv[3:4, :]
        vec_ref[7:8, :] = stm[3:4, :]
        vec_ref[8:9, :] = stm[0:1, :]
        vec_ref[9:12, :] = stv[0:3, :]
        vec_ref[12:14, 0:128] = stl[0:2, :] + stlc[0:2, :]
        vec_ref[14:15, :] = stm[2:3, :]
        vbuf[me] = vec_ref[...]
        sends = [_remote(vec_ref, vbuf.at[me], s_v, r_v, k - 1, _peer(pos, k)) for k in range(1, 8)]
        for cp in sends:
            cp.start()
        for k in range(1, 8):
            _remote(vec_ref, vbuf.at[_dev_id(_peer(pos, k))], s_v, r_v, k - 1, _peer(pos, k)).wait_recv()
        tot = vbuf[0]
        for d in range(1, N_DEV):
            tot = tot + vbuf[d]
        loss_ref[...] = jnp.zeros((8, 128), F32) + (0.5 / D) * _sum_all(tot[14:15, :])
        dm[...] = jnp.zeros_like(dm)
        for d in range(N_DEV):
            for r in range(3):
                dm[d:d + 1, r * D:(r + 1) * D] = vbuf[d, r:r + 1, :]
        dm[8:9, 0:D] = tot[3:4, :]
        dm[8:9, D:2 * D] = tot[4:5, :]
        for t in range(N_SHARD):
            @pl.when(s == t)
            def _(t=t):
                dm_sh[...] = dm[:, t * Wm:(t + 1) * Wm]
                gcw[...] = tot[9:12, t * Dq:(t + 1) * Dq]
        tot_ref[...] = tot
        part = lax.dot_general(dm_sh[8:16, :], aw_ref[...], (((1,), (1,)), ((), ())),
                               precision=lax.Precision.HIGHEST, preferred_element_type=F32)
        amine[...] = part
        abuf[s] = part
        asend = [_remote(amine, abuf.at[s], s_a, r_a, j, _peer(pos, k)) for j, k in enumerate(CHIP_FLIPS)]
        for cp in asend:
            cp.start()
        for j, k in enumerate(CHIP_FLIPS):
            _remote(amine, abuf.at[_shard_of(_peer(pos, k))], s_a, r_a, j, _peer(pos, k)).wait_recv()
        da = abuf[0]
        for t in range(1, N_SHARD):
            da = da + abuf[t]
        da_ref[...] = da
        for cp in sends + asend:
            cp.wait_send()

    row = lambda *shape: jax.ShapeDtypeStruct(shape, F32)
    return pl.pallas_call(
        body, name="bwd_small",
        in_specs=[VMEM_FULL] * 7, out_specs=[VMEM_FULL] * 5,
        out_shape=[row(16, D), row(16, Wm), row(3, Dq), row(8, D), row(8, 128)],
        scratch_shapes=[pltpu.VMEM((16, D), F32), pltpu.VMEM((N_DEV, 16, D), F32), pltpu.VMEM((16, 3 * D), F32),
                        pltpu.VMEM((8, D), F32), pltpu.VMEM((N_SHARD, 8, D), F32),
                        pltpu.SemaphoreType.DMA((7,)), pltpu.SemaphoreType.DMA((7,)),
                        pltpu.SemaphoreType.DMA((3,)), pltpu.SemaphoreType.DMA((3,))],
        compiler_params=_cparams(None, VMEM_LIMIT),
    )(*stats, ada_w)


def _small_update(tot, dm_sh, gcw, da, act, p_row, p_ab, p_cw, p_dl):
    D = act.shape[1]
    Wm = dm_sh.shape[1]
    Dq = gcw.shape[1]

    def body(tot_ref, dm_ref, gcw_ref, da_ref, act_ref, prow, pab, pcw, pdl, gaw_ref, *outs):
        o_q = [outs[8 * q:8 * (q + 1)] for q in range(4)]
        tot = tot_ref[...]
        gaw_ref[...] = lax.dot_general(act_ref[...], dm_ref[...], (((0,), (0,)), ((), ())),
                                       precision=lax.Precision.HIGHEST, preferred_element_type=F32)
        cc = prow[0, 0:1, :]
        sg = _sigmoid(cc)
        g_cctx = da_ref[0:1, :] * (sg * (1.0 + cc * (1.0 - sg)))

        def place_all(o, val):
            o[...] = val

        def emit(k, w, g, m, v, place=place_all):
            for q, val in enumerate((g,) + _adam_math(w, g, m, v)):
                place(o_q[q][k], val)

        g_rows = [g_cctx, tot[5:6, :], tot[6:7, :], tot[7:8, :], tot[8:9, :]]
        for k, g in enumerate(g_rows):
            emit(k, prow[0, k:k + 1, :], g, prow[1, k:k + 1, :], prow[2, k:k + 1, :])

        def place_ab(o, val):
            for r in range(3):
                o[0:1, r * D:(r + 1) * D] = val[r:r + 1, :]

        g_ab = jnp.concatenate([tot[0:1, :] + tot[3:4, :], tot[1:2, :] + tot[4:5, :], tot[2:3, :]], axis=0)
        emit(5, pab[0], g_ab, pab[1], pab[2], place_ab)
        emit(6, pcw[0], gcw_ref[...], pcw[1], pcw[2])
        g_dl = jnp.concatenate([tot[12:14, 0:128] * _sigmoid(-pdl[0, 0:2, :]), jnp.zeros((6, 128), F32)], axis=0)
        emit(7, pdl[0], g_dl, pdl[1], pdl[2])

    row = lambda *shape: jax.ShapeDtypeStruct(shape, F32)
    per_q = [row(1, D)] * 5 + [row(1, 3 * D), row(3, Dq), row(8, 128)]
    res = pl.pallas_call(
        body, name="small_update",
        in_specs=[VMEM_FULL] * 9, out_specs=[VMEM_FULL] * 33,
        out_shape=[row(D, Wm)] + per_q * 4,
        compiler_params=_cparams(None, VMEM_LIMIT),
    )(tot, dm_sh, gcw, da, act, p_row, p_ab, p_cw, p_dl)
    return res[0], [res[1 + 8 * q:1 + 8 * (q + 1)] for q in range(4)]


def _pad_rows(a, rows=8):
    return jnp.pad(a, ((0, rows - a.shape[0]), (0, 0)))


def kernel(x, c, ctx, c_ctx, norm_w, ada_w, ada_b, w_in, conv_w, conv_b, decay_logit, gn_w, w_a, w_b, w_out, final_norm_w, loss_target, m_c_ctx, m_norm_w, m_ada_w, m_ada_b, m_w_in, m_conv_w, m_conv_b, m_decay_logit, m_gn_w, m_w_a, m_w_b, m_w_out, m_final_norm_w, v_c_ctx, v_norm_w, v_ada_w, v_ada_b, v_w_in, v_conv_w, v_conv_b, v_decay_logit, v_gn_w, v_w_a, v_w_b, v_w_out, v_final_norm_w):
    L, D = x.shape[1], x.shape[2]
    H = D // DV
    Wc = w_in.shape[2]
    Do = D // 8
    pos = _position()
    me = _dev_id(pos)
    cidx = jnp.reshape(pos[2], (1,)).astype(jnp.int32)
    sidx = jnp.reshape(_shard_of(pos), (1,)).astype(jnp.int32)

    act, mod, conv_w8 = _fwd_small(_pad_rows(c), _pad_rows(c_ctx[None]), ada_w[0], ada_b, _pad_rows(conv_w[0]))
    mod_x = lax.dynamic_slice_in_dim(mod, me, 1, axis=0).reshape(3, D)
    mod_c = mod[8].reshape(3, D)
    lg = jax.nn.log_sigmoid(decay_logit[0])

    w3_s = tuple(w[0].reshape(2, Do, D) for w in (w_a, w_b, w_out))

    def project(xm, c2, s2):
        p, qr, kr, w_in_full, w3_full = _ag_in_proj(xm, w_in[0], w3_s, c2, s2)
        return p, qr, kr, w_in_full, w3_full.reshape(3, D, D)

    csidx = jnp.concatenate([cidx, sidx])
    groups, dret_c, xmt, cmt, dx1, sc_x, w_in_full, gh_3, sts = _local_step(
        x[0], ctx[0], loss_target[0], mod_x, mod_c, norm_w, conv_w8, conv_b, lg, gn_w, final_norm_w[None],
        project, csidx)
    st_mid, st_conv, st_lg, st_lgc, st_c = sts

    dw_mine, ra_in = _dw_in(xmt, groups, cmt, dret_c, D, True)
    cs_in = _sum_pair_in(dw_mine, ra_in)
    grad_x, st_x, rb_in = _dxm(groups, 0, w_in_full, x[0], norm_w, sc_x, dx1, "dxm_x", _chips_exchange_in(cs_in))
    gh_in = _sum_chips_in(csidx, cs_in, rb_in)
    g_in, g_3 = _rs_final(gh_in, gh_3)
    g_w_in = g_in.reshape(D, Wc)
    g_3 = g_3.reshape(3, D // 4, D)

    zeros3 = jnp.zeros((3, D), F32)
    p_row = jnp.concatenate(
        [r for t in ((c_ctx[None], norm_w, conv_b, gn_w, final_norm_w[None], zeros3),
                     (m_c_ctx[None], m_norm_w, m_conv_b, m_gn_w, m_final_norm_w[None], zeros3),
                     (v_c_ctx[None], v_norm_w, v_conv_b, v_gn_w, v_final_norm_w[None], zeros3)) for r in t],
        axis=0).reshape(3, 8, D)
    p_ab = jnp.concatenate([ada_b, m_ada_b, v_ada_b], axis=0).reshape(3, 3, D)
    p_cw = jnp.concatenate([conv_w, m_conv_w, v_conv_w], axis=0)
    p_dl = jnp.pad(jnp.concatenate([decay_logit, m_decay_logit, v_decay_logit], axis=0), ((0, 0), (0, 6), (0, 128 - H)))
    tot, dm_sh, gcw, da, loss_t = _bwd_small((st_x, st_mid, st_c, st_conv, st_lg, st_lgc), ada_w[0],
                                             conv_w.shape[2])
    g_ada_w, small = _small_update(tot, dm_sh, gcw, da, act, p_row, p_ab, p_cw, p_dl)

    upd_in = _adamw(w_in[0], g_w_in, m_w_in[0], v_w_in[0], "adamw_w_in")
    upd_ada = _adamw(ada_w[0], g_ada_w, m_ada_w[0], v_ada_w[0], "adamw_ada_w")
    upd_a = _adamw(w_a[0], g_3[0], m_w_a[0], v_w_a[0], "adamw_w_a")
    upd_b = _adamw(w_b[0], g_3[1], m_w_b[0], v_w_b[0], "adamw_w_b")
    upd_o = _adamw(w_out[0], g_3[2], m_w_out[0], v_w_out[0], "adamw_w_out")

    def leaves(q):
        big = lambda g, upd: (g if q == 0 else upd[q - 1])[None]
        r_cctx, r_norm, r_convb, r_gn, r_fnorm, r_ab, r_cw, r_dl = small[q]
        return [r_cctx.reshape(D), r_norm, big(g_ada_w, upd_ada), r_ab, big(g_w_in, upd_in),
                r_cw[None], r_convb, r_dl[0:2, 0:H][None], r_gn,
                big(g_3[0], upd_a), big(g_3[1], upd_b), big(g_3[2], upd_o), r_fnorm.reshape(D)]

    loss = loss_t[0, 0]
    return (loss, grad_x[None], *leaves(0), *leaves(1), *leaves(2), *leaves(3))
```

```python
from typing import Callable, NamedTuple

import jax
import jax.numpy as jnp
from jax import lax
from jax.experimental import pallas as pl
from jax.experimental.pallas import tpu as pltpu

F32 = jnp.float32
BF16 = jnp.bfloat16
MESH = pl.DeviceIdType.MESH

CHUNK = 128
RET_CPB = 4
DV = 128
DK = 64
GRID_W = 64
ROPE_BASE = 10000.0
EPS = 1e-6
K_SCALE = DK ** -0.5
N_SHARD = 4
N_DEV = 8

ADAM_LR = 0.001
ADAM_B1 = 0.9
ADAM_B2 = 0.999
ADAM_EPS = 1e-08
ADAM_WD = 0.01
ADAM_STEP = 10

VMEM_LIMIT = 56 * 1024 * 1024


def _cparams(sem=None, vmem=None):
    kw = {}
    if sem is not None:
        kw["dimension_semantics"] = sem
    if vmem is not None:
        kw["vmem_limit_bytes"] = vmem
    return pltpu.CompilerParams(**kw)


def _dot(a, b):
    return jnp.dot(a, b, preferred_element_type=F32)


def _dot_nt(a, b):
    return lax.dot_general(a, b, (((1,), (1,)), ((), ())), preferred_element_type=F32)


def _dot_tn(a, b):
    return lax.dot_general(a, b, (((0,), (0,)), ((), ())), preferred_element_type=F32)


def _sigmoid(x):
    return 1.0 / (1.0 + jnp.exp(-x))


def _sum_all(x):
    return jnp.sum(jnp.sum(x, axis=1, keepdims=True), axis=0, keepdims=True)


def _swap_halves(t):
    n = t.shape[1]
    lane = lax.broadcasted_iota(jnp.int32, t.shape, 1)
    low = (lane & 32) == 0
    return jnp.where(low, pltpu.roll(t, n - 32, 1), pltpu.roll(t, 32, 1))


def _vec_spec(d):
    return pl.BlockSpec((1, d), lambda *a: (0, 0))


def _norm_mod(x, nw, sc, sh, name):
    L, D = x.shape
    tl = min(256, L)

    def body(x_ref, nw_ref, sc_ref, sh_ref, xm_ref, xmt_ref):
        xv = x_ref[...]
        r = lax.rsqrt(jnp.mean(xv * xv, axis=-1, keepdims=True) + EPS)
        xm = (xv * r * nw_ref[...]) * (1.0 + sc_ref[...]) + sh_ref[...]
        xm_ref[...] = xm.astype(BF16)
        xmt_ref[...] = xm.T.astype(BF16)

    return pl.pallas_call(
        body, name=name, grid=(L // tl,),
        in_specs=[pl.BlockSpec((tl, D), lambda i: (i, 0)), _vec_spec(D), _vec_spec(D), _vec_spec(D)],
        out_specs=[pl.BlockSpec((tl, D), lambda i: (i, 0)), pl.BlockSpec((D, tl), lambda i: (0, i))],
        out_shape=[jax.ShapeDtypeStruct((L, D), BF16), jax.ShapeDtypeStruct((D, L), BF16)],
        compiler_params=_cparams(("parallel",)),
    )(x, nw, sc, sh)


QK_BLOCK, V_BLOCK = 4, 5


def _in_proj(xm, w, name, first=0, count=None):
    M, D = xm.shape
    count = w.shape[1] // D if count is None else count
    tm = min(1024, M)

    def body(a_ref, b_ref, o_ref, qk_ref):
        acc = _dot(a_ref[...], b_ref[...])
        o_ref[...] = acc.astype(o_ref.dtype)

        @pl.when(pl.program_id(1) == QK_BLOCK - first)
        def _():
            qk_ref[...] = acc

    return pl.pallas_call(
        body, name=name, grid=(M // tm, count),
        in_specs=[pl.BlockSpec((tm, D), lambda i, j: (i, 0)), pl.BlockSpec((D, D), lambda i, j: (0, first + j))],
        out_specs=[pl.BlockSpec((tm, D), lambda i, j: (i, j)), pl.BlockSpec((tm, D), lambda i, j: (i, 0))],
        out_shape=[jax.ShapeDtypeStruct((M, count * D), BF16), jax.ShapeDtypeStruct((M, D), F32)],
        compiler_params=_cparams(("parallel", "arbitrary")),
    )(xm, w)


def _halo_specs(tl, L, D, col):
    hb = tl // 16
    last = L // 16 - 1
    prev = pl.BlockSpec((16, D), lambda i: (jnp.maximum(i * hb - 1, 0), col))
    nxt = pl.BlockSpec((16, D), lambda i: (jnp.minimum((i + 1) * hb, last), col))
    return prev, nxt


def _shift_rows(u, above, below):
    tl = u.shape[0]
    row = lax.broadcasted_iota(jnp.int32, u.shape, 0)
    dn = jnp.where(row == 0, above, pltpu.roll(u, 1, 0))
    up = jnp.where(row == tl - 1, below, pltpu.roll(u, tl - 1, 0))
    return dn, up


def _rope_tables(L):
    pos = jnp.arange(L)
    row = (pos // GRID_W).astype(F32)
    col = (pos % GRID_W).astype(F32)
    nf = DK // 4
    inv = ROPE_BASE ** (-jnp.arange(nf, dtype=F32) / nf)
    ang = jnp.concatenate([row[:, None] * inv, col[:, None] * inv], axis=-1)
    cos, sin = jnp.cos(ang), jnp.sin(ang)
    return jnp.concatenate([cos, cos, cos, cos], axis=-1), jnp.concatenate([-sin, sin, -sin, sin], axis=-1)


def _smem_spec():
    return pl.BlockSpec(memory_space=pltpu.SMEM)


def _pair_select(e0, e1):
    row = lax.broadcasted_iota(jnp.int32, e0.shape, 0)
    return jnp.where(row < DK, e0, e1)


def _head_lane_mask(shape, e):
    lane = lax.broadcasted_iota(jnp.int32, shape, 1)
    return (lane < DK) if e == 0 else (lane >= DK)


def _ctx_states(pc, pqk_c, lg, D):
    Lc = pc.shape[0]
    H = D // DV

    def body(lg_ref, k_ref, v_ref, s_ref):
        m = lax.broadcasted_iota(jnp.int32, (Lc, DV), 0).astype(F32)
        for pr in range(H // 2):
            k2 = k_ref[:, pr * 128:(pr + 1) * 128].astype(F32) * K_SCALE
            res = [[None, None], [None, None]]
            for e in range(2):
                h = 2 * pr + e
                v = v_ref[:, h * DV:(h + 1) * DV]
                dec_f = jnp.exp(lg_ref[0, h] * (Lc - 1.0 - m))
                dec_b = jnp.exp(lg_ref[1, h] * m)
                res[0][e] = _dot_tn((k2 * dec_f).astype(BF16), v)
                res[1][e] = _dot_tn((k2 * dec_b).astype(BF16), v)
            s_ref[0, pr] = _pair_select(res[0][0], res[0][1])
            s_ref[1, pr] = _pair_select(res[1][0], res[1][1])

    return pl.pallas_call(
        body, name="ctx_states", grid=(1,),
        in_specs=[_smem_spec(), pl.BlockSpec((Lc, D // 2), lambda i: (0, 1)), pl.BlockSpec((Lc, D), lambda i: (0, 1))],
        out_specs=pl.BlockSpec((2, H // 2, 128, 128), lambda i: (0, 0, 0, 0)),
        out_shape=jax.ShapeDtypeStruct((2, H // 2, 128, 128), F32),
    )(lg, pqk_c, pc)


T_M, T_MT = 0, 1
T_MF1, T_MB1 = 2, 3
T_QF, T_QB = 4, 5
T_KF, T_KB = 6, 7


def _decay_tables(lg, H):
    def body(lg_ref, t_ref):
        h = pl.program_id(0)
        lgf, lgb = lg_ref[0, h], lg_ref[1, h]
        i = lax.broadcasted_iota(jnp.int32, (CHUNK, CHUNK), 0).astype(F32)
        j = lax.broadcasted_iota(jnp.int32, (CHUNK, CHUNK), 1).astype(F32)
        d = i - j
        mf = jnp.where(d > 0, jnp.exp(lgf * jnp.maximum(d, 0.0)), 0.0)
        mb = jnp.where(d < 0, jnp.exp(lgb * jnp.maximum(-d, 0.0)), 0.0)
        mf_t = jnp.where(d < 0, jnp.exp(lgf * jnp.maximum(-d, 0.0)), 0.0)
        mb_t = jnp.where(d > 0, jnp.exp(lgb * jnp.maximum(d, 0.0)), 0.0)
        diag = jnp.where(d == 0, 2.0, 0.0)
        t_ref[0, T_M] = mf + mb + diag
        t_ref[0, T_MT] = mf_t + mb_t + diag
        t_ref[0, T_MF1] = mf * d
        t_ref[0, T_MB1] = mb * (-d)
        t_ref[0, T_QF] = jnp.exp(lgf * (i + 1.0))
        t_ref[0, T_QB] = jnp.exp(lgb * (CHUNK - i))
        t_ref[0, T_KF] = jnp.exp(lgf * (CHUNK - 1.0 - i))
        t_ref[0, T_KB] = jnp.exp(lgb * i)

    return pl.pallas_call(
        body, name="decay_tables", grid=(H,), in_specs=[_smem_spec()],
        out_specs=pl.BlockSpec((1, 8, CHUNK, CHUNK), lambda h: (h, 0, 0, 0)),
        out_shape=jax.ShapeDtypeStruct((H, 8, CHUNK, CHUNK), F32),
    )(lg)


def _tab_spec(H):
    return pl.BlockSpec((H, 8, CHUNK, CHUNK), lambda n: (0, 0, 0, 0))


def _chunk_decay(tab_ref, h):
    return tab_ref[h, T_QF, CHUNK - 1:CHUNK, :], tab_ref[h, T_QB, 0:1, :]


def _ret_states(kr, p, s0, tab, D, exchange=None):
    L = kr.shape[0]
    H = D // DV
    N = L // CHUNK
    HP = H // 2

    def body(tab_ref, kf_ref, kb_ref, vf_ref, vb_ref, s0_ref, sf_out, sb_out, sf, sb):
        n = pl.program_id(0)

        @pl.when(n == 0)
        def _():
            sf[...] = s0_ref[0]
            sb[...] = s0_ref[1]

        for cc in range(RET_CPB):
            cf_, cb_ = cc, RET_CPB - 1 - cc
            rf, rb = slice(cf_ * CHUNK, (cf_ + 1) * CHUNK), slice(cb_ * CHUNK, (cb_ + 1) * CHUNK)
            sf_out[cf_] = sf[...]
            sb_out[cb_] = sb[...]
            for pr in range(HP):
                kf2 = kf_ref[rf, pr * 128:(pr + 1) * 128].astype(F32)
                kb2 = kb_ref[rb, pr * 128:(pr + 1) * 128].astype(F32)
                inc_f, inc_b, gf, gb = [], [], [], []
                for e in range(2):
                    h = 2 * pr + e
                    inc_f.append(_dot_tn((kf2 * tab_ref[h, T_KF]).astype(BF16), vf_ref[rf, h * DV:(h + 1) * DV]))
                    inc_b.append(_dot_tn((kb2 * tab_ref[h, T_KB]).astype(BF16), vb_ref[rb, h * DV:(h + 1) * DV]))
                    cf, cb = _chunk_decay(tab_ref, h)
                    gf.append(jnp.broadcast_to(cf, (128, 128)))
                    gb.append(jnp.broadcast_to(cb, (128, 128)))
                sf[pr] = _pair_select(gf[0], gf[1]) * sf[pr] + _pair_select(inc_f[0], inc_f[1])
                sb[pr] = _pair_select(gb[0], gb[1]) * sb[pr] + _pair_select(inc_b[0], inc_b[1])

    st = jax.ShapeDtypeStruct((N, HP, 128, 128), F32)
    R = RET_CPB * CHUNK
    NB = N // RET_CPB
    return _riding_call(
        body, exchange, NB, name="ret_states", args=(tab, kr, kr, p, p, s0),
        in_specs=[_tab_spec(H),
                  pl.BlockSpec((R, D // 2), lambda n: (n, 0)),
                  pl.BlockSpec((R, D // 2), lambda n: (NB - 1 - n, 0)),
                  pl.BlockSpec((R, D), lambda n: (n, 5)),
                  pl.BlockSpec((R, D), lambda n: (NB - 1 - n, 5)),
                  pl.BlockSpec((2, HP, 128, 128), lambda n: (0, 0, 0, 0))],
        out_specs=[pl.BlockSpec((RET_CPB, HP, 128, 128), lambda n: (n, 0, 0, 0)),
                   pl.BlockSpec((RET_CPB, HP, 128, 128), lambda n: (NB - 1 - n, 0, 0, 0))],
        out_shape=[st, st],
        scratch=[pltpu.VMEM((HP, 128, 128), F32), pltpu.VMEM((HP, 128, 128), F32)],
        cparams=_cparams(("arbitrary",)))


def _ret_out(qr, kr, p, sf_prev, sb_prev, gn_w, tab, D, exchange=None):
    L = qr.shape[0]
    H = D // DV
    N = L // CHUNK
    HP = H // 2

    def body(tab_ref, q_ref, k_ref, v_ref, zb_ref, sf_ref, sb_ref, gn_ref, o_ref, yb_ref):
        def chunk(cc, carry):
            rows = pl.ds(pl.multiple_of(cc * CHUNK, CHUNK), CHUNK)
            for pr in range(HP):
                q2 = q_ref[rows, pr * 128:(pr + 1) * 128]
                k2 = k_ref[rows, pr * 128:(pr + 1) * 128]
                sfp = sf_ref[cc, pr].astype(BF16)
                sbp = sb_ref[cc, pr].astype(BF16)
                for e in range(2):
                    h = 2 * pr + e
                    sl = slice(h * DV, (h + 1) * DV)
                    qm = jnp.where(_head_lane_mask(q2.shape, e), q2, jnp.zeros_like(q2))
                    a = (_dot_nt(qm, k2) * tab_ref[h, T_M]).astype(BF16)
                    qf = qm.astype(F32)
                    o = _dot(a, v_ref[rows, sl])
                    o += _dot((qf * tab_ref[h, T_QF]).astype(BF16), sfp)
                    o += _dot((qf * tab_ref[h, T_QB]).astype(BF16), sbp)
                    o_ref[rows, sl] = o
                    mu = jnp.mean(o, axis=-1, keepdims=True)
                    oc = o - mu
                    rstd = lax.rsqrt(jnp.mean(oc * oc, axis=-1, keepdims=True) + EPS)
                    zb = zb_ref[rows, sl].astype(F32)
                    yb_ref[rows, sl] = (zb * _sigmoid(zb) * (oc * rstd * gn_ref[:, sl])).astype(BF16)
            return carry

        lax.fori_loop(0, RET_CPB, chunk, 0)

    R = RET_CPB * CHUNK
    return _riding_call(
        body, exchange, N // RET_CPB, name="ret_out", args=(tab, qr, kr, p, p, sf_prev, sb_prev, gn_w),
        in_specs=[_tab_spec(H),
                  pl.BlockSpec((R, D // 2), lambda n: (n, 0)),
                  pl.BlockSpec((R, D // 2), lambda n: (n, 0)),
                  pl.BlockSpec((R, D), lambda n: (n, 5)),
                  pl.BlockSpec((R, D), lambda n: (n, 6)),
                  pl.BlockSpec((RET_CPB, HP, 128, 128), lambda n: (n, 0, 0, 0)),
                  pl.BlockSpec((RET_CPB, HP, 128, 128), lambda n: (n, 0, 0, 0)),
                  _vec_spec(D)],
        out_specs=[pl.BlockSpec((R, D), lambda n: (n, 0)), pl.BlockSpec((R, D), lambda n: (n, 0))],
        out_shape=[jax.ShapeDtypeStruct((L, D), F32), jax.ShapeDtypeStruct((L, D), BF16)],
        cparams=_cparams(("arbitrary",)))


def _mid(p, yb, o, x, tgt, w3, g, fw, conv_w, conv_b, gn_w, D):
    L = x.shape[0]
    H = D // DV
    tm = min(256, L)
    nt = L // tm

    def body(h_ref, bg_ref, cg_ref, za_ref, hp_ref, hn_ref, cp_ref, cn_ref, yb_ref, ga_ref, gb_ref, zb_ref, o_ref,
             x_ref, t_ref, w_hbm, g_ref, fw_ref, cw_ref, cb_ref, gn_ref,
             dx1_ref, dya_ref, do_ref, dzb_ref, dgab_ref, dw_hbm, st_ref, w_vm, dw_acc, sem):
        i = pl.program_id(0)

        @pl.when(i == 0)
        def _():
            cp = pltpu.make_async_copy(w_hbm, w_vm, sem)
            cp.start()
            dw_acc[...] = jnp.zeros_like(dw_acc)
            st_ref[...] = jnp.zeros_like(st_ref)
            cp.wait()

        u = cg_ref[...].astype(F32) * h_ref[...].astype(F32)
        above = jnp.where(i == 0, 0.0, cp_ref[15:16, :].astype(F32) * hp_ref[15:16, :].astype(F32))
        below = jnp.where(i == nt - 1, 0.0, cn_ref[0:1, :].astype(F32) * hn_ref[0:1, :].astype(F32))
        dn, up = _shift_rows(u, above, below)
        co = cw_ref[0:1, :] * dn + cw_ref[1:2, :] * u + cw_ref[2:3, :] * up + cb_ref[...]
        za = za_ref[...].astype(F32)
        ya_b = (za * _sigmoid(za) * bg_ref[...].astype(F32) * co).astype(BF16)
        yb_b = yb_ref[...]
        y_a = _dot(ya_b, w_vm[0])
        y_b = _dot(yb_b, w_vm[1])
        sga = _sigmoid(ga_ref[...].astype(F32))
        sgb = _sigmoid(gb_ref[...].astype(F32))
        mix_b = (sga * y_a + sgb * y_b).astype(BF16)
        y_x = _dot(mix_b, w_vm[2])
        gvec, fwv = g_ref[...], fw_ref[...]
        x1 = x_ref[...] + gvec * y_x
        r1 = lax.rsqrt(jnp.mean(x1 * x1, axis=-1, keepdims=True) + EPS)
        xh = x1 * r1
        diff = xh * fwv - t_ref[...]
        dout = diff * (1.0 / D)
        dxh = dout * fwv
        dx1 = r1 * (dxh - xh * jnp.mean(dxh * xh, axis=-1, keepdims=True))
        dx1_ref[...] = dx1
        st_ref[0:1, :] += jnp.sum(dout * xh, axis=0, keepdims=True)
        st_ref[1:2, :] += jnp.sum(dx1 * y_x, axis=0, keepdims=True)
        st_ref[2:3, :] += jnp.sum(diff * diff, axis=0, keepdims=True)
        dyx_b = (dx1 * gvec).astype(BF16)
        dmix = _dot_nt(dyx_b, w_vm[2])
        dw_acc[2] += _dot_tn(mix_b, dyx_b)
        dya_b = (dmix * sga).astype(BF16)
        dyb_b = (dmix * sgb).astype(BF16)
        dgab_ref[:, 0:D] = (dmix * y_a * sga * (1.0 - sga)).astype(BF16)
        dgab_ref[:, D:2 * D] = (dmix * y_b * sgb * (1.0 - sgb)).astype(BF16)
        dya_ref[...] = _dot_nt(dya_b, w_vm[0])
        dyb = _dot_nt(dyb_b, w_vm[1])
        dw_acc[0] += _dot_tn(ya_b, dya_b)
        dw_acc[1] += _dot_tn(yb_b, dyb_b)

        for h in range(H):
            sl = slice(h * DV, (h + 1) * DV)
            ov = o_ref[:, sl]
            oc = ov - jnp.mean(ov, axis=-1, keepdims=True)
            rstd = lax.rsqrt(jnp.mean(oc * oc, axis=-1, keepdims=True) + EPS)
            rn = oc * rstd
            gw = gn_ref[:, sl]
            zb = zb_ref[:, sl].astype(F32)
            sz = _sigmoid(zb)
            dy = dyb[:, sl]
            dzb_ref[:, sl] = (dy * (rn * gw) * (sz * (1.0 + zb * (1.0 - sz)))).astype(BF16)
            dretn = dy * (zb * sz)
            st_ref[3:4, sl] += jnp.sum(dretn * rn, axis=0, keepdims=True)
            drn = dretn * gw
            do_ref[:, sl] = (rstd * (drn - jnp.mean(drn, axis=-1, keepdims=True)
                                     - rn * jnp.mean(drn * rn, axis=-1, keepdims=True))).astype(BF16)

        @pl.when(i == nt - 1)
        def _():
            out = pltpu.make_async_copy(dw_acc, dw_hbm, sem)
            out.start()
            out.wait()

    row = lambda col: pl.BlockSpec((tm, D), lambda i: (i, col))
    any_spec = pl.BlockSpec(memory_space=pl.ANY)
    f32o = jax.ShapeDtypeStruct((L, D), F32)
    bf16o = jax.ShapeDtypeStruct((L, D), BF16)
    hp, hn = _halo_specs(tm, L, D, 0)
    cp, cn = _halo_specs(tm, L, D, 2)
    return pl.pallas_call(
        body, name="mid", grid=(nt,),
        in_specs=[row(0), row(1), row(2), row(3), hp, hn, cp, cn, row(0), row(7), row(8), row(6), row(0),
                  row(0), row(0), any_spec, _vec_spec(D), _vec_spec(D),
                  pl.BlockSpec((8, D), lambda i: (0, 0)), _vec_spec(D), _vec_spec(D)],
        out_specs=[row(0), row(0), row(0), row(0), pl.BlockSpec((tm, 2 * D), lambda i: (i, 0)), any_spec,
                   pl.BlockSpec((8, D), lambda i: (0, 0))],
        out_shape=[f32o, f32o, bf16o, bf16o, jax.ShapeDtypeStruct((L, 2 * D), BF16),
                   jax.ShapeDtypeStruct((3, D, D), F32), jax.ShapeDtypeStruct((8, D), F32)],
        scratch_shapes=[pltpu.VMEM((3, D, D), BF16), pltpu.VMEM((3, D, D), F32), pltpu.SemaphoreType.DMA],
        compiler_params=_cparams(("arbitrary",), VMEM_LIMIT),
    )(p, p, p, p, p, p, p, p, yb, p, p, p, o, x, tgt, w3, g, fw, conv_w, conv_b, gn_w)


def _conv_bwd(dya, p, conv_w, conv_b, D, exchange=None):
    L = p.shape[0]
    tl = min(256, L)
    nt = L // tl

    def body(d_ref, h_ref, bg_ref, cg_ref, za_ref,
             dp_ref, dn_ref, hp_ref, hn_ref, bp_ref, bn_ref, cp_ref, cn_ref, zp_ref, zn_ref,
             w_ref, b_ref, dc_ref, st_ref):
        i = pl.program_id(0)

        @pl.when(i == 0)
        def _():
            st_ref[...] = jnp.zeros_like(st_ref)

        first, last = i == 0, i == nt - 1
        h = h_ref[...].astype(F32)
        cg = cg_ref[...].astype(F32)
        bg = bg_ref[...].astype(F32)
        za = za_ref[...].astype(F32)
        dy = d_ref[...].astype(F32)
        u = cg * h
        u_above = jnp.where(first, 0.0, cp_ref[15:16, :].astype(F32) * hp_ref[15:16, :].astype(F32))
        u_below = jnp.where(last, 0.0, cn_ref[0:1, :].astype(F32) * hn_ref[0:1, :].astype(F32))
        u_dn, u_up = _shift_rows(u, u_above, u_below)
        w0, w1, w2 = w_ref[0:1, :], w_ref[1:2, :], w_ref[2:3, :]
        co = w0 * u_dn + w1 * u + w2 * u_up + b_ref[...]
        sz = _sigmoid(za)
        silu = za * sz
        dc_ref[:, 3 * D:4 * D] = (dy * bg * co * (sz * (1.0 + za * (1.0 - sz)))).astype(BF16)
        dc_ref[:, D:2 * D] = (dy * silu * co).astype(BF16)
        dco = dy * silu * bg

        def edge(dr, zr, br, r):
            z = zr[r:r + 1, :].astype(F32)
            return dr[r:r + 1, :].astype(F32) * (z * _sigmoid(z)) * br[r:r + 1, :].astype(F32)

        dco_above = jnp.where(first, 0.0, edge(dp_ref, zp_ref, bp_ref, 15))
        dco_below = jnp.where(last, 0.0, edge(dn_ref, zn_ref, bn_ref, 0))
        dco_dn, dco_up = _shift_rows(dco, dco_above, dco_below)
        du = w0 * dco_up + w1 * dco + w2 * dco_dn
        dc_ref[:, 2 * D:3 * D] = (du * h).astype(BF16)
        dc_ref[:, 0:D] = (du * cg).astype(BF16)
        st_ref[0:1, :] += jnp.sum(dco * u_dn, axis=0, keepdims=True)
        st_ref[1:2, :] += jnp.sum(dco * u, axis=0, keepdims=True)
        st_ref[2:3, :] += jnp.sum(dco * u_up, axis=0, keepdims=True)
        st_ref[3:4, :] += jnp.sum(dco, axis=0, keepdims=True)

    main = lambda col: pl.BlockSpec((tl, D), lambda i: (i, col))
    halos = []
    for col in (0, 0, 1, 2, 3):
        halos.extend(_halo_specs(tl, L, D, col))
    return _riding_call(
        body, exchange, nt, name="conv_bwd",
        args=(dya, p, p, p, p, dya, dya, p, p, p, p, p, p, p, p, conv_w, conv_b),
        in_specs=[main(0), main(0), main(1), main(2), main(3)] + halos
                 + [pl.BlockSpec((8, D), lambda i: (0, 0)), _vec_spec(D)],
        out_specs=[pl.BlockSpec((tl, 4 * D), lambda i: (i, 0)), pl.BlockSpec((8, D), lambda i: (0, 0))],
        out_shape=[jax.ShapeDtypeStruct((L, 4 * D), BF16), jax.ShapeDtypeStruct((8, D), F32)],
        cparams=_cparams(("arbitrary",)))


def _ret_bwd_states(qr, do, tab, D):
    L = qr.shape[0]
    H = D // DV
    N = L // CHUNK
    HP = H // 2

    def body(tab_ref, qf_ref, qb_ref, dof_ref, dob_ref, dsf_out, dsb_out, ds0_out, dsf, dsb):
        n = pl.program_id(0)

        @pl.when(n == 0)
        def _():
            dsf[...] = jnp.zeros_like(dsf)
            dsb[...] = jnp.zeros_like(dsb)

        for cc in range(RET_CPB):
            cf_, cb_ = RET_CPB - 1 - cc, cc
            rf, rb = slice(cf_ * CHUNK, (cf_ + 1) * CHUNK), slice(cb_ * CHUNK, (cb_ + 1) * CHUNK)
            dsf_out[cf_] = dsf[...]
            dsb_out[cb_] = dsb[...]
            for pr in range(HP):
                qf2 = qf_ref[rf, pr * 128:(pr + 1) * 128].astype(F32)
                qb2 = qb_ref[rb, pr * 128:(pr + 1) * 128].astype(F32)
                inc_f, inc_b, gf, gb = [], [], [], []
                for e in range(2):
                    h = 2 * pr + e
                    inc_f.append(_dot_tn((qf2 * tab_ref[h, T_QF]).astype(BF16), dof_ref[rf, h * DV:(h + 1) * DV]))
                    inc_b.append(_dot_tn((qb2 * tab_ref[h, T_QB]).astype(BF16), dob_ref[rb, h * DV:(h + 1) * DV]))
                    cf, cb = _chunk_decay(tab_ref, h)
                    gf.append(jnp.broadcast_to(cf, (128, 128)))
                    gb.append(jnp.broadcast_to(cb, (128, 128)))
                dsf[pr] = _pair_select(gf[0], gf[1]) * dsf[pr] + _pair_select(inc_f[0], inc_f[1])
                dsb[pr] = _pair_select(gb[0], gb[1]) * dsb[pr] + _pair_select(inc_b[0], inc_b[1])

        @pl.when(n == NB - 1)
        def _():
            ds0_out[0] = dsf[...]
            ds0_out[1] = dsb[...]

    st = jax.ShapeDtypeStruct((N, HP, 128, 128), F32)
    R = RET_CPB * CHUNK
    NB = N // RET_CPB
    return pl.pallas_call(
        body, name="ret_bwd_states", grid=(NB,),
        in_specs=[_tab_spec(H),
                  pl.BlockSpec((R, D // 2), lambda n: (NB - 1 - n, 0)),
                  pl.BlockSpec((R, D // 2), lambda n: (n, 0)),
                  pl.BlockSpec((R, D), lambda n: (NB - 1 - n, 0)),
                  pl.BlockSpec((R, D), lambda n: (n, 0))],
        out_specs=[pl.BlockSpec((RET_CPB, HP, 128, 128), lambda n: (NB - 1 - n, 0, 0, 0)),
                   pl.BlockSpec((RET_CPB, HP, 128, 128), lambda n: (n, 0, 0, 0)),
                   pl.BlockSpec((2, HP, 128, 128), lambda n: (0, 0, 0, 0))],
        out_shape=[st, st, jax.ShapeDtypeStruct((2, HP, 128, 128), F32)],
        scratch_shapes=[pltpu.VMEM((HP, 128, 128), F32), pltpu.VMEM((HP, 128, 128), F32)],
        compiler_params=_cparams(("arbitrary",)),
    )(tab, qr, qr, do, do)


def _ret_bwd_main(qr, kr, p, do, sf_prev, sb_prev, dsf, dsb, c2, s2, tab, D, exchange=None):
    L = qr.shape[0]
    H = D // DV
    N = L // CHUNK
    HP = H // 2
    W = D // 2

    def body(tab_ref, q_ref, k_ref, v_ref, do_ref, sf_ref, sb_ref, dsf_ref, dsb_ref, c_ref, s_ref,
             dr_ref, st_ref, dl_acc):
        @pl.when(pl.program_id(0) == 0)
        def _():
            dl_acc[...] = jnp.zeros_like(dl_acc)

        i = lax.broadcasted_iota(jnp.int32, (CHUNK, 128), 0).astype(F32)
        rowid = lax.broadcasted_iota(jnp.int32, (128, 128), 0)

        def chunk(cc, carry):
            rows = pl.ds(pl.multiple_of(cc * CHUNK, CHUNK), CHUNK)
            c, s = c_ref[rows, :], s_ref[rows, :]
            for pr in range(HP):
                ps = slice(pr * 128, (pr + 1) * 128)
                q2, k2 = q_ref[rows, ps], k_ref[rows, ps]
                sf32, sb32 = sf_ref[cc, pr], sb_ref[cc, pr]
                dsf32, dsb32 = dsf_ref[cc, pr], dsb_ref[cc, pr]
                sfp, sbp = sf32.astype(BF16), sb32.astype(BF16)
                dsfp, dsbp = dsf32.astype(BF16), dsb32.astype(BF16)
                dq2 = jnp.zeros((CHUNK, 128), F32)
                dk2 = jnp.zeros((CHUNK, 128), F32)
                for e in range(2):
                    h = 2 * pr + e
                    sl = slice(h * DV, (h + 1) * DV)
                    hm = _head_lane_mask(q2.shape, e)
                    qm = jnp.where(hm, q2, jnp.zeros_like(q2))
                    km = jnp.where(hm, k2, jnp.zeros_like(k2))
                    qf, kf = qm.astype(F32), km.astype(F32)
                    v, do = v_ref[rows, sl], do_ref[rows, sl]
                    vf, dof = v.astype(F32), do.astype(F32)
                    m_t = tab_ref[h, T_MT]
                    sc = _dot_nt(qm, k2)
                    dpm = _dot_nt(do, v)
                    dsc = (dpm * tab_ref[h, T_M]).astype(BF16)
                    a_t = (_dot_nt(km, q2) * m_t).astype(BF16)
                    dsc_t = (_dot_nt(v, do) * m_t).astype(BF16)
                    dq_f, dq_b = tab_ref[h, T_QF], tab_ref[h, T_QB]
                    dk_f, dk_b = tab_ref[h, T_KF], tab_ref[h, T_KB]
                    dq = _dot(dsc, km)
                    dq += jnp.where(hm, dq_f * _dot_nt(do, sfp) + dq_b * _dot_nt(do, sbp), 0.0)
                    dk = _dot(dsc_t, qm)
                    dk += jnp.where(hm, dk_f * _dot_nt(v, dsfp) + dk_b * _dot_nt(v, dsbp), 0.0)
                    kdf = _dot((kf * dk_f).astype(BF16), dsfp)
                    kdb = _dot((kf * dk_b).astype(BF16), dsbp)
                    dr_ref[rows, D + h * DV:D + (h + 1) * DV] = (_dot(a_t, do) + kdf + kdb).astype(BF16)
                    dq2 += dq
                    dk2 += dk
                    xf = _dot((qf * dq_f).astype(BF16), sfp)
                    xb = _dot((qf * dq_b).astype(BF16), sbp)
                    pair = (rowid < DK) if e == 0 else (rowid >= DK)
                    gcf, gcb = tab_ref[h, T_QF, CHUNK - 1:CHUNK, 0:1], tab_ref[h, T_QB, 0:1, 0:1]
                    scdp = sc * dpm
                    dl_acc[h, 0] += scdp * tab_ref[h, T_MF1] + xf * dof * (i + 1.0) \
                        + kdf * vf * (CHUNK - 1.0 - i) + (CHUNK * gcf) * jnp.where(pair, dsf32 * sf32, 0.0)
                    dl_acc[h, 1] += scdp * tab_ref[h, T_MB1] + xb * dof * (CHUNK - i) \
                        + kdb * vf * i + (CHUNK * gcb) * jnp.where(pair, dsb32 * sb32, 0.0)
                dr_ref[rows, ps] = (dq2 * c - _swap_halves(dq2) * s).astype(BF16)
                dr_ref[rows, W + pr * 128:W + (pr + 1) * 128] = \
                    ((dk2 * c - _swap_halves(dk2) * s) * K_SCALE).astype(BF16)
            return carry

        lax.fori_loop(0, RET_CPB, chunk, 0)

        @pl.when(pl.program_id(0) == N // RET_CPB - 1)
        def _():
            lane = lax.broadcasted_iota(jnp.int32, (1, 128), 1)
            acc = [jnp.zeros((1, 128), F32), jnp.zeros((1, 128), F32)]
            for h in range(H):
                for b in range(2):
                    acc[b] += jnp.where(lane == h, _sum_all(dl_acc[h, b]), 0.0)
            st_ref[...] = jnp.zeros_like(st_ref)
            st_ref[0:1, :] = acc[0]
            st_ref[1:2, :] = acc[1]

    R = RET_CPB * CHUNK
    st_spec = pl.BlockSpec((RET_CPB, HP, 128, 128), lambda n: (n, 0, 0, 0))
    half = pl.BlockSpec((R, W), lambda n: (n, 0))
    rope = pl.BlockSpec((R, 128), lambda n: (n, 0))
    return _riding_call(
        body, exchange, N // RET_CPB, name="ret_bwd_main",
        args=(tab, qr, kr, p, do, sf_prev, sb_prev, dsf, dsb, c2, s2),
        in_specs=[_tab_spec(H), half, half,
                  pl.BlockSpec((R, D), lambda n: (n, 5)),
                  pl.BlockSpec((R, D), lambda n: (n, 0)),
                  st_spec, st_spec, st_spec, st_spec, rope, rope],
        out_specs=[pl.BlockSpec((R, 2 * D), lambda n: (n, 0)),
                   pl.BlockSpec((8, 128), lambda n: (0, 0))],
        out_shape=[jax.ShapeDtypeStruct((L, 2 * D), BF16), jax.ShapeDtypeStruct((8, 128), F32)],
        scratch=[pltpu.VMEM((H, 2, CHUNK, 128), F32)],
        cparams=_cparams(("arbitrary",)))


def _ctx_bwd(pc, pqk_c, ds0, lg, D):
    Lc = pc.shape[0]
    H = D // DV
    HP = H // 2
    W = D // 2

    def body(lg_ref, k_ref, v_ref, ds_ref, dr_ref, st_ref):
        dqk_ref = dr_ref.at[:, 0:D]
        dv_ref = dr_ref.at[:, D:2 * D]
        m = lax.broadcasted_iota(jnp.int32, (Lc, 128), 0).astype(F32)
        lane = lax.broadcasted_iota(jnp.int32, (1, 128), 1)
        acc_f = jnp.zeros((1, 128), F32)
        acc_b = jnp.zeros((1, 128), F32)
        dqk_ref[:, 0:W] = jnp.zeros((Lc, W), BF16)
        for pr in range(HP):
            ps = slice(pr * 128, (pr + 1) * 128)
            k2 = k_ref[:, ps].astype(F32) * K_SCALE
            dsfp, dsbp = ds_ref[0, pr].astype(BF16), ds_ref[1, pr].astype(BF16)
            dk2 = jnp.zeros((Lc, 128), F32)
            for e in range(2):
                h = 2 * pr + e
                sl = slice(h * DV, (h + 1) * DV)
                hm = _head_lane_mask(k2.shape, e)
                km = jnp.where(hm, k2, 0.0)
                v = v_ref[:, sl]
                vf = v.astype(F32)
                dec_f = jnp.exp(lg_ref[0, h] * (Lc - 1.0 - m))
                dec_b = jnp.exp(lg_ref[1, h] * m)
                kdf = _dot((km * dec_f).astype(BF16), dsfp)
                kdb = _dot((km * dec_b).astype(BF16), dsbp)
                dv_ref[:, sl] = (kdf + kdb).astype(BF16)
                dk2 += jnp.where(hm, dec_f * _dot_nt(v, dsfp) + dec_b * _dot_nt(v, dsbp), 0.0)
                acc_f += jnp.where(lane == h, _sum_all(kdf * vf * (Lc - 1.0 - m)), 0.0)
                acc_b += jnp.where(lane == h, _sum_all(kdb * vf * m), 0.0)
            dqk_ref[:, W + pr * 128:W + (pr + 1) * 128] = (dk2 * K_SCALE).astype(BF16)
        st_ref[...] = jnp.zeros_like(st_ref)
        st_ref[0:1, :] = acc_f
        st_ref[1:2, :] = acc_b

    return pl.pallas_call(
        body, name="ctx_bwd", grid=(1,),
        in_specs=[_smem_spec(), pl.BlockSpec((Lc, W), lambda i: (0, 1)), pl.BlockSpec((Lc, D), lambda i: (0, 1)),
                  pl.BlockSpec((2, HP, 128, 128), lambda i: (0, 0, 0, 0))],
        out_specs=[pl.BlockSpec((Lc, 2 * D), lambda i: (0, 0)), pl.BlockSpec((8, 128), lambda i: (0, 0))],
        out_shape=[jax.ShapeDtypeStruct((Lc, 2 * D), BF16), jax.ShapeDtypeStruct((8, 128), F32)],
    )(lg, pqk_c, pc, ds0)


class _Exchange(NamedTuple):
    inputs: tuple
    out_shapes: tuple
    n_copies: int
    build: Callable
    in_place: bool = False


def _exchange_parts(exchange):
    if exchange is None:
        return [], [], [], [], []
    n = exchange.n_copies
    return (list(exchange.inputs), [ANY] * len(exchange.inputs), list(exchange.out_shapes),
            [ANY] * len(exchange.out_shapes), [pltpu.SemaphoreType.DMA((n,)), pltpu.SemaphoreType.DMA((n,))])


def _riding_call(body, exchange, n_steps, *, args, in_specs, out_specs, out_shape, name, cparams, scratch=()):
    ex_args, ex_in_specs, ex_shapes, ex_out_specs, ex_scratch = _exchange_parts(exchange)
    n_in, n_out, n_sc = len(args), len(out_shape), len(scratch)

    def riding(*refs):
        k = n_in + len(ex_args)
        ins, ex_in = refs[:n_in], refs[n_in:k]
        outs, ex_out = refs[k:k + n_out], refs[k + n_out:k + n_out + len(ex_shapes)]
        k += n_out + len(ex_shapes)
        own_scratch, ex_sems = refs[k:k + n_sc], refs[k + n_sc:]
        step = pl.program_id(0)
        if exchange is not None:
            @pl.when(step == 0)
            def _():
                for rc in exchange.build(ex_in, ex_out, *ex_sems):
                    rc.start()
        body(*ins, *outs, *own_scratch)
        if exchange is not None:
            @pl.when(step == n_steps - 1)
            def _():
                for rc in exchange.build(ex_in, ex_out, *ex_sems):
                    rc.wait()

    aliases = {}
    if exchange is not None and exchange.in_place:
        aliases = {n_in + i: n_out + i for i in range(len(ex_args))}
    return tuple(pl.pallas_call(
        riding, name=name, grid=(n_steps,),
        in_specs=list(in_specs) + ex_in_specs, out_specs=list(out_specs) + ex_out_specs,
        out_shape=list(out_shape) + ex_shapes, scratch_shapes=list(scratch) + ex_scratch,
        input_output_aliases=aliases, compiler_params=cparams,
    )(*args, *ex_args))


def _dxm(groups, col0, w, x, nw, sc, dx1, name, exchange=None):
    L, D = x.shape
    tm = min(256, L)
    nt = L // tm
    ng = len(groups)
    widths = [g.shape[1] for g in groups]
    wtot = sum(widths)
    with_dx = dx1 is not None
    ex_args, ex_in_specs, ex_shapes, ex_out_specs, ex_scratch = _exchange_parts(exchange)
    n_in = ng + 4 + (1 if with_dx else 0)
    n_out = 2 if with_dx else 1

    def body(*refs):
        group_refs = refs[:ng]
        w_hbm, x_ref, nw_ref, sc_ref = refs[ng:ng + 4]
        ex_in = refs[n_in:n_in + len(ex_args)]
        outs = refs[n_in + len(ex_args):]
        if with_dx:
            dx1_ref, gx_ref, st_ref = refs[ng + 4], outs[0], outs[1]
        else:
            st_ref = outs[0]
        ex_out = outs[n_out:n_out + len(ex_shapes)]
        w_vm, sem = outs[n_out + len(ex_shapes):n_out + len(ex_shapes) + 2]
        ex_sems = outs[n_out + len(ex_shapes) + 2:]
        i = pl.program_id(0)

        @pl.when(i == 0)
        def _():
            cp = pltpu.make_async_copy(w_hbm.at[:, col0 * D:col0 * D + wtot], w_vm, sem)
            cp.start()
            if exchange is not None:
                for rc in exchange.build(ex_in, ex_out, *ex_sems):
                    rc.start()
            st_ref[...] = jnp.zeros_like(st_ref)
            cp.wait()

        dxm, off = None, 0
        for g_ref, wd in zip(group_refs, widths):
            part = _dot_nt(g_ref[...], w_vm[:, off:off + wd])
            dxm = part if dxm is None else dxm + part
            off += wd

        xv = x_ref[...]
        r = lax.rsqrt(jnp.mean(xv * xv, axis=-1, keepdims=True) + EPS)
        xh = xv * r
        nwv = nw_ref[...]
        dxn = dxm * (1.0 + sc_ref[...])
        st_ref[0:1, :] += jnp.sum(dxm, axis=0, keepdims=True)
        st_ref[1:2, :] += jnp.sum(dxm * (xh * nwv), axis=0, keepdims=True)
        st_ref[2:3, :] += jnp.sum(dxn * xh, axis=0, keepdims=True)
        if with_dx:
            dxh = dxn * nwv
            gx_ref[...] = dx1_ref[...] + r * (dxh - xh * jnp.mean(dxh * xh, axis=-1, keepdims=True))

        if exchange is not None:
            @pl.when(i == nt - 1)
            def _():
                for rc in exchange.build(ex_in, ex_out, *ex_sems):
                    rc.wait()

    row = pl.BlockSpec((tm, D), lambda i: (i, 0))
    in_specs = [pl.BlockSpec((tm, wd), lambda i: (i, 0)) for wd in widths] + [ANY, row, _vec_spec(D), _vec_spec(D)]
    out_specs = [pl.BlockSpec((8, D), lambda i: (0, 0))]
    out_shape = [jax.ShapeDtypeStruct((8, D), F32)]
    args = list(groups) + [w, x, nw, sc]
    if with_dx:
        in_specs.append(row)
        out_specs.insert(0, row)
        out_shape.insert(0, jax.ShapeDtypeStruct((L, D), F32))
        args.append(dx1)
    res = pl.pallas_call(
        body, name=name, grid=(nt,),
        in_specs=in_specs + ex_in_specs, out_specs=out_specs + ex_out_specs, out_shape=out_shape + ex_shapes,
        scratch_shapes=[pltpu.VMEM((D, wtot), BF16), pltpu.SemaphoreType.DMA] + ex_scratch,
        compiler_params=_cparams(("arbitrary",), VMEM_LIMIT),
    )(*args, *ex_args)
    gx = res[0] if with_dx else None
    return (gx, res[n_out - 1], *res[n_out:])


DW_TN = 256
DW_RING = 4


def _dw_in(xmt, groups, cmt, dr_c, D, pair):
    L = xmt.shape[1]
    Lc = cmt.shape[1]
    Dh = D // 2
    tn = min(DW_TN, D)
    nblk = [g.shape[1] // tn for g in groups]
    starts = [sum(nblk[:g]) for g in range(len(groups))]
    ng = len(groups)
    nj = sum(nblk)
    rows_out = Dh if pair else D

    def body(*refs):
        xt_hbm = refs[0]
        group_refs = refs[1:1 + ng]
        ct_hbm, drc_ref, o_ref = refs[1 + ng:4 + ng]
        rest = refs[4 + ng:]
        if pair:
            ra_hbm, xt_vm, ct_vm, loc, ring, s_send, s_recv = rest
            pos = _position()
            sib = _peer(pos, 1)
        else:
            xt_vm, ct_vm, loc = rest
        j = pl.program_id(0)

        @pl.when(j == 0)
        def _():
            if pair:
                c = pos[2]
                other = pl.ds(pl.multiple_of((1 - c) * Dh, Dh), Dh)
                mine = pl.ds(pl.multiple_of(c * Dh, Dh), Dh)
                cps = [pltpu.make_async_copy(xt_hbm.at[other, :], xt_vm.at[0:Dh, :], loc.at[0]),
                       pltpu.make_async_copy(xt_hbm.at[mine, :], xt_vm.at[Dh:D, :], loc.at[1]),
                       pltpu.make_async_copy(ct_hbm.at[other, :], ct_vm.at[0:Dh, :], loc.at[2]),
                       pltpu.make_async_copy(ct_hbm.at[mine, :], ct_vm.at[Dh:D, :], loc.at[3])]
            else:
                cps = [pltpu.make_async_copy(xt_hbm, xt_vm, loc.at[0]), pltpu.make_async_copy(ct_hbm, ct_vm, loc.at[1])]
            for cp in cps:
                cp.start()
            for cp in cps:
                cp.wait()

        def send(slot):
            cols = pl.ds(pl.multiple_of(j * tn, 128), tn)
            return pltpu.make_async_remote_copy(src_ref=ring.at[slot], dst_ref=ra_hbm.at[:, cols],
                                                send_sem=s_send.at[slot], recv_sem=s_recv,
                                                device_id=sib, device_id_type=MESH)

        for g in range(ng):
            @pl.when((j >= starts[g]) & (j < starts[g] + nblk[g]))
            def _(g=g):
                acc = _dot(xt_vm[...], group_refs[g][...])
                if g == 1:
                    acc += _dot(ct_vm[...], drc_ref[...])
                if not pair:
                    o_ref[...] = acc
                    return
                o_ref[...] = acc[Dh:, :]
                slot = lax.rem(j, DW_RING)

                @pl.when(j >= DW_RING)
                def _():
                    send(slot).wait_send()

                ring[slot] = acc[0:Dh, :]
                send(slot).start()

        if pair:
            @pl.when(j == nj - 1)
            def _():
                pltpu.make_async_remote_copy(src_ref=ra_hbm, dst_ref=ra_hbm, send_sem=s_send.at[0], recv_sem=s_recv,
                                             device_id=sib, device_id_type=MESH).wait_recv()
                for slot in range(DW_RING):
                    send(slot).wait_send()

    def group_spec(g, rows):
        return pl.BlockSpec((rows, tn), lambda j: (0, jnp.clip(j - starts[g], 0, nblk[g] - 1)))

    out_specs = [pl.BlockSpec((rows_out, tn), lambda j: (0, j))]
    out_shape = [jax.ShapeDtypeStruct((rows_out, nj * tn), F32)]
    scratch = [pltpu.VMEM((D, L), BF16), pltpu.VMEM((D, Lc), BF16), pltpu.SemaphoreType.DMA((4,))]
    if pair:
        out_specs.append(ANY)
        out_shape.append(jax.ShapeDtypeStruct((Dh, nj * tn), F32))
        scratch += [pltpu.VMEM((DW_RING, Dh, tn), F32), pltpu.SemaphoreType.DMA((DW_RING,)), pltpu.SemaphoreType.DMA]
    return tuple(pl.pallas_call(
        body, name="dw_in", grid=(nj,),
        in_specs=[ANY] + [group_spec(g, L) for g in range(ng)] + [ANY, group_spec(1, Lc)],
        out_specs=out_specs, out_shape=out_shape, scratch_shapes=scratch,
        compiler_params=_cparams(("arbitrary",), VMEM_LIMIT),
    )(xmt, *groups, cmt, dr_c))


def _local_step(x, ctx, tgt, mod_x, mod_c, norm_w, conv_w8, conv_b, lg, gn_w, fw, project, w3, csidx=None):
    L, D = x.shape
    sh_x, sc_x, g_x = mod_x[0:1], mod_x[1:2], mod_x[2:3]
    sh_c, sc_c = mod_c[0:1], mod_c[1:2]
    c2, s2 = _rope_tables(L)
    tab = _decay_tables(lg, D // DV)

    xm, xmt = _norm_mod(x, norm_w, sc_x, sh_x, "norm_mod_x")
    cm, cmt = _norm_mod(ctx, norm_w, sc_c, sh_c, "norm_mod_ctx")
    reduce = csidx is not None
    p, qr, kr, w_in = project(xm, c2, s2)
    pc, pqk_c = _in_proj(cm, w_in, "in_proj_ctx", QK_BLOCK, 2)
    s0 = _ctx_states(pc, pqk_c, lg, D)
    sf_prev, sb_prev, *w3_a = _ret_states(kr, p, s0, tab, D, _w3_chips_exchange(w3) if reduce else None)
    o, yb, *w3_b = _ret_out(qr, kr, p, sf_prev, sb_prev, gn_w, tab, D, _w3_pair_exchange(w3_a[0]) if reduce else None)
    if reduce:
        w3 = w3_b[0].reshape(3, D, D)
    dx1, dya, do, dzb, dgab, dw3, st_mid = _mid(p, yb, o, x, tgt, w3, g_x, fw, conv_w8, conv_b, gn_w, D)
    dw3_5 = dw3.reshape(3, N_SHARD, 2, D // 8, D)
    dconv, st_conv, *ra_3 = _conv_bwd(dya, p, conv_w8, conv_b, D, _pair_exchange_w3(dw3_5) if reduce else None)
    dsf, dsb, ds0 = _ret_bwd_states(qr, do, tab, D)
    cs_3 = _sum_pair_w3(csidx[0:1], dw3_5, ra_3[0]) if reduce else None
    dret, st_lg, *rb_3 = _ret_bwd_main(qr, kr, p, do, sf_prev, sb_prev, dsf, dsb, c2, s2, tab, D,
                                       _chips_exchange_w3(cs_3) if reduce else None)
    g_3 = _sum_chips_w3(csidx, cs_3, rb_3[0]) if reduce else dw3
    dret_c, st_lgc = _ctx_bwd(pc, pqk_c, ds0, lg, D)
    groups = (dconv, dret, dzb, dgab)
    _, st_c = _dxm((dret_c,), 4, w_in, ctx, norm_w, sc_c, None, "dxm_ctx")
    return groups, dret_c, xmt, cmt, dx1, sc_x, w_in, g_3, (st_mid, st_conv, st_lg, st_lgc, st_c)


CHIP_FLIPS = (4, 2, 6)
ANY = pl.BlockSpec(memory_space=pl.ANY)
VMEM_FULL = pl.BlockSpec(memory_space=pltpu.VMEM)


def _position():
    return lax.axis_index("x"), lax.axis_index("y"), lax.axis_index("c")


def _peer(pos, k):
    x, y, c = pos
    return (1 - x if k & 4 else x, 1 - y if k & 2 else y, 1 - c if k & 1 else c)


def _dev_id(pos):
    return 4 * pos[0] + 2 * pos[1] + pos[2]


def _shard_of(pos):
    return 2 * pos[0] + pos[1]


def _remote(src, dst, send_sems, recv_sems, idx, to):
    return pltpu.make_async_remote_copy(src_ref=src, dst_ref=dst, send_sem=send_sems.at[idx],
                                        recv_sem=recv_sems.at[idx], device_id=to, device_id_type=MESH)


def _dot_f32(a, b):
    return jnp.dot(a, b, precision=lax.Precision.HIGHEST, preferred_element_type=F32)


def _silu(x):
    return x * _sigmoid(x)


def _fwd_small(c8, cctx8, ada_w, ada_b, conv_w8):
    D = c8.shape[1]
    Wm = ada_w.shape[1]
    Dq = conv_w8.shape[1]

    def body(c_ref, cc_ref, aw_ref, ab_ref, cw_ref, act_ref, mod_ref, cwf_ref,
             cbuf, pmine, pbuf, wbuf, s_c, r_c, s_p, r_p, s_w, r_w):
        pos = _position()
        me, s = _dev_id(pos), _shard_of(pos)
        cbuf[me] = c_ref[...]
        wbuf[s] = cw_ref[...]
        sends = [_remote(c_ref, cbuf.at[me], s_c, r_c, k - 1, _peer(pos, k)) for k in range(1, 8)]
        sends += [_remote(cw_ref, wbuf.at[s], s_w, r_w, j, _peer(pos, k)) for j, k in enumerate(CHIP_FLIPS)]
        for cp in sends:
            cp.start()
        for k in range(1, 8):
            _remote(c_ref, cbuf.at[_dev_id(_peer(pos, k))], s_c, r_c, k - 1, _peer(pos, k)).wait_recv()
        for d in range(N_DEV):
            act_ref[d:d + 1, :] = _silu(cbuf[d, 0:1, :])
        act_ref[8:9, :] = _silu(cc_ref[0:1, :])
        act_ref[9:16, :] = jnp.zeros((7, D), F32)
        part = _dot_f32(act_ref[...], aw_ref[...])
        pmine[...] = part
        pbuf[s] = part
        psend = [_remote(pmine, pbuf.at[s], s_p, r_p, j, _peer(pos, k)) for j, k in enumerate(CHIP_FLIPS)]
        for cp in psend:
            cp.start()
        for j, k in enumerate(CHIP_FLIPS):
            t = _shard_of(_peer(pos, k))
            _remote(pmine, pbuf.at[t], s_p, r_p, j, _peer(pos, k)).wait_recv()
            _remote(cw_ref, wbuf.at[t], s_w, r_w, j, _peer(pos, k)).wait_recv()
        for t in range(N_SHARD):
            mod_ref[:, t * Wm:(t + 1) * Wm] = pbuf[t] + ab_ref[:, t * Wm:(t + 1) * Wm]
            cwf_ref[:, t * Dq:(t + 1) * Dq] = wbuf[t]
        for cp in sends + psend:
            cp.wait_send()

    return pl.pallas_call(
        body, name="fwd_small",
        in_specs=[VMEM_FULL] * 5, out_specs=[VMEM_FULL] * 3,
        out_shape=[jax.ShapeDtypeStruct((16, D), F32), jax.ShapeDtypeStruct((16, 3 * D), F32),
                   jax.ShapeDtypeStruct((8, D), F32)],
        scratch_shapes=[pltpu.VMEM((N_DEV, 8, D), F32), pltpu.VMEM((16, Wm), F32),
                        pltpu.VMEM((N_SHARD, 16, Wm), F32), pltpu.VMEM((N_SHARD, 8, Dq), F32),
                        pltpu.SemaphoreType.DMA((7,)), pltpu.SemaphoreType.DMA((7,)),
                        pltpu.SemaphoreType.DMA((3,)), pltpu.SemaphoreType.DMA((3,)),
                        pltpu.SemaphoreType.DMA((3,)), pltpu.SemaphoreType.DMA((3,))],
        compiler_params=_cparams(None, VMEM_LIMIT),
    )(c8, cctx8, ada_w, ada_b, conv_w8)


AG_CHUNKS = 3


def _ag_in_proj(xm, w_in_s, c2, s2):
    L, D = xm.shape
    Wc = w_in_s.shape[1]
    Wq = Wc // AG_CHUNKS
    Dh = D // 2
    TM = min(1024, L // 4)
    NT = L // TM
    RC = min(128, Dh)
    NQ = AG_CHUNKS

    def body(xm_ref, wi_hbm, c_ref, s_ref, p_hbm, qr_hbm, kr_hbm, fi_hbm,
             w_vm, cast_buf, stage, qk_stage, ici_s, ici_r, d2d_s, d2d_r, loc, out_sem, qk_sem):
        pos = _position()
        c = pos[2]
        s = _shard_of(pos)
        sib = _peer(pos, 1)
        mine = pl.ds(pl.multiple_of(c * Dh, Dh), Dh)
        other = pl.ds(pl.multiple_of((1 - c) * Dh, Dh), Dh)

        def cast_half(hf):
            def step(i, carry):
                rows = pl.ds(pl.multiple_of(hf * Dh + i * RC, RC), RC)
                cp = pltpu.make_async_copy(wi_hbm.at[rows, :], cast_buf, loc.at[0])
                cp.start()
                cp.wait()
                for q in range(NQ):
                    w_vm[0, q, rows, :] = cast_buf[:, q * Wq:(q + 1) * Wq].astype(BF16)
                return carry
            lax.fori_loop(0, Dh // RC, step, 0)

        def abs_col(t, q):
            return pl.ds(pl.multiple_of(t * Wc + q * Wq, 128), Wq)

        cast_half(c)
        sends = []
        for q in range(NQ):
            for j, k in enumerate(CHIP_FLIPS):
                sends.append(_remote(w_vm.at[0, q, mine, :], w_vm.at[1 + j, q, mine, :], ici_s, ici_r,
                                     q * 3 + j, _peer(pos, k)))
        for cp in sends:
            cp.start()
        cast_half(1 - c)
        local = [pltpu.make_async_copy(w_vm.at[0, q], fi_hbm.at[:, abs_col(s, q)], loc.at[1 + q]) for q in range(NQ)]
        for cp in local:
            cp.start()

        def out_copy(slot, rows, cols):
            return pltpu.make_async_copy(stage.at[slot], p_hbm.at[rows, cols], out_sem.at[slot])

        def block(r, q, t, first):
            cols = abs_col(t, q)

            def row_tile(rt, carry):
                rows = pl.ds(pl.multiple_of(rt * TM, TM), TM)
                acc = _dot(xm_ref[rows, :], w_vm[r, q])
                slot = lax.rem(rt, 2)

                @pl.when(rt >= 2 if first else rt >= 0)
                def _():
                    out_copy(slot, rows, cols).wait()

                stage[slot] = acc.astype(BF16)
                out_copy(slot, rows, cols).start()

                def rotary(lo, scale, dst_hbm):
                    c, s = c_ref[rows, :], s_ref[rows, :]
                    for pr in range(Dh // 128):
                        tq = acc[:, lo + pr * 128:lo + (pr + 1) * 128] * scale
                        qk_stage[:, pr * 128:(pr + 1) * 128] = (tq * c + _swap_halves(tq) * s).astype(BF16)
                    cp = pltpu.make_async_copy(qk_stage, dst_hbm.at[rows, :], qk_sem)
                    cp.start()
                    cp.wait()

                if q == NQ - 1:
                    @pl.when(t == 1)
                    def _():
                        rotary(Wq - Dh, 1.0, qr_hbm)
                if q == 0:
                    @pl.when(t == 2)
                    def _():
                        rotary(0, K_SCALE, kr_hbm)
                return carry

            lax.fori_loop(0, NT, row_tile, 0)

        for q in range(NQ):
            block(0, q, s, q == 0)
        passed = []
        for q in range(NQ):
            for j, k in enumerate(CHIP_FLIPS):
                r, idx = 1 + j, q * 3 + j
                t = _shard_of(_peer(pos, k))
                _remote(w_vm.at[r, q, mine, :], w_vm.at[r, q, mine, :], ici_s, ici_r, idx, sib).wait_recv()
                fwd = _remote(w_vm.at[r, q, mine, :], w_vm.at[r, q, mine, :], d2d_s, d2d_r, idx, sib)
                fwd.start()
                passed.append(fwd)
                _remote(w_vm.at[r, q, other, :], w_vm.at[r, q, other, :], d2d_s, d2d_r, idx, sib).wait_recv()
                block(r, q, t, False)
                cp = pltpu.make_async_copy(w_vm.at[r, q], fi_hbm.at[:, abs_col(t, q)], loc.at[1 + NQ + idx])
                cp.start()
                local.append(cp)
        for cp in sends + passed:
            cp.wait_send()
        for cp in local:
            cp.wait()
        for slot in range(2):
            out_copy(slot, pl.ds(0, TM), abs_col(s, 0)).wait()

    n_loc = 1 + NQ + 3 * NQ
    return pl.pallas_call(
        body, name="ag_in_proj",
        in_specs=[VMEM_FULL, ANY, VMEM_FULL, VMEM_FULL], out_specs=[ANY] * 4,
        out_shape=[jax.ShapeDtypeStruct((L, N_SHARD * Wc), BF16),
                   jax.ShapeDtypeStruct((L, Dh), BF16), jax.ShapeDtypeStruct((L, Dh), BF16),
                   jax.ShapeDtypeStruct((D, N_SHARD * Wc), BF16)],
        scratch_shapes=[pltpu.VMEM((N_SHARD, NQ, D, Wq), BF16), pltpu.VMEM((RC, Wc), F32),
                        pltpu.VMEM((2, TM, Wq), BF16), pltpu.VMEM((TM, Dh), BF16),
                        pltpu.SemaphoreType.DMA((3 * NQ,)), pltpu.SemaphoreType.DMA((3 * NQ,)),
                        pltpu.SemaphoreType.DMA((3 * NQ,)), pltpu.SemaphoreType.DMA((3 * NQ,)),
                        pltpu.SemaphoreType.DMA((n_loc,)), pltpu.SemaphoreType.DMA((2,)), pltpu.SemaphoreType.DMA],
        compiler_params=_cparams(None, VMEM_LIMIT),
    )(xm, w_in_s, c2, s2)


def _w3_stage(sidx, w_a, w_b, w_out):
    _, Do, D = w_a.shape

    def body(s_ref, wa_ref, wb_ref, wo_ref, o_ref):
        for a, w_ref in enumerate((wa_ref, wb_ref, wo_ref)):
            @pl.when(pl.program_id(0) == a)
            def _(w_ref=w_ref):
                o_ref[...] = w_ref[...].astype(BF16)

    full = pl.BlockSpec((2, Do, D), lambda a, s: (0, 0, 0))
    return pl.pallas_call(
        body, name="w3_stage",
        grid_spec=pltpu.PrefetchScalarGridSpec(
            num_scalar_prefetch=1, grid=(3,), in_specs=[full, full, full],
            out_specs=pl.BlockSpec((None, None, 2, Do, D), lambda a, s: (a, s[0], 0, 0, 0))),
        out_shape=jax.ShapeDtypeStruct((3, N_SHARD, 2, Do, D), BF16),
    )(sidx, w_a, w_b, w_out)


def _w3_chips_exchange(f3):
    def build(ins, outs, send, recv):
        pos = _position()
        own = outs[0].at[:, _shard_of(pos), pos[2]]
        return [_remote(own, own, send, recv, j, _peer(pos, k)) for j, k in enumerate(CHIP_FLIPS)]

    return _Exchange((f3,), (jax.ShapeDtypeStruct(f3.shape, f3.dtype),), 3, build, in_place=True)


def _w3_pair_exchange(f3):
    def build(ins, outs, send, recv):
        pos = _position()
        cps = []
        for j, k in enumerate(CHIP_FLIPS):
            got = outs[0].at[:, _shard_of(_peer(pos, k)), pos[2]]
            cps.append(_remote(got, got, send, recv, j, _peer(pos, 1)))
        return cps

    return _Exchange((f3,), (jax.ShapeDtypeStruct(f3.shape, f3.dtype),), 3, build, in_place=True)


def _pair_exchange_w3(dw3):
    _, _, _, Do, D = dw3.shape

    def build(ins, outs, send, recv):
        pos = _position()
        return [_remote(ins[0].at[:, :, 1 - pos[2]], outs[0], send, recv, 0, _peer(pos, 1))]

    return _Exchange((dw3,), (jax.ShapeDtypeStruct((3, N_SHARD, Do, D), F32),), 1, build)


def _sum_pair_in(dw_mine, ri):
    Dh, Wf = dw_mine.shape
    Wc = Wf // N_SHARD
    tr = min(256, Dh)

    def body(a_ref, b_ref, o_ref):
        o_ref[...] = (a_ref[...] + b_ref[...]).astype(BF16)

    return pl.pallas_call(
        body, name="sum_pair_in", grid=(Dh // tr, N_SHARD),
        in_specs=[pl.BlockSpec((tr, Wc), lambda i, t: (i, t)), pl.BlockSpec((tr, Wc), lambda i, t: (i, t))],
        out_specs=pl.BlockSpec((None, tr, Wc), lambda i, t: (t, i, 0)),
        out_shape=jax.ShapeDtypeStruct((N_SHARD, Dh, Wc), BF16),
        compiler_params=_cparams(("parallel", "parallel")),
    )(dw_mine, ri)


def _sum_pair_w3(cidx, dw3, r3):
    _, _, _, Do, D = dw3.shape

    def body(c_ref, a_ref, b_ref, o_ref):
        o_ref[...] = (a_ref[...] + b_ref[...]).astype(BF16)

    return pl.pallas_call(
        body, name="sum_pair_w3",
        grid_spec=pltpu.PrefetchScalarGridSpec(
            num_scalar_prefetch=1, grid=(3,),
            in_specs=[pl.BlockSpec((None, N_SHARD, None, Do, D), lambda a, c: (a, 0, c[0], 0, 0)),
                      pl.BlockSpec((None, N_SHARD, Do, D), lambda a, c: (a, 0, 0, 0))],
            out_specs=pl.BlockSpec((None, N_SHARD, Do, D), lambda a, c: (a, 0, 0, 0))),
        out_shape=jax.ShapeDtypeStruct((3, N_SHARD, Do, D), BF16),
        compiler_params=_cparams(("parallel",)),
    )(cidx, dw3, r3)


def _chips_exchange_in(cs_in):
    _, Dh, Wc = cs_in.shape

    def build(ins, outs, send, recv):
        pos = _position()
        return [_remote(ins[0].at[_shard_of(_peer(pos, k))], outs[0].at[j], send, recv, j, _peer(pos, k))
                for j, k in enumerate(CHIP_FLIPS)]

    return _Exchange((cs_in,), (jax.ShapeDtypeStruct((3, Dh, Wc), BF16),), 3, build)


def _chips_exchange_w3(cs_3):
    _, _, Do, D = cs_3.shape

    def build(ins, outs, send, recv):
        pos = _position()
        return [_remote(ins[0].at[:, _shard_of(_peer(pos, k))], outs[0].at[j], send, recv, j, _peer(pos, k))
                for j, k in enumerate(CHIP_FLIPS)]

    return _Exchange((cs_3,), (jax.ShapeDtypeStruct((3, 3, Do, D), BF16),), 3, build)


def _sum_chips_in(csidx, cs_in, rb_in, through):
    _, Dh, Wc = cs_in.shape
    tr = min(256, Dh)

    def body(s_ref, a_ref, b_ref, through_in, o_ref, through_out):
        acc = a_ref[...].astype(F32)
        for j in range(3):
            acc = acc + b_ref[j].astype(F32)
        o_ref[...] = acc

    return pl.pallas_call(
        body, name="sum_chips_in",
        grid_spec=pltpu.PrefetchScalarGridSpec(
            num_scalar_prefetch=1, grid=(Dh // tr,),
            in_specs=[pl.BlockSpec((None, tr, Wc), lambda i, s: (s[1], i, 0)),
                      pl.BlockSpec((3, tr, Wc), lambda i, s: (0, i, 0)), ANY],
            out_specs=[pl.BlockSpec((None, tr, Wc), lambda i, s: (s[0], i, 0)), ANY]),
        out_shape=[jax.ShapeDtypeStruct((2, Dh, Wc), F32), jax.ShapeDtypeStruct(through.shape, through.dtype)],
        input_output_aliases={3: 1},
        compiler_params=_cparams(("parallel",)),
    )(csidx, cs_in, rb_in, through)


def _sum_chips_w3(csidx, cs_3, rb_3):
    _, _, Do, D = cs_3.shape

    def body(s_ref, a_ref, b_ref, o_ref):
        acc = a_ref[...].astype(F32)
        for j in range(3):
            acc = acc + b_ref[j].astype(F32)
        o_ref[...] = acc

    return pl.pallas_call(
        body, name="sum_chips_w3",
        grid_spec=pltpu.PrefetchScalarGridSpec(
            num_scalar_prefetch=1, grid=(3,),
            in_specs=[pl.BlockSpec((None, None, Do, D), lambda a, s: (a, s[1], 0, 0)),
                      pl.BlockSpec((3, None, Do, D), lambda a, s: (0, a, 0, 0))],
            out_specs=pl.BlockSpec((None, None, Do, D), lambda a, s: (a, s[0], 0, 0))),
        out_shape=jax.ShapeDtypeStruct((3, 2, Do, D), F32),
        compiler_params=_cparams(("parallel",)),
    )(csidx, cs_3, rb_3)


def _adam_math(w, g, m, v):
    m = ADAM_B1 * m + (1.0 - ADAM_B1) * g
    v = ADAM_B2 * v + (1.0 - ADAM_B2) * (g * g)
    m_hat = m / (1.0 - ADAM_B1 ** ADAM_STEP)
    v_hat = v / (1.0 - ADAM_B2 ** ADAM_STEP)
    delta = -ADAM_LR * (m_hat / (jnp.sqrt(v_hat) + ADAM_EPS) + ADAM_WD * w)
    return delta, m, v


def _adamw(w, g, m, v, name):
    R, C = w.shape
    tr = min(128, R)

    def body(w_ref, g_ref, m_ref, v_ref, d_ref, nm_ref, nv_ref):
        d_ref[...], nm_ref[...], nv_ref[...] = _adam_math(w_ref[...], g_ref[...], m_ref[...], v_ref[...])

    blk = pl.BlockSpec((tr, C), lambda i: (i, 0))
    return pl.pallas_call(
        body, name=name, grid=(R // tr,), in_specs=[blk] * 4, out_specs=[blk] * 3,
        out_shape=[jax.ShapeDtypeStruct((R, C), F32)] * 3,
        compiler_params=_cparams(("parallel",), VMEM_LIMIT),
    )(w, g, m, v)


SMALL_ROWS = ("c_ctx", "norm_w", "conv_b", "gn_w", "final_norm_w")


def _bwd_small(stats, ada_w, Dq, gh_in, gh_3):
    D = stats[0].shape[1]
    Wm = ada_w.shape[1]

    def body(stx, stm, stc, stv, stl, stlc, aw_ref, gi_in, g3_in, tot_ref, dm_sh, gcw, da_ref, loss_ref, gi_ref, g3_ref,
             vec_ref, vbuf, dm, amine, abuf, s_v, r_v, s_a, r_a, s_g, r_g):
        pos = _position()
        me, s = _dev_id(pos), _shard_of(pos)
        c, sib = pos[2], _peer(pos, 1)
        halves = [_remote(gi_in.at[c], gi_ref.at[c], s_g, r_g, 0, sib),
                  _remote(g3_in.at[:, c], g3_ref.at[:, c], s_g, r_g, 1, sib)]
        for cp in halves:
            cp.start()
        vec_ref[...] = jnp.zeros_like(vec_ref)
        vec_ref[0:2, :] = stx[0:2, :]
        vec_ref[2:3, :] = stm[1:2, :]
        vec_ref[3:5, :] = stc[0:2, :]
        vec_ref[5:6, :] = stx[2:3, :] + stc[2:3, :]
        vec_ref[6:7, :] = stv[3:4, :]
        vec_ref[7:8, :] = stm[3:4, :]
        vec_ref[8:9, :] = stm[0:1, :]
        vec_ref[9:12, :] = stv[0:3, :]
        vec_ref[12:14, 0:128] = stl[0:2, :] + stlc[0:2, :]
        vec_ref[14:15, :] = stm[2:3, :]
        vbuf[me] = vec_ref[...]
        sends = [_remote(vec_ref, vbuf.at[me], s_v, r_v, k - 1, _peer(pos, k)) for k in range(1, 8)]
        for cp in sends:
            cp.start()
        for k in range(1, 8):
            _remote(vec_ref, vbuf.at[_dev_id(_peer(pos, k))], s_v, r_v, k - 1, _peer(pos, k)).wait_recv()
        tot = vbuf[0]
        for d in range(1, N_DEV):
            tot = tot + vbuf[d]
        loss_ref[...] = jnp.zeros((8, 128), F32) + (0.5 / D) * _sum_all(tot[14:15, :])
        dm[...] = jnp.zeros_like(dm)
        for d in range(N_DEV):
            for r in range(3):
                dm[d:d + 1, r * D:(r + 1) * D] = vbuf[d, r:r + 1, :]
        dm[8:9, 0:D] = tot[3:4, :]
        dm[8:9, D:2 * D] = tot[4:5, :]
        for t in range(N_SHARD):
            @pl.when(s == t)
            def _(t=t):
                dm_sh[...] = dm[:, t * Wm:(t + 1) * Wm]
                gcw[...] = tot[9:12, t * Dq:(t + 1) * Dq]
        tot_ref[...] = tot
        part = lax.dot_general(dm_sh[8:16, :], aw_ref[...], (((1,), (1,)), ((), ())),
                               precision=lax.Precision.HIGHEST, preferred_element_type=F32)
        amine[...] = part
        abuf[s] = part
        asend = [_remote(amine, abuf.at[s], s_a, r_a, j, _peer(pos, k)) for j, k in enumerate(CHIP_FLIPS)]
        for cp in asend:
            cp.start()
        for j, k in enumerate(CHIP_FLIPS):
            _remote(amine, abuf.at[_shard_of(_peer(pos, k))], s_a, r_a, j, _peer(pos, k)).wait_recv()
        da = abuf[0]
        for t in range(1, N_SHARD):
            da = da + abuf[t]
        da_ref[...] = da
        _remote(gi_in.at[1 - c], gi_ref.at[1 - c], s_g, r_g, 0, sib).wait_recv()
        _remote(g3_in.at[:, 1 - c], g3_ref.at[:, 1 - c], s_g, r_g, 1, sib).wait_recv()
        for cp in sends + asend + halves:
            cp.wait_send()

    row = lambda *shape: jax.ShapeDtypeStruct(shape, F32)
    return pl.pallas_call(
        body, name="bwd_small",
        in_specs=[VMEM_FULL] * 7 + [ANY, ANY], out_specs=[VMEM_FULL] * 5 + [ANY, ANY],
        input_output_aliases={7: 5, 8: 6},
        out_shape=[row(16, D), row(16, Wm), row(3, Dq), row(8, D), row(8, 128), row(*gh_in.shape), row(*gh_3.shape)],
        scratch_shapes=[pltpu.VMEM((16, D), F32), pltpu.VMEM((N_DEV, 16, D), F32), pltpu.VMEM((16, 3 * D), F32),
                        pltpu.VMEM((8, D), F32), pltpu.VMEM((N_SHARD, 8, D), F32),
                        pltpu.SemaphoreType.DMA((7,)), pltpu.SemaphoreType.DMA((7,)),
                        pltpu.SemaphoreType.DMA((3,)), pltpu.SemaphoreType.DMA((3,)),
                        pltpu.SemaphoreType.DMA((2,)), pltpu.SemaphoreType.DMA((2,))],
        compiler_params=_cparams(None, VMEM_LIMIT),
    )(*stats, ada_w, gh_in, gh_3)


def _small_update(tot, dm_sh, gcw, da, act, p_row, p_ab, p_cw, p_dl):
    D = act.shape[1]
    Wm = dm_sh.shape[1]
    Dq = gcw.shape[1]

    def body(tot_ref, dm_ref, gcw_ref, da_ref, act_ref, prow, pab, pcw, pdl, gaw_ref, *outs):
        o_q = [outs[8 * q:8 * (q + 1)] for q in range(4)]
        tot = tot_ref[...]
        gaw_ref[...] = lax.dot_general(act_ref[...], dm_ref[...], (((0,), (0,)), ((), ())),
                                       precision=lax.Precision.HIGHEST, preferred_element_type=F32)
        cc = prow[0, 0:1, :]
        sg = _sigmoid(cc)
        g_cctx = da_ref[0:1, :] * (sg * (1.0 + cc * (1.0 - sg)))

        def place_all(o, val):
            o[...] = val

        def emit(k, w, g, m, v, place=place_all):
            for q, val in enumerate((g,) + _adam_math(w, g, m, v)):
                place(o_q[q][k], val)

        g_rows = [g_cctx, tot[5:6, :], tot[6:7, :], tot[7:8, :], tot[8:9, :]]
        for k, g in enumerate(g_rows):
            emit(k, prow[0, k:k + 1, :], g, prow[1, k:k + 1, :], prow[2, k:k + 1, :])

        def place_ab(o, val):
            for r in range(3):
                o[0:1, r * D:(r + 1) * D] = val[r:r + 1, :]

        g_ab = jnp.concatenate([tot[0:1, :] + tot[3:4, :], tot[1:2, :] + tot[4:5, :], tot[2:3, :]], axis=0)
        emit(5, pab[0], g_ab, pab[1], pab[2], place_ab)
        emit(6, pcw[0], gcw_ref[...], pcw[1], pcw[2])
        g_dl = jnp.concatenate([tot[12:14, 0:128] * _sigmoid(-pdl[0, 0:2, :]), jnp.zeros((6, 128), F32)], axis=0)
        emit(7, pdl[0], g_dl, pdl[1], pdl[2])

    row = lambda *shape: jax.ShapeDtypeStruct(shape, F32)
    per_q = [row(1, D)] * 5 + [row(1, 3 * D), row(3, Dq), row(8, 128)]
    res = pl.pallas_call(
        body, name="small_update",
        in_specs=[VMEM_FULL] * 9, out_specs=[VMEM_FULL] * 33,
        out_shape=[row(D, Wm)] + per_q * 4,
        compiler_params=_cparams(None, VMEM_LIMIT),
    )(tot, dm_sh, gcw, da, act, p_row, p_ab, p_cw, p_dl)
    return res[0], [res[1 + 8 * q:1 + 8 * (q + 1)] for q in range(4)]


def _pad_rows(a, rows=8):
    return jnp.pad(a, ((0, rows - a.shape[0]), (0, 0)))


def kernel(x, c, ctx, c_ctx, norm_w, ada_w, ada_b, w_in, conv_w, conv_b, decay_logit, gn_w, w_a, w_b, w_out, final_norm_w, loss_target, m_c_ctx, m_norm_w, m_ada_w, m_ada_b, m_w_in, m_conv_w, m_conv_b, m_decay_logit, m_gn_w, m_w_a, m_w_b, m_w_out, m_final_norm_w, v_c_ctx, v_norm_w, v_ada_w, v_ada_b, v_w_in, v_conv_w, v_conv_b, v_decay_logit, v_gn_w, v_w_a, v_w_b, v_w_out, v_final_norm_w):
    L, D = x.shape[1], x.shape[2]
    H = D // DV
    Wc = w_in.shape[2]
    Do = D // 8
    pos = _position()
    me = _dev_id(pos)
    cidx = jnp.reshape(pos[2], (1,)).astype(jnp.int32)
    sidx = jnp.reshape(_shard_of(pos), (1,)).astype(jnp.int32)

    act, mod, conv_w8 = _fwd_small(_pad_rows(c), _pad_rows(c_ctx[None]), ada_w[0], ada_b, _pad_rows(conv_w[0]))
    mod_x = lax.dynamic_slice_in_dim(mod, me, 1, axis=0).reshape(3, D)
    mod_c = mod[8].reshape(3, D)
    lg = jax.nn.log_sigmoid(decay_logit[0])

    w3_staged = _w3_stage(sidx, *(w[0].reshape(2, Do, D) for w in (w_a, w_b, w_out)))
    csidx = jnp.concatenate([cidx, sidx])
    groups, dret_c, xmt, cmt, dx1, sc_x, w_in_full, gh_3, sts = _local_step(
        x[0], ctx[0], loss_target[0], mod_x, mod_c, norm_w, conv_w8, conv_b, lg, gn_w, final_norm_w[None],
        lambda xm, c2, s2: _ag_in_proj(xm, w_in[0], c2, s2), w3_staged, csidx)
    st_mid, st_conv, st_lg, st_lgc, st_c = sts

    dw_mine, ra_in = _dw_in(xmt, groups, cmt, dret_c, D, True)
    cs_in = _sum_pair_in(dw_mine, ra_in)
    grad_x, st_x, rb_in = _dxm(groups, 0, w_in_full, x[0], norm_w, sc_x, dx1, "dxm_x", _chips_exchange_in(cs_in))
    gh_in, grad_x = _sum_chips_in(csidx, cs_in, rb_in, grad_x)

    zeros3 = jnp.zeros((3, D), F32)
    p_row = jnp.concatenate(
        [r for t in ((c_ctx[None], norm_w, conv_b, gn_w, final_norm_w[None], zeros3),
                     (m_c_ctx[None], m_norm_w, m_conv_b, m_gn_w, m_final_norm_w[None], zeros3),
                     (v_c_ctx[None], v_norm_w, v_conv_b, v_gn_w, v_final_norm_w[None], zeros3)) for r in t],
        axis=0).reshape(3, 8, D)
    p_ab = jnp.concatenate([ada_b, m_ada_b, v_ada_b], axis=0).reshape(3, 3, D)
    p_cw = jnp.concatenate([conv_w, m_conv_w, v_conv_w], axis=0)
    p_dl = jnp.pad(jnp.concatenate([decay_logit, m_decay_logit, v_decay_logit], axis=0), ((0, 0), (0, 6), (0, 128 - H)))
    tot, dm_sh, gcw, da, loss_t, g_in, g_3 = _bwd_small((st_x, st_mid, st_c, st_conv, st_lg, st_lgc), ada_w[0],
                                                        conv_w.shape[2], gh_in, gh_3)
    g_w_in = g_in.reshape(D, Wc)
    g_3 = g_3.reshape(3, D // 4, D)
    g_ada_w, small = _small_update(tot, dm_sh, gcw, da, act, p_row, p_ab, p_cw, p_dl)

    upd_in = _adamw(w_in[0], g_w_in, m_w_in[0], v_w_in[0], "adamw_w_in")
    upd_ada = _adamw(ada_w[0], g_ada_w, m_ada_w[0], v_ada_w[0], "adamw_ada_w")
    upd_a = _adamw(w_a[0], g_3[0], m_w_a[0], v_w_a[0], "adamw_w_a")
    upd_b = _adamw(w_b[0], g_3[1], m_w_b[0], v_w_b[0], "adamw_w_b")
    upd_o = _adamw(w_out[0], g_3[2], m_w_out[0], v_w_out[0], "adamw_w_out")

    def leaves(q):
        big = lambda g, upd: (g if q == 0 else upd[q - 1])[None]
        r_cctx, r_norm, r_convb, r_gn, r_fnorm, r_ab, r_cw, r_dl = small[q]
        return [r_cctx.reshape(D), r_norm, big(g_ada_w, upd_ada), r_ab, big(g_w_in, upd_in),
                r_cw[None], r_convb, r_dl[0:2, 0:H][None], r_gn,
                big(g_3[0], upd_a), big(g_3[1], upd_b), big(g_3[2], upd_o), r_fnorm.reshape(D)]

    loss = loss_t[0, 0]
    return (loss, grad_x[None], *leaves(0), *leaves(1), *leaves(2), *leaves(3))
```

```python
from typing import Callable, NamedTuple

import jax
import jax.numpy as jnp
from jax import lax
from jax.experimental import pallas as pl
from jax.experimental.pallas import tpu as pltpu

F32 = jnp.float32
BF16 = jnp.bfloat16
MESH = pl.DeviceIdType.MESH

CHUNK = 128
RET_CPB = 4
DV = 128
DK = 64
GRID_W = 64
ROPE_BASE = 10000.0
EPS = 1e-6
K_SCALE = DK ** -0.5
N_SHARD = 4
N_DEV = 8

ADAM_LR = 0.001
ADAM_B1 = 0.9
ADAM_B2 = 0.999
ADAM_EPS = 1e-08
ADAM_WD = 0.01
ADAM_STEP = 10

VMEM_LIMIT = 56 * 1024 * 1024


def _cparams(sem=None, vmem=None):
    kw = {}
    if sem is not None:
        kw["dimension_semantics"] = sem
    if vmem is not None:
        kw["vmem_limit_bytes"] = vmem
    return pltpu.CompilerParams(**kw)


def _dot(a, b):
    return jnp.dot(a, b, preferred_element_type=F32)


def _dot_nt(a, b):
    return lax.dot_general(a, b, (((1,), (1,)), ((), ())), preferred_element_type=F32)


def _dot_tn(a, b):
    return lax.dot_general(a, b, (((0,), (0,)), ((), ())), preferred_element_type=F32)


def _sigmoid(x):
    return 1.0 / (1.0 + jnp.exp(-x))


def _sum_all(x):
    return jnp.sum(jnp.sum(x, axis=1, keepdims=True), axis=0, keepdims=True)


def _swap_halves(t):
    n = t.shape[1]
    lane = lax.broadcasted_iota(jnp.int32, t.shape, 1)
    low = (lane & 32) == 0
    return jnp.where(low, pltpu.roll(t, n - 32, 1), pltpu.roll(t, 32, 1))


def _vec_spec(d):
    return pl.BlockSpec((1, d), lambda *a: (0, 0))


def _norm_mod(x, nw, sc, sh, name):
    L, D = x.shape
    tl = min(256, L)

    def body(x_ref, nw_ref, sc_ref, sh_ref, xm_ref, xmt_ref):
        xv = x_ref[...]
        r = lax.rsqrt(jnp.mean(xv * xv, axis=-1, keepdims=True) + EPS)
        xm = (xv * r * nw_ref[...]) * (1.0 + sc_ref[...]) + sh_ref[...]
        xm_ref[...] = xm.astype(BF16)
        xmt_ref[...] = xm.T.astype(BF16)

    return pl.pallas_call(
        body, name=name, grid=(L // tl,),
        in_specs=[pl.BlockSpec((tl, D), lambda i: (i, 0)), _vec_spec(D), _vec_spec(D), _vec_spec(D)],
        out_specs=[pl.BlockSpec((tl, D), lambda i: (i, 0)), pl.BlockSpec((D, tl), lambda i: (0, i))],
        out_shape=[jax.ShapeDtypeStruct((L, D), BF16), jax.ShapeDtypeStruct((D, L), BF16)],
        compiler_params=_cparams(("parallel",)),
    )(x, nw, sc, sh)


QK_BLOCK, V_BLOCK = 4, 5


def _in_proj(xm, w, name, first=0, count=None):
    M, D = xm.shape
    count = w.shape[1] // D if count is None else count
    tm = min(1024, M)

    def body(a_ref, b_ref, o_ref, qk_ref):
        acc = _dot(a_ref[...], b_ref[...])
        o_ref[...] = acc.astype(o_ref.dtype)

        @pl.when(pl.program_id(1) == QK_BLOCK - first)
        def _():
            qk_ref[...] = acc

    return pl.pallas_call(
        body, name=name, grid=(M // tm, count),
        in_specs=[pl.BlockSpec((tm, D), lambda i, j: (i, 0)), pl.BlockSpec((D, D), lambda i, j: (0, first + j))],
        out_specs=[pl.BlockSpec((tm, D), lambda i, j: (i, j)), pl.BlockSpec((tm, D), lambda i, j: (i, 0))],
        out_shape=[jax.ShapeDtypeStruct((M, count * D), BF16), jax.ShapeDtypeStruct((M, D), F32)],
        compiler_params=_cparams(("parallel", "arbitrary")),
    )(xm, w)


def _halo_specs(tl, L, D, col):
    hb = tl // 16
    last = L // 16 - 1
    prev = pl.BlockSpec((16, D), lambda i: (jnp.maximum(i * hb - 1, 0), col))
    nxt = pl.BlockSpec((16, D), lambda i: (jnp.minimum((i + 1) * hb, last), col))
    return prev, nxt


def _shift_rows(u, above, below):
    tl = u.shape[0]
    row = lax.broadcasted_iota(jnp.int32, u.shape, 0)
    dn = jnp.where(row == 0, above, pltpu.roll(u, 1, 0))
    up = jnp.where(row == tl - 1, below, pltpu.roll(u, tl - 1, 0))
    return dn, up


def _rope_tables(L):
    pos = jnp.arange(L)
    row = (pos // GRID_W).astype(F32)
    col = (pos % GRID_W).astype(F32)
    nf = DK // 4
    inv = ROPE_BASE ** (-jnp.arange(nf, dtype=F32) / nf)
    ang = jnp.concatenate([row[:, None] * inv, col[:, None] * inv], axis=-1)
    cos, sin = jnp.cos(ang), jnp.sin(ang)
    return jnp.concatenate([cos, cos, cos, cos], axis=-1), jnp.concatenate([-sin, sin, -sin, sin], axis=-1)


def _smem_spec():
    return pl.BlockSpec(memory_space=pltpu.SMEM)


def _pair_select(e0, e1):
    row = lax.broadcasted_iota(jnp.int32, e0.shape, 0)
    return jnp.where(row < DK, e0, e1)


def _head_lane_mask(shape, e):
    lane = lax.broadcasted_iota(jnp.int32, shape, 1)
    return (lane < DK) if e == 0 else (lane >= DK)


def _ctx_states(pc, pqk_c, lg, D):
    Lc = pc.shape[0]
    H = D // DV

    def body(lg_ref, k_ref, v_ref, s_ref):
        m = lax.broadcasted_iota(jnp.int32, (Lc, DV), 0).astype(F32)
        for pr in range(H // 2):
            k2 = k_ref[:, pr * 128:(pr + 1) * 128].astype(F32) * K_SCALE
            res = [[None, None], [None, None]]
            for e in range(2):
                h = 2 * pr + e
                v = v_ref[:, h * DV:(h + 1) * DV]
                dec_f = jnp.exp(lg_ref[0, h] * (Lc - 1.0 - m))
                dec_b = jnp.exp(lg_ref[1, h] * m)
                res[0][e] = _dot_tn((k2 * dec_f).astype(BF16), v)
                res[1][e] = _dot_tn((k2 * dec_b).astype(BF16), v)
            s_ref[0, pr] = _pair_select(res[0][0], res[0][1])
            s_ref[1, pr] = _pair_select(res[1][0], res[1][1])

    return pl.pallas_call(
        body, name="ctx_states", grid=(1,),
        in_specs=[_smem_spec(), pl.BlockSpec((Lc, D // 2), lambda i: (0, 1)), pl.BlockSpec((Lc, D), lambda i: (0, 1))],
        out_specs=pl.BlockSpec((2, H // 2, 128, 128), lambda i: (0, 0, 0, 0)),
        out_shape=jax.ShapeDtypeStruct((2, H // 2, 128, 128), F32),
    )(lg, pqk_c, pc)


T_M, T_MT = 0, 1
T_MF1, T_MB1 = 2, 3
T_QF, T_QB = 4, 5
T_KF, T_KB = 6, 7


def _decay_tables(lg, H):
    def body(lg_ref, t_ref):
        h = pl.program_id(0)
        lgf, lgb = lg_ref[0, h], lg_ref[1, h]
        i = lax.broadcasted_iota(jnp.int32, (CHUNK, CHUNK), 0).astype(F32)
        j = lax.broadcasted_iota(jnp.int32, (CHUNK, CHUNK), 1).astype(F32)
        d = i - j
        mf = jnp.where(d > 0, jnp.exp(lgf * jnp.maximum(d, 0.0)), 0.0)
        mb = jnp.where(d < 0, jnp.exp(lgb * jnp.maximum(-d, 0.0)), 0.0)
        mf_t = jnp.where(d < 0, jnp.exp(lgf * jnp.maximum(-d, 0.0)), 0.0)
        mb_t = jnp.where(d > 0, jnp.exp(lgb * jnp.maximum(d, 0.0)), 0.0)
        diag = jnp.where(d == 0, 2.0, 0.0)
        t_ref[0, T_M] = mf + mb + diag
        t_ref[0, T_MT] = mf_t + mb_t + diag
        t_ref[0, T_MF1] = mf * d
        t_ref[0, T_MB1] = mb * (-d)
        t_ref[0, T_QF] = jnp.exp(lgf * (i + 1.0))
        t_ref[0, T_QB] = jnp.exp(lgb * (CHUNK - i))
        t_ref[0, T_KF] = jnp.exp(lgf * (CHUNK - 1.0 - i))
        t_ref[0, T_KB] = jnp.exp(lgb * i)

    return pl.pallas_call(
        body, name="decay_tables", grid=(H,), in_specs=[_smem_spec()],
        out_specs=pl.BlockSpec((1, 8, CHUNK, CHUNK), lambda h: (h, 0, 0, 0)),
        out_shape=jax.ShapeDtypeStruct((H, 8, CHUNK, CHUNK), F32),
    )(lg)


def _tab_spec(H):
    return pl.BlockSpec((H, 8, CHUNK, CHUNK), lambda n: (0, 0, 0, 0))


def _chunk_decay(tab_ref, h):
    return tab_ref[h, T_QF, CHUNK - 1:CHUNK, :], tab_ref[h, T_QB, 0:1, :]


def _ret_states(kr, p, s0, tab, D, exchange=None):
    L = kr.shape[0]
    H = D // DV
    N = L // CHUNK
    HP = H // 2

    def body(tab_ref, kf_ref, kb_ref, vf_ref, vb_ref, s0_ref, sf_out, sb_out, sf, sb):
        n = pl.program_id(0)

        @pl.when(n == 0)
        def _():
            sf[...] = s0_ref[0]
            sb[...] = s0_ref[1]

        for cc in range(RET_CPB):
            cf_, cb_ = cc, RET_CPB - 1 - cc
            rf, rb = slice(cf_ * CHUNK, (cf_ + 1) * CHUNK), slice(cb_ * CHUNK, (cb_ + 1) * CHUNK)
            sf_out[cf_] = sf[...]
            sb_out[cb_] = sb[...]
            for pr in range(HP):
                kf2 = kf_ref[rf, pr * 128:(pr + 1) * 128].astype(F32)
                kb2 = kb_ref[rb, pr * 128:(pr + 1) * 128].astype(F32)
                inc_f, inc_b, gf, gb = [], [], [], []
                for e in range(2):
                    h = 2 * pr + e
                    inc_f.append(_dot_tn((kf2 * tab_ref[h, T_KF]).astype(BF16), vf_ref[rf, h * DV:(h + 1) * DV]))
                    inc_b.append(_dot_tn((kb2 * tab_ref[h, T_KB]).astype(BF16), vb_ref[rb, h * DV:(h + 1) * DV]))
                    cf, cb = _chunk_decay(tab_ref, h)
                    gf.append(jnp.broadcast_to(cf, (128, 128)))
                    gb.append(jnp.broadcast_to(cb, (128, 128)))
                sf[pr] = _pair_select(gf[0], gf[1]) * sf[pr] + _pair_select(inc_f[0], inc_f[1])
                sb[pr] = _pair_select(gb[0], gb[1]) * sb[pr] + _pair_select(inc_b[0], inc_b[1])

    st = jax.ShapeDtypeStruct((N, HP, 128, 128), F32)
    R = RET_CPB * CHUNK
    NB = N // RET_CPB
    return _riding_call(
        body, exchange, NB, name="ret_states", args=(tab, kr, kr, p, p, s0),
        in_specs=[_tab_spec(H),
                  pl.BlockSpec((R, D // 2), lambda n: (n, 0)),
                  pl.BlockSpec((R, D // 2), lambda n: (NB - 1 - n, 0)),
                  pl.BlockSpec((R, D), lambda n: (n, 5)),
                  pl.BlockSpec((R, D), lambda n: (NB - 1 - n, 5)),
                  pl.BlockSpec((2, HP, 128, 128), lambda n: (0, 0, 0, 0))],
        out_specs=[pl.BlockSpec((RET_CPB, HP, 128, 128), lambda n: (n, 0, 0, 0)),
                   pl.BlockSpec((RET_CPB, HP, 128, 128), lambda n: (NB - 1 - n, 0, 0, 0))],
        out_shape=[st, st],
        scratch=[pltpu.VMEM((HP, 128, 128), F32), pltpu.VMEM((HP, 128, 128), F32)],
        cparams=_cparams(("arbitrary",)))


def _ret_out(qr, kr, p, sf_prev, sb_prev, gn_w, tab, D, exchange=None):
    L = qr.shape[0]
    H = D // DV
    N = L // CHUNK
    HP = H // 2

    def body(tab_ref, q_ref, k_ref, v_ref, zb_ref, sf_ref, sb_ref, gn_ref, o_ref, yb_ref):
        def chunk(cc, carry):
            rows = pl.ds(pl.multiple_of(cc * CHUNK, CHUNK), CHUNK)
            for pr in range(HP):
                q2 = q_ref[rows, pr * 128:(pr + 1) * 128]
                k2 = k_ref[rows, pr * 128:(pr + 1) * 128]
                sfp = sf_ref[cc, pr].astype(BF16)
                sbp = sb_ref[cc, pr].astype(BF16)
                for e in range(2):
                    h = 2 * pr + e
                    sl = slice(h * DV, (h + 1) * DV)
                    qm = jnp.where(_head_lane_mask(q2.shape, e), q2, jnp.zeros_like(q2))
                    a = (_dot_nt(qm, k2) * tab_ref[h, T_M]).astype(BF16)
                    qf = qm.astype(F32)
                    o = _dot(a, v_ref[rows, sl])
                    o += _dot((qf * tab_ref[h, T_QF]).astype(BF16), sfp)
                    o += _dot((qf * tab_ref[h, T_QB]).astype(BF16), sbp)
                    o_ref[rows, sl] = o
                    mu = jnp.mean(o, axis=-1, keepdims=True)
                    oc = o - mu
                    rstd = lax.rsqrt(jnp.mean(oc * oc, axis=-1, keepdims=True) + EPS)
                    zb = zb_ref[rows, sl].astype(F32)
                    yb_ref[rows, sl] = (zb * _sigmoid(zb) * (oc * rstd * gn_ref[:, sl])).astype(BF16)
            return carry

        lax.fori_loop(0, RET_CPB, chunk, 0)

    R = RET_CPB * CHUNK
    return _riding_call(
        body, exchange, N // RET_CPB, name="ret_out", args=(tab, qr, kr, p, p, sf_prev, sb_prev, gn_w),
        in_specs=[_tab_spec(H),
                  pl.BlockSpec((R, D // 2), lambda n: (n, 0)),
                  pl.BlockSpec((R, D // 2), lambda n: (n, 0)),
                  pl.BlockSpec((R, D), lambda n: (n, 5)),
                  pl.BlockSpec((R, D), lambda n: (n, 6)),
                  pl.BlockSpec((RET_CPB, HP, 128, 128), lambda n: (n, 0, 0, 0)),
                  pl.BlockSpec((RET_CPB, HP, 128, 128), lambda n: (n, 0, 0, 0)),
                  _vec_spec(D)],
        out_specs=[pl.BlockSpec((R, D), lambda n: (n, 0)), pl.BlockSpec((R, D), lambda n: (n, 0))],
        out_shape=[jax.ShapeDtypeStruct((L, D), F32), jax.ShapeDtypeStruct((L, D), BF16)],
        cparams=_cparams(("arbitrary",)))


def _mid(p, yb, o, x, tgt, w3, g, fw, conv_w, conv_b, gn_w, D):
    L = x.shape[0]
    H = D // DV
    tm = min(256, L)
    nt = L // tm

    def body(h_ref, bg_ref, cg_ref, za_ref, hp_ref, hn_ref, cp_ref, cn_ref, yb_ref, ga_ref, gb_ref, zb_ref, o_ref,
             x_ref, t_ref, w_hbm, g_ref, fw_ref, cw_ref, cb_ref, gn_ref,
             dx1_ref, dya_ref, do_ref, dzb_ref, dgab_ref, dw_hbm, st_ref, w_vm, dw_acc, sem):
        i = pl.program_id(0)

        @pl.when(i == 0)
        def _():
            cp = pltpu.make_async_copy(w_hbm, w_vm, sem)
            cp.start()
            dw_acc[...] = jnp.zeros_like(dw_acc)
            st_ref[...] = jnp.zeros_like(st_ref)
            cp.wait()

        u = cg_ref[...].astype(F32) * h_ref[...].astype(F32)
        above = jnp.where(i == 0, 0.0, cp_ref[15:16, :].astype(F32) * hp_ref[15:16, :].astype(F32))
        below = jnp.where(i == nt - 1, 0.0, cn_ref[0:1, :].astype(F32) * hn_ref[0:1, :].astype(F32))
        dn, up = _shift_rows(u, above, below)
        co = cw_ref[0:1, :] * dn + cw_ref[1:2, :] * u + cw_ref[2:3, :] * up + cb_ref[...]
        za = za_ref[...].astype(F32)
        ya_b = (za * _sigmoid(za) * bg_ref[...].astype(F32) * co).astype(BF16)
        yb_b = yb_ref[...]
        y_a = _dot(ya_b, w_vm[0])
        y_b = _dot(yb_b, w_vm[1])
        sga = _sigmoid(ga_ref[...].astype(F32))
        sgb = _sigmoid(gb_ref[...].astype(F32))
        mix_b = (sga * y_a + sgb * y_b).astype(BF16)
        y_x = _dot(mix_b, w_vm[2])
        gvec, fwv = g_ref[...], fw_ref[...]
        x1 = x_ref[...] + gvec * y_x
        r1 = lax.rsqrt(jnp.mean(x1 * x1, axis=-1, keepdims=True) + EPS)
        xh = x1 * r1
        diff = xh * fwv - t_ref[...]
        dout = diff * (1.0 / D)
        dxh = dout * fwv
        dx1 = r1 * (dxh - xh * jnp.mean(dxh * xh, axis=-1, keepdims=True))
        dx1_ref[...] = dx1
        st_ref[0:1, :] += jnp.sum(dout * xh, axis=0, keepdims=True)
        st_ref[1:2, :] += jnp.sum(dx1 * y_x, axis=0, keepdims=True)
        st_ref[2:3, :] += jnp.sum(diff * diff, axis=0, keepdims=True)
        dyx_b = (dx1 * gvec).astype(BF16)
        dmix = _dot_nt(dyx_b, w_vm[2])
        dw_acc[2] += _dot_tn(mix_b, dyx_b)
        dya_b = (dmix * sga).astype(BF16)
        dyb_b = (dmix * sgb).astype(BF16)
        dgab_ref[:, 0:D] = (dmix * y_a * sga * (1.0 - sga)).astype(BF16)
        dgab_ref[:, D:2 * D] = (dmix * y_b * sgb * (1.0 - sgb)).astype(BF16)
        dya_ref[...] = _dot_nt(dya_b, w_vm[0])
        dyb = _dot_nt(dyb_b, w_vm[1])
        dw_acc[0] += _dot_tn(ya_b, dya_b)
        dw_acc[1] += _dot_tn(yb_b, dyb_b)

        for h in range(H):
            sl = slice(h * DV, (h + 1) * DV)
            ov = o_ref[:, sl]
            oc = ov - jnp.mean(ov, axis=-1, keepdims=True)
            rstd = lax.rsqrt(jnp.mean(oc * oc, axis=-1, keepdims=True) + EPS)
            rn = oc * rstd
            gw = gn_ref[:, sl]
            zb = zb_ref[:, sl].astype(F32)
            sz = _sigmoid(zb)
            dy = dyb[:, sl]
            dzb_ref[:, sl] = (dy * (rn * gw) * (sz * (1.0 + zb * (1.0 - sz)))).astype(BF16)
            dretn = dy * (zb * sz)
            st_ref[3:4, sl] += jnp.sum(dretn * rn, axis=0, keepdims=True)
            drn = dretn * gw
            do_ref[:, sl] = (rstd * (drn - jnp.mean(drn, axis=-1, keepdims=True)
                                     - rn * jnp.mean(drn * rn, axis=-1, keepdims=True))).astype(BF16)

        @pl.when(i == nt - 1)
        def _():
            out = pltpu.make_async_copy(dw_acc, dw_hbm, sem)
            out.start()
            out.wait()

    row = lambda col: pl.BlockSpec((tm, D), lambda i: (i, col))
    any_spec = pl.BlockSpec(memory_space=pl.ANY)
    f32o = jax.ShapeDtypeStruct((L, D), F32)
    bf16o = jax.ShapeDtypeStruct((L, D), BF16)
    hp, hn = _halo_specs(tm, L, D, 0)
    cp, cn = _halo_specs(tm, L, D, 2)
    return pl.pallas_call(
        body, name="mid", grid=(nt,),
        in_specs=[row(0), row(1), row(2), row(3), hp, hn, cp, cn, row(0), row(7), row(8), row(6), row(0),
                  row(0), row(0), any_spec, _vec_spec(D), _vec_spec(D),
                  pl.BlockSpec((8, D), lambda i: (0, 0)), _vec_spec(D), _vec_spec(D)],
        out_specs=[row(0), row(0), row(0), row(0), pl.BlockSpec((tm, 2 * D), lambda i: (i, 0)), any_spec,
                   pl.BlockSpec((8, D), lambda i: (0, 0))],
        out_shape=[f32o, f32o, bf16o, bf16o, jax.ShapeDtypeStruct((L, 2 * D), BF16),
                   jax.ShapeDtypeStruct((3, D, D), F32), jax.ShapeDtypeStruct((8, D), F32)],
        scratch_shapes=[pltpu.VMEM((3, D, D), BF16), pltpu.VMEM((3, D, D), F32), pltpu.SemaphoreType.DMA],
        compiler_params=_cparams(("arbitrary",), VMEM_LIMIT),
    )(p, p, p, p, p, p, p, p, yb, p, p, p, o, x, tgt, w3, g, fw, conv_w, conv_b, gn_w)


def _conv_bwd(dya, p, conv_w, conv_b, D, exchange=None):
    L = p.shape[0]
    tl = min(256, L)
    nt = L // tl

    def body(d_ref, h_ref, bg_ref, cg_ref, za_ref,
             dp_ref, dn_ref, hp_ref, hn_ref, bp_ref, bn_ref, cp_ref, cn_ref, zp_ref, zn_ref,
             w_ref, b_ref, dc_ref, st_ref):
        i = pl.program_id(0)

        @pl.when(i == 0)
        def _():
            st_ref[...] = jnp.zeros_like(st_ref)

        first, last = i == 0, i == nt - 1
        h = h_ref[...].astype(F32)
        cg = cg_ref[...].astype(F32)
        bg = bg_ref[...].astype(F32)
        za = za_ref[...].astype(F32)
        dy = d_ref[...].astype(F32)
        u = cg * h
        u_above = jnp.where(first, 0.0, cp_ref[15:16, :].astype(F32) * hp_ref[15:16, :].astype(F32))
        u_below = jnp.where(last, 0.0, cn_ref[0:1, :].astype(F32) * hn_ref[0:1, :].astype(F32))
        u_dn, u_up = _shift_rows(u, u_above, u_below)
        w0, w1, w2 = w_ref[0:1, :], w_ref[1:2, :], w_ref[2:3, :]
        co = w0 * u_dn + w1 * u + w2 * u_up + b_ref[...]
        sz = _sigmoid(za)
        silu = za * sz
        dc_ref[:, 3 * D:4 * D] = (dy * bg * co * (sz * (1.0 + za * (1.0 - sz)))).astype(BF16)
        dc_ref[:, D:2 * D] = (dy * silu * co).astype(BF16)
        dco = dy * silu * bg

        def edge(dr, zr, br, r):
            z = zr[r:r + 1, :].astype(F32)
            return dr[r:r + 1, :].astype(F32) * (z * _sigmoid(z)) * br[r:r + 1, :].astype(F32)

        dco_above = jnp.where(first, 0.0, edge(dp_ref, zp_ref, bp_ref, 15))
        dco_below = jnp.where(last, 0.0, edge(dn_ref, zn_ref, bn_ref, 0))
        dco_dn, dco_up = _shift_rows(dco, dco_above, dco_below)
        du = w0 * dco_up + w1 * dco + w2 * dco_dn
        dc_ref[:, 2 * D:3 * D] = (du * h).astype(BF16)
        dc_ref[:, 0:D] = (du * cg).astype(BF16)
        st_ref[0:1, :] += jnp.sum(dco * u_dn, axis=0, keepdims=True)
        st_ref[1:2, :] += jnp.sum(dco * u, axis=0, keepdims=True)
        st_ref[2:3, :] += jnp.sum(dco * u_up, axis=0, keepdims=True)
        st_ref[3:4, :] += jnp.sum(dco, axis=0, keepdims=True)

    main = lambda col: pl.BlockSpec((tl, D), lambda i: (i, col))
    halos = []
    for col in (0, 0, 1, 2, 3):
        halos.extend(_halo_specs(tl, L, D, col))
    return _riding_call(
        body, exchange, nt, name="conv_bwd",
        args=(dya, p, p, p, p, dya, dya, p, p, p, p, p, p, p, p, conv_w, conv_b),
        in_specs=[main(0), main(0), main(1), main(2), main(3)] + halos
                 + [pl.BlockSpec((8, D), lambda i: (0, 0)), _vec_spec(D)],
        out_specs=[pl.BlockSpec((tl, 4 * D), lambda i: (i, 0)), pl.BlockSpec((8, D), lambda i: (0, 0))],
        out_shape=[jax.ShapeDtypeStruct((L, 4 * D), BF16), jax.ShapeDtypeStruct((8, D), F32)],
        cparams=_cparams(("arbitrary",)))


def _ret_bwd_states(qr, do, tab, D):
    L = qr.shape[0]
    H = D // DV
    N = L // CHUNK
    HP = H // 2

    def body(tab_ref, qf_ref, qb_ref, dof_ref, dob_ref, dsf_out, dsb_out, ds0_out, dsf, dsb):
        n = pl.program_id(0)

        @pl.when(n == 0)
        def _():
            dsf[...] = jnp.zeros_like(dsf)
            dsb[...] = jnp.zeros_like(dsb)

        for cc in range(RET_CPB):
            cf_, cb_ = RET_CPB - 1 - cc, cc
            rf, rb = slice(cf_ * CHUNK, (cf_ + 1) * CHUNK), slice(cb_ * CHUNK, (cb_ + 1) * CHUNK)
            dsf_out[cf_] = dsf[...]
            dsb_out[cb_] = dsb[...]
            for pr in range(HP):
                qf2 = qf_ref[rf, pr * 128:(pr + 1) * 128].astype(F32)
                qb2 = qb_ref[rb, pr * 128:(pr + 1) * 128].astype(F32)
                inc_f, inc_b, gf, gb = [], [], [], []
                for e in range(2):
                    h = 2 * pr + e
                    inc_f.append(_dot_tn((qf2 * tab_ref[h, T_QF]).astype(BF16), dof_ref[rf, h * DV:(h + 1) * DV]))
                    inc_b.append(_dot_tn((qb2 * tab_ref[h, T_QB]).astype(BF16), dob_ref[rb, h * DV:(h + 1) * DV]))
                    cf, cb = _chunk_decay(tab_ref, h)
                    gf.append(jnp.broadcast_to(cf, (128, 128)))
                    gb.append(jnp.broadcast_to(cb, (128, 128)))
                dsf[pr] = _pair_select(gf[0], gf[1]) * dsf[pr] + _pair_select(inc_f[0], inc_f[1])
                dsb[pr] = _pair_select(gb[0], gb[1]) * dsb[pr] + _pair_select(inc_b[0], inc_b[1])

        @pl.when(n == NB - 1)
        def _():
            ds0_out[0] = dsf[...]
            ds0_out[1] = dsb[...]

    st = jax.ShapeDtypeStruct((N, HP, 128, 128), F32)
    R = RET_CPB * CHUNK
    NB = N // RET_CPB
    return pl.pallas_call(
        body, name="ret_bwd_states", grid=(NB,),
        in_specs=[_tab_spec(H),
                  pl.BlockSpec((R, D // 2), lambda n: (NB - 1 - n, 0)),
                  pl.BlockSpec((R, D // 2), lambda n: (n, 0)),
                  pl.BlockSpec((R, D), lambda n: (NB - 1 - n, 0)),
                  pl.BlockSpec((R, D), lambda n: (n, 0))],
        out_specs=[pl.BlockSpec((RET_CPB, HP, 128, 128), lambda n: (NB - 1 - n, 0, 0, 0)),
                   pl.BlockSpec((RET_CPB, HP, 128, 128), lambda n: (n, 0, 0, 0)),
                   pl.BlockSpec((2, HP, 128, 128), lambda n: (0, 0, 0, 0))],
        out_shape=[st, st, jax.ShapeDtypeStruct((2, HP, 128, 128), F32)],
        scratch_shapes=[pltpu.VMEM((HP, 128, 128), F32), pltpu.VMEM((HP, 128, 128), F32)],
        compiler_params=_cparams(("arbitrary",)),
    )(tab, qr, qr, do, do)


def _ret_bwd_main(qr, kr, p, do, sf_prev, sb_prev, dsf, dsb, c2, s2, tab, D, exchange=None):
    L = qr.shape[0]
    H = D // DV
    N = L // CHUNK
    HP = H // 2
    W = D // 2

    def body(tab_ref, q_ref, k_ref, v_ref, do_ref, sf_ref, sb_ref, dsf_ref, dsb_ref, c_ref, s_ref,
             dr_ref, st_ref, dl_acc):
        @pl.when(pl.program_id(0) == 0)
        def _():
            dl_acc[...] = jnp.zeros_like(dl_acc)

        i = lax.broadcasted_iota(jnp.int32, (CHUNK, 128), 0).astype(F32)
        rowid = lax.broadcasted_iota(jnp.int32, (128, 128), 0)

        def chunk(cc, carry):
            rows = pl.ds(pl.multiple_of(cc * CHUNK, CHUNK), CHUNK)
            c, s = c_ref[rows, :], s_ref[rows, :]
            for pr in range(HP):
                ps = slice(pr * 128, (pr + 1) * 128)
                q2, k2 = q_ref[rows, ps], k_ref[rows, ps]
                sf32, sb32 = sf_ref[cc, pr], sb_ref[cc, pr]
                dsf32, dsb32 = dsf_ref[cc, pr], dsb_ref[cc, pr]
                sfp, sbp = sf32.astype(BF16), sb32.astype(BF16)
                dsfp, dsbp = dsf32.astype(BF16), dsb32.astype(BF16)
                dq2 = jnp.zeros((CHUNK, 128), F32)
                dk2 = jnp.zeros((CHUNK, 128), F32)
                for e in range(2):
                    h = 2 * pr + e
                    sl = slice(h * DV, (h + 1) * DV)
                    hm = _head_lane_mask(q2.shape, e)
                    qm = jnp.where(hm, q2, jnp.zeros_like(q2))
                    km = jnp.where(hm, k2, jnp.zeros_like(k2))
                    qf, kf = qm.astype(F32), km.astype(F32)
                    v, do = v_ref[rows, sl], do_ref[rows, sl]
                    vf, dof = v.astype(F32), do.astype(F32)
                    m_t = tab_ref[h, T_MT]
                    sc = _dot_nt(qm, k2)
                    dpm = _dot_nt(do, v)
                    dsc = (dpm * tab_ref[h, T_M]).astype(BF16)
                    a_t = (_dot_nt(km, q2) * m_t).astype(BF16)
                    dsc_t = (_dot_nt(v, do) * m_t).astype(BF16)
                    dq_f, dq_b = tab_ref[h, T_QF], tab_ref[h, T_QB]
                    dk_f, dk_b = tab_ref[h, T_KF], tab_ref[h, T_KB]
                    dq = _dot(dsc, km)
                    dq += jnp.where(hm, dq_f * _dot_nt(do, sfp) + dq_b * _dot_nt(do, sbp), 0.0)
                    dk = _dot(dsc_t, qm)
                    dk += jnp.where(hm, dk_f * _dot_nt(v, dsfp) + dk_b * _dot_nt(v, dsbp), 0.0)
                    kdf = _dot((kf * dk_f).astype(BF16), dsfp)
                    kdb = _dot((kf * dk_b).astype(BF16), dsbp)
                    dr_ref[rows, D + h * DV:D + (h + 1) * DV] = (_dot(a_t, do) + kdf + kdb).astype(BF16)
                    dq2 += dq
                    dk2 += dk
                    xf = _dot((qf * dq_f).astype(BF16), sfp)
                    xb = _dot((qf * dq_b).astype(BF16), sbp)
                    pair = (rowid < DK) if e == 0 else (rowid >= DK)
                    gcf, gcb = tab_ref[h, T_QF, CHUNK - 1:CHUNK, 0:1], tab_ref[h, T_QB, 0:1, 0:1]
                    scdp = sc * dpm
                    dl_acc[h, 0] += scdp * tab_ref[h, T_MF1] + xf * dof * (i + 1.0) \
                        + kdf * vf * (CHUNK - 1.0 - i) + (CHUNK * gcf) * jnp.where(pair, dsf32 * sf32, 0.0)
                    dl_acc[h, 1] += scdp * tab_ref[h, T_MB1] + xb * dof * (CHUNK - i) \
                        + kdb * vf * i + (CHUNK * gcb) * jnp.where(pair, dsb32 * sb32, 0.0)
                dr_ref[rows, ps] = (dq2 * c - _swap_halves(dq2) * s).astype(BF16)
                dr_ref[rows, W + pr * 128:W + (pr + 1) * 128] = \
                    ((dk2 * c - _swap_halves(dk2) * s) * K_SCALE).astype(BF16)
            return carry

        lax.fori_loop(0, RET_CPB, chunk, 0)

        @pl.when(pl.program_id(0) == N // RET_CPB - 1)
        def _():
            lane = lax.broadcasted_iota(jnp.int32, (1, 128), 1)
            acc = [jnp.zeros((1, 128), F32), jnp.zeros((1, 128), F32)]
            for h in range(H):
                for b in range(2):
                    acc[b] += jnp.where(lane == h, _sum_all(dl_acc[h, b]), 0.0)
            st_ref[...] = jnp.zeros_like(st_ref)
            st_ref[0:1, :] = acc[0]
            st_ref[1:2, :] = acc[1]

    R = RET_CPB * CHUNK
    st_spec = pl.BlockSpec((RET_CPB, HP, 128, 128), lambda n: (n, 0, 0, 0))
    half = pl.BlockSpec((R, W), lambda n: (n, 0))
    rope = pl.BlockSpec((R, 128), lambda n: (n, 0))
    return _riding_call(
        body, exchange, N // RET_CPB, name="ret_bwd_main",
        args=(tab, qr, kr, p, do, sf_prev, sb_prev, dsf, dsb, c2, s2),
        in_specs=[_tab_spec(H), half, half,
                  pl.BlockSpec((R, D), lambda n: (n, 5)),
                  pl.BlockSpec((R, D), lambda n: (n, 0)),
                  st_spec, st_spec, st_spec, st_spec, rope, rope],
        out_specs=[pl.BlockSpec((R, 2 * D), lambda n: (n, 0)),
                   pl.BlockSpec((8, 128), lambda n: (0, 0))],
        out_shape=[jax.ShapeDtypeStruct((L, 2 * D), BF16), jax.ShapeDtypeStruct((8, 128), F32)],
        scratch=[pltpu.VMEM((H, 2, CHUNK, 128), F32)],
        cparams=_cparams(("arbitrary",)))


def _ctx_bwd(pc, pqk_c, ds0, lg, D):
    Lc = pc.shape[0]
    H = D // DV
    HP = H // 2
    W = D // 2

    def body(lg_ref, k_ref, v_ref, ds_ref, dr_ref, st_ref):
        dqk_ref = dr_ref.at[:, 0:D]
        dv_ref = dr_ref.at[:, D:2 * D]
        m = lax.broadcasted_iota(jnp.int32, (Lc, 128), 0).astype(F32)
        lane = lax.broadcasted_iota(jnp.int32, (1, 128), 1)
        acc_f = jnp.zeros((1, 128), F32)
        acc_b = jnp.zeros((1, 128), F32)
        dqk_ref[:, 0:W] = jnp.zeros((Lc, W), BF16)
        for pr in range(HP):
            ps = slice(pr * 128, (pr + 1) * 128)
            k2 = k_ref[:, ps].astype(F32) * K_SCALE
            dsfp, dsbp = ds_ref[0, pr].astype(BF16), ds_ref[1, pr].astype(BF16)
            dk2 = jnp.zeros((Lc, 128), F32)
            for e in range(2):
                h = 2 * pr + e
                sl = slice(h * DV, (h + 1) * DV)
                hm = _head_lane_mask(k2.shape, e)
                km = jnp.where(hm, k2, 0.0)
                v = v_ref[:, sl]
                vf = v.astype(F32)
                dec_f = jnp.exp(lg_ref[0, h] * (Lc - 1.0 - m))
                dec_b = jnp.exp(lg_ref[1, h] * m)
                kdf = _dot((km * dec_f).astype(BF16), dsfp)
                kdb = _dot((km * dec_b).astype(BF16), dsbp)
                dv_ref[:, sl] = (kdf + kdb).astype(BF16)
                dk2 += jnp.where(hm, dec_f * _dot_nt(v, dsfp) + dec_b * _dot_nt(v, dsbp), 0.0)
                acc_f += jnp.where(lane == h, _sum_all(kdf * vf * (Lc - 1.0 - m)), 0.0)
                acc_b += jnp.where(lane == h, _sum_all(kdb * vf * m), 0.0)
            dqk_ref[:, W + pr * 128:W + (pr + 1) * 128] = (dk2 * K_SCALE).astype(BF16)
        st_ref[...] = jnp.zeros_like(st_ref)
        st_ref[0:1, :] = acc_f
        st_ref[1:2, :] = acc_b

    return pl.pallas_call(
        body, name="ctx_bwd", grid=(1,),
        in_specs=[_smem_spec(), pl.BlockSpec((Lc, W), lambda i: (0, 1)), pl.BlockSpec((Lc, D), lambda i: (0, 1)),
                  pl.BlockSpec((2, HP, 128, 128), lambda i: (0, 0, 0, 0))],
        out_specs=[pl.BlockSpec((Lc, 2 * D), lambda i: (0, 0)), pl.BlockSpec((8, 128), lambda i: (0, 0))],
        out_shape=[jax.ShapeDtypeStruct((Lc, 2 * D), BF16), jax.ShapeDtypeStruct((8, 128), F32)],
    )(lg, pqk_c, pc, ds0)


class _Exchange(NamedTuple):
    inputs: tuple
    out_shapes: tuple
    n_copies: int
    build: Callable
    in_place: bool = False


def _exchange_parts(exchange):
    if exchange is None:
        return [], [], [], [], []
    n = exchange.n_copies
    return (list(exchange.inputs), [ANY] * len(exchange.inputs), list(exchange.out_shapes),
            [ANY] * len(exchange.out_shapes), [pltpu.SemaphoreType.DMA((n,)), pltpu.SemaphoreType.DMA((n,))])


def _riding_call(body, exchange, n_steps, *, args, in_specs, out_specs, out_shape, name, cparams, scratch=()):
    ex_args, ex_in_specs, ex_shapes, ex_out_specs, ex_scratch = _exchange_parts(exchange)
    n_in, n_out, n_sc = len(args), len(out_shape), len(scratch)

    def riding(*refs):
        k = n_in + len(ex_args)
        ins, ex_in = refs[:n_in], refs[n_in:k]
        outs, ex_out = refs[k:k + n_out], refs[k + n_out:k + n_out + len(ex_shapes)]
        k += n_out + len(ex_shapes)
        own_scratch, ex_sems = refs[k:k + n_sc], refs[k + n_sc:]
        step = pl.program_id(0)
        if exchange is not None:
            @pl.when(step == 0)
            def _():
                for rc in exchange.build(ex_in, ex_out, *ex_sems):
                    rc.start()
        body(*ins, *outs, *own_scratch)
        if exchange is not None:
            @pl.when(step == n_steps - 1)
            def _():
                for rc in exchange.build(ex_in, ex_out, *ex_sems):
                    rc.wait()

    aliases = {}
    if exchange is not None and exchange.in_place:
        aliases = {n_in + i: n_out + i for i in range(len(ex_args))}
    return tuple(pl.pallas_call(
        riding, name=name, grid=(n_steps,),
        in_specs=list(in_specs) + ex_in_specs, out_specs=list(out_specs) + ex_out_specs,
        out_shape=list(out_shape) + ex_shapes, scratch_shapes=list(scratch) + ex_scratch,
        input_output_aliases=aliases, compiler_params=cparams,
    )(*args, *ex_args))


def _dxm(groups, col0, w, x, nw, sc, dx1, name, exchange=None):
    L, D = x.shape
    tm = min(256, L)
    nt = L // tm
    ng = len(groups)
    widths = [g.shape[1] for g in groups]
    wtot = sum(widths)
    with_dx = dx1 is not None
    ex_args, ex_in_specs, ex_shapes, ex_out_specs, ex_scratch = _exchange_parts(exchange)
    n_in = ng + 4 + (1 if with_dx else 0)
    n_out = 2 if with_dx else 1

    def body(*refs):
        group_refs = refs[:ng]
        w_hbm, x_ref, nw_ref, sc_ref = refs[ng:ng + 4]
        ex_in = refs[n_in:n_in + len(ex_args)]
        outs = refs[n_in + len(ex_args):]
        if with_dx:
            dx1_ref, gx_ref, st_ref = refs[ng + 4], outs[0], outs[1]
        else:
            st_ref = outs[0]
        ex_out = outs[n_out:n_out + len(ex_shapes)]
        w_vm, sem = outs[n_out + len(ex_shapes):n_out + len(ex_shapes) + 2]
        ex_sems = outs[n_out + len(ex_shapes) + 2:]
        i = pl.program_id(0)

        @pl.when(i == 0)
        def _():
            cp = pltpu.make_async_copy(w_hbm.at[:, col0 * D:col0 * D + wtot], w_vm, sem)
            cp.start()
            if exchange is not None:
                for rc in exchange.build(ex_in, ex_out, *ex_sems):
                    rc.start()
            st_ref[...] = jnp.zeros_like(st_ref)
            cp.wait()

        dxm, off = None, 0
        for g_ref, wd in zip(group_refs, widths):
            part = _dot_nt(g_ref[...], w_vm[:, off:off + wd])
            dxm = part if dxm is None else dxm + part
            off += wd

        xv = x_ref[...]
        r = lax.rsqrt(jnp.mean(xv * xv, axis=-1, keepdims=True) + EPS)
        xh = xv * r
        nwv = nw_ref[...]
        dxn = dxm * (1.0 + sc_ref[...])
        st_ref[0:1, :] += jnp.sum(dxm, axis=0, keepdims=True)
        st_ref[1:2, :] += jnp.sum(dxm * (xh * nwv), axis=0, keepdims=True)
        st_ref[2:3, :] += jnp.sum(dxn * xh, axis=0, keepdims=True)
        if with_dx:
            dxh = dxn * nwv
            gx_ref[...] = dx1_ref[...] + r * (dxh - xh * jnp.mean(dxh * xh, axis=-1, keepdims=True))

        if exchange is not None:
            @pl.when(i == nt - 1)
            def _():
                for rc in exchange.build(ex_in, ex_out, *ex_sems):
                    rc.wait()

    row = pl.BlockSpec((tm, D), lambda i: (i, 0))
    in_specs = [pl.BlockSpec((tm, wd), lambda i: (i, 0)) for wd in widths] + [ANY, row, _vec_spec(D), _vec_spec(D)]
    out_specs = [pl.BlockSpec((8, D), lambda i: (0, 0))]
    out_shape = [jax.ShapeDtypeStruct((8, D), F32)]
    args = list(groups) + [w, x, nw, sc]
    if with_dx:
        in_specs.append(row)
        out_specs.insert(0, row)
        out_shape.insert(0, jax.ShapeDtypeStruct((L, D), F32))
        args.append(dx1)
    res = pl.pallas_call(
        body, name=name, grid=(nt,),
        in_specs=in_specs + ex_in_specs, out_specs=out_specs + ex_out_specs, out_shape=out_shape + ex_shapes,
        scratch_shapes=[pltpu.VMEM((D, wtot), BF16), pltpu.SemaphoreType.DMA] + ex_scratch,
        compiler_params=_cparams(("arbitrary",), VMEM_LIMIT),
    )(*args, *ex_args)
    gx = res[0] if with_dx else None
    return (gx, res[n_out - 1], *res[n_out:])


DW_TN = 256
DW_RING = 4


def _dw_in(xmt, groups, cmt, dr_c, D, pair):
    L = xmt.shape[1]
    Lc = cmt.shape[1]
    Dh = D // 2
    tn = min(DW_TN, D)
    nblk = [g.shape[1] // tn for g in groups]
    starts = [sum(nblk[:g]) for g in range(len(groups))]
    ng = len(groups)
    nj = sum(nblk)
    rows_out = Dh if pair else D

    def body(*refs):
        xt_hbm = refs[0]
        group_refs = refs[1:1 + ng]
        ct_hbm, drc_ref, o_ref = refs[1 + ng:4 + ng]
        rest = refs[4 + ng:]
        if pair:
            ra_hbm, xt_vm, ct_vm, loc, ring, s_send, s_recv = rest
            pos = _position()
            sib = _peer(pos, 1)
        else:
            xt_vm, ct_vm, loc = rest
        j = pl.program_id(0)

        @pl.when(j == 0)
        def _():
            if pair:
                c = pos[2]
                other = pl.ds(pl.multiple_of((1 - c) * Dh, Dh), Dh)
                mine = pl.ds(pl.multiple_of(c * Dh, Dh), Dh)
                cps = [pltpu.make_async_copy(xt_hbm.at[other, :], xt_vm.at[0:Dh, :], loc.at[0]),
                       pltpu.make_async_copy(xt_hbm.at[mine, :], xt_vm.at[Dh:D, :], loc.at[1]),
                       pltpu.make_async_copy(ct_hbm.at[other, :], ct_vm.at[0:Dh, :], loc.at[2]),
                       pltpu.make_async_copy(ct_hbm.at[mine, :], ct_vm.at[Dh:D, :], loc.at[3])]
            else:
                cps = [pltpu.make_async_copy(xt_hbm, xt_vm, loc.at[0]), pltpu.make_async_copy(ct_hbm, ct_vm, loc.at[1])]
            for cp in cps:
                cp.start()
            for cp in cps:
                cp.wait()

        def send(slot):
            cols = pl.ds(pl.multiple_of(j * tn, 128), tn)
            return pltpu.make_async_remote_copy(src_ref=ring.at[slot], dst_ref=ra_hbm.at[:, cols],
                                                send_sem=s_send.at[slot], recv_sem=s_recv,
                                                device_id=sib, device_id_type=MESH)

        for g in range(ng):
            @pl.when((j >= starts[g]) & (j < starts[g] + nblk[g]))
            def _(g=g):
                acc = _dot(xt_vm[...], group_refs[g][...])
                if g == 1:
                    acc += _dot(ct_vm[...], drc_ref[...])
                if not pair:
                    o_ref[...] = acc
                    return
                o_ref[...] = acc[Dh:, :]
                slot = lax.rem(j, DW_RING)

                @pl.when(j >= DW_RING)
                def _():
                    send(slot).wait_send()

                ring[slot] = acc[0:Dh, :]
                send(slot).start()

        if pair:
            @pl.when(j == nj - 1)
            def _():
                pltpu.make_async_remote_copy(src_ref=ra_hbm, dst_ref=ra_hbm, send_sem=s_send.at[0], recv_sem=s_recv,
                                             device_id=sib, device_id_type=MESH).wait_recv()
                for slot in range(DW_RING):
                    send(slot).wait_send()

    def group_spec(g, rows):
        return pl.BlockSpec((rows, tn), lambda j: (0, jnp.clip(j - starts[g], 0, nblk[g] - 1)))

    out_specs = [pl.BlockSpec((rows_out, tn), lambda j: (0, j))]
    out_shape = [jax.ShapeDtypeStruct((rows_out, nj * tn), F32)]
    scratch = [pltpu.VMEM((D, L), BF16), pltpu.VMEM((D, Lc), BF16), pltpu.SemaphoreType.DMA((4,))]
    if pair:
        out_specs.append(ANY)
        out_shape.append(jax.ShapeDtypeStruct((Dh, nj * tn), F32))
        scratch += [pltpu.VMEM((DW_RING, Dh, tn), F32), pltpu.SemaphoreType.DMA((DW_RING,)), pltpu.SemaphoreType.DMA]
    return tuple(pl.pallas_call(
        body, name="dw_in", grid=(nj,),
        in_specs=[ANY] + [group_spec(g, L) for g in range(ng)] + [ANY, group_spec(1, Lc)],
        out_specs=out_specs, out_shape=out_shape, scratch_shapes=scratch,
        compiler_params=_cparams(("arbitrary",), VMEM_LIMIT),
    )(xmt, *groups, cmt, dr_c))


def _local_step(x, ctx, tgt, mod_x, mod_c, norm_w, conv_w8, conv_b, lg, gn_w, fw, project, csidx=None):
    L, D = x.shape
    sh_x, sc_x, g_x = mod_x[0:1], mod_x[1:2], mod_x[2:3]
    sh_c, sc_c = mod_c[0:1], mod_c[1:2]
    c2, s2 = _rope_tables(L)
    tab = _decay_tables(lg, D // DV)

    xm, xmt = _norm_mod(x, norm_w, sc_x, sh_x, "norm_mod_x")
    cm, cmt = _norm_mod(ctx, norm_w, sc_c, sh_c, "norm_mod_ctx")
    reduce = csidx is not None
    p, qr, kr, w_in, w3 = project(xm, c2, s2)
    pc, pqk_c = _in_proj(cm, w_in, "in_proj_ctx", QK_BLOCK, 2)
    s0 = _ctx_states(pc, pqk_c, lg, D)
    sf_prev, sb_prev = _ret_states(kr, p, s0, tab, D)
    o, yb = _ret_out(qr, kr, p, sf_prev, sb_prev, gn_w, tab, D)
    dx1, dya, do, dzb, dgab, dw3, st_mid = _mid(p, yb, o, x, tgt, w3, g_x, fw, conv_w8, conv_b, gn_w, D)
    dw3_5 = dw3.reshape(3, N_SHARD, 2, D // 8, D)
    dconv, st_conv, *ra_3 = _conv_bwd(dya, p, conv_w8, conv_b, D, _pair_exchange_w3(dw3_5) if reduce else None)
    dsf, dsb, ds0 = _ret_bwd_states(qr, do, tab, D)
    cs_3 = _sum_pair_w3(csidx[0:1], dw3_5, ra_3[0]) if reduce else None
    dret, st_lg, *rb_3 = _ret_bwd_main(qr, kr, p, do, sf_prev, sb_prev, dsf, dsb, c2, s2, tab, D,
                                       _chips_exchange_w3(cs_3) if reduce else None)
    g_3 = _sum_chips_w3(csidx, cs_3, rb_3[0]) if reduce else dw3
    dret_c, st_lgc = _ctx_bwd(pc, pqk_c, ds0, lg, D)
    groups = (dconv, dret, dzb, dgab)
    _, st_c = _dxm((dret_c,), 4, w_in, ctx, norm_w, sc_c, None, "dxm_ctx")
    return groups, dret_c, xmt, cmt, dx1, sc_x, w_in, g_3, (st_mid, st_conv, st_lg, st_lgc, st_c)


CHIP_FLIPS = (4, 2, 6)
ANY = pl.BlockSpec(memory_space=pl.ANY)
VMEM_FULL = pl.BlockSpec(memory_space=pltpu.VMEM)


def _position():
    return lax.axis_index("x"), lax.axis_index("y"), lax.axis_index("c")


def _peer(pos, k):
    x, y, c = pos
    return (1 - x if k & 4 else x, 1 - y if k & 2 else y, 1 - c if k & 1 else c)


def _dev_id(pos):
    return 4 * pos[0] + 2 * pos[1] + pos[2]


def _shard_of(pos):
    return 2 * pos[0] + pos[1]


def _remote(src, dst, send_sems, recv_sems, idx, to):
    return pltpu.make_async_remote_copy(src_ref=src, dst_ref=dst, send_sem=send_sems.at[idx],
                                        recv_sem=recv_sems.at[idx], device_id=to, device_id_type=MESH)


def _dot_f32(a, b):
    return jnp.dot(a, b, precision=lax.Precision.HIGHEST, preferred_element_type=F32)


def _silu(x):
    return x * _sigmoid(x)


def _fwd_small(c8, cctx8, ada_w, ada_b, conv_w8):
    D = c8.shape[1]
    Wm = ada_w.shape[1]
    Dq = conv_w8.shape[1]

    def body(c_ref, cc_ref, aw_ref, ab_ref, cw_ref, act_ref, mod_ref, cwf_ref,
             cbuf, pmine, pbuf, wbuf, s_c, r_c, s_p, r_p, s_w, r_w):
        pos = _position()
        me, s = _dev_id(pos), _shard_of(pos)
        cbuf[me] = c_ref[...]
        wbuf[s] = cw_ref[...]
        sends = [_remote(c_ref, cbuf.at[me], s_c, r_c, k - 1, _peer(pos, k)) for k in range(1, 8)]
        sends += [_remote(cw_ref, wbuf.at[s], s_w, r_w, j, _peer(pos, k)) for j, k in enumerate(CHIP_FLIPS)]
        for cp in sends:
            cp.start()
        for k in range(1, 8):
            _remote(c_ref, cbuf.at[_dev_id(_peer(pos, k))], s_c, r_c, k - 1, _peer(pos, k)).wait_recv()
        for d in range(N_DEV):
            act_ref[d:d + 1, :] = _silu(cbuf[d, 0:1, :])
        act_ref[8:9, :] = _silu(cc_ref[0:1, :])
        act_ref[9:16, :] = jnp.zeros((7, D), F32)
        part = _dot_f32(act_ref[...], aw_ref[...])
        pmine[...] = part
        pbuf[s] = part
        psend = [_remote(pmine, pbuf.at[s], s_p, r_p, j, _peer(pos, k)) for j, k in enumerate(CHIP_FLIPS)]
        for cp in psend:
            cp.start()
        for j, k in enumerate(CHIP_FLIPS):
            t = _shard_of(_peer(pos, k))
            _remote(pmine, pbuf.at[t], s_p, r_p, j, _peer(pos, k)).wait_recv()
            _remote(cw_ref, wbuf.at[t], s_w, r_w, j, _peer(pos, k)).wait_recv()
        for t in range(N_SHARD):
            mod_ref[:, t * Wm:(t + 1) * Wm] = pbuf[t] + ab_ref[:, t * Wm:(t + 1) * Wm]
            cwf_ref[:, t * Dq:(t + 1) * Dq] = wbuf[t]
        for cp in sends + psend:
            cp.wait_send()

    return pl.pallas_call(
        body, name="fwd_small",
        in_specs=[VMEM_FULL] * 5, out_specs=[VMEM_FULL] * 3,
        out_shape=[jax.ShapeDtypeStruct((16, D), F32), jax.ShapeDtypeStruct((16, 3 * D), F32),
                   jax.ShapeDtypeStruct((8, D), F32)],
        scratch_shapes=[pltpu.VMEM((N_DEV, 8, D), F32), pltpu.VMEM((16, Wm), F32),
                        pltpu.VMEM((N_SHARD, 16, Wm), F32), pltpu.VMEM((N_SHARD, 8, Dq), F32),
                        pltpu.SemaphoreType.DMA((7,)), pltpu.SemaphoreType.DMA((7,)),
                        pltpu.SemaphoreType.DMA((3,)), pltpu.SemaphoreType.DMA((3,)),
                        pltpu.SemaphoreType.DMA((3,)), pltpu.SemaphoreType.DMA((3,))],
        compiler_params=_cparams(None, VMEM_LIMIT),
    )(c8, cctx8, ada_w, ada_b, conv_w8)


AG_CHUNKS = 3


def _ag_in_proj(xm, w_in_s, w3_s, c2, s2):
    L, D = xm.shape
    Wc = w_in_s.shape[1]
    Wq = Wc // AG_CHUNKS
    Dh = D // 2
    Do = w3_s[0].shape[1]
    TM = min(1024, L // 4)
    NT = L // TM
    RC = min(128, Dh)
    NQ = AG_CHUNKS
    order = [(q, j) for q in range(NQ) for j in (0, 1)] + [(q, 2) for q in range(NQ)]

    def body(xm_ref, wi_hbm, wa_ref, wb_ref, wo_ref, c_ref, s_ref, p_hbm, qr_hbm, kr_hbm, fi_hbm, f3_hbm,
             w_vm, cast_buf, s3, stage, qk_stage, ici_s, ici_r, d2d_s, d2d_r, w3_s_, w3_r_, loc, out_sem, qk_sem):
        pos = _position()
        c = pos[2]
        s = _shard_of(pos)
        sib = _peer(pos, 1)
        mine = pl.ds(pl.multiple_of(c * Dh, Dh), Dh)
        other = pl.ds(pl.multiple_of((1 - c) * Dh, Dh), Dh)

        def cast_half(hf):
            def step(i, carry):
                rows = pl.ds(pl.multiple_of(hf * Dh + i * RC, RC), RC)
                cp = pltpu.make_async_copy(wi_hbm.at[rows, :], cast_buf, loc.at[0])
                cp.start()
                cp.wait()
                for q in range(NQ):
                    w_vm[0, q, rows, :] = cast_buf[:, q * Wq:(q + 1) * Wq].astype(BF16)
                return carry
            lax.fori_loop(0, Dh // RC, step, 0)

        def abs_col(t, q):
            return pl.ds(pl.multiple_of(t * Wc + q * Wq, 128), Wq)

        cast_half(c)
        for a, w_ref in enumerate((wa_ref, wb_ref, wo_ref)):
            s3[a] = w_ref[...].astype(BF16)
        sends = [_remote(w_vm.at[0, q, mine, :], w_vm.at[1 + j, q, mine, :], ici_s, ici_r, q * 3 + j,
                         _peer(pos, CHIP_FLIPS[j])) for q, j in order]
        for j, k in enumerate(CHIP_FLIPS):
            sends.append(_remote(s3.at[:, c], f3_hbm.at[:, s, c], w3_s_, w3_r_, j, _peer(pos, k)))
        for cp in sends:
            cp.start()
        cast_half(1 - c)
        local = [pltpu.make_async_copy(s3, f3_hbm.at[:, s], loc.at[1])]
        local += [pltpu.make_async_copy(w_vm.at[0, q], fi_hbm.at[:, abs_col(s, q)], loc.at[2 + q]) for q in range(NQ)]
        for cp in local:
            cp.start()

        def out_copy(slot, rows, cols):
            return pltpu.make_async_copy(stage.at[slot], p_hbm.at[rows, cols], out_sem.at[slot])

        def block(r, q, t, first):
            cols = abs_col(t, q)

            def row_tile(rt, carry):
                rows = pl.ds(pl.multiple_of(rt * TM, TM), TM)
                acc = _dot(xm_ref[rows, :], w_vm[r, q])
                slot = lax.rem(rt, 2)

                @pl.when(rt >= 2 if first else rt >= 0)
                def _():
                    out_copy(slot, rows, cols).wait()

                stage[slot] = acc.astype(BF16)
                out_copy(slot, rows, cols).start()

                def rotary(lo, scale, dst_hbm):
                    c, s = c_ref[rows, :], s_ref[rows, :]
                    for pr in range(Dh // 128):
                        tq = acc[:, lo + pr * 128:lo + (pr + 1) * 128] * scale
                        qk_stage[:, pr * 128:(pr + 1) * 128] = (tq * c + _swap_halves(tq) * s).astype(BF16)
                    cp = pltpu.make_async_copy(qk_stage, dst_hbm.at[rows, :], qk_sem)
                    cp.start()
                    cp.wait()

                if q == NQ - 1:
                    @pl.when(t == 1)
                    def _():
                        rotary(Wq - Dh, 1.0, qr_hbm)
                if q == 0:
                    @pl.when(t == 2)
                    def _():
                        rotary(0, K_SCALE, kr_hbm)
                return carry

            lax.fori_loop(0, NT, row_tile, 0)

        for q in range(NQ):
            block(0, q, s, q == 0)
        passed = []
        for q, j in order:
            r, idx = 1 + j, q * 3 + j
            t = _shard_of(_peer(pos, CHIP_FLIPS[j]))
            _remote(w_vm.at[r, q, mine, :], w_vm.at[r, q, mine, :], ici_s, ici_r, idx, sib).wait_recv()
            fwd = _remote(w_vm.at[r, q, mine, :], w_vm.at[r, q, mine, :], d2d_s, d2d_r, idx, sib)
            fwd.start()
            passed.append(fwd)
            _remote(w_vm.at[r, q, other, :], w_vm.at[r, q, other, :], d2d_s, d2d_r, idx, sib).wait_recv()
            block(r, q, t, False)
            cp = pltpu.make_async_copy(w_vm.at[r, q], fi_hbm.at[:, abs_col(t, q)], loc.at[2 + NQ + idx])
            cp.start()
            local.append(cp)
        for j, k in enumerate(CHIP_FLIPS):
            t = _shard_of(_peer(pos, k))
            _remote(s3.at[:, c], f3_hbm.at[:, t, c], w3_s_, w3_r_, j, sib).wait_recv()
            fwd = _remote(f3_hbm.at[:, t, c], f3_hbm.at[:, t, c], w3_s_, w3_r_, 3 + j, sib)
            fwd.start()
            passed.append(fwd)
        for j, k in enumerate(CHIP_FLIPS):
            t = _shard_of(_peer(pos, k))
            _remote(s3.at[:, c], f3_hbm.at[:, t, 1 - c], w3_s_, w3_r_, 3 + j, sib).wait_recv()
        for cp in sends + passed:
            cp.wait_send()
        for cp in local:
            cp.wait()
        for slot in range(2):
            out_copy(slot, pl.ds(0, TM), abs_col(s, 0)).wait()

    n_loc = 2 + NQ + 3 * NQ
    return pl.pallas_call(
        body, name="ag_in_proj",
        in_specs=[VMEM_FULL, ANY, VMEM_FULL, VMEM_FULL, VMEM_FULL, VMEM_FULL, VMEM_FULL], out_specs=[ANY] * 5,
        out_shape=[jax.ShapeDtypeStruct((L, N_SHARD * Wc), BF16),
                   jax.ShapeDtypeStruct((L, Dh), BF16), jax.ShapeDtypeStruct((L, Dh), BF16),
                   jax.ShapeDtypeStruct((D, N_SHARD * Wc), BF16), jax.ShapeDtypeStruct((3, N_SHARD, 2, Do, D), BF16)],
        scratch_shapes=[pltpu.VMEM((N_SHARD, NQ, D, Wq), BF16), pltpu.VMEM((RC, Wc), F32),
                        pltpu.VMEM((3, 2, Do, D), BF16), pltpu.VMEM((2, TM, Wq), BF16), pltpu.VMEM((TM, Dh), BF16),
                        pltpu.SemaphoreType.DMA((3 * NQ,)), pltpu.SemaphoreType.DMA((3 * NQ,)),
                        pltpu.SemaphoreType.DMA((3 * NQ,)), pltpu.SemaphoreType.DMA((3 * NQ,)),
                        pltpu.SemaphoreType.DMA((6,)), pltpu.SemaphoreType.DMA((6,)),
                        pltpu.SemaphoreType.DMA((n_loc,)), pltpu.SemaphoreType.DMA((2,)), pltpu.SemaphoreType.DMA],
        compiler_params=_cparams(None, VMEM_LIMIT),
    )(xm, w_in_s, *w3_s, c2, s2)


def _pair_exchange_w3(dw3):
    _, _, _, Do, D = dw3.shape

    def build(ins, outs, send, recv):
        pos = _position()
        return [_remote(ins[0].at[:, :, 1 - pos[2]], outs[0], send, recv, 0, _peer(pos, 1))]

    return _Exchange((dw3,), (jax.ShapeDtypeStruct((3, N_SHARD, Do, D), F32),), 1, build)


def _sum_pair_in(dw_mine, ri):
    Dh, Wf = dw_mine.shape
    Wc = Wf // N_SHARD
    tr = min(256, Dh)

    def body(a_ref, b_ref, o_ref):
        o_ref[...] = (a_ref[...] + b_ref[...]).astype(BF16)

    return pl.pallas_call(
        body, name="sum_pair_in", grid=(Dh // tr, N_SHARD),
        in_specs=[pl.BlockSpec((tr, Wc), lambda i, t: (i, t)), pl.BlockSpec((tr, Wc), lambda i, t: (i, t))],
        out_specs=pl.BlockSpec((None, tr, Wc), lambda i, t: (t, i, 0)),
        out_shape=jax.ShapeDtypeStruct((N_SHARD, Dh, Wc), BF16),
        compiler_params=_cparams(("parallel", "parallel")),
    )(dw_mine, ri)


def _sum_pair_w3(cidx, dw3, r3):
    _, _, _, Do, D = dw3.shape

    def body(c_ref, a_ref, b_ref, o_ref):
        o_ref[...] = (a_ref[...] + b_ref[...]).astype(BF16)

    return pl.pallas_call(
        body, name="sum_pair_w3",
        grid_spec=pltpu.PrefetchScalarGridSpec(
            num_scalar_prefetch=1, grid=(3,),
            in_specs=[pl.BlockSpec((None, N_SHARD, None, Do, D), lambda a, c: (a, 0, c[0], 0, 0)),
                      pl.BlockSpec((None, N_SHARD, Do, D), lambda a, c: (a, 0, 0, 0))],
            out_specs=pl.BlockSpec((None, N_SHARD, Do, D), lambda a, c: (a, 0, 0, 0))),
        out_shape=jax.ShapeDtypeStruct((3, N_SHARD, Do, D), BF16),
        compiler_params=_cparams(("parallel",)),
    )(cidx, dw3, r3)


def _chips_exchange_in(cs_in):
    _, Dh, Wc = cs_in.shape

    def build(ins, outs, send, recv):
        pos = _position()
        return [_remote(ins[0].at[_shard_of(_peer(pos, k))], outs[0].at[j], send, recv, j, _peer(pos, k))
                for j, k in enumerate(CHIP_FLIPS)]

    return _Exchange((cs_in,), (jax.ShapeDtypeStruct((3, Dh, Wc), BF16),), 3, build)


def _chips_exchange_w3(cs_3):
    _, _, Do, D = cs_3.shape

    def build(ins, outs, send, recv):
        pos = _position()
        return [_remote(ins[0].at[:, _shard_of(_peer(pos, k))], outs[0].at[j], send, recv, j, _peer(pos, k))
                for j, k in enumerate(CHIP_FLIPS)]

    return _Exchange((cs_3,), (jax.ShapeDtypeStruct((3, 3, Do, D), BF16),), 3, build)


def _sum_chips_in(csidx, cs_in, rb_in):
    _, Dh, Wc = cs_in.shape
    tr = min(256, Dh)

    def body(s_ref, a_ref, b_ref, o_ref):
        acc = a_ref[...].astype(F32)
        for j in range(3):
            acc = acc + b_ref[j].astype(F32)
        o_ref[...] = acc

    return pl.pallas_call(
        body, name="sum_chips_in",
        grid_spec=pltpu.PrefetchScalarGridSpec(
            num_scalar_prefetch=1, grid=(Dh // tr,),
            in_specs=[pl.BlockSpec((None, tr, Wc), lambda i, s: (s[1], i, 0)),
                      pl.BlockSpec((3, tr, Wc), lambda i, s: (0, i, 0))],
            out_specs=pl.BlockSpec((None, tr, Wc), lambda i, s: (s[0], i, 0))),
        out_shape=jax.ShapeDtypeStruct((2, Dh, Wc), F32),
        compiler_params=_cparams(("parallel",)),
    )(csidx, cs_in, rb_in)


def _sum_chips_w3(csidx, cs_3, rb_3):
    _, _, Do, D = cs_3.shape

    def body(s_ref, a_ref, b_ref, o_ref):
        acc = a_ref[...].astype(F32)
        for j in range(3):
            acc = acc + b_ref[j].astype(F32)
        o_ref[...] = acc

    return pl.pallas_call(
        body, name="sum_chips_w3",
        grid_spec=pltpu.PrefetchScalarGridSpec(
            num_scalar_prefetch=1, grid=(3,),
            in_specs=[pl.BlockSpec((None, None, Do, D), lambda a, s: (a, s[1], 0, 0)),
                      pl.BlockSpec((3, None, Do, D), lambda a, s: (0, a, 0, 0))],
            out_specs=pl.BlockSpec((None, None, Do, D), lambda a, s: (a, s[0], 0, 0))),
        out_shape=jax.ShapeDtypeStruct((3, 2, Do, D), F32),
        compiler_params=_cparams(("parallel",)),
    )(csidx, cs_3, rb_3)


def _adam_math(w, g, m, v):
    m = ADAM_B1 * m + (1.0 - ADAM_B1) * g
    v = ADAM_B2 * v + (1.0 - ADAM_B2) * (g * g)
    m_hat = m / (1.0 - ADAM_B1 ** ADAM_STEP)
    v_hat = v / (1.0 - ADAM_B2 ** ADAM_STEP)
    delta = -ADAM_LR * (m_hat / (jnp.sqrt(v_hat) + ADAM_EPS) + ADAM_WD * w)
    return delta, m, v


def _adamw(w, g, m, v, name):
    R, C = w.shape
    tr = min(128, R)

    def body(w_ref, g_ref, m_ref, v_ref, d_ref, nm_ref, nv_ref):
        d_ref[...], nm_ref[...], nv_ref[...] = _adam_math(w_ref[...], g_ref[...], m_ref[...], v_ref[...])

    blk = pl.BlockSpec((tr, C), lambda i: (i, 0))
    return pl.pallas_call(
        body, name=name, grid=(R // tr,), in_specs=[blk] * 4, out_specs=[blk] * 3,
        out_shape=[jax.ShapeDtypeStruct((R, C), F32)] * 3,
        compiler_params=_cparams(("parallel",), VMEM_LIMIT),
    )(w, g, m, v)


SMALL_ROWS = ("c_ctx", "norm_w", "conv_b", "gn_w", "final_norm_w")


def _bwd_small(stats, ada_w, Dq, gh_in, gh_3):
    D = stats[0].shape[1]
    Wm = ada_w.shape[1]

    def body(stx, stm, stc, stv, stl, stlc, aw_ref, gi_in, g3_in, tot_ref, dm_sh, gcw, da_ref, loss_ref, gi_ref, g3_ref,
             vec_ref, vbuf, dm, amine, abuf, s_v, r_v, s_a, r_a, s_g, r_g):
        pos = _position()
        me, s = _dev_id(pos), _shard_of(pos)
        c, sib = pos[2], _peer(pos, 1)
        halves = [_remote(gi_in.at[c], gi_ref.at[c], s_g, r_g, 0, sib),
                  _remote(g3_in.at[:, c], g3_ref.at[:, c], s_g, r_g, 1, sib)]
        for cp in halves:
            cp.start()
        vec_ref[...] = jnp.zeros_like(vec_ref)
        vec_ref[0:2, :] = stx[0:2, :]
        vec_ref[2:3, :] = stm[1:2, :]
        vec_ref[3:5, :] = stc[0:2, :]
        vec_ref[5:6, :] = stx[2:3, :] + stc[2:3, :]
        vec_ref[6:7, :] = stv[3:4, :]
        vec_ref[7:8, :] = stm[3:4, :]
        vec_ref[8:9, :] = stm[0:1, :]
        vec_ref[9:12, :] = stv[0:3, :]
        vec_ref[12:14, 0:128] = stl[0:2, :] + stlc[0:2, :]
        vec_ref[14:15, :] = stm[2:3, :]
        vbuf[me] = vec_ref[...]
        sends = [_remote(vec_ref, vbuf.at[me], s_v, r_v, k - 1, _peer(pos, k)) for k in range(1, 8)]
        for cp in sends:
            cp.start()
        for k in range(1, 8):
            _remote(vec_ref, vbuf.at[_dev_id(_peer(pos, k))], s_v, r_v, k - 1, _peer(pos, k)).wait_recv()
        tot = vbuf[0]
        for d in range(1, N_DEV):
            tot = tot + vbuf[d]
        loss_ref[...] = jnp.zeros((8, 128), F32) + (0.5 / D) * _sum_all(tot[14:15, :])
        dm[...] = jnp.zeros_like(dm)
        for d in range(N_DEV):
            for r in range(3):
                dm[d:d + 1, r * D:(r + 1) * D] = vbuf[d, r:r + 1, :]
        dm[8:9, 0:D] = tot[3:4, :]
        dm[8:9, D:2 * D] = tot[4:5, :]
        for t in range(N_SHARD):
            @pl.when(s == t)
            def _(t=t):
                dm_sh[...] = dm[:, t * Wm:(t + 1) * Wm]
                gcw[...] = tot[9:12, t * Dq:(t + 1) * Dq]
        tot_ref[...] = tot
        part = lax.dot_general(dm_sh[8:16, :], aw_ref[...], (((1,), (1,)), ((), ())),
                               precision=lax.Precision.HIGHEST, preferred_element_type=F32)
        amine[...] = part
        abuf[s] = part
        asend = [_remote(amine, abuf.at[s], s_a, r_a, j, _peer(pos, k)) for j, k in enumerate(CHIP_FLIPS)]
        for cp in asend:
            cp.start()
        for j, k in enumerate(CHIP_FLIPS):
            _remote(amine, abuf.at[_shard_of(_peer(pos, k))], s_a, r_a, j, _peer(pos, k)).wait_recv()
        da = abuf[0]
        for t in range(1, N_SHARD):
            da = da + abuf[t]
        da_ref[...] = da
        _remote(gi_in.at[1 - c], gi_ref.at[1 - c], s_g, r_g, 0, sib).wait_recv()
        _remote(g3_in.at[:, 1 - c], g3_ref.at[:, 1 - c], s_g, r_g, 1, sib).wait_recv()
        for cp in sends + asend + halves:
            cp.wait_send()

    row = lambda *shape: jax.ShapeDtypeStruct(shape, F32)
    return pl.pallas_call(
        body, name="bwd_small",
        in_specs=[VMEM_FULL] * 7 + [ANY, ANY], out_specs=[VMEM_FULL] * 5 + [ANY, ANY],
        input_output_aliases={7: 5, 8: 6},
        out_shape=[row(16, D), row(16, Wm), row(3, Dq), row(8, D), row(8, 128), row(*gh_in.shape), row(*gh_3.shape)],
        scratch_shapes=[pltpu.VMEM((16, D), F32), pltpu.VMEM((N_DEV, 16, D), F32), pltpu.VMEM((16, 3 * D), F32),
                        pltpu.VMEM((8, D), F32), pltpu.VMEM((N_SHARD, 8, D), F32),
                        pltpu.SemaphoreType.DMA((7,)), pltpu.SemaphoreType.DMA((7,)),
                        pltpu.SemaphoreType.DMA((3,)), pltpu.SemaphoreType.DMA((3,)),
                        pltpu.SemaphoreType.DMA((2,)), pltpu.SemaphoreType.DMA((2,))],
        compiler_params=_cparams(None, VMEM_LIMIT),
    )(*stats, ada_w, gh_in, gh_3)


def _small_update(tot, dm_sh, gcw, da, act, p_row, p_ab, p_cw, p_dl):
    D = act.shape[1]
    Wm = dm_sh.shape[1]
    Dq = gcw.shape[1]

    def body(tot_ref, dm_ref, gcw_ref, da_ref, act_ref, prow, pab, pcw, pdl, gaw_ref, *outs):
        o_q = [outs[8 * q:8 * (q + 1)] for q in range(4)]
        tot = tot_ref[...]
        gaw_ref[...] = lax.dot_general(act_ref[...], dm_ref[...], (((0,), (0,)), ((), ())),
                                       precision=lax.Precision.HIGHEST, preferred_element_type=F32)
        cc = prow[0, 0:1, :]
        sg = _sigmoid(cc)
        g_cctx = da_ref[0:1, :] * (sg * (1.0 + cc * (1.0 - sg)))

        def place_all(o, val):
            o[...] = val

        def emit(k, w, g, m, v, place=place_all):
            for q, val in enumerate((g,) + _adam_math(w, g, m, v)):
                place(o_q[q][k], val)

        g_rows = [g_cctx, tot[5:6, :], tot[6:7, :], tot[7:8, :], tot[8:9, :]]
        for k, g in enumerate(g_rows):
            emit(k, prow[0, k:k + 1, :], g, prow[1, k:k + 1, :], prow[2, k:k + 1, :])

        def place_ab(o, val):
            for r in range(3):
                o[0:1, r * D:(r + 1) * D] = val[r:r + 1, :]

        g_ab = jnp.concatenate([tot[0:1, :] + tot[3:4, :], tot[1:2, :] + tot[4:5, :], tot[2:3, :]], axis=0)
        emit(5, pab[0], g_ab, pab[1], pab[2], place_ab)
        emit(6, pcw[0], gcw_ref[...], pcw[1], pcw[2])
        g_dl = jnp.concatenate([tot[12:14, 0:128] * _sigmoid(-pdl[0, 0:2, :]), jnp.zeros((6, 128), F32)], axis=0)
        emit(7, pdl[0], g_dl, pdl[1], pdl[2])

    row = lambda *shape: jax.ShapeDtypeStruct(shape, F32)
    per_q = [row(1, D)] * 5 + [row(1, 3 * D), row(3, Dq), row(8, 128)]
    res = pl.pallas_call(
        body, name="small_update",
        in_specs=[VMEM_FULL] * 9, out_specs=[VMEM_FULL] * 33,
        out_shape=[row(D, Wm)] + per_q * 4,
        compiler_params=_cparams(None, VMEM_LIMIT),
    )(tot, dm_sh, gcw, da, act, p_row, p_ab, p_cw, p_dl)
    return res[0], [res[1 + 8 * q:1 + 8 * (q + 1)] for q in range(4)]


def _pad_rows(a, rows=8):
    return jnp.pad(a, ((0, rows - a.shape[0]), (0, 0)))


def kernel(x, c, ctx, c_ctx, norm_w, ada_w, ada_b, w_in, conv_w, conv_b, decay_logit, gn_w, w_a, w_b, w_out, final_norm_w, loss_target, m_c_ctx, m_norm_w, m_ada_w, m_ada_b, m_w_in, m_conv_w, m_conv_b, m_decay_logit, m_gn_w, m_w_a, m_w_b, m_w_out, m_final_norm_w, v_c_ctx, v_norm_w, v_ada_w, v_ada_b, v_w_in, v_conv_w, v_conv_b, v_decay_logit, v_gn_w, v_w_a, v_w_b, v_w_out, v_final_norm_w):
    L, D = x.shape[1], x.shape[2]
    H = D // DV
    Wc = w_in.shape[2]
    Do = D // 8
    pos = _position()
    me = _dev_id(pos)
    cidx = jnp.reshape(pos[2], (1,)).astype(jnp.int32)
    sidx = jnp.reshape(_shard_of(pos), (1,)).astype(jnp.int32)

    act, mod, conv_w8 = _fwd_small(_pad_rows(c), _pad_rows(c_ctx[None]), ada_w[0], ada_b, _pad_rows(conv_w[0]))
    mod_x = lax.dynamic_slice_in_dim(mod, me, 1, axis=0).reshape(3, D)
    mod_c = mod[8].reshape(3, D)
    lg = jax.nn.log_sigmoid(decay_logit[0])

    w3_s = tuple(w[0].reshape(2, Do, D) for w in (w_a, w_b, w_out))

    def project(xm, c2, s2):
        p, qr, kr, w_in_full, w3_full = _ag_in_proj(xm, w_in[0], w3_s, c2, s2)
        return p, qr, kr, w_in_full, w3_full.reshape(3, D, D)

    csidx = jnp.concatenate([cidx, sidx])
    groups, dret_c, xmt, cmt, dx1, sc_x, w_in_full, gh_3, sts = _local_step(
        x[0], ctx[0], loss_target[0], mod_x, mod_c, norm_w, conv_w8, conv_b, lg, gn_w, final_norm_w[None],
        project, csidx)
    st_mid, st_conv, st_lg, st_lgc, st_c = sts

    dw_mine, ra_in = _dw_in(xmt, groups, cmt, dret_c, D, True)
    cs_in = _sum_pair_in(dw_mine, ra_in)
    grad_x, st_x, rb_in = _dxm(groups, 0, w_in_full, x[0], norm_w, sc_x, dx1, "dxm_x", _chips_exchange_in(cs_in))
    gh_in = _sum_chips_in(csidx, cs_in, rb_in)

    zeros3 = jnp.zeros((3, D), F32)
    p_row = jnp.concatenate(
        [r for t in ((c_ctx[None], norm_w, conv_b, gn_w, final_norm_w[None], zeros3),
                     (m_c_ctx[None], m_norm_w, m_conv_b, m_gn_w, m_final_norm_w[None], zeros3),
                     (v_c_ctx[None], v_norm_w, v_conv_b, v_gn_w, v_final_norm_w[None], zeros3)) for r in t],
        axis=0).reshape(3, 8, D)
    p_ab = jnp.concatenate([ada_b, m_ada_b, v_ada_b], axis=0).reshape(3, 3, D)
    p_cw = jnp.concatenate([conv_w, m_conv_w, v_conv_w], axis=0)
    p_dl = jnp.pad(jnp.concatenate([decay_logit, m_decay_logit, v_decay_logit], axis=0), ((0, 0), (0, 6), (0, 128 - H)))
    tot, dm_sh, gcw, da, loss_t, g_in, g_3 = _bwd_small((st_x, st_mid, st_c, st_conv, st_lg, st_lgc), ada_w[0],
                                                        conv_w.shape[2], gh_in, gh_3)
    g_w_in = g_in.reshape(D, Wc)
    g_3 = g_3.reshape(3, D // 4, D)
    g_ada_w, small = _small_update(tot, dm_sh, gcw, da, act, p_row, p_ab, p_cw, p_dl)

    upd_in = _adamw(w_in[0], g_w_in, m_w_in[0], v_w_in[0], "adamw_w_in")
    upd_ada = _adamw(ada_w[0], g_ada_w, m_ada_w[0], v_ada_w[0], "adamw_ada_w")
    upd_a = _adamw(w_a[0], g_3[0], m_w_a[0], v_w_a[0], "adamw_w_a")
    upd_b = _adamw(w_b[0], g_3[1], m_w_b[0], v_w_b[0], "adamw_w_b")
    upd_o = _adamw(w_out[0], g_3[2], m_w_out[0], v_w_out[0], "adamw_w_out")

    def leaves(q):
        big = lambda g, upd: (g if q == 0 else upd[q - 1])[None]
        r_cctx, r_norm, r_convb, r_gn, r_fnorm, r_ab, r_cw, r_dl = small[q]
        return [r_cctx.reshape(D), r_norm, big(g_ada_w, upd_ada), r_ab, big(g_w_in, upd_in),
                r_cw[None], r_convb, r_dl[0:2, 0:H][None], r_gn,
                big(g_3[0], upd_a), big(g_3[1], upd_b), big(g_3[2], upd_o), r_fnorm.reshape(D)]

    loss = loss_t[0, 0]
    return (loss, grad_x[None], *leaves(0), *leaves(1), *leaves(2), *leaves(3))
```

```python
from typing import Callable, NamedTuple

import jax
import jax.numpy as jnp
from jax import lax
from jax.experimental import pallas as pl
from jax.experimental.pallas import tpu as pltpu

F32 = jnp.float32
BF16 = jnp.bfloat16
MESH = pl.DeviceIdType.MESH

CHUNK = 128
RET_CPB = 4
DV = 128
DK = 64
GRID_W = 64
ROPE_BASE = 10000.0
EPS = 1e-6
K_SCALE = DK ** -0.5
N_SHARD = 4
N_DEV = 8

ADAM_LR = 0.001
ADAM_B1 = 0.9
ADAM_B2 = 0.999
ADAM_EPS = 1e-08
ADAM_WD = 0.01
ADAM_STEP = 10

VMEM_LIMIT = 56 * 1024 * 1024


def _cparams(sem=None, vmem=None):
    kw = {}
    if sem is not None:
        kw["dimension_semantics"] = sem
    if vmem is not None:
        kw["vmem_limit_bytes"] = vmem
    return pltpu.CompilerParams(**kw)


def _dot(a, b):
    return jnp.dot(a, b, preferred_element_type=F32)


def _dot_nt(a, b):
    return lax.dot_general(a, b, (((1,), (1,)), ((), ())), preferred_element_type=F32)


def _dot_tn(a, b):
    return lax.dot_general(a, b, (((0,), (0,)), ((), ())), preferred_element_type=F32)


def _sigmoid(x):
    return 1.0 / (1.0 + jnp.exp(-x))


def _sum_all(x):
    return jnp.sum(jnp.sum(x, axis=1, keepdims=True), axis=0, keepdims=True)


def _swap_halves(t):
    n = t.shape[1]
    lane = lax.broadcasted_iota(jnp.int32, t.shape, 1)
    low = (lane & 32) == 0
    return jnp.where(low, pltpu.roll(t, n - 32, 1), pltpu.roll(t, 32, 1))


def _vec_spec(d):
    return pl.BlockSpec((1, d), lambda *a: (0, 0))


def _norm_mod(x, nw, sc, sh, name):
    L, D = x.shape
    tl = min(256, L)

    def body(x_ref, nw_ref, sc_ref, sh_ref, xm_ref, xmt_ref):
        xv = x_ref[...]
        r = lax.rsqrt(jnp.mean(xv * xv, axis=-1, keepdims=True) + EPS)
        xm = (xv * r * nw_ref[...]) * (1.0 + sc_ref[...]) + sh_ref[...]
        xm_b = xm.astype(BF16)
        xm_ref[...] = xm_b
        xmt_ref[...] = xm_b.T

    return pl.pallas_call(
        body, name=name, grid=(L // tl,),
        in_specs=[pl.BlockSpec((tl, D), lambda i: (i, 0)), _vec_spec(D), _vec_spec(D), _vec_spec(D)],
        out_specs=[pl.BlockSpec((tl, D), lambda i: (i, 0)), pl.BlockSpec((D, tl), lambda i: (0, i))],
        out_shape=[jax.ShapeDtypeStruct((L, D), BF16), jax.ShapeDtypeStruct((D, L), BF16)],
        compiler_params=_cparams(("parallel",)),
    )(x, nw, sc, sh)


QK_BLOCK, V_BLOCK = 4, 5


def _in_proj(xm, w, name, first=0, count=None):
    M, D = xm.shape
    count = w.shape[1] // D if count is None else count
    tm = min(1024, M)

    def body(a_ref, b_ref, o_ref, qk_ref):
        acc = _dot(a_ref[...], b_ref[...])
        o_ref[...] = acc.astype(o_ref.dtype)

        @pl.when(pl.program_id(1) == QK_BLOCK - first)
        def _():
            qk_ref[...] = acc

    return pl.pallas_call(
        body, name=name, grid=(M // tm, count),
        in_specs=[pl.BlockSpec((tm, D), lambda i, j: (i, 0)), pl.BlockSpec((D, D), lambda i, j: (0, first + j))],
        out_specs=[pl.BlockSpec((tm, D), lambda i, j: (i, j)), pl.BlockSpec((tm, D), lambda i, j: (i, 0))],
        out_shape=[jax.ShapeDtypeStruct((M, count * D), BF16), jax.ShapeDtypeStruct((M, D), F32)],
        compiler_params=_cparams(("parallel", "arbitrary")),
    )(xm, w)


def _halo_specs(tl, L, D, col):
    hb = tl // 16
    last = L // 16 - 1
    prev = pl.BlockSpec((16, D), lambda i: (jnp.maximum(i * hb - 1, 0), col))
    nxt = pl.BlockSpec((16, D), lambda i: (jnp.minimum((i + 1) * hb, last), col))
    return prev, nxt


def _shift_rows(u, above, below):
    tl = u.shape[0]
    row = lax.broadcasted_iota(jnp.int32, u.shape, 0)
    dn = jnp.where(row == 0, above, pltpu.roll(u, 1, 0))
    up = jnp.where(row == tl - 1, below, pltpu.roll(u, tl - 1, 0))
    return dn, up


def _rope_tables(L):
    pos = jnp.arange(L)
    row = (pos // GRID_W).astype(F32)
    col = (pos % GRID_W).astype(F32)
    nf = DK // 4
    inv = ROPE_BASE ** (-jnp.arange(nf, dtype=F32) / nf)
    ang = jnp.concatenate([row[:, None] * inv, col[:, None] * inv], axis=-1)
    cos, sin = jnp.cos(ang), jnp.sin(ang)
    return jnp.concatenate([cos, cos, cos, cos], axis=-1), jnp.concatenate([-sin, sin, -sin, sin], axis=-1)


def _smem_spec():
    return pl.BlockSpec(memory_space=pltpu.SMEM)


def _pair_select(e0, e1):
    row = lax.broadcasted_iota(jnp.int32, e0.shape, 0)
    return jnp.where(row < DK, e0, e1)


def _head_lane_mask(shape, e):
    lane = lax.broadcasted_iota(jnp.int32, shape, 1)
    return (lane < DK) if e == 0 else (lane >= DK)


def _ctx_states(pc, pqk_c, lg, D):
    Lc = pc.shape[0]
    H = D // DV

    def body(lg_ref, k_ref, v_ref, s_ref):
        m = lax.broadcasted_iota(jnp.int32, (Lc, DV), 0).astype(F32)
        for pr in range(H // 2):
            k2 = k_ref[:, pr * 128:(pr + 1) * 128].astype(F32) * K_SCALE
            res = [[None, None], [None, None]]
            for e in range(2):
                h = 2 * pr + e
                v = v_ref[:, h * DV:(h + 1) * DV]
                dec_f = jnp.exp(lg_ref[0, h] * (Lc - 1.0 - m))
                dec_b = jnp.exp(lg_ref[1, h] * m)
                res[0][e] = _dot_tn((k2 * dec_f).astype(BF16), v)
                res[1][e] = _dot_tn((k2 * dec_b).astype(BF16), v)
            s_ref[0, pr] = _pair_select(res[0][0], res[0][1])
            s_ref[1, pr] = _pair_select(res[1][0], res[1][1])

    return pl.pallas_call(
        body, name="ctx_states", grid=(1,),
        in_specs=[_smem_spec(), pl.BlockSpec((Lc, D // 2), lambda i: (0, 1)), pl.BlockSpec((Lc, D), lambda i: (0, 1))],
        out_specs=pl.BlockSpec((2, H // 2, 128, 128), lambda i: (0, 0, 0, 0)),
        out_shape=jax.ShapeDtypeStruct((2, H // 2, 128, 128), F32),
    )(lg, pqk_c, pc)


T_M, T_MT = 0, 1
T_MF1, T_MB1 = 2, 3
T_QF, T_QB = 4, 5
T_KF, T_KB = 6, 7


def _decay_tables(lg, H):
    def body(lg_ref, t_ref):
        h = pl.program_id(0)
        lgf, lgb = lg_ref[0, h], lg_ref[1, h]
        i = lax.broadcasted_iota(jnp.int32, (CHUNK, CHUNK), 0).astype(F32)
        j = lax.broadcasted_iota(jnp.int32, (CHUNK, CHUNK), 1).astype(F32)
        d = i - j
        mf = jnp.where(d > 0, jnp.exp(lgf * jnp.maximum(d, 0.0)), 0.0)
        mb = jnp.where(d < 0, jnp.exp(lgb * jnp.maximum(-d, 0.0)), 0.0)
        mf_t = jnp.where(d < 0, jnp.exp(lgf * jnp.maximum(-d, 0.0)), 0.0)
        mb_t = jnp.where(d > 0, jnp.exp(lgb * jnp.maximum(d, 0.0)), 0.0)
        diag = jnp.where(d == 0, 2.0, 0.0)
        t_ref[0, T_M] = mf + mb + diag
        t_ref[0, T_MT] = mf_t + mb_t + diag
        t_ref[0, T_MF1] = mf * d
        t_ref[0, T_MB1] = mb * (-d)
        t_ref[0, T_QF] = jnp.exp(lgf * (i + 1.0))
        t_ref[0, T_QB] = jnp.exp(lgb * (CHUNK - i))
        t_ref[0, T_KF] = jnp.exp(lgf * (CHUNK - 1.0 - i))
        t_ref[0, T_KB] = jnp.exp(lgb * i)

    return pl.pallas_call(
        body, name="decay_tables", grid=(H,), in_specs=[_smem_spec()],
        out_specs=pl.BlockSpec((1, 8, CHUNK, CHUNK), lambda h: (h, 0, 0, 0)),
        out_shape=jax.ShapeDtypeStruct((H, 8, CHUNK, CHUNK), F32),
    )(lg)


def _tab_spec(H):
    return pl.BlockSpec((H, 8, CHUNK, CHUNK), lambda n: (0, 0, 0, 0))


def _chunk_decay(tab_ref, h):
    return tab_ref[h, T_QF, CHUNK - 1:CHUNK, :], tab_ref[h, T_QB, 0:1, :]


def _ret_states(kr, p, s0, tab, D, exchange=None):
    L = kr.shape[0]
    H = D // DV
    N = L // CHUNK
    HP = H // 2

    def body(tab_ref, kf_ref, kb_ref, vf_ref, vb_ref, s0_ref, sf_out, sb_out, sf, sb):
        n = pl.program_id(0)

        @pl.when(n == 0)
        def _():
            sf[...] = s0_ref[0]
            sb[...] = s0_ref[1]

        for cc in range(RET_CPB):
            cf_, cb_ = cc, RET_CPB - 1 - cc
            rf, rb = slice(cf_ * CHUNK, (cf_ + 1) * CHUNK), slice(cb_ * CHUNK, (cb_ + 1) * CHUNK)
            sf_out[cf_] = sf[...]
            sb_out[cb_] = sb[...]
            for pr in range(HP):
                kf2 = kf_ref[rf, pr * 128:(pr + 1) * 128].astype(F32)
                kb2 = kb_ref[rb, pr * 128:(pr + 1) * 128].astype(F32)
                inc_f, inc_b, gf, gb = [], [], [], []
                for e in range(2):
                    h = 2 * pr + e
                    inc_f.append(_dot_tn((kf2 * tab_ref[h, T_KF]).astype(BF16), vf_ref[rf, h * DV:(h + 1) * DV]))
                    inc_b.append(_dot_tn((kb2 * tab_ref[h, T_KB]).astype(BF16), vb_ref[rb, h * DV:(h + 1) * DV]))
                    cf, cb = _chunk_decay(tab_ref, h)
                    gf.append(jnp.broadcast_to(cf, (128, 128)))
                    gb.append(jnp.broadcast_to(cb, (128, 128)))
                sf[pr] = _pair_select(gf[0], gf[1]) * sf[pr] + _pair_select(inc_f[0], inc_f[1])
                sb[pr] = _pair_select(gb[0], gb[1]) * sb[pr] + _pair_select(inc_b[0], inc_b[1])

    st = jax.ShapeDtypeStruct((N, HP, 128, 128), F32)
    R = RET_CPB * CHUNK
    NB = N // RET_CPB
    return _riding_call(
        body, exchange, NB, name="ret_states", args=(tab, kr, kr, p, p, s0),
        in_specs=[_tab_spec(H),
                  pl.BlockSpec((R, D // 2), lambda n: (n, 0)),
                  pl.BlockSpec((R, D // 2), lambda n: (NB - 1 - n, 0)),
                  pl.BlockSpec((R, D), lambda n: (n, 5)),
                  pl.BlockSpec((R, D), lambda n: (NB - 1 - n, 5)),
                  pl.BlockSpec((2, HP, 128, 128), lambda n: (0, 0, 0, 0))],
        out_specs=[pl.BlockSpec((RET_CPB, HP, 128, 128), lambda n: (n, 0, 0, 0)),
                   pl.BlockSpec((RET_CPB, HP, 128, 128), lambda n: (NB - 1 - n, 0, 0, 0))],
        out_shape=[st, st],
        scratch=[pltpu.VMEM((HP, 128, 128), F32), pltpu.VMEM((HP, 128, 128), F32)],
        cparams=_cparams(("arbitrary",)))


def _ret_out(qr, kr, p, sf_prev, sb_prev, gn_w, tab, D, exchange=None):
    L = qr.shape[0]
    H = D // DV
    N = L // CHUNK
    HP = H // 2

    def body(tab_ref, q_ref, k_ref, v_ref, zb_ref, sf_ref, sb_ref, gn_ref, o_ref, yb_ref):
        def chunk(cc, carry):
            rows = pl.ds(pl.multiple_of(cc * CHUNK, CHUNK), CHUNK)
            for pr in range(HP):
                q2 = q_ref[rows, pr * 128:(pr + 1) * 128]
                k2 = k_ref[rows, pr * 128:(pr + 1) * 128]
                sfp = sf_ref[cc, pr].astype(BF16)
                sbp = sb_ref[cc, pr].astype(BF16)
                for e in range(2):
                    h = 2 * pr + e
                    sl = slice(h * DV, (h + 1) * DV)
                    qm = jnp.where(_head_lane_mask(q2.shape, e), q2, jnp.zeros_like(q2))
                    a = (_dot_nt(qm, k2) * tab_ref[h, T_M]).astype(BF16)
                    qf = qm.astype(F32)
                    o = _dot(a, v_ref[rows, sl])
                    o += _dot((qf * tab_ref[h, T_QF]).astype(BF16), sfp)
                    o += _dot((qf * tab_ref[h, T_QB]).astype(BF16), sbp)
                    o_ref[rows, sl] = o
                    mu = jnp.mean(o, axis=-1, keepdims=True)
                    oc = o - mu
                    rstd = lax.rsqrt(jnp.mean(oc * oc, axis=-1, keepdims=True) + EPS)
                    zb = zb_ref[rows, sl].astype(F32)
                    yb_ref[rows, sl] = (zb * _sigmoid(zb) * (oc * rstd * gn_ref[:, sl])).astype(BF16)
            return carry

        lax.fori_loop(0, RET_CPB, chunk, 0)

    R = RET_CPB * CHUNK
    return _riding_call(
        body, exchange, N // RET_CPB, name="ret_out", args=(tab, qr, kr, p, p, sf_prev, sb_prev, gn_w),
        in_specs=[_tab_spec(H),
                  pl.BlockSpec((R, D // 2), lambda n: (n, 0)),
                  pl.BlockSpec((R, D // 2), lambda n: (n, 0)),
                  pl.BlockSpec((R, D), lambda n: (n, 5)),
                  pl.BlockSpec((R, D), lambda n: (n, 6)),
                  pl.BlockSpec((RET_CPB, HP, 128, 128), lambda n: (n, 0, 0, 0)),
                  pl.BlockSpec((RET_CPB, HP, 128, 128), lambda n: (n, 0, 0, 0)),
                  _vec_spec(D)],
        out_specs=[pl.BlockSpec((R, D), lambda n: (n, 0)), pl.BlockSpec((R, D), lambda n: (n, 0))],
        out_shape=[jax.ShapeDtypeStruct((L, D), F32), jax.ShapeDtypeStruct((L, D), BF16)],
        cparams=_cparams(("arbitrary",)))


def _mid(p, yb, o, x, tgt, w3, g, fw, conv_w, conv_b, gn_w, D):
    L = x.shape[0]
    H = D // DV
    tm = min(256, L)
    nt = L // tm

    def body(h_ref, bg_ref, cg_ref, za_ref, hp_ref, hn_ref, cp_ref, cn_ref, yb_ref, ga_ref, gb_ref, zb_ref, o_ref,
             x_ref, t_ref, w_hbm, g_ref, fw_ref, cw_ref, cb_ref, gn_ref,
             dx1_ref, dya_ref, do_ref, dzb_ref, dgab_ref, dw_hbm, st_ref, w_vm, dw_acc, sem):
        i = pl.program_id(0)

        @pl.when(i == 0)
        def _():
            cp = pltpu.make_async_copy(w_hbm, w_vm, sem)
            cp.start()
            dw_acc[...] = jnp.zeros_like(dw_acc)
            st_ref[...] = jnp.zeros_like(st_ref)
            cp.wait()

        u = cg_ref[...].astype(F32) * h_ref[...].astype(F32)
        above = jnp.where(i == 0, 0.0, cp_ref[15:16, :].astype(F32) * hp_ref[15:16, :].astype(F32))
        below = jnp.where(i == nt - 1, 0.0, cn_ref[0:1, :].astype(F32) * hn_ref[0:1, :].astype(F32))
        dn, up = _shift_rows(u, above, below)
        co = cw_ref[0:1, :] * dn + cw_ref[1:2, :] * u + cw_ref[2:3, :] * up + cb_ref[...]
        za = za_ref[...].astype(F32)
        ya_b = (za * _sigmoid(za) * bg_ref[...].astype(F32) * co).astype(BF16)
        yb_b = yb_ref[...]
        y_a = _dot(ya_b, w_vm[0])
        y_b = _dot(yb_b, w_vm[1])
        sga = _sigmoid(ga_ref[...].astype(F32))
        sgb = _sigmoid(gb_ref[...].astype(F32))
        mix_b = (sga * y_a + sgb * y_b).astype(BF16)
        y_x = _dot(mix_b, w_vm[2])
        gvec, fwv = g_ref[...], fw_ref[...]
        x1 = x_ref[...] + gvec * y_x
        r1 = lax.rsqrt(jnp.mean(x1 * x1, axis=-1, keepdims=True) + EPS)
        xh = x1 * r1
        diff = xh * fwv - t_ref[...]
        dout = diff * (1.0 / D)
        dxh = dout * fwv
        dx1 = r1 * (dxh - xh * jnp.mean(dxh * xh, axis=-1, keepdims=True))
        dx1_ref[...] = dx1
        st_ref[0:1, :] += jnp.sum(dout * xh, axis=0, keepdims=True)
        st_ref[1:2, :] += jnp.sum(dx1 * y_x, axis=0, keepdims=True)
        st_ref[2:3, :] += jnp.sum(diff * diff, axis=0, keepdims=True)
        dyx_b = (dx1 * gvec).astype(BF16)
        dmix = _dot_nt(dyx_b, w_vm[2])
        dw_acc[2] += _dot_tn(mix_b, dyx_b)
        dya_b = (dmix * sga).astype(BF16)
        dyb_b = (dmix * sgb).astype(BF16)
        dgab_ref[:, 0:D] = (dmix * y_a * sga * (1.0 - sga)).astype(BF16)
        dgab_ref[:, D:2 * D] = (dmix * y_b * sgb * (1.0 - sgb)).astype(BF16)
        dya_ref[...] = _dot_nt(dya_b, w_vm[0])
        dyb = _dot_nt(dyb_b, w_vm[1])
        dw_acc[0] += _dot_tn(ya_b, dya_b)
        dw_acc[1] += _dot_tn(yb_b, dyb_b)

        for h in range(H):
            sl = slice(h * DV, (h + 1) * DV)
            ov = o_ref[:, sl]
            oc = ov - jnp.mean(ov, axis=-1, keepdims=True)
            rstd = lax.rsqrt(jnp.mean(oc * oc, axis=-1, keepdims=True) + EPS)
            rn = oc * rstd
            gw = gn_ref[:, sl]
            zb = zb_ref[:, sl].astype(F32)
            sz = _sigmoid(zb)
            dy = dyb[:, sl]
            dzb_ref[:, sl] = (dy * (rn * gw) * (sz * (1.0 + zb * (1.0 - sz)))).astype(BF16)
            dretn = dy * (zb * sz)
            st_ref[3:4, sl] += jnp.sum(dretn * rn, axis=0, keepdims=True)
            drn = dretn * gw
            do_ref[:, sl] = (rstd * (drn - jnp.mean(drn, axis=-1, keepdims=True)
                                     - rn * jnp.mean(drn * rn, axis=-1, keepdims=True))).astype(BF16)

        @pl.when(i == nt - 1)
        def _():
            out = pltpu.make_async_copy(dw_acc, dw_hbm, sem)
            out.start()
            out.wait()

    row = lambda col: pl.BlockSpec((tm, D), lambda i: (i, col))
    any_spec = pl.BlockSpec(memory_space=pl.ANY)
    f32o = jax.ShapeDtypeStruct((L, D), F32)
    bf16o = jax.ShapeDtypeStruct((L, D), BF16)
    hp, hn = _halo_specs(tm, L, D, 0)
    cp, cn = _halo_specs(tm, L, D, 2)
    return pl.pallas_call(
        body, name="mid", grid=(nt,),
        in_specs=[row(0), row(1), row(2), row(3), hp, hn, cp, cn, row(0), row(7), row(8), row(6), row(0),
                  row(0), row(0), any_spec, _vec_spec(D), _vec_spec(D),
                  pl.BlockSpec((8, D), lambda i: (0, 0)), _vec_spec(D), _vec_spec(D)],
        out_specs=[row(0), row(0), row(0), row(0), pl.BlockSpec((tm, 2 * D), lambda i: (i, 0)), any_spec,
                   pl.BlockSpec((8, D), lambda i: (0, 0))],
        out_shape=[f32o, f32o, bf16o, bf16o, jax.ShapeDtypeStruct((L, 2 * D), BF16),
                   jax.ShapeDtypeStruct((3, D, D), F32), jax.ShapeDtypeStruct((8, D), F32)],
        scratch_shapes=[pltpu.VMEM((3, D, D), BF16), pltpu.VMEM((3, D, D), F32), pltpu.SemaphoreType.DMA],
        compiler_params=_cparams(("arbitrary",), VMEM_LIMIT),
    )(p, p, p, p, p, p, p, p, yb, p, p, p, o, x, tgt, w3, g, fw, conv_w, conv_b, gn_w)


def _conv_bwd(dya, p, conv_w, conv_b, D, exchange=None):
    L = p.shape[0]
    tl = min(256, L)
    nt = L // tl

    def body(d_ref, h_ref, bg_ref, cg_ref, za_ref,
             dp_ref, dn_ref, hp_ref, hn_ref, bp_ref, bn_ref, cp_ref, cn_ref, zp_ref, zn_ref,
             w_ref, b_ref, dc_ref, st_ref):
        i = pl.program_id(0)

        @pl.when(i == 0)
        def _():
            st_ref[...] = jnp.zeros_like(st_ref)

        first, last = i == 0, i == nt - 1
        h = h_ref[...].astype(F32)
        cg = cg_ref[...].astype(F32)
        bg = bg_ref[...].astype(F32)
        za = za_ref[...].astype(F32)
        dy = d_ref[...].astype(F32)
        u = cg * h
        u_above = jnp.where(first, 0.0, cp_ref[15:16, :].astype(F32) * hp_ref[15:16, :].astype(F32))
        u_below = jnp.where(last, 0.0, cn_ref[0:1, :].astype(F32) * hn_ref[0:1, :].astype(F32))
        u_dn, u_up = _shift_rows(u, u_above, u_below)
        w0, w1, w2 = w_ref[0:1, :], w_ref[1:2, :], w_ref[2:3, :]
        co = w0 * u_dn + w1 * u + w2 * u_up + b_ref[...]
        sz = _sigmoid(za)
        silu = za * sz
        dc_ref[:, 3 * D:4 * D] = (dy * bg * co * (sz * (1.0 + za * (1.0 - sz)))).astype(BF16)
        dc_ref[:, D:2 * D] = (dy * silu * co).astype(BF16)
        dco = dy * silu * bg

        def edge(dr, zr, br, r):
            z = zr[r:r + 1, :].astype(F32)
            return dr[r:r + 1, :].astype(F32) * (z * _sigmoid(z)) * br[r:r + 1, :].astype(F32)

        dco_above = jnp.where(first, 0.0, edge(dp_ref, zp_ref, bp_ref, 15))
        dco_below = jnp.where(last, 0.0, edge(dn_ref, zn_ref, bn_ref, 0))
        dco_dn, dco_up = _shift_rows(dco, dco_above, dco_below)
        du = w0 * dco_up + w1 * dco + w2 * dco_dn
        dc_ref[:, 2 * D:3 * D] = (du * h).astype(BF16)
        dc_ref[:, 0:D] = (du * cg).astype(BF16)
        st_ref[0:1, :] += jnp.sum(dco * u_dn, axis=0, keepdims=True)
        st_ref[1:2, :] += jnp.sum(dco * u, axis=0, keepdims=True)
        st_ref[2:3, :] += jnp.sum(dco * u_up, axis=0, keepdims=True)
        st_ref[3:4, :] += jnp.sum(dco, axis=0, keepdims=True)

    main = lambda col: pl.BlockSpec((tl, D), lambda i: (i, col))
    halos = []
    for col in (0, 0, 1, 2, 3):
        halos.extend(_halo_specs(tl, L, D, col))
    return _riding_call(
        body, exchange, nt, name="conv_bwd",
        args=(dya, p, p, p, p, dya, dya, p, p, p, p, p, p, p, p, conv_w, conv_b),
        in_specs=[main(0), main(0), main(1), main(2), main(3)] + halos
                 + [pl.BlockSpec((8, D), lambda i: (0, 0)), _vec_spec(D)],
        out_specs=[pl.BlockSpec((tl, 4 * D), lambda i: (i, 0)), pl.BlockSpec((8, D), lambda i: (0, 0))],
        out_shape=[jax.ShapeDtypeStruct((L, 4 * D), BF16), jax.ShapeDtypeStruct((8, D), F32)],
        cparams=_cparams(("arbitrary",)))


def _ret_bwd_states(qr, do, tab, D):
    L = qr.shape[0]
    H = D // DV
    N = L // CHUNK
    HP = H // 2

    def body(tab_ref, qf_ref, qb_ref, dof_ref, dob_ref, dsf_out, dsb_out, ds0_out, dsf, dsb):
        n = pl.program_id(0)

        @pl.when(n == 0)
        def _():
            dsf[...] = jnp.zeros_like(dsf)
            dsb[...] = jnp.zeros_like(dsb)

        for cc in range(RET_CPB):
            cf_, cb_ = RET_CPB - 1 - cc, cc
            rf, rb = slice(cf_ * CHUNK, (cf_ + 1) * CHUNK), slice(cb_ * CHUNK, (cb_ + 1) * CHUNK)
            dsf_out[cf_] = dsf[...]
            dsb_out[cb_] = dsb[...]
            for pr in range(HP):
                qf2 = qf_ref[rf, pr * 128:(pr + 1) * 128].astype(F32)
                qb2 = qb_ref[rb, pr * 128:(pr + 1) * 128].astype(F32)
                inc_f, inc_b, gf, gb = [], [], [], []
                for e in range(2):
                    h = 2 * pr + e
                    inc_f.append(_dot_tn((qf2 * tab_ref[h, T_QF]).astype(BF16), dof_ref[rf, h * DV:(h + 1) * DV]))
                    inc_b.append(_dot_tn((qb2 * tab_ref[h, T_QB]).astype(BF16), dob_ref[rb, h * DV:(h + 1) * DV]))
                    cf, cb = _chunk_decay(tab_ref, h)
                    gf.append(jnp.broadcast_to(cf, (128, 128)))
                    gb.append(jnp.broadcast_to(cb, (128, 128)))
                dsf[pr] = _pair_select(gf[0], gf[1]) * dsf[pr] + _pair_select(inc_f[0], inc_f[1])
                dsb[pr] = _pair_select(gb[0], gb[1]) * dsb[pr] + _pair_select(inc_b[0], inc_b[1])

        @pl.when(n == NB - 1)
        def _():
            ds0_out[0] = dsf[...]
            ds0_out[1] = dsb[...]

    st = jax.ShapeDtypeStruct((N, HP, 128, 128), F32)
    R = RET_CPB * CHUNK
    NB = N // RET_CPB
    return pl.pallas_call(
        body, name="ret_bwd_states", grid=(NB,),
        in_specs=[_tab_spec(H),
                  pl.BlockSpec((R, D // 2), lambda n: (NB - 1 - n, 0)),
                  pl.BlockSpec((R, D // 2), lambda n: (n, 0)),
                  pl.BlockSpec((R, D), lambda n: (NB - 1 - n, 0)),
                  pl.BlockSpec((R, D), lambda n: (n, 0))],
        out_specs=[pl.BlockSpec((RET_CPB, HP, 128, 128), lambda n: (NB - 1 - n, 0, 0, 0)),
                   pl.BlockSpec((RET_CPB, HP, 128, 128), lambda n: (n, 0, 0, 0)),
                   pl.BlockSpec((2, HP, 128, 128), lambda n: (0, 0, 0, 0))],
        out_shape=[st, st, jax.ShapeDtypeStruct((2, HP, 128, 128), F32)],
        scratch_shapes=[pltpu.VMEM((HP, 128, 128), F32), pltpu.VMEM((HP, 128, 128), F32)],
        compiler_params=_cparams(("arbitrary",)),
    )(tab, qr, qr, do, do)


def _ret_bwd_main(qr, kr, p, do, sf_prev, sb_prev, dsf, dsb, c2, s2, tab, D, exchange=None):
    L = qr.shape[0]
    H = D // DV
    N = L // CHUNK
    HP = H // 2
    W = D // 2

    def body(tab_ref, q_ref, k_ref, v_ref, do_ref, sf_ref, sb_ref, dsf_ref, dsb_ref, c_ref, s_ref,
             dr_ref, st_ref, dl_acc):
        @pl.when(pl.program_id(0) == 0)
        def _():
            dl_acc[...] = jnp.zeros_like(dl_acc)

        i = lax.broadcasted_iota(jnp.int32, (CHUNK, 128), 0).astype(F32)
        rowid = lax.broadcasted_iota(jnp.int32, (128, 128), 0)

        def chunk(cc, carry):
            rows = pl.ds(pl.multiple_of(cc * CHUNK, CHUNK), CHUNK)
            c, s = c_ref[rows, :], s_ref[rows, :]
            for pr in range(HP):
                ps = slice(pr * 128, (pr + 1) * 128)
                q2, k2 = q_ref[rows, ps], k_ref[rows, ps]
                sf32, sb32 = sf_ref[cc, pr], sb_ref[cc, pr]
                dsf32, dsb32 = dsf_ref[cc, pr], dsb_ref[cc, pr]
                sfp, sbp = sf32.astype(BF16), sb32.astype(BF16)
                dsfp, dsbp = dsf32.astype(BF16), dsb32.astype(BF16)
                dq2 = jnp.zeros((CHUNK, 128), F32)
                dk2 = jnp.zeros((CHUNK, 128), F32)
                for e in range(2):
                    h = 2 * pr + e
                    sl = slice(h * DV, (h + 1) * DV)
                    hm = _head_lane_mask(q2.shape, e)
                    qm = jnp.where(hm, q2, jnp.zeros_like(q2))
                    km = jnp.where(hm, k2, jnp.zeros_like(k2))
                    qf, kf = qm.astype(F32), km.astype(F32)
                    v, do = v_ref[rows, sl], do_ref[rows, sl]
                    vf, dof = v.astype(F32), do.astype(F32)
                    m_t = tab_ref[h, T_MT]
                    sc = _dot_nt(qm, k2)
                    dpm = _dot_nt(do, v)
                    dsc = (dpm * tab_ref[h, T_M]).astype(BF16)
                    a_t = (_dot_nt(km, q2) * m_t).astype(BF16)
                    dsc_t = (_dot_nt(v, do) * m_t).astype(BF16)
                    dq_f, dq_b = tab_ref[h, T_QF], tab_ref[h, T_QB]
                    dk_f, dk_b = tab_ref[h, T_KF], tab_ref[h, T_KB]
                    dq = _dot(dsc, km)
                    dq += jnp.where(hm, dq_f * _dot_nt(do, sfp) + dq_b * _dot_nt(do, sbp), 0.0)
                    dk = _dot(dsc_t, qm)
                    dk += jnp.where(hm, dk_f * _dot_nt(v, dsfp) + dk_b * _dot_nt(v, dsbp), 0.0)
                    kdf = _dot((kf * dk_f).astype(BF16), dsfp)
                    kdb = _dot((kf * dk_b).astype(BF16), dsbp)
                    dr_ref[rows, D + h * DV:D + (h + 1) * DV] = (_dot(a_t, do) + kdf + kdb).astype(BF16)
                    dq2 += dq
                    dk2 += dk
                    xf = _dot((qf * dq_f).astype(BF16), sfp)
                    xb = _dot((qf * dq_b).astype(BF16), sbp)
                    pair = (rowid < DK) if e == 0 else (rowid >= DK)
                    gcf, gcb = tab_ref[h, T_QF, CHUNK - 1:CHUNK, 0:1], tab_ref[h, T_QB, 0:1, 0:1]
                    scdp = sc * dpm
                    dl_acc[h, 0] += scdp * tab_ref[h, T_MF1] + xf * dof * (i + 1.0) \
                        + kdf * vf * (CHUNK - 1.0 - i) + (CHUNK * gcf) * jnp.where(pair, dsf32 * sf32, 0.0)
                    dl_acc[h, 1] += scdp * tab_ref[h, T_MB1] + xb * dof * (CHUNK - i) \
                        + kdb * vf * i + (CHUNK * gcb) * jnp.where(pair, dsb32 * sb32, 0.0)
                dr_ref[rows, ps] = (dq2 * c - _swap_halves(dq2) * s).astype(BF16)
                dr_ref[rows, W + pr * 128:W + (pr + 1) * 128] = \
                    ((dk2 * c - _swap_halves(dk2) * s) * K_SCALE).astype(BF16)
            return carry

        lax.fori_loop(0, RET_CPB, chunk, 0)

        @pl.when(pl.program_id(0) == N // RET_CPB - 1)
        def _():
            lane = lax.broadcasted_iota(jnp.int32, (1, 128), 1)
            acc = [jnp.zeros((1, 128), F32), jnp.zeros((1, 128), F32)]
            for h in range(H):
                for b in range(2):
                    acc[b] += jnp.where(lane == h, _sum_all(dl_acc[h, b]), 0.0)
            st_ref[...] = jnp.zeros_like(st_ref)
            st_ref[0:1, :] = acc[0]
            st_ref[1:2, :] = acc[1]

    R = RET_CPB * CHUNK
    st_spec = pl.BlockSpec((RET_CPB, HP, 128, 128), lambda n: (n, 0, 0, 0))
    half = pl.BlockSpec((R, W), lambda n: (n, 0))
    rope = pl.BlockSpec((R, 128), lambda n: (n, 0))
    return _riding_call(
        body, exchange, N // RET_CPB, name="ret_bwd_main",
        args=(tab, qr, kr, p, do, sf_prev, sb_prev, dsf, dsb, c2, s2),
        in_specs=[_tab_spec(H), half, half,
                  pl.BlockSpec((R, D), lambda n: (n, 5)),
                  pl.BlockSpec((R, D), lambda n: (n, 0)),
                  st_spec, st_spec, st_spec, st_spec, rope, rope],
        out_specs=[pl.BlockSpec((R, 2 * D), lambda n: (n, 0)),
                   pl.BlockSpec((8, 128), lambda n: (0, 0))],
        out_shape=[jax.ShapeDtypeStruct((L, 2 * D), BF16), jax.ShapeDtypeStruct((8, 128), F32)],
        scratch=[pltpu.VMEM((H, 2, CHUNK, 128), F32)],
        cparams=_cparams(("arbitrary",)))


def _ctx_bwd(pc, pqk_c, ds0, lg, D):
    Lc = pc.shape[0]
    H = D // DV
    HP = H // 2
    W = D // 2

    def body(lg_ref, k_ref, v_ref, ds_ref, dr_ref, st_ref):
        dqk_ref = dr_ref.at[:, 0:D]
        dv_ref = dr_ref.at[:, D:2 * D]
        m = lax.broadcasted_iota(jnp.int32, (Lc, 128), 0).astype(F32)
        lane = lax.broadcasted_iota(jnp.int32, (1, 128), 1)
        acc_f = jnp.zeros((1, 128), F32)
        acc_b = jnp.zeros((1, 128), F32)
        dqk_ref[:, 0:W] = jnp.zeros((Lc, W), BF16)
        for pr in range(HP):
            ps = slice(pr * 128, (pr + 1) * 128)
            k2 = k_ref[:, ps].astype(F32) * K_SCALE
            dsfp, dsbp = ds_ref[0, pr].astype(BF16), ds_ref[1, pr].astype(BF16)
            dk2 = jnp.zeros((Lc, 128), F32)
            for e in range(2):
                h = 2 * pr + e
                sl = slice(h * DV, (h + 1) * DV)
                hm = _head_lane_mask(k2.shape, e)
                km = jnp.where(hm, k2, 0.0)
                v = v_ref[:, sl]
                vf = v.astype(F32)
                dec_f = jnp.exp(lg_ref[0, h] * (Lc - 1.0 - m))
                dec_b = jnp.exp(lg_ref[1, h] * m)
                kdf = _dot((km * dec_f).astype(BF16), dsfp)
                kdb = _dot((km * dec_b).astype(BF16), dsbp)
                dv_ref[:, sl] = (kdf + kdb).astype(BF16)
                dk2 += jnp.where(hm, dec_f * _dot_nt(v, dsfp) + dec_b * _dot_nt(v, dsbp), 0.0)
                acc_f += jnp.where(lane == h, _sum_all(kdf * vf * (Lc - 1.0 - m)), 0.0)
                acc_b += jnp.where(lane == h, _sum_all(kdb * vf * m), 0.0)
            dqk_ref[:, W + pr * 128:W + (pr + 1) * 128] = (dk2 * K_SCALE).astype(BF16)
        st_ref[...] = jnp.zeros_like(st_ref)
        st_ref[0:1, :] = acc_f
        st_ref[1:2, :] = acc_b

    return pl.pallas_call(
        body, name="ctx_bwd", grid=(1,),
        in_specs=[_smem_spec(), pl.BlockSpec((Lc, W), lambda i: (0, 1)), pl.BlockSpec((Lc, D), lambda i: (0, 1)),
                  pl.BlockSpec((2, HP, 128, 128), lambda i: (0, 0, 0, 0))],
        out_specs=[pl.BlockSpec((Lc, 2 * D), lambda i: (0, 0)), pl.BlockSpec((8, 128), lambda i: (0, 0))],
        out_shape=[jax.ShapeDtypeStruct((Lc, 2 * D), BF16), jax.ShapeDtypeStruct((8, 128), F32)],
    )(lg, pqk_c, pc, ds0)


class _Exchange(NamedTuple):
    inputs: tuple
    out_shapes: tuple
    n_copies: int
    build: Callable
    in_place: bool = False


def _exchange_parts(exchange):
    if exchange is None:
        return [], [], [], [], []
    n = exchange.n_copies
    return (list(exchange.inputs), [ANY] * len(exchange.inputs), list(exchange.out_shapes),
            [ANY] * len(exchange.out_shapes), [pltpu.SemaphoreType.DMA((n,)), pltpu.SemaphoreType.DMA((n,))])


def _riding_call(body, exchange, n_steps, *, args, in_specs, out_specs, out_shape, name, cparams, scratch=()):
    ex_args, ex_in_specs, ex_shapes, ex_out_specs, ex_scratch = _exchange_parts(exchange)
    n_in, n_out, n_sc = len(args), len(out_shape), len(scratch)

    def riding(*refs):
        k = n_in + len(ex_args)
        ins, ex_in = refs[:n_in], refs[n_in:k]
        outs, ex_out = refs[k:k + n_out], refs[k + n_out:k + n_out + len(ex_shapes)]
        k += n_out + len(ex_shapes)
        own_scratch, ex_sems = refs[k:k + n_sc], refs[k + n_sc:]
        step = pl.program_id(0)
        if exchange is not None:
            @pl.when(step == 0)
            def _():
                for rc in exchange.build(ex_in, ex_out, *ex_sems):
                    rc.start()
        body(*ins, *outs, *own_scratch)
        if exchange is not None:
            @pl.when(step == n_steps - 1)
            def _():
                for rc in exchange.build(ex_in, ex_out, *ex_sems):
                    rc.wait()

    aliases = {}
    if exchange is not None and exchange.in_place:
        aliases = {n_in + i: n_out + i for i in range(len(ex_args))}
    return tuple(pl.pallas_call(
        riding, name=name, grid=(n_steps,),
        in_specs=list(in_specs) + ex_in_specs, out_specs=list(out_specs) + ex_out_specs,
        out_shape=list(out_shape) + ex_shapes, scratch_shapes=list(scratch) + ex_scratch,
        input_output_aliases=aliases, compiler_params=cparams,
    )(*args, *ex_args))


def _dxm(groups, col0, w, x, nw, sc, dx1, name, exchange=None):
    L, D = x.shape
    tm = min(256, L)
    nt = L // tm
    ng = len(groups)
    widths = [g.shape[1] for g in groups]
    wtot = sum(widths)
    with_dx = dx1 is not None
    ex_args, ex_in_specs, ex_shapes, ex_out_specs, ex_scratch = _exchange_parts(exchange)
    n_in = ng + 4 + (1 if with_dx else 0)
    n_out = 2 if with_dx else 1

    def body(*refs):
        group_refs = refs[:ng]
        w_hbm, x_ref, nw_ref, sc_ref = refs[ng:ng + 4]
        ex_in = refs[n_in:n_in + len(ex_args)]
        outs = refs[n_in + len(ex_args):]
        if with_dx:
            dx1_ref, gx_ref, st_ref = refs[ng + 4], outs[0], outs[1]
        else:
            st_ref = outs[0]
        ex_out = outs[n_out:n_out + len(ex_shapes)]
        w_vm, sem = outs[n_out + len(ex_shapes):n_out + len(ex_shapes) + 2]
        ex_sems = outs[n_out + len(ex_shapes) + 2:]
        i = pl.program_id(0)

        @pl.when(i == 0)
        def _():
            cp = pltpu.make_async_copy(w_hbm.at[:, col0 * D:col0 * D + wtot], w_vm, sem)
            cp.start()
            if exchange is not None:
                for rc in exchange.build(ex_in, ex_out, *ex_sems):
                    rc.start()
            st_ref[...] = jnp.zeros_like(st_ref)
            cp.wait()

        dxm, off = None, 0
        for g_ref, wd in zip(group_refs, widths):
            part = _dot_nt(g_ref[...], w_vm[:, off:off + wd])
            dxm = part if dxm is None else dxm + part
            off += wd

        xv = x_ref[...]
        r = lax.rsqrt(jnp.mean(xv * xv, axis=-1, keepdims=True) + EPS)
        xh = xv * r
        nwv = nw_ref[...]
        dxn = dxm * (1.0 + sc_ref[...])
        st_ref[0:1, :] += jnp.sum(dxm, axis=0, keepdims=True)
        st_ref[1:2, :] += jnp.sum(dxm * (xh * nwv), axis=0, keepdims=True)
        st_ref[2:3, :] += jnp.sum(dxn * xh, axis=0, keepdims=True)
        if with_dx:
            dxh = dxn * nwv
            gx_ref[...] = dx1_ref[...] + r * (dxh - xh * jnp.mean(dxh * xh, axis=-1, keepdims=True))

        if exchange is not None:
            @pl.when(i == nt - 1)
            def _():
                for rc in exchange.build(ex_in, ex_out, *ex_sems):
                    rc.wait()

    row = pl.BlockSpec((tm, D), lambda i: (i, 0))
    in_specs = [pl.BlockSpec((tm, wd), lambda i: (i, 0)) for wd in widths] + [ANY, row, _vec_spec(D), _vec_spec(D)]
    out_specs = [pl.BlockSpec((8, D), lambda i: (0, 0))]
    out_shape = [jax.ShapeDtypeStruct((8, D), F32)]
    args = list(groups) + [w, x, nw, sc]
    if with_dx:
        in_specs.append(row)
        out_specs.insert(0, row)
        out_shape.insert(0, jax.ShapeDtypeStruct((L, D), F32))
        args.append(dx1)
    res = pl.pallas_call(
        body, name=name, grid=(nt,),
        in_specs=in_specs + ex_in_specs, out_specs=out_specs + ex_out_specs, out_shape=out_shape + ex_shapes,
        scratch_shapes=[pltpu.VMEM((D, wtot), BF16), pltpu.SemaphoreType.DMA] + ex_scratch,
        compiler_params=_cparams(("arbitrary",), VMEM_LIMIT),
    )(*args, *ex_args)
    gx = res[0] if with_dx else None
    return (gx, res[n_out - 1], *res[n_out:])


DW_TN = 512
DW_RING = 4


def _dw_in(xmt, groups, cmt, dr_c, D, pair):
    L = xmt.shape[1]
    Lc = cmt.shape[1]
    Dh = D // 2
    tn = min(DW_TN, D)
    nblk = [g.shape[1] // tn for g in groups]
    starts = [sum(nblk[:g]) for g in range(len(groups))]
    ng = len(groups)
    nj = sum(nblk)
    rows_out = Dh if pair else D

    def body(*refs):
        xt_hbm = refs[0]
        group_refs = refs[1:1 + ng]
        ct_hbm, drc_ref, o_ref = refs[1 + ng:4 + ng]
        rest = refs[4 + ng:]
        if pair:
            ra_hbm, xt_vm, ct_vm, loc, ring, s_send, s_recv = rest
            pos = _position()
            sib = _peer(pos, 1)
        else:
            xt_vm, ct_vm, loc = rest
        j = pl.program_id(0)

        @pl.when(j == 0)
        def _():
            if pair:
                c = pos[2]
                other = pl.ds(pl.multiple_of((1 - c) * Dh, Dh), Dh)
                mine = pl.ds(pl.multiple_of(c * Dh, Dh), Dh)
                cps = [pltpu.make_async_copy(xt_hbm.at[other, :], xt_vm.at[0:Dh, :], loc.at[0]),
                       pltpu.make_async_copy(xt_hbm.at[mine, :], xt_vm.at[Dh:D, :], loc.at[1]),
                       pltpu.make_async_copy(ct_hbm.at[other, :], ct_vm.at[0:Dh, :], loc.at[2]),
                       pltpu.make_async_copy(ct_hbm.at[mine, :], ct_vm.at[Dh:D, :], loc.at[3])]
            else:
                cps = [pltpu.make_async_copy(xt_hbm, xt_vm, loc.at[0]), pltpu.make_async_copy(ct_hbm, ct_vm, loc.at[1])]
            for cp in cps:
                cp.start()
            for cp in cps:
                cp.wait()

        def send(slot):
            cols = pl.ds(pl.multiple_of(j * tn, 128), tn)
            return pltpu.make_async_remote_copy(src_ref=ring.at[slot], dst_ref=ra_hbm.at[:, cols],
                                                send_sem=s_send.at[slot], recv_sem=s_recv,
                                                device_id=sib, device_id_type=MESH)

        for g in range(ng):
            @pl.when((j >= starts[g]) & (j < starts[g] + nblk[g]))
            def _(g=g):
                acc = _dot(xt_vm[...], group_refs[g][...])
                if g == 1:
                    acc += _dot(ct_vm[...], drc_ref[...])
                if not pair:
                    o_ref[...] = acc
                    return
                o_ref[...] = acc[Dh:, :]
                slot = lax.rem(j, DW_RING)

                @pl.when(j >= DW_RING)
                def _():
                    send(slot).wait_send()

                ring[slot] = acc[0:Dh, :]
                send(slot).start()

        if pair:
            @pl.when(j == nj - 1)
            def _():
                pltpu.make_async_remote_copy(src_ref=ra_hbm, dst_ref=ra_hbm, send_sem=s_send.at[0], recv_sem=s_recv,
                                             device_id=sib, device_id_type=MESH).wait_recv()
                for slot in range(DW_RING):
                    send(slot).wait_send()

    def group_spec(g, rows):
        return pl.BlockSpec((rows, tn), lambda j: (0, jnp.clip(j - starts[g], 0, nblk[g] - 1)))

    out_specs = [pl.BlockSpec((rows_out, tn), lambda j: (0, j))]
    out_shape = [jax.ShapeDtypeStruct((rows_out, nj * tn), F32)]
    scratch = [pltpu.VMEM((D, L), BF16), pltpu.VMEM((D, Lc), BF16), pltpu.SemaphoreType.DMA((4,))]
    if pair:
        out_specs.append(ANY)
        out_shape.append(jax.ShapeDtypeStruct((Dh, nj * tn), F32))
        scratch += [pltpu.VMEM((DW_RING, Dh, tn), F32), pltpu.SemaphoreType.DMA((DW_RING,)), pltpu.SemaphoreType.DMA]
    return tuple(pl.pallas_call(
        body, name="dw_in", grid=(nj,),
        in_specs=[ANY] + [group_spec(g, L) for g in range(ng)] + [ANY, group_spec(1, Lc)],
        out_specs=out_specs, out_shape=out_shape, scratch_shapes=scratch,
        compiler_params=_cparams(("arbitrary",), VMEM_LIMIT),
    )(xmt, *groups, cmt, dr_c))


def _local_step(x, ctx, tgt, mod_x, mod_c, norm_w, conv_w8, conv_b, lg, gn_w, fw, project, csidx=None):
    L, D = x.shape
    sh_x, sc_x, g_x = mod_x[0:1], mod_x[1:2], mod_x[2:3]
    sh_c, sc_c = mod_c[0:1], mod_c[1:2]
    c2, s2 = _rope_tables(L)
    tab = _decay_tables(lg, D // DV)

    xm, xmt = _norm_mod(x, norm_w, sc_x, sh_x, "norm_mod_x")
    cm, cmt = _norm_mod(ctx, norm_w, sc_c, sh_c, "norm_mod_ctx")
    reduce = csidx is not None
    p, qr, kr, w_in, w3 = project(xm, c2, s2)
    pc, pqk_c = _in_proj(cm, w_in, "in_proj_ctx", QK_BLOCK, 2)
    s0 = _ctx_states(pc, pqk_c, lg, D)
    sf_prev, sb_prev = _ret_states(kr, p, s0, tab, D)
    o, yb = _ret_out(qr, kr, p, sf_prev, sb_prev, gn_w, tab, D)
    dx1, dya, do, dzb, dgab, dw3, st_mid = _mid(p, yb, o, x, tgt, w3, g_x, fw, conv_w8, conv_b, gn_w, D)
    dw3_5 = dw3.reshape(3, N_SHARD, 2, D // 8, D)
    dconv, st_conv, *ra_3 = _conv_bwd(dya, p, conv_w8, conv_b, D, _pair_exchange_w3(dw3_5) if reduce else None)
    dsf, dsb, ds0 = _ret_bwd_states(qr, do, tab, D)
    cs_3 = _sum_pair_w3(csidx[0:1], dw3_5, ra_3[0]) if reduce else None
    dret, st_lg, *rb_3 = _ret_bwd_main(qr, kr, p, do, sf_prev, sb_prev, dsf, dsb, c2, s2, tab, D,
                                       _chips_exchange_w3(cs_3) if reduce else None)
    g_3 = _sum_chips_w3(csidx, cs_3, rb_3[0]) if reduce else dw3
    dret_c, st_lgc = _ctx_bwd(pc, pqk_c, ds0, lg, D)
    groups = (dconv, dret, dzb, dgab)
    _, st_c = _dxm((dret_c,), 4, w_in, ctx, norm_w, sc_c, None, "dxm_ctx")
    return groups, dret_c, xmt, cmt, dx1, sc_x, w_in, g_3, (st_mid, st_conv, st_lg, st_lgc, st_c)


CHIP_FLIPS = (4, 2, 6)
ANY = pl.BlockSpec(memory_space=pl.ANY)
VMEM_FULL = pl.BlockSpec(memory_space=pltpu.VMEM)


def _position():
    return lax.axis_index("x"), lax.axis_index("y"), lax.axis_index("c")


def _peer(pos, k):
    x, y, c = pos
    return (1 - x if k & 4 else x, 1 - y if k & 2 else y, 1 - c if k & 1 else c)


def _dev_id(pos):
    return 4 * pos[0] + 2 * pos[1] + pos[2]


def _shard_of(pos):
    return 2 * pos[0] + pos[1]


def _remote(src, dst, send_sems, recv_sems, idx, to):
    return pltpu.make_async_remote_copy(src_ref=src, dst_ref=dst, send_sem=send_sems.at[idx],
                                        recv_sem=recv_sems.at[idx], device_id=to, device_id_type=MESH)


def _dot_f32(a, b):
    return jnp.dot(a, b, precision=lax.Precision.HIGHEST, preferred_element_type=F32)


def _silu(x):
    return x * _sigmoid(x)


def _fwd_small(c8, cctx8, ada_w, ada_b, conv_w8):
    D = c8.shape[1]
    Wm = ada_w.shape[1]
    Dq = conv_w8.shape[1]

    def body(c_ref, cc_ref, aw_ref, ab_ref, cw_ref, act_ref, mod_ref, cwf_ref,
             cbuf, pmine, pbuf, wbuf, s_c, r_c, s_p, r_p, s_w, r_w):
        pos = _position()
        me, s = _dev_id(pos), _shard_of(pos)
        cbuf[me] = c_ref[...]
        wbuf[s] = cw_ref[...]
        sends = [_remote(c_ref, cbuf.at[me], s_c, r_c, k - 1, _peer(pos, k)) for k in range(1, 8)]
        sends += [_remote(cw_ref, wbuf.at[s], s_w, r_w, j, _peer(pos, k)) for j, k in enumerate(CHIP_FLIPS)]
        for cp in sends:
            cp.start()
        for k in range(1, 8):
            _remote(c_ref, cbuf.at[_dev_id(_peer(pos, k))], s_c, r_c, k - 1, _peer(pos, k)).wait_recv()
        for d in range(N_DEV):
            act_ref[d:d + 1, :] = _silu(cbuf[d, 0:1, :])
        act_ref[8:9, :] = _silu(cc_ref[0:1, :])
        act_ref[9:16, :] = jnp.zeros((7, D), F32)
        part = _dot_f32(act_ref[...], aw_ref[...])
        pmine[...] = part
        pbuf[s] = part
        psend = [_remote(pmine, pbuf.at[s], s_p, r_p, j, _peer(pos, k)) for j, k in enumerate(CHIP_FLIPS)]
        for cp in psend:
            cp.start()
        for j, k in enumerate(CHIP_FLIPS):
            t = _shard_of(_peer(pos, k))
            _remote(pmine, pbuf.at[t], s_p, r_p, j, _peer(pos, k)).wait_recv()
            _remote(cw_ref, wbuf.at[t], s_w, r_w, j, _peer(pos, k)).wait_recv()
        for t in range(N_SHARD):
            mod_ref[:, t * Wm:(t + 1) * Wm] = pbuf[t] + ab_ref[:, t * Wm:(t + 1) * Wm]
            cwf_ref[:, t * Dq:(t + 1) * Dq] = wbuf[t]
        for cp in sends + psend:
            cp.wait_send()

    return pl.pallas_call(
        body, name="fwd_small",
        in_specs=[VMEM_FULL] * 5, out_specs=[VMEM_FULL] * 3,
        out_shape=[jax.ShapeDtypeStruct((16, D), F32), jax.ShapeDtypeStruct((16, 3 * D), F32),
                   jax.ShapeDtypeStruct((8, D), F32)],
        scratch_shapes=[pltpu.VMEM((N_DEV, 8, D), F32), pltpu.VMEM((16, Wm), F32),
                        pltpu.VMEM((N_SHARD, 16, Wm), F32), pltpu.VMEM((N_SHARD, 8, Dq), F32),
                        pltpu.SemaphoreType.DMA((7,)), pltpu.SemaphoreType.DMA((7,)),
                        pltpu.SemaphoreType.DMA((3,)), pltpu.SemaphoreType.DMA((3,)),
                        pltpu.SemaphoreType.DMA((3,)), pltpu.SemaphoreType.DMA((3,))],
        compiler_params=_cparams(None, VMEM_LIMIT),
    )(c8, cctx8, ada_w, ada_b, conv_w8)


AG_CHUNKS = 3


def _ag_in_proj(xm, w_in_s, w3_s, c2, s2):
    L, D = xm.shape
    Wc = w_in_s.shape[1]
    Wq = Wc // AG_CHUNKS
    Dh = D // 2
    Do = w3_s[0].shape[1]
    TM = min(1024, L // 4)
    NT = L // TM
    RC = min(128, Dh)
    NQ = AG_CHUNKS
    order = [(q, j) for q in range(NQ) for j in (0, 1)] + [(q, 2) for q in range(NQ)]

    def body(xm_ref, wi_hbm, wa_ref, wb_ref, wo_ref, c_ref, s_ref, p_hbm, qr_hbm, kr_hbm, fi_hbm, f3_hbm,
             w_vm, cast_buf, s3, stage, qk_stage, ici_s, ici_r, d2d_s, d2d_r, w3_s_, w3_r_, loc, out_sem, qk_sem):
        pos = _position()
        c = pos[2]
        s = _shard_of(pos)
        sib = _peer(pos, 1)
        mine = pl.ds(pl.multiple_of(c * Dh, Dh), Dh)
        other = pl.ds(pl.multiple_of((1 - c) * Dh, Dh), Dh)

        def cast_half(hf):
            def step(i, carry):
                rows = pl.ds(pl.multiple_of(hf * Dh + i * RC, RC), RC)
                cp = pltpu.make_async_copy(wi_hbm.at[rows, :], cast_buf, loc.at[0])
                cp.start()
                cp.wait()
                for q in range(NQ):
                    w_vm[0, q, rows, :] = cast_buf[:, q * Wq:(q + 1) * Wq].astype(BF16)
                return carry
            lax.fori_loop(0, Dh // RC, step, 0)

        def abs_col(t, q):
            return pl.ds(pl.multiple_of(t * Wc + q * Wq, 128), Wq)

        cast_half(c)
        for a, w_ref in enumerate((wa_ref, wb_ref, wo_ref)):
            s3[a] = w_ref[...].astype(BF16)
        sends = [_remote(w_vm.at[0, q, mine, :], w_vm.at[1 + j, q, mine, :], ici_s, ici_r, q * 3 + j,
                         _peer(pos, CHIP_FLIPS[j])) for q, j in order]
        for j, k in enumerate(CHIP_FLIPS):
            sends.append(_remote(s3.at[:, c], f3_hbm.at[:, s, c], w3_s_, w3_r_, j, _peer(pos, k)))
        for cp in sends:
            cp.start()
        cast_half(1 - c)
        local = [pltpu.make_async_copy(s3, f3_hbm.at[:, s], loc.at[1])]
        local += [pltpu.make_async_copy(w_vm.at[0, q], fi_hbm.at[:, abs_col(s, q)], loc.at[2 + q]) for q in range(NQ)]
        for cp in local:
            cp.start()

        def out_copy(slot, rows, cols):
            return pltpu.make_async_copy(stage.at[slot], p_hbm.at[rows, cols], out_sem.at[slot])

        def block(r, q, t, first):
            cols = abs_col(t, q)

            def row_tile(rt, carry):
                rows = pl.ds(pl.multiple_of(rt * TM, TM), TM)
                acc = _dot(xm_ref[rows, :], w_vm[r, q])
                slot = lax.rem(rt, 2)

                @pl.when(rt >= 2 if first else rt >= 0)
                def _():
                    out_copy(slot, rows, cols).wait()

                stage[slot] = acc.astype(BF16)
                out_copy(slot, rows, cols).start()

                def rotary(lo, scale, dst_hbm):
                    c, s = c_ref[rows, :], s_ref[rows, :]
                    for pr in range(Dh // 128):
                        tq = acc[:, lo + pr * 128:lo + (pr + 1) * 128] * scale
                        qk_stage[:, pr * 128:(pr + 1) * 128] = (tq * c + _swap_halves(tq) * s).astype(BF16)
                    cp = pltpu.make_async_copy(qk_stage, dst_hbm.at[rows, :], qk_sem)
                    cp.start()
                    cp.wait()

                if q == NQ - 1:
                    @pl.when(t == 1)
                    def _():
                        rotary(Wq - Dh, 1.0, qr_hbm)
                if q == 0:
                    @pl.when(t == 2)
                    def _():
                        rotary(0, K_SCALE, kr_hbm)
                return carry

            lax.fori_loop(0, NT, row_tile, 0)

        for q in range(NQ):
            block(0, q, s, q == 0)
        passed = []
        for q, j in order:
            r, idx = 1 + j, q * 3 + j
            t = _shard_of(_peer(pos, CHIP_FLIPS[j]))
            _remote(w_vm.at[r, q, mine, :], w_vm.at[r, q, mine, :], ici_s, ici_r, idx, sib).wait_recv()
            fwd = _remote(w_vm.at[r, q, mine, :], w_vm.at[r, q, mine, :], d2d_s, d2d_r, idx, sib)
            fwd.start()
            passed.append(fwd)
            _remote(w_vm.at[r, q, other, :], w_vm.at[r, q, other, :], d2d_s, d2d_r, idx, sib).wait_recv()
            block(r, q, t, False)
            cp = pltpu.make_async_copy(w_vm.at[r, q], fi_hbm.at[:, abs_col(t, q)], loc.at[2 + NQ + idx])
            cp.start()
            local.append(cp)
        for j, k in enumerate(CHIP_FLIPS):
            t = _shard_of(_peer(pos, k))
            _remote(s3.at[:, c], f3_hbm.at[:, t, c], w3_s_, w3_r_, j, sib).wait_recv()
            fwd = _remote(f3_hbm.at[:, t, c], f3_hbm.at[:, t, c], w3_s_, w3_r_, 3 + j, sib)
            fwd.start()
            passed.append(fwd)
        for j, k in enumerate(CHIP_FLIPS):
            t = _shard_of(_peer(pos, k))
            _remote(s3.at[:, c], f3_hbm.at[:, t, 1 - c], w3_s_, w3_r_, 3 + j, sib).wait_recv()
        for cp in sends + passed:
            cp.wait_send()
        for cp in local:
            cp.wait()
        for slot in range(2):
            out_copy(slot, pl.ds(0, TM), abs_col(s, 0)).wait()

    n_loc = 2 + NQ + 3 * NQ
    return pl.pallas_call(
        body, name="ag_in_proj",
        in_specs=[VMEM_FULL, ANY, VMEM_FULL, VMEM_FULL, VMEM_FULL, VMEM_FULL, VMEM_FULL], out_specs=[ANY] * 5,
        out_shape=[jax.ShapeDtypeStruct((L, N_SHARD * Wc), BF16),
                   jax.ShapeDtypeStruct((L, Dh), BF16), jax.ShapeDtypeStruct((L, Dh), BF16),
                   jax.ShapeDtypeStruct((D, N_SHARD * Wc), BF16), jax.ShapeDtypeStruct((3, N_SHARD, 2, Do, D), BF16)],
        scratch_shapes=[pltpu.VMEM((N_SHARD, NQ, D, Wq), BF16), pltpu.VMEM((RC, Wc), F32),
                        pltpu.VMEM((3, 2, Do, D), BF16), pltpu.VMEM((2, TM, Wq), BF16), pltpu.VMEM((TM, Dh), BF16),
                        pltpu.SemaphoreType.DMA((3 * NQ,)), pltpu.SemaphoreType.DMA((3 * NQ,)),
                        pltpu.SemaphoreType.DMA((3 * NQ,)), pltpu.SemaphoreType.DMA((3 * NQ,)),
                        pltpu.SemaphoreType.DMA((6,)), pltpu.SemaphoreType.DMA((6,)),
                        pltpu.SemaphoreType.DMA((n_loc,)), pltpu.SemaphoreType.DMA((2,)), pltpu.SemaphoreType.DMA],
        compiler_params=_cparams(None, VMEM_LIMIT),
    )(xm, w_in_s, *w3_s, c2, s2)


def _pair_exchange_w3(dw3):
    _, _, _, Do, D = dw3.shape

    def build(ins, outs, send, recv):
        pos = _position()
        return [_remote(ins[0].at[:, :, 1 - pos[2]], outs[0], send, recv, 0, _peer(pos, 1))]

    return _Exchange((dw3,), (jax.ShapeDtypeStruct((3, N_SHARD, Do, D), F32),), 1, build)


def _sum_pair_in(dw_mine, ri):
    Dh, Wf = dw_mine.shape
    Wc = Wf // N_SHARD
    tr = min(256, Dh)

    def body(a_ref, b_ref, o_ref):
        o_ref[...] = (a_ref[...] + b_ref[...]).astype(BF16)

    return pl.pallas_call(
        body, name="sum_pair_in", grid=(Dh // tr, N_SHARD),
        in_specs=[pl.BlockSpec((tr, Wc), lambda i, t: (i, t)), pl.BlockSpec((tr, Wc), lambda i, t: (i, t))],
        out_specs=pl.BlockSpec((None, tr, Wc), lambda i, t: (t, i, 0)),
        out_shape=jax.ShapeDtypeStruct((N_SHARD, Dh, Wc), BF16),
        compiler_params=_cparams(("parallel", "parallel")),
    )(dw_mine, ri)


def _sum_pair_w3(cidx, dw3, r3):
    _, _, _, Do, D = dw3.shape

    def body(c_ref, a_ref, b_ref, o_ref):
        o_ref[...] = (a_ref[...] + b_ref[...]).astype(BF16)

    return pl.pallas_call(
        body, name="sum_pair_w3",
        grid_spec=pltpu.PrefetchScalarGridSpec(
            num_scalar_prefetch=1, grid=(3,),
            in_specs=[pl.BlockSpec((None, N_SHARD, None, Do, D), lambda a, c: (a, 0, c[0], 0, 0)),
                      pl.BlockSpec((None, N_SHARD, Do, D), lambda a, c: (a, 0, 0, 0))],
            out_specs=pl.BlockSpec((None, N_SHARD, Do, D), lambda a, c: (a, 0, 0, 0))),
        out_shape=jax.ShapeDtypeStruct((3, N_SHARD, Do, D), BF16),
        compiler_params=_cparams(("parallel",)),
    )(cidx, dw3, r3)


def _chips_exchange_in(cs_in):
    _, Dh, Wc = cs_in.shape

    def build(ins, outs, send, recv):
        pos = _position()
        return [_remote(ins[0].at[_shard_of(_peer(pos, k))], outs[0].at[j], send, recv, j, _peer(pos, k))
                for j, k in enumerate(CHIP_FLIPS)]

    return _Exchange((cs_in,), (jax.ShapeDtypeStruct((3, Dh, Wc), BF16),), 3, build)


def _chips_exchange_w3(cs_3):
    _, _, Do, D = cs_3.shape

    def build(ins, outs, send, recv):
        pos = _position()
        return [_remote(ins[0].at[:, _shard_of(_peer(pos, k))], outs[0].at[j], send, recv, j, _peer(pos, k))
                for j, k in enumerate(CHIP_FLIPS)]

    return _Exchange((cs_3,), (jax.ShapeDtypeStruct((3, 3, Do, D), BF16),), 3, build)


def _sum_chips_in(csidx, cs_in, rb_in):
    _, Dh, Wc = cs_in.shape
    tr = min(256, Dh)

    def body(s_ref, a_ref, b_ref, o_ref):
        acc = a_ref[...].astype(F32)
        for j in range(3):
            acc = acc + b_ref[j].astype(F32)
        o_ref[...] = acc

    return pl.pallas_call(
        body, name="sum_chips_in",
        grid_spec=pltpu.PrefetchScalarGridSpec(
            num_scalar_prefetch=1, grid=(Dh // tr,),
            in_specs=[pl.BlockSpec((None, tr, Wc), lambda i, s: (s[1], i, 0)),
                      pl.BlockSpec((3, tr, Wc), lambda i, s: (0, i, 0))],
            out_specs=pl.BlockSpec((None, tr, Wc), lambda i, s: (s[0], i, 0))),
        out_shape=jax.ShapeDtypeStruct((2, Dh, Wc), F32),
        compiler_params=_cparams(("parallel",)),
    )(csidx, cs_in, rb_in)


def _sum_chips_w3(csidx, cs_3, rb_3):
    _, _, Do, D = cs_3.shape

    def body(s_ref, a_ref, b_ref, o_ref):
        acc = a_ref[...].astype(F32)
        for j in range(3):
            acc = acc + b_ref[j].astype(F32)
        o_ref[...] = acc

    return pl.pallas_call(
        body, name="sum_chips_w3",
        grid_spec=pltpu.PrefetchScalarGridSpec(
            num_scalar_prefetch=1, grid=(3,),
            in_specs=[pl.BlockSpec((None, None, Do, D), lambda a, s: (a, s[1], 0, 0)),
                      pl.BlockSpec((3, None, Do, D), lambda a, s: (0, a, 0, 0))],
            out_specs=pl.BlockSpec((None, None, Do, D), lambda a, s: (a, s[0], 0, 0))),
        out_shape=jax.ShapeDtypeStruct((3, 2, Do, D), F32),
        compiler_params=_cparams(("parallel",)),
    )(csidx, cs_3, rb_3)


def _adam_math(w, g, m, v):
    m = ADAM_B1 * m + (1.0 - ADAM_B1) * g
    v = ADAM_B2 * v + (1.0 - ADAM_B2) * (g * g)
    m_hat = m / (1.0 - ADAM_B1 ** ADAM_STEP)
    v_hat = v / (1.0 - ADAM_B2 ** ADAM_STEP)
    delta = -ADAM_LR * (m_hat / (jnp.sqrt(v_hat) + ADAM_EPS) + ADAM_WD * w)
    return delta, m, v


def _adamw(w, g, m, v, name):
    R, C = w.shape
    tr = min(128, R)

    def body(w_ref, g_ref, m_ref, v_ref, d_ref, nm_ref, nv_ref):
        d_ref[...], nm_ref[...], nv_ref[...] = _adam_math(w_ref[...], g_ref[...], m_ref[...], v_ref[...])

    blk = pl.BlockSpec((tr, C), lambda i: (i, 0))
    return pl.pallas_call(
        body, name=name, grid=(R // tr,), in_specs=[blk] * 4, out_specs=[blk] * 3,
        out_shape=[jax.ShapeDtypeStruct((R, C), F32)] * 3,
        compiler_params=_cparams(("parallel",), VMEM_LIMIT),
    )(w, g, m, v)


def _adamw3(ws, g3, ms, vs):
    R, C = ws[0].shape

    def body(*refs):
        w_refs, m_refs, v_refs = refs[0:3], refs[3:6], refs[6:9]
        g_ref, outs = refs[9], refs[10:]
        for a in range(3):
            @pl.when(pl.program_id(0) == a)
            def _(a=a):
                res = _adam_math(w_refs[a][...], g_ref[...], m_refs[a][...], v_refs[a][...])
                for q in range(3):
                    outs[3 * a + q][...] = res[q]

    full = pl.BlockSpec((R, C), lambda a: (0, 0))
    res = pl.pallas_call(
        body, name="adamw_w3", grid=(3,),
        in_specs=[full] * 9 + [pl.BlockSpec((None, R, C), lambda a: (a, 0, 0))], out_specs=[full] * 9,
        out_shape=[jax.ShapeDtypeStruct((R, C), F32)] * 9,
        compiler_params=_cparams(("arbitrary",), VMEM_LIMIT),
    )(*ws, *ms, *vs, g3)
    return res[0:3], res[3:6], res[6:9]


SMALL_ROWS = ("c_ctx", "norm_w", "conv_b", "gn_w", "final_norm_w")


def _bwd_small(stats, ada_w, Dq, gh_in, gh_3):
    D = stats[0].shape[1]
    Wm = ada_w.shape[1]

    def body(stx, stm, stc, stv, stl, stlc, aw_ref, gi_in, g3_in, tot_ref, dm_sh, gcw, da_ref, loss_ref, gi_ref, g3_ref,
             vec_ref, vbuf, dm, amine, abuf, s_v, r_v, s_a, r_a, s_g, r_g):
        pos = _position()
        me, s = _dev_id(pos), _shard_of(pos)
        c, sib = pos[2], _peer(pos, 1)
        halves = [_remote(gi_in.at[c], gi_ref.at[c], s_g, r_g, 0, sib),
                  _remote(g3_in.at[:, c], g3_ref.at[:, c], s_g, r_g, 1, sib)]
        for cp in halves:
            cp.start()
        vec_ref[...] = jnp.zeros_like(vec_ref)
        vec_ref[0:2, :] = stx[0:2, :]
        vec_ref[2:3, :] = stm[1:2, :]
        vec_ref[3:5, :] = stc[0:2, :]
        vec_ref[5:6, :] = stx[2:3, :] + stc[2:3, :]
        vec_ref[6:7, :] = stv[3:4, :]
        vec_ref[7:8, :] = stm[3:4, :]
        vec_ref[8:9, :] = stm[0:1, :]
        vec_ref[9:12, :] = stv[0:3, :]
        vec_ref[12:14, 0:128] = stl[0:2, :] + stlc[0:2, :]
        vec_ref[14:15, :] = stm[2:3, :]
        vbuf[me] = vec_ref[...]
        sends = [_remote(vec_ref, vbuf.at[me], s_v, r_v, k - 1, _peer(pos, k)) for k in range(1, 8)]
        for cp in sends:
            cp.start()
        for k in range(1, 8):
            _remote(vec_ref, vbuf.at[_dev_id(_peer(pos, k))], s_v, r_v, k - 1, _peer(pos, k)).wait_recv()
        tot = vbuf[0]
        for d in range(1, N_DEV):
            tot = tot + vbuf[d]
        loss_ref[...] = jnp.zeros((8, 128), F32) + (0.5 / D) * _sum_all(tot[14:15, :])
        dm[...] = jnp.zeros_like(dm)
        for d in range(N_DEV):
            for r in range(3):
                dm[d:d + 1, r * D:(r + 1) * D] = vbuf[d, r:r + 1, :]
        dm[8:9, 0:D] = tot[3:4, :]
        dm[8:9, D:2 * D] = tot[4:5, :]
        for t in range(N_SHARD):
            @pl.when(s == t)
            def _(t=t):
                dm_sh[...] = dm[:, t * Wm:(t + 1) * Wm]
                gcw[...] = tot[9:12, t * Dq:(t + 1) * Dq]
        tot_ref[...] = tot
        part = lax.dot_general(dm_sh[8:16, :], aw_ref[...], (((1,), (1,)), ((), ())),
                               precision=lax.Precision.HIGHEST, preferred_element_type=F32)
        amine[...] = part
        abuf[s] = part
        asend = [_remote(amine, abuf.at[s], s_a, r_a, j, _peer(pos, k)) for j, k in enumerate(CHIP_FLIPS)]
        for cp in asend:
            cp.start()
        for j, k in enumerate(CHIP_FLIPS):
            _remote(amine, abuf.at[_shard_of(_peer(pos, k))], s_a, r_a, j, _peer(pos, k)).wait_recv()
        da = abuf[0]
        for t in range(1, N_SHARD):
            da = da + abuf[t]
        da_ref[...] = da
        _remote(gi_in.at[1 - c], gi_ref.at[1 - c], s_g, r_g, 0, sib).wait_recv()
        _remote(g3_in.at[:, 1 - c], g3_ref.at[:, 1 - c], s_g, r_g, 1, sib).wait_recv()
        for cp in sends + asend + halves:
            cp.wait_send()

    row = lambda *shape: jax.ShapeDtypeStruct(shape, F32)
    return pl.pallas_call(
        body, name="bwd_small",
        in_specs=[VMEM_FULL] * 7 + [ANY, ANY], out_specs=[VMEM_FULL] * 5 + [ANY, ANY],
        input_output_aliases={7: 5, 8: 6},
        out_shape=[row(16, D), row(16, Wm), row(3, Dq), row(8, D), row(8, 128), row(*gh_in.shape), row(*gh_3.shape)],
        scratch_shapes=[pltpu.VMEM((16, D), F32), pltpu.VMEM((N_DEV, 16, D), F32), pltpu.VMEM((16, 3 * D), F32),
                        pltpu.VMEM((8, D), F32), pltpu.VMEM((N_SHARD, 8, D), F32),
                        pltpu.SemaphoreType.DMA((7,)), pltpu.SemaphoreType.DMA((7,)),
                        pltpu.SemaphoreType.DMA((3,)), pltpu.SemaphoreType.DMA((3,)),
                        pltpu.SemaphoreType.DMA((2,)), pltpu.SemaphoreType.DMA((2,))],
        compiler_params=_cparams(None, VMEM_LIMIT),
    )(*stats, ada_w, gh_in, gh_3)


def _small_update(tot, dm_sh, gcw, da, act, p_row, p_ab, p_cw, p_dl):
    D = act.shape[1]
    Wm = dm_sh.shape[1]
    Dq = gcw.shape[1]

    def body(tot_ref, dm_ref, gcw_ref, da_ref, act_ref, prow, pab, pcw, pdl, gaw_ref, *outs):
        o_q = [outs[8 * q:8 * (q + 1)] for q in range(4)]
        tot = tot_ref[...]
        gaw_ref[...] = lax.dot_general(act_ref[...], dm_ref[...], (((0,), (0,)), ((), ())),
                                       precision=lax.Precision.HIGHEST, preferred_element_type=F32)
        cc = prow[0, 0:1, :]
        sg = _sigmoid(cc)
        g_cctx = da_ref[0:1, :] * (sg * (1.0 + cc * (1.0 - sg)))

        def place_all(o, val):
            o[...] = val

        def emit(k, w, g, m, v, place=place_all):
            for q, val in enumerate((g,) + _adam_math(w, g, m, v)):
                place(o_q[q][k], val)

        g_rows = [g_cctx, tot[5:6, :], tot[6:7, :], tot[7:8, :], tot[8:9, :]]
        for k, g in enumerate(g_rows):
            emit(k, prow[0, k:k + 1, :], g, prow[1, k:k + 1, :], prow[2, k:k + 1, :])

        def place_ab(o, val):
            for r in range(3):
                o[0:1, r * D:(r + 1) * D] = val[r:r + 1, :]

        g_ab = jnp.concatenate([tot[0:1, :] + tot[3:4, :], tot[1:2, :] + tot[4:5, :], tot[2:3, :]], axis=0)
        emit(5, pab[0], g_ab, pab[1], pab[2], place_ab)
        emit(6, pcw[0], gcw_ref[...], pcw[1], pcw[2])
        g_dl = jnp.concatenate([tot[12:14, 0:128] * _sigmoid(-pdl[0, 0:2, :]), jnp.zeros((6, 128), F32)], axis=0)
        emit(7, pdl[0], g_dl, pdl[1], pdl[2])

    row = lambda *shape: jax.ShapeDtypeStruct(shape, F32)
    per_q = [row(1, D)] * 5 + [row(1, 3 * D), row(3, Dq), row(8, 128)]
    res = pl.pallas_call(
        body, name="small_update",
        in_specs=[VMEM_FULL] * 9, out_specs=[VMEM_FULL] * 33,
        out_shape=[row(D, Wm)] + per_q * 4,
        compiler_params=_cparams(None, VMEM_LIMIT),
    )(tot, dm_sh, gcw, da, act, p_row, p_ab, p_cw, p_dl)
    return res[0], [res[1 + 8 * q:1 + 8 * (q + 1)] for q in range(4)]


def _pad_rows(a, rows=8):
    return jnp.pad(a, ((0, rows - a.shape[0]), (0, 0)))


def kernel(x, c, ctx, c_ctx, norm_w, ada_w, ada_b, w_in, conv_w, conv_b, decay_logit, gn_w, w_a, w_b, w_out, final_norm_w, loss_target, m_c_ctx, m_norm_w, m_ada_w, m_ada_b, m_w_in, m_conv_w, m_conv_b, m_decay_logit, m_gn_w, m_w_a, m_w_b, m_w_out, m_final_norm_w, v_c_ctx, v_norm_w, v_ada_w, v_ada_b, v_w_in, v_conv_w, v_conv_b, v_decay_logit, v_gn_w, v_w_a, v_w_b, v_w_out, v_final_norm_w):
    L, D = x.shape[1], x.shape[2]
    H = D // DV
    Wc = w_in.shape[2]
    Do = D // 8
    pos = _position()
    me = _dev_id(pos)
    cidx = jnp.reshape(pos[2], (1,)).astype(jnp.int32)
    sidx = jnp.reshape(_shard_of(pos), (1,)).astype(jnp.int32)

    act, mod, conv_w8 = _fwd_small(_pad_rows(c), _pad_rows(c_ctx[None]), ada_w[0], ada_b, _pad_rows(conv_w[0]))
    mod_x = lax.dynamic_slice_in_dim(mod, me, 1, axis=0).reshape(3, D)
    mod_c = mod[8].reshape(3, D)
    lg = jax.nn.log_sigmoid(decay_logit[0])

    w3_s = tuple(w[0].reshape(2, Do, D) for w in (w_a, w_b, w_out))

    def project(xm, c2, s2):
        p, qr, kr, w_in_full, w3_full = _ag_in_proj(xm, w_in[0], w3_s, c2, s2)
        return p, qr, kr, w_in_full, w3_full.reshape(3, D, D)

    csidx = jnp.concatenate([cidx, sidx])
    groups, dret_c, xmt, cmt, dx1, sc_x, w_in_full, gh_3, sts = _local_step(
        x[0], ctx[0], loss_target[0], mod_x, mod_c, norm_w, conv_w8, conv_b, lg, gn_w, final_norm_w[None],
        project, csidx)
    st_mid, st_conv, st_lg, st_lgc, st_c = sts

    dw_mine, ra_in = _dw_in(xmt, groups, cmt, dret_c, D, True)
    cs_in = _sum_pair_in(dw_mine, ra_in)
    grad_x, st_x, rb_in = _dxm(groups, 0, w_in_full, x[0], norm_w, sc_x, dx1, "dxm_x", _chips_exchange_in(cs_in))
    gh_in = _sum_chips_in(csidx, cs_in, rb_in)

    zeros3 = jnp.zeros((3, D), F32)
    p_row = jnp.concatenate(
        [r for t in ((c_ctx[None], norm_w, conv_b, gn_w, final_norm_w[None], zeros3),
                     (m_c_ctx[None], m_norm_w, m_conv_b, m_gn_w, m_final_norm_w[None], zeros3),
                     (v_c_ctx[None], v_norm_w, v_conv_b, v_gn_w, v_final_norm_w[None], zeros3)) for r in t],
        axis=0).reshape(3, 8, D)
    p_ab = jnp.concatenate([ada_b, m_ada_b, v_ada_b], axis=0).reshape(3, 3, D)
    p_cw = jnp.concatenate([conv_w, m_conv_w, v_conv_w], axis=0)
    p_dl = jnp.pad(jnp.concatenate([decay_logit, m_decay_logit, v_decay_logit], axis=0), ((0, 0), (0, 6), (0, 128 - H)))
    tot, dm_sh, gcw, da, loss_t, g_in, g_3 = _bwd_small((st_x, st_mid, st_c, st_conv, st_lg, st_lgc), ada_w[0],
                                                        conv_w.shape[2], gh_in, gh_3)
    g_w_in = g_in.reshape(D, Wc)
    g_3 = g_3.reshape(3, D // 4, D)
    g_ada_w, small = _small_update(tot, dm_sh, gcw, da, act, p_row, p_ab, p_cw, p_dl)

    upd_in = _adamw(w_in[0], g_w_in, m_w_in[0], v_w_in[0], "adamw_w_in")
    upd_ada = _adamw(ada_w[0], g_ada_w, m_ada_w[0], v_ada_w[0], "adamw_ada_w")
    upd_a, upd_b, upd_o = _adamw3((w_a[0], w_b[0], w_out[0]), g_3, (m_w_a[0], m_w_b[0], m_w_out[0]),
                                  (v_w_a[0], v_w_b[0], v_w_out[0]))

    def leaves(q):
        big = lambda g, upd: (g if q == 0 else upd[q - 1])[None]
        r_cctx, r_norm, r_convb, r_gn, r_fnorm, r_ab, r_cw, r_dl = small[q]
        return [r_cctx.reshape(D), r_norm, big(g_ada_w, upd_ada), r_ab, big(g_w_in, upd_in),
                r_cw[None], r_convb, r_dl[0:2, 0:H][None], r_gn,
                big(g_3[0], upd_a), big(g_3[1], upd_b), big(g_3[2], upd_o), r_fnorm.reshape(D)]

    loss = loss_t[0, 0]
    return (loss, grad_x[None], *leaves(0), *leaves(1), *leaves(2), *leaves(3))
```

```python
from typing import Callable, NamedTuple

import jax
import jax.numpy as jnp
from jax import lax
from jax.experimental import pallas as pl
from jax.experimental.pallas import tpu as pltpu

F32 = jnp.float32
BF16 = jnp.bfloat16
MESH = pl.DeviceIdType.MESH

CHUNK = 128
RET_CPB = 4
DV = 128
DK = 64
GRID_W = 64
ROPE_BASE = 10000.0
EPS = 1e-6
K_SCALE = DK ** -0.5
N_SHARD = 4
N_DEV = 8

ADAM_LR = 0.001
ADAM_B1 = 0.9
ADAM_B2 = 0.999
ADAM_EPS = 1e-08
ADAM_WD = 0.01
ADAM_STEP = 10

VMEM_LIMIT = 56 * 1024 * 1024


def _cparams(sem=None, vmem=None):
    kw = {}
    if sem is not None:
        kw["dimension_semantics"] = sem
    if vmem is not None:
        kw["vmem_limit_bytes"] = vmem
    return pltpu.CompilerParams(**kw)


def _dot(a, b):
    return jnp.dot(a, b, preferred_element_type=F32)


def _dot_nt(a, b):
    return lax.dot_general(a, b, (((1,), (1,)), ((), ())), preferred_element_type=F32)


def _dot_tn(a, b):
    return lax.dot_general(a, b, (((0,), (0,)), ((), ())), preferred_element_type=F32)


def _sigmoid(x):
    return 1.0 / (1.0 + jnp.exp(-x))


def _sum_all(x):
    return jnp.sum(jnp.sum(x, axis=1, keepdims=True), axis=0, keepdims=True)


def _swap_halves(t):
    n = t.shape[1]
    lane = lax.broadcasted_iota(jnp.int32, t.shape, 1)
    low = (lane & 32) == 0
    return jnp.where(low, pltpu.roll(t, n - 32, 1), pltpu.roll(t, 32, 1))


def _vec_spec(d):
    return pl.BlockSpec((1, d), lambda *a: (0, 0))


def _norm_mod(x, nw, sc, sh, name):
    L, D = x.shape
    tl = min(256, L)

    def body(x_ref, nw_ref, sc_ref, sh_ref, xm_ref, xmt_ref):
        xv = x_ref[...]
        r = lax.rsqrt(jnp.mean(xv * xv, axis=-1, keepdims=True) + EPS)
        xm = (xv * r * nw_ref[...]) * (1.0 + sc_ref[...]) + sh_ref[...]
        xm_b = xm.astype(BF16)
        xm_ref[...] = xm_b
        xmt_ref[...] = xm_b.T

    return pl.pallas_call(
        body, name=name, grid=(L // tl,),
        in_specs=[pl.BlockSpec((tl, D), lambda i: (i, 0)), _vec_spec(D), _vec_spec(D), _vec_spec(D)],
        out_specs=[pl.BlockSpec((tl, D), lambda i: (i, 0)), pl.BlockSpec((D, tl), lambda i: (0, i))],
        out_shape=[jax.ShapeDtypeStruct((L, D), BF16), jax.ShapeDtypeStruct((D, L), BF16)],
        compiler_params=_cparams(("parallel",)),
    )(x, nw, sc, sh)


QK_BLOCK, V_BLOCK = 4, 5


def _in_proj(xm, w, name, first=0, count=None):
    M, D = xm.shape
    count = w.shape[1] // D if count is None else count
    tm = min(1024, M)

    def body(a_ref, b_ref, o_ref, qk_ref):
        acc = _dot(a_ref[...], b_ref[...])
        o_ref[...] = acc.astype(o_ref.dtype)

        @pl.when(pl.program_id(1) == QK_BLOCK - first)
        def _():
            qk_ref[...] = acc

    return pl.pallas_call(
        body, name=name, grid=(M // tm, count),
        in_specs=[pl.BlockSpec((tm, D), lambda i, j: (i, 0)), pl.BlockSpec((D, D), lambda i, j: (0, first + j))],
        out_specs=[pl.BlockSpec((tm, D), lambda i, j: (i, j)), pl.BlockSpec((tm, D), lambda i, j: (i, 0))],
        out_shape=[jax.ShapeDtypeStruct((M, count * D), BF16), jax.ShapeDtypeStruct((M, D), F32)],
        compiler_params=_cparams(("parallel", "arbitrary")),
    )(xm, w)


def _halo_specs(tl, L, D, col):
    hb = tl // 16
    last = L // 16 - 1
    prev = pl.BlockSpec((16, D), lambda i: (jnp.maximum(i * hb - 1, 0), col))
    nxt = pl.BlockSpec((16, D), lambda i: (jnp.minimum((i + 1) * hb, last), col))
    return prev, nxt


def _shift_rows(u, above, below):
    tl = u.shape[0]
    row = lax.broadcasted_iota(jnp.int32, u.shape, 0)
    dn = jnp.where(row == 0, above, pltpu.roll(u, 1, 0))
    up = jnp.where(row == tl - 1, below, pltpu.roll(u, tl - 1, 0))
    return dn, up


def _rope_tables(L):
    pos = jnp.arange(L)
    row = (pos // GRID_W).astype(F32)
    col = (pos % GRID_W).astype(F32)
    nf = DK // 4
    inv = ROPE_BASE ** (-jnp.arange(nf, dtype=F32) / nf)
    ang = jnp.concatenate([row[:, None] * inv, col[:, None] * inv], axis=-1)
    cos, sin = jnp.cos(ang), jnp.sin(ang)
    return jnp.concatenate([cos, cos, cos, cos], axis=-1), jnp.concatenate([-sin, sin, -sin, sin], axis=-1)


def _smem_spec():
    return pl.BlockSpec(memory_space=pltpu.SMEM)


def _pair_select(e0, e1):
    row = lax.broadcasted_iota(jnp.int32, e0.shape, 0)
    return jnp.where(row < DK, e0, e1)


def _head_lane_mask(shape, e):
    lane = lax.broadcasted_iota(jnp.int32, shape, 1)
    return (lane < DK) if e == 0 else (lane >= DK)


def _ctx_states(pc, pqk_c, lg, D):
    Lc = pc.shape[0]
    H = D // DV

    def body(lg_ref, k_ref, v_ref, s_ref):
        m = lax.broadcasted_iota(jnp.int32, (Lc, DV), 0).astype(F32)
        for pr in range(H // 2):
            k2 = k_ref[:, pr * 128:(pr + 1) * 128].astype(F32) * K_SCALE
            res = [[None, None], [None, None]]
            for e in range(2):
                h = 2 * pr + e
                v = v_ref[:, h * DV:(h + 1) * DV]
                dec_f = jnp.exp(lg_ref[0, h] * (Lc - 1.0 - m))
                dec_b = jnp.exp(lg_ref[1, h] * m)
                res[0][e] = _dot_tn((k2 * dec_f).astype(BF16), v)
                res[1][e] = _dot_tn((k2 * dec_b).astype(BF16), v)
            s_ref[0, pr] = _pair_select(res[0][0], res[0][1])
            s_ref[1, pr] = _pair_select(res[1][0], res[1][1])

    return pl.pallas_call(
        body, name="ctx_states", grid=(1,),
        in_specs=[_smem_spec(), pl.BlockSpec((Lc, D // 2), lambda i: (0, 1)), pl.BlockSpec((Lc, D), lambda i: (0, 1))],
        out_specs=pl.BlockSpec((2, H // 2, 128, 128), lambda i: (0, 0, 0, 0)),
        out_shape=jax.ShapeDtypeStruct((2, H // 2, 128, 128), F32),
    )(lg, pqk_c, pc)


T_M, T_MT = 0, 1
T_MF1, T_MB1 = 2, 3
T_QF, T_QB = 4, 5
T_KF, T_KB = 6, 7


def _decay_tables(lg, H):
    def body(lg_ref, t_ref):
        h = pl.program_id(0)
        lgf, lgb = lg_ref[0, h], lg_ref[1, h]
        i = lax.broadcasted_iota(jnp.int32, (CHUNK, CHUNK), 0).astype(F32)
        j = lax.broadcasted_iota(jnp.int32, (CHUNK, CHUNK), 1).astype(F32)
        d = i - j
        mf = jnp.where(d > 0, jnp.exp(lgf * jnp.maximum(d, 0.0)), 0.0)
        mb = jnp.where(d < 0, jnp.exp(lgb * jnp.maximum(-d, 0.0)), 0.0)
        mf_t = jnp.where(d < 0, jnp.exp(lgf * jnp.maximum(-d, 0.0)), 0.0)
        mb_t = jnp.where(d > 0, jnp.exp(lgb * jnp.maximum(d, 0.0)), 0.0)
        diag = jnp.where(d == 0, 2.0, 0.0)
        t_ref[0, T_M] = mf + mb + diag
        t_ref[0, T_MT] = mf_t + mb_t + diag
        t_ref[0, T_MF1] = mf * d
        t_ref[0, T_MB1] = mb * (-d)
        t_ref[0, T_QF] = jnp.exp(lgf * (i + 1.0))
        t_ref[0, T_QB] = jnp.exp(lgb * (CHUNK - i))
        t_ref[0, T_KF] = jnp.exp(lgf * (CHUNK - 1.0 - i))
        t_ref[0, T_KB] = jnp.exp(lgb * i)

    return pl.pallas_call(
        body, name="decay_tables", grid=(H,), in_specs=[_smem_spec()],
        out_specs=pl.BlockSpec((1, 8, CHUNK, CHUNK), lambda h: (h, 0, 0, 0)),
        out_shape=jax.ShapeDtypeStruct((H, 8, CHUNK, CHUNK), F32),
    )(lg)


def _tab_spec(H):
    return pl.BlockSpec((H, 8, CHUNK, CHUNK), lambda n: (0, 0, 0, 0))


def _chunk_decay(tab_ref, h):
    return tab_ref[h, T_QF, CHUNK - 1:CHUNK, :], tab_ref[h, T_QB, 0:1, :]


def _ret_states(kr, p, s0, tab, D):
    L = kr.shape[0]
    H = D // DV
    N = L // CHUNK
    HP = H // 2

    def body(tab_ref, kf_ref, kb_ref, vf_ref, vb_ref, s0_ref, sf_out, sb_out, sf, sb):
        n = pl.program_id(0)

        @pl.when(n == 0)
        def _():
            sf[...] = s0_ref[0]
            sb[...] = s0_ref[1]

        for cc in range(RET_CPB):
            cf_, cb_ = cc, RET_CPB - 1 - cc
            rf, rb = slice(cf_ * CHUNK, (cf_ + 1) * CHUNK), slice(cb_ * CHUNK, (cb_ + 1) * CHUNK)
            sf_out[cf_] = sf[...]
            sb_out[cb_] = sb[...]
            for pr in range(HP):
                kf2 = kf_ref[rf, pr * 128:(pr + 1) * 128].astype(F32)
                kb2 = kb_ref[rb, pr * 128:(pr + 1) * 128].astype(F32)
                inc_f, inc_b, gf, gb = [], [], [], []
                for e in range(2):
                    h = 2 * pr + e
                    inc_f.append(_dot_tn((kf2 * tab_ref[h, T_KF]).astype(BF16), vf_ref[rf, h * DV:(h + 1) * DV]))
                    inc_b.append(_dot_tn((kb2 * tab_ref[h, T_KB]).astype(BF16), vb_ref[rb, h * DV:(h + 1) * DV]))
                    cf, cb = _chunk_decay(tab_ref, h)
                    gf.append(jnp.broadcast_to(cf, (128, 128)))
                    gb.append(jnp.broadcast_to(cb, (128, 128)))
                sf[pr] = _pair_select(gf[0], gf[1]) * sf[pr] + _pair_select(inc_f[0], inc_f[1])
                sb[pr] = _pair_select(gb[0], gb[1]) * sb[pr] + _pair_select(inc_b[0], inc_b[1])

    st = jax.ShapeDtypeStruct((N, HP, 128, 128), F32)
    R = RET_CPB * CHUNK
    NB = N // RET_CPB
    return _riding_call(
        body, None, NB, name="ret_states", args=(tab, kr, kr, p, p, s0),
        in_specs=[_tab_spec(H),
                  pl.BlockSpec((R, D // 2), lambda n: (n, 0)),
                  pl.BlockSpec((R, D // 2), lambda n: (NB - 1 - n, 0)),
                  pl.BlockSpec((R, D), lambda n: (n, 5)),
                  pl.BlockSpec((R, D), lambda n: (NB - 1 - n, 5)),
                  pl.BlockSpec((2, HP, 128, 128), lambda n: (0, 0, 0, 0))],
        out_specs=[pl.BlockSpec((RET_CPB, HP, 128, 128), lambda n: (n, 0, 0, 0)),
                   pl.BlockSpec((RET_CPB, HP, 128, 128), lambda n: (NB - 1 - n, 0, 0, 0))],
        out_shape=[st, st],
        scratch=[pltpu.VMEM((HP, 128, 128), F32), pltpu.VMEM((HP, 128, 128), F32)],
        cparams=_cparams(("arbitrary",)))


def _ret_out(qr, kr, p, sf_prev, sb_prev, gn_w, tab, D):
    L = qr.shape[0]
    H = D // DV
    N = L // CHUNK
    HP = H // 2

    def body(tab_ref, q_ref, k_ref, v_ref, zb_ref, sf_ref, sb_ref, gn_ref, o_ref, yb_ref):
        def chunk(cc, carry):
            rows = pl.ds(pl.multiple_of(cc * CHUNK, CHUNK), CHUNK)
            for pr in range(HP):
                q2 = q_ref[rows, pr * 128:(pr + 1) * 128]
                k2 = k_ref[rows, pr * 128:(pr + 1) * 128]
                sfp = sf_ref[cc, pr].astype(BF16)
                sbp = sb_ref[cc, pr].astype(BF16)
                for e in range(2):
                    h = 2 * pr + e
                    sl = slice(h * DV, (h + 1) * DV)
                    qm = jnp.where(_head_lane_mask(q2.shape, e), q2, jnp.zeros_like(q2))
                    a = (_dot_nt(qm, k2) * tab_ref[h, T_M]).astype(BF16)
                    qf = qm.astype(F32)
                    o = _dot(a, v_ref[rows, sl])
                    o += _dot((qf * tab_ref[h, T_QF]).astype(BF16), sfp)
                    o += _dot((qf * tab_ref[h, T_QB]).astype(BF16), sbp)
                    o_ref[rows, sl] = o
                    mu = jnp.mean(o, axis=-1, keepdims=True)
                    oc = o - mu
                    rstd = lax.rsqrt(jnp.mean(oc * oc, axis=-1, keepdims=True) + EPS)
                    zb = zb_ref[rows, sl].astype(F32)
                    yb_ref[rows, sl] = (zb * _sigmoid(zb) * (oc * rstd * gn_ref[:, sl])).astype(BF16)
            return carry

        lax.fori_loop(0, RET_CPB, chunk, 0)

    R = RET_CPB * CHUNK
    return _riding_call(
        body, None, N // RET_CPB, name="ret_out", args=(tab, qr, kr, p, p, sf_prev, sb_prev, gn_w),
        in_specs=[_tab_spec(H),
                  pl.BlockSpec((R, D // 2), lambda n: (n, 0)),
                  pl.BlockSpec((R, D // 2), lambda n: (n, 0)),
                  pl.BlockSpec((R, D), lambda n: (n, 5)),
                  pl.BlockSpec((R, D), lambda n: (n, 6)),
                  pl.BlockSpec((RET_CPB, HP, 128, 128), lambda n: (n, 0, 0, 0)),
                  pl.BlockSpec((RET_CPB, HP, 128, 128), lambda n: (n, 0, 0, 0)),
                  _vec_spec(D)],
        out_specs=[pl.BlockSpec((R, D), lambda n: (n, 0)), pl.BlockSpec((R, D), lambda n: (n, 0))],
        out_shape=[jax.ShapeDtypeStruct((L, D), F32), jax.ShapeDtypeStruct((L, D), BF16)],
        cparams=_cparams(("arbitrary",)))


def _mid(p, yb, o, x, tgt, w3, g, fw, conv_w, conv_b, gn_w, D):
    L = x.shape[0]
    H = D // DV
    tm = min(256, L)
    nt = L // tm

    def body(h_ref, bg_ref, cg_ref, za_ref, hp_ref, hn_ref, cp_ref, cn_ref, yb_ref, ga_ref, gb_ref, zb_ref, o_ref,
             x_ref, t_ref, w_hbm, g_ref, fw_ref, cw_ref, cb_ref, gn_ref,
             dx1_ref, dya_ref, do_ref, dzb_ref, dgab_ref, dw_hbm, st_ref, w_vm, dw_acc, sem):
        i = pl.program_id(0)

        @pl.when(i == 0)
        def _():
            cp = pltpu.make_async_copy(w_hbm, w_vm, sem)
            cp.start()
            dw_acc[...] = jnp.zeros_like(dw_acc)
            st_ref[...] = jnp.zeros_like(st_ref)
            cp.wait()

        u = cg_ref[...].astype(F32) * h_ref[...].astype(F32)
        above = jnp.where(i == 0, 0.0, cp_ref[15:16, :].astype(F32) * hp_ref[15:16, :].astype(F32))
        below = jnp.where(i == nt - 1, 0.0, cn_ref[0:1, :].astype(F32) * hn_ref[0:1, :].astype(F32))
        dn, up = _shift_rows(u, above, below)
        co = cw_ref[0:1, :] * dn + cw_ref[1:2, :] * u + cw_ref[2:3, :] * up + cb_ref[...]
        za = za_ref[...].astype(F32)
        ya_b = (za * _sigmoid(za) * bg_ref[...].astype(F32) * co).astype(BF16)
        yb_b = yb_ref[...]
        y_a = _dot(ya_b, w_vm[0])
        y_b = _dot(yb_b, w_vm[1])
        sga = _sigmoid(ga_ref[...].astype(F32))
        sgb = _sigmoid(gb_ref[...].astype(F32))
        mix_b = (sga * y_a + sgb * y_b).astype(BF16)
        y_x = _dot(mix_b, w_vm[2])
        gvec, fwv = g_ref[...], fw_ref[...]
        x1 = x_ref[...] + gvec * y_x
        r1 = lax.rsqrt(jnp.mean(x1 * x1, axis=-1, keepdims=True) + EPS)
        xh = x1 * r1
        diff = xh * fwv - t_ref[...]
        dout = diff * (1.0 / D)
        dxh = dout * fwv
        dx1 = r1 * (dxh - xh * jnp.mean(dxh * xh, axis=-1, keepdims=True))
        dx1_ref[...] = dx1
        st_ref[0:1, :] += jnp.sum(dout * xh, axis=0, keepdims=True)
        st_ref[1:2, :] += jnp.sum(dx1 * y_x, axis=0, keepdims=True)
        st_ref[2:3, :] += jnp.sum(diff * diff, axis=0, keepdims=True)
        dyx_b = (dx1 * gvec).astype(BF16)
        dmix = _dot_nt(dyx_b, w_vm[2])
        dw_acc[2] += _dot_tn(mix_b, dyx_b)
        dya_b = (dmix * sga).astype(BF16)
        dyb_b = (dmix * sgb).astype(BF16)
        dgab_ref[:, 0:D] = (dmix * y_a * sga * (1.0 - sga)).astype(BF16)
        dgab_ref[:, D:2 * D] = (dmix * y_b * sgb * (1.0 - sgb)).astype(BF16)
        dya_ref[...] = _dot_nt(dya_b, w_vm[0])
        dyb = _dot_nt(dyb_b, w_vm[1])
        dw_acc[0] += _dot_tn(ya_b, dya_b)
        dw_acc[1] += _dot_tn(yb_b, dyb_b)

        for h in range(H):
            sl = slice(h * DV, (h + 1) * DV)
            ov = o_ref[:, sl]
            oc = ov - jnp.mean(ov, axis=-1, keepdims=True)
            rstd = lax.rsqrt(jnp.mean(oc * oc, axis=-1, keepdims=True) + EPS)
            rn = oc * rstd
            gw = gn_ref[:, sl]
            zb = zb_ref[:, sl].astype(F32)
            sz = _sigmoid(zb)
            dy = dyb[:, sl]
            dzb_ref[:, sl] = (dy * (rn * gw) * (sz * (1.0 + zb * (1.0 - sz)))).astype(BF16)
            dretn = dy * (zb * sz)
            st_ref[3:4, sl] += jnp.sum(dretn * rn, axis=0, keepdims=True)
            drn = dretn * gw
            do_ref[:, sl] = (rstd * (drn - jnp.mean(drn, axis=-1, keepdims=True)
                                     - rn * jnp.mean(drn * rn, axis=-1, keepdims=True))).astype(BF16)

        @pl.when(i == nt - 1)
        def _():
            out = pltpu.make_async_copy(dw_acc, dw_hbm, sem)
            out.start()
            out.wait()

    row = lambda col: pl.BlockSpec((tm, D), lambda i: (i, col))
    any_spec = pl.BlockSpec(memory_space=pl.ANY)
    f32o = jax.ShapeDtypeStruct((L, D), F32)
    bf16o = jax.ShapeDtypeStruct((L, D), BF16)
    hp, hn = _halo_specs(tm, L, D, 0)
    cp, cn = _halo_specs(tm, L, D, 2)
    return pl.pallas_call(
        body, name="mid", grid=(nt,),
        in_specs=[row(0), row(1), row(2), row(3), hp, hn, cp, cn, row(0), row(7), row(8), row(6), row(0),
                  row(0), row(0), any_spec, _vec_spec(D), _vec_spec(D),
                  pl.BlockSpec((8, D), lambda i: (0, 0)), _vec_spec(D), _vec_spec(D)],
        out_specs=[row(0), row(0), row(0), row(0), pl.BlockSpec((tm, 2 * D), lambda i: (i, 0)), any_spec,
                   pl.BlockSpec((8, D), lambda i: (0, 0))],
        out_shape=[f32o, f32o, bf16o, bf16o, jax.ShapeDtypeStruct((L, 2 * D), BF16),
                   jax.ShapeDtypeStruct((3, D, D), F32), jax.ShapeDtypeStruct((8, D), F32)],
        scratch_shapes=[pltpu.VMEM((3, D, D), BF16), pltpu.VMEM((3, D, D), F32), pltpu.SemaphoreType.DMA],
        compiler_params=_cparams(("arbitrary",), VMEM_LIMIT),
    )(p, p, p, p, p, p, p, p, yb, p, p, p, o, x, tgt, w3, g, fw, conv_w, conv_b, gn_w)


def _conv_bwd(dya, p, conv_w, conv_b, D, exchange=None):
    L = p.shape[0]
    tl = min(256, L)
    nt = L // tl

    def body(d_ref, h_ref, bg_ref, cg_ref, za_ref,
             dp_ref, dn_ref, hp_ref, hn_ref, bp_ref, bn_ref, cp_ref, cn_ref, zp_ref, zn_ref,
             w_ref, b_ref, dc_ref, st_ref):
        i = pl.program_id(0)

        @pl.when(i == 0)
        def _():
            st_ref[...] = jnp.zeros_like(st_ref)

        first, last = i == 0, i == nt - 1
        h = h_ref[...].astype(F32)
        cg = cg_ref[...].astype(F32)
        bg = bg_ref[...].astype(F32)
        za = za_ref[...].astype(F32)
        dy = d_ref[...].astype(F32)
        u = cg * h
        u_above = jnp.where(first, 0.0, cp_ref[15:16, :].astype(F32) * hp_ref[15:16, :].astype(F32))
        u_below = jnp.where(last, 0.0, cn_ref[0:1, :].astype(F32) * hn_ref[0:1, :].astype(F32))
        u_dn, u_up = _shift_rows(u, u_above, u_below)
        w0, w1, w2 = w_ref[0:1, :], w_ref[1:2, :], w_ref[2:3, :]
        co = w0 * u_dn + w1 * u + w2 * u_up + b_ref[...]
        sz = _sigmoid(za)
        silu = za * sz
        dc_ref[:, 3 * D:4 * D] = (dy * bg * co * (sz * (1.0 + za * (1.0 - sz)))).astype(BF16)
        dc_ref[:, D:2 * D] = (dy * silu * co).astype(BF16)
        dco = dy * silu * bg

        def edge(dr, zr, br, r):
            z = zr[r:r + 1, :].astype(F32)
            return dr[r:r + 1, :].astype(F32) * (z * _sigmoid(z)) * br[r:r + 1, :].astype(F32)

        dco_above = jnp.where(first, 0.0, edge(dp_ref, zp_ref, bp_ref, 15))
        dco_below = jnp.where(last, 0.0, edge(dn_ref, zn_ref, bn_ref, 0))
        dco_dn, dco_up = _shift_rows(dco, dco_above, dco_below)
        du = w0 * dco_up + w1 * dco + w2 * dco_dn
        dc_ref[:, 2 * D:3 * D] = (du * h).astype(BF16)
        dc_ref[:, 0:D] = (du * cg).astype(BF16)
        st_ref[0:1, :] += jnp.sum(dco * u_dn, axis=0, keepdims=True)
        st_ref[1:2, :] += jnp.sum(dco * u, axis=0, keepdims=True)
        st_ref[2:3, :] += jnp.sum(dco * u_up, axis=0, keepdims=True)
        st_ref[3:4, :] += jnp.sum(dco, axis=0, keepdims=True)

    main = lambda col: pl.BlockSpec((tl, D), lambda i: (i, col))
    halos = []
    for col in (0, 0, 1, 2, 3):
        halos.extend(_halo_specs(tl, L, D, col))
    return _riding_call(
        body, exchange, nt, name="conv_bwd",
        args=(dya, p, p, p, p, dya, dya, p, p, p, p, p, p, p, p, conv_w, conv_b),
        in_specs=[main(0), main(0), main(1), main(2), main(3)] + halos
                 + [pl.BlockSpec((8, D), lambda i: (0, 0)), _vec_spec(D)],
        out_specs=[pl.BlockSpec((tl, 4 * D), lambda i: (i, 0)), pl.BlockSpec((8, D), lambda i: (0, 0))],
        out_shape=[jax.ShapeDtypeStruct((L, 4 * D), BF16), jax.ShapeDtypeStruct((8, D), F32)],
        cparams=_cparams(("arbitrary",)))


def _ret_bwd_states(qr, do, tab, D):
    L = qr.shape[0]
    H = D // DV
    N = L // CHUNK
    HP = H // 2

    def body(tab_ref, qf_ref, qb_ref, dof_ref, dob_ref, dsf_out, dsb_out, ds0_out, dsf, dsb):
        n = pl.program_id(0)

        @pl.when(n == 0)
        def _():
            dsf[...] = jnp.zeros_like(dsf)
            dsb[...] = jnp.zeros_like(dsb)

        for cc in range(RET_CPB):
            cf_, cb_ = RET_CPB - 1 - cc, cc
            rf, rb = slice(cf_ * CHUNK, (cf_ + 1) * CHUNK), slice(cb_ * CHUNK, (cb_ + 1) * CHUNK)
            dsf_out[cf_] = dsf[...]
            dsb_out[cb_] = dsb[...]
            for pr in range(HP):
                qf2 = qf_ref[rf, pr * 128:(pr + 1) * 128].astype(F32)
                qb2 = qb_ref[rb, pr * 128:(pr + 1) * 128].astype(F32)
                inc_f, inc_b, gf, gb = [], [], [], []
                for e in range(2):
                    h = 2 * pr + e
                    inc_f.append(_dot_tn((qf2 * tab_ref[h, T_QF]).astype(BF16), dof_ref[rf, h * DV:(h + 1) * DV]))
                    inc_b.append(_dot_tn((qb2 * tab_ref[h, T_QB]).astype(BF16), dob_ref[rb, h * DV:(h + 1) * DV]))
                    cf, cb = _chunk_decay(tab_ref, h)
                    gf.append(jnp.broadcast_to(cf, (128, 128)))
                    gb.append(jnp.broadcast_to(cb, (128, 128)))
                dsf[pr] = _pair_select(gf[0], gf[1]) * dsf[pr] + _pair_select(inc_f[0], inc_f[1])
                dsb[pr] = _pair_select(gb[0], gb[1]) * dsb[pr] + _pair_select(inc_b[0], inc_b[1])

        @pl.when(n == NB - 1)
        def _():
            ds0_out[0] = dsf[...]
            ds0_out[1] = dsb[...]

    st = jax.ShapeDtypeStruct((N, HP, 128, 128), F32)
    R = RET_CPB * CHUNK
    NB = N // RET_CPB
    return pl.pallas_call(
        body, name="ret_bwd_states", grid=(NB,),
        in_specs=[_tab_spec(H),
                  pl.BlockSpec((R, D // 2), lambda n: (NB - 1 - n, 0)),
                  pl.BlockSpec((R, D // 2), lambda n: (n, 0)),
                  pl.BlockSpec((R, D), lambda n: (NB - 1 - n, 0)),
                  pl.BlockSpec((R, D), lambda n: (n, 0))],
        out_specs=[pl.BlockSpec((RET_CPB, HP, 128, 128), lambda n: (NB - 1 - n, 0, 0, 0)),
                   pl.BlockSpec((RET_CPB, HP, 128, 128), lambda n: (n, 0, 0, 0)),
                   pl.BlockSpec((2, HP, 128, 128), lambda n: (0, 0, 0, 0))],
        out_shape=[st, st, jax.ShapeDtypeStruct((2, HP, 128, 128), F32)],
        scratch_shapes=[pltpu.VMEM((HP, 128, 128), F32), pltpu.VMEM((HP, 128, 128), F32)],
        compiler_params=_cparams(("arbitrary",)),
    )(tab, qr, qr, do, do)


def _ret_bwd_main(qr, kr, p, do, sf_prev, sb_prev, dsf, dsb, c2, s2, tab, D, exchange=None):
    L = qr.shape[0]
    H = D // DV
    N = L // CHUNK
    HP = H // 2
    W = D // 2

    def body(tab_ref, q_ref, k_ref, v_ref, do_ref, sf_ref, sb_ref, dsf_ref, dsb_ref, c_ref, s_ref,
             dr_ref, st_ref, dl_acc):
        @pl.when(pl.program_id(0) == 0)
        def _():
            dl_acc[...] = jnp.zeros_like(dl_acc)

        i = lax.broadcasted_iota(jnp.int32, (CHUNK, 128), 0).astype(F32)
        rowid = lax.broadcasted_iota(jnp.int32, (128, 128), 0)

        def chunk(cc, carry):
            rows = pl.ds(pl.multiple_of(cc * CHUNK, CHUNK), CHUNK)
            c, s = c_ref[rows, :], s_ref[rows, :]
            for pr in range(HP):
                ps = slice(pr * 128, (pr + 1) * 128)
                q2, k2 = q_ref[rows, ps], k_ref[rows, ps]
                sf32, sb32 = sf_ref[cc, pr], sb_ref[cc, pr]
                dsf32, dsb32 = dsf_ref[cc, pr], dsb_ref[cc, pr]
                sfp, sbp = sf32.astype(BF16), sb32.astype(BF16)
                dsfp, dsbp = dsf32.astype(BF16), dsb32.astype(BF16)
                dq2 = jnp.zeros((CHUNK, 128), F32)
                dk2 = jnp.zeros((CHUNK, 128), F32)
                for e in range(2):
                    h = 2 * pr + e
                    sl = slice(h * DV, (h + 1) * DV)
                    hm = _head_lane_mask(q2.shape, e)
                    qm = jnp.where(hm, q2, jnp.zeros_like(q2))
                    km = jnp.where(hm, k2, jnp.zeros_like(k2))
                    qf, kf = qm.astype(F32), km.astype(F32)
                    v, do = v_ref[rows, sl], do_ref[rows, sl]
                    vf, dof = v.astype(F32), do.astype(F32)
                    m_t = tab_ref[h, T_MT]
                    sc = _dot_nt(qm, k2)
                    dpm = _dot_nt(do, v)
                    dsc = (dpm * tab_ref[h, T_M]).astype(BF16)
                    a_t = (_dot_nt(km, q2) * m_t).astype(BF16)
                    dsc_t = (_dot_nt(v, do) * m_t).astype(BF16)
                    dq_f, dq_b = tab_ref[h, T_QF], tab_ref[h, T_QB]
                    dk_f, dk_b = tab_ref[h, T_KF], tab_ref[h, T_KB]
                    dq = _dot(dsc, km)
                    dq += jnp.where(hm, dq_f * _dot_nt(do, sfp) + dq_b * _dot_nt(do, sbp), 0.0)
                    dk = _dot(dsc_t, qm)
                    dk += jnp.where(hm, dk_f * _dot_nt(v, dsfp) + dk_b * _dot_nt(v, dsbp), 0.0)
                    kdf = _dot((kf * dk_f).astype(BF16), dsfp)
                    kdb = _dot((kf * dk_b).astype(BF16), dsbp)
                    dr_ref[rows, D + h * DV:D + (h + 1) * DV] = (_dot(a_t, do) + kdf + kdb).astype(BF16)
                    dq2 += dq
                    dk2 += dk
                    xf = _dot((qf * dq_f).astype(BF16), sfp)
                    xb = _dot((qf * dq_b).astype(BF16), sbp)
                    pair = (rowid < DK) if e == 0 else (rowid >= DK)
                    gcf, gcb = tab_ref[h, T_QF, CHUNK - 1:CHUNK, 0:1], tab_ref[h, T_QB, 0:1, 0:1]
                    scdp = sc * dpm
                    dl_acc[h, 0] += scdp * tab_ref[h, T_MF1] + xf * dof * (i + 1.0) \
                        + kdf * vf * (CHUNK - 1.0 - i) + (CHUNK * gcf) * jnp.where(pair, dsf32 * sf32, 0.0)
                    dl_acc[h, 1] += scdp * tab_ref[h, T_MB1] + xb * dof * (CHUNK - i) \
                        + kdb * vf * i + (CHUNK * gcb) * jnp.where(pair, dsb32 * sb32, 0.0)
                dr_ref[rows, ps] = (dq2 * c - _swap_halves(dq2) * s).astype(BF16)
                dr_ref[rows, W + pr * 128:W + (pr + 1) * 128] = \
                    ((dk2 * c - _swap_halves(dk2) * s) * K_SCALE).astype(BF16)
            return carry

        lax.fori_loop(0, RET_CPB, chunk, 0)

        @pl.when(pl.program_id(0) == N // RET_CPB - 1)
        def _():
            lane = lax.broadcasted_iota(jnp.int32, (1, 128), 1)
            acc = [jnp.zeros((1, 128), F32), jnp.zeros((1, 128), F32)]
            for h in range(H):
                for b in range(2):
                    acc[b] += jnp.where(lane == h, _sum_all(dl_acc[h, b]), 0.0)
            st_ref[...] = jnp.zeros_like(st_ref)
            st_ref[0:1, :] = acc[0]
            st_ref[1:2, :] = acc[1]

    R = RET_CPB * CHUNK
    st_spec = pl.BlockSpec((RET_CPB, HP, 128, 128), lambda n: (n, 0, 0, 0))
    half = pl.BlockSpec((R, W), lambda n: (n, 0))
    rope = pl.BlockSpec((R, 128), lambda n: (n, 0))
    return _riding_call(
        body, exchange, N // RET_CPB, name="ret_bwd_main",
        args=(tab, qr, kr, p, do, sf_prev, sb_prev, dsf, dsb, c2, s2),
        in_specs=[_tab_spec(H), half, half,
                  pl.BlockSpec((R, D), lambda n: (n, 5)),
                  pl.BlockSpec((R, D), lambda n: (n, 0)),
                  st_spec, st_spec, st_spec, st_spec, rope, rope],
        out_specs=[pl.BlockSpec((R, 2 * D), lambda n: (n, 0)),
                   pl.BlockSpec((8, 128), lambda n: (0, 0))],
        out_shape=[jax.ShapeDtypeStruct((L, 2 * D), BF16), jax.ShapeDtypeStruct((8, 128), F32)],
        scratch=[pltpu.VMEM((H, 2, CHUNK, 128), F32)],
        cparams=_cparams(("arbitrary",)))


def _ctx_bwd(pc, pqk_c, ds0, lg, D):
    Lc = pc.shape[0]
    H = D // DV
    HP = H // 2
    W = D // 2

    def body(lg_ref, k_ref, v_ref, ds_ref, dr_ref, st_ref):
        dqk_ref = dr_ref.at[:, 0:D]
        dv_ref = dr_ref.at[:, D:2 * D]
        m = lax.broadcasted_iota(jnp.int32, (Lc, 128), 0).astype(F32)
        lane = lax.broadcasted_iota(jnp.int32, (1, 128), 1)
        acc_f = jnp.zeros((1, 128), F32)
        acc_b = jnp.zeros((1, 128), F32)
        dqk_ref[:, 0:W] = jnp.zeros((Lc, W), BF16)
        for pr in range(HP):
            ps = slice(pr * 128, (pr + 1) * 128)
            k2 = k_ref[:, ps].astype(F32) * K_SCALE
            dsfp, dsbp = ds_ref[0, pr].astype(BF16), ds_ref[1, pr].astype(BF16)
            dk2 = jnp.zeros((Lc, 128), F32)
            for e in range(2):
                h = 2 * pr + e
                sl = slice(h * DV, (h + 1) * DV)
                hm = _head_lane_mask(k2.shape, e)
                km = jnp.where(hm, k2, 0.0)
                v = v_ref[:, sl]
                vf = v.astype(F32)
                dec_f = jnp.exp(lg_ref[0, h] * (Lc - 1.0 - m))
                dec_b = jnp.exp(lg_ref[1, h] * m)
                kdf = _dot((km * dec_f).astype(BF16), dsfp)
                kdb = _dot((km * dec_b).astype(BF16), dsbp)
                dv_ref[:, sl] = (kdf + kdb).astype(BF16)
                dk2 += jnp.where(hm, dec_f * _dot_nt(v, dsfp) + dec_b * _dot_nt(v, dsbp), 0.0)
                acc_f += jnp.where(lane == h, _sum_all(kdf * vf * (Lc - 1.0 - m)), 0.0)
                acc_b += jnp.where(lane == h, _sum_all(kdb * vf * m), 0.0)
            dqk_ref[:, W + pr * 128:W + (pr + 1) * 128] = (dk2 * K_SCALE).astype(BF16)
        st_ref[...] = jnp.zeros_like(st_ref)
        st_ref[0:1, :] = acc_f
        st_ref[1:2, :] = acc_b

    return pl.pallas_call(
        body, name="ctx_bwd", grid=(1,),
        in_specs=[_smem_spec(), pl.BlockSpec((Lc, W), lambda i: (0, 1)), pl.BlockSpec((Lc, D), lambda i: (0, 1)),
                  pl.BlockSpec((2, HP, 128, 128), lambda i: (0, 0, 0, 0))],
        out_specs=[pl.BlockSpec((Lc, 2 * D), lambda i: (0, 0)), pl.BlockSpec((8, 128), lambda i: (0, 0))],
        out_shape=[jax.ShapeDtypeStruct((Lc, 2 * D), BF16), jax.ShapeDtypeStruct((8, 128), F32)],
    )(lg, pqk_c, pc, ds0)


class _Exchange(NamedTuple):
    inputs: tuple
    out_shapes: tuple
    n_copies: int
    build: Callable


def _exchange_parts(exchange):
    if exchange is None:
        return [], [], [], [], []
    n = exchange.n_copies
    return (list(exchange.inputs), [ANY] * len(exchange.inputs), list(exchange.out_shapes),
            [ANY] * len(exchange.out_shapes), [pltpu.SemaphoreType.DMA((n,)), pltpu.SemaphoreType.DMA((n,))])


def _riding_call(body, exchange, n_steps, *, args, in_specs, out_specs, out_shape, name, cparams, scratch=()):
    ex_args, ex_in_specs, ex_shapes, ex_out_specs, ex_scratch = _exchange_parts(exchange)
    n_in, n_out, n_sc = len(args), len(out_shape), len(scratch)

    def riding(*refs):
        k = n_in + len(ex_args)
        ins, ex_in = refs[:n_in], refs[n_in:k]
        outs, ex_out = refs[k:k + n_out], refs[k + n_out:k + n_out + len(ex_shapes)]
        k += n_out + len(ex_shapes)
        own_scratch, ex_sems = refs[k:k + n_sc], refs[k + n_sc:]
        step = pl.program_id(0)
        if exchange is not None:
            @pl.when(step == 0)
            def _():
                for rc in exchange.build(ex_in, ex_out, *ex_sems):
                    rc.start()
        body(*ins, *outs, *own_scratch)
        if exchange is not None:
            @pl.when(step == n_steps - 1)
            def _():
                for rc in exchange.build(ex_in, ex_out, *ex_sems):
                    rc.wait()

    return tuple(pl.pallas_call(
        riding, name=name, grid=(n_steps,),
        in_specs=list(in_specs) + ex_in_specs, out_specs=list(out_specs) + ex_out_specs,
        out_shape=list(out_shape) + ex_shapes, scratch_shapes=list(scratch) + ex_scratch,
        compiler_params=cparams,
    )(*args, *ex_args))


def _dxm(groups, col0, w, x, nw, sc, dx1, name, exchange=None):
    L, D = x.shape
    tm = min(256, L)
    nt = L // tm
    ng = len(groups)
    widths = [g.shape[1] for g in groups]
    wtot = sum(widths)
    with_dx = dx1 is not None
    ex_args, ex_in_specs, ex_shapes, ex_out_specs, ex_scratch = _exchange_parts(exchange)
    n_in = ng + 4 + (1 if with_dx else 0)
    n_out = 2 if with_dx else 1

    def body(*refs):
        group_refs = refs[:ng]
        w_hbm, x_ref, nw_ref, sc_ref = refs[ng:ng + 4]
        ex_in = refs[n_in:n_in + len(ex_args)]
        outs = refs[n_in + len(ex_args):]
        if with_dx:
            dx1_ref, gx_ref, st_ref = refs[ng + 4], outs[0], outs[1]
        else:
            st_ref = outs[0]
        ex_out = outs[n_out:n_out + len(ex_shapes)]
        w_vm, sem = outs[n_out + len(ex_shapes):n_out + len(ex_shapes) + 2]
        ex_sems = outs[n_out + len(ex_shapes) + 2:]
        i = pl.program_id(0)

        offs = [sum(widths[:g]) for g in range(ng)]
        loads = [pltpu.make_async_copy(w_hbm.at[:, col0 * D + offs[g]:col0 * D + offs[g] + widths[g]],
                                       w_vm.at[:, offs[g]:offs[g] + widths[g]], sem.at[g]) for g in range(ng)]

        @pl.when(i == 0)
        def _():
            for cp in loads:
                cp.start()
            if exchange is not None:
                for rc in exchange.build(ex_in, ex_out, *ex_sems):
                    rc.start()
            st_ref[...] = jnp.zeros_like(st_ref)

        dxm = None
        for g, g_ref in enumerate(group_refs):
            @pl.when(i == 0)
            def _(g=g):
                loads[g].wait()

            part = _dot_nt(g_ref[...], w_vm[:, offs[g]:offs[g] + widths[g]])
            dxm = part if dxm is None else dxm + part

        xv = x_ref[...]
        r = lax.rsqrt(jnp.mean(xv * xv, axis=-1, keepdims=True) + EPS)
        xh = xv * r
        nwv = nw_ref[...]
        dxn = dxm * (1.0 + sc_ref[...])
        st_ref[0:1, :] += jnp.sum(dxm, axis=0, keepdims=True)
        st_ref[1:2, :] += jnp.sum(dxm * (xh * nwv), axis=0, keepdims=True)
        st_ref[2:3, :] += jnp.sum(dxn * xh, axis=0, keepdims=True)
        if with_dx:
            dxh = dxn * nwv
            gx_ref[...] = dx1_ref[...] + r * (dxh - xh * jnp.mean(dxh * xh, axis=-1, keepdims=True))

        if exchange is not None:
            @pl.when(i == nt - 1)
            def _():
                for rc in exchange.build(ex_in, ex_out, *ex_sems):
                    rc.wait()

    row = pl.BlockSpec((tm, D), lambda i: (i, 0))
    in_specs = [pl.BlockSpec((tm, wd), lambda i: (i, 0)) for wd in widths] + [ANY, row, _vec_spec(D), _vec_spec(D)]
    out_specs = [pl.BlockSpec((8, D), lambda i: (0, 0))]
    out_shape = [jax.ShapeDtypeStruct((8, D), F32)]
    args = list(groups) + [w, x, nw, sc]
    if with_dx:
        in_specs.append(row)
        out_specs.insert(0, row)
        out_shape.insert(0, jax.ShapeDtypeStruct((L, D), F32))
        args.append(dx1)
    res = pl.pallas_call(
        body, name=name, grid=(nt,),
        in_specs=in_specs + ex_in_specs, out_specs=out_specs + ex_out_specs, out_shape=out_shape + ex_shapes,
        scratch_shapes=[pltpu.VMEM((D, wtot), BF16), pltpu.SemaphoreType.DMA((ng,))] + ex_scratch,
        compiler_params=_cparams(("arbitrary",), VMEM_LIMIT),
    )(*args, *ex_args)
    gx = res[0] if with_dx else None
    return (gx, res[n_out - 1], *res[n_out:])


DW_TN = 512
DW_RING = 4


def _dw_in(xmt, groups, cmt, dr_c, D, pair):
    L = xmt.shape[1]
    Lc = cmt.shape[1]
    Dh = D // 2
    tn = min(DW_TN, D)
    nblk = [g.shape[1] // tn for g in groups]
    starts = [sum(nblk[:g]) for g in range(len(groups))]
    ng = len(groups)
    nj = sum(nblk)
    rows_out = Dh if pair else D

    def body(*refs):
        xt_hbm = refs[0]
        group_refs = refs[1:1 + ng]
        ct_hbm, drc_ref, o_ref = refs[1 + ng:4 + ng]
        rest = refs[4 + ng:]
        if pair:
            ra_hbm, xt_vm, ct_vm, loc, ring, s_send, s_recv = rest
            pos = _position()
            sib = _peer(pos, 1)
        else:
            xt_vm, ct_vm, loc = rest
        j = pl.program_id(0)

        @pl.when(j == 0)
        def _():
            if pair:
                c = pos[2]
                other = pl.ds(pl.multiple_of((1 - c) * Dh, Dh), Dh)
                mine = pl.ds(pl.multiple_of(c * Dh, Dh), Dh)
                cps = [pltpu.make_async_copy(xt_hbm.at[other, :], xt_vm.at[0:Dh, :], loc.at[0]),
                       pltpu.make_async_copy(xt_hbm.at[mine, :], xt_vm.at[Dh:D, :], loc.at[1]),
                       pltpu.make_async_copy(ct_hbm.at[other, :], ct_vm.at[0:Dh, :], loc.at[2]),
                       pltpu.make_async_copy(ct_hbm.at[mine, :], ct_vm.at[Dh:D, :], loc.at[3])]
            else:
                cps = [pltpu.make_async_copy(xt_hbm, xt_vm, loc.at[0]), pltpu.make_async_copy(ct_hbm, ct_vm, loc.at[1])]
            for cp in cps:
                cp.start()
            for cp in cps:
                cp.wait()

        def send(slot):
            cols = pl.ds(pl.multiple_of(j * tn, 128), tn)
            return pltpu.make_async_remote_copy(src_ref=ring.at[slot], dst_ref=ra_hbm.at[:, cols],
                                                send_sem=s_send.at[slot], recv_sem=s_recv,
                                                device_id=sib, device_id_type=MESH)

        for g in range(ng):
            @pl.when((j >= starts[g]) & (j < starts[g] + nblk[g]))
            def _(g=g):
                acc = _dot(xt_vm[...], group_refs[g][...])
                if g == 1:
                    acc += _dot(ct_vm[...], drc_ref[...])
                if not pair:
                    o_ref[...] = acc
                    return
                o_ref[...] = acc[Dh:, :]
                slot = lax.rem(j, DW_RING)

                @pl.when(j >= DW_RING)
                def _():
                    send(slot).wait_send()

                ring[slot] = acc[0:Dh, :]
                send(slot).start()

        if pair:
            @pl.when(j == nj - 1)
            def _():
                pltpu.make_async_remote_copy(src_ref=ra_hbm, dst_ref=ra_hbm, send_sem=s_send.at[0], recv_sem=s_recv,
                                             device_id=sib, device_id_type=MESH).wait_recv()
                for slot in range(DW_RING):
                    send(slot).wait_send()

    def group_spec(g, rows):
        return pl.BlockSpec((rows, tn), lambda j: (0, jnp.clip(j - starts[g], 0, nblk[g] - 1)))

    out_specs = [pl.BlockSpec((rows_out, tn), lambda j: (0, j))]
    out_shape = [jax.ShapeDtypeStruct((rows_out, nj * tn), F32)]
    scratch = [pltpu.VMEM((D, L), BF16), pltpu.VMEM((D, Lc), BF16), pltpu.SemaphoreType.DMA((4,))]
    if pair:
        out_specs.append(ANY)
        out_shape.append(jax.ShapeDtypeStruct((Dh, nj * tn), F32))
        scratch += [pltpu.VMEM((DW_RING, Dh, tn), F32), pltpu.SemaphoreType.DMA((DW_RING,)), pltpu.SemaphoreType.DMA]
    return tuple(pl.pallas_call(
        body, name="dw_in", grid=(nj,),
        in_specs=[ANY] + [group_spec(g, L) for g in range(ng)] + [ANY, group_spec(1, Lc)],
        out_specs=out_specs, out_shape=out_shape, scratch_shapes=scratch,
        compiler_params=_cparams(("arbitrary",), VMEM_LIMIT),
    )(xmt, *groups, cmt, dr_c))


def _local_step(x, ctx, tgt, mod_x, mod_c, norm_w, conv_w8, conv_b, lg, gn_w, fw, project, csidx=None):
    L, D = x.shape
    sh_x, sc_x, g_x = mod_x[0:1], mod_x[1:2], mod_x[2:3]
    sh_c, sc_c = mod_c[0:1], mod_c[1:2]
    c2, s2 = _rope_tables(L)
    tab = _decay_tables(lg, D // DV)

    xm, xmt = _norm_mod(x, norm_w, sc_x, sh_x, "norm_mod_x")
    cm, cmt = _norm_mod(ctx, norm_w, sc_c, sh_c, "norm_mod_ctx")
    reduce = csidx is not None
    p, qr, kr, w_in, w3 = project(xm, c2, s2)
    pc, pqk_c = _in_proj(cm, w_in, "in_proj_ctx", QK_BLOCK, 2)
    s0 = _ctx_states(pc, pqk_c, lg, D)
    sf_prev, sb_prev = _ret_states(kr, p, s0, tab, D)
    o, yb = _ret_out(qr, kr, p, sf_prev, sb_prev, gn_w, tab, D)
    dx1, dya, do, dzb, dgab, dw3, st_mid = _mid(p, yb, o, x, tgt, w3, g_x, fw, conv_w8, conv_b, gn_w, D)
    dw3_5 = dw3.reshape(3, N_SHARD, 2, D // 8, D)
    dconv, st_conv, *ra_3 = _conv_bwd(dya, p, conv_w8, conv_b, D, _pair_exchange_w3(dw3_5) if reduce else None)
    dsf, dsb, ds0 = _ret_bwd_states(qr, do, tab, D)
    cs_3 = _sum_pair_w3(csidx[0:1], dw3_5, ra_3[0]) if reduce else None
    dret, st_lg, *rb_3 = _ret_bwd_main(qr, kr, p, do, sf_prev, sb_prev, dsf, dsb, c2, s2, tab, D,
                                       _chips_exchange_w3(cs_3) if reduce else None)
    g_3 = _sum_chips_w3(csidx, cs_3, rb_3[0]) if reduce else dw3
    dret_c, st_lgc = _ctx_bwd(pc, pqk_c, ds0, lg, D)
    groups = (dconv, dret, dzb, dgab)
    _, st_c = _dxm((dret_c,), 4, w_in, ctx, norm_w, sc_c, None, "dxm_ctx")
    return groups, dret_c, xmt, cmt, dx1, sc_x, w_in, g_3, (st_mid, st_conv, st_lg, st_lgc, st_c)


CHIP_FLIPS = (4, 2, 6)
ANY = pl.BlockSpec(memory_space=pl.ANY)
VMEM_FULL = pl.BlockSpec(memory_space=pltpu.VMEM)


def _position():
    return lax.axis_index("x"), lax.axis_index("y"), lax.axis_index("c")


def _peer(pos, k):
    x, y, c = pos
    return (1 - x if k & 4 else x, 1 - y if k & 2 else y, 1 - c if k & 1 else c)


def _dev_id(pos):
    return 4 * pos[0] + 2 * pos[1] + pos[2]


def _shard_of(pos):
    return 2 * pos[0] + pos[1]


def _remote(src, dst, send_sems, recv_sems, idx, to):
    return pltpu.make_async_remote_copy(src_ref=src, dst_ref=dst, send_sem=send_sems.at[idx],
                                        recv_sem=recv_sems.at[idx], device_id=to, device_id_type=MESH)


def _dot_f32(a, b):
    return jnp.dot(a, b, precision=lax.Precision.HIGHEST, preferred_element_type=F32)


def _silu(x):
    return x * _sigmoid(x)


def _fwd_small(c8, cctx8, ada_w, ada_b, conv_w8):
    D = c8.shape[1]
    Wm = ada_w.shape[1]
    Dq = conv_w8.shape[1]

    def body(c_ref, cc_ref, aw_ref, ab_ref, cw_ref, act_ref, mod_ref, cwf_ref,
             cbuf, pmine, pbuf, wbuf, s_c, r_c, s_p, r_p, s_w, r_w):
        pos = _position()
        me, s = _dev_id(pos), _shard_of(pos)
        cbuf[me] = c_ref[...]
        wbuf[s] = cw_ref[...]
        sends = [_remote(c_ref, cbuf.at[me], s_c, r_c, k - 1, _peer(pos, k)) for k in range(1, 8)]
        sends += [_remote(cw_ref, wbuf.at[s], s_w, r_w, j, _peer(pos, k)) for j, k in enumerate(CHIP_FLIPS)]
        for cp in sends:
            cp.start()
        for k in range(1, 8):
            _remote(c_ref, cbuf.at[_dev_id(_peer(pos, k))], s_c, r_c, k - 1, _peer(pos, k)).wait_recv()
        for d in range(N_DEV):
            act_ref[d:d + 1, :] = _silu(cbuf[d, 0:1, :])
        act_ref[8:9, :] = _silu(cc_ref[0:1, :])
        act_ref[9:16, :] = jnp.zeros((7, D), F32)
        part = _dot_f32(act_ref[...], aw_ref[...])
        pmine[...] = part
        pbuf[s] = part
        psend = [_remote(pmine, pbuf.at[s], s_p, r_p, j, _peer(pos, k)) for j, k in enumerate(CHIP_FLIPS)]
        for cp in psend:
            cp.start()
        for j, k in enumerate(CHIP_FLIPS):
            t = _shard_of(_peer(pos, k))
            _remote(pmine, pbuf.at[t], s_p, r_p, j, _peer(pos, k)).wait_recv()
            _remote(cw_ref, wbuf.at[t], s_w, r_w, j, _peer(pos, k)).wait_recv()
        for t in range(N_SHARD):
            mod_ref[:, t * Wm:(t + 1) * Wm] = pbuf[t] + ab_ref[:, t * Wm:(t + 1) * Wm]
            cwf_ref[:, t * Dq:(t + 1) * Dq] = wbuf[t]
        for cp in sends + psend:
            cp.wait_send()

    return pl.pallas_call(
        body, name="fwd_small",
        in_specs=[VMEM_FULL] * 5, out_specs=[VMEM_FULL] * 3,
        out_shape=[jax.ShapeDtypeStruct((16, D), F32), jax.ShapeDtypeStruct((16, 3 * D), F32),
                   jax.ShapeDtypeStruct((8, D), F32)],
        scratch_shapes=[pltpu.VMEM((N_DEV, 8, D), F32), pltpu.VMEM((16, Wm), F32),
                        pltpu.VMEM((N_SHARD, 16, Wm), F32), pltpu.VMEM((N_SHARD, 8, Dq), F32),
                        pltpu.SemaphoreType.DMA((7,)), pltpu.SemaphoreType.DMA((7,)),
                        pltpu.SemaphoreType.DMA((3,)), pltpu.SemaphoreType.DMA((3,)),
                        pltpu.SemaphoreType.DMA((3,)), pltpu.SemaphoreType.DMA((3,))],
        compiler_params=_cparams(None, VMEM_LIMIT),
    )(c8, cctx8, ada_w, ada_b, conv_w8)


AG_CHUNKS = 3


def _ag_in_proj(xm, w_in_s, w3_s, c2, s2):
    L, D = xm.shape
    Wc = w_in_s.shape[1]
    Wq = Wc // AG_CHUNKS
    Dh = D // 2
    Do = w3_s[0].shape[1]
    TM = min(1024, L // 4)
    NT = L // TM
    RC = min(256, Dh)
    NQ = AG_CHUNKS
    order = [(q, j) for q in range(NQ) for j in (0, 1)] + [(q, 2) for q in range(NQ)]

    def body(xm_ref, wi_hbm, wa_ref, wb_ref, wo_ref, c_ref, s_ref, p_hbm, qr_hbm, kr_hbm, fi_hbm, f3_hbm,
             w_vm, cast_buf, s3, stage, qk_stage, ici_s, ici_r, d2d_s, d2d_r, w3_s_, w3_r_, loc, out_sem, qk_sem):
        pos = _position()
        c = pos[2]
        s = _shard_of(pos)
        sib = _peer(pos, 1)
        mine = pl.ds(pl.multiple_of(c * Dh, Dh), Dh)
        other = pl.ds(pl.multiple_of((1 - c) * Dh, Dh), Dh)

        def cast(hf, q):
            def step(i, carry):
                rows = pl.ds(pl.multiple_of(hf * Dh + i * RC, RC), RC)
                cp = pltpu.make_async_copy(wi_hbm.at[rows, q * Wq:(q + 1) * Wq], cast_buf, loc.at[0])
                cp.start()
                cp.wait()
                w_vm[0, q, rows, :] = cast_buf[...].astype(BF16)
                return carry
            lax.fori_loop(0, Dh // RC, step, 0)

        def abs_col(t, q):
            return pl.ds(pl.multiple_of(t * Wc + q * Wq, 128), Wq)

        sends = {(q, j): _remote(w_vm.at[0, q, mine, :], w_vm.at[1 + j, q, mine, :], ici_s, ici_r, q * 3 + j,
                                 _peer(pos, CHIP_FLIPS[j])) for q, j in order}
        for q in range(NQ):
            cast(c, q)
            sends[(q, 0)].start()
            sends[(q, 1)].start()
        for q in range(NQ):
            sends[(q, 2)].start()
        sends = list(sends.values())
        for a, w_ref in enumerate((wa_ref, wb_ref, wo_ref)):
            s3[a] = w_ref[...].astype(BF16)
        for j, k in enumerate(CHIP_FLIPS):
            sends.append(_remote(s3.at[:, c], f3_hbm.at[:, s, c], w3_s_, w3_r_, j, _peer(pos, k)))
            sends[-1].start()
        for q in range(NQ):
            cast(1 - c, q)
        local = [pltpu.make_async_copy(s3, f3_hbm.at[:, s], loc.at[1])]
        local += [pltpu.make_async_copy(w_vm.at[0, q], fi_hbm.at[:, abs_col(s, q)], loc.at[2 + q]) for q in range(NQ)]
        for cp in local:
            cp.start()

        def out_copy(slot, rows, cols):
            return pltpu.make_async_copy(stage.at[slot], p_hbm.at[rows, cols], out_sem.at[slot])

        def block(r, q, t, first):
            cols = abs_col(t, q)

            def row_tile(rt, carry):
                rows = pl.ds(pl.multiple_of(rt * TM, TM), TM)
                acc = _dot(xm_ref[rows, :], w_vm[r, q])
                slot = lax.rem(rt, 2)

                @pl.when(rt >= 2 if first else rt >= 0)
                def _():
                    out_copy(slot, rows, cols).wait()

                stage[slot] = acc.astype(BF16)
                out_copy(slot, rows, cols).start()

                def rotary(lo, scale, dst_hbm):
                    c, s = c_ref[rows, :], s_ref[rows, :]
                    for pr in range(Dh // 128):
                        tq = acc[:, lo + pr * 128:lo + (pr + 1) * 128] * scale
                        qk_stage[:, pr * 128:(pr + 1) * 128] = (tq * c + _swap_halves(tq) * s).astype(BF16)
                    cp = pltpu.make_async_copy(qk_stage, dst_hbm.at[rows, :], qk_sem)
                    cp.start()
                    cp.wait()

                if q == NQ - 1:
                    @pl.when(t == 1)
                    def _():
                        rotary(Wq - Dh, 1.0, qr_hbm)
                if q == 0:
                    @pl.when(t == 2)
                    def _():
                        rotary(0, K_SCALE, kr_hbm)
                return carry

            lax.fori_loop(0, NT, row_tile, 0)

        for q in range(NQ):
            block(0, q, s, q == 0)
        passed = []
        for q, j in order:
            r, idx = 1 + j, q * 3 + j
            t = _shard_of(_peer(pos, CHIP_FLIPS[j]))
            _remote(w_vm.at[r, q, mine, :], w_vm.at[r, q, mine, :], ici_s, ici_r, idx, sib).wait_recv()
            fwd = _remote(w_vm.at[r, q, mine, :], w_vm.at[r, q, mine, :], d2d_s, d2d_r, idx, sib)
            fwd.start()
            passed.append(fwd)
            _remote(w_vm.at[r, q, other, :], w_vm.at[r, q, other, :], d2d_s, d2d_r, idx, sib).wait_recv()
            block(r, q, t, False)
            cp = pltpu.make_async_copy(w_vm.at[r, q], fi_hbm.at[:, abs_col(t, q)], loc.at[2 + NQ + idx])
            cp.start()
            local.append(cp)
        for j, k in enumerate(CHIP_FLIPS):
            t = _shard_of(_peer(pos, k))
            _remote(s3.at[:, c], f3_hbm.at[:, t, c], w3_s_, w3_r_, j, sib).wait_recv()
            fwd = _remote(f3_hbm.at[:, t, c], f3_hbm.at[:, t, c], w3_s_, w3_r_, 3 + j, sib)
            fwd.start()
            passed.append(fwd)
        for j, k in enumerate(CHIP_FLIPS):
            t = _shard_of(_peer(pos, k))
            _remote(s3.at[:, c], f3_hbm.at[:, t, 1 - c], w3_s_, w3_r_, 3 + j, sib).wait_recv()
        for cp in sends + passed:
            cp.wait_send()
        for cp in local:
            cp.wait()
        for slot in range(2):
            out_copy(slot, pl.ds(0, TM), abs_col(s, 0)).wait()

    n_loc = 2 + NQ + 3 * NQ
    return pl.pallas_call(
        body, name="ag_in_proj",
        in_specs=[VMEM_FULL, ANY, VMEM_FULL, VMEM_FULL, VMEM_FULL, VMEM_FULL, VMEM_FULL], out_specs=[ANY] * 5,
        out_shape=[jax.ShapeDtypeStruct((L, N_SHARD * Wc), BF16),
                   jax.ShapeDtypeStruct((L, Dh), BF16), jax.ShapeDtypeStruct((L, Dh), BF16),
                   jax.ShapeDtypeStruct((D, N_SHARD * Wc), BF16), jax.ShapeDtypeStruct((3, N_SHARD, 2, Do, D), BF16)],
        scratch_shapes=[pltpu.VMEM((N_SHARD, NQ, D, Wq), BF16), pltpu.VMEM((RC, Wq), F32),
                        pltpu.VMEM((3, 2, Do, D), BF16), pltpu.VMEM((2, TM, Wq), BF16), pltpu.VMEM((TM, Dh), BF16),
                        pltpu.SemaphoreType.DMA((3 * NQ,)), pltpu.SemaphoreType.DMA((3 * NQ,)),
                        pltpu.SemaphoreType.DMA((3 * NQ,)), pltpu.SemaphoreType.DMA((3 * NQ,)),
                        pltpu.SemaphoreType.DMA((6,)), pltpu.SemaphoreType.DMA((6,)),
                        pltpu.SemaphoreType.DMA((n_loc,)), pltpu.SemaphoreType.DMA((2,)), pltpu.SemaphoreType.DMA],
        compiler_params=_cparams(None, VMEM_LIMIT),
    )(xm, w_in_s, *w3_s, c2, s2)


def _pair_exchange_w3(dw3):
    _, _, _, Do, D = dw3.shape

    def build(ins, outs, send, recv):
        pos = _position()
        return [_remote(ins[0].at[:, :, 1 - pos[2]], outs[0], send, recv, 0, _peer(pos, 1))]

    return _Exchange((dw3,), (jax.ShapeDtypeStruct((3, N_SHARD, Do, D), F32),), 1, build)


def _sum_pair_in(dw_mine, ri):
    Dh, Wf = dw_mine.shape
    Wc = Wf // N_SHARD
    tr = min(256, Dh)

    def body(a_ref, b_ref, o_ref):
        o_ref[...] = (a_ref[...] + b_ref[...]).astype(BF16)

    return pl.pallas_call(
        body, name="sum_pair_in", grid=(Dh // tr, N_SHARD),
        in_specs=[pl.BlockSpec((tr, Wc), lambda i, t: (i, t)), pl.BlockSpec((tr, Wc), lambda i, t: (i, t))],
        out_specs=pl.BlockSpec((None, tr, Wc), lambda i, t: (t, i, 0)),
        out_shape=jax.ShapeDtypeStruct((N_SHARD, Dh, Wc), BF16),
        compiler_params=_cparams(("parallel", "parallel")),
    )(dw_mine, ri)


def _sum_pair_w3(cidx, dw3, r3):
    _, _, _, Do, D = dw3.shape

    def body(c_ref, a_ref, b_ref, o_ref):
        o_ref[...] = (a_ref[...] + b_ref[...]).astype(BF16)

    return pl.pallas_call(
        body, name="sum_pair_w3",
        grid_spec=pltpu.PrefetchScalarGridSpec(
            num_scalar_prefetch=1, grid=(3,),
            in_specs=[pl.BlockSpec((None, N_SHARD, None, Do, D), lambda a, c: (a, 0, c[0], 0, 0)),
                      pl.BlockSpec((None, N_SHARD, Do, D), lambda a, c: (a, 0, 0, 0))],
            out_specs=pl.BlockSpec((None, N_SHARD, Do, D), lambda a, c: (a, 0, 0, 0))),
        out_shape=jax.ShapeDtypeStruct((3, N_SHARD, Do, D), BF16),
        compiler_params=_cparams(("parallel",)),
    )(cidx, dw3, r3)


def _chips_exchange_in(cs_in):
    _, Dh, Wc = cs_in.shape

    def build(ins, outs, send, recv):
        pos = _position()
        return [_remote(ins[0].at[_shard_of(_peer(pos, k))], outs[0].at[j], send, recv, j, _peer(pos, k))
                for j, k in enumerate(CHIP_FLIPS)]

    return _Exchange((cs_in,), (jax.ShapeDtypeStruct((3, Dh, Wc), BF16),), 3, build)


def _chips_exchange_w3(cs_3):
    _, _, Do, D = cs_3.shape

    def build(ins, outs, send, recv):
        pos = _position()
        return [_remote(ins[0].at[:, _shard_of(_peer(pos, k))], outs[0].at[j], send, recv, j, _peer(pos, k))
                for j, k in enumerate(CHIP_FLIPS)]

    return _Exchange((cs_3,), (jax.ShapeDtypeStruct((3, 3, Do, D), BF16),), 3, build)


def _sum_chips_in(csidx, cs_in, rb_in):
    _, Dh, Wc = cs_in.shape
    tr = min(256, Dh)

    def body(s_ref, a_ref, b_ref, o_ref):
        acc = a_ref[...].astype(F32)
        for j in range(3):
            acc = acc + b_ref[j].astype(F32)
        o_ref[...] = acc

    return pl.pallas_call(
        body, name="sum_chips_in",
        grid_spec=pltpu.PrefetchScalarGridSpec(
            num_scalar_prefetch=1, grid=(Dh // tr,),
            in_specs=[pl.BlockSpec((None, tr, Wc), lambda i, s: (s[1], i, 0)),
                      pl.BlockSpec((3, tr, Wc), lambda i, s: (0, i, 0))],
            out_specs=pl.BlockSpec((None, tr, Wc), lambda i, s: (s[0], i, 0))),
        out_shape=jax.ShapeDtypeStruct((2, Dh, Wc), F32),
        compiler_params=_cparams(("parallel",)),
    )(csidx, cs_in, rb_in)


def _sum_chips_w3(csidx, cs_3, rb_3):
    _, _, Do, D = cs_3.shape

    def body(s_ref, a_ref, b_ref, o_ref):
        acc = a_ref[...].astype(F32)
        for j in range(3):
            acc = acc + b_ref[j].astype(F32)
        o_ref[...] = acc

    return pl.pallas_call(
        body, name="sum_chips_w3",
        grid_spec=pltpu.PrefetchScalarGridSpec(
            num_scalar_prefetch=1, grid=(3,),
            in_specs=[pl.BlockSpec((None, None, Do, D), lambda a, s: (a, s[1], 0, 0)),
                      pl.BlockSpec((3, None, Do, D), lambda a, s: (0, a, 0, 0))],
            out_specs=pl.BlockSpec((None, None, Do, D), lambda a, s: (a, s[0], 0, 0))),
        out_shape=jax.ShapeDtypeStruct((3, 2, Do, D), F32),
        compiler_params=_cparams(("parallel",)),
    )(csidx, cs_3, rb_3)


def _adam_math(w, g, m, v):
    m = ADAM_B1 * m + (1.0 - ADAM_B1) * g
    v = ADAM_B2 * v + (1.0 - ADAM_B2) * (g * g)
    m_hat = m / (1.0 - ADAM_B1 ** ADAM_STEP)
    v_hat = v / (1.0 - ADAM_B2 ** ADAM_STEP)
    delta = -ADAM_LR * (m_hat / (jnp.sqrt(v_hat) + ADAM_EPS) + ADAM_WD * w)
    return delta, m, v


def _adamw(w, g, m, v, name):
    R, C = w.shape
    tr = min(128, R)

    def body(w_ref, g_ref, m_ref, v_ref, d_ref, nm_ref, nv_ref):
        d_ref[...], nm_ref[...], nv_ref[...] = _adam_math(w_ref[...], g_ref[...], m_ref[...], v_ref[...])

    blk = pl.BlockSpec((tr, C), lambda i: (i, 0))
    return pl.pallas_call(
        body, name=name, grid=(R // tr,), in_specs=[blk] * 4, out_specs=[blk] * 3,
        out_shape=[jax.ShapeDtypeStruct((R, C), F32)] * 3,
        compiler_params=_cparams(("parallel",), VMEM_LIMIT),
    )(w, g, m, v)


def _adamw3(ws, g3, ms, vs):
    R, C = ws[0].shape

    def body(*refs):
        w_refs, m_refs, v_refs = refs[0:3], refs[3:6], refs[6:9]
        g_ref, outs = refs[9], refs[10:]
        for a in range(3):
            @pl.when(pl.program_id(0) == a)
            def _(a=a):
                res = _adam_math(w_refs[a][...], g_ref[...], m_refs[a][...], v_refs[a][...])
                for q in range(3):
                    outs[3 * a + q][...] = res[q]

    full = pl.BlockSpec((R, C), lambda a: (0, 0))
    res = pl.pallas_call(
        body, name="adamw_w3", grid=(3,),
        in_specs=[full] * 9 + [pl.BlockSpec((None, R, C), lambda a: (a, 0, 0))], out_specs=[full] * 9,
        out_shape=[jax.ShapeDtypeStruct((R, C), F32)] * 9,
        compiler_params=_cparams(("arbitrary",), VMEM_LIMIT),
    )(*ws, *ms, *vs, g3)
    return res[0:3], res[3:6], res[6:9]


SMALL_ROWS = ("c_ctx", "norm_w", "conv_b", "gn_w", "final_norm_w")


def _bwd_small(stats, ada_w, Dq, gh_in, gh_3):
    D = stats[0].shape[1]
    Wm = ada_w.shape[1]

    def body(stx, stm, stc, stv, stl, stlc, aw_ref, gi_in, g3_in, tot_ref, dm_sh, gcw, da_ref, loss_ref, gi_ref, g3_ref,
             vec_ref, vbuf, dm, amine, abuf, s_v, r_v, s_a, r_a, s_g, r_g):
        pos = _position()
        me, s = _dev_id(pos), _shard_of(pos)
        c, sib = pos[2], _peer(pos, 1)
        halves = [_remote(gi_in.at[c], gi_ref.at[c], s_g, r_g, 0, sib),
                  _remote(g3_in.at[:, c], g3_ref.at[:, c], s_g, r_g, 1, sib)]
        for cp in halves:
            cp.start()
        vec_ref[...] = jnp.zeros_like(vec_ref)
        vec_ref[0:2, :] = stx[0:2, :]
        vec_ref[2:3, :] = stm[1:2, :]
        vec_ref[3:5, :] = stc[0:2, :]
        vec_ref[5:6, :] = stx[2:3, :] + stc[2:3, :]
        vec_ref[6:7, :] = stv[3:4, :]
        vec_ref[7:8, :] = stm[3:4, :]
        vec_ref[8:9, :] = stm[0:1, :]
        vec_ref[9:12, :] = stv[0:3, :]
        vec_ref[12:14, 0:128] = stl[0:2, :] + stlc[0:2, :]
        vec_ref[14:15, :] = stm[2:3, :]
        vbuf[me] = vec_ref[...]
        sends = [_remote(vec_ref, vbuf.at[me], s_v, r_v, k - 1, _peer(pos, k)) for k in range(1, 8)]
        for cp in sends:
            cp.start()
        for k in range(1, 8):
            _remote(vec_ref, vbuf.at[_dev_id(_peer(pos, k))], s_v, r_v, k - 1, _peer(pos, k)).wait_recv()
        tot = vbuf[0]
        for d in range(1, N_DEV):
            tot = tot + vbuf[d]
        loss_ref[...] = jnp.zeros((8, 128), F32) + (0.5 / D) * _sum_all(tot[14:15, :])
        dm[...] = jnp.zeros_like(dm)
        for d in range(N_DEV):
            for r in range(3):
                dm[d:d + 1, r * D:(r + 1) * D] = vbuf[d, r:r + 1, :]
        dm[8:9, 0:D] = tot[3:4, :]
        dm[8:9, D:2 * D] = tot[4:5, :]
        for t in range(N_SHARD):
            @pl.when(s == t)
            def _(t=t):
                dm_sh[...] = dm[:, t * Wm:(t + 1) * Wm]
                gcw[...] = tot[9:12, t * Dq:(t + 1) * Dq]
        tot_ref[...] = tot
        part = lax.dot_general(dm_sh[8:16, :], aw_ref[...], (((1,), (1,)), ((), ())),
                               precision=lax.Precision.HIGHEST, preferred_element_type=F32)
        amine[...] = part
        abuf[s] = part
        asend = [_remote(amine, abuf.at[s], s_a, r_a, j, _peer(pos, k)) for j, k in enumerate(CHIP_FLIPS)]
        for cp in asend:
            cp.start()
        for j, k in enumerate(CHIP_FLIPS):
            _remote(amine, abuf.at[_shard_of(_peer(pos, k))], s_a, r_a, j, _peer(pos, k)).wait_recv()
        da = abuf[0]
        for t in range(1, N_SHARD):
            da = da + abuf[t]
        da_ref[...] = da
        _remote(gi_in.at[1 - c], gi_ref.at[1 - c], s_g, r_g, 0, sib).wait_recv()
        _remote(g3_in.at[:, 1 - c], g3_ref.at[:, 1 - c], s_g, r_g, 1, sib).wait_recv()
        for cp in sends + asend + halves:
            cp.wait_send()

    row = lambda *shape: jax.ShapeDtypeStruct(shape, F32)
    return pl.pallas_call(
        body, name="bwd_small",
        in_specs=[VMEM_FULL] * 7 + [ANY, ANY], out_specs=[VMEM_FULL] * 5 + [ANY, ANY],
        input_output_aliases={7: 5, 8: 6},
        out_shape=[row(16, D), row(16, Wm), row(3, Dq), row(8, D), row(8, 128), row(*gh_in.shape), row(*gh_3.shape)],
        scratch_shapes=[pltpu.VMEM((16, D), F32), pltpu.VMEM((N_DEV, 16, D), F32), pltpu.VMEM((16, 3 * D), F32),
                        pltpu.VMEM((8, D), F32), pltpu.VMEM((N_SHARD, 8, D), F32),
                        pltpu.SemaphoreType.DMA((7,)), pltpu.SemaphoreType.DMA((7,)),
                        pltpu.SemaphoreType.DMA((3,)), pltpu.SemaphoreType.DMA((3,)),
                        pltpu.SemaphoreType.DMA((2,)), pltpu.SemaphoreType.DMA((2,))],
        compiler_params=_cparams(None, VMEM_LIMIT),
    )(*stats, ada_w, gh_in, gh_3)


def _small_update(tot, dm_sh, gcw, da, act, p_row, p_ab, p_cw, p_dl):
    D = act.shape[1]
    Wm = dm_sh.shape[1]
    Dq = gcw.shape[1]

    def body(tot_ref, dm_ref, gcw_ref, da_ref, act_ref, prow, pab, pcw, pdl, gaw_ref, *outs):
        o_q = [outs[8 * q:8 * (q + 1)] for q in range(4)]
        tot = tot_ref[...]
        gaw_ref[...] = lax.dot_general(act_ref[...], dm_ref[...], (((0,), (0,)), ((), ())),
                                       precision=lax.Precision.HIGHEST, preferred_element_type=F32)
        cc = prow[0, 0:1, :]
        sg = _sigmoid(cc)
        g_cctx = da_ref[0:1, :] * (sg * (1.0 + cc * (1.0 - sg)))

        def place_all(o, val):
            o[...] = val

        def emit(k, w, g, m, v, place=place_all):
            for q, val in enumerate((g,) + _adam_math(w, g, m, v)):
                place(o_q[q][k], val)

        g_rows = [g_cctx, tot[5:6, :], tot[6:7, :], tot[7:8, :], tot[8:9, :]]
        for k, g in enumerate(g_rows):
            emit(k, prow[0, k:k + 1, :], g, prow[1, k:k + 1, :], prow[2, k:k + 1, :])

        def place_ab(o, val):
            for r in range(3):
                o[0:1, r * D:(r + 1) * D] = val[r:r + 1, :]

        g_ab = jnp.concatenate([tot[0:1, :] + tot[3:4, :], tot[1:2, :] + tot[4:5, :], tot[2:3, :]], axis=0)
        emit(5, pab[0], g_ab, pab[1], pab[2], place_ab)
        emit(6, pcw[0], gcw_ref[...], pcw[1], pcw[2])
        g_dl = jnp.concatenate([tot[12:14, 0:128] * _sigmoid(-pdl[0, 0:2, :]), jnp.zeros((6, 128), F32)], axis=0)
        emit(7, pdl[0], g_dl, pdl[1], pdl[2])

    row = lambda *shape: jax.ShapeDtypeStruct(shape, F32)
    per_q = [row(1, D)] * 5 + [row(1, 3 * D), row(3, Dq), row(8, 128)]
    res = pl.pallas_call(
        body, name="small_update",
        in_specs=[VMEM_FULL] * 9, out_specs=[VMEM_FULL] * 33,
        out_shape=[row(D, Wm)] + per_q * 4,
        compiler_params=_cparams(None, VMEM_LIMIT),
    )(tot, dm_sh, gcw, da, act, p_row, p_ab, p_cw, p_dl)
    return res[0], [res[1 + 8 * q:1 + 8 * (q + 1)] for q in range(4)]


def _pad_rows(a, rows=8):
    return jnp.pad(a, ((0, rows - a.shape[0]), (0, 0)))


def kernel(x, c, ctx, c_ctx, norm_w, ada_w, ada_b, w_in, conv_w, conv_b, decay_logit, gn_w, w_a, w_b, w_out, final_norm_w, loss_target, m_c_ctx, m_norm_w, m_ada_w, m_ada_b, m_w_in, m_conv_w, m_conv_b, m_decay_logit, m_gn_w, m_w_a, m_w_b, m_w_out, m_final_norm_w, v_c_ctx, v_norm_w, v_ada_w, v_ada_b, v_w_in, v_conv_w, v_conv_b, v_decay_logit, v_gn_w, v_w_a, v_w_b, v_w_out, v_final_norm_w):
    L, D = x.shape[1], x.shape[2]
    H = D // DV
    Wc = w_in.shape[2]
    Do = D // 8
    pos = _position()
    me = _dev_id(pos)
    cidx = jnp.reshape(pos[2], (1,)).astype(jnp.int32)
    sidx = jnp.reshape(_shard_of(pos), (1,)).astype(jnp.int32)

    act, mod, conv_w8 = _fwd_small(_pad_rows(c), _pad_rows(c_ctx[None]), ada_w[0], ada_b, _pad_rows(conv_w[0]))
    mod_x = lax.dynamic_slice_in_dim(mod, me, 1, axis=0).reshape(3, D)
    mod_c = mod[8].reshape(3, D)
    lg = jax.nn.log_sigmoid(decay_logit[0])

    w3_s = tuple(w[0].reshape(2, Do, D) for w in (w_a, w_b, w_out))

    def project(xm, c2, s2):
        p, qr, kr, w_in_full, w3_full = _ag_in_proj(xm, w_in[0], w3_s, c2, s2)
        return p, qr, kr, w_in_full, w3_full.reshape(3, D, D)

    csidx = jnp.concatenate([cidx, sidx])
    groups, dret_c, xmt, cmt, dx1, sc_x, w_in_full, gh_3, sts = _local_step(
        x[0], ctx[0], loss_target[0], mod_x, mod_c, norm_w, conv_w8, conv_b, lg, gn_w, final_norm_w[None],
        project, csidx)
    st_mid, st_conv, st_lg, st_lgc, st_c = sts

    dw_mine, ra_in = _dw_in(xmt, groups, cmt, dret_c, D, True)
    cs_in = _sum_pair_in(dw_mine, ra_in)
    grad_x, st_x, rb_in = _dxm(groups, 0, w_in_full, x[0], norm_w, sc_x, dx1, "dxm_x", _chips_exchange_in(cs_in))
    gh_in = _sum_chips_in(csidx, cs_in, rb_in)

    zeros3 = jnp.zeros((3, D), F32)
    p_row = jnp.concatenate(
        [r for t in ((c_ctx[None], norm_w, conv_b, gn_w, final_norm_w[None], zeros3),
                     (m_c_ctx[None], m_norm_w, m_conv_b, m_gn_w, m_final_norm_w[None], zeros3),
                     (v_c_ctx[None], v_norm_w, v_conv_b, v_gn_w, v_final_norm_w[None], zeros3)) for r in t],
        axis=0).reshape(3, 8, D)
    p_ab = jnp.concatenate([ada_b, m_ada_b, v_ada_b], axis=0).reshape(3, 3, D)
    p_cw = jnp.concatenate([conv_w, m_conv_w, v_conv_w], axis=0)
    p_dl = jnp.pad(jnp.concatenate([decay_logit, m_decay_logit, v_decay_logit], axis=0), ((0, 0), (0, 6), (0, 128 - H)))
    tot, dm_sh, gcw, da, loss_t, g_in, g_3 = _bwd_small((st_x, st_mid, st_c, st_conv, st_lg, st_lgc), ada_w[0],
                                                        conv_w.shape[2], gh_in, gh_3)
    g_w_in = g_in.reshape(D, Wc)
    g_3 = g_3.reshape(3, D // 4, D)
    g_ada_w, small = _small_update(tot, dm_sh, gcw, da, act, p_row, p_ab, p_cw, p_dl)

    upd_in = _adamw(w_in[0], g_w_in, m_w_in[0], v_w_in[0], "adamw_w_in")
    upd_ada = _adamw(ada_w[0], g_ada_w, m_ada_w[0], v_ada_w[0], "adamw_ada_w")
    upd_a, upd_b, upd_o = _adamw3((w_a[0], w_b[0], w_out[0]), g_3, (m_w_a[0], m_w_b[0], m_w_out[0]),
                                  (v_w_a[0], v_w_b[0], v_w_out[0]))

    def leaves(q):
        big = lambda g, upd: (g if q == 0 else upd[q - 1])[None]
        r_cctx, r_norm, r_convb, r_gn, r_fnorm, r_ab, r_cw, r_dl = small[q]
        return [r_cctx.reshape(D), r_norm, big(g_ada_w, upd_ada), r_ab, big(g_w_in, upd_in),
                r_cw[None], r_convb, r_dl[0:2, 0:H][None], r_gn,
                big(g_3[0], upd_a), big(g_3[1], upd_b), big(g_3[2], upd_o), r_fnorm.reshape(D)]

    loss = loss_t[0, 0]
    return (loss, grad_x[None], *leaves(0), *leaves(1), *leaves(2), *leaves(3))
```

```python
from typing import Callable, NamedTuple

import jax
import jax.numpy as jnp
from jax import lax
from jax.experimental import pallas as pl
from jax.experimental.pallas import tpu as pltpu

F32 = jnp.float32
BF16 = jnp.bfloat16
MESH = pl.DeviceIdType.MESH

CHUNK = 128
RET_CPB = 4
DV = 128
DK = 64
GRID_W = 64
ROPE_BASE = 10000.0
EPS = 1e-6
K_SCALE = DK ** -0.5
N_SHARD = 4
N_DEV = 8

ADAM_LR = 0.001
ADAM_B1 = 0.9
ADAM_B2 = 0.999
ADAM_EPS = 1e-08
ADAM_WD = 0.01
ADAM_STEP = 10

VMEM_LIMIT = 56 * 1024 * 1024


def _cparams(sem=None, vmem=None):
    kw = {}
    if sem is not None:
        kw["dimension_semantics"] = sem
    if vmem is not None:
        kw["vmem_limit_bytes"] = vmem
    return pltpu.CompilerParams(**kw)


def _dot(a, b):
    return jnp.dot(a, b, preferred_element_type=F32)


def _dot_nt(a, b):
    return lax.dot_general(a, b, (((1,), (1,)), ((), ())), preferred_element_type=F32)


def _dot_tn(a, b):
    return lax.dot_general(a, b, (((0,), (0,)), ((), ())), preferred_element_type=F32)


def _sigmoid(x):
    return 1.0 / (1.0 + jnp.exp(-x))


def _sum_all(x):
    return jnp.sum(jnp.sum(x, axis=1, keepdims=True), axis=0, keepdims=True)


def _swap_halves(t):
    n = t.shape[1]
    lane = lax.broadcasted_iota(jnp.int32, t.shape, 1)
    low = (lane & 32) == 0
    return jnp.where(low, pltpu.roll(t, n - 32, 1), pltpu.roll(t, 32, 1))


def _vec_spec(d):
    return pl.BlockSpec((1, d), lambda *a: (0, 0))


def _norm_mod(x, nw, sc, sh, name):
    L, D = x.shape
    tl = min(256, L)

    def body(x_ref, nw_ref, sc_ref, sh_ref, xm_ref, xmt_ref):
        xv = x_ref[...]
        r = lax.rsqrt(jnp.mean(xv * xv, axis=-1, keepdims=True) + EPS)
        xm = (xv * r * nw_ref[...]) * (1.0 + sc_ref[...]) + sh_ref[...]
        xm_b = xm.astype(BF16)
        xm_ref[...] = xm_b
        xmt_ref[...] = xm_b.T

    return pl.pallas_call(
        body, name=name, grid=(L // tl,),
        in_specs=[pl.BlockSpec((tl, D), lambda i: (i, 0)), _vec_spec(D), _vec_spec(D), _vec_spec(D)],
        out_specs=[pl.BlockSpec((tl, D), lambda i: (i, 0)), pl.BlockSpec((D, tl), lambda i: (0, i))],
        out_shape=[jax.ShapeDtypeStruct((L, D), BF16), jax.ShapeDtypeStruct((D, L), BF16)],
        compiler_params=_cparams(("parallel",)),
    )(x, nw, sc, sh)


QK_BLOCK, V_BLOCK = 4, 5


def _in_proj(xm, w, name, first=0, count=None):
    M, D = xm.shape
    count = w.shape[1] // D if count is None else count
    tm = min(1024, M)

    def body(a_ref, b_ref, o_ref, qk_ref):
        acc = _dot(a_ref[...], b_ref[...])
        o_ref[...] = acc.astype(o_ref.dtype)

        @pl.when(pl.program_id(1) == QK_BLOCK - first)
        def _():
            qk_ref[...] = acc

    return pl.pallas_call(
        body, name=name, grid=(M // tm, count),
        in_specs=[pl.BlockSpec((tm, D), lambda i, j: (i, 0)), pl.BlockSpec((D, D), lambda i, j: (0, first + j))],
        out_specs=[pl.BlockSpec((tm, D), lambda i, j: (i, j)), pl.BlockSpec((tm, D), lambda i, j: (i, 0))],
        out_shape=[jax.ShapeDtypeStruct((M, count * D), BF16), jax.ShapeDtypeStruct((M, D), F32)],
        compiler_params=_cparams(("parallel", "arbitrary")),
    )(xm, w)


def _halo_specs(tl, L, D, col):
    hb = tl // 16
    last = L // 16 - 1
    prev = pl.BlockSpec((16, D), lambda i: (jnp.maximum(i * hb - 1, 0), col))
    nxt = pl.BlockSpec((16, D), lambda i: (jnp.minimum((i + 1) * hb, last), col))
    return prev, nxt


def _shift_rows(u, above, below):
    tl = u.shape[0]
    row = lax.broadcasted_iota(jnp.int32, u.shape, 0)
    dn = jnp.where(row == 0, above, pltpu.roll(u, 1, 0))
    up = jnp.where(row == tl - 1, below, pltpu.roll(u, tl - 1, 0))
    return dn, up


def _rope_tables(L):
    pos = jnp.arange(L)
    row = (pos // GRID_W).astype(F32)
    col = (pos % GRID_W).astype(F32)
    nf = DK // 4
    inv = ROPE_BASE ** (-jnp.arange(nf, dtype=F32) / nf)
    ang = jnp.concatenate([row[:, None] * inv, col[:, None] * inv], axis=-1)
    cos, sin = jnp.cos(ang), jnp.sin(ang)
    return jnp.concatenate([cos, cos, cos, cos], axis=-1), jnp.concatenate([-sin, sin, -sin, sin], axis=-1)


def _smem_spec():
    return pl.BlockSpec(memory_space=pltpu.SMEM)


def _pair_select(e0, e1):
    row = lax.broadcasted_iota(jnp.int32, e0.shape, 0)
    return jnp.where(row < DK, e0, e1)


def _head_lane_mask(shape, e):
    lane = lax.broadcasted_iota(jnp.int32, shape, 1)
    return (lane < DK) if e == 0 else (lane >= DK)


def _ctx_states(pc, pqk_c, lg, D):
    Lc = pc.shape[0]
    H = D // DV

    def body(lg_ref, k_ref, v_ref, s_ref):
        m = lax.broadcasted_iota(jnp.int32, (Lc, DV), 0).astype(F32)
        for pr in range(H // 2):
            k2 = k_ref[:, pr * 128:(pr + 1) * 128].astype(F32) * K_SCALE
            res = [[None, None], [None, None]]
            for e in range(2):
                h = 2 * pr + e
                v = v_ref[:, h * DV:(h + 1) * DV]
                dec_f = jnp.exp(lg_ref[0, h] * (Lc - 1.0 - m))
                dec_b = jnp.exp(lg_ref[1, h] * m)
                res[0][e] = _dot_tn((k2 * dec_f).astype(BF16), v)
                res[1][e] = _dot_tn((k2 * dec_b).astype(BF16), v)
            s_ref[0, pr] = _pair_select(res[0][0], res[0][1])
            s_ref[1, pr] = _pair_select(res[1][0], res[1][1])

    return pl.pallas_call(
        body, name="ctx_states", grid=(1,),
        in_specs=[_smem_spec(), pl.BlockSpec((Lc, D // 2), lambda i: (0, 1)), pl.BlockSpec((Lc, D), lambda i: (0, 1))],
        out_specs=pl.BlockSpec((2, H // 2, 128, 128), lambda i: (0, 0, 0, 0)),
        out_shape=jax.ShapeDtypeStruct((2, H // 2, 128, 128), F32),
    )(lg, pqk_c, pc)


T_M, T_MT = 0, 1
T_MF1, T_MB1 = 2, 3
T_QF, T_QB = 4, 5
T_KF, T_KB = 6, 7


def _decay_tables(lg, H):
    def body(lg_ref, t_ref):
        h = pl.program_id(0)
        lgf, lgb = lg_ref[0, h], lg_ref[1, h]
        i = lax.broadcasted_iota(jnp.int32, (CHUNK, CHUNK), 0).astype(F32)
        j = lax.broadcasted_iota(jnp.int32, (CHUNK, CHUNK), 1).astype(F32)
        d = i - j
        mf = jnp.where(d > 0, jnp.exp(lgf * jnp.maximum(d, 0.0)), 0.0)
        mb = jnp.where(d < 0, jnp.exp(lgb * jnp.maximum(-d, 0.0)), 0.0)
        mf_t = jnp.where(d < 0, jnp.exp(lgf * jnp.maximum(-d, 0.0)), 0.0)
        mb_t = jnp.where(d > 0, jnp.exp(lgb * jnp.maximum(d, 0.0)), 0.0)
        diag = jnp.where(d == 0, 2.0, 0.0)
        t_ref[0, T_M] = mf + mb + diag
        t_ref[0, T_MT] = mf_t + mb_t + diag
        t_ref[0, T_MF1] = mf * d
        t_ref[0, T_MB1] = mb * (-d)
        t_ref[0, T_QF] = jnp.exp(lgf * (i + 1.0))
        t_ref[0, T_QB] = jnp.exp(lgb * (CHUNK - i))
        t_ref[0, T_KF] = jnp.exp(lgf * (CHUNK - 1.0 - i))
        t_ref[0, T_KB] = jnp.exp(lgb * i)

    return pl.pallas_call(
        body, name="decay_tables", grid=(H,), in_specs=[_smem_spec()],
        out_specs=pl.BlockSpec((1, 8, CHUNK, CHUNK), lambda h: (h, 0, 0, 0)),
        out_shape=jax.ShapeDtypeStruct((H, 8, CHUNK, CHUNK), F32),
    )(lg)


def _tab_spec(H):
    return pl.BlockSpec((H, 8, CHUNK, CHUNK), lambda n: (0, 0, 0, 0))


def _chunk_decay(tab_ref, h):
    return tab_ref[h, T_QF, CHUNK - 1:CHUNK, :], tab_ref[h, T_QB, 0:1, :]


def _ret_states(kr, p, s0, tab, D):
    L = kr.shape[0]
    H = D // DV
    N = L // CHUNK
    HP = H // 2

    def body(tab_ref, kf_ref, kb_ref, vf_ref, vb_ref, s0_ref, sf_out, sb_out, sf, sb):
        n = pl.program_id(0)

        @pl.when(n == 0)
        def _():
            sf[...] = s0_ref[0]
            sb[...] = s0_ref[1]

        for cc in range(RET_CPB):
            cf_, cb_ = cc, RET_CPB - 1 - cc
            rf, rb = slice(cf_ * CHUNK, (cf_ + 1) * CHUNK), slice(cb_ * CHUNK, (cb_ + 1) * CHUNK)
            sf_out[cf_] = sf[...]
            sb_out[cb_] = sb[...]
            for pr in range(HP):
                kf2 = kf_ref[rf, pr * 128:(pr + 1) * 128].astype(F32)
                kb2 = kb_ref[rb, pr * 128:(pr + 1) * 128].astype(F32)
                inc_f, inc_b, gf, gb = [], [], [], []
                for e in range(2):
                    h = 2 * pr + e
                    inc_f.append(_dot_tn((kf2 * tab_ref[h, T_KF]).astype(BF16), vf_ref[rf, h * DV:(h + 1) * DV]))
                    inc_b.append(_dot_tn((kb2 * tab_ref[h, T_KB]).astype(BF16), vb_ref[rb, h * DV:(h + 1) * DV]))
                    cf, cb = _chunk_decay(tab_ref, h)
                    gf.append(jnp.broadcast_to(cf, (128, 128)))
                    gb.append(jnp.broadcast_to(cb, (128, 128)))
                sf[pr] = _pair_select(gf[0], gf[1]) * sf[pr] + _pair_select(inc_f[0], inc_f[1])
                sb[pr] = _pair_select(gb[0], gb[1]) * sb[pr] + _pair_select(inc_b[0], inc_b[1])

    st = jax.ShapeDtypeStruct((N, HP, 128, 128), F32)
    R = RET_CPB * CHUNK
    NB = N // RET_CPB
    return _riding_call(
        body, None, NB, name="ret_states", args=(tab, kr, kr, p, p, s0),
        in_specs=[_tab_spec(H),
                  pl.BlockSpec((R, D // 2), lambda n: (n, 0)),
                  pl.BlockSpec((R, D // 2), lambda n: (NB - 1 - n, 0)),
                  pl.BlockSpec((R, D), lambda n: (n, 5)),
                  pl.BlockSpec((R, D), lambda n: (NB - 1 - n, 5)),
                  pl.BlockSpec((2, HP, 128, 128), lambda n: (0, 0, 0, 0))],
        out_specs=[pl.BlockSpec((RET_CPB, HP, 128, 128), lambda n: (n, 0, 0, 0)),
                   pl.BlockSpec((RET_CPB, HP, 128, 128), lambda n: (NB - 1 - n, 0, 0, 0))],
        out_shape=[st, st],
        scratch=[pltpu.VMEM((HP, 128, 128), F32), pltpu.VMEM((HP, 128, 128), F32)],
        cparams=_cparams(("arbitrary",)))


def _ret_out(qr, kr, p, sf_prev, sb_prev, gn_w, tab, D):
    L = qr.shape[0]
    H = D // DV
    N = L // CHUNK
    HP = H // 2

    def body(tab_ref, q_ref, k_ref, v_ref, zb_ref, sf_ref, sb_ref, gn_ref, o_ref, yb_ref):
        def chunk(cc, carry):
            rows = pl.ds(pl.multiple_of(cc * CHUNK, CHUNK), CHUNK)
            for pr in range(HP):
                q2 = q_ref[rows, pr * 128:(pr + 1) * 128]
                k2 = k_ref[rows, pr * 128:(pr + 1) * 128]
                sfp = sf_ref[cc, pr].astype(BF16)
                sbp = sb_ref[cc, pr].astype(BF16)
                for e in range(2):
                    h = 2 * pr + e
                    sl = slice(h * DV, (h + 1) * DV)
                    qm = jnp.where(_head_lane_mask(q2.shape, e), q2, jnp.zeros_like(q2))
                    a = (_dot_nt(qm, k2) * tab_ref[h, T_M]).astype(BF16)
                    qf = qm.astype(F32)
                    o = _dot(a, v_ref[rows, sl])
                    o += _dot((qf * tab_ref[h, T_QF]).astype(BF16), sfp)
                    o += _dot((qf * tab_ref[h, T_QB]).astype(BF16), sbp)
                    o_ref[rows, sl] = o
                    mu = jnp.mean(o, axis=-1, keepdims=True)
                    oc = o - mu
                    rstd = lax.rsqrt(jnp.mean(oc * oc, axis=-1, keepdims=True) + EPS)
                    zb = zb_ref[rows, sl].astype(F32)
                    yb_ref[rows, sl] = (zb * _sigmoid(zb) * (oc * rstd * gn_ref[:, sl])).astype(BF16)
            return carry

        lax.fori_loop(0, RET_CPB, chunk, 0)

    R = RET_CPB * CHUNK
    return _riding_call(
        body, None, N // RET_CPB, name="ret_out", args=(tab, qr, kr, p, p, sf_prev, sb_prev, gn_w),
        in_specs=[_tab_spec(H),
                  pl.BlockSpec((R, D // 2), lambda n: (n, 0)),
                  pl.BlockSpec((R, D // 2), lambda n: (n, 0)),
                  pl.BlockSpec((R, D), lambda n: (n, 5)),
                  pl.BlockSpec((R, D), lambda n: (n, 6)),
                  pl.BlockSpec((RET_CPB, HP, 128, 128), lambda n: (n, 0, 0, 0)),
                  pl.BlockSpec((RET_CPB, HP, 128, 128), lambda n: (n, 0, 0, 0)),
                  _vec_spec(D)],
        out_specs=[pl.BlockSpec((R, D), lambda n: (n, 0)), pl.BlockSpec((R, D), lambda n: (n, 0))],
        out_shape=[jax.ShapeDtypeStruct((L, D), F32), jax.ShapeDtypeStruct((L, D), BF16)],
        cparams=_cparams(("arbitrary",)))


def _mid(p, yb, o, x, tgt, w3, g, fw, conv_w, conv_b, gn_w, D):
    L = x.shape[0]
    H = D // DV
    tm = min(256, L)
    nt = L // tm

    def body(h_ref, bg_ref, cg_ref, za_ref, hp_ref, hn_ref, cp_ref, cn_ref, yb_ref, ga_ref, gb_ref, zb_ref, o_ref,
             x_ref, t_ref, w_hbm, g_ref, fw_ref, cw_ref, cb_ref, gn_ref,
             dx1_ref, dya_ref, do_ref, dzb_ref, dgab_ref, dw_hbm, st_ref, w_vm, dw_acc, sem):
        i = pl.program_id(0)

        @pl.when(i == 0)
        def _():
            cp = pltpu.make_async_copy(w_hbm, w_vm, sem)
            cp.start()
            dw_acc[...] = jnp.zeros_like(dw_acc)
            st_ref[...] = jnp.zeros_like(st_ref)
            cp.wait()

        u = cg_ref[...].astype(F32) * h_ref[...].astype(F32)
        above = jnp.where(i == 0, 0.0, cp_ref[15:16, :].astype(F32) * hp_ref[15:16, :].astype(F32))
        below = jnp.where(i == nt - 1, 0.0, cn_ref[0:1, :].astype(F32) * hn_ref[0:1, :].astype(F32))
        dn, up = _shift_rows(u, above, below)
        co = cw_ref[0:1, :] * dn + cw_ref[1:2, :] * u + cw_ref[2:3, :] * up + cb_ref[...]
        za = za_ref[...].astype(F32)
        ya_b = (za * _sigmoid(za) * bg_ref[...].astype(F32) * co).astype(BF16)
        yb_b = yb_ref[...]
        y_a = _dot(ya_b, w_vm[0])
        y_b = _dot(yb_b, w_vm[1])
        sga = _sigmoid(ga_ref[...].astype(F32))
        sgb = _sigmoid(gb_ref[...].astype(F32))
        mix_b = (sga * y_a + sgb * y_b).astype(BF16)
        y_x = _dot(mix_b, w_vm[2])
        gvec, fwv = g_ref[...], fw_ref[...]
        x1 = x_ref[...] + gvec * y_x
        r1 = lax.rsqrt(jnp.mean(x1 * x1, axis=-1, keepdims=True) + EPS)
        xh = x1 * r1
        diff = xh * fwv - t_ref[...]
        dout = diff * (1.0 / D)
        dxh = dout * fwv
        dx1 = r1 * (dxh - xh * jnp.mean(dxh * xh, axis=-1, keepdims=True))
        dx1_ref[...] = dx1
        st_ref[0:1, :] += jnp.sum(dout * xh, axis=0, keepdims=True)
        st_ref[1:2, :] += jnp.sum(dx1 * y_x, axis=0, keepdims=True)
        st_ref[2:3, :] += jnp.sum(diff * diff, axis=0, keepdims=True)
        dyx_b = (dx1 * gvec).astype(BF16)
        dmix = _dot_nt(dyx_b, w_vm[2])
        dw_acc[2] += _dot_tn(mix_b, dyx_b)
        dya_b = (dmix * sga).astype(BF16)
        dyb_b = (dmix * sgb).astype(BF16)
        dgab_ref[:, 0:D] = (dmix * y_a * sga * (1.0 - sga)).astype(BF16)
        dgab_ref[:, D:2 * D] = (dmix * y_b * sgb * (1.0 - sgb)).astype(BF16)
        dya_ref[...] = _dot_nt(dya_b, w_vm[0])
        dyb = _dot_nt(dyb_b, w_vm[1])
        dw_acc[0] += _dot_tn(ya_b, dya_b)
        dw_acc[1] += _dot_tn(yb_b, dyb_b)

        for h in range(H):
            sl = slice(h * DV, (h + 1) * DV)
            ov = o_ref[:, sl]
            oc = ov - jnp.mean(ov, axis=-1, keepdims=True)
            rstd = lax.rsqrt(jnp.mean(oc * oc, axis=-1, keepdims=True) + EPS)
            rn = oc * rstd
            gw = gn_ref[:, sl]
            zb = zb_ref[:, sl].astype(F32)
            sz = _sigmoid(zb)
            dy = dyb[:, sl]
            dzb_ref[:, sl] = (dy * (rn * gw) * (sz * (1.0 + zb * (1.0 - sz)))).astype(BF16)
            dretn = dy * (zb * sz)
            st_ref[3:4, sl] += jnp.sum(dretn * rn, axis=0, keepdims=True)
            drn = dretn * gw
            do_ref[:, sl] = (rstd * (drn - jnp.mean(drn, axis=-1, keepdims=True)
                                     - rn * jnp.mean(drn * rn, axis=-1, keepdims=True))).astype(BF16)

        @pl.when(i == nt - 1)
        def _():
            out = pltpu.make_async_copy(dw_acc, dw_hbm, sem)
            out.start()
            out.wait()

    row = lambda col: pl.BlockSpec((tm, D), lambda i: (i, col))
    any_spec = pl.BlockSpec(memory_space=pl.ANY)
    f32o = jax.ShapeDtypeStruct((L, D), F32)
    bf16o = jax.ShapeDtypeStruct((L, D), BF16)
    hp, hn = _halo_specs(tm, L, D, 0)
    cp, cn = _halo_specs(tm, L, D, 2)
    return pl.pallas_call(
        body, name="mid", grid=(nt,),
        in_specs=[row(0), row(1), row(2), row(3), hp, hn, cp, cn, row(0), row(7), row(8), row(6), row(0),
                  row(0), row(0), any_spec, _vec_spec(D), _vec_spec(D),
                  pl.BlockSpec((8, D), lambda i: (0, 0)), _vec_spec(D), _vec_spec(D)],
        out_specs=[row(0), row(0), row(0), row(0), pl.BlockSpec((tm, 2 * D), lambda i: (i, 0)), any_spec,
                   pl.BlockSpec((8, D), lambda i: (0, 0))],
        out_shape=[f32o, f32o, bf16o, bf16o, jax.ShapeDtypeStruct((L, 2 * D), BF16),
                   jax.ShapeDtypeStruct((3, D, D), F32), jax.ShapeDtypeStruct((8, D), F32)],
        scratch_shapes=[pltpu.VMEM((3, D, D), BF16), pltpu.VMEM((3, D, D), F32), pltpu.SemaphoreType.DMA],
        compiler_params=_cparams(("arbitrary",), VMEM_LIMIT),
    )(p, p, p, p, p, p, p, p, yb, p, p, p, o, x, tgt, w3, g, fw, conv_w, conv_b, gn_w)


def _conv_bwd(dya, p, conv_w, conv_b, D, exchange=None):
    L = p.shape[0]
    tl = min(256, L)
    nt = L // tl

    def body(d_ref, h_ref, bg_ref, cg_ref, za_ref,
             dp_ref, dn_ref, hp_ref, hn_ref, bp_ref, bn_ref, cp_ref, cn_ref, zp_ref, zn_ref,
             w_ref, b_ref, dc_ref, st_ref):
        i = pl.program_id(0)

        @pl.when(i == 0)
        def _():
            st_ref[...] = jnp.zeros_like(st_ref)

        first, last = i == 0, i == nt - 1
        h = h_ref[...].astype(F32)
        cg = cg_ref[...].astype(F32)
        bg = bg_ref[...].astype(F32)
        za = za_ref[...].astype(F32)
        dy = d_ref[...].astype(F32)
        u = cg * h
        u_above = jnp.where(first, 0.0, cp_ref[15:16, :].astype(F32) * hp_ref[15:16, :].astype(F32))
        u_below = jnp.where(last, 0.0, cn_ref[0:1, :].astype(F32) * hn_ref[0:1, :].astype(F32))
        u_dn, u_up = _shift_rows(u, u_above, u_below)
        w0, w1, w2 = w_ref[0:1, :], w_ref[1:2, :], w_ref[2:3, :]
        co = w0 * u_dn + w1 * u + w2 * u_up + b_ref[...]
        sz = _sigmoid(za)
        silu = za * sz
        dc_ref[:, 3 * D:4 * D] = (dy * bg * co * (sz * (1.0 + za * (1.0 - sz)))).astype(BF16)
        dc_ref[:, D:2 * D] = (dy * silu * co).astype(BF16)
        dco = dy * silu * bg

        def edge(dr, zr, br, r):
            z = zr[r:r + 1, :].astype(F32)
            return dr[r:r + 1, :].astype(F32) * (z * _sigmoid(z)) * br[r:r + 1, :].astype(F32)

        dco_above = jnp.where(first, 0.0, edge(dp_ref, zp_ref, bp_ref, 15))
        dco_below = jnp.where(last, 0.0, edge(dn_ref, zn_ref, bn_ref, 0))
        dco_dn, dco_up = _shift_rows(dco, dco_above, dco_below)
        du = w0 * dco_up + w1 * dco + w2 * dco_dn
        dc_ref[:, 2 * D:3 * D] = (du * h).astype(BF16)
        dc_ref[:, 0:D] = (du * cg).astype(BF16)
        st_ref[0:1, :] += jnp.sum(dco * u_dn, axis=0, keepdims=True)
        st_ref[1:2, :] += jnp.sum(dco * u, axis=0, keepdims=True)
        st_ref[2:3, :] += jnp.sum(dco * u_up, axis=0, keepdims=True)
        st_ref[3:4, :] += jnp.sum(dco, axis=0, keepdims=True)

    main = lambda col: pl.BlockSpec((tl, D), lambda i: (i, col))
    halos = []
    for col in (0, 0, 1, 2, 3):
        halos.extend(_halo_specs(tl, L, D, col))
    return _riding_call(
        body, exchange, nt, name="conv_bwd",
        args=(dya, p, p, p, p, dya, dya, p, p, p, p, p, p, p, p, conv_w, conv_b),
        in_specs=[main(0), main(0), main(1), main(2), main(3)] + halos
                 + [pl.BlockSpec((8, D), lambda i: (0, 0)), _vec_spec(D)],
        out_specs=[pl.BlockSpec((tl, 4 * D), lambda i: (i, 0)), pl.BlockSpec((8, D), lambda i: (0, 0))],
        out_shape=[jax.ShapeDtypeStruct((L, 4 * D), BF16), jax.ShapeDtypeStruct((8, D), F32)],
        cparams=_cparams(("arbitrary",)))


def _ret_bwd_states(qr, do, tab, D):
    L = qr.shape[0]
    H = D // DV
    N = L // CHUNK
    HP = H // 2

    def body(tab_ref, qf_ref, qb_ref, dof_ref, dob_ref, dsf_out, dsb_out, ds0_out, dsf, dsb):
        n = pl.program_id(0)

        @pl.when(n == 0)
        def _():
            dsf[...] = jnp.zeros_like(dsf)
            dsb[...] = jnp.zeros_like(dsb)

        for cc in range(RET_CPB):
            cf_, cb_ = RET_CPB - 1 - cc, cc
            rf, rb = slice(cf_ * CHUNK, (cf_ + 1) * CHUNK), slice(cb_ * CHUNK, (cb_ + 1) * CHUNK)
            dsf_out[cf_] = dsf[...]
            dsb_out[cb_] = dsb[...]
            for pr in range(HP):
                qf2 = qf_ref[rf, pr * 128:(pr + 1) * 128].astype(F32)
                qb2 = qb_ref[rb, pr * 128:(pr + 1) * 128].astype(F32)
                inc_f, inc_b, gf, gb = [], [], [], []
                for e in range(2):
                    h = 2 * pr + e
                    inc_f.append(_dot_tn((qf2 * tab_ref[h, T_QF]).astype(BF16), dof_ref[rf, h * DV:(h + 1) * DV]))
                    inc_b.append(_dot_tn((qb2 * tab_ref[h, T_QB]).astype(BF16), dob_ref[rb, h * DV:(h + 1) * DV]))
                    cf, cb = _chunk_decay(tab_ref, h)
                    gf.append(jnp.broadcast_to(cf, (128, 128)))
                    gb.append(jnp.broadcast_to(cb, (128, 128)))
                dsf[pr] = _pair_select(gf[0], gf[1]) * dsf[pr] + _pair_select(inc_f[0], inc_f[1])
                dsb[pr] = _pair_select(gb[0], gb[1]) * dsb[pr] + _pair_select(inc_b[0], inc_b[1])

        @pl.when(n == NB - 1)
        def _():
            ds0_out[0] = dsf[...]
            ds0_out[1] = dsb[...]

    st = jax.ShapeDtypeStruct((N, HP, 128, 128), F32)
    R = RET_CPB * CHUNK
    NB = N // RET_CPB
    return pl.pallas_call(
        body, name="ret_bwd_states", grid=(NB,),
        in_specs=[_tab_spec(H),
                  pl.BlockSpec((R, D // 2), lambda n: (NB - 1 - n, 0)),
                  pl.BlockSpec((R, D // 2), lambda n: (n, 0)),
                  pl.BlockSpec((R, D), lambda n: (NB - 1 - n, 0)),
                  pl.BlockSpec((R, D), lambda n: (n, 0))],
        out_specs=[pl.BlockSpec((RET_CPB, HP, 128, 128), lambda n: (NB - 1 - n, 0, 0, 0)),
                   pl.BlockSpec((RET_CPB, HP, 128, 128), lambda n: (n, 0, 0, 0)),
                   pl.BlockSpec((2, HP, 128, 128), lambda n: (0, 0, 0, 0))],
        out_shape=[st, st, jax.ShapeDtypeStruct((2, HP, 128, 128), F32)],
        scratch_shapes=[pltpu.VMEM((HP, 128, 128), F32), pltpu.VMEM((HP, 128, 128), F32)],
        compiler_params=_cparams(("arbitrary",)),
    )(tab, qr, qr, do, do)


def _ret_bwd_main(qr, kr, p, do, sf_prev, sb_prev, dsf, dsb, c2, s2, tab, D, exchange=None):
    L = qr.shape[0]
    H = D // DV
    N = L // CHUNK
    HP = H // 2
    W = D // 2

    def body(tab_ref, q_ref, k_ref, v_ref, do_ref, sf_ref, sb_ref, dsf_ref, dsb_ref, c_ref, s_ref,
             dr_ref, st_ref, dl_acc):
        @pl.when(pl.program_id(0) == 0)
        def _():
            dl_acc[...] = jnp.zeros_like(dl_acc)

        i = lax.broadcasted_iota(jnp.int32, (CHUNK, 128), 0).astype(F32)
        rowid = lax.broadcasted_iota(jnp.int32, (128, 128), 0)

        def chunk(cc, carry):
            rows = pl.ds(pl.multiple_of(cc * CHUNK, CHUNK), CHUNK)
            c, s = c_ref[rows, :], s_ref[rows, :]
            for pr in range(HP):
                ps = slice(pr * 128, (pr + 1) * 128)
                q2, k2 = q_ref[rows, ps], k_ref[rows, ps]
                sf32, sb32 = sf_ref[cc, pr], sb_ref[cc, pr]
                dsf32, dsb32 = dsf_ref[cc, pr], dsb_ref[cc, pr]
                sfp, sbp = sf32.astype(BF16), sb32.astype(BF16)
                dsfp, dsbp = dsf32.astype(BF16), dsb32.astype(BF16)
                dq2 = jnp.zeros((CHUNK, 128), F32)
                dk2 = jnp.zeros((CHUNK, 128), F32)
                for e in range(2):
                    h = 2 * pr + e
                    sl = slice(h * DV, (h + 1) * DV)
                    hm = _head_lane_mask(q2.shape, e)
                    qm = jnp.where(hm, q2, jnp.zeros_like(q2))
                    km = jnp.where(hm, k2, jnp.zeros_like(k2))
                    qf, kf = qm.astype(F32), km.astype(F32)
                    v, do = v_ref[rows, sl], do_ref[rows, sl]
                    vf, dof = v.astype(F32), do.astype(F32)
                    m_t = tab_ref[h, T_MT]
                    sc = _dot_nt(qm, k2)
                    dpm = _dot_nt(do, v)
                    dsc = (dpm * tab_ref[h, T_M]).astype(BF16)
                    a_t = (_dot_nt(km, q2) * m_t).astype(BF16)
                    dsc_t = (_dot_nt(v, do) * m_t).astype(BF16)
                    dq_f, dq_b = tab_ref[h, T_QF], tab_ref[h, T_QB]
                    dk_f, dk_b = tab_ref[h, T_KF], tab_ref[h, T_KB]
                    dq = _dot(dsc, km)
                    dq += jnp.where(hm, dq_f * _dot_nt(do, sfp) + dq_b * _dot_nt(do, sbp), 0.0)
                    dk = _dot(dsc_t, qm)
                    dk += jnp.where(hm, dk_f * _dot_nt(v, dsfp) + dk_b * _dot_nt(v, dsbp), 0.0)
                    kdf = _dot((kf * dk_f).astype(BF16), dsfp)
                    kdb = _dot((kf * dk_b).astype(BF16), dsbp)
                    dr_ref[rows, D + h * DV:D + (h + 1) * DV] = (_dot(a_t, do) + kdf + kdb).astype(BF16)
                    dq2 += dq
                    dk2 += dk
                    xf = _dot((qf * dq_f).astype(BF16), sfp)
                    xb = _dot((qf * dq_b).astype(BF16), sbp)
                    pair = (rowid < DK) if e == 0 else (rowid >= DK)
                    gcf, gcb = tab_ref[h, T_QF, CHUNK - 1:CHUNK, 0:1], tab_ref[h, T_QB, 0:1, 0:1]
                    scdp = sc * dpm
                    dl_acc[h, 0] += scdp * tab_ref[h, T_MF1] + xf * dof * (i + 1.0) \
                        + kdf * vf * (CHUNK - 1.0 - i) + (CHUNK * gcf) * jnp.where(pair, dsf32 * sf32, 0.0)
                    dl_acc[h, 1] += scdp * tab_ref[h, T_MB1] + xb * dof * (CHUNK - i) \
                        + kdb * vf * i + (CHUNK * gcb) * jnp.where(pair, dsb32 * sb32, 0.0)
                dr_ref[rows, ps] = (dq2 * c - _swap_halves(dq2) * s).astype(BF16)
                dr_ref[rows, W + pr * 128:W + (pr + 1) * 128] = \
                    ((dk2 * c - _swap_halves(dk2) * s) * K_SCALE).astype(BF16)
            return carry

        lax.fori_loop(0, RET_CPB, chunk, 0)

        @pl.when(pl.program_id(0) == N // RET_CPB - 1)
        def _():
            lane = lax.broadcasted_iota(jnp.int32, (1, 128), 1)
            acc = [jnp.zeros((1, 128), F32), jnp.zeros((1, 128), F32)]
            for h in range(H):
                for b in range(2):
                    acc[b] += jnp.where(lane == h, _sum_all(dl_acc[h, b]), 0.0)
            st_ref[...] = jnp.zeros_like(st_ref)
            st_ref[0:1, :] = acc[0]
            st_ref[1:2, :] = acc[1]

    R = RET_CPB * CHUNK
    st_spec = pl.BlockSpec((RET_CPB, HP, 128, 128), lambda n: (n, 0, 0, 0))
    half = pl.BlockSpec((R, W), lambda n: (n, 0))
    rope = pl.BlockSpec((R, 128), lambda n: (n, 0))
    return _riding_call(
        body, exchange, N // RET_CPB, name="ret_bwd_main",
        args=(tab, qr, kr, p, do, sf_prev, sb_prev, dsf, dsb, c2, s2),
        in_specs=[_tab_spec(H), half, half,
                  pl.BlockSpec((R, D), lambda n: (n, 5)),
                  pl.BlockSpec((R, D), lambda n: (n, 0)),
                  st_spec, st_spec, st_spec, st_spec, rope, rope],
        out_specs=[pl.BlockSpec((R, 2 * D), lambda n: (n, 0)),
                   pl.BlockSpec((8, 128), lambda n: (0, 0))],
        out_shape=[jax.ShapeDtypeStruct((L, 2 * D), BF16), jax.ShapeDtypeStruct((8, 128), F32)],
        scratch=[pltpu.VMEM((H, 2, CHUNK, 128), F32)],
        cparams=_cparams(("arbitrary",)))


def _ctx_bwd(pc, pqk_c, ds0, lg, D):
    Lc = pc.shape[0]
    H = D // DV
    HP = H // 2
    W = D // 2

    def body(lg_ref, k_ref, v_ref, ds_ref, dr_ref, st_ref):
        dqk_ref = dr_ref.at[:, 0:D]
        dv_ref = dr_ref.at[:, D:2 * D]
        m = lax.broadcasted_iota(jnp.int32, (Lc, 128), 0).astype(F32)
        lane = lax.broadcasted_iota(jnp.int32, (1, 128), 1)
        acc_f = jnp.zeros((1, 128), F32)
        acc_b = jnp.zeros((1, 128), F32)
        dqk_ref[:, 0:W] = jnp.zeros((Lc, W), BF16)
        for pr in range(HP):
            ps = slice(pr * 128, (pr + 1) * 128)
            k2 = k_ref[:, ps].astype(F32) * K_SCALE
            dsfp, dsbp = ds_ref[0, pr].astype(BF16), ds_ref[1, pr].astype(BF16)
            dk2 = jnp.zeros((Lc, 128), F32)
            for e in range(2):
                h = 2 * pr + e
                sl = slice(h * DV, (h + 1) * DV)
                hm = _head_lane_mask(k2.shape, e)
                km = jnp.where(hm, k2, 0.0)
                v = v_ref[:, sl]
                vf = v.astype(F32)
                dec_f = jnp.exp(lg_ref[0, h] * (Lc - 1.0 - m))
                dec_b = jnp.exp(lg_ref[1, h] * m)
                kdf = _dot((km * dec_f).astype(BF16), dsfp)
                kdb = _dot((km * dec_b).astype(BF16), dsbp)
                dv_ref[:, sl] = (kdf + kdb).astype(BF16)
                dk2 += jnp.where(hm, dec_f * _dot_nt(v, dsfp) + dec_b * _dot_nt(v, dsbp), 0.0)
                acc_f += jnp.where(lane == h, _sum_all(kdf * vf * (Lc - 1.0 - m)), 0.0)
                acc_b += jnp.where(lane == h, _sum_all(kdb * vf * m), 0.0)
            dqk_ref[:, W + pr * 128:W + (pr + 1) * 128] = (dk2 * K_SCALE).astype(BF16)
        st_ref[...] = jnp.zeros_like(st_ref)
        st_ref[0:1, :] = acc_f
        st_ref[1:2, :] = acc_b

    return pl.pallas_call(
        body, name="ctx_bwd", grid=(1,),
        in_specs=[_smem_spec(), pl.BlockSpec((Lc, W), lambda i: (0, 1)), pl.BlockSpec((Lc, D), lambda i: (0, 1)),
                  pl.BlockSpec((2, HP, 128, 128), lambda i: (0, 0, 0, 0))],
        out_specs=[pl.BlockSpec((Lc, 2 * D), lambda i: (0, 0)), pl.BlockSpec((8, 128), lambda i: (0, 0))],
        out_shape=[jax.ShapeDtypeStruct((Lc, 2 * D), BF16), jax.ShapeDtypeStruct((8, 128), F32)],
    )(lg, pqk_c, pc, ds0)


class _Exchange(NamedTuple):
    inputs: tuple
    out_shapes: tuple
    n_copies: int
    build: Callable


def _exchange_parts(exchange):
    if exchange is None:
        return [], [], [], [], []
    n = exchange.n_copies
    return (list(exchange.inputs), [ANY] * len(exchange.inputs), list(exchange.out_shapes),
            [ANY] * len(exchange.out_shapes), [pltpu.SemaphoreType.DMA((n,)), pltpu.SemaphoreType.DMA((n,))])


def _riding_call(body, exchange, n_steps, *, args, in_specs, out_specs, out_shape, name, cparams, scratch=()):
    ex_args, ex_in_specs, ex_shapes, ex_out_specs, ex_scratch = _exchange_parts(exchange)
    n_in, n_out, n_sc = len(args), len(out_shape), len(scratch)

    def riding(*refs):
        k = n_in + len(ex_args)
        ins, ex_in = refs[:n_in], refs[n_in:k]
        outs, ex_out = refs[k:k + n_out], refs[k + n_out:k + n_out + len(ex_shapes)]
        k += n_out + len(ex_shapes)
        own_scratch, ex_sems = refs[k:k + n_sc], refs[k + n_sc:]
        step = pl.program_id(0)
        if exchange is not None:
            @pl.when(step == 0)
            def _():
                for rc in exchange.build(ex_in, ex_out, *ex_sems):
                    rc.start()
        body(*ins, *outs, *own_scratch)
        if exchange is not None:
            @pl.when(step == n_steps - 1)
            def _():
                for rc in exchange.build(ex_in, ex_out, *ex_sems):
                    rc.wait()

    return tuple(pl.pallas_call(
        riding, name=name, grid=(n_steps,),
        in_specs=list(in_specs) + ex_in_specs, out_specs=list(out_specs) + ex_out_specs,
        out_shape=list(out_shape) + ex_shapes, scratch_shapes=list(scratch) + ex_scratch,
        compiler_params=cparams,
    )(*args, *ex_args))


def _dxm(groups, col0, w, x, nw, sc, dx1, name, exchange=None):
    L, D = x.shape
    tm = min(256, L)
    nt = L // tm
    ng = len(groups)
    widths = [g.shape[1] for g in groups]
    wtot = sum(widths)
    with_dx = dx1 is not None
    ex_args, ex_in_specs, ex_shapes, ex_out_specs, ex_scratch = _exchange_parts(exchange)
    n_in = ng + 4 + (1 if with_dx else 0)
    n_out = 2 if with_dx else 1

    def body(*refs):
        group_refs = refs[:ng]
        w_hbm, x_ref, nw_ref, sc_ref = refs[ng:ng + 4]
        ex_in = refs[n_in:n_in + len(ex_args)]
        outs = refs[n_in + len(ex_args):]
        if with_dx:
            dx1_ref, gx_ref, st_ref = refs[ng + 4], outs[0], outs[1]
        else:
            st_ref = outs[0]
        ex_out = outs[n_out:n_out + len(ex_shapes)]
        w_vm, sem = outs[n_out + len(ex_shapes):n_out + len(ex_shapes) + 2]
        ex_sems = outs[n_out + len(ex_shapes) + 2:]
        i = pl.program_id(0)

        @pl.when(i == 0)
        def _():
            cp = pltpu.make_async_copy(w_hbm.at[:, col0 * D:col0 * D + wtot], w_vm, sem)
            cp.start()
            if exchange is not None:
                for rc in exchange.build(ex_in, ex_out, *ex_sems):
                    rc.start()
            st_ref[...] = jnp.zeros_like(st_ref)
            cp.wait()

        dxm, off = None, 0
        for g_ref, wd in zip(group_refs, widths):
            part = _dot_nt(g_ref[...], w_vm[:, off:off + wd])
            dxm = part if dxm is None else dxm + part
            off += wd

        xv = x_ref[...]
        r = lax.rsqrt(jnp.mean(xv * xv, axis=-1, keepdims=True) + EPS)
        xh = xv * r
        nwv = nw_ref[...]
        dxn = dxm * (1.0 + sc_ref[...])
        st_ref[0:1, :] += jnp.sum(dxm, axis=0, keepdims=True)
        st_ref[1:2, :] += jnp.sum(dxm * (xh * nwv), axis=0, keepdims=True)
        st_ref[2:3, :] += jnp.sum(dxn * xh, axis=0, keepdims=True)
        if with_dx:
            dxh = dxn * nwv
            gx_ref[...] = dx1_ref[...] + r * (dxh - xh * jnp.mean(dxh * xh, axis=-1, keepdims=True))

        if exchange is not None:
            @pl.when(i == nt - 1)
            def _():
                for rc in exchange.build(ex_in, ex_out, *ex_sems):
                    rc.wait()

    row = pl.BlockSpec((tm, D), lambda i: (i, 0))
    in_specs = [pl.BlockSpec((tm, wd), lambda i: (i, 0)) for wd in widths] + [ANY, row, _vec_spec(D), _vec_spec(D)]
    out_specs = [pl.BlockSpec((8, D), lambda i: (0, 0))]
    out_shape = [jax.ShapeDtypeStruct((8, D), F32)]
    args = list(groups) + [w, x, nw, sc]
    if with_dx:
        in_specs.append(row)
        out_specs.insert(0, row)
        out_shape.insert(0, jax.ShapeDtypeStruct((L, D), F32))
        args.append(dx1)
    res = pl.pallas_call(
        body, name=name, grid=(nt,),
        in_specs=in_specs + ex_in_specs, out_specs=out_specs + ex_out_specs, out_shape=out_shape + ex_shapes,
        scratch_shapes=[pltpu.VMEM((D, wtot), BF16), pltpu.SemaphoreType.DMA] + ex_scratch,
        compiler_params=_cparams(("arbitrary",), VMEM_LIMIT),
    )(*args, *ex_args)
    gx = res[0] if with_dx else None
    return (gx, res[n_out - 1], *res[n_out:])


DW_TN = 512
DW_RING = 4


def _dw_in(xmt, groups, cmt, dr_c, D, pair):
    L = xmt.shape[1]
    Lc = cmt.shape[1]
    Dh = D // 2
    tn = min(DW_TN, D)
    nblk = [g.shape[1] // tn for g in groups]
    starts = [sum(nblk[:g]) for g in range(len(groups))]
    ng = len(groups)
    nj = sum(nblk)
    rows_out = Dh if pair else D

    def body(*refs):
        xt_hbm = refs[0]
        group_refs = refs[1:1 + ng]
        ct_hbm, drc_ref, o_ref = refs[1 + ng:4 + ng]
        rest = refs[4 + ng:]
        if pair:
            ra_hbm, xt_vm, ct_vm, loc, ring, s_send, s_recv = rest
            pos = _position()
            sib = _peer(pos, 1)
        else:
            xt_vm, ct_vm, loc = rest
        j = pl.program_id(0)

        @pl.when(j == 0)
        def _():
            if pair:
                c = pos[2]
                other = pl.ds(pl.multiple_of((1 - c) * Dh, Dh), Dh)
                mine = pl.ds(pl.multiple_of(c * Dh, Dh), Dh)
                cps = [pltpu.make_async_copy(xt_hbm.at[other, :], xt_vm.at[0:Dh, :], loc.at[0]),
                       pltpu.make_async_copy(xt_hbm.at[mine, :], xt_vm.at[Dh:D, :], loc.at[1]),
                       pltpu.make_async_copy(ct_hbm.at[other, :], ct_vm.at[0:Dh, :], loc.at[2]),
                       pltpu.make_async_copy(ct_hbm.at[mine, :], ct_vm.at[Dh:D, :], loc.at[3])]
            else:
                cps = [pltpu.make_async_copy(xt_hbm, xt_vm, loc.at[0]), pltpu.make_async_copy(ct_hbm, ct_vm, loc.at[1])]
            for cp in cps:
                cp.start()
            for cp in cps:
                cp.wait()

        def send(slot):
            cols = pl.ds(pl.multiple_of(j * tn, 128), tn)
            return pltpu.make_async_remote_copy(src_ref=ring.at[slot], dst_ref=ra_hbm.at[:, cols],
                                                send_sem=s_send.at[slot], recv_sem=s_recv,
                                                device_id=sib, device_id_type=MESH)

        for g in range(ng):
            @pl.when((j >= starts[g]) & (j < starts[g] + nblk[g]))
            def _(g=g):
                acc = _dot(xt_vm[...], group_refs[g][...])
                if g == 1:
                    acc += _dot(ct_vm[...], drc_ref[...])
                if not pair:
                    o_ref[...] = acc
                    return
                o_ref[...] = acc[Dh:, :]
                slot = lax.rem(j, DW_RING)

                @pl.when(j >= DW_RING)
                def _():
                    send(slot).wait_send()

                ring[slot] = acc[0:Dh, :]
                send(slot).start()

        if pair:
            @pl.when(j == nj - 1)
            def _():
                pltpu.make_async_remote_copy(src_ref=ra_hbm, dst_ref=ra_hbm, send_sem=s_send.at[0], recv_sem=s_recv,
                                             device_id=sib, device_id_type=MESH).wait_recv()
                for slot in range(DW_RING):
                    send(slot).wait_send()

    def group_spec(g, rows):
        return pl.BlockSpec((rows, tn), lambda j: (0, jnp.clip(j - starts[g], 0, nblk[g] - 1)))

    out_specs = [pl.BlockSpec((rows_out, tn), lambda j: (0, j))]
    out_shape = [jax.ShapeDtypeStruct((rows_out, nj * tn), F32)]
    scratch = [pltpu.VMEM((D, L), BF16), pltpu.VMEM((D, Lc), BF16), pltpu.SemaphoreType.DMA((4,))]
    if pair:
        out_specs.append(ANY)
        out_shape.append(jax.ShapeDtypeStruct((Dh, nj * tn), F32))
        scratch += [pltpu.VMEM((DW_RING, Dh, tn), F32), pltpu.SemaphoreType.DMA((DW_RING,)), pltpu.SemaphoreType.DMA]
    return tuple(pl.pallas_call(
        body, name="dw_in", grid=(nj,),
        in_specs=[ANY] + [group_spec(g, L) for g in range(ng)] + [ANY, group_spec(1, Lc)],
        out_specs=out_specs, out_shape=out_shape, scratch_shapes=scratch,
        compiler_params=_cparams(("arbitrary",), VMEM_LIMIT),
    )(xmt, *groups, cmt, dr_c))


def _local_step(x, ctx, tgt, mod_x, mod_c, norm_w, conv_w8, conv_b, lg, gn_w, fw, project, csidx=None):
    L, D = x.shape
    sh_x, sc_x, g_x = mod_x[0:1], mod_x[1:2], mod_x[2:3]
    sh_c, sc_c = mod_c[0:1], mod_c[1:2]
    c2, s2 = _rope_tables(L)
    tab = _decay_tables(lg, D // DV)

    xm, xmt = _norm_mod(x, norm_w, sc_x, sh_x, "norm_mod_x")
    cm, cmt = _norm_mod(ctx, norm_w, sc_c, sh_c, "norm_mod_ctx")
    reduce = csidx is not None
    p, qr, kr, w_in, w3 = project(xm, c2, s2)
    pc, pqk_c = _in_proj(cm, w_in, "in_proj_ctx", QK_BLOCK, 2)
    s0 = _ctx_states(pc, pqk_c, lg, D)
    sf_prev, sb_prev = _ret_states(kr, p, s0, tab, D)
    o, yb = _ret_out(qr, kr, p, sf_prev, sb_prev, gn_w, tab, D)
    dx1, dya, do, dzb, dgab, dw3, st_mid = _mid(p, yb, o, x, tgt, w3, g_x, fw, conv_w8, conv_b, gn_w, D)
    dw3_5 = dw3.reshape(3, N_SHARD, 2, D // 8, D)
    dconv, st_conv, *ra_3 = _conv_bwd(dya, p, conv_w8, conv_b, D, _pair_exchange_w3(dw3_5) if reduce else None)
    dsf, dsb, ds0 = _ret_bwd_states(qr, do, tab, D)
    cs_3 = _sum_pair_w3(csidx[0:1], dw3_5, ra_3[0]) if reduce else None
    dret, st_lg, *rb_3 = _ret_bwd_main(qr, kr, p, do, sf_prev, sb_prev, dsf, dsb, c2, s2, tab, D,
                                       _chips_exchange_w3(cs_3) if reduce else None)
    g_3 = _sum_chips_w3(csidx, cs_3, rb_3[0]) if reduce else dw3
    dret_c, st_lgc = _ctx_bwd(pc, pqk_c, ds0, lg, D)
    groups = (dconv, dret, dzb, dgab)
    _, st_c = _dxm((dret_c,), 4, w_in, ctx, norm_w, sc_c, None, "dxm_ctx")
    return groups, dret_c, xmt, cmt, dx1, sc_x, w_in, g_3, (st_mid, st_conv, st_lg, st_lgc, st_c)


CHIP_FLIPS = (4, 2, 6)
ANY = pl.BlockSpec(memory_space=pl.ANY)
VMEM_FULL = pl.BlockSpec(memory_space=pltpu.VMEM)


def _position():
    return lax.axis_index("x"), lax.axis_index("y"), lax.axis_index("c")


def _peer(pos, k):
    x, y, c = pos
    return (1 - x if k & 4 else x, 1 - y if k & 2 else y, 1 - c if k & 1 else c)


def _dev_id(pos):
    return 4 * pos[0] + 2 * pos[1] + pos[2]


def _shard_of(pos):
    return 2 * pos[0] + pos[1]


def _remote(src, dst, send_sems, recv_sems, idx, to):
    return pltpu.make_async_remote_copy(src_ref=src, dst_ref=dst, send_sem=send_sems.at[idx],
                                        recv_sem=recv_sems.at[idx], device_id=to, device_id_type=MESH)


def _dot_f32(a, b):
    return jnp.dot(a, b, precision=lax.Precision.HIGHEST, preferred_element_type=F32)


def _silu(x):
    return x * _sigmoid(x)


def _fwd_small(c8, cctx8, ada_w, ada_b, conv_w8):
    D = c8.shape[1]
    Wm = ada_w.shape[1]
    Dq = conv_w8.shape[1]

    def body(c_ref, cc_ref, aw_ref, ab_ref, cw_ref, act_ref, mod_ref, cwf_ref,
             cbuf, pmine, pbuf, wbuf, s_c, r_c, s_p, r_p, s_w, r_w):
        pos = _position()
        me, s = _dev_id(pos), _shard_of(pos)
        cbuf[me] = c_ref[...]
        wbuf[s] = cw_ref[...]
        sends = [_remote(c_ref, cbuf.at[me], s_c, r_c, k - 1, _peer(pos, k)) for k in range(1, 8)]
        sends += [_remote(cw_ref, wbuf.at[s], s_w, r_w, j, _peer(pos, k)) for j, k in enumerate(CHIP_FLIPS)]
        for cp in sends:
            cp.start()
        for k in range(1, 8):
            _remote(c_ref, cbuf.at[_dev_id(_peer(pos, k))], s_c, r_c, k - 1, _peer(pos, k)).wait_recv()
        for d in range(N_DEV):
            act_ref[d:d + 1, :] = _silu(cbuf[d, 0:1, :])
        act_ref[8:9, :] = _silu(cc_ref[0:1, :])
        act_ref[9:16, :] = jnp.zeros((7, D), F32)
        part = _dot_f32(act_ref[...], aw_ref[...])
        pmine[...] = part
        pbuf[s] = part
        psend = [_remote(pmine, pbuf.at[s], s_p, r_p, j, _peer(pos, k)) for j, k in enumerate(CHIP_FLIPS)]
        for cp in psend:
            cp.start()
        for j, k in enumerate(CHIP_FLIPS):
            t = _shard_of(_peer(pos, k))
            _remote(pmine, pbuf.at[t], s_p, r_p, j, _peer(pos, k)).wait_recv()
            _remote(cw_ref, wbuf.at[t], s_w, r_w, j, _peer(pos, k)).wait_recv()
        for t in range(N_SHARD):
            mod_ref[:, t * Wm:(t + 1) * Wm] = pbuf[t] + ab_ref[:, t * Wm:(t + 1) * Wm]
            cwf_ref[:, t * Dq:(t + 1) * Dq] = wbuf[t]
        for cp in sends + psend:
            cp.wait_send()

    return pl.pallas_call(
        body, name="fwd_small",
        in_specs=[VMEM_FULL] * 5, out_specs=[VMEM_FULL] * 3,
        out_shape=[jax.ShapeDtypeStruct((16, D), F32), jax.ShapeDtypeStruct((16, 3 * D), F32),
                   jax.ShapeDtypeStruct((8, D), F32)],
        scratch_shapes=[pltpu.VMEM((N_DEV, 8, D), F32), pltpu.VMEM((16, Wm), F32),
                        pltpu.VMEM((N_SHARD, 16, Wm), F32), pltpu.VMEM((N_SHARD, 8, Dq), F32),
                        pltpu.SemaphoreType.DMA((7,)), pltpu.SemaphoreType.DMA((7,)),
                        pltpu.SemaphoreType.DMA((3,)), pltpu.SemaphoreType.DMA((3,)),
                        pltpu.SemaphoreType.DMA((3,)), pltpu.SemaphoreType.DMA((3,))],
        compiler_params=_cparams(None, VMEM_LIMIT),
    )(c8, cctx8, ada_w, ada_b, conv_w8)


AG_CHUNKS = 3


def _ag_in_proj(xm, w_in_s, w3_s, c2, s2):
    L, D = xm.shape
    Wc = w_in_s.shape[1]
    Wq = Wc // AG_CHUNKS
    Dh = D // 2
    Do = w3_s[0].shape[1]
    TM = min(1024, L // 4)
    NT = L // TM
    RC = min(128, Dh)
    NQ = AG_CHUNKS
    order = [(q, j) for q in range(NQ) for j in (0, 1)] + [(q, 2) for q in range(NQ)]

    def body(xm_ref, wi_hbm, wa_ref, wb_ref, wo_ref, c_ref, s_ref, p_hbm, qr_hbm, kr_hbm, fi_hbm, f3_hbm,
             w_vm, cast_buf, s3, stage, qk_stage, ici_s, ici_r, d2d_s, d2d_r, w3_s_, w3_r_, fw_s, fw_r,
             loc, out_sem, qk_sem):
        pos = _position()
        c = pos[2]
        s = _shard_of(pos)
        sib = _peer(pos, 1)
        mine = pl.ds(pl.multiple_of(c * Dh, Dh), Dh)
        other = pl.ds(pl.multiple_of((1 - c) * Dh, Dh), Dh)

        def cast_half(hf):
            def step(i, carry):
                rows = pl.ds(pl.multiple_of(hf * Dh + i * RC, RC), RC)
                cp = pltpu.make_async_copy(wi_hbm.at[rows, :], cast_buf, loc.at[0])
                cp.start()
                cp.wait()
                for q in range(NQ):
                    w_vm[0, q, rows, :] = cast_buf[:, q * Wq:(q + 1) * Wq].astype(BF16)
                return carry
            lax.fori_loop(0, Dh // RC, step, 0)

        def abs_col(t, q):
            return pl.ds(pl.multiple_of(t * Wc + q * Wq, 128), Wq)

        cast_half(c)
        for a, w_ref in enumerate((wa_ref, wb_ref, wo_ref)):
            s3[a] = w_ref[...].astype(BF16)
        sends = [_remote(w_vm.at[0, q, mine, :], w_vm.at[1 + j, q, mine, :], ici_s, ici_r, q * 3 + j,
                         _peer(pos, CHIP_FLIPS[j])) for q, j in order if j < 2]
        for j, k in enumerate(CHIP_FLIPS):
            sends.append(_remote(s3.at[:, c], f3_hbm.at[:, s, c], w3_s_, w3_r_, j, _peer(pos, k)))
        for cp in sends:
            cp.start()
        cast_half(1 - c)
        local = [pltpu.make_async_copy(s3, f3_hbm.at[:, s], loc.at[1])]
        local += [pltpu.make_async_copy(w_vm.at[0, q], fi_hbm.at[:, abs_col(s, q)], loc.at[2 + q]) for q in range(NQ)]
        for cp in local:
            cp.start()

        def out_copy(slot, rows, cols):
            return pltpu.make_async_copy(stage.at[slot], p_hbm.at[rows, cols], out_sem.at[slot])

        def block(r, q, t, first):
            cols = abs_col(t, q)

            def row_tile(rt, carry):
                rows = pl.ds(pl.multiple_of(rt * TM, TM), TM)
                acc = _dot(xm_ref[rows, :], w_vm[r, q])
                slot = lax.rem(rt, 2)

                @pl.when(rt >= 2 if first else rt >= 0)
                def _():
                    out_copy(slot, rows, cols).wait()

                stage[slot] = acc.astype(BF16)
                out_copy(slot, rows, cols).start()

                def rotary(lo, scale, dst_hbm):
                    c, s = c_ref[rows, :], s_ref[rows, :]
                    for pr in range(Dh // 128):
                        tq = acc[:, lo + pr * 128:lo + (pr + 1) * 128] * scale
                        qk_stage[:, pr * 128:(pr + 1) * 128] = (tq * c + _swap_halves(tq) * s).astype(BF16)
                    cp = pltpu.make_async_copy(qk_stage, dst_hbm.at[rows, :], qk_sem)
                    cp.start()
                    cp.wait()

                if q == NQ - 1:
                    @pl.when(t == 1)
                    def _():
                        rotary(Wq - Dh, 1.0, qr_hbm)
                if q == 0:
                    @pl.when(t == 2)
                    def _():
                        rotary(0, K_SCALE, kr_hbm)
                return carry

            lax.fori_loop(0, NT, row_tile, 0)

        for q in range(NQ):
            block(0, q, s, q == 0)
        passed = []
        for q, j in order:
            r, idx = 1 + j, q * 3 + j
            t = _shard_of(_peer(pos, CHIP_FLIPS[j]))
            landed = w_vm.at[r, q, mine, :]
            if j == 2:
                _remote(landed, landed, fw_s, fw_r, q, sib).wait_recv()
            else:
                _remote(landed, landed, ici_s, ici_r, idx, sib).wait_recv()
                if j == q % 2:
                    on = _remote(landed, w_vm.at[3, q, mine, :], fw_s, fw_r, q, _peer(pos, CHIP_FLIPS[1 - j]))
                    on.start()
                    passed.append(on)
            fwd = _remote(w_vm.at[r, q, mine, :], w_vm.at[r, q, mine, :], d2d_s, d2d_r, idx, sib)
            fwd.start()
            passed.append(fwd)
            _remote(w_vm.at[r, q, other, :], w_vm.at[r, q, other, :], d2d_s, d2d_r, idx, sib).wait_recv()
            block(r, q, t, False)
            cp = pltpu.make_async_copy(w_vm.at[r, q], fi_hbm.at[:, abs_col(t, q)], loc.at[2 + NQ + idx])
            cp.start()
            local.append(cp)
        for j, k in enumerate(CHIP_FLIPS):
            t = _shard_of(_peer(pos, k))
            _remote(s3.at[:, c], f3_hbm.at[:, t, c], w3_s_, w3_r_, j, sib).wait_recv()
            fwd = _remote(f3_hbm.at[:, t, c], f3_hbm.at[:, t, c], w3_s_, w3_r_, 3 + j, sib)
            fwd.start()
            passed.append(fwd)
        for j, k in enumerate(CHIP_FLIPS):
            t = _shard_of(_peer(pos, k))
            _remote(s3.at[:, c], f3_hbm.at[:, t, 1 - c], w3_s_, w3_r_, 3 + j, sib).wait_recv()
        for cp in sends + passed:
            cp.wait_send()
        for cp in local:
            cp.wait()
        for slot in range(2):
            out_copy(slot, pl.ds(0, TM), abs_col(s, 0)).wait()

    n_loc = 2 + NQ + 3 * NQ
    return pl.pallas_call(
        body, name="ag_in_proj",
        in_specs=[VMEM_FULL, ANY, VMEM_FULL, VMEM_FULL, VMEM_FULL, VMEM_FULL, VMEM_FULL], out_specs=[ANY] * 5,
        out_shape=[jax.ShapeDtypeStruct((L, N_SHARD * Wc), BF16),
                   jax.ShapeDtypeStruct((L, Dh), BF16), jax.ShapeDtypeStruct((L, Dh), BF16),
                   jax.ShapeDtypeStruct((D, N_SHARD * Wc), BF16), jax.ShapeDtypeStruct((3, N_SHARD, 2, Do, D), BF16)],
        scratch_shapes=[pltpu.VMEM((N_SHARD, NQ, D, Wq), BF16), pltpu.VMEM((RC, Wc), F32),
                        pltpu.VMEM((3, 2, Do, D), BF16), pltpu.VMEM((2, TM, Wq), BF16), pltpu.VMEM((TM, Dh), BF16),
                        pltpu.SemaphoreType.DMA((3 * NQ,)), pltpu.SemaphoreType.DMA((3 * NQ,)),
                        pltpu.SemaphoreType.DMA((3 * NQ,)), pltpu.SemaphoreType.DMA((3 * NQ,)),
                        pltpu.SemaphoreType.DMA((6,)), pltpu.SemaphoreType.DMA((6,)),
                        pltpu.SemaphoreType.DMA((NQ,)), pltpu.SemaphoreType.DMA((NQ,)),
                        pltpu.SemaphoreType.DMA((n_loc,)), pltpu.SemaphoreType.DMA((2,)), pltpu.SemaphoreType.DMA],
        compiler_params=_cparams(None, VMEM_LIMIT),
    )(xm, w_in_s, *w3_s, c2, s2)


def _pair_exchange_w3(dw3):
    _, _, _, Do, D = dw3.shape

    def build(ins, outs, send, recv):
        pos = _position()
        return [_remote(ins[0].at[:, :, 1 - pos[2]], outs[0], send, recv, 0, _peer(pos, 1))]

    return _Exchange((dw3,), (jax.ShapeDtypeStruct((3, N_SHARD, Do, D), F32),), 1, build)


def _sum_pair_in(dw_mine, ri):
    Dh, Wf = dw_mine.shape
    Wc = Wf // N_SHARD
    tr = min(256, Dh)

    def body(a_ref, b_ref, o_ref):
        o_ref[...] = (a_ref[...] + b_ref[...]).astype(BF16)

    return pl.pallas_call(
        body, name="sum_pair_in", grid=(Dh // tr, N_SHARD),
        in_specs=[pl.BlockSpec((tr, Wc), lambda i, t: (i, t)), pl.BlockSpec((tr, Wc), lambda i, t: (i, t))],
        out_specs=pl.BlockSpec((None, tr, Wc), lambda i, t: (t, i, 0)),
        out_shape=jax.ShapeDtypeStruct((N_SHARD, Dh, Wc), BF16),
        compiler_params=_cparams(("parallel", "parallel")),
    )(dw_mine, ri)


def _sum_pair_w3(cidx, dw3, r3):
    _, _, _, Do, D = dw3.shape

    def body(c_ref, a_ref, b_ref, o_ref):
        o_ref[...] = (a_ref[...] + b_ref[...]).astype(BF16)

    return pl.pallas_call(
        body, name="sum_pair_w3",
        grid_spec=pltpu.PrefetchScalarGridSpec(
            num_scalar_prefetch=1, grid=(3,),
            in_specs=[pl.BlockSpec((None, N_SHARD, None, Do, D), lambda a, c: (a, 0, c[0], 0, 0)),
                      pl.BlockSpec((None, N_SHARD, Do, D), lambda a, c: (a, 0, 0, 0))],
            out_specs=pl.BlockSpec((None, N_SHARD, Do, D), lambda a, c: (a, 0, 0, 0))),
        out_shape=jax.ShapeDtypeStruct((3, N_SHARD, Do, D), BF16),
        compiler_params=_cparams(("parallel",)),
    )(cidx, dw3, r3)


def _chips_exchange_in(cs_in):
    _, Dh, Wc = cs_in.shape

    def build(ins, outs, send, recv):
        pos = _position()
        return [_remote(ins[0].at[_shard_of(_peer(pos, k))], outs[0].at[j], send, recv, j, _peer(pos, k))
                for j, k in enumerate(CHIP_FLIPS)]

    return _Exchange((cs_in,), (jax.ShapeDtypeStruct((3, Dh, Wc), BF16),), 3, build)


def _chips_exchange_w3(cs_3):
    _, _, Do, D = cs_3.shape

    def build(ins, outs, send, recv):
        pos = _position()
        return [_remote(ins[0].at[:, _shard_of(_peer(pos, k))], outs[0].at[j], send, recv, j, _peer(pos, k))
                for j, k in enumerate(CHIP_FLIPS)]

    return _Exchange((cs_3,), (jax.ShapeDtypeStruct((3, 3, Do, D), BF16),), 3, build)


def _sum_chips_in(csidx, cs_in, rb_in):
    _, Dh, Wc = cs_in.shape
    tr = min(256, Dh)

    def body(s_ref, a_ref, b_ref, o_ref):
        acc = a_ref[...].astype(F32)
        for j in range(3):
            acc = acc + b_ref[j].astype(F32)
        o_ref[...] = acc

    return pl.pallas_call(
        body, name="sum_chips_in",
        grid_spec=pltpu.PrefetchScalarGridSpec(
            num_scalar_prefetch=1, grid=(Dh // tr,),
            in_specs=[pl.BlockSpec((None, tr, Wc), lambda i, s: (s[1], i, 0)),
                      pl.BlockSpec((3, tr, Wc), lambda i, s: (0, i, 0))],
            out_specs=pl.BlockSpec((None, tr, Wc), lambda i, s: (s[0], i, 0))),
        out_shape=jax.ShapeDtypeStruct((2, Dh, Wc), F32),
        compiler_params=_cparams(("parallel",)),
    )(csidx, cs_in, rb_in)


def _sum_chips_w3(csidx, cs_3, rb_3):
    _, _, Do, D = cs_3.shape

    def body(s_ref, a_ref, b_ref, o_ref):
        acc = a_ref[...].astype(F32)
        for j in range(3):
            acc = acc + b_ref[j].astype(F32)
        o_ref[...] = acc

    return pl.pallas_call(
        body, name="sum_chips_w3",
        grid_spec=pltpu.PrefetchScalarGridSpec(
            num_scalar_prefetch=1, grid=(3,),
            in_specs=[pl.BlockSpec((None, None, Do, D), lambda a, s: (a, s[1], 0, 0)),
                      pl.BlockSpec((3, None, Do, D), lambda a, s: (0, a, 0, 0))],
            out_specs=pl.BlockSpec((None, None, Do, D), lambda a, s: (a, s[0], 0, 0))),
        out_shape=jax.ShapeDtypeStruct((3, 2, Do, D), F32),
        compiler_params=_cparams(("parallel",)),
    )(csidx, cs_3, rb_3)


def _adam_math(w, g, m, v):
    m = ADAM_B1 * m + (1.0 - ADAM_B1) * g
    v = ADAM_B2 * v + (1.0 - ADAM_B2) * (g * g)
    m_hat = m / (1.0 - ADAM_B1 ** ADAM_STEP)
    v_hat = v / (1.0 - ADAM_B2 ** ADAM_STEP)
    delta = -ADAM_LR * (m_hat / (jnp.sqrt(v_hat) + ADAM_EPS) + ADAM_WD * w)
    return delta, m, v


def _adamw(w, g, m, v, name):
    R, C = w.shape
    tr = min(128, R)

    def body(w_ref, g_ref, m_ref, v_ref, d_ref, nm_ref, nv_ref):
        d_ref[...], nm_ref[...], nv_ref[...] = _adam_math(w_ref[...], g_ref[...], m_ref[...], v_ref[...])

    blk = pl.BlockSpec((tr, C), lambda i: (i, 0))
    return pl.pallas_call(
        body, name=name, grid=(R // tr,), in_specs=[blk] * 4, out_specs=[blk] * 3,
        out_shape=[jax.ShapeDtypeStruct((R, C), F32)] * 3,
        compiler_params=_cparams(("parallel",), VMEM_LIMIT),
    )(w, g, m, v)


def _adamw3(ws, g3, ms, vs):
    R, C = ws[0].shape

    def body(*refs):
        w_refs, m_refs, v_refs = refs[0:3], refs[3:6], refs[6:9]
        g_ref, outs = refs[9], refs[10:]
        for a in range(3):
            @pl.when(pl.program_id(0) == a)
            def _(a=a):
                res = _adam_math(w_refs[a][...], g_ref[...], m_refs[a][...], v_refs[a][...])
                for q in range(3):
                    outs[3 * a + q][...] = res[q]

    full = pl.BlockSpec((R, C), lambda a: (0, 0))
    res = pl.pallas_call(
        body, name="adamw_w3", grid=(3,),
        in_specs=[full] * 9 + [pl.BlockSpec((None, R, C), lambda a: (a, 0, 0))], out_specs=[full] * 9,
        out_shape=[jax.ShapeDtypeStruct((R, C), F32)] * 9,
        compiler_params=_cparams(("arbitrary",), VMEM_LIMIT),
    )(*ws, *ms, *vs, g3)
    return res[0:3], res[3:6], res[6:9]


SMALL_ROWS = ("c_ctx", "norm_w", "conv_b", "gn_w", "final_norm_w")


def _bwd_small(stats, ada_w, Dq, gh_in, gh_3):
    D = stats[0].shape[1]
    Wm = ada_w.shape[1]

    def body(stx, stm, stc, stv, stl, stlc, aw_ref, gi_in, g3_in, tot_ref, dm_sh, gcw, da_ref, loss_ref, gi_ref, g3_ref,
             vec_ref, vbuf, dm, amine, abuf, s_v, r_v, s_a, r_a, s_g, r_g):
        pos = _position()
        me, s = _dev_id(pos), _shard_of(pos)
        c, sib = pos[2], _peer(pos, 1)
        halves = [_remote(gi_in.at[c], gi_ref.at[c], s_g, r_g, 0, sib),
                  _remote(g3_in.at[:, c], g3_ref.at[:, c], s_g, r_g, 1, sib)]
        for cp in halves:
            cp.start()
        vec_ref[...] = jnp.zeros_like(vec_ref)
        vec_ref[0:2, :] = stx[0:2, :]
        vec_ref[2:3, :] = stm[1:2, :]
        vec_ref[3:5, :] = stc[0:2, :]
        vec_ref[5:6, :] = stx[2:3, :] + stc[2:3, :]
        vec_ref[6:7, :] = stv[3:4, :]
        vec_ref[7:8, :] = stm[3:4, :]
        vec_ref[8:9, :] = stm[0:1, :]
        vec_ref[9:12, :] = stv[0:3, :]
        vec_ref[12:14, 0:128] = stl[0:2, :] + stlc[0:2, :]
        vec_ref[14:15, :] = stm[2:3, :]
        vbuf[me] = vec_ref[...]
        sends = [_remote(vec_ref, vbuf.at[me], s_v, r_v, k - 1, _peer(pos, k)) for k in range(1, 8)]
        for cp in sends:
            cp.start()
        for k in range(1, 8):
            _remote(vec_ref, vbuf.at[_dev_id(_peer(pos, k))], s_v, r_v, k - 1, _peer(pos, k)).wait_recv()
        tot = vbuf[0]
        for d in range(1, N_DEV):
            tot = tot + vbuf[d]
        loss_ref[...] = jnp.zeros((8, 128), F32) + (0.5 / D) * _sum_all(tot[14:15, :])
        dm[...] = jnp.zeros_like(dm)
        for d in range(N_DEV):
            for r in range(3):
                dm[d:d + 1, r * D:(r + 1) * D] = vbuf[d, r:r + 1, :]
        dm[8:9, 0:D] = tot[3:4, :]
        dm[8:9, D:2 * D] = tot[4:5, :]
        for t in range(N_SHARD):
            @pl.when(s == t)
            def _(t=t):
                dm_sh[...] = dm[:, t * Wm:(t + 1) * Wm]
                gcw[...] = tot[9:12, t * Dq:(t + 1) * Dq]
        tot_ref[...] = tot
        part = lax.dot_general(dm_sh[8:16, :], aw_ref[...], (((1,), (1,)), ((), ())),
                               precision=lax.Precision.HIGHEST, preferred_element_type=F32)
        amine[...] = part
        abuf[s] = part
        asend = [_remote(amine, abuf.at[s], s_a, r_a, j, _peer(pos, k)) for j, k in enumerate(CHIP_FLIPS)]
        for cp in asend:
            cp.start()
        for j, k in enumerate(CHIP_FLIPS):
            _remote(amine, abuf.at[_shard_of(_peer(pos, k))], s_a, r_a, j, _peer(pos, k)).wait_recv()
        da = abuf[0]
        for t in range(1, N_SHARD):
            da = da + abuf[t]
        da_ref[...] = da
        _remote(gi_in.at[1 - c], gi_ref.at[1 - c], s_g, r_g, 0, sib).wait_recv()
        _remote(g3_in.at[:, 1 - c], g3_ref.at[:, 1 - c], s_g, r_g, 1, sib).wait_recv()
        for cp in sends + asend + halves:
            cp.wait_send()

    row = lambda *shape: jax.ShapeDtypeStruct(shape, F32)
    return pl.pallas_call(
        body, name="bwd_small",
        in_specs=[VMEM_FULL] * 7 + [ANY, ANY], out_specs=[VMEM_FULL] * 5 + [ANY, ANY],
        input_output_aliases={7: 5, 8: 6},
        out_shape=[row(16, D), row(16, Wm), row(3, Dq), row(8, D), row(8, 128), row(*gh_in.shape), row(*gh_3.shape)],
        scratch_shapes=[pltpu.VMEM((16, D), F32), pltpu.VMEM((N_DEV, 16, D), F32), pltpu.VMEM((16, 3 * D), F32),
                        pltpu.VMEM((8, D), F32), pltpu.VMEM((N_SHARD, 8, D), F32),
                        pltpu.SemaphoreType.DMA((7,)), pltpu.SemaphoreType.DMA((7,)),
                        pltpu.SemaphoreType.DMA((3,)), pltpu.SemaphoreType.DMA((3,)),
                        pltpu.SemaphoreType.DMA((2,)), pltpu.SemaphoreType.DMA((2,))],
        compiler_params=_cparams(None, VMEM_LIMIT),
    )(*stats, ada_w, gh_in, gh_3)


def _small_update(tot, dm_sh, gcw, da, act, p_row, p_ab, p_cw, p_dl):
    D = act.shape[1]
    Wm = dm_sh.shape[1]
    Dq = gcw.shape[1]

    def body(tot_ref, dm_ref, gcw_ref, da_ref, act_ref, prow, pab, pcw, pdl, gaw_ref, *outs):
        o_q = [outs[8 * q:8 * (q + 1)] for q in range(4)]
        tot = tot_ref[...]
        gaw_ref[...] = lax.dot_general(act_ref[...], dm_ref[...], (((0,), (0,)), ((), ())),
                                       precision=lax.Precision.HIGHEST, preferred_element_type=F32)
        cc = prow[0, 0:1, :]
        sg = _sigmoid(cc)
        g_cctx = da_ref[0:1, :] * (sg * (1.0 + cc * (1.0 - sg)))

        def place_all(o, val):
            o[...] = val

        def emit(k, w, g, m, v, place=place_all):
            for q, val in enumerate((g,) + _adam_math(w, g, m, v)):
                place(o_q[q][k], val)

        g_rows = [g_cctx, tot[5:6, :], tot[6:7, :], tot[7:8, :], tot[8:9, :]]
        for k, g in enumerate(g_rows):
            emit(k, prow[0, k:k + 1, :], g, prow[1, k:k + 1, :], prow[2, k:k + 1, :])

        def place_ab(o, val):
            for r in range(3):
                o[0:1, r * D:(r + 1) * D] = val[r:r + 1, :]

        g_ab = jnp.concatenate([tot[0:1, :] + tot[3:4, :], tot[1:2, :] + tot[4:5, :], tot[2:3, :]], axis=0)
        emit(5, pab[0], g_ab, pab[1], pab[2], place_ab)
        emit(6, pcw[0], gcw_ref[...], pcw[1], pcw[2])
        g_dl = jnp.concatenate([tot[12:14, 0:128] * _sigmoid(-pdl[0, 0:2, :]), jnp.zeros((6, 128), F32)], axis=0)
        emit(7, pdl[0], g_dl, pdl[1], pdl[2])

    row = lambda *shape: jax.ShapeDtypeStruct(shape, F32)
    per_q = [row(1, D)] * 5 + [row(1, 3 * D), row(3, Dq), row(8, 128)]
    res = pl.pallas_call(
        body, name="small_update",
        in_specs=[VMEM_FULL] * 9, out_specs=[VMEM_FULL] * 33,
        out_shape=[row(D, Wm)] + per_q * 4,
        compiler_params=_cparams(None, VMEM_LIMIT),
    )(tot, dm_sh, gcw, da, act, p_row, p_ab, p_cw, p_dl)
    return res[0], [res[1 + 8 * q:1 + 8 * (q + 1)] for q in range(4)]


def _pad_rows(a, rows=8):
    return jnp.pad(a, ((0, rows - a.shape[0]), (0, 0)))


def kernel(x, c, ctx, c_ctx, norm_w, ada_w, ada_b, w_in, conv_w, conv_b, decay_logit, gn_w, w_a, w_b, w_out, final_norm_w, loss_target, m_c_ctx, m_norm_w, m_ada_w, m_ada_b, m_w_in, m_conv_w, m_conv_b, m_decay_logit, m_gn_w, m_w_a, m_w_b, m_w_out, m_final_norm_w, v_c_ctx, v_norm_w, v_ada_w, v_ada_b, v_w_in, v_conv_w, v_conv_b, v_decay_logit, v_gn_w, v_w_a, v_w_b, v_w_out, v_final_norm_w):
    L, D = x.shape[1], x.shape[2]
    H = D // DV
    Wc = w_in.shape[2]
    Do = D // 8
    pos = _position()
    me = _dev_id(pos)
    cidx = jnp.reshape(pos[2], (1,)).astype(jnp.int32)
    sidx = jnp.reshape(_shard_of(pos), (1,)).astype(jnp.int32)

    act, mod, conv_w8 = _fwd_small(_pad_rows(c), _pad_rows(c_ctx[None]), ada_w[0], ada_b, _pad_rows(conv_w[0]))
    mod_x = lax.dynamic_slice_in_dim(mod, me, 1, axis=0).reshape(3, D)
    mod_c = mod[8].reshape(3, D)
    lg = jax.nn.log_sigmoid(decay_logit[0])

    w3_s = tuple(w[0].reshape(2, Do, D) for w in (w_a, w_b, w_out))

    def project(xm, c2, s2):
        p, qr, kr, w_in_full, w3_full = _ag_in_proj(xm, w_in[0], w3_s, c2, s2)
        return p, qr, kr, w_in_full, w3_full.reshape(3, D, D)

    csidx = jnp.concatenate([cidx, sidx])
    groups, dret_c, xmt, cmt, dx1, sc_x, w_in_full, gh_3, sts = _local_step(
        x[0], ctx[0], loss_target[0], mod_x, mod_c, norm_w, conv_w8, conv_b, lg, gn_w, final_norm_w[None],
        project, csidx)
    st_mid, st_conv, st_lg, st_lgc, st_c = sts

    dw_mine, ra_in = _dw_in(xmt, groups, cmt, dret_c, D, True)
    cs_in = _sum_pair_in(dw_mine, ra_in)
    grad_x, st_x, rb_in = _dxm(groups, 0, w_in_full, x[0], norm_w, sc_x, dx1, "dxm_x", _chips_exchange_in(cs_in))
    gh_in = _sum_chips_in(csidx, cs_in, rb_in)

    zeros3 = jnp.zeros((3, D), F32)
    p_row = jnp.concatenate(
        [r for t in ((c_ctx[None], norm_w, conv_b, gn_w, final_norm_w[None], zeros3),
                     (m_c_ctx[None], m_norm_w, m_conv_b, m_gn_w, m_final_norm_w[None], zeros3),
                     (v_c_ctx[None], v_norm_w, v_conv_b, v_gn_w, v_final_norm_w[None], zeros3)) for r in t],
        axis=0).reshape(3, 8, D)
    p_ab = jnp.concatenate([ada_b, m_ada_b, v_ada_b], axis=0).reshape(3, 3, D)
    p_cw = jnp.concatenate([conv_w, m_conv_w, v_conv_w], axis=0)
    p_dl = jnp.pad(jnp.concatenate([decay_logit, m_decay_logit, v_decay_logit], axis=0), ((0, 0), (0, 6), (0, 128 - H)))
    tot, dm_sh, gcw, da, loss_t, g_in, g_3 = _bwd_small((st_x, st_mid, st_c, st_conv, st_lg, st_lgc), ada_w[0],
                                                        conv_w.shape[2], gh_in, gh_3)
    g_w_in = g_in.reshape(D, Wc)
    g_3 = g_3.reshape(3, D // 4, D)
    g_ada_w, small = _small_update(tot, dm_sh, gcw, da, act, p_row, p_ab, p_cw, p_dl)

    upd_in = _adamw(w_in[0], g_w_in, m_w_in[0], v_w_in[0], "adamw_w_in")
    upd_ada = _adamw(ada_w[0], g_ada_w, m_ada_w[0], v_ada_w[0], "adamw_ada_w")
    upd_a, upd_b, upd_o = _adamw3((w_a[0], w_b[0], w_out[0]), g_3, (m_w_a[0], m_w_b[0], m_w_out[0]),
                                  (v_w_a[0], v_w_b[0], v_w_out[0]))

    def leaves(q):
        big = lambda g, upd: (g if q == 0 else upd[q - 1])[None]
        r_cctx, r_norm, r_convb, r_gn, r_fnorm, r_ab, r_cw, r_dl = small[q]
        return [r_cctx.reshape(D), r_norm, big(g_ada_w, upd_ada), r_ab, big(g_w_in, upd_in),
                r_cw[None], r_convb, r_dl[0:2, 0:H][None], r_gn,
                big(g_3[0], upd_a), big(g_3[1], upd_b), big(g_3[2], upd_o), r_fnorm.reshape(D)]

    loss = loss_t[0, 0]
    return (loss, grad_x[None], *leaves(0), *leaves(1), *leaves(2), *leaves(3))
```

```python
from typing import Callable, NamedTuple

import jax
import jax.numpy as jnp
from jax import lax
from jax.experimental import pallas as pl
from jax.experimental.pallas import tpu as pltpu

F32 = jnp.float32
BF16 = jnp.bfloat16
MESH = pl.DeviceIdType.MESH

CHUNK = 128
RET_CPB = 4
DV = 128
DK = 64
GRID_W = 64
ROPE_BASE = 10000.0
EPS = 1e-6
K_SCALE = DK ** -0.5
N_SHARD = 4
N_DEV = 8

ADAM_LR = 0.001
ADAM_B1 = 0.9
ADAM_B2 = 0.999
ADAM_EPS = 1e-08
ADAM_WD = 0.01
ADAM_STEP = 10

VMEM_LIMIT = 56 * 1024 * 1024


def _cparams(sem=None, vmem=None):
    kw = {}
    if sem is not None:
        kw["dimension_semantics"] = sem
    if vmem is not None:
        kw["vmem_limit_bytes"] = vmem
    return pltpu.CompilerParams(**kw)


def _dot(a, b):
    return jnp.dot(a, b, preferred_element_type=F32)


def _dot_nt(a, b):
    return lax.dot_general(a, b, (((1,), (1,)), ((), ())), preferred_element_type=F32)


def _dot_tn(a, b):
    return lax.dot_general(a, b, (((0,), (0,)), ((), ())), preferred_element_type=F32)


def _sigmoid(x):
    return 1.0 / (1.0 + jnp.exp(-x))


def _sum_all(x):
    return jnp.sum(jnp.sum(x, axis=1, keepdims=True), axis=0, keepdims=True)


def _swap_halves(t):
    n = t.shape[1]
    lane = lax.broadcasted_iota(jnp.int32, t.shape, 1)
    low = (lane & 32) == 0
    return jnp.where(low, pltpu.roll(t, n - 32, 1), pltpu.roll(t, 32, 1))


def _vec_spec(d):
    return pl.BlockSpec((1, d), lambda *a: (0, 0))


def _norm_mod(x, nw, sc, sh, name):
    L, D = x.shape
    tl = min(256, L)

    def body(x_ref, nw_ref, sc_ref, sh_ref, xm_ref, xmt_ref):
        xv = x_ref[...]
        r = lax.rsqrt(jnp.mean(xv * xv, axis=-1, keepdims=True) + EPS)
        xm = (xv * r * nw_ref[...]) * (1.0 + sc_ref[...]) + sh_ref[...]
        xm_b = xm.astype(BF16)
        xm_ref[...] = xm_b
        xmt_ref[...] = xm_b.T

    return pl.pallas_call(
        body, name=name, grid=(L // tl,),
        in_specs=[pl.BlockSpec((tl, D), lambda i: (i, 0)), _vec_spec(D), _vec_spec(D), _vec_spec(D)],
        out_specs=[pl.BlockSpec((tl, D), lambda i: (i, 0)), pl.BlockSpec((D, tl), lambda i: (0, i))],
        out_shape=[jax.ShapeDtypeStruct((L, D), BF16), jax.ShapeDtypeStruct((D, L), BF16)],
        compiler_params=_cparams(("parallel",)),
    )(x, nw, sc, sh)


QK_BLOCK, V_BLOCK = 4, 5


def _in_proj(xm, w, name, first=0, count=None):
    M, D = xm.shape
    count = w.shape[1] // D if count is None else count
    tm = min(1024, M)

    def body(a_ref, b_ref, o_ref, qk_ref):
        acc = _dot(a_ref[...], b_ref[...])
        o_ref[...] = acc.astype(o_ref.dtype)

        @pl.when(pl.program_id(1) == QK_BLOCK - first)
        def _():
            qk_ref[...] = acc

    return pl.pallas_call(
        body, name=name, grid=(M // tm, count),
        in_specs=[pl.BlockSpec((tm, D), lambda i, j: (i, 0)), pl.BlockSpec((D, D), lambda i, j: (0, first + j))],
        out_specs=[pl.BlockSpec((tm, D), lambda i, j: (i, j)), pl.BlockSpec((tm, D), lambda i, j: (i, 0))],
        out_shape=[jax.ShapeDtypeStruct((M, count * D), BF16), jax.ShapeDtypeStruct((M, D), F32)],
        compiler_params=_cparams(("parallel", "arbitrary")),
    )(xm, w)


def _halo_specs(tl, L, D, col):
    hb = tl // 16
    last = L // 16 - 1
    prev = pl.BlockSpec((16, D), lambda i: (jnp.maximum(i * hb - 1, 0), col))
    nxt = pl.BlockSpec((16, D), lambda i: (jnp.minimum((i + 1) * hb, last), col))
    return prev, nxt


def _shift_rows(u, above, below):
    tl = u.shape[0]
    row = lax.broadcasted_iota(jnp.int32, u.shape, 0)
    dn = jnp.where(row == 0, above, pltpu.roll(u, 1, 0))
    up = jnp.where(row == tl - 1, below, pltpu.roll(u, tl - 1, 0))
    return dn, up


def _rope_tables(L):
    pos = jnp.arange(L)
    row = (pos // GRID_W).astype(F32)
    col = (pos % GRID_W).astype(F32)
    nf = DK // 4
    inv = ROPE_BASE ** (-jnp.arange(nf, dtype=F32) / nf)
    ang = jnp.concatenate([row[:, None] * inv, col[:, None] * inv], axis=-1)
    cos, sin = jnp.cos(ang), jnp.sin(ang)
    return jnp.concatenate([cos, cos, cos, cos], axis=-1), jnp.concatenate([-sin, sin, -sin, sin], axis=-1)


def _smem_spec():
    return pl.BlockSpec(memory_space=pltpu.SMEM)


def _pair_select(e0, e1):
    row = lax.broadcasted_iota(jnp.int32, e0.shape, 0)
    return jnp.where(row < DK, e0, e1)


def _head_lane_mask(shape, e):
    lane = lax.broadcasted_iota(jnp.int32, shape, 1)
    return (lane < DK) if e == 0 else (lane >= DK)


def _ctx_states(pc, pqk_c, lg, D):
    Lc = pc.shape[0]
    H = D // DV

    def body(lg_ref, k_ref, v_ref, s_ref):
        m = lax.broadcasted_iota(jnp.int32, (Lc, DV), 0).astype(F32)
        for pr in range(H // 2):
            k2 = k_ref[:, pr * 128:(pr + 1) * 128].astype(F32) * K_SCALE
            res = [[None, None], [None, None]]
            for e in range(2):
                h = 2 * pr + e
                v = v_ref[:, h * DV:(h + 1) * DV]
                dec_f = jnp.exp(lg_ref[0, h] * (Lc - 1.0 - m))
                dec_b = jnp.exp(lg_ref[1, h] * m)
                res[0][e] = _dot_tn((k2 * dec_f).astype(BF16), v)
                res[1][e] = _dot_tn((k2 * dec_b).astype(BF16), v)
            s_ref[0, pr] = _pair_select(res[0][0], res[0][1])
            s_ref[1, pr] = _pair_select(res[1][0], res[1][1])

    return pl.pallas_call(
        body, name="ctx_states", grid=(1,),
        in_specs=[_smem_spec(), pl.BlockSpec((Lc, D // 2), lambda i: (0, 1)), pl.BlockSpec((Lc, D), lambda i: (0, 1))],
        out_specs=pl.BlockSpec((2, H // 2, 128, 128), lambda i: (0, 0, 0, 0)),
        out_shape=jax.ShapeDtypeStruct((2, H // 2, 128, 128), F32),
    )(lg, pqk_c, pc)


T_M, T_MT = 0, 1
T_MF1, T_MB1 = 2, 3
T_QF, T_QB = 4, 5
T_KF, T_KB = 6, 7


def _decay_tables(lg, H):
    def body(lg_ref, t_ref):
        h = pl.program_id(0)
        lgf, lgb = lg_ref[0, h], lg_ref[1, h]
        i = lax.broadcasted_iota(jnp.int32, (CHUNK, CHUNK), 0).astype(F32)
        j = lax.broadcasted_iota(jnp.int32, (CHUNK, CHUNK), 1).astype(F32)
        d = i - j
        mf = jnp.where(d > 0, jnp.exp(lgf * jnp.maximum(d, 0.0)), 0.0)
        mb = jnp.where(d < 0, jnp.exp(lgb * jnp.maximum(-d, 0.0)), 0.0)
        mf_t = jnp.where(d < 0, jnp.exp(lgf * jnp.maximum(-d, 0.0)), 0.0)
        mb_t = jnp.where(d > 0, jnp.exp(lgb * jnp.maximum(d, 0.0)), 0.0)
        diag = jnp.where(d == 0, 2.0, 0.0)
        t_ref[0, T_M] = mf + mb + diag
        t_ref[0, T_MT] = mf_t + mb_t + diag
        t_ref[0, T_MF1] = mf * d
        t_ref[0, T_MB1] = mb * (-d)
        t_ref[0, T_QF] = jnp.exp(lgf * (i + 1.0))
        t_ref[0, T_QB] = jnp.exp(lgb * (CHUNK - i))
        t_ref[0, T_KF] = jnp.exp(lgf * (CHUNK - 1.0 - i))
        t_ref[0, T_KB] = jnp.exp(lgb * i)

    return pl.pallas_call(
        body, name="decay_tables", grid=(H,), in_specs=[_smem_spec()],
        out_specs=pl.BlockSpec((1, 8, CHUNK, CHUNK), lambda h: (h, 0, 0, 0)),
        out_shape=jax.ShapeDtypeStruct((H, 8, CHUNK, CHUNK), F32),
    )(lg)


def _tab_spec(H):
    return pl.BlockSpec((H, 8, CHUNK, CHUNK), lambda n: (0, 0, 0, 0))


def _chunk_decay(tab_ref, h):
    return tab_ref[h, T_QF, CHUNK - 1:CHUNK, :], tab_ref[h, T_QB, 0:1, :]


def _ret_states(kr, p, s0, tab, D):
    L = kr.shape[0]
    H = D // DV
    N = L // CHUNK
    HP = H // 2

    def body(tab_ref, kf_ref, kb_ref, vf_ref, vb_ref, s0_ref, sf_out, sb_out, sf, sb):
        n = pl.program_id(0)

        @pl.when(n == 0)
        def _():
            sf[...] = s0_ref[0]
            sb[...] = s0_ref[1]

        for cc in range(RET_CPB):
            cf_, cb_ = cc, RET_CPB - 1 - cc
            rf, rb = slice(cf_ * CHUNK, (cf_ + 1) * CHUNK), slice(cb_ * CHUNK, (cb_ + 1) * CHUNK)
            sf_out[cf_] = sf[...]
            sb_out[cb_] = sb[...]
            for pr in range(HP):
                kf2 = kf_ref[rf, pr * 128:(pr + 1) * 128].astype(F32)
                kb2 = kb_ref[rb, pr * 128:(pr + 1) * 128].astype(F32)
                inc_f, inc_b, gf, gb = [], [], [], []
                for e in range(2):
                    h = 2 * pr + e
                    inc_f.append(_dot_tn((kf2 * tab_ref[h, T_KF]).astype(BF16), vf_ref[rf, h * DV:(h + 1) * DV]))
                    inc_b.append(_dot_tn((kb2 * tab_ref[h, T_KB]).astype(BF16), vb_ref[rb, h * DV:(h + 1) * DV]))
                    cf, cb = _chunk_decay(tab_ref, h)
                    gf.append(jnp.broadcast_to(cf, (128, 128)))
                    gb.append(jnp.broadcast_to(cb, (128, 128)))
                sf[pr] = _pair_select(gf[0], gf[1]) * sf[pr] + _pair_select(inc_f[0], inc_f[1])
                sb[pr] = _pair_select(gb[0], gb[1]) * sb[pr] + _pair_select(inc_b[0], inc_b[1])

    st = jax.ShapeDtypeStruct((N, HP, 128, 128), F32)
    R = RET_CPB * CHUNK
    NB = N // RET_CPB
    return _riding_call(
        body, None, NB, name="ret_states", args=(tab, kr, kr, p, p, s0),
        in_specs=[_tab_spec(H),
                  pl.BlockSpec((R, D // 2), lambda n: (n, 0)),
                  pl.BlockSpec((R, D // 2), lambda n: (NB - 1 - n, 0)),
                  pl.BlockSpec((R, D), lambda n: (n, 5)),
                  pl.BlockSpec((R, D), lambda n: (NB - 1 - n, 5)),
                  pl.BlockSpec((2, HP, 128, 128), lambda n: (0, 0, 0, 0))],
        out_specs=[pl.BlockSpec((RET_CPB, HP, 128, 128), lambda n: (n, 0, 0, 0)),
                   pl.BlockSpec((RET_CPB, HP, 128, 128), lambda n: (NB - 1 - n, 0, 0, 0))],
        out_shape=[st, st],
        scratch=[pltpu.VMEM((HP, 128, 128), F32), pltpu.VMEM((HP, 128, 128), F32)],
        cparams=_cparams(("arbitrary",)))


def _ret_out(qr, kr, p, sf_prev, sb_prev, gn_w, tab, D):
    L = qr.shape[0]
    H = D // DV
    N = L // CHUNK
    HP = H // 2

    def body(tab_ref, q_ref, k_ref, v_ref, zb_ref, sf_ref, sb_ref, gn_ref, o_ref, yb_ref):
        def chunk(cc, carry):
            rows = pl.ds(pl.multiple_of(cc * CHUNK, CHUNK), CHUNK)
            for pr in range(HP):
                q2 = q_ref[rows, pr * 128:(pr + 1) * 128]
                k2 = k_ref[rows, pr * 128:(pr + 1) * 128]
                sfp = sf_ref[cc, pr].astype(BF16)
                sbp = sb_ref[cc, pr].astype(BF16)
                for e in range(2):
                    h = 2 * pr + e
                    sl = slice(h * DV, (h + 1) * DV)
                    qm = jnp.where(_head_lane_mask(q2.shape, e), q2, jnp.zeros_like(q2))
                    a = (_dot_nt(qm, k2) * tab_ref[h, T_M]).astype(BF16)
                    qf = qm.astype(F32)
                    o = _dot(a, v_ref[rows, sl])
                    o += _dot((qf * tab_ref[h, T_QF]).astype(BF16), sfp)
                    o += _dot((qf * tab_ref[h, T_QB]).astype(BF16), sbp)
                    o_ref[rows, sl] = o
                    mu = jnp.mean(o, axis=-1, keepdims=True)
                    oc = o - mu
                    rstd = lax.rsqrt(jnp.mean(oc * oc, axis=-1, keepdims=True) + EPS)
                    zb = zb_ref[rows, sl].astype(F32)
                    yb_ref[rows, sl] = (zb * _sigmoid(zb) * (oc * rstd * gn_ref[:, sl])).astype(BF16)
            return carry

        lax.fori_loop(0, RET_CPB, chunk, 0)

    R = RET_CPB * CHUNK
    return _riding_call(
        body, None, N // RET_CPB, name="ret_out", args=(tab, qr, kr, p, p, sf_prev, sb_prev, gn_w),
        in_specs=[_tab_spec(H),
                  pl.BlockSpec((R, D // 2), lambda n: (n, 0)),
                  pl.BlockSpec((R, D // 2), lambda n: (n, 0)),
                  pl.BlockSpec((R, D), lambda n: (n, 5)),
                  pl.BlockSpec((R, D), lambda n: (n, 6)),
                  pl.BlockSpec((RET_CPB, HP, 128, 128), lambda n: (n, 0, 0, 0)),
                  pl.BlockSpec((RET_CPB, HP, 128, 128), lambda n: (n, 0, 0, 0)),
                  _vec_spec(D)],
        out_specs=[pl.BlockSpec((R, D), lambda n: (n, 0)), pl.BlockSpec((R, D), lambda n: (n, 0))],
        out_shape=[jax.ShapeDtypeStruct((L, D), F32), jax.ShapeDtypeStruct((L, D), BF16)],
        cparams=_cparams(("arbitrary",)))


def _mid(p, yb, o, x, tgt, w3, g, fw, conv_w, conv_b, gn_w, D):
    L = x.shape[0]
    H = D // DV
    tm = min(256, L)
    nt = L // tm

    def body(h_ref, bg_ref, cg_ref, za_ref, hp_ref, hn_ref, cp_ref, cn_ref, yb_ref, ga_ref, gb_ref, zb_ref, o_ref,
             x_ref, t_ref, w_hbm, g_ref, fw_ref, cw_ref, cb_ref, gn_ref,
             dx1_ref, dya_ref, do_ref, dzb_ref, dgab_ref, dw_hbm, st_ref, w_vm, dw_acc, sem):
        i = pl.program_id(0)

        @pl.when(i == 0)
        def _():
            cp = pltpu.make_async_copy(w_hbm, w_vm, sem)
            cp.start()
            dw_acc[...] = jnp.zeros_like(dw_acc)
            st_ref[...] = jnp.zeros_like(st_ref)
            cp.wait()

        u = cg_ref[...].astype(F32) * h_ref[...].astype(F32)
        above = jnp.where(i == 0, 0.0, cp_ref[15:16, :].astype(F32) * hp_ref[15:16, :].astype(F32))
        below = jnp.where(i == nt - 1, 0.0, cn_ref[0:1, :].astype(F32) * hn_ref[0:1, :].astype(F32))
        dn, up = _shift_rows(u, above, below)
        co = cw_ref[0:1, :] * dn + cw_ref[1:2, :] * u + cw_ref[2:3, :] * up + cb_ref[...]
        za = za_ref[...].astype(F32)
        ya_b = (za * _sigmoid(za) * bg_ref[...].astype(F32) * co).astype(BF16)
        yb_b = yb_ref[...]
        y_a = _dot(ya_b, w_vm[0])
        y_b = _dot(yb_b, w_vm[1])
        sga = _sigmoid(ga_ref[...].astype(F32))
        sgb = _sigmoid(gb_ref[...].astype(F32))
        mix_b = (sga * y_a + sgb * y_b).astype(BF16)
        y_x = _dot(mix_b, w_vm[2])
        gvec, fwv = g_ref[...], fw_ref[...]
        x1 = x_ref[...] + gvec * y_x
        r1 = lax.rsqrt(jnp.mean(x1 * x1, axis=-1, keepdims=True) + EPS)
        xh = x1 * r1
        diff = xh * fwv - t_ref[...]
        dout = diff * (1.0 / D)
        dxh = dout * fwv
        dx1 = r1 * (dxh - xh * jnp.mean(dxh * xh, axis=-1, keepdims=True))
        dx1_ref[...] = dx1
        st_ref[0:1, :] += jnp.sum(dout * xh, axis=0, keepdims=True)
        st_ref[1:2, :] += jnp.sum(dx1 * y_x, axis=0, keepdims=True)
        st_ref[2:3, :] += jnp.sum(diff * diff, axis=0, keepdims=True)
        dyx_b = (dx1 * gvec).astype(BF16)
        dmix = _dot_nt(dyx_b, w_vm[2])
        dw_acc[2] += _dot_tn(mix_b, dyx_b)
        dya_b = (dmix * sga).astype(BF16)
        dyb_b = (dmix * sgb).astype(BF16)
        dgab_ref[:, 0:D] = (dmix * y_a * sga * (1.0 - sga)).astype(BF16)
        dgab_ref[:, D:2 * D] = (dmix * y_b * sgb * (1.0 - sgb)).astype(BF16)
        dya_ref[...] = _dot_nt(dya_b, w_vm[0])
        dyb = _dot_nt(dyb_b, w_vm[1])
        dw_acc[0] += _dot_tn(ya_b, dya_b)
        dw_acc[1] += _dot_tn(yb_b, dyb_b)

        for h in range(H):
            sl = slice(h * DV, (h + 1) * DV)
            ov = o_ref[:, sl]
            oc = ov - jnp.mean(ov, axis=-1, keepdims=True)
            rstd = lax.rsqrt(jnp.mean(oc * oc, axis=-1, keepdims=True) + EPS)
            rn = oc * rstd
            gw = gn_ref[:, sl]
            zb = zb_ref[:, sl].astype(F32)
            sz = _sigmoid(zb)
            dy = dyb[:, sl]
            dzb_ref[:, sl] = (dy * (rn * gw) * (sz * (1.0 + zb * (1.0 - sz)))).astype(BF16)
            dretn = dy * (zb * sz)
            st_ref[3:4, sl] += jnp.sum(dretn * rn, axis=0, keepdims=True)
            drn = dretn * gw
            do_ref[:, sl] = (rstd * (drn - jnp.mean(drn, axis=-1, keepdims=True)
                                     - rn * jnp.mean(drn * rn, axis=-1, keepdims=True))).astype(BF16)

        @pl.when(i == nt - 1)
        def _():
            out = pltpu.make_async_copy(dw_acc, dw_hbm, sem)
            out.start()
            out.wait()

    row = lambda col: pl.BlockSpec((tm, D), lambda i: (i, col))
    any_spec = pl.BlockSpec(memory_space=pl.ANY)
    f32o = jax.ShapeDtypeStruct((L, D), F32)
    bf16o = jax.ShapeDtypeStruct((L, D), BF16)
    hp, hn = _halo_specs(tm, L, D, 0)
    cp, cn = _halo_specs(tm, L, D, 2)
    return pl.pallas_call(
        body, name="mid", grid=(nt,),
        in_specs=[row(0), row(1), row(2), row(3), hp, hn, cp, cn, row(0), row(7), row(8), row(6), row(0),
                  row(0), row(0), any_spec, _vec_spec(D), _vec_spec(D),
                  pl.BlockSpec((8, D), lambda i: (0, 0)), _vec_spec(D), _vec_spec(D)],
        out_specs=[row(0), row(0), row(0), row(0), pl.BlockSpec((tm, 2 * D), lambda i: (i, 0)), any_spec,
                   pl.BlockSpec((8, D), lambda i: (0, 0))],
        out_shape=[f32o, f32o, bf16o, bf16o, jax.ShapeDtypeStruct((L, 2 * D), BF16),
                   jax.ShapeDtypeStruct((3, D, D), F32), jax.ShapeDtypeStruct((8, D), F32)],
        scratch_shapes=[pltpu.VMEM((3, D, D), BF16), pltpu.VMEM((3, D, D), F32), pltpu.SemaphoreType.DMA],
        compiler_params=_cparams(("arbitrary",), VMEM_LIMIT),
    )(p, p, p, p, p, p, p, p, yb, p, p, p, o, x, tgt, w3, g, fw, conv_w, conv_b, gn_w)


def _conv_bwd(dya, p, conv_w, conv_b, D, exchange=None):
    L = p.shape[0]
    tl = min(256, L)
    nt = L // tl

    def body(d_ref, h_ref, bg_ref, cg_ref, za_ref,
             dp_ref, dn_ref, hp_ref, hn_ref, bp_ref, bn_ref, cp_ref, cn_ref, zp_ref, zn_ref,
             w_ref, b_ref, dc_ref, st_ref):
        i = pl.program_id(0)

        @pl.when(i == 0)
        def _():
            st_ref[...] = jnp.zeros_like(st_ref)

        first, last = i == 0, i == nt - 1
        h = h_ref[...].astype(F32)
        cg = cg_ref[...].astype(F32)
        bg = bg_ref[...].astype(F32)
        za = za_ref[...].astype(F32)
        dy = d_ref[...].astype(F32)
        u = cg * h
        u_above = jnp.where(first, 0.0, cp_ref[15:16, :].astype(F32) * hp_ref[15:16, :].astype(F32))
        u_below = jnp.where(last, 0.0, cn_ref[0:1, :].astype(F32) * hn_ref[0:1, :].astype(F32))
        u_dn, u_up = _shift_rows(u, u_above, u_below)
        w0, w1, w2 = w_ref[0:1, :], w_ref[1:2, :], w_ref[2:3, :]
        co = w0 * u_dn + w1 * u + w2 * u_up + b_ref[...]
        sz = _sigmoid(za)
        silu = za * sz
        dc_ref[:, 3 * D:4 * D] = (dy * bg * co * (sz * (1.0 + za * (1.0 - sz)))).astype(BF16)
        dc_ref[:, D:2 * D] = (dy * silu * co).astype(BF16)
        dco = dy * silu * bg

        def edge(dr, zr, br, r):
            z = zr[r:r + 1, :].astype(F32)
            return dr[r:r + 1, :].astype(F32) * (z * _sigmoid(z)) * br[r:r + 1, :].astype(F32)

        dco_above = jnp.where(first, 0.0, edge(dp_ref, zp_ref, bp_ref, 15))
        dco_below = jnp.where(last, 0.0, edge(dn_ref, zn_ref, bn_ref, 0))
        dco_dn, dco_up = _shift_rows(dco, dco_above, dco_below)
        du = w0 * dco_up + w1 * dco + w2 * dco_dn
        dc_ref[:, 2 * D:3 * D] = (du * h).astype(BF16)
        dc_ref[:, 0:D] = (du * cg).astype(BF16)
        st_ref[0:1, :] += jnp.sum(dco * u_dn, axis=0, keepdims=True)
        st_ref[1:2, :] += jnp.sum(dco * u, axis=0, keepdims=True)
        st_ref[2:3, :] += jnp.sum(dco * u_up, axis=0, keepdims=True)
        st_ref[3:4, :] += jnp.sum(dco, axis=0, keepdims=True)

    main = lambda col: pl.BlockSpec((tl, D), lambda i: (i, col))
    halos = []
    for col in (0, 0, 1, 2, 3):
        halos.extend(_halo_specs(tl, L, D, col))
    return _riding_call(
        body, exchange, nt, name="conv_bwd",
        args=(dya, p, p, p, p, dya, dya, p, p, p, p, p, p, p, p, conv_w, conv_b),
        in_specs=[main(0), main(0), main(1), main(2), main(3)] + halos
                 + [pl.BlockSpec((8, D), lambda i: (0, 0)), _vec_spec(D)],
        out_specs=[pl.BlockSpec((tl, 4 * D), lambda i: (i, 0)), pl.BlockSpec((8, D), lambda i: (0, 0))],
        out_shape=[jax.ShapeDtypeStruct((L, 4 * D), BF16), jax.ShapeDtypeStruct((8, D), F32)],
        cparams=_cparams(("arbitrary",)))


def _ret_bwd_states(qr, do, tab, D):
    L = qr.shape[0]
    H = D // DV
    N = L // CHUNK
    HP = H // 2

    def body(tab_ref, qf_ref, qb_ref, dof_ref, dob_ref, dsf_out, dsb_out, ds0_out, dsf, dsb):
        n = pl.program_id(0)

        @pl.when(n == 0)
        def _():
            dsf[...] = jnp.zeros_like(dsf)
            dsb[...] = jnp.zeros_like(dsb)

        for cc in range(RET_CPB):
            cf_, cb_ = RET_CPB - 1 - cc, cc
            rf, rb = slice(cf_ * CHUNK, (cf_ + 1) * CHUNK), slice(cb_ * CHUNK, (cb_ + 1) * CHUNK)
            dsf_out[cf_] = dsf[...]
            dsb_out[cb_] = dsb[...]
            for pr in range(HP):
                qf2 = qf_ref[rf, pr * 128:(pr + 1) * 128].astype(F32)
                qb2 = qb_ref[rb, pr * 128:(pr + 1) * 128].astype(F32)
                inc_f, inc_b, gf, gb = [], [], [], []
                for e in range(2):
                    h = 2 * pr + e
                    inc_f.append(_dot_tn((qf2 * tab_ref[h, T_QF]).astype(BF16), dof_ref[rf, h * DV:(h + 1) * DV]))
                    inc_b.append(_dot_tn((qb2 * tab_ref[h, T_QB]).astype(BF16), dob_ref[rb, h * DV:(h + 1) * DV]))
                    cf, cb = _chunk_decay(tab_ref, h)
                    gf.append(jnp.broadcast_to(cf, (128, 128)))
                    gb.append(jnp.broadcast_to(cb, (128, 128)))
                dsf[pr] = _pair_select(gf[0], gf[1]) * dsf[pr] + _pair_select(inc_f[0], inc_f[1])
                dsb[pr] = _pair_select(gb[0], gb[1]) * dsb[pr] + _pair_select(inc_b[0], inc_b[1])

        @pl.when(n == NB - 1)
        def _():
            ds0_out[0] = dsf[...]
            ds0_out[1] = dsb[...]

    st = jax.ShapeDtypeStruct((N, HP, 128, 128), F32)
    R = RET_CPB * CHUNK
    NB = N // RET_CPB
    return pl.pallas_call(
        body, name="ret_bwd_states", grid=(NB,),
        in_specs=[_tab_spec(H),
                  pl.BlockSpec((R, D // 2), lambda n: (NB - 1 - n, 0)),
                  pl.BlockSpec((R, D // 2), lambda n: (n, 0)),
                  pl.BlockSpec((R, D), lambda n: (NB - 1 - n, 0)),
                  pl.BlockSpec((R, D), lambda n: (n, 0))],
        out_specs=[pl.BlockSpec((RET_CPB, HP, 128, 128), lambda n: (NB - 1 - n, 0, 0, 0)),
                   pl.BlockSpec((RET_CPB, HP, 128, 128), lambda n: (n, 0, 0, 0)),
                   pl.BlockSpec((2, HP, 128, 128), lambda n: (0, 0, 0, 0))],
        out_shape=[st, st, jax.ShapeDtypeStruct((2, HP, 128, 128), F32)],
        scratch_shapes=[pltpu.VMEM((HP, 128, 128), F32), pltpu.VMEM((HP, 128, 128), F32)],
        compiler_params=_cparams(("arbitrary",)),
    )(tab, qr, qr, do, do)


def _ret_bwd_main(qr, kr, p, do, sf_prev, sb_prev, dsf, dsb, c2, s2, tab, D, exchange=None):
    L = qr.shape[0]
    H = D // DV
    N = L // CHUNK
    HP = H // 2
    W = D // 2

    def body(tab_ref, q_ref, k_ref, v_ref, do_ref, sf_ref, sb_ref, dsf_ref, dsb_ref, c_ref, s_ref,
             dr_ref, st_ref, dl_acc):
        @pl.when(pl.program_id(0) == 0)
        def _():
            dl_acc[...] = jnp.zeros_like(dl_acc)

        i = lax.broadcasted_iota(jnp.int32, (CHUNK, 128), 0).astype(F32)
        rowid = lax.broadcasted_iota(jnp.int32, (128, 128), 0)

        def chunk(cc, carry):
            rows = pl.ds(pl.multiple_of(cc * CHUNK, CHUNK), CHUNK)
            c, s = c_ref[rows, :], s_ref[rows, :]
            for pr in range(HP):
                ps = slice(pr * 128, (pr + 1) * 128)
                q2, k2 = q_ref[rows, ps], k_ref[rows, ps]
                sf32, sb32 = sf_ref[cc, pr], sb_ref[cc, pr]
                dsf32, dsb32 = dsf_ref[cc, pr], dsb_ref[cc, pr]
                sfp, sbp = sf32.astype(BF16), sb32.astype(BF16)
                dsfp, dsbp = dsf32.astype(BF16), dsb32.astype(BF16)
                dq2 = jnp.zeros((CHUNK, 128), F32)
                dk2 = jnp.zeros((CHUNK, 128), F32)
                for e in range(2):
                    h = 2 * pr + e
                    sl = slice(h * DV, (h + 1) * DV)
                    hm = _head_lane_mask(q2.shape, e)
                    qm = jnp.where(hm, q2, jnp.zeros_like(q2))
                    km = jnp.where(hm, k2, jnp.zeros_like(k2))
                    qf, kf = qm.astype(F32), km.astype(F32)
                    v, do = v_ref[rows, sl], do_ref[rows, sl]
                    vf, dof = v.astype(F32), do.astype(F32)
                    m_t = tab_ref[h, T_MT]
                    sc = _dot_nt(qm, k2)
                    dpm = _dot_nt(do, v)
                    dsc = (dpm * tab_ref[h, T_M]).astype(BF16)
                    a_t = (_dot_nt(km, q2) * m_t).astype(BF16)
                    dsc_t = (_dot_nt(v, do) * m_t).astype(BF16)
                    dq_f, dq_b = tab_ref[h, T_QF], tab_ref[h, T_QB]
                    dk_f, dk_b = tab_ref[h, T_KF], tab_ref[h, T_KB]
                    dq = _dot(dsc, km)
                    dq += jnp.where(hm, dq_f * _dot_nt(do, sfp) + dq_b * _dot_nt(do, sbp), 0.0)
                    dk = _dot(dsc_t, qm)
                    dk += jnp.where(hm, dk_f * _dot_nt(v, dsfp) + dk_b * _dot_nt(v, dsbp), 0.0)
                    kdf = _dot((kf * dk_f).astype(BF16), dsfp)
                    kdb = _dot((kf * dk_b).astype(BF16), dsbp)
                    dr_ref[rows, D + h * DV:D + (h + 1) * DV] = (_dot(a_t, do) + kdf + kdb).astype(BF16)
                    dq2 += dq
                    dk2 += dk
                    xf = _dot((qf * dq_f).astype(BF16), sfp)
                    xb = _dot((qf * dq_b).astype(BF16), sbp)
                    pair = (rowid < DK) if e == 0 else (rowid >= DK)
                    gcf, gcb = tab_ref[h, T_QF, CHUNK - 1:CHUNK, 0:1], tab_ref[h, T_QB, 0:1, 0:1]
                    scdp = sc * dpm
                    dl_acc[h, 0] += scdp * tab_ref[h, T_MF1] + xf * dof * (i + 1.0) \
                        + kdf * vf * (CHUNK - 1.0 - i) + (CHUNK * gcf) * jnp.where(pair, dsf32 * sf32, 0.0)
                    dl_acc[h, 1] += scdp * tab_ref[h, T_MB1] + xb * dof * (CHUNK - i) \
                        + kdb * vf * i + (CHUNK * gcb) * jnp.where(pair, dsb32 * sb32, 0.0)
                dr_ref[rows, ps] = (dq2 * c - _swap_halves(dq2) * s).astype(BF16)
                dr_ref[rows, W + pr * 128:W + (pr + 1) * 128] = \
                    ((dk2 * c - _swap_halves(dk2) * s) * K_SCALE).astype(BF16)
            return carry

        lax.fori_loop(0, RET_CPB, chunk, 0)

        @pl.when(pl.program_id(0) == N // RET_CPB - 1)
        def _():
            lane = lax.broadcasted_iota(jnp.int32, (1, 128), 1)
            acc = [jnp.zeros((1, 128), F32), jnp.zeros((1, 128), F32)]
            for h in range(H):
                for b in range(2):
                    acc[b] += jnp.where(lane == h, _sum_all(dl_acc[h, b]), 0.0)
            st_ref[...] = jnp.zeros_like(st_ref)
            st_ref[0:1, :] = acc[0]
            st_ref[1:2, :] = acc[1]

    R = RET_CPB * CHUNK
    st_spec = pl.BlockSpec((RET_CPB, HP, 128, 128), lambda n: (n, 0, 0, 0))
    half = pl.BlockSpec((R, W), lambda n: (n, 0))
    rope = pl.BlockSpec((R, 128), lambda n: (n, 0))
    return _riding_call(
        body, exchange, N // RET_CPB, name="ret_bwd_main",
        args=(tab, qr, kr, p, do, sf_prev, sb_prev, dsf, dsb, c2, s2),
        in_specs=[_tab_spec(H), half, half,
                  pl.BlockSpec((R, D), lambda n: (n, 5)),
                  pl.BlockSpec((R, D), lambda n: (n, 0)),
                  st_spec, st_spec, st_spec, st_spec, rope, rope],
        out_specs=[pl.BlockSpec((R, 2 * D), lambda n: (n, 0)),
                   pl.BlockSpec((8, 128), lambda n: (0, 0))],
        out_shape=[jax.ShapeDtypeStruct((L, 2 * D), BF16), jax.ShapeDtypeStruct((8, 128), F32)],
        scratch=[pltpu.VMEM((H, 2, CHUNK, 128), F32)],
        cparams=_cparams(("arbitrary",)))


def _ctx_bwd(pc, pqk_c, ds0, lg, D):
    Lc = pc.shape[0]
    H = D // DV
    HP = H // 2
    W = D // 2

    def body(lg_ref, k_ref, v_ref, ds_ref, dr_ref, st_ref):
        dqk_ref = dr_ref.at[:, 0:D]
        dv_ref = dr_ref.at[:, D:2 * D]
        m = lax.broadcasted_iota(jnp.int32, (Lc, 128), 0).astype(F32)
        lane = lax.broadcasted_iota(jnp.int32, (1, 128), 1)
        acc_f = jnp.zeros((1, 128), F32)
        acc_b = jnp.zeros((1, 128), F32)
        dqk_ref[:, 0:W] = jnp.zeros((Lc, W), BF16)
        for pr in range(HP):
            ps = slice(pr * 128, (pr + 1) * 128)
            k2 = k_ref[:, ps].astype(F32) * K_SCALE
            dsfp, dsbp = ds_ref[0, pr].astype(BF16), ds_ref[1, pr].astype(BF16)
            dk2 = jnp.zeros((Lc, 128), F32)
            for e in range(2):
                h = 2 * pr + e
                sl = slice(h * DV, (h + 1) * DV)
                hm = _head_lane_mask(k2.shape, e)
                km = jnp.where(hm, k2, 0.0)
                v = v_ref[:, sl]
                vf = v.astype(F32)
                dec_f = jnp.exp(lg_ref[0, h] * (Lc - 1.0 - m))
                dec_b = jnp.exp(lg_ref[1, h] * m)
                kdf = _dot((km * dec_f).astype(BF16), dsfp)
                kdb = _dot((km * dec_b).astype(BF16), dsbp)
                dv_ref[:, sl] = (kdf + kdb).astype(BF16)
                dk2 += jnp.where(hm, dec_f * _dot_nt(v, dsfp) + dec_b * _dot_nt(v, dsbp), 0.0)
                acc_f += jnp.where(lane == h, _sum_all(kdf * vf * (Lc - 1.0 - m)), 0.0)
                acc_b += jnp.where(lane == h, _sum_all(kdb * vf * m), 0.0)
            dqk_ref[:, W + pr * 128:W + (pr + 1) * 128] = (dk2 * K_SCALE).astype(BF16)
        st_ref[...] = jnp.zeros_like(st_ref)
        st_ref[0:1, :] = acc_f
        st_ref[1:2, :] = acc_b

    return pl.pallas_call(
        body, name="ctx_bwd", grid=(1,),
        in_specs=[_smem_spec(), pl.BlockSpec((Lc, W), lambda i: (0, 1)), pl.BlockSpec((Lc, D), lambda i: (0, 1)),
                  pl.BlockSpec((2, HP, 128, 128), lambda i: (0, 0, 0, 0))],
        out_specs=[pl.BlockSpec((Lc, 2 * D), lambda i: (0, 0)), pl.BlockSpec((8, 128), lambda i: (0, 0))],
        out_shape=[jax.ShapeDtypeStruct((Lc, 2 * D), BF16), jax.ShapeDtypeStruct((8, 128), F32)],
    )(lg, pqk_c, pc, ds0)


class _Exchange(NamedTuple):
    inputs: tuple
    out_shapes: tuple
    n_copies: int
    build: Callable


def _exchange_parts(exchange):
    if exchange is None:
        return [], [], [], [], []
    n = exchange.n_copies
    return (list(exchange.inputs), [ANY] * len(exchange.inputs), list(exchange.out_shapes),
            [ANY] * len(exchange.out_shapes), [pltpu.SemaphoreType.DMA((n,)), pltpu.SemaphoreType.DMA((n,))])


def _riding_call(body, exchange, n_steps, *, args, in_specs, out_specs, out_shape, name, cparams, scratch=()):
    ex_args, ex_in_specs, ex_shapes, ex_out_specs, ex_scratch = _exchange_parts(exchange)
    n_in, n_out, n_sc = len(args), len(out_shape), len(scratch)

    def riding(*refs):
        k = n_in + len(ex_args)
        ins, ex_in = refs[:n_in], refs[n_in:k]
        outs, ex_out = refs[k:k + n_out], refs[k + n_out:k + n_out + len(ex_shapes)]
        k += n_out + len(ex_shapes)
        own_scratch, ex_sems = refs[k:k + n_sc], refs[k + n_sc:]
        step = pl.program_id(0)
        if exchange is not None:
            @pl.when(step == 0)
            def _():
                for rc in exchange.build(ex_in, ex_out, *ex_sems):
                    rc.start()
        body(*ins, *outs, *own_scratch)
        if exchange is not None:
            @pl.when(step == n_steps - 1)
            def _():
                for rc in exchange.build(ex_in, ex_out, *ex_sems):
                    rc.wait()

    return tuple(pl.pallas_call(
        riding, name=name, grid=(n_steps,),
        in_specs=list(in_specs) + ex_in_specs, out_specs=list(out_specs) + ex_out_specs,
        out_shape=list(out_shape) + ex_shapes, scratch_shapes=list(scratch) + ex_scratch,
        compiler_params=cparams,
    )(*args, *ex_args))


def _dxm(groups, col0, w, x, nw, sc, dx1, name, exchange=None):
    L, D = x.shape
    tm = min(256, L)
    nt = L // tm
    ng = len(groups)
    widths = [g.shape[1] for g in groups]
    wtot = sum(widths)
    with_dx = dx1 is not None
    ex_args, ex_in_specs, ex_shapes, ex_out_specs, ex_scratch = _exchange_parts(exchange)
    n_in = ng + 4 + (1 if with_dx else 0)
    n_out = 2 if with_dx else 1

    def body(*refs):
        group_refs = refs[:ng]
        w_hbm, x_ref, nw_ref, sc_ref = refs[ng:ng + 4]
        ex_in = refs[n_in:n_in + len(ex_args)]
        outs = refs[n_in + len(ex_args):]
        if with_dx:
            dx1_ref, gx_ref, st_ref = refs[ng + 4], outs[0], outs[1]
        else:
            st_ref = outs[0]
        ex_out = outs[n_out:n_out + len(ex_shapes)]
        w_vm, sem = outs[n_out + len(ex_shapes):n_out + len(ex_shapes) + 2]
        ex_sems = outs[n_out + len(ex_shapes) + 2:]
        i = pl.program_id(0)

        @pl.when(i == 0)
        def _():
            cp = pltpu.make_async_copy(w_hbm.at[:, col0 * D:col0 * D + wtot], w_vm, sem)
            cp.start()
            if exchange is not None:
                for rc in exchange.build(ex_in, ex_out, *ex_sems):
                    rc.start()
            st_ref[...] = jnp.zeros_like(st_ref)
            cp.wait()

        dxm, off = None, 0
        for g_ref, wd in zip(group_refs, widths):
            part = _dot_nt(g_ref[...], w_vm[:, off:off + wd])
            dxm = part if dxm is None else dxm + part
            off += wd

        xv = x_ref[...]
        r = lax.rsqrt(jnp.mean(xv * xv, axis=-1, keepdims=True) + EPS)
        xh = xv * r
        nwv = nw_ref[...]
        dxn = dxm * (1.0 + sc_ref[...])
        st_ref[0:1, :] += jnp.sum(dxm, axis=0, keepdims=True)
        st_ref[1:2, :] += jnp.sum(dxm * (xh * nwv), axis=0, keepdims=True)
        st_ref[2:3, :] += jnp.sum(dxn * xh, axis=0, keepdims=True)
        if with_dx:
            dxh = dxn * nwv
            gx_ref[...] = dx1_ref[...] + r * (dxh - xh * jnp.mean(dxh * xh, axis=-1, keepdims=True))

        if exchange is not None:
            @pl.when(i == nt - 1)
            def _():
                for rc in exchange.build(ex_in, ex_out, *ex_sems):
                    rc.wait()

    row = pl.BlockSpec((tm, D), lambda i: (i, 0))
    in_specs = [pl.BlockSpec((tm, wd), lambda i: (i, 0)) for wd in widths] + [ANY, row, _vec_spec(D), _vec_spec(D)]
    out_specs = [pl.BlockSpec((8, D), lambda i: (0, 0))]
    out_shape = [jax.ShapeDtypeStruct((8, D), F32)]
    args = list(groups) + [w, x, nw, sc]
    if with_dx:
        in_specs.append(row)
        out_specs.insert(0, row)
        out_shape.insert(0, jax.ShapeDtypeStruct((L, D), F32))
        args.append(dx1)
    res = pl.pallas_call(
        body, name=name, grid=(nt,),
        in_specs=in_specs + ex_in_specs, out_specs=out_specs + ex_out_specs, out_shape=out_shape + ex_shapes,
        scratch_shapes=[pltpu.VMEM((D, wtot), BF16), pltpu.SemaphoreType.DMA] + ex_scratch,
        compiler_params=_cparams(("arbitrary",), VMEM_LIMIT),
    )(*args, *ex_args)
    gx = res[0] if with_dx else None
    return (gx, res[n_out - 1], *res[n_out:])


DW_TN = 512
DW_RING = 4


def _dw_in(xmt, groups, cmt, dr_c, D, pair):
    L = xmt.shape[1]
    Lc = cmt.shape[1]
    Dh = D // 2
    tn = min(DW_TN, D)
    nblk = [g.shape[1] // tn for g in groups]
    starts = [sum(nblk[:g]) for g in range(len(groups))]
    ng = len(groups)
    nj = sum(nblk)
    rows_out = Dh if pair else D

    def body(*refs):
        xt_hbm = refs[0]
        group_refs = refs[1:1 + ng]
        ct_hbm, drc_ref, o_ref = refs[1 + ng:4 + ng]
        rest = refs[4 + ng:]
        if pair:
            ra_hbm, xt_vm, ct_vm, loc, ring, s_send, s_recv = rest
            pos = _position()
            sib = _peer(pos, 1)
        else:
            xt_vm, ct_vm, loc = rest
        j = pl.program_id(0)

        @pl.when(j == 0)
        def _():
            if pair:
                c = pos[2]
                other = pl.ds(pl.multiple_of((1 - c) * Dh, Dh), Dh)
                mine = pl.ds(pl.multiple_of(c * Dh, Dh), Dh)
                cps = [pltpu.make_async_copy(xt_hbm.at[other, :], xt_vm.at[0:Dh, :], loc.at[0]),
                       pltpu.make_async_copy(xt_hbm.at[mine, :], xt_vm.at[Dh:D, :], loc.at[1]),
                       pltpu.make_async_copy(ct_hbm.at[other, :], ct_vm.at[0:Dh, :], loc.at[2]),
                       pltpu.make_async_copy(ct_hbm.at[mine, :], ct_vm.at[Dh:D, :], loc.at[3])]
            else:
                cps = [pltpu.make_async_copy(xt_hbm, xt_vm, loc.at[0]), pltpu.make_async_copy(ct_hbm, ct_vm, loc.at[1])]
            for cp in cps:
                cp.start()
            for cp in cps:
                cp.wait()

        def send(slot):
            cols = pl.ds(pl.multiple_of(j * tn, 128), tn)
            return pltpu.make_async_remote_copy(src_ref=ring.at[slot], dst_ref=ra_hbm.at[:, cols],
                                                send_sem=s_send.at[slot], recv_sem=s_recv,
                                                device_id=sib, device_id_type=MESH)

        for g in range(ng):
            @pl.when((j >= starts[g]) & (j < starts[g] + nblk[g]))
            def _(g=g):
                acc = _dot(xt_vm[...], group_refs[g][...])
                if g == 1:
                    acc += _dot(ct_vm[...], drc_ref[...])
                if not pair:
                    o_ref[...] = acc
                    return
                o_ref[...] = acc[Dh:, :]
                slot = lax.rem(j, DW_RING)

                @pl.when(j >= DW_RING)
                def _():
                    send(slot).wait_send()

                ring[slot] = acc[0:Dh, :]
                send(slot).start()

        if pair:
            @pl.when(j == nj - 1)
            def _():
                pltpu.make_async_remote_copy(src_ref=ra_hbm, dst_ref=ra_hbm, send_sem=s_send.at[0], recv_sem=s_recv,
                                             device_id=sib, device_id_type=MESH).wait_recv()
                for slot in range(DW_RING):
                    send(slot).wait_send()

    def group_spec(g, rows):
        return pl.BlockSpec((rows, tn), lambda j: (0, jnp.clip(j - starts[g], 0, nblk[g] - 1)))

    out_specs = [pl.BlockSpec((rows_out, tn), lambda j: (0, j))]
    out_shape = [jax.ShapeDtypeStruct((rows_out, nj * tn), F32)]
    scratch = [pltpu.VMEM((D, L), BF16), pltpu.VMEM((D, Lc), BF16), pltpu.SemaphoreType.DMA((4,))]
    if pair:
        out_specs.append(ANY)
        out_shape.append(jax.ShapeDtypeStruct((Dh, nj * tn), F32))
        scratch += [pltpu.VMEM((DW_RING, Dh, tn), F32), pltpu.SemaphoreType.DMA((DW_RING,)), pltpu.SemaphoreType.DMA]
    return tuple(pl.pallas_call(
        body, name="dw_in", grid=(nj,),
        in_specs=[ANY] + [group_spec(g, L) for g in range(ng)] + [ANY, group_spec(1, Lc)],
        out_specs=out_specs, out_shape=out_shape, scratch_shapes=scratch,
        compiler_params=_cparams(("arbitrary",), VMEM_LIMIT),
    )(xmt, *groups, cmt, dr_c))


def _local_step(x, ctx, tgt, mod_x, mod_c, norm_w, conv_w8, conv_b, lg, gn_w, fw, project, csidx=None):
    L, D = x.shape
    sh_x, sc_x, g_x = mod_x[0:1], mod_x[1:2], mod_x[2:3]
    sh_c, sc_c = mod_c[0:1], mod_c[1:2]
    c2, s2 = _rope_tables(L)
    tab = _decay_tables(lg, D // DV)

    xm, xmt = _norm_mod(x, norm_w, sc_x, sh_x, "norm_mod_x")
    cm, cmt = _norm_mod(ctx, norm_w, sc_c, sh_c, "norm_mod_ctx")
    reduce = csidx is not None
    p, qr, kr, w_in, w3 = project(xm, c2, s2)
    pc, pqk_c = _in_proj(cm, w_in, "in_proj_ctx", QK_BLOCK, 2)
    s0 = _ctx_states(pc, pqk_c, lg, D)
    sf_prev, sb_prev = _ret_states(kr, p, s0, tab, D)
    o, yb = _ret_out(qr, kr, p, sf_prev, sb_prev, gn_w, tab, D)
    dx1, dya, do, dzb, dgab, dw3, st_mid = _mid(p, yb, o, x, tgt, w3, g_x, fw, conv_w8, conv_b, gn_w, D)
    dw3_5 = dw3.reshape(3, N_SHARD, 2, D // 8, D)
    dconv, st_conv, *ra_3 = _conv_bwd(dya, p, conv_w8, conv_b, D, _pair_exchange_w3(dw3_5) if reduce else None)
    dsf, dsb, ds0 = _ret_bwd_states(qr, do, tab, D)
    cs_3 = _sum_pair_w3(csidx[0:1], dw3_5, ra_3[0]) if reduce else None
    dret, st_lg, *rb_3 = _ret_bwd_main(qr, kr, p, do, sf_prev, sb_prev, dsf, dsb, c2, s2, tab, D,
                                       _chips_exchange_w3(cs_3) if reduce else None)
    g_3 = _sum_chips_w3(csidx, cs_3, rb_3[0]) if reduce else dw3
    dret_c, st_lgc = _ctx_bwd(pc, pqk_c, ds0, lg, D)
    groups = (dconv, dret, dzb, dgab)
    _, st_c = _dxm((dret_c,), 4, w_in, ctx, norm_w, sc_c, None, "dxm_ctx")
    return groups, dret_c, xmt, cmt, dx1, sc_x, w_in, g_3, (st_mid, st_conv, st_lg, st_lgc, st_c)


CHIP_FLIPS = (4, 2, 6)
ANY = pl.BlockSpec(memory_space=pl.ANY)
VMEM_FULL = pl.BlockSpec(memory_space=pltpu.VMEM)


def _position():
    return lax.axis_index("x"), lax.axis_index("y"), lax.axis_index("c")


def _peer(pos, k):
    x, y, c = pos
    return (1 - x if k & 4 else x, 1 - y if k & 2 else y, 1 - c if k & 1 else c)


def _dev_id(pos):
    return 4 * pos[0] + 2 * pos[1] + pos[2]


def _shard_of(pos):
    return 2 * pos[0] + pos[1]


def _remote(src, dst, send_sems, recv_sems, idx, to):
    return pltpu.make_async_remote_copy(src_ref=src, dst_ref=dst, send_sem=send_sems.at[idx],
                                        recv_sem=recv_sems.at[idx], device_id=to, device_id_type=MESH)


def _dot_f32(a, b):
    return jnp.dot(a, b, precision=lax.Precision.HIGHEST, preferred_element_type=F32)


def _silu(x):
    return x * _sigmoid(x)


def _fwd_small(c8, cctx8, ada_w, ada_b, conv_w8):
    D = c8.shape[1]
    Wm = ada_w.shape[1]
    Dq = conv_w8.shape[1]

    def body(c_ref, cc_ref, aw_ref, ab_ref, cw_ref, act_ref, mod_ref, cwf_ref,
             cbuf, pmine, pbuf, wbuf, s_c, r_c, s_p, r_p, s_w, r_w):
        pos = _position()
        me, s = _dev_id(pos), _shard_of(pos)
        cbuf[me] = c_ref[...]
        wbuf[s] = cw_ref[...]
        sends = [_remote(c_ref, cbuf.at[me], s_c, r_c, k - 1, _peer(pos, k)) for k in range(1, 8)]
        sends += [_remote(cw_ref, wbuf.at[s], s_w, r_w, j, _peer(pos, k)) for j, k in enumerate(CHIP_FLIPS)]
        for cp in sends:
            cp.start()
        for k in range(1, 8):
            _remote(c_ref, cbuf.at[_dev_id(_peer(pos, k))], s_c, r_c, k - 1, _peer(pos, k)).wait_recv()
        for d in range(N_DEV):
            act_ref[d:d + 1, :] = _silu(cbuf[d, 0:1, :])
        act_ref[8:9, :] = _silu(cc_ref[0:1, :])
        act_ref[9:16, :] = jnp.zeros((7, D), F32)
        part = _dot_f32(act_ref[...], aw_ref[...])
        pmine[...] = part
        pbuf[s] = part
        psend = [_remote(pmine, pbuf.at[s], s_p, r_p, j, _peer(pos, k)) for j, k in enumerate(CHIP_FLIPS)]
        for cp in psend:
            cp.start()
        for j, k in enumerate(CHIP_FLIPS):
            t = _shard_of(_peer(pos, k))
            _remote(pmine, pbuf.at[t], s_p, r_p, j, _peer(pos, k)).wait_recv()
            _remote(cw_ref, wbuf.at[t], s_w, r_w, j, _peer(pos, k)).wait_recv()
        for t in range(N_SHARD):
            mod_ref[:, t * Wm:(t + 1) * Wm] = pbuf[t] + ab_ref[:, t * Wm:(t + 1) * Wm]
            cwf_ref[:, t * Dq:(t + 1) * Dq] = wbuf[t]
        for cp in sends + psend:
            cp.wait_send()

    return pl.pallas_call(
        body, name="fwd_small",
        in_specs=[VMEM_FULL] * 5, out_specs=[VMEM_FULL] * 3,
        out_shape=[jax.ShapeDtypeStruct((16, D), F32), jax.ShapeDtypeStruct((16, 3 * D), F32),
                   jax.ShapeDtypeStruct((8, D), F32)],
        scratch_shapes=[pltpu.VMEM((N_DEV, 8, D), F32), pltpu.VMEM((16, Wm), F32),
                        pltpu.VMEM((N_SHARD, 16, Wm), F32), pltpu.VMEM((N_SHARD, 8, Dq), F32),
                        pltpu.SemaphoreType.DMA((7,)), pltpu.SemaphoreType.DMA((7,)),
                        pltpu.SemaphoreType.DMA((3,)), pltpu.SemaphoreType.DMA((3,)),
                        pltpu.SemaphoreType.DMA((3,)), pltpu.SemaphoreType.DMA((3,))],
        compiler_params=_cparams(None, VMEM_LIMIT),
    )(c8, cctx8, ada_w, ada_b, conv_w8)


AG_CHUNKS = 3


def _ag_in_proj(xm, w_in_s, w3_s, c2, s2):
    L, D = xm.shape
    Wc = w_in_s.shape[1]
    Wq = Wc // AG_CHUNKS
    Dh = D // 2
    Do = w3_s[0].shape[1]
    TM = min(1024, L // 4)
    NT = L // TM
    RC = min(128, Dh)
    NQ = AG_CHUNKS
    order = [(q, j) for q in range(NQ) for j in (0, 1)] + [(q, 2) for q in range(NQ)]

    def body(xm_ref, wi_hbm, wa_ref, wb_ref, wo_ref, c_ref, s_ref, p_hbm, qr_hbm, kr_hbm, fi_hbm, f3_hbm,
             w_vm, cast_buf, s3, stage, qk_stage, ici_s, ici_r, d2d_s, d2d_r, w3_s_, w3_r_, loc, out_sem, qk_sem):
        pos = _position()
        c = pos[2]
        s = _shard_of(pos)
        sib = _peer(pos, 1)
        mine = pl.ds(pl.multiple_of(c * Dh, Dh), Dh)
        other = pl.ds(pl.multiple_of((1 - c) * Dh, Dh), Dh)

        def cast_half(hf):
            def step(i, carry):
                rows = pl.ds(pl.multiple_of(hf * Dh + i * RC, RC), RC)
                cp = pltpu.make_async_copy(wi_hbm.at[rows, :], cast_buf, loc.at[0])
                cp.start()
                cp.wait()
                for q in range(NQ):
                    w_vm[0, q, rows, :] = cast_buf[:, q * Wq:(q + 1) * Wq].astype(BF16)
                return carry
            lax.fori_loop(0, Dh // RC, step, 0)

        def abs_col(t, q):
            return pl.ds(pl.multiple_of(t * Wc + q * Wq, 128), Wq)

        cast_half(c)
        for a, w_ref in enumerate((wa_ref, wb_ref, wo_ref)):
            s3[a] = w_ref[...].astype(BF16)
        sends = [_remote(w_vm.at[0, q, mine, :], w_vm.at[1 + j, q, mine, :], ici_s, ici_r, q * 3 + j,
                         _peer(pos, CHIP_FLIPS[j])) for q, j in order]
        for j, k in enumerate(CHIP_FLIPS):
            sends.append(_remote(s3.at[:, c], f3_hbm.at[:, s, c], w3_s_, w3_r_, j, _peer(pos, k)))
        for cp in sends:
            cp.start()
        cast_half(1 - c)
        local = [pltpu.make_async_copy(s3, f3_hbm.at[:, s], loc.at[1])]
        local += [pltpu.make_async_copy(w_vm.at[0, q], fi_hbm.at[:, abs_col(s, q)], loc.at[2 + q]) for q in range(NQ)]
        for cp in local:
            cp.start()

        def out_copy(slot, rows, cols):
            return pltpu.make_async_copy(stage.at[slot], p_hbm.at[rows, cols], out_sem.at[slot])

        def block(r, q, t, first):
            cols = abs_col(t, q)

            def row_tile(rt, carry):
                rows = pl.ds(pl.multiple_of(rt * TM, TM), TM)
                acc = _dot(xm_ref[rows, :], w_vm[r, q])
                slot = lax.rem(rt, 2)

                @pl.when(rt >= 2 if first else rt >= 0)
                def _():
                    out_copy(slot, rows, cols).wait()

                stage[slot] = acc.astype(BF16)
                out_copy(slot, rows, cols).start()

                def rotary(lo, scale, dst_hbm):
                    c, s = c_ref[rows, :], s_ref[rows, :]
                    for pr in range(Dh // 128):
                        tq = acc[:, lo + pr * 128:lo + (pr + 1) * 128] * scale
                        qk_stage[:, pr * 128:(pr + 1) * 128] = (tq * c + _swap_halves(tq) * s).astype(BF16)
                    cp = pltpu.make_async_copy(qk_stage, dst_hbm.at[rows, :], qk_sem)
                    cp.start()
                    cp.wait()

                if q == NQ - 1:
                    @pl.when(t == 1)
                    def _():
                        rotary(Wq - Dh, 1.0, qr_hbm)
                if q == 0:
                    @pl.when(t == 2)
                    def _():
                        rotary(0, K_SCALE, kr_hbm)
                return carry

            lax.fori_loop(0, NT, row_tile, 0)

        passed = []

        def hand_on(q, j):
            half = w_vm.at[1 + j, q, mine, :]
            _remote(half, half, ici_s, ici_r, q * 3 + j, sib).wait_recv()
            fwd = _remote(half, half, d2d_s, d2d_r, q * 3 + j, sib)
            fwd.start()
            passed.append(fwd)

        for q in range(NQ):
            if q == NQ - 1:
                hand_on(*order[0])
            block(0, q, s, q == 0)
        for n, (q, j) in enumerate(order):
            r, idx = 1 + j, q * 3 + j
            t = _shard_of(_peer(pos, CHIP_FLIPS[j]))
            if n + 1 < len(order):
                hand_on(*order[n + 1])
            _remote(w_vm.at[r, q, other, :], w_vm.at[r, q, other, :], d2d_s, d2d_r, idx, sib).wait_recv()
            block(r, q, t, False)
            cp = pltpu.make_async_copy(w_vm.at[r, q], fi_hbm.at[:, abs_col(t, q)], loc.at[2 + NQ + idx])
            cp.start()
            local.append(cp)
        for j, k in enumerate(CHIP_FLIPS):
            t = _shard_of(_peer(pos, k))
            _remote(s3.at[:, c], f3_hbm.at[:, t, c], w3_s_, w3_r_, j, sib).wait_recv()
            fwd = _remote(f3_hbm.at[:, t, c], f3_hbm.at[:, t, c], w3_s_, w3_r_, 3 + j, sib)
            fwd.start()
            passed.append(fwd)
        for j, k in enumerate(CHIP_FLIPS):
            t = _shard_of(_peer(pos, k))
            _remote(s3.at[:, c], f3_hbm.at[:, t, 1 - c], w3_s_, w3_r_, 3 + j, sib).wait_recv()
        for cp in sends + passed:
            cp.wait_send()
        for cp in local:
            cp.wait()
        for slot in range(2):
            out_copy(slot, pl.ds(0, TM), abs_col(s, 0)).wait()

    n_loc = 2 + NQ + 3 * NQ
    return pl.pallas_call(
        body, name="ag_in_proj",
        in_specs=[VMEM_FULL, ANY, VMEM_FULL, VMEM_FULL, VMEM_FULL, VMEM_FULL, VMEM_FULL], out_specs=[ANY] * 5,
        out_shape=[jax.ShapeDtypeStruct((L, N_SHARD * Wc), BF16),
                   jax.ShapeDtypeStruct((L, Dh), BF16), jax.ShapeDtypeStruct((L, Dh), BF16),
                   jax.ShapeDtypeStruct((D, N_SHARD * Wc), BF16), jax.ShapeDtypeStruct((3, N_SHARD, 2, Do, D), BF16)],
        scratch_shapes=[pltpu.VMEM((N_SHARD, NQ, D, Wq), BF16), pltpu.VMEM((RC, Wc), F32),
                        pltpu.VMEM((3, 2, Do, D), BF16), pltpu.VMEM((2, TM, Wq), BF16), pltpu.VMEM((TM, Dh), BF16),
                        pltpu.SemaphoreType.DMA((3 * NQ,)), pltpu.SemaphoreType.DMA((3 * NQ,)),
                        pltpu.SemaphoreType.DMA((3 * NQ,)), pltpu.SemaphoreType.DMA((3 * NQ,)),
                        pltpu.SemaphoreType.DMA((6,)), pltpu.SemaphoreType.DMA((6,)),
                        pltpu.SemaphoreType.DMA((n_loc,)), pltpu.SemaphoreType.DMA((2,)), pltpu.SemaphoreType.DMA],
        compiler_params=_cparams(None, VMEM_LIMIT),
    )(xm, w_in_s, *w3_s, c2, s2)


def _pair_exchange_w3(dw3):
    _, _, _, Do, D = dw3.shape

    def build(ins, outs, send, recv):
        pos = _position()
        return [_remote(ins[0].at[:, :, 1 - pos[2]], outs[0], send, recv, 0, _peer(pos, 1))]

    return _Exchange((dw3,), (jax.ShapeDtypeStruct((3, N_SHARD, Do, D), F32),), 1, build)


def _sum_pair_in(dw_mine, ri):
    Dh, Wf = dw_mine.shape
    Wc = Wf // N_SHARD
    tr = min(256, Dh)

    def body(a_ref, b_ref, o_ref):
        o_ref[...] = (a_ref[...] + b_ref[...]).astype(BF16)

    return pl.pallas_call(
        body, name="sum_pair_in", grid=(Dh // tr, N_SHARD),
        in_specs=[pl.BlockSpec((tr, Wc), lambda i, t: (i, t)), pl.BlockSpec((tr, Wc), lambda i, t: (i, t))],
        out_specs=pl.BlockSpec((None, tr, Wc), lambda i, t: (t, i, 0)),
        out_shape=jax.ShapeDtypeStruct((N_SHARD, Dh, Wc), BF16),
        compiler_params=_cparams(("parallel", "parallel")),
    )(dw_mine, ri)


def _sum_pair_w3(cidx, dw3, r3):
    _, _, _, Do, D = dw3.shape

    def body(c_ref, a_ref, b_ref, o_ref):
        o_ref[...] = (a_ref[...] + b_ref[...]).astype(BF16)

    return pl.pallas_call(
        body, name="sum_pair_w3",
        grid_spec=pltpu.PrefetchScalarGridSpec(
            num_scalar_prefetch=1, grid=(3,),
            in_specs=[pl.BlockSpec((None, N_SHARD, None, Do, D), lambda a, c: (a, 0, c[0], 0, 0)),
                      pl.BlockSpec((None, N_SHARD, Do, D), lambda a, c: (a, 0, 0, 0))],
            out_specs=pl.BlockSpec((None, N_SHARD, Do, D), lambda a, c: (a, 0, 0, 0))),
        out_shape=jax.ShapeDtypeStruct((3, N_SHARD, Do, D), BF16),
        compiler_params=_cparams(("parallel",)),
    )(cidx, dw3, r3)


def _chips_exchange_in(cs_in):
    _, Dh, Wc = cs_in.shape

    def build(ins, outs, send, recv):
        pos = _position()
        return [_remote(ins[0].at[_shard_of(_peer(pos, k))], outs[0].at[j], send, recv, j, _peer(pos, k))
                for j, k in enumerate(CHIP_FLIPS)]

    return _Exchange((cs_in,), (jax.ShapeDtypeStruct((3, Dh, Wc), BF16),), 3, build)


def _chips_exchange_w3(cs_3):
    _, _, Do, D = cs_3.shape

    def build(ins, outs, send, recv):
        pos = _position()
        return [_remote(ins[0].at[:, _shard_of(_peer(pos, k))], outs[0].at[j], send, recv, j, _peer(pos, k))
                for j, k in enumerate(CHIP_FLIPS)]

    return _Exchange((cs_3,), (jax.ShapeDtypeStruct((3, 3, Do, D), BF16),), 3, build)


def _sum_chips_in(csidx, cs_in, rb_in):
    _, Dh, Wc = cs_in.shape
    tr = min(256, Dh)

    def body(s_ref, a_ref, b_ref, o_ref):
        acc = a_ref[...].astype(F32)
        for j in range(3):
            acc = acc + b_ref[j].astype(F32)
        o_ref[...] = acc

    return pl.pallas_call(
        body, name="sum_chips_in",
        grid_spec=pltpu.PrefetchScalarGridSpec(
            num_scalar_prefetch=1, grid=(Dh // tr,),
            in_specs=[pl.BlockSpec((None, tr, Wc), lambda i, s: (s[1], i, 0)),
                      pl.BlockSpec((3, tr, Wc), lambda i, s: (0, i, 0))],
            out_specs=pl.BlockSpec((None, tr, Wc), lambda i, s: (s[0], i, 0))),
        out_shape=jax.ShapeDtypeStruct((2, Dh, Wc), F32),
        compiler_params=_cparams(("parallel",)),
    )(csidx, cs_in, rb_in)


def _sum_chips_w3(csidx, cs_3, rb_3):
    _, _, Do, D = cs_3.shape

    def body(s_ref, a_ref, b_ref, o_ref):
        acc = a_ref[...].astype(F32)
        for j in range(3):
            acc = acc + b_ref[j].astype(F32)
        o_ref[...] = acc

    return pl.pallas_call(
        body, name="sum_chips_w3",
        grid_spec=pltpu.PrefetchScalarGridSpec(
            num_scalar_prefetch=1, grid=(3,),
            in_specs=[pl.BlockSpec((None, None, Do, D), lambda a, s: (a, s[1], 0, 0)),
                      pl.BlockSpec((3, None, Do, D), lambda a, s: (0, a, 0, 0))],
            out_specs=pl.BlockSpec((None, None, Do, D), lambda a, s: (a, s[0], 0, 0))),
        out_shape=jax.ShapeDtypeStruct((3, 2, Do, D), F32),
        compiler_params=_cparams(("parallel",)),
    )(csidx, cs_3, rb_3)


def _adam_math(w, g, m, v):
    m = ADAM_B1 * m + (1.0 - ADAM_B1) * g
    v = ADAM_B2 * v + (1.0 - ADAM_B2) * (g * g)
    m_hat = m / (1.0 - ADAM_B1 ** ADAM_STEP)
    v_hat = v / (1.0 - ADAM_B2 ** ADAM_STEP)
    delta = -ADAM_LR * (m_hat / (jnp.sqrt(v_hat) + ADAM_EPS) + ADAM_WD * w)
    return delta, m, v


def _adamw(w, g, m, v, name):
    R, C = w.shape
    tr = min(128, R)

    def body(w_ref, g_ref, m_ref, v_ref, d_ref, nm_ref, nv_ref):
        d_ref[...], nm_ref[...], nv_ref[...] = _adam_math(w_ref[...], g_ref[...], m_ref[...], v_ref[...])

    blk = pl.BlockSpec((tr, C), lambda i: (i, 0))
    return pl.pallas_call(
        body, name=name, grid=(R // tr,), in_specs=[blk] * 4, out_specs=[blk] * 3,
        out_shape=[jax.ShapeDtypeStruct((R, C), F32)] * 3,
        compiler_params=_cparams(("parallel",), VMEM_LIMIT),
    )(w, g, m, v)


def _adamw3(ws, g3, ms, vs):
    R, C = ws[0].shape

    def body(*refs):
        w_refs, m_refs, v_refs = refs[0:3], refs[3:6], refs[6:9]
        g_ref, outs = refs[9], refs[10:]
        for a in range(3):
            @pl.when(pl.program_id(0) == a)
            def _(a=a):
                res = _adam_math(w_refs[a][...], g_ref[...], m_refs[a][...], v_refs[a][...])
                for q in range(3):
                    outs[3 * a + q][...] = res[q]

    full = pl.BlockSpec((R, C), lambda a: (0, 0))
    res = pl.pallas_call(
        body, name="adamw_w3", grid=(3,),
        in_specs=[full] * 9 + [pl.BlockSpec((None, R, C), lambda a: (a, 0, 0))], out_specs=[full] * 9,
        out_shape=[jax.ShapeDtypeStruct((R, C), F32)] * 9,
        compiler_params=_cparams(("arbitrary",), VMEM_LIMIT),
    )(*ws, *ms, *vs, g3)
    return res[0:3], res[3:6], res[6:9]


SMALL_ROWS = ("c_ctx", "norm_w", "conv_b", "gn_w", "final_norm_w")


def _bwd_small(stats, ada_w, Dq, gh_in, gh_3):
    D = stats[0].shape[1]
    Wm = ada_w.shape[1]

    def body(stx, stm, stc, stv, stl, stlc, aw_ref, gi_in, g3_in, tot_ref, dm_sh, gcw, da_ref, loss_ref, gi_ref, g3_ref,
             vec_ref, vbuf, dm, amine, abuf, s_v, r_v, s_a, r_a, s_g, r_g):
        pos = _position()
        me, s = _dev_id(pos), _shard_of(pos)
        c, sib = pos[2], _peer(pos, 1)
        halves = [_remote(gi_in.at[c], gi_ref.at[c], s_g, r_g, 0, sib),
                  _remote(g3_in.at[:, c], g3_ref.at[:, c], s_g, r_g, 1, sib)]
        for cp in halves:
            cp.start()
        vec_ref[...] = jnp.zeros_like(vec_ref)
        vec_ref[0:2, :] = stx[0:2, :]
        vec_ref[2:3, :] = stm[1:2, :]
        vec_ref[3:5, :] = stc[0:2, :]
        vec_ref[5:6, :] = stx[2:3, :] + stc[2:3, :]
        vec_ref[6:7, :] = stv[3:4, :]
        vec_ref[7:8, :] = stm[3:4, :]
        vec_ref[8:9, :] = stm[0:1, :]
        vec_ref[9:12, :] = stv[0:3, :]
        vec_ref[12:14, 0:128] = stl[0:2, :] + stlc[0:2, :]
        vec_ref[14:15, :] = stm[2:3, :]
        vbuf[me] = vec_ref[...]
        sends = [_remote(vec_ref, vbuf.at[me], s_v, r_v, k - 1, _peer(pos, k)) for k in range(1, 8)]
        for cp in sends:
            cp.start()
        for k in range(1, 8):
            _remote(vec_ref, vbuf.at[_dev_id(_peer(pos, k))], s_v, r_v, k - 1, _peer(pos, k)).wait_recv()
        tot = vbuf[0]
        for d in range(1, N_DEV):
            tot = tot + vbuf[d]
        loss_ref[...] = jnp.zeros((8, 128), F32) + (0.5 / D) * _sum_all(tot[14:15, :])
        dm[...] = jnp.zeros_like(dm)
        for d in range(N_DEV):
            for r in range(3):
                dm[d:d + 1, r * D:(r + 1) * D] = vbuf[d, r:r + 1, :]
        dm[8:9, 0:D] = tot[3:4, :]
        dm[8:9, D:2 * D] = tot[4:5, :]
        for t in range(N_SHARD):
            @pl.when(s == t)
            def _(t=t):
                dm_sh[...] = dm[:, t * Wm:(t + 1) * Wm]
                gcw[...] = tot[9:12, t * Dq:(t + 1) * Dq]
        tot_ref[...] = tot
        part = lax.dot_general(dm_sh[8:16, :], aw_ref[...], (((1,), (1,)), ((), ())),
                               precision=lax.Precision.HIGHEST, preferred_element_type=F32)
        amine[...] = part
        abuf[s] = part
        asend = [_remote(amine, abuf.at[s], s_a, r_a, j, _peer(pos, k)) for j, k in enumerate(CHIP_FLIPS)]
        for cp in asend:
            cp.start()
        for j, k in enumerate(CHIP_FLIPS):
            _remote(amine, abuf.at[_shard_of(_peer(pos, k))], s_a, r_a, j, _peer(pos, k)).wait_recv()
        da = abuf[0]
        for t in range(1, N_SHARD):
            da = da + abuf[t]
        da_ref[...] = da
        _remote(gi_in.at[1 - c], gi_ref.at[1 - c], s_g, r_g, 0, sib).wait_recv()
        _remote(g3_in.at[:, 1 - c], g3_ref.at[:, 1 - c], s_g, r_g, 1, sib).wait_recv()
        for cp in sends + asend + halves:
            cp.wait_send()

    row = lambda *shape: jax.ShapeDtypeStruct(shape, F32)
    return pl.pallas_call(
        body, name="bwd_small",
        in_specs=[VMEM_FULL] * 7 + [ANY, ANY], out_specs=[VMEM_FULL] * 5 + [ANY, ANY],
        input_output_aliases={7: 5, 8: 6},
        out_shape=[row(16, D), row(16, Wm), row(3, Dq), row(8, D), row(8, 128), row(*gh_in.shape), row(*gh_3.shape)],
        scratch_shapes=[pltpu.VMEM((16, D), F32), pltpu.VMEM((N_DEV, 16, D), F32), pltpu.VMEM((16, 3 * D), F32),
                        pltpu.VMEM((8, D), F32), pltpu.VMEM((N_SHARD, 8, D), F32),
                        pltpu.SemaphoreType.DMA((7,)), pltpu.SemaphoreType.DMA((7,)),
                        pltpu.SemaphoreType.DMA((3,)), pltpu.SemaphoreType.DMA((3,)),
                        pltpu.SemaphoreType.DMA((2,)), pltpu.SemaphoreType.DMA((2,))],
        compiler_params=_cparams(None, VMEM_LIMIT),
    )(*stats, ada_w, gh_in, gh_3)


def _small_update(tot, dm_sh, gcw, da, act, p_row, p_ab, p_cw, p_dl):
    D = act.shape[1]
    Wm = dm_sh.shape[1]
    Dq = gcw.shape[1]

    def body(tot_ref, dm_ref, gcw_ref, da_ref, act_ref, prow, pab, pcw, pdl, gaw_ref, *outs):
        o_q = [outs[8 * q:8 * (q + 1)] for q in range(4)]
        tot = tot_ref[...]
        gaw_ref[...] = lax.dot_general(act_ref[...], dm_ref[...], (((0,), (0,)), ((), ())),
                                       precision=lax.Precision.HIGHEST, preferred_element_type=F32)
        cc = prow[0, 0:1, :]
        sg = _sigmoid(cc)
        g_cctx = da_ref[0:1, :] * (sg * (1.0 + cc * (1.0 - sg)))

        def place_all(o, val):
            o[...] = val

        def emit(k, w, g, m, v, place=place_all):
            for q, val in enumerate((g,) + _adam_math(w, g, m, v)):
                place(o_q[q][k], val)

        g_rows = [g_cctx, tot[5:6, :], tot[6:7, :], tot[7:8, :], tot[8:9, :]]
        for k, g in enumerate(g_rows):
            emit(k, prow[0, k:k + 1, :], g, prow[1, k:k + 1, :], prow[2, k:k + 1, :])

        def place_ab(o, val):
            for r in range(3):
                o[0:1, r * D:(r + 1) * D] = val[r:r + 1, :]

        g_ab = jnp.concatenate([tot[0:1, :] + tot[3:4, :], tot[1:2, :] + tot[4:5, :], tot[2:3, :]], axis=0)
        emit(5, pab[0], g_ab, pab[1], pab[2], place_ab)
        emit(6, pcw[0], gcw_ref[...], pcw[1], pcw[2])
        g_dl = jnp.concatenate([tot[12:14, 0:128] * _sigmoid(-pdl[0, 0:2, :]), jnp.zeros((6, 128), F32)], axis=0)
        emit(7, pdl[0], g_dl, pdl[1], pdl[2])

    row = lambda *shape: jax.ShapeDtypeStruct(shape, F32)
    per_q = [row(1, D)] * 5 + [row(1, 3 * D), row(3, Dq), row(8, 128)]
    res = pl.pallas_call(
        body, name="small_update",
        in_specs=[VMEM_FULL] * 9, out_specs=[VMEM_FULL] * 33,
        out_shape=[row(D, Wm)] + per_q * 4,
        compiler_params=_cparams(None, VMEM_LIMIT),
    )(tot, dm_sh, gcw, da, act, p_row, p_ab, p_cw, p_dl)
    return res[0], [res[1 + 8 * q:1 + 8 * (q + 1)] for q in range(4)]


def _pad_rows(a, rows=8):
    return jnp.pad(a, ((0, rows - a.shape[0]), (0, 0)))


def kernel(x, c, ctx, c_ctx, norm_w, ada_w, ada_b, w_in, conv_w, conv_b, decay_logit, gn_w, w_a, w_b, w_out, final_norm_w, loss_target, m_c_ctx, m_norm_w, m_ada_w, m_ada_b, m_w_in, m_conv_w, m_conv_b, m_decay_logit, m_gn_w, m_w_a, m_w_b, m_w_out, m_final_norm_w, v_c_ctx, v_norm_w, v_ada_w, v_ada_b, v_w_in, v_conv_w, v_conv_b, v_decay_logit, v_gn_w, v_w_a, v_w_b, v_w_out, v_final_norm_w):
    L, D = x.shape[1], x.shape[2]
    H = D // DV
    Wc = w_in.shape[2]
    Do = D // 8
    pos = _position()
    me = _dev_id(pos)
    cidx = jnp.reshape(pos[2], (1,)).astype(jnp.int32)
    sidx = jnp.reshape(_shard_of(pos), (1,)).astype(jnp.int32)

    act, mod, conv_w8 = _fwd_small(_pad_rows(c), _pad_rows(c_ctx[None]), ada_w[0], ada_b, _pad_rows(conv_w[0]))
    mod_x = lax.dynamic_slice_in_dim(mod, me, 1, axis=0).reshape(3, D)
    mod_c = mod[8].reshape(3, D)
    lg = jax.nn.log_sigmoid(decay_logit[0])

    w3_s = tuple(w[0].reshape(2, Do, D) for w in (w_a, w_b, w_out))

    def project(xm, c2, s2):
        p, qr, kr, w_in_full, w3_full = _ag_in_proj(xm, w_in[0], w3_s, c2, s2)
        return p, qr, kr, w_in_full, w3_full.reshape(3, D, D)

    csidx = jnp.concatenate([cidx, sidx])
    groups, dret_c, xmt, cmt, dx1, sc_x, w_in_full, gh_3, sts = _local_step(
        x[0], ctx[0], loss_target[0], mod_x, mod_c, norm_w, conv_w8, conv_b, lg, gn_w, final_norm_w[None],
        project, csidx)
    st_mid, st_conv, st_lg, st_lgc, st_c = sts

    dw_mine, ra_in = _dw_in(xmt, groups, cmt, dret_c, D, True)
    cs_in = _sum_pair_in(dw_mine, ra_in)
    grad_x, st_x, rb_in = _dxm(groups, 0, w_in_full, x[0], norm_w, sc_x, dx1, "dxm_x", _chips_exchange_in(cs_in))
    gh_in = _sum_chips_in(csidx, cs_in, rb_in)

    zeros3 = jnp.zeros((3, D), F32)
    p_row = jnp.concatenate(
        [r for t in ((c_ctx[None], norm_w, conv_b, gn_w, final_norm_w[None], zeros3),
                     (m_c_ctx[None], m_norm_w, m_conv_b, m_gn_w, m_final_norm_w[None], zeros3),
                     (v_c_ctx[None], v_norm_w, v_conv_b, v_gn_w, v_final_norm_w[None], zeros3)) for r in t],
        axis=0).reshape(3, 8, D)
    p_ab = jnp.concatenate([ada_b, m_ada_b, v_ada_b], axis=0).reshape(3, 3, D)
    p_cw = jnp.concatenate([conv_w, m_conv_w, v_conv_w], axis=0)
    p_dl = jnp.pad(jnp.concatenate([decay_logit, m_decay_logit, v_decay_logit], axis=0), ((0, 0), (0, 6), (0, 128 - H)))
    tot, dm_sh, gcw, da, loss_t, g_in, g_3 = _bwd_small((st_x, st_mid, st_c, st_conv, st_lg, st_lgc), ada_w[0],
                                                        conv_w.shape[2], gh_in, gh_3)
    g_w_in = g_in.reshape(D, Wc)
    g_3 = g_3.reshape(3, D // 4, D)
    g_ada_w, small = _small_update(tot, dm_sh, gcw, da, act, p_row, p_ab, p_cw, p_dl)

    upd_in = _adamw(w_in[0], g_w_in, m_w_in[0], v_w_in[0], "adamw_w_in")
    upd_ada = _adamw(ada_w[0], g_ada_w, m_ada_w[0], v_ada_w[0], "adamw_ada_w")
    upd_a, upd_b, upd_o = _adamw3((w_a[0], w_b[0], w_out[0]), g_3, (m_w_a[0], m_w_b[0], m_w_out[0]),
                                  (v_w_a[0], v_w_b[0], v_w_out[0]))

    def leaves(q):
        big = lambda g, upd: (g if q == 0 else upd[q - 1])[None]
        r_cctx, r_norm, r_convb, r_gn, r_fnorm, r_ab, r_cw, r_dl = small[q]
        return [r_cctx.reshape(D), r_norm, big(g_ada_w, upd_ada), r_ab, big(g_w_in, upd_in),
                r_cw[None], r_convb, r_dl[0:2, 0:H][None], r_gn,
                big(g_3[0], upd_a), big(g_3[1], upd_b), big(g_3[2], upd_o), r_fnorm.reshape(D)]

    loss = loss_t[0, 0]
    return (loss, grad_x[None], *leaves(0), *leaves(1), *leaves(2), *leaves(3))
```

```python
from typing import Callable, NamedTuple

import jax
import jax.numpy as jnp
from jax import lax
from jax.experimental import pallas as pl
from jax.experimental.pallas import tpu as pltpu

F32 = jnp.float32
BF16 = jnp.bfloat16
MESH = pl.DeviceIdType.MESH

CHUNK = 128
RET_CPB = 4
DV = 128
DK = 64
GRID_W = 64
ROPE_BASE = 10000.0
EPS = 1e-6
K_SCALE = DK ** -0.5
N_SHARD = 4
N_DEV = 8

ADAM_LR = 0.001
ADAM_B1 = 0.9
ADAM_B2 = 0.999
ADAM_EPS = 1e-08
ADAM_WD = 0.01
ADAM_STEP = 10

VMEM_LIMIT = 56 * 1024 * 1024


def _cparams(sem=None, vmem=None):
    kw = {}
    if sem is not None:
        kw["dimension_semantics"] = sem
    if vmem is not None:
        kw["vmem_limit_bytes"] = vmem
    return pltpu.CompilerParams(**kw)


def _dot(a, b):
    return jnp.dot(a, b, preferred_element_type=F32)


def _dot_nt(a, b):
    return lax.dot_general(a, b, (((1,), (1,)), ((), ())), preferred_element_type=F32)


def _dot_tn(a, b):
    return lax.dot_general(a, b, (((0,), (0,)), ((), ())), preferred_element_type=F32)


def _sigmoid(x):
    return 1.0 / (1.0 + jnp.exp(-x))


def _sum_all(x):
    return jnp.sum(jnp.sum(x, axis=1, keepdims=True), axis=0, keepdims=True)


def _swap_halves(t):
    n = t.shape[1]
    lane = lax.broadcasted_iota(jnp.int32, t.shape, 1)
    low = (lane & 32) == 0
    return jnp.where(low, pltpu.roll(t, n - 32, 1), pltpu.roll(t, 32, 1))


def _vec_spec(d):
    return pl.BlockSpec((1, d), lambda *a: (0, 0))


def _norm_mod(x, nw, sc, sh, name):
    L, D = x.shape
    tl = min(256, L)

    def body(x_ref, nw_ref, sc_ref, sh_ref, xm_ref, xmt_ref):
        xv = x_ref[...]
        r = lax.rsqrt(jnp.mean(xv * xv, axis=-1, keepdims=True) + EPS)
        xm = (xv * r * nw_ref[...]) * (1.0 + sc_ref[...]) + sh_ref[...]
        xm_b = xm.astype(BF16)
        xm_ref[...] = xm_b
        xmt_ref[...] = xm_b.T

    return pl.pallas_call(
        body, name=name, grid=(L // tl,),
        in_specs=[pl.BlockSpec((tl, D), lambda i: (i, 0)), _vec_spec(D), _vec_spec(D), _vec_spec(D)],
        out_specs=[pl.BlockSpec((tl, D), lambda i: (i, 0)), pl.BlockSpec((D, tl), lambda i: (0, i))],
        out_shape=[jax.ShapeDtypeStruct((L, D), BF16), jax.ShapeDtypeStruct((D, L), BF16)],
        compiler_params=_cparams(("parallel",)),
    )(x, nw, sc, sh)


QK_BLOCK, V_BLOCK = 4, 5


def _in_proj(xm, w, name, first=0, count=None):
    M, D = xm.shape
    count = w.shape[1] // D if count is None else count
    tm = min(1024, M)

    def body(a_ref, b_ref, o_ref, qk_ref):
        acc = _dot(a_ref[...], b_ref[...])
        o_ref[...] = acc.astype(o_ref.dtype)

        @pl.when(pl.program_id(1) == QK_BLOCK - first)
        def _():
            qk_ref[...] = acc

    return pl.pallas_call(
        body, name=name, grid=(M // tm, count),
        in_specs=[pl.BlockSpec((tm, D), lambda i, j: (i, 0)), pl.BlockSpec((D, D), lambda i, j: (0, first + j))],
        out_specs=[pl.BlockSpec((tm, D), lambda i, j: (i, j)), pl.BlockSpec((tm, D), lambda i, j: (i, 0))],
        out_shape=[jax.ShapeDtypeStruct((M, count * D), BF16), jax.ShapeDtypeStruct((M, D), F32)],
        compiler_params=_cparams(("parallel", "arbitrary")),
    )(xm, w)


def _halo_specs(tl, L, D, col):
    hb = tl // 16
    last = L // 16 - 1
    prev = pl.BlockSpec((16, D), lambda i: (jnp.maximum(i * hb - 1, 0), col))
    nxt = pl.BlockSpec((16, D), lambda i: (jnp.minimum((i + 1) * hb, last), col))
    return prev, nxt


def _shift_rows(u, above, below):
    tl = u.shape[0]
    row = lax.broadcasted_iota(jnp.int32, u.shape, 0)
    dn = jnp.where(row == 0, above, pltpu.roll(u, 1, 0))
    up = jnp.where(row == tl - 1, below, pltpu.roll(u, tl - 1, 0))
    return dn, up


def _rope_tables(L):
    pos = jnp.arange(L)
    row = (pos // GRID_W).astype(F32)
    col = (pos % GRID_W).astype(F32)
    nf = DK // 4
    inv = ROPE_BASE ** (-jnp.arange(nf, dtype=F32) / nf)
    ang = jnp.concatenate([row[:, None] * inv, col[:, None] * inv], axis=-1)
    cos, sin = jnp.cos(ang), jnp.sin(ang)
    return jnp.concatenate([cos, cos, cos, cos], axis=-1), jnp.concatenate([-sin, sin, -sin, sin], axis=-1)


def _smem_spec():
    return pl.BlockSpec(memory_space=pltpu.SMEM)


def _pair_select(e0, e1):
    row = lax.broadcasted_iota(jnp.int32, e0.shape, 0)
    return jnp.where(row < DK, e0, e1)


def _head_lane_mask(shape, e):
    lane = lax.broadcasted_iota(jnp.int32, shape, 1)
    return (lane < DK) if e == 0 else (lane >= DK)


def _ctx_states(pc, pqk_c, lg, D):
    Lc = pc.shape[0]
    H = D // DV

    def body(lg_ref, k_ref, v_ref, s_ref):
        m = lax.broadcasted_iota(jnp.int32, (Lc, DV), 0).astype(F32)
        for pr in range(H // 2):
            k2 = k_ref[:, pr * 128:(pr + 1) * 128].astype(F32) * K_SCALE
            res = [[None, None], [None, None]]
            for e in range(2):
                h = 2 * pr + e
                v = v_ref[:, h * DV:(h + 1) * DV]
                dec_f = jnp.exp(lg_ref[0, h] * (Lc - 1.0 - m))
                dec_b = jnp.exp(lg_ref[1, h] * m)
                res[0][e] = _dot_tn((k2 * dec_f).astype(BF16), v)
                res[1][e] = _dot_tn((k2 * dec_b).astype(BF16), v)
            s_ref[0, pr] = _pair_select(res[0][0], res[0][1])
            s_ref[1, pr] = _pair_select(res[1][0], res[1][1])

    return pl.pallas_call(
        body, name="ctx_states", grid=(1,),
        in_specs=[_smem_spec(), pl.BlockSpec((Lc, D // 2), lambda i: (0, 1)), pl.BlockSpec((Lc, D), lambda i: (0, 1))],
        out_specs=pl.BlockSpec((2, H // 2, 128, 128), lambda i: (0, 0, 0, 0)),
        out_shape=jax.ShapeDtypeStruct((2, H // 2, 128, 128), F32),
    )(lg, pqk_c, pc)


T_M, T_MT = 0, 1
T_MF1, T_MB1 = 2, 3
T_QF, T_QB = 4, 5
T_KF, T_KB = 6, 7


def _decay_tables(lg, H):
    def body(lg_ref, t_ref):
        h = pl.program_id(0)
        lgf, lgb = lg_ref[0, h], lg_ref[1, h]
        i = lax.broadcasted_iota(jnp.int32, (CHUNK, CHUNK), 0).astype(F32)
        j = lax.broadcasted_iota(jnp.int32, (CHUNK, CHUNK), 1).astype(F32)
        d = i - j
        mf = jnp.where(d > 0, jnp.exp(lgf * jnp.maximum(d, 0.0)), 0.0)
        mb = jnp.where(d < 0, jnp.exp(lgb * jnp.maximum(-d, 0.0)), 0.0)
        mf_t = jnp.where(d < 0, jnp.exp(lgf * jnp.maximum(-d, 0.0)), 0.0)
        mb_t = jnp.where(d > 0, jnp.exp(lgb * jnp.maximum(d, 0.0)), 0.0)
        diag = jnp.where(d == 0, 2.0, 0.0)
        t_ref[0, T_M] = mf + mb + diag
        t_ref[0, T_MT] = mf_t + mb_t + diag
        t_ref[0, T_MF1] = mf * d
        t_ref[0, T_MB1] = mb * (-d)
        t_ref[0, T_QF] = jnp.exp(lgf * (i + 1.0))
        t_ref[0, T_QB] = jnp.exp(lgb * (CHUNK - i))
        t_ref[0, T_KF] = jnp.exp(lgf * (CHUNK - 1.0 - i))
        t_ref[0, T_KB] = jnp.exp(lgb * i)

    return pl.pallas_call(
        body, name="decay_tables", grid=(H,), in_specs=[_smem_spec()],
        out_specs=pl.BlockSpec((1, 8, CHUNK, CHUNK), lambda h: (h, 0, 0, 0)),
        out_shape=jax.ShapeDtypeStruct((H, 8, CHUNK, CHUNK), F32),
    )(lg)


def _tab_spec(H):
    return pl.BlockSpec((H, 8, CHUNK, CHUNK), lambda n: (0, 0, 0, 0))


def _chunk_decay(tab_ref, h):
    return tab_ref[h, T_QF, CHUNK - 1:CHUNK, :], tab_ref[h, T_QB, 0:1, :]


def _ret_states(kr, p, s0, tab, D):
    L = kr.shape[0]
    H = D // DV
    N = L // CHUNK
    HP = H // 2

    def body(tab_ref, kf_ref, kb_ref, vf_ref, vb_ref, s0_ref, sf_out, sb_out, sf, sb):
        n = pl.program_id(0)

        @pl.when(n == 0)
        def _():
            sf[...] = s0_ref[0]
            sb[...] = s0_ref[1]

        for cc in range(RET_CPB):
            cf_, cb_ = cc, RET_CPB - 1 - cc
            rf, rb = slice(cf_ * CHUNK, (cf_ + 1) * CHUNK), slice(cb_ * CHUNK, (cb_ + 1) * CHUNK)
            sf_out[cf_] = sf[...]
            sb_out[cb_] = sb[...]
            for pr in range(HP):
                kf2 = kf_ref[rf, pr * 128:(pr + 1) * 128].astype(F32)
                kb2 = kb_ref[rb, pr * 128:(pr + 1) * 128].astype(F32)
                inc_f, inc_b, gf, gb = [], [], [], []
                for e in range(2):
                    h = 2 * pr + e
                    inc_f.append(_dot_tn((kf2 * tab_ref[h, T_KF]).astype(BF16), vf_ref[rf, h * DV:(h + 1) * DV]))
                    inc_b.append(_dot_tn((kb2 * tab_ref[h, T_KB]).astype(BF16), vb_ref[rb, h * DV:(h + 1) * DV]))
                    cf, cb = _chunk_decay(tab_ref, h)
                    gf.append(jnp.broadcast_to(cf, (128, 128)))
                    gb.append(jnp.broadcast_to(cb, (128, 128)))
                sf[pr] = _pair_select(gf[0], gf[1]) * sf[pr] + _pair_select(inc_f[0], inc_f[1])
                sb[pr] = _pair_select(gb[0], gb[1]) * sb[pr] + _pair_select(inc_b[0], inc_b[1])

    st = jax.ShapeDtypeStruct((N, HP, 128, 128), F32)
    R = RET_CPB * CHUNK
    NB = N // RET_CPB
    return _riding_call(
        body, None, NB, name="ret_states", args=(tab, kr, kr, p, p, s0),
        in_specs=[_tab_spec(H),
                  pl.BlockSpec((R, D // 2), lambda n: (n, 0)),
                  pl.BlockSpec((R, D // 2), lambda n: (NB - 1 - n, 0)),
                  pl.BlockSpec((R, D), lambda n: (n, 5)),
                  pl.BlockSpec((R, D), lambda n: (NB - 1 - n, 5)),
                  pl.BlockSpec((2, HP, 128, 128), lambda n: (0, 0, 0, 0))],
        out_specs=[pl.BlockSpec((RET_CPB, HP, 128, 128), lambda n: (n, 0, 0, 0)),
                   pl.BlockSpec((RET_CPB, HP, 128, 128), lambda n: (NB - 1 - n, 0, 0, 0))],
        out_shape=[st, st],
        scratch=[pltpu.VMEM((HP, 128, 128), F32), pltpu.VMEM((HP, 128, 128), F32)],
        cparams=_cparams(("arbitrary",)))


def _ret_out(qr, kr, p, sf_prev, sb_prev, gn_w, tab, D):
    L = qr.shape[0]
    H = D // DV
    N = L // CHUNK
    HP = H // 2

    def body(tab_ref, q_ref, k_ref, v_ref, zb_ref, sf_ref, sb_ref, gn_ref, o_ref, yb_ref):
        def chunk(cc, carry):
            rows = pl.ds(pl.multiple_of(cc * CHUNK, CHUNK), CHUNK)
            for pr in range(HP):
                q2 = q_ref[rows, pr * 128:(pr + 1) * 128]
                k2 = k_ref[rows, pr * 128:(pr + 1) * 128]
                sfp = sf_ref[cc, pr].astype(BF16)
                sbp = sb_ref[cc, pr].astype(BF16)
                for e in range(2):
                    h = 2 * pr + e
                    sl = slice(h * DV, (h + 1) * DV)
                    qm = jnp.where(_head_lane_mask(q2.shape, e), q2, jnp.zeros_like(q2))
                    a = (_dot_nt(qm, k2) * tab_ref[h, T_M]).astype(BF16)
                    qf = qm.astype(F32)
                    o = _dot(a, v_ref[rows, sl])
                    o += _dot((qf * tab_ref[h, T_QF]).astype(BF16), sfp)
                    o += _dot((qf * tab_ref[h, T_QB]).astype(BF16), sbp)
                    o_ref[rows, sl] = o
                    mu = jnp.mean(o, axis=-1, keepdims=True)
                    oc = o - mu
                    rstd = lax.rsqrt(jnp.mean(oc * oc, axis=-1, keepdims=True) + EPS)
                    zb = zb_ref[rows, sl].astype(F32)
                    yb_ref[rows, sl] = (zb * _sigmoid(zb) * (oc * rstd * gn_ref[:, sl])).astype(BF16)
            return carry

        lax.fori_loop(0, RET_CPB, chunk, 0)

    R = RET_CPB * CHUNK
    return _riding_call(
        body, None, N // RET_CPB, name="ret_out", args=(tab, qr, kr, p, p, sf_prev, sb_prev, gn_w),
        in_specs=[_tab_spec(H),
                  pl.BlockSpec((R, D // 2), lambda n: (n, 0)),
                  pl.BlockSpec((R, D // 2), lambda n: (n, 0)),
                  pl.BlockSpec((R, D), lambda n: (n, 5)),
                  pl.BlockSpec((R, D), lambda n: (n, 6)),
                  pl.BlockSpec((RET_CPB, HP, 128, 128), lambda n: (n, 0, 0, 0)),
                  pl.BlockSpec((RET_CPB, HP, 128, 128), lambda n: (n, 0, 0, 0)),
                  _vec_spec(D)],
        out_specs=[pl.BlockSpec((R, D), lambda n: (n, 0)), pl.BlockSpec((R, D), lambda n: (n, 0))],
        out_shape=[jax.ShapeDtypeStruct((L, D), F32), jax.ShapeDtypeStruct((L, D), BF16)],
        cparams=_cparams(("arbitrary",)))


def _mid(p, yb, o, x, tgt, w3, g, fw, conv_w, conv_b, gn_w, D):
    L = x.shape[0]
    H = D // DV
    tm = min(256, L)
    nt = L // tm

    def body(h_ref, bg_ref, cg_ref, za_ref, hp_ref, hn_ref, cp_ref, cn_ref, yb_ref, ga_ref, gb_ref, zb_ref, o_ref,
             x_ref, t_ref, w_hbm, g_ref, fw_ref, cw_ref, cb_ref, gn_ref,
             dx1_ref, dya_ref, do_ref, dzb_ref, dgab_ref, dw_hbm, st_ref, w_vm, dw_acc, sem):
        i = pl.program_id(0)

        @pl.when(i == 0)
        def _():
            cp = pltpu.make_async_copy(w_hbm, w_vm, sem)
            cp.start()
            dw_acc[...] = jnp.zeros_like(dw_acc)
            st_ref[...] = jnp.zeros_like(st_ref)
            cp.wait()

        u = cg_ref[...].astype(F32) * h_ref[...].astype(F32)
        above = jnp.where(i == 0, 0.0, cp_ref[15:16, :].astype(F32) * hp_ref[15:16, :].astype(F32))
        below = jnp.where(i == nt - 1, 0.0, cn_ref[0:1, :].astype(F32) * hn_ref[0:1, :].astype(F32))
        dn, up = _shift_rows(u, above, below)
        co = cw_ref[0:1, :] * dn + cw_ref[1:2, :] * u + cw_ref[2:3, :] * up + cb_ref[...]
        za = za_ref[...].astype(F32)
        ya_b = (za * _sigmoid(za) * bg_ref[...].astype(F32) * co).astype(BF16)
        yb_b = yb_ref[...]
        y_a = _dot(ya_b, w_vm[0])
        y_b = _dot(yb_b, w_vm[1])
        sga = _sigmoid(ga_ref[...].astype(F32))
        sgb = _sigmoid(gb_ref[...].astype(F32))
        mix_b = (sga * y_a + sgb * y_b).astype(BF16)
        y_x = _dot(mix_b, w_vm[2])
        gvec, fwv = g_ref[...], fw_ref[...]
        x1 = x_ref[...] + gvec * y_x
        r1 = lax.rsqrt(jnp.mean(x1 * x1, axis=-1, keepdims=True) + EPS)
        xh = x1 * r1
        diff = xh * fwv - t_ref[...]
        dout = diff * (1.0 / D)
        dxh = dout * fwv
        dx1 = r1 * (dxh - xh * jnp.mean(dxh * xh, axis=-1, keepdims=True))
        dx1_ref[...] = dx1
        st_ref[0:1, :] += jnp.sum(dout * xh, axis=0, keepdims=True)
        st_ref[1:2, :] += jnp.sum(dx1 * y_x, axis=0, keepdims=True)
        st_ref[2:3, :] += jnp.sum(diff * diff, axis=0, keepdims=True)
        dyx_b = (dx1 * gvec).astype(BF16)
        dmix = _dot_nt(dyx_b, w_vm[2])
        dw_acc[2] += _dot_tn(mix_b, dyx_b)
        dya_b = (dmix * sga).astype(BF16)
        dyb_b = (dmix * sgb).astype(BF16)
        dgab_ref[:, 0:D] = (dmix * y_a * sga * (1.0 - sga)).astype(BF16)
        dgab_ref[:, D:2 * D] = (dmix * y_b * sgb * (1.0 - sgb)).astype(BF16)
        dya_ref[...] = _dot_nt(dya_b, w_vm[0])
        dyb = _dot_nt(dyb_b, w_vm[1])
        dw_acc[0] += _dot_tn(ya_b, dya_b)
        dw_acc[1] += _dot_tn(yb_b, dyb_b)

        for h in range(H):
            sl = slice(h * DV, (h + 1) * DV)
            ov = o_ref[:, sl]
            oc = ov - jnp.mean(ov, axis=-1, keepdims=True)
            rstd = lax.rsqrt(jnp.mean(oc * oc, axis=-1, keepdims=True) + EPS)
            rn = oc * rstd
            gw = gn_ref[:, sl]
            zb = zb_ref[:, sl].astype(F32)
            sz = _sigmoid(zb)
            dy = dyb[:, sl]
            dzb_ref[:, sl] = (dy * (rn * gw) * (sz * (1.0 + zb * (1.0 - sz)))).astype(BF16)
            dretn = dy * (zb * sz)
            st_ref[3:4, sl] += jnp.sum(dretn * rn, axis=0, keepdims=True)
            drn = dretn * gw
            do_ref[:, sl] = (rstd * (drn - jnp.mean(drn, axis=-1, keepdims=True)
                                     - rn * jnp.mean(drn * rn, axis=-1, keepdims=True))).astype(BF16)

        @pl.when(i == nt - 1)
        def _():
            out = pltpu.make_async_copy(dw_acc, dw_hbm, sem)
            out.start()
            out.wait()

    row = lambda col: pl.BlockSpec((tm, D), lambda i: (i, col))
    any_spec = pl.BlockSpec(memory_space=pl.ANY)
    f32o = jax.ShapeDtypeStruct((L, D), F32)
    bf16o = jax.ShapeDtypeStruct((L, D), BF16)
    hp, hn = _halo_specs(tm, L, D, 0)
    cp, cn = _halo_specs(tm, L, D, 2)
    return pl.pallas_call(
        body, name="mid", grid=(nt,),
        in_specs=[row(0), row(1), row(2), row(3), hp, hn, cp, cn, row(0), row(7), row(8), row(6), row(0),
                  row(0), row(0), any_spec, _vec_spec(D), _vec_spec(D),
                  pl.BlockSpec((8, D), lambda i: (0, 0)), _vec_spec(D), _vec_spec(D)],
        out_specs=[row(0), row(0), row(0), row(0), pl.BlockSpec((tm, 2 * D), lambda i: (i, 0)), any_spec,
                   pl.BlockSpec((8, D), lambda i: (0, 0))],
        out_shape=[f32o, f32o, bf16o, bf16o, jax.ShapeDtypeStruct((L, 2 * D), BF16),
                   jax.ShapeDtypeStruct((3, D, D), F32), jax.ShapeDtypeStruct((8, D), F32)],
        scratch_shapes=[pltpu.VMEM((3, D, D), BF16), pltpu.VMEM((3, D, D), F32), pltpu.SemaphoreType.DMA],
        compiler_params=_cparams(("arbitrary",), VMEM_LIMIT),
    )(p, p, p, p, p, p, p, p, yb, p, p, p, o, x, tgt, w3, g, fw, conv_w, conv_b, gn_w)


def _conv_bwd(dya, p, conv_w, conv_b, D, exchange=None):
    L = p.shape[0]
    tl = min(256, L)
    nt = L // tl

    def body(d_ref, h_ref, bg_ref, cg_ref, za_ref,
             dp_ref, dn_ref, hp_ref, hn_ref, bp_ref, bn_ref, cp_ref, cn_ref, zp_ref, zn_ref,
             w_ref, b_ref, dc_ref, st_ref):
        i = pl.program_id(0)

        @pl.when(i == 0)
        def _():
            st_ref[...] = jnp.zeros_like(st_ref)

        first, last = i == 0, i == nt - 1
        h = h_ref[...].astype(F32)
        cg = cg_ref[...].astype(F32)
        bg = bg_ref[...].astype(F32)
        za = za_ref[...].astype(F32)
        dy = d_ref[...].astype(F32)
        u = cg * h
        u_above = jnp.where(first, 0.0, cp_ref[15:16, :].astype(F32) * hp_ref[15:16, :].astype(F32))
        u_below = jnp.where(last, 0.0, cn_ref[0:1, :].astype(F32) * hn_ref[0:1, :].astype(F32))
        u_dn, u_up = _shift_rows(u, u_above, u_below)
        w0, w1, w2 = w_ref[0:1, :], w_ref[1:2, :], w_ref[2:3, :]
        co = w0 * u_dn + w1 * u + w2 * u_up + b_ref[...]
        sz = _sigmoid(za)
        silu = za * sz
        dc_ref[:, 3 * D:4 * D] = (dy * bg * co * (sz * (1.0 + za * (1.0 - sz)))).astype(BF16)
        dc_ref[:, D:2 * D] = (dy * silu * co).astype(BF16)
        dco = dy * silu * bg

        def edge(dr, zr, br, r):
            z = zr[r:r + 1, :].astype(F32)
            return dr[r:r + 1, :].astype(F32) * (z * _sigmoid(z)) * br[r:r + 1, :].astype(F32)

        dco_above = jnp.where(first, 0.0, edge(dp_ref, zp_ref, bp_ref, 15))
        dco_below = jnp.where(last, 0.0, edge(dn_ref, zn_ref, bn_ref, 0))
        dco_dn, dco_up = _shift_rows(dco, dco_above, dco_below)
        du = w0 * dco_up + w1 * dco + w2 * dco_dn
        dc_ref[:, 2 * D:3 * D] = (du * h).astype(BF16)
        dc_ref[:, 0:D] = (du * cg).astype(BF16)
        st_ref[0:1, :] += jnp.sum(dco * u_dn, axis=0, keepdims=True)
        st_ref[1:2, :] += jnp.sum(dco * u, axis=0, keepdims=True)
        st_ref[2:3, :] += jnp.sum(dco * u_up, axis=0, keepdims=True)
        st_ref[3:4, :] += jnp.sum(dco, axis=0, keepdims=True)

    main = lambda col: pl.BlockSpec((tl, D), lambda i: (i, col))
    halos = []
    for col in (0, 0, 1, 2, 3):
        halos.extend(_halo_specs(tl, L, D, col))
    return _riding_call(
        body, exchange, nt, name="conv_bwd",
        args=(dya, p, p, p, p, dya, dya, p, p, p, p, p, p, p, p, conv_w, conv_b),
        in_specs=[main(0), main(0), main(1), main(2), main(3)] + halos
                 + [pl.BlockSpec((8, D), lambda i: (0, 0)), _vec_spec(D)],
        out_specs=[pl.BlockSpec((tl, 4 * D), lambda i: (i, 0)), pl.BlockSpec((8, D), lambda i: (0, 0))],
        out_shape=[jax.ShapeDtypeStruct((L, 4 * D), BF16), jax.ShapeDtypeStruct((8, D), F32)],
        cparams=_cparams(("arbitrary",)))


def _ret_bwd_states(qr, do, tab, D):
    L = qr.shape[0]
    H = D // DV
    N = L // CHUNK
    HP = H // 2

    def body(tab_ref, qf_ref, qb_ref, dof_ref, dob_ref, dsf_out, dsb_out, ds0_out, dsf, dsb):
        n = pl.program_id(0)

        @pl.when(n == 0)
        def _():
            dsf[...] = jnp.zeros_like(dsf)
            dsb[...] = jnp.zeros_like(dsb)

        for cc in range(RET_CPB):
            cf_, cb_ = RET_CPB - 1 - cc, cc
            rf, rb = slice(cf_ * CHUNK, (cf_ + 1) * CHUNK), slice(cb_ * CHUNK, (cb_ + 1) * CHUNK)
            dsf_out[cf_] = dsf[...]
            dsb_out[cb_] = dsb[...]
            for pr in range(HP):
                qf2 = qf_ref[rf, pr * 128:(pr + 1) * 128].astype(F32)
                qb2 = qb_ref[rb, pr * 128:(pr + 1) * 128].astype(F32)
                inc_f, inc_b, gf, gb = [], [], [], []
                for e in range(2):
                    h = 2 * pr + e
                    inc_f.append(_dot_tn((qf2 * tab_ref[h, T_QF]).astype(BF16), dof_ref[rf, h * DV:(h + 1) * DV]))
                    inc_b.append(_dot_tn((qb2 * tab_ref[h, T_QB]).astype(BF16), dob_ref[rb, h * DV:(h + 1) * DV]))
                    cf, cb = _chunk_decay(tab_ref, h)
                    gf.append(jnp.broadcast_to(cf, (128, 128)))
                    gb.append(jnp.broadcast_to(cb, (128, 128)))
                dsf[pr] = _pair_select(gf[0], gf[1]) * dsf[pr] + _pair_select(inc_f[0], inc_f[1])
                dsb[pr] = _pair_select(gb[0], gb[1]) * dsb[pr] + _pair_select(inc_b[0], inc_b[1])

        @pl.when(n == NB - 1)
        def _():
            ds0_out[0] = dsf[...]
            ds0_out[1] = dsb[...]

    st = jax.ShapeDtypeStruct((N, HP, 128, 128), F32)
    R = RET_CPB * CHUNK
    NB = N // RET_CPB
    return pl.pallas_call(
        body, name="ret_bwd_states", grid=(NB,),
        in_specs=[_tab_spec(H),
                  pl.BlockSpec((R, D // 2), lambda n: (NB - 1 - n, 0)),
                  pl.BlockSpec((R, D // 2), lambda n: (n, 0)),
                  pl.BlockSpec((R, D), lambda n: (NB - 1 - n, 0)),
                  pl.BlockSpec((R, D), lambda n: (n, 0))],
        out_specs=[pl.BlockSpec((RET_CPB, HP, 128, 128), lambda n: (NB - 1 - n, 0, 0, 0)),
                   pl.BlockSpec((RET_CPB, HP, 128, 128), lambda n: (n, 0, 0, 0)),
                   pl.BlockSpec((2, HP, 128, 128), lambda n: (0, 0, 0, 0))],
        out_shape=[st, st, jax.ShapeDtypeStruct((2, HP, 128, 128), F32)],
        scratch_shapes=[pltpu.VMEM((HP, 128, 128), F32), pltpu.VMEM((HP, 128, 128), F32)],
        compiler_params=_cparams(("arbitrary",)),
    )(tab, qr, qr, do, do)


def _ret_bwd_main(qr, kr, p, do, sf_prev, sb_prev, dsf, dsb, c2, s2, tab, D, exchange=None):
    L = qr.shape[0]
    H = D // DV
    N = L // CHUNK
    HP = H // 2
    W = D // 2

    def body(tab_ref, q_ref, k_ref, v_ref, do_ref, sf_ref, sb_ref, dsf_ref, dsb_ref, c_ref, s_ref,
             dr_ref, st_ref, dl_acc):
        @pl.when(pl.program_id(0) == 0)
        def _():
            dl_acc[...] = jnp.zeros_like(dl_acc)

        i = lax.broadcasted_iota(jnp.int32, (CHUNK, 128), 0).astype(F32)
        rowid = lax.broadcasted_iota(jnp.int32, (128, 128), 0)

        def chunk(cc, carry):
            rows = pl.ds(pl.multiple_of(cc * CHUNK, CHUNK), CHUNK)
            c, s = c_ref[rows, :], s_ref[rows, :]
            for pr in range(HP):
                ps = slice(pr * 128, (pr + 1) * 128)
                q2, k2 = q_ref[rows, ps], k_ref[rows, ps]
                sf32, sb32 = sf_ref[cc, pr], sb_ref[cc, pr]
                dsf32, dsb32 = dsf_ref[cc, pr], dsb_ref[cc, pr]
                sfp, sbp = sf32.astype(BF16), sb32.astype(BF16)
                dsfp, dsbp = dsf32.astype(BF16), dsb32.astype(BF16)
                dq2 = jnp.zeros((CHUNK, 128), F32)
                dk2 = jnp.zeros((CHUNK, 128), F32)
                for e in range(2):
                    h = 2 * pr + e
                    sl = slice(h * DV, (h + 1) * DV)
                    hm = _head_lane_mask(q2.shape, e)
                    qm = jnp.where(hm, q2, jnp.zeros_like(q2))
                    km = jnp.where(hm, k2, jnp.zeros_like(k2))
                    qf, kf = qm.astype(F32), km.astype(F32)
                    v, do = v_ref[rows, sl], do_ref[rows, sl]
                    vf, dof = v.astype(F32), do.astype(F32)
                    m_t = tab_ref[h, T_MT]
                    sc = _dot_nt(qm, k2)
                    dpm = _dot_nt(do, v)
                    dsc = (dpm * tab_ref[h, T_M]).astype(BF16)
                    a_t = (_dot_nt(km, q2) * m_t).astype(BF16)
                    dsc_t = (_dot_nt(v, do) * m_t).astype(BF16)
                    dq_f, dq_b = tab_ref[h, T_QF], tab_ref[h, T_QB]
                    dk_f, dk_b = tab_ref[h, T_KF], tab_ref[h, T_KB]
                    dq = _dot(dsc, km)
                    dq += jnp.where(hm, dq_f * _dot_nt(do, sfp) + dq_b * _dot_nt(do, sbp), 0.0)
                    dk = _dot(dsc_t, qm)
                    dk += jnp.where(hm, dk_f * _dot_nt(v, dsfp) + dk_b * _dot_nt(v, dsbp), 0.0)
                    kdf = _dot((kf * dk_f).astype(BF16), dsfp)
                    kdb = _dot((kf * dk_b).astype(BF16), dsbp)
                    dr_ref[rows, D + h * DV:D + (h + 1) * DV] = (_dot(a_t, do) + kdf + kdb).astype(BF16)
                    dq2 += dq
                    dk2 += dk
                    xf = _dot((qf * dq_f).astype(BF16), sfp)
                    xb = _dot((qf * dq_b).astype(BF16), sbp)
                    pair = (rowid < DK) if e == 0 else (rowid >= DK)
                    gcf, gcb = tab_ref[h, T_QF, CHUNK - 1:CHUNK, 0:1], tab_ref[h, T_QB, 0:1, 0:1]
                    scdp = sc * dpm
                    dl_acc[h, 0] += scdp * tab_ref[h, T_MF1] + xf * dof * (i + 1.0) \
                        + kdf * vf * (CHUNK - 1.0 - i) + (CHUNK * gcf) * jnp.where(pair, dsf32 * sf32, 0.0)
                    dl_acc[h, 1] += scdp * tab_ref[h, T_MB1] + xb * dof * (CHUNK - i) \
                        + kdb * vf * i + (CHUNK * gcb) * jnp.where(pair, dsb32 * sb32, 0.0)
                dr_ref[rows, ps] = (dq2 * c - _swap_halves(dq2) * s).astype(BF16)
                dr_ref[rows, W + pr * 128:W + (pr + 1) * 128] = \
                    ((dk2 * c - _swap_halves(dk2) * s) * K_SCALE).astype(BF16)
            return carry

        lax.fori_loop(0, RET_CPB, chunk, 0)

        @pl.when(pl.program_id(0) == N // RET_CPB - 1)
        def _():
            lane = lax.broadcasted_iota(jnp.int32, (1, 128), 1)
            acc = [jnp.zeros((1, 128), F32), jnp.zeros((1, 128), F32)]
            for h in range(H):
                for b in range(2):
                    acc[b] += jnp.where(lane == h, _sum_all(dl_acc[h, b]), 0.0)
            st_ref[...] = jnp.zeros_like(st_ref)
            st_ref[0:1, :] = acc[0]
            st_ref[1:2, :] = acc[1]

    R = RET_CPB * CHUNK
    st_spec = pl.BlockSpec((RET_CPB, HP, 128, 128), lambda n: (n, 0, 0, 0))
    half = pl.BlockSpec((R, W), lambda n: (n, 0))
    rope = pl.BlockSpec((R, 128), lambda n: (n, 0))
    return _riding_call(
        body, exchange, N // RET_CPB, name="ret_bwd_main",
        args=(tab, qr, kr, p, do, sf_prev, sb_prev, dsf, dsb, c2, s2),
        in_specs=[_tab_spec(H), half, half,
                  pl.BlockSpec((R, D), lambda n: (n, 5)),
                  pl.BlockSpec((R, D), lambda n: (n, 0)),
                  st_spec, st_spec, st_spec, st_spec, rope, rope],
        out_specs=[pl.BlockSpec((R, 2 * D), lambda n: (n, 0)),
                   pl.BlockSpec((8, 128), lambda n: (0, 0))],
        out_shape=[jax.ShapeDtypeStruct((L, 2 * D), BF16), jax.ShapeDtypeStruct((8, 128), F32)],
        scratch=[pltpu.VMEM((H, 2, CHUNK, 128), F32)],
        cparams=_cparams(("arbitrary",)))


def _ctx_bwd(pc, pqk_c, ds0, lg, D):
    Lc = pc.shape[0]
    H = D // DV
    HP = H // 2
    W = D // 2

    def body(lg_ref, k_ref, v_ref, ds_ref, dr_ref, st_ref):
        dqk_ref = dr_ref.at[:, 0:D]
        dv_ref = dr_ref.at[:, D:2 * D]
        m = lax.broadcasted_iota(jnp.int32, (Lc, 128), 0).astype(F32)
        lane = lax.broadcasted_iota(jnp.int32, (1, 128), 1)
        acc_f = jnp.zeros((1, 128), F32)
        acc_b = jnp.zeros((1, 128), F32)
        dqk_ref[:, 0:W] = jnp.zeros((Lc, W), BF16)
        for pr in range(HP):
            ps = slice(pr * 128, (pr + 1) * 128)
            k2 = k_ref[:, ps].astype(F32) * K_SCALE
            dsfp, dsbp = ds_ref[0, pr].astype(BF16), ds_ref[1, pr].astype(BF16)
            dk2 = jnp.zeros((Lc, 128), F32)
            for e in range(2):
                h = 2 * pr + e
                sl = slice(h * DV, (h + 1) * DV)
                hm = _head_lane_mask(k2.shape, e)
                km = jnp.where(hm, k2, 0.0)
                v = v_ref[:, sl]
                vf = v.astype(F32)
                dec_f = jnp.exp(lg_ref[0, h] * (Lc - 1.0 - m))
                dec_b = jnp.exp(lg_ref[1, h] * m)
                kdf = _dot((km * dec_f).astype(BF16), dsfp)
                kdb = _dot((km * dec_b).astype(BF16), dsbp)
                dv_ref[:, sl] = (kdf + kdb).astype(BF16)
                dk2 += jnp.where(hm, dec_f * _dot_nt(v, dsfp) + dec_b * _dot_nt(v, dsbp), 0.0)
                acc_f += jnp.where(lane == h, _sum_all(kdf * vf * (Lc - 1.0 - m)), 0.0)
                acc_b += jnp.where(lane == h, _sum_all(kdb * vf * m), 0.0)
            dqk_ref[:, W + pr * 128:W + (pr + 1) * 128] = (dk2 * K_SCALE).astype(BF16)
        st_ref[...] = jnp.zeros_like(st_ref)
        st_ref[0:1, :] = acc_f
        st_ref[1:2, :] = acc_b

    return pl.pallas_call(
        body, name="ctx_bwd", grid=(1,),
        in_specs=[_smem_spec(), pl.BlockSpec((Lc, W), lambda i: (0, 1)), pl.BlockSpec((Lc, D), lambda i: (0, 1)),
                  pl.BlockSpec((2, HP, 128, 128), lambda i: (0, 0, 0, 0))],
        out_specs=[pl.BlockSpec((Lc, 2 * D), lambda i: (0, 0)), pl.BlockSpec((8, 128), lambda i: (0, 0))],
        out_shape=[jax.ShapeDtypeStruct((Lc, 2 * D), BF16), jax.ShapeDtypeStruct((8, 128), F32)],
    )(lg, pqk_c, pc, ds0)


class _Exchange(NamedTuple):
    inputs: tuple
    out_shapes: tuple
    n_copies: int
    build: Callable


def _exchange_parts(exchange):
    if exchange is None:
        return [], [], [], [], []
    n = exchange.n_copies
    return (list(exchange.inputs), [ANY] * len(exchange.inputs), list(exchange.out_shapes),
            [ANY] * len(exchange.out_shapes), [pltpu.SemaphoreType.DMA((n,)), pltpu.SemaphoreType.DMA((n,))])


def _riding_call(body, exchange, n_steps, *, args, in_specs, out_specs, out_shape, name, cparams, scratch=()):
    ex_args, ex_in_specs, ex_shapes, ex_out_specs, ex_scratch = _exchange_parts(exchange)
    n_in, n_out, n_sc = len(args), len(out_shape), len(scratch)

    def riding(*refs):
        k = n_in + len(ex_args)
        ins, ex_in = refs[:n_in], refs[n_in:k]
        outs, ex_out = refs[k:k + n_out], refs[k + n_out:k + n_out + len(ex_shapes)]
        k += n_out + len(ex_shapes)
        own_scratch, ex_sems = refs[k:k + n_sc], refs[k + n_sc:]
        step = pl.program_id(0)
        if exchange is not None:
            @pl.when(step == 0)
            def _():
                for rc in exchange.build(ex_in, ex_out, *ex_sems):
                    rc.start()
        body(*ins, *outs, *own_scratch)
        if exchange is not None:
            @pl.when(step == n_steps - 1)
            def _():
                for rc in exchange.build(ex_in, ex_out, *ex_sems):
                    rc.wait()

    return tuple(pl.pallas_call(
        riding, name=name, grid=(n_steps,),
        in_specs=list(in_specs) + ex_in_specs, out_specs=list(out_specs) + ex_out_specs,
        out_shape=list(out_shape) + ex_shapes, scratch_shapes=list(scratch) + ex_scratch,
        compiler_params=cparams,
    )(*args, *ex_args))


def _dxm(groups, col0, w, x, nw, sc, dx1, name, exchange=None):
    L, D = x.shape
    tm = min(256, L)
    nt = L // tm
    ng = len(groups)
    widths = [g.shape[1] for g in groups]
    wtot = sum(widths)
    with_dx = dx1 is not None
    ex_args, ex_in_specs, ex_shapes, ex_out_specs, ex_scratch = _exchange_parts(exchange)
    n_in = ng + 4 + (1 if with_dx else 0)
    n_out = 2 if with_dx else 1

    def body(*refs):
        group_refs = refs[:ng]
        w_hbm, x_ref, nw_ref, sc_ref = refs[ng:ng + 4]
        ex_in = refs[n_in:n_in + len(ex_args)]
        outs = refs[n_in + len(ex_args):]
        if with_dx:
            dx1_ref, gx_ref, st_ref = refs[ng + 4], outs[0], outs[1]
        else:
            st_ref = outs[0]
        ex_out = outs[n_out:n_out + len(ex_shapes)]
        w_vm, sem = outs[n_out + len(ex_shapes):n_out + len(ex_shapes) + 2]
        ex_sems = outs[n_out + len(ex_shapes) + 2:]
        i = pl.program_id(0)

        @pl.when(i == 0)
        def _():
            cp = pltpu.make_async_copy(w_hbm.at[:, col0 * D:col0 * D + wtot], w_vm, sem)
            cp.start()
            if exchange is not None:
                for rc in exchange.build(ex_in, ex_out, *ex_sems):
                    rc.start()
            st_ref[...] = jnp.zeros_like(st_ref)
            cp.wait()

        dxm, off = None, 0
        for g_ref, wd in zip(group_refs, widths):
            part = _dot_nt(g_ref[...], w_vm[:, off:off + wd])
            dxm = part if dxm is None else dxm + part
            off += wd

        xv = x_ref[...]
        r = lax.rsqrt(jnp.mean(xv * xv, axis=-1, keepdims=True) + EPS)
        xh = xv * r
        nwv = nw_ref[...]
        dxn = dxm * (1.0 + sc_ref[...])
        st_ref[0:1, :] += jnp.sum(dxm, axis=0, keepdims=True)
        st_ref[1:2, :] += jnp.sum(dxm * (xh * nwv), axis=0, keepdims=True)
        st_ref[2:3, :] += jnp.sum(dxn * xh, axis=0, keepdims=True)
        if with_dx:
            dxh = dxn * nwv
            gx_ref[...] = dx1_ref[...] + r * (dxh - xh * jnp.mean(dxh * xh, axis=-1, keepdims=True))

        if exchange is not None:
            @pl.when(i == nt - 1)
            def _():
                for rc in exchange.build(ex_in, ex_out, *ex_sems):
                    rc.wait()

    row = pl.BlockSpec((tm, D), lambda i: (i, 0))
    in_specs = [pl.BlockSpec((tm, wd), lambda i: (i, 0)) for wd in widths] + [ANY, row, _vec_spec(D), _vec_spec(D)]
    out_specs = [pl.BlockSpec((8, D), lambda i: (0, 0))]
    out_shape = [jax.ShapeDtypeStruct((8, D), F32)]
    args = list(groups) + [w, x, nw, sc]
    if with_dx:
        in_specs.append(row)
        out_specs.insert(0, row)
        out_shape.insert(0, jax.ShapeDtypeStruct((L, D), F32))
        args.append(dx1)
    res = pl.pallas_call(
        body, name=name, grid=(nt,),
        in_specs=in_specs + ex_in_specs, out_specs=out_specs + ex_out_specs, out_shape=out_shape + ex_shapes,
        scratch_shapes=[pltpu.VMEM((D, wtot), BF16), pltpu.SemaphoreType.DMA] + ex_scratch,
        compiler_params=_cparams(("arbitrary",), VMEM_LIMIT),
    )(*args, *ex_args)
    gx = res[0] if with_dx else None
    return (gx, res[n_out - 1], *res[n_out:])


DW_TN = 512
DW_RING = 4


def _dw_in(xmt, groups, cmt, dr_c, D, pair):
    L = xmt.shape[1]
    Lc = cmt.shape[1]
    Dh = D // 2
    tn = min(DW_TN, D)
    nblk = [g.shape[1] // tn for g in groups]
    starts = [sum(nblk[:g]) for g in range(len(groups))]
    ng = len(groups)
    nj = sum(nblk)
    rows_out = Dh if pair else D

    def body(*refs):
        xt_hbm = refs[0]
        group_refs = refs[1:1 + ng]
        ct_hbm, drc_ref, o_ref = refs[1 + ng:4 + ng]
        rest = refs[4 + ng:]
        if pair:
            ra_hbm, xt_vm, ct_vm, loc, ring, s_send, s_recv = rest
            pos = _position()
            sib = _peer(pos, 1)
        else:
            xt_vm, ct_vm, loc = rest
        j = pl.program_id(0)

        @pl.when(j == 0)
        def _():
            if pair:
                c = pos[2]
                other = pl.ds(pl.multiple_of((1 - c) * Dh, Dh), Dh)
                mine = pl.ds(pl.multiple_of(c * Dh, Dh), Dh)
                cps = [pltpu.make_async_copy(xt_hbm.at[other, :], xt_vm.at[0:Dh, :], loc.at[0]),
                       pltpu.make_async_copy(xt_hbm.at[mine, :], xt_vm.at[Dh:D, :], loc.at[1]),
                       pltpu.make_async_copy(ct_hbm.at[other, :], ct_vm.at[0:Dh, :], loc.at[2]),
                       pltpu.make_async_copy(ct_hbm.at[mine, :], ct_vm.at[Dh:D, :], loc.at[3])]
            else:
                cps = [pltpu.make_async_copy(xt_hbm, xt_vm, loc.at[0]), pltpu.make_async_copy(ct_hbm, ct_vm, loc.at[1])]
            for cp in cps:
                cp.start()
            for cp in cps:
                cp.wait()

        def send(slot):
            cols = pl.ds(pl.multiple_of(j * tn, 128), tn)
            return pltpu.make_async_remote_copy(src_ref=ring.at[slot], dst_ref=ra_hbm.at[:, cols],
                                                send_sem=s_send.at[slot], recv_sem=s_recv,
                                                device_id=sib, device_id_type=MESH)

        for g in range(ng):
            @pl.when((j >= starts[g]) & (j < starts[g] + nblk[g]))
            def _(g=g):
                acc = _dot(xt_vm[...], group_refs[g][...])
                if g == 1:
                    acc += _dot(ct_vm[...], drc_ref[...])
                if not pair:
                    o_ref[...] = acc
                    return
                o_ref[...] = acc[Dh:, :]
                slot = lax.rem(j, DW_RING)

                @pl.when(j >= DW_RING)
                def _():
                    send(slot).wait_send()

                ring[slot] = acc[0:Dh, :]
                send(slot).start()

        if pair:
            @pl.when(j == nj - 1)
            def _():
                pltpu.make_async_remote_copy(src_ref=ra_hbm, dst_ref=ra_hbm, send_sem=s_send.at[0], recv_sem=s_recv,
                                             device_id=sib, device_id_type=MESH).wait_recv()
                for slot in range(DW_RING):
                    send(slot).wait_send()

    def group_spec(g, rows):
        return pl.BlockSpec((rows, tn), lambda j: (0, jnp.clip(j - starts[g], 0, nblk[g] - 1)))

    out_specs = [pl.BlockSpec((rows_out, tn), lambda j: (0, j))]
    out_shape = [jax.ShapeDtypeStruct((rows_out, nj * tn), F32)]
    scratch = [pltpu.VMEM((D, L), BF16), pltpu.VMEM((D, Lc), BF16), pltpu.SemaphoreType.DMA((4,))]
    if pair:
        out_specs.append(ANY)
        out_shape.append(jax.ShapeDtypeStruct((Dh, nj * tn), F32))
        scratch += [pltpu.VMEM((DW_RING, Dh, tn), F32), pltpu.SemaphoreType.DMA((DW_RING,)), pltpu.SemaphoreType.DMA]
    return tuple(pl.pallas_call(
        body, name="dw_in", grid=(nj,),
        in_specs=[ANY] + [group_spec(g, L) for g in range(ng)] + [ANY, group_spec(1, Lc)],
        out_specs=out_specs, out_shape=out_shape, scratch_shapes=scratch,
        compiler_params=_cparams(("arbitrary",), VMEM_LIMIT),
    )(xmt, *groups, cmt, dr_c))


def _local_step(x, ctx, tgt, mod_x, mod_c, norm_w, conv_w8, conv_b, lg, gn_w, fw, project, csidx=None):
    L, D = x.shape
    sh_x, sc_x, g_x = mod_x[0:1], mod_x[1:2], mod_x[2:3]
    sh_c, sc_c = mod_c[0:1], mod_c[1:2]
    c2, s2 = _rope_tables(L)
    tab = _decay_tables(lg, D // DV)

    xm, xmt = _norm_mod(x, norm_w, sc_x, sh_x, "norm_mod_x")
    cm, cmt = _norm_mod(ctx, norm_w, sc_c, sh_c, "norm_mod_ctx")
    reduce = csidx is not None
    p, qr, kr, w_in, w3 = project(xm, c2, s2)
    pc, pqk_c = _in_proj(cm, w_in, "in_proj_ctx", QK_BLOCK, 2)
    s0 = _ctx_states(pc, pqk_c, lg, D)
    sf_prev, sb_prev = _ret_states(kr, p, s0, tab, D)
    o, yb = _ret_out(qr, kr, p, sf_prev, sb_prev, gn_w, tab, D)
    dx1, dya, do, dzb, dgab, dw3, st_mid = _mid(p, yb, o, x, tgt, w3, g_x, fw, conv_w8, conv_b, gn_w, D)
    dw3_5 = dw3.reshape(3, N_SHARD, 2, D // 8, D)
    dconv, st_conv, *ra_3 = _conv_bwd(dya, p, conv_w8, conv_b, D, _pair_exchange_w3(dw3_5) if reduce else None)
    dsf, dsb, ds0 = _ret_bwd_states(qr, do, tab, D)
    cs_3 = _sum_pair_w3(csidx[0:1], dw3_5, ra_3[0]) if reduce else None
    dret, st_lg, *rb_3 = _ret_bwd_main(qr, kr, p, do, sf_prev, sb_prev, dsf, dsb, c2, s2, tab, D,
                                       _chips_exchange_w3(cs_3) if reduce else None)
    g_3 = _sum_chips_w3(csidx, cs_3, rb_3[0]) if reduce else dw3
    dret_c, st_lgc = _ctx_bwd(pc, pqk_c, ds0, lg, D)
    groups = (dconv, dret, dzb, dgab)
    _, st_c = _dxm((dret_c,), 4, w_in, ctx, norm_w, sc_c, None, "dxm_ctx")
    return groups, dret_c, xmt, cmt, dx1, sc_x, w_in, g_3, (st_mid, st_conv, st_lg, st_lgc, st_c)


CHIP_FLIPS = (4, 2, 6)
ANY = pl.BlockSpec(memory_space=pl.ANY)
VMEM_FULL = pl.BlockSpec(memory_space=pltpu.VMEM)


def _position():
    return lax.axis_index("x"), lax.axis_index("y"), lax.axis_index("c")


def _peer(pos, k):
    x, y, c = pos
    return (1 - x if k & 4 else x, 1 - y if k & 2 else y, 1 - c if k & 1 else c)


def _dev_id(pos):
    return 4 * pos[0] + 2 * pos[1] + pos[2]


def _shard_of(pos):
    return 2 * pos[0] + pos[1]


def _remote(src, dst, send_sems, recv_sems, idx, to):
    return pltpu.make_async_remote_copy(src_ref=src, dst_ref=dst, send_sem=send_sems.at[idx],
                                        recv_sem=recv_sems.at[idx], device_id=to, device_id_type=MESH)


def _dot_f32(a, b):
    return jnp.dot(a, b, precision=lax.Precision.HIGHEST, preferred_element_type=F32)


def _silu(x):
    return x * _sigmoid(x)


def _fwd_small(c8, cctx8, ada_w, ada_b, conv_w8):
    D = c8.shape[1]
    Wm = ada_w.shape[1]
    Dq = conv_w8.shape[1]

    def body(c_ref, cc_ref, aw_ref, ab_ref, cw_ref, act_ref, mod_ref, cwf_ref,
             cbuf, pmine, pbuf, wbuf, s_c, r_c, s_p, r_p, s_w, r_w):
        pos = _position()
        me, s = _dev_id(pos), _shard_of(pos)
        cbuf[me] = c_ref[...]
        wbuf[s] = cw_ref[...]
        sends = [_remote(c_ref, cbuf.at[me], s_c, r_c, k - 1, _peer(pos, k)) for k in range(1, 8)]
        sends += [_remote(cw_ref, wbuf.at[s], s_w, r_w, j, _peer(pos, k)) for j, k in enumerate(CHIP_FLIPS)]
        for cp in sends:
            cp.start()
        for k in range(1, 8):
            _remote(c_ref, cbuf.at[_dev_id(_peer(pos, k))], s_c, r_c, k - 1, _peer(pos, k)).wait_recv()
        for d in range(N_DEV):
            act_ref[d:d + 1, :] = _silu(cbuf[d, 0:1, :])
        act_ref[8:9, :] = _silu(cc_ref[0:1, :])
        act_ref[9:16, :] = jnp.zeros((7, D), F32)
        part = _dot_f32(act_ref[...], aw_ref[...])
        pmine[...] = part
        pbuf[s] = part
        psend = [_remote(pmine, pbuf.at[s], s_p, r_p, j, _peer(pos, k)) for j, k in enumerate(CHIP_FLIPS)]
        for cp in psend:
            cp.start()
        for j, k in enumerate(CHIP_FLIPS):
            t = _shard_of(_peer(pos, k))
            _remote(pmine, pbuf.at[t], s_p, r_p, j, _peer(pos, k)).wait_recv()
            _remote(cw_ref, wbuf.at[t], s_w, r_w, j, _peer(pos, k)).wait_recv()
        for t in range(N_SHARD):
            mod_ref[:, t * Wm:(t + 1) * Wm] = pbuf[t] + ab_ref[:, t * Wm:(t + 1) * Wm]
            cwf_ref[:, t * Dq:(t + 1) * Dq] = wbuf[t]
        for cp in sends + psend:
            cp.wait_send()

    return pl.pallas_call(
        body, name="fwd_small",
        in_specs=[VMEM_FULL] * 5, out_specs=[VMEM_FULL] * 3,
        out_shape=[jax.ShapeDtypeStruct((16, D), F32), jax.ShapeDtypeStruct((16, 3 * D), F32),
                   jax.ShapeDtypeStruct((8, D), F32)],
        scratch_shapes=[pltpu.VMEM((N_DEV, 8, D), F32), pltpu.VMEM((16, Wm), F32),
                        pltpu.VMEM((N_SHARD, 16, Wm), F32), pltpu.VMEM((N_SHARD, 8, Dq), F32),
                        pltpu.SemaphoreType.DMA((7,)), pltpu.SemaphoreType.DMA((7,)),
                        pltpu.SemaphoreType.DMA((3,)), pltpu.SemaphoreType.DMA((3,)),
                        pltpu.SemaphoreType.DMA((3,)), pltpu.SemaphoreType.DMA((3,))],
        compiler_params=_cparams(None, VMEM_LIMIT),
    )(c8, cctx8, ada_w, ada_b, conv_w8)


AG_CHUNKS = 3


def _ag_in_proj(xm, w_in_s, w3_s, c2, s2):
    L, D = xm.shape
    Wc = w_in_s.shape[1]
    Wq = Wc // AG_CHUNKS
    Dh = D // 2
    Do = w3_s[0].shape[1]
    TM = min(1024, L // 4)
    NT = L // TM
    RC = min(128, Dh)
    NQ = AG_CHUNKS
    order = [(q, j) for q in range(NQ) for j in (0, 1)] + [(q, 2) for q in range(NQ)]

    def body(xm_ref, wi_hbm, wa_ref, wb_ref, wo_ref, c_ref, s_ref, p_hbm, qr_hbm, kr_hbm, fi_hbm, f3_hbm,
             w_vm, cast_buf, s3, stage, qk_stage, ici_s, ici_r, d2d_s, d2d_r, w3_s_, w3_r_, fw_s, fw_r,
             loc, out_sem, qk_sem):
        pos = _position()
        c = pos[2]
        s = _shard_of(pos)
        sib = _peer(pos, 1)
        mine = pl.ds(pl.multiple_of(c * Dh, Dh), Dh)
        other = pl.ds(pl.multiple_of((1 - c) * Dh, Dh), Dh)

        def cast_half(hf):
            def step(i, carry):
                rows = pl.ds(pl.multiple_of(hf * Dh + i * RC, RC), RC)
                cp = pltpu.make_async_copy(wi_hbm.at[rows, :], cast_buf, loc.at[0])
                cp.start()
                cp.wait()
                for q in range(NQ):
                    w_vm[0, q, rows, :] = cast_buf[:, q * Wq:(q + 1) * Wq].astype(BF16)
                return carry
            lax.fori_loop(0, Dh // RC, step, 0)

        def abs_col(t, q):
            return pl.ds(pl.multiple_of(t * Wc + q * Wq, 128), Wq)

        cast_half(c)
        for a, w_ref in enumerate((wa_ref, wb_ref, wo_ref)):
            s3[a] = w_ref[...].astype(BF16)
        sends = [_remote(w_vm.at[0, q, mine, :], w_vm.at[1 + j, q, mine, :], ici_s, ici_r, q * 3 + j,
                         _peer(pos, CHIP_FLIPS[j])) for q, j in order if j < 2]
        for j, k in enumerate(CHIP_FLIPS):
            sends.append(_remote(s3.at[:, c], f3_hbm.at[:, s, c], w3_s_, w3_r_, j, _peer(pos, k)))
        for cp in sends:
            cp.start()
        cast_half(1 - c)
        local = [pltpu.make_async_copy(s3, f3_hbm.at[:, s], loc.at[1])]
        local += [pltpu.make_async_copy(w_vm.at[0, q], fi_hbm.at[:, abs_col(s, q)], loc.at[2 + q]) for q in range(NQ)]
        for cp in local:
            cp.start()

        def out_copy(slot, rows, cols):
            return pltpu.make_async_copy(stage.at[slot], p_hbm.at[rows, cols], out_sem.at[slot])

        def block(r, q, t, first):
            cols = abs_col(t, q)

            def row_tile(rt, carry):
                rows = pl.ds(pl.multiple_of(rt * TM, TM), TM)
                acc = _dot(xm_ref[rows, :], w_vm[r, q])
                slot = lax.rem(rt, 2)

                @pl.when(rt >= 2 if first else rt >= 0)
                def _():
                    out_copy(slot, rows, cols).wait()

                stage[slot] = acc.astype(BF16)
                out_copy(slot, rows, cols).start()

                def rotary(lo, scale, dst_hbm):
                    c, s = c_ref[rows, :], s_ref[rows, :]
                    for pr in range(Dh // 128):
                        tq = acc[:, lo + pr * 128:lo + (pr + 1) * 128] * scale
                        qk_stage[:, pr * 128:(pr + 1) * 128] = (tq * c + _swap_halves(tq) * s).astype(BF16)
                    cp = pltpu.make_async_copy(qk_stage, dst_hbm.at[rows, :], qk_sem)
                    cp.start()
                    cp.wait()

                if q == NQ - 1:
                    @pl.when(t == 1)
                    def _():
                        rotary(Wq - Dh, 1.0, qr_hbm)
                if q == 0:
                    @pl.when(t == 2)
                    def _():
                        rotary(0, K_SCALE, kr_hbm)
                return carry

            lax.fori_loop(0, NT, row_tile, 0)

        passed = []

        def hand_on(q, j):
            half = w_vm.at[1 + j, q, mine, :]
            if j == 2:
                _remote(half, half, fw_s, fw_r, q, sib).wait_recv()
            else:
                _remote(half, half, ici_s, ici_r, q * 3 + j, sib).wait_recv()
                if j == q % 2:
                    on = _remote(half, w_vm.at[3, q, mine, :], fw_s, fw_r, q, _peer(pos, CHIP_FLIPS[1 - j]))
                    on.start()
                    passed.append(on)
            fwd = _remote(half, half, d2d_s, d2d_r, q * 3 + j, sib)
            fwd.start()
            passed.append(fwd)

        for q in range(NQ):
            if q == NQ - 1:
                hand_on(*order[0])
            block(0, q, s, q == 0)
        for n, (q, j) in enumerate(order):
            r, idx = 1 + j, q * 3 + j
            t = _shard_of(_peer(pos, CHIP_FLIPS[j]))
            if n + 1 < len(order):
                hand_on(*order[n + 1])
            _remote(w_vm.at[r, q, other, :], w_vm.at[r, q, other, :], d2d_s, d2d_r, idx, sib).wait_recv()
            block(r, q, t, False)
            cp = pltpu.make_async_copy(w_vm.at[r, q], fi_hbm.at[:, abs_col(t, q)], loc.at[2 + NQ + idx])
            cp.start()
            local.append(cp)
        for j, k in enumerate(CHIP_FLIPS):
            t = _shard_of(_peer(pos, k))
            _remote(s3.at[:, c], f3_hbm.at[:, t, c], w3_s_, w3_r_, j, sib).wait_recv()
            fwd = _remote(f3_hbm.at[:, t, c], f3_hbm.at[:, t, c], w3_s_, w3_r_, 3 + j, sib)
            fwd.start()
            passed.append(fwd)
        for j, k in enumerate(CHIP_FLIPS):
            t = _shard_of(_peer(pos, k))
            _remote(s3.at[:, c], f3_hbm.at[:, t, 1 - c], w3_s_, w3_r_, 3 + j, sib).wait_recv()
        for cp in sends + passed:
            cp.wait_send()
        for cp in local:
            cp.wait()
        for slot in range(2):
            out_copy(slot, pl.ds(0, TM), abs_col(s, 0)).wait()

    n_loc = 2 + NQ + 3 * NQ
    return pl.pallas_call(
        body, name="ag_in_proj",
        in_specs=[VMEM_FULL, ANY, VMEM_FULL, VMEM_FULL, VMEM_FULL, VMEM_FULL, VMEM_FULL], out_specs=[ANY] * 5,
        out_shape=[jax.ShapeDtypeStruct((L, N_SHARD * Wc), BF16),
                   jax.ShapeDtypeStruct((L, Dh), BF16), jax.ShapeDtypeStruct((L, Dh), BF16),
                   jax.ShapeDtypeStruct((D, N_SHARD * Wc), BF16), jax.ShapeDtypeStruct((3, N_SHARD, 2, Do, D), BF16)],
        scratch_shapes=[pltpu.VMEM((N_SHARD, NQ, D, Wq), BF16), pltpu.VMEM((RC, Wc), F32),
                        pltpu.VMEM((3, 2, Do, D), BF16), pltpu.VMEM((2, TM, Wq), BF16), pltpu.VMEM((TM, Dh), BF16),
                        pltpu.SemaphoreType.DMA((3 * NQ,)), pltpu.SemaphoreType.DMA((3 * NQ,)),
                        pltpu.SemaphoreType.DMA((3 * NQ,)), pltpu.SemaphoreType.DMA((3 * NQ,)),
                        pltpu.SemaphoreType.DMA((6,)), pltpu.SemaphoreType.DMA((6,)),
                        pltpu.SemaphoreType.DMA((NQ,)), pltpu.SemaphoreType.DMA((NQ,)),
                        pltpu.SemaphoreType.DMA((n_loc,)), pltpu.SemaphoreType.DMA((2,)), pltpu.SemaphoreType.DMA],
        compiler_params=_cparams(None, VMEM_LIMIT),
    )(xm, w_in_s, *w3_s, c2, s2)


def _pair_exchange_w3(dw3):
    _, _, _, Do, D = dw3.shape

    def build(ins, outs, send, recv):
        pos = _position()
        return [_remote(ins[0].at[:, :, 1 - pos[2]], outs[0], send, recv, 0, _peer(pos, 1))]

    return _Exchange((dw3,), (jax.ShapeDtypeStruct((3, N_SHARD, Do, D), F32),), 1, build)


def _sum_pair_in(dw_mine, ri):
    Dh, Wf = dw_mine.shape
    Wc = Wf // N_SHARD
    tr = min(256, Dh)

    def body(a_ref, b_ref, o_ref):
        o_ref[...] = (a_ref[...] + b_ref[...]).astype(BF16)

    return pl.pallas_call(
        body, name="sum_pair_in", grid=(Dh // tr, N_SHARD),
        in_specs=[pl.BlockSpec((tr, Wc), lambda i, t: (i, t)), pl.BlockSpec((tr, Wc), lambda i, t: (i, t))],
        out_specs=pl.BlockSpec((None, tr, Wc), lambda i, t: (t, i, 0)),
        out_shape=jax.ShapeDtypeStruct((N_SHARD, Dh, Wc), BF16),
        compiler_params=_cparams(("parallel", "parallel")),
    )(dw_mine, ri)


def _sum_pair_w3(cidx, dw3, r3):
    _, _, _, Do, D = dw3.shape

    def body(c_ref, a_ref, b_ref, o_ref):
        o_ref[...] = (a_ref[...] + b_ref[...]).astype(BF16)

    return pl.pallas_call(
        body, name="sum_pair_w3",
        grid_spec=pltpu.PrefetchScalarGridSpec(
            num_scalar_prefetch=1, grid=(3,),
            in_specs=[pl.BlockSpec((None, N_SHARD, None, Do, D), lambda a, c: (a, 0, c[0], 0, 0)),
                      pl.BlockSpec((None, N_SHARD, Do, D), lambda a, c: (a, 0, 0, 0))],
            out_specs=pl.BlockSpec((None, N_SHARD, Do, D), lambda a, c: (a, 0, 0, 0))),
        out_shape=jax.ShapeDtypeStruct((3, N_SHARD, Do, D), BF16),
        compiler_params=_cparams(("parallel",)),
    )(cidx, dw3, r3)


def _chips_exchange_in(cs_in):
    _, Dh, Wc = cs_in.shape

    def build(ins, outs, send, recv):
        pos = _position()
        return [_remote(ins[0].at[_shard_of(_peer(pos, k))], outs[0].at[j], send, recv, j, _peer(pos, k))
                for j, k in enumerate(CHIP_FLIPS)]

    return _Exchange((cs_in,), (jax.ShapeDtypeStruct((3, Dh, Wc), BF16),), 3, build)


def _chips_exchange_w3(cs_3):
    _, _, Do, D = cs_3.shape

    def build(ins, outs, send, recv):
        pos = _position()
        return [_remote(ins[0].at[:, _shard_of(_peer(pos, k))], outs[0].at[j], send, recv, j, _peer(pos, k))
                for j, k in enumerate(CHIP_FLIPS)]

    return _Exchange((cs_3,), (jax.ShapeDtypeStruct((3, 3, Do, D), BF16),), 3, build)


def _sum_chips_in(csidx, cs_in, rb_in):
    _, Dh, Wc = cs_in.shape
    tr = min(256, Dh)

    def body(s_ref, a_ref, b_ref, o_ref):
        acc = a_ref[...].astype(F32)
        for j in range(3):
            acc = acc + b_ref[j].astype(F32)
        o_ref[...] = acc

    return pl.pallas_call(
        body, name="sum_chips_in",
        grid_spec=pltpu.PrefetchScalarGridSpec(
            num_scalar_prefetch=1, grid=(Dh // tr,),
            in_specs=[pl.BlockSpec((None, tr, Wc), lambda i, s: (s[1], i, 0)),
                      pl.BlockSpec((3, tr, Wc), lambda i, s: (0, i, 0))],
            out_specs=pl.BlockSpec((None, tr, Wc), lambda i, s: (s[0], i, 0))),
        out_shape=jax.ShapeDtypeStruct((2, Dh, Wc), F32),
        compiler_params=_cparams(("parallel",)),
    )(csidx, cs_in, rb_in)


def _sum_chips_w3(csidx, cs_3, rb_3):
    _, _, Do, D = cs_3.shape

    def body(s_ref, a_ref, b_ref, o_ref):
        acc = a_ref[...].astype(F32)
        for j in range(3):
            acc = acc + b_ref[j].astype(F32)
        o_ref[...] = acc

    return pl.pallas_call(
        body, name="sum_chips_w3",
        grid_spec=pltpu.PrefetchScalarGridSpec(
            num_scalar_prefetch=1, grid=(3,),
            in_specs=[pl.BlockSpec((None, None, Do, D), lambda a, s: (a, s[1], 0, 0)),
                      pl.BlockSpec((3, None, Do, D), lambda a, s: (0, a, 0, 0))],
            out_specs=pl.BlockSpec((None, None, Do, D), lambda a, s: (a, s[0], 0, 0))),
        out_shape=jax.ShapeDtypeStruct((3, 2, Do, D), F32),
        compiler_params=_cparams(("parallel",)),
    )(csidx, cs_3, rb_3)


def _adam_math(w, g, m, v):
    m = ADAM_B1 * m + (1.0 - ADAM_B1) * g
    v = ADAM_B2 * v + (1.0 - ADAM_B2) * (g * g)
    m_hat = m / (1.0 - ADAM_B1 ** ADAM_STEP)
    v_hat = v / (1.0 - ADAM_B2 ** ADAM_STEP)
    delta = -ADAM_LR * (m_hat / (jnp.sqrt(v_hat) + ADAM_EPS) + ADAM_WD * w)
    return delta, m, v


def _adamw(w, g, m, v, name):
    R, C = w.shape
    tr = min(128, R)

    def body(w_ref, g_ref, m_ref, v_ref, d_ref, nm_ref, nv_ref):
        d_ref[...], nm_ref[...], nv_ref[...] = _adam_math(w_ref[...], g_ref[...], m_ref[...], v_ref[...])

    blk = pl.BlockSpec((tr, C), lambda i: (i, 0))
    return pl.pallas_call(
        body, name=name, grid=(R // tr,), in_specs=[blk] * 4, out_specs=[blk] * 3,
        out_shape=[jax.ShapeDtypeStruct((R, C), F32)] * 3,
        compiler_params=_cparams(("parallel",), VMEM_LIMIT),
    )(w, g, m, v)


def _adamw3(ws, g3, ms, vs):
    R, C = ws[0].shape

    def body(*refs):
        w_refs, m_refs, v_refs = refs[0:3], refs[3:6], refs[6:9]
        g_ref, outs = refs[9], refs[10:]
        for a in range(3):
            @pl.when(pl.program_id(0) == a)
            def _(a=a):
                res = _adam_math(w_refs[a][...], g_ref[...], m_refs[a][...], v_refs[a][...])
                for q in range(3):
                    outs[3 * a + q][...] = res[q]

    full = pl.BlockSpec((R, C), lambda a: (0, 0))
    res = pl.pallas_call(
        body, name="adamw_w3", grid=(3,),
        in_specs=[full] * 9 + [pl.BlockSpec((None, R, C), lambda a: (a, 0, 0))], out_specs=[full] * 9,
        out_shape=[jax.ShapeDtypeStruct((R, C), F32)] * 9,
        compiler_params=_cparams(("arbitrary",), VMEM_LIMIT),
    )(*ws, *ms, *vs, g3)
    return res[0:3], res[3:6], res[6:9]


SMALL_ROWS = ("c_ctx", "norm_w", "conv_b", "gn_w", "final_norm_w")


def _bwd_small(stats, ada_w, Dq, gh_in, gh_3):
    D = stats[0].shape[1]
    Wm = ada_w.shape[1]

    def body(stx, stm, stc, stv, stl, stlc, aw_ref, gi_in, g3_in, tot_ref, dm_sh, gcw, da_ref, loss_ref, gi_ref, g3_ref,
             vec_ref, vbuf, dm, amine, abuf, s_v, r_v, s_a, r_a, s_g, r_g):
        pos = _position()
        me, s = _dev_id(pos), _shard_of(pos)
        c, sib = pos[2], _peer(pos, 1)
        halves = [_remote(gi_in.at[c], gi_ref.at[c], s_g, r_g, 0, sib),
                  _remote(g3_in.at[:, c], g3_ref.at[:, c], s_g, r_g, 1, sib)]
        for cp in halves:
            cp.start()
        vec_ref[...] = jnp.zeros_like(vec_ref)
        vec_ref[0:2, :] = stx[0:2, :]
        vec_ref[2:3, :] = stm[1:2, :]
        vec_ref[3:5, :] = stc[0:2, :]
        vec_ref[5:6, :] = stx[2:3, :] + stc[2:3, :]
        vec_ref[6:7, :] = stv[3:4, :]
        vec_ref[7:8, :] = stm[3:4, :]
        vec_ref[8:9, :] = stm[0:1, :]
        vec_ref[9:12, :] = stv[0:3, :]
        vec_ref[12:14, 0:128] = stl[0:2, :] + stlc[0:2, :]
        vec_ref[14:15, :] = stm[2:3, :]
        vbuf[me] = vec_ref[...]
        sends = [_remote(vec_ref, vbuf.at[me], s_v, r_v, k - 1, _peer(pos, k)) for k in range(1, 8)]
        for cp in sends:
            cp.start()
        for k in range(1, 8):
            _remote(vec_ref, vbuf.at[_dev_id(_peer(pos, k))], s_v, r_v, k - 1, _peer(pos, k)).wait_recv()
        tot = vbuf[0]
        for d in range(1, N_DEV):
            tot = tot + vbuf[d]
        loss_ref[...] = jnp.zeros((8, 128), F32) + (0.5 / D) * _sum_all(tot[14:15, :])
        dm[...] = jnp.zeros_like(dm)
        for d in range(N_DEV):
            for r in range(3):
                dm[d:d + 1, r * D:(r + 1) * D] = vbuf[d, r:r + 1, :]
        dm[8:9, 0:D] = tot[3:4, :]
        dm[8:9, D:2 * D] = tot[4:5, :]
        for t in range(N_SHARD):
            @pl.when(s == t)
            def _(t=t):
                dm_sh[...] = dm[:, t * Wm:(t + 1) * Wm]
                gcw[...] = tot[9:12, t * Dq:(t + 1) * Dq]
        tot_ref[...] = tot
        part = lax.dot_general(dm_sh[8:16, :], aw_ref[...], (((1,), (1,)), ((), ())),
                               precision=lax.Precision.HIGHEST, preferred_element_type=F32)
        amine[...] = part
        abuf[s] = part
        asend = [_remote(amine, abuf.at[s], s_a, r_a, j, _peer(pos, k)) for j, k in enumerate(CHIP_FLIPS)]
        for cp in asend:
            cp.start()
        for j, k in enumerate(CHIP_FLIPS):
            _remote(amine, abuf.at[_shard_of(_peer(pos, k))], s_a, r_a, j, _peer(pos, k)).wait_recv()
        da = abuf[0]
        for t in range(1, N_SHARD):
            da = da + abuf[t]
        da_ref[...] = da
        _remote(gi_in.at[1 - c], gi_ref.at[1 - c], s_g, r_g, 0, sib).wait_recv()
        _remote(g3_in.at[:, 1 - c], g3_ref.at[:, 1 - c], s_g, r_g, 1, sib).wait_recv()
        for cp in sends + asend + halves:
            cp.wait_send()

    row = lambda *shape: jax.ShapeDtypeStruct(shape, F32)
    return pl.pallas_call(
        body, name="bwd_small",
        in_specs=[VMEM_FULL] * 7 + [ANY, ANY], out_specs=[VMEM_FULL] * 5 + [ANY, ANY],
        input_output_aliases={7: 5, 8: 6},
        out_shape=[row(16, D), row(16, Wm), row(3, Dq), row(8, D), row(8, 128), row(*gh_in.shape), row(*gh_3.shape)],
        scratch_shapes=[pltpu.VMEM((16, D), F32), pltpu.VMEM((N_DEV, 16, D), F32), pltpu.VMEM((16, 3 * D), F32),
                        pltpu.VMEM((8, D), F32), pltpu.VMEM((N_SHARD, 8, D), F32),
                        pltpu.SemaphoreType.DMA((7,)), pltpu.SemaphoreType.DMA((7,)),
                        pltpu.SemaphoreType.DMA((3,)), pltpu.SemaphoreType.DMA((3,)),
                        pltpu.SemaphoreType.DMA((2,)), pltpu.SemaphoreType.DMA((2,))],
        compiler_params=_cparams(None, VMEM_LIMIT),
    )(*stats, ada_w, gh_in, gh_3)


def _small_update(tot, dm_sh, gcw, da, act, p_row, p_ab, p_cw, p_dl):
    D = act.shape[1]
    Wm = dm_sh.shape[1]
    Dq = gcw.shape[1]

    def body(tot_ref, dm_ref, gcw_ref, da_ref, act_ref, prow, pab, pcw, pdl, gaw_ref, *outs):
        o_q = [outs[8 * q:8 * (q + 1)] for q in range(4)]
        tot = tot_ref[...]
        gaw_ref[...] = lax.dot_general(act_ref[...], dm_ref[...], (((0,), (0,)), ((), ())),
                                       precision=lax.Precision.HIGHEST, preferred_element_type=F32)
        cc = prow[0, 0:1, :]
        sg = _sigmoid(cc)
        g_cctx = da_ref[0:1, :] * (sg * (1.0 + cc * (1.0 - sg)))

        def place_all(o, val):
            o[...] = val

        def emit(k, w, g, m, v, place=place_all):
            for q, val in enumerate((g,) + _adam_math(w, g, m, v)):
                place(o_q[q][k], val)

        g_rows = [g_cctx, tot[5:6, :], tot[6:7, :], tot[7:8, :], tot[8:9, :]]
        for k, g in enumerate(g_rows):
            emit(k, prow[0, k:k + 1, :], g, prow[1, k:k + 1, :], prow[2, k:k + 1, :])

        def place_ab(o, val):
            for r in range(3):
                o[0:1, r * D:(r + 1) * D] = val[r:r + 1, :]

        g_ab = jnp.concatenate([tot[0:1, :] + tot[3:4, :], tot[1:2, :] + tot[4:5, :], tot[2:3, :]], axis=0)
        emit(5, pab[0], g_ab, pab[1], pab[2], place_ab)
        emit(6, pcw[0], gcw_ref[...], pcw[1], pcw[2])
        g_dl = jnp.concatenate([tot[12:14, 0:128] * _sigmoid(-pdl[0, 0:2, :]), jnp.zeros((6, 128), F32)], axis=0)
        emit(7, pdl[0], g_dl, pdl[1], pdl[2])

    row = lambda *shape: jax.ShapeDtypeStruct(shape, F32)
    per_q = [row(1, D)] * 5 + [row(1, 3 * D), row(3, Dq), row(8, 128)]
    res = pl.pallas_call(
        body, name="small_update",
        in_specs=[VMEM_FULL] * 9, out_specs=[VMEM_FULL] * 33,
        out_shape=[row(D, Wm)] + per_q * 4,
        compiler_params=_cparams(None, VMEM_LIMIT),
    )(tot, dm_sh, gcw, da, act, p_row, p_ab, p_cw, p_dl)
    return res[0], [res[1 + 8 * q:1 + 8 * (q + 1)] for q in range(4)]


def _pad_rows(a, rows=8):
    return jnp.pad(a, ((0, rows - a.shape[0]), (0, 0)))


def kernel(x, c, ctx, c_ctx, norm_w, ada_w, ada_b, w_in, conv_w, conv_b, decay_logit, gn_w, w_a, w_b, w_out, final_norm_w, loss_target, m_c_ctx, m_norm_w, m_ada_w, m_ada_b, m_w_in, m_conv_w, m_conv_b, m_decay_logit, m_gn_w, m_w_a, m_w_b, m_w_out, m_final_norm_w, v_c_ctx, v_norm_w, v_ada_w, v_ada_b, v_w_in, v_conv_w, v_conv_b, v_decay_logit, v_gn_w, v_w_a, v_w_b, v_w_out, v_final_norm_w):
    L, D = x.shape[1], x.shape[2]
    H = D // DV
    Wc = w_in.shape[2]
    Do = D // 8
    pos = _position()
    me = _dev_id(pos)
    cidx = jnp.reshape(pos[2], (1,)).astype(jnp.int32)
    sidx = jnp.reshape(_shard_of(pos), (1,)).astype(jnp.int32)

    act, mod, conv_w8 = _fwd_small(_pad_rows(c), _pad_rows(c_ctx[None]), ada_w[0], ada_b, _pad_rows(conv_w[0]))
    mod_x = lax.dynamic_slice_in_dim(mod, me, 1, axis=0).reshape(3, D)
    mod_c = mod[8].reshape(3, D)
    lg = jax.nn.log_sigmoid(decay_logit[0])

    w3_s = tuple(w[0].reshape(2, Do, D) for w in (w_a, w_b, w_out))

    def project(xm, c2, s2):
        p, qr, kr, w_in_full, w3_full = _ag_in_proj(xm, w_in[0], w3_s, c2, s2)
        return p, qr, kr, w_in_full, w3_full.reshape(3, D, D)

    csidx = jnp.concatenate([cidx, sidx])
    groups, dret_c, xmt, cmt, dx1, sc_x, w_in_full, gh_3, sts = _local_step(
        x[0], ctx[0], loss_target[0], mod_x, mod_c, norm_w, conv_w8, conv_b, lg, gn_w, final_norm_w[None],
        project, csidx)
    st_mid, st_conv, st_lg, st_lgc, st_c = sts

    dw_mine, ra_in = _dw_in(xmt, groups, cmt, dret_c, D, True)
    cs_in = _sum_pair_in(dw_mine, ra_in)
    grad_x, st_x, rb_in = _dxm(groups, 0, w_in_full, x[0], norm_w, sc_x, dx1, "dxm_x", _chips_exchange_in(cs_in))
    gh_in = _sum_chips_in(csidx, cs_in, rb_in)

    zeros3 = jnp.zeros((3, D), F32)
    p_row = jnp.concatenate(
        [r for t in ((c_ctx[None], norm_w, conv_b, gn_w, final_norm_w[None], zeros3),
                     (m_c_ctx[None], m_norm_w, m_conv_b, m_gn_w, m_final_norm_w[None], zeros3),
                     (v_c_ctx[None], v_norm_w, v_conv_b, v_gn_w, v_final_norm_w[None], zeros3)) for r in t],
        axis=0).reshape(3, 8, D)
    p_ab = jnp.concatenate([ada_b, m_ada_b, v_ada_b], axis=0).reshape(3, 3, D)
    p_cw = jnp.concatenate([conv_w, m_conv_w, v_conv_w], axis=0)
    p_dl = jnp.pad(jnp.concatenate([decay_logit, m_decay_logit, v_decay_logit], axis=0), ((0, 0), (0, 6), (0, 128 - H)))
    tot, dm_sh, gcw, da, loss_t, g_in, g_3 = _bwd_small((st_x, st_mid, st_c, st_conv, st_lg, st_lgc), ada_w[0],
                                                        conv_w.shape[2], gh_in, gh_3)
    g_w_in = g_in.reshape(D, Wc)
    g_3 = g_3.reshape(3, D // 4, D)
    g_ada_w, small = _small_update(tot, dm_sh, gcw, da, act, p_row, p_ab, p_cw, p_dl)

    upd_in = _adamw(w_in[0], g_w_in, m_w_in[0], v_w_in[0], "adamw_w_in")
    upd_ada = _adamw(ada_w[0], g_ada_w, m_ada_w[0], v_ada_w[0], "adamw_ada_w")
    upd_a, upd_b, upd_o = _adamw3((w_a[0], w_b[0], w_out[0]), g_3, (m_w_a[0], m_w_b[0], m_w_out[0]),
                                  (v_w_a[0], v_w_b[0], v_w_out[0]))

    def leaves(q):
        big = lambda g, upd: (g if q == 0 else upd[q - 1])[None]
        r_cctx, r_norm, r_convb, r_gn, r_fnorm, r_ab, r_cw, r_dl = small[q]
        return [r_cctx.reshape(D), r_norm, big(g_ada_w, upd_ada), r_ab, big(g_w_in, upd_in),
                r_cw[None], r_convb, r_dl[0:2, 0:H][None], r_gn,
                big(g_3[0], upd_a), big(g_3[1], upd_b), big(g_3[2], upd_o), r_fnorm.reshape(D)]

    loss = loss_t[0, 0]
    return (loss, grad_x[None], *leaves(0), *leaves(1), *leaves(2), *leaves(3))
```

```python
from typing import Callable, NamedTuple

import jax
import jax.numpy as jnp
from jax import lax
from jax.experimental import pallas as pl
from jax.experimental.pallas import tpu as pltpu

F32 = jnp.float32
BF16 = jnp.bfloat16
MESH = pl.DeviceIdType.MESH

CHUNK = 128
RET_CPB = 4
DV = 128
DK = 64
GRID_W = 64
ROPE_BASE = 10000.0
EPS = 1e-6
K_SCALE = DK ** -0.5
N_SHARD = 4
N_DEV = 8

ADAM_LR = 0.001
ADAM_B1 = 0.9
ADAM_B2 = 0.999
ADAM_EPS = 1e-08
ADAM_WD = 0.01
ADAM_STEP = 10

VMEM_LIMIT = 56 * 1024 * 1024


def _cparams(sem=None, vmem=None):
    kw = {}
    if sem is not None:
        kw["dimension_semantics"] = sem
    if vmem is not None:
        kw["vmem_limit_bytes"] = vmem
    return pltpu.CompilerParams(**kw)


def _dot(a, b):
    return jnp.dot(a, b, preferred_element_type=F32)


def _dot_nt(a, b):
    return lax.dot_general(a, b, (((1,), (1,)), ((), ())), preferred_element_type=F32)


def _dot_tn(a, b):
    return lax.dot_general(a, b, (((0,), (0,)), ((), ())), preferred_element_type=F32)


def _sigmoid(x):
    return 1.0 / (1.0 + jnp.exp(-x))


def _sum_all(x):
    return jnp.sum(jnp.sum(x, axis=1, keepdims=True), axis=0, keepdims=True)


def _swap_halves(t):
    n = t.shape[1]
    lane = lax.broadcasted_iota(jnp.int32, t.shape, 1)
    low = (lane & 32) == 0
    return jnp.where(low, pltpu.roll(t, n - 32, 1), pltpu.roll(t, 32, 1))


def _vec_spec(d):
    return pl.BlockSpec((1, d), lambda *a: (0, 0))


def _norm_mod(x, nw, sc, sh, name):
    L, D = x.shape
    tl = min(256, L)

    def body(x_ref, nw_ref, sc_ref, sh_ref, xm_ref, xmt_ref):
        xv = x_ref[...]
        r = lax.rsqrt(jnp.mean(xv * xv, axis=-1, keepdims=True) + EPS)
        xm = (xv * r * nw_ref[...]) * (1.0 + sc_ref[...]) + sh_ref[...]
        xm_b = xm.astype(BF16)
        xm_ref[...] = xm_b
        xmt_ref[...] = xm_b.T

    return pl.pallas_call(
        body, name=name, grid=(L // tl,),
        in_specs=[pl.BlockSpec((tl, D), lambda i: (i, 0)), _vec_spec(D), _vec_spec(D), _vec_spec(D)],
        out_specs=[pl.BlockSpec((tl, D), lambda i: (i, 0)), pl.BlockSpec((D, tl), lambda i: (0, i))],
        out_shape=[jax.ShapeDtypeStruct((L, D), BF16), jax.ShapeDtypeStruct((D, L), BF16)],
        compiler_params=_cparams(("parallel",)),
    )(x, nw, sc, sh)


QK_BLOCK, V_BLOCK = 4, 5


def _in_proj(xm, w, name, first=0, count=None):
    M, D = xm.shape
    count = w.shape[1] // D if count is None else count
    tm = min(1024, M)

    def body(a_ref, b_ref, o_ref, qk_ref):
        acc = _dot(a_ref[...], b_ref[...])
        o_ref[...] = acc.astype(o_ref.dtype)

        @pl.when(pl.program_id(1) == QK_BLOCK - first)
        def _():
            qk_ref[...] = acc

    return pl.pallas_call(
        body, name=name, grid=(M // tm, count),
        in_specs=[pl.BlockSpec((tm, D), lambda i, j: (i, 0)), pl.BlockSpec((D, D), lambda i, j: (0, first + j))],
        out_specs=[pl.BlockSpec((tm, D), lambda i, j: (i, j)), pl.BlockSpec((tm, D), lambda i, j: (i, 0))],
        out_shape=[jax.ShapeDtypeStruct((M, count * D), BF16), jax.ShapeDtypeStruct((M, D), F32)],
        compiler_params=_cparams(("parallel", "arbitrary")),
    )(xm, w)


def _halo_specs(tl, L, D, col):
    hb = tl // 16
    last = L // 16 - 1
    prev = pl.BlockSpec((16, D), lambda i: (jnp.maximum(i * hb - 1, 0), col))
    nxt = pl.BlockSpec((16, D), lambda i: (jnp.minimum((i + 1) * hb, last), col))
    return prev, nxt


def _shift_rows(u, above, below):
    tl = u.shape[0]
    row = lax.broadcasted_iota(jnp.int32, u.shape, 0)
    dn = jnp.where(row == 0, above, pltpu.roll(u, 1, 0))
    up = jnp.where(row == tl - 1, below, pltpu.roll(u, tl - 1, 0))
    return dn, up


def _rope_tables(L):
    pos = jnp.arange(L)
    row = (pos // GRID_W).astype(F32)
    col = (pos % GRID_W).astype(F32)
    nf = DK // 4
    inv = ROPE_BASE ** (-jnp.arange(nf, dtype=F32) / nf)
    ang = jnp.concatenate([row[:, None] * inv, col[:, None] * inv], axis=-1)
    cos, sin = jnp.cos(ang), jnp.sin(ang)
    return jnp.concatenate([cos, cos, cos, cos], axis=-1), jnp.concatenate([-sin, sin, -sin, sin], axis=-1)


def _smem_spec():
    return pl.BlockSpec(memory_space=pltpu.SMEM)


def _pair_select(e0, e1):
    row = lax.broadcasted_iota(jnp.int32, e0.shape, 0)
    return jnp.where(row < DK, e0, e1)


def _head_lane_mask(shape, e):
    lane = lax.broadcasted_iota(jnp.int32, shape, 1)
    return (lane < DK) if e == 0 else (lane >= DK)


def _ctx_states(pc, pqk_c, lg, D):
    Lc = pc.shape[0]
    H = D // DV

    def body(lg_ref, k_ref, v_ref, s_ref):
        m = lax.broadcasted_iota(jnp.int32, (Lc, DV), 0).astype(F32)
        for pr in range(H // 2):
            k2 = k_ref[:, pr * 128:(pr + 1) * 128].astype(F32) * K_SCALE
            res = [[None, None], [None, None]]
            for e in range(2):
                h = 2 * pr + e
                v = v_ref[:, h * DV:(h + 1) * DV]
                dec_f = jnp.exp(lg_ref[0, h] * (Lc - 1.0 - m))
                dec_b = jnp.exp(lg_ref[1, h] * m)
                res[0][e] = _dot_tn((k2 * dec_f).astype(BF16), v)
                res[1][e] = _dot_tn((k2 * dec_b).astype(BF16), v)
            s_ref[0, pr] = _pair_select(res[0][0], res[0][1])
            s_ref[1, pr] = _pair_select(res[1][0], res[1][1])

    return pl.pallas_call(
        body, name="ctx_states", grid=(1,),
        in_specs=[_smem_spec(), pl.BlockSpec((Lc, D // 2), lambda i: (0, 1)), pl.BlockSpec((Lc, D), lambda i: (0, 1))],
        out_specs=pl.BlockSpec((2, H // 2, 128, 128), lambda i: (0, 0, 0, 0)),
        out_shape=jax.ShapeDtypeStruct((2, H // 2, 128, 128), F32),
    )(lg, pqk_c, pc)


T_M, T_MT = 0, 1
T_MF1, T_MB1 = 2, 3
T_QF, T_QB = 4, 5
T_KF, T_KB = 6, 7


def _decay_tables(lg, H):
    def body(lg_ref, t_ref):
        h = pl.program_id(0)
        lgf, lgb = lg_ref[0, h], lg_ref[1, h]
        i = lax.broadcasted_iota(jnp.int32, (CHUNK, CHUNK), 0).astype(F32)
        j = lax.broadcasted_iota(jnp.int32, (CHUNK, CHUNK), 1).astype(F32)
        d = i - j
        mf = jnp.where(d > 0, jnp.exp(lgf * jnp.maximum(d, 0.0)), 0.0)
        mb = jnp.where(d < 0, jnp.exp(lgb * jnp.maximum(-d, 0.0)), 0.0)
        mf_t = jnp.where(d < 0, jnp.exp(lgf * jnp.maximum(-d, 0.0)), 0.0)
        mb_t = jnp.where(d > 0, jnp.exp(lgb * jnp.maximum(d, 0.0)), 0.0)
        diag = jnp.where(d == 0, 2.0, 0.0)
        t_ref[0, T_M] = mf + mb + diag
        t_ref[0, T_MT] = mf_t + mb_t + diag
        t_ref[0, T_MF1] = mf * d
        t_ref[0, T_MB1] = mb * (-d)
        t_ref[0, T_QF] = jnp.exp(lgf * (i + 1.0))
        t_ref[0, T_QB] = jnp.exp(lgb * (CHUNK - i))
        t_ref[0, T_KF] = jnp.exp(lgf * (CHUNK - 1.0 - i))
        t_ref[0, T_KB] = jnp.exp(lgb * i)

    return pl.pallas_call(
        body, name="decay_tables", grid=(H,), in_specs=[_smem_spec()],
        out_specs=pl.BlockSpec((1, 8, CHUNK, CHUNK), lambda h: (h, 0, 0, 0)),
        out_shape=jax.ShapeDtypeStruct((H, 8, CHUNK, CHUNK), F32),
    )(lg)


def _tab_spec(H):
    return pl.BlockSpec((H, 8, CHUNK, CHUNK), lambda n: (0, 0, 0, 0))


def _chunk_decay(tab_ref, h):
    return tab_ref[h, T_QF, CHUNK - 1:CHUNK, :], tab_ref[h, T_QB, 0:1, :]


def _ret_states(kr, p, s0, tab, D):
    L = kr.shape[0]
    H = D // DV
    N = L // CHUNK
    HP = H // 2

    def body(tab_ref, kf_ref, kb_ref, vf_ref, vb_ref, s0_ref, sf_out, sb_out, sf, sb):
        n = pl.program_id(0)

        @pl.when(n == 0)
        def _():
            sf[...] = s0_ref[0]
            sb[...] = s0_ref[1]

        for cc in range(RET_CPB):
            cf_, cb_ = cc, RET_CPB - 1 - cc
            rf, rb = slice(cf_ * CHUNK, (cf_ + 1) * CHUNK), slice(cb_ * CHUNK, (cb_ + 1) * CHUNK)
            sf_out[cf_] = sf[...]
            sb_out[cb_] = sb[...]
            for pr in range(HP):
                kf2 = kf_ref[rf, pr * 128:(pr + 1) * 128].astype(F32)
                kb2 = kb_ref[rb, pr * 128:(pr + 1) * 128].astype(F32)
                inc_f, inc_b, gf, gb = [], [], [], []
                for e in range(2):
                    h = 2 * pr + e
                    inc_f.append(_dot_tn((kf2 * tab_ref[h, T_KF]).astype(BF16), vf_ref[rf, h * DV:(h + 1) * DV]))
                    inc_b.append(_dot_tn((kb2 * tab_ref[h, T_KB]).astype(BF16), vb_ref[rb, h * DV:(h + 1) * DV]))
                    cf, cb = _chunk_decay(tab_ref, h)
                    gf.append(jnp.broadcast_to(cf, (128, 128)))
                    gb.append(jnp.broadcast_to(cb, (128, 128)))
                sf[pr] = _pair_select(gf[0], gf[1]) * sf[pr] + _pair_select(inc_f[0], inc_f[1])
                sb[pr] = _pair_select(gb[0], gb[1]) * sb[pr] + _pair_select(inc_b[0], inc_b[1])

    st = jax.ShapeDtypeStruct((N, HP, 128, 128), F32)
    R = RET_CPB * CHUNK
    NB = N // RET_CPB
    return _riding_call(
        body, None, NB, name="ret_states", args=(tab, kr, kr, p, p, s0),
        in_specs=[_tab_spec(H),
                  pl.BlockSpec((R, D // 2), lambda n: (n, 0)),
                  pl.BlockSpec((R, D // 2), lambda n: (NB - 1 - n, 0)),
                  pl.BlockSpec((R, D), lambda n: (n, 5)),
                  pl.BlockSpec((R, D), lambda n: (NB - 1 - n, 5)),
                  pl.BlockSpec((2, HP, 128, 128), lambda n: (0, 0, 0, 0))],
        out_specs=[pl.BlockSpec((RET_CPB, HP, 128, 128), lambda n: (n, 0, 0, 0)),
                   pl.BlockSpec((RET_CPB, HP, 128, 128), lambda n: (NB - 1 - n, 0, 0, 0))],
        out_shape=[st, st],
        scratch=[pltpu.VMEM((HP, 128, 128), F32), pltpu.VMEM((HP, 128, 128), F32)],
        cparams=_cparams(("arbitrary",)))


def _ret_out(qr, kr, p, sf_prev, sb_prev, gn_w, tab, D):
    L = qr.shape[0]
    H = D // DV
    N = L // CHUNK
    HP = H // 2

    def body(tab_ref, q_ref, k_ref, v_ref, zb_ref, sf_ref, sb_ref, gn_ref, o_ref, yb_ref):
        def chunk(cc, carry):
            rows = pl.ds(pl.multiple_of(cc * CHUNK, CHUNK), CHUNK)
            for pr in range(HP):
                q2 = q_ref[rows, pr * 128:(pr + 1) * 128]
                k2 = k_ref[rows, pr * 128:(pr + 1) * 128]
                sfp = sf_ref[cc, pr].astype(BF16)
                sbp = sb_ref[cc, pr].astype(BF16)
                for e in range(2):
                    h = 2 * pr + e
                    sl = slice(h * DV, (h + 1) * DV)
                    qm = jnp.where(_head_lane_mask(q2.shape, e), q2, jnp.zeros_like(q2))
                    a = (_dot_nt(qm, k2) * tab_ref[h, T_M]).astype(BF16)
                    qf = qm.astype(F32)
                    o = _dot(a, v_ref[rows, sl])
                    o += _dot((qf * tab_ref[h, T_QF]).astype(BF16), sfp)
                    o += _dot((qf * tab_ref[h, T_QB]).astype(BF16), sbp)
                    o_ref[rows, sl] = o
                    mu = jnp.mean(o, axis=-1, keepdims=True)
                    oc = o - mu
                    rstd = lax.rsqrt(jnp.mean(oc * oc, axis=-1, keepdims=True) + EPS)
                    zb = zb_ref[rows, sl].astype(F32)
                    yb_ref[rows, sl] = (zb * _sigmoid(zb) * (oc * rstd * gn_ref[:, sl])).astype(BF16)
            return carry

        lax.fori_loop(0, RET_CPB, chunk, 0)

    R = RET_CPB * CHUNK
    return _riding_call(
        body, None, N // RET_CPB, name="ret_out", args=(tab, qr, kr, p, p, sf_prev, sb_prev, gn_w),
        in_specs=[_tab_spec(H),
                  pl.BlockSpec((R, D // 2), lambda n: (n, 0)),
                  pl.BlockSpec((R, D // 2), lambda n: (n, 0)),
                  pl.BlockSpec((R, D), lambda n: (n, 5)),
                  pl.BlockSpec((R, D), lambda n: (n, 6)),
                  pl.BlockSpec((RET_CPB, HP, 128, 128), lambda n: (n, 0, 0, 0)),
                  pl.BlockSpec((RET_CPB, HP, 128, 128), lambda n: (n, 0, 0, 0)),
                  _vec_spec(D)],
        out_specs=[pl.BlockSpec((R, D), lambda n: (n, 0)), pl.BlockSpec((R, D), lambda n: (n, 0))],
        out_shape=[jax.ShapeDtypeStruct((L, D), F32), jax.ShapeDtypeStruct((L, D), BF16)],
        cparams=_cparams(("arbitrary",)))


def _mid(p, yb, o, x, tgt, w3, g, fw, conv_w, conv_b, gn_w, D):
    L = x.shape[0]
    H = D // DV
    tm = min(256, L)
    nt = L // tm

    def body(h_ref, bg_ref, cg_ref, za_ref, hp_ref, hn_ref, cp_ref, cn_ref, yb_ref, ga_ref, gb_ref, zb_ref, o_ref,
             x_ref, t_ref, w_hbm, g_ref, fw_ref, cw_ref, cb_ref, gn_ref,
             dx1_ref, dya_ref, do_ref, dzb_ref, dgab_ref, dw_hbm, st_ref, w_vm, dw_acc, sem):
        i = pl.program_id(0)

        @pl.when(i == 0)
        def _():
            cp = pltpu.make_async_copy(w_hbm, w_vm, sem)
            cp.start()
            dw_acc[...] = jnp.zeros_like(dw_acc)
            st_ref[...] = jnp.zeros_like(st_ref)
            cp.wait()

        u = cg_ref[...].astype(F32) * h_ref[...].astype(F32)
        above = jnp.where(i == 0, 0.0, cp_ref[15:16, :].astype(F32) * hp_ref[15:16, :].astype(F32))
        below = jnp.where(i == nt - 1, 0.0, cn_ref[0:1, :].astype(F32) * hn_ref[0:1, :].astype(F32))
        dn, up = _shift_rows(u, above, below)
        co = cw_ref[0:1, :] * dn + cw_ref[1:2, :] * u + cw_ref[2:3, :] * up + cb_ref[...]
        za = za_ref[...].astype(F32)
        ya_b = (za * _sigmoid(za) * bg_ref[...].astype(F32) * co).astype(BF16)
        yb_b = yb_ref[...]
        y_a = _dot(ya_b, w_vm[0])
        y_b = _dot(yb_b, w_vm[1])
        sga = _sigmoid(ga_ref[...].astype(F32))
        sgb = _sigmoid(gb_ref[...].astype(F32))
        mix_b = (sga * y_a + sgb * y_b).astype(BF16)
        y_x = _dot(mix_b, w_vm[2])
        gvec, fwv = g_ref[...], fw_ref[...]
        x1 = x_ref[...] + gvec * y_x
        r1 = lax.rsqrt(jnp.mean(x1 * x1, axis=-1, keepdims=True) + EPS)
        xh = x1 * r1
        diff = xh * fwv - t_ref[...]
        dout = diff * (1.0 / D)
        dxh = dout * fwv
        dx1 = r1 * (dxh - xh * jnp.mean(dxh * xh, axis=-1, keepdims=True))
        dx1_ref[...] = dx1
        st_ref[0:1, :] += jnp.sum(dout * xh, axis=0, keepdims=True)
        st_ref[1:2, :] += jnp.sum(dx1 * y_x, axis=0, keepdims=True)
        st_ref[2:3, :] += jnp.sum(diff * diff, axis=0, keepdims=True)
        dyx_b = (dx1 * gvec).astype(BF16)
        dmix = _dot_nt(dyx_b, w_vm[2])
        dw_acc[2] += _dot_tn(mix_b, dyx_b)
        dya_b = (dmix * sga).astype(BF16)
        dyb_b = (dmix * sgb).astype(BF16)
        dgab_ref[:, 0:D] = (dmix * y_a * sga * (1.0 - sga)).astype(BF16)
        dgab_ref[:, D:2 * D] = (dmix * y_b * sgb * (1.0 - sgb)).astype(BF16)
        dya_ref[...] = _dot_nt(dya_b, w_vm[0])
        dyb = _dot_nt(dyb_b, w_vm[1])
        dw_acc[0] += _dot_tn(ya_b, dya_b)
        dw_acc[1] += _dot_tn(yb_b, dyb_b)

        for h in range(H):
            sl = slice(h * DV, (h + 1) * DV)
            ov = o_ref[:, sl]
            oc = ov - jnp.mean(ov, axis=-1, keepdims=True)
            rstd = lax.rsqrt(jnp.mean(oc * oc, axis=-1, keepdims=True) + EPS)
            rn = oc * rstd
            gw = gn_ref[:, sl]
            zb = zb_ref[:, sl].astype(F32)
            sz = _sigmoid(zb)
            dy = dyb[:, sl]
            dzb_ref[:, sl] = (dy * (rn * gw) * (sz * (1.0 + zb * (1.0 - sz)))).astype(BF16)
            dretn = dy * (zb * sz)
            st_ref[3:4, sl] += jnp.sum(dretn * rn, axis=0, keepdims=True)
            drn = dretn * gw
            do_ref[:, sl] = (rstd * (drn - jnp.mean(drn, axis=-1, keepdims=True)
                                     - rn * jnp.mean(drn * rn, axis=-1, keepdims=True))).astype(BF16)

        @pl.when(i == nt - 1)
        def _():
            out = pltpu.make_async_copy(dw_acc, dw_hbm, sem)
            out.start()
            out.wait()

    row = lambda col: pl.BlockSpec((tm, D), lambda i: (i, col))
    any_spec = pl.BlockSpec(memory_space=pl.ANY)
    f32o = jax.ShapeDtypeStruct((L, D), F32)
    bf16o = jax.ShapeDtypeStruct((L, D), BF16)
    hp, hn = _halo_specs(tm, L, D, 0)
    cp, cn = _halo_specs(tm, L, D, 2)
    return pl.pallas_call(
        body, name="mid", grid=(nt,),
        in_specs=[row(0), row(1), row(2), row(3), hp, hn, cp, cn, row(0), row(7), row(8), row(6), row(0),
                  row(0), row(0), any_spec, _vec_spec(D), _vec_spec(D),
                  pl.BlockSpec((8, D), lambda i: (0, 0)), _vec_spec(D), _vec_spec(D)],
        out_specs=[row(0), row(0), row(0), row(0), pl.BlockSpec((tm, 2 * D), lambda i: (i, 0)), any_spec,
                   pl.BlockSpec((8, D), lambda i: (0, 0))],
        out_shape=[f32o, f32o, bf16o, bf16o, jax.ShapeDtypeStruct((L, 2 * D), BF16),
                   jax.ShapeDtypeStruct((3, D, D), F32), jax.ShapeDtypeStruct((8, D), F32)],
        scratch_shapes=[pltpu.VMEM((3, D, D), BF16), pltpu.VMEM((3, D, D), F32), pltpu.SemaphoreType.DMA],
        compiler_params=_cparams(("arbitrary",), VMEM_LIMIT),
    )(p, p, p, p, p, p, p, p, yb, p, p, p, o, x, tgt, w3, g, fw, conv_w, conv_b, gn_w)


def _conv_bwd(dya, p, conv_w, conv_b, D, exchange=None):
    L = p.shape[0]
    tl = min(256, L)
    nt = L // tl

    def body(d_ref, h_ref, bg_ref, cg_ref, za_ref,
             dp_ref, dn_ref, hp_ref, hn_ref, bp_ref, bn_ref, cp_ref, cn_ref, zp_ref, zn_ref,
             w_ref, b_ref, dc_ref, st_ref):
        i = pl.program_id(0)

        @pl.when(i == 0)
        def _():
            st_ref[...] = jnp.zeros_like(st_ref)

        first, last = i == 0, i == nt - 1
        h = h_ref[...].astype(F32)
        cg = cg_ref[...].astype(F32)
        bg = bg_ref[...].astype(F32)
        za = za_ref[...].astype(F32)
        dy = d_ref[...].astype(F32)
        u = cg * h
        u_above = jnp.where(first, 0.0, cp_ref[15:16, :].astype(F32) * hp_ref[15:16, :].astype(F32))
        u_below = jnp.where(last, 0.0, cn_ref[0:1, :].astype(F32) * hn_ref[0:1, :].astype(F32))
        u_dn, u_up = _shift_rows(u, u_above, u_below)
        w0, w1, w2 = w_ref[0:1, :], w_ref[1:2, :], w_ref[2:3, :]
        co = w0 * u_dn + w1 * u + w2 * u_up + b_ref[...]
        sz = _sigmoid(za)
        silu = za * sz
        dc_ref[:, 3 * D:4 * D] = (dy * bg * co * (sz * (1.0 + za * (1.0 - sz)))).astype(BF16)
        dc_ref[:, D:2 * D] = (dy * silu * co).astype(BF16)
        dco = dy * silu * bg

        def edge(dr, zr, br, r):
            z = zr[r:r + 1, :].astype(F32)
            return dr[r:r + 1, :].astype(F32) * (z * _sigmoid(z)) * br[r:r + 1, :].astype(F32)

        dco_above = jnp.where(first, 0.0, edge(dp_ref, zp_ref, bp_ref, 15))
        dco_below = jnp.where(last, 0.0, edge(dn_ref, zn_ref, bn_ref, 0))
        dco_dn, dco_up = _shift_rows(dco, dco_above, dco_below)
        du = w0 * dco_up + w1 * dco + w2 * dco_dn
        dc_ref[:, 2 * D:3 * D] = (du * h).astype(BF16)
        dc_ref[:, 0:D] = (du * cg).astype(BF16)
        st_ref[0:1, :] += jnp.sum(dco * u_dn, axis=0, keepdims=True)
        st_ref[1:2, :] += jnp.sum(dco * u, axis=0, keepdims=True)
        st_ref[2:3, :] += jnp.sum(dco * u_up, axis=0, keepdims=True)
        st_ref[3:4, :] += jnp.sum(dco, axis=0, keepdims=True)

    main = lambda col: pl.BlockSpec((tl, D), lambda i: (i, col))
    halos = []
    for col in (0, 0, 1, 2, 3):
        halos.extend(_halo_specs(tl, L, D, col))
    return _riding_call(
        body, exchange, nt, name="conv_bwd",
        args=(dya, p, p, p, p, dya, dya, p, p, p, p, p, p, p, p, conv_w, conv_b),
        in_specs=[main(0), main(0), main(1), main(2), main(3)] + halos
                 + [pl.BlockSpec((8, D), lambda i: (0, 0)), _vec_spec(D)],
        out_specs=[pl.BlockSpec((tl, 4 * D), lambda i: (i, 0)), pl.BlockSpec((8, D), lambda i: (0, 0))],
        out_shape=[jax.ShapeDtypeStruct((L, 4 * D), BF16), jax.ShapeDtypeStruct((8, D), F32)],
        cparams=_cparams(("arbitrary",)))


def _ret_bwd_states(qr, do, tab, D):
    L = qr.shape[0]
    H = D // DV
    N = L // CHUNK
    HP = H // 2

    def body(tab_ref, qf_ref, qb_ref, dof_ref, dob_ref, dsf_out, dsb_out, ds0_out, dsf, dsb):
        n = pl.program_id(0)

        @pl.when(n == 0)
        def _():
            dsf[...] = jnp.zeros_like(dsf)
            dsb[...] = jnp.zeros_like(dsb)

        for cc in range(RET_CPB):
            cf_, cb_ = RET_CPB - 1 - cc, cc
            rf, rb = slice(cf_ * CHUNK, (cf_ + 1) * CHUNK), slice(cb_ * CHUNK, (cb_ + 1) * CHUNK)
            dsf_out[cf_] = dsf[...]
            dsb_out[cb_] = dsb[...]
            for pr in range(HP):
                qf2 = qf_ref[rf, pr * 128:(pr + 1) * 128].astype(F32)
                qb2 = qb_ref[rb, pr * 128:(pr + 1) * 128].astype(F32)
                inc_f, inc_b, gf, gb = [], [], [], []
                for e in range(2):
                    h = 2 * pr + e
                    inc_f.append(_dot_tn((qf2 * tab_ref[h, T_QF]).astype(BF16), dof_ref[rf, h * DV:(h + 1) * DV]))
                    inc_b.append(_dot_tn((qb2 * tab_ref[h, T_QB]).astype(BF16), dob_ref[rb, h * DV:(h + 1) * DV]))
                    cf, cb = _chunk_decay(tab_ref, h)
                    gf.append(jnp.broadcast_to(cf, (128, 128)))
                    gb.append(jnp.broadcast_to(cb, (128, 128)))
                dsf[pr] = _pair_select(gf[0], gf[1]) * dsf[pr] + _pair_select(inc_f[0], inc_f[1])
                dsb[pr] = _pair_select(gb[0], gb[1]) * dsb[pr] + _pair_select(inc_b[0], inc_b[1])

        @pl.when(n == NB - 1)
        def _():
            ds0_out[0] = dsf[...]
            ds0_out[1] = dsb[...]

    st = jax.ShapeDtypeStruct((N, HP, 128, 128), F32)
    R = RET_CPB * CHUNK
    NB = N // RET_CPB
    return pl.pallas_call(
        body, name="ret_bwd_states", grid=(NB,),
        in_specs=[_tab_spec(H),
                  pl.BlockSpec((R, D // 2), lambda n: (NB - 1 - n, 0)),
                  pl.BlockSpec((R, D // 2), lambda n: (n, 0)),
                  pl.BlockSpec((R, D), lambda n: (NB - 1 - n, 0)),
                  pl.BlockSpec((R, D), lambda n: (n, 0))],
        out_specs=[pl.BlockSpec((RET_CPB, HP, 128, 128), lambda n: (NB - 1 - n, 0, 0, 0)),
                   pl.BlockSpec((RET_CPB, HP, 128, 128), lambda n: (n, 0, 0, 0)),
                   pl.BlockSpec((2, HP, 128, 128), lambda n: (0, 0, 0, 0))],
        out_shape=[st, st, jax.ShapeDtypeStruct((2, HP, 128, 128), F32)],
        scratch_shapes=[pltpu.VMEM((HP, 128, 128), F32), pltpu.VMEM((HP, 128, 128), F32)],
        compiler_params=_cparams(("arbitrary",)),
    )(tab, qr, qr, do, do)


def _ret_bwd_main(qr, kr, p, do, sf_prev, sb_prev, dsf, dsb, c2, s2, tab, D, exchange=None):
    L = qr.shape[0]
    H = D // DV
    N = L // CHUNK
    HP = H // 2
    W = D // 2

    def body(tab_ref, q_ref, k_ref, v_ref, do_ref, sf_ref, sb_ref, dsf_ref, dsb_ref, c_ref, s_ref,
             dr_ref, st_ref, dl_acc):
        @pl.when(pl.program_id(0) == 0)
        def _():
            dl_acc[...] = jnp.zeros_like(dl_acc)

        i = lax.broadcasted_iota(jnp.int32, (CHUNK, 128), 0).astype(F32)
        rowid = lax.broadcasted_iota(jnp.int32, (128, 128), 0)

        def chunk(cc, carry):
            rows = pl.ds(pl.multiple_of(cc * CHUNK, CHUNK), CHUNK)
            c, s = c_ref[rows, :], s_ref[rows, :]
            for pr in range(HP):
                ps = slice(pr * 128, (pr + 1) * 128)
                q2, k2 = q_ref[rows, ps], k_ref[rows, ps]
                sf32, sb32 = sf_ref[cc, pr], sb_ref[cc, pr]
                dsf32, dsb32 = dsf_ref[cc, pr], dsb_ref[cc, pr]
                sfp, sbp = sf32.astype(BF16), sb32.astype(BF16)
                dsfp, dsbp = dsf32.astype(BF16), dsb32.astype(BF16)
                dq2 = jnp.zeros((CHUNK, 128), F32)
                dk2 = jnp.zeros((CHUNK, 128), F32)
                for e in range(2):
                    h = 2 * pr + e
                    sl = slice(h * DV, (h + 1) * DV)
                    hm = _head_lane_mask(q2.shape, e)
                    qm = jnp.where(hm, q2, jnp.zeros_like(q2))
                    km = jnp.where(hm, k2, jnp.zeros_like(k2))
                    qf, kf = qm.astype(F32), km.astype(F32)
                    v, do = v_ref[rows, sl], do_ref[rows, sl]
                    vf, dof = v.astype(F32), do.astype(F32)
                    m_t = tab_ref[h, T_MT]
                    sc = _dot_nt(qm, k2)
                    dpm = _dot_nt(do, v)
                    dsc = (dpm * tab_ref[h, T_M]).astype(BF16)
                    a_t = (_dot_nt(km, q2) * m_t).astype(BF16)
                    dsc_t = (_dot_nt(v, do) * m_t).astype(BF16)
                    dq_f, dq_b = tab_ref[h, T_QF], tab_ref[h, T_QB]
                    dk_f, dk_b = tab_ref[h, T_KF], tab_ref[h, T_KB]
                    dq = _dot(dsc, km)
                    dq += jnp.where(hm, dq_f * _dot_nt(do, sfp) + dq_b * _dot_nt(do, sbp), 0.0)
                    dk = _dot(dsc_t, qm)
                    dk += jnp.where(hm, dk_f * _dot_nt(v, dsfp) + dk_b * _dot_nt(v, dsbp), 0.0)
                    kdf = _dot((kf * dk_f).astype(BF16), dsfp)
                    kdb = _dot((kf * dk_b).astype(BF16), dsbp)
                    dr_ref[rows, D + h * DV:D + (h + 1) * DV] = (_dot(a_t, do) + kdf + kdb).astype(BF16)
                    dq2 += dq
                    dk2 += dk
                    xf = _dot((qf * dq_f).astype(BF16), sfp)
                    xb = _dot((qf * dq_b).astype(BF16), sbp)
                    pair = (rowid < DK) if e == 0 else (rowid >= DK)
                    gcf, gcb = tab_ref[h, T_QF, CHUNK - 1:CHUNK, 0:1], tab_ref[h, T_QB, 0:1, 0:1]
                    scdp = sc * dpm
                    dl_acc[h, 0] += scdp * tab_ref[h, T_MF1] + xf * dof * (i + 1.0) \
                        + kdf * vf * (CHUNK - 1.0 - i) + (CHUNK * gcf) * jnp.where(pair, dsf32 * sf32, 0.0)
                    dl_acc[h, 1] += scdp * tab_ref[h, T_MB1] + xb * dof * (CHUNK - i) \
                        + kdb * vf * i + (CHUNK * gcb) * jnp.where(pair, dsb32 * sb32, 0.0)
                dr_ref[rows, ps] = (dq2 * c - _swap_halves(dq2) * s).astype(BF16)
                dr_ref[rows, W + pr * 128:W + (pr + 1) * 128] = \
                    ((dk2 * c - _swap_halves(dk2) * s) * K_SCALE).astype(BF16)
            return carry

        lax.fori_loop(0, RET_CPB, chunk, 0)

        @pl.when(pl.program_id(0) == N // RET_CPB - 1)
        def _():
            lane = lax.broadcasted_iota(jnp.int32, (1, 128), 1)
            acc = [jnp.zeros((1, 128), F32), jnp.zeros((1, 128), F32)]
            for h in range(H):
                for b in range(2):
                    acc[b] += jnp.where(lane == h, _sum_all(dl_acc[h, b]), 0.0)
            st_ref[...] = jnp.zeros_like(st_ref)
            st_ref[0:1, :] = acc[0]
            st_ref[1:2, :] = acc[1]

    R = RET_CPB * CHUNK
    st_spec = pl.BlockSpec((RET_CPB, HP, 128, 128), lambda n: (n, 0, 0, 0))
    half = pl.BlockSpec((R, W), lambda n: (n, 0))
    rope = pl.BlockSpec((R, 128), lambda n: (n, 0))
    return _riding_call(
        body, exchange, N // RET_CPB, name="ret_bwd_main",
        args=(tab, qr, kr, p, do, sf_prev, sb_prev, dsf, dsb, c2, s2),
        in_specs=[_tab_spec(H), half, half,
                  pl.BlockSpec((R, D), lambda n: (n, 5)),
                  pl.BlockSpec((R, D), lambda n: (n, 0)),
                  st_spec, st_spec, st_spec, st_spec, rope, rope],
        out_specs=[pl.BlockSpec((R, 2 * D), lambda n: (n, 0)),
                   pl.BlockSpec((8, 128), lambda n: (0, 0))],
        out_shape=[jax.ShapeDtypeStruct((L, 2 * D), BF16), jax.ShapeDtypeStruct((8, 128), F32)],
        scratch=[pltpu.VMEM((H, 2, CHUNK, 128), F32)],
        cparams=_cparams(("arbitrary",)))


def _ctx_bwd(pc, pqk_c, ds0, lg, D):
    Lc = pc.shape[0]
    H = D // DV
    HP = H // 2
    W = D // 2

    def body(lg_ref, k_ref, v_ref, ds_ref, dr_ref, st_ref):
        dqk_ref = dr_ref.at[:, 0:D]
        dv_ref = dr_ref.at[:, D:2 * D]
        m = lax.broadcasted_iota(jnp.int32, (Lc, 128), 0).astype(F32)
        lane = lax.broadcasted_iota(jnp.int32, (1, 128), 1)
        acc_f = jnp.zeros((1, 128), F32)
        acc_b = jnp.zeros((1, 128), F32)
        dqk_ref[:, 0:W] = jnp.zeros((Lc, W), BF16)
        for pr in range(HP):
            ps = slice(pr * 128, (pr + 1) * 128)
            k2 = k_ref[:, ps].astype(F32) * K_SCALE
            dsfp, dsbp = ds_ref[0, pr].astype(BF16), ds_ref[1, pr].astype(BF16)
            dk2 = jnp.zeros((Lc, 128), F32)
            for e in range(2):
                h = 2 * pr + e
                sl = slice(h * DV, (h + 1) * DV)
                hm = _head_lane_mask(k2.shape, e)
                km = jnp.where(hm, k2, 0.0)
                v = v_ref[:, sl]
                vf = v.astype(F32)
                dec_f = jnp.exp(lg_ref[0, h] * (Lc - 1.0 - m))
                dec_b = jnp.exp(lg_ref[1, h] * m)
                kdf = _dot((km * dec_f).astype(BF16), dsfp)
                kdb = _dot((km * dec_b).astype(BF16), dsbp)
                dv_ref[:, sl] = (kdf + kdb).astype(BF16)
                dk2 += jnp.where(hm, dec_f * _dot_nt(v, dsfp) + dec_b * _dot_nt(v, dsbp), 0.0)
                acc_f += jnp.where(lane == h, _sum_all(kdf * vf * (Lc - 1.0 - m)), 0.0)
                acc_b += jnp.where(lane == h, _sum_all(kdb * vf * m), 0.0)
            dqk_ref[:, W + pr * 128:W + (pr + 1) * 128] = (dk2 * K_SCALE).astype(BF16)
        st_ref[...] = jnp.zeros_like(st_ref)
        st_ref[0:1, :] = acc_f
        st_ref[1:2, :] = acc_b

    return pl.pallas_call(
        body, name="ctx_bwd", grid=(1,),
        in_specs=[_smem_spec(), pl.BlockSpec((Lc, W), lambda i: (0, 1)), pl.BlockSpec((Lc, D), lambda i: (0, 1)),
                  pl.BlockSpec((2, HP, 128, 128), lambda i: (0, 0, 0, 0))],
        out_specs=[pl.BlockSpec((Lc, 2 * D), lambda i: (0, 0)), pl.BlockSpec((8, 128), lambda i: (0, 0))],
        out_shape=[jax.ShapeDtypeStruct((Lc, 2 * D), BF16), jax.ShapeDtypeStruct((8, 128), F32)],
    )(lg, pqk_c, pc, ds0)


class _Exchange(NamedTuple):
    inputs: tuple
    out_shapes: tuple
    n_copies: int
    build: Callable


def _exchange_parts(exchange):
    if exchange is None:
        return [], [], [], [], []
    n = exchange.n_copies
    return (list(exchange.inputs), [ANY] * len(exchange.inputs), list(exchange.out_shapes),
            [ANY] * len(exchange.out_shapes), [pltpu.SemaphoreType.DMA((n,)), pltpu.SemaphoreType.DMA((n,))])


def _riding_call(body, exchange, n_steps, *, args, in_specs, out_specs, out_shape, name, cparams, scratch=()):
    ex_args, ex_in_specs, ex_shapes, ex_out_specs, ex_scratch = _exchange_parts(exchange)
    n_in, n_out, n_sc = len(args), len(out_shape), len(scratch)

    def riding(*refs):
        k = n_in + len(ex_args)
        ins, ex_in = refs[:n_in], refs[n_in:k]
        outs, ex_out = refs[k:k + n_out], refs[k + n_out:k + n_out + len(ex_shapes)]
        k += n_out + len(ex_shapes)
        own_scratch, ex_sems = refs[k:k + n_sc], refs[k + n_sc:]
        step = pl.program_id(0)
        if exchange is not None:
            @pl.when(step == 0)
            def _():
                for rc in exchange.build(ex_in, ex_out, *ex_sems):
                    rc.start()
        body(*ins, *outs, *own_scratch)
        if exchange is not None:
            @pl.when(step == n_steps - 1)
            def _():
                for rc in exchange.build(ex_in, ex_out, *ex_sems):
                    rc.wait()

    return tuple(pl.pallas_call(
        riding, name=name, grid=(n_steps,),
        in_specs=list(in_specs) + ex_in_specs, out_specs=list(out_specs) + ex_out_specs,
        out_shape=list(out_shape) + ex_shapes, scratch_shapes=list(scratch) + ex_scratch,
        compiler_params=cparams,
    )(*args, *ex_args))


def _dxm(groups, col0, w, x, nw, sc, dx1, name, exchange=None):
    L, D = x.shape
    tm = min(256, L)
    nt = L // tm
    ng = len(groups)
    widths = [g.shape[1] for g in groups]
    wtot = sum(widths)
    with_dx = dx1 is not None
    ex_args, ex_in_specs, ex_shapes, ex_out_specs, ex_scratch = _exchange_parts(exchange)
    n_in = ng + 4 + (1 if with_dx else 0)
    n_out = 2 if with_dx else 1

    def body(*refs):
        group_refs = refs[:ng]
        w_hbm, x_ref, nw_ref, sc_ref = refs[ng:ng + 4]
        ex_in = refs[n_in:n_in + len(ex_args)]
        outs = refs[n_in + len(ex_args):]
        if with_dx:
            dx1_ref, gx_ref, st_ref = refs[ng + 4], outs[0], outs[1]
        else:
            st_ref = outs[0]
        ex_out = outs[n_out:n_out + len(ex_shapes)]
        w_vm, sem = outs[n_out + len(ex_shapes):n_out + len(ex_shapes) + 2]
        ex_sems = outs[n_out + len(ex_shapes) + 2:]
        i = pl.program_id(0)

        @pl.when(i == 0)
        def _():
            cp = pltpu.make_async_copy(w_hbm.at[:, col0 * D:col0 * D + wtot], w_vm, sem)
            cp.start()
            if exchange is not None:
                for rc in exchange.build(ex_in, ex_out, *ex_sems):
                    rc.start()
            st_ref[...] = jnp.zeros_like(st_ref)
            cp.wait()

        dxm, off = None, 0
        for g_ref, wd in zip(group_refs, widths):
            part = _dot_nt(g_ref[...], w_vm[:, off:off + wd])
            dxm = part if dxm is None else dxm + part
            off += wd

        xv = x_ref[...]
        r = lax.rsqrt(jnp.mean(xv * xv, axis=-1, keepdims=True) + EPS)
        xh = xv * r
        nwv = nw_ref[...]
        dxn = dxm * (1.0 + sc_ref[...])
        st_ref[0:1, :] += jnp.sum(dxm, axis=0, keepdims=True)
        st_ref[1:2, :] += jnp.sum(dxm * (xh * nwv), axis=0, keepdims=True)
        st_ref[2:3, :] += jnp.sum(dxn * xh, axis=0, keepdims=True)
        if with_dx:
            dxh = dxn * nwv
            gx_ref[...] = dx1_ref[...] + r * (dxh - xh * jnp.mean(dxh * xh, axis=-1, keepdims=True))

        if exchange is not None:
            @pl.when(i == nt - 1)
            def _():
                for rc in exchange.build(ex_in, ex_out, *ex_sems):
                    rc.wait()

    row = pl.BlockSpec((tm, D), lambda i: (i, 0))
    in_specs = [pl.BlockSpec((tm, wd), lambda i: (i, 0)) for wd in widths] + [ANY, row, _vec_spec(D), _vec_spec(D)]
    out_specs = [pl.BlockSpec((8, D), lambda i: (0, 0))]
    out_shape = [jax.ShapeDtypeStruct((8, D), F32)]
    args = list(groups) + [w, x, nw, sc]
    if with_dx:
        in_specs.append(row)
        out_specs.insert(0, row)
        out_shape.insert(0, jax.ShapeDtypeStruct((L, D), F32))
        args.append(dx1)
    res = pl.pallas_call(
        body, name=name, grid=(nt,),
        in_specs=in_specs + ex_in_specs, out_specs=out_specs + ex_out_specs, out_shape=out_shape + ex_shapes,
        scratch_shapes=[pltpu.VMEM((D, wtot), BF16), pltpu.SemaphoreType.DMA] + ex_scratch,
        compiler_params=_cparams(("arbitrary",), VMEM_LIMIT),
    )(*args, *ex_args)
    gx = res[0] if with_dx else None
    return (gx, res[n_out - 1], *res[n_out:])


DW_TN = 512
DW_RING = 4


def _dw_in(xmt, groups, cmt, dr_c, D, pair):
    L = xmt.shape[1]
    Lc = cmt.shape[1]
    Dh = D // 2
    tn = min(DW_TN, D)
    nblk = [g.shape[1] // tn for g in groups]
    starts = [sum(nblk[:g]) for g in range(len(groups))]
    ng = len(groups)
    nj = sum(nblk)
    rows_out = Dh if pair else D

    def body(*refs):
        xt_hbm = refs[0]
        group_refs = refs[1:1 + ng]
        ct_hbm, drc_ref, o_ref = refs[1 + ng:4 + ng]
        rest = refs[4 + ng:]
        if pair:
            ra_hbm, xt_vm, ct_vm, loc, ring, s_send, s_recv = rest
            pos = _position()
            sib = _peer(pos, 1)
        else:
            xt_vm, ct_vm, loc = rest
        j = pl.program_id(0)

        @pl.when(j == 0)
        def _():
            if pair:
                c = pos[2]
                other = pl.ds(pl.multiple_of((1 - c) * Dh, Dh), Dh)
                mine = pl.ds(pl.multiple_of(c * Dh, Dh), Dh)
                cps = [pltpu.make_async_copy(xt_hbm.at[other, :], xt_vm.at[0:Dh, :], loc.at[0]),
                       pltpu.make_async_copy(xt_hbm.at[mine, :], xt_vm.at[Dh:D, :], loc.at[1]),
                       pltpu.make_async_copy(ct_hbm.at[other, :], ct_vm.at[0:Dh, :], loc.at[2]),
                       pltpu.make_async_copy(ct_hbm.at[mine, :], ct_vm.at[Dh:D, :], loc.at[3])]
            else:
                cps = [pltpu.make_async_copy(xt_hbm, xt_vm, loc.at[0]), pltpu.make_async_copy(ct_hbm, ct_vm, loc.at[1])]
            for cp in cps:
                cp.start()
            for cp in cps:
                cp.wait()

        def send(slot):
            cols = pl.ds(pl.multiple_of(j * tn, 128), tn)
            return pltpu.make_async_remote_copy(src_ref=ring.at[slot], dst_ref=ra_hbm.at[:, cols],
                                                send_sem=s_send.at[slot], recv_sem=s_recv,
                                                device_id=sib, device_id_type=MESH)

        for g in range(ng):
            @pl.when((j >= starts[g]) & (j < starts[g] + nblk[g]))
            def _(g=g):
                acc = _dot(xt_vm[...], group_refs[g][...])
                if g == 1:
                    acc += _dot(ct_vm[...], drc_ref[...])
                if not pair:
                    o_ref[...] = acc
                    return
                o_ref[...] = acc[Dh:, :]
                slot = lax.rem(j, DW_RING)

                @pl.when(j >= DW_RING)
                def _():
                    send(slot).wait_send()

                ring[slot] = acc[0:Dh, :]
                send(slot).start()

        if pair:
            @pl.when(j == nj - 1)
            def _():
                pltpu.make_async_remote_copy(src_ref=ra_hbm, dst_ref=ra_hbm, send_sem=s_send.at[0], recv_sem=s_recv,
                                             device_id=sib, device_id_type=MESH).wait_recv()
                for slot in range(DW_RING):
                    send(slot).wait_send()

    def group_spec(g, rows):
        return pl.BlockSpec((rows, tn), lambda j: (0, jnp.clip(j - starts[g], 0, nblk[g] - 1)))

    out_specs = [pl.BlockSpec((rows_out, tn), lambda j: (0, j))]
    out_shape = [jax.ShapeDtypeStruct((rows_out, nj * tn), F32)]
    scratch = [pltpu.VMEM((D, L), BF16), pltpu.VMEM((D, Lc), BF16), pltpu.SemaphoreType.DMA((4,))]
    if pair:
        out_specs.append(ANY)
        out_shape.append(jax.ShapeDtypeStruct((Dh, nj * tn), F32))
        scratch += [pltpu.VMEM((DW_RING, Dh, tn), F32), pltpu.SemaphoreType.DMA((DW_RING,)), pltpu.SemaphoreType.DMA]
    return tuple(pl.pallas_call(
        body, name="dw_in", grid=(nj,),
        in_specs=[ANY] + [group_spec(g, L) for g in range(ng)] + [ANY, group_spec(1, Lc)],
        out_specs=out_specs, out_shape=out_shape, scratch_shapes=scratch,
        compiler_params=_cparams(("arbitrary",), VMEM_LIMIT),
    )(xmt, *groups, cmt, dr_c))


def _local_step(x, ctx, tgt, mod_x, mod_c, norm_w, conv_w8, conv_b, lg, gn_w, fw, project, csidx=None):
    L, D = x.shape
    sh_x, sc_x, g_x = mod_x[0:1], mod_x[1:2], mod_x[2:3]
    sh_c, sc_c = mod_c[0:1], mod_c[1:2]
    c2, s2 = _rope_tables(L)
    tab = _decay_tables(lg, D // DV)

    xm, xmt = _norm_mod(x, norm_w, sc_x, sh_x, "norm_mod_x")
    cm, cmt = _norm_mod(ctx, norm_w, sc_c, sh_c, "norm_mod_ctx")
    reduce = csidx is not None
    p, qr, kr, w_in, w3 = project(xm, c2, s2)
    pc, pqk_c = _in_proj(cm, w_in, "in_proj_ctx", QK_BLOCK, 2)
    s0 = _ctx_states(pc, pqk_c, lg, D)
    sf_prev, sb_prev = _ret_states(kr, p, s0, tab, D)
    o, yb = _ret_out(qr, kr, p, sf_prev, sb_prev, gn_w, tab, D)
    dx1, dya, do, dzb, dgab, dw3, st_mid = _mid(p, yb, o, x, tgt, w3, g_x, fw, conv_w8, conv_b, gn_w, D)
    dw3_5 = dw3.reshape(3, N_SHARD, 2, D // 8, D)
    dconv, st_conv, *ra_3 = _conv_bwd(dya, p, conv_w8, conv_b, D, _pair_exchange_w3(dw3_5) if reduce else None)
    dsf, dsb, ds0 = _ret_bwd_states(qr, do, tab, D)
    cs_3 = _sum_pair_w3(csidx[0:1], dw3_5, ra_3[0]) if reduce else None
    dret, st_lg, *rb_3 = _ret_bwd_main(qr, kr, p, do, sf_prev, sb_prev, dsf, dsb, c2, s2, tab, D,
                                       _chips_exchange_w3(cs_3) if reduce else None)
    g_3 = _sum_chips_w3(csidx, cs_3, rb_3[0]) if reduce else dw3
    dret_c, st_lgc = _ctx_bwd(pc, pqk_c, ds0, lg, D)
    groups = (dconv, dret, dzb, dgab)
    _, st_c = _dxm((dret_c,), 4, w_in, ctx, norm_w, sc_c, None, "dxm_ctx")
    return groups, dret_c, xmt, cmt, dx1, sc_x, w_in, g_3, (st_mid, st_conv, st_lg, st_lgc, st_c)


CHIP_FLIPS = (4, 2, 6)
ANY = pl.BlockSpec(memory_space=pl.ANY)
VMEM_FULL = pl.BlockSpec(memory_space=pltpu.VMEM)


def _position():
    return lax.axis_index("x"), lax.axis_index("y"), lax.axis_index("c")


def _peer(pos, k):
    x, y, c = pos
    return (1 - x if k & 4 else x, 1 - y if k & 2 else y, 1 - c if k & 1 else c)


def _dev_id(pos):
    return 4 * pos[0] + 2 * pos[1] + pos[2]


def _shard_of(pos):
    return 2 * pos[0] + pos[1]


def _remote(src, dst, send_sems, recv_sems, idx, to):
    return pltpu.make_async_remote_copy(src_ref=src, dst_ref=dst, send_sem=send_sems.at[idx],
                                        recv_sem=recv_sems.at[idx], device_id=to, device_id_type=MESH)


def _dot_f32(a, b):
    return jnp.dot(a, b, precision=lax.Precision.HIGHEST, preferred_element_type=F32)


def _silu(x):
    return x * _sigmoid(x)


def _fwd_small(c8, cctx8, ada_w, ada_b, conv_w8):
    D = c8.shape[1]
    Wm = ada_w.shape[1]
    Dq = conv_w8.shape[1]

    def body(c_ref, cc_ref, aw_ref, ab_ref, cw_ref, act_ref, mod_ref, cwf_ref,
             cbuf, pmine, pbuf, wbuf, s_c, r_c, s_p, r_p, s_w, r_w):
        pos = _position()
        me, s = _dev_id(pos), _shard_of(pos)
        cbuf[me] = c_ref[...]
        wbuf[s] = cw_ref[...]
        sends = [_remote(c_ref, cbuf.at[me], s_c, r_c, k - 1, _peer(pos, k)) for k in range(1, 8)]
        sends += [_remote(cw_ref, wbuf.at[s], s_w, r_w, j, _peer(pos, k)) for j, k in enumerate(CHIP_FLIPS)]
        for cp in sends:
            cp.start()
        for k in range(1, 8):
            _remote(c_ref, cbuf.at[_dev_id(_peer(pos, k))], s_c, r_c, k - 1, _peer(pos, k)).wait_recv()
        for d in range(N_DEV):
            act_ref[d:d + 1, :] = _silu(cbuf[d, 0:1, :])
        act_ref[8:9, :] = _silu(cc_ref[0:1, :])
        act_ref[9:16, :] = jnp.zeros((7, D), F32)
        part = _dot_f32(act_ref[...], aw_ref[...])
        pmine[...] = part
        pbuf[s] = part
        psend = [_remote(pmine, pbuf.at[s], s_p, r_p, j, _peer(pos, k)) for j, k in enumerate(CHIP_FLIPS)]
        for cp in psend:
            cp.start()
        for j, k in enumerate(CHIP_FLIPS):
            t = _shard_of(_peer(pos, k))
            _remote(pmine, pbuf.at[t], s_p, r_p, j, _peer(pos, k)).wait_recv()
            _remote(cw_ref, wbuf.at[t], s_w, r_w, j, _peer(pos, k)).wait_recv()
        for t in range(N_SHARD):
            mod_ref[:, t * Wm:(t + 1) * Wm] = pbuf[t] + ab_ref[:, t * Wm:(t + 1) * Wm]
            cwf_ref[:, t * Dq:(t + 1) * Dq] = wbuf[t]
        for cp in sends + psend:
            cp.wait_send()

    return pl.pallas_call(
        body, name="fwd_small",
        in_specs=[VMEM_FULL] * 5, out_specs=[VMEM_FULL] * 3,
        out_shape=[jax.ShapeDtypeStruct((16, D), F32), jax.ShapeDtypeStruct((16, 3 * D), F32),
                   jax.ShapeDtypeStruct((8, D), F32)],
        scratch_shapes=[pltpu.VMEM((N_DEV, 8, D), F32), pltpu.VMEM((16, Wm), F32),
                        pltpu.VMEM((N_SHARD, 16, Wm), F32), pltpu.VMEM((N_SHARD, 8, Dq), F32),
                        pltpu.SemaphoreType.DMA((7,)), pltpu.SemaphoreType.DMA((7,)),
                        pltpu.SemaphoreType.DMA((3,)), pltpu.SemaphoreType.DMA((3,)),
                        pltpu.SemaphoreType.DMA((3,)), pltpu.SemaphoreType.DMA((3,))],
        compiler_params=_cparams(None, VMEM_LIMIT),
    )(c8, cctx8, ada_w, ada_b, conv_w8)


AG_CHUNKS = 3


def _ag_in_proj(xm, w_in_s, w3_s, c2, s2):
    L, D = xm.shape
    Wc = w_in_s.shape[1]
    Wq = Wc // AG_CHUNKS
    Dh = D // 2
    Do = w3_s[0].shape[1]
    TM = min(1024, L // 4)
    NT = L // TM
    RC = min(128, Dh)
    NQ = AG_CHUNKS
    order = [(q, j) for q in range(NQ) for j in (0, 1)] + [(q, 2) for q in range(NQ)]

    def body(xm_ref, wi_hbm, wa_ref, wb_ref, wo_ref, c_ref, s_ref, p_hbm, qr_hbm, kr_hbm, fi_hbm, f3_hbm,
             w_vm, cast_buf, s3, stage, qk_stage, ici_s, ici_r, d2d_s, d2d_r, w3_s_, w3_r_, fw_s, fw_r,
             loc, out_sem, qk_sem):
        pos = _position()
        c = pos[2]
        s = _shard_of(pos)
        sib = _peer(pos, 1)
        mine = pl.ds(pl.multiple_of(c * Dh, Dh), Dh)
        other = pl.ds(pl.multiple_of((1 - c) * Dh, Dh), Dh)

        def cast_half(hf):
            def step(i, carry):
                rows = pl.ds(pl.multiple_of(hf * Dh + i * RC, RC), RC)
                cp = pltpu.make_async_copy(wi_hbm.at[rows, :], cast_buf, loc.at[0])
                cp.start()
                cp.wait()
                for q in range(NQ):
                    w_vm[0, q, rows, :] = cast_buf[:, q * Wq:(q + 1) * Wq].astype(BF16)
                return carry
            lax.fori_loop(0, Dh // RC, step, 0)

        def abs_col(t, q):
            return pl.ds(pl.multiple_of(t * Wc + q * Wq, 128), Wq)

        cast_half(c)
        for a, w_ref in enumerate((wa_ref, wb_ref, wo_ref)):
            s3[a] = w_ref[...].astype(BF16)
        sends = [_remote(w_vm.at[0, q, mine, :], w_vm.at[1 + j, q, mine, :], ici_s, ici_r, q * 3 + j,
                         _peer(pos, CHIP_FLIPS[j])) for q, j in order if j < 2]
        for cp in sends:
            cp.start()
        w3_sends = [_remote(s3.at[:, c], f3_hbm.at[:, s, c], w3_s_, w3_r_, j, _peer(pos, k))
                    for j, k in enumerate(CHIP_FLIPS)]
        cast_half(1 - c)
        local = [pltpu.make_async_copy(s3, f3_hbm.at[:, s], loc.at[1])]
        local += [pltpu.make_async_copy(w_vm.at[0, q], fi_hbm.at[:, abs_col(s, q)], loc.at[2 + q]) for q in range(NQ)]
        for cp in local:
            cp.start()

        def out_copy(slot, rows, cols):
            return pltpu.make_async_copy(stage.at[slot], p_hbm.at[rows, cols], out_sem.at[slot])

        def block(r, q, t, first):
            cols = abs_col(t, q)

            def row_tile(rt, carry):
                rows = pl.ds(pl.multiple_of(rt * TM, TM), TM)
                acc = _dot(xm_ref[rows, :], w_vm[r, q])
                slot = lax.rem(rt, 2)

                @pl.when(rt >= 2 if first else rt >= 0)
                def _():
                    out_copy(slot, rows, cols).wait()

                stage[slot] = acc.astype(BF16)
                out_copy(slot, rows, cols).start()

                def rotary(lo, scale, dst_hbm):
                    c, s = c_ref[rows, :], s_ref[rows, :]
                    for pr in range(Dh // 128):
                        tq = acc[:, lo + pr * 128:lo + (pr + 1) * 128] * scale
                        qk_stage[:, pr * 128:(pr + 1) * 128] = (tq * c + _swap_halves(tq) * s).astype(BF16)
                    cp = pltpu.make_async_copy(qk_stage, dst_hbm.at[rows, :], qk_sem)
                    cp.start()
                    cp.wait()

                if q == NQ - 1:
                    @pl.when(t == 1)
                    def _():
                        rotary(Wq - Dh, 1.0, qr_hbm)
                if q == 0:
                    @pl.when(t == 2)
                    def _():
                        rotary(0, K_SCALE, kr_hbm)
                return carry

            lax.fori_loop(0, NT, row_tile, 0)

        passed = []

        def hand_on(q, j):
            half = w_vm.at[1 + j, q, mine, :]
            if j == 2:
                _remote(half, half, fw_s, fw_r, q, sib).wait_recv()
            else:
                _remote(half, half, ici_s, ici_r, q * 3 + j, sib).wait_recv()
                if j == q % 2:
                    on = _remote(half, w_vm.at[3, q, mine, :], fw_s, fw_r, q, _peer(pos, CHIP_FLIPS[1 - j]))
                    on.start()
                    passed.append(on)
            fwd = _remote(half, half, d2d_s, d2d_r, q * 3 + j, sib)
            fwd.start()
            passed.append(fwd)

        for q in range(NQ):
            if q == NQ - 1:
                hand_on(*order[0])
            block(0, q, s, q == 0)
        for n, (q, j) in enumerate(order):
            r, idx = 1 + j, q * 3 + j
            t = _shard_of(_peer(pos, CHIP_FLIPS[j]))
            if n + 1 < len(order):
                hand_on(*order[n + 1])
            if n + 1 == 2 * NQ - 1:
                for cp in w3_sends:
                    cp.start()
            _remote(w_vm.at[r, q, other, :], w_vm.at[r, q, other, :], d2d_s, d2d_r, idx, sib).wait_recv()
            block(r, q, t, False)
            cp = pltpu.make_async_copy(w_vm.at[r, q], fi_hbm.at[:, abs_col(t, q)], loc.at[2 + NQ + idx])
            cp.start()
            local.append(cp)
        for j, k in enumerate(CHIP_FLIPS):
            t = _shard_of(_peer(pos, k))
            _remote(s3.at[:, c], f3_hbm.at[:, t, c], w3_s_, w3_r_, j, sib).wait_recv()
            fwd = _remote(f3_hbm.at[:, t, c], f3_hbm.at[:, t, c], w3_s_, w3_r_, 3 + j, sib)
            fwd.start()
            passed.append(fwd)
        for j, k in enumerate(CHIP_FLIPS):
            t = _shard_of(_peer(pos, k))
            _remote(s3.at[:, c], f3_hbm.at[:, t, 1 - c], w3_s_, w3_r_, 3 + j, sib).wait_recv()
        for cp in sends + w3_sends + passed:
            cp.wait_send()
        for cp in local:
            cp.wait()
        for slot in range(2):
            out_copy(slot, pl.ds(0, TM), abs_col(s, 0)).wait()

    n_loc = 2 + NQ + 3 * NQ
    return pl.pallas_call(
        body, name="ag_in_proj",
        in_specs=[VMEM_FULL, ANY, VMEM_FULL, VMEM_FULL, VMEM_FULL, VMEM_FULL, VMEM_FULL], out_specs=[ANY] * 5,
        out_shape=[jax.ShapeDtypeStruct((L, N_SHARD * Wc), BF16),
                   jax.ShapeDtypeStruct((L, Dh), BF16), jax.ShapeDtypeStruct((L, Dh), BF16),
                   jax.ShapeDtypeStruct((D, N_SHARD * Wc), BF16), jax.ShapeDtypeStruct((3, N_SHARD, 2, Do, D), BF16)],
        scratch_shapes=[pltpu.VMEM((N_SHARD, NQ, D, Wq), BF16), pltpu.VMEM((RC, Wc), F32),
                        pltpu.VMEM((3, 2, Do, D), BF16), pltpu.VMEM((2, TM, Wq), BF16), pltpu.VMEM((TM, Dh), BF16),
                        pltpu.SemaphoreType.DMA((3 * NQ,)), pltpu.SemaphoreType.DMA((3 * NQ,)),
                        pltpu.SemaphoreType.DMA((3 * NQ,)), pltpu.SemaphoreType.DMA((3 * NQ,)),
                        pltpu.SemaphoreType.DMA((6,)), pltpu.SemaphoreType.DMA((6,)),
                        pltpu.SemaphoreType.DMA((NQ,)), pltpu.SemaphoreType.DMA((NQ,)),
                        pltpu.SemaphoreType.DMA((n_loc,)), pltpu.SemaphoreType.DMA((2,)), pltpu.SemaphoreType.DMA],
        compiler_params=_cparams(None, VMEM_LIMIT),
    )(xm, w_in_s, *w3_s, c2, s2)


def _pair_exchange_w3(dw3):
    _, _, _, Do, D = dw3.shape

    def build(ins, outs, send, recv):
        pos = _position()
        return [_remote(ins[0].at[:, :, 1 - pos[2]], outs[0], send, recv, 0, _peer(pos, 1))]

    return _Exchange((dw3,), (jax.ShapeDtypeStruct((3, N_SHARD, Do, D), F32),), 1, build)


def _sum_pair_in(dw_mine, ri):
    Dh, Wf = dw_mine.shape
    Wc = Wf // N_SHARD
    tr = min(256, Dh)

    def body(a_ref, b_ref, o_ref):
        o_ref[...] = (a_ref[...] + b_ref[...]).astype(BF16)

    return pl.pallas_call(
        body, name="sum_pair_in", grid=(Dh // tr, N_SHARD),
        in_specs=[pl.BlockSpec((tr, Wc), lambda i, t: (i, t)), pl.BlockSpec((tr, Wc), lambda i, t: (i, t))],
        out_specs=pl.BlockSpec((None, tr, Wc), lambda i, t: (t, i, 0)),
        out_shape=jax.ShapeDtypeStruct((N_SHARD, Dh, Wc), BF16),
        compiler_params=_cparams(("parallel", "parallel")),
    )(dw_mine, ri)


def _sum_pair_w3(cidx, dw3, r3):
    _, _, _, Do, D = dw3.shape

    def body(c_ref, a_ref, b_ref, o_ref):
        o_ref[...] = (a_ref[...] + b_ref[...]).astype(BF16)

    return pl.pallas_call(
        body, name="sum_pair_w3",
        grid_spec=pltpu.PrefetchScalarGridSpec(
            num_scalar_prefetch=1, grid=(3,),
            in_specs=[pl.BlockSpec((None, N_SHARD, None, Do, D), lambda a, c: (a, 0, c[0], 0, 0)),
                      pl.BlockSpec((None, N_SHARD, Do, D), lambda a, c: (a, 0, 0, 0))],
            out_specs=pl.BlockSpec((None, N_SHARD, Do, D), lambda a, c: (a, 0, 0, 0))),
        out_shape=jax.ShapeDtypeStruct((3, N_SHARD, Do, D), BF16),
        compiler_params=_cparams(("parallel",)),
    )(cidx, dw3, r3)


def _chips_exchange_in(cs_in):
    _, Dh, Wc = cs_in.shape

    def build(ins, outs, send, recv):
        pos = _position()
        return [_remote(ins[0].at[_shard_of(_peer(pos, k))], outs[0].at[j], send, recv, j, _peer(pos, k))
                for j, k in enumerate(CHIP_FLIPS)]

    return _Exchange((cs_in,), (jax.ShapeDtypeStruct((3, Dh, Wc), BF16),), 3, build)


def _chips_exchange_w3(cs_3):
    _, _, Do, D = cs_3.shape

    def build(ins, outs, send, recv):
        pos = _position()
        return [_remote(ins[0].at[:, _shard_of(_peer(pos, k))], outs[0].at[j], send, recv, j, _peer(pos, k))
                for j, k in enumerate(CHIP_FLIPS)]

    return _Exchange((cs_3,), (jax.ShapeDtypeStruct((3, 3, Do, D), BF16),), 3, build)


def _sum_chips_in(csidx, cs_in, rb_in):
    _, Dh, Wc = cs_in.shape
    tr = min(256, Dh)

    def body(s_ref, a_ref, b_ref, o_ref):
        acc = a_ref[...].astype(F32)
        for j in range(3):
            acc = acc + b_ref[j].astype(F32)
        o_ref[...] = acc

    return pl.pallas_call(
        body, name="sum_chips_in",
        grid_spec=pltpu.PrefetchScalarGridSpec(
            num_scalar_prefetch=1, grid=(Dh // tr,),
            in_specs=[pl.BlockSpec((None, tr, Wc), lambda i, s: (s[1], i, 0)),
                      pl.BlockSpec((3, tr, Wc), lambda i, s: (0, i, 0))],
            out_specs=pl.BlockSpec((None, tr, Wc), lambda i, s: (s[0], i, 0))),
        out_shape=jax.ShapeDtypeStruct((2, Dh, Wc), F32),
        compiler_params=_cparams(("parallel",)),
    )(csidx, cs_in, rb_in)


def _sum_chips_w3(csidx, cs_3, rb_3):
    _, _, Do, D = cs_3.shape

    def body(s_ref, a_ref, b_ref, o_ref):
        acc = a_ref[...].astype(F32)
        for j in range(3):
            acc = acc + b_ref[j].astype(F32)
        o_ref[...] = acc

    return pl.pallas_call(
        body, name="sum_chips_w3",
        grid_spec=pltpu.PrefetchScalarGridSpec(
            num_scalar_prefetch=1, grid=(3,),
            in_specs=[pl.BlockSpec((None, None, Do, D), lambda a, s: (a, s[1], 0, 0)),
                      pl.BlockSpec((3, None, Do, D), lambda a, s: (0, a, 0, 0))],
            out_specs=pl.BlockSpec((None, None, Do, D), lambda a, s: (a, s[0], 0, 0))),
        out_shape=jax.ShapeDtypeStruct((3, 2, Do, D), F32),
        compiler_params=_cparams(("parallel",)),
    )(csidx, cs_3, rb_3)


def _adam_math(w, g, m, v):
    m = ADAM_B1 * m + (1.0 - ADAM_B1) * g
    v = ADAM_B2 * v + (1.0 - ADAM_B2) * (g * g)
    m_hat = m / (1.0 - ADAM_B1 ** ADAM_STEP)
    v_hat = v / (1.0 - ADAM_B2 ** ADAM_STEP)
    delta = -ADAM_LR * (m_hat / (jnp.sqrt(v_hat) + ADAM_EPS) + ADAM_WD * w)
    return delta, m, v


def _adamw(w, g, m, v, name):
    R, C = w.shape
    tr = min(128, R)

    def body(w_ref, g_ref, m_ref, v_ref, d_ref, nm_ref, nv_ref):
        d_ref[...], nm_ref[...], nv_ref[...] = _adam_math(w_ref[...], g_ref[...], m_ref[...], v_ref[...])

    blk = pl.BlockSpec((tr, C), lambda i: (i, 0))
    return pl.pallas_call(
        body, name=name, grid=(R // tr,), in_specs=[blk] * 4, out_specs=[blk] * 3,
        out_shape=[jax.ShapeDtypeStruct((R, C), F32)] * 3,
        compiler_params=_cparams(("parallel",), VMEM_LIMIT),
    )(w, g, m, v)


def _adamw3(ws, g3, ms, vs):
    R, C = ws[0].shape

    def body(*refs):
        w_refs, m_refs, v_refs = refs[0:3], refs[3:6], refs[6:9]
        g_ref, outs = refs[9], refs[10:]
        for a in range(3):
            @pl.when(pl.program_id(0) == a)
            def _(a=a):
                res = _adam_math(w_refs[a][...], g_ref[...], m_refs[a][...], v_refs[a][...])
                for q in range(3):
                    outs[3 * a + q][...] = res[q]

    full = pl.BlockSpec((R, C), lambda a: (0, 0))
    res = pl.pallas_call(
        body, name="adamw_w3", grid=(3,),
        in_specs=[full] * 9 + [pl.BlockSpec((None, R, C), lambda a: (a, 0, 0))], out_specs=[full] * 9,
        out_shape=[jax.ShapeDtypeStruct((R, C), F32)] * 9,
        compiler_params=_cparams(("arbitrary",), VMEM_LIMIT),
    )(*ws, *ms, *vs, g3)
    return res[0:3], res[3:6], res[6:9]


SMALL_ROWS = ("c_ctx", "norm_w", "conv_b", "gn_w", "final_norm_w")


def _bwd_small(stats, ada_w, Dq, gh_in, gh_3):
    D = stats[0].shape[1]
    Wm = ada_w.shape[1]

    def body(stx, stm, stc, stv, stl, stlc, aw_ref, gi_in, g3_in, tot_ref, dm_sh, gcw, da_ref, loss_ref, gi_ref, g3_ref,
             vec_ref, vbuf, dm, amine, abuf, s_v, r_v, s_a, r_a, s_g, r_g):
        pos = _position()
        me, s = _dev_id(pos), _shard_of(pos)
        c, sib = pos[2], _peer(pos, 1)
        halves = [_remote(gi_in.at[c], gi_ref.at[c], s_g, r_g, 0, sib),
                  _remote(g3_in.at[:, c], g3_ref.at[:, c], s_g, r_g, 1, sib)]
        for cp in halves:
            cp.start()
        vec_ref[...] = jnp.zeros_like(vec_ref)
        vec_ref[0:2, :] = stx[0:2, :]
        vec_ref[2:3, :] = stm[1:2, :]
        vec_ref[3:5, :] = stc[0:2, :]
        vec_ref[5:6, :] = stx[2:3, :] + stc[2:3, :]
        vec_ref[6:7, :] = stv[3:4, :]
        vec_ref[7:8, :] = stm[3:4, :]
        vec_ref[8:9, :] = stm[0:1, :]
        vec_ref[9:12, :] = stv[0:3, :]
        vec_ref[12:14, 0:128] = stl[0:2, :] + stlc[0:2, :]
        vec_ref[14:15, :] = stm[2:3, :]
        vbuf[me] = vec_ref[...]
        sends = [_remote(vec_ref, vbuf.at[me], s_v, r_v, k - 1, _peer(pos, k)) for k in range(1, 8)]
        for cp in sends:
            cp.start()
        for k in range(1, 8):
            _remote(vec_ref, vbuf.at[_dev_id(_peer(pos, k))], s_v, r_v, k - 1, _peer(pos, k)).wait_recv()
        tot = vbuf[0]
        for d in range(1, N_DEV):
            tot = tot + vbuf[d]
        loss_ref[...] = jnp.zeros((8, 128), F32) + (0.5 / D) * _sum_all(tot[14:15, :])
        dm[...] = jnp.zeros_like(dm)
        for d in range(N_DEV):
            for r in range(3):
                dm[d:d + 1, r * D:(r + 1) * D] = vbuf[d, r:r + 1, :]
        dm[8:9, 0:D] = tot[3:4, :]
        dm[8:9, D:2 * D] = tot[4:5, :]
        for t in range(N_SHARD):
            @pl.when(s == t)
            def _(t=t):
                dm_sh[...] = dm[:, t * Wm:(t + 1) * Wm]
                gcw[...] = tot[9:12, t * Dq:(t + 1) * Dq]
        tot_ref[...] = tot
        part = lax.dot_general(dm_sh[8:16, :], aw_ref[...], (((1,), (1,)), ((), ())),
                               precision=lax.Precision.HIGHEST, preferred_element_type=F32)
        amine[...] = part
        abuf[s] = part
        asend = [_remote(amine, abuf.at[s], s_a, r_a, j, _peer(pos, k)) for j, k in enumerate(CHIP_FLIPS)]
        for cp in asend:
            cp.start()
        for j, k in enumerate(CHIP_FLIPS):
            _remote(amine, abuf.at[_shard_of(_peer(pos, k))], s_a, r_a, j, _peer(pos, k)).wait_recv()
        da = abuf[0]
        for t in range(1, N_SHARD):
            da = da + abuf[t]
        da_ref[...] = da
        _remote(gi_in.at[1 - c], gi_ref.at[1 - c], s_g, r_g, 0, sib).wait_recv()
        _remote(g3_in.at[:, 1 - c], g3_ref.at[:, 1 - c], s_g, r_g, 1, sib).wait_recv()
        for cp in sends + asend + halves:
            cp.wait_send()

    row = lambda *shape: jax.ShapeDtypeStruct(shape, F32)
    return pl.pallas_call(
        body, name="bwd_small",
        in_specs=[VMEM_FULL] * 7 + [ANY, ANY], out_specs=[VMEM_FULL] * 5 + [ANY, ANY],
        input_output_aliases={7: 5, 8: 6},
        out_shape=[row(16, D), row(16, Wm), row(3, Dq), row(8, D), row(8, 128), row(*gh_in.shape), row(*gh_3.shape)],
        scratch_shapes=[pltpu.VMEM((16, D), F32), pltpu.VMEM((N_DEV, 16, D), F32), pltpu.VMEM((16, 3 * D), F32),
                        pltpu.VMEM((8, D), F32), pltpu.VMEM((N_SHARD, 8, D), F32),
                        pltpu.SemaphoreType.DMA((7,)), pltpu.SemaphoreType.DMA((7,)),
                        pltpu.SemaphoreType.DMA((3,)), pltpu.SemaphoreType.DMA((3,)),
                        pltpu.SemaphoreType.DMA((2,)), pltpu.SemaphoreType.DMA((2,))],
        compiler_params=_cparams(None, VMEM_LIMIT),
    )(*stats, ada_w, gh_in, gh_3)


def _small_update(tot, dm_sh, gcw, da, act, p_row, p_ab, p_cw, p_dl):
    D = act.shape[1]
    Wm = dm_sh.shape[1]
    Dq = gcw.shape[1]

    def body(tot_ref, dm_ref, gcw_ref, da_ref, act_ref, prow, pab, pcw, pdl, gaw_ref, *outs):
        o_q = [outs[8 * q:8 * (q + 1)] for q in range(4)]
        tot = tot_ref[...]
        gaw_ref[...] = lax.dot_general(act_ref[...], dm_ref[...], (((0,), (0,)), ((), ())),
                                       precision=lax.Precision.HIGHEST, preferred_element_type=F32)
        cc = prow[0, 0:1, :]
        sg = _sigmoid(cc)
        g_cctx = da_ref[0:1, :] * (sg * (1.0 + cc * (1.0 - sg)))

        def place_all(o, val):
            o[...] = val

        def emit(k, w, g, m, v, place=place_all):
            for q, val in enumerate((g,) + _adam_math(w, g, m, v)):
                place(o_q[q][k], val)

        g_rows = [g_cctx, tot[5:6, :], tot[6:7, :], tot[7:8, :], tot[8:9, :]]
        for k, g in enumerate(g_rows):
            emit(k, prow[0, k:k + 1, :], g, prow[1, k:k + 1, :], prow[2, k:k + 1, :])

        def place_ab(o, val):
            for r in range(3):
                o[0:1, r * D:(r + 1) * D] = val[r:r + 1, :]

        g_ab = jnp.concatenate([tot[0:1, :] + tot[3:4, :], tot[1:2, :] + tot[4:5, :], tot[2:3, :]], axis=0)
        emit(5, pab[0], g_ab, pab[1], pab[2], place_ab)
        emit(6, pcw[0], gcw_ref[...], pcw[1], pcw[2])
        g_dl = jnp.concatenate([tot[12:14, 0:128] * _sigmoid(-pdl[0, 0:2, :]), jnp.zeros((6, 128), F32)], axis=0)
        emit(7, pdl[0], g_dl, pdl[1], pdl[2])

    row = lambda *shape: jax.ShapeDtypeStruct(shape, F32)
    per_q = [row(1, D)] * 5 + [row(1, 3 * D), row(3, Dq), row(8, 128)]
    res = pl.pallas_call(
        body, name="small_update",
        in_specs=[VMEM_FULL] * 9, out_specs=[VMEM_FULL] * 33,
        out_shape=[row(D, Wm)] + per_q * 4,
        compiler_params=_cparams(None, VMEM_LIMIT),
    )(tot, dm_sh, gcw, da, act, p_row, p_ab, p_cw, p_dl)
    return res[0], [res[1 + 8 * q:1 + 8 * (q + 1)] for q in range(4)]


def _pad_rows(a, rows=8):
    return jnp.pad(a, ((0, rows - a.shape[0]), (0, 0)))


def kernel(x, c, ctx, c_ctx, norm_w, ada_w, ada_b, w_in, conv_w, conv_b, decay_logit, gn_w, w_a, w_b, w_out, final_norm_w, loss_target, m_c_ctx, m_norm_w, m_ada_w, m_ada_b, m_w_in, m_conv_w, m_conv_b, m_decay_logit, m_gn_w, m_w_a, m_w_b, m_w_out, m_final_norm_w, v_c_ctx, v_norm_w, v_ada_w, v_ada_b, v_w_in, v_conv_w, v_conv_b, v_decay_logit, v_gn_w, v_w_a, v_w_b, v_w_out, v_final_norm_w):
    L, D = x.shape[1], x.shape[2]
    H = D // DV
    Wc = w_in.shape[2]
    Do = D // 8
    pos = _position()
    me = _dev_id(pos)
    cidx = jnp.reshape(pos[2], (1,)).astype(jnp.int32)
    sidx = jnp.reshape(_shard_of(pos), (1,)).astype(jnp.int32)

    act, mod, conv_w8 = _fwd_small(_pad_rows(c), _pad_rows(c_ctx[None]), ada_w[0], ada_b, _pad_rows(conv_w[0]))
    mod_x = lax.dynamic_slice_in_dim(mod, me, 1, axis=0).reshape(3, D)
    mod_c = mod[8].reshape(3, D)
    lg = jax.nn.log_sigmoid(decay_logit[0])

    w3_s = tuple(w[0].reshape(2, Do, D) for w in (w_a, w_b, w_out))

    def project(xm, c2, s2):
        p, qr, kr, w_in_full, w3_full = _ag_in_proj(xm, w_in[0], w3_s, c2, s2)
        return p, qr, kr, w_in_full, w3_full.reshape(3, D, D)

    csidx = jnp.concatenate([cidx, sidx])
    groups, dret_c, xmt, cmt, dx1, sc_x, w_in_full, gh_3, sts = _local_step(
        x[0], ctx[0], loss_target[0], mod_x, mod_c, norm_w, conv_w8, conv_b, lg, gn_w, final_norm_w[None],
        project, csidx)
    st_mid, st_conv, st_lg, st_lgc, st_c = sts

    dw_mine, ra_in = _dw_in(xmt, groups, cmt, dret_c, D, True)
    cs_in = _sum_pair_in(dw_mine, ra_in)
    grad_x, st_x, rb_in = _dxm(groups, 0, w_in_full, x[0], norm_w, sc_x, dx1, "dxm_x", _chips_exchange_in(cs_in))
    gh_in = _sum_chips_in(csidx, cs_in, rb_in)

    zeros3 = jnp.zeros((3, D), F32)
    p_row = jnp.concatenate(
        [r for t in ((c_ctx[None], norm_w, conv_b, gn_w, final_norm_w[None], zeros3),
                     (m_c_ctx[None], m_norm_w, m_conv_b, m_gn_w, m_final_norm_w[None], zeros3),
                     (v_c_ctx[None], v_norm_w, v_conv_b, v_gn_w, v_final_norm_w[None], zeros3)) for r in t],
        axis=0).reshape(3, 8, D)
    p_ab = jnp.concatenate([ada_b, m_ada_b, v_ada_b], axis=0).reshape(3, 3, D)
    p_cw = jnp.concatenate([conv_w, m_conv_w, v_conv_w], axis=0)
    p_dl = jnp.pad(jnp.concatenate([decay_logit, m_decay_logit, v_decay_logit], axis=0), ((0, 0), (0, 6), (0, 128 - H)))
    tot, dm_sh, gcw, da, loss_t, g_in, g_3 = _bwd_small((st_x, st_mid, st_c, st_conv, st_lg, st_lgc), ada_w[0],
                                                        conv_w.shape[2], gh_in, gh_3)
    g_w_in = g_in.reshape(D, Wc)
    g_3 = g_3.reshape(3, D // 4, D)
    g_ada_w, small = _small_update(tot, dm_sh, gcw, da, act, p_row, p_ab, p_cw, p_dl)

    upd_in = _adamw(w_in[0], g_w_in, m_w_in[0], v_w_in[0], "adamw_w_in")
    upd_ada = _adamw(ada_w[0], g_ada_w, m_ada_w[0], v_ada_w[0], "adamw_ada_w")
    upd_a, upd_b, upd_o = _adamw3((w_a[0], w_b[0], w_out[0]), g_3, (m_w_a[0], m_w_b[0], m_w_out[0]),
                                  (v_w_a[0], v_w_b[0], v_w_out[0]))

    def leaves(q):
        big = lambda g, upd: (g if q == 0 else upd[q - 1])[None]
        r_cctx, r_norm, r_convb, r_gn, r_fnorm, r_ab, r_cw, r_dl = small[q]
        return [r_cctx.reshape(D), r_norm, big(g_ada_w, upd_ada), r_ab, big(g_w_in, upd_in),
                r_cw[None], r_convb, r_dl[0:2, 0:H][None], r_gn,
                big(g_3[0], upd_a), big(g_3[1], upd_b), big(g_3[2], upd_o), r_fnorm.reshape(D)]

    loss = loss_t[0, 0]
    return (loss, grad_x[None], *leaves(0), *leaves(1), *leaves(2), *leaves(3))
```

```python
from typing import Callable, NamedTuple

import jax
import jax.numpy as jnp
from jax import lax
from jax.experimental import pallas as pl
from jax.experimental.pallas import tpu as pltpu

F32 = jnp.float32
BF16 = jnp.bfloat16
MESH = pl.DeviceIdType.MESH

CHUNK = 128
RET_CPB = 4
DV = 128
DK = 64
GRID_W = 64
ROPE_BASE = 10000.0
EPS = 1e-6
K_SCALE = DK ** -0.5
N_SHARD = 4
N_DEV = 8

ADAM_LR = 0.001
ADAM_B1 = 0.9
ADAM_B2 = 0.999
ADAM_EPS = 1e-08
ADAM_WD = 0.01
ADAM_STEP = 10

VMEM_LIMIT = 56 * 1024 * 1024


def _cparams(sem=None, vmem=None):
    kw = {}
    if sem is not None:
        kw["dimension_semantics"] = sem
    if vmem is not None:
        kw["vmem_limit_bytes"] = vmem
    return pltpu.CompilerParams(**kw)


def _dot(a, b):
    return jnp.dot(a, b, preferred_element_type=F32)


def _dot_nt(a, b):
    return lax.dot_general(a, b, (((1,), (1,)), ((), ())), preferred_element_type=F32)


def _dot_tn(a, b):
    return lax.dot_general(a, b, (((0,), (0,)), ((), ())), preferred_element_type=F32)


def _sigmoid(x):
    return 1.0 / (1.0 + jnp.exp(-x))


def _sum_all(x):
    return jnp.sum(jnp.sum(x, axis=1, keepdims=True), axis=0, keepdims=True)


def _swap_halves(t):
    n = t.shape[1]
    lane = lax.broadcasted_iota(jnp.int32, t.shape, 1)
    low = (lane & 32) == 0
    return jnp.where(low, pltpu.roll(t, n - 32, 1), pltpu.roll(t, 32, 1))


def _vec_spec(d):
    return pl.BlockSpec((1, d), lambda *a: (0, 0))


def _norm_mod(x, nw, sc, sh, name, also_bf16=None):
    L, D = x.shape
    tl = min(256, L)
    nt = L // tl

    def body(x_ref, nw_ref, sc_ref, sh_ref, *rest):
        xm_ref, xmt_ref = rest[-3:-1] if also_bf16 is not None else rest
        xv = x_ref[...]
        r = lax.rsqrt(jnp.mean(xv * xv, axis=-1, keepdims=True) + EPS)
        xm = (xv * r * nw_ref[...]) * (1.0 + sc_ref[...]) + sh_ref[...]
        xm_b = xm.astype(BF16)
        xm_ref[...] = xm_b
        xmt_ref[...] = xm_b.T
        if also_bf16 is not None:
            rest[-1][...] = rest[0][...].astype(BF16)

    in_specs = [pl.BlockSpec((tl, D), lambda i: (i, 0)), _vec_spec(D), _vec_spec(D), _vec_spec(D)]
    out_specs = [pl.BlockSpec((tl, D), lambda i: (i, 0)), pl.BlockSpec((D, tl), lambda i: (0, i))]
    out_shape = [jax.ShapeDtypeStruct((L, D), BF16), jax.ShapeDtypeStruct((D, L), BF16)]
    args = [x, nw, sc, sh]
    if also_bf16 is not None:
        R, C = also_bf16.shape
        slab = pl.BlockSpec((R // nt, C), lambda i: (i, 0))
        in_specs.append(slab)
        out_specs.append(slab)
        out_shape.append(jax.ShapeDtypeStruct((R, C), BF16))
        args.append(also_bf16)
    return pl.pallas_call(
        body, name=name, grid=(nt,), in_specs=in_specs, out_specs=out_specs, out_shape=out_shape,
        compiler_params=_cparams(("parallel",)),
    )(*args)


QK_BLOCK, V_BLOCK = 4, 5


def _in_proj(xm, w, name, first=0, count=None):
    M, D = xm.shape
    count = w.shape[1] // D if count is None else count
    tm = min(1024, M)

    def body(a_ref, b_ref, o_ref, qk_ref):
        acc = _dot(a_ref[...], b_ref[...])
        o_ref[...] = acc.astype(o_ref.dtype)

        @pl.when(pl.program_id(1) == QK_BLOCK - first)
        def _():
            qk_ref[...] = acc

    return pl.pallas_call(
        body, name=name, grid=(M // tm, count),
        in_specs=[pl.BlockSpec((tm, D), lambda i, j: (i, 0)), pl.BlockSpec((D, D), lambda i, j: (0, first + j))],
        out_specs=[pl.BlockSpec((tm, D), lambda i, j: (i, j)), pl.BlockSpec((tm, D), lambda i, j: (i, 0))],
        out_shape=[jax.ShapeDtypeStruct((M, count * D), BF16), jax.ShapeDtypeStruct((M, D), F32)],
        compiler_params=_cparams(("parallel", "arbitrary")),
    )(xm, w)


def _halo_specs(tl, L, D, col):
    hb = tl // 16
    last = L // 16 - 1
    prev = pl.BlockSpec((16, D), lambda i: (jnp.maximum(i * hb - 1, 0), col))
    nxt = pl.BlockSpec((16, D), lambda i: (jnp.minimum((i + 1) * hb, last), col))
    return prev, nxt


def _shift_rows(u, above, below):
    tl = u.shape[0]
    row = lax.broadcasted_iota(jnp.int32, u.shape, 0)
    dn = jnp.where(row == 0, above, pltpu.roll(u, 1, 0))
    up = jnp.where(row == tl - 1, below, pltpu.roll(u, tl - 1, 0))
    return dn, up


def _rope_tables(L):
    pos = jnp.arange(L)
    row = (pos // GRID_W).astype(F32)
    col = (pos % GRID_W).astype(F32)
    nf = DK // 4
    inv = ROPE_BASE ** (-jnp.arange(nf, dtype=F32) / nf)
    ang = jnp.concatenate([row[:, None] * inv, col[:, None] * inv], axis=-1)
    cos, sin = jnp.cos(ang), jnp.sin(ang)
    return jnp.concatenate([cos, cos, cos, cos], axis=-1), jnp.concatenate([-sin, sin, -sin, sin], axis=-1)


def _smem_spec():
    return pl.BlockSpec(memory_space=pltpu.SMEM)


def _pair_select(e0, e1):
    row = lax.broadcasted_iota(jnp.int32, e0.shape, 0)
    return jnp.where(row < DK, e0, e1)


def _head_lane_mask(shape, e):
    lane = lax.broadcasted_iota(jnp.int32, shape, 1)
    return (lane < DK) if e == 0 else (lane >= DK)


def _ctx_states(pc, pqk_c, lg, D):
    Lc = pc.shape[0]
    H = D // DV

    def body(lg_ref, k_ref, v_ref, s_ref):
        m = lax.broadcasted_iota(jnp.int32, (Lc, DV), 0).astype(F32)
        for pr in range(H // 2):
            k2 = k_ref[:, pr * 128:(pr + 1) * 128].astype(F32) * K_SCALE
            res = [[None, None], [None, None]]
            for e in range(2):
                h = 2 * pr + e
                v = v_ref[:, h * DV:(h + 1) * DV]
                dec_f = jnp.exp(lg_ref[0, h] * (Lc - 1.0 - m))
                dec_b = jnp.exp(lg_ref[1, h] * m)
                res[0][e] = _dot_tn((k2 * dec_f).astype(BF16), v)
                res[1][e] = _dot_tn((k2 * dec_b).astype(BF16), v)
            s_ref[0, pr] = _pair_select(res[0][0], res[0][1])
            s_ref[1, pr] = _pair_select(res[1][0], res[1][1])

    return pl.pallas_call(
        body, name="ctx_states", grid=(1,),
        in_specs=[_smem_spec(), pl.BlockSpec((Lc, D // 2), lambda i: (0, 1)), pl.BlockSpec((Lc, D), lambda i: (0, 1))],
        out_specs=pl.BlockSpec((2, H // 2, 128, 128), lambda i: (0, 0, 0, 0)),
        out_shape=jax.ShapeDtypeStruct((2, H // 2, 128, 128), F32),
    )(lg, pqk_c, pc)


T_M, T_MT = 0, 1
T_MF1, T_MB1 = 2, 3
T_QF, T_QB = 4, 5
T_KF, T_KB = 6, 7


def _decay_tables(lg, H):
    def body(lg_ref, t_ref):
        h = pl.program_id(0)
        lgf, lgb = lg_ref[0, h], lg_ref[1, h]
        i = lax.broadcasted_iota(jnp.int32, (CHUNK, CHUNK), 0).astype(F32)
        j = lax.broadcasted_iota(jnp.int32, (CHUNK, CHUNK), 1).astype(F32)
        d = i - j
        mf = jnp.where(d > 0, jnp.exp(lgf * jnp.maximum(d, 0.0)), 0.0)
        mb = jnp.where(d < 0, jnp.exp(lgb * jnp.maximum(-d, 0.0)), 0.0)
        mf_t = jnp.where(d < 0, jnp.exp(lgf * jnp.maximum(-d, 0.0)), 0.0)
        mb_t = jnp.where(d > 0, jnp.exp(lgb * jnp.maximum(d, 0.0)), 0.0)
        diag = jnp.where(d == 0, 2.0, 0.0)
        t_ref[0, T_M] = mf + mb + diag
        t_ref[0, T_MT] = mf_t + mb_t + diag
        t_ref[0, T_MF1] = mf * d
        t_ref[0, T_MB1] = mb * (-d)
        t_ref[0, T_QF] = jnp.exp(lgf * (i + 1.0))
        t_ref[0, T_QB] = jnp.exp(lgb * (CHUNK - i))
        t_ref[0, T_KF] = jnp.exp(lgf * (CHUNK - 1.0 - i))
        t_ref[0, T_KB] = jnp.exp(lgb * i)

    return pl.pallas_call(
        body, name="decay_tables", grid=(H,), in_specs=[_smem_spec()],
        out_specs=pl.BlockSpec((1, 8, CHUNK, CHUNK), lambda h: (h, 0, 0, 0)),
        out_shape=jax.ShapeDtypeStruct((H, 8, CHUNK, CHUNK), F32),
    )(lg)


def _tab_spec(H):
    return pl.BlockSpec((H, 8, CHUNK, CHUNK), lambda n: (0, 0, 0, 0))


def _chunk_decay(tab_ref, h):
    return tab_ref[h, T_QF, CHUNK - 1:CHUNK, :], tab_ref[h, T_QB, 0:1, :]


def _ret_states(kr, p, s0, tab, D):
    L = kr.shape[0]
    H = D // DV
    N = L // CHUNK
    HP = H // 2

    def body(tab_ref, kf_ref, kb_ref, vf_ref, vb_ref, s0_ref, sf_out, sb_out, sf, sb):
        n = pl.program_id(0)

        @pl.when(n == 0)
        def _():
            sf[...] = s0_ref[0]
            sb[...] = s0_ref[1]

        for cc in range(RET_CPB):
            cf_, cb_ = cc, RET_CPB - 1 - cc
            rf, rb = slice(cf_ * CHUNK, (cf_ + 1) * CHUNK), slice(cb_ * CHUNK, (cb_ + 1) * CHUNK)
            sf_out[cf_] = sf[...]
            sb_out[cb_] = sb[...]
            for pr in range(HP):
                kf2 = kf_ref[rf, pr * 128:(pr + 1) * 128].astype(F32)
                kb2 = kb_ref[rb, pr * 128:(pr + 1) * 128].astype(F32)
                inc_f, inc_b, gf, gb = [], [], [], []
                for e in range(2):
                    h = 2 * pr + e
                    inc_f.append(_dot_tn((kf2 * tab_ref[h, T_KF]).astype(BF16), vf_ref[rf, h * DV:(h + 1) * DV]))
                    inc_b.append(_dot_tn((kb2 * tab_ref[h, T_KB]).astype(BF16), vb_ref[rb, h * DV:(h + 1) * DV]))
                    cf, cb = _chunk_decay(tab_ref, h)
                    gf.append(jnp.broadcast_to(cf, (128, 128)))
                    gb.append(jnp.broadcast_to(cb, (128, 128)))
                sf[pr] = _pair_select(gf[0], gf[1]) * sf[pr] + _pair_select(inc_f[0], inc_f[1])
                sb[pr] = _pair_select(gb[0], gb[1]) * sb[pr] + _pair_select(inc_b[0], inc_b[1])

    st = jax.ShapeDtypeStruct((N, HP, 128, 128), F32)
    R = RET_CPB * CHUNK
    NB = N // RET_CPB
    return _riding_call(
        body, None, NB, name="ret_states", args=(tab, kr, kr, p, p, s0),
        in_specs=[_tab_spec(H),
                  pl.BlockSpec((R, D // 2), lambda n: (n, 0)),
                  pl.BlockSpec((R, D // 2), lambda n: (NB - 1 - n, 0)),
                  pl.BlockSpec((R, D), lambda n: (n, 5)),
                  pl.BlockSpec((R, D), lambda n: (NB - 1 - n, 5)),
                  pl.BlockSpec((2, HP, 128, 128), lambda n: (0, 0, 0, 0))],
        out_specs=[pl.BlockSpec((RET_CPB, HP, 128, 128), lambda n: (n, 0, 0, 0)),
                   pl.BlockSpec((RET_CPB, HP, 128, 128), lambda n: (NB - 1 - n, 0, 0, 0))],
        out_shape=[st, st],
        scratch=[pltpu.VMEM((HP, 128, 128), F32), pltpu.VMEM((HP, 128, 128), F32)],
        cparams=_cparams(("arbitrary",)))


def _ret_out(qr, kr, p, sf_prev, sb_prev, gn_w, tab, D):
    L = qr.shape[0]
    H = D // DV
    N = L // CHUNK
    HP = H // 2

    def body(tab_ref, q_ref, k_ref, v_ref, zb_ref, sf_ref, sb_ref, gn_ref, o_ref, yb_ref):
        def chunk(cc, carry):
            rows = pl.ds(pl.multiple_of(cc * CHUNK, CHUNK), CHUNK)
            for pr in range(HP):
                q2 = q_ref[rows, pr * 128:(pr + 1) * 128]
                k2 = k_ref[rows, pr * 128:(pr + 1) * 128]
                sfp = sf_ref[cc, pr].astype(BF16)
                sbp = sb_ref[cc, pr].astype(BF16)
                for e in range(2):
                    h = 2 * pr + e
                    sl = slice(h * DV, (h + 1) * DV)
                    qm = jnp.where(_head_lane_mask(q2.shape, e), q2, jnp.zeros_like(q2))
                    a = (_dot_nt(qm, k2) * tab_ref[h, T_M]).astype(BF16)
                    qf = qm.astype(F32)
                    o = _dot(a, v_ref[rows, sl])
                    o += _dot((qf * tab_ref[h, T_QF]).astype(BF16), sfp)
                    o += _dot((qf * tab_ref[h, T_QB]).astype(BF16), sbp)
                    o_ref[rows, sl] = o
                    mu = jnp.mean(o, axis=-1, keepdims=True)
                    oc = o - mu
                    rstd = lax.rsqrt(jnp.mean(oc * oc, axis=-1, keepdims=True) + EPS)
                    zb = zb_ref[rows, sl].astype(F32)
                    yb_ref[rows, sl] = (zb * _sigmoid(zb) * (oc * rstd * gn_ref[:, sl])).astype(BF16)
            return carry

        lax.fori_loop(0, RET_CPB, chunk, 0)

    R = RET_CPB * CHUNK
    return _riding_call(
        body, None, N // RET_CPB, name="ret_out", args=(tab, qr, kr, p, p, sf_prev, sb_prev, gn_w),
        in_specs=[_tab_spec(H),
                  pl.BlockSpec((R, D // 2), lambda n: (n, 0)),
                  pl.BlockSpec((R, D // 2), lambda n: (n, 0)),
                  pl.BlockSpec((R, D), lambda n: (n, 5)),
                  pl.BlockSpec((R, D), lambda n: (n, 6)),
                  pl.BlockSpec((RET_CPB, HP, 128, 128), lambda n: (n, 0, 0, 0)),
                  pl.BlockSpec((RET_CPB, HP, 128, 128), lambda n: (n, 0, 0, 0)),
                  _vec_spec(D)],
        out_specs=[pl.BlockSpec((R, D), lambda n: (n, 0)), pl.BlockSpec((R, D), lambda n: (n, 0))],
        out_shape=[jax.ShapeDtypeStruct((L, D), F32), jax.ShapeDtypeStruct((L, D), BF16)],
        cparams=_cparams(("arbitrary",)))


def _mid(p, yb, o, x, tgt, w3, g, fw, conv_w, conv_b, gn_w, D):
    L = x.shape[0]
    H = D // DV
    tm = min(256, L)
    nt = L // tm

    def body(h_ref, bg_ref, cg_ref, za_ref, hp_ref, hn_ref, cp_ref, cn_ref, yb_ref, ga_ref, gb_ref, zb_ref, o_ref,
             x_ref, t_ref, w_hbm, g_ref, fw_ref, cw_ref, cb_ref, gn_ref,
             dx1_ref, dya_ref, do_ref, dzb_ref, dgab_ref, dw_hbm, st_ref, w_vm, dw_acc, sem):
        i = pl.program_id(0)

        @pl.when(i == 0)
        def _():
            cp = pltpu.make_async_copy(w_hbm, w_vm, sem)
            cp.start()
            dw_acc[...] = jnp.zeros_like(dw_acc)
            st_ref[...] = jnp.zeros_like(st_ref)
            cp.wait()

        u = cg_ref[...].astype(F32) * h_ref[...].astype(F32)
        above = jnp.where(i == 0, 0.0, cp_ref[15:16, :].astype(F32) * hp_ref[15:16, :].astype(F32))
        below = jnp.where(i == nt - 1, 0.0, cn_ref[0:1, :].astype(F32) * hn_ref[0:1, :].astype(F32))
        dn, up = _shift_rows(u, above, below)
        co = cw_ref[0:1, :] * dn + cw_ref[1:2, :] * u + cw_ref[2:3, :] * up + cb_ref[...]
        za = za_ref[...].astype(F32)
        ya_b = (za * _sigmoid(za) * bg_ref[...].astype(F32) * co).astype(BF16)
        yb_b = yb_ref[...]
        y_a = _dot(ya_b, w_vm[0])
        y_b = _dot(yb_b, w_vm[1])
        sga = _sigmoid(ga_ref[...].astype(F32))
        sgb = _sigmoid(gb_ref[...].astype(F32))
        mix_b = (sga * y_a + sgb * y_b).astype(BF16)
        y_x = _dot(mix_b, w_vm[2])
        gvec, fwv = g_ref[...], fw_ref[...]
        x1 = x_ref[...] + gvec * y_x
        r1 = lax.rsqrt(jnp.mean(x1 * x1, axis=-1, keepdims=True) + EPS)
        xh = x1 * r1
        diff = xh * fwv - t_ref[...]
        dout = diff * (1.0 / D)
        dxh = dout * fwv
        dx1 = r1 * (dxh - xh * jnp.mean(dxh * xh, axis=-1, keepdims=True))
        dx1_ref[...] = dx1
        st_ref[0:1, :] += jnp.sum(dout * xh, axis=0, keepdims=True)
        st_ref[1:2, :] += jnp.sum(dx1 * y_x, axis=0, keepdims=True)
        st_ref[2:3, :] += jnp.sum(diff * diff, axis=0, keepdims=True)
        dyx_b = (dx1 * gvec).astype(BF16)
        dmix = _dot_nt(dyx_b, w_vm[2])
        dw_acc[2] += _dot_tn(mix_b, dyx_b)
        dya_b = (dmix * sga).astype(BF16)
        dyb_b = (dmix * sgb).astype(BF16)
        dgab_ref[:, 0:D] = (dmix * y_a * sga * (1.0 - sga)).astype(BF16)
        dgab_ref[:, D:2 * D] = (dmix * y_b * sgb * (1.0 - sgb)).astype(BF16)
        dya_ref[...] = _dot_nt(dya_b, w_vm[0])
        dyb = _dot_nt(dyb_b, w_vm[1])
        dw_acc[0] += _dot_tn(ya_b, dya_b)
        dw_acc[1] += _dot_tn(yb_b, dyb_b)

        for h in range(H):
            sl = slice(h * DV, (h + 1) * DV)
            ov = o_ref[:, sl]
            oc = ov - jnp.mean(ov, axis=-1, keepdims=True)
            rstd = lax.rsqrt(jnp.mean(oc * oc, axis=-1, keepdims=True) + EPS)
            rn = oc * rstd
            gw = gn_ref[:, sl]
            zb = zb_ref[:, sl].astype(F32)
            sz = _sigmoid(zb)
            dy = dyb[:, sl]
            dzb_ref[:, sl] = (dy * (rn * gw) * (sz * (1.0 + zb * (1.0 - sz)))).astype(BF16)
            dretn = dy * (zb * sz)
            st_ref[3:4, sl] += jnp.sum(dretn * rn, axis=0, keepdims=True)
            drn = dretn * gw
            do_ref[:, sl] = (rstd * (drn - jnp.mean(drn, axis=-1, keepdims=True)
                                     - rn * jnp.mean(drn * rn, axis=-1, keepdims=True))).astype(BF16)

        @pl.when(i == nt - 1)
        def _():
            out = pltpu.make_async_copy(dw_acc, dw_hbm, sem)
            out.start()
            out.wait()

    row = lambda col: pl.BlockSpec((tm, D), lambda i: (i, col))
    any_spec = pl.BlockSpec(memory_space=pl.ANY)
    f32o = jax.ShapeDtypeStruct((L, D), F32)
    bf16o = jax.ShapeDtypeStruct((L, D), BF16)
    hp, hn = _halo_specs(tm, L, D, 0)
    cp, cn = _halo_specs(tm, L, D, 2)
    return pl.pallas_call(
        body, name="mid", grid=(nt,),
        in_specs=[row(0), row(1), row(2), row(3), hp, hn, cp, cn, row(0), row(7), row(8), row(6), row(0),
                  row(0), row(0), any_spec, _vec_spec(D), _vec_spec(D),
                  pl.BlockSpec((8, D), lambda i: (0, 0)), _vec_spec(D), _vec_spec(D)],
        out_specs=[row(0), row(0), row(0), row(0), pl.BlockSpec((tm, 2 * D), lambda i: (i, 0)), any_spec,
                   pl.BlockSpec((8, D), lambda i: (0, 0))],
        out_shape=[f32o, f32o, bf16o, bf16o, jax.ShapeDtypeStruct((L, 2 * D), BF16),
                   jax.ShapeDtypeStruct((3, D, D), F32), jax.ShapeDtypeStruct((8, D), F32)],
        scratch_shapes=[pltpu.VMEM((3, D, D), BF16), pltpu.VMEM((3, D, D), F32), pltpu.SemaphoreType.DMA],
        compiler_params=_cparams(("arbitrary",), VMEM_LIMIT),
    )(p, p, p, p, p, p, p, p, yb, p, p, p, o, x, tgt, w3, g, fw, conv_w, conv_b, gn_w)


def _conv_bwd(dya, p, conv_w, conv_b, D, exchange=None):
    L = p.shape[0]
    tl = min(256, L)
    nt = L // tl

    def body(d_ref, h_ref, bg_ref, cg_ref, za_ref,
             dp_ref, dn_ref, hp_ref, hn_ref, bp_ref, bn_ref, cp_ref, cn_ref, zp_ref, zn_ref,
             w_ref, b_ref, dc_ref, st_ref):
        i = pl.program_id(0)

        @pl.when(i == 0)
        def _():
            st_ref[...] = jnp.zeros_like(st_ref)

        first, last = i == 0, i == nt - 1
        h = h_ref[...].astype(F32)
        cg = cg_ref[...].astype(F32)
        bg = bg_ref[...].astype(F32)
        za = za_ref[...].astype(F32)
        dy = d_ref[...].astype(F32)
        u = cg * h
        u_above = jnp.where(first, 0.0, cp_ref[15:16, :].astype(F32) * hp_ref[15:16, :].astype(F32))
        u_below = jnp.where(last, 0.0, cn_ref[0:1, :].astype(F32) * hn_ref[0:1, :].astype(F32))
        u_dn, u_up = _shift_rows(u, u_above, u_below)
        w0, w1, w2 = w_ref[0:1, :], w_ref[1:2, :], w_ref[2:3, :]
        co = w0 * u_dn + w1 * u + w2 * u_up + b_ref[...]
        sz = _sigmoid(za)
        silu = za * sz
        dc_ref[:, 3 * D:4 * D] = (dy * bg * co * (sz * (1.0 + za * (1.0 - sz)))).astype(BF16)
        dc_ref[:, D:2 * D] = (dy * silu * co).astype(BF16)
        dco = dy * silu * bg

        def edge(dr, zr, br, r):
            z = zr[r:r + 1, :].astype(F32)
            return dr[r:r + 1, :].astype(F32) * (z * _sigmoid(z)) * br[r:r + 1, :].astype(F32)

        dco_above = jnp.where(first, 0.0, edge(dp_ref, zp_ref, bp_ref, 15))
        dco_below = jnp.where(last, 0.0, edge(dn_ref, zn_ref, bn_ref, 0))
        dco_dn, dco_up = _shift_rows(dco, dco_above, dco_below)
        du = w0 * dco_up + w1 * dco + w2 * dco_dn
        dc_ref[:, 2 * D:3 * D] = (du * h).astype(BF16)
        dc_ref[:, 0:D] = (du * cg).astype(BF16)
        st_ref[0:1, :] += jnp.sum(dco * u_dn, axis=0, keepdims=True)
        st_ref[1:2, :] += jnp.sum(dco * u, axis=0, keepdims=True)
        st_ref[2:3, :] += jnp.sum(dco * u_up, axis=0, keepdims=True)
        st_ref[3:4, :] += jnp.sum(dco, axis=0, keepdims=True)

    main = lambda col: pl.BlockSpec((tl, D), lambda i: (i, col))
    halos = []
    for col in (0, 0, 1, 2, 3):
        halos.extend(_halo_specs(tl, L, D, col))
    return _riding_call(
        body, exchange, nt, name="conv_bwd",
        args=(dya, p, p, p, p, dya, dya, p, p, p, p, p, p, p, p, conv_w, conv_b),
        in_specs=[main(0), main(0), main(1), main(2), main(3)] + halos
                 + [pl.BlockSpec((8, D), lambda i: (0, 0)), _vec_spec(D)],
        out_specs=[pl.BlockSpec((tl, 4 * D), lambda i: (i, 0)), pl.BlockSpec((8, D), lambda i: (0, 0))],
        out_shape=[jax.ShapeDtypeStruct((L, 4 * D), BF16), jax.ShapeDtypeStruct((8, D), F32)],
        cparams=_cparams(("arbitrary",)))


def _ret_bwd_states(qr, do, tab, D):
    L = qr.shape[0]
    H = D // DV
    N = L // CHUNK
    HP = H // 2

    def body(tab_ref, qf_ref, qb_ref, dof_ref, dob_ref, dsf_out, dsb_out, ds0_out, dsf, dsb):
        n = pl.program_id(0)

        @pl.when(n == 0)
        def _():
            dsf[...] = jnp.zeros_like(dsf)
            dsb[...] = jnp.zeros_like(dsb)

        for cc in range(RET_CPB):
            cf_, cb_ = RET_CPB - 1 - cc, cc
            rf, rb = slice(cf_ * CHUNK, (cf_ + 1) * CHUNK), slice(cb_ * CHUNK, (cb_ + 1) * CHUNK)
            dsf_out[cf_] = dsf[...]
            dsb_out[cb_] = dsb[...]
            for pr in range(HP):
                qf2 = qf_ref[rf, pr * 128:(pr + 1) * 128].astype(F32)
                qb2 = qb_ref[rb, pr * 128:(pr + 1) * 128].astype(F32)
                inc_f, inc_b, gf, gb = [], [], [], []
                for e in range(2):
                    h = 2 * pr + e
                    inc_f.append(_dot_tn((qf2 * tab_ref[h, T_QF]).astype(BF16), dof_ref[rf, h * DV:(h + 1) * DV]))
                    inc_b.append(_dot_tn((qb2 * tab_ref[h, T_QB]).astype(BF16), dob_ref[rb, h * DV:(h + 1) * DV]))
                    cf, cb = _chunk_decay(tab_ref, h)
                    gf.append(jnp.broadcast_to(cf, (128, 128)))
                    gb.append(jnp.broadcast_to(cb, (128, 128)))
                dsf[pr] = _pair_select(gf[0], gf[1]) * dsf[pr] + _pair_select(inc_f[0], inc_f[1])
                dsb[pr] = _pair_select(gb[0], gb[1]) * dsb[pr] + _pair_select(inc_b[0], inc_b[1])

        @pl.when(n == NB - 1)
        def _():
            ds0_out[0] = dsf[...]
            ds0_out[1] = dsb[...]

    st = jax.ShapeDtypeStruct((N, HP, 128, 128), F32)
    R = RET_CPB * CHUNK
    NB = N // RET_CPB
    return pl.pallas_call(
        body, name="ret_bwd_states", grid=(NB,),
        in_specs=[_tab_spec(H),
                  pl.BlockSpec((R, D // 2), lambda n: (NB - 1 - n, 0)),
                  pl.BlockSpec((R, D // 2), lambda n: (n, 0)),
                  pl.BlockSpec((R, D), lambda n: (NB - 1 - n, 0)),
                  pl.BlockSpec((R, D), lambda n: (n, 0))],
        out_specs=[pl.BlockSpec((RET_CPB, HP, 128, 128), lambda n: (NB - 1 - n, 0, 0, 0)),
                   pl.BlockSpec((RET_CPB, HP, 128, 128), lambda n: (n, 0, 0, 0)),
                   pl.BlockSpec((2, HP, 128, 128), lambda n: (0, 0, 0, 0))],
        out_shape=[st, st, jax.ShapeDtypeStruct((2, HP, 128, 128), F32)],
        scratch_shapes=[pltpu.VMEM((HP, 128, 128), F32), pltpu.VMEM((HP, 128, 128), F32)],
        compiler_params=_cparams(("arbitrary",)),
    )(tab, qr, qr, do, do)


def _ret_bwd_main(qr, kr, p, do, sf_prev, sb_prev, dsf, dsb, c2, s2, tab, D, exchange=None):
    L = qr.shape[0]
    H = D // DV
    N = L // CHUNK
    HP = H // 2
    W = D // 2

    def body(tab_ref, q_ref, k_ref, v_ref, do_ref, sf_ref, sb_ref, dsf_ref, dsb_ref, c_ref, s_ref,
             dr_ref, st_ref, dl_acc):
        @pl.when(pl.program_id(0) == 0)
        def _():
            dl_acc[...] = jnp.zeros_like(dl_acc)

        i = lax.broadcasted_iota(jnp.int32, (CHUNK, 128), 0).astype(F32)
        rowid = lax.broadcasted_iota(jnp.int32, (128, 128), 0)

        def chunk(cc, carry):
            rows = pl.ds(pl.multiple_of(cc * CHUNK, CHUNK), CHUNK)
            c, s = c_ref[rows, :], s_ref[rows, :]
            for pr in range(HP):
                ps = slice(pr * 128, (pr + 1) * 128)
                q2, k2 = q_ref[rows, ps], k_ref[rows, ps]
                sf32, sb32 = sf_ref[cc, pr], sb_ref[cc, pr]
                dsf32, dsb32 = dsf_ref[cc, pr], dsb_ref[cc, pr]
                sfp, sbp = sf32.astype(BF16), sb32.astype(BF16)
                dsfp, dsbp = dsf32.astype(BF16), dsb32.astype(BF16)
                dq2 = jnp.zeros((CHUNK, 128), F32)
                dk2 = jnp.zeros((CHUNK, 128), F32)
                for e in range(2):
                    h = 2 * pr + e
                    sl = slice(h * DV, (h + 1) * DV)
                    hm = _head_lane_mask(q2.shape, e)
                    qm = jnp.where(hm, q2, jnp.zeros_like(q2))
                    km = jnp.where(hm, k2, jnp.zeros_like(k2))
                    qf, kf = qm.astype(F32), km.astype(F32)
                    v, do = v_ref[rows, sl], do_ref[rows, sl]
                    vf, dof = v.astype(F32), do.astype(F32)
                    m_t = tab_ref[h, T_MT]
                    sc = _dot_nt(qm, k2)
                    dpm = _dot_nt(do, v)
                    dsc = (dpm * tab_ref[h, T_M]).astype(BF16)
                    a_t = (_dot_nt(km, q2) * m_t).astype(BF16)
                    dsc_t = (_dot_nt(v, do) * m_t).astype(BF16)
                    dq_f, dq_b = tab_ref[h, T_QF], tab_ref[h, T_QB]
                    dk_f, dk_b = tab_ref[h, T_KF], tab_ref[h, T_KB]
                    dq = _dot(dsc, km)
                    dq += jnp.where(hm, dq_f * _dot_nt(do, sfp) + dq_b * _dot_nt(do, sbp), 0.0)
                    dk = _dot(dsc_t, qm)
                    dk += jnp.where(hm, dk_f * _dot_nt(v, dsfp) + dk_b * _dot_nt(v, dsbp), 0.0)
                    kdf = _dot((kf * dk_f).astype(BF16), dsfp)
                    kdb = _dot((kf * dk_b).astype(BF16), dsbp)
                    dr_ref[rows, D + h * DV:D + (h + 1) * DV] = (_dot(a_t, do) + kdf + kdb).astype(BF16)
                    dq2 += dq
                    dk2 += dk
                    xf = _dot((qf * dq_f).astype(BF16), sfp)
                    xb = _dot((qf * dq_b).astype(BF16), sbp)
                    pair = (rowid < DK) if e == 0 else (rowid >= DK)
                    gcf, gcb = tab_ref[h, T_QF, CHUNK - 1:CHUNK, 0:1], tab_ref[h, T_QB, 0:1, 0:1]
                    scdp = sc * dpm
                    dl_acc[h, 0] += scdp * tab_ref[h, T_MF1] + xf * dof * (i + 1.0) \
                        + kdf * vf * (CHUNK - 1.0 - i) + (CHUNK * gcf) * jnp.where(pair, dsf32 * sf32, 0.0)
                    dl_acc[h, 1] += scdp * tab_ref[h, T_MB1] + xb * dof * (CHUNK - i) \
                        + kdb * vf * i + (CHUNK * gcb) * jnp.where(pair, dsb32 * sb32, 0.0)
                dr_ref[rows, ps] = (dq2 * c - _swap_halves(dq2) * s).astype(BF16)
                dr_ref[rows, W + pr * 128:W + (pr + 1) * 128] = \
                    ((dk2 * c - _swap_halves(dk2) * s) * K_SCALE).astype(BF16)
            return carry

        lax.fori_loop(0, RET_CPB, chunk, 0)

        @pl.when(pl.program_id(0) == N // RET_CPB - 1)
        def _():
            lane = lax.broadcasted_iota(jnp.int32, (1, 128), 1)
            acc = [jnp.zeros((1, 128), F32), jnp.zeros((1, 128), F32)]
            for h in range(H):
                for b in range(2):
                    acc[b] += jnp.where(lane == h, _sum_all(dl_acc[h, b]), 0.0)
            st_ref[...] = jnp.zeros_like(st_ref)
            st_ref[0:1, :] = acc[0]
            st_ref[1:2, :] = acc[1]

    R = RET_CPB * CHUNK
    st_spec = pl.BlockSpec((RET_CPB, HP, 128, 128), lambda n: (n, 0, 0, 0))
    half = pl.BlockSpec((R, W), lambda n: (n, 0))
    rope = pl.BlockSpec((R, 128), lambda n: (n, 0))
    return _riding_call(
        body, exchange, N // RET_CPB, name="ret_bwd_main",
        args=(tab, qr, kr, p, do, sf_prev, sb_prev, dsf, dsb, c2, s2),
        in_specs=[_tab_spec(H), half, half,
                  pl.BlockSpec((R, D), lambda n: (n, 5)),
                  pl.BlockSpec((R, D), lambda n: (n, 0)),
                  st_spec, st_spec, st_spec, st_spec, rope, rope],
        out_specs=[pl.BlockSpec((R, 2 * D), lambda n: (n, 0)),
                   pl.BlockSpec((8, 128), lambda n: (0, 0))],
        out_shape=[jax.ShapeDtypeStruct((L, 2 * D), BF16), jax.ShapeDtypeStruct((8, 128), F32)],
        scratch=[pltpu.VMEM((H, 2, CHUNK, 128), F32)],
        cparams=_cparams(("arbitrary",)))


def _ctx_bwd(pc, pqk_c, ds0, lg, D):
    Lc = pc.shape[0]
    H = D // DV
    HP = H // 2
    W = D // 2

    def body(lg_ref, k_ref, v_ref, ds_ref, dr_ref, st_ref):
        dqk_ref = dr_ref.at[:, 0:D]
        dv_ref = dr_ref.at[:, D:2 * D]
        m = lax.broadcasted_iota(jnp.int32, (Lc, 128), 0).astype(F32)
        lane = lax.broadcasted_iota(jnp.int32, (1, 128), 1)
        acc_f = jnp.zeros((1, 128), F32)
        acc_b = jnp.zeros((1, 128), F32)
        dqk_ref[:, 0:W] = jnp.zeros((Lc, W), BF16)
        for pr in range(HP):
            ps = slice(pr * 128, (pr + 1) * 128)
            k2 = k_ref[:, ps].astype(F32) * K_SCALE
            dsfp, dsbp = ds_ref[0, pr].astype(BF16), ds_ref[1, pr].astype(BF16)
            dk2 = jnp.zeros((Lc, 128), F32)
            for e in range(2):
                h = 2 * pr + e
                sl = slice(h * DV, (h + 1) * DV)
                hm = _head_lane_mask(k2.shape, e)
                km = jnp.where(hm, k2, 0.0)
                v = v_ref[:, sl]
                vf = v.astype(F32)
                dec_f = jnp.exp(lg_ref[0, h] * (Lc - 1.0 - m))
                dec_b = jnp.exp(lg_ref[1, h] * m)
                kdf = _dot((km * dec_f).astype(BF16), dsfp)
                kdb = _dot((km * dec_b).astype(BF16), dsbp)
                dv_ref[:, sl] = (kdf + kdb).astype(BF16)
                dk2 += jnp.where(hm, dec_f * _dot_nt(v, dsfp) + dec_b * _dot_nt(v, dsbp), 0.0)
                acc_f += jnp.where(lane == h, _sum_all(kdf * vf * (Lc - 1.0 - m)), 0.0)
                acc_b += jnp.where(lane == h, _sum_all(kdb * vf * m), 0.0)
            dqk_ref[:, W + pr * 128:W + (pr + 1) * 128] = (dk2 * K_SCALE).astype(BF16)
        st_ref[...] = jnp.zeros_like(st_ref)
        st_ref[0:1, :] = acc_f
        st_ref[1:2, :] = acc_b

    return pl.pallas_call(
        body, name="ctx_bwd", grid=(1,),
        in_specs=[_smem_spec(), pl.BlockSpec((Lc, W), lambda i: (0, 1)), pl.BlockSpec((Lc, D), lambda i: (0, 1)),
                  pl.BlockSpec((2, HP, 128, 128), lambda i: (0, 0, 0, 0))],
        out_specs=[pl.BlockSpec((Lc, 2 * D), lambda i: (0, 0)), pl.BlockSpec((8, 128), lambda i: (0, 0))],
        out_shape=[jax.ShapeDtypeStruct((Lc, 2 * D), BF16), jax.ShapeDtypeStruct((8, 128), F32)],
    )(lg, pqk_c, pc, ds0)


class _Exchange(NamedTuple):
    inputs: tuple
    out_shapes: tuple
    n_copies: int
    build: Callable


def _exchange_parts(exchange):
    if exchange is None:
        return [], [], [], [], []
    n = exchange.n_copies
    return (list(exchange.inputs), [ANY] * len(exchange.inputs), list(exchange.out_shapes),
            [ANY] * len(exchange.out_shapes), [pltpu.SemaphoreType.DMA((n,)), pltpu.SemaphoreType.DMA((n,))])


def _riding_call(body, exchange, n_steps, *, args, in_specs, out_specs, out_shape, name, cparams, scratch=()):
    ex_args, ex_in_specs, ex_shapes, ex_out_specs, ex_scratch = _exchange_parts(exchange)
    n_in, n_out, n_sc = len(args), len(out_shape), len(scratch)

    def riding(*refs):
        k = n_in + len(ex_args)
        ins, ex_in = refs[:n_in], refs[n_in:k]
        outs, ex_out = refs[k:k + n_out], refs[k + n_out:k + n_out + len(ex_shapes)]
        k += n_out + len(ex_shapes)
        own_scratch, ex_sems = refs[k:k + n_sc], refs[k + n_sc:]
        step = pl.program_id(0)
        if exchange is not None:
            @pl.when(step == 0)
            def _():
                for rc in exchange.build(ex_in, ex_out, *ex_sems):
                    rc.start()
        body(*ins, *outs, *own_scratch)
        if exchange is not None:
            @pl.when(step == n_steps - 1)
            def _():
                for rc in exchange.build(ex_in, ex_out, *ex_sems):
                    rc.wait()

    return tuple(pl.pallas_call(
        riding, name=name, grid=(n_steps,),
        in_specs=list(in_specs) + ex_in_specs, out_specs=list(out_specs) + ex_out_specs,
        out_shape=list(out_shape) + ex_shapes, scratch_shapes=list(scratch) + ex_scratch,
        compiler_params=cparams,
    )(*args, *ex_args))


def _dxm(groups, col0, w, x, nw, sc, dx1, name, exchange=None):
    L, D = x.shape
    tm = min(256, L)
    nt = L // tm
    ng = len(groups)
    widths = [g.shape[1] for g in groups]
    wtot = sum(widths)
    with_dx = dx1 is not None
    ex_args, ex_in_specs, ex_shapes, ex_out_specs, ex_scratch = _exchange_parts(exchange)
    n_in = ng + 4 + (1 if with_dx else 0)
    n_out = 2 if with_dx else 1

    def body(*refs):
        group_refs = refs[:ng]
        w_hbm, x_ref, nw_ref, sc_ref = refs[ng:ng + 4]
        ex_in = refs[n_in:n_in + len(ex_args)]
        outs = refs[n_in + len(ex_args):]
        if with_dx:
            dx1_ref, gx_ref, st_ref = refs[ng + 4], outs[0], outs[1]
        else:
            st_ref = outs[0]
        ex_out = outs[n_out:n_out + len(ex_shapes)]
        w_vm, sem = outs[n_out + len(ex_shapes):n_out + len(ex_shapes) + 2]
        ex_sems = outs[n_out + len(ex_shapes) + 2:]
        i = pl.program_id(0)

        @pl.when(i == 0)
        def _():
            cp = pltpu.make_async_copy(w_hbm.at[:, col0 * D:col0 * D + wtot], w_vm, sem)
            cp.start()
            if exchange is not None:
                for rc in exchange.build(ex_in, ex_out, *ex_sems):
                    rc.start()
            st_ref[...] = jnp.zeros_like(st_ref)
            cp.wait()

        dxm, off = None, 0
        for g_ref, wd in zip(group_refs, widths):
            part = _dot_nt(g_ref[...], w_vm[:, off:off + wd])
            dxm = part if dxm is None else dxm + part
            off += wd

        xv = x_ref[...]
        r = lax.rsqrt(jnp.mean(xv * xv, axis=-1, keepdims=True) + EPS)
        xh = xv * r
        nwv = nw_ref[...]
        dxn = dxm * (1.0 + sc_ref[...])
        st_ref[0:1, :] += jnp.sum(dxm, axis=0, keepdims=True)
        st_ref[1:2, :] += jnp.sum(dxm * (xh * nwv), axis=0, keepdims=True)
        st_ref[2:3, :] += jnp.sum(dxn * xh, axis=0, keepdims=True)
        if with_dx:
            dxh = dxn * nwv
            gx_ref[...] = dx1_ref[...] + r * (dxh - xh * jnp.mean(dxh * xh, axis=-1, keepdims=True))

        if exchange is not None:
            @pl.when(i == nt - 1)
            def _():
                for rc in exchange.build(ex_in, ex_out, *ex_sems):
                    rc.wait()

    row = pl.BlockSpec((tm, D), lambda i: (i, 0))
    in_specs = [pl.BlockSpec((tm, wd), lambda i: (i, 0)) for wd in widths] + [ANY, row, _vec_spec(D), _vec_spec(D)]
    out_specs = [pl.BlockSpec((8, D), lambda i: (0, 0))]
    out_shape = [jax.ShapeDtypeStruct((8, D), F32)]
    args = list(groups) + [w, x, nw, sc]
    if with_dx:
        in_specs.append(row)
        out_specs.insert(0, row)
        out_shape.insert(0, jax.ShapeDtypeStruct((L, D), F32))
        args.append(dx1)
    res = pl.pallas_call(
        body, name=name, grid=(nt,),
        in_specs=in_specs + ex_in_specs, out_specs=out_specs + ex_out_specs, out_shape=out_shape + ex_shapes,
        scratch_shapes=[pltpu.VMEM((D, wtot), BF16), pltpu.SemaphoreType.DMA] + ex_scratch,
        compiler_params=_cparams(("arbitrary",), VMEM_LIMIT),
    )(*args, *ex_args)
    gx = res[0] if with_dx else None
    return (gx, res[n_out - 1], *res[n_out:])


DW_TN = 512
DW_RING = 4


def _dw_in(xmt, groups, cmt, dr_c, D, pair):
    L = xmt.shape[1]
    Lc = cmt.shape[1]
    Dh = D // 2
    tn = min(DW_TN, D)
    nblk = [g.shape[1] // tn for g in groups]
    starts = [sum(nblk[:g]) for g in range(len(groups))]
    ng = len(groups)
    nj = sum(nblk)
    rows_out = Dh if pair else D

    def body(*refs):
        xt_hbm = refs[0]
        group_refs = refs[1:1 + ng]
        ct_hbm, drc_ref, o_ref = refs[1 + ng:4 + ng]
        rest = refs[4 + ng:]
        if pair:
            ra_hbm, xt_vm, ct_vm, loc, ring, s_send, s_recv = rest
            pos = _position()
            sib = _peer(pos, 1)
        else:
            xt_vm, ct_vm, loc = rest
        j = pl.program_id(0)

        @pl.when(j == 0)
        def _():
            if pair:
                c = pos[2]
                other = pl.ds(pl.multiple_of((1 - c) * Dh, Dh), Dh)
                mine = pl.ds(pl.multiple_of(c * Dh, Dh), Dh)
                cps = [pltpu.make_async_copy(xt_hbm.at[other, :], xt_vm.at[0:Dh, :], loc.at[0]),
                       pltpu.make_async_copy(xt_hbm.at[mine, :], xt_vm.at[Dh:D, :], loc.at[1]),
                       pltpu.make_async_copy(ct_hbm.at[other, :], ct_vm.at[0:Dh, :], loc.at[2]),
                       pltpu.make_async_copy(ct_hbm.at[mine, :], ct_vm.at[Dh:D, :], loc.at[3])]
            else:
                cps = [pltpu.make_async_copy(xt_hbm, xt_vm, loc.at[0]), pltpu.make_async_copy(ct_hbm, ct_vm, loc.at[1])]
            for cp in cps:
                cp.start()
            for cp in cps:
                cp.wait()

        def send(slot):
            cols = pl.ds(pl.multiple_of(j * tn, 128), tn)
            return pltpu.make_async_remote_copy(src_ref=ring.at[slot], dst_ref=ra_hbm.at[:, cols],
                                                send_sem=s_send.at[slot], recv_sem=s_recv,
                                                device_id=sib, device_id_type=MESH)

        for g in range(ng):
            @pl.when((j >= starts[g]) & (j < starts[g] + nblk[g]))
            def _(g=g):
                acc = _dot(xt_vm[...], group_refs[g][...])
                if g == 1:
                    acc += _dot(ct_vm[...], drc_ref[...])
                if not pair:
                    o_ref[...] = acc
                    return
                o_ref[...] = acc[Dh:, :]
                slot = lax.rem(j, DW_RING)

                @pl.when(j >= DW_RING)
                def _():
                    send(slot).wait_send()

                ring[slot] = acc[0:Dh, :]
                send(slot).start()

        if pair:
            @pl.when(j == nj - 1)
            def _():
                pltpu.make_async_remote_copy(src_ref=ra_hbm, dst_ref=ra_hbm, send_sem=s_send.at[0], recv_sem=s_recv,
                                             device_id=sib, device_id_type=MESH).wait_recv()
                for slot in range(DW_RING):
                    send(slot).wait_send()

    def group_spec(g, rows):
        return pl.BlockSpec((rows, tn), lambda j: (0, jnp.clip(j - starts[g], 0, nblk[g] - 1)))

    out_specs = [pl.BlockSpec((rows_out, tn), lambda j: (0, j))]
    out_shape = [jax.ShapeDtypeStruct((rows_out, nj * tn), F32)]
    scratch = [pltpu.VMEM((D, L), BF16), pltpu.VMEM((D, Lc), BF16), pltpu.SemaphoreType.DMA((4,))]
    if pair:
        out_specs.append(ANY)
        out_shape.append(jax.ShapeDtypeStruct((Dh, nj * tn), F32))
        scratch += [pltpu.VMEM((DW_RING, Dh, tn), F32), pltpu.SemaphoreType.DMA((DW_RING,)), pltpu.SemaphoreType.DMA]
    return tuple(pl.pallas_call(
        body, name="dw_in", grid=(nj,),
        in_specs=[ANY] + [group_spec(g, L) for g in range(ng)] + [ANY, group_spec(1, Lc)],
        out_specs=out_specs, out_shape=out_shape, scratch_shapes=scratch,
        compiler_params=_cparams(("arbitrary",), VMEM_LIMIT),
    )(xmt, *groups, cmt, dr_c))


def _local_step(x, ctx, tgt, mod_x, mod_c, norm_w, conv_w8, conv_b, lg, gn_w, fw, project, csidx=None, w_in_s=None):
    L, D = x.shape
    sh_x, sc_x, g_x = mod_x[0:1], mod_x[1:2], mod_x[2:3]
    sh_c, sc_c = mod_c[0:1], mod_c[1:2]
    c2, s2 = _rope_tables(L)
    tab = _decay_tables(lg, D // DV)

    xm, xmt, *w_bf = _norm_mod(x, norm_w, sc_x, sh_x, "norm_mod_x", w_in_s)
    cm, cmt = _norm_mod(ctx, norm_w, sc_c, sh_c, "norm_mod_ctx")
    reduce = csidx is not None
    p, qr, kr, w_in, w3 = project(xm, c2, s2, *w_bf)
    pc, pqk_c = _in_proj(cm, w_in, "in_proj_ctx", QK_BLOCK, 2)
    s0 = _ctx_states(pc, pqk_c, lg, D)
    sf_prev, sb_prev = _ret_states(kr, p, s0, tab, D)
    o, yb = _ret_out(qr, kr, p, sf_prev, sb_prev, gn_w, tab, D)
    dx1, dya, do, dzb, dgab, dw3, st_mid = _mid(p, yb, o, x, tgt, w3, g_x, fw, conv_w8, conv_b, gn_w, D)
    dw3_5 = dw3.reshape(3, N_SHARD, 2, D // 8, D)
    dconv, st_conv, *ra_3 = _conv_bwd(dya, p, conv_w8, conv_b, D, _pair_exchange_w3(dw3_5) if reduce else None)
    dsf, dsb, ds0 = _ret_bwd_states(qr, do, tab, D)
    cs_3 = _sum_pair_w3(csidx[0:1], dw3_5, ra_3[0]) if reduce else None
    dret, st_lg, *rb_3 = _ret_bwd_main(qr, kr, p, do, sf_prev, sb_prev, dsf, dsb, c2, s2, tab, D,
                                       _chips_exchange_w3(cs_3) if reduce else None)
    g_3 = _sum_chips_w3(csidx, cs_3, rb_3[0]) if reduce else dw3
    dret_c, st_lgc = _ctx_bwd(pc, pqk_c, ds0, lg, D)
    groups = (dconv, dret, dzb, dgab)
    _, st_c = _dxm((dret_c,), 4, w_in, ctx, norm_w, sc_c, None, "dxm_ctx")
    return groups, dret_c, xmt, cmt, dx1, sc_x, w_in, g_3, (st_mid, st_conv, st_lg, st_lgc, st_c)


CHIP_FLIPS = (4, 2, 6)
ANY = pl.BlockSpec(memory_space=pl.ANY)
VMEM_FULL = pl.BlockSpec(memory_space=pltpu.VMEM)


def _position():
    return lax.axis_index("x"), lax.axis_index("y"), lax.axis_index("c")


def _peer(pos, k):
    x, y, c = pos
    return (1 - x if k & 4 else x, 1 - y if k & 2 else y, 1 - c if k & 1 else c)


def _dev_id(pos):
    return 4 * pos[0] + 2 * pos[1] + pos[2]


def _shard_of(pos):
    return 2 * pos[0] + pos[1]


def _remote(src, dst, send_sems, recv_sems, idx, to):
    return pltpu.make_async_remote_copy(src_ref=src, dst_ref=dst, send_sem=send_sems.at[idx],
                                        recv_sem=recv_sems.at[idx], device_id=to, device_id_type=MESH)


def _dot_f32(a, b):
    return jnp.dot(a, b, precision=lax.Precision.HIGHEST, preferred_element_type=F32)


def _silu(x):
    return x * _sigmoid(x)


def _fwd_small(c8, cctx8, ada_w, ada_b, conv_w8):
    D = c8.shape[1]
    Wm = ada_w.shape[1]
    Dq = conv_w8.shape[1]

    def body(c_ref, cc_ref, aw_ref, ab_ref, cw_ref, act_ref, mod_ref, cwf_ref,
             cbuf, pmine, pbuf, wbuf, s_c, r_c, s_p, r_p, s_w, r_w):
        pos = _position()
        me, s = _dev_id(pos), _shard_of(pos)
        cbuf[me] = c_ref[...]
        wbuf[s] = cw_ref[...]
        sends = [_remote(c_ref, cbuf.at[me], s_c, r_c, k - 1, _peer(pos, k)) for k in range(1, 8)]
        sends += [_remote(cw_ref, wbuf.at[s], s_w, r_w, j, _peer(pos, k)) for j, k in enumerate(CHIP_FLIPS)]
        for cp in sends:
            cp.start()
        for k in range(1, 8):
            _remote(c_ref, cbuf.at[_dev_id(_peer(pos, k))], s_c, r_c, k - 1, _peer(pos, k)).wait_recv()
        for d in range(N_DEV):
            act_ref[d:d + 1, :] = _silu(cbuf[d, 0:1, :])
        act_ref[8:9, :] = _silu(cc_ref[0:1, :])
        act_ref[9:16, :] = jnp.zeros((7, D), F32)
        part = _dot_f32(act_ref[...], aw_ref[...])
        pmine[...] = part
        pbuf[s] = part
        psend = [_remote(pmine, pbuf.at[s], s_p, r_p, j, _peer(pos, k)) for j, k in enumerate(CHIP_FLIPS)]
        for cp in psend:
            cp.start()
        for j, k in enumerate(CHIP_FLIPS):
            t = _shard_of(_peer(pos, k))
            _remote(pmine, pbuf.at[t], s_p, r_p, j, _peer(pos, k)).wait_recv()
            _remote(cw_ref, wbuf.at[t], s_w, r_w, j, _peer(pos, k)).wait_recv()
        for t in range(N_SHARD):
            mod_ref[:, t * Wm:(t + 1) * Wm] = pbuf[t] + ab_ref[:, t * Wm:(t + 1) * Wm]
            cwf_ref[:, t * Dq:(t + 1) * Dq] = wbuf[t]
        for cp in sends + psend:
            cp.wait_send()

    return pl.pallas_call(
        body, name="fwd_small",
        in_specs=[VMEM_FULL] * 5, out_specs=[VMEM_FULL] * 3,
        out_shape=[jax.ShapeDtypeStruct((16, D), F32), jax.ShapeDtypeStruct((16, 3 * D), F32),
                   jax.ShapeDtypeStruct((8, D), F32)],
        scratch_shapes=[pltpu.VMEM((N_DEV, 8, D), F32), pltpu.VMEM((16, Wm), F32),
                        pltpu.VMEM((N_SHARD, 16, Wm), F32), pltpu.VMEM((N_SHARD, 8, Dq), F32),
                        pltpu.SemaphoreType.DMA((7,)), pltpu.SemaphoreType.DMA((7,)),
                        pltpu.SemaphoreType.DMA((3,)), pltpu.SemaphoreType.DMA((3,)),
                        pltpu.SemaphoreType.DMA((3,)), pltpu.SemaphoreType.DMA((3,))],
        compiler_params=_cparams(None, VMEM_LIMIT),
    )(c8, cctx8, ada_w, ada_b, conv_w8)


AG_CHUNKS = 3


def _ag_in_proj(xm, w_in_s, w3_s, c2, s2):
    L, D = xm.shape
    Wc = w_in_s.shape[1]
    Wq = Wc // AG_CHUNKS
    Dh = D // 2
    Do = w3_s[0].shape[1]
    TM = min(1024, L // 4)
    NT = L // TM
    NQ = AG_CHUNKS
    order = [(q, j) for q in range(NQ) for j in (0, 1)] + [(q, 2) for q in range(NQ)]

    def body(xm_ref, wi_hbm, wa_ref, wb_ref, wo_ref, c_ref, s_ref, p_hbm, qr_hbm, kr_hbm, fi_hbm, f3_hbm,
             w_vm, s3, stage, qk_stage, ici_s, ici_r, d2d_s, d2d_r, w3_s_, w3_r_, fw_s, fw_r,
             loc, out_sem, qk_sem):
        pos = _position()
        c = pos[2]
        s = _shard_of(pos)
        sib = _peer(pos, 1)
        mine = pl.ds(pl.multiple_of(c * Dh, Dh), Dh)
        other = pl.ds(pl.multiple_of((1 - c) * Dh, Dh), Dh)

        def abs_col(t, q):
            return pl.ds(pl.multiple_of(t * Wc + q * Wq, 128), Wq)

        own = [pltpu.make_async_copy(wi_hbm.at[:, q * Wq:(q + 1) * Wq], w_vm.at[0, q], loc.at[2 + 4 * NQ + q])
               for q in range(NQ)]
        for cp in own:
            cp.start()
        for cp in own:
            cp.wait()
        sends = [_remote(w_vm.at[0, q, mine, :], w_vm.at[1 + j, q, mine, :], ici_s, ici_r, q * 3 + j,
                         _peer(pos, CHIP_FLIPS[j])) for q, j in order if j < 2]
        for cp in sends:
            cp.start()
        for a, w_ref in enumerate((wa_ref, wb_ref, wo_ref)):
            s3[a] = w_ref[...].astype(BF16)
        w3_sends = [_remote(s3.at[:, c], f3_hbm.at[:, s, c], w3_s_, w3_r_, j, _peer(pos, k))
                    for j, k in enumerate(CHIP_FLIPS)]
        local = [pltpu.make_async_copy(s3, f3_hbm.at[:, s], loc.at[1])]
        local += [pltpu.make_async_copy(w_vm.at[0, q], fi_hbm.at[:, abs_col(s, q)], loc.at[2 + q]) for q in range(NQ)]
        for cp in local:
            cp.start()

        def out_copy(slot, rows, cols):
            return pltpu.make_async_copy(stage.at[slot], p_hbm.at[rows, cols], out_sem.at[slot])

        def block(r, q, t, first):
            cols = abs_col(t, q)

            def row_tile(rt, carry):
                rows = pl.ds(pl.multiple_of(rt * TM, TM), TM)
                acc = _dot(xm_ref[rows, :], w_vm[r, q])
                slot = lax.rem(rt, 2)

                @pl.when(rt >= 2 if first else rt >= 0)
                def _():
                    out_copy(slot, rows, cols).wait()

                stage[slot] = acc.astype(BF16)
                out_copy(slot, rows, cols).start()

                def rotary(lo, scale, dst_hbm):
                    c, s = c_ref[rows, :], s_ref[rows, :]
                    for pr in range(Dh // 128):
                        tq = acc[:, lo + pr * 128:lo + (pr + 1) * 128] * scale
                        qk_stage[:, pr * 128:(pr + 1) * 128] = (tq * c + _swap_halves(tq) * s).astype(BF16)
                    cp = pltpu.make_async_copy(qk_stage, dst_hbm.at[rows, :], qk_sem)
                    cp.start()
                    cp.wait()

                if q == NQ - 1:
                    @pl.when(t == 1)
                    def _():
                        rotary(Wq - Dh, 1.0, qr_hbm)
                if q == 0:
                    @pl.when(t == 2)
                    def _():
                        rotary(0, K_SCALE, kr_hbm)
                return carry

            lax.fori_loop(0, NT, row_tile, 0)

        passed = []

        def hand_on(q, j):
            half = w_vm.at[1 + j, q, mine, :]
            if j == 2:
                _remote(half, half, fw_s, fw_r, q, sib).wait_recv()
            else:
                _remote(half, half, ici_s, ici_r, q * 3 + j, sib).wait_recv()

                @pl.when(c == (0 if j == q % 2 else 1))
                def _():
                    _remote(half, w_vm.at[3, q, mine, :], fw_s, fw_r, q, _peer(pos, CHIP_FLIPS[1 - j])).start()
            fwd = _remote(half, half, d2d_s, d2d_r, q * 3 + j, sib)
            fwd.start()
            passed.append(fwd)

        for q in range(NQ):
            if q == NQ - 1:
                hand_on(*order[0])
            block(0, q, s, q == 0)
        for n, (q, j) in enumerate(order):
            r, idx = 1 + j, q * 3 + j
            t = _shard_of(_peer(pos, CHIP_FLIPS[j]))
            if n + 1 < len(order):
                hand_on(*order[n + 1])
            if n + 1 == 2 * NQ - 1:
                for cp in w3_sends:
                    cp.start()
            _remote(w_vm.at[r, q, other, :], w_vm.at[r, q, other, :], d2d_s, d2d_r, idx, sib).wait_recv()
            block(r, q, t, False)
            cp = pltpu.make_async_copy(w_vm.at[r, q], fi_hbm.at[:, abs_col(t, q)], loc.at[2 + NQ + idx])
            cp.start()
            local.append(cp)
        for j, k in enumerate(CHIP_FLIPS):
            t = _shard_of(_peer(pos, k))
            _remote(s3.at[:, c], f3_hbm.at[:, t, c], w3_s_, w3_r_, j, sib).wait_recv()
            fwd = _remote(f3_hbm.at[:, t, c], f3_hbm.at[:, t, c], w3_s_, w3_r_, 3 + j, sib)
            fwd.start()
            passed.append(fwd)
        for j, k in enumerate(CHIP_FLIPS):
            t = _shard_of(_peer(pos, k))
            _remote(s3.at[:, c], f3_hbm.at[:, t, 1 - c], w3_s_, w3_r_, 3 + j, sib).wait_recv()
        for cp in sends + w3_sends + passed:
            cp.wait_send()
        for q in range(NQ):
            _remote(w_vm.at[1, q, mine, :], w_vm.at[3, q, mine, :], fw_s, fw_r, q, sib).wait_send()
        for cp in local:
            cp.wait()
        for slot in range(2):
            out_copy(slot, pl.ds(0, TM), abs_col(s, 0)).wait()

    n_loc = 2 + 5 * NQ
    return pl.pallas_call(
        body, name="ag_in_proj",
        in_specs=[VMEM_FULL, ANY, VMEM_FULL, VMEM_FULL, VMEM_FULL, VMEM_FULL, VMEM_FULL], out_specs=[ANY] * 5,
        out_shape=[jax.ShapeDtypeStruct((L, N_SHARD * Wc), BF16),
                   jax.ShapeDtypeStruct((L, Dh), BF16), jax.ShapeDtypeStruct((L, Dh), BF16),
                   jax.ShapeDtypeStruct((D, N_SHARD * Wc), BF16), jax.ShapeDtypeStruct((3, N_SHARD, 2, Do, D), BF16)],
        scratch_shapes=[pltpu.VMEM((N_SHARD, NQ, D, Wq), BF16), pltpu.VMEM((3, 2, Do, D), BF16), pltpu.VMEM((2, TM, Wq), BF16), pltpu.VMEM((TM, Dh), BF16),
                        pltpu.SemaphoreType.DMA((3 * NQ,)), pltpu.SemaphoreType.DMA((3 * NQ,)),
                        pltpu.SemaphoreType.DMA((3 * NQ,)), pltpu.SemaphoreType.DMA((3 * NQ,)),
                        pltpu.SemaphoreType.DMA((6,)), pltpu.SemaphoreType.DMA((6,)),
                        pltpu.SemaphoreType.DMA((NQ,)), pltpu.SemaphoreType.DMA((NQ,)),
                        pltpu.SemaphoreType.DMA((n_loc,)), pltpu.SemaphoreType.DMA((2,)), pltpu.SemaphoreType.DMA],
        compiler_params=_cparams(None, VMEM_LIMIT),
    )(xm, w_in_s, *w3_s, c2, s2)


def _pair_exchange_w3(dw3):
    _, _, _, Do, D = dw3.shape

    def build(ins, outs, send, recv):
        pos = _position()
        return [_remote(ins[0].at[:, :, 1 - pos[2]], outs[0], send, recv, 0, _peer(pos, 1))]

    return _Exchange((dw3,), (jax.ShapeDtypeStruct((3, N_SHARD, Do, D), F32),), 1, build)


def _sum_pair_in(dw_mine, ri):
    Dh, Wf = dw_mine.shape
    Wc = Wf // N_SHARD
    tr = min(256, Dh)

    def body(a_ref, b_ref, o_ref):
        o_ref[...] = (a_ref[...] + b_ref[...]).astype(BF16)

    return pl.pallas_call(
        body, name="sum_pair_in", grid=(Dh // tr, N_SHARD),
        in_specs=[pl.BlockSpec((tr, Wc), lambda i, t: (i, t)), pl.BlockSpec((tr, Wc), lambda i, t: (i, t))],
        out_specs=pl.BlockSpec((None, tr, Wc), lambda i, t: (t, i, 0)),
        out_shape=jax.ShapeDtypeStruct((N_SHARD, Dh, Wc), BF16),
        compiler_params=_cparams(("parallel", "parallel")),
    )(dw_mine, ri)


def _sum_pair_w3(cidx, dw3, r3):
    _, _, _, Do, D = dw3.shape

    def body(c_ref, a_ref, b_ref, o_ref):
        o_ref[...] = (a_ref[...] + b_ref[...]).astype(BF16)

    return pl.pallas_call(
        body, name="sum_pair_w3",
        grid_spec=pltpu.PrefetchScalarGridSpec(
            num_scalar_prefetch=1, grid=(3,),
            in_specs=[pl.BlockSpec((None, N_SHARD, None, Do, D), lambda a, c: (a, 0, c[0], 0, 0)),
                      pl.BlockSpec((None, N_SHARD, Do, D), lambda a, c: (a, 0, 0, 0))],
            out_specs=pl.BlockSpec((None, N_SHARD, Do, D), lambda a, c: (a, 0, 0, 0))),
        out_shape=jax.ShapeDtypeStruct((3, N_SHARD, Do, D), BF16),
        compiler_params=_cparams(("parallel",)),
    )(cidx, dw3, r3)


def _chips_exchange_in(cs_in):
    _, Dh, Wc = cs_in.shape

    def build(ins, outs, send, recv):
        pos = _position()
        return [_remote(ins[0].at[_shard_of(_peer(pos, k))], outs[0].at[j], send, recv, j, _peer(pos, k))
                for j, k in enumerate(CHIP_FLIPS)]

    return _Exchange((cs_in,), (jax.ShapeDtypeStruct((3, Dh, Wc), BF16),), 3, build)


def _chips_exchange_w3(cs_3):
    _, _, Do, D = cs_3.shape

    def build(ins, outs, send, recv):
        pos = _position()
        return [_remote(ins[0].at[:, _shard_of(_peer(pos, k))], outs[0].at[j], send, recv, j, _peer(pos, k))
                for j, k in enumerate(CHIP_FLIPS)]

    return _Exchange((cs_3,), (jax.ShapeDtypeStruct((3, 3, Do, D), BF16),), 3, build)


def _sum_chips_in(csidx, cs_in, rb_in):
    _, Dh, Wc = cs_in.shape
    tr = min(256, Dh)

    def body(s_ref, a_ref, b_ref, o_ref):
        acc = a_ref[...].astype(F32)
        for j in range(3):
            acc = acc + b_ref[j].astype(F32)
        o_ref[...] = acc

    return pl.pallas_call(
        body, name="sum_chips_in",
        grid_spec=pltpu.PrefetchScalarGridSpec(
            num_scalar_prefetch=1, grid=(Dh // tr,),
            in_specs=[pl.BlockSpec((None, tr, Wc), lambda i, s: (s[1], i, 0)),
                      pl.BlockSpec((3, tr, Wc), lambda i, s: (0, i, 0))],
            out_specs=pl.BlockSpec((None, tr, Wc), lambda i, s: (s[0], i, 0))),
        out_shape=jax.ShapeDtypeStruct((2, Dh, Wc), F32),
        compiler_params=_cparams(("parallel",)),
    )(csidx, cs_in, rb_in)


def _sum_chips_w3(csidx, cs_3, rb_3):
    _, _, Do, D = cs_3.shape

    def body(s_ref, a_ref, b_ref, o_ref):
        acc = a_ref[...].astype(F32)
        for j in range(3):
            acc = acc + b_ref[j].astype(F32)
        o_ref[...] = acc

    return pl.pallas_call(
        body, name="sum_chips_w3",
        grid_spec=pltpu.PrefetchScalarGridSpec(
            num_scalar_prefetch=1, grid=(3,),
            in_specs=[pl.BlockSpec((None, None, Do, D), lambda a, s: (a, s[1], 0, 0)),
                      pl.BlockSpec((3, None, Do, D), lambda a, s: (0, a, 0, 0))],
            out_specs=pl.BlockSpec((None, None, Do, D), lambda a, s: (a, s[0], 0, 0))),
        out_shape=jax.ShapeDtypeStruct((3, 2, Do, D), F32),
        compiler_params=_cparams(("parallel",)),
    )(csidx, cs_3, rb_3)


def _adam_math(w, g, m, v):
    m = ADAM_B1 * m + (1.0 - ADAM_B1) * g
    v = ADAM_B2 * v + (1.0 - ADAM_B2) * (g * g)
    m_hat = m / (1.0 - ADAM_B1 ** ADAM_STEP)
    v_hat = v / (1.0 - ADAM_B2 ** ADAM_STEP)
    delta = -ADAM_LR * (m_hat / (jnp.sqrt(v_hat) + ADAM_EPS) + ADAM_WD * w)
    return delta, m, v


def _adamw(w, g, m, v, name):
    R, C = w.shape
    tr = min(128, R)

    def body(w_ref, g_ref, m_ref, v_ref, d_ref, nm_ref, nv_ref):
        d_ref[...], nm_ref[...], nv_ref[...] = _adam_math(w_ref[...], g_ref[...], m_ref[...], v_ref[...])

    blk = pl.BlockSpec((tr, C), lambda i: (i, 0))
    return pl.pallas_call(
        body, name=name, grid=(R // tr,), in_specs=[blk] * 4, out_specs=[blk] * 3,
        out_shape=[jax.ShapeDtypeStruct((R, C), F32)] * 3,
        compiler_params=_cparams(("parallel",), VMEM_LIMIT),
    )(w, g, m, v)


def _adamw3(ws, g3, ms, vs):
    R, C = ws[0].shape

    def body(*refs):
        w_refs, m_refs, v_refs = refs[0:3], refs[3:6], refs[6:9]
        g_ref, outs = refs[9], refs[10:]
        for a in range(3):
            @pl.when(pl.program_id(0) == a)
            def _(a=a):
                res = _adam_math(w_refs[a][...], g_ref[...], m_refs[a][...], v_refs[a][...])
                for q in range(3):
                    outs[3 * a + q][...] = res[q]

    full = pl.BlockSpec((R, C), lambda a: (0, 0))
    res = pl.pallas_call(
        body, name="adamw_w3", grid=(3,),
        in_specs=[full] * 9 + [pl.BlockSpec((None, R, C), lambda a: (a, 0, 0))], out_specs=[full] * 9,
        out_shape=[jax.ShapeDtypeStruct((R, C), F32)] * 9,
        compiler_params=_cparams(("arbitrary",), VMEM_LIMIT),
    )(*ws, *ms, *vs, g3)
    return res[0:3], res[3:6], res[6:9]


SMALL_ROWS = ("c_ctx", "norm_w", "conv_b", "gn_w", "final_norm_w")


def _bwd_small(stats, ada_w, Dq, gh_in, gh_3):
    D = stats[0].shape[1]
    Wm = ada_w.shape[1]

    def body(stx, stm, stc, stv, stl, stlc, aw_ref, gi_in, g3_in, tot_ref, dm_sh, gcw, da_ref, loss_ref, gi_ref, g3_ref,
             vec_ref, vbuf, dm, amine, abuf, s_v, r_v, s_a, r_a, s_g, r_g):
        pos = _position()
        me, s = _dev_id(pos), _shard_of(pos)
        c, sib = pos[2], _peer(pos, 1)
        halves = [_remote(gi_in.at[c], gi_ref.at[c], s_g, r_g, 0, sib),
                  _remote(g3_in.at[:, c], g3_ref.at[:, c], s_g, r_g, 1, sib)]
        for cp in halves:
            cp.start()
        vec_ref[...] = jnp.zeros_like(vec_ref)
        vec_ref[0:2, :] = stx[0:2, :]
        vec_ref[2:3, :] = stm[1:2, :]
        vec_ref[3:5, :] = stc[0:2, :]
        vec_ref[5:6, :] = stx[2:3, :] + stc[2:3, :]
        vec_ref[6:7, :] = stv[3:4, :]
        vec_ref[7:8, :] = stm[3:4, :]
        vec_ref[8:9, :] = stm[0:1, :]
        vec_ref[9:12, :] = stv[0:3, :]
        vec_ref[12:14, 0:128] = stl[0:2, :] + stlc[0:2, :]
        vec_ref[14:15, :] = stm[2:3, :]
        vbuf[me] = vec_ref[...]
        sends = [_remote(vec_ref, vbuf.at[me], s_v, r_v, k - 1, _peer(pos, k)) for k in range(1, 8)]
        for cp in sends:
            cp.start()
        for k in range(1, 8):
            _remote(vec_ref, vbuf.at[_dev_id(_peer(pos, k))], s_v, r_v, k - 1, _peer(pos, k)).wait_recv()
        tot = vbuf[0]
        for d in range(1, N_DEV):
            tot = tot + vbuf[d]
        loss_ref[...] = jnp.zeros((8, 128), F32) + (0.5 / D) * _sum_all(tot[14:15, :])
        dm[...] = jnp.zeros_like(dm)
        for d in range(N_DEV):
            for r in range(3):
                dm[d:d + 1, r * D:(r + 1) * D] = vbuf[d, r:r + 1, :]
        dm[8:9, 0:D] = tot[3:4, :]
        dm[8:9, D:2 * D] = tot[4:5, :]
        for t in range(N_SHARD):
            @pl.when(s == t)
            def _(t=t):
                dm_sh[...] = dm[:, t * Wm:(t + 1) * Wm]
                gcw[...] = tot[9:12, t * Dq:(t + 1) * Dq]
        tot_ref[...] = tot
        part = lax.dot_general(dm_sh[8:16, :], aw_ref[...], (((1,), (1,)), ((), ())),
                               precision=lax.Precision.HIGHEST, preferred_element_type=F32)
        amine[...] = part
        abuf[s] = part
        asend = [_remote(amine, abuf.at[s], s_a, r_a, j, _peer(pos, k)) for j, k in enumerate(CHIP_FLIPS)]
        for cp in asend:
            cp.start()
        for j, k in enumerate(CHIP_FLIPS):
            _remote(amine, abuf.at[_shard_of(_peer(pos, k))], s_a, r_a, j, _peer(pos, k)).wait_recv()
        da = abuf[0]
        for t in range(1, N_SHARD):
            da = da + abuf[t]
        da_ref[...] = da
        _remote(gi_in.at[1 - c], gi_ref.at[1 - c], s_g, r_g, 0, sib).wait_recv()
        _remote(g3_in.at[:, 1 - c], g3_ref.at[:, 1 - c], s_g, r_g, 1, sib).wait_recv()
        for cp in sends + asend + halves:
            cp.wait_send()

    row = lambda *shape: jax.ShapeDtypeStruct(shape, F32)
    return pl.pallas_call(
        body, name="bwd_small",
        in_specs=[VMEM_FULL] * 7 + [ANY, ANY], out_specs=[VMEM_FULL] * 5 + [ANY, ANY],
        input_output_aliases={7: 5, 8: 6},
        out_shape=[row(16, D), row(16, Wm), row(3, Dq), row(8, D), row(8, 128), row(*gh_in.shape), row(*gh_3.shape)],
        scratch_shapes=[pltpu.VMEM((16, D), F32), pltpu.VMEM((N_DEV, 16, D), F32), pltpu.VMEM((16, 3 * D), F32),
                        pltpu.VMEM((8, D), F32), pltpu.VMEM((N_SHARD, 8, D), F32),
                        pltpu.SemaphoreType.DMA((7,)), pltpu.SemaphoreType.DMA((7,)),
                        pltpu.SemaphoreType.DMA((3,)), pltpu.SemaphoreType.DMA((3,)),
                        pltpu.SemaphoreType.DMA((2,)), pltpu.SemaphoreType.DMA((2,))],
        compiler_params=_cparams(None, VMEM_LIMIT),
    )(*stats, ada_w, gh_in, gh_3)


def _small_update(tot, dm_sh, gcw, da, act, p_row, p_ab, p_cw, p_dl):
    D = act.shape[1]
    Wm = dm_sh.shape[1]
    Dq = gcw.shape[1]

    def body(tot_ref, dm_ref, gcw_ref, da_ref, act_ref, prow, pab, pcw, pdl, gaw_ref, *outs):
        o_q = [outs[8 * q:8 * (q + 1)] for q in range(4)]
        tot = tot_ref[...]
        gaw_ref[...] = lax.dot_general(act_ref[...], dm_ref[...], (((0,), (0,)), ((), ())),
                                       precision=lax.Precision.HIGHEST, preferred_element_type=F32)
        cc = prow[0, 0:1, :]
        sg = _sigmoid(cc)
        g_cctx = da_ref[0:1, :] * (sg * (1.0 + cc * (1.0 - sg)))

        def place_all(o, val):
            o[...] = val

        def emit(k, w, g, m, v, place=place_all):
            for q, val in enumerate((g,) + _adam_math(w, g, m, v)):
                place(o_q[q][k], val)

        g_rows = [g_cctx, tot[5:6, :], tot[6:7, :], tot[7:8, :], tot[8:9, :]]
        for k, g in enumerate(g_rows):
            emit(k, prow[0, k:k + 1, :], g, prow[1, k:k + 1, :], prow[2, k:k + 1, :])

        def place_ab(o, val):
            for r in range(3):
                o[0:1, r * D:(r + 1) * D] = val[r:r + 1, :]

        g_ab = jnp.concatenate([tot[0:1, :] + tot[3:4, :], tot[1:2, :] + tot[4:5, :], tot[2:3, :]], axis=0)
        emit(5, pab[0], g_ab, pab[1], pab[2], place_ab)
        emit(6, pcw[0], gcw_ref[...], pcw[1], pcw[2])
        g_dl = jnp.concatenate([tot[12:14, 0:128] * _sigmoid(-pdl[0, 0:2, :]), jnp.zeros((6, 128), F32)], axis=0)
        emit(7, pdl[0], g_dl, pdl[1], pdl[2])

    row = lambda *shape: jax.ShapeDtypeStruct(shape, F32)
    per_q = [row(1, D)] * 5 + [row(1, 3 * D), row(3, Dq), row(8, 128)]
    res = pl.pallas_call(
        body, name="small_update",
        in_specs=[VMEM_FULL] * 9, out_specs=[VMEM_FULL] * 33,
        out_shape=[row(D, Wm)] + per_q * 4,
        compiler_params=_cparams(None, VMEM_LIMIT),
    )(tot, dm_sh, gcw, da, act, p_row, p_ab, p_cw, p_dl)
    return res[0], [res[1 + 8 * q:1 + 8 * (q + 1)] for q in range(4)]


def _pad_rows(a, rows=8):
    return jnp.pad(a, ((0, rows - a.shape[0]), (0, 0)))


def kernel(x, c, ctx, c_ctx, norm_w, ada_w, ada_b, w_in, conv_w, conv_b, decay_logit, gn_w, w_a, w_b, w_out, final_norm_w, loss_target, m_c_ctx, m_norm_w, m_ada_w, m_ada_b, m_w_in, m_conv_w, m_conv_b, m_decay_logit, m_gn_w, m_w_a, m_w_b, m_w_out, m_final_norm_w, v_c_ctx, v_norm_w, v_ada_w, v_ada_b, v_w_in, v_conv_w, v_conv_b, v_decay_logit, v_gn_w, v_w_a, v_w_b, v_w_out, v_final_norm_w):
    L, D = x.shape[1], x.shape[2]
    H = D // DV
    Wc = w_in.shape[2]
    Do = D // 8
    pos = _position()
    me = _dev_id(pos)
    cidx = jnp.reshape(pos[2], (1,)).astype(jnp.int32)
    sidx = jnp.reshape(_shard_of(pos), (1,)).astype(jnp.int32)

    act, mod, conv_w8 = _fwd_small(_pad_rows(c), _pad_rows(c_ctx[None]), ada_w[0], ada_b, _pad_rows(conv_w[0]))
    mod_x = lax.dynamic_slice_in_dim(mod, me, 1, axis=0).reshape(3, D)
    mod_c = mod[8].reshape(3, D)
    lg = jax.nn.log_sigmoid(decay_logit[0])

    w3_s = tuple(w[0].reshape(2, Do, D) for w in (w_a, w_b, w_out))

    def project(xm, c2, s2, w_in_bf):
        p, qr, kr, w_in_full, w3_full = _ag_in_proj(xm, w_in_bf, w3_s, c2, s2)
        return p, qr, kr, w_in_full, w3_full.reshape(3, D, D)

    csidx = jnp.concatenate([cidx, sidx])
    groups, dret_c, xmt, cmt, dx1, sc_x, w_in_full, gh_3, sts = _local_step(
        x[0], ctx[0], loss_target[0], mod_x, mod_c, norm_w, conv_w8, conv_b, lg, gn_w, final_norm_w[None],
        project, csidx, w_in[0])
    st_mid, st_conv, st_lg, st_lgc, st_c = sts

    dw_mine, ra_in = _dw_in(xmt, groups, cmt, dret_c, D, True)
    cs_in = _sum_pair_in(dw_mine, ra_in)
    grad_x, st_x, rb_in = _dxm(groups, 0, w_in_full, x[0], norm_w, sc_x, dx1, "dxm_x", _chips_exchange_in(cs_in))
    gh_in = _sum_chips_in(csidx, cs_in, rb_in)

    zeros3 = jnp.zeros((3, D), F32)
    p_row = jnp.concatenate(
        [r for t in ((c_ctx[None], norm_w, conv_b, gn_w, final_norm_w[None], zeros3),
                     (m_c_ctx[None], m_norm_w, m_conv_b, m_gn_w, m_final_norm_w[None], zeros3),
                     (v_c_ctx[None], v_norm_w, v_conv_b, v_gn_w, v_final_norm_w[None], zeros3)) for r in t],
        axis=0).reshape(3, 8, D)
    p_ab = jnp.concatenate([ada_b, m_ada_b, v_ada_b], axis=0).reshape(3, 3, D)
    p_cw = jnp.concatenate([conv_w, m_conv_w, v_conv_w], axis=0)
    p_dl = jnp.pad(jnp.concatenate([decay_logit, m_decay_logit, v_decay_logit], axis=0), ((0, 0), (0, 6), (0, 128 - H)))
    tot, dm_sh, gcw, da, loss_t, g_in, g_3 = _bwd_small((st_x, st_mid, st_c, st_conv, st_lg, st_lgc), ada_w[0],
                                                        conv_w.shape[2], gh_in, gh_3)
    g_w_in = g_in.reshape(D, Wc)
    g_3 = g_3.reshape(3, D // 4, D)
    g_ada_w, small = _small_update(tot, dm_sh, gcw, da, act, p_row, p_ab, p_cw, p_dl)

    upd_in = _adamw(w_in[0], g_w_in, m_w_in[0], v_w_in[0], "adamw_w_in")
    upd_ada = _adamw(ada_w[0], g_ada_w, m_ada_w[0], v_ada_w[0], "adamw_ada_w")
    upd_a, upd_b, upd_o = _adamw3((w_a[0], w_b[0], w_out[0]), g_3, (m_w_a[0], m_w_b[0], m_w_out[0]),
                                  (v_w_a[0], v_w_b[0], v_w_out[0]))

    def leaves(q):
        big = lambda g, upd: (g if q == 0 else upd[q - 1])[None]
        r_cctx, r_norm, r_convb, r_gn, r_fnorm, r_ab, r_cw, r_dl = small[q]
        return [r_cctx.reshape(D), r_norm, big(g_ada_w, upd_ada), r_ab, big(g_w_in, upd_in),
                r_cw[None], r_convb, r_dl[0:2, 0:H][None], r_gn,
                big(g_3[0], upd_a), big(g_3[1], upd_b), big(g_3[2], upd_o), r_fnorm.reshape(D)]

    loss = loss_t[0, 0]
    return (loss, grad_x[None], *leaves(0), *leaves(1), *leaves(2), *leaves(3))
```

```python
from typing import Callable, NamedTuple

import jax
import jax.numpy as jnp
from jax import lax
from jax.experimental import pallas as pl
from jax.experimental.pallas import tpu as pltpu

F32 = jnp.float32
BF16 = jnp.bfloat16
MESH = pl.DeviceIdType.MESH

CHUNK = 128
RET_CPB = 4
DV = 128
DK = 64
GRID_W = 64
ROPE_BASE = 10000.0
EPS = 1e-6
K_SCALE = DK ** -0.5
N_SHARD = 4
N_DEV = 8

ADAM_LR = 0.001
ADAM_B1 = 0.9
ADAM_B2 = 0.999
ADAM_EPS = 1e-08
ADAM_WD = 0.01
ADAM_STEP = 10

VMEM_LIMIT = 56 * 1024 * 1024


def _cparams(sem=None, vmem=None):
    kw = {}
    if sem is not None:
        kw["dimension_semantics"] = sem
    if vmem is not None:
        kw["vmem_limit_bytes"] = vmem
    return pltpu.CompilerParams(**kw)


def _dot(a, b):
    return jnp.dot(a, b, preferred_element_type=F32)


def _dot_nt(a, b):
    return lax.dot_general(a, b, (((1,), (1,)), ((), ())), preferred_element_type=F32)


def _dot_tn(a, b):
    return lax.dot_general(a, b, (((0,), (0,)), ((), ())), preferred_element_type=F32)


def _sigmoid(x):
    return 1.0 / (1.0 + jnp.exp(-x))


def _sum_all(x):
    return jnp.sum(jnp.sum(x, axis=1, keepdims=True), axis=0, keepdims=True)


def _swap_halves(t):
    n = t.shape[1]
    lane = lax.broadcasted_iota(jnp.int32, t.shape, 1)
    low = (lane & 32) == 0
    return jnp.where(low, pltpu.roll(t, n - 32, 1), pltpu.roll(t, 32, 1))


def _vec_spec(d):
    return pl.BlockSpec((1, d), lambda *a: (0, 0))


def _norm_mod(x, nw, sc, sh, name, also_bf16=None):
    L, D = x.shape
    tl = min(256, L)
    nt = L // tl

    def body(x_ref, nw_ref, sc_ref, sh_ref, *rest):
        xm_ref, xmt_ref = rest[-3:-1] if also_bf16 is not None else rest
        xv = x_ref[...]
        r = lax.rsqrt(jnp.mean(xv * xv, axis=-1, keepdims=True) + EPS)
        xm = (xv * r * nw_ref[...]) * (1.0 + sc_ref[...]) + sh_ref[...]
        xm_b = xm.astype(BF16)
        xm_ref[...] = xm_b
        xmt_ref[...] = xm_b.T
        if also_bf16 is not None:
            rest[-1][...] = rest[0][...].astype(BF16)

    in_specs = [pl.BlockSpec((tl, D), lambda i: (i, 0)), _vec_spec(D), _vec_spec(D), _vec_spec(D)]
    out_specs = [pl.BlockSpec((tl, D), lambda i: (i, 0)), pl.BlockSpec((D, tl), lambda i: (0, i))]
    out_shape = [jax.ShapeDtypeStruct((L, D), BF16), jax.ShapeDtypeStruct((D, L), BF16)]
    args = [x, nw, sc, sh]
    if also_bf16 is not None:
        R, C = also_bf16.shape
        slab = pl.BlockSpec((R // nt, C), lambda i: (i, 0))
        in_specs.append(slab)
        out_specs.append(slab)
        out_shape.append(jax.ShapeDtypeStruct((R, C), BF16))
        args.append(also_bf16)
    return pl.pallas_call(
        body, name=name, grid=(nt,), in_specs=in_specs, out_specs=out_specs, out_shape=out_shape,
        compiler_params=_cparams(("parallel",)),
    )(*args)


QK_BLOCK, V_BLOCK = 4, 5


def _in_proj(xm, w, name, first=0, count=None):
    M, D = xm.shape
    count = w.shape[1] // D if count is None else count
    tm = min(1024, M)

    def body(a_ref, b_ref, o_ref, qk_ref):
        acc = _dot(a_ref[...], b_ref[...])
        o_ref[...] = acc.astype(o_ref.dtype)

        @pl.when(pl.program_id(1) == QK_BLOCK - first)
        def _():
            qk_ref[...] = acc

    return pl.pallas_call(
        body, name=name, grid=(M // tm, count),
        in_specs=[pl.BlockSpec((tm, D), lambda i, j: (i, 0)), pl.BlockSpec((D, D), lambda i, j: (0, first + j))],
        out_specs=[pl.BlockSpec((tm, D), lambda i, j: (i, j)), pl.BlockSpec((tm, D), lambda i, j: (i, 0))],
        out_shape=[jax.ShapeDtypeStruct((M, count * D), BF16), jax.ShapeDtypeStruct((M, D), F32)],
        compiler_params=_cparams(("parallel", "arbitrary")),
    )(xm, w)


def _halo_specs(tl, L, D, col):
    hb = tl // 16
    last = L // 16 - 1
    prev = pl.BlockSpec((16, D), lambda i: (jnp.maximum(i * hb - 1, 0), col))
    nxt = pl.BlockSpec((16, D), lambda i: (jnp.minimum((i + 1) * hb, last), col))
    return prev, nxt


def _shift_rows(u, above, below):
    tl = u.shape[0]
    row = lax.broadcasted_iota(jnp.int32, u.shape, 0)
    dn = jnp.where(row == 0, above, pltpu.roll(u, 1, 0))
    up = jnp.where(row == tl - 1, below, pltpu.roll(u, tl - 1, 0))
    return dn, up


def _rope_tables(L):
    pos = jnp.arange(L)
    row = (pos // GRID_W).astype(F32)
    col = (pos % GRID_W).astype(F32)
    nf = DK // 4
    inv = ROPE_BASE ** (-jnp.arange(nf, dtype=F32) / nf)
    ang = jnp.concatenate([row[:, None] * inv, col[:, None] * inv], axis=-1)
    cos, sin = jnp.cos(ang), jnp.sin(ang)
    return jnp.concatenate([cos, cos, cos, cos], axis=-1), jnp.concatenate([-sin, sin, -sin, sin], axis=-1)


def _smem_spec():
    return pl.BlockSpec(memory_space=pltpu.SMEM)


def _pair_select(e0, e1):
    row = lax.broadcasted_iota(jnp.int32, e0.shape, 0)
    return jnp.where(row < DK, e0, e1)


def _head_lane_mask(shape, e):
    lane = lax.broadcasted_iota(jnp.int32, shape, 1)
    return (lane < DK) if e == 0 else (lane >= DK)


def _ctx_states(pc, pqk_c, lg, D):
    Lc = pc.shape[0]
    H = D // DV

    def body(lg_ref, k_ref, v_ref, s_ref):
        m = lax.broadcasted_iota(jnp.int32, (Lc, DV), 0).astype(F32)
        for pr in range(H // 2):
            k2 = k_ref[:, pr * 128:(pr + 1) * 128].astype(F32) * K_SCALE
            res = [[None, None], [None, None]]
            for e in range(2):
                h = 2 * pr + e
                v = v_ref[:, h * DV:(h + 1) * DV]
                dec_f = jnp.exp(lg_ref[0, h] * (Lc - 1.0 - m))
                dec_b = jnp.exp(lg_ref[1, h] * m)
                res[0][e] = _dot_tn((k2 * dec_f).astype(BF16), v)
                res[1][e] = _dot_tn((k2 * dec_b).astype(BF16), v)
            s_ref[0, pr] = _pair_select(res[0][0], res[0][1])
            s_ref[1, pr] = _pair_select(res[1][0], res[1][1])

    return pl.pallas_call(
        body, name="ctx_states", grid=(1,),
        in_specs=[_smem_spec(), pl.BlockSpec((Lc, D // 2), lambda i: (0, 1)), pl.BlockSpec((Lc, D), lambda i: (0, 1))],
        out_specs=pl.BlockSpec((2, H // 2, 128, 128), lambda i: (0, 0, 0, 0)),
        out_shape=jax.ShapeDtypeStruct((2, H // 2, 128, 128), F32),
    )(lg, pqk_c, pc)


T_M, T_MT = 0, 1
T_MF1, T_MB1 = 2, 3
T_QF, T_QB = 4, 5
T_KF, T_KB = 6, 7


def _decay_tables(lg, H):
    def body(lg_ref, t_ref):
        h = pl.program_id(0)
        lgf, lgb = lg_ref[0, h], lg_ref[1, h]
        i = lax.broadcasted_iota(jnp.int32, (CHUNK, CHUNK), 0).astype(F32)
        j = lax.broadcasted_iota(jnp.int32, (CHUNK, CHUNK), 1).astype(F32)
        d = i - j
        mf = jnp.where(d > 0, jnp.exp(lgf * jnp.maximum(d, 0.0)), 0.0)
        mb = jnp.where(d < 0, jnp.exp(lgb * jnp.maximum(-d, 0.0)), 0.0)
        mf_t = jnp.where(d < 0, jnp.exp(lgf * jnp.maximum(-d, 0.0)), 0.0)
        mb_t = jnp.where(d > 0, jnp.exp(lgb * jnp.maximum(d, 0.0)), 0.0)
        diag = jnp.where(d == 0, 2.0, 0.0)
        t_ref[0, T_M] = mf + mb + diag
        t_ref[0, T_MT] = mf_t + mb_t + diag
        t_ref[0, T_MF1] = mf * d
        t_ref[0, T_MB1] = mb * (-d)
        t_ref[0, T_QF] = jnp.exp(lgf * (i + 1.0))
        t_ref[0, T_QB] = jnp.exp(lgb * (CHUNK - i))
        t_ref[0, T_KF] = jnp.exp(lgf * (CHUNK - 1.0 - i))
        t_ref[0, T_KB] = jnp.exp(lgb * i)

    return pl.pallas_call(
        body, name="decay_tables", grid=(H,), in_specs=[_smem_spec()],
        out_specs=pl.BlockSpec((1, 8, CHUNK, CHUNK), lambda h: (h, 0, 0, 0)),
        out_shape=jax.ShapeDtypeStruct((H, 8, CHUNK, CHUNK), F32),
    )(lg)


def _tab_spec(H):
    return pl.BlockSpec((H, 8, CHUNK, CHUNK), lambda n: (0, 0, 0, 0))


def _chunk_decay(tab_ref, h):
    return tab_ref[h, T_QF, CHUNK - 1:CHUNK, :], tab_ref[h, T_QB, 0:1, :]


def _ret_states(kr, p, s0, tab, D):
    L = kr.shape[0]
    H = D // DV
    N = L // CHUNK
    HP = H // 2

    def body(tab_ref, kf_ref, kb_ref, vf_ref, vb_ref, s0_ref, sf_out, sb_out, sf, sb):
        n = pl.program_id(0)

        @pl.when(n == 0)
        def _():
            sf[...] = s0_ref[0]
            sb[...] = s0_ref[1]

        for cc in range(RET_CPB):
            cf_, cb_ = cc, RET_CPB - 1 - cc
            rf, rb = slice(cf_ * CHUNK, (cf_ + 1) * CHUNK), slice(cb_ * CHUNK, (cb_ + 1) * CHUNK)
            sf_out[cf_] = sf[...]
            sb_out[cb_] = sb[...]
            for pr in range(HP):
                kf2 = kf_ref[rf, pr * 128:(pr + 1) * 128].astype(F32)
                kb2 = kb_ref[rb, pr * 128:(pr + 1) * 128].astype(F32)
                inc_f, inc_b, gf, gb = [], [], [], []
                for e in range(2):
                    h = 2 * pr + e
                    inc_f.append(_dot_tn((kf2 * tab_ref[h, T_KF]).astype(BF16), vf_ref[rf, h * DV:(h + 1) * DV]))
                    inc_b.append(_dot_tn((kb2 * tab_ref[h, T_KB]).astype(BF16), vb_ref[rb, h * DV:(h + 1) * DV]))
                    cf, cb = _chunk_decay(tab_ref, h)
                    gf.append(jnp.broadcast_to(cf, (128, 128)))
                    gb.append(jnp.broadcast_to(cb, (128, 128)))
                sf[pr] = _pair_select(gf[0], gf[1]) * sf[pr] + _pair_select(inc_f[0], inc_f[1])
                sb[pr] = _pair_select(gb[0], gb[1]) * sb[pr] + _pair_select(inc_b[0], inc_b[1])

    st = jax.ShapeDtypeStruct((N, HP, 128, 128), F32)
    R = RET_CPB * CHUNK
    NB = N // RET_CPB
    return _riding_call(
        body, None, NB, name="ret_states", args=(tab, kr, kr, p, p, s0),
        in_specs=[_tab_spec(H),
                  pl.BlockSpec((R, D // 2), lambda n: (n, 0)),
                  pl.BlockSpec((R, D // 2), lambda n: (NB - 1 - n, 0)),
                  pl.BlockSpec((R, D), lambda n: (n, 5)),
                  pl.BlockSpec((R, D), lambda n: (NB - 1 - n, 5)),
                  pl.BlockSpec((2, HP, 128, 128), lambda n: (0, 0, 0, 0))],
        out_specs=[pl.BlockSpec((RET_CPB, HP, 128, 128), lambda n: (n, 0, 0, 0)),
                   pl.BlockSpec((RET_CPB, HP, 128, 128), lambda n: (NB - 1 - n, 0, 0, 0))],
        out_shape=[st, st],
        scratch=[pltpu.VMEM((HP, 128, 128), F32), pltpu.VMEM((HP, 128, 128), F32)],
        cparams=_cparams(("arbitrary",)))


def _ret_out(qr, kr, p, sf_prev, sb_prev, gn_w, tab, D):
    L = qr.shape[0]
    H = D // DV
    N = L // CHUNK
    HP = H // 2

    def body(tab_ref, q_ref, k_ref, v_ref, zb_ref, sf_ref, sb_ref, gn_ref, o_ref, yb_ref):
        def chunk(cc, carry):
            rows = pl.ds(pl.multiple_of(cc * CHUNK, CHUNK), CHUNK)
            for pr in range(HP):
                q2 = q_ref[rows, pr * 128:(pr + 1) * 128]
                k2 = k_ref[rows, pr * 128:(pr + 1) * 128]
                sfp = sf_ref[cc, pr].astype(BF16)
                sbp = sb_ref[cc, pr].astype(BF16)
                for e in range(2):
                    h = 2 * pr + e
                    sl = slice(h * DV, (h + 1) * DV)
                    qm = jnp.where(_head_lane_mask(q2.shape, e), q2, jnp.zeros_like(q2))
                    a = (_dot_nt(qm, k2) * tab_ref[h, T_M]).astype(BF16)
                    qf = qm.astype(F32)
                    o = _dot(a, v_ref[rows, sl])
                    o += _dot((qf * tab_ref[h, T_QF]).astype(BF16), sfp)
                    o += _dot((qf * tab_ref[h, T_QB]).astype(BF16), sbp)
                    o_ref[rows, sl] = o
                    mu = jnp.mean(o, axis=-1, keepdims=True)
                    oc = o - mu
                    rstd = lax.rsqrt(jnp.mean(oc * oc, axis=-1, keepdims=True) + EPS)
                    zb = zb_ref[rows, sl].astype(F32)
                    yb_ref[rows, sl] = (zb * _sigmoid(zb) * (oc * rstd * gn_ref[:, sl])).astype(BF16)
            return carry

        lax.fori_loop(0, RET_CPB, chunk, 0)

    R = RET_CPB * CHUNK
    return _riding_call(
        body, None, N // RET_CPB, name="ret_out", args=(tab, qr, kr, p, p, sf_prev, sb_prev, gn_w),
        in_specs=[_tab_spec(H),
                  pl.BlockSpec((R, D // 2), lambda n: (n, 0)),
                  pl.BlockSpec((R, D // 2), lambda n: (n, 0)),
                  pl.BlockSpec((R, D), lambda n: (n, 5)),
                  pl.BlockSpec((R, D), lambda n: (n, 6)),
                  pl.BlockSpec((RET_CPB, HP, 128, 128), lambda n: (n, 0, 0, 0)),
                  pl.BlockSpec((RET_CPB, HP, 128, 128), lambda n: (n, 0, 0, 0)),
                  _vec_spec(D)],
        out_specs=[pl.BlockSpec((R, D), lambda n: (n, 0)), pl.BlockSpec((R, D), lambda n: (n, 0))],
        out_shape=[jax.ShapeDtypeStruct((L, D), F32), jax.ShapeDtypeStruct((L, D), BF16)],
        cparams=_cparams(("arbitrary",)))


def _mid(p, yb, o, x, tgt, w3, g, fw, conv_w, conv_b, gn_w, D):
    L = x.shape[0]
    H = D // DV
    tm = min(256, L)
    nt = L // tm

    def body(h_ref, bg_ref, cg_ref, za_ref, hp_ref, hn_ref, cp_ref, cn_ref, yb_ref, ga_ref, gb_ref, zb_ref, o_ref,
             x_ref, t_ref, w_hbm, g_ref, fw_ref, cw_ref, cb_ref, gn_ref,
             dx1_ref, dya_ref, do_ref, dzb_ref, dgab_ref, dw_hbm, st_ref, w_vm, dw_acc, sem):
        i = pl.program_id(0)

        @pl.when(i == 0)
        def _():
            cp = pltpu.make_async_copy(w_hbm, w_vm, sem)
            cp.start()
            dw_acc[...] = jnp.zeros_like(dw_acc)
            st_ref[...] = jnp.zeros_like(st_ref)
            cp.wait()

        u = cg_ref[...].astype(F32) * h_ref[...].astype(F32)
        above = jnp.where(i == 0, 0.0, cp_ref[15:16, :].astype(F32) * hp_ref[15:16, :].astype(F32))
        below = jnp.where(i == nt - 1, 0.0, cn_ref[0:1, :].astype(F32) * hn_ref[0:1, :].astype(F32))
        dn, up = _shift_rows(u, above, below)
        co = cw_ref[0:1, :] * dn + cw_ref[1:2, :] * u + cw_ref[2:3, :] * up + cb_ref[...]
        za = za_ref[...].astype(F32)
        ya_b = (za * _sigmoid(za) * bg_ref[...].astype(F32) * co).astype(BF16)
        yb_b = yb_ref[...]
        y_a = _dot(ya_b, w_vm[0])
        y_b = _dot(yb_b, w_vm[1])
        sga = _sigmoid(ga_ref[...].astype(F32))
        sgb = _sigmoid(gb_ref[...].astype(F32))
        mix_b = (sga * y_a + sgb * y_b).astype(BF16)
        y_x = _dot(mix_b, w_vm[2])
        gvec, fwv = g_ref[...], fw_ref[...]
        x1 = x_ref[...] + gvec * y_x
        r1 = lax.rsqrt(jnp.mean(x1 * x1, axis=-1, keepdims=True) + EPS)
        xh = x1 * r1
        diff = xh * fwv - t_ref[...]
        dout = diff * (1.0 / D)
        dxh = dout * fwv
        dx1 = r1 * (dxh - xh * jnp.mean(dxh * xh, axis=-1, keepdims=True))
        dx1_ref[...] = dx1
        st_ref[0:1, :] += jnp.sum(dout * xh, axis=0, keepdims=True)
        st_ref[1:2, :] += jnp.sum(dx1 * y_x, axis=0, keepdims=True)
        st_ref[2:3, :] += jnp.sum(diff * diff, axis=0, keepdims=True)
        dyx_b = (dx1 * gvec).astype(BF16)
        dmix = _dot_nt(dyx_b, w_vm[2])
        dw_acc[2] += _dot_tn(mix_b, dyx_b)
        dya_b = (dmix * sga).astype(BF16)
        dyb_b = (dmix * sgb).astype(BF16)
        dgab_ref[:, 0:D] = (dmix * y_a * sga * (1.0 - sga)).astype(BF16)
        dgab_ref[:, D:2 * D] = (dmix * y_b * sgb * (1.0 - sgb)).astype(BF16)
        dya_ref[...] = _dot_nt(dya_b, w_vm[0])
        dyb = _dot_nt(dyb_b, w_vm[1])
        dw_acc[0] += _dot_tn(ya_b, dya_b)
        dw_acc[1] += _dot_tn(yb_b, dyb_b)

        for h in range(H):
            sl = slice(h * DV, (h + 1) * DV)
            ov = o_ref[:, sl]
            oc = ov - jnp.mean(ov, axis=-1, keepdims=True)
            rstd = lax.rsqrt(jnp.mean(oc * oc, axis=-1, keepdims=True) + EPS)
            rn = oc * rstd
            gw = gn_ref[:, sl]
            zb = zb_ref[:, sl].astype(F32)
            sz = _sigmoid(zb)
            dy = dyb[:, sl]
            dzb_ref[:, sl] = (dy * (rn * gw) * (sz * (1.0 + zb * (1.0 - sz)))).astype(BF16)
            dretn = dy * (zb * sz)
            st_ref[3:4, sl] += jnp.sum(dretn * rn, axis=0, keepdims=True)
            drn = dretn * gw
            do_ref[:, sl] = (rstd * (drn - jnp.mean(drn, axis=-1, keepdims=True)
                                     - rn * jnp.mean(drn * rn, axis=-1, keepdims=True))).astype(BF16)

        @pl.when(i == nt - 1)
        def _():
            out = pltpu.make_async_copy(dw_acc, dw_hbm, sem)
            out.start()
            out.wait()

    row = lambda col: pl.BlockSpec((tm, D), lambda i: (i, col))
    any_spec = pl.BlockSpec(memory_space=pl.ANY)
    f32o = jax.ShapeDtypeStruct((L, D), F32)
    bf16o = jax.ShapeDtypeStruct((L, D), BF16)
    hp, hn = _halo_specs(tm, L, D, 0)
    cp, cn = _halo_specs(tm, L, D, 2)
    return pl.pallas_call(
        body, name="mid", grid=(nt,),
        in_specs=[row(0), row(1), row(2), row(3), hp, hn, cp, cn, row(0), row(7), row(8), row(6), row(0),
                  row(0), row(0), any_spec, _vec_spec(D), _vec_spec(D),
                  pl.BlockSpec((8, D), lambda i: (0, 0)), _vec_spec(D), _vec_spec(D)],
        out_specs=[row(0), row(0), row(0), row(0), pl.BlockSpec((tm, 2 * D), lambda i: (i, 0)), any_spec,
                   pl.BlockSpec((8, D), lambda i: (0, 0))],
        out_shape=[f32o, f32o, bf16o, bf16o, jax.ShapeDtypeStruct((L, 2 * D), BF16),
                   jax.ShapeDtypeStruct((3, D, D), F32), jax.ShapeDtypeStruct((8, D), F32)],
        scratch_shapes=[pltpu.VMEM((3, D, D), BF16), pltpu.VMEM((3, D, D), F32), pltpu.SemaphoreType.DMA],
        compiler_params=_cparams(("arbitrary",), VMEM_LIMIT),
    )(p, p, p, p, p, p, p, p, yb, p, p, p, o, x, tgt, w3, g, fw, conv_w, conv_b, gn_w)


def _conv_bwd(dya, p, conv_w, conv_b, D, exchange=None):
    L = p.shape[0]
    tl = min(256, L)
    nt = L // tl

    def body(d_ref, h_ref, bg_ref, cg_ref, za_ref,
             dp_ref, dn_ref, hp_ref, hn_ref, bp_ref, bn_ref, cp_ref, cn_ref, zp_ref, zn_ref,
             w_ref, b_ref, dc_ref, st_ref):
        i = pl.program_id(0)

        @pl.when(i == 0)
        def _():
            st_ref[...] = jnp.zeros_like(st_ref)

        first, last = i == 0, i == nt - 1
        h = h_ref[...].astype(F32)
        cg = cg_ref[...].astype(F32)
        bg = bg_ref[...].astype(F32)
        za = za_ref[...].astype(F32)
        dy = d_ref[...].astype(F32)
        u = cg * h
        u_above = jnp.where(first, 0.0, cp_ref[15:16, :].astype(F32) * hp_ref[15:16, :].astype(F32))
        u_below = jnp.where(last, 0.0, cn_ref[0:1, :].astype(F32) * hn_ref[0:1, :].astype(F32))
        u_dn, u_up = _shift_rows(u, u_above, u_below)
        w0, w1, w2 = w_ref[0:1, :], w_ref[1:2, :], w_ref[2:3, :]
        co = w0 * u_dn + w1 * u + w2 * u_up + b_ref[...]
        sz = _sigmoid(za)
        silu = za * sz
        dc_ref[:, 3 * D:4 * D] = (dy * bg * co * (sz * (1.0 + za * (1.0 - sz)))).astype(BF16)
        dc_ref[:, D:2 * D] = (dy * silu * co).astype(BF16)
        dco = dy * silu * bg

        def edge(dr, zr, br, r):
            z = zr[r:r + 1, :].astype(F32)
            return dr[r:r + 1, :].astype(F32) * (z * _sigmoid(z)) * br[r:r + 1, :].astype(F32)

        dco_above = jnp.where(first, 0.0, edge(dp_ref, zp_ref, bp_ref, 15))
        dco_below = jnp.where(last, 0.0, edge(dn_ref, zn_ref, bn_ref, 0))
        dco_dn, dco_up = _shift_rows(dco, dco_above, dco_below)
        du = w0 * dco_up + w1 * dco + w2 * dco_dn
        dc_ref[:, 2 * D:3 * D] = (du * h).astype(BF16)
        dc_ref[:, 0:D] = (du * cg).astype(BF16)
        st_ref[0:1, :] += jnp.sum(dco * u_dn, axis=0, keepdims=True)
        st_ref[1:2, :] += jnp.sum(dco * u, axis=0, keepdims=True)
        st_ref[2:3, :] += jnp.sum(dco * u_up, axis=0, keepdims=True)
        st_ref[3:4, :] += jnp.sum(dco, axis=0, keepdims=True)

    main = lambda col: pl.BlockSpec((tl, D), lambda i: (i, col))
    halos = []
    for col in (0, 0, 1, 2, 3):
        halos.extend(_halo_specs(tl, L, D, col))
    return _riding_call(
        body, exchange, nt, name="conv_bwd",
        args=(dya, p, p, p, p, dya, dya, p, p, p, p, p, p, p, p, conv_w, conv_b),
        in_specs=[main(0), main(0), main(1), main(2), main(3)] + halos
                 + [pl.BlockSpec((8, D), lambda i: (0, 0)), _vec_spec(D)],
        out_specs=[pl.BlockSpec((tl, 4 * D), lambda i: (i, 0)), pl.BlockSpec((8, D), lambda i: (0, 0))],
        out_shape=[jax.ShapeDtypeStruct((L, 4 * D), BF16), jax.ShapeDtypeStruct((8, D), F32)],
        cparams=_cparams(("arbitrary",)))


def _ret_bwd_states(qr, do, tab, D):
    L = qr.shape[0]
    H = D // DV
    N = L // CHUNK
    HP = H // 2

    def body(tab_ref, qf_ref, qb_ref, dof_ref, dob_ref, dsf_out, dsb_out, ds0_out, dsf, dsb):
        n = pl.program_id(0)

        @pl.when(n == 0)
        def _():
            dsf[...] = jnp.zeros_like(dsf)
            dsb[...] = jnp.zeros_like(dsb)

        for cc in range(RET_CPB):
            cf_, cb_ = RET_CPB - 1 - cc, cc
            rf, rb = slice(cf_ * CHUNK, (cf_ + 1) * CHUNK), slice(cb_ * CHUNK, (cb_ + 1) * CHUNK)
            dsf_out[cf_] = dsf[...]
            dsb_out[cb_] = dsb[...]
            for pr in range(HP):
                qf2 = qf_ref[rf, pr * 128:(pr + 1) * 128].astype(F32)
                qb2 = qb_ref[rb, pr * 128:(pr + 1) * 128].astype(F32)
                inc_f, inc_b, gf, gb = [], [], [], []
                for e in range(2):
                    h = 2 * pr + e
                    inc_f.append(_dot_tn((qf2 * tab_ref[h, T_QF]).astype(BF16), dof_ref[rf, h * DV:(h + 1) * DV]))
                    inc_b.append(_dot_tn((qb2 * tab_ref[h, T_QB]).astype(BF16), dob_ref[rb, h * DV:(h + 1) * DV]))
                    cf, cb = _chunk_decay(tab_ref, h)
                    gf.append(jnp.broadcast_to(cf, (128, 128)))
                    gb.append(jnp.broadcast_to(cb, (128, 128)))
                dsf[pr] = _pair_select(gf[0], gf[1]) * dsf[pr] + _pair_select(inc_f[0], inc_f[1])
                dsb[pr] = _pair_select(gb[0], gb[1]) * dsb[pr] + _pair_select(inc_b[0], inc_b[1])

        @pl.when(n == NB - 1)
        def _():
            ds0_out[0] = dsf[...]
            ds0_out[1] = dsb[...]

    st = jax.ShapeDtypeStruct((N, HP, 128, 128), F32)
    R = RET_CPB * CHUNK
    NB = N // RET_CPB
    return pl.pallas_call(
        body, name="ret_bwd_states", grid=(NB,),
        in_specs=[_tab_spec(H),
                  pl.BlockSpec((R, D // 2), lambda n: (NB - 1 - n, 0)),
                  pl.BlockSpec((R, D // 2), lambda n: (n, 0)),
                  pl.BlockSpec((R, D), lambda n: (NB - 1 - n, 0)),
                  pl.BlockSpec((R, D), lambda n: (n, 0))],
        out_specs=[pl.BlockSpec((RET_CPB, HP, 128, 128), lambda n: (NB - 1 - n, 0, 0, 0)),
                   pl.BlockSpec((RET_CPB, HP, 128, 128), lambda n: (n, 0, 0, 0)),
                   pl.BlockSpec((2, HP, 128, 128), lambda n: (0, 0, 0, 0))],
        out_shape=[st, st, jax.ShapeDtypeStruct((2, HP, 128, 128), F32)],
        scratch_shapes=[pltpu.VMEM((HP, 128, 128), F32), pltpu.VMEM((HP, 128, 128), F32)],
        compiler_params=_cparams(("arbitrary",)),
    )(tab, qr, qr, do, do)


def _ret_bwd_main(qr, kr, p, do, sf_prev, sb_prev, dsf, dsb, c2, s2, tab, D, exchange=None):
    L = qr.shape[0]
    H = D // DV
    N = L // CHUNK
    HP = H // 2
    W = D // 2

    def body(tab_ref, q_ref, k_ref, v_ref, do_ref, sf_ref, sb_ref, dsf_ref, dsb_ref, c_ref, s_ref,
             dr_ref, st_ref, dl_acc):
        @pl.when(pl.program_id(0) == 0)
        def _():
            dl_acc[...] = jnp.zeros_like(dl_acc)

        i = lax.broadcasted_iota(jnp.int32, (CHUNK, 128), 0).astype(F32)
        rowid = lax.broadcasted_iota(jnp.int32, (128, 128), 0)

        def chunk(cc, carry):
            rows = pl.ds(pl.multiple_of(cc * CHUNK, CHUNK), CHUNK)
            c, s = c_ref[rows, :], s_ref[rows, :]
            for pr in range(HP):
                ps = slice(pr * 128, (pr + 1) * 128)
                q2, k2 = q_ref[rows, ps], k_ref[rows, ps]
                sf32, sb32 = sf_ref[cc, pr], sb_ref[cc, pr]
                dsf32, dsb32 = dsf_ref[cc, pr], dsb_ref[cc, pr]
                sfp, sbp = sf32.astype(BF16), sb32.astype(BF16)
                dsfp, dsbp = dsf32.astype(BF16), dsb32.astype(BF16)
                dq2 = jnp.zeros((CHUNK, 128), F32)
                dk2 = jnp.zeros((CHUNK, 128), F32)
                for e in range(2):
                    h = 2 * pr + e
                    sl = slice(h * DV, (h + 1) * DV)
                    hm = _head_lane_mask(q2.shape, e)
                    qm = jnp.where(hm, q2, jnp.zeros_like(q2))
                    km = jnp.where(hm, k2, jnp.zeros_like(k2))
                    qf, kf = qm.astype(F32), km.astype(F32)
                    v, do = v_ref[rows, sl], do_ref[rows, sl]
                    vf, dof = v.astype(F32), do.astype(F32)
                    m_t = tab_ref[h, T_MT]
                    sc = _dot_nt(qm, k2)
                    dpm = _dot_nt(do, v)
                    dsc = (dpm * tab_ref[h, T_M]).astype(BF16)
                    a_t = (_dot_nt(km, q2) * m_t).astype(BF16)
                    dsc_t = (_dot_nt(v, do) * m_t).astype(BF16)
                    dq_f, dq_b = tab_ref[h, T_QF], tab_ref[h, T_QB]
                    dk_f, dk_b = tab_ref[h, T_KF], tab_ref[h, T_KB]
                    dq = _dot(dsc, km)
                    dq += jnp.where(hm, dq_f * _dot_nt(do, sfp) + dq_b * _dot_nt(do, sbp), 0.0)
                    dk = _dot(dsc_t, qm)
                    dk += jnp.where(hm, dk_f * _dot_nt(v, dsfp) + dk_b * _dot_nt(v, dsbp), 0.0)
                    kdf = _dot((kf * dk_f).astype(BF16), dsfp)
                    kdb = _dot((kf * dk_b).astype(BF16), dsbp)
                    dr_ref[rows, D + h * DV:D + (h + 1) * DV] = (_dot(a_t, do) + kdf + kdb).astype(BF16)
                    dq2 += dq
                    dk2 += dk
                    xf = _dot((qf * dq_f).astype(BF16), sfp)
                    xb = _dot((qf * dq_b).astype(BF16), sbp)
                    pair = (rowid < DK) if e == 0 else (rowid >= DK)
                    gcf, gcb = tab_ref[h, T_QF, CHUNK - 1:CHUNK, 0:1], tab_ref[h, T_QB, 0:1, 0:1]
                    scdp = sc * dpm
                    dl_acc[h, 0] += scdp * tab_ref[h, T_MF1] + xf * dof * (i + 1.0) \
                        + kdf * vf * (CHUNK - 1.0 - i) + (CHUNK * gcf) * jnp.where(pair, dsf32 * sf32, 0.0)
                    dl_acc[h, 1] += scdp * tab_ref[h, T_MB1] + xb * dof * (CHUNK - i) \
                        + kdb * vf * i + (CHUNK * gcb) * jnp.where(pair, dsb32 * sb32, 0.0)
                dr_ref[rows, ps] = (dq2 * c - _swap_halves(dq2) * s).astype(BF16)
                dr_ref[rows, W + pr * 128:W + (pr + 1) * 128] = \
                    ((dk2 * c - _swap_halves(dk2) * s) * K_SCALE).astype(BF16)
            return carry

        lax.fori_loop(0, RET_CPB, chunk, 0)

        @pl.when(pl.program_id(0) == N // RET_CPB - 1)
        def _():
            lane = lax.broadcasted_iota(jnp.int32, (1, 128), 1)
            acc = [jnp.zeros((1, 128), F32), jnp.zeros((1, 128), F32)]
            for h in range(H):
                for b in range(2):
                    acc[b] += jnp.where(lane == h, _sum_all(dl_acc[h, b]), 0.0)
            st_ref[...] = jnp.zeros_like(st_ref)
            st_ref[0:1, :] = acc[0]
            st_ref[1:2, :] = acc[1]

    R = RET_CPB * CHUNK
    st_spec = pl.BlockSpec((RET_CPB, HP, 128, 128), lambda n: (n, 0, 0, 0))
    half = pl.BlockSpec((R, W), lambda n: (n, 0))
    rope = pl.BlockSpec((R, 128), lambda n: (n, 0))
    return _riding_call(
        body, exchange, N // RET_CPB, name="ret_bwd_main",
        args=(tab, qr, kr, p, do, sf_prev, sb_prev, dsf, dsb, c2, s2),
        in_specs=[_tab_spec(H), half, half,
                  pl.BlockSpec((R, D), lambda n: (n, 5)),
                  pl.BlockSpec((R, D), lambda n: (n, 0)),
                  st_spec, st_spec, st_spec, st_spec, rope, rope],
        out_specs=[pl.BlockSpec((R, 2 * D), lambda n: (n, 0)),
                   pl.BlockSpec((8, 128), lambda n: (0, 0))],
        out_shape=[jax.ShapeDtypeStruct((L, 2 * D), BF16), jax.ShapeDtypeStruct((8, 128), F32)],
        scratch=[pltpu.VMEM((H, 2, CHUNK, 128), F32)],
        cparams=_cparams(("arbitrary",)))


def _ctx_bwd(pc, pqk_c, ds0, lg, D):
    Lc = pc.shape[0]
    H = D // DV
    HP = H // 2
    W = D // 2

    def body(lg_ref, k_ref, v_ref, ds_ref, dr_ref, st_ref):
        dqk_ref = dr_ref.at[:, 0:D]
        dv_ref = dr_ref.at[:, D:2 * D]
        m = lax.broadcasted_iota(jnp.int32, (Lc, 128), 0).astype(F32)
        lane = lax.broadcasted_iota(jnp.int32, (1, 128), 1)
        acc_f = jnp.zeros((1, 128), F32)
        acc_b = jnp.zeros((1, 128), F32)
        dqk_ref[:, 0:W] = jnp.zeros((Lc, W), BF16)
        for pr in range(HP):
            ps = slice(pr * 128, (pr + 1) * 128)
            k2 = k_ref[:, ps].astype(F32) * K_SCALE
            dsfp, dsbp = ds_ref[0, pr].astype(BF16), ds_ref[1, pr].astype(BF16)
            dk2 = jnp.zeros((Lc, 128), F32)
            for e in range(2):
                h = 2 * pr + e
                sl = slice(h * DV, (h + 1) * DV)
                hm = _head_lane_mask(k2.shape, e)
                km = jnp.where(hm, k2, 0.0)
                v = v_ref[:, sl]
                vf = v.astype(F32)
                dec_f = jnp.exp(lg_ref[0, h] * (Lc - 1.0 - m))
                dec_b = jnp.exp(lg_ref[1, h] * m)
                kdf = _dot((km * dec_f).astype(BF16), dsfp)
                kdb = _dot((km * dec_b).astype(BF16), dsbp)
                dv_ref[:, sl] = (kdf + kdb).astype(BF16)
                dk2 += jnp.where(hm, dec_f * _dot_nt(v, dsfp) + dec_b * _dot_nt(v, dsbp), 0.0)
                acc_f += jnp.where(lane == h, _sum_all(kdf * vf * (Lc - 1.0 - m)), 0.0)
                acc_b += jnp.where(lane == h, _sum_all(kdb * vf * m), 0.0)
            dqk_ref[:, W + pr * 128:W + (pr + 1) * 128] = (dk2 * K_SCALE).astype(BF16)
        st_ref[...] = jnp.zeros_like(st_ref)
        st_ref[0:1, :] = acc_f
        st_ref[1:2, :] = acc_b

    return pl.pallas_call(
        body, name="ctx_bwd", grid=(1,),
        in_specs=[_smem_spec(), pl.BlockSpec((Lc, W), lambda i: (0, 1)), pl.BlockSpec((Lc, D), lambda i: (0, 1)),
                  pl.BlockSpec((2, HP, 128, 128), lambda i: (0, 0, 0, 0))],
        out_specs=[pl.BlockSpec((Lc, 2 * D), lambda i: (0, 0)), pl.BlockSpec((8, 128), lambda i: (0, 0))],
        out_shape=[jax.ShapeDtypeStruct((Lc, 2 * D), BF16), jax.ShapeDtypeStruct((8, 128), F32)],
    )(lg, pqk_c, pc, ds0)


class _Exchange(NamedTuple):
    inputs: tuple
    out_shapes: tuple
    n_copies: int
    build: Callable


def _exchange_parts(exchange):
    if exchange is None:
        return [], [], [], [], []
    n = exchange.n_copies
    return (list(exchange.inputs), [ANY] * len(exchange.inputs), list(exchange.out_shapes),
            [ANY] * len(exchange.out_shapes), [pltpu.SemaphoreType.DMA((n,)), pltpu.SemaphoreType.DMA((n,))])


def _riding_call(body, exchange, n_steps, *, args, in_specs, out_specs, out_shape, name, cparams, scratch=()):
    ex_args, ex_in_specs, ex_shapes, ex_out_specs, ex_scratch = _exchange_parts(exchange)
    n_in, n_out, n_sc = len(args), len(out_shape), len(scratch)

    def riding(*refs):
        k = n_in + len(ex_args)
        ins, ex_in = refs[:n_in], refs[n_in:k]
        outs, ex_out = refs[k:k + n_out], refs[k + n_out:k + n_out + len(ex_shapes)]
        k += n_out + len(ex_shapes)
        own_scratch, ex_sems = refs[k:k + n_sc], refs[k + n_sc:]
        step = pl.program_id(0)
        if exchange is not None:
            @pl.when(step == 0)
            def _():
                for rc in exchange.build(ex_in, ex_out, *ex_sems):
                    rc.start()
        body(*ins, *outs, *own_scratch)
        if exchange is not None:
            @pl.when(step == n_steps - 1)
            def _():
                for rc in exchange.build(ex_in, ex_out, *ex_sems):
                    rc.wait()

    return tuple(pl.pallas_call(
        riding, name=name, grid=(n_steps,),
        in_specs=list(in_specs) + ex_in_specs, out_specs=list(out_specs) + ex_out_specs,
        out_shape=list(out_shape) + ex_shapes, scratch_shapes=list(scratch) + ex_scratch,
        compiler_params=cparams,
    )(*args, *ex_args))


def _dxm(groups, col0, w, x, nw, sc, dx1, name, exchange=None):
    L, D = x.shape
    tm = min(256, L)
    nt = L // tm
    ng = len(groups)
    widths = [g.shape[1] for g in groups]
    wtot = sum(widths)
    with_dx = dx1 is not None
    ex_args, ex_in_specs, ex_shapes, ex_out_specs, ex_scratch = _exchange_parts(exchange)
    n_in = ng + 4 + (1 if with_dx else 0)
    n_out = 2 if with_dx else 1

    def body(*refs):
        group_refs = refs[:ng]
        w_hbm, x_ref, nw_ref, sc_ref = refs[ng:ng + 4]
        ex_in = refs[n_in:n_in + len(ex_args)]
        outs = refs[n_in + len(ex_args):]
        if with_dx:
            dx1_ref, gx_ref, st_ref = refs[ng + 4], outs[0], outs[1]
        else:
            st_ref = outs[0]
        ex_out = outs[n_out:n_out + len(ex_shapes)]
        w_vm, sem = outs[n_out + len(ex_shapes):n_out + len(ex_shapes) + 2]
        ex_sems = outs[n_out + len(ex_shapes) + 2:]
        i = pl.program_id(0)

        @pl.when(i == 0)
        def _():
            cp = pltpu.make_async_copy(w_hbm.at[:, col0 * D:col0 * D + wtot], w_vm, sem)
            cp.start()
            if exchange is not None:
                for rc in exchange.build(ex_in, ex_out, *ex_sems):
                    rc.start()
            st_ref[...] = jnp.zeros_like(st_ref)
            cp.wait()

        dxm, off = None, 0
        for g_ref, wd in zip(group_refs, widths):
            part = _dot_nt(g_ref[...], w_vm[:, off:off + wd])
            dxm = part if dxm is None else dxm + part
            off += wd

        xv = x_ref[...]
        r = lax.rsqrt(jnp.mean(xv * xv, axis=-1, keepdims=True) + EPS)
        xh = xv * r
        nwv = nw_ref[...]
        dxn = dxm * (1.0 + sc_ref[...])
        st_ref[0:1, :] += jnp.sum(dxm, axis=0, keepdims=True)
        st_ref[1:2, :] += jnp.sum(dxm * (xh * nwv), axis=0, keepdims=True)
        st_ref[2:3, :] += jnp.sum(dxn * xh, axis=0, keepdims=True)
        if with_dx:
            dxh = dxn * nwv
            gx_ref[...] = dx1_ref[...] + r * (dxh - xh * jnp.mean(dxh * xh, axis=-1, keepdims=True))

        if exchange is not None:
            @pl.when(i == nt - 1)
            def _():
                for rc in exchange.build(ex_in, ex_out, *ex_sems):
                    rc.wait()

    row = pl.BlockSpec((tm, D), lambda i: (i, 0))
    in_specs = [pl.BlockSpec((tm, wd), lambda i: (i, 0)) for wd in widths] + [ANY, row, _vec_spec(D), _vec_spec(D)]
    out_specs = [pl.BlockSpec((8, D), lambda i: (0, 0))]
    out_shape = [jax.ShapeDtypeStruct((8, D), F32)]
    args = list(groups) + [w, x, nw, sc]
    if with_dx:
        in_specs.append(row)
        out_specs.insert(0, row)
        out_shape.insert(0, jax.ShapeDtypeStruct((L, D), F32))
        args.append(dx1)
    res = pl.pallas_call(
        body, name=name, grid=(nt,),
        in_specs=in_specs + ex_in_specs, out_specs=out_specs + ex_out_specs, out_shape=out_shape + ex_shapes,
        scratch_shapes=[pltpu.VMEM((D, wtot), BF16), pltpu.SemaphoreType.DMA] + ex_scratch,
        compiler_params=_cparams(("arbitrary",), VMEM_LIMIT),
    )(*args, *ex_args)
    gx = res[0] if with_dx else None
    return (gx, res[n_out - 1], *res[n_out:])


DW_TN = 512
DW_RING = 4


def _dw_in(xmt, groups, cmt, dr_c, D, pair):
    L = xmt.shape[1]
    Lc = cmt.shape[1]
    Dh = D // 2
    tn = min(DW_TN, D)
    nblk = [g.shape[1] // tn for g in groups]
    starts = [sum(nblk[:g]) for g in range(len(groups))]
    ng = len(groups)
    nj = sum(nblk)
    rows_out = Dh if pair else D

    def body(*refs):
        xt_hbm = refs[0]
        group_refs = refs[1:1 + ng]
        ct_hbm, drc_ref, o_ref = refs[1 + ng:4 + ng]
        rest = refs[4 + ng:]
        if pair:
            ra_hbm, xt_vm, ct_vm, loc, ring, s_send, s_recv = rest
            pos = _position()
            sib = _peer(pos, 1)
        else:
            xt_vm, ct_vm, loc = rest
        j = pl.program_id(0)

        @pl.when(j == 0)
        def _():
            if pair:
                c = pos[2]
                other = pl.ds(pl.multiple_of((1 - c) * Dh, Dh), Dh)
                mine = pl.ds(pl.multiple_of(c * Dh, Dh), Dh)
                cps = [pltpu.make_async_copy(xt_hbm.at[other, :], xt_vm.at[0:Dh, :], loc.at[0]),
                       pltpu.make_async_copy(xt_hbm.at[mine, :], xt_vm.at[Dh:D, :], loc.at[1]),
                       pltpu.make_async_copy(ct_hbm.at[other, :], ct_vm.at[0:Dh, :], loc.at[2]),
                       pltpu.make_async_copy(ct_hbm.at[mine, :], ct_vm.at[Dh:D, :], loc.at[3])]
            else:
                cps = [pltpu.make_async_copy(xt_hbm, xt_vm, loc.at[0]), pltpu.make_async_copy(ct_hbm, ct_vm, loc.at[1])]
            for cp in cps:
                cp.start()
            for cp in cps:
                cp.wait()

        def send(slot):
            cols = pl.ds(pl.multiple_of(j * tn, 128), tn)
            return pltpu.make_async_remote_copy(src_ref=ring.at[slot], dst_ref=ra_hbm.at[:, cols],
                                                send_sem=s_send.at[slot], recv_sem=s_recv,
                                                device_id=sib, device_id_type=MESH)

        for g in range(ng):
            @pl.when((j >= starts[g]) & (j < starts[g] + nblk[g]))
            def _(g=g):
                acc = _dot(xt_vm[...], group_refs[g][...])
                if g == 1:
                    acc += _dot(ct_vm[...], drc_ref[...])
                if not pair:
                    o_ref[...] = acc
                    return
                o_ref[...] = acc[Dh:, :]
                slot = lax.rem(j, DW_RING)

                @pl.when(j >= DW_RING)
                def _():
                    send(slot).wait_send()

                ring[slot] = acc[0:Dh, :]
                send(slot).start()

        if pair:
            @pl.when(j == nj - 1)
            def _():
                pltpu.make_async_remote_copy(src_ref=ra_hbm, dst_ref=ra_hbm, send_sem=s_send.at[0], recv_sem=s_recv,
                                             device_id=sib, device_id_type=MESH).wait_recv()
                for slot in range(DW_RING):
                    send(slot).wait_send()

    def group_spec(g, rows):
        return pl.BlockSpec((rows, tn), lambda j: (0, jnp.clip(j - starts[g], 0, nblk[g] - 1)))

    out_specs = [pl.BlockSpec((rows_out, tn), lambda j: (0, j))]
    out_shape = [jax.ShapeDtypeStruct((rows_out, nj * tn), F32)]
    scratch = [pltpu.VMEM((D, L), BF16), pltpu.VMEM((D, Lc), BF16), pltpu.SemaphoreType.DMA((4,))]
    if pair:
        out_specs.append(ANY)
        out_shape.append(jax.ShapeDtypeStruct((Dh, nj * tn), F32))
        scratch += [pltpu.VMEM((DW_RING, Dh, tn), F32), pltpu.SemaphoreType.DMA((DW_RING,)), pltpu.SemaphoreType.DMA]
    return tuple(pl.pallas_call(
        body, name="dw_in", grid=(nj,),
        in_specs=[ANY] + [group_spec(g, L) for g in range(ng)] + [ANY, group_spec(1, Lc)],
        out_specs=out_specs, out_shape=out_shape, scratch_shapes=scratch,
        compiler_params=_cparams(("arbitrary",), VMEM_LIMIT),
    )(xmt, *groups, cmt, dr_c))


def _local_step(x, ctx, tgt, mod_x, mod_c, norm_w, conv_w8, conv_b, lg, gn_w, fw, project, csidx=None, w_in_s=None):
    L, D = x.shape
    sh_x, sc_x, g_x = mod_x[0:1], mod_x[1:2], mod_x[2:3]
    sh_c, sc_c = mod_c[0:1], mod_c[1:2]
    c2, s2 = _rope_tables(L)
    tab = _decay_tables(lg, D // DV)

    xm, xmt, *w_bf = _norm_mod(x, norm_w, sc_x, sh_x, "norm_mod_x", w_in_s)
    cm, cmt = _norm_mod(ctx, norm_w, sc_c, sh_c, "norm_mod_ctx")
    reduce = csidx is not None
    p, qr, kr, w_in, w3 = project(xm, c2, s2, *w_bf)
    pc, pqk_c = _in_proj(cm, w_in, "in_proj_ctx", QK_BLOCK, 2)
    s0 = _ctx_states(pc, pqk_c, lg, D)
    sf_prev, sb_prev = _ret_states(kr, p, s0, tab, D)
    o, yb = _ret_out(qr, kr, p, sf_prev, sb_prev, gn_w, tab, D)
    dx1, dya, do, dzb, dgab, dw3, st_mid = _mid(p, yb, o, x, tgt, w3, g_x, fw, conv_w8, conv_b, gn_w, D)
    dw3_5 = dw3.reshape(3, N_SHARD, 2, D // 8, D)
    dconv, st_conv, *ra_3 = _conv_bwd(dya, p, conv_w8, conv_b, D, _pair_exchange_w3(dw3_5) if reduce else None)
    dsf, dsb, ds0 = _ret_bwd_states(qr, do, tab, D)
    cs_3 = _sum_pair_w3(csidx[0:1], dw3_5, ra_3[0]) if reduce else None
    dret, st_lg, *rb_3 = _ret_bwd_main(qr, kr, p, do, sf_prev, sb_prev, dsf, dsb, c2, s2, tab, D,
                                       _chips_exchange_w3(cs_3) if reduce else None)
    g_3 = _sum_chips_w3(csidx, cs_3, rb_3[0]) if reduce else dw3
    dret_c, st_lgc = _ctx_bwd(pc, pqk_c, ds0, lg, D)
    groups = (dconv, dret, dzb, dgab)
    _, st_c = _dxm((dret_c,), 4, w_in, ctx, norm_w, sc_c, None, "dxm_ctx")
    return groups, dret_c, xmt, cmt, dx1, sc_x, w_in, g_3, (st_mid, st_conv, st_lg, st_lgc, st_c)


CHIP_FLIPS = (4, 2, 6)
ANY = pl.BlockSpec(memory_space=pl.ANY)
VMEM_FULL = pl.BlockSpec(memory_space=pltpu.VMEM)


def _position():
    return lax.axis_index("x"), lax.axis_index("y"), lax.axis_index("c")


def _peer(pos, k):
    x, y, c = pos
    return (1 - x if k & 4 else x, 1 - y if k & 2 else y, 1 - c if k & 1 else c)


def _dev_id(pos):
    return 4 * pos[0] + 2 * pos[1] + pos[2]


def _shard_of(pos):
    return 2 * pos[0] + pos[1]


def _remote(src, dst, send_sems, recv_sems, idx, to):
    return pltpu.make_async_remote_copy(src_ref=src, dst_ref=dst, send_sem=send_sems.at[idx],
                                        recv_sem=recv_sems.at[idx], device_id=to, device_id_type=MESH)


def _dot_f32(a, b):
    return jnp.dot(a, b, precision=lax.Precision.HIGHEST, preferred_element_type=F32)


def _silu(x):
    return x * _sigmoid(x)


def _fwd_small(c8, cctx8, ada_w, ada_b, conv_w8):
    D = c8.shape[1]
    Wm = ada_w.shape[1]
    Dq = conv_w8.shape[1]

    def body(c_ref, cc_ref, aw_ref, ab_ref, cw_ref, act_ref, mod_ref, cwf_ref,
             cbuf, pmine, pbuf, wbuf, s_c, r_c, s_p, r_p, s_w, r_w):
        pos = _position()
        me, s = _dev_id(pos), _shard_of(pos)
        cbuf[me] = c_ref[...]
        wbuf[s] = cw_ref[...]
        sends = [_remote(c_ref, cbuf.at[me], s_c, r_c, k - 1, _peer(pos, k)) for k in range(1, 8)]
        sends += [_remote(cw_ref, wbuf.at[s], s_w, r_w, j, _peer(pos, k)) for j, k in enumerate(CHIP_FLIPS)]
        for cp in sends:
            cp.start()
        for k in range(1, 8):
            _remote(c_ref, cbuf.at[_dev_id(_peer(pos, k))], s_c, r_c, k - 1, _peer(pos, k)).wait_recv()
        for d in range(N_DEV):
            act_ref[d:d + 1, :] = _silu(cbuf[d, 0:1, :])
        act_ref[8:9, :] = _silu(cc_ref[0:1, :])
        act_ref[9:16, :] = jnp.zeros((7, D), F32)
        part = _dot_f32(act_ref[...], aw_ref[...])
        pmine[...] = part
        pbuf[s] = part
        psend = [_remote(pmine, pbuf.at[s], s_p, r_p, j, _peer(pos, k)) for j, k in enumerate(CHIP_FLIPS)]
        for cp in psend:
            cp.start()
        for j, k in enumerate(CHIP_FLIPS):
            t = _shard_of(_peer(pos, k))
            _remote(pmine, pbuf.at[t], s_p, r_p, j, _peer(pos, k)).wait_recv()
            _remote(cw_ref, wbuf.at[t], s_w, r_w, j, _peer(pos, k)).wait_recv()
        for t in range(N_SHARD):
            mod_ref[:, t * Wm:(t + 1) * Wm] = pbuf[t] + ab_ref[:, t * Wm:(t + 1) * Wm]
            cwf_ref[:, t * Dq:(t + 1) * Dq] = wbuf[t]
        for cp in sends + psend:
            cp.wait_send()

    return pl.pallas_call(
        body, name="fwd_small",
        in_specs=[VMEM_FULL] * 5, out_specs=[VMEM_FULL] * 3,
        out_shape=[jax.ShapeDtypeStruct((16, D), F32), jax.ShapeDtypeStruct((16, 3 * D), F32),
                   jax.ShapeDtypeStruct((8, D), F32)],
        scratch_shapes=[pltpu.VMEM((N_DEV, 8, D), F32), pltpu.VMEM((16, Wm), F32),
                        pltpu.VMEM((N_SHARD, 16, Wm), F32), pltpu.VMEM((N_SHARD, 8, Dq), F32),
                        pltpu.SemaphoreType.DMA((7,)), pltpu.SemaphoreType.DMA((7,)),
                        pltpu.SemaphoreType.DMA((3,)), pltpu.SemaphoreType.DMA((3,)),
                        pltpu.SemaphoreType.DMA((3,)), pltpu.SemaphoreType.DMA((3,))],
        compiler_params=_cparams(None, VMEM_LIMIT),
    )(c8, cctx8, ada_w, ada_b, conv_w8)


AG_CHUNKS = 3


def _ag_in_proj(xm, w_in_s, w3_s, c2, s2):
    L, D = xm.shape
    Wc = w_in_s.shape[1]
    Wq = Wc // AG_CHUNKS
    Dh = D // 2
    Do = w3_s[0].shape[1]
    TM = min(1024, L // 4)
    NT = L // TM
    NQ = AG_CHUNKS
    order = [(q, j) for q in range(NQ) for j in (0, 1)] + [(q, 2) for q in range(NQ)]

    def body(xm_hbm, wi_hbm, wa_ref, wb_ref, wo_ref, c_hbm, s_hbm, p_hbm, qr_hbm, kr_hbm, fi_hbm, f3_hbm,
             w_vm, xm_ref, c_ref, s_ref, s3, stage, qk_stage, ici_s, ici_r, d2d_s, d2d_r, w3_s_, w3_r_, fw_s, fw_r,
             loc, out_sem, qk_sem):
        pos = _position()
        c = pos[2]
        s = _shard_of(pos)
        sib = _peer(pos, 1)
        mine = pl.ds(pl.multiple_of(c * Dh, Dh), Dh)
        other = pl.ds(pl.multiple_of((1 - c) * Dh, Dh), Dh)

        def abs_col(t, q):
            return pl.ds(pl.multiple_of(t * Wc + q * Wq, 128), Wq)

        own = [pltpu.make_async_copy(wi_hbm.at[:, q * Wq:(q + 1) * Wq], w_vm.at[0, q], loc.at[2 + 4 * NQ + q])
               for q in range(NQ)]
        for cp in own:
            cp.start()
        for cp in own:
            cp.wait()
        sends = [_remote(w_vm.at[0, q, mine, :], w_vm.at[1 + j, q, mine, :], ici_s, ici_r, q * 3 + j,
                         _peer(pos, CHIP_FLIPS[j])) for q, j in order if j < 2]
        for cp in sends:
            cp.start()
        loads = [pltpu.make_async_copy(xm_hbm, xm_ref, loc.at[2 + 5 * NQ]),
                 pltpu.make_async_copy(c_hbm, c_ref, loc.at[3 + 5 * NQ]),
                 pltpu.make_async_copy(s_hbm, s_ref, loc.at[4 + 5 * NQ])]
        for cp in loads:
            cp.start()
        for a, w_ref in enumerate((wa_ref, wb_ref, wo_ref)):
            s3[a] = w_ref[...].astype(BF16)
        for cp in loads:
            cp.wait()
        w3_sends = [_remote(s3.at[:, c], f3_hbm.at[:, s, c], w3_s_, w3_r_, j, _peer(pos, k))
                    for j, k in enumerate(CHIP_FLIPS)]
        local = [pltpu.make_async_copy(s3, f3_hbm.at[:, s], loc.at[1])]
        local += [pltpu.make_async_copy(w_vm.at[0, q], fi_hbm.at[:, abs_col(s, q)], loc.at[2 + q]) for q in range(NQ)]
        for cp in local:
            cp.start()

        def out_copy(slot, rows, cols):
            return pltpu.make_async_copy(stage.at[slot], p_hbm.at[rows, cols], out_sem.at[slot])

        def block(r, q, t, first):
            cols = abs_col(t, q)

            def row_tile(rt, carry):
                rows = pl.ds(pl.multiple_of(rt * TM, TM), TM)
                acc = _dot(xm_ref[rows, :], w_vm[r, q])
                slot = lax.rem(rt, 2)

                @pl.when(rt >= 2 if first else rt >= 0)
                def _():
                    out_copy(slot, rows, cols).wait()

                stage[slot] = acc.astype(BF16)
                out_copy(slot, rows, cols).start()

                def rotary(lo, scale, dst_hbm):
                    c, s = c_ref[rows, :], s_ref[rows, :]
                    for pr in range(Dh // 128):
                        tq = acc[:, lo + pr * 128:lo + (pr + 1) * 128] * scale
                        qk_stage[:, pr * 128:(pr + 1) * 128] = (tq * c + _swap_halves(tq) * s).astype(BF16)
                    cp = pltpu.make_async_copy(qk_stage, dst_hbm.at[rows, :], qk_sem)
                    cp.start()
                    cp.wait()

                if q == NQ - 1:
                    @pl.when(t == 1)
                    def _():
                        rotary(Wq - Dh, 1.0, qr_hbm)
                if q == 0:
                    @pl.when(t == 2)
                    def _():
                        rotary(0, K_SCALE, kr_hbm)
                return carry

            lax.fori_loop(0, NT, row_tile, 0)

        passed = []

        def hand_on(q, j):
            half = w_vm.at[1 + j, q, mine, :]
            if j == 2:
                _remote(half, half, fw_s, fw_r, q, sib).wait_recv()
            else:
                _remote(half, half, ici_s, ici_r, q * 3 + j, sib).wait_recv()

                @pl.when(c == (0 if j == q % 2 else 1))
                def _():
                    _remote(half, w_vm.at[3, q, mine, :], fw_s, fw_r, q, _peer(pos, CHIP_FLIPS[1 - j])).start()
            fwd = _remote(half, half, d2d_s, d2d_r, q * 3 + j, sib)
            fwd.start()
            passed.append(fwd)

        for q in range(NQ):
            if q == NQ - 1:
                hand_on(*order[0])
            block(0, q, s, q == 0)
        for n, (q, j) in enumerate(order):
            r, idx = 1 + j, q * 3 + j
            t = _shard_of(_peer(pos, CHIP_FLIPS[j]))
            if n + 1 < len(order):
                hand_on(*order[n + 1])
            if n + 1 == 2 * NQ - 1:
                for cp in w3_sends:
                    cp.start()
            _remote(w_vm.at[r, q, other, :], w_vm.at[r, q, other, :], d2d_s, d2d_r, idx, sib).wait_recv()
            block(r, q, t, False)
            cp = pltpu.make_async_copy(w_vm.at[r, q], fi_hbm.at[:, abs_col(t, q)], loc.at[2 + NQ + idx])
            cp.start()
            local.append(cp)
        for j, k in enumerate(CHIP_FLIPS):
            t = _shard_of(_peer(pos, k))
            _remote(s3.at[:, c], f3_hbm.at[:, t, c], w3_s_, w3_r_, j, sib).wait_recv()
            fwd = _remote(f3_hbm.at[:, t, c], f3_hbm.at[:, t, c], w3_s_, w3_r_, 3 + j, sib)
            fwd.start()
            passed.append(fwd)
        for j, k in enumerate(CHIP_FLIPS):
            t = _shard_of(_peer(pos, k))
            _remote(s3.at[:, c], f3_hbm.at[:, t, 1 - c], w3_s_, w3_r_, 3 + j, sib).wait_recv()
        for cp in sends + w3_sends + passed:
            cp.wait_send()
        for q in range(NQ):
            _remote(w_vm.at[1, q, mine, :], w_vm.at[3, q, mine, :], fw_s, fw_r, q, sib).wait_send()
        for cp in local:
            cp.wait()
        for slot in range(2):
            out_copy(slot, pl.ds(0, TM), abs_col(s, 0)).wait()

    n_loc = 5 + 5 * NQ
    return pl.pallas_call(
        body, name="ag_in_proj",
        in_specs=[ANY, ANY, VMEM_FULL, VMEM_FULL, VMEM_FULL, ANY, ANY], out_specs=[ANY] * 5,
        out_shape=[jax.ShapeDtypeStruct((L, N_SHARD * Wc), BF16),
                   jax.ShapeDtypeStruct((L, Dh), BF16), jax.ShapeDtypeStruct((L, Dh), BF16),
                   jax.ShapeDtypeStruct((D, N_SHARD * Wc), BF16), jax.ShapeDtypeStruct((3, N_SHARD, 2, Do, D), BF16)],
        scratch_shapes=[pltpu.VMEM((N_SHARD, NQ, D, Wq), BF16), pltpu.VMEM((L, D), BF16),
                        pltpu.VMEM((L, 128), F32), pltpu.VMEM((L, 128), F32), pltpu.VMEM((3, 2, Do, D), BF16),
                        pltpu.VMEM((2, TM, Wq), BF16), pltpu.VMEM((TM, Dh), BF16),
                        pltpu.SemaphoreType.DMA((3 * NQ,)), pltpu.SemaphoreType.DMA((3 * NQ,)),
                        pltpu.SemaphoreType.DMA((3 * NQ,)), pltpu.SemaphoreType.DMA((3 * NQ,)),
                        pltpu.SemaphoreType.DMA((6,)), pltpu.SemaphoreType.DMA((6,)),
                        pltpu.SemaphoreType.DMA((NQ,)), pltpu.SemaphoreType.DMA((NQ,)),
                        pltpu.SemaphoreType.DMA((n_loc,)), pltpu.SemaphoreType.DMA((2,)), pltpu.SemaphoreType.DMA],
        compiler_params=_cparams(None, VMEM_LIMIT),
    )(xm, w_in_s, *w3_s, c2, s2)


def _pair_exchange_w3(dw3):
    _, _, _, Do, D = dw3.shape

    def build(ins, outs, send, recv):
        pos = _position()
        return [_remote(ins[0].at[:, :, 1 - pos[2]], outs[0], send, recv, 0, _peer(pos, 1))]

    return _Exchange((dw3,), (jax.ShapeDtypeStruct((3, N_SHARD, Do, D), F32),), 1, build)


def _sum_pair_in(dw_mine, ri):
    Dh, Wf = dw_mine.shape
    Wc = Wf // N_SHARD
    tr = min(256, Dh)

    def body(a_ref, b_ref, o_ref):
        o_ref[...] = (a_ref[...] + b_ref[...]).astype(BF16)

    return pl.pallas_call(
        body, name="sum_pair_in", grid=(Dh // tr, N_SHARD),
        in_specs=[pl.BlockSpec((tr, Wc), lambda i, t: (i, t)), pl.BlockSpec((tr, Wc), lambda i, t: (i, t))],
        out_specs=pl.BlockSpec((None, tr, Wc), lambda i, t: (t, i, 0)),
        out_shape=jax.ShapeDtypeStruct((N_SHARD, Dh, Wc), BF16),
        compiler_params=_cparams(("parallel", "parallel")),
    )(dw_mine, ri)


def _sum_pair_w3(cidx, dw3, r3):
    _, _, _, Do, D = dw3.shape

    def body(c_ref, a_ref, b_ref, o_ref):
        o_ref[...] = (a_ref[...] + b_ref[...]).astype(BF16)

    return pl.pallas_call(
        body, name="sum_pair_w3",
        grid_spec=pltpu.PrefetchScalarGridSpec(
            num_scalar_prefetch=1, grid=(3,),
            in_specs=[pl.BlockSpec((None, N_SHARD, None, Do, D), lambda a, c: (a, 0, c[0], 0, 0)),
                      pl.BlockSpec((None, N_SHARD, Do, D), lambda a, c: (a, 0, 0, 0))],
            out_specs=pl.BlockSpec((None, N_SHARD, Do, D), lambda a, c: (a, 0, 0, 0))),
        out_shape=jax.ShapeDtypeStruct((3, N_SHARD, Do, D), BF16),
        compiler_params=_cparams(("parallel",)),
    )(cidx, dw3, r3)


def _chips_exchange_in(cs_in):
    _, Dh, Wc = cs_in.shape

    def build(ins, outs, send, recv):
        pos = _position()
        return [_remote(ins[0].at[_shard_of(_peer(pos, k))], outs[0].at[j], send, recv, j, _peer(pos, k))
                for j, k in enumerate(CHIP_FLIPS)]

    return _Exchange((cs_in,), (jax.ShapeDtypeStruct((3, Dh, Wc), BF16),), 3, build)


def _chips_exchange_w3(cs_3):
    _, _, Do, D = cs_3.shape

    def build(ins, outs, send, recv):
        pos = _position()
        return [_remote(ins[0].at[:, _shard_of(_peer(pos, k))], outs[0].at[j], send, recv, j, _peer(pos, k))
                for j, k in enumerate(CHIP_FLIPS)]

    return _Exchange((cs_3,), (jax.ShapeDtypeStruct((3, 3, Do, D), BF16),), 3, build)


def _sum_chips_in(csidx, cs_in, rb_in):
    _, Dh, Wc = cs_in.shape
    tr = min(256, Dh)

    def body(s_ref, a_ref, b_ref, o_ref):
        acc = a_ref[...].astype(F32)
        for j in range(3):
            acc = acc + b_ref[j].astype(F32)
        o_ref[...] = acc

    return pl.pallas_call(
        body, name="sum_chips_in",
        grid_spec=pltpu.PrefetchScalarGridSpec(
            num_scalar_prefetch=1, grid=(Dh // tr,),
            in_specs=[pl.BlockSpec((None, tr, Wc), lambda i, s: (s[1], i, 0)),
                      pl.BlockSpec((3, tr, Wc), lambda i, s: (0, i, 0))],
            out_specs=pl.BlockSpec((None, tr, Wc), lambda i, s: (s[0], i, 0))),
        out_shape=jax.ShapeDtypeStruct((2, Dh, Wc), F32),
        compiler_params=_cparams(("parallel",)),
    )(csidx, cs_in, rb_in)


def _sum_chips_w3(csidx, cs_3, rb_3):
    _, _, Do, D = cs_3.shape

    def body(s_ref, a_ref, b_ref, o_ref):
        acc = a_ref[...].astype(F32)
        for j in range(3):
            acc = acc + b_ref[j].astype(F32)
        o_ref[...] = acc

    return pl.pallas_call(
        body, name="sum_chips_w3",
        grid_spec=pltpu.PrefetchScalarGridSpec(
            num_scalar_prefetch=1, grid=(3,),
            in_specs=[pl.BlockSpec((None, None, Do, D), lambda a, s: (a, s[1], 0, 0)),
                      pl.BlockSpec((3, None, Do, D), lambda a, s: (0, a, 0, 0))],
            out_specs=pl.BlockSpec((None, None, Do, D), lambda a, s: (a, s[0], 0, 0))),
        out_shape=jax.ShapeDtypeStruct((3, 2, Do, D), F32),
        compiler_params=_cparams(("parallel",)),
    )(csidx, cs_3, rb_3)


def _adam_math(w, g, m, v):
    m = ADAM_B1 * m + (1.0 - ADAM_B1) * g
    v = ADAM_B2 * v + (1.0 - ADAM_B2) * (g * g)
    m_hat = m / (1.0 - ADAM_B1 ** ADAM_STEP)
    v_hat = v / (1.0 - ADAM_B2 ** ADAM_STEP)
    delta = -ADAM_LR * (m_hat / (jnp.sqrt(v_hat) + ADAM_EPS) + ADAM_WD * w)
    return delta, m, v


def _adamw(w, g, m, v, name):
    R, C = w.shape
    tr = min(128, R)

    def body(w_ref, g_ref, m_ref, v_ref, d_ref, nm_ref, nv_ref):
        d_ref[...], nm_ref[...], nv_ref[...] = _adam_math(w_ref[...], g_ref[...], m_ref[...], v_ref[...])

    blk = pl.BlockSpec((tr, C), lambda i: (i, 0))
    return pl.pallas_call(
        body, name=name, grid=(R // tr,), in_specs=[blk] * 4, out_specs=[blk] * 3,
        out_shape=[jax.ShapeDtypeStruct((R, C), F32)] * 3,
        compiler_params=_cparams(("parallel",), VMEM_LIMIT),
    )(w, g, m, v)


def _adamw3(ws, g3, ms, vs):
    R, C = ws[0].shape

    def body(*refs):
        w_refs, m_refs, v_refs = refs[0:3], refs[3:6], refs[6:9]
        g_ref, outs = refs[9], refs[10:]
        for a in range(3):
            @pl.when(pl.program_id(0) == a)
            def _(a=a):
                res = _adam_math(w_refs[a][...], g_ref[...], m_refs[a][...], v_refs[a][...])
                for q in range(3):
                    outs[3 * a + q][...] = res[q]

    full = pl.BlockSpec((R, C), lambda a: (0, 0))
    res = pl.pallas_call(
        body, name="adamw_w3", grid=(3,),
        in_specs=[full] * 9 + [pl.BlockSpec((None, R, C), lambda a: (a, 0, 0))], out_specs=[full] * 9,
        out_shape=[jax.ShapeDtypeStruct((R, C), F32)] * 9,
        compiler_params=_cparams(("arbitrary",), VMEM_LIMIT),
    )(*ws, *ms, *vs, g3)
    return res[0:3], res[3:6], res[6:9]


SMALL_ROWS = ("c_ctx", "norm_w", "conv_b", "gn_w", "final_norm_w")


def _bwd_small(stats, ada_w, Dq, gh_in, gh_3):
    D = stats[0].shape[1]
    Wm = ada_w.shape[1]

    def body(stx, stm, stc, stv, stl, stlc, aw_ref, gi_in, g3_in, tot_ref, dm_sh, gcw, da_ref, loss_ref, gi_ref, g3_ref,
             vec_ref, vbuf, dm, amine, abuf, s_v, r_v, s_a, r_a, s_g, r_g):
        pos = _position()
        me, s = _dev_id(pos), _shard_of(pos)
        c, sib = pos[2], _peer(pos, 1)
        halves = [_remote(gi_in.at[c], gi_ref.at[c], s_g, r_g, 0, sib),
                  _remote(g3_in.at[:, c], g3_ref.at[:, c], s_g, r_g, 1, sib)]
        for cp in halves:
            cp.start()
        vec_ref[...] = jnp.zeros_like(vec_ref)
        vec_ref[0:2, :] = stx[0:2, :]
        vec_ref[2:3, :] = stm[1:2, :]
        vec_ref[3:5, :] = stc[0:2, :]
        vec_ref[5:6, :] = stx[2:3, :] + stc[2:3, :]
        vec_ref[6:7, :] = stv[3:4, :]
        vec_ref[7:8, :] = stm[3:4, :]
        vec_ref[8:9, :] = stm[0:1, :]
        vec_ref[9:12, :] = stv[0:3, :]
        vec_ref[12:14, 0:128] = stl[0:2, :] + stlc[0:2, :]
        vec_ref[14:15, :] = stm[2:3, :]
        vbuf[me] = vec_ref[...]
        sends = [_remote(vec_ref, vbuf.at[me], s_v, r_v, k - 1, _peer(pos, k)) for k in range(1, 8)]
        for cp in sends:
            cp.start()
        for k in range(1, 8):
            _remote(vec_ref, vbuf.at[_dev_id(_peer(pos, k))], s_v, r_v, k - 1, _peer(pos, k)).wait_recv()
        tot = vbuf[0]
        for d in range(1, N_DEV):
            tot = tot + vbuf[d]
        loss_ref[...] = jnp.zeros((8, 128), F32) + (0.5 / D) * _sum_all(tot[14:15, :])
        dm[...] = jnp.zeros_like(dm)
        for d in range(N_DEV):
            for r in range(3):
                dm[d:d + 1, r * D:(r + 1) * D] = vbuf[d, r:r + 1, :]
        dm[8:9, 0:D] = tot[3:4, :]
        dm[8:9, D:2 * D] = tot[4:5, :]
        for t in range(N_SHARD):
            @pl.when(s == t)
            def _(t=t):
                dm_sh[...] = dm[:, t * Wm:(t + 1) * Wm]
                gcw[...] = tot[9:12, t * Dq:(t + 1) * Dq]
        tot_ref[...] = tot
        part = lax.dot_general(dm_sh[8:16, :], aw_ref[...], (((1,), (1,)), ((), ())),
                               precision=lax.Precision.HIGHEST, preferred_element_type=F32)
        amine[...] = part
        abuf[s] = part
        asend = [_remote(amine, abuf.at[s], s_a, r_a, j, _peer(pos, k)) for j, k in enumerate(CHIP_FLIPS)]
        for cp in asend:
            cp.start()
        for j, k in enumerate(CHIP_FLIPS):
            _remote(amine, abuf.at[_shard_of(_peer(pos, k))], s_a, r_a, j, _peer(pos, k)).wait_recv()
        da = abuf[0]
        for t in range(1, N_SHARD):
            da = da + abuf[t]
        da_ref[...] = da
        _remote(gi_in.at[1 - c], gi_ref.at[1 - c], s_g, r_g, 0, sib).wait_recv()
        _remote(g3_in.at[:, 1 - c], g3_ref.at[:, 1 - c], s_g, r_g, 1, sib).wait_recv()
        for cp in sends + asend + halves:
            cp.wait_send()

    row = lambda *shape: jax.ShapeDtypeStruct(shape, F32)
    return pl.pallas_call(
        body, name="bwd_small",
        in_specs=[VMEM_FULL] * 7 + [ANY, ANY], out_specs=[VMEM_FULL] * 5 + [ANY, ANY],
        input_output_aliases={7: 5, 8: 6},
        out_shape=[row(16, D), row(16, Wm), row(3, Dq), row(8, D), row(8, 128), row(*gh_in.shape), row(*gh_3.shape)],
        scratch_shapes=[pltpu.VMEM((16, D), F32), pltpu.VMEM((N_DEV, 16, D), F32), pltpu.VMEM((16, 3 * D), F32),
                        pltpu.VMEM((8, D), F32), pltpu.VMEM((N_SHARD, 8, D), F32),
                        pltpu.SemaphoreType.DMA((7,)), pltpu.SemaphoreType.DMA((7,)),
                        pltpu.SemaphoreType.DMA((3,)), pltpu.SemaphoreType.DMA((3,)),
                        pltpu.SemaphoreType.DMA((2,)), pltpu.SemaphoreType.DMA((2,))],
        compiler_params=_cparams(None, VMEM_LIMIT),
    )(*stats, ada_w, gh_in, gh_3)


def _small_update(tot, dm_sh, gcw, da, act, p_row, p_ab, p_cw, p_dl):
    D = act.shape[1]
    Wm = dm_sh.shape[1]
    Dq = gcw.shape[1]

    def body(tot_ref, dm_ref, gcw_ref, da_ref, act_ref, prow, pab, pcw, pdl, gaw_ref, *outs):
        o_q = [outs[8 * q:8 * (q + 1)] for q in range(4)]
        tot = tot_ref[...]
        gaw_ref[...] = lax.dot_general(act_ref[...], dm_ref[...], (((0,), (0,)), ((), ())),
                                       precision=lax.Precision.HIGHEST, preferred_element_type=F32)
        cc = prow[0, 0:1, :]
        sg = _sigmoid(cc)
        g_cctx = da_ref[0:1, :] * (sg * (1.0 + cc * (1.0 - sg)))

        def place_all(o, val):
            o[...] = val

        def emit(k, w, g, m, v, place=place_all):
            for q, val in enumerate((g,) + _adam_math(w, g, m, v)):
                place(o_q[q][k], val)

        g_rows = [g_cctx, tot[5:6, :], tot[6:7, :], tot[7:8, :], tot[8:9, :]]
        for k, g in enumerate(g_rows):
            emit(k, prow[0, k:k + 1, :], g, prow[1, k:k + 1, :], prow[2, k:k + 1, :])

        def place_ab(o, val):
            for r in range(3):
                o[0:1, r * D:(r + 1) * D] = val[r:r + 1, :]

        g_ab = jnp.concatenate([tot[0:1, :] + tot[3:4, :], tot[1:2, :] + tot[4:5, :], tot[2:3, :]], axis=0)
        emit(5, pab[0], g_ab, pab[1], pab[2], place_ab)
        emit(6, pcw[0], gcw_ref[...], pcw[1], pcw[2])
        g_dl = jnp.concatenate([tot[12:14, 0:128] * _sigmoid(-pdl[0, 0:2, :]), jnp.zeros((6, 128), F32)], axis=0)
        emit(7, pdl[0], g_dl, pdl[1], pdl[2])

    row = lambda *shape: jax.ShapeDtypeStruct(shape, F32)
    per_q = [row(1, D)] * 5 + [row(1, 3 * D), row(3, Dq), row(8, 128)]
    res = pl.pallas_call(
        body, name="small_update",
        in_specs=[VMEM_FULL] * 9, out_specs=[VMEM_FULL] * 33,
        out_shape=[row(D, Wm)] + per_q * 4,
        compiler_params=_cparams(None, VMEM_LIMIT),
    )(tot, dm_sh, gcw, da, act, p_row, p_ab, p_cw, p_dl)
    return res[0], [res[1 + 8 * q:1 + 8 * (q + 1)] for q in range(4)]


def _pad_rows(a, rows=8):
    return jnp.pad(a, ((0, rows - a.shape[0]), (0, 0)))


def kernel(x, c, ctx, c_ctx, norm_w, ada_w, ada_b, w_in, conv_w, conv_b, decay_logit, gn_w, w_a, w_b, w_out, final_norm_w, loss_target, m_c_ctx, m_norm_w, m_ada_w, m_ada_b, m_w_in, m_conv_w, m_conv_b, m_decay_logit, m_gn_w, m_w_a, m_w_b, m_w_out, m_final_norm_w, v_c_ctx, v_norm_w, v_ada_w, v_ada_b, v_w_in, v_conv_w, v_conv_b, v_decay_logit, v_gn_w, v_w_a, v_w_b, v_w_out, v_final_norm_w):
    L, D = x.shape[1], x.shape[2]
    H = D // DV
    Wc = w_in.shape[2]
    Do = D // 8
    pos = _position()
    me = _dev_id(pos)
    cidx = jnp.reshape(pos[2], (1,)).astype(jnp.int32)
    sidx = jnp.reshape(_shard_of(pos), (1,)).astype(jnp.int32)

    act, mod, conv_w8 = _fwd_small(_pad_rows(c), _pad_rows(c_ctx[None]), ada_w[0], ada_b, _pad_rows(conv_w[0]))
    mod_x = lax.dynamic_slice_in_dim(mod, me, 1, axis=0).reshape(3, D)
    mod_c = mod[8].reshape(3, D)
    lg = jax.nn.log_sigmoid(decay_logit[0])

    w3_s = tuple(w[0].reshape(2, Do, D) for w in (w_a, w_b, w_out))

    def project(xm, c2, s2, w_in_bf):
        p, qr, kr, w_in_full, w3_full = _ag_in_proj(xm, w_in_bf, w3_s, c2, s2)
        return p, qr, kr, w_in_full, w3_full.reshape(3, D, D)

    csidx = jnp.concatenate([cidx, sidx])
    groups, dret_c, xmt, cmt, dx1, sc_x, w_in_full, gh_3, sts = _local_step(
        x[0], ctx[0], loss_target[0], mod_x, mod_c, norm_w, conv_w8, conv_b, lg, gn_w, final_norm_w[None],
        project, csidx, w_in[0])
    st_mid, st_conv, st_lg, st_lgc, st_c = sts

    dw_mine, ra_in = _dw_in(xmt, groups, cmt, dret_c, D, True)
    cs_in = _sum_pair_in(dw_mine, ra_in)
    grad_x, st_x, rb_in = _dxm(groups, 0, w_in_full, x[0], norm_w, sc_x, dx1, "dxm_x", _chips_exchange_in(cs_in))
    gh_in = _sum_chips_in(csidx, cs_in, rb_in)

    zeros3 = jnp.zeros((3, D), F32)
    p_row = jnp.concatenate(
        [r for t in ((c_ctx[None], norm_w, conv_b, gn_w, final_norm_w[None], zeros3),
                     (m_c_ctx[None], m_norm_w, m_conv_b, m_gn_w, m_final_norm_w[None], zeros3),
                     (v_c_ctx[None], v_norm_w, v_conv_b, v_gn_w, v_final_norm_w[None], zeros3)) for r in t],
        axis=0).reshape(3, 8, D)
    p_ab = jnp.concatenate([ada_b, m_ada_b, v_ada_b], axis=0).reshape(3, 3, D)
    p_cw = jnp.concatenate([conv_w, m_conv_w, v_conv_w], axis=0)
    p_dl = jnp.pad(jnp.concatenate([decay_logit, m_decay_logit, v_decay_logit], axis=0), ((0, 0), (0, 6), (0, 128 - H)))
    tot, dm_sh, gcw, da, loss_t, g_in, g_3 = _bwd_small((st_x, st_mid, st_c, st_conv, st_lg, st_lgc), ada_w[0],
                                                        conv_w.shape[2], gh_in, gh_3)
    g_w_in = g_in.reshape(D, Wc)
    g_3 = g_3.reshape(3, D // 4, D)
    g_ada_w, small = _small_update(tot, dm_sh, gcw, da, act, p_row, p_ab, p_cw, p_dl)

    upd_in = _adamw(w_in[0], g_w_in, m_w_in[0], v_w_in[0], "adamw_w_in")
    upd_ada = _adamw(ada_w[0], g_ada_w, m_ada_w[0], v_ada_w[0], "adamw_ada_w")
    upd_a, upd_b, upd_o = _adamw3((w_a[0], w_b[0], w_out[0]), g_3, (m_w_a[0], m_w_b[0], m_w_out[0]),
                                  (v_w_a[0], v_w_b[0], v_w_out[0]))

    def leaves(q):
        big = lambda g, upd: (g if q == 0 else upd[q - 1])[None]
        r_cctx, r_norm, r_convb, r_gn, r_fnorm, r_ab, r_cw, r_dl = small[q]
        return [r_cctx.reshape(D), r_norm, big(g_ada_w, upd_ada), r_ab, big(g_w_in, upd_in),
                r_cw[None], r_convb, r_dl[0:2, 0:H][None], r_gn,
                big(g_3[0], upd_a), big(g_3[1], upd_b), big(g_3[2], upd_o), r_fnorm.reshape(D)]

    loss = loss_t[0, 0]
    return (loss, grad_x[None], *leaves(0), *leaves(1), *leaves(2), *leaves(3))
```

```python
from typing import Callable, NamedTuple

import jax
import jax.numpy as jnp
import numpy as np
from jax import lax
from jax.experimental import pallas as pl
from jax.experimental.pallas import tpu as pltpu

F32 = jnp.float32
BF16 = jnp.bfloat16
MESH = pl.DeviceIdType.MESH

CHUNK = 128
RET_CPB = 4
DV = 128
DK = 64
GRID_W = 64
ROPE_BASE = 10000.0
EPS = 1e-6
K_SCALE = DK ** -0.5
N_SHARD = 4
N_DEV = 8

ADAM_LR = 0.001
ADAM_B1 = 0.9
ADAM_B2 = 0.999
ADAM_EPS = 1e-08
ADAM_WD = 0.01
ADAM_STEP = 10

VMEM_LIMIT = 56 * 1024 * 1024


def _cparams(sem=None, vmem=None):
    kw = {}
    if sem is not None:
        kw["dimension_semantics"] = sem
    if vmem is not None:
        kw["vmem_limit_bytes"] = vmem
    return pltpu.CompilerParams(**kw)


def _dot(a, b):
    return jnp.dot(a, b, preferred_element_type=F32)


def _dot_nt(a, b):
    return lax.dot_general(a, b, (((1,), (1,)), ((), ())), preferred_element_type=F32)


def _dot_tn(a, b):
    return lax.dot_general(a, b, (((0,), (0,)), ((), ())), preferred_element_type=F32)


def _sigmoid(x):
    return 1.0 / (1.0 + jnp.exp(-x))


def _sum_all(x):
    return jnp.sum(jnp.sum(x, axis=1, keepdims=True), axis=0, keepdims=True)


def _swap_halves(t):
    n = t.shape[1]
    lane = lax.broadcasted_iota(jnp.int32, t.shape, 1)
    low = (lane & 32) == 0
    return jnp.where(low, pltpu.roll(t, n - 32, 1), pltpu.roll(t, 32, 1))


def _vec_spec(d):
    return pl.BlockSpec((1, d), lambda *a: (0, 0))


def _norm_mod(x, nw, sc, sh, name, also_bf16=None):
    L, D = x.shape
    tl = min(256, L)
    nt = L // tl

    def body(x_ref, nw_ref, sc_ref, sh_ref, *rest):
        xm_ref, xmt_ref = rest[-3:-1] if also_bf16 is not None else rest
        xv = x_ref[...]
        r = lax.rsqrt(jnp.mean(xv * xv, axis=-1, keepdims=True) + EPS)
        xm = (xv * r * nw_ref[...]) * (1.0 + sc_ref[...]) + sh_ref[...]
        xm_b = xm.astype(BF16)
        xm_ref[...] = xm_b
        xmt_ref[...] = xm_b.T
        if also_bf16 is not None:
            rest[-1][...] = rest[0][...].astype(BF16)

    in_specs = [pl.BlockSpec((tl, D), lambda i: (i, 0)), _vec_spec(D), _vec_spec(D), _vec_spec(D)]
    out_specs = [pl.BlockSpec((tl, D), lambda i: (i, 0)), pl.BlockSpec((D, tl), lambda i: (0, i))]
    out_shape = [jax.ShapeDtypeStruct((L, D), BF16), jax.ShapeDtypeStruct((D, L), BF16)]
    args = [x, nw, sc, sh]
    if also_bf16 is not None:
        R, C = also_bf16.shape
        slab = pl.BlockSpec((R // nt, C), lambda i: (i, 0))
        in_specs.append(slab)
        out_specs.append(slab)
        out_shape.append(jax.ShapeDtypeStruct((R, C), BF16))
        args.append(also_bf16)
    return pl.pallas_call(
        body, name=name, grid=(nt,), in_specs=in_specs, out_specs=out_specs, out_shape=out_shape,
        compiler_params=_cparams(("parallel",)),
    )(*args)


QK_BLOCK, V_BLOCK = 4, 5


def _in_proj(xm, w, name, first=0, count=None):
    M, D = xm.shape
    count = w.shape[1] // D if count is None else count
    tm = min(1024, M)

    def body(a_ref, b_ref, o_ref, qk_ref):
        acc = _dot(a_ref[...], b_ref[...])
        o_ref[...] = acc.astype(o_ref.dtype)

        @pl.when(pl.program_id(1) == QK_BLOCK - first)
        def _():
            qk_ref[...] = acc

    return pl.pallas_call(
        body, name=name, grid=(M // tm, count),
        in_specs=[pl.BlockSpec((tm, D), lambda i, j: (i, 0)), pl.BlockSpec((D, D), lambda i, j: (0, first + j))],
        out_specs=[pl.BlockSpec((tm, D), lambda i, j: (i, j)), pl.BlockSpec((tm, D), lambda i, j: (i, 0))],
        out_shape=[jax.ShapeDtypeStruct((M, count * D), BF16), jax.ShapeDtypeStruct((M, D), F32)],
        compiler_params=_cparams(("parallel", "arbitrary")),
    )(xm, w)


def _halo_specs(tl, L, D, col):
    hb = tl // 16
    last = L // 16 - 1
    prev = pl.BlockSpec((16, D), lambda i: (jnp.maximum(i * hb - 1, 0), col))
    nxt = pl.BlockSpec((16, D), lambda i: (jnp.minimum((i + 1) * hb, last), col))
    return prev, nxt


def _shift_rows(u, above, below):
    tl = u.shape[0]
    row = lax.broadcasted_iota(jnp.int32, u.shape, 0)
    dn = jnp.where(row == 0, above, pltpu.roll(u, 1, 0))
    up = jnp.where(row == tl - 1, below, pltpu.roll(u, tl - 1, 0))
    return dn, up


def _rope_tables(L):
    pos = np.arange(L)
    row = (pos // GRID_W).astype(np.float32)
    col = (pos % GRID_W).astype(np.float32)
    nf = DK // 4
    inv = np.float32(ROPE_BASE) ** (-np.arange(nf, dtype=np.float32) / np.float32(nf))
    ang = np.concatenate([row[:, None] * inv, col[:, None] * inv], axis=-1).astype(np.float32)
    cos, sin = np.cos(ang), np.sin(ang)
    return (jnp.asarray(np.concatenate([cos, cos, cos, cos], axis=-1), F32),
            jnp.asarray(np.concatenate([-sin, sin, -sin, sin], axis=-1), F32))


def _smem_spec():
    return pl.BlockSpec(memory_space=pltpu.SMEM)


def _pair_select(e0, e1):
    row = lax.broadcasted_iota(jnp.int32, e0.shape, 0)
    return jnp.where(row < DK, e0, e1)


def _head_lane_mask(shape, e):
    lane = lax.broadcasted_iota(jnp.int32, shape, 1)
    return (lane < DK) if e == 0 else (lane >= DK)


def _ctx_states(pc, pqk_c, lg, D):
    Lc = pc.shape[0]
    H = D // DV

    def body(lg_ref, k_ref, v_ref, s_ref):
        m = lax.broadcasted_iota(jnp.int32, (Lc, DV), 0).astype(F32)
        for pr in range(H // 2):
            k2 = k_ref[:, pr * 128:(pr + 1) * 128].astype(F32) * K_SCALE
            res = [[None, None], [None, None]]
            for e in range(2):
                h = 2 * pr + e
                v = v_ref[:, h * DV:(h + 1) * DV]
                dec_f = jnp.exp(lg_ref[0, h] * (Lc - 1.0 - m))
                dec_b = jnp.exp(lg_ref[1, h] * m)
                res[0][e] = _dot_tn((k2 * dec_f).astype(BF16), v)
                res[1][e] = _dot_tn((k2 * dec_b).astype(BF16), v)
            s_ref[0, pr] = _pair_select(res[0][0], res[0][1])
            s_ref[1, pr] = _pair_select(res[1][0], res[1][1])

    return pl.pallas_call(
        body, name="ctx_states", grid=(1,),
        in_specs=[_smem_spec(), pl.BlockSpec((Lc, D // 2), lambda i: (0, 1)), pl.BlockSpec((Lc, D), lambda i: (0, 1))],
        out_specs=pl.BlockSpec((2, H // 2, 128, 128), lambda i: (0, 0, 0, 0)),
        out_shape=jax.ShapeDtypeStruct((2, H // 2, 128, 128), F32),
    )(lg, pqk_c, pc)


T_M, T_MT = 0, 1
T_MF1, T_MB1 = 2, 3
T_QF, T_QB = 4, 5
T_KF, T_KB = 6, 7


def _decay_tables(lg, H):
    def body(lg_ref, t_ref):
        h = pl.program_id(0)
        lgf, lgb = lg_ref[0, h], lg_ref[1, h]
        i = lax.broadcasted_iota(jnp.int32, (CHUNK, CHUNK), 0).astype(F32)
        j = lax.broadcasted_iota(jnp.int32, (CHUNK, CHUNK), 1).astype(F32)
        d = i - j
        mf = jnp.where(d > 0, jnp.exp(lgf * jnp.maximum(d, 0.0)), 0.0)
        mb = jnp.where(d < 0, jnp.exp(lgb * jnp.maximum(-d, 0.0)), 0.0)
        mf_t = jnp.where(d < 0, jnp.exp(lgf * jnp.maximum(-d, 0.0)), 0.0)
        mb_t = jnp.where(d > 0, jnp.exp(lgb * jnp.maximum(d, 0.0)), 0.0)
        diag = jnp.where(d == 0, 2.0, 0.0)
        t_ref[0, T_M] = mf + mb + diag
        t_ref[0, T_MT] = mf_t + mb_t + diag
        t_ref[0, T_MF1] = mf * d
        t_ref[0, T_MB1] = mb * (-d)
        t_ref[0, T_QF] = jnp.exp(lgf * (i + 1.0))
        t_ref[0, T_QB] = jnp.exp(lgb * (CHUNK - i))
        t_ref[0, T_KF] = jnp.exp(lgf * (CHUNK - 1.0 - i))
        t_ref[0, T_KB] = jnp.exp(lgb * i)

    return pl.pallas_call(
        body, name="decay_tables", grid=(H,), in_specs=[_smem_spec()],
        out_specs=pl.BlockSpec((1, 8, CHUNK, CHUNK), lambda h: (h, 0, 0, 0)),
        out_shape=jax.ShapeDtypeStruct((H, 8, CHUNK, CHUNK), F32),
    )(lg)


def _tab_spec(H):
    return pl.BlockSpec((H, 8, CHUNK, CHUNK), lambda n: (0, 0, 0, 0))


def _chunk_decay(tab_ref, h):
    return tab_ref[h, T_QF, CHUNK - 1:CHUNK, :], tab_ref[h, T_QB, 0:1, :]


def _ret_states(kr, p, s0, tab, D):
    L = kr.shape[0]
    H = D // DV
    N = L // CHUNK
    HP = H // 2

    def body(tab_ref, kf_ref, kb_ref, vf_ref, vb_ref, s0_ref, sf_out, sb_out, sf, sb):
        n = pl.program_id(0)

        @pl.when(n == 0)
        def _():
            sf[...] = s0_ref[0]
            sb[...] = s0_ref[1]

        for cc in range(RET_CPB):
            cf_, cb_ = cc, RET_CPB - 1 - cc
            rf, rb = slice(cf_ * CHUNK, (cf_ + 1) * CHUNK), slice(cb_ * CHUNK, (cb_ + 1) * CHUNK)
            sf_out[cf_] = sf[...]
            sb_out[cb_] = sb[...]
            for pr in range(HP):
                kf2 = kf_ref[rf, pr * 128:(pr + 1) * 128].astype(F32)
                kb2 = kb_ref[rb, pr * 128:(pr + 1) * 128].astype(F32)
                inc_f, inc_b, gf, gb = [], [], [], []
                for e in range(2):
                    h = 2 * pr + e
                    inc_f.append(_dot_tn((kf2 * tab_ref[h, T_KF]).astype(BF16), vf_ref[rf, h * DV:(h + 1) * DV]))
                    inc_b.append(_dot_tn((kb2 * tab_ref[h, T_KB]).astype(BF16), vb_ref[rb, h * DV:(h + 1) * DV]))
                    cf, cb = _chunk_decay(tab_ref, h)
                    gf.append(jnp.broadcast_to(cf, (128, 128)))
                    gb.append(jnp.broadcast_to(cb, (128, 128)))
                sf[pr] = _pair_select(gf[0], gf[1]) * sf[pr] + _pair_select(inc_f[0], inc_f[1])
                sb[pr] = _pair_select(gb[0], gb[1]) * sb[pr] + _pair_select(inc_b[0], inc_b[1])

    st = jax.ShapeDtypeStruct((N, HP, 128, 128), F32)
    R = RET_CPB * CHUNK
    NB = N // RET_CPB
    return _riding_call(
        body, None, NB, name="ret_states", args=(tab, kr, kr, p, p, s0),
        in_specs=[_tab_spec(H),
                  pl.BlockSpec((R, D // 2), lambda n: (n, 0)),
                  pl.BlockSpec((R, D // 2), lambda n: (NB - 1 - n, 0)),
                  pl.BlockSpec((R, D), lambda n: (n, 5)),
                  pl.BlockSpec((R, D), lambda n: (NB - 1 - n, 5)),
                  pl.BlockSpec((2, HP, 128, 128), lambda n: (0, 0, 0, 0))],
        out_specs=[pl.BlockSpec((RET_CPB, HP, 128, 128), lambda n: (n, 0, 0, 0)),
                   pl.BlockSpec((RET_CPB, HP, 128, 128), lambda n: (NB - 1 - n, 0, 0, 0))],
        out_shape=[st, st],
        scratch=[pltpu.VMEM((HP, 128, 128), F32), pltpu.VMEM((HP, 128, 128), F32)],
        cparams=_cparams(("arbitrary",)))


def _ret_out(qr, kr, p, sf_prev, sb_prev, gn_w, tab, D):
    L = qr.shape[0]
    H = D // DV
    N = L // CHUNK
    HP = H // 2

    def body(tab_ref, q_ref, k_ref, v_ref, zb_ref, sf_ref, sb_ref, gn_ref, o_ref, yb_ref):
        def chunk(cc, carry):
            rows = pl.ds(pl.multiple_of(cc * CHUNK, CHUNK), CHUNK)
            for pr in range(HP):
                q2 = q_ref[rows, pr * 128:(pr + 1) * 128]
                k2 = k_ref[rows, pr * 128:(pr + 1) * 128]
                sfp = sf_ref[cc, pr].astype(BF16)
                sbp = sb_ref[cc, pr].astype(BF16)
                for e in range(2):
                    h = 2 * pr + e
                    sl = slice(h * DV, (h + 1) * DV)
                    qm = jnp.where(_head_lane_mask(q2.shape, e), q2, jnp.zeros_like(q2))
                    a = (_dot_nt(qm, k2) * tab_ref[h, T_M]).astype(BF16)
                    qf = qm.astype(F32)
                    o = _dot(a, v_ref[rows, sl])
                    o += _dot((qf * tab_ref[h, T_QF]).astype(BF16), sfp)
                    o += _dot((qf * tab_ref[h, T_QB]).astype(BF16), sbp)
                    o_ref[rows, sl] = o
                    mu = jnp.mean(o, axis=-1, keepdims=True)
                    oc = o - mu
                    rstd = lax.rsqrt(jnp.mean(oc * oc, axis=-1, keepdims=True) + EPS)
                    zb = zb_ref[rows, sl].astype(F32)
                    yb_ref[rows, sl] = (zb * _sigmoid(zb) * (oc * rstd * gn_ref[:, sl])).astype(BF16)
            return carry

        lax.fori_loop(0, RET_CPB, chunk, 0)

    R = RET_CPB * CHUNK
    return _riding_call(
        body, None, N // RET_CPB, name="ret_out", args=(tab, qr, kr, p, p, sf_prev, sb_prev, gn_w),
        in_specs=[_tab_spec(H),
                  pl.BlockSpec((R, D // 2), lambda n: (n, 0)),
                  pl.BlockSpec((R, D // 2), lambda n: (n, 0)),
                  pl.BlockSpec((R, D), lambda n: (n, 5)),
                  pl.BlockSpec((R, D), lambda n: (n, 6)),
                  pl.BlockSpec((RET_CPB, HP, 128, 128), lambda n: (n, 0, 0, 0)),
                  pl.BlockSpec((RET_CPB, HP, 128, 128), lambda n: (n, 0, 0, 0)),
                  _vec_spec(D)],
        out_specs=[pl.BlockSpec((R, D), lambda n: (n, 0)), pl.BlockSpec((R, D), lambda n: (n, 0))],
        out_shape=[jax.ShapeDtypeStruct((L, D), F32), jax.ShapeDtypeStruct((L, D), BF16)],
        cparams=_cparams(("arbitrary",)))


def _mid(p, yb, o, x, tgt, w3, g, fw, conv_w, conv_b, gn_w, D):
    L = x.shape[0]
    H = D // DV
    tm = min(256, L)
    nt = L // tm

    def body(h_ref, bg_ref, cg_ref, za_ref, hp_ref, hn_ref, cp_ref, cn_ref, yb_ref, ga_ref, gb_ref, zb_ref, o_ref,
             x_ref, t_ref, w_hbm, g_ref, fw_ref, cw_ref, cb_ref, gn_ref,
             dx1_ref, dya_ref, do_ref, dzb_ref, dgab_ref, dw_hbm, st_ref, w_vm, dw_acc, sem):
        i = pl.program_id(0)

        @pl.when(i == 0)
        def _():
            cp = pltpu.make_async_copy(w_hbm, w_vm, sem)
            cp.start()
            dw_acc[...] = jnp.zeros_like(dw_acc)
            st_ref[...] = jnp.zeros_like(st_ref)
            cp.wait()

        u = cg_ref[...].astype(F32) * h_ref[...].astype(F32)
        above = jnp.where(i == 0, 0.0, cp_ref[15:16, :].astype(F32) * hp_ref[15:16, :].astype(F32))
        below = jnp.where(i == nt - 1, 0.0, cn_ref[0:1, :].astype(F32) * hn_ref[0:1, :].astype(F32))
        dn, up = _shift_rows(u, above, below)
        co = cw_ref[0:1, :] * dn + cw_ref[1:2, :] * u + cw_ref[2:3, :] * up + cb_ref[...]
        za = za_ref[...].astype(F32)
        ya_b = (za * _sigmoid(za) * bg_ref[...].astype(F32) * co).astype(BF16)
        yb_b = yb_ref[...]
        y_a = _dot(ya_b, w_vm[0])
        y_b = _dot(yb_b, w_vm[1])
        sga = _sigmoid(ga_ref[...].astype(F32))
        sgb = _sigmoid(gb_ref[...].astype(F32))
        mix_b = (sga * y_a + sgb * y_b).astype(BF16)
        y_x = _dot(mix_b, w_vm[2])
        gvec, fwv = g_ref[...], fw_ref[...]
        x1 = x_ref[...] + gvec * y_x
        r1 = lax.rsqrt(jnp.mean(x1 * x1, axis=-1, keepdims=True) + EPS)
        xh = x1 * r1
        diff = xh * fwv - t_ref[...]
        dout = diff * (1.0 / D)
        dxh = dout * fwv
        dx1 = r1 * (dxh - xh * jnp.mean(dxh * xh, axis=-1, keepdims=True))
        dx1_ref[...] = dx1
        st_ref[0:1, :] += jnp.sum(dout * xh, axis=0, keepdims=True)
        st_ref[1:2, :] += jnp.sum(dx1 * y_x, axis=0, keepdims=True)
        st_ref[2:3, :] += jnp.sum(diff * diff, axis=0, keepdims=True)
        dyx_b = (dx1 * gvec).astype(BF16)
        dmix = _dot_nt(dyx_b, w_vm[2])
        dw_acc[2] += _dot_tn(mix_b, dyx_b)
        dya_b = (dmix * sga).astype(BF16)
        dyb_b = (dmix * sgb).astype(BF16)
        dgab_ref[:, 0:D] = (dmix * y_a * sga * (1.0 - sga)).astype(BF16)
        dgab_ref[:, D:2 * D] = (dmix * y_b * sgb * (1.0 - sgb)).astype(BF16)
        dya_ref[...] = _dot_nt(dya_b, w_vm[0])
        dyb = _dot_nt(dyb_b, w_vm[1])
        dw_acc[0] += _dot_tn(ya_b, dya_b)
        dw_acc[1] += _dot_tn(yb_b, dyb_b)

        for h in range(H):
            sl = slice(h * DV, (h + 1) * DV)
            ov = o_ref[:, sl]
            oc = ov - jnp.mean(ov, axis=-1, keepdims=True)
            rstd = lax.rsqrt(jnp.mean(oc * oc, axis=-1, keepdims=True) + EPS)
            rn = oc * rstd
            gw = gn_ref[:, sl]
            zb = zb_ref[:, sl].astype(F32)
            sz = _sigmoid(zb)
            dy = dyb[:, sl]
            dzb_ref[:, sl] = (dy * (rn * gw) * (sz * (1.0 + zb * (1.0 - sz)))).astype(BF16)
            dretn = dy * (zb * sz)
            st_ref[3:4, sl] += jnp.sum(dretn * rn, axis=0, keepdims=True)
            drn = dretn * gw
            do_ref[:, sl] = (rstd * (drn - jnp.mean(drn, axis=-1, keepdims=True)
                                     - rn * jnp.mean(drn * rn, axis=-1, keepdims=True))).astype(BF16)

        @pl.when(i == nt - 1)
        def _():
            out = pltpu.make_async_copy(dw_acc, dw_hbm, sem)
            out.start()
            out.wait()

    row = lambda col: pl.BlockSpec((tm, D), lambda i: (i, col))
    any_spec = pl.BlockSpec(memory_space=pl.ANY)
    f32o = jax.ShapeDtypeStruct((L, D), F32)
    bf16o = jax.ShapeDtypeStruct((L, D), BF16)
    hp, hn = _halo_specs(tm, L, D, 0)
    cp, cn = _halo_specs(tm, L, D, 2)
    return pl.pallas_call(
        body, name="mid", grid=(nt,),
        in_specs=[row(0), row(1), row(2), row(3), hp, hn, cp, cn, row(0), row(7), row(8), row(6), row(0),
                  row(0), row(0), any_spec, _vec_spec(D), _vec_spec(D),
                  pl.BlockSpec((8, D), lambda i: (0, 0)), _vec_spec(D), _vec_spec(D)],
        out_specs=[row(0), row(0), row(0), row(0), pl.BlockSpec((tm, 2 * D), lambda i: (i, 0)), any_spec,
                   pl.BlockSpec((8, D), lambda i: (0, 0))],
        out_shape=[f32o, f32o, bf16o, bf16o, jax.ShapeDtypeStruct((L, 2 * D), BF16),
                   jax.ShapeDtypeStruct((3, D, D), F32), jax.ShapeDtypeStruct((8, D), F32)],
        scratch_shapes=[pltpu.VMEM((3, D, D), BF16), pltpu.VMEM((3, D, D), F32), pltpu.SemaphoreType.DMA],
        compiler_params=_cparams(("arbitrary",), VMEM_LIMIT),
    )(p, p, p, p, p, p, p, p, yb, p, p, p, o, x, tgt, w3, g, fw, conv_w, conv_b, gn_w)


def _conv_bwd(dya, p, conv_w, conv_b, D, exchange=None):
    L = p.shape[0]
    tl = min(256, L)
    nt = L // tl

    def body(d_ref, h_ref, bg_ref, cg_ref, za_ref,
             dp_ref, dn_ref, hp_ref, hn_ref, bp_ref, bn_ref, cp_ref, cn_ref, zp_ref, zn_ref,
             w_ref, b_ref, dc_ref, st_ref):
        i = pl.program_id(0)

        @pl.when(i == 0)
        def _():
            st_ref[...] = jnp.zeros_like(st_ref)

        first, last = i == 0, i == nt - 1
        h = h_ref[...].astype(F32)
        cg = cg_ref[...].astype(F32)
        bg = bg_ref[...].astype(F32)
        za = za_ref[...].astype(F32)
        dy = d_ref[...].astype(F32)
        u = cg * h
        u_above = jnp.where(first, 0.0, cp_ref[15:16, :].astype(F32) * hp_ref[15:16, :].astype(F32))
        u_below = jnp.where(last, 0.0, cn_ref[0:1, :].astype(F32) * hn_ref[0:1, :].astype(F32))
        u_dn, u_up = _shift_rows(u, u_above, u_below)
        w0, w1, w2 = w_ref[0:1, :], w_ref[1:2, :], w_ref[2:3, :]
        co = w0 * u_dn + w1 * u + w2 * u_up + b_ref[...]
        sz = _sigmoid(za)
        silu = za * sz
        dc_ref[:, 3 * D:4 * D] = (dy * bg * co * (sz * (1.0 + za * (1.0 - sz)))).astype(BF16)
        dc_ref[:, D:2 * D] = (dy * silu * co).astype(BF16)
        dco = dy * silu * bg

        def edge(dr, zr, br, r):
            z = zr[r:r + 1, :].astype(F32)
            return dr[r:r + 1, :].astype(F32) * (z * _sigmoid(z)) * br[r:r + 1, :].astype(F32)

        dco_above = jnp.where(first, 0.0, edge(dp_ref, zp_ref, bp_ref, 15))
        dco_below = jnp.where(last, 0.0, edge(dn_ref, zn_ref, bn_ref, 0))
        dco_dn, dco_up = _shift_rows(dco, dco_above, dco_below)
        du = w0 * dco_up + w1 * dco + w2 * dco_dn
        dc_ref[:, 2 * D:3 * D] = (du * h).astype(BF16)
        dc_ref[:, 0:D] = (du * cg).astype(BF16)
        st_ref[0:1, :] += jnp.sum(dco * u_dn, axis=0, keepdims=True)
        st_ref[1:2, :] += jnp.sum(dco * u, axis=0, keepdims=True)
        st_ref[2:3, :] += jnp.sum(dco * u_up, axis=0, keepdims=True)
        st_ref[3:4, :] += jnp.sum(dco, axis=0, keepdims=True)

    main = lambda col: pl.BlockSpec((tl, D), lambda i: (i, col))
    halos = []
    for col in (0, 0, 1, 2, 3):
        halos.extend(_halo_specs(tl, L, D, col))
    return _riding_call(
        body, exchange, nt, name="conv_bwd",
        args=(dya, p, p, p, p, dya, dya, p, p, p, p, p, p, p, p, conv_w, conv_b),
        in_specs=[main(0), main(0), main(1), main(2), main(3)] + halos
                 + [pl.BlockSpec((8, D), lambda i: (0, 0)), _vec_spec(D)],
        out_specs=[pl.BlockSpec((tl, 4 * D), lambda i: (i, 0)), pl.BlockSpec((8, D), lambda i: (0, 0))],
        out_shape=[jax.ShapeDtypeStruct((L, 4 * D), BF16), jax.ShapeDtypeStruct((8, D), F32)],
        cparams=_cparams(("arbitrary",)))


def _ret_bwd_states(qr, do, tab, D):
    L = qr.shape[0]
    H = D // DV
    N = L // CHUNK
    HP = H // 2

    def body(tab_ref, qf_ref, qb_ref, dof_ref, dob_ref, dsf_out, dsb_out, ds0_out, dsf, dsb):
        n = pl.program_id(0)

        @pl.when(n == 0)
        def _():
            dsf[...] = jnp.zeros_like(dsf)
            dsb[...] = jnp.zeros_like(dsb)

        for cc in range(RET_CPB):
            cf_, cb_ = RET_CPB - 1 - cc, cc
            rf, rb = slice(cf_ * CHUNK, (cf_ + 1) * CHUNK), slice(cb_ * CHUNK, (cb_ + 1) * CHUNK)
            dsf_out[cf_] = dsf[...]
            dsb_out[cb_] = dsb[...]
            for pr in range(HP):
                qf2 = qf_ref[rf, pr * 128:(pr + 1) * 128].astype(F32)
                qb2 = qb_ref[rb, pr * 128:(pr + 1) * 128].astype(F32)
                inc_f, inc_b, gf, gb = [], [], [], []
                for e in range(2):
                    h = 2 * pr + e
                    inc_f.append(_dot_tn((qf2 * tab_ref[h, T_QF]).astype(BF16), dof_ref[rf, h * DV:(h + 1) * DV]))
                    inc_b.append(_dot_tn((qb2 * tab_ref[h, T_QB]).astype(BF16), dob_ref[rb, h * DV:(h + 1) * DV]))
                    cf, cb = _chunk_decay(tab_ref, h)
                    gf.append(jnp.broadcast_to(cf, (128, 128)))
                    gb.append(jnp.broadcast_to(cb, (128, 128)))
                dsf[pr] = _pair_select(gf[0], gf[1]) * dsf[pr] + _pair_select(inc_f[0], inc_f[1])
                dsb[pr] = _pair_select(gb[0], gb[1]) * dsb[pr] + _pair_select(inc_b[0], inc_b[1])

        @pl.when(n == NB - 1)
        def _():
            ds0_out[0] = dsf[...]
            ds0_out[1] = dsb[...]

    st = jax.ShapeDtypeStruct((N, HP, 128, 128), F32)
    R = RET_CPB * CHUNK
    NB = N // RET_CPB
    return pl.pallas_call(
        body, name="ret_bwd_states", grid=(NB,),
        in_specs=[_tab_spec(H),
                  pl.BlockSpec((R, D // 2), lambda n: (NB - 1 - n, 0)),
                  pl.BlockSpec((R, D // 2), lambda n: (n, 0)),
                  pl.BlockSpec((R, D), lambda n: (NB - 1 - n, 0)),
                  pl.BlockSpec((R, D), lambda n: (n, 0))],
        out_specs=[pl.BlockSpec((RET_CPB, HP, 128, 128), lambda n: (NB - 1 - n, 0, 0, 0)),
                   pl.BlockSpec((RET_CPB, HP, 128, 128), lambda n: (n, 0, 0, 0)),
                   pl.BlockSpec((2, HP, 128, 128), lambda n: (0, 0, 0, 0))],
        out_shape=[st, st, jax.ShapeDtypeStruct((2, HP, 128, 128), F32)],
        scratch_shapes=[pltpu.VMEM((HP, 128, 128), F32), pltpu.VMEM((HP, 128, 128), F32)],
        compiler_params=_cparams(("arbitrary",)),
    )(tab, qr, qr, do, do)


def _ret_bwd_main(qr, kr, p, do, sf_prev, sb_prev, dsf, dsb, c2, s2, tab, D, exchange=None):
    L = qr.shape[0]
    H = D // DV
    N = L // CHUNK
    HP = H // 2
    W = D // 2

    def body(tab_ref, q_ref, k_ref, v_ref, do_ref, sf_ref, sb_ref, dsf_ref, dsb_ref, c_ref, s_ref,
             dr_ref, st_ref, dl_acc):
        @pl.when(pl.program_id(0) == 0)
        def _():
            dl_acc[...] = jnp.zeros_like(dl_acc)

        i = lax.broadcasted_iota(jnp.int32, (CHUNK, 128), 0).astype(F32)
        rowid = lax.broadcasted_iota(jnp.int32, (128, 128), 0)

        def chunk(cc, carry):
            rows = pl.ds(pl.multiple_of(cc * CHUNK, CHUNK), CHUNK)
            c, s = c_ref[rows, :], s_ref[rows, :]
            for pr in range(HP):
                ps = slice(pr * 128, (pr + 1) * 128)
                q2, k2 = q_ref[rows, ps], k_ref[rows, ps]
                sf32, sb32 = sf_ref[cc, pr], sb_ref[cc, pr]
                dsf32, dsb32 = dsf_ref[cc, pr], dsb_ref[cc, pr]
                sfp, sbp = sf32.astype(BF16), sb32.astype(BF16)
                dsfp, dsbp = dsf32.astype(BF16), dsb32.astype(BF16)
                dq2 = jnp.zeros((CHUNK, 128), F32)
                dk2 = jnp.zeros((CHUNK, 128), F32)
                for e in range(2):
                    h = 2 * pr + e
                    sl = slice(h * DV, (h + 1) * DV)
                    hm = _head_lane_mask(q2.shape, e)
                    qm = jnp.where(hm, q2, jnp.zeros_like(q2))
                    km = jnp.where(hm, k2, jnp.zeros_like(k2))
                    qf, kf = qm.astype(F32), km.astype(F32)
                    v, do = v_ref[rows, sl], do_ref[rows, sl]
                    vf, dof = v.astype(F32), do.astype(F32)
                    m_t = tab_ref[h, T_MT]
                    sc = _dot_nt(qm, k2)
                    dpm = _dot_nt(do, v)
                    dsc = (dpm * tab_ref[h, T_M]).astype(BF16)
                    a_t = (_dot_nt(km, q2) * m_t).astype(BF16)
                    dsc_t = (_dot_nt(v, do) * m_t).astype(BF16)
                    dq_f, dq_b = tab_ref[h, T_QF], tab_ref[h, T_QB]
                    dk_f, dk_b = tab_ref[h, T_KF], tab_ref[h, T_KB]
                    dq = _dot(dsc, km)
                    dq += jnp.where(hm, dq_f * _dot_nt(do, sfp) + dq_b * _dot_nt(do, sbp), 0.0)
                    dk = _dot(dsc_t, qm)
                    dk += jnp.where(hm, dk_f * _dot_nt(v, dsfp) + dk_b * _dot_nt(v, dsbp), 0.0)
                    kdf = _dot((kf * dk_f).astype(BF16), dsfp)
                    kdb = _dot((kf * dk_b).astype(BF16), dsbp)
                    dr_ref[rows, D + h * DV:D + (h + 1) * DV] = (_dot(a_t, do) + kdf + kdb).astype(BF16)
                    dq2 += dq
                    dk2 += dk
                    xf = _dot((qf * dq_f).astype(BF16), sfp)
                    xb = _dot((qf * dq_b).astype(BF16), sbp)
                    pair = (rowid < DK) if e == 0 else (rowid >= DK)
                    gcf, gcb = tab_ref[h, T_QF, CHUNK - 1:CHUNK, 0:1], tab_ref[h, T_QB, 0:1, 0:1]
                    scdp = sc * dpm
                    dl_acc[h, 0] += scdp * tab_ref[h, T_MF1] + xf * dof * (i + 1.0) \
                        + kdf * vf * (CHUNK - 1.0 - i) + (CHUNK * gcf) * jnp.where(pair, dsf32 * sf32, 0.0)
                    dl_acc[h, 1] += scdp * tab_ref[h, T_MB1] + xb * dof * (CHUNK - i) \
                        + kdb * vf * i + (CHUNK * gcb) * jnp.where(pair, dsb32 * sb32, 0.0)
                dr_ref[rows, ps] = (dq2 * c - _swap_halves(dq2) * s).astype(BF16)
                dr_ref[rows, W + pr * 128:W + (pr + 1) * 128] = \
                    ((dk2 * c - _swap_halves(dk2) * s) * K_SCALE).astype(BF16)
            return carry

        lax.fori_loop(0, RET_CPB, chunk, 0)

        @pl.when(pl.program_id(0) == N // RET_CPB - 1)
        def _():
            lane = lax.broadcasted_iota(jnp.int32, (1, 128), 1)
            acc = [jnp.zeros((1, 128), F32), jnp.zeros((1, 128), F32)]
            for h in range(H):
                for b in range(2):
                    acc[b] += jnp.where(lane == h, _sum_all(dl_acc[h, b]), 0.0)
            st_ref[...] = jnp.zeros_like(st_ref)
            st_ref[0:1, :] = acc[0]
            st_ref[1:2, :] = acc[1]

    R = RET_CPB * CHUNK
    st_spec = pl.BlockSpec((RET_CPB, HP, 128, 128), lambda n: (n, 0, 0, 0))
    half = pl.BlockSpec((R, W), lambda n: (n, 0))
    rope = pl.BlockSpec((R, 128), lambda n: (n, 0))
    return _riding_call(
        body, exchange, N // RET_CPB, name="ret_bwd_main",
        args=(tab, qr, kr, p, do, sf_prev, sb_prev, dsf, dsb, c2, s2),
        in_specs=[_tab_spec(H), half, half,
                  pl.BlockSpec((R, D), lambda n: (n, 5)),
                  pl.BlockSpec((R, D), lambda n: (n, 0)),
                  st_spec, st_spec, st_spec, st_spec, rope, rope],
        out_specs=[pl.BlockSpec((R, 2 * D), lambda n: (n, 0)),
                   pl.BlockSpec((8, 128), lambda n: (0, 0))],
        out_shape=[jax.ShapeDtypeStruct((L, 2 * D), BF16), jax.ShapeDtypeStruct((8, 128), F32)],
        scratch=[pltpu.VMEM((H, 2, CHUNK, 128), F32)],
        cparams=_cparams(("arbitrary",)))


def _ctx_bwd(pc, pqk_c, ds0, lg, D):
    Lc = pc.shape[0]
    H = D // DV
    HP = H // 2
    W = D // 2

    def body(lg_ref, k_ref, v_ref, ds_ref, dr_ref, st_ref):
        dqk_ref = dr_ref.at[:, 0:D]
        dv_ref = dr_ref.at[:, D:2 * D]
        m = lax.broadcasted_iota(jnp.int32, (Lc, 128), 0).astype(F32)
        lane = lax.broadcasted_iota(jnp.int32, (1, 128), 1)
        acc_f = jnp.zeros((1, 128), F32)
        acc_b = jnp.zeros((1, 128), F32)
        dqk_ref[:, 0:W] = jnp.zeros((Lc, W), BF16)
        for pr in range(HP):
            ps = slice(pr * 128, (pr + 1) * 128)
            k2 = k_ref[:, ps].astype(F32) * K_SCALE
            dsfp, dsbp = ds_ref[0, pr].astype(BF16), ds_ref[1, pr].astype(BF16)
            dk2 = jnp.zeros((Lc, 128), F32)
            for e in range(2):
                h = 2 * pr + e
                sl = slice(h * DV, (h + 1) * DV)
                hm = _head_lane_mask(k2.shape, e)
                km = jnp.where(hm, k2, 0.0)
                v = v_ref[:, sl]
                vf = v.astype(F32)
                dec_f = jnp.exp(lg_ref[0, h] * (Lc - 1.0 - m))
                dec_b = jnp.exp(lg_ref[1, h] * m)
                kdf = _dot((km * dec_f).astype(BF16), dsfp)
                kdb = _dot((km * dec_b).astype(BF16), dsbp)
                dv_ref[:, sl] = (kdf + kdb).astype(BF16)
                dk2 += jnp.where(hm, dec_f * _dot_nt(v, dsfp) + dec_b * _dot_nt(v, dsbp), 0.0)
                acc_f += jnp.where(lane == h, _sum_all(kdf * vf * (Lc - 1.0 - m)), 0.0)
                acc_b += jnp.where(lane == h, _sum_all(kdb * vf * m), 0.0)
            dqk_ref[:, W + pr * 128:W + (pr + 1) * 128] = (dk2 * K_SCALE).astype(BF16)
        st_ref[...] = jnp.zeros_like(st_ref)
        st_ref[0:1, :] = acc_f
        st_ref[1:2, :] = acc_b

    return pl.pallas_call(
        body, name="ctx_bwd", grid=(1,),
        in_specs=[_smem_spec(), pl.BlockSpec((Lc, W), lambda i: (0, 1)), pl.BlockSpec((Lc, D), lambda i: (0, 1)),
                  pl.BlockSpec((2, HP, 128, 128), lambda i: (0, 0, 0, 0))],
        out_specs=[pl.BlockSpec((Lc, 2 * D), lambda i: (0, 0)), pl.BlockSpec((8, 128), lambda i: (0, 0))],
        out_shape=[jax.ShapeDtypeStruct((Lc, 2 * D), BF16), jax.ShapeDtypeStruct((8, 128), F32)],
    )(lg, pqk_c, pc, ds0)


class _Exchange(NamedTuple):
    inputs: tuple
    out_shapes: tuple
    n_copies: int
    build: Callable


def _exchange_parts(exchange):
    if exchange is None:
        return [], [], [], [], []
    n = exchange.n_copies
    return (list(exchange.inputs), [ANY] * len(exchange.inputs), list(exchange.out_shapes),
            [ANY] * len(exchange.out_shapes), [pltpu.SemaphoreType.DMA((n,)), pltpu.SemaphoreType.DMA((n,))])


def _riding_call(body, exchange, n_steps, *, args, in_specs, out_specs, out_shape, name, cparams, scratch=()):
    ex_args, ex_in_specs, ex_shapes, ex_out_specs, ex_scratch = _exchange_parts(exchange)
    n_in, n_out, n_sc = len(args), len(out_shape), len(scratch)

    def riding(*refs):
        k = n_in + len(ex_args)
        ins, ex_in = refs[:n_in], refs[n_in:k]
        outs, ex_out = refs[k:k + n_out], refs[k + n_out:k + n_out + len(ex_shapes)]
        k += n_out + len(ex_shapes)
        own_scratch, ex_sems = refs[k:k + n_sc], refs[k + n_sc:]
        step = pl.program_id(0)
        if exchange is not None:
            @pl.when(step == 0)
            def _():
                for rc in exchange.build(ex_in, ex_out, *ex_sems):
                    rc.start()
        body(*ins, *outs, *own_scratch)
        if exchange is not None:
            @pl.when(step == n_steps - 1)
            def _():
                for rc in exchange.build(ex_in, ex_out, *ex_sems):
                    rc.wait()

    return tuple(pl.pallas_call(
        riding, name=name, grid=(n_steps,),
        in_specs=list(in_specs) + ex_in_specs, out_specs=list(out_specs) + ex_out_specs,
        out_shape=list(out_shape) + ex_shapes, scratch_shapes=list(scratch) + ex_scratch,
        compiler_params=cparams,
    )(*args, *ex_args))


def _dxm(groups, col0, w, x, nw, sc, dx1, name, exchange=None):
    L, D = x.shape
    tm = min(256, L)
    nt = L // tm
    ng = len(groups)
    widths = [g.shape[1] for g in groups]
    wtot = sum(widths)
    with_dx = dx1 is not None
    ex_args, ex_in_specs, ex_shapes, ex_out_specs, ex_scratch = _exchange_parts(exchange)
    n_in = ng + 4 + (1 if with_dx else 0)
    n_out = 2 if with_dx else 1

    def body(*refs):
        group_refs = refs[:ng]
        w_hbm, x_ref, nw_ref, sc_ref = refs[ng:ng + 4]
        ex_in = refs[n_in:n_in + len(ex_args)]
        outs = refs[n_in + len(ex_args):]
        if with_dx:
            dx1_ref, gx_ref, st_ref = refs[ng + 4], outs[0], outs[1]
        else:
            st_ref = outs[0]
        ex_out = outs[n_out:n_out + len(ex_shapes)]
        w_vm, sem = outs[n_out + len(ex_shapes):n_out + len(ex_shapes) + 2]
        ex_sems = outs[n_out + len(ex_shapes) + 2:]
        i = pl.program_id(0)

        @pl.when(i == 0)
        def _():
            cp = pltpu.make_async_copy(w_hbm.at[:, col0 * D:col0 * D + wtot], w_vm, sem)
            cp.start()
            if exchange is not None:
                for rc in exchange.build(ex_in, ex_out, *ex_sems):
                    rc.start()
            st_ref[...] = jnp.zeros_like(st_ref)
            cp.wait()

        dxm, off = None, 0
        for g_ref, wd in zip(group_refs, widths):
            part = _dot_nt(g_ref[...], w_vm[:, off:off + wd])
            dxm = part if dxm is None else dxm + part
            off += wd

        xv = x_ref[...]
        r = lax.rsqrt(jnp.mean(xv * xv, axis=-1, keepdims=True) + EPS)
        xh = xv * r
        nwv = nw_ref[...]
        dxn = dxm * (1.0 + sc_ref[...])
        st_ref[0:1, :] += jnp.sum(dxm, axis=0, keepdims=True)
        st_ref[1:2, :] += jnp.sum(dxm * (xh * nwv), axis=0, keepdims=True)
        st_ref[2:3, :] += jnp.sum(dxn * xh, axis=0, keepdims=True)
        if with_dx:
            dxh = dxn * nwv
            gx_ref[...] = dx1_ref[...] + r * (dxh - xh * jnp.mean(dxh * xh, axis=-1, keepdims=True))

        if exchange is not None:
            @pl.when(i == nt - 1)
            def _():
                for rc in exchange.build(ex_in, ex_out, *ex_sems):
                    rc.wait()

    row = pl.BlockSpec((tm, D), lambda i: (i, 0))
    in_specs = [pl.BlockSpec((tm, wd), lambda i: (i, 0)) for wd in widths] + [ANY, row, _vec_spec(D), _vec_spec(D)]
    out_specs = [pl.BlockSpec((8, D), lambda i: (0, 0))]
    out_shape = [jax.ShapeDtypeStruct((8, D), F32)]
    args = list(groups) + [w, x, nw, sc]
    if with_dx:
        in_specs.append(row)
        out_specs.insert(0, row)
        out_shape.insert(0, jax.ShapeDtypeStruct((L, D), F32))
        args.append(dx1)
    res = pl.pallas_call(
        body, name=name, grid=(nt,),
        in_specs=in_specs + ex_in_specs, out_specs=out_specs + ex_out_specs, out_shape=out_shape + ex_shapes,
        scratch_shapes=[pltpu.VMEM((D, wtot), BF16), pltpu.SemaphoreType.DMA] + ex_scratch,
        compiler_params=_cparams(("arbitrary",), VMEM_LIMIT),
    )(*args, *ex_args)
    gx = res[0] if with_dx else None
    return (gx, res[n_out - 1], *res[n_out:])


DW_TN = 512
DW_RING = 4


def _dw_in(xmt, groups, cmt, dr_c, D, pair):
    L = xmt.shape[1]
    Lc = cmt.shape[1]
    Dh = D // 2
    tn = min(DW_TN, D)
    nblk = [g.shape[1] // tn for g in groups]
    starts = [sum(nblk[:g]) for g in range(len(groups))]
    ng = len(groups)
    nj = sum(nblk)
    rows_out = Dh if pair else D

    def body(*refs):
        xt_hbm = refs[0]
        group_refs = refs[1:1 + ng]
        ct_hbm, drc_ref, o_ref = refs[1 + ng:4 + ng]
        rest = refs[4 + ng:]
        if pair:
            ra_hbm, xt_vm, ct_vm, loc, ring, s_send, s_recv = rest
            pos = _position()
            sib = _peer(pos, 1)
        else:
            xt_vm, ct_vm, loc = rest
        j = pl.program_id(0)

        @pl.when(j == 0)
        def _():
            if pair:
                c = pos[2]
                other = pl.ds(pl.multiple_of((1 - c) * Dh, Dh), Dh)
                mine = pl.ds(pl.multiple_of(c * Dh, Dh), Dh)
                cps = [pltpu.make_async_copy(xt_hbm.at[other, :], xt_vm.at[0:Dh, :], loc.at[0]),
                       pltpu.make_async_copy(xt_hbm.at[mine, :], xt_vm.at[Dh:D, :], loc.at[1]),
                       pltpu.make_async_copy(ct_hbm.at[other, :], ct_vm.at[0:Dh, :], loc.at[2]),
                       pltpu.make_async_copy(ct_hbm.at[mine, :], ct_vm.at[Dh:D, :], loc.at[3])]
            else:
                cps = [pltpu.make_async_copy(xt_hbm, xt_vm, loc.at[0]), pltpu.make_async_copy(ct_hbm, ct_vm, loc.at[1])]
            for cp in cps:
                cp.start()
            for cp in cps:
                cp.wait()

        def send(slot):
            cols = pl.ds(pl.multiple_of(j * tn, 128), tn)
            return pltpu.make_async_remote_copy(src_ref=ring.at[slot], dst_ref=ra_hbm.at[:, cols],
                                                send_sem=s_send.at[slot], recv_sem=s_recv,
                                                device_id=sib, device_id_type=MESH)

        for g in range(ng):
            @pl.when((j >= starts[g]) & (j < starts[g] + nblk[g]))
            def _(g=g):
                acc = _dot(xt_vm[...], group_refs[g][...])
                if g == 1:
                    acc += _dot(ct_vm[...], drc_ref[...])
                if not pair:
                    o_ref[...] = acc
                    return
                o_ref[...] = acc[Dh:, :]
                slot = lax.rem(j, DW_RING)

                @pl.when(j >= DW_RING)
                def _():
                    send(slot).wait_send()

                ring[slot] = acc[0:Dh, :]
                send(slot).start()

        if pair:
            @pl.when(j == nj - 1)
            def _():
                pltpu.make_async_remote_copy(src_ref=ra_hbm, dst_ref=ra_hbm, send_sem=s_send.at[0], recv_sem=s_recv,
                                             device_id=sib, device_id_type=MESH).wait_recv()
                for slot in range(DW_RING):
                    send(slot).wait_send()

    def group_spec(g, rows):
        return pl.BlockSpec((rows, tn), lambda j: (0, jnp.clip(j - starts[g], 0, nblk[g] - 1)))

    out_specs = [pl.BlockSpec((rows_out, tn), lambda j: (0, j))]
    out_shape = [jax.ShapeDtypeStruct((rows_out, nj * tn), F32)]
    scratch = [pltpu.VMEM((D, L), BF16), pltpu.VMEM((D, Lc), BF16), pltpu.SemaphoreType.DMA((4,))]
    if pair:
        out_specs.append(ANY)
        out_shape.append(jax.ShapeDtypeStruct((Dh, nj * tn), F32))
        scratch += [pltpu.VMEM((DW_RING, Dh, tn), F32), pltpu.SemaphoreType.DMA((DW_RING,)), pltpu.SemaphoreType.DMA]
    return tuple(pl.pallas_call(
        body, name="dw_in", grid=(nj,),
        in_specs=[ANY] + [group_spec(g, L) for g in range(ng)] + [ANY, group_spec(1, Lc)],
        out_specs=out_specs, out_shape=out_shape, scratch_shapes=scratch,
        compiler_params=_cparams(("arbitrary",), VMEM_LIMIT),
    )(xmt, *groups, cmt, dr_c))


def _local_step(x, ctx, tgt, mod_x, mod_c, norm_w, conv_w8, conv_b, lg, gn_w, fw, project, csidx=None, w_in_s=None):
    L, D = x.shape
    sh_x, sc_x, g_x = mod_x[0:1], mod_x[1:2], mod_x[2:3]
    sh_c, sc_c = mod_c[0:1], mod_c[1:2]
    c2, s2 = _rope_tables(L)
    tab = _decay_tables(lg, D // DV)

    xm, xmt, *w_bf = _norm_mod(x, norm_w, sc_x, sh_x, "norm_mod_x", w_in_s)
    cm, cmt = _norm_mod(ctx, norm_w, sc_c, sh_c, "norm_mod_ctx")
    reduce = csidx is not None
    p, qr, kr, w_in, w3 = project(xm, c2, s2, *w_bf)
    pc, pqk_c = _in_proj(cm, w_in, "in_proj_ctx", QK_BLOCK, 2)
    s0 = _ctx_states(pc, pqk_c, lg, D)
    sf_prev, sb_prev = _ret_states(kr, p, s0, tab, D)
    o, yb = _ret_out(qr, kr, p, sf_prev, sb_prev, gn_w, tab, D)
    dx1, dya, do, dzb, dgab, dw3, st_mid = _mid(p, yb, o, x, tgt, w3, g_x, fw, conv_w8, conv_b, gn_w, D)
    dw3_5 = dw3.reshape(3, N_SHARD, 2, D // 8, D)
    dconv, st_conv, *ra_3 = _conv_bwd(dya, p, conv_w8, conv_b, D, _pair_exchange_w3(dw3_5) if reduce else None)
    dsf, dsb, ds0 = _ret_bwd_states(qr, do, tab, D)
    cs_3 = _sum_pair_w3(csidx[0:1], dw3_5, ra_3[0]) if reduce else None
    dret, st_lg, *rb_3 = _ret_bwd_main(qr, kr, p, do, sf_prev, sb_prev, dsf, dsb, c2, s2, tab, D,
                                       _chips_exchange_w3(cs_3) if reduce else None)
    g_3 = _sum_chips_w3(csidx, cs_3, rb_3[0]) if reduce else dw3
    dret_c, st_lgc = _ctx_bwd(pc, pqk_c, ds0, lg, D)
    groups = (dconv, dret, dzb, dgab)
    _, st_c = _dxm((dret_c,), 4, w_in, ctx, norm_w, sc_c, None, "dxm_ctx")
    return groups, dret_c, xmt, cmt, dx1, sc_x, w_in, g_3, (st_mid, st_conv, st_lg, st_lgc, st_c)


CHIP_FLIPS = (4, 2, 6)
ANY = pl.BlockSpec(memory_space=pl.ANY)
VMEM_FULL = pl.BlockSpec(memory_space=pltpu.VMEM)


def _position():
    return lax.axis_index("x"), lax.axis_index("y"), lax.axis_index("c")


def _peer(pos, k):
    x, y, c = pos
    return (1 - x if k & 4 else x, 1 - y if k & 2 else y, 1 - c if k & 1 else c)


def _dev_id(pos):
    return 4 * pos[0] + 2 * pos[1] + pos[2]


def _shard_of(pos):
    return 2 * pos[0] + pos[1]


def _remote(src, dst, send_sems, recv_sems, idx, to):
    return pltpu.make_async_remote_copy(src_ref=src, dst_ref=dst, send_sem=send_sems.at[idx],
                                        recv_sem=recv_sems.at[idx], device_id=to, device_id_type=MESH)


def _dot_f32(a, b):
    return jnp.dot(a, b, precision=lax.Precision.HIGHEST, preferred_element_type=F32)


def _silu(x):
    return x * _sigmoid(x)


def _fwd_small(c8, cctx8, ada_w, ada_b, conv_w8):
    D = c8.shape[1]
    Wm = ada_w.shape[1]
    Dq = conv_w8.shape[1]

    def body(c_ref, cc_ref, aw_ref, ab_ref, cw_ref, act_ref, mod_ref, cwf_ref,
             cbuf, pmine, pbuf, wbuf, s_c, r_c, s_p, r_p, s_w, r_w):
        pos = _position()
        me, s = _dev_id(pos), _shard_of(pos)
        cbuf[me] = c_ref[...]
        wbuf[s] = cw_ref[...]
        sends = [_remote(c_ref, cbuf.at[me], s_c, r_c, k - 1, _peer(pos, k)) for k in range(1, 8)]
        sends += [_remote(cw_ref, wbuf.at[s], s_w, r_w, j, _peer(pos, k)) for j, k in enumerate(CHIP_FLIPS)]
        for cp in sends:
            cp.start()
        for k in range(1, 8):
            _remote(c_ref, cbuf.at[_dev_id(_peer(pos, k))], s_c, r_c, k - 1, _peer(pos, k)).wait_recv()
        for d in range(N_DEV):
            act_ref[d:d + 1, :] = _silu(cbuf[d, 0:1, :])
        act_ref[8:9, :] = _silu(cc_ref[0:1, :])
        act_ref[9:16, :] = jnp.zeros((7, D), F32)
        part = _dot_f32(act_ref[...], aw_ref[...])
        pmine[...] = part
        pbuf[s] = part
        psend = [_remote(pmine, pbuf.at[s], s_p, r_p, j, _peer(pos, k)) for j, k in enumerate(CHIP_FLIPS)]
        for cp in psend:
            cp.start()
        for j, k in enumerate(CHIP_FLIPS):
            t = _shard_of(_peer(pos, k))
            _remote(pmine, pbuf.at[t], s_p, r_p, j, _peer(pos, k)).wait_recv()
            _remote(cw_ref, wbuf.at[t], s_w, r_w, j, _peer(pos, k)).wait_recv()
        for t in range(N_SHARD):
            mod_ref[:, t * Wm:(t + 1) * Wm] = pbuf[t] + ab_ref[:, t * Wm:(t + 1) * Wm]
            cwf_ref[:, t * Dq:(t + 1) * Dq] = wbuf[t]
        for cp in sends + psend:
            cp.wait_send()

    return pl.pallas_call(
        body, name="fwd_small",
        in_specs=[VMEM_FULL] * 5, out_specs=[VMEM_FULL] * 3,
        out_shape=[jax.ShapeDtypeStruct((16, D), F32), jax.ShapeDtypeStruct((16, 3 * D), F32),
                   jax.ShapeDtypeStruct((8, D), F32)],
        scratch_shapes=[pltpu.VMEM((N_DEV, 8, D), F32), pltpu.VMEM((16, Wm), F32),
                        pltpu.VMEM((N_SHARD, 16, Wm), F32), pltpu.VMEM((N_SHARD, 8, Dq), F32),
                        pltpu.SemaphoreType.DMA((7,)), pltpu.SemaphoreType.DMA((7,)),
                        pltpu.SemaphoreType.DMA((3,)), pltpu.SemaphoreType.DMA((3,)),
                        pltpu.SemaphoreType.DMA((3,)), pltpu.SemaphoreType.DMA((3,))],
        compiler_params=_cparams(None, VMEM_LIMIT),
    )(c8, cctx8, ada_w, ada_b, conv_w8)


AG_CHUNKS = 3


def _ag_in_proj(xm, w_in_s, w3_s, c2, s2):
    L, D = xm.shape
    Wc = w_in_s.shape[1]
    Wq = Wc // AG_CHUNKS
    Dh = D // 2
    Do = w3_s[0].shape[1]
    TM = min(1024, L // 4)
    NT = L // TM
    NQ = AG_CHUNKS
    order = [(q, j) for q in range(NQ) for j in (0, 1)] + [(q, 2) for q in range(NQ)]

    def body(xm_ref, wi_hbm, wa_ref, wb_ref, wo_ref, c_ref, s_ref, p_hbm, qr_hbm, kr_hbm, fi_hbm, f3_hbm,
             w_vm, s3, stage, qk_stage, ici_s, ici_r, d2d_s, d2d_r, w3_s_, w3_r_, fw_s, fw_r,
             loc, out_sem, qk_sem):
        pos = _position()
        c = pos[2]
        s = _shard_of(pos)
        sib = _peer(pos, 1)
        mine = pl.ds(pl.multiple_of(c * Dh, Dh), Dh)
        other = pl.ds(pl.multiple_of((1 - c) * Dh, Dh), Dh)

        def abs_col(t, q):
            return pl.ds(pl.multiple_of(t * Wc + q * Wq, 128), Wq)

        own = [pltpu.make_async_copy(wi_hbm.at[:, q * Wq:(q + 1) * Wq], w_vm.at[0, q], loc.at[2 + 4 * NQ + q])
               for q in range(NQ)]
        for cp in own:
            cp.start()
        for cp in own:
            cp.wait()
        sends = [_remote(w_vm.at[0, q, mine, :], w_vm.at[1 + j, q, mine, :], ici_s, ici_r, q * 3 + j,
                         _peer(pos, CHIP_FLIPS[j])) for q, j in order if j < 2]
        for cp in sends:
            cp.start()
        for a, w_ref in enumerate((wa_ref, wb_ref, wo_ref)):
            s3[a] = w_ref[...].astype(BF16)
        w3_sends = [_remote(s3.at[:, c], f3_hbm.at[:, s, c], w3_s_, w3_r_, j, _peer(pos, k))
                    for j, k in enumerate(CHIP_FLIPS)]
        local = [pltpu.make_async_copy(s3, f3_hbm.at[:, s], loc.at[1])]
        local += [pltpu.make_async_copy(w_vm.at[0, q], fi_hbm.at[:, abs_col(s, q)], loc.at[2 + q]) for q in range(NQ)]
        for cp in local:
            cp.start()

        def out_copy(slot, rows, cols):
            return pltpu.make_async_copy(stage.at[slot], p_hbm.at[rows, cols], out_sem.at[slot])

        def block(r, q, t, first):
            cols = abs_col(t, q)

            def row_tile(rt, carry):
                rows = pl.ds(pl.multiple_of(rt * TM, TM), TM)
                acc = _dot(xm_ref[rows, :], w_vm[r, q])
                slot = lax.rem(rt, 2)

                @pl.when(rt >= 2 if first else rt >= 0)
                def _():
                    out_copy(slot, rows, cols).wait()

                stage[slot] = acc.astype(BF16)
                out_copy(slot, rows, cols).start()

                def rotary(lo, scale, dst_hbm):
                    c, s = c_ref[rows, :], s_ref[rows, :]
                    for pr in range(Dh // 128):
                        tq = acc[:, lo + pr * 128:lo + (pr + 1) * 128] * scale
                        qk_stage[:, pr * 128:(pr + 1) * 128] = (tq * c + _swap_halves(tq) * s).astype(BF16)
                    cp = pltpu.make_async_copy(qk_stage, dst_hbm.at[rows, :], qk_sem)
                    cp.start()
                    cp.wait()

                if q == NQ - 1:
                    @pl.when(t == 1)
                    def _():
                        rotary(Wq - Dh, 1.0, qr_hbm)
                if q == 0:
                    @pl.when(t == 2)
                    def _():
                        rotary(0, K_SCALE, kr_hbm)
                return carry

            lax.fori_loop(0, NT, row_tile, 0)

        passed = []

        def hand_on(q, j):
            half = w_vm.at[1 + j, q, mine, :]
            if j == 2:
                _remote(half, half, fw_s, fw_r, q, sib).wait_recv()
            else:
                _remote(half, half, ici_s, ici_r, q * 3 + j, sib).wait_recv()

                @pl.when(c == (0 if j == q % 2 else 1))
                def _():
                    _remote(half, w_vm.at[3, q, mine, :], fw_s, fw_r, q, _peer(pos, CHIP_FLIPS[1 - j])).start()
            fwd = _remote(half, half, d2d_s, d2d_r, q * 3 + j, sib)
            fwd.start()
            passed.append(fwd)

        for q in range(NQ):
            if q == NQ - 1:
                hand_on(*order[0])
            block(0, q, s, q == 0)
        for n, (q, j) in enumerate(order):
            r, idx = 1 + j, q * 3 + j
            t = _shard_of(_peer(pos, CHIP_FLIPS[j]))
            if n + 1 < len(order):
                hand_on(*order[n + 1])
            if n + 1 == 2 * NQ - 1:
                for cp in w3_sends:
                    cp.start()
            _remote(w_vm.at[r, q, other, :], w_vm.at[r, q, other, :], d2d_s, d2d_r, idx, sib).wait_recv()
            block(r, q, t, False)
            cp = pltpu.make_async_copy(w_vm.at[r, q], fi_hbm.at[:, abs_col(t, q)], loc.at[2 + NQ + idx])
            cp.start()
            local.append(cp)
        for j, k in enumerate(CHIP_FLIPS):
            t = _shard_of(_peer(pos, k))
            _remote(s3.at[:, c], f3_hbm.at[:, t, c], w3_s_, w3_r_, j, sib).wait_recv()
            fwd = _remote(f3_hbm.at[:, t, c], f3_hbm.at[:, t, c], w3_s_, w3_r_, 3 + j, sib)
            fwd.start()
            passed.append(fwd)
        for j, k in enumerate(CHIP_FLIPS):
            t = _shard_of(_peer(pos, k))
            _remote(s3.at[:, c], f3_hbm.at[:, t, 1 - c], w3_s_, w3_r_, 3 + j, sib).wait_recv()
        for cp in sends + w3_sends + passed:
            cp.wait_send()
        for q in range(NQ):
            _remote(w_vm.at[1, q, mine, :], w_vm.at[3, q, mine, :], fw_s, fw_r, q, sib).wait_send()
        for cp in local:
            cp.wait()
        for slot in range(2):
            out_copy(slot, pl.ds(0, TM), abs_col(s, 0)).wait()

    n_loc = 2 + 5 * NQ
    return pl.pallas_call(
        body, name="ag_in_proj",
        in_specs=[VMEM_FULL, ANY, VMEM_FULL, VMEM_FULL, VMEM_FULL, VMEM_FULL, VMEM_FULL], out_specs=[ANY] * 5,
        out_shape=[jax.ShapeDtypeStruct((L, N_SHARD * Wc), BF16),
                   jax.ShapeDtypeStruct((L, Dh), BF16), jax.ShapeDtypeStruct((L, Dh), BF16),
                   jax.ShapeDtypeStruct((D, N_SHARD * Wc), BF16), jax.ShapeDtypeStruct((3, N_SHARD, 2, Do, D), BF16)],
        scratch_shapes=[pltpu.VMEM((N_SHARD, NQ, D, Wq), BF16), pltpu.VMEM((3, 2, Do, D), BF16), pltpu.VMEM((2, TM, Wq), BF16), pltpu.VMEM((TM, Dh), BF16),
                        pltpu.SemaphoreType.DMA((3 * NQ,)), pltpu.SemaphoreType.DMA((3 * NQ,)),
                        pltpu.SemaphoreType.DMA((3 * NQ,)), pltpu.SemaphoreType.DMA((3 * NQ,)),
                        pltpu.SemaphoreType.DMA((6,)), pltpu.SemaphoreType.DMA((6,)),
                        pltpu.SemaphoreType.DMA((NQ,)), pltpu.SemaphoreType.DMA((NQ,)),
                        pltpu.SemaphoreType.DMA((n_loc,)), pltpu.SemaphoreType.DMA((2,)), pltpu.SemaphoreType.DMA],
        compiler_params=_cparams(None, VMEM_LIMIT),
    )(xm, w_in_s, *w3_s, c2, s2)


def _pair_exchange_w3(dw3):
    _, _, _, Do, D = dw3.shape

    def build(ins, outs, send, recv):
        pos = _position()
        return [_remote(ins[0].at[:, :, 1 - pos[2]], outs[0], send, recv, 0, _peer(pos, 1))]

    return _Exchange((dw3,), (jax.ShapeDtypeStruct((3, N_SHARD, Do, D), F32),), 1, build)


def _sum_pair_in(dw_mine, ri):
    Dh, Wf = dw_mine.shape
    Wc = Wf // N_SHARD
    tr = min(256, Dh)

    def body(a_ref, b_ref, o_ref):
        o_ref[...] = (a_ref[...] + b_ref[...]).astype(BF16)

    return pl.pallas_call(
        body, name="sum_pair_in", grid=(Dh // tr, N_SHARD),
        in_specs=[pl.BlockSpec((tr, Wc), lambda i, t: (i, t)), pl.BlockSpec((tr, Wc), lambda i, t: (i, t))],
        out_specs=pl.BlockSpec((None, tr, Wc), lambda i, t: (t, i, 0)),
        out_shape=jax.ShapeDtypeStruct((N_SHARD, Dh, Wc), BF16),
        compiler_params=_cparams(("parallel", "parallel")),
    )(dw_mine, ri)


def _sum_pair_w3(cidx, dw3, r3):
    _, _, _, Do, D = dw3.shape

    def body(c_ref, a_ref, b_ref, o_ref):
        o_ref[...] = (a_ref[...] + b_ref[...]).astype(BF16)

    return pl.pallas_call(
        body, name="sum_pair_w3",
        grid_spec=pltpu.PrefetchScalarGridSpec(
            num_scalar_prefetch=1, grid=(3,),
            in_specs=[pl.BlockSpec((None, N_SHARD, None, Do, D), lambda a, c: (a, 0, c[0], 0, 0)),
                      pl.BlockSpec((None, N_SHARD, Do, D), lambda a, c: (a, 0, 0, 0))],
            out_specs=pl.BlockSpec((None, N_SHARD, Do, D), lambda a, c: (a, 0, 0, 0))),
        out_shape=jax.ShapeDtypeStruct((3, N_SHARD, Do, D), BF16),
        compiler_params=_cparams(("parallel",)),
    )(cidx, dw3, r3)


def _chips_exchange_in(cs_in):
    _, Dh, Wc = cs_in.shape

    def build(ins, outs, send, recv):
        pos = _position()
        return [_remote(ins[0].at[_shard_of(_peer(pos, k))], outs[0].at[j], send, recv, j, _peer(pos, k))
                for j, k in enumerate(CHIP_FLIPS)]

    return _Exchange((cs_in,), (jax.ShapeDtypeStruct((3, Dh, Wc), BF16),), 3, build)


def _chips_exchange_w3(cs_3):
    _, _, Do, D = cs_3.shape

    def build(ins, outs, send, recv):
        pos = _position()
        return [_remote(ins[0].at[:, _shard_of(_peer(pos, k))], outs[0].at[j], send, recv, j, _peer(pos, k))
                for j, k in enumerate(CHIP_FLIPS)]

    return _Exchange((cs_3,), (jax.ShapeDtypeStruct((3, 3, Do, D), BF16),), 3, build)


def _sum_chips_in(csidx, cs_in, rb_in):
    _, Dh, Wc = cs_in.shape
    tr = min(256, Dh)

    def body(s_ref, a_ref, b_ref, o_ref):
        acc = a_ref[...].astype(F32)
        for j in range(3):
            acc = acc + b_ref[j].astype(F32)
        o_ref[...] = acc

    return pl.pallas_call(
        body, name="sum_chips_in",
        grid_spec=pltpu.PrefetchScalarGridSpec(
            num_scalar_prefetch=1, grid=(Dh // tr,),
            in_specs=[pl.BlockSpec((None, tr, Wc), lambda i, s: (s[1], i, 0)),
                      pl.BlockSpec((3, tr, Wc), lambda i, s: (0, i, 0))],
            out_specs=pl.BlockSpec((None, tr, Wc), lambda i, s: (s[0], i, 0))),
        out_shape=jax.ShapeDtypeStruct((2, Dh, Wc), F32),
        compiler_params=_cparams(("parallel",)),
    )(csidx, cs_in, rb_in)


def _sum_chips_w3(csidx, cs_3, rb_3):
    _, _, Do, D = cs_3.shape

    def body(s_ref, a_ref, b_ref, o_ref):
        acc = a_ref[...].astype(F32)
        for j in range(3):
            acc = acc + b_ref[j].astype(F32)
        o_ref[...] = acc

    return pl.pallas_call(
        body, name="sum_chips_w3",
        grid_spec=pltpu.PrefetchScalarGridSpec(
            num_scalar_prefetch=1, grid=(3,),
            in_specs=[pl.BlockSpec((None, None, Do, D), lambda a, s: (a, s[1], 0, 0)),
                      pl.BlockSpec((3, None, Do, D), lambda a, s: (0, a, 0, 0))],
            out_specs=pl.BlockSpec((None, None, Do, D), lambda a, s: (a, s[0], 0, 0))),
        out_shape=jax.ShapeDtypeStruct((3, 2, Do, D), F32),
        compiler_params=_cparams(("parallel",)),
    )(csidx, cs_3, rb_3)


def _adam_math(w, g, m, v):
    m = ADAM_B1 * m + (1.0 - ADAM_B1) * g
    v = ADAM_B2 * v + (1.0 - ADAM_B2) * (g * g)
    m_hat = m / (1.0 - ADAM_B1 ** ADAM_STEP)
    v_hat = v / (1.0 - ADAM_B2 ** ADAM_STEP)
    delta = -ADAM_LR * (m_hat / (jnp.sqrt(v_hat) + ADAM_EPS) + ADAM_WD * w)
    return delta, m, v


def _adamw(w, g, m, v, name):
    R, C = w.shape
    tr = min(128, R)

    def body(w_ref, g_ref, m_ref, v_ref, d_ref, nm_ref, nv_ref):
        d_ref[...], nm_ref[...], nv_ref[...] = _adam_math(w_ref[...], g_ref[...], m_ref[...], v_ref[...])

    blk = pl.BlockSpec((tr, C), lambda i: (i, 0))
    return pl.pallas_call(
        body, name=name, grid=(R // tr,), in_specs=[blk] * 4, out_specs=[blk] * 3,
        out_shape=[jax.ShapeDtypeStruct((R, C), F32)] * 3,
        compiler_params=_cparams(("parallel",), VMEM_LIMIT),
    )(w, g, m, v)


def _adamw3(ws, g3, ms, vs):
    R, C = ws[0].shape

    def body(*refs):
        w_refs, m_refs, v_refs = refs[0:3], refs[3:6], refs[6:9]
        g_ref, outs = refs[9], refs[10:]
        for a in range(3):
            @pl.when(pl.program_id(0) == a)
            def _(a=a):
                res = _adam_math(w_refs[a][...], g_ref[...], m_refs[a][...], v_refs[a][...])
                for q in range(3):
                    outs[3 * a + q][...] = res[q]

    full = pl.BlockSpec((R, C), lambda a: (0, 0))
    res = pl.pallas_call(
        body, name="adamw_w3", grid=(3,),
        in_specs=[full] * 9 + [pl.BlockSpec((None, R, C), lambda a: (a, 0, 0))], out_specs=[full] * 9,
        out_shape=[jax.ShapeDtypeStruct((R, C), F32)] * 9,
        compiler_params=_cparams(("arbitrary",), VMEM_LIMIT),
    )(*ws, *ms, *vs, g3)
    return res[0:3], res[3:6], res[6:9]


SMALL_ROWS = ("c_ctx", "norm_w", "conv_b", "gn_w", "final_norm_w")


def _bwd_small(stats, ada_w, Dq, gh_in, gh_3):
    D = stats[0].shape[1]
    Wm = ada_w.shape[1]

    def body(stx, stm, stc, stv, stl, stlc, aw_ref, gi_in, g3_in, tot_ref, dm_sh, gcw, da_ref, loss_ref, gi_ref, g3_ref,
             vec_ref, vbuf, dm, amine, abuf, s_v, r_v, s_a, r_a, s_g, r_g):
        pos = _position()
        me, s = _dev_id(pos), _shard_of(pos)
        c, sib = pos[2], _peer(pos, 1)
        halves = [_remote(gi_in.at[c], gi_ref.at[c], s_g, r_g, 0, sib),
                  _remote(g3_in.at[:, c], g3_ref.at[:, c], s_g, r_g, 1, sib)]
        for cp in halves:
            cp.start()
        vec_ref[...] = jnp.zeros_like(vec_ref)
        vec_ref[0:2, :] = stx[0:2, :]
        vec_ref[2:3, :] = stm[1:2, :]
        vec_ref[3:5, :] = stc[0:2, :]
        vec_ref[5:6, :] = stx[2:3, :] + stc[2:3, :]
        vec_ref[6:7, :] = stv[3:4, :]
        vec_ref[7:8, :] = stm[3:4, :]
        vec_ref[8:9, :] = stm[0:1, :]
        vec_ref[9:12, :] = stv[0:3, :]
        vec_ref[12:14, 0:128] = stl[0:2, :] + stlc[0:2, :]
        vec_ref[14:15, :] = stm[2:3, :]
        vbuf[me] = vec_ref[...]
        sends = [_remote(vec_ref, vbuf.at[me], s_v, r_v, k - 1, _peer(pos, k)) for k in range(1, 8)]
        for cp in sends:
            cp.start()
        for k in range(1, 8):
            _remote(vec_ref, vbuf.at[_dev_id(_peer(pos, k))], s_v, r_v, k - 1, _peer(pos, k)).wait_recv()
        tot = vbuf[0]
        for d in range(1, N_DEV):
            tot = tot + vbuf[d]
        loss_ref[...] = jnp.zeros((8, 128), F32) + (0.5 / D) * _sum_all(tot[14:15, :])
        dm[...] = jnp.zeros_like(dm)
        for d in range(N_DEV):
            for r in range(3):
                dm[d:d + 1, r * D:(r + 1) * D] = vbuf[d, r:r + 1, :]
        dm[8:9, 0:D] = tot[3:4, :]
        dm[8:9, D:2 * D] = tot[4:5, :]
        for t in range(N_SHARD):
            @pl.when(s == t)
            def _(t=t):
                dm_sh[...] = dm[:, t * Wm:(t + 1) * Wm]
                gcw[...] = tot[9:12, t * Dq:(t + 1) * Dq]
        tot_ref[...] = tot
        part = lax.dot_general(dm_sh[8:16, :], aw_ref[...], (((1,), (1,)), ((), ())),
                               precision=lax.Precision.HIGHEST, preferred_element_type=F32)
        amine[...] = part
        abuf[s] = part
        asend = [_remote(amine, abuf.at[s], s_a, r_a, j, _peer(pos, k)) for j, k in enumerate(CHIP_FLIPS)]
        for cp in asend:
            cp.start()
        for j, k in enumerate(CHIP_FLIPS):
            _remote(amine, abuf.at[_shard_of(_peer(pos, k))], s_a, r_a, j, _peer(pos, k)).wait_recv()
        da = abuf[0]
        for t in range(1, N_SHARD):
            da = da + abuf[t]
        da_ref[...] = da
        _remote(gi_in.at[1 - c], gi_ref.at[1 - c], s_g, r_g, 0, sib).wait_recv()
        _remote(g3_in.at[:, 1 - c], g3_ref.at[:, 1 - c], s_g, r_g, 1, sib).wait_recv()
        for cp in sends + asend + halves:
            cp.wait_send()

    row = lambda *shape: jax.ShapeDtypeStruct(shape, F32)
    return pl.pallas_call(
        body, name="bwd_small",
        in_specs=[VMEM_FULL] * 7 + [ANY, ANY], out_specs=[VMEM_FULL] * 5 + [ANY, ANY],
        input_output_aliases={7: 5, 8: 6},
        out_shape=[row(16, D), row(16, Wm), row(3, Dq), row(8, D), row(8, 128), row(*gh_in.shape), row(*gh_3.shape)],
        scratch_shapes=[pltpu.VMEM((16, D), F32), pltpu.VMEM((N_DEV, 16, D), F32), pltpu.VMEM((16, 3 * D), F32),
                        pltpu.VMEM((8, D), F32), pltpu.VMEM((N_SHARD, 8, D), F32),
                        pltpu.SemaphoreType.DMA((7,)), pltpu.SemaphoreType.DMA((7,)),
                        pltpu.SemaphoreType.DMA((3,)), pltpu.SemaphoreType.DMA((3,)),
                        pltpu.SemaphoreType.DMA((2,)), pltpu.SemaphoreType.DMA((2,))],
        compiler_params=_cparams(None, VMEM_LIMIT),
    )(*stats, ada_w, gh_in, gh_3)


def _small_update(tot, dm_sh, gcw, da, act, p_row, p_ab, p_cw, p_dl):
    D = act.shape[1]
    Wm = dm_sh.shape[1]
    Dq = gcw.shape[1]

    def body(tot_ref, dm_ref, gcw_ref, da_ref, act_ref, prow, pab, pcw, pdl, gaw_ref, *outs):
        o_q = [outs[8 * q:8 * (q + 1)] for q in range(4)]
        tot = tot_ref[...]
        gaw_ref[...] = lax.dot_general(act_ref[...], dm_ref[...], (((0,), (0,)), ((), ())),
                                       precision=lax.Precision.HIGHEST, preferred_element_type=F32)
        cc = prow[0, 0:1, :]
        sg = _sigmoid(cc)
        g_cctx = da_ref[0:1, :] * (sg * (1.0 + cc * (1.0 - sg)))

        def place_all(o, val):
            o[...] = val

        def emit(k, w, g, m, v, place=place_all):
            for q, val in enumerate((g,) + _adam_math(w, g, m, v)):
                place(o_q[q][k], val)

        g_rows = [g_cctx, tot[5:6, :], tot[6:7, :], tot[7:8, :], tot[8:9, :]]
        for k, g in enumerate(g_rows):
            emit(k, prow[0, k:k + 1, :], g, prow[1, k:k + 1, :], prow[2, k:k + 1, :])

        def place_ab(o, val):
            for r in range(3):
                o[0:1, r * D:(r + 1) * D] = val[r:r + 1, :]

        g_ab = jnp.concatenate([tot[0:1, :] + tot[3:4, :], tot[1:2, :] + tot[4:5, :], tot[2:3, :]], axis=0)
        emit(5, pab[0], g_ab, pab[1], pab[2], place_ab)
        emit(6, pcw[0], gcw_ref[...], pcw[1], pcw[2])
        g_dl = jnp.concatenate([tot[12:14, 0:128] * _sigmoid(-pdl[0, 0:2, :]), jnp.zeros((6, 128), F32)], axis=0)
        emit(7, pdl[0], g_dl, pdl[1], pdl[2])

    row = lambda *shape: jax.ShapeDtypeStruct(shape, F32)
    per_q = [row(1, D)] * 5 + [row(1, 3 * D), row(3, Dq), row(8, 128)]
    res = pl.pallas_call(
        body, name="small_update",
        in_specs=[VMEM_FULL] * 9, out_specs=[VMEM_FULL] * 33,
        out_shape=[row(D, Wm)] + per_q * 4,
        compiler_params=_cparams(None, VMEM_LIMIT),
    )(tot, dm_sh, gcw, da, act, p_row, p_ab, p_cw, p_dl)
    return res[0], [res[1 + 8 * q:1 + 8 * (q + 1)] for q in range(4)]


def _pad_rows(a, rows=8):
    return jnp.pad(a, ((0, rows - a.shape[0]), (0, 0)))


def kernel(x, c, ctx, c_ctx, norm_w, ada_w, ada_b, w_in, conv_w, conv_b, decay_logit, gn_w, w_a, w_b, w_out, final_norm_w, loss_target, m_c_ctx, m_norm_w, m_ada_w, m_ada_b, m_w_in, m_conv_w, m_conv_b, m_decay_logit, m_gn_w, m_w_a, m_w_b, m_w_out, m_final_norm_w, v_c_ctx, v_norm_w, v_ada_w, v_ada_b, v_w_in, v_conv_w, v_conv_b, v_decay_logit, v_gn_w, v_w_a, v_w_b, v_w_out, v_final_norm_w):
    L, D = x.shape[1], x.shape[2]
    H = D // DV
    Wc = w_in.shape[2]
    Do = D // 8
    pos = _position()
    me = _dev_id(pos)
    cidx = jnp.reshape(pos[2], (1,)).astype(jnp.int32)
    sidx = jnp.reshape(_shard_of(pos), (1,)).astype(jnp.int32)

    act, mod, conv_w8 = _fwd_small(_pad_rows(c), _pad_rows(c_ctx[None]), ada_w[0], ada_b, _pad_rows(conv_w[0]))
    mod_x = lax.dynamic_slice_in_dim(mod, me, 1, axis=0).reshape(3, D)
    mod_c = mod[8].reshape(3, D)
    lg = jax.nn.log_sigmoid(decay_logit[0])

    w3_s = tuple(w[0].reshape(2, Do, D) for w in (w_a, w_b, w_out))

    def project(xm, c2, s2, w_in_bf):
        p, qr, kr, w_in_full, w3_full = _ag_in_proj(xm, w_in_bf, w3_s, c2, s2)
        return p, qr, kr, w_in_full, w3_full.reshape(3, D, D)

    csidx = jnp.concatenate([cidx, sidx])
    groups, dret_c, xmt, cmt, dx1, sc_x, w_in_full, gh_3, sts = _local_step(
        x[0], ctx[0], loss_target[0], mod_x, mod_c, norm_w, conv_w8, conv_b, lg, gn_w, final_norm_w[None],
        project, csidx, w_in[0])
    st_mid, st_conv, st_lg, st_lgc, st_c = sts

    dw_mine, ra_in = _dw_in(xmt, groups, cmt, dret_c, D, True)
    cs_in = _sum_pair_in(dw_mine, ra_in)
    grad_x, st_x, rb_in = _dxm(groups, 0, w_in_full, x[0], norm_w, sc_x, dx1, "dxm_x", _chips_exchange_in(cs_in))
    gh_in = _sum_chips_in(csidx, cs_in, rb_in)

    zeros3 = jnp.zeros((3, D), F32)
    p_row = jnp.concatenate(
        [r for t in ((c_ctx[None], norm_w, conv_b, gn_w, final_norm_w[None], zeros3),
                     (m_c_ctx[None], m_norm_w, m_conv_b, m_gn_w, m_final_norm_w[None], zeros3),
                     (v_c_ctx[None], v_norm_w, v_conv_b, v_gn_w, v_final_norm_w[None], zeros3)) for r in t],
        axis=0).reshape(3, 8, D)
    p_ab = jnp.concatenate([ada_b, m_ada_b, v_ada_b], axis=0).reshape(3, 3, D)
    p_cw = jnp.concatenate([conv_w, m_conv_w, v_conv_w], axis=0)
    p_dl = jnp.pad(jnp.concatenate([decay_logit, m_decay_logit, v_decay_logit], axis=0), ((0, 0), (0, 6), (0, 128 - H)))
    tot, dm_sh, gcw, da, loss_t, g_in, g_3 = _bwd_small((st_x, st_mid, st_c, st_conv, st_lg, st_lgc), ada_w[0],
                                                        conv_w.shape[2], gh_in, gh_3)
    g_w_in = g_in.reshape(D, Wc)
    g_3 = g_3.reshape(3, D // 4, D)
    g_ada_w, small = _small_update(tot, dm_sh, gcw, da, act, p_row, p_ab, p_cw, p_dl)

    upd_in = _adamw(w_in[0], g_w_in, m_w_in[0], v_w_in[0], "adamw_w_in")
    upd_ada = _adamw(ada_w[0], g_ada_w, m_ada_w[0], v_ada_w[0], "adamw_ada_w")
    upd_a, upd_b, upd_o = _adamw3((w_a[0], w_b[0], w_out[0]), g_3, (m_w_a[0], m_w_b[0], m_w_out[0]),
                                  (v_w_a[0], v_w_b[0], v_w_out[0]))

    def leaves(q):
        big = lambda g, upd: (g if q == 0 else upd[q - 1])[None]
        r_cctx, r_norm, r_convb, r_gn, r_fnorm, r_ab, r_cw, r_dl = small[q]
        return [r_cctx.reshape(D), r_norm, big(g_ada_w, upd_ada), r_ab, big(g_w_in, upd_in),
                r_cw[None], r_convb, r_dl[0:2, 0:H][None], r_gn,
                big(g_3[0], upd_a), big(g_3[1], upd_b), big(g_3[2], upd_o), r_fnorm.reshape(D)]

    loss = loss_t[0, 0]
    return (loss, grad_x[None], *leaves(0), *leaves(1), *leaves(2), *leaves(3))
```

```python
from typing import Callable, NamedTuple

import jax
import jax.numpy as jnp
import numpy as np
from jax import lax
from jax.experimental import pallas as pl
from jax.experimental.pallas import tpu as pltpu

F32 = jnp.float32
BF16 = jnp.bfloat16
MESH = pl.DeviceIdType.MESH

CHUNK = 128
RET_CPB = 4
DV = 128
DK = 64
GRID_W = 64
ROPE_BASE = 10000.0
EPS = 1e-6
K_SCALE = DK ** -0.5
N_SHARD = 4
N_DEV = 8

ADAM_LR = 0.001
ADAM_B1 = 0.9
ADAM_B2 = 0.999
ADAM_EPS = 1e-08
ADAM_WD = 0.01
ADAM_STEP = 10

VMEM_LIMIT = 56 * 1024 * 1024


def _cparams(sem=None, vmem=None):
    kw = {}
    if sem is not None:
        kw["dimension_semantics"] = sem
    if vmem is not None:
        kw["vmem_limit_bytes"] = vmem
    return pltpu.CompilerParams(**kw)


def _dot(a, b):
    return jnp.dot(a, b, preferred_element_type=F32)


def _dot_nt(a, b):
    return lax.dot_general(a, b, (((1,), (1,)), ((), ())), preferred_element_type=F32)


def _dot_tn(a, b):
    return lax.dot_general(a, b, (((0,), (0,)), ((), ())), preferred_element_type=F32)


def _sigmoid(x):
    return 1.0 / (1.0 + jnp.exp(-x))


def _sum_all(x):
    return jnp.sum(jnp.sum(x, axis=1, keepdims=True), axis=0, keepdims=True)


def _swap_halves(t):
    n = t.shape[1]
    lane = lax.broadcasted_iota(jnp.int32, t.shape, 1)
    low = (lane & 32) == 0
    return jnp.where(low, pltpu.roll(t, n - 32, 1), pltpu.roll(t, 32, 1))


def _vec_spec(d):
    return pl.BlockSpec((1, d), lambda *a: (0, 0))


def _norm_mod(x, nw, sc, sh, name, also_bf16=None):
    L, D = x.shape
    tl = min(256, L)
    nt = L // tl

    def body(x_ref, nw_ref, sc_ref, sh_ref, *rest):
        xm_ref, xmt_ref = rest[-3:-1] if also_bf16 is not None else rest
        xv = x_ref[...]
        r = lax.rsqrt(jnp.mean(xv * xv, axis=-1, keepdims=True) + EPS)
        xm = (xv * r * nw_ref[...]) * (1.0 + sc_ref[...]) + sh_ref[...]
        xm_b = xm.astype(BF16)
        xm_ref[...] = xm_b
        xmt_ref[...] = xm_b.T
        if also_bf16 is not None:
            rest[-1][...] = rest[0][...].astype(BF16)

    in_specs = [pl.BlockSpec((tl, D), lambda i: (i, 0)), _vec_spec(D), _vec_spec(D), _vec_spec(D)]
    out_specs = [pl.BlockSpec((tl, D), lambda i: (i, 0)), pl.BlockSpec((D, tl), lambda i: (0, i))]
    out_shape = [jax.ShapeDtypeStruct((L, D), BF16), jax.ShapeDtypeStruct((D, L), BF16)]
    args = [x, nw, sc, sh]
    if also_bf16 is not None:
        R, C = also_bf16.shape
        slab = pl.BlockSpec((R // nt, C), lambda i: (i, 0))
        in_specs.append(slab)
        out_specs.append(slab)
        out_shape.append(jax.ShapeDtypeStruct((R, C), BF16))
        args.append(also_bf16)
    return pl.pallas_call(
        body, name=name, grid=(nt,), in_specs=in_specs, out_specs=out_specs, out_shape=out_shape,
        compiler_params=_cparams(("parallel",)),
    )(*args)


QK_BLOCK, V_BLOCK = 4, 5


def _in_proj(xm, w, name, first=0, count=None):
    M, D = xm.shape
    count = w.shape[1] // D if count is None else count
    tm = min(1024, M)

    def body(a_ref, b_ref, o_ref, qk_ref):
        acc = _dot(a_ref[...], b_ref[...])
        o_ref[...] = acc.astype(o_ref.dtype)

        @pl.when(pl.program_id(1) == QK_BLOCK - first)
        def _():
            qk_ref[...] = acc

    return pl.pallas_call(
        body, name=name, grid=(M // tm, count),
        in_specs=[pl.BlockSpec((tm, D), lambda i, j: (i, 0)), pl.BlockSpec((D, D), lambda i, j: (0, first + j))],
        out_specs=[pl.BlockSpec((tm, D), lambda i, j: (i, j)), pl.BlockSpec((tm, D), lambda i, j: (i, 0))],
        out_shape=[jax.ShapeDtypeStruct((M, count * D), BF16), jax.ShapeDtypeStruct((M, D), F32)],
        compiler_params=_cparams(("parallel", "arbitrary")),
    )(xm, w)


def _halo_specs(tl, L, D, col):
    hb = tl // 16
    last = L // 16 - 1
    prev = pl.BlockSpec((16, D), lambda i: (jnp.maximum(i * hb - 1, 0), col))
    nxt = pl.BlockSpec((16, D), lambda i: (jnp.minimum((i + 1) * hb, last), col))
    return prev, nxt


def _shift_rows(u, above, below):
    tl = u.shape[0]
    row = lax.broadcasted_iota(jnp.int32, u.shape, 0)
    dn = jnp.where(row == 0, above, pltpu.roll(u, 1, 0))
    up = jnp.where(row == tl - 1, below, pltpu.roll(u, tl - 1, 0))
    return dn, up


def _rope_tables(L):
    pos = np.arange(L)
    row = (pos // GRID_W).astype(np.float32)
    col = (pos % GRID_W).astype(np.float32)
    nf = DK // 4
    inv = np.float32(ROPE_BASE) ** (-np.arange(nf, dtype=np.float32) / np.float32(nf))
    ang = np.concatenate([row[:, None] * inv, col[:, None] * inv], axis=-1).astype(np.float32)
    cos, sin = np.cos(ang), np.sin(ang)
    return (jnp.asarray(np.concatenate([cos, cos, cos, cos], axis=-1), F32),
            jnp.asarray(np.concatenate([-sin, sin, -sin, sin], axis=-1), F32))


def _smem_spec():
    return pl.BlockSpec(memory_space=pltpu.SMEM)


def _pair_select(e0, e1):
    row = lax.broadcasted_iota(jnp.int32, e0.shape, 0)
    return jnp.where(row < DK, e0, e1)


def _head_lane_mask(shape, e):
    lane = lax.broadcasted_iota(jnp.int32, shape, 1)
    return (lane < DK) if e == 0 else (lane >= DK)


def _ctx_states(pc, pqk_c, lg, D):
    Lc = pc.shape[0]
    H = D // DV

    def body(lg_ref, k_ref, v_ref, s_ref):
        m = lax.broadcasted_iota(jnp.int32, (Lc, DV), 0).astype(F32)
        for pr in range(H // 2):
            k2 = k_ref[:, pr * 128:(pr + 1) * 128].astype(F32) * K_SCALE
            res = [[None, None], [None, None]]
            for e in range(2):
                h = 2 * pr + e
                v = v_ref[:, h * DV:(h + 1) * DV]
                dec_f = jnp.exp(lg_ref[0, h] * (Lc - 1.0 - m))
                dec_b = jnp.exp(lg_ref[1, h] * m)
                res[0][e] = _dot_tn((k2 * dec_f).astype(BF16), v)
                res[1][e] = _dot_tn((k2 * dec_b).astype(BF16), v)
            s_ref[0, pr] = _pair_select(res[0][0], res[0][1])
            s_ref[1, pr] = _pair_select(res[1][0], res[1][1])

    return pl.pallas_call(
        body, name="ctx_states", grid=(1,),
        in_specs=[_smem_spec(), pl.BlockSpec((Lc, D // 2), lambda i: (0, 1)), pl.BlockSpec((Lc, D), lambda i: (0, 1))],
        out_specs=pl.BlockSpec((2, H // 2, 128, 128), lambda i: (0, 0, 0, 0)),
        out_shape=jax.ShapeDtypeStruct((2, H // 2, 128, 128), F32),
    )(lg, pqk_c, pc)


T_M, T_MT = 0, 1
T_MF1, T_MB1 = 2, 3
T_QF, T_QB = 4, 5
T_KF, T_KB = 6, 7


def _decay_tables(lg, H):
    def body(lg_ref, t_ref):
        h = pl.program_id(0)
        lgf, lgb = lg_ref[0, h], lg_ref[1, h]
        i = lax.broadcasted_iota(jnp.int32, (CHUNK, CHUNK), 0).astype(F32)
        j = lax.broadcasted_iota(jnp.int32, (CHUNK, CHUNK), 1).astype(F32)
        d = i - j
        mf = jnp.where(d > 0, jnp.exp(lgf * jnp.maximum(d, 0.0)), 0.0)
        mb = jnp.where(d < 0, jnp.exp(lgb * jnp.maximum(-d, 0.0)), 0.0)
        mf_t = jnp.where(d < 0, jnp.exp(lgf * jnp.maximum(-d, 0.0)), 0.0)
        mb_t = jnp.where(d > 0, jnp.exp(lgb * jnp.maximum(d, 0.0)), 0.0)
        diag = jnp.where(d == 0, 2.0, 0.0)
        t_ref[0, T_M] = mf + mb + diag
        t_ref[0, T_MT] = mf_t + mb_t + diag
        t_ref[0, T_MF1] = mf * d
        t_ref[0, T_MB1] = mb * (-d)
        t_ref[0, T_QF] = jnp.exp(lgf * (i + 1.0))
        t_ref[0, T_QB] = jnp.exp(lgb * (CHUNK - i))
        t_ref[0, T_KF] = jnp.exp(lgf * (CHUNK - 1.0 - i))
        t_ref[0, T_KB] = jnp.exp(lgb * i)

    return pl.pallas_call(
        body, name="decay_tables", grid=(H,), in_specs=[_smem_spec()],
        out_specs=pl.BlockSpec((1, 8, CHUNK, CHUNK), lambda h: (h, 0, 0, 0)),
        out_shape=jax.ShapeDtypeStruct((H, 8, CHUNK, CHUNK), F32),
    )(lg)


def _tab_spec(H):
    return pl.BlockSpec((H, 8, CHUNK, CHUNK), lambda n: (0, 0, 0, 0))


def _chunk_decay(tab_ref, h):
    return tab_ref[h, T_QF, CHUNK - 1:CHUNK, :], tab_ref[h, T_QB, 0:1, :]


def _ret_states(kr, p, s0, tab, D):
    L = kr.shape[0]
    H = D // DV
    N = L // CHUNK
    HP = H // 2

    def body(tab_ref, kf_ref, kb_ref, vf_ref, vb_ref, s0_ref, sf_out, sb_out, sf, sb):
        n = pl.program_id(0)

        @pl.when(n == 0)
        def _():
            sf[...] = s0_ref[0]
            sb[...] = s0_ref[1]

        for cc in range(RET_CPB):
            cf_, cb_ = cc, RET_CPB - 1 - cc
            rf, rb = slice(cf_ * CHUNK, (cf_ + 1) * CHUNK), slice(cb_ * CHUNK, (cb_ + 1) * CHUNK)
            sf_out[cf_] = sf[...]
            sb_out[cb_] = sb[...]
            for pr in range(HP):
                kf2 = kf_ref[rf, pr * 128:(pr + 1) * 128].astype(F32)
                kb2 = kb_ref[rb, pr * 128:(pr + 1) * 128].astype(F32)
                inc_f, inc_b, gf, gb = [], [], [], []
                for e in range(2):
                    h = 2 * pr + e
                    inc_f.append(_dot_tn((kf2 * tab_ref[h, T_KF]).astype(BF16), vf_ref[rf, h * DV:(h + 1) * DV]))
                    inc_b.append(_dot_tn((kb2 * tab_ref[h, T_KB]).astype(BF16), vb_ref[rb, h * DV:(h + 1) * DV]))
                    cf, cb = _chunk_decay(tab_ref, h)
                    gf.append(jnp.broadcast_to(cf, (128, 128)))
                    gb.append(jnp.broadcast_to(cb, (128, 128)))
                sf[pr] = _pair_select(gf[0], gf[1]) * sf[pr] + _pair_select(inc_f[0], inc_f[1])
                sb[pr] = _pair_select(gb[0], gb[1]) * sb[pr] + _pair_select(inc_b[0], inc_b[1])

    st = jax.ShapeDtypeStruct((N, HP, 128, 128), F32)
    R = RET_CPB * CHUNK
    NB = N // RET_CPB
    return _riding_call(
        body, None, NB, name="ret_states", args=(tab, kr, kr, p, p, s0),
        in_specs=[_tab_spec(H),
                  pl.BlockSpec((R, D // 2), lambda n: (n, 0)),
                  pl.BlockSpec((R, D // 2), lambda n: (NB - 1 - n, 0)),
                  pl.BlockSpec((R, D), lambda n: (n, 5)),
                  pl.BlockSpec((R, D), lambda n: (NB - 1 - n, 5)),
                  pl.BlockSpec((2, HP, 128, 128), lambda n: (0, 0, 0, 0))],
        out_specs=[pl.BlockSpec((RET_CPB, HP, 128, 128), lambda n: (n, 0, 0, 0)),
                   pl.BlockSpec((RET_CPB, HP, 128, 128), lambda n: (NB - 1 - n, 0, 0, 0))],
        out_shape=[st, st],
        scratch=[pltpu.VMEM((HP, 128, 128), F32), pltpu.VMEM((HP, 128, 128), F32)],
        cparams=_cparams(("arbitrary",)))


def _ret_out(qr, kr, p, sf_prev, sb_prev, gn_w, tab, D):
    L = qr.shape[0]
    H = D // DV
    N = L // CHUNK
    HP = H // 2

    def body(tab_ref, q_ref, k_ref, v_ref, zb_ref, sf_ref, sb_ref, gn_ref, o_ref, yb_ref):
        def chunk(cc, carry):
            rows = pl.ds(pl.multiple_of(cc * CHUNK, CHUNK), CHUNK)
            for pr in range(HP):
                q2 = q_ref[rows, pr * 128:(pr + 1) * 128]
                k2 = k_ref[rows, pr * 128:(pr + 1) * 128]
                sfp = sf_ref[cc, pr].astype(BF16)
                sbp = sb_ref[cc, pr].astype(BF16)
                for e in range(2):
                    h = 2 * pr + e
                    sl = slice(h * DV, (h + 1) * DV)
                    qm = jnp.where(_head_lane_mask(q2.shape, e), q2, jnp.zeros_like(q2))
                    a = (_dot_nt(qm, k2) * tab_ref[h, T_M]).astype(BF16)
                    qf = qm.astype(F32)
                    o = _dot(a, v_ref[rows, sl])
                    o += _dot((qf * tab_ref[h, T_QF]).astype(BF16), sfp)
                    o += _dot((qf * tab_ref[h, T_QB]).astype(BF16), sbp)
                    o_ref[rows, sl] = o
                    mu = jnp.mean(o, axis=-1, keepdims=True)
                    oc = o - mu
                    rstd = lax.rsqrt(jnp.mean(oc * oc, axis=-1, keepdims=True) + EPS)
                    zb = zb_ref[rows, sl].astype(F32)
                    yb_ref[rows, sl] = (zb * _sigmoid(zb) * (oc * rstd * gn_ref[:, sl])).astype(BF16)
            return carry

        lax.fori_loop(0, RET_CPB, chunk, 0)

    R = RET_CPB * CHUNK
    return _riding_call(
        body, None, N // RET_CPB, name="ret_out", args=(tab, qr, kr, p, p, sf_prev, sb_prev, gn_w),
        in_specs=[_tab_spec(H),
                  pl.BlockSpec((R, D // 2), lambda n: (n, 0)),
                  pl.BlockSpec((R, D // 2), lambda n: (n, 0)),
                  pl.BlockSpec((R, D), lambda n: (n, 5)),
                  pl.BlockSpec((R, D), lambda n: (n, 6)),
                  pl.BlockSpec((RET_CPB, HP, 128, 128), lambda n: (n, 0, 0, 0)),
                  pl.BlockSpec((RET_CPB, HP, 128, 128), lambda n: (n, 0, 0, 0)),
                  _vec_spec(D)],
        out_specs=[pl.BlockSpec((R, D), lambda n: (n, 0)), pl.BlockSpec((R, D), lambda n: (n, 0))],
        out_shape=[jax.ShapeDtypeStruct((L, D), F32), jax.ShapeDtypeStruct((L, D), BF16)],
        cparams=_cparams(("arbitrary",)))


def _mid(p, yb, o, x, tgt, w3, g, fw, conv_w, conv_b, gn_w, D):
    L = x.shape[0]
    H = D // DV
    tm = min(256, L)
    nt = L // tm

    def body(h_ref, bg_ref, cg_ref, za_ref, hp_ref, hn_ref, cp_ref, cn_ref, yb_ref, ga_ref, gb_ref, zb_ref, o_ref,
             x_ref, t_ref, w_hbm, g_ref, fw_ref, cw_ref, cb_ref, gn_ref,
             dx1_ref, dya_ref, do_ref, dzb_ref, dgab_ref, dw_hbm, st_ref, w_vm, dw_acc, sem):
        i = pl.program_id(0)

        @pl.when(i == 0)
        def _():
            cp = pltpu.make_async_copy(w_hbm, w_vm, sem)
            cp.start()
            dw_acc[...] = jnp.zeros_like(dw_acc)
            st_ref[...] = jnp.zeros_like(st_ref)
            cp.wait()

        u = cg_ref[...].astype(F32) * h_ref[...].astype(F32)
        above = jnp.where(i == 0, 0.0, cp_ref[15:16, :].astype(F32) * hp_ref[15:16, :].astype(F32))
        below = jnp.where(i == nt - 1, 0.0, cn_ref[0:1, :].astype(F32) * hn_ref[0:1, :].astype(F32))
        dn, up = _shift_rows(u, above, below)
        co = cw_ref[0:1, :] * dn + cw_ref[1:2, :] * u + cw_ref[2:3, :] * up + cb_ref[...]
        za = za_ref[...].astype(F32)
        ya_b = (za * _sigmoid(za) * bg_ref[...].astype(F32) * co).astype(BF16)
        yb_b = yb_ref[...]
        y_a = _dot(ya_b, w_vm[0])
        y_b = _dot(yb_b, w_vm[1])
        sga = _sigmoid(ga_ref[...].astype(F32))
        sgb = _sigmoid(gb_ref[...].astype(F32))
        mix_b = (sga * y_a + sgb * y_b).astype(BF16)
        y_x = _dot(mix_b, w_vm[2])
        gvec, fwv = g_ref[...], fw_ref[...]
        x1 = x_ref[...] + gvec * y_x
        r1 = lax.rsqrt(jnp.mean(x1 * x1, axis=-1, keepdims=True) + EPS)
        xh = x1 * r1
        diff = xh * fwv - t_ref[...]
        dout = diff * (1.0 / D)
        dxh = dout * fwv
        dx1 = r1 * (dxh - xh * jnp.mean(dxh * xh, axis=-1, keepdims=True))
        dx1_ref[...] = dx1
        st_ref[0:1, :] += jnp.sum(dout * xh, axis=0, keepdims=True)
        st_ref[1:2, :] += jnp.sum(dx1 * y_x, axis=0, keepdims=True)
        st_ref[2:3, :] += jnp.sum(diff * diff, axis=0, keepdims=True)
        dyx_b = (dx1 * gvec).astype(BF16)
        dmix = _dot_nt(dyx_b, w_vm[2])
        dw_acc[2] += _dot_tn(mix_b, dyx_b)
        dya_b = (dmix * sga).astype(BF16)
        dyb_b = (dmix * sgb).astype(BF16)
        dgab_ref[:, 0:D] = (dmix * y_a * sga * (1.0 - sga)).astype(BF16)
        dgab_ref[:, D:2 * D] = (dmix * y_b * sgb * (1.0 - sgb)).astype(BF16)
        dya_ref[...] = _dot_nt(dya_b, w_vm[0])
        dyb = _dot_nt(dyb_b, w_vm[1])
        dw_acc[0] += _dot_tn(ya_b, dya_b)
        dw_acc[1] += _dot_tn(yb_b, dyb_b)

        for h in range(H):
            sl = slice(h * DV, (h + 1) * DV)
            ov = o_ref[:, sl]
            oc = ov - jnp.mean(ov, axis=-1, keepdims=True)
            rstd = lax.rsqrt(jnp.mean(oc * oc, axis=-1, keepdims=True) + EPS)
            rn = oc * rstd
            gw = gn_ref[:, sl]
            zb = zb_ref[:, sl].astype(F32)
            sz = _sigmoid(zb)
            dy = dyb[:, sl]
            dzb_ref[:, sl] = (dy * (rn * gw) * (sz * (1.0 + zb * (1.0 - sz)))).astype(BF16)
            dretn = dy * (zb * sz)
            st_ref[3:4, sl] += jnp.sum(dretn * rn, axis=0, keepdims=True)
            drn = dretn * gw
            do_ref[:, sl] = (rstd * (drn - jnp.mean(drn, axis=-1, keepdims=True)
                                     - rn * jnp.mean(drn * rn, axis=-1, keepdims=True))).astype(BF16)

        @pl.when(i == nt - 1)
        def _():
            out = pltpu.make_async_copy(dw_acc, dw_hbm, sem)
            out.start()
            out.wait()

    row = lambda col: pl.BlockSpec((tm, D), lambda i: (i, col))
    any_spec = pl.BlockSpec(memory_space=pl.ANY)
    f32o = jax.ShapeDtypeStruct((L, D), F32)
    bf16o = jax.ShapeDtypeStruct((L, D), BF16)
    hp, hn = _halo_specs(tm, L, D, 0)
    cp, cn = _halo_specs(tm, L, D, 2)
    return pl.pallas_call(
        body, name="mid", grid=(nt,),
        in_specs=[row(0), row(1), row(2), row(3), hp, hn, cp, cn, row(0), row(7), row(8), row(6), row(0),
                  row(0), row(0), any_spec, _vec_spec(D), _vec_spec(D),
                  pl.BlockSpec((8, D), lambda i: (0, 0)), _vec_spec(D), _vec_spec(D)],
        out_specs=[row(0), row(0), row(0), row(0), pl.BlockSpec((tm, 2 * D), lambda i: (i, 0)), any_spec,
                   pl.BlockSpec((8, D), lambda i: (0, 0))],
        out_shape=[f32o, f32o, bf16o, bf16o, jax.ShapeDtypeStruct((L, 2 * D), BF16),
                   jax.ShapeDtypeStruct((3, D, D), F32), jax.ShapeDtypeStruct((8, D), F32)],
        scratch_shapes=[pltpu.VMEM((3, D, D), BF16), pltpu.VMEM((3, D, D), F32), pltpu.SemaphoreType.DMA],
        compiler_params=_cparams(("arbitrary",), VMEM_LIMIT),
    )(p, p, p, p, p, p, p, p, yb, p, p, p, o, x, tgt, w3, g, fw, conv_w, conv_b, gn_w)


def _conv_bwd(dya, p, conv_w, conv_b, D, exchange=None):
    L = p.shape[0]
    tl = min(256, L)
    nt = L // tl

    def body(d_ref, h_ref, bg_ref, cg_ref, za_ref,
             dp_ref, dn_ref, hp_ref, hn_ref, bp_ref, bn_ref, cp_ref, cn_ref, zp_ref, zn_ref,
             w_ref, b_ref, dc_ref, st_ref):
        i = pl.program_id(0)

        @pl.when(i == 0)
        def _():
            st_ref[...] = jnp.zeros_like(st_ref)

        first, last = i == 0, i == nt - 1
        h = h_ref[...].astype(F32)
        cg = cg_ref[...].astype(F32)
        bg = bg_ref[...].astype(F32)
        za = za_ref[...].astype(F32)
        dy = d_ref[...].astype(F32)
        u = cg * h
        u_above = jnp.where(first, 0.0, cp_ref[15:16, :].astype(F32) * hp_ref[15:16, :].astype(F32))
        u_below = jnp.where(last, 0.0, cn_ref[0:1, :].astype(F32) * hn_ref[0:1, :].astype(F32))
        u_dn, u_up = _shift_rows(u, u_above, u_below)
        w0, w1, w2 = w_ref[0:1, :], w_ref[1:2, :], w_ref[2:3, :]
        co = w0 * u_dn + w1 * u + w2 * u_up + b_ref[...]
        sz = _sigmoid(za)
        silu = za * sz
        dc_ref[:, 3 * D:4 * D] = (dy * bg * co * (sz * (1.0 + za * (1.0 - sz)))).astype(BF16)
        dc_ref[:, D:2 * D] = (dy * silu * co).astype(BF16)
        dco = dy * silu * bg

        def edge(dr, zr, br, r):
            z = zr[r:r + 1, :].astype(F32)
            return dr[r:r + 1, :].astype(F32) * (z * _sigmoid(z)) * br[r:r + 1, :].astype(F32)

        dco_above = jnp.where(first, 0.0, edge(dp_ref, zp_ref, bp_ref, 15))
        dco_below = jnp.where(last, 0.0, edge(dn_ref, zn_ref, bn_ref, 0))
        dco_dn, dco_up = _shift_rows(dco, dco_above, dco_below)
        du = w0 * dco_up + w1 * dco + w2 * dco_dn
        dc_ref[:, 2 * D:3 * D] = (du * h).astype(BF16)
        dc_ref[:, 0:D] = (du * cg).astype(BF16)
        st_ref[0:1, :] += jnp.sum(dco * u_dn, axis=0, keepdims=True)
        st_ref[1:2, :] += jnp.sum(dco * u, axis=0, keepdims=True)
        st_ref[2:3, :] += jnp.sum(dco * u_up, axis=0, keepdims=True)
        st_ref[3:4, :] += jnp.sum(dco, axis=0, keepdims=True)

    main = lambda col: pl.BlockSpec((tl, D), lambda i: (i, col))
    halos = []
    for col in (0, 0, 1, 2, 3):
        halos.extend(_halo_specs(tl, L, D, col))
    return _riding_call(
        body, exchange, nt, name="conv_bwd",
        args=(dya, p, p, p, p, dya, dya, p, p, p, p, p, p, p, p, conv_w, conv_b),
        in_specs=[main(0), main(0), main(1), main(2), main(3)] + halos
                 + [pl.BlockSpec((8, D), lambda i: (0, 0)), _vec_spec(D)],
        out_specs=[pl.BlockSpec((tl, 4 * D), lambda i: (i, 0)), pl.BlockSpec((8, D), lambda i: (0, 0))],
        out_shape=[jax.ShapeDtypeStruct((L, 4 * D), BF16), jax.ShapeDtypeStruct((8, D), F32)],
        cparams=_cparams(("arbitrary",)))


def _ret_bwd_states(qr, do, tab, D):
    L = qr.shape[0]
    H = D // DV
    N = L // CHUNK
    HP = H // 2

    def body(tab_ref, qf_ref, qb_ref, dof_ref, dob_ref, dsf_out, dsb_out, ds0_out, dsf, dsb):
        n = pl.program_id(0)

        @pl.when(n == 0)
        def _():
            dsf[...] = jnp.zeros_like(dsf)
            dsb[...] = jnp.zeros_like(dsb)

        for cc in range(RET_CPB):
            cf_, cb_ = RET_CPB - 1 - cc, cc
            rf, rb = slice(cf_ * CHUNK, (cf_ + 1) * CHUNK), slice(cb_ * CHUNK, (cb_ + 1) * CHUNK)
            dsf_out[cf_] = dsf[...]
            dsb_out[cb_] = dsb[...]
            for pr in range(HP):
                qf2 = qf_ref[rf, pr * 128:(pr + 1) * 128].astype(F32)
                qb2 = qb_ref[rb, pr * 128:(pr + 1) * 128].astype(F32)
                inc_f, inc_b, gf, gb = [], [], [], []
                for e in range(2):
                    h = 2 * pr + e
                    inc_f.append(_dot_tn((qf2 * tab_ref[h, T_QF]).astype(BF16), dof_ref[rf, h * DV:(h + 1) * DV]))
                    inc_b.append(_dot_tn((qb2 * tab_ref[h, T_QB]).astype(BF16), dob_ref[rb, h * DV:(h + 1) * DV]))
                    cf, cb = _chunk_decay(tab_ref, h)
                    gf.append(jnp.broadcast_to(cf, (128, 128)))
                    gb.append(jnp.broadcast_to(cb, (128, 128)))
                dsf[pr] = _pair_select(gf[0], gf[1]) * dsf[pr] + _pair_select(inc_f[0], inc_f[1])
                dsb[pr] = _pair_select(gb[0], gb[1]) * dsb[pr] + _pair_select(inc_b[0], inc_b[1])

        @pl.when(n == NB - 1)
        def _():
            ds0_out[0] = dsf[...]
            ds0_out[1] = dsb[...]

    st = jax.ShapeDtypeStruct((N, HP, 128, 128), F32)
    R = RET_CPB * CHUNK
    NB = N // RET_CPB
    return pl.pallas_call(
        body, name="ret_bwd_states", grid=(NB,),
        in_specs=[_tab_spec(H),
                  pl.BlockSpec((R, D // 2), lambda n: (NB - 1 - n, 0)),
                  pl.BlockSpec((R, D // 2), lambda n: (n, 0)),
                  pl.BlockSpec((R, D), lambda n: (NB - 1 - n, 0)),
                  pl.BlockSpec((R, D), lambda n: (n, 0))],
        out_specs=[pl.BlockSpec((RET_CPB, HP, 128, 128), lambda n: (NB - 1 - n, 0, 0, 0)),
                   pl.BlockSpec((RET_CPB, HP, 128, 128), lambda n: (n, 0, 0, 0)),
                   pl.BlockSpec((2, HP, 128, 128), lambda n: (0, 0, 0, 0))],
        out_shape=[st, st, jax.ShapeDtypeStruct((2, HP, 128, 128), F32)],
        scratch_shapes=[pltpu.VMEM((HP, 128, 128), F32), pltpu.VMEM((HP, 128, 128), F32)],
        compiler_params=_cparams(("arbitrary",)),
    )(tab, qr, qr, do, do)


def _ret_bwd_main(qr, kr, p, do, sf_prev, sb_prev, dsf, dsb, c2, s2, tab, D, exchange=None):
    L = qr.shape[0]
    H = D // DV
    N = L // CHUNK
    HP = H // 2
    W = D // 2

    def body(tab_ref, q_ref, k_ref, v_ref, do_ref, sf_ref, sb_ref, dsf_ref, dsb_ref, c_ref, s_ref,
             dr_ref, st_ref, dl_acc):
        @pl.when(pl.program_id(0) == 0)
        def _():
            dl_acc[...] = jnp.zeros_like(dl_acc)

        i = lax.broadcasted_iota(jnp.int32, (CHUNK, 128), 0).astype(F32)
        rowid = lax.broadcasted_iota(jnp.int32, (128, 128), 0)

        def chunk(cc, carry):
            rows = pl.ds(pl.multiple_of(cc * CHUNK, CHUNK), CHUNK)
            c, s = c_ref[rows, :], s_ref[rows, :]
            for pr in range(HP):
                ps = slice(pr * 128, (pr + 1) * 128)
                q2, k2 = q_ref[rows, ps], k_ref[rows, ps]
                sf32, sb32 = sf_ref[cc, pr], sb_ref[cc, pr]
                dsf32, dsb32 = dsf_ref[cc, pr], dsb_ref[cc, pr]
                sfp, sbp = sf32.astype(BF16), sb32.astype(BF16)
                dsfp, dsbp = dsf32.astype(BF16), dsb32.astype(BF16)
                dq2 = jnp.zeros((CHUNK, 128), F32)
                dk2 = jnp.zeros((CHUNK, 128), F32)
                for e in range(2):
                    h = 2 * pr + e
                    sl = slice(h * DV, (h + 1) * DV)
                    hm = _head_lane_mask(q2.shape, e)
                    qm = jnp.where(hm, q2, jnp.zeros_like(q2))
                    km = jnp.where(hm, k2, jnp.zeros_like(k2))
                    qf, kf = qm.astype(F32), km.astype(F32)
                    v, do = v_ref[rows, sl], do_ref[rows, sl]
                    vf, dof = v.astype(F32), do.astype(F32)
                    m_t = tab_ref[h, T_MT]
                    sc = _dot_nt(qm, k2)
                    dpm = _dot_nt(do, v)
                    dsc = (dpm * tab_ref[h, T_M]).astype(BF16)
                    a_t = (_dot_nt(km, q2) * m_t).astype(BF16)
                    dsc_t = (_dot_nt(v, do) * m_t).astype(BF16)
                    dq_f, dq_b = tab_ref[h, T_QF], tab_ref[h, T_QB]
                    dk_f, dk_b = tab_ref[h, T_KF], tab_ref[h, T_KB]
                    dq = _dot(dsc, km)
                    dq += jnp.where(hm, dq_f * _dot_nt(do, sfp) + dq_b * _dot_nt(do, sbp), 0.0)
                    dk = _dot(dsc_t, qm)
                    dk += jnp.where(hm, dk_f * _dot_nt(v, dsfp) + dk_b * _dot_nt(v, dsbp), 0.0)
                    kdf = _dot((kf * dk_f).astype(BF16), dsfp)
                    kdb = _dot((kf * dk_b).astype(BF16), dsbp)
                    dr_ref[rows, D + h * DV:D + (h + 1) * DV] = (_dot(a_t, do) + kdf + kdb).astype(BF16)
                    dq2 += dq
                    dk2 += dk
                    xf = _dot((qf * dq_f).astype(BF16), sfp)
                    xb = _dot((qf * dq_b).astype(BF16), sbp)
                    pair = (rowid < DK) if e == 0 else (rowid >= DK)
                    gcf, gcb = tab_ref[h, T_QF, CHUNK - 1:CHUNK, 0:1], tab_ref[h, T_QB, 0:1, 0:1]
                    scdp = sc * dpm
                    dl_acc[h, 0] += scdp * tab_ref[h, T_MF1] + xf * dof * (i + 1.0) \
                        + kdf * vf * (CHUNK - 1.0 - i) + (CHUNK * gcf) * jnp.where(pair, dsf32 * sf32, 0.0)
                    dl_acc[h, 1] += scdp * tab_ref[h, T_MB1] + xb * dof * (CHUNK - i) \
                        + kdb * vf * i + (CHUNK * gcb) * jnp.where(pair, dsb32 * sb32, 0.0)
                dr_ref[rows, ps] = (dq2 * c - _swap_halves(dq2) * s).astype(BF16)
                dr_ref[rows, W + pr * 128:W + (pr + 1) * 128] = \
                    ((dk2 * c - _swap_halves(dk2) * s) * K_SCALE).astype(BF16)
            return carry

        lax.fori_loop(0, RET_CPB, chunk, 0)

        @pl.when(pl.program_id(0) == N // RET_CPB - 1)
        def _():
            lane = lax.broadcasted_iota(jnp.int32, (1, 128), 1)
            acc = [jnp.zeros((1, 128), F32), jnp.zeros((1, 128), F32)]
            for h in range(H):
                for b in range(2):
                    acc[b] += jnp.where(lane == h, _sum_all(dl_acc[h, b]), 0.0)
            st_ref[...] = jnp.zeros_like(st_ref)
            st_ref[0:1, :] = acc[0]
            st_ref[1:2, :] = acc[1]

    R = RET_CPB * CHUNK
    st_spec = pl.BlockSpec((RET_CPB, HP, 128, 128), lambda n: (n, 0, 0, 0))
    half = pl.BlockSpec((R, W), lambda n: (n, 0))
    rope = pl.BlockSpec((R, 128), lambda n: (n, 0))
    return _riding_call(
        body, exchange, N // RET_CPB, name="ret_bwd_main",
        args=(tab, qr, kr, p, do, sf_prev, sb_prev, dsf, dsb, c2, s2),
        in_specs=[_tab_spec(H), half, half,
                  pl.BlockSpec((R, D), lambda n: (n, 5)),
                  pl.BlockSpec((R, D), lambda n: (n, 0)),
                  st_spec, st_spec, st_spec, st_spec, rope, rope],
        out_specs=[pl.BlockSpec((R, 2 * D), lambda n: (n, 0)),
                   pl.BlockSpec((8, 128), lambda n: (0, 0))],
        out_shape=[jax.ShapeDtypeStruct((L, 2 * D), BF16), jax.ShapeDtypeStruct((8, 128), F32)],
        scratch=[pltpu.VMEM((H, 2, CHUNK, 128), F32)],
        cparams=_cparams(("arbitrary",)))


def _ctx_bwd(pc, pqk_c, ds0, lg, D):
    Lc = pc.shape[0]
    H = D // DV
    HP = H // 2
    W = D // 2

    def body(lg_ref, k_ref, v_ref, ds_ref, dr_ref, st_ref):
        dqk_ref = dr_ref.at[:, 0:D]
        dv_ref = dr_ref.at[:, D:2 * D]
        m = lax.broadcasted_iota(jnp.int32, (Lc, 128), 0).astype(F32)
        lane = lax.broadcasted_iota(jnp.int32, (1, 128), 1)
        acc_f = jnp.zeros((1, 128), F32)
        acc_b = jnp.zeros((1, 128), F32)
        dqk_ref[:, 0:W] = jnp.zeros((Lc, W), BF16)
        for pr in range(HP):
            ps = slice(pr * 128, (pr + 1) * 128)
            k2 = k_ref[:, ps].astype(F32) * K_SCALE
            dsfp, dsbp = ds_ref[0, pr].astype(BF16), ds_ref[1, pr].astype(BF16)
            dk2 = jnp.zeros((Lc, 128), F32)
            for e in range(2):
                h = 2 * pr + e
                sl = slice(h * DV, (h + 1) * DV)
                hm = _head_lane_mask(k2.shape, e)
                km = jnp.where(hm, k2, 0.0)
                v = v_ref[:, sl]
                vf = v.astype(F32)
                dec_f = jnp.exp(lg_ref[0, h] * (Lc - 1.0 - m))
                dec_b = jnp.exp(lg_ref[1, h] * m)
                kdf = _dot((km * dec_f).astype(BF16), dsfp)
                kdb = _dot((km * dec_b).astype(BF16), dsbp)
                dv_ref[:, sl] = (kdf + kdb).astype(BF16)
                dk2 += jnp.where(hm, dec_f * _dot_nt(v, dsfp) + dec_b * _dot_nt(v, dsbp), 0.0)
                acc_f += jnp.where(lane == h, _sum_all(kdf * vf * (Lc - 1.0 - m)), 0.0)
                acc_b += jnp.where(lane == h, _sum_all(kdb * vf * m), 0.0)
            dqk_ref[:, W + pr * 128:W + (pr + 1) * 128] = (dk2 * K_SCALE).astype(BF16)
        st_ref[...] = jnp.zeros_like(st_ref)
        st_ref[0:1, :] = acc_f
        st_ref[1:2, :] = acc_b

    return pl.pallas_call(
        body, name="ctx_bwd", grid=(1,),
        in_specs=[_smem_spec(), pl.BlockSpec((Lc, W), lambda i: (0, 1)), pl.BlockSpec((Lc, D), lambda i: (0, 1)),
                  pl.BlockSpec((2, HP, 128, 128), lambda i: (0, 0, 0, 0))],
        out_specs=[pl.BlockSpec((Lc, 2 * D), lambda i: (0, 0)), pl.BlockSpec((8, 128), lambda i: (0, 0))],
        out_shape=[jax.ShapeDtypeStruct((Lc, 2 * D), BF16), jax.ShapeDtypeStruct((8, 128), F32)],
    )(lg, pqk_c, pc, ds0)


class _Exchange(NamedTuple):
    inputs: tuple
    out_shapes: tuple
    n_copies: int
    build: Callable


def _exchange_parts(exchange):
    if exchange is None:
        return [], [], [], [], []
    n = exchange.n_copies
    return (list(exchange.inputs), [ANY] * len(exchange.inputs), list(exchange.out_shapes),
            [ANY] * len(exchange.out_shapes), [pltpu.SemaphoreType.DMA((n,)), pltpu.SemaphoreType.DMA((n,))])


def _riding_call(body, exchange, n_steps, *, args, in_specs, out_specs, out_shape, name, cparams, scratch=()):
    ex_args, ex_in_specs, ex_shapes, ex_out_specs, ex_scratch = _exchange_parts(exchange)
    n_in, n_out, n_sc = len(args), len(out_shape), len(scratch)

    def riding(*refs):
        k = n_in + len(ex_args)
        ins, ex_in = refs[:n_in], refs[n_in:k]
        outs, ex_out = refs[k:k + n_out], refs[k + n_out:k + n_out + len(ex_shapes)]
        k += n_out + len(ex_shapes)
        own_scratch, ex_sems = refs[k:k + n_sc], refs[k + n_sc:]
        step = pl.program_id(0)
        if exchange is not None:
            @pl.when(step == 0)
            def _():
                for rc in exchange.build(ex_in, ex_out, *ex_sems):
                    rc.start()
        body(*ins, *outs, *own_scratch)
        if exchange is not None:
            @pl.when(step == n_steps - 1)
            def _():
                for rc in exchange.build(ex_in, ex_out, *ex_sems):
                    rc.wait()

    return tuple(pl.pallas_call(
        riding, name=name, grid=(n_steps,),
        in_specs=list(in_specs) + ex_in_specs, out_specs=list(out_specs) + ex_out_specs,
        out_shape=list(out_shape) + ex_shapes, scratch_shapes=list(scratch) + ex_scratch,
        compiler_params=cparams,
    )(*args, *ex_args))


def _dxm(groups, col0, w, x, nw, sc, dx1, name, exchange=None):
    L, D = x.shape
    tm = min(256, L)
    nt = L // tm
    ng = len(groups)
    widths = [g.shape[1] for g in groups]
    wtot = sum(widths)
    with_dx = dx1 is not None
    ex_args, ex_in_specs, ex_shapes, ex_out_specs, ex_scratch = _exchange_parts(exchange)
    n_in = ng + 4 + (1 if with_dx else 0)
    n_out = 2 if with_dx else 1

    def body(*refs):
        group_refs = refs[:ng]
        w_hbm, x_ref, nw_ref, sc_ref = refs[ng:ng + 4]
        ex_in = refs[n_in:n_in + len(ex_args)]
        outs = refs[n_in + len(ex_args):]
        if with_dx:
            dx1_ref, gx_ref, st_ref = refs[ng + 4], outs[0], outs[1]
        else:
            st_ref = outs[0]
        ex_out = outs[n_out:n_out + len(ex_shapes)]
        w_vm, sem = outs[n_out + len(ex_shapes):n_out + len(ex_shapes) + 2]
        ex_sems = outs[n_out + len(ex_shapes) + 2:]
        i = pl.program_id(0)

        @pl.when(i == 0)
        def _():
            cp = pltpu.make_async_copy(w_hbm.at[:, col0 * D:col0 * D + wtot], w_vm, sem)
            cp.start()
            if exchange is not None:
                for rc in exchange.build(ex_in, ex_out, *ex_sems):
                    rc.start()
            st_ref[...] = jnp.zeros_like(st_ref)
            cp.wait()

        dxm, off = None, 0
        for g_ref, wd in zip(group_refs, widths):
            part = _dot_nt(g_ref[...], w_vm[:, off:off + wd])
            dxm = part if dxm is None else dxm + part
            off += wd

        xv = x_ref[...]
        r = lax.rsqrt(jnp.mean(xv * xv, axis=-1, keepdims=True) + EPS)
        xh = xv * r
        nwv = nw_ref[...]
        dxn = dxm * (1.0 + sc_ref[...])
        st_ref[0:1, :] += jnp.sum(dxm, axis=0, keepdims=True)
        st_ref[1:2, :] += jnp.sum(dxm * (xh * nwv), axis=0, keepdims=True)
        st_ref[2:3, :] += jnp.sum(dxn * xh, axis=0, keepdims=True)
        if with_dx:
            dxh = dxn * nwv
            gx_ref[...] = dx1_ref[...] + r * (dxh - xh * jnp.mean(dxh * xh, axis=-1, keepdims=True))

        if exchange is not None:
            @pl.when(i == nt - 1)
            def _():
                for rc in exchange.build(ex_in, ex_out, *ex_sems):
                    rc.wait()

    row = pl.BlockSpec((tm, D), lambda i: (i, 0))
    in_specs = [pl.BlockSpec((tm, wd), lambda i: (i, 0)) for wd in widths] + [ANY, row, _vec_spec(D), _vec_spec(D)]
    out_specs = [pl.BlockSpec((8, D), lambda i: (0, 0))]
    out_shape = [jax.ShapeDtypeStruct((8, D), F32)]
    args = list(groups) + [w, x, nw, sc]
    if with_dx:
        in_specs.append(row)
        out_specs.insert(0, row)
        out_shape.insert(0, jax.ShapeDtypeStruct((L, D), F32))
        args.append(dx1)
    res = pl.pallas_call(
        body, name=name, grid=(nt,),
        in_specs=in_specs + ex_in_specs, out_specs=out_specs + ex_out_specs, out_shape=out_shape + ex_shapes,
        scratch_shapes=[pltpu.VMEM((D, wtot), BF16), pltpu.SemaphoreType.DMA] + ex_scratch,
        compiler_params=_cparams(("arbitrary",), VMEM_LIMIT),
    )(*args, *ex_args)
    gx = res[0] if with_dx else None
    return (gx, res[n_out - 1], *res[n_out:])


DW_TN = 512
DW_RING = 4


def _dw_in(xmt, groups, cmt, dr_c, D, pair):
    L = xmt.shape[1]
    Lc = cmt.shape[1]
    Dh = D // 2
    tn = min(DW_TN, D)
    nblk = [g.shape[1] // tn for g in groups]
    starts = [sum(nblk[:g]) for g in range(len(groups))]
    ng = len(groups)
    nj = sum(nblk)
    rows_out = Dh if pair else D

    def body(*refs):
        xt_hbm = refs[0]
        group_refs = refs[1:1 + ng]
        ct_hbm, drc_ref, o_ref = refs[1 + ng:4 + ng]
        rest = refs[4 + ng:]
        if pair:
            ra_hbm, xt_vm, ct_vm, loc, ring, s_send, s_recv = rest
            pos = _position()
            sib = _peer(pos, 1)
        else:
            xt_vm, ct_vm, loc = rest
        j = pl.program_id(0)

        @pl.when(j == 0)
        def _():
            if pair:
                c = pos[2]
                other = pl.ds(pl.multiple_of((1 - c) * Dh, Dh), Dh)
                mine = pl.ds(pl.multiple_of(c * Dh, Dh), Dh)
                cps = [pltpu.make_async_copy(xt_hbm.at[other, :], xt_vm.at[0:Dh, :], loc.at[0]),
                       pltpu.make_async_copy(xt_hbm.at[mine, :], xt_vm.at[Dh:D, :], loc.at[1]),
                       pltpu.make_async_copy(ct_hbm.at[other, :], ct_vm.at[0:Dh, :], loc.at[2]),
                       pltpu.make_async_copy(ct_hbm.at[mine, :], ct_vm.at[Dh:D, :], loc.at[3])]
            else:
                cps = [pltpu.make_async_copy(xt_hbm, xt_vm, loc.at[0]), pltpu.make_async_copy(ct_hbm, ct_vm, loc.at[1])]
            for cp in cps:
                cp.start()
            for cp in cps:
                cp.wait()

        def send(slot):
            cols = pl.ds(pl.multiple_of(j * tn, 128), tn)
            return pltpu.make_async_remote_copy(src_ref=ring.at[slot], dst_ref=ra_hbm.at[:, cols],
                                                send_sem=s_send.at[slot], recv_sem=s_recv,
                                                device_id=sib, device_id_type=MESH)

        for g in range(ng):
            @pl.when((j >= starts[g]) & (j < starts[g] + nblk[g]))
            def _(g=g):
                acc = _dot(xt_vm[...], group_refs[g][...])
                if g == 1:
                    acc += _dot(ct_vm[...], drc_ref[...])
                if not pair:
                    o_ref[...] = acc
                    return
                o_ref[...] = acc[Dh:, :]
                slot = lax.rem(j, DW_RING)

                @pl.when(j >= DW_RING)
                def _():
                    send(slot).wait_send()

                ring[slot] = acc[0:Dh, :]
                send(slot).start()

        if pair:
            @pl.when(j == nj - 1)
            def _():
                pltpu.make_async_remote_copy(src_ref=ra_hbm, dst_ref=ra_hbm, send_sem=s_send.at[0], recv_sem=s_recv,
                                             device_id=sib, device_id_type=MESH).wait_recv()
                for slot in range(DW_RING):
                    send(slot).wait_send()

    def group_spec(g, rows):
        return pl.BlockSpec((rows, tn), lambda j: (0, jnp.clip(j - starts[g], 0, nblk[g] - 1)))

    out_specs = [pl.BlockSpec((rows_out, tn), lambda j: (0, j))]
    out_shape = [jax.ShapeDtypeStruct((rows_out, nj * tn), F32)]
    scratch = [pltpu.VMEM((D, L), BF16), pltpu.VMEM((D, Lc), BF16), pltpu.SemaphoreType.DMA((4,))]
    if pair:
        out_specs.append(ANY)
        out_shape.append(jax.ShapeDtypeStruct((Dh, nj * tn), F32))
        scratch += [pltpu.VMEM((DW_RING, Dh, tn), F32), pltpu.SemaphoreType.DMA((DW_RING,)), pltpu.SemaphoreType.DMA]
    return tuple(pl.pallas_call(
        body, name="dw_in", grid=(nj,),
        in_specs=[ANY] + [group_spec(g, L) for g in range(ng)] + [ANY, group_spec(1, Lc)],
        out_specs=out_specs, out_shape=out_shape, scratch_shapes=scratch,
        compiler_params=_cparams(("arbitrary",), VMEM_LIMIT),
    )(xmt, *groups, cmt, dr_c))


def _local_step(x, ctx, tgt, mod_x, mod_c, norm_w, conv_w8, conv_b, lg, gn_w, fw, project, csidx=None, w_in_s=None):
    L, D = x.shape
    sh_x, sc_x, g_x = mod_x[0:1], mod_x[1:2], mod_x[2:3]
    sh_c, sc_c = mod_c[0:1], mod_c[1:2]
    c2, s2 = _rope_tables(L)
    tab = _decay_tables(lg, D // DV)

    xm, xmt, *w_bf = _norm_mod(x, norm_w, sc_x, sh_x, "norm_mod_x", w_in_s)
    cm, cmt = _norm_mod(ctx, norm_w, sc_c, sh_c, "norm_mod_ctx")
    reduce = csidx is not None
    p, qr, kr, w_in, w3 = project(xm, c2, s2, *w_bf)
    pc, pqk_c = _in_proj(cm, w_in, "in_proj_ctx", QK_BLOCK, 2)
    s0 = _ctx_states(pc, pqk_c, lg, D)
    sf_prev, sb_prev = _ret_states(kr, p, s0, tab, D)
    o, yb = _ret_out(qr, kr, p, sf_prev, sb_prev, gn_w, tab, D)
    dx1, dya, do, dzb, dgab, dw3, st_mid = _mid(p, yb, o, x, tgt, w3, g_x, fw, conv_w8, conv_b, gn_w, D)
    dw3_5 = dw3.reshape(3, N_SHARD, 2, D // 8, D)
    dconv, st_conv, *ra_3 = _conv_bwd(dya, p, conv_w8, conv_b, D, _pair_exchange_w3(dw3_5) if reduce else None)
    dsf, dsb, ds0 = _ret_bwd_states(qr, do, tab, D)
    cs_3 = _sum_pair_w3(csidx[0:1], dw3_5, ra_3[0]) if reduce else None
    dret, st_lg, *rb_3 = _ret_bwd_main(qr, kr, p, do, sf_prev, sb_prev, dsf, dsb, c2, s2, tab, D,
                                       _chips_exchange_w3(cs_3) if reduce else None)
    g_3 = _sum_chips_w3(csidx, cs_3, rb_3[0]) if reduce else dw3
    dret_c, st_lgc = _ctx_bwd(pc, pqk_c, ds0, lg, D)
    groups = (dconv, dret, dzb, dgab)
    _, st_c = _dxm((dret_c,), 4, w_in, ctx, norm_w, sc_c, None, "dxm_ctx")
    return groups, dret_c, xmt, cmt, dx1, sc_x, w_in, g_3, (st_mid, st_conv, st_lg, st_lgc, st_c)


CHIP_FLIPS = (4, 2, 6)
ANY = pl.BlockSpec(memory_space=pl.ANY)
VMEM_FULL = pl.BlockSpec(memory_space=pltpu.VMEM)


def _position():
    return lax.axis_index("x"), lax.axis_index("y"), lax.axis_index("c")


def _peer(pos, k):
    x, y, c = pos
    return (1 - x if k & 4 else x, 1 - y if k & 2 else y, 1 - c if k & 1 else c)


def _dev_id(pos):
    return 4 * pos[0] + 2 * pos[1] + pos[2]


def _shard_of(pos):
    return 2 * pos[0] + pos[1]


def _remote(src, dst, send_sems, recv_sems, idx, to):
    return pltpu.make_async_remote_copy(src_ref=src, dst_ref=dst, send_sem=send_sems.at[idx],
                                        recv_sem=recv_sems.at[idx], device_id=to, device_id_type=MESH)


def _dot_f32(a, b):
    return jnp.dot(a, b, precision=lax.Precision.HIGHEST, preferred_element_type=F32)


def _silu(x):
    return x * _sigmoid(x)


def _fwd_small(c8, cctx8, ada_w, ada_b, conv_w8):
    D = c8.shape[1]
    Wm = ada_w.shape[1]
    Dq = conv_w8.shape[1]

    def body(c_ref, cc_ref, aw_ref, ab_ref, cw_ref, act_ref, mod_ref, cwf_ref,
             cbuf, pmine, pbuf, wbuf, s_c, r_c, s_p, r_p, s_w, r_w):
        pos = _position()
        me, s = _dev_id(pos), _shard_of(pos)
        cbuf[me] = c_ref[...]
        wbuf[s] = cw_ref[...]
        sends = [_remote(c_ref, cbuf.at[me], s_c, r_c, k - 1, _peer(pos, k)) for k in range(1, 8)]
        sends += [_remote(cw_ref, wbuf.at[s], s_w, r_w, j, _peer(pos, k)) for j, k in enumerate(CHIP_FLIPS)]
        for cp in sends:
            cp.start()
        for k in range(1, 8):
            _remote(c_ref, cbuf.at[_dev_id(_peer(pos, k))], s_c, r_c, k - 1, _peer(pos, k)).wait_recv()
        for d in range(N_DEV):
            act_ref[d:d + 1, :] = _silu(cbuf[d, 0:1, :])
        act_ref[8:9, :] = _silu(cc_ref[0:1, :])
        act_ref[9:16, :] = jnp.zeros((7, D), F32)
        part = _dot_f32(act_ref[...], aw_ref[...])
        pmine[...] = part
        pbuf[s] = part
        psend = [_remote(pmine, pbuf.at[s], s_p, r_p, j, _peer(pos, k)) for j, k in enumerate(CHIP_FLIPS)]
        for cp in psend:
            cp.start()
        for j, k in enumerate(CHIP_FLIPS):
            t = _shard_of(_peer(pos, k))
            _remote(pmine, pbuf.at[t], s_p, r_p, j, _peer(pos, k)).wait_recv()
            _remote(cw_ref, wbuf.at[t], s_w, r_w, j, _peer(pos, k)).wait_recv()
        for t in range(N_SHARD):
            mod_ref[:, t * Wm:(t + 1) * Wm] = pbuf[t] + ab_ref[:, t * Wm:(t + 1) * Wm]
            cwf_ref[:, t * Dq:(t + 1) * Dq] = wbuf[t]
        for cp in sends + psend:
            cp.wait_send()

    return pl.pallas_call(
        body, name="fwd_small",
        in_specs=[VMEM_FULL] * 5, out_specs=[VMEM_FULL] * 3,
        out_shape=[jax.ShapeDtypeStruct((16, D), F32), jax.ShapeDtypeStruct((16, 3 * D), F32),
                   jax.ShapeDtypeStruct((8, D), F32)],
        scratch_shapes=[pltpu.VMEM((N_DEV, 8, D), F32), pltpu.VMEM((16, Wm), F32),
                        pltpu.VMEM((N_SHARD, 16, Wm), F32), pltpu.VMEM((N_SHARD, 8, Dq), F32),
                        pltpu.SemaphoreType.DMA((7,)), pltpu.SemaphoreType.DMA((7,)),
                        pltpu.SemaphoreType.DMA((3,)), pltpu.SemaphoreType.DMA((3,)),
                        pltpu.SemaphoreType.DMA((3,)), pltpu.SemaphoreType.DMA((3,))],
        compiler_params=_cparams(None, VMEM_LIMIT),
    )(c8, cctx8, ada_w, ada_b, conv_w8)


AG_CHUNKS = 3


def _ag_in_proj(xm, w_in_s, w3_s, c2, s2):
    L, D = xm.shape
    Wc = w_in_s.shape[1]
    Wq = Wc // AG_CHUNKS
    Dh = D // 2
    Do = w3_s[0].shape[1]
    TM = min(1024, L // 4)
    NT = L // TM
    NQ = AG_CHUNKS
    order = [(q, j) for q in range(NQ) for j in (0, 1)] + [(q, 2) for q in range(NQ)]

    def body(xm_ref, wi_hbm, wa_ref, wb_ref, wo_ref, c_ref, s_ref, p_hbm, qr_hbm, kr_hbm, fi_hbm, f3_hbm,
             w_vm, s3, stage, qk_stage, ici_s, ici_r, d2d_s, d2d_r, w3_s_, w3_r_, fw_s, fw_r,
             loc, out_sem, qk_sem):
        pos = _position()
        c = pos[2]
        s = _shard_of(pos)
        sib = _peer(pos, 1)
        mine = pl.ds(pl.multiple_of(c * Dh, Dh), Dh)
        other = pl.ds(pl.multiple_of((1 - c) * Dh, Dh), Dh)

        def abs_col(t, q):
            return pl.ds(pl.multiple_of(t * Wc + q * Wq, 128), Wq)

        own = [pltpu.make_async_copy(wi_hbm.at[:, q * Wq:(q + 1) * Wq], w_vm.at[0, q], loc.at[2 + 4 * NQ + q])
               for q in range(NQ)]
        for cp in own:
            cp.start()
        for cp in own:
            cp.wait()
        sends = [_remote(w_vm.at[0, q, mine, :], w_vm.at[1 + j, q, mine, :], ici_s, ici_r, q * 3 + j,
                         _peer(pos, CHIP_FLIPS[j])) for q, j in order if j < 2]
        for cp in sends:
            cp.start()
        for a, w_ref in enumerate((wa_ref, wb_ref, wo_ref)):
            s3[a] = w_ref[...].astype(BF16)
        w3_sends = [_remote(s3.at[:, c], f3_hbm.at[:, s, c], w3_s_, w3_r_, j, _peer(pos, k))
                    for j, k in enumerate(CHIP_FLIPS)]
        local = [pltpu.make_async_copy(s3, f3_hbm.at[:, s], loc.at[1])]
        local += [pltpu.make_async_copy(w_vm.at[0, q], fi_hbm.at[:, abs_col(s, q)], loc.at[2 + q]) for q in range(NQ)]
        for cp in local:
            cp.start()

        def out_copy(slot, rows, cols):
            return pltpu.make_async_copy(stage.at[slot], p_hbm.at[rows, cols], out_sem.at[slot])

        def block(r, q, t, first):
            cols = abs_col(t, q)

            def row_tile(rt, carry):
                rows = pl.ds(pl.multiple_of(rt * TM, TM), TM)
                acc = _dot(xm_ref[rows, :], w_vm[r, q])
                slot = lax.rem(rt, 2)

                @pl.when(rt >= 2 if first else rt >= 0)
                def _():
                    out_copy(slot, rows, cols).wait()

                stage[slot] = acc.astype(BF16)
                out_copy(slot, rows, cols).start()

                def rotary(lo, scale, dst_hbm):
                    c, s = c_ref[rows, :], s_ref[rows, :]
                    for pr in range(Dh // 128):
                        tq = acc[:, lo + pr * 128:lo + (pr + 1) * 128] * scale
                        qk_stage[:, pr * 128:(pr + 1) * 128] = (tq * c + _swap_halves(tq) * s).astype(BF16)
                    cp = pltpu.make_async_copy(qk_stage, dst_hbm.at[rows, :], qk_sem)
                    cp.start()
                    cp.wait()

                if q == NQ - 1:
                    @pl.when(t == 1)
                    def _():
                        rotary(Wq - Dh, 1.0, qr_hbm)
                if q == 0:
                    @pl.when(t == 2)
                    def _():
                        rotary(0, K_SCALE, kr_hbm)
                return carry

            lax.fori_loop(0, NT, row_tile, 0)

        passed = []

        def hand_on(q, j):
            half = w_vm.at[1 + j, q, mine, :]
            if j == 2:
                _remote(half, half, fw_s, fw_r, q, sib).wait_recv()
            else:
                _remote(half, half, ici_s, ici_r, q * 3 + j, sib).wait_recv()

                @pl.when(c == (0 if j == q % 2 else 1))
                def _():
                    _remote(half, w_vm.at[3, q, mine, :], fw_s, fw_r, q, _peer(pos, CHIP_FLIPS[1 - j])).start()
            fwd = _remote(half, half, d2d_s, d2d_r, q * 3 + j, sib)
            fwd.start()
            passed.append(fwd)

        for q in range(NQ):
            if q == NQ - 1:
                hand_on(*order[0])
            block(0, q, s, q == 0)
        for n, (q, j) in enumerate(order):
            r, idx = 1 + j, q * 3 + j
            t = _shard_of(_peer(pos, CHIP_FLIPS[j]))
            if n + 1 < len(order):
                hand_on(*order[n + 1])
            if n + 1 == 2 * NQ - 1:
                for cp in w3_sends:
                    cp.start()
            _remote(w_vm.at[r, q, other, :], w_vm.at[r, q, other, :], d2d_s, d2d_r, idx, sib).wait_recv()
            block(r, q, t, False)
            cp = pltpu.make_async_copy(w_vm.at[r, q], fi_hbm.at[:, abs_col(t, q)], loc.at[2 + NQ + idx])
            cp.start()
            local.append(cp)
        for j, k in enumerate(CHIP_FLIPS):
            t = _shard_of(_peer(pos, k))
            _remote(s3.at[:, c], f3_hbm.at[:, t, c], w3_s_, w3_r_, j, sib).wait_recv()
            fwd = _remote(f3_hbm.at[:, t, c], f3_hbm.at[:, t, c], w3_s_, w3_r_, 3 + j, sib)
            fwd.start()
            passed.append(fwd)
        for j, k in enumerate(CHIP_FLIPS):
            t = _shard_of(_peer(pos, k))
            _remote(s3.at[:, c], f3_hbm.at[:, t, 1 - c], w3_s_, w3_r_, 3 + j, sib).wait_recv()
        for cp in sends + w3_sends + passed:
            cp.wait_send()
        for q in range(NQ):
            _remote(w_vm.at[1, q, mine, :], w_vm.at[3, q, mine, :], fw_s, fw_r, q, sib).wait_send()
        for cp in local:
            cp.wait()
        for slot in range(2):
            out_copy(slot, pl.ds(0, TM), abs_col(s, 0)).wait()

    n_loc = 2 + 5 * NQ
    return pl.pallas_call(
        body, name="ag_in_proj",
        in_specs=[VMEM_FULL, ANY, VMEM_FULL, VMEM_FULL, VMEM_FULL, VMEM_FULL, VMEM_FULL], out_specs=[ANY] * 5,
        out_shape=[jax.ShapeDtypeStruct((L, N_SHARD * Wc), BF16),
                   jax.ShapeDtypeStruct((L, Dh), BF16), jax.ShapeDtypeStruct((L, Dh), BF16),
                   jax.ShapeDtypeStruct((D, N_SHARD * Wc), BF16), jax.ShapeDtypeStruct((3, N_SHARD, 2, Do, D), BF16)],
        scratch_shapes=[pltpu.VMEM((N_SHARD, NQ, D, Wq), BF16), pltpu.VMEM((3, 2, Do, D), BF16), pltpu.VMEM((2, TM, Wq), BF16), pltpu.VMEM((TM, Dh), BF16),
                        pltpu.SemaphoreType.DMA((3 * NQ,)), pltpu.SemaphoreType.DMA((3 * NQ,)),
                        pltpu.SemaphoreType.DMA((3 * NQ,)), pltpu.SemaphoreType.DMA((3 * NQ,)),
                        pltpu.SemaphoreType.DMA((6,)), pltpu.SemaphoreType.DMA((6,)),
                        pltpu.SemaphoreType.DMA((NQ,)), pltpu.SemaphoreType.DMA((NQ,)),
                        pltpu.SemaphoreType.DMA((n_loc,)), pltpu.SemaphoreType.DMA((2,)), pltpu.SemaphoreType.DMA],
        compiler_params=_cparams(None, VMEM_LIMIT),
    )(xm, w_in_s, *w3_s, c2, s2)


def _pair_exchange_w3(dw3):
    _, _, _, Do, D = dw3.shape

    def build(ins, outs, send, recv):
        pos = _position()
        return [_remote(ins[0].at[:, :, 1 - pos[2]], outs[0], send, recv, 0, _peer(pos, 1))]

    return _Exchange((dw3,), (jax.ShapeDtypeStruct((3, N_SHARD, Do, D), F32),), 1, build)


def _sum_pair_in(dw_mine, ri):
    Dh, Wf = dw_mine.shape
    Wc = Wf // N_SHARD
    tr = min(256, Dh)

    def body(a_ref, b_ref, o_ref):
        o_ref[...] = (a_ref[...] + b_ref[...]).astype(BF16)

    return pl.pallas_call(
        body, name="sum_pair_in", grid=(Dh // tr, N_SHARD),
        in_specs=[pl.BlockSpec((tr, Wc), lambda i, t: (i, t)), pl.BlockSpec((tr, Wc), lambda i, t: (i, t))],
        out_specs=pl.BlockSpec((None, tr, Wc), lambda i, t: (t, i, 0)),
        out_shape=jax.ShapeDtypeStruct((N_SHARD, Dh, Wc), BF16),
        compiler_params=_cparams(("parallel", "parallel")),
    )(dw_mine, ri)


def _sum_pair_w3(cidx, dw3, r3):
    _, _, _, Do, D = dw3.shape

    def body(c_ref, a_ref, b_ref, o_ref):
        o_ref[...] = (a_ref[...] + b_ref[...]).astype(BF16)

    return pl.pallas_call(
        body, name="sum_pair_w3",
        grid_spec=pltpu.PrefetchScalarGridSpec(
            num_scalar_prefetch=1, grid=(3,),
            in_specs=[pl.BlockSpec((None, N_SHARD, None, Do, D), lambda a, c: (a, 0, c[0], 0, 0)),
                      pl.BlockSpec((None, N_SHARD, Do, D), lambda a, c: (a, 0, 0, 0))],
            out_specs=pl.BlockSpec((None, N_SHARD, Do, D), lambda a, c: (a, 0, 0, 0))),
        out_shape=jax.ShapeDtypeStruct((3, N_SHARD, Do, D), BF16),
        compiler_params=_cparams(("parallel",)),
    )(cidx, dw3, r3)


def _chips_exchange_in(cs_in):
    _, Dh, Wc = cs_in.shape

    def build(ins, outs, send, recv):
        pos = _position()
        return [_remote(ins[0].at[_shard_of(_peer(pos, k))], outs[0].at[j], send, recv, j, _peer(pos, k))
                for j, k in enumerate(CHIP_FLIPS)]

    return _Exchange((cs_in,), (jax.ShapeDtypeStruct((3, Dh, Wc), BF16),), 3, build)


def _chips_exchange_w3(cs_3):
    _, _, Do, D = cs_3.shape

    def build(ins, outs, send, recv):
        pos = _position()
        return [_remote(ins[0].at[:, _shard_of(_peer(pos, k))], outs[0].at[j], send, recv, j, _peer(pos, k))
                for j, k in enumerate(CHIP_FLIPS)]

    return _Exchange((cs_3,), (jax.ShapeDtypeStruct((3, 3, Do, D), BF16),), 3, build)


def _sum_chips_in(csidx, cs_in, rb_in):
    _, Dh, Wc = cs_in.shape
    tr = min(256, Dh)

    def body(s_ref, a_ref, b_ref, o_ref):
        acc = a_ref[...].astype(F32)
        for j in range(3):
            acc = acc + b_ref[j].astype(F32)
        o_ref[...] = acc

    return pl.pallas_call(
        body, name="sum_chips_in",
        grid_spec=pltpu.PrefetchScalarGridSpec(
            num_scalar_prefetch=1, grid=(Dh // tr,),
            in_specs=[pl.BlockSpec((None, tr, Wc), lambda i, s: (s[1], i, 0)),
                      pl.BlockSpec((3, tr, Wc), lambda i, s: (0, i, 0))],
            out_specs=pl.BlockSpec((None, tr, Wc), lambda i, s: (s[0], i, 0))),
        out_shape=jax.ShapeDtypeStruct((2, Dh, Wc), F32),
        compiler_params=_cparams(("parallel",)),
    )(csidx, cs_in, rb_in)


def _sum_chips_w3(csidx, cs_3, rb_3):
    _, _, Do, D = cs_3.shape

    def body(s_ref, a_ref, b_ref, o_ref):
        acc = a_ref[...].astype(F32)
        for j in range(3):
            acc = acc + b_ref[j].astype(F32)
        o_ref[...] = acc

    return pl.pallas_call(
        body, name="sum_chips_w3",
        grid_spec=pltpu.PrefetchScalarGridSpec(
            num_scalar_prefetch=1, grid=(3,),
            in_specs=[pl.BlockSpec((None, None, Do, D), lambda a, s: (a, s[1], 0, 0)),
                      pl.BlockSpec((3, None, Do, D), lambda a, s: (0, a, 0, 0))],
            out_specs=pl.BlockSpec((None, None, Do, D), lambda a, s: (a, s[0], 0, 0))),
        out_shape=jax.ShapeDtypeStruct((3, 2, Do, D), F32),
        compiler_params=_cparams(("parallel",)),
    )(csidx, cs_3, rb_3)


def _adam_math(w, g, m, v):
    m = ADAM_B1 * m + (1.0 - ADAM_B1) * g
    v = ADAM_B2 * v + (1.0 - ADAM_B2) * (g * g)
    m_hat = m / (1.0 - ADAM_B1 ** ADAM_STEP)
    v_hat = v / (1.0 - ADAM_B2 ** ADAM_STEP)
    delta = -ADAM_LR * (m_hat / (jnp.sqrt(v_hat) + ADAM_EPS) + ADAM_WD * w)
    return delta, m, v


def _adamw(w, g, m, v, name):
    R, C = w.shape
    tr = min(128, R)

    def body(w_ref, g_ref, m_ref, v_ref, d_ref, nm_ref, nv_ref):
        d_ref[...], nm_ref[...], nv_ref[...] = _adam_math(w_ref[...], g_ref[...], m_ref[...], v_ref[...])

    blk = pl.BlockSpec((tr, C), lambda i: (i, 0))
    return pl.pallas_call(
        body, name=name, grid=(R // tr,), in_specs=[blk] * 4, out_specs=[blk] * 3,
        out_shape=[jax.ShapeDtypeStruct((R, C), F32)] * 3,
        compiler_params=_cparams(("parallel",), VMEM_LIMIT),
    )(w, g, m, v)


def _adamw3(ws, g3, ms, vs):
    R, C = ws[0].shape

    def body(*refs):
        w_refs, m_refs, v_refs = refs[0:3], refs[3:6], refs[6:9]
        g_ref, outs = refs[9], refs[10:]
        for a in range(3):
            @pl.when(pl.program_id(0) == a)
            def _(a=a):
                res = _adam_math(w_refs[a][...], g_ref[...], m_refs[a][...], v_refs[a][...])
                for q in range(3):
                    outs[3 * a + q][...] = res[q]

    full = pl.BlockSpec((R, C), lambda a: (0, 0))
    res = pl.pallas_call(
        body, name="adamw_w3", grid=(3,),
        in_specs=[full] * 9 + [pl.BlockSpec((None, R, C), lambda a: (a, 0, 0))], out_specs=[full] * 9,
        out_shape=[jax.ShapeDtypeStruct((R, C), F32)] * 9,
        compiler_params=_cparams(("arbitrary",), VMEM_LIMIT),
    )(*ws, *ms, *vs, g3)
    return res[0:3], res[3:6], res[6:9]


SMALL_ROWS = ("c_ctx", "norm_w", "conv_b", "gn_w", "final_norm_w")


def _bwd_small(stats, ada_w, Dq, gh_in, gh_3):
    D = stats[0].shape[1]
    Wm = ada_w.shape[1]

    def body(stx, stm, stc, stv, stl, stlc, aw_ref, gi_in, g3_in, tot_ref, dm_sh, gcw, da_ref, gi_ref, g3_ref,
             vec_ref, vbuf, dm, amine, abuf, s_v, r_v, s_a, r_a, s_g, r_g):
        pos = _position()
        me, s = _dev_id(pos), _shard_of(pos)
        c, sib = pos[2], _peer(pos, 1)
        halves = [_remote(gi_in.at[c], gi_ref.at[c], s_g, r_g, 0, sib),
                  _remote(g3_in.at[:, c], g3_ref.at[:, c], s_g, r_g, 1, sib)]
        for cp in halves:
            cp.start()
        vec_ref[...] = jnp.zeros_like(vec_ref)
        vec_ref[0:2, :] = stx[0:2, :]
        vec_ref[2:3, :] = stm[1:2, :]
        vec_ref[3:5, :] = stc[0:2, :]
        vec_ref[5:6, :] = stx[2:3, :] + stc[2:3, :]
        vec_ref[6:7, :] = stv[3:4, :]
        vec_ref[7:8, :] = stm[3:4, :]
        vec_ref[8:9, :] = stm[0:1, :]
        vec_ref[9:12, :] = stv[0:3, :]
        vec_ref[12:14, 0:128] = stl[0:2, :] + stlc[0:2, :]
        vec_ref[14:15, :] = stm[2:3, :]
        vbuf[me] = vec_ref[...]
        sends = [_remote(vec_ref, vbuf.at[me], s_v, r_v, k - 1, _peer(pos, k)) for k in range(1, 8)]
        for cp in sends:
            cp.start()
        for k in range(1, 8):
            _remote(vec_ref, vbuf.at[_dev_id(_peer(pos, k))], s_v, r_v, k - 1, _peer(pos, k)).wait_recv()
        tot = vbuf[0]
        for d in range(1, N_DEV):
            tot = tot + vbuf[d]
        dm[...] = jnp.zeros_like(dm)
        for d in range(N_DEV):
            for r in range(3):
                dm[d:d + 1, r * D:(r + 1) * D] = vbuf[d, r:r + 1, :]
        dm[8:9, 0:D] = tot[3:4, :]
        dm[8:9, D:2 * D] = tot[4:5, :]
        for t in range(N_SHARD):
            @pl.when(s == t)
            def _(t=t):
                dm_sh[...] = dm[:, t * Wm:(t + 1) * Wm]
                gcw[...] = tot[9:12, t * Dq:(t + 1) * Dq]
        tot_ref[...] = tot
        part = lax.dot_general(dm_sh[8:16, :], aw_ref[...], (((1,), (1,)), ((), ())),
                               precision=lax.Precision.HIGHEST, preferred_element_type=F32)
        amine[...] = part
        abuf[s] = part
        asend = [_remote(amine, abuf.at[s], s_a, r_a, j, _peer(pos, k)) for j, k in enumerate(CHIP_FLIPS)]
        for cp in asend:
            cp.start()
        for j, k in enumerate(CHIP_FLIPS):
            _remote(amine, abuf.at[_shard_of(_peer(pos, k))], s_a, r_a, j, _peer(pos, k)).wait_recv()
        da = abuf[0]
        for t in range(1, N_SHARD):
            da = da + abuf[t]
        da_ref[...] = da
        _remote(gi_in.at[1 - c], gi_ref.at[1 - c], s_g, r_g, 0, sib).wait_recv()
        _remote(g3_in.at[:, 1 - c], g3_ref.at[:, 1 - c], s_g, r_g, 1, sib).wait_recv()
        for cp in sends + asend + halves:
            cp.wait_send()

    row = lambda *shape: jax.ShapeDtypeStruct(shape, F32)
    return pl.pallas_call(
        body, name="bwd_small",
        in_specs=[VMEM_FULL] * 7 + [ANY, ANY], out_specs=[VMEM_FULL] * 4 + [ANY, ANY],
        input_output_aliases={7: 4, 8: 5},
        out_shape=[row(16, D), row(16, Wm), row(3, Dq), row(8, D), row(*gh_in.shape), row(*gh_3.shape)],
        scratch_shapes=[pltpu.VMEM((16, D), F32), pltpu.VMEM((N_DEV, 16, D), F32), pltpu.VMEM((16, 3 * D), F32),
                        pltpu.VMEM((8, D), F32), pltpu.VMEM((N_SHARD, 8, D), F32),
                        pltpu.SemaphoreType.DMA((7,)), pltpu.SemaphoreType.DMA((7,)),
                        pltpu.SemaphoreType.DMA((3,)), pltpu.SemaphoreType.DMA((3,)),
                        pltpu.SemaphoreType.DMA((2,)), pltpu.SemaphoreType.DMA((2,))],
        compiler_params=_cparams(None, VMEM_LIMIT),
    )(*stats, ada_w, gh_in, gh_3)


def _small_update(tot, dm_sh, gcw, da, act, rows, ab, cw, dl):
    D = act.shape[1]
    Wm = dm_sh.shape[1]
    Dq = gcw.shape[1]
    H = dl[0].shape[1]
    params = tuple(rows) + (ab, cw, dl)
    n_p = len(params)
    whole = (slice(None), slice(None))

    def body(tot_ref, dm_ref, gcw_ref, da_ref, act_ref, *refs):
        wmv = [refs[3 * k:3 * k + 3] for k in range(n_p)]
        gaw_ref, loss_ref = refs[3 * n_p:3 * n_p + 2]
        outs = refs[3 * n_p + 2:]
        o_q = [outs[n_p * q:n_p * (q + 1)] for q in range(4)]
        tot = tot_ref[...]
        gaw_ref[...] = lax.dot_general(act_ref[...], dm_ref[...], (((0,), (0,)), ((), ())),
                                       precision=lax.Precision.HIGHEST, preferred_element_type=F32)
        loss_ref[...] = (0.5 / D) * _sum_all(tot[14:15, :])
        cc = wmv[0][0][...]
        sg = _sigmoid(cc)
        g_cctx = da_ref[0:1, :] * (sg * (1.0 + cc * (1.0 - sg)))

        def emit(k, g, at=whole):
            w_ref, m_ref, v_ref = wmv[k]
            for q, val in enumerate((g,) + _adam_math(w_ref[at], g, m_ref[at], v_ref[at])):
                o_q[q][k][at] = val

        for k, g in enumerate([g_cctx, tot[5:6, :], tot[6:7, :], tot[7:8, :], tot[8:9, :]]):
            emit(k, g)
        for r, g in enumerate([tot[0:1, :] + tot[3:4, :], tot[1:2, :] + tot[4:5, :], tot[2:3, :]]):
            emit(n_p - 3, g, (slice(0, 1), slice(r * D, (r + 1) * D)))
        emit(n_p - 2, gcw_ref[...])
        emit(n_p - 1, tot[12:14, 0:H] * _sigmoid(-wmv[n_p - 1][0][...]))

    row = lambda *shape: jax.ShapeDtypeStruct(shape, F32)
    per_q = [row(1, D)] * len(rows) + [row(1, 3 * D), row(3, Dq), row(2, H)]
    res = pl.pallas_call(
        body, name="small_update",
        in_specs=[VMEM_FULL] * (5 + 3 * n_p), out_specs=[VMEM_FULL] * (2 + 4 * n_p),
        out_shape=[row(D, Wm), row(1, 1)] + per_q * 4,
        compiler_params=_cparams(None, VMEM_LIMIT),
    )(tot, dm_sh, gcw, da, act, *[a for p in params for a in p])
    return res[0], res[1], [res[2 + n_p * q:2 + n_p * (q + 1)] for q in range(4)]


def _pad_rows(a, rows=8):
    return jnp.pad(a, ((0, rows - a.shape[0]), (0, 0)))


def kernel(x, c, ctx, c_ctx, norm_w, ada_w, ada_b, w_in, conv_w, conv_b, decay_logit, gn_w, w_a, w_b, w_out, final_norm_w, loss_target, m_c_ctx, m_norm_w, m_ada_w, m_ada_b, m_w_in, m_conv_w, m_conv_b, m_decay_logit, m_gn_w, m_w_a, m_w_b, m_w_out, m_final_norm_w, v_c_ctx, v_norm_w, v_ada_w, v_ada_b, v_w_in, v_conv_w, v_conv_b, v_decay_logit, v_gn_w, v_w_a, v_w_b, v_w_out, v_final_norm_w):
    L, D = x.shape[1], x.shape[2]
    Wc = w_in.shape[2]
    Do = D // 8
    pos = _position()
    me = _dev_id(pos)
    cidx = jnp.reshape(pos[2], (1,)).astype(jnp.int32)
    sidx = jnp.reshape(_shard_of(pos), (1,)).astype(jnp.int32)

    act, mod, conv_w8 = _fwd_small(_pad_rows(c), _pad_rows(c_ctx[None]), ada_w[0], ada_b, _pad_rows(conv_w[0]))
    mod_x = lax.dynamic_slice_in_dim(mod, me, 1, axis=0).reshape(3, D)
    mod_c = mod[8].reshape(3, D)
    lg = jax.nn.log_sigmoid(decay_logit[0])

    w3_s = tuple(w[0].reshape(2, Do, D) for w in (w_a, w_b, w_out))

    def project(xm, c2, s2, w_in_bf):
        p, qr, kr, w_in_full, w3_full = _ag_in_proj(xm, w_in_bf, w3_s, c2, s2)
        return p, qr, kr, w_in_full, w3_full.reshape(3, D, D)

    csidx = jnp.concatenate([cidx, sidx])
    groups, dret_c, xmt, cmt, dx1, sc_x, w_in_full, gh_3, sts = _local_step(
        x[0], ctx[0], loss_target[0], mod_x, mod_c, norm_w, conv_w8, conv_b, lg, gn_w, final_norm_w[None],
        project, csidx, w_in[0])
    st_mid, st_conv, st_lg, st_lgc, st_c = sts

    dw_mine, ra_in = _dw_in(xmt, groups, cmt, dret_c, D, True)
    cs_in = _sum_pair_in(dw_mine, ra_in)
    grad_x, st_x, rb_in = _dxm(groups, 0, w_in_full, x[0], norm_w, sc_x, dx1, "dxm_x", _chips_exchange_in(cs_in))
    gh_in = _sum_chips_in(csidx, cs_in, rb_in)

    tot, dm_sh, gcw, da, g_in, g_3 = _bwd_small((st_x, st_mid, st_c, st_conv, st_lg, st_lgc), ada_w[0],
                                                conv_w.shape[2], gh_in, gh_3)
    g_w_in = g_in.reshape(D, Wc)
    g_3 = g_3.reshape(3, D // 4, D)
    rows = ((c_ctx[None], m_c_ctx[None], v_c_ctx[None]), (norm_w, m_norm_w, v_norm_w), (conv_b, m_conv_b, v_conv_b),
            (gn_w, m_gn_w, v_gn_w), (final_norm_w[None], m_final_norm_w[None], v_final_norm_w[None]))
    g_ada_w, loss, small = _small_update(tot, dm_sh, gcw, da, act, rows, (ada_b, m_ada_b, v_ada_b),
                                         (conv_w[0], m_conv_w[0], v_conv_w[0]),
                                         (decay_logit[0], m_decay_logit[0], v_decay_logit[0]))

    upd_in = _adamw(w_in[0], g_w_in, m_w_in[0], v_w_in[0], "adamw_w_in")
    upd_ada = _adamw(ada_w[0], g_ada_w, m_ada_w[0], v_ada_w[0], "adamw_ada_w")
    upd_a, upd_b, upd_o = _adamw3((w_a[0], w_b[0], w_out[0]), g_3, (m_w_a[0], m_w_b[0], m_w_out[0]),
                                  (v_w_a[0], v_w_b[0], v_w_out[0]))

    def leaves(q):
        big = lambda g, upd: (g if q == 0 else upd[q - 1])[None]
        r_cctx, r_norm, r_convb, r_gn, r_fnorm, r_ab, r_cw, r_dl = small[q]
        return [r_cctx.reshape(D), r_norm, big(g_ada_w, upd_ada), r_ab, big(g_w_in, upd_in),
                r_cw[None], r_convb, r_dl[None], r_gn,
                big(g_3[0], upd_a), big(g_3[1], upd_b), big(g_3[2], upd_o), r_fnorm.reshape(D)]

    return (loss.reshape(()), grad_x[None], *leaves(0), *leaves(1), *leaves(2), *leaves(3))
```

```python
from typing import Callable, NamedTuple

import jax
import jax.numpy as jnp
import numpy as np
from jax import lax
from jax.experimental import pallas as pl
from jax.experimental.pallas import tpu as pltpu

F32 = jnp.float32
BF16 = jnp.bfloat16
MESH = pl.DeviceIdType.MESH

CHUNK = 128
RET_CPB = 4
DV = 128
DK = 64
GRID_W = 64
ROPE_BASE = 10000.0
EPS = 1e-6
K_SCALE = DK ** -0.5
N_SHARD = 4
N_DEV = 8

ADAM_LR = 0.001
ADAM_B1 = 0.9
ADAM_B2 = 0.999
ADAM_EPS = 1e-08
ADAM_WD = 0.01
ADAM_STEP = 10

VMEM_LIMIT = 56 * 1024 * 1024


def _cparams(sem=None, vmem=None):
    kw = {}
    if sem is not None:
        kw["dimension_semantics"] = sem
    if vmem is not None:
        kw["vmem_limit_bytes"] = vmem
    return pltpu.CompilerParams(**kw)


def _dot(a, b):
    return jnp.dot(a, b, preferred_element_type=F32)


def _dot_nt(a, b):
    return lax.dot_general(a, b, (((1,), (1,)), ((), ())), preferred_element_type=F32)


def _dot_tn(a, b):
    return lax.dot_general(a, b, (((0,), (0,)), ((), ())), preferred_element_type=F32)


def _sigmoid(x):
    return 1.0 / (1.0 + jnp.exp(-x))


def _sum_all(x):
    return jnp.sum(jnp.sum(x, axis=1, keepdims=True), axis=0, keepdims=True)


def _swap_halves(t):
    n = t.shape[1]
    lane = lax.broadcasted_iota(jnp.int32, t.shape, 1)
    low = (lane & 32) == 0
    return jnp.where(low, pltpu.roll(t, n - 32, 1), pltpu.roll(t, 32, 1))


def _vec_spec(d):
    return pl.BlockSpec((1, d), lambda *a: (0, 0))


def _norm_mod(x, nw, sc, sh, name, also_bf16=None):
    L, D = x.shape
    tl = min(256, L)
    nt = L // tl

    def body(x_ref, nw_ref, sc_ref, sh_ref, *rest):
        xm_ref, xmt_ref = rest[-3:-1] if also_bf16 is not None else rest
        xv = x_ref[...]
        r = lax.rsqrt(jnp.mean(xv * xv, axis=-1, keepdims=True) + EPS)
        xm = (xv * r * nw_ref[...]) * (1.0 + sc_ref[...]) + sh_ref[...]
        xm_b = xm.astype(BF16)
        xm_ref[...] = xm_b
        xmt_ref[...] = xm_b.T
        if also_bf16 is not None:
            rest[-1][...] = rest[0][...].astype(BF16)

    in_specs = [pl.BlockSpec((tl, D), lambda i: (i, 0)), _vec_spec(D), _vec_spec(D), _vec_spec(D)]
    out_specs = [pl.BlockSpec((tl, D), lambda i: (i, 0)), pl.BlockSpec((D, tl), lambda i: (0, i))]
    out_shape = [jax.ShapeDtypeStruct((L, D), BF16), jax.ShapeDtypeStruct((D, L), BF16)]
    args = [x, nw, sc, sh]
    if also_bf16 is not None:
        R, C = also_bf16.shape
        slab = pl.BlockSpec((R // nt, C), lambda i: (i, 0))
        in_specs.append(slab)
        out_specs.append(slab)
        out_shape.append(jax.ShapeDtypeStruct((R, C), BF16))
        args.append(also_bf16)
    return pl.pallas_call(
        body, name=name, grid=(nt,), in_specs=in_specs, out_specs=out_specs, out_shape=out_shape,
        compiler_params=_cparams(("parallel",)),
    )(*args)


QK_BLOCK, V_BLOCK = 4, 5


def _in_proj(xm, w, name, first=0, count=None):
    M, D = xm.shape
    count = w.shape[1] // D if count is None else count
    tm = min(1024, M)

    def body(a_ref, b_ref, o_ref, qk_ref):
        acc = _dot(a_ref[...], b_ref[...])
        o_ref[...] = acc.astype(o_ref.dtype)

        @pl.when(pl.program_id(1) == QK_BLOCK - first)
        def _():
            qk_ref[...] = acc

    return pl.pallas_call(
        body, name=name, grid=(M // tm, count),
        in_specs=[pl.BlockSpec((tm, D), lambda i, j: (i, 0)), pl.BlockSpec((D, D), lambda i, j: (0, first + j))],
        out_specs=[pl.BlockSpec((tm, D), lambda i, j: (i, j)), pl.BlockSpec((tm, D), lambda i, j: (i, 0))],
        out_shape=[jax.ShapeDtypeStruct((M, count * D), BF16), jax.ShapeDtypeStruct((M, D), F32)],
        compiler_params=_cparams(("parallel", "arbitrary")),
    )(xm, w)


def _halo_specs(tl, L, D, col):
    hb = tl // 16
    last = L // 16 - 1
    prev = pl.BlockSpec((16, D), lambda i: (jnp.maximum(i * hb - 1, 0), col))
    nxt = pl.BlockSpec((16, D), lambda i: (jnp.minimum((i + 1) * hb, last), col))
    return prev, nxt


def _shift_rows(u, above, below):
    tl = u.shape[0]
    row = lax.broadcasted_iota(jnp.int32, u.shape, 0)
    dn = jnp.where(row == 0, above, pltpu.roll(u, 1, 0))
    up = jnp.where(row == tl - 1, below, pltpu.roll(u, tl - 1, 0))
    return dn, up


def _rope_tables(L):
    pos = np.arange(L)
    row = (pos // GRID_W).astype(np.float32)
    col = (pos % GRID_W).astype(np.float32)
    nf = DK // 4
    inv = np.float32(ROPE_BASE) ** (-np.arange(nf, dtype=np.float32) / np.float32(nf))
    ang = np.concatenate([row[:, None] * inv, col[:, None] * inv], axis=-1).astype(np.float32)
    cos, sin = np.cos(ang), np.sin(ang)
    return (jnp.asarray(np.concatenate([cos, cos, cos, cos], axis=-1), F32),
            jnp.asarray(np.concatenate([-sin, sin, -sin, sin], axis=-1), F32))


def _smem_spec():
    return pl.BlockSpec(memory_space=pltpu.SMEM)


def _pair_select(e0, e1):
    row = lax.broadcasted_iota(jnp.int32, e0.shape, 0)
    return jnp.where(row < DK, e0, e1)


def _head_lane_mask(shape, e):
    lane = lax.broadcasted_iota(jnp.int32, shape, 1)
    return (lane < DK) if e == 0 else (lane >= DK)


def _ctx_states(pc, pqk_c, lg, D):
    Lc = pc.shape[0]
    H = D // DV

    def body(lg_ref, k_ref, v_ref, s_ref):
        m = lax.broadcasted_iota(jnp.int32, (Lc, DV), 0).astype(F32)
        for pr in range(H // 2):
            k2 = k_ref[:, pr * 128:(pr + 1) * 128].astype(F32) * K_SCALE
            res = [[None, None], [None, None]]
            for e in range(2):
                h = 2 * pr + e
                v = v_ref[:, h * DV:(h + 1) * DV]
                dec_f = jnp.exp(lg_ref[0, h] * (Lc - 1.0 - m))
                dec_b = jnp.exp(lg_ref[1, h] * m)
                res[0][e] = _dot_tn((k2 * dec_f).astype(BF16), v)
                res[1][e] = _dot_tn((k2 * dec_b).astype(BF16), v)
            s_ref[0, pr] = _pair_select(res[0][0], res[0][1])
            s_ref[1, pr] = _pair_select(res[1][0], res[1][1])

    return pl.pallas_call(
        body, name="ctx_states", grid=(1,),
        in_specs=[_smem_spec(), pl.BlockSpec((Lc, D // 2), lambda i: (0, 1)), pl.BlockSpec((Lc, D), lambda i: (0, 1))],
        out_specs=pl.BlockSpec((2, H // 2, 128, 128), lambda i: (0, 0, 0, 0)),
        out_shape=jax.ShapeDtypeStruct((2, H // 2, 128, 128), F32),
    )(lg, pqk_c, pc)


T_M, T_MT = 0, 1
T_MF1, T_MB1 = 2, 3
T_QF, T_QB = 4, 5
T_KF, T_KB = 6, 7


def _decay_tables(lg, H):
    def body(lg_ref, t_ref):
        h = pl.program_id(0)
        lgf, lgb = lg_ref[0, h], lg_ref[1, h]
        i = lax.broadcasted_iota(jnp.int32, (CHUNK, CHUNK), 0).astype(F32)
        j = lax.broadcasted_iota(jnp.int32, (CHUNK, CHUNK), 1).astype(F32)
        d = i - j
        mf = jnp.where(d > 0, jnp.exp(lgf * jnp.maximum(d, 0.0)), 0.0)
        mb = jnp.where(d < 0, jnp.exp(lgb * jnp.maximum(-d, 0.0)), 0.0)
        mf_t = jnp.where(d < 0, jnp.exp(lgf * jnp.maximum(-d, 0.0)), 0.0)
        mb_t = jnp.where(d > 0, jnp.exp(lgb * jnp.maximum(d, 0.0)), 0.0)
        diag = jnp.where(d == 0, 2.0, 0.0)
        t_ref[0, T_M] = mf + mb + diag
        t_ref[0, T_MT] = mf_t + mb_t + diag
        t_ref[0, T_MF1] = mf * d
        t_ref[0, T_MB1] = mb * (-d)
        t_ref[0, T_QF] = jnp.exp(lgf * (i + 1.0))
        t_ref[0, T_QB] = jnp.exp(lgb * (CHUNK - i))
        t_ref[0, T_KF] = jnp.exp(lgf * (CHUNK - 1.0 - i))
        t_ref[0, T_KB] = jnp.exp(lgb * i)

    return pl.pallas_call(
        body, name="decay_tables", grid=(H,), in_specs=[_smem_spec()],
        out_specs=pl.BlockSpec((1, 8, CHUNK, CHUNK), lambda h: (h, 0, 0, 0)),
        out_shape=jax.ShapeDtypeStruct((H, 8, CHUNK, CHUNK), F32),
    )(lg)


def _tab_spec(H):
    return pl.BlockSpec((H, 8, CHUNK, CHUNK), lambda n: (0, 0, 0, 0))


def _chunk_decay(tab_ref, h):
    return tab_ref[h, T_QF, CHUNK - 1:CHUNK, :], tab_ref[h, T_QB, 0:1, :]


def _ret_states(kr, p, s0, tab, D):
    L = kr.shape[0]
    H = D // DV
    N = L // CHUNK
    HP = H // 2

    def body(tab_ref, kf_ref, kb_ref, vf_ref, vb_ref, s0_ref, sf_out, sb_out, sf, sb):
        n = pl.program_id(0)

        @pl.when(n == 0)
        def _():
            sf[...] = s0_ref[0]
            sb[...] = s0_ref[1]

        for cc in range(RET_CPB):
            cf_, cb_ = cc, RET_CPB - 1 - cc
            rf, rb = slice(cf_ * CHUNK, (cf_ + 1) * CHUNK), slice(cb_ * CHUNK, (cb_ + 1) * CHUNK)
            sf_out[cf_] = sf[...]
            sb_out[cb_] = sb[...]
            for pr in range(HP):
                kf2 = kf_ref[rf, pr * 128:(pr + 1) * 128].astype(F32)
                kb2 = kb_ref[rb, pr * 128:(pr + 1) * 128].astype(F32)
                inc_f, inc_b, gf, gb = [], [], [], []
                for e in range(2):
                    h = 2 * pr + e
                    inc_f.append(_dot_tn((kf2 * tab_ref[h, T_KF]).astype(BF16), vf_ref[rf, h * DV:(h + 1) * DV]))
                    inc_b.append(_dot_tn((kb2 * tab_ref[h, T_KB]).astype(BF16), vb_ref[rb, h * DV:(h + 1) * DV]))
                    cf, cb = _chunk_decay(tab_ref, h)
                    gf.append(jnp.broadcast_to(cf, (128, 128)))
                    gb.append(jnp.broadcast_to(cb, (128, 128)))
                sf[pr] = _pair_select(gf[0], gf[1]) * sf[pr] + _pair_select(inc_f[0], inc_f[1])
                sb[pr] = _pair_select(gb[0], gb[1]) * sb[pr] + _pair_select(inc_b[0], inc_b[1])

    st = jax.ShapeDtypeStruct((N, HP, 128, 128), F32)
    R = RET_CPB * CHUNK
    NB = N // RET_CPB
    return _riding_call(
        body, None, NB, name="ret_states", args=(tab, kr, kr, p, p, s0),
        in_specs=[_tab_spec(H),
                  pl.BlockSpec((R, D // 2), lambda n: (n, 0)),
                  pl.BlockSpec((R, D // 2), lambda n: (NB - 1 - n, 0)),
                  pl.BlockSpec((R, D), lambda n: (n, 5)),
                  pl.BlockSpec((R, D), lambda n: (NB - 1 - n, 5)),
                  pl.BlockSpec((2, HP, 128, 128), lambda n: (0, 0, 0, 0))],
        out_specs=[pl.BlockSpec((RET_CPB, HP, 128, 128), lambda n: (n, 0, 0, 0)),
                   pl.BlockSpec((RET_CPB, HP, 128, 128), lambda n: (NB - 1 - n, 0, 0, 0))],
        out_shape=[st, st],
        scratch=[pltpu.VMEM((HP, 128, 128), F32), pltpu.VMEM((HP, 128, 128), F32)],
        cparams=_cparams(("arbitrary",)))


def _ret_out(qr, kr, p, sf_prev, sb_prev, gn_w, tab, D):
    L = qr.shape[0]
    H = D // DV
    N = L // CHUNK
    HP = H // 2

    def body(tab_ref, q_ref, k_ref, v_ref, zb_ref, sf_ref, sb_ref, gn_ref, o_ref, yb_ref):
        def chunk(cc, carry):
            rows = pl.ds(pl.multiple_of(cc * CHUNK, CHUNK), CHUNK)
            for pr in range(HP):
                q2 = q_ref[rows, pr * 128:(pr + 1) * 128]
                k2 = k_ref[rows, pr * 128:(pr + 1) * 128]
                sfp = sf_ref[cc, pr].astype(BF16)
                sbp = sb_ref[cc, pr].astype(BF16)
                for e in range(2):
                    h = 2 * pr + e
                    sl = slice(h * DV, (h + 1) * DV)
                    qm = jnp.where(_head_lane_mask(q2.shape, e), q2, jnp.zeros_like(q2))
                    a = (_dot_nt(qm, k2) * tab_ref[h, T_M]).astype(BF16)
                    qf = qm.astype(F32)
                    o = _dot(a, v_ref[rows, sl])
                    o += _dot((qf * tab_ref[h, T_QF]).astype(BF16), sfp)
                    o += _dot((qf * tab_ref[h, T_QB]).astype(BF16), sbp)
                    o_ref[rows, sl] = o
                    mu = jnp.mean(o, axis=-1, keepdims=True)
                    oc = o - mu
                    rstd = lax.rsqrt(jnp.mean(oc * oc, axis=-1, keepdims=True) + EPS)
                    zb = zb_ref[rows, sl].astype(F32)
                    yb_ref[rows, sl] = (zb * _sigmoid(zb) * (oc * rstd * gn_ref[:, sl])).astype(BF16)
            return carry

        lax.fori_loop(0, RET_CPB, chunk, 0)

    R = RET_CPB * CHUNK
    return _riding_call(
        body, None, N // RET_CPB, name="ret_out", args=(tab, qr, kr, p, p, sf_prev, sb_prev, gn_w),
        in_specs=[_tab_spec(H),
                  pl.BlockSpec((R, D // 2), lambda n: (n, 0)),
                  pl.BlockSpec((R, D // 2), lambda n: (n, 0)),
                  pl.BlockSpec((R, D), lambda n: (n, 5)),
                  pl.BlockSpec((R, D), lambda n: (n, 6)),
                  pl.BlockSpec((RET_CPB, HP, 128, 128), lambda n: (n, 0, 0, 0)),
                  pl.BlockSpec((RET_CPB, HP, 128, 128), lambda n: (n, 0, 0, 0)),
                  _vec_spec(D)],
        out_specs=[pl.BlockSpec((R, D), lambda n: (n, 0)), pl.BlockSpec((R, D), lambda n: (n, 0))],
        out_shape=[jax.ShapeDtypeStruct((L, D), F32), jax.ShapeDtypeStruct((L, D), BF16)],
        cparams=_cparams(("arbitrary",)))


def _mid(p, yb, o, x, tgt, w3, g, fw, conv_w, conv_b, gn_w, D):
    L = x.shape[0]
    H = D // DV
    tm = min(256, L)
    nt = L // tm

    def body(h_ref, bg_ref, cg_ref, za_ref, hp_ref, hn_ref, cp_ref, cn_ref, yb_ref, ga_ref, gb_ref, zb_ref, o_ref,
             x_ref, t_ref, w_hbm, g_ref, fw_ref, cw_ref, cb_ref, gn_ref,
             dx1_ref, dya_ref, do_ref, dzb_ref, dgab_ref, dw_hbm, st_ref, w_vm, dw_acc, sem):
        i = pl.program_id(0)

        @pl.when(i == 0)
        def _():
            cp = pltpu.make_async_copy(w_hbm, w_vm, sem)
            cp.start()
            dw_acc[...] = jnp.zeros_like(dw_acc)
            st_ref[...] = jnp.zeros_like(st_ref)
            cp.wait()

        u = cg_ref[...].astype(F32) * h_ref[...].astype(F32)
        above = jnp.where(i == 0, 0.0, cp_ref[15:16, :].astype(F32) * hp_ref[15:16, :].astype(F32))
        below = jnp.where(i == nt - 1, 0.0, cn_ref[0:1, :].astype(F32) * hn_ref[0:1, :].astype(F32))
        dn, up = _shift_rows(u, above, below)
        co = cw_ref[0:1, :] * dn + cw_ref[1:2, :] * u + cw_ref[2:3, :] * up + cb_ref[...]
        za = za_ref[...].astype(F32)
        ya_b = (za * _sigmoid(za) * bg_ref[...].astype(F32) * co).astype(BF16)
        yb_b = yb_ref[...]
        y_a = _dot(ya_b, w_vm[0])
        y_b = _dot(yb_b, w_vm[1])
        sga = _sigmoid(ga_ref[...].astype(F32))
        sgb = _sigmoid(gb_ref[...].astype(F32))
        mix_b = (sga * y_a + sgb * y_b).astype(BF16)
        y_x = _dot(mix_b, w_vm[2])
        gvec, fwv = g_ref[...], fw_ref[...]
        x1 = x_ref[...] + gvec * y_x
        r1 = lax.rsqrt(jnp.mean(x1 * x1, axis=-1, keepdims=True) + EPS)
        xh = x1 * r1
        diff = xh * fwv - t_ref[...]
        dout = diff * (1.0 / D)
        dxh = dout * fwv
        dx1 = r1 * (dxh - xh * jnp.mean(dxh * xh, axis=-1, keepdims=True))
        dx1_ref[...] = dx1
        st_ref[0:1, :] += jnp.sum(dout * xh, axis=0, keepdims=True)
        st_ref[1:2, :] += jnp.sum(dx1 * y_x, axis=0, keepdims=True)
        st_ref[2:3, :] += jnp.sum(diff * diff, axis=0, keepdims=True)
        dyx_b = (dx1 * gvec).astype(BF16)
        dmix = _dot_nt(dyx_b, w_vm[2])
        dw_acc[2] += _dot_tn(mix_b, dyx_b)
        dya_b = (dmix * sga).astype(BF16)
        dyb_b = (dmix * sgb).astype(BF16)
        dgab_ref[:, 0:D] = (dmix * y_a * sga * (1.0 - sga)).astype(BF16)
        dgab_ref[:, D:2 * D] = (dmix * y_b * sgb * (1.0 - sgb)).astype(BF16)
        dya_ref[...] = _dot_nt(dya_b, w_vm[0])
        dyb = _dot_nt(dyb_b, w_vm[1])
        dw_acc[0] += _dot_tn(ya_b, dya_b)
        dw_acc[1] += _dot_tn(yb_b, dyb_b)

        for h in range(H):
            sl = slice(h * DV, (h + 1) * DV)
            ov = o_ref[:, sl]
            oc = ov - jnp.mean(ov, axis=-1, keepdims=True)
            rstd = lax.rsqrt(jnp.mean(oc * oc, axis=-1, keepdims=True) + EPS)
            rn = oc * rstd
            gw = gn_ref[:, sl]
            zb = zb_ref[:, sl].astype(F32)
            sz = _sigmoid(zb)
            dy = dyb[:, sl]
            dzb_ref[:, sl] = (dy * (rn * gw) * (sz * (1.0 + zb * (1.0 - sz)))).astype(BF16)
            dretn = dy * (zb * sz)
            st_ref[3:4, sl] += jnp.sum(dretn * rn, axis=0, keepdims=True)
            drn = dretn * gw
            do_ref[:, sl] = (rstd * (drn - jnp.mean(drn, axis=-1, keepdims=True)
                                     - rn * jnp.mean(drn * rn, axis=-1, keepdims=True))).astype(BF16)

        @pl.when(i == nt - 1)
        def _():
            out = pltpu.make_async_copy(dw_acc, dw_hbm, sem)
            out.start()
            out.wait()

    row = lambda col: pl.BlockSpec((tm, D), lambda i: (i, col))
    any_spec = pl.BlockSpec(memory_space=pl.ANY)
    f32o = jax.ShapeDtypeStruct((L, D), F32)
    bf16o = jax.ShapeDtypeStruct((L, D), BF16)
    hp, hn = _halo_specs(tm, L, D, 0)
    cp, cn = _halo_specs(tm, L, D, 2)
    return pl.pallas_call(
        body, name="mid", grid=(nt,),
        in_specs=[row(0), row(1), row(2), row(3), hp, hn, cp, cn, row(0), row(7), row(8), row(6), row(0),
                  row(0), row(0), any_spec, _vec_spec(D), _vec_spec(D),
                  pl.BlockSpec((8, D), lambda i: (0, 0)), _vec_spec(D), _vec_spec(D)],
        out_specs=[row(0), row(0), row(0), row(0), pl.BlockSpec((tm, 2 * D), lambda i: (i, 0)), any_spec,
                   pl.BlockSpec((8, D), lambda i: (0, 0))],
        out_shape=[f32o, f32o, bf16o, bf16o, jax.ShapeDtypeStruct((L, 2 * D), BF16),
                   jax.ShapeDtypeStruct((3, D, D), F32), jax.ShapeDtypeStruct((8, D), F32)],
        scratch_shapes=[pltpu.VMEM((3, D, D), BF16), pltpu.VMEM((3, D, D), F32), pltpu.SemaphoreType.DMA],
        compiler_params=_cparams(("arbitrary",), VMEM_LIMIT),
    )(p, p, p, p, p, p, p, p, yb, p, p, p, o, x, tgt, w3, g, fw, conv_w, conv_b, gn_w)


def _conv_bwd(dya, p, conv_w, conv_b, D, exchange=None):
    L = p.shape[0]
    tl = min(256, L)
    nt = L // tl

    def body(d_ref, h_ref, bg_ref, cg_ref, za_ref,
             dp_ref, dn_ref, hp_ref, hn_ref, bp_ref, bn_ref, cp_ref, cn_ref, zp_ref, zn_ref,
             w_ref, b_ref, dc_ref, st_ref):
        i = pl.program_id(0)

        @pl.when(i == 0)
        def _():
            st_ref[...] = jnp.zeros_like(st_ref)

        first, last = i == 0, i == nt - 1
        h = h_ref[...].astype(F32)
        cg = cg_ref[...].astype(F32)
        bg = bg_ref[...].astype(F32)
        za = za_ref[...].astype(F32)
        dy = d_ref[...].astype(F32)
        u = cg * h
        u_above = jnp.where(first, 0.0, cp_ref[15:16, :].astype(F32) * hp_ref[15:16, :].astype(F32))
        u_below = jnp.where(last, 0.0, cn_ref[0:1, :].astype(F32) * hn_ref[0:1, :].astype(F32))
        u_dn, u_up = _shift_rows(u, u_above, u_below)
        w0, w1, w2 = w_ref[0:1, :], w_ref[1:2, :], w_ref[2:3, :]
        co = w0 * u_dn + w1 * u + w2 * u_up + b_ref[...]
        sz = _sigmoid(za)
        silu = za * sz
        dc_ref[:, 3 * D:4 * D] = (dy * bg * co * (sz * (1.0 + za * (1.0 - sz)))).astype(BF16)
        dc_ref[:, D:2 * D] = (dy * silu * co).astype(BF16)
        dco = dy * silu * bg

        def edge(dr, zr, br, r):
            z = zr[r:r + 1, :].astype(F32)
            return dr[r:r + 1, :].astype(F32) * (z * _sigmoid(z)) * br[r:r + 1, :].astype(F32)

        dco_above = jnp.where(first, 0.0, edge(dp_ref, zp_ref, bp_ref, 15))
        dco_below = jnp.where(last, 0.0, edge(dn_ref, zn_ref, bn_ref, 0))
        dco_dn, dco_up = _shift_rows(dco, dco_above, dco_below)
        du = w0 * dco_up + w1 * dco + w2 * dco_dn
        dc_ref[:, 2 * D:3 * D] = (du * h).astype(BF16)
        dc_ref[:, 0:D] = (du * cg).astype(BF16)
        st_ref[0:1, :] += jnp.sum(dco * u_dn, axis=0, keepdims=True)
        st_ref[1:2, :] += jnp.sum(dco * u, axis=0, keepdims=True)
        st_ref[2:3, :] += jnp.sum(dco * u_up, axis=0, keepdims=True)
        st_ref[3:4, :] += jnp.sum(dco, axis=0, keepdims=True)

    main = lambda col: pl.BlockSpec((tl, D), lambda i: (i, col))
    halos = []
    for col in (0, 0, 1, 2, 3):
        halos.extend(_halo_specs(tl, L, D, col))
    return _riding_call(
        body, exchange, nt, name="conv_bwd",
        args=(dya, p, p, p, p, dya, dya, p, p, p, p, p, p, p, p, conv_w, conv_b),
        in_specs=[main(0), main(0), main(1), main(2), main(3)] + halos
                 + [pl.BlockSpec((8, D), lambda i: (0, 0)), _vec_spec(D)],
        out_specs=[pl.BlockSpec((tl, 4 * D), lambda i: (i, 0)), pl.BlockSpec((8, D), lambda i: (0, 0))],
        out_shape=[jax.ShapeDtypeStruct((L, 4 * D), BF16), jax.ShapeDtypeStruct((8, D), F32)],
        cparams=_cparams(("arbitrary",)))


def _ret_bwd_states(qr, do, tab, D):
    L = qr.shape[0]
    H = D // DV
    N = L // CHUNK
    HP = H // 2

    def body(tab_ref, qf_ref, qb_ref, dof_ref, dob_ref, dsf_out, dsb_out, ds0_out, dsf, dsb):
        n = pl.program_id(0)

        @pl.when(n == 0)
        def _():
            dsf[...] = jnp.zeros_like(dsf)
            dsb[...] = jnp.zeros_like(dsb)

        for cc in range(RET_CPB):
            cf_, cb_ = RET_CPB - 1 - cc, cc
            rf, rb = slice(cf_ * CHUNK, (cf_ + 1) * CHUNK), slice(cb_ * CHUNK, (cb_ + 1) * CHUNK)
            dsf_out[cf_] = dsf[...]
            dsb_out[cb_] = dsb[...]
            for pr in range(HP):
                qf2 = qf_ref[rf, pr * 128:(pr + 1) * 128].astype(F32)
                qb2 = qb_ref[rb, pr * 128:(pr + 1) * 128].astype(F32)
                inc_f, inc_b, gf, gb = [], [], [], []
                for e in range(2):
                    h = 2 * pr + e
                    inc_f.append(_dot_tn((qf2 * tab_ref[h, T_QF]).astype(BF16), dof_ref[rf, h * DV:(h + 1) * DV]))
                    inc_b.append(_dot_tn((qb2 * tab_ref[h, T_QB]).astype(BF16), dob_ref[rb, h * DV:(h + 1) * DV]))
                    cf, cb = _chunk_decay(tab_ref, h)
                    gf.append(jnp.broadcast_to(cf, (128, 128)))
                    gb.append(jnp.broadcast_to(cb, (128, 128)))
                dsf[pr] = _pair_select(gf[0], gf[1]) * dsf[pr] + _pair_select(inc_f[0], inc_f[1])
                dsb[pr] = _pair_select(gb[0], gb[1]) * dsb[pr] + _pair_select(inc_b[0], inc_b[1])

        @pl.when(n == NB - 1)
        def _():
            ds0_out[0] = dsf[...]
            ds0_out[1] = dsb[...]

    st = jax.ShapeDtypeStruct((N, HP, 128, 128), F32)
    R = RET_CPB * CHUNK
    NB = N // RET_CPB
    return pl.pallas_call(
        body, name="ret_bwd_states", grid=(NB,),
        in_specs=[_tab_spec(H),
                  pl.BlockSpec((R, D // 2), lambda n: (NB - 1 - n, 0)),
                  pl.BlockSpec((R, D // 2), lambda n: (n, 0)),
                  pl.BlockSpec((R, D), lambda n: (NB - 1 - n, 0)),
                  pl.BlockSpec((R, D), lambda n: (n, 0))],
        out_specs=[pl.BlockSpec((RET_CPB, HP, 128, 128), lambda n: (NB - 1 - n, 0, 0, 0)),
                   pl.BlockSpec((RET_CPB, HP, 128, 128), lambda n: (n, 0, 0, 0)),
                   pl.BlockSpec((2, HP, 128, 128), lambda n: (0, 0, 0, 0))],
        out_shape=[st, st, jax.ShapeDtypeStruct((2, HP, 128, 128), F32)],
        scratch_shapes=[pltpu.VMEM((HP, 128, 128), F32), pltpu.VMEM((HP, 128, 128), F32)],
        compiler_params=_cparams(("arbitrary",)),
    )(tab, qr, qr, do, do)


def _ret_bwd_main(qr, kr, p, do, sf_prev, sb_prev, dsf, dsb, c2, s2, tab, D, exchange=None):
    L = qr.shape[0]
    H = D // DV
    N = L // CHUNK
    HP = H // 2
    W = D // 2

    def body(tab_ref, q_ref, k_ref, v_ref, do_ref, sf_ref, sb_ref, dsf_ref, dsb_ref, c_ref, s_ref,
             dr_ref, st_ref, dl_acc):
        @pl.when(pl.program_id(0) == 0)
        def _():
            dl_acc[...] = jnp.zeros_like(dl_acc)

        i = lax.broadcasted_iota(jnp.int32, (CHUNK, 128), 0).astype(F32)
        rowid = lax.broadcasted_iota(jnp.int32, (128, 128), 0)

        def chunk(cc, carry):
            rows = pl.ds(pl.multiple_of(cc * CHUNK, CHUNK), CHUNK)
            c, s = c_ref[rows, :], s_ref[rows, :]
            for pr in range(HP):
                ps = slice(pr * 128, (pr + 1) * 128)
                q2, k2 = q_ref[rows, ps], k_ref[rows, ps]
                sf32, sb32 = sf_ref[cc, pr], sb_ref[cc, pr]
                dsf32, dsb32 = dsf_ref[cc, pr], dsb_ref[cc, pr]
                sfp, sbp = sf32.astype(BF16), sb32.astype(BF16)
                dsfp, dsbp = dsf32.astype(BF16), dsb32.astype(BF16)
                dq2 = jnp.zeros((CHUNK, 128), F32)
                dk2 = jnp.zeros((CHUNK, 128), F32)
                for e in range(2):
                    h = 2 * pr + e
                    sl = slice(h * DV, (h + 1) * DV)
                    hm = _head_lane_mask(q2.shape, e)
                    qm = jnp.where(hm, q2, jnp.zeros_like(q2))
                    km = jnp.where(hm, k2, jnp.zeros_like(k2))
                    qf, kf = qm.astype(F32), km.astype(F32)
                    v, do = v_ref[rows, sl], do_ref[rows, sl]
                    vf, dof = v.astype(F32), do.astype(F32)
                    m_t = tab_ref[h, T_MT]
                    sc = _dot_nt(qm, k2)
                    dpm = _dot_nt(do, v)
                    dsc = (dpm * tab_ref[h, T_M]).astype(BF16)
                    a_t = (_dot_nt(km, q2) * m_t).astype(BF16)
                    dsc_t = (_dot_nt(v, do) * m_t).astype(BF16)
                    dq_f, dq_b = tab_ref[h, T_QF], tab_ref[h, T_QB]
                    dk_f, dk_b = tab_ref[h, T_KF], tab_ref[h, T_KB]
                    dq = _dot(dsc, km)
                    dq += jnp.where(hm, dq_f * _dot_nt(do, sfp) + dq_b * _dot_nt(do, sbp), 0.0)
                    dk = _dot(dsc_t, qm)
                    dk += jnp.where(hm, dk_f * _dot_nt(v, dsfp) + dk_b * _dot_nt(v, dsbp), 0.0)
                    kdf = _dot((kf * dk_f).astype(BF16), dsfp)
                    kdb = _dot((kf * dk_b).astype(BF16), dsbp)
                    dr_ref[rows, D + h * DV:D + (h + 1) * DV] = (_dot(a_t, do) + kdf + kdb).astype(BF16)
                    dq2 += dq
                    dk2 += dk
                    xf = _dot((qf * dq_f).astype(BF16), sfp)
                    xb = _dot((qf * dq_b).astype(BF16), sbp)
                    pair = (rowid < DK) if e == 0 else (rowid >= DK)
                    gcf, gcb = tab_ref[h, T_QF, CHUNK - 1:CHUNK, 0:1], tab_ref[h, T_QB, 0:1, 0:1]
                    scdp = sc * dpm
                    dl_acc[h, 0] += scdp * tab_ref[h, T_MF1] + xf * dof * (i + 1.0) \
                        + kdf * vf * (CHUNK - 1.0 - i) + (CHUNK * gcf) * jnp.where(pair, dsf32 * sf32, 0.0)
                    dl_acc[h, 1] += scdp * tab_ref[h, T_MB1] + xb * dof * (CHUNK - i) \
                        + kdb * vf * i + (CHUNK * gcb) * jnp.where(pair, dsb32 * sb32, 0.0)
                dr_ref[rows, ps] = (dq2 * c - _swap_halves(dq2) * s).astype(BF16)
                dr_ref[rows, W + pr * 128:W + (pr + 1) * 128] = \
                    ((dk2 * c - _swap_halves(dk2) * s) * K_SCALE).astype(BF16)
            return carry

        lax.fori_loop(0, RET_CPB, chunk, 0)

        @pl.when(pl.program_id(0) == N // RET_CPB - 1)
        def _():
            lane = lax.broadcasted_iota(jnp.int32, (1, 128), 1)
            acc = [jnp.zeros((1, 128), F32), jnp.zeros((1, 128), F32)]
            for h in range(H):
                for b in range(2):
                    acc[b] += jnp.where(lane == h, _sum_all(dl_acc[h, b]), 0.0)
            st_ref[...] = jnp.zeros_like(st_ref)
            st_ref[0:1, :] = acc[0]
            st_ref[1:2, :] = acc[1]

    R = RET_CPB * CHUNK
    st_spec = pl.BlockSpec((RET_CPB, HP, 128, 128), lambda n: (n, 0, 0, 0))
    half = pl.BlockSpec((R, W), lambda n: (n, 0))
    rope = pl.BlockSpec((R, 128), lambda n: (n, 0))
    return _riding_call(
        body, exchange, N // RET_CPB, name="ret_bwd_main",
        args=(tab, qr, kr, p, do, sf_prev, sb_prev, dsf, dsb, c2, s2),
        in_specs=[_tab_spec(H), half, half,
                  pl.BlockSpec((R, D), lambda n: (n, 5)),
                  pl.BlockSpec((R, D), lambda n: (n, 0)),
                  st_spec, st_spec, st_spec, st_spec, rope, rope],
        out_specs=[pl.BlockSpec((R, 2 * D), lambda n: (n, 0)),
                   pl.BlockSpec((8, 128), lambda n: (0, 0))],
        out_shape=[jax.ShapeDtypeStruct((L, 2 * D), BF16), jax.ShapeDtypeStruct((8, 128), F32)],
        scratch=[pltpu.VMEM((H, 2, CHUNK, 128), F32)],
        cparams=_cparams(("arbitrary",)))


def _ctx_bwd(pc, pqk_c, ds0, lg, D):
    Lc = pc.shape[0]
    H = D // DV
    HP = H // 2
    W = D // 2

    def body(lg_ref, k_ref, v_ref, ds_ref, dr_ref, st_ref):
        dqk_ref = dr_ref.at[:, 0:D]
        dv_ref = dr_ref.at[:, D:2 * D]
        m = lax.broadcasted_iota(jnp.int32, (Lc, 128), 0).astype(F32)
        lane = lax.broadcasted_iota(jnp.int32, (1, 128), 1)
        acc_f = jnp.zeros((1, 128), F32)
        acc_b = jnp.zeros((1, 128), F32)
        dqk_ref[:, 0:W] = jnp.zeros((Lc, W), BF16)
        for pr in range(HP):
            ps = slice(pr * 128, (pr + 1) * 128)
            k2 = k_ref[:, ps].astype(F32) * K_SCALE
            dsfp, dsbp = ds_ref[0, pr].astype(BF16), ds_ref[1, pr].astype(BF16)
            dk2 = jnp.zeros((Lc, 128), F32)
            for e in range(2):
                h = 2 * pr + e
                sl = slice(h * DV, (h + 1) * DV)
                hm = _head_lane_mask(k2.shape, e)
                km = jnp.where(hm, k2, 0.0)
                v = v_ref[:, sl]
                vf = v.astype(F32)
                dec_f = jnp.exp(lg_ref[0, h] * (Lc - 1.0 - m))
                dec_b = jnp.exp(lg_ref[1, h] * m)
                kdf = _dot((km * dec_f).astype(BF16), dsfp)
                kdb = _dot((km * dec_b).astype(BF16), dsbp)
                dv_ref[:, sl] = (kdf + kdb).astype(BF16)
                dk2 += jnp.where(hm, dec_f * _dot_nt(v, dsfp) + dec_b * _dot_nt(v, dsbp), 0.0)
                acc_f += jnp.where(lane == h, _sum_all(kdf * vf * (Lc - 1.0 - m)), 0.0)
                acc_b += jnp.where(lane == h, _sum_all(kdb * vf * m), 0.0)
            dqk_ref[:, W + pr * 128:W + (pr + 1) * 128] = (dk2 * K_SCALE).astype(BF16)
        st_ref[...] = jnp.zeros_like(st_ref)
        st_ref[0:1, :] = acc_f
        st_ref[1:2, :] = acc_b

    return pl.pallas_call(
        body, name="ctx_bwd", grid=(1,),
        in_specs=[_smem_spec(), pl.BlockSpec((Lc, W), lambda i: (0, 1)), pl.BlockSpec((Lc, D), lambda i: (0, 1)),
                  pl.BlockSpec((2, HP, 128, 128), lambda i: (0, 0, 0, 0))],
        out_specs=[pl.BlockSpec((Lc, 2 * D), lambda i: (0, 0)), pl.BlockSpec((8, 128), lambda i: (0, 0))],
        out_shape=[jax.ShapeDtypeStruct((Lc, 2 * D), BF16), jax.ShapeDtypeStruct((8, 128), F32)],
    )(lg, pqk_c, pc, ds0)


class _Exchange(NamedTuple):
    inputs: tuple
    out_shapes: tuple
    n_copies: int
    build: Callable


def _exchange_parts(exchange):
    if exchange is None:
        return [], [], [], [], []
    n = exchange.n_copies
    return (list(exchange.inputs), [ANY] * len(exchange.inputs), list(exchange.out_shapes),
            [ANY] * len(exchange.out_shapes), [pltpu.SemaphoreType.DMA((n,)), pltpu.SemaphoreType.DMA((n,))])


def _riding_call(body, exchange, n_steps, *, args, in_specs, out_specs, out_shape, name, cparams, scratch=()):
    ex_args, ex_in_specs, ex_shapes, ex_out_specs, ex_scratch = _exchange_parts(exchange)
    n_in, n_out, n_sc = len(args), len(out_shape), len(scratch)

    def riding(*refs):
        k = n_in + len(ex_args)
        ins, ex_in = refs[:n_in], refs[n_in:k]
        outs, ex_out = refs[k:k + n_out], refs[k + n_out:k + n_out + len(ex_shapes)]
        k += n_out + len(ex_shapes)
        own_scratch, ex_sems = refs[k:k + n_sc], refs[k + n_sc:]
        step = pl.program_id(0)
        if exchange is not None:
            @pl.when(step == 0)
            def _():
                for rc in exchange.build(ex_in, ex_out, *ex_sems):
                    rc.start()
        body(*ins, *outs, *own_scratch)
        if exchange is not None:
            @pl.when(step == n_steps - 1)
            def _():
                for rc in exchange.build(ex_in, ex_out, *ex_sems):
                    rc.wait()

    return tuple(pl.pallas_call(
        riding, name=name, grid=(n_steps,),
        in_specs=list(in_specs) + ex_in_specs, out_specs=list(out_specs) + ex_out_specs,
        out_shape=list(out_shape) + ex_shapes, scratch_shapes=list(scratch) + ex_scratch,
        compiler_params=cparams,
    )(*args, *ex_args))


def _dxm(groups, col0, w, x, nw, sc, dx1, name, exchange=None):
    L, D = x.shape
    tm = min(256, L)
    nt = L // tm
    ng = len(groups)
    widths = [g.shape[1] for g in groups]
    wtot = sum(widths)
    with_dx = dx1 is not None
    ex_args, ex_in_specs, ex_shapes, ex_out_specs, ex_scratch = _exchange_parts(exchange)
    n_in = ng + 4 + (1 if with_dx else 0)
    n_out = 2 if with_dx else 1

    def body(*refs):
        group_refs = refs[:ng]
        w_hbm, x_ref, nw_ref, sc_ref = refs[ng:ng + 4]
        ex_in = refs[n_in:n_in + len(ex_args)]
        outs = refs[n_in + len(ex_args):]
        if with_dx:
            dx1_ref, gx_ref, st_ref = refs[ng + 4], outs[0], outs[1]
        else:
            st_ref = outs[0]
        ex_out = outs[n_out:n_out + len(ex_shapes)]
        w_vm, sem = outs[n_out + len(ex_shapes):n_out + len(ex_shapes) + 2]
        ex_sems = outs[n_out + len(ex_shapes) + 2:]
        i = pl.program_id(0)

        @pl.when(i == 0)
        def _():
            cp = pltpu.make_async_copy(w_hbm.at[:, col0 * D:col0 * D + wtot], w_vm, sem)
            cp.start()
            if exchange is not None:
                for rc in exchange.build(ex_in, ex_out, *ex_sems):
                    rc.start()
            st_ref[...] = jnp.zeros_like(st_ref)
            cp.wait()

        dxm, off = None, 0
        for g_ref, wd in zip(group_refs, widths):
            part = _dot_nt(g_ref[...], w_vm[:, off:off + wd])
            dxm = part if dxm is None else dxm + part
            off += wd

        xv = x_ref[...]
        r = lax.rsqrt(jnp.mean(xv * xv, axis=-1, keepdims=True) + EPS)
        xh = xv * r
        nwv = nw_ref[...]
        dxn = dxm * (1.0 + sc_ref[...])
        st_ref[0:1, :] += jnp.sum(dxm, axis=0, keepdims=True)
        st_ref[1:2, :] += jnp.sum(dxm * (xh * nwv), axis=0, keepdims=True)
        st_ref[2:3, :] += jnp.sum(dxn * xh, axis=0, keepdims=True)
        if with_dx:
            dxh = dxn * nwv
            gx_ref[...] = dx1_ref[...] + r * (dxh - xh * jnp.mean(dxh * xh, axis=-1, keepdims=True))

        if exchange is not None:
            @pl.when(i == nt - 1)
            def _():
                for rc in exchange.build(ex_in, ex_out, *ex_sems):
                    rc.wait()

    row = pl.BlockSpec((tm, D), lambda i: (i, 0))
    in_specs = [pl.BlockSpec((tm, wd), lambda i: (i, 0)) for wd in widths] + [ANY, row, _vec_spec(D), _vec_spec(D)]
    out_specs = [pl.BlockSpec((8, D), lambda i: (0, 0))]
    out_shape = [jax.ShapeDtypeStruct((8, D), F32)]
    args = list(groups) + [w, x, nw, sc]
    if with_dx:
        in_specs.append(row)
        out_specs.insert(0, row)
        out_shape.insert(0, jax.ShapeDtypeStruct((L, D), F32))
        args.append(dx1)
    res = pl.pallas_call(
        body, name=name, grid=(nt,),
        in_specs=in_specs + ex_in_specs, out_specs=out_specs + ex_out_specs, out_shape=out_shape + ex_shapes,
        scratch_shapes=[pltpu.VMEM((D, wtot), BF16), pltpu.SemaphoreType.DMA] + ex_scratch,
        compiler_params=_cparams(("arbitrary",), VMEM_LIMIT),
    )(*args, *ex_args)
    gx = res[0] if with_dx else None
    return (gx, res[n_out - 1], *res[n_out:])


DW_TN = 512
DW_RING = 4


def _dw_in(xmt, groups, cmt, dr_c, D, pair):
    L = xmt.shape[1]
    Lc = cmt.shape[1]
    Dh = D // 2
    tn = min(DW_TN, D)
    nblk = [g.shape[1] // tn for g in groups]
    starts = [sum(nblk[:g]) for g in range(len(groups))]
    ng = len(groups)
    nj = sum(nblk)
    rows_out = Dh if pair else D

    def body(*refs):
        xt_hbm = refs[0]
        group_refs = refs[1:1 + ng]
        ct_hbm, drc_ref, o_ref = refs[1 + ng:4 + ng]
        rest = refs[4 + ng:]
        if pair:
            ra_hbm, xt_vm, ct_vm, loc, ring, s_send, s_recv = rest
            pos = _position()
            sib = _peer(pos, 1)
        else:
            xt_vm, ct_vm, loc = rest
        j = pl.program_id(0)

        @pl.when(j == 0)
        def _():
            if pair:
                c = pos[2]
                other = pl.ds(pl.multiple_of((1 - c) * Dh, Dh), Dh)
                mine = pl.ds(pl.multiple_of(c * Dh, Dh), Dh)
                cps = [pltpu.make_async_copy(xt_hbm.at[other, :], xt_vm.at[0:Dh, :], loc.at[0]),
                       pltpu.make_async_copy(xt_hbm.at[mine, :], xt_vm.at[Dh:D, :], loc.at[1]),
                       pltpu.make_async_copy(ct_hbm.at[other, :], ct_vm.at[0:Dh, :], loc.at[2]),
                       pltpu.make_async_copy(ct_hbm.at[mine, :], ct_vm.at[Dh:D, :], loc.at[3])]
            else:
                cps = [pltpu.make_async_copy(xt_hbm, xt_vm, loc.at[0]), pltpu.make_async_copy(ct_hbm, ct_vm, loc.at[1])]
            for cp in cps:
                cp.start()
            for cp in cps:
                cp.wait()

        def send(slot):
            cols = pl.ds(pl.multiple_of(j * tn, 128), tn)
            return pltpu.make_async_remote_copy(src_ref=ring.at[slot], dst_ref=ra_hbm.at[:, cols],
                                                send_sem=s_send.at[slot], recv_sem=s_recv,
                                                device_id=sib, device_id_type=MESH)

        for g in range(ng):
            @pl.when((j >= starts[g]) & (j < starts[g] + nblk[g]))
            def _(g=g):
                acc = _dot(xt_vm[...], group_refs[g][...])
                if g == 1:
                    acc += _dot(ct_vm[...], drc_ref[...])
                if not pair:
                    o_ref[...] = acc
                    return
                o_ref[...] = acc[Dh:, :]
                slot = lax.rem(j, DW_RING)

                @pl.when(j >= DW_RING)
                def _():
                    send(slot).wait_send()

                ring[slot] = acc[0:Dh, :]
                send(slot).start()

        if pair:
            @pl.when(j == nj - 1)
            def _():
                pltpu.make_async_remote_copy(src_ref=ra_hbm, dst_ref=ra_hbm, send_sem=s_send.at[0], recv_sem=s_recv,
                                             device_id=sib, device_id_type=MESH).wait_recv()
                for slot in range(DW_RING):
                    send(slot).wait_send()

    def group_spec(g, rows):
        return pl.BlockSpec((rows, tn), lambda j: (0, jnp.clip(j - starts[g], 0, nblk[g] - 1)))

    out_specs = [pl.BlockSpec((rows_out, tn), lambda j: (0, j))]
    out_shape = [jax.ShapeDtypeStruct((rows_out, nj * tn), F32)]
    scratch = [pltpu.VMEM((D, L), BF16), pltpu.VMEM((D, Lc), BF16), pltpu.SemaphoreType.DMA((4,))]
    if pair:
        out_specs.append(ANY)
        out_shape.append(jax.ShapeDtypeStruct((Dh, nj * tn), F32))
        scratch += [pltpu.VMEM((DW_RING, Dh, tn), F32), pltpu.SemaphoreType.DMA((DW_RING,)), pltpu.SemaphoreType.DMA]
    return tuple(pl.pallas_call(
        body, name="dw_in", grid=(nj,),
        in_specs=[ANY] + [group_spec(g, L) for g in range(ng)] + [ANY, group_spec(1, Lc)],
        out_specs=out_specs, out_shape=out_shape, scratch_shapes=scratch,
        compiler_params=_cparams(("arbitrary",), VMEM_LIMIT),
    )(xmt, *groups, cmt, dr_c))


def _local_step(x, ctx, tgt, mod_x, mod_c, norm_w, conv_w8, conv_b, lg, gn_w, fw, project, csidx=None, w_in_s=None):
    L, D = x.shape
    sh_x, sc_x, g_x = mod_x
    sh_c, sc_c = mod_c
    c2, s2 = _rope_tables(L)
    tab = _decay_tables(lg, D // DV)

    xm, xmt, *w_bf = _norm_mod(x, norm_w, sc_x, sh_x, "norm_mod_x", w_in_s)
    cm, cmt = _norm_mod(ctx, norm_w, sc_c, sh_c, "norm_mod_ctx")
    reduce = csidx is not None
    p, qr, kr, w_in, w3 = project(xm, c2, s2, *w_bf)
    pc, pqk_c = _in_proj(cm, w_in, "in_proj_ctx", QK_BLOCK, 2)
    s0 = _ctx_states(pc, pqk_c, lg, D)
    sf_prev, sb_prev = _ret_states(kr, p, s0, tab, D)
    o, yb = _ret_out(qr, kr, p, sf_prev, sb_prev, gn_w, tab, D)
    dx1, dya, do, dzb, dgab, dw3, st_mid = _mid(p, yb, o, x, tgt, w3, g_x, fw, conv_w8, conv_b, gn_w, D)
    dw3_5 = dw3.reshape(3, N_SHARD, 2, D // 8, D)
    dconv, st_conv, *ra_3 = _conv_bwd(dya, p, conv_w8, conv_b, D, _pair_exchange_w3(dw3_5) if reduce else None)
    dsf, dsb, ds0 = _ret_bwd_states(qr, do, tab, D)
    cs_3 = _sum_pair_w3(csidx[0:1], dw3_5, ra_3[0]) if reduce else None
    dret, st_lg, *rb_3 = _ret_bwd_main(qr, kr, p, do, sf_prev, sb_prev, dsf, dsb, c2, s2, tab, D,
                                       _chips_exchange_w3(cs_3) if reduce else None)
    g_3 = _sum_chips_w3(csidx, cs_3, rb_3[0]) if reduce else dw3
    dret_c, st_lgc = _ctx_bwd(pc, pqk_c, ds0, lg, D)
    groups = (dconv, dret, dzb, dgab)
    _, st_c = _dxm((dret_c,), 4, w_in, ctx, norm_w, sc_c, None, "dxm_ctx")
    return groups, dret_c, xmt, cmt, dx1, sc_x, w_in, g_3, (st_mid, st_conv, st_lg, st_lgc, st_c)


CHIP_FLIPS = (4, 2, 6)
ANY = pl.BlockSpec(memory_space=pl.ANY)
VMEM_FULL = pl.BlockSpec(memory_space=pltpu.VMEM)


def _position():
    return lax.axis_index("x"), lax.axis_index("y"), lax.axis_index("c")


def _peer(pos, k):
    x, y, c = pos
    return (1 - x if k & 4 else x, 1 - y if k & 2 else y, 1 - c if k & 1 else c)


def _dev_id(pos):
    return 4 * pos[0] + 2 * pos[1] + pos[2]


def _shard_of(pos):
    return 2 * pos[0] + pos[1]


def _remote(src, dst, send_sems, recv_sems, idx, to):
    return pltpu.make_async_remote_copy(src_ref=src, dst_ref=dst, send_sem=send_sems.at[idx],
                                        recv_sem=recv_sems.at[idx], device_id=to, device_id_type=MESH)


def _dot_f32(a, b):
    return jnp.dot(a, b, precision=lax.Precision.HIGHEST, preferred_element_type=F32)


def _silu(x):
    return x * _sigmoid(x)


def _fwd_small(c, c_ctx, ada_w, ada_b, conv_w):
    D = c.shape[1]
    Wm = ada_w.shape[1]
    Dq = conv_w.shape[1]

    def body(c_ref, cc_ref, aw_ref, ab_ref, cw_ref, act_ref, shx_ref, scx_ref, gx_ref, shc_ref, scc_ref, cwf_ref,
             mod_ref, cbuf, pmine, pbuf, wbuf, s_c, r_c, s_p, r_p, s_w, r_w):
        pos = _position()
        me, s = _dev_id(pos), _shard_of(pos)
        cbuf[me] = c_ref[...]
        wbuf[s] = cw_ref[...]
        sends = [_remote(c_ref, cbuf.at[me], s_c, r_c, k - 1, _peer(pos, k)) for k in range(1, 8)]
        sends += [_remote(cw_ref, wbuf.at[s], s_w, r_w, j, _peer(pos, k)) for j, k in enumerate(CHIP_FLIPS)]
        for cp in sends:
            cp.start()
        for k in range(1, 8):
            _remote(c_ref, cbuf.at[_dev_id(_peer(pos, k))], s_c, r_c, k - 1, _peer(pos, k)).wait_recv()
        for d in range(N_DEV):
            act_ref[d:d + 1, :] = _silu(cbuf[d])
        act_ref[8:9, :] = _silu(cc_ref[...])
        act_ref[9:16, :] = jnp.zeros((7, D), F32)
        part = _dot_f32(act_ref[...], aw_ref[...])
        pmine[...] = part
        pbuf[s] = part
        psend = [_remote(pmine, pbuf.at[s], s_p, r_p, j, _peer(pos, k)) for j, k in enumerate(CHIP_FLIPS)]
        for cp in psend:
            cp.start()
        for j, k in enumerate(CHIP_FLIPS):
            t = _shard_of(_peer(pos, k))
            _remote(pmine, pbuf.at[t], s_p, r_p, j, _peer(pos, k)).wait_recv()
            _remote(cw_ref, wbuf.at[t], s_w, r_w, j, _peer(pos, k)).wait_recv()
        cwf_ref[...] = jnp.zeros_like(cwf_ref)
        for t in range(N_SHARD):
            mod_ref[:, t * Wm:(t + 1) * Wm] = pbuf[t] + ab_ref[:, t * Wm:(t + 1) * Wm]
            cwf_ref[0:3, t * Dq:(t + 1) * Dq] = wbuf[t]
        for r, o_ref in enumerate((shx_ref, scx_ref, gx_ref)):
            o_ref[...] = mod_ref[pl.ds(me, 1), r * D:(r + 1) * D]
        for r, o_ref in enumerate((shc_ref, scc_ref)):
            o_ref[...] = mod_ref[8:9, r * D:(r + 1) * D]
        for cp in sends + psend:
            cp.wait_send()

    row = jax.ShapeDtypeStruct((1, D), F32)
    res = pl.pallas_call(
        body, name="fwd_small",
        in_specs=[VMEM_FULL] * 5, out_specs=[VMEM_FULL] * 7,
        out_shape=[jax.ShapeDtypeStruct((16, D), F32)] + [row] * 5 + [jax.ShapeDtypeStruct((8, D), F32)],
        scratch_shapes=[pltpu.VMEM((16, 3 * D), F32), pltpu.VMEM((N_DEV, 1, D), F32), pltpu.VMEM((16, Wm), F32),
                        pltpu.VMEM((N_SHARD, 16, Wm), F32), pltpu.VMEM((N_SHARD, 3, Dq), F32),
                        pltpu.SemaphoreType.DMA((7,)), pltpu.SemaphoreType.DMA((7,)),
                        pltpu.SemaphoreType.DMA((3,)), pltpu.SemaphoreType.DMA((3,)),
                        pltpu.SemaphoreType.DMA((3,)), pltpu.SemaphoreType.DMA((3,))],
        compiler_params=_cparams(None, VMEM_LIMIT),
    )(c, c_ctx, ada_w, ada_b, conv_w)
    return res[0], tuple(res[1:4]), tuple(res[4:6]), res[6]


AG_CHUNKS = 3


def _ag_in_proj(xm, w_in_s, w3_s, c2, s2):
    L, D = xm.shape
    Wc = w_in_s.shape[1]
    Wq = Wc // AG_CHUNKS
    Dh = D // 2
    Do = w3_s[0].shape[1]
    TM = min(1024, L // 4)
    NT = L // TM
    NQ = AG_CHUNKS
    order = [(q, j) for q in range(NQ) for j in (0, 1)] + [(q, 2) for q in range(NQ)]

    def body(xm_ref, wi_hbm, wa_ref, wb_ref, wo_ref, c_ref, s_ref, p_hbm, qr_hbm, kr_hbm, fi_hbm, f3_hbm,
             w_vm, s3, stage, qk_stage, ici_s, ici_r, d2d_s, d2d_r, w3_s_, w3_r_, fw_s, fw_r,
             loc, out_sem, qk_sem):
        pos = _position()
        c = pos[2]
        s = _shard_of(pos)
        sib = _peer(pos, 1)
        mine = pl.ds(pl.multiple_of(c * Dh, Dh), Dh)
        other = pl.ds(pl.multiple_of((1 - c) * Dh, Dh), Dh)

        def abs_col(t, q):
            return pl.ds(pl.multiple_of(t * Wc + q * Wq, 128), Wq)

        own = [pltpu.make_async_copy(wi_hbm.at[:, q * Wq:(q + 1) * Wq], w_vm.at[0, q], loc.at[2 + 4 * NQ + q])
               for q in range(NQ)]
        for cp in own:
            cp.start()
        for cp in own:
            cp.wait()
        sends = [_remote(w_vm.at[0, q, mine, :], w_vm.at[1 + j, q, mine, :], ici_s, ici_r, q * 3 + j,
                         _peer(pos, CHIP_FLIPS[j])) for q, j in order if j < 2]
        for cp in sends:
            cp.start()
        for a, w_ref in enumerate((wa_ref, wb_ref, wo_ref)):
            s3[a] = w_ref[...].astype(BF16)
        w3_sends = [_remote(s3.at[:, c], f3_hbm.at[:, s, c], w3_s_, w3_r_, j, _peer(pos, k))
                    for j, k in enumerate(CHIP_FLIPS)]
        local = [pltpu.make_async_copy(s3, f3_hbm.at[:, s], loc.at[1])]
        local += [pltpu.make_async_copy(w_vm.at[0, q], fi_hbm.at[:, abs_col(s, q)], loc.at[2 + q]) for q in range(NQ)]
        for cp in local:
            cp.start()

        def out_copy(slot, rows, cols):
            return pltpu.make_async_copy(stage.at[slot], p_hbm.at[rows, cols], out_sem.at[slot])

        def block(r, q, t, first):
            cols = abs_col(t, q)

            def row_tile(rt, carry):
                rows = pl.ds(pl.multiple_of(rt * TM, TM), TM)
                acc = _dot(xm_ref[rows, :], w_vm[r, q])
                slot = lax.rem(rt, 2)

                @pl.when(rt >= 2 if first else rt >= 0)
                def _():
                    out_copy(slot, rows, cols).wait()

                stage[slot] = acc.astype(BF16)
                out_copy(slot, rows, cols).start()

                def rotary(lo, scale, dst_hbm):
                    c, s = c_ref[rows, :], s_ref[rows, :]
                    for pr in range(Dh // 128):
                        tq = acc[:, lo + pr * 128:lo + (pr + 1) * 128] * scale
                        qk_stage[:, pr * 128:(pr + 1) * 128] = (tq * c + _swap_halves(tq) * s).astype(BF16)
                    cp = pltpu.make_async_copy(qk_stage, dst_hbm.at[rows, :], qk_sem)
                    cp.start()
                    cp.wait()

                if q == NQ - 1:
                    @pl.when(t == 1)
                    def _():
                        rotary(Wq - Dh, 1.0, qr_hbm)
                if q == 0:
                    @pl.when(t == 2)
                    def _():
                        rotary(0, K_SCALE, kr_hbm)
                return carry

            lax.fori_loop(0, NT, row_tile, 0)

        passed = []

        def hand_on(q, j):
            half = w_vm.at[1 + j, q, mine, :]
            if j == 2:
                _remote(half, half, fw_s, fw_r, q, sib).wait_recv()
            else:
                _remote(half, half, ici_s, ici_r, q * 3 + j, sib).wait_recv()

                @pl.when(c == (0 if j == q % 2 else 1))
                def _():
                    _remote(half, w_vm.at[3, q, mine, :], fw_s, fw_r, q, _peer(pos, CHIP_FLIPS[1 - j])).start()
            fwd = _remote(half, half, d2d_s, d2d_r, q * 3 + j, sib)
            fwd.start()
            passed.append(fwd)

        for q in range(NQ):
            if q == NQ - 1:
                hand_on(*order[0])
            block(0, q, s, q == 0)
        for n, (q, j) in enumerate(order):
            r, idx = 1 + j, q * 3 + j
            t = _shard_of(_peer(pos, CHIP_FLIPS[j]))
            if n + 1 < len(order):
                hand_on(*order[n + 1])
            if n + 1 == 2 * NQ - 1:
                for cp in w3_sends:
                    cp.start()
            _remote(w_vm.at[r, q, other, :], w_vm.at[r, q, other, :], d2d_s, d2d_r, idx, sib).wait_recv()
            block(r, q, t, False)
            cp = pltpu.make_async_copy(w_vm.at[r, q], fi_hbm.at[:, abs_col(t, q)], loc.at[2 + NQ + idx])
            cp.start()
            local.append(cp)
        for j, k in enumerate(CHIP_FLIPS):
            t = _shard_of(_peer(pos, k))
            _remote(s3.at[:, c], f3_hbm.at[:, t, c], w3_s_, w3_r_, j, sib).wait_recv()
            fwd = _remote(f3_hbm.at[:, t, c], f3_hbm.at[:, t, c], w3_s_, w3_r_, 3 + j, sib)
            fwd.start()
            passed.append(fwd)
        for j, k in enumerate(CHIP_FLIPS):
            t = _shard_of(_peer(pos, k))
            _remote(s3.at[:, c], f3_hbm.at[:, t, 1 - c], w3_s_, w3_r_, 3 + j, sib).wait_recv()
        for cp in sends + w3_sends + passed:
            cp.wait_send()
        for q in range(NQ):
            _remote(w_vm.at[1, q, mine, :], w_vm.at[3, q, mine, :], fw_s, fw_r, q, sib).wait_send()
        for cp in local:
            cp.wait()
        for slot in range(2):
            out_copy(slot, pl.ds(0, TM), abs_col(s, 0)).wait()

    n_loc = 2 + 5 * NQ
    return pl.pallas_call(
        body, name="ag_in_proj",
        in_specs=[VMEM_FULL, ANY, VMEM_FULL, VMEM_FULL, VMEM_FULL, VMEM_FULL, VMEM_FULL], out_specs=[ANY] * 5,
        out_shape=[jax.ShapeDtypeStruct((L, N_SHARD * Wc), BF16),
                   jax.ShapeDtypeStruct((L, Dh), BF16), jax.ShapeDtypeStruct((L, Dh), BF16),
                   jax.ShapeDtypeStruct((D, N_SHARD * Wc), BF16), jax.ShapeDtypeStruct((3, N_SHARD, 2, Do, D), BF16)],
        scratch_shapes=[pltpu.VMEM((N_SHARD, NQ, D, Wq), BF16), pltpu.VMEM((3, 2, Do, D), BF16), pltpu.VMEM((2, TM, Wq), BF16), pltpu.VMEM((TM, Dh), BF16),
                        pltpu.SemaphoreType.DMA((3 * NQ,)), pltpu.SemaphoreType.DMA((3 * NQ,)),
                        pltpu.SemaphoreType.DMA((3 * NQ,)), pltpu.SemaphoreType.DMA((3 * NQ,)),
                        pltpu.SemaphoreType.DMA((6,)), pltpu.SemaphoreType.DMA((6,)),
                        pltpu.SemaphoreType.DMA((NQ,)), pltpu.SemaphoreType.DMA((NQ,)),
                        pltpu.SemaphoreType.DMA((n_loc,)), pltpu.SemaphoreType.DMA((2,)), pltpu.SemaphoreType.DMA],
        compiler_params=_cparams(None, VMEM_LIMIT),
    )(xm, w_in_s, *w3_s, c2, s2)


def _pair_exchange_w3(dw3):
    _, _, _, Do, D = dw3.shape

    def build(ins, outs, send, recv):
        pos = _position()
        return [_remote(ins[0].at[:, :, 1 - pos[2]], outs[0], send, recv, 0, _peer(pos, 1))]

    return _Exchange((dw3,), (jax.ShapeDtypeStruct((3, N_SHARD, Do, D), F32),), 1, build)


def _sum_pair_in(dw_mine, ri):
    Dh, Wf = dw_mine.shape
    Wc = Wf // N_SHARD
    tr = min(256, Dh)

    def body(a_ref, b_ref, o_ref):
        o_ref[...] = (a_ref[...] + b_ref[...]).astype(BF16)

    return pl.pallas_call(
        body, name="sum_pair_in", grid=(Dh // tr, N_SHARD),
        in_specs=[pl.BlockSpec((tr, Wc), lambda i, t: (i, t)), pl.BlockSpec((tr, Wc), lambda i, t: (i, t))],
        out_specs=pl.BlockSpec((None, tr, Wc), lambda i, t: (t, i, 0)),
        out_shape=jax.ShapeDtypeStruct((N_SHARD, Dh, Wc), BF16),
        compiler_params=_cparams(("parallel", "parallel")),
    )(dw_mine, ri)


def _sum_pair_w3(cidx, dw3, r3):
    _, _, _, Do, D = dw3.shape

    def body(c_ref, a_ref, b_ref, o_ref):
        o_ref[...] = (a_ref[...] + b_ref[...]).astype(BF16)

    return pl.pallas_call(
        body, name="sum_pair_w3",
        grid_spec=pltpu.PrefetchScalarGridSpec(
            num_scalar_prefetch=1, grid=(3,),
            in_specs=[pl.BlockSpec((None, N_SHARD, None, Do, D), lambda a, c: (a, 0, c[0], 0, 0)),
                      pl.BlockSpec((None, N_SHARD, Do, D), lambda a, c: (a, 0, 0, 0))],
            out_specs=pl.BlockSpec((None, N_SHARD, Do, D), lambda a, c: (a, 0, 0, 0))),
        out_shape=jax.ShapeDtypeStruct((3, N_SHARD, Do, D), BF16),
        compiler_params=_cparams(("parallel",)),
    )(cidx, dw3, r3)


def _chips_exchange_in(cs_in):
    _, Dh, Wc = cs_in.shape

    def build(ins, outs, send, recv):
        pos = _position()
        return [_remote(ins[0].at[_shard_of(_peer(pos, k))], outs[0].at[j], send, recv, j, _peer(pos, k))
                for j, k in enumerate(CHIP_FLIPS)]

    return _Exchange((cs_in,), (jax.ShapeDtypeStruct((3, Dh, Wc), BF16),), 3, build)


def _chips_exchange_w3(cs_3):
    _, _, Do, D = cs_3.shape

    def build(ins, outs, send, recv):
        pos = _position()
        return [_remote(ins[0].at[:, _shard_of(_peer(pos, k))], outs[0].at[j], send, recv, j, _peer(pos, k))
                for j, k in enumerate(CHIP_FLIPS)]

    return _Exchange((cs_3,), (jax.ShapeDtypeStruct((3, 3, Do, D), BF16),), 3, build)


def _sum_chips_in(csidx, cs_in, rb_in):
    _, Dh, Wc = cs_in.shape
    tr = min(256, Dh)

    def body(s_ref, a_ref, b_ref, o_ref):
        acc = a_ref[...].astype(F32)
        for j in range(3):
            acc = acc + b_ref[j].astype(F32)
        o_ref[...] = acc

    return pl.pallas_call(
        body, name="sum_chips_in",
        grid_spec=pltpu.PrefetchScalarGridSpec(
            num_scalar_prefetch=1, grid=(Dh // tr,),
            in_specs=[pl.BlockSpec((None, tr, Wc), lambda i, s: (s[1], i, 0)),
                      pl.BlockSpec((3, tr, Wc), lambda i, s: (0, i, 0))],
            out_specs=pl.BlockSpec((None, tr, Wc), lambda i, s: (s[0], i, 0))),
        out_shape=jax.ShapeDtypeStruct((2, Dh, Wc), F32),
        compiler_params=_cparams(("parallel",)),
    )(csidx, cs_in, rb_in)


def _sum_chips_w3(csidx, cs_3, rb_3):
    _, _, Do, D = cs_3.shape

    def body(s_ref, a_ref, b_ref, o_ref):
        acc = a_ref[...].astype(F32)
        for j in range(3):
            acc = acc + b_ref[j].astype(F32)
        o_ref[...] = acc

    return pl.pallas_call(
        body, name="sum_chips_w3",
        grid_spec=pltpu.PrefetchScalarGridSpec(
            num_scalar_prefetch=1, grid=(3,),
            in_specs=[pl.BlockSpec((None, None, Do, D), lambda a, s: (a, s[1], 0, 0)),
                      pl.BlockSpec((3, None, Do, D), lambda a, s: (0, a, 0, 0))],
            out_specs=pl.BlockSpec((None, None, Do, D), lambda a, s: (a, s[0], 0, 0))),
        out_shape=jax.ShapeDtypeStruct((3, 2, Do, D), F32),
        compiler_params=_cparams(("parallel",)),
    )(csidx, cs_3, rb_3)


def _adam_math(w, g, m, v):
    m = ADAM_B1 * m + (1.0 - ADAM_B1) * g
    v = ADAM_B2 * v + (1.0 - ADAM_B2) * (g * g)
    m_hat = m / (1.0 - ADAM_B1 ** ADAM_STEP)
    v_hat = v / (1.0 - ADAM_B2 ** ADAM_STEP)
    delta = -ADAM_LR * (m_hat / (jnp.sqrt(v_hat) + ADAM_EPS) + ADAM_WD * w)
    return delta, m, v


def _adamw(w, g, m, v, name):
    R, C = w.shape
    tr = min(128, R)

    def body(w_ref, g_ref, m_ref, v_ref, d_ref, nm_ref, nv_ref):
        d_ref[...], nm_ref[...], nv_ref[...] = _adam_math(w_ref[...], g_ref[...], m_ref[...], v_ref[...])

    blk = pl.BlockSpec((tr, C), lambda i: (i, 0))
    return pl.pallas_call(
        body, name=name, grid=(R // tr,), in_specs=[blk] * 4, out_specs=[blk] * 3,
        out_shape=[jax.ShapeDtypeStruct((R, C), F32)] * 3,
        compiler_params=_cparams(("parallel",), VMEM_LIMIT),
    )(w, g, m, v)


def _adamw3(ws, g3, ms, vs):
    R, C = ws[0].shape

    def body(*refs):
        w_refs, m_refs, v_refs = refs[0:3], refs[3:6], refs[6:9]
        g_ref, outs = refs[9], refs[10:]
        for a in range(3):
            @pl.when(pl.program_id(0) == a)
            def _(a=a):
                res = _adam_math(w_refs[a][...], g_ref[...], m_refs[a][...], v_refs[a][...])
                for q in range(3):
                    outs[3 * a + q][...] = res[q]

    full = pl.BlockSpec((R, C), lambda a: (0, 0))
    res = pl.pallas_call(
        body, name="adamw_w3", grid=(3,),
        in_specs=[full] * 9 + [pl.BlockSpec((None, R, C), lambda a: (a, 0, 0))], out_specs=[full] * 9,
        out_shape=[jax.ShapeDtypeStruct((R, C), F32)] * 9,
        compiler_params=_cparams(("arbitrary",), VMEM_LIMIT),
    )(*ws, *ms, *vs, g3)
    return res[0:3], res[3:6], res[6:9]


SMALL_ROWS = ("c_ctx", "norm_w", "conv_b", "gn_w", "final_norm_w")


def _bwd_small(stats, ada_w, Dq, gh_in, gh_3):
    D = stats[0].shape[1]
    Wm = ada_w.shape[1]

    def body(stx, stm, stc, stv, stl, stlc, aw_ref, gi_in, g3_in, tot_ref, dm_sh, gcw, da_ref, gi_ref, g3_ref,
             vec_ref, vbuf, dm, amine, abuf, s_v, r_v, s_a, r_a, s_g, r_g):
        pos = _position()
        me, s = _dev_id(pos), _shard_of(pos)
        c, sib = pos[2], _peer(pos, 1)
        halves = [_remote(gi_in.at[c], gi_ref.at[c], s_g, r_g, 0, sib),
                  _remote(g3_in.at[:, c], g3_ref.at[:, c], s_g, r_g, 1, sib)]
        for cp in halves:
            cp.start()
        vec_ref[...] = jnp.zeros_like(vec_ref)
        vec_ref[0:2, :] = stx[0:2, :]
        vec_ref[2:3, :] = stm[1:2, :]
        vec_ref[3:5, :] = stc[0:2, :]
        vec_ref[5:6, :] = stx[2:3, :] + stc[2:3, :]
        vec_ref[6:7, :] = stv[3:4, :]
        vec_ref[7:8, :] = stm[3:4, :]
        vec_ref[8:9, :] = stm[0:1, :]
        vec_ref[9:12, :] = stv[0:3, :]
        vec_ref[12:14, 0:128] = stl[0:2, :] + stlc[0:2, :]
        vec_ref[14:15, :] = stm[2:3, :]
        vbuf[me] = vec_ref[...]
        sends = [_remote(vec_ref, vbuf.at[me], s_v, r_v, k - 1, _peer(pos, k)) for k in range(1, 8)]
        for cp in sends:
            cp.start()
        for k in range(1, 8):
            _remote(vec_ref, vbuf.at[_dev_id(_peer(pos, k))], s_v, r_v, k - 1, _peer(pos, k)).wait_recv()
        tot = vbuf[0]
        for d in range(1, N_DEV):
            tot = tot + vbuf[d]
        dm[...] = jnp.zeros_like(dm)
        for d in range(N_DEV):
            for r in range(3):
                dm[d:d + 1, r * D:(r + 1) * D] = vbuf[d, r:r + 1, :]
        dm[8:9, 0:D] = tot[3:4, :]
        dm[8:9, D:2 * D] = tot[4:5, :]
        for t in range(N_SHARD):
            @pl.when(s == t)
            def _(t=t):
                dm_sh[...] = dm[:, t * Wm:(t + 1) * Wm]
                gcw[...] = tot[9:12, t * Dq:(t + 1) * Dq]
        tot_ref[...] = tot
        part = lax.dot_general(dm_sh[8:16, :], aw_ref[...], (((1,), (1,)), ((), ())),
                               precision=lax.Precision.HIGHEST, preferred_element_type=F32)
        amine[...] = part
        abuf[s] = part
        asend = [_remote(amine, abuf.at[s], s_a, r_a, j, _peer(pos, k)) for j, k in enumerate(CHIP_FLIPS)]
        for cp in asend:
            cp.start()
        for j, k in enumerate(CHIP_FLIPS):
            _remote(amine, abuf.at[_shard_of(_peer(pos, k))], s_a, r_a, j, _peer(pos, k)).wait_recv()
        da = abuf[0]
        for t in range(1, N_SHARD):
            da = da + abuf[t]
        da_ref[...] = da
        _remote(gi_in.at[1 - c], gi_ref.at[1 - c], s_g, r_g, 0, sib).wait_recv()
        _remote(g3_in.at[:, 1 - c], g3_ref.at[:, 1 - c], s_g, r_g, 1, sib).wait_recv()
        for cp in sends + asend + halves:
            cp.wait_send()

    row = lambda *shape: jax.ShapeDtypeStruct(shape, F32)
    return pl.pallas_call(
        body, name="bwd_small",
        in_specs=[VMEM_FULL] * 7 + [ANY, ANY], out_specs=[VMEM_FULL] * 4 + [ANY, ANY],
        input_output_aliases={7: 4, 8: 5},
        out_shape=[row(16, D), row(16, Wm), row(3, Dq), row(8, D), row(*gh_in.shape), row(*gh_3.shape)],
        scratch_shapes=[pltpu.VMEM((16, D), F32), pltpu.VMEM((N_DEV, 16, D), F32), pltpu.VMEM((16, 3 * D), F32),
                        pltpu.VMEM((8, D), F32), pltpu.VMEM((N_SHARD, 8, D), F32),
                        pltpu.SemaphoreType.DMA((7,)), pltpu.SemaphoreType.DMA((7,)),
                        pltpu.SemaphoreType.DMA((3,)), pltpu.SemaphoreType.DMA((3,)),
                        pltpu.SemaphoreType.DMA((2,)), pltpu.SemaphoreType.DMA((2,))],
        compiler_params=_cparams(None, VMEM_LIMIT),
    )(*stats, ada_w, gh_in, gh_3)


def _small_update(tot, dm_sh, gcw, da, act, rows, ab, cw, dl):
    D = act.shape[1]
    Wm = dm_sh.shape[1]
    Dq = gcw.shape[1]
    H = dl[0].shape[1]
    params = tuple(rows) + (ab, cw, dl)
    n_p = len(params)
    whole = (slice(None), slice(None))

    def body(tot_ref, dm_ref, gcw_ref, da_ref, act_ref, *refs):
        wmv = [refs[3 * k:3 * k + 3] for k in range(n_p)]
        gaw_ref, loss_ref = refs[3 * n_p:3 * n_p + 2]
        outs = refs[3 * n_p + 2:]
        o_q = [outs[n_p * q:n_p * (q + 1)] for q in range(4)]
        tot = tot_ref[...]
        gaw_ref[...] = lax.dot_general(act_ref[...], dm_ref[...], (((0,), (0,)), ((), ())),
                                       precision=lax.Precision.HIGHEST, preferred_element_type=F32)
        loss_ref[...] = (0.5 / D) * _sum_all(tot[14:15, :])
        cc = wmv[0][0][...]
        sg = _sigmoid(cc)
        g_cctx = da_ref[0:1, :] * (sg * (1.0 + cc * (1.0 - sg)))

        def emit(k, g, at=whole):
            w_ref, m_ref, v_ref = wmv[k]
            for q, val in enumerate((g,) + _adam_math(w_ref[at], g, m_ref[at], v_ref[at])):
                o_q[q][k][at] = val

        for k, g in enumerate([g_cctx, tot[5:6, :], tot[6:7, :], tot[7:8, :], tot[8:9, :]]):
            emit(k, g)
        for r, g in enumerate([tot[0:1, :] + tot[3:4, :], tot[1:2, :] + tot[4:5, :], tot[2:3, :]]):
            emit(n_p - 3, g, (slice(0, 1), slice(r * D, (r + 1) * D)))
        emit(n_p - 2, gcw_ref[...])
        emit(n_p - 1, tot[12:14, 0:H] * _sigmoid(-wmv[n_p - 1][0][...]))

    row = lambda *shape: jax.ShapeDtypeStruct(shape, F32)
    per_q = [row(1, D)] * len(rows) + [row(1, 3 * D), row(3, Dq), row(2, H)]
    res = pl.pallas_call(
        body, name="small_update",
        in_specs=[VMEM_FULL] * (5 + 3 * n_p), out_specs=[VMEM_FULL] * (2 + 4 * n_p),
        out_shape=[row(D, Wm), row(1, 1)] + per_q * 4,
        compiler_params=_cparams(None, VMEM_LIMIT),
    )(tot, dm_sh, gcw, da, act, *[a for p in params for a in p])
    return res[0], res[1], [res[2 + n_p * q:2 + n_p * (q + 1)] for q in range(4)]


def kernel(x, c, ctx, c_ctx, norm_w, ada_w, ada_b, w_in, conv_w, conv_b, decay_logit, gn_w, w_a, w_b, w_out, final_norm_w, loss_target, m_c_ctx, m_norm_w, m_ada_w, m_ada_b, m_w_in, m_conv_w, m_conv_b, m_decay_logit, m_gn_w, m_w_a, m_w_b, m_w_out, m_final_norm_w, v_c_ctx, v_norm_w, v_ada_w, v_ada_b, v_w_in, v_conv_w, v_conv_b, v_decay_logit, v_gn_w, v_w_a, v_w_b, v_w_out, v_final_norm_w):
    L, D = x.shape[1], x.shape[2]
    Wc = w_in.shape[2]
    Do = D // 8
    pos = _position()
    cidx = jnp.reshape(pos[2], (1,)).astype(jnp.int32)
    sidx = jnp.reshape(_shard_of(pos), (1,)).astype(jnp.int32)

    act, mod_x, mod_c, conv_w8 = _fwd_small(c, c_ctx[None], ada_w[0], ada_b, conv_w[0])
    lg = jax.nn.log_sigmoid(decay_logit[0])

    w3_s = tuple(w[0].reshape(2, Do, D) for w in (w_a, w_b, w_out))

    def project(xm, c2, s2, w_in_bf):
        p, qr, kr, w_in_full, w3_full = _ag_in_proj(xm, w_in_bf, w3_s, c2, s2)
        return p, qr, kr, w_in_full, w3_full.reshape(3, D, D)

    csidx = jnp.concatenate([cidx, sidx])
    groups, dret_c, xmt, cmt, dx1, sc_x, w_in_full, gh_3, sts = _local_step(
        x[0], ctx[0], loss_target[0], mod_x, mod_c, norm_w, conv_w8, conv_b, lg, gn_w, final_norm_w[None],
        project, csidx, w_in[0])
    st_mid, st_conv, st_lg, st_lgc, st_c = sts

    dw_mine, ra_in = _dw_in(xmt, groups, cmt, dret_c, D, True)
    cs_in = _sum_pair_in(dw_mine, ra_in)
    grad_x, st_x, rb_in = _dxm(groups, 0, w_in_full, x[0], norm_w, sc_x, dx1, "dxm_x", _chips_exchange_in(cs_in))
    gh_in = _sum_chips_in(csidx, cs_in, rb_in)

    tot, dm_sh, gcw, da, g_in, g_3 = _bwd_small((st_x, st_mid, st_c, st_conv, st_lg, st_lgc), ada_w[0],
                                                conv_w.shape[2], gh_in, gh_3)
    g_w_in = g_in.reshape(D, Wc)
    g_3 = g_3.reshape(3, D // 4, D)
    rows = ((c_ctx[None], m_c_ctx[None], v_c_ctx[None]), (norm_w, m_norm_w, v_norm_w), (conv_b, m_conv_b, v_conv_b),
            (gn_w, m_gn_w, v_gn_w), (final_norm_w[None], m_final_norm_w[None], v_final_norm_w[None]))
    g_ada_w, loss, small = _small_update(tot, dm_sh, gcw, da, act, rows, (ada_b, m_ada_b, v_ada_b),
                                         (conv_w[0], m_conv_w[0], v_conv_w[0]),
                                         (decay_logit[0], m_decay_logit[0], v_decay_logit[0]))

    upd_in = _adamw(w_in[0], g_w_in, m_w_in[0], v_w_in[0], "adamw_w_in")
    upd_ada = _adamw(ada_w[0], g_ada_w, m_ada_w[0], v_ada_w[0], "adamw_ada_w")
    upd_a, upd_b, upd_o = _adamw3((w_a[0], w_b[0], w_out[0]), g_3, (m_w_a[0], m_w_b[0], m_w_out[0]),
                                  (v_w_a[0], v_w_b[0], v_w_out[0]))

    def leaves(q):
        big = lambda g, upd: (g if q == 0 else upd[q - 1])[None]
        r_cctx, r_norm, r_convb, r_gn, r_fnorm, r_ab, r_cw, r_dl = small[q]
        return [r_cctx.reshape(D), r_norm, big(g_ada_w, upd_ada), r_ab, big(g_w_in, upd_in),
                r_cw[None], r_convb, r_dl[None], r_gn,
                big(g_3[0], upd_a), big(g_3[1], upd_b), big(g_3[2], upd_o), r_fnorm.reshape(D)]

    return (loss.reshape(()), grad_x[None], *leaves(0), *leaves(1), *leaves(2), *leaves(3))
```

```python
from typing import Callable, NamedTuple

import jax
import jax.numpy as jnp
import numpy as np
from jax import lax
from jax.experimental import pallas as pl
from jax.experimental.pallas import tpu as pltpu

F32 = jnp.float32
BF16 = jnp.bfloat16
MESH = pl.DeviceIdType.MESH

CHUNK = 128
RET_CPB = 4
DV = 128
DK = 64
GRID_W = 64
ROPE_BASE = 10000.0
EPS = 1e-6
K_SCALE = DK ** -0.5
N_SHARD = 4
N_DEV = 8

ADAM_LR = 0.001
ADAM_B1 = 0.9
ADAM_B2 = 0.999
ADAM_EPS = 1e-08
ADAM_WD = 0.01
ADAM_STEP = 10

VMEM_LIMIT = 56 * 1024 * 1024


def _cparams(sem=None, vmem=None):
    kw = {}
    if sem is not None:
        kw["dimension_semantics"] = sem
    if vmem is not None:
        kw["vmem_limit_bytes"] = vmem
    return pltpu.CompilerParams(**kw)


def _dot(a, b):
    return jnp.dot(a, b, preferred_element_type=F32)


def _dot_nt(a, b):
    return lax.dot_general(a, b, (((1,), (1,)), ((), ())), preferred_element_type=F32)


def _dot_tn(a, b):
    return lax.dot_general(a, b, (((0,), (0,)), ((), ())), preferred_element_type=F32)


def _sigmoid(x):
    return 1.0 / (1.0 + jnp.exp(-x))


def _sum_all(x):
    return jnp.sum(jnp.sum(x, axis=1, keepdims=True), axis=0, keepdims=True)


def _swap_halves(t):
    n = t.shape[1]
    lane = lax.broadcasted_iota(jnp.int32, t.shape, 1)
    low = (lane & 32) == 0
    return jnp.where(low, pltpu.roll(t, n - 32, 1), pltpu.roll(t, 32, 1))


def _vec_spec(d):
    return pl.BlockSpec((1, d), lambda *a: (0, 0))


def _norm_mod(x, nw, sc, sh, name, also_bf16=None):
    L, D = x.shape
    tl = min(256, L)
    nt = L // tl

    def body(x_ref, nw_ref, sc_ref, sh_ref, *rest):
        xm_ref, xmt_ref = rest[-3:-1] if also_bf16 is not None else rest
        xv = x_ref[...]
        r = lax.rsqrt(jnp.mean(xv * xv, axis=-1, keepdims=True) + EPS)
        xm = (xv * r * nw_ref[...]) * (1.0 + sc_ref[...]) + sh_ref[...]
        xm_b = xm.astype(BF16)
        xm_ref[...] = xm_b
        xmt_ref[...] = xm_b.T
        if also_bf16 is not None:
            rest[-1][...] = rest[0][...].astype(BF16)

    in_specs = [pl.BlockSpec((tl, D), lambda i: (i, 0)), _vec_spec(D), _vec_spec(D), _vec_spec(D)]
    out_specs = [pl.BlockSpec((tl, D), lambda i: (i, 0)), pl.BlockSpec((D, tl), lambda i: (0, i))]
    out_shape = [jax.ShapeDtypeStruct((L, D), BF16), jax.ShapeDtypeStruct((D, L), BF16)]
    args = [x, nw, sc, sh]
    if also_bf16 is not None:
        R, C = also_bf16.shape
        slab = pl.BlockSpec((R // nt, C), lambda i: (i, 0))
        in_specs.append(slab)
        out_specs.append(slab)
        out_shape.append(jax.ShapeDtypeStruct((R, C), BF16))
        args.append(also_bf16)
    return pl.pallas_call(
        body, name=name, grid=(nt,), in_specs=in_specs, out_specs=out_specs, out_shape=out_shape,
        compiler_params=_cparams(("parallel",)),
    )(*args)


QK_BLOCK, V_BLOCK = 4, 5


def _in_proj(xm, w, name, first=0, count=None):
    M, D = xm.shape
    count = w.shape[1] // D if count is None else count
    tm = min(1024, M)

    def body(a_ref, b_ref, o_ref, qk_ref):
        acc = _dot(a_ref[...], b_ref[...])
        o_ref[...] = acc.astype(o_ref.dtype)

        @pl.when(pl.program_id(1) == QK_BLOCK - first)
        def _():
            qk_ref[...] = acc

    return pl.pallas_call(
        body, name=name, grid=(M // tm, count),
        in_specs=[pl.BlockSpec((tm, D), lambda i, j: (i, 0)), pl.BlockSpec((D, D), lambda i, j: (0, first + j))],
        out_specs=[pl.BlockSpec((tm, D), lambda i, j: (i, j)), pl.BlockSpec((tm, D), lambda i, j: (i, 0))],
        out_shape=[jax.ShapeDtypeStruct((M, count * D), BF16), jax.ShapeDtypeStruct((M, D), F32)],
        compiler_params=_cparams(("parallel", "arbitrary")),
    )(xm, w)


def _halo_specs(tl, L, D, col):
    hb = tl // 16
    last = L // 16 - 1
    prev = pl.BlockSpec((16, D), lambda i: (jnp.maximum(i * hb - 1, 0), col))
    nxt = pl.BlockSpec((16, D), lambda i: (jnp.minimum((i + 1) * hb, last), col))
    return prev, nxt


def _shift_rows(u, above, below):
    tl = u.shape[0]
    row = lax.broadcasted_iota(jnp.int32, u.shape, 0)
    dn = jnp.where(row == 0, above, pltpu.roll(u, 1, 0))
    up = jnp.where(row == tl - 1, below, pltpu.roll(u, tl - 1, 0))
    return dn, up


def _rope_tables(L):
    pos = np.arange(L)
    row = (pos // GRID_W).astype(np.float32)
    col = (pos % GRID_W).astype(np.float32)
    nf = DK // 4
    inv = np.float32(ROPE_BASE) ** (-np.arange(nf, dtype=np.float32) / np.float32(nf))
    ang = np.concatenate([row[:, None] * inv, col[:, None] * inv], axis=-1).astype(np.float32)
    cos, sin = np.cos(ang), np.sin(ang)
    return (jnp.asarray(np.concatenate([cos, cos, cos, cos], axis=-1), F32),
            jnp.asarray(np.concatenate([-sin, sin, -sin, sin], axis=-1), F32))


def _smem_spec():
    return pl.BlockSpec(memory_space=pltpu.SMEM)


def _pair_select(e0, e1):
    row = lax.broadcasted_iota(jnp.int32, e0.shape, 0)
    return jnp.where(row < DK, e0, e1)


def _head_lane_mask(shape, e):
    lane = lax.broadcasted_iota(jnp.int32, shape, 1)
    return (lane < DK) if e == 0 else (lane >= DK)


def _ctx_states(pc, pqk_c, lg, D):
    Lc = pc.shape[0]
    H = D // DV

    def body(lg_ref, k_ref, v_ref, s_ref):
        m = lax.broadcasted_iota(jnp.int32, (Lc, DV), 0).astype(F32)
        for pr in range(H // 2):
            k2 = k_ref[:, pr * 128:(pr + 1) * 128].astype(F32) * K_SCALE
            res = [[None, None], [None, None]]
            for e in range(2):
                h = 2 * pr + e
                v = v_ref[:, h * DV:(h + 1) * DV]
                dec_f = jnp.exp(lg_ref[0, h] * (Lc - 1.0 - m))
                dec_b = jnp.exp(lg_ref[1, h] * m)
                res[0][e] = _dot_tn((k2 * dec_f).astype(BF16), v)
                res[1][e] = _dot_tn((k2 * dec_b).astype(BF16), v)
            s_ref[0, pr] = _pair_select(res[0][0], res[0][1])
            s_ref[1, pr] = _pair_select(res[1][0], res[1][1])

    return pl.pallas_call(
        body, name="ctx_states", grid=(1,),
        in_specs=[_smem_spec(), pl.BlockSpec((Lc, D // 2), lambda i: (0, 1)), pl.BlockSpec((Lc, D), lambda i: (0, 1))],
        out_specs=pl.BlockSpec((2, H // 2, 128, 128), lambda i: (0, 0, 0, 0)),
        out_shape=jax.ShapeDtypeStruct((2, H // 2, 128, 128), F32),
    )(lg, pqk_c, pc)


T_M, T_MT = 0, 1
T_MF1, T_MB1 = 2, 3
T_QF, T_QB = 4, 5
T_KF, T_KB = 6, 7


def _decay_tables(lg, H):
    def body(lg_ref, t_ref):
        h = pl.program_id(0)
        lgf, lgb = lg_ref[0, h], lg_ref[1, h]
        i = lax.broadcasted_iota(jnp.int32, (CHUNK, CHUNK), 0).astype(F32)
        j = lax.broadcasted_iota(jnp.int32, (CHUNK, CHUNK), 1).astype(F32)
        d = i - j
        mf = jnp.where(d > 0, jnp.exp(lgf * jnp.maximum(d, 0.0)), 0.0)
        mb = jnp.where(d < 0, jnp.exp(lgb * jnp.maximum(-d, 0.0)), 0.0)
        mf_t = jnp.where(d < 0, jnp.exp(lgf * jnp.maximum(-d, 0.0)), 0.0)
        mb_t = jnp.where(d > 0, jnp.exp(lgb * jnp.maximum(d, 0.0)), 0.0)
        diag = jnp.where(d == 0, 2.0, 0.0)
        t_ref[0, T_M] = mf + mb + diag
        t_ref[0, T_MT] = mf_t + mb_t + diag
        t_ref[0, T_MF1] = mf * d
        t_ref[0, T_MB1] = mb * (-d)
        t_ref[0, T_QF] = jnp.exp(lgf * (i + 1.0))
        t_ref[0, T_QB] = jnp.exp(lgb * (CHUNK - i))
        t_ref[0, T_KF] = jnp.exp(lgf * (CHUNK - 1.0 - i))
        t_ref[0, T_KB] = jnp.exp(lgb * i)

    return pl.pallas_call(
        body, name="decay_tables", grid=(H,), in_specs=[_smem_spec()],
        out_specs=pl.BlockSpec((1, 8, CHUNK, CHUNK), lambda h: (h, 0, 0, 0)),
        out_shape=jax.ShapeDtypeStruct((H, 8, CHUNK, CHUNK), F32),
    )(lg)


def _tab_spec(H):
    return pl.BlockSpec((H, 8, CHUNK, CHUNK), lambda n: (0, 0, 0, 0))


def _chunk_decay(tab_ref, h):
    return tab_ref[h, T_QF, CHUNK - 1:CHUNK, :], tab_ref[h, T_QB, 0:1, :]


def _ret_states(kr, p, s0, tab, D):
    L = kr.shape[0]
    H = D // DV
    N = L // CHUNK
    HP = H // 2

    def body(tab_ref, kf_ref, kb_ref, vf_ref, vb_ref, s0_ref, sf_out, sb_out, sf, sb):
        n = pl.program_id(0)

        @pl.when(n == 0)
        def _():
            sf[...] = s0_ref[0]
            sb[...] = s0_ref[1]

        for cc in range(RET_CPB):
            cf_, cb_ = cc, RET_CPB - 1 - cc
            rf, rb = slice(cf_ * CHUNK, (cf_ + 1) * CHUNK), slice(cb_ * CHUNK, (cb_ + 1) * CHUNK)
            sf_out[cf_] = sf[...]
            sb_out[cb_] = sb[...]
            for pr in range(HP):
                kf2 = kf_ref[rf, pr * 128:(pr + 1) * 128].astype(F32)
                kb2 = kb_ref[rb, pr * 128:(pr + 1) * 128].astype(F32)
                inc_f, inc_b, gf, gb = [], [], [], []
                for e in range(2):
                    h = 2 * pr + e
                    inc_f.append(_dot_tn((kf2 * tab_ref[h, T_KF]).astype(BF16), vf_ref[rf, h * DV:(h + 1) * DV]))
                    inc_b.append(_dot_tn((kb2 * tab_ref[h, T_KB]).astype(BF16), vb_ref[rb, h * DV:(h + 1) * DV]))
                    cf, cb = _chunk_decay(tab_ref, h)
                    gf.append(jnp.broadcast_to(cf, (128, 128)))
                    gb.append(jnp.broadcast_to(cb, (128, 128)))
                sf[pr] = _pair_select(gf[0], gf[1]) * sf[pr] + _pair_select(inc_f[0], inc_f[1])
                sb[pr] = _pair_select(gb[0], gb[1]) * sb[pr] + _pair_select(inc_b[0], inc_b[1])

    st = jax.ShapeDtypeStruct((N, HP, 128, 128), F32)
    R = RET_CPB * CHUNK
    NB = N // RET_CPB
    return _riding_call(
        body, None, NB, name="ret_states", args=(tab, kr, kr, p, p, s0),
        in_specs=[_tab_spec(H),
                  pl.BlockSpec((R, D // 2), lambda n: (n, 0)),
                  pl.BlockSpec((R, D // 2), lambda n: (NB - 1 - n, 0)),
                  pl.BlockSpec((R, D), lambda n: (n, 5)),
                  pl.BlockSpec((R, D), lambda n: (NB - 1 - n, 5)),
                  pl.BlockSpec((2, HP, 128, 128), lambda n: (0, 0, 0, 0))],
        out_specs=[pl.BlockSpec((RET_CPB, HP, 128, 128), lambda n: (n, 0, 0, 0)),
                   pl.BlockSpec((RET_CPB, HP, 128, 128), lambda n: (NB - 1 - n, 0, 0, 0))],
        out_shape=[st, st],
        scratch=[pltpu.VMEM((HP, 128, 128), F32), pltpu.VMEM((HP, 128, 128), F32)],
        cparams=_cparams(("arbitrary",)))


def _ret_out(qr, kr, p, sf_prev, sb_prev, gn_w, tab, D):
    L = qr.shape[0]
    H = D // DV
    N = L // CHUNK
    HP = H // 2

    def body(tab_ref, q_ref, k_ref, v_ref, zb_ref, sf_ref, sb_ref, gn_ref, o_ref, yb_ref):
        def chunk(cc, carry):
            rows = pl.ds(pl.multiple_of(cc * CHUNK, CHUNK), CHUNK)
            for pr in range(HP):
                q2 = q_ref[rows, pr * 128:(pr + 1) * 128]
                k2 = k_ref[rows, pr * 128:(pr + 1) * 128]
                sfp = sf_ref[cc, pr].astype(BF16)
                sbp = sb_ref[cc, pr].astype(BF16)
                for e in range(2):
                    h = 2 * pr + e
                    sl = slice(h * DV, (h + 1) * DV)
                    qm = jnp.where(_head_lane_mask(q2.shape, e), q2, jnp.zeros_like(q2))
                    a = (_dot_nt(qm, k2) * tab_ref[h, T_M]).astype(BF16)
                    qf = qm.astype(F32)
                    o = _dot(a, v_ref[rows, sl])
                    o += _dot((qf * tab_ref[h, T_QF]).astype(BF16), sfp)
                    o += _dot((qf * tab_ref[h, T_QB]).astype(BF16), sbp)
                    o_ref[rows, sl] = o
                    mu = jnp.mean(o, axis=-1, keepdims=True)
                    oc = o - mu
                    rstd = lax.rsqrt(jnp.mean(oc * oc, axis=-1, keepdims=True) + EPS)
                    zb = zb_ref[rows, sl].astype(F32)
                    yb_ref[rows, sl] = (zb * _sigmoid(zb) * (oc * rstd * gn_ref[:, sl])).astype(BF16)
            return carry

        lax.fori_loop(0, RET_CPB, chunk, 0)

    R = RET_CPB * CHUNK
    return _riding_call(
        body, None, N // RET_CPB, name="ret_out", args=(tab, qr, kr, p, p, sf_prev, sb_prev, gn_w),
        in_specs=[_tab_spec(H),
                  pl.BlockSpec((R, D // 2), lambda n: (n, 0)),
                  pl.BlockSpec((R, D // 2), lambda n: (n, 0)),
                  pl.BlockSpec((R, D), lambda n: (n, 5)),
                  pl.BlockSpec((R, D), lambda n: (n, 6)),
                  pl.BlockSpec((RET_CPB, HP, 128, 128), lambda n: (n, 0, 0, 0)),
                  pl.BlockSpec((RET_CPB, HP, 128, 128), lambda n: (n, 0, 0, 0)),
                  _vec_spec(D)],
        out_specs=[pl.BlockSpec((R, D), lambda n: (n, 0)), pl.BlockSpec((R, D), lambda n: (n, 0))],
        out_shape=[jax.ShapeDtypeStruct((L, D), F32), jax.ShapeDtypeStruct((L, D), BF16)],
        cparams=_cparams(("arbitrary",)))


def _mid(p, yb, o, x, tgt, w3, g, fw, conv_w, conv_b, gn_w, D):
    L = x.shape[0]
    H = D // DV
    tm = min(256, L)
    nt = L // tm

    def body(h_ref, bg_ref, cg_ref, za_ref, hp_ref, hn_ref, cp_ref, cn_ref, yb_ref, ga_ref, gb_ref, zb_ref, o_ref,
             x_ref, t_ref, w_hbm, g_ref, fw_ref, cw_ref, cb_ref, gn_ref,
             dx1_ref, dya_ref, do_ref, dzb_ref, dgab_ref, dw_hbm, st_ref, w_vm, dw_acc, sem):
        i = pl.program_id(0)

        @pl.when(i == 0)
        def _():
            cp = pltpu.make_async_copy(w_hbm, w_vm, sem)
            cp.start()
            dw_acc[...] = jnp.zeros_like(dw_acc)
            st_ref[...] = jnp.zeros_like(st_ref)
            cp.wait()

        u = cg_ref[...].astype(F32) * h_ref[...].astype(F32)
        above = jnp.where(i == 0, 0.0, cp_ref[15:16, :].astype(F32) * hp_ref[15:16, :].astype(F32))
        below = jnp.where(i == nt - 1, 0.0, cn_ref[0:1, :].astype(F32) * hn_ref[0:1, :].astype(F32))
        dn, up = _shift_rows(u, above, below)
        co = cw_ref[0:1, :] * dn + cw_ref[1:2, :] * u + cw_ref[2:3, :] * up + cb_ref[...]
        za = za_ref[...].astype(F32)
        ya_b = (za * _sigmoid(za) * bg_ref[...].astype(F32) * co).astype(BF16)
        yb_b = yb_ref[...]
        y_a = _dot(ya_b, w_vm[0])
        y_b = _dot(yb_b, w_vm[1])
        sga = _sigmoid(ga_ref[...].astype(F32))
        sgb = _sigmoid(gb_ref[...].astype(F32))
        mix_b = (sga * y_a + sgb * y_b).astype(BF16)
        y_x = _dot(mix_b, w_vm[2])
        gvec, fwv = g_ref[...], fw_ref[...]
        x1 = x_ref[...] + gvec * y_x
        r1 = lax.rsqrt(jnp.mean(x1 * x1, axis=-1, keepdims=True) + EPS)
        xh = x1 * r1
        diff = xh * fwv - t_ref[...]
        dout = diff * (1.0 / D)
        dxh = dout * fwv
        dx1 = r1 * (dxh - xh * jnp.mean(dxh * xh, axis=-1, keepdims=True))
        dx1_ref[...] = dx1
        st_ref[0:1, :] += jnp.sum(dout * xh, axis=0, keepdims=True)
        st_ref[1:2, :] += jnp.sum(dx1 * y_x, axis=0, keepdims=True)
        st_ref[2:3, :] += jnp.sum(diff * diff, axis=0, keepdims=True)
        dyx_b = (dx1 * gvec).astype(BF16)
        dmix = _dot_nt(dyx_b, w_vm[2])
        dw_acc[2] += _dot_tn(mix_b, dyx_b)
        dya_b = (dmix * sga).astype(BF16)
        dyb_b = (dmix * sgb).astype(BF16)
        dgab_ref[:, 0:D] = (dmix * y_a * sga * (1.0 - sga)).astype(BF16)
        dgab_ref[:, D:2 * D] = (dmix * y_b * sgb * (1.0 - sgb)).astype(BF16)
        dya_ref[...] = _dot_nt(dya_b, w_vm[0])
        dyb = _dot_nt(dyb_b, w_vm[1])
        dw_acc[0] += _dot_tn(ya_b, dya_b)
        dw_acc[1] += _dot_tn(yb_b, dyb_b)

        for h in range(H):
            sl = slice(h * DV, (h + 1) * DV)
            ov = o_ref[:, sl]
            oc = ov - jnp.mean(ov, axis=-1, keepdims=True)
            rstd = lax.rsqrt(jnp.mean(oc * oc, axis=-1, keepdims=True) + EPS)
            rn = oc * rstd
            gw = gn_ref[:, sl]
            zb = zb_ref[:, sl].astype(F32)
            sz = _sigmoid(zb)
            dy = dyb[:, sl]
            dzb_ref[:, sl] = (dy * (rn * gw) * (sz * (1.0 + zb * (1.0 - sz)))).astype(BF16)
            dretn = dy * (zb * sz)
            st_ref[3:4, sl] += jnp.sum(dretn * rn, axis=0, keepdims=True)
            drn = dretn * gw
            do_ref[:, sl] = (rstd * (drn - jnp.mean(drn, axis=-1, keepdims=True)
                                     - rn * jnp.mean(drn * rn, axis=-1, keepdims=True))).astype(BF16)

        @pl.when(i == nt - 1)
        def _():
            out = pltpu.make_async_copy(dw_acc, dw_hbm, sem)
            out.start()
            out.wait()

    row = lambda col: pl.BlockSpec((tm, D), lambda i: (i, col))
    any_spec = pl.BlockSpec(memory_space=pl.ANY)
    f32o = jax.ShapeDtypeStruct((L, D), F32)
    bf16o = jax.ShapeDtypeStruct((L, D), BF16)
    hp, hn = _halo_specs(tm, L, D, 0)
    cp, cn = _halo_specs(tm, L, D, 2)
    return pl.pallas_call(
        body, name="mid", grid=(nt,),
        in_specs=[row(0), row(1), row(2), row(3), hp, hn, cp, cn, row(0), row(7), row(8), row(6), row(0),
                  row(0), row(0), any_spec, _vec_spec(D), _vec_spec(D),
                  pl.BlockSpec((8, D), lambda i: (0, 0)), _vec_spec(D), _vec_spec(D)],
        out_specs=[row(0), row(0), row(0), row(0), pl.BlockSpec((tm, 2 * D), lambda i: (i, 0)), any_spec,
                   pl.BlockSpec((8, D), lambda i: (0, 0))],
        out_shape=[f32o, f32o, bf16o, bf16o, jax.ShapeDtypeStruct((L, 2 * D), BF16),
                   jax.ShapeDtypeStruct((3, D, D), F32), jax.ShapeDtypeStruct((8, D), F32)],
        scratch_shapes=[pltpu.VMEM((3, D, D), BF16), pltpu.VMEM((3, D, D), F32), pltpu.SemaphoreType.DMA],
        compiler_params=_cparams(("arbitrary",), VMEM_LIMIT),
    )(p, p, p, p, p, p, p, p, yb, p, p, p, o, x, tgt, w3, g, fw, conv_w, conv_b, gn_w)


def _conv_bwd(dya, p, conv_w, conv_b, D, exchange=None):
    L = p.shape[0]
    tl = min(256, L)
    nt = L // tl

    def body(d_ref, h_ref, bg_ref, cg_ref, za_ref,
             dp_ref, dn_ref, hp_ref, hn_ref, bp_ref, bn_ref, cp_ref, cn_ref, zp_ref, zn_ref,
             w_ref, b_ref, dc_ref, st_ref):
        i = pl.program_id(0)

        @pl.when(i == 0)
        def _():
            st_ref[...] = jnp.zeros_like(st_ref)

        first, last = i == 0, i == nt - 1
        h = h_ref[...].astype(F32)
        cg = cg_ref[...].astype(F32)
        bg = bg_ref[...].astype(F32)
        za = za_ref[...].astype(F32)
        dy = d_ref[...].astype(F32)
        u = cg * h
        u_above = jnp.where(first, 0.0, cp_ref[15:16, :].astype(F32) * hp_ref[15:16, :].astype(F32))
        u_below = jnp.where(last, 0.0, cn_ref[0:1, :].astype(F32) * hn_ref[0:1, :].astype(F32))
        u_dn, u_up = _shift_rows(u, u_above, u_below)
        w0, w1, w2 = w_ref[0:1, :], w_ref[1:2, :], w_ref[2:3, :]
        co = w0 * u_dn + w1 * u + w2 * u_up + b_ref[...]
        sz = _sigmoid(za)
        silu = za * sz
        dc_ref[:, 3 * D:4 * D] = (dy * bg * co * (sz * (1.0 + za * (1.0 - sz)))).astype(BF16)
        dc_ref[:, D:2 * D] = (dy * silu * co).astype(BF16)
        dco = dy * silu * bg

        def edge(dr, zr, br, r):
            z = zr[r:r + 1, :].astype(F32)
            return dr[r:r + 1, :].astype(F32) * (z * _sigmoid(z)) * br[r:r + 1, :].astype(F32)

        dco_above = jnp.where(first, 0.0, edge(dp_ref, zp_ref, bp_ref, 15))
        dco_below = jnp.where(last, 0.0, edge(dn_ref, zn_ref, bn_ref, 0))
        dco_dn, dco_up = _shift_rows(dco, dco_above, dco_below)
        du = w0 * dco_up + w1 * dco + w2 * dco_dn
        dc_ref[:, 2 * D:3 * D] = (du * h).astype(BF16)
        dc_ref[:, 0:D] = (du * cg).astype(BF16)
        st_ref[0:1, :] += jnp.sum(dco * u_dn, axis=0, keepdims=True)
        st_ref[1:2, :] += jnp.sum(dco * u, axis=0, keepdims=True)
        st_ref[2:3, :] += jnp.sum(dco * u_up, axis=0, keepdims=True)
        st_ref[3:4, :] += jnp.sum(dco, axis=0, keepdims=True)

    main = lambda col: pl.BlockSpec((tl, D), lambda i: (i, col))
    halos = []
    for col in (0, 0, 1, 2, 3):
        halos.extend(_halo_specs(tl, L, D, col))
    return _riding_call(
        body, exchange, nt, name="conv_bwd",
        args=(dya, p, p, p, p, dya, dya, p, p, p, p, p, p, p, p, conv_w, conv_b),
        in_specs=[main(0), main(0), main(1), main(2), main(3)] + halos
                 + [pl.BlockSpec((8, D), lambda i: (0, 0)), _vec_spec(D)],
        out_specs=[pl.BlockSpec((tl, 4 * D), lambda i: (i, 0)), pl.BlockSpec((8, D), lambda i: (0, 0))],
        out_shape=[jax.ShapeDtypeStruct((L, 4 * D), BF16), jax.ShapeDtypeStruct((8, D), F32)],
        cparams=_cparams(("arbitrary",)))


def _ret_bwd_states(qr, do, tab, D):
    L = qr.shape[0]
    H = D // DV
    N = L // CHUNK
    HP = H // 2

    def body(tab_ref, qf_ref, qb_ref, dof_ref, dob_ref, dsf_out, dsb_out, ds0_out, dsf, dsb):
        n = pl.program_id(0)

        @pl.when(n == 0)
        def _():
            dsf[...] = jnp.zeros_like(dsf)
            dsb[...] = jnp.zeros_like(dsb)

        for cc in range(RET_CPB):
            cf_, cb_ = RET_CPB - 1 - cc, cc
            rf, rb = slice(cf_ * CHUNK, (cf_ + 1) * CHUNK), slice(cb_ * CHUNK, (cb_ + 1) * CHUNK)
            dsf_out[cf_] = dsf[...]
            dsb_out[cb_] = dsb[...]
            for pr in range(HP):
                qf2 = qf_ref[rf, pr * 128:(pr + 1) * 128].astype(F32)
                qb2 = qb_ref[rb, pr * 128:(pr + 1) * 128].astype(F32)
                inc_f, inc_b, gf, gb = [], [], [], []
                for e in range(2):
                    h = 2 * pr + e
                    inc_f.append(_dot_tn((qf2 * tab_ref[h, T_QF]).astype(BF16), dof_ref[rf, h * DV:(h + 1) * DV]))
                    inc_b.append(_dot_tn((qb2 * tab_ref[h, T_QB]).astype(BF16), dob_ref[rb, h * DV:(h + 1) * DV]))
                    cf, cb = _chunk_decay(tab_ref, h)
                    gf.append(jnp.broadcast_to(cf, (128, 128)))
                    gb.append(jnp.broadcast_to(cb, (128, 128)))
                dsf[pr] = _pair_select(gf[0], gf[1]) * dsf[pr] + _pair_select(inc_f[0], inc_f[1])
                dsb[pr] = _pair_select(gb[0], gb[1]) * dsb[pr] + _pair_select(inc_b[0], inc_b[1])

        @pl.when(n == NB - 1)
        def _():
            ds0_out[0] = dsf[...]
            ds0_out[1] = dsb[...]

    st = jax.ShapeDtypeStruct((N, HP, 128, 128), F32)
    R = RET_CPB * CHUNK
    NB = N // RET_CPB
    return pl.pallas_call(
        body, name="ret_bwd_states", grid=(NB,),
        in_specs=[_tab_spec(H),
                  pl.BlockSpec((R, D // 2), lambda n: (NB - 1 - n, 0)),
                  pl.BlockSpec((R, D // 2), lambda n: (n, 0)),
                  pl.BlockSpec((R, D), lambda n: (NB - 1 - n, 0)),
                  pl.BlockSpec((R, D), lambda n: (n, 0))],
        out_specs=[pl.BlockSpec((RET_CPB, HP, 128, 128), lambda n: (NB - 1 - n, 0, 0, 0)),
                   pl.BlockSpec((RET_CPB, HP, 128, 128), lambda n: (n, 0, 0, 0)),
                   pl.BlockSpec((2, HP, 128, 128), lambda n: (0, 0, 0, 0))],
        out_shape=[st, st, jax.ShapeDtypeStruct((2, HP, 128, 128), F32)],
        scratch_shapes=[pltpu.VMEM((HP, 128, 128), F32), pltpu.VMEM((HP, 128, 128), F32)],
        compiler_params=_cparams(("arbitrary",)),
    )(tab, qr, qr, do, do)


def _ret_bwd_main(qr, kr, p, do, sf_prev, sb_prev, dsf, dsb, c2, s2, tab, D, exchange=None):
    L = qr.shape[0]
    H = D // DV
    N = L // CHUNK
    HP = H // 2
    W = D // 2

    def body(tab_ref, q_ref, k_ref, v_ref, do_ref, sf_ref, sb_ref, dsf_ref, dsb_ref, c_ref, s_ref,
             dr_ref, st_ref, dl_acc):
        @pl.when(pl.program_id(0) == 0)
        def _():
            dl_acc[...] = jnp.zeros_like(dl_acc)

        i = lax.broadcasted_iota(jnp.int32, (CHUNK, 128), 0).astype(F32)
        rowid = lax.broadcasted_iota(jnp.int32, (128, 128), 0)

        def chunk(cc, carry):
            rows = pl.ds(pl.multiple_of(cc * CHUNK, CHUNK), CHUNK)
            c, s = c_ref[rows, :], s_ref[rows, :]
            for pr in range(HP):
                ps = slice(pr * 128, (pr + 1) * 128)
                q2, k2 = q_ref[rows, ps], k_ref[rows, ps]
                sf32, sb32 = sf_ref[cc, pr], sb_ref[cc, pr]
                dsf32, dsb32 = dsf_ref[cc, pr], dsb_ref[cc, pr]
                sfp, sbp = sf32.astype(BF16), sb32.astype(BF16)
                dsfp, dsbp = dsf32.astype(BF16), dsb32.astype(BF16)
                dq2 = jnp.zeros((CHUNK, 128), F32)
                dk2 = jnp.zeros((CHUNK, 128), F32)
                for e in range(2):
                    h = 2 * pr + e
                    sl = slice(h * DV, (h + 1) * DV)
                    hm = _head_lane_mask(q2.shape, e)
                    qm = jnp.where(hm, q2, jnp.zeros_like(q2))
                    km = jnp.where(hm, k2, jnp.zeros_like(k2))
                    qf, kf = qm.astype(F32), km.astype(F32)
                    v, do = v_ref[rows, sl], do_ref[rows, sl]
                    vf, dof = v.astype(F32), do.astype(F32)
                    m_t = tab_ref[h, T_MT]
                    sc = _dot_nt(qm, k2)
                    dpm = _dot_nt(do, v)
                    dsc = (dpm * tab_ref[h, T_M]).astype(BF16)
                    a_t = (_dot_nt(km, q2) * m_t).astype(BF16)
                    dsc_t = (_dot_nt(v, do) * m_t).astype(BF16)
                    dq_f, dq_b = tab_ref[h, T_QF], tab_ref[h, T_QB]
                    dk_f, dk_b = tab_ref[h, T_KF], tab_ref[h, T_KB]
                    dq = _dot(dsc, km)
                    dq += jnp.where(hm, dq_f * _dot_nt(do, sfp) + dq_b * _dot_nt(do, sbp), 0.0)
                    dk = _dot(dsc_t, qm)
                    dk += jnp.where(hm, dk_f * _dot_nt(v, dsfp) + dk_b * _dot_nt(v, dsbp), 0.0)
                    kdf = _dot((kf * dk_f).astype(BF16), dsfp)
                    kdb = _dot((kf * dk_b).astype(BF16), dsbp)
                    dr_ref[rows, D + h * DV:D + (h + 1) * DV] = (_dot(a_t, do) + kdf + kdb).astype(BF16)
                    dq2 += dq
                    dk2 += dk
                    xf = _dot((qf * dq_f).astype(BF16), sfp)
                    xb = _dot((qf * dq_b).astype(BF16), sbp)
                    pair = (rowid < DK) if e == 0 else (rowid >= DK)
                    gcf, gcb = tab_ref[h, T_QF, CHUNK - 1:CHUNK, 0:1], tab_ref[h, T_QB, 0:1, 0:1]
                    scdp = sc * dpm
                    dl_acc[h, 0] += scdp * tab_ref[h, T_MF1] + xf * dof * (i + 1.0) \
                        + kdf * vf * (CHUNK - 1.0 - i) + (CHUNK * gcf) * jnp.where(pair, dsf32 * sf32, 0.0)
                    dl_acc[h, 1] += scdp * tab_ref[h, T_MB1] + xb * dof * (CHUNK - i) \
                        + kdb * vf * i + (CHUNK * gcb) * jnp.where(pair, dsb32 * sb32, 0.0)
                dr_ref[rows, ps] = (dq2 * c - _swap_halves(dq2) * s).astype(BF16)
                dr_ref[rows, W + pr * 128:W + (pr + 1) * 128] = \
                    ((dk2 * c - _swap_halves(dk2) * s) * K_SCALE).astype(BF16)
            return carry

        lax.fori_loop(0, RET_CPB, chunk, 0)

        @pl.when(pl.program_id(0) == N // RET_CPB - 1)
        def _():
            lane = lax.broadcasted_iota(jnp.int32, (1, 128), 1)
            acc = [jnp.zeros((1, 128), F32), jnp.zeros((1, 128), F32)]
            for h in range(H):
                for b in range(2):
                    acc[b] += jnp.where(lane == h, _sum_all(dl_acc[h, b]), 0.0)
            st_ref[...] = jnp.zeros_like(st_ref)
            st_ref[0:1, :] = acc[0]
            st_ref[1:2, :] = acc[1]

    R = RET_CPB * CHUNK
    st_spec = pl.BlockSpec((RET_CPB, HP, 128, 128), lambda n: (n, 0, 0, 0))
    half = pl.BlockSpec((R, W), lambda n: (n, 0))
    rope = pl.BlockSpec((R, 128), lambda n: (n, 0))
    return _riding_call(
        body, exchange, N // RET_CPB, name="ret_bwd_main",
        args=(tab, qr, kr, p, do, sf_prev, sb_prev, dsf, dsb, c2, s2),
        in_specs=[_tab_spec(H), half, half,
                  pl.BlockSpec((R, D), lambda n: (n, 5)),
                  pl.BlockSpec((R, D), lambda n: (n, 0)),
                  st_spec, st_spec, st_spec, st_spec, rope, rope],
        out_specs=[pl.BlockSpec((R, 2 * D), lambda n: (n, 0)),
                   pl.BlockSpec((8, 128), lambda n: (0, 0))],
        out_shape=[jax.ShapeDtypeStruct((L, 2 * D), BF16), jax.ShapeDtypeStruct((8, 128), F32)],
        scratch=[pltpu.VMEM((H, 2, CHUNK, 128), F32)],
        cparams=_cparams(("arbitrary",)))


def _ctx_bwd(pc, pqk_c, ds0, lg, D):
    Lc = pc.shape[0]
    H = D // DV
    HP = H // 2
    W = D // 2

    def body(lg_ref, k_ref, v_ref, ds_ref, dr_ref, st_ref):
        dqk_ref = dr_ref.at[:, 0:D]
        dv_ref = dr_ref.at[:, D:2 * D]
        m = lax.broadcasted_iota(jnp.int32, (Lc, 128), 0).astype(F32)
        lane = lax.broadcasted_iota(jnp.int32, (1, 128), 1)
        acc_f = jnp.zeros((1, 128), F32)
        acc_b = jnp.zeros((1, 128), F32)
        dqk_ref[:, 0:W] = jnp.zeros((Lc, W), BF16)
        for pr in range(HP):
            ps = slice(pr * 128, (pr + 1) * 128)
            k2 = k_ref[:, ps].astype(F32) * K_SCALE
            dsfp, dsbp = ds_ref[0, pr].astype(BF16), ds_ref[1, pr].astype(BF16)
            dk2 = jnp.zeros((Lc, 128), F32)
            for e in range(2):
                h = 2 * pr + e
                sl = slice(h * DV, (h + 1) * DV)
                hm = _head_lane_mask(k2.shape, e)
                km = jnp.where(hm, k2, 0.0)
                v = v_ref[:, sl]
                vf = v.astype(F32)
                dec_f = jnp.exp(lg_ref[0, h] * (Lc - 1.0 - m))
                dec_b = jnp.exp(lg_ref[1, h] * m)
                kdf = _dot((km * dec_f).astype(BF16), dsfp)
                kdb = _dot((km * dec_b).astype(BF16), dsbp)
                dv_ref[:, sl] = (kdf + kdb).astype(BF16)
                dk2 += jnp.where(hm, dec_f * _dot_nt(v, dsfp) + dec_b * _dot_nt(v, dsbp), 0.0)
                acc_f += jnp.where(lane == h, _sum_all(kdf * vf * (Lc - 1.0 - m)), 0.0)
                acc_b += jnp.where(lane == h, _sum_all(kdb * vf * m), 0.0)
            dqk_ref[:, W + pr * 128:W + (pr + 1) * 128] = (dk2 * K_SCALE).astype(BF16)
        st_ref[...] = jnp.zeros_like(st_ref)
        st_ref[0:1, :] = acc_f
        st_ref[1:2, :] = acc_b

    return pl.pallas_call(
        body, name="ctx_bwd", grid=(1,),
        in_specs=[_smem_spec(), pl.BlockSpec((Lc, W), lambda i: (0, 1)), pl.BlockSpec((Lc, D), lambda i: (0, 1)),
                  pl.BlockSpec((2, HP, 128, 128), lambda i: (0, 0, 0, 0))],
        out_specs=[pl.BlockSpec((Lc, 2 * D), lambda i: (0, 0)), pl.BlockSpec((8, 128), lambda i: (0, 0))],
        out_shape=[jax.ShapeDtypeStruct((Lc, 2 * D), BF16), jax.ShapeDtypeStruct((8, 128), F32)],
    )(lg, pqk_c, pc, ds0)


class _Exchange(NamedTuple):
    inputs: tuple
    out_shapes: tuple
    n_copies: int
    build: Callable


def _exchange_parts(exchange):
    if exchange is None:
        return [], [], [], [], []
    n = exchange.n_copies
    return (list(exchange.inputs), [ANY] * len(exchange.inputs), list(exchange.out_shapes),
            [ANY] * len(exchange.out_shapes), [pltpu.SemaphoreType.DMA((n,)), pltpu.SemaphoreType.DMA((n,))])


def _riding_call(body, exchange, n_steps, *, args, in_specs, out_specs, out_shape, name, cparams, scratch=()):
    ex_args, ex_in_specs, ex_shapes, ex_out_specs, ex_scratch = _exchange_parts(exchange)
    n_in, n_out, n_sc = len(args), len(out_shape), len(scratch)

    def riding(*refs):
        k = n_in + len(ex_args)
        ins, ex_in = refs[:n_in], refs[n_in:k]
        outs, ex_out = refs[k:k + n_out], refs[k + n_out:k + n_out + len(ex_shapes)]
        k += n_out + len(ex_shapes)
        own_scratch, ex_sems = refs[k:k + n_sc], refs[k + n_sc:]
        step = pl.program_id(0)
        if exchange is not None:
            @pl.when(step == 0)
            def _():
                for rc in exchange.build(ex_in, ex_out, *ex_sems):
                    rc.start()
        body(*ins, *outs, *own_scratch)
        if exchange is not None:
            @pl.when(step == n_steps - 1)
            def _():
                for rc in exchange.build(ex_in, ex_out, *ex_sems):
                    rc.wait()

    return tuple(pl.pallas_call(
        riding, name=name, grid=(n_steps,),
        in_specs=list(in_specs) + ex_in_specs, out_specs=list(out_specs) + ex_out_specs,
        out_shape=list(out_shape) + ex_shapes, scratch_shapes=list(scratch) + ex_scratch,
        compiler_params=cparams,
    )(*args, *ex_args))


def _dxm(groups, col0, w, x, nw, sc, dx1, name, exchange=None):
    L, D = x.shape
    tm = min(256, L)
    nt = L // tm
    ng = len(groups)
    widths = [g.shape[1] for g in groups]
    wtot = sum(widths)
    with_dx = dx1 is not None
    ex_args, ex_in_specs, ex_shapes, ex_out_specs, ex_scratch = _exchange_parts(exchange)
    n_in = ng + 4 + (1 if with_dx else 0)
    n_out = 2 if with_dx else 1

    def body(*refs):
        group_refs = refs[:ng]
        w_hbm, x_ref, nw_ref, sc_ref = refs[ng:ng + 4]
        ex_in = refs[n_in:n_in + len(ex_args)]
        outs = refs[n_in + len(ex_args):]
        if with_dx:
            dx1_ref, gx_ref, st_ref = refs[ng + 4], outs[0], outs[1]
        else:
            st_ref = outs[0]
        ex_out = outs[n_out:n_out + len(ex_shapes)]
        w_vm, sem = outs[n_out + len(ex_shapes):n_out + len(ex_shapes) + 2]
        ex_sems = outs[n_out + len(ex_shapes) + 2:]
        i = pl.program_id(0)

        @pl.when(i == 0)
        def _():
            cp = pltpu.make_async_copy(w_hbm.at[:, col0 * D:col0 * D + wtot], w_vm, sem)
            cp.start()
            if exchange is not None:
                for rc in exchange.build(ex_in, ex_out, *ex_sems):
                    rc.start()
            st_ref[...] = jnp.zeros_like(st_ref)
            cp.wait()

        dxm, off = None, 0
        for g_ref, wd in zip(group_refs, widths):
            part = _dot_nt(g_ref[...], w_vm[:, off:off + wd])
            dxm = part if dxm is None else dxm + part
            off += wd

        xv = x_ref[...]
        r = lax.rsqrt(jnp.mean(xv * xv, axis=-1, keepdims=True) + EPS)
        xh = xv * r
        nwv = nw_ref[...]
        dxn = dxm * (1.0 + sc_ref[...])
        st_ref[0:1, :] += jnp.sum(dxm, axis=0, keepdims=True)
        st_ref[1:2, :] += jnp.sum(dxm * (xh * nwv), axis=0, keepdims=True)
        st_ref[2:3, :] += jnp.sum(dxn * xh, axis=0, keepdims=True)
        if with_dx:
            dxh = dxn * nwv
            gx_ref[...] = dx1_ref[...] + r * (dxh - xh * jnp.mean(dxh * xh, axis=-1, keepdims=True))

        if exchange is not None:
            @pl.when(i == nt - 1)
            def _():
                for rc in exchange.build(ex_in, ex_out, *ex_sems):
                    rc.wait()

    row = pl.BlockSpec((tm, D), lambda i: (i, 0))
    in_specs = [pl.BlockSpec((tm, wd), lambda i: (i, 0)) for wd in widths] + [ANY, row, _vec_spec(D), _vec_spec(D)]
    out_specs = [pl.BlockSpec((8, D), lambda i: (0, 0))]
    out_shape = [jax.ShapeDtypeStruct((8, D), F32)]
    args = list(groups) + [w, x, nw, sc]
    if with_dx:
        in_specs.append(row)
        out_specs.insert(0, row)
        out_shape.insert(0, jax.ShapeDtypeStruct((L, D), F32))
        args.append(dx1)
    res = pl.pallas_call(
        body, name=name, grid=(nt,),
        in_specs=in_specs + ex_in_specs, out_specs=out_specs + ex_out_specs, out_shape=out_shape + ex_shapes,
        scratch_shapes=[pltpu.VMEM((D, wtot), BF16), pltpu.SemaphoreType.DMA] + ex_scratch,
        compiler_params=_cparams(("arbitrary",), VMEM_LIMIT),
    )(*args, *ex_args)
    gx = res[0] if with_dx else None
    return (gx, res[n_out - 1], *res[n_out:])


DW_TN = 512
DW_RING = 4


def _dw_in(xmt, groups, cmt, dr_c, D, pair):
    L = xmt.shape[1]
    Lc = cmt.shape[1]
    Dh = D // 2
    tn = min(DW_TN, D)
    nblk = [g.shape[1] // tn for g in groups]
    starts = [sum(nblk[:g]) for g in range(len(groups))]
    ng = len(groups)
    nj = sum(nblk)
    rows_out = Dh if pair else D

    def body(*refs):
        xt_hbm = refs[0]
        group_refs = refs[1:1 + ng]
        ct_hbm, drc_ref, o_ref = refs[1 + ng:4 + ng]
        rest = refs[4 + ng:]
        if pair:
            ra_hbm, xt_vm, ct_vm, loc, ring, s_send, s_recv = rest
            pos = _position()
            sib = _peer(pos, 1)
        else:
            xt_vm, ct_vm, loc = rest
        j = pl.program_id(0)

        @pl.when(j == 0)
        def _():
            if pair:
                c = pos[2]
                other = pl.ds(pl.multiple_of((1 - c) * Dh, Dh), Dh)
                mine = pl.ds(pl.multiple_of(c * Dh, Dh), Dh)
                cps = [pltpu.make_async_copy(xt_hbm.at[other, :], xt_vm.at[0:Dh, :], loc.at[0]),
                       pltpu.make_async_copy(xt_hbm.at[mine, :], xt_vm.at[Dh:D, :], loc.at[1]),
                       pltpu.make_async_copy(ct_hbm.at[other, :], ct_vm.at[0:Dh, :], loc.at[2]),
                       pltpu.make_async_copy(ct_hbm.at[mine, :], ct_vm.at[Dh:D, :], loc.at[3])]
            else:
                cps = [pltpu.make_async_copy(xt_hbm, xt_vm, loc.at[0]), pltpu.make_async_copy(ct_hbm, ct_vm, loc.at[1])]
            for cp in cps:
                cp.start()
            for cp in cps:
                cp.wait()

        def send(slot):
            cols = pl.ds(pl.multiple_of(j * tn, 128), tn)
            return pltpu.make_async_remote_copy(src_ref=ring.at[slot], dst_ref=ra_hbm.at[:, cols],
                                                send_sem=s_send.at[slot], recv_sem=s_recv,
                                                device_id=sib, device_id_type=MESH)

        for g in range(ng):
            @pl.when((j >= starts[g]) & (j < starts[g] + nblk[g]))
            def _(g=g):
                acc = _dot(xt_vm[...], group_refs[g][...])
                if g == 1:
                    acc += _dot(ct_vm[...], drc_ref[...])
                if not pair:
                    o_ref[...] = acc
                    return
                o_ref[...] = acc[Dh:, :]
                slot = lax.rem(j, DW_RING)

                @pl.when(j >= DW_RING)
                def _():
                    send(slot).wait_send()

                ring[slot] = acc[0:Dh, :]
                send(slot).start()

        if pair:
            @pl.when(j == nj - 1)
            def _():
                pltpu.make_async_remote_copy(src_ref=ra_hbm, dst_ref=ra_hbm, send_sem=s_send.at[0], recv_sem=s_recv,
                                             device_id=sib, device_id_type=MESH).wait_recv()
                for slot in range(DW_RING):
                    send(slot).wait_send()

    def group_spec(g, rows):
        return pl.BlockSpec((rows, tn), lambda j: (0, jnp.clip(j - starts[g], 0, nblk[g] - 1)))

    out_specs = [pl.BlockSpec((rows_out, tn), lambda j: (0, j))]
    out_shape = [jax.ShapeDtypeStruct((rows_out, nj * tn), F32)]
    scratch = [pltpu.VMEM((D, L), BF16), pltpu.VMEM((D, Lc), BF16), pltpu.SemaphoreType.DMA((4,))]
    if pair:
        out_specs.append(ANY)
        out_shape.append(jax.ShapeDtypeStruct((Dh, nj * tn), F32))
        scratch += [pltpu.VMEM((DW_RING, Dh, tn), F32), pltpu.SemaphoreType.DMA((DW_RING,)), pltpu.SemaphoreType.DMA]
    return tuple(pl.pallas_call(
        body, name="dw_in", grid=(nj,),
        in_specs=[ANY] + [group_spec(g, L) for g in range(ng)] + [ANY, group_spec(1, Lc)],
        out_specs=out_specs, out_shape=out_shape, scratch_shapes=scratch,
        compiler_params=_cparams(("arbitrary",), VMEM_LIMIT),
    )(xmt, *groups, cmt, dr_c))


def _local_step(x, ctx, tgt, mod_x, mod_c, norm_w, conv_w8, conv_b, lg, gn_w, fw, project, csidx=None, w_in_s=None):
    L, D = x.shape
    sh_x, sc_x, g_x = mod_x
    sh_c, sc_c = mod_c
    c2, s2 = _rope_tables(L)
    tab = _decay_tables(lg, D // DV)

    xm, xmt, *w_bf = _norm_mod(x, norm_w, sc_x, sh_x, "norm_mod_x", w_in_s)
    cm, cmt = _norm_mod(ctx, norm_w, sc_c, sh_c, "norm_mod_ctx")
    reduce = csidx is not None
    p, qr, kr, w_in, w3 = project(xm, c2, s2, *w_bf)
    pc, pqk_c = _in_proj(cm, w_in, "in_proj_ctx", QK_BLOCK, 2)
    s0 = _ctx_states(pc, pqk_c, lg, D)
    sf_prev, sb_prev = _ret_states(kr, p, s0, tab, D)
    o, yb = _ret_out(qr, kr, p, sf_prev, sb_prev, gn_w, tab, D)
    dx1, dya, do, dzb, dgab, dw3, st_mid = _mid(p, yb, o, x, tgt, w3, g_x, fw, conv_w8, conv_b, gn_w, D)
    dw3_5 = dw3.reshape(3, N_SHARD, 2, D // 8, D)
    dconv, st_conv, *ra_3 = _conv_bwd(dya, p, conv_w8, conv_b, D, _pair_exchange_w3(dw3_5) if reduce else None)
    dsf, dsb, ds0 = _ret_bwd_states(qr, do, tab, D)
    cs_3 = _sum_pair_w3(csidx[0:1], dw3_5, ra_3[0]) if reduce else None
    dret, st_lg, *rb_3 = _ret_bwd_main(qr, kr, p, do, sf_prev, sb_prev, dsf, dsb, c2, s2, tab, D,
                                       _chips_exchange_w3(cs_3) if reduce else None)
    g_3 = _sum_chips_w3(csidx, cs_3, rb_3[0]) if reduce else dw3
    dret_c, st_lgc = _ctx_bwd(pc, pqk_c, ds0, lg, D)
    groups = (dconv, dret, dzb, dgab)
    _, st_c = _dxm((dret_c,), 4, w_in, ctx, norm_w, sc_c, None, "dxm_ctx")
    return groups, dret_c, xmt, cmt, dx1, sc_x, w_in, g_3, (st_mid, st_conv, st_lg, st_lgc, st_c)


CHIP_FLIPS = (4, 2, 6)
ANY = pl.BlockSpec(memory_space=pl.ANY)
VMEM_FULL = pl.BlockSpec(memory_space=pltpu.VMEM)


def _position():
    return lax.axis_index("x"), lax.axis_index("y"), lax.axis_index("c")


def _peer(pos, k):
    x, y, c = pos
    return (1 - x if k & 4 else x, 1 - y if k & 2 else y, 1 - c if k & 1 else c)


def _dev_id(pos):
    return 4 * pos[0] + 2 * pos[1] + pos[2]


def _shard_of(pos):
    return 2 * pos[0] + pos[1]


def _remote(src, dst, send_sems, recv_sems, idx, to):
    return pltpu.make_async_remote_copy(src_ref=src, dst_ref=dst, send_sem=send_sems.at[idx],
                                        recv_sem=recv_sems.at[idx], device_id=to, device_id_type=MESH)


def _dot_f32(a, b):
    return jnp.dot(a, b, precision=lax.Precision.HIGHEST, preferred_element_type=F32)


def _silu(x):
    return x * _sigmoid(x)


def _fwd_small(c, c_ctx, ada_w, ada_b, conv_w):
    D = c.shape[1]
    Wm = ada_w.shape[1]
    Dq = conv_w.shape[1]

    def body(c_ref, cc_ref, aw_ref, ab_ref, cw_ref, act_ref, shx_ref, scx_ref, gx_ref, shc_ref, scc_ref, cwf_ref,
             mod_ref, cbuf, pmine, pbuf, wbuf, s_c, r_c, s_p, r_p, s_w, r_w):
        pos = _position()
        me, s = _dev_id(pos), _shard_of(pos)
        cbuf[me] = c_ref[...]
        wbuf[s] = cw_ref[...]
        sends = [_remote(c_ref, cbuf.at[me], s_c, r_c, k - 1, _peer(pos, k)) for k in range(1, 8)]
        sends += [_remote(cw_ref, wbuf.at[s], s_w, r_w, j, _peer(pos, k)) for j, k in enumerate(CHIP_FLIPS)]
        for cp in sends:
            cp.start()
        for k in range(1, 8):
            _remote(c_ref, cbuf.at[_dev_id(_peer(pos, k))], s_c, r_c, k - 1, _peer(pos, k)).wait_recv()
        for d in range(N_DEV):
            act_ref[d:d + 1, :] = _silu(cbuf[d])
        act_ref[8:9, :] = _silu(cc_ref[...])
        act_ref[9:16, :] = jnp.zeros((7, D), F32)
        part = _dot_f32(act_ref[...], aw_ref[...])
        pmine[...] = part
        pbuf[s] = part
        psend = [_remote(pmine, pbuf.at[s], s_p, r_p, j, _peer(pos, k)) for j, k in enumerate(CHIP_FLIPS)]
        for cp in psend:
            cp.start()
        for j, k in enumerate(CHIP_FLIPS):
            t = _shard_of(_peer(pos, k))
            _remote(pmine, pbuf.at[t], s_p, r_p, j, _peer(pos, k)).wait_recv()
            _remote(cw_ref, wbuf.at[t], s_w, r_w, j, _peer(pos, k)).wait_recv()
        cwf_ref[...] = jnp.zeros_like(cwf_ref)
        for t in range(N_SHARD):
            mod_ref[:, t * Wm:(t + 1) * Wm] = pbuf[t] + ab_ref[:, t * Wm:(t + 1) * Wm]
            cwf_ref[0:3, t * Dq:(t + 1) * Dq] = wbuf[t]
        for r, o_ref in enumerate((shx_ref, scx_ref, gx_ref)):
            o_ref[...] = mod_ref[pl.ds(me, 1), r * D:(r + 1) * D]
        for r, o_ref in enumerate((shc_ref, scc_ref)):
            o_ref[...] = mod_ref[8:9, r * D:(r + 1) * D]
        for cp in sends + psend:
            cp.wait_send()

    row = jax.ShapeDtypeStruct((1, D), F32)
    res = pl.pallas_call(
        body, name="fwd_small",
        in_specs=[VMEM_FULL] * 5, out_specs=[VMEM_FULL] * 7,
        out_shape=[jax.ShapeDtypeStruct((16, D), F32)] + [row] * 5 + [jax.ShapeDtypeStruct((8, D), F32)],
        scratch_shapes=[pltpu.VMEM((16, 3 * D), F32), pltpu.VMEM((N_DEV, 1, D), F32), pltpu.VMEM((16, Wm), F32),
                        pltpu.VMEM((N_SHARD, 16, Wm), F32), pltpu.VMEM((N_SHARD, 3, Dq), F32),
                        pltpu.SemaphoreType.DMA((7,)), pltpu.SemaphoreType.DMA((7,)),
                        pltpu.SemaphoreType.DMA((3,)), pltpu.SemaphoreType.DMA((3,)),
                        pltpu.SemaphoreType.DMA((3,)), pltpu.SemaphoreType.DMA((3,))],
        compiler_params=_cparams(None, VMEM_LIMIT),
    )(c, c_ctx, ada_w, ada_b, conv_w)
    return res[0], tuple(res[1:4]), tuple(res[4:6]), res[6]


AG_CHUNKS = 3


def _ag_in_proj(xm, w_in_s, w3_s, c2, s2):
    L, D = xm.shape
    Wc = w_in_s.shape[1]
    Wq = Wc // AG_CHUNKS
    Dh = D // 2
    Do = w3_s[0].shape[1]
    TM = min(1024, L // 4)
    NT = L // TM
    NQ = AG_CHUNKS
    order = [(q, j) for q in range(NQ) for j in (0, 1)] + [(q, 2) for q in range(NQ)]

    def body(xm_ref, wi_hbm, wa_ref, wb_ref, wo_ref, c_ref, s_ref, p_hbm, qr_hbm, kr_hbm, fi_hbm, f3_hbm,
             w_vm, s3, stage, qk_stage, ici_s, ici_r, d2d_s, d2d_r, w3_s_, w3_r_, fw_s, fw_r,
             loc, out_sem, qk_sem):
        pos = _position()
        c = pos[2]
        s = _shard_of(pos)
        sib = _peer(pos, 1)
        mine = pl.ds(pl.multiple_of(c * Dh, Dh), Dh)
        other = pl.ds(pl.multiple_of((1 - c) * Dh, Dh), Dh)

        def abs_col(t, q):
            return pl.ds(pl.multiple_of(t * Wc + q * Wq, 128), Wq)

        own = [pltpu.make_async_copy(wi_hbm.at[:, q * Wq:(q + 1) * Wq], w_vm.at[0, q], loc.at[2 + 4 * NQ + q])
               for q in range(NQ)]
        for cp in own:
            cp.start()
        for cp in own:
            cp.wait()
        sends = [_remote(w_vm.at[0, q, mine, :], w_vm.at[1 + j, q, mine, :], ici_s, ici_r, q * 3 + j,
                         _peer(pos, CHIP_FLIPS[j])) for q, j in order if j < 2]
        for cp in sends:
            cp.start()
        for a, w_ref in enumerate((wa_ref, wb_ref, wo_ref)):
            s3[a] = w_ref[...].astype(BF16)
        w3_sends = [_remote(s3.at[:, c], f3_hbm.at[:, s, c], w3_s_, w3_r_, j, _peer(pos, k))
                    for j, k in enumerate(CHIP_FLIPS)]
        local = [pltpu.make_async_copy(s3, f3_hbm.at[:, s], loc.at[1])]
        local += [pltpu.make_async_copy(w_vm.at[0, q], fi_hbm.at[:, abs_col(s, q)], loc.at[2 + q]) for q in range(NQ)]
        for cp in local:
            cp.start()

        def out_copy(slot, rows, cols):
            return pltpu.make_async_copy(stage.at[slot], p_hbm.at[rows, cols], out_sem.at[slot])

        def block(r, q, t, first):
            cols = abs_col(t, q)

            def row_tile(rt, carry):
                rows = pl.ds(pl.multiple_of(rt * TM, TM), TM)
                acc = _dot(xm_ref[rows, :], w_vm[r, q])
                slot = lax.rem(rt, 2)

                @pl.when(rt >= 2 if first else rt >= 0)
                def _():
                    out_copy(slot, rows, cols).wait()

                stage[slot] = acc.astype(BF16)
                out_copy(slot, rows, cols).start()

                def rotary(lo, scale, dst_hbm):
                    c, s = c_ref[rows, :], s_ref[rows, :]
                    for pr in range(Dh // 128):
                        tq = acc[:, lo + pr * 128:lo + (pr + 1) * 128] * scale
                        qk_stage[:, pr * 128:(pr + 1) * 128] = (tq * c + _swap_halves(tq) * s).astype(BF16)
                    cp = pltpu.make_async_copy(qk_stage, dst_hbm.at[rows, :], qk_sem)
                    cp.start()
                    cp.wait()

                if q == NQ - 1:
                    @pl.when(t == 1)
                    def _():
                        rotary(Wq - Dh, 1.0, qr_hbm)
                if q == 0:
                    @pl.when(t == 2)
                    def _():
                        rotary(0, K_SCALE, kr_hbm)
                return carry

            lax.fori_loop(0, NT, row_tile, 0)

        passed = []

        def hand_on(q, j):
            half = w_vm.at[1 + j, q, mine, :]
            if j == 2:
                _remote(half, half, fw_s, fw_r, q, sib).wait_recv()
            else:
                _remote(half, half, ici_s, ici_r, q * 3 + j, sib).wait_recv()

                @pl.when(c == (0 if j == q % 2 else 1))
                def _():
                    _remote(half, w_vm.at[3, q, mine, :], fw_s, fw_r, q, _peer(pos, CHIP_FLIPS[1 - j])).start()
            fwd = _remote(half, half, d2d_s, d2d_r, q * 3 + j, sib)
            fwd.start()
            passed.append(fwd)

        for q in range(NQ):
            if q == NQ - 1:
                hand_on(*order[0])
            block(0, q, s, q == 0)
        for n, (q, j) in enumerate(order):
            r, idx = 1 + j, q * 3 + j
            t = _shard_of(_peer(pos, CHIP_FLIPS[j]))
            if n + 1 < len(order):
                hand_on(*order[n + 1])
            if n + 1 == 2 * NQ - 1:
                for cp in w3_sends:
                    cp.start()
            _remote(w_vm.at[r, q, other, :], w_vm.at[r, q, other, :], d2d_s, d2d_r, idx, sib).wait_recv()
            block(r, q, t, False)
            cp = pltpu.make_async_copy(w_vm.at[r, q], fi_hbm.at[:, abs_col(t, q)], loc.at[2 + NQ + idx])
            cp.start()
            local.append(cp)
        for j, k in enumerate(CHIP_FLIPS):
            t = _shard_of(_peer(pos, k))
            _remote(s3.at[:, c], f3_hbm.at[:, t, c], w3_s_, w3_r_, j, sib).wait_recv()
            fwd = _remote(f3_hbm.at[:, t, c], f3_hbm.at[:, t, c], w3_s_, w3_r_, 3 + j, sib)
            fwd.start()
            passed.append(fwd)
        for j, k in enumerate(CHIP_FLIPS):
            t = _shard_of(_peer(pos, k))
            _remote(s3.at[:, c], f3_hbm.at[:, t, 1 - c], w3_s_, w3_r_, 3 + j, sib).wait_recv()
        for cp in sends + w3_sends + passed:
            cp.wait_send()
        for q in range(NQ):
            _remote(w_vm.at[1, q, mine, :], w_vm.at[3, q, mine, :], fw_s, fw_r, q, sib).wait_send()
        for cp in local:
            cp.wait()
        for slot in range(2):
            out_copy(slot, pl.ds(0, TM), abs_col(s, 0)).wait()

    n_loc = 2 + 5 * NQ
    return pl.pallas_call(
        body, name="ag_in_proj",
        in_specs=[VMEM_FULL, ANY, VMEM_FULL, VMEM_FULL, VMEM_FULL, VMEM_FULL, VMEM_FULL], out_specs=[ANY] * 5,
        out_shape=[jax.ShapeDtypeStruct((L, N_SHARD * Wc), BF16),
                   jax.ShapeDtypeStruct((L, Dh), BF16), jax.ShapeDtypeStruct((L, Dh), BF16),
                   jax.ShapeDtypeStruct((D, N_SHARD * Wc), BF16), jax.ShapeDtypeStruct((3, N_SHARD, 2, Do, D), BF16)],
        scratch_shapes=[pltpu.VMEM((N_SHARD, NQ, D, Wq), BF16), pltpu.VMEM((3, 2, Do, D), BF16), pltpu.VMEM((2, TM, Wq), BF16), pltpu.VMEM((TM, Dh), BF16),
                        pltpu.SemaphoreType.DMA((3 * NQ,)), pltpu.SemaphoreType.DMA((3 * NQ,)),
                        pltpu.SemaphoreType.DMA((3 * NQ,)), pltpu.SemaphoreType.DMA((3 * NQ,)),
                        pltpu.SemaphoreType.DMA((6,)), pltpu.SemaphoreType.DMA((6,)),
                        pltpu.SemaphoreType.DMA((NQ,)), pltpu.SemaphoreType.DMA((NQ,)),
                        pltpu.SemaphoreType.DMA((n_loc,)), pltpu.SemaphoreType.DMA((2,)), pltpu.SemaphoreType.DMA],
        compiler_params=_cparams(None, VMEM_LIMIT),
    )(xm, w_in_s, *w3_s, c2, s2)


def _pair_exchange_w3(dw3):
    _, _, _, Do, D = dw3.shape

    def build(ins, outs, send, recv):
        pos = _position()
        return [_remote(ins[0].at[:, :, 1 - pos[2]], outs[0], send, recv, 0, _peer(pos, 1))]

    return _Exchange((dw3,), (jax.ShapeDtypeStruct((3, N_SHARD, Do, D), F32),), 1, build)


def _sum_pair_in(dw_mine, ri):
    Dh, Wf = dw_mine.shape
    Wc = Wf // N_SHARD
    tr = min(256, Dh)

    def body(a_ref, b_ref, o_ref):
        o_ref[...] = (a_ref[...] + b_ref[...]).astype(BF16)

    return pl.pallas_call(
        body, name="sum_pair_in", grid=(Dh // tr, N_SHARD),
        in_specs=[pl.BlockSpec((tr, Wc), lambda i, t: (i, t)), pl.BlockSpec((tr, Wc), lambda i, t: (i, t))],
        out_specs=pl.BlockSpec((None, tr, Wc), lambda i, t: (t, i, 0)),
        out_shape=jax.ShapeDtypeStruct((N_SHARD, Dh, Wc), BF16),
        compiler_params=_cparams(("parallel", "parallel")),
    )(dw_mine, ri)


def _sum_pair_w3(cidx, dw3, r3):
    _, _, _, Do, D = dw3.shape

    def body(c_ref, a_ref, b_ref, o_ref):
        o_ref[...] = (a_ref[...] + b_ref[...]).astype(BF16)

    return pl.pallas_call(
        body, name="sum_pair_w3",
        grid_spec=pltpu.PrefetchScalarGridSpec(
            num_scalar_prefetch=1, grid=(3,),
            in_specs=[pl.BlockSpec((None, N_SHARD, None, Do, D), lambda a, c: (a, 0, c[0], 0, 0)),
                      pl.BlockSpec((None, N_SHARD, Do, D), lambda a, c: (a, 0, 0, 0))],
            out_specs=pl.BlockSpec((None, N_SHARD, Do, D), lambda a, c: (a, 0, 0, 0))),
        out_shape=jax.ShapeDtypeStruct((3, N_SHARD, Do, D), BF16),
        compiler_params=_cparams(("parallel",)),
    )(cidx, dw3, r3)


def _chips_exchange_in(cs_in):
    _, Dh, Wc = cs_in.shape

    def build(ins, outs, send, recv):
        pos = _position()
        return [_remote(ins[0].at[_shard_of(_peer(pos, k))], outs[0].at[j], send, recv, j, _peer(pos, k))
                for j, k in enumerate(CHIP_FLIPS)]

    return _Exchange((cs_in,), (jax.ShapeDtypeStruct((3, Dh, Wc), BF16),), 3, build)


def _chips_exchange_w3(cs_3):
    _, _, Do, D = cs_3.shape

    def build(ins, outs, send, recv):
        pos = _position()
        return [_remote(ins[0].at[:, _shard_of(_peer(pos, k))], outs[0].at[j], send, recv, j, _peer(pos, k))
                for j, k in enumerate(CHIP_FLIPS)]

    return _Exchange((cs_3,), (jax.ShapeDtypeStruct((3, 3, Do, D), BF16),), 3, build)


def _sum_chips_in(csidx, cs_in, rb_in):
    _, Dh, Wc = cs_in.shape
    tr = min(256, Dh)

    def body(s_ref, a_ref, b_ref, o_ref):
        acc = a_ref[...].astype(F32)
        for j in range(3):
            acc = acc + b_ref[j].astype(F32)
        o_ref[...] = acc

    return pl.pallas_call(
        body, name="sum_chips_in",
        grid_spec=pltpu.PrefetchScalarGridSpec(
            num_scalar_prefetch=1, grid=(Dh // tr,),
            in_specs=[pl.BlockSpec((None, tr, Wc), lambda i, s: (s[1], i, 0)),
                      pl.BlockSpec((3, tr, Wc), lambda i, s: (0, i, 0))],
            out_specs=pl.BlockSpec((None, tr, Wc), lambda i, s: (s[0], i, 0))),
        out_shape=jax.ShapeDtypeStruct((2, Dh, Wc), F32),
        compiler_params=_cparams(("parallel",)),
    )(csidx, cs_in, rb_in)


def _sum_chips_w3(csidx, cs_3, rb_3):
    _, _, Do, D = cs_3.shape

    def body(s_ref, a_ref, b_ref, o_ref):
        acc = a_ref[...].astype(F32)
        for j in range(3):
            acc = acc + b_ref[j].astype(F32)
        o_ref[...] = acc

    return pl.pallas_call(
        body, name="sum_chips_w3",
        grid_spec=pltpu.PrefetchScalarGridSpec(
            num_scalar_prefetch=1, grid=(3,),
            in_specs=[pl.BlockSpec((None, None, Do, D), lambda a, s: (a, s[1], 0, 0)),
                      pl.BlockSpec((3, None, Do, D), lambda a, s: (0, a, 0, 0))],
            out_specs=pl.BlockSpec((None, None, Do, D), lambda a, s: (a, s[0], 0, 0))),
        out_shape=jax.ShapeDtypeStruct((3, 2, Do, D), F32),
        compiler_params=_cparams(("parallel",)),
    )(csidx, cs_3, rb_3)


def _adam_math(w, g, m, v):
    m = ADAM_B1 * m + (1.0 - ADAM_B1) * g
    v = ADAM_B2 * v + (1.0 - ADAM_B2) * (g * g)
    m_hat = m / (1.0 - ADAM_B1 ** ADAM_STEP)
    v_hat = v / (1.0 - ADAM_B2 ** ADAM_STEP)
    delta = -ADAM_LR * (m_hat / (jnp.sqrt(v_hat) + ADAM_EPS) + ADAM_WD * w)
    return delta, m, v


def _adamw(w, g, m, v):
    R, C = w.shape
    tr = min(128, R)

    def body(w_ref, g_ref, m_ref, v_ref, d_ref, nm_ref, nv_ref):
        d_ref[...], nm_ref[...], nv_ref[...] = _adam_math(w_ref[...], g_ref[...], m_ref[...], v_ref[...])

    blk = pl.BlockSpec((tr, C), lambda i: (i, 0))
    return pl.pallas_call(
        body, name="adamw_w_in", grid=(R // tr,), in_specs=[blk] * 4, out_specs=[blk] * 3,
        out_shape=[jax.ShapeDtypeStruct((R, C), F32)] * 3,
        compiler_params=_cparams(("parallel",), VMEM_LIMIT),
    )(w, g, m, v)


SMALL_ROWS = ("c_ctx", "norm_w", "conv_b", "gn_w", "final_norm_w")


def _bwd_small(stats, ada_w, Dq, gh_in, gh_3):
    D = stats[0].shape[1]
    Wm = ada_w.shape[1]

    def body(stx, stm, stc, stv, stl, stlc, aw_ref, gi_in, g3_in, tot_ref, dm_sh, gcw, da_ref, gi_ref, g3_ref,
             vec_ref, vbuf, dm, amine, abuf, s_v, r_v, s_a, r_a, s_g, r_g):
        pos = _position()
        me, s = _dev_id(pos), _shard_of(pos)
        c, sib = pos[2], _peer(pos, 1)
        halves = [_remote(gi_in.at[c], gi_ref.at[c], s_g, r_g, 0, sib),
                  _remote(g3_in.at[:, c], g3_ref.at[:, c], s_g, r_g, 1, sib)]
        for cp in halves:
            cp.start()
        vec_ref[...] = jnp.zeros_like(vec_ref)
        vec_ref[0:2, :] = stx[0:2, :]
        vec_ref[2:3, :] = stm[1:2, :]
        vec_ref[3:5, :] = stc[0:2, :]
        vec_ref[5:6, :] = stx[2:3, :] + stc[2:3, :]
        vec_ref[6:7, :] = stv[3:4, :]
        vec_ref[7:8, :] = stm[3:4, :]
        vec_ref[8:9, :] = stm[0:1, :]
        vec_ref[9:12, :] = stv[0:3, :]
        vec_ref[12:14, 0:128] = stl[0:2, :] + stlc[0:2, :]
        vec_ref[14:15, :] = stm[2:3, :]
        vbuf[me] = vec_ref[...]
        sends = [_remote(vec_ref, vbuf.at[me], s_v, r_v, k - 1, _peer(pos, k)) for k in range(1, 8)]
        for cp in sends:
            cp.start()
        for k in range(1, 8):
            _remote(vec_ref, vbuf.at[_dev_id(_peer(pos, k))], s_v, r_v, k - 1, _peer(pos, k)).wait_recv()
        tot = vbuf[0]
        for d in range(1, N_DEV):
            tot = tot + vbuf[d]
        dm[...] = jnp.zeros_like(dm)
        for d in range(N_DEV):
            for r in range(3):
                dm[d:d + 1, r * D:(r + 1) * D] = vbuf[d, r:r + 1, :]
        dm[8:9, 0:D] = tot[3:4, :]
        dm[8:9, D:2 * D] = tot[4:5, :]
        for t in range(N_SHARD):
            @pl.when(s == t)
            def _(t=t):
                dm_sh[...] = dm[:, t * Wm:(t + 1) * Wm]
                gcw[...] = tot[9:12, t * Dq:(t + 1) * Dq]
        tot_ref[...] = tot
        part = lax.dot_general(dm_sh[8:16, :], aw_ref[...], (((1,), (1,)), ((), ())),
                               precision=lax.Precision.HIGHEST, preferred_element_type=F32)
        amine[...] = part
        abuf[s] = part
        asend = [_remote(amine, abuf.at[s], s_a, r_a, j, _peer(pos, k)) for j, k in enumerate(CHIP_FLIPS)]
        for cp in asend:
            cp.start()
        for j, k in enumerate(CHIP_FLIPS):
            _remote(amine, abuf.at[_shard_of(_peer(pos, k))], s_a, r_a, j, _peer(pos, k)).wait_recv()
        da = abuf[0]
        for t in range(1, N_SHARD):
            da = da + abuf[t]
        da_ref[...] = da
        _remote(gi_in.at[1 - c], gi_ref.at[1 - c], s_g, r_g, 0, sib).wait_recv()
        _remote(g3_in.at[:, 1 - c], g3_ref.at[:, 1 - c], s_g, r_g, 1, sib).wait_recv()
        for cp in sends + asend + halves:
            cp.wait_send()

    row = lambda *shape: jax.ShapeDtypeStruct(shape, F32)
    return pl.pallas_call(
        body, name="bwd_small",
        in_specs=[VMEM_FULL] * 7 + [ANY, ANY], out_specs=[VMEM_FULL] * 4 + [ANY, ANY],
        input_output_aliases={7: 4, 8: 5},
        out_shape=[row(16, D), row(16, Wm), row(3, Dq), row(8, D), row(*gh_in.shape), row(*gh_3.shape)],
        scratch_shapes=[pltpu.VMEM((16, D), F32), pltpu.VMEM((N_DEV, 16, D), F32), pltpu.VMEM((16, 3 * D), F32),
                        pltpu.VMEM((8, D), F32), pltpu.VMEM((N_SHARD, 8, D), F32),
                        pltpu.SemaphoreType.DMA((7,)), pltpu.SemaphoreType.DMA((7,)),
                        pltpu.SemaphoreType.DMA((3,)), pltpu.SemaphoreType.DMA((3,)),
                        pltpu.SemaphoreType.DMA((2,)), pltpu.SemaphoreType.DMA((2,))],
        compiler_params=_cparams(None, VMEM_LIMIT),
    )(*stats, ada_w, gh_in, gh_3)


ADAM_SLAB = 16


def _small_update(tot, dm_sh, gcw, da, act, g3, rows, ab, cw, dl, aw, w3):
    D = act.shape[1]
    Wm = dm_sh.shape[1]
    Dq = gcw.shape[1]
    H = dl[0].shape[1]
    params = tuple(rows) + (ab, cw, dl)
    shards = (aw,) + tuple(w3)
    n_p, n_s = len(params), len(shards)
    whole = (slice(None), slice(None))

    def body(tot_ref, dm_ref, gcw_ref, da_ref, act_ref, g3_ref, *refs):
        wmv = [refs[3 * k:3 * k + 3] for k in range(n_p + n_s)]
        refs = refs[3 * (n_p + n_s):]
        gaw_ref, loss_ref = refs[0:2]
        outs, shard_outs = refs[2:2 + 4 * n_p], refs[2 + 4 * n_p:]
        o_q = [outs[n_p * q:n_p * (q + 1)] for q in range(4)]
        tot = tot_ref[...]
        gaw_ref[...] = lax.dot_general(act_ref[...], dm_ref[...], (((0,), (0,)), ((), ())),
                                       precision=lax.Precision.HIGHEST, preferred_element_type=F32)
        loss_ref[...] = (0.5 / D) * _sum_all(tot[14:15, :])
        cc = wmv[0][0][...]
        sg = _sigmoid(cc)
        g_cctx = da_ref[0:1, :] * (sg * (1.0 + cc * (1.0 - sg)))

        def emit(k, g, at=whole):
            w_ref, m_ref, v_ref = wmv[k]
            for q, val in enumerate((g,) + _adam_math(w_ref[at], g, m_ref[at], v_ref[at])):
                o_q[q][k][at] = val

        for k, g in enumerate([g_cctx, tot[5:6, :], tot[6:7, :], tot[7:8, :], tot[8:9, :]]):
            emit(k, g)
        for r, g in enumerate([tot[0:1, :] + tot[3:4, :], tot[1:2, :] + tot[4:5, :], tot[2:3, :]]):
            emit(n_p - 3, g, (slice(0, 1), slice(r * D, (r + 1) * D)))
        emit(n_p - 2, gcw_ref[...])
        emit(n_p - 1, tot[12:14, 0:H] * _sigmoid(-wmv[n_p - 1][0][...]))

        def shard_step(k, g_ref):
            w_ref, m_ref, v_ref = wmv[n_p + k]

            def slab(i, carry):
                sl = pl.ds(pl.multiple_of(i * ADAM_SLAB, ADAM_SLAB), ADAM_SLAB)
                res = _adam_math(w_ref[sl, :], g_ref[sl, :], m_ref[sl, :], v_ref[sl, :])
                for o_ref, val in zip(shard_outs[3 * k:3 * k + 3], res):
                    o_ref[sl, :] = val
                return carry

            lax.fori_loop(0, w_ref.shape[0] // ADAM_SLAB, slab, 0)

        shard_step(0, gaw_ref)
        for a in range(n_s - 1):
            shard_step(1 + a, g3_ref.at[a])

    row = lambda *shape: jax.ShapeDtypeStruct(shape, F32)
    per_q = [row(1, D)] * len(rows) + [row(1, 3 * D), row(3, Dq), row(2, H)]
    per_shard = [row(*s[0].shape) for s in shards for _ in range(3)]
    res = pl.pallas_call(
        body, name="small_update",
        in_specs=[VMEM_FULL] * (6 + 3 * (n_p + n_s)), out_specs=[VMEM_FULL] * (2 + 4 * n_p + 3 * n_s),
        out_shape=[row(D, Wm), row(1, 1)] + per_q * 4 + per_shard,
        compiler_params=_cparams(None, VMEM_LIMIT),
    )(tot, dm_sh, gcw, da, act, g3, *[a for p in params + shards for a in p])
    k = 2 + 4 * n_p
    return (res[0], res[1], [res[2 + n_p * q:2 + n_p * (q + 1)] for q in range(4)],
            [res[k + 3 * s:k + 3 * s + 3] for s in range(n_s)])


def kernel(x, c, ctx, c_ctx, norm_w, ada_w, ada_b, w_in, conv_w, conv_b, decay_logit, gn_w, w_a, w_b, w_out, final_norm_w, loss_target, m_c_ctx, m_norm_w, m_ada_w, m_ada_b, m_w_in, m_conv_w, m_conv_b, m_decay_logit, m_gn_w, m_w_a, m_w_b, m_w_out, m_final_norm_w, v_c_ctx, v_norm_w, v_ada_w, v_ada_b, v_w_in, v_conv_w, v_conv_b, v_decay_logit, v_gn_w, v_w_a, v_w_b, v_w_out, v_final_norm_w):
    L, D = x.shape[1], x.shape[2]
    Wc = w_in.shape[2]
    Do = D // 8
    pos = _position()
    cidx = jnp.reshape(pos[2], (1,)).astype(jnp.int32)
    sidx = jnp.reshape(_shard_of(pos), (1,)).astype(jnp.int32)

    act, mod_x, mod_c, conv_w8 = _fwd_small(c, c_ctx[None], ada_w[0], ada_b, conv_w[0])
    lg = jax.nn.log_sigmoid(decay_logit[0])

    w3_s = tuple(w[0].reshape(2, Do, D) for w in (w_a, w_b, w_out))

    def project(xm, c2, s2, w_in_bf):
        p, qr, kr, w_in_full, w3_full = _ag_in_proj(xm, w_in_bf, w3_s, c2, s2)
        return p, qr, kr, w_in_full, w3_full.reshape(3, D, D)

    csidx = jnp.concatenate([cidx, sidx])
    groups, dret_c, xmt, cmt, dx1, sc_x, w_in_full, gh_3, sts = _local_step(
        x[0], ctx[0], loss_target[0], mod_x, mod_c, norm_w, conv_w8, conv_b, lg, gn_w, final_norm_w[None],
        project, csidx, w_in[0])
    st_mid, st_conv, st_lg, st_lgc, st_c = sts

    dw_mine, ra_in = _dw_in(xmt, groups, cmt, dret_c, D, True)
    cs_in = _sum_pair_in(dw_mine, ra_in)
    grad_x, st_x, rb_in = _dxm(groups, 0, w_in_full, x[0], norm_w, sc_x, dx1, "dxm_x", _chips_exchange_in(cs_in))
    gh_in = _sum_chips_in(csidx, cs_in, rb_in)

    tot, dm_sh, gcw, da, g_in, g_3 = _bwd_small((st_x, st_mid, st_c, st_conv, st_lg, st_lgc), ada_w[0],
                                                conv_w.shape[2], gh_in, gh_3)
    g_w_in = g_in.reshape(D, Wc)
    g_3 = g_3.reshape(3, D // 4, D)
    rows = ((c_ctx[None], m_c_ctx[None], v_c_ctx[None]), (norm_w, m_norm_w, v_norm_w), (conv_b, m_conv_b, v_conv_b),
            (gn_w, m_gn_w, v_gn_w), (final_norm_w[None], m_final_norm_w[None], v_final_norm_w[None]))
    w3 = ((w_a[0], m_w_a[0], v_w_a[0]), (w_b[0], m_w_b[0], v_w_b[0]), (w_out[0], m_w_out[0], v_w_out[0]))
    g_ada_w, loss, small, (upd_ada, upd_a, upd_b, upd_o) = _small_update(
        tot, dm_sh, gcw, da, act, g_3, rows, (ada_b, m_ada_b, v_ada_b), (conv_w[0], m_conv_w[0], v_conv_w[0]),
        (decay_logit[0], m_decay_logit[0], v_decay_logit[0]), (ada_w[0], m_ada_w[0], v_ada_w[0]), w3)

    upd_in = _adamw(w_in[0], g_w_in, m_w_in[0], v_w_in[0])

    def leaves(q):
        big = lambda g, upd: (g if q == 0 else upd[q - 1])[None]
        r_cctx, r_norm, r_convb, r_gn, r_fnorm, r_ab, r_cw, r_dl = small[q]
        return [r_cctx.reshape(D), r_norm, big(g_ada_w, upd_ada), r_ab, big(g_w_in, upd_in),
                r_cw[None], r_convb, r_dl[None], r_gn,
                big(g_3[0], upd_a), big(g_3[1], upd_b), big(g_3[2], upd_o), r_fnorm.reshape(D)]

    return (loss.reshape(()), grad_x[None], *leaves(0), *leaves(1), *leaves(2), *leaves(3))
```

```python
from typing import Callable, NamedTuple

import jax
import jax.numpy as jnp
import numpy as np
from jax import lax
from jax.experimental import pallas as pl
from jax.experimental.pallas import tpu as pltpu

F32 = jnp.float32
BF16 = jnp.bfloat16
MESH = pl.DeviceIdType.MESH

CHUNK = 128
RET_CPB = 4
DV = 128
DK = 64
GRID_W = 64
ROPE_BASE = 10000.0
EPS = 1e-6
K_SCALE = DK ** -0.5
N_SHARD = 4
N_DEV = 8

ADAM_LR = 0.001
ADAM_B1 = 0.9
ADAM_B2 = 0.999
ADAM_EPS = 1e-08
ADAM_WD = 0.01
ADAM_STEP = 10

VMEM_LIMIT = 56 * 1024 * 1024


def _cparams(sem=None, vmem=None):
    kw = {}
    if sem is not None:
        kw["dimension_semantics"] = sem
    if vmem is not None:
        kw["vmem_limit_bytes"] = vmem
    return pltpu.CompilerParams(**kw)


def _dot(a, b):
    return jnp.dot(a, b, preferred_element_type=F32)


def _dot_nt(a, b):
    return lax.dot_general(a, b, (((1,), (1,)), ((), ())), preferred_element_type=F32)


def _dot_tn(a, b):
    return lax.dot_general(a, b, (((0,), (0,)), ((), ())), preferred_element_type=F32)


def _sigmoid(x):
    return 1.0 / (1.0 + jnp.exp(-x))


def _sum_all(x):
    return jnp.sum(jnp.sum(x, axis=1, keepdims=True), axis=0, keepdims=True)


def _swap_halves(t):
    n = t.shape[1]
    lane = lax.broadcasted_iota(jnp.int32, t.shape, 1)
    low = (lane & 32) == 0
    return jnp.where(low, pltpu.roll(t, n - 32, 1), pltpu.roll(t, 32, 1))


def _vec_spec(d):
    return pl.BlockSpec((1, d), lambda *a: (0, 0))


def _norm_mod(x, nw, sc, sh, name, also_bf16=None):
    L, D = x.shape
    tl = min(256, L)
    nt = L // tl

    def body(x_ref, nw_ref, sc_ref, sh_ref, *rest):
        xm_ref, xmt_ref = rest[-3:-1] if also_bf16 is not None else rest
        xv = x_ref[...]
        r = lax.rsqrt(jnp.mean(xv * xv, axis=-1, keepdims=True) + EPS)
        xm = (xv * r * nw_ref[...]) * (1.0 + sc_ref[...]) + sh_ref[...]
        xm_b = xm.astype(BF16)
        xm_ref[...] = xm_b
        xmt_ref[...] = xm_b.T
        if also_bf16 is not None:
            rest[-1][...] = rest[0][...].astype(BF16)

    in_specs = [pl.BlockSpec((tl, D), lambda i: (i, 0)), _vec_spec(D), _vec_spec(D), _vec_spec(D)]
    out_specs = [pl.BlockSpec((tl, D), lambda i: (i, 0)), pl.BlockSpec((D, tl), lambda i: (0, i))]
    out_shape = [jax.ShapeDtypeStruct((L, D), BF16), jax.ShapeDtypeStruct((D, L), BF16)]
    args = [x, nw, sc, sh]
    if also_bf16 is not None:
        R, C = also_bf16.shape
        slab = pl.BlockSpec((R // nt, C), lambda i: (i, 0))
        in_specs.append(slab)
        out_specs.append(slab)
        out_shape.append(jax.ShapeDtypeStruct((R, C), BF16))
        args.append(also_bf16)
    return pl.pallas_call(
        body, name=name, grid=(nt,), in_specs=in_specs, out_specs=out_specs, out_shape=out_shape,
        compiler_params=_cparams(("parallel",)),
    )(*args)


QK_BLOCK, V_BLOCK = 4, 5


def _in_proj(xm, w, name, first=0, count=None):
    M, D = xm.shape
    count = w.shape[1] // D if count is None else count
    tm = min(1024, M)

    def body(a_ref, b_ref, o_ref, qk_ref):
        acc = _dot(a_ref[...], b_ref[...])
        o_ref[...] = acc.astype(o_ref.dtype)

        @pl.when(pl.program_id(1) == QK_BLOCK - first)
        def _():
            qk_ref[...] = acc

    return pl.pallas_call(
        body, name=name, grid=(M // tm, count),
        in_specs=[pl.BlockSpec((tm, D), lambda i, j: (i, 0)), pl.BlockSpec((D, D), lambda i, j: (0, first + j))],
        out_specs=[pl.BlockSpec((tm, D), lambda i, j: (i, j)), pl.BlockSpec((tm, D), lambda i, j: (i, 0))],
        out_shape=[jax.ShapeDtypeStruct((M, count * D), BF16), jax.ShapeDtypeStruct((M, D), F32)],
        compiler_params=_cparams(("parallel", "arbitrary")),
    )(xm, w)


def _halo_specs(tl, L, D, col):
    hb = tl // 16
    last = L // 16 - 1
    prev = pl.BlockSpec((16, D), lambda i: (jnp.maximum(i * hb - 1, 0), col))
    nxt = pl.BlockSpec((16, D), lambda i: (jnp.minimum((i + 1) * hb, last), col))
    return prev, nxt


def _shift_rows(u, above, below):
    tl = u.shape[0]
    row = lax.broadcasted_iota(jnp.int32, u.shape, 0)
    dn = jnp.where(row == 0, above, pltpu.roll(u, 1, 0))
    up = jnp.where(row == tl - 1, below, pltpu.roll(u, tl - 1, 0))
    return dn, up


def _rope_tables(L):
    pos = np.arange(L)
    row = (pos // GRID_W).astype(np.float32)
    col = (pos % GRID_W).astype(np.float32)
    nf = DK // 4
    inv = np.float32(ROPE_BASE) ** (-np.arange(nf, dtype=np.float32) / np.float32(nf))
    ang = np.concatenate([row[:, None] * inv, col[:, None] * inv], axis=-1).astype(np.float32)
    cos, sin = np.cos(ang), np.sin(ang)
    return (jnp.asarray(np.concatenate([cos, cos, cos, cos], axis=-1), F32),
            jnp.asarray(np.concatenate([-sin, sin, -sin, sin], axis=-1), F32))


def _smem_spec():
    return pl.BlockSpec(memory_space=pltpu.SMEM)


def _pair_select(e0, e1):
    row = lax.broadcasted_iota(jnp.int32, e0.shape, 0)
    return jnp.where(row < DK, e0, e1)


def _head_lane_mask(shape, e):
    lane = lax.broadcasted_iota(jnp.int32, shape, 1)
    return (lane < DK) if e == 0 else (lane >= DK)


def _ctx_states(pc, pqk_c, lg, D):
    Lc = pc.shape[0]
    H = D // DV

    def body(lg_ref, k_ref, v_ref, s_ref):
        m = lax.broadcasted_iota(jnp.int32, (Lc, DV), 0).astype(F32)
        for pr in range(H // 2):
            k2 = k_ref[:, pr * 128:(pr + 1) * 128].astype(F32) * K_SCALE
            res = [[None, None], [None, None]]
            for e in range(2):
                h = 2 * pr + e
                v = v_ref[:, h * DV:(h + 1) * DV]
                dec_f = jnp.exp(lg_ref[0, h] * (Lc - 1.0 - m))
                dec_b = jnp.exp(lg_ref[1, h] * m)
                res[0][e] = _dot_tn((k2 * dec_f).astype(BF16), v)
                res[1][e] = _dot_tn((k2 * dec_b).astype(BF16), v)
            s_ref[0, pr] = _pair_select(res[0][0], res[0][1])
            s_ref[1, pr] = _pair_select(res[1][0], res[1][1])

    return pl.pallas_call(
        body, name="ctx_states", grid=(1,),
        in_specs=[_smem_spec(), pl.BlockSpec((Lc, D // 2), lambda i: (0, 1)), pl.BlockSpec((Lc, D), lambda i: (0, 1))],
        out_specs=pl.BlockSpec((2, H // 2, 128, 128), lambda i: (0, 0, 0, 0)),
        out_shape=jax.ShapeDtypeStruct((2, H // 2, 128, 128), F32),
    )(lg, pqk_c, pc)


T_M, T_MT = 0, 1
T_MF1, T_MB1 = 2, 3
T_QF, T_QB = 4, 5
T_KF, T_KB = 6, 7


def _decay_tables(lg, H):
    def body(lg_ref, t_ref):
        h = pl.program_id(0)
        lgf, lgb = lg_ref[0, h], lg_ref[1, h]
        i = lax.broadcasted_iota(jnp.int32, (CHUNK, CHUNK), 0).astype(F32)
        j = lax.broadcasted_iota(jnp.int32, (CHUNK, CHUNK), 1).astype(F32)
        d = i - j
        mf = jnp.where(d > 0, jnp.exp(lgf * jnp.maximum(d, 0.0)), 0.0)
        mb = jnp.where(d < 0, jnp.exp(lgb * jnp.maximum(-d, 0.0)), 0.0)
        mf_t = jnp.where(d < 0, jnp.exp(lgf * jnp.maximum(-d, 0.0)), 0.0)
        mb_t = jnp.where(d > 0, jnp.exp(lgb * jnp.maximum(d, 0.0)), 0.0)
        diag = jnp.where(d == 0, 2.0, 0.0)
        t_ref[0, T_M] = mf + mb + diag
        t_ref[0, T_MT] = mf_t + mb_t + diag
        t_ref[0, T_MF1] = mf * d
        t_ref[0, T_MB1] = mb * (-d)
        t_ref[0, T_QF] = jnp.exp(lgf * (i + 1.0))
        t_ref[0, T_QB] = jnp.exp(lgb * (CHUNK - i))
        t_ref[0, T_KF] = jnp.exp(lgf * (CHUNK - 1.0 - i))
        t_ref[0, T_KB] = jnp.exp(lgb * i)

    return pl.pallas_call(
        body, name="decay_tables", grid=(H,), in_specs=[_smem_spec()],
        out_specs=pl.BlockSpec((1, 8, CHUNK, CHUNK), lambda h: (h, 0, 0, 0)),
        out_shape=jax.ShapeDtypeStruct((H, 8, CHUNK, CHUNK), F32),
    )(lg)


def _tab_spec(H):
    return pl.BlockSpec((H, 8, CHUNK, CHUNK), lambda n: (0, 0, 0, 0))


def _chunk_decay(tab_ref, h):
    return tab_ref[h, T_QF, CHUNK - 1:CHUNK, :], tab_ref[h, T_QB, 0:1, :]


def _ret_states(kr, p, s0, tab, D):
    L = kr.shape[0]
    H = D // DV
    N = L // CHUNK
    HP = H // 2

    def body(tab_ref, kf_ref, kb_ref, vf_ref, vb_ref, s0_ref, sf_out, sb_out, sf, sb):
        n = pl.program_id(0)

        @pl.when(n == 0)
        def _():
            sf[...] = s0_ref[0]
            sb[...] = s0_ref[1]

        for cc in range(RET_CPB):
            cf_, cb_ = cc, RET_CPB - 1 - cc
            rf, rb = slice(cf_ * CHUNK, (cf_ + 1) * CHUNK), slice(cb_ * CHUNK, (cb_ + 1) * CHUNK)
            sf_out[cf_] = sf[...]
            sb_out[cb_] = sb[...]
            for pr in range(HP):
                kf2 = kf_ref[rf, pr * 128:(pr + 1) * 128].astype(F32)
                kb2 = kb_ref[rb, pr * 128:(pr + 1) * 128].astype(F32)
                inc_f, inc_b, gf, gb = [], [], [], []
                for e in range(2):
                    h = 2 * pr + e
                    inc_f.append(_dot_tn((kf2 * tab_ref[h, T_KF]).astype(BF16), vf_ref[rf, h * DV:(h + 1) * DV]))
                    inc_b.append(_dot_tn((kb2 * tab_ref[h, T_KB]).astype(BF16), vb_ref[rb, h * DV:(h + 1) * DV]))
                    cf, cb = _chunk_decay(tab_ref, h)
                    gf.append(jnp.broadcast_to(cf, (128, 128)))
                    gb.append(jnp.broadcast_to(cb, (128, 128)))
                sf[pr] = _pair_select(gf[0], gf[1]) * sf[pr] + _pair_select(inc_f[0], inc_f[1])
                sb[pr] = _pair_select(gb[0], gb[1]) * sb[pr] + _pair_select(inc_b[0], inc_b[1])

    st = jax.ShapeDtypeStruct((N, HP, 128, 128), F32)
    R = RET_CPB * CHUNK
    NB = N // RET_CPB
    return _riding_call(
        body, None, NB, name="ret_states", args=(tab, kr, kr, p, p, s0),
        in_specs=[_tab_spec(H),
                  pl.BlockSpec((R, D // 2), lambda n: (n, 0)),
                  pl.BlockSpec((R, D // 2), lambda n: (NB - 1 - n, 0)),
                  pl.BlockSpec((R, D), lambda n: (n, 5)),
                  pl.BlockSpec((R, D), lambda n: (NB - 1 - n, 5)),
                  pl.BlockSpec((2, HP, 128, 128), lambda n: (0, 0, 0, 0))],
        out_specs=[pl.BlockSpec((RET_CPB, HP, 128, 128), lambda n: (n, 0, 0, 0)),
                   pl.BlockSpec((RET_CPB, HP, 128, 128), lambda n: (NB - 1 - n, 0, 0, 0))],
        out_shape=[st, st],
        scratch=[pltpu.VMEM((HP, 128, 128), F32), pltpu.VMEM((HP, 128, 128), F32)],
        cparams=_cparams(("arbitrary",)))


def _ret_out(qr, kr, p, sf_prev, sb_prev, gn_w, tab, D):
    L = qr.shape[0]
    H = D // DV
    N = L // CHUNK
    HP = H // 2

    def body(tab_ref, q_ref, k_ref, v_ref, zb_ref, sf_ref, sb_ref, gn_ref, o_ref, yb_ref):
        def chunk(cc, carry):
            rows = pl.ds(pl.multiple_of(cc * CHUNK, CHUNK), CHUNK)
            for pr in range(HP):
                q2 = q_ref[rows, pr * 128:(pr + 1) * 128]
                k2 = k_ref[rows, pr * 128:(pr + 1) * 128]
                sfp = sf_ref[cc, pr].astype(BF16)
                sbp = sb_ref[cc, pr].astype(BF16)
                for e in range(2):
                    h = 2 * pr + e
                    sl = slice(h * DV, (h + 1) * DV)
                    qm = jnp.where(_head_lane_mask(q2.shape, e), q2, jnp.zeros_like(q2))
                    a = (_dot_nt(qm, k2) * tab_ref[h, T_M]).astype(BF16)
                    qf = qm.astype(F32)
                    o = _dot(a, v_ref[rows, sl])
                    o += _dot((qf * tab_ref[h, T_QF]).astype(BF16), sfp)
                    o += _dot((qf * tab_ref[h, T_QB]).astype(BF16), sbp)
                    o_ref[rows, sl] = o
                    mu = jnp.mean(o, axis=-1, keepdims=True)
                    oc = o - mu
                    rstd = lax.rsqrt(jnp.mean(oc * oc, axis=-1, keepdims=True) + EPS)
                    zb = zb_ref[rows, sl].astype(F32)
                    yb_ref[rows, sl] = (zb * _sigmoid(zb) * (oc * rstd * gn_ref[:, sl])).astype(BF16)
            return carry

        lax.fori_loop(0, RET_CPB, chunk, 0)

    R = RET_CPB * CHUNK
    return _riding_call(
        body, None, N // RET_CPB, name="ret_out", args=(tab, qr, kr, p, p, sf_prev, sb_prev, gn_w),
        in_specs=[_tab_spec(H),
                  pl.BlockSpec((R, D // 2), lambda n: (n, 0)),
                  pl.BlockSpec((R, D // 2), lambda n: (n, 0)),
                  pl.BlockSpec((R, D), lambda n: (n, 5)),
                  pl.BlockSpec((R, D), lambda n: (n, 6)),
                  pl.BlockSpec((RET_CPB, HP, 128, 128), lambda n: (n, 0, 0, 0)),
                  pl.BlockSpec((RET_CPB, HP, 128, 128), lambda n: (n, 0, 0, 0)),
                  _vec_spec(D)],
        out_specs=[pl.BlockSpec((R, D), lambda n: (n, 0)), pl.BlockSpec((R, D), lambda n: (n, 0))],
        out_shape=[jax.ShapeDtypeStruct((L, D), F32), jax.ShapeDtypeStruct((L, D), BF16)],
        cparams=_cparams(("arbitrary",)))


def _mid(p, yb, o, x, tgt, w3, g, fw, conv_w, conv_b, gn_w, D):
    L = x.shape[0]
    H = D // DV
    tm = min(256, L)
    nt = L // tm

    def body(h_ref, bg_ref, cg_ref, za_ref, hp_ref, hn_ref, cp_ref, cn_ref, yb_ref, ga_ref, gb_ref, zb_ref, o_ref,
             x_ref, t_ref, w_hbm, g_ref, fw_ref, cw_ref, cb_ref, gn_ref,
             dx1_ref, dya_ref, do_ref, dzb_ref, dgab_ref, dw_hbm, st_ref, w_vm, dw_acc, sem):
        i = pl.program_id(0)

        @pl.when(i == 0)
        def _():
            cp = pltpu.make_async_copy(w_hbm, w_vm, sem)
            cp.start()
            dw_acc[...] = jnp.zeros_like(dw_acc)
            st_ref[...] = jnp.zeros_like(st_ref)
            cp.wait()

        u = cg_ref[...].astype(F32) * h_ref[...].astype(F32)
        above = jnp.where(i == 0, 0.0, cp_ref[15:16, :].astype(F32) * hp_ref[15:16, :].astype(F32))
        below = jnp.where(i == nt - 1, 0.0, cn_ref[0:1, :].astype(F32) * hn_ref[0:1, :].astype(F32))
        dn, up = _shift_rows(u, above, below)
        co = cw_ref[0:1, :] * dn + cw_ref[1:2, :] * u + cw_ref[2:3, :] * up + cb_ref[...]
        za = za_ref[...].astype(F32)
        ya_b = (za * _sigmoid(za) * bg_ref[...].astype(F32) * co).astype(BF16)
        yb_b = yb_ref[...]
        y_a = _dot(ya_b, w_vm[0])
        y_b = _dot(yb_b, w_vm[1])
        sga = _sigmoid(ga_ref[...].astype(F32))
        sgb = _sigmoid(gb_ref[...].astype(F32))
        mix_b = (sga * y_a + sgb * y_b).astype(BF16)
        y_x = _dot(mix_b, w_vm[2])
        gvec, fwv = g_ref[...], fw_ref[...]
        x1 = x_ref[...] + gvec * y_x
        r1 = lax.rsqrt(jnp.mean(x1 * x1, axis=-1, keepdims=True) + EPS)
        xh = x1 * r1
        diff = xh * fwv - t_ref[...]
        dout = diff * (1.0 / D)
        dxh = dout * fwv
        dx1 = r1 * (dxh - xh * jnp.mean(dxh * xh, axis=-1, keepdims=True))
        dx1_ref[...] = dx1
        st_ref[0:1, :] += jnp.sum(dout * xh, axis=0, keepdims=True)
        st_ref[1:2, :] += jnp.sum(dx1 * y_x, axis=0, keepdims=True)
        st_ref[2:3, :] += jnp.sum(diff * diff, axis=0, keepdims=True)
        dyx_b = (dx1 * gvec).astype(BF16)
        dmix = _dot_nt(dyx_b, w_vm[2])
        dw_acc[2] += _dot_tn(mix_b, dyx_b)
        dya_b = (dmix * sga).astype(BF16)
        dyb_b = (dmix * sgb).astype(BF16)
        dgab_ref[:, 0:D] = (dmix * y_a * sga * (1.0 - sga)).astype(BF16)
        dgab_ref[:, D:2 * D] = (dmix * y_b * sgb * (1.0 - sgb)).astype(BF16)
        dya_ref[...] = _dot_nt(dya_b, w_vm[0])
        dyb = _dot_nt(dyb_b, w_vm[1])
        dw_acc[0] += _dot_tn(ya_b, dya_b)
        dw_acc[1] += _dot_tn(yb_b, dyb_b)

        for h in range(H):
            sl = slice(h * DV, (h + 1) * DV)
            ov = o_ref[:, sl]
            oc = ov - jnp.mean(ov, axis=-1, keepdims=True)
            rstd = lax.rsqrt(jnp.mean(oc * oc, axis=-1, keepdims=True) + EPS)
            rn = oc * rstd
            gw = gn_ref[:, sl]
            zb = zb_ref[:, sl].astype(F32)
            sz = _sigmoid(zb)
            dy = dyb[:, sl]
            dzb_ref[:, sl] = (dy * (rn * gw) * (sz * (1.0 + zb * (1.0 - sz)))).astype(BF16)
            dretn = dy * (zb * sz)
            st_ref[3:4, sl] += jnp.sum(dretn * rn, axis=0, keepdims=True)
            drn = dretn * gw
            do_ref[:, sl] = (rstd * (drn - jnp.mean(drn, axis=-1, keepdims=True)
                                     - rn * jnp.mean(drn * rn, axis=-1, keepdims=True))).astype(BF16)

        @pl.when(i == nt - 1)
        def _():
            out = pltpu.make_async_copy(dw_acc, dw_hbm, sem)
            out.start()
            out.wait()

    row = lambda col: pl.BlockSpec((tm, D), lambda i: (i, col))
    any_spec = pl.BlockSpec(memory_space=pl.ANY)
    f32o = jax.ShapeDtypeStruct((L, D), F32)
    bf16o = jax.ShapeDtypeStruct((L, D), BF16)
    hp, hn = _halo_specs(tm, L, D, 0)
    cp, cn = _halo_specs(tm, L, D, 2)
    return pl.pallas_call(
        body, name="mid", grid=(nt,),
        in_specs=[row(0), row(1), row(2), row(3), hp, hn, cp, cn, row(0), row(7), row(8), row(6), row(0),
                  row(0), row(0), any_spec, _vec_spec(D), _vec_spec(D),
                  pl.BlockSpec((8, D), lambda i: (0, 0)), _vec_spec(D), _vec_spec(D)],
        out_specs=[row(0), row(0), row(0), row(0), pl.BlockSpec((tm, 2 * D), lambda i: (i, 0)), any_spec,
                   pl.BlockSpec((8, D), lambda i: (0, 0))],
        out_shape=[f32o, f32o, bf16o, bf16o, jax.ShapeDtypeStruct((L, 2 * D), BF16),
                   jax.ShapeDtypeStruct((3, D, D), F32), jax.ShapeDtypeStruct((8, D), F32)],
        scratch_shapes=[pltpu.VMEM((3, D, D), BF16), pltpu.VMEM((3, D, D), F32), pltpu.SemaphoreType.DMA],
        compiler_params=_cparams(("arbitrary",), VMEM_LIMIT),
    )(p, p, p, p, p, p, p, p, yb, p, p, p, o, x, tgt, w3, g, fw, conv_w, conv_b, gn_w)


def _conv_bwd(dya, p, conv_w, conv_b, D, exchange=None):
    L = p.shape[0]
    tl = min(256, L)
    nt = L // tl

    def body(d_ref, h_ref, bg_ref, cg_ref, za_ref,
             dp_ref, dn_ref, hp_ref, hn_ref, bp_ref, bn_ref, cp_ref, cn_ref, zp_ref, zn_ref,
             w_ref, b_ref, dc_ref, st_ref):
        i = pl.program_id(0)

        @pl.when(i == 0)
        def _():
            st_ref[...] = jnp.zeros_like(st_ref)

        first, last = i == 0, i == nt - 1
        h = h_ref[...].astype(F32)
        cg = cg_ref[...].astype(F32)
        bg = bg_ref[...].astype(F32)
        za = za_ref[...].astype(F32)
        dy = d_ref[...].astype(F32)
        u = cg * h
        u_above = jnp.where(first, 0.0, cp_ref[15:16, :].astype(F32) * hp_ref[15:16, :].astype(F32))
        u_below = jnp.where(last, 0.0, cn_ref[0:1, :].astype(F32) * hn_ref[0:1, :].astype(F32))
        u_dn, u_up = _shift_rows(u, u_above, u_below)
        w0, w1, w2 = w_ref[0:1, :], w_ref[1:2, :], w_ref[2:3, :]
        co = w0 * u_dn + w1 * u + w2 * u_up + b_ref[...]
        sz = _sigmoid(za)
        silu = za * sz
        dc_ref[:, 3 * D:4 * D] = (dy * bg * co * (sz * (1.0 + za * (1.0 - sz)))).astype(BF16)
        dc_ref[:, D:2 * D] = (dy * silu * co).astype(BF16)
        dco = dy * silu * bg

        def edge(dr, zr, br, r):
            z = zr[r:r + 1, :].astype(F32)
            return dr[r:r + 1, :].astype(F32) * (z * _sigmoid(z)) * br[r:r + 1, :].astype(F32)

        dco_above = jnp.where(first, 0.0, edge(dp_ref, zp_ref, bp_ref, 15))
        dco_below = jnp.where(last, 0.0, edge(dn_ref, zn_ref, bn_ref, 0))
        dco_dn, dco_up = _shift_rows(dco, dco_above, dco_below)
        du = w0 * dco_up + w1 * dco + w2 * dco_dn
        dc_ref[:, 2 * D:3 * D] = (du * h).astype(BF16)
        dc_ref[:, 0:D] = (du * cg).astype(BF16)
        st_ref[0:1, :] += jnp.sum(dco * u_dn, axis=0, keepdims=True)
        st_ref[1:2, :] += jnp.sum(dco * u, axis=0, keepdims=True)
        st_ref[2:3, :] += jnp.sum(dco * u_up, axis=0, keepdims=True)
        st_ref[3:4, :] += jnp.sum(dco, axis=0, keepdims=True)

    main = lambda col: pl.BlockSpec((tl, D), lambda i: (i, col))
    halos = []
    for col in (0, 0, 1, 2, 3):
        halos.extend(_halo_specs(tl, L, D, col))
    return _riding_call(
        body, exchange, nt, name="conv_bwd",
        args=(dya, p, p, p, p, dya, dya, p, p, p, p, p, p, p, p, conv_w, conv_b),
        in_specs=[main(0), main(0), main(1), main(2), main(3)] + halos
                 + [pl.BlockSpec((8, D), lambda i: (0, 0)), _vec_spec(D)],
        out_specs=[pl.BlockSpec((tl, 4 * D), lambda i: (i, 0)), pl.BlockSpec((8, D), lambda i: (0, 0))],
        out_shape=[jax.ShapeDtypeStruct((L, 4 * D), BF16), jax.ShapeDtypeStruct((8, D), F32)],
        cparams=_cparams(("arbitrary",)))


def _ret_bwd_states(qr, do, tab, D):
    L = qr.shape[0]
    H = D // DV
    N = L // CHUNK
    HP = H // 2

    def body(tab_ref, qf_ref, qb_ref, dof_ref, dob_ref, dsf_out, dsb_out, ds0_out, dsf, dsb):
        n = pl.program_id(0)

        @pl.when(n == 0)
        def _():
            dsf[...] = jnp.zeros_like(dsf)
            dsb[...] = jnp.zeros_like(dsb)

        for cc in range(RET_CPB):
            cf_, cb_ = RET_CPB - 1 - cc, cc
            rf, rb = slice(cf_ * CHUNK, (cf_ + 1) * CHUNK), slice(cb_ * CHUNK, (cb_ + 1) * CHUNK)
            dsf_out[cf_] = dsf[...]
            dsb_out[cb_] = dsb[...]
            for pr in range(HP):
                qf2 = qf_ref[rf, pr * 128:(pr + 1) * 128].astype(F32)
                qb2 = qb_ref[rb, pr * 128:(pr + 1) * 128].astype(F32)
                inc_f, inc_b, gf, gb = [], [], [], []
                for e in range(2):
                    h = 2 * pr + e
                    inc_f.append(_dot_tn((qf2 * tab_ref[h, T_QF]).astype(BF16), dof_ref[rf, h * DV:(h + 1) * DV]))
                    inc_b.append(_dot_tn((qb2 * tab_ref[h, T_QB]).astype(BF16), dob_ref[rb, h * DV:(h + 1) * DV]))
                    cf, cb = _chunk_decay(tab_ref, h)
                    gf.append(jnp.broadcast_to(cf, (128, 128)))
                    gb.append(jnp.broadcast_to(cb, (128, 128)))
                dsf[pr] = _pair_select(gf[0], gf[1]) * dsf[pr] + _pair_select(inc_f[0], inc_f[1])
                dsb[pr] = _pair_select(gb[0], gb[1]) * dsb[pr] + _pair_select(inc_b[0], inc_b[1])

        @pl.when(n == NB - 1)
        def _():
            ds0_out[0] = dsf[...]
            ds0_out[1] = dsb[...]

    st = jax.ShapeDtypeStruct((N, HP, 128, 128), F32)
    R = RET_CPB * CHUNK
    NB = N // RET_CPB
    return pl.pallas_call(
        body, name="ret_bwd_states", grid=(NB,),
        in_specs=[_tab_spec(H),
                  pl.BlockSpec((R, D // 2), lambda n: (NB - 1 - n, 0)),
                  pl.BlockSpec((R, D // 2), lambda n: (n, 0)),
                  pl.BlockSpec((R, D), lambda n: (NB - 1 - n, 0)),
                  pl.BlockSpec((R, D), lambda n: (n, 0))],
        out_specs=[pl.BlockSpec((RET_CPB, HP, 128, 128), lambda n: (NB - 1 - n, 0, 0, 0)),
                   pl.BlockSpec((RET_CPB, HP, 128, 128), lambda n: (n, 0, 0, 0)),
                   pl.BlockSpec((2, HP, 128, 128), lambda n: (0, 0, 0, 0))],
        out_shape=[st, st, jax.ShapeDtypeStruct((2, HP, 128, 128), F32)],
        scratch_shapes=[pltpu.VMEM((HP, 128, 128), F32), pltpu.VMEM((HP, 128, 128), F32)],
        compiler_params=_cparams(("arbitrary",)),
    )(tab, qr, qr, do, do)


def _ret_bwd_main(qr, kr, p, do, sf_prev, sb_prev, dsf, dsb, c2, s2, tab, D, exchange=None):
    L = qr.shape[0]
    H = D // DV
    N = L // CHUNK
    HP = H // 2
    W = D // 2

    def body(tab_ref, q_ref, k_ref, v_ref, do_ref, sf_ref, sb_ref, dsf_ref, dsb_ref, c_ref, s_ref,
             dr_ref, st_ref, dl_acc):
        @pl.when(pl.program_id(0) == 0)
        def _():
            dl_acc[...] = jnp.zeros_like(dl_acc)

        i = lax.broadcasted_iota(jnp.int32, (CHUNK, 128), 0).astype(F32)
        rowid = lax.broadcasted_iota(jnp.int32, (128, 128), 0)

        def chunk(cc, carry):
            rows = pl.ds(pl.multiple_of(cc * CHUNK, CHUNK), CHUNK)
            c, s = c_ref[rows, :], s_ref[rows, :]
            for pr in range(HP):
                ps = slice(pr * 128, (pr + 1) * 128)
                q2, k2 = q_ref[rows, ps], k_ref[rows, ps]
                sf32, sb32 = sf_ref[cc, pr], sb_ref[cc, pr]
                dsf32, dsb32 = dsf_ref[cc, pr], dsb_ref[cc, pr]
                sfp, sbp = sf32.astype(BF16), sb32.astype(BF16)
                dsfp, dsbp = dsf32.astype(BF16), dsb32.astype(BF16)
                dq2 = jnp.zeros((CHUNK, 128), F32)
                dk2 = jnp.zeros((CHUNK, 128), F32)
                for e in range(2):
                    h = 2 * pr + e
                    sl = slice(h * DV, (h + 1) * DV)
                    hm = _head_lane_mask(q2.shape, e)
                    qm = jnp.where(hm, q2, jnp.zeros_like(q2))
                    km = jnp.where(hm, k2, jnp.zeros_like(k2))
                    qf, kf = qm.astype(F32), km.astype(F32)
                    v, do = v_ref[rows, sl], do_ref[rows, sl]
                    vf, dof = v.astype(F32), do.astype(F32)
                    m_t = tab_ref[h, T_MT]
                    sc = _dot_nt(qm, k2)
                    dpm = _dot_nt(do, v)
                    dsc = (dpm * tab_ref[h, T_M]).astype(BF16)
                    a_t = (_dot_nt(km, q2) * m_t).astype(BF16)
                    dsc_t = (_dot_nt(v, do) * m_t).astype(BF16)
                    dq_f, dq_b = tab_ref[h, T_QF], tab_ref[h, T_QB]
                    dk_f, dk_b = tab_ref[h, T_KF], tab_ref[h, T_KB]
                    dq = _dot(dsc, km)
                    dq += jnp.where(hm, dq_f * _dot_nt(do, sfp) + dq_b * _dot_nt(do, sbp), 0.0)
                    dk = _dot(dsc_t, qm)
                    dk += jnp.where(hm, dk_f * _dot_nt(v, dsfp) + dk_b * _dot_nt(v, dsbp), 0.0)
                    kdf = _dot((kf * dk_f).astype(BF16), dsfp)
                    kdb = _dot((kf * dk_b).astype(BF16), dsbp)
                    dr_ref[rows, D + h * DV:D + (h + 1) * DV] = (_dot(a_t, do) + kdf + kdb).astype(BF16)
                    dq2 += dq
                    dk2 += dk
                    xf = _dot((qf * dq_f).astype(BF16), sfp)
                    xb = _dot((qf * dq_b).astype(BF16), sbp)
                    pair = (rowid < DK) if e == 0 else (rowid >= DK)
                    gcf, gcb = tab_ref[h, T_QF, CHUNK - 1:CHUNK, 0:1], tab_ref[h, T_QB, 0:1, 0:1]
                    scdp = sc * dpm
                    dl_acc[h, 0] += scdp * tab_ref[h, T_MF1] + xf * dof * (i + 1.0) \
                        + kdf * vf * (CHUNK - 1.0 - i) + (CHUNK * gcf) * jnp.where(pair, dsf32 * sf32, 0.0)
                    dl_acc[h, 1] += scdp * tab_ref[h, T_MB1] + xb * dof * (CHUNK - i) \
                        + kdb * vf * i + (CHUNK * gcb) * jnp.where(pair, dsb32 * sb32, 0.0)
                dr_ref[rows, ps] = (dq2 * c - _swap_halves(dq2) * s).astype(BF16)
                dr_ref[rows, W + pr * 128:W + (pr + 1) * 128] = \
                    ((dk2 * c - _swap_halves(dk2) * s) * K_SCALE).astype(BF16)
            return carry

        lax.fori_loop(0, RET_CPB, chunk, 0)

        @pl.when(pl.program_id(0) == N // RET_CPB - 1)
        def _():
            lane = lax.broadcasted_iota(jnp.int32, (1, 128), 1)
            acc = [jnp.zeros((1, 128), F32), jnp.zeros((1, 128), F32)]
            for h in range(H):
                for b in range(2):
                    acc[b] += jnp.where(lane == h, _sum_all(dl_acc[h, b]), 0.0)
            st_ref[...] = jnp.zeros_like(st_ref)
            st_ref[0:1, :] = acc[0]
            st_ref[1:2, :] = acc[1]

    R = RET_CPB * CHUNK
    st_spec = pl.BlockSpec((RET_CPB, HP, 128, 128), lambda n: (n, 0, 0, 0))
    half = pl.BlockSpec((R, W), lambda n: (n, 0))
    rope = pl.BlockSpec((R, 128), lambda n: (n, 0))
    return _riding_call(
        body, exchange, N // RET_CPB, name="ret_bwd_main",
        args=(tab, qr, kr, p, do, sf_prev, sb_prev, dsf, dsb, c2, s2),
        in_specs=[_tab_spec(H), half, half,
                  pl.BlockSpec((R, D), lambda n: (n, 5)),
                  pl.BlockSpec((R, D), lambda n: (n, 0)),
                  st_spec, st_spec, st_spec, st_spec, rope, rope],
        out_specs=[pl.BlockSpec((R, 2 * D), lambda n: (n, 0)),
                   pl.BlockSpec((8, 128), lambda n: (0, 0))],
        out_shape=[jax.ShapeDtypeStruct((L, 2 * D), BF16), jax.ShapeDtypeStruct((8, 128), F32)],
        scratch=[pltpu.VMEM((H, 2, CHUNK, 128), F32)],
        cparams=_cparams(("arbitrary",)))


def _ctx_bwd(pc, pqk_c, ds0, lg, D):
    Lc = pc.shape[0]
    H = D // DV
    HP = H // 2
    W = D // 2

    def body(lg_ref, k_ref, v_ref, ds_ref, dr_ref, st_ref):
        dqk_ref = dr_ref.at[:, 0:D]
        dv_ref = dr_ref.at[:, D:2 * D]
        m = lax.broadcasted_iota(jnp.int32, (Lc, 128), 0).astype(F32)
        lane = lax.broadcasted_iota(jnp.int32, (1, 128), 1)
        acc_f = jnp.zeros((1, 128), F32)
        acc_b = jnp.zeros((1, 128), F32)
        dqk_ref[:, 0:W] = jnp.zeros((Lc, W), BF16)
        for pr in range(HP):
            ps = slice(pr * 128, (pr + 1) * 128)
            k2 = k_ref[:, ps].astype(F32) * K_SCALE
            dsfp, dsbp = ds_ref[0, pr].astype(BF16), ds_ref[1, pr].astype(BF16)
            dk2 = jnp.zeros((Lc, 128), F32)
            for e in range(2):
                h = 2 * pr + e
                sl = slice(h * DV, (h + 1) * DV)
                hm = _head_lane_mask(k2.shape, e)
                km = jnp.where(hm, k2, 0.0)
                v = v_ref[:, sl]
                vf = v.astype(F32)
                dec_f = jnp.exp(lg_ref[0, h] * (Lc - 1.0 - m))
                dec_b = jnp.exp(lg_ref[1, h] * m)
                kdf = _dot((km * dec_f).astype(BF16), dsfp)
                kdb = _dot((km * dec_b).astype(BF16), dsbp)
                dv_ref[:, sl] = (kdf + kdb).astype(BF16)
                dk2 += jnp.where(hm, dec_f * _dot_nt(v, dsfp) + dec_b * _dot_nt(v, dsbp), 0.0)
                acc_f += jnp.where(lane == h, _sum_all(kdf * vf * (Lc - 1.0 - m)), 0.0)
                acc_b += jnp.where(lane == h, _sum_all(kdb * vf * m), 0.0)
            dqk_ref[:, W + pr * 128:W + (pr + 1) * 128] = (dk2 * K_SCALE).astype(BF16)
        st_ref[...] = jnp.zeros_like(st_ref)
        st_ref[0:1, :] = acc_f
        st_ref[1:2, :] = acc_b

    return pl.pallas_call(
        body, name="ctx_bwd", grid=(1,),
        in_specs=[_smem_spec(), pl.BlockSpec((Lc, W), lambda i: (0, 1)), pl.BlockSpec((Lc, D), lambda i: (0, 1)),
                  pl.BlockSpec((2, HP, 128, 128), lambda i: (0, 0, 0, 0))],
        out_specs=[pl.BlockSpec((Lc, 2 * D), lambda i: (0, 0)), pl.BlockSpec((8, 128), lambda i: (0, 0))],
        out_shape=[jax.ShapeDtypeStruct((Lc, 2 * D), BF16), jax.ShapeDtypeStruct((8, 128), F32)],
    )(lg, pqk_c, pc, ds0)


class _Exchange(NamedTuple):
    inputs: tuple
    out_shapes: tuple
    n_copies: int
    build: Callable


def _exchange_parts(exchange):
    if exchange is None:
        return [], [], [], [], []
    n = exchange.n_copies
    return (list(exchange.inputs), [ANY] * len(exchange.inputs), list(exchange.out_shapes),
            [ANY] * len(exchange.out_shapes), [pltpu.SemaphoreType.DMA((n,)), pltpu.SemaphoreType.DMA((n,))])


def _riding_call(body, exchange, n_steps, *, args, in_specs, out_specs, out_shape, name, cparams, scratch=()):
    ex_args, ex_in_specs, ex_shapes, ex_out_specs, ex_scratch = _exchange_parts(exchange)
    n_in, n_out, n_sc = len(args), len(out_shape), len(scratch)

    def riding(*refs):
        k = n_in + len(ex_args)
        ins, ex_in = refs[:n_in], refs[n_in:k]
        outs, ex_out = refs[k:k + n_out], refs[k + n_out:k + n_out + len(ex_shapes)]
        k += n_out + len(ex_shapes)
        own_scratch, ex_sems = refs[k:k + n_sc], refs[k + n_sc:]
        step = pl.program_id(0)
        if exchange is not None:
            @pl.when(step == 0)
            def _():
                for rc in exchange.build(ex_in, ex_out, *ex_sems):
                    rc.start()
        body(*ins, *outs, *own_scratch)
        if exchange is not None:
            @pl.when(step == n_steps - 1)
            def _():
                for rc in exchange.build(ex_in, ex_out, *ex_sems):
                    rc.wait()

    return tuple(pl.pallas_call(
        riding, name=name, grid=(n_steps,),
        in_specs=list(in_specs) + ex_in_specs, out_specs=list(out_specs) + ex_out_specs,
        out_shape=list(out_shape) + ex_shapes, scratch_shapes=list(scratch) + ex_scratch,
        compiler_params=cparams,
    )(*args, *ex_args))


def _dxm(groups, col0, w, x, nw, sc, dx1, name, exchange=None):
    L, D = x.shape
    tm = min(256, L)
    nt = L // tm
    ng = len(groups)
    widths = [g.shape[1] for g in groups]
    wtot = sum(widths)
    with_dx = dx1 is not None
    ex_args, ex_in_specs, ex_shapes, ex_out_specs, ex_scratch = _exchange_parts(exchange)
    n_in = ng + 4 + (1 if with_dx else 0)
    n_out = 2 if with_dx else 1

    def body(*refs):
        group_refs = refs[:ng]
        w_hbm, x_ref, nw_ref, sc_ref = refs[ng:ng + 4]
        ex_in = refs[n_in:n_in + len(ex_args)]
        outs = refs[n_in + len(ex_args):]
        if with_dx:
            dx1_ref, gx_ref, st_ref = refs[ng + 4], outs[0], outs[1]
        else:
            st_ref = outs[0]
        ex_out = outs[n_out:n_out + len(ex_shapes)]
        w_vm, sem = outs[n_out + len(ex_shapes):n_out + len(ex_shapes) + 2]
        ex_sems = outs[n_out + len(ex_shapes) + 2:]
        i = pl.program_id(0)

        @pl.when(i == 0)
        def _():
            cp = pltpu.make_async_copy(w_hbm.at[:, col0 * D:col0 * D + wtot], w_vm, sem)
            cp.start()
            if exchange is not None:
                for rc in exchange.build(ex_in, ex_out, *ex_sems):
                    rc.start()
            st_ref[...] = jnp.zeros_like(st_ref)
            cp.wait()

        dxm, off = None, 0
        for g_ref, wd in zip(group_refs, widths):
            part = _dot_nt(g_ref[...], w_vm[:, off:off + wd])
            dxm = part if dxm is None else dxm + part
            off += wd

        xv = x_ref[...]
        r = lax.rsqrt(jnp.mean(xv * xv, axis=-1, keepdims=True) + EPS)
        xh = xv * r
        nwv = nw_ref[...]
        dxn = dxm * (1.0 + sc_ref[...])
        st_ref[0:1, :] += jnp.sum(dxm, axis=0, keepdims=True)
        st_ref[1:2, :] += jnp.sum(dxm * (xh * nwv), axis=0, keepdims=True)
        st_ref[2:3, :] += jnp.sum(dxn * xh, axis=0, keepdims=True)
        if with_dx:
            dxh = dxn * nwv
            gx_ref[...] = dx1_ref[...] + r * (dxh - xh * jnp.mean(dxh * xh, axis=-1, keepdims=True))

        if exchange is not None:
            @pl.when(i == nt - 1)
            def _():
                for rc in exchange.build(ex_in, ex_out, *ex_sems):
                    rc.wait()

    row = pl.BlockSpec((tm, D), lambda i: (i, 0))
    in_specs = [pl.BlockSpec((tm, wd), lambda i: (i, 0)) for wd in widths] + [ANY, row, _vec_spec(D), _vec_spec(D)]
    out_specs = [pl.BlockSpec((8, D), lambda i: (0, 0))]
    out_shape = [jax.ShapeDtypeStruct((8, D), F32)]
    args = list(groups) + [w, x, nw, sc]
    if with_dx:
        in_specs.append(row)
        out_specs.insert(0, row)
        out_shape.insert(0, jax.ShapeDtypeStruct((L, D), F32))
        args.append(dx1)
    res = pl.pallas_call(
        body, name=name, grid=(nt,),
        in_specs=in_specs + ex_in_specs, out_specs=out_specs + ex_out_specs, out_shape=out_shape + ex_shapes,
        scratch_shapes=[pltpu.VMEM((D, wtot), BF16), pltpu.SemaphoreType.DMA] + ex_scratch,
        compiler_params=_cparams(("arbitrary",), VMEM_LIMIT),
    )(*args, *ex_args)
    gx = res[0] if with_dx else None
    return (gx, res[n_out - 1], *res[n_out:])


DW_TN = 512
DW_RING = 4


def _dw_in(xmt, groups, cmt, dr_c, D, pair):
    L = xmt.shape[1]
    Lc = cmt.shape[1]
    Dh = D // 2
    tn = min(DW_TN, D)
    nblk = [g.shape[1] // tn for g in groups]
    starts = [sum(nblk[:g]) for g in range(len(groups))]
    ng = len(groups)
    nj = sum(nblk)
    rows_out = Dh if pair else D

    def body(*refs):
        xt_hbm = refs[0]
        group_refs = refs[1:1 + ng]
        ct_hbm, drc_ref, o_ref = refs[1 + ng:4 + ng]
        rest = refs[4 + ng:]
        if pair:
            ra_hbm, xt_vm, ct_vm, loc, ring, s_send, s_recv = rest
            pos = _position()
            sib = _peer(pos, 1)
        else:
            xt_vm, ct_vm, loc = rest
        j = pl.program_id(0)

        @pl.when(j == 0)
        def _():
            if pair:
                c = pos[2]
                other = pl.ds(pl.multiple_of((1 - c) * Dh, Dh), Dh)
                mine = pl.ds(pl.multiple_of(c * Dh, Dh), Dh)
                cps = [pltpu.make_async_copy(xt_hbm.at[other, :], xt_vm.at[0:Dh, :], loc.at[0]),
                       pltpu.make_async_copy(xt_hbm.at[mine, :], xt_vm.at[Dh:D, :], loc.at[1]),
                       pltpu.make_async_copy(ct_hbm.at[other, :], ct_vm.at[0:Dh, :], loc.at[2]),
                       pltpu.make_async_copy(ct_hbm.at[mine, :], ct_vm.at[Dh:D, :], loc.at[3])]
            else:
                cps = [pltpu.make_async_copy(xt_hbm, xt_vm, loc.at[0]), pltpu.make_async_copy(ct_hbm, ct_vm, loc.at[1])]
            for cp in cps:
                cp.start()
            for cp in cps:
                cp.wait()

        def send(slot):
            cols = pl.ds(pl.multiple_of(j * tn, 128), tn)
            return pltpu.make_async_remote_copy(src_ref=ring.at[slot], dst_ref=ra_hbm.at[:, cols],
                                                send_sem=s_send.at[slot], recv_sem=s_recv,
                                                device_id=sib, device_id_type=MESH)

        for g in range(ng):
            @pl.when((j >= starts[g]) & (j < starts[g] + nblk[g]))
            def _(g=g):
                acc = _dot(xt_vm[...], group_refs[g][...])
                if g == 1:
                    acc += _dot(ct_vm[...], drc_ref[...])
                if not pair:
                    o_ref[...] = acc
                    return
                o_ref[...] = acc[Dh:, :]
                slot = lax.rem(j, DW_RING)

                @pl.when(j >= DW_RING)
                def _():
                    send(slot).wait_send()

                ring[slot] = acc[0:Dh, :]
                send(slot).start()

        if pair:
            @pl.when(j == nj - 1)
            def _():
                pltpu.make_async_remote_copy(src_ref=ra_hbm, dst_ref=ra_hbm, send_sem=s_send.at[0], recv_sem=s_recv,
                                             device_id=sib, device_id_type=MESH).wait_recv()
                for slot in range(DW_RING):
                    send(slot).wait_send()

    def group_spec(g, rows):
        return pl.BlockSpec((rows, tn), lambda j: (0, jnp.clip(j - starts[g], 0, nblk[g] - 1)))

    out_specs = [pl.BlockSpec((rows_out, tn), lambda j: (0, j))]
    out_shape = [jax.ShapeDtypeStruct((rows_out, nj * tn), F32)]
    scratch = [pltpu.VMEM((D, L), BF16), pltpu.VMEM((D, Lc), BF16), pltpu.SemaphoreType.DMA((4,))]
    if pair:
        out_specs.append(ANY)
        out_shape.append(jax.ShapeDtypeStruct((Dh, nj * tn), F32))
        scratch += [pltpu.VMEM((DW_RING, Dh, tn), F32), pltpu.SemaphoreType.DMA((DW_RING,)), pltpu.SemaphoreType.DMA]
    return tuple(pl.pallas_call(
        body, name="dw_in", grid=(nj,),
        in_specs=[ANY] + [group_spec(g, L) for g in range(ng)] + [ANY, group_spec(1, Lc)],
        out_specs=out_specs, out_shape=out_shape, scratch_shapes=scratch,
        compiler_params=_cparams(("arbitrary",), VMEM_LIMIT),
    )(xmt, *groups, cmt, dr_c))


def _local_step(x, ctx, tgt, mod_x, mod_c, norm_w, conv_w8, conv_b, lg, gn_w, fw, project, csidx=None, w_in_s=None):
    L, D = x.shape
    sh_x, sc_x, g_x = mod_x
    sh_c, sc_c = mod_c
    c2, s2 = _rope_tables(L)
    tab = _decay_tables(lg, D // DV)

    xm, xmt, *w_bf = _norm_mod(x, norm_w, sc_x, sh_x, "norm_mod_x", w_in_s)
    cm, cmt = _norm_mod(ctx, norm_w, sc_c, sh_c, "norm_mod_ctx")
    reduce = csidx is not None
    p, qr, kr, w_in, w3 = project(xm, c2, s2, *w_bf)
    pc, pqk_c = _in_proj(cm, w_in, "in_proj_ctx", QK_BLOCK, 2)
    s0 = _ctx_states(pc, pqk_c, lg, D)
    sf_prev, sb_prev = _ret_states(kr, p, s0, tab, D)
    o, yb = _ret_out(qr, kr, p, sf_prev, sb_prev, gn_w, tab, D)
    dx1, dya, do, dzb, dgab, dw3, st_mid = _mid(p, yb, o, x, tgt, w3, g_x, fw, conv_w8, conv_b, gn_w, D)
    dw3_5 = dw3.reshape(3, N_SHARD, 2, D // 8, D)
    dconv, st_conv, *ra_3 = _conv_bwd(dya, p, conv_w8, conv_b, D, _pair_exchange_w3(dw3_5) if reduce else None)
    dsf, dsb, ds0 = _ret_bwd_states(qr, do, tab, D)
    cs_3 = _sum_pair_w3(csidx[0:1], dw3_5, ra_3[0]) if reduce else None
    dret, st_lg, *rb_3 = _ret_bwd_main(qr, kr, p, do, sf_prev, sb_prev, dsf, dsb, c2, s2, tab, D,
                                       _chips_exchange_w3(cs_3) if reduce else None)
    g_3 = _sum_chips_w3(csidx, cs_3, rb_3[0]) if reduce else dw3
    dret_c, st_lgc = _ctx_bwd(pc, pqk_c, ds0, lg, D)
    groups = (dconv, dret, dzb, dgab)
    _, st_c = _dxm((dret_c,), 4, w_in, ctx, norm_w, sc_c, None, "dxm_ctx")
    return groups, dret_c, xmt, cmt, dx1, sc_x, w_in, g_3, (st_mid, st_conv, st_lg, st_lgc, st_c)


CHIP_FLIPS = (4, 2, 6)
ANY = pl.BlockSpec(memory_space=pl.ANY)
VMEM_FULL = pl.BlockSpec(memory_space=pltpu.VMEM)


def _position():
    return lax.axis_index("x"), lax.axis_index("y"), lax.axis_index("c")


def _peer(pos, k):
    x, y, c = pos
    return (1 - x if k & 4 else x, 1 - y if k & 2 else y, 1 - c if k & 1 else c)


def _dev_id(pos):
    return 4 * pos[0] + 2 * pos[1] + pos[2]


def _shard_of(pos):
    return 2 * pos[0] + pos[1]


def _remote(src, dst, send_sems, recv_sems, idx, to):
    return pltpu.make_async_remote_copy(src_ref=src, dst_ref=dst, send_sem=send_sems.at[idx],
                                        recv_sem=recv_sems.at[idx], device_id=to, device_id_type=MESH)


def _dot_f32(a, b):
    return jnp.dot(a, b, precision=lax.Precision.HIGHEST, preferred_element_type=F32)


def _silu(x):
    return x * _sigmoid(x)


def _fwd_small(c, c_ctx, ada_w, ada_b, conv_w):
    D = c.shape[1]
    Wm = ada_w.shape[1]
    Dq = conv_w.shape[1]

    def body(c_ref, cc_ref, aw_ref, ab_ref, cw_ref, act_ref, shx_ref, scx_ref, gx_ref, shc_ref, scc_ref, cwf_ref,
             mod_ref, cbuf, pmine, pbuf, wbuf, s_c, r_c, s_p, r_p, s_w, r_w):
        pos = _position()
        me, s = _dev_id(pos), _shard_of(pos)
        cbuf[me] = c_ref[...]
        wbuf[s] = cw_ref[...]
        sends = [_remote(c_ref, cbuf.at[me], s_c, r_c, k - 1, _peer(pos, k)) for k in range(1, 8)]
        sends += [_remote(cw_ref, wbuf.at[s], s_w, r_w, j, _peer(pos, k)) for j, k in enumerate(CHIP_FLIPS)]
        for cp in sends:
            cp.start()
        for k in range(1, 8):
            _remote(c_ref, cbuf.at[_dev_id(_peer(pos, k))], s_c, r_c, k - 1, _peer(pos, k)).wait_recv()
        for d in range(N_DEV):
            act_ref[d:d + 1, :] = _silu(cbuf[d])
        act_ref[8:9, :] = _silu(cc_ref[...])
        act_ref[9:16, :] = jnp.zeros((7, D), F32)
        part = _dot_f32(act_ref[...], aw_ref[...])
        pmine[...] = part
        pbuf[s] = part
        psend = [_remote(pmine, pbuf.at[s], s_p, r_p, j, _peer(pos, k)) for j, k in enumerate(CHIP_FLIPS)]
        for cp in psend:
            cp.start()
        for j, k in enumerate(CHIP_FLIPS):
            t = _shard_of(_peer(pos, k))
            _remote(pmine, pbuf.at[t], s_p, r_p, j, _peer(pos, k)).wait_recv()
            _remote(cw_ref, wbuf.at[t], s_w, r_w, j, _peer(pos, k)).wait_recv()
        cwf_ref[...] = jnp.zeros_like(cwf_ref)
        for t in range(N_SHARD):
            mod_ref[:, t * Wm:(t + 1) * Wm] = pbuf[t] + ab_ref[:, t * Wm:(t + 1) * Wm]
            cwf_ref[0:3, t * Dq:(t + 1) * Dq] = wbuf[t]
        for r, o_ref in enumerate((shx_ref, scx_ref, gx_ref)):
            o_ref[...] = mod_ref[pl.ds(me, 1), r * D:(r + 1) * D]
        for r, o_ref in enumerate((shc_ref, scc_ref)):
            o_ref[...] = mod_ref[8:9, r * D:(r + 1) * D]
        for cp in sends + psend:
            cp.wait_send()

    row = jax.ShapeDtypeStruct((1, D), F32)
    res = pl.pallas_call(
        body, name="fwd_small",
        in_specs=[VMEM_FULL] * 5, out_specs=[VMEM_FULL] * 7,
        out_shape=[jax.ShapeDtypeStruct((16, D), F32)] + [row] * 5 + [jax.ShapeDtypeStruct((8, D), F32)],
        scratch_shapes=[pltpu.VMEM((16, 3 * D), F32), pltpu.VMEM((N_DEV, 1, D), F32), pltpu.VMEM((16, Wm), F32),
                        pltpu.VMEM((N_SHARD, 16, Wm), F32), pltpu.VMEM((N_SHARD, 3, Dq), F32),
                        pltpu.SemaphoreType.DMA((7,)), pltpu.SemaphoreType.DMA((7,)),
                        pltpu.SemaphoreType.DMA((3,)), pltpu.SemaphoreType.DMA((3,)),
                        pltpu.SemaphoreType.DMA((3,)), pltpu.SemaphoreType.DMA((3,))],
        compiler_params=_cparams(None, VMEM_LIMIT),
    )(c, c_ctx, ada_w, ada_b, conv_w)
    return res[0], tuple(res[1:4]), tuple(res[4:6]), res[6]


AG_CHUNKS = 3


def _ag_in_proj(xm, w_in_s, w3_s, c2, s2):
    L, D = xm.shape
    Wc = w_in_s.shape[1]
    Wq = Wc // AG_CHUNKS
    Dh = D // 2
    Do = w3_s[0].shape[1]
    TM = min(1024, L // 4)
    NT = L // TM
    NQ = AG_CHUNKS
    order = [(q, j) for q in range(NQ) for j in (0, 1)] + [(q, 2) for q in range(NQ)]

    def body(xm_ref, wi_hbm, wa_ref, wb_ref, wo_ref, c_ref, s_ref, p_hbm, qr_hbm, kr_hbm, fi_hbm, f3_hbm,
             w_vm, s3, stage, qk_stage, ici_s, ici_r, d2d_s, d2d_r, w3_s_, w3_r_, fw_s, fw_r,
             loc, out_sem, qk_sem):
        pos = _position()
        c = pos[2]
        s = _shard_of(pos)
        sib = _peer(pos, 1)
        mine = pl.ds(pl.multiple_of(c * Dh, Dh), Dh)
        other = pl.ds(pl.multiple_of((1 - c) * Dh, Dh), Dh)

        def abs_col(t, q):
            return pl.ds(pl.multiple_of(t * Wc + q * Wq, 128), Wq)

        own = [pltpu.make_async_copy(wi_hbm.at[:, q * Wq:(q + 1) * Wq], w_vm.at[0, q], loc.at[2 + 4 * NQ + q])
               for q in range(NQ)]
        for cp in own:
            cp.start()
        for cp in own:
            cp.wait()
        sends = [_remote(w_vm.at[0, q, mine, :], w_vm.at[1 + j, q, mine, :], ici_s, ici_r, q * 3 + j,
                         _peer(pos, CHIP_FLIPS[j])) for q, j in order if j < 2]
        for cp in sends:
            cp.start()
        for a, w_ref in enumerate((wa_ref, wb_ref, wo_ref)):
            s3[a] = w_ref[...].astype(BF16)
        w3_sends = [_remote(s3.at[:, c], f3_hbm.at[:, s, c], w3_s_, w3_r_, j, _peer(pos, k))
                    for j, k in enumerate(CHIP_FLIPS)]
        local = [pltpu.make_async_copy(s3, f3_hbm.at[:, s], loc.at[1])]
        local += [pltpu.make_async_copy(w_vm.at[0, q], fi_hbm.at[:, abs_col(s, q)], loc.at[2 + q]) for q in range(NQ)]
        for cp in local:
            cp.start()

        def out_copy(slot, rows, cols):
            return pltpu.make_async_copy(stage.at[slot], p_hbm.at[rows, cols], out_sem.at[slot])

        def block(r, q, t, first):
            cols = abs_col(t, q)

            def row_tile(rt, carry):
                rows = pl.ds(pl.multiple_of(rt * TM, TM), TM)
                acc = _dot(xm_ref[rows, :], w_vm[r, q])
                slot = lax.rem(rt, 2)

                @pl.when(rt >= 2 if first else rt >= 0)
                def _():
                    out_copy(slot, rows, cols).wait()

                stage[slot] = acc.astype(BF16)
                out_copy(slot, rows, cols).start()

                def rotary(lo, scale, dst_hbm):
                    c, s = c_ref[rows, :], s_ref[rows, :]
                    for pr in range(Dh // 128):
                        tq = acc[:, lo + pr * 128:lo + (pr + 1) * 128] * scale
                        qk_stage[:, pr * 128:(pr + 1) * 128] = (tq * c + _swap_halves(tq) * s).astype(BF16)
                    cp = pltpu.make_async_copy(qk_stage, dst_hbm.at[rows, :], qk_sem)
                    cp.start()
                    cp.wait()

                if q == NQ - 1:
                    @pl.when(t == 1)
                    def _():
                        rotary(Wq - Dh, 1.0, qr_hbm)
                if q == 0:
                    @pl.when(t == 2)
                    def _():
                        rotary(0, K_SCALE, kr_hbm)
                return carry

            lax.fori_loop(0, NT, row_tile, 0)

        passed = []

        def hand_on(q, j):
            half = w_vm.at[1 + j, q, mine, :]
            if j == 2:
                _remote(half, half, fw_s, fw_r, q, sib).wait_recv()
            else:
                _remote(half, half, ici_s, ici_r, q * 3 + j, sib).wait_recv()

                @pl.when(c == (0 if j == q % 2 else 1))
                def _():
                    _remote(half, w_vm.at[3, q, mine, :], fw_s, fw_r, q, _peer(pos, CHIP_FLIPS[1 - j])).start()
            fwd = _remote(half, half, d2d_s, d2d_r, q * 3 + j, sib)
            fwd.start()
            passed.append(fwd)

        for q in range(NQ):
            if q == NQ - 1:
                hand_on(*order[0])
            block(0, q, s, q == 0)
        for n, (q, j) in enumerate(order):
            r, idx = 1 + j, q * 3 + j
            t = _shard_of(_peer(pos, CHIP_FLIPS[j]))
            if n + 1 < len(order):
                hand_on(*order[n + 1])
            if n + 1 == 2 * NQ - 1:
                for cp in w3_sends:
                    cp.start()
            _remote(w_vm.at[r, q, other, :], w_vm.at[r, q, other, :], d2d_s, d2d_r, idx, sib).wait_recv()
            block(r, q, t, False)
            cp = pltpu.make_async_copy(w_vm.at[r, q], fi_hbm.at[:, abs_col(t, q)], loc.at[2 + NQ + idx])
            cp.start()
            local.append(cp)
        for j, k in enumerate(CHIP_FLIPS):
            t = _shard_of(_peer(pos, k))
            _remote(s3.at[:, c], f3_hbm.at[:, t, c], w3_s_, w3_r_, j, sib).wait_recv()
            fwd = _remote(f3_hbm.at[:, t, c], f3_hbm.at[:, t, c], w3_s_, w3_r_, 3 + j, sib)
            fwd.start()
            passed.append(fwd)
        for j, k in enumerate(CHIP_FLIPS):
            t = _shard_of(_peer(pos, k))
            _remote(s3.at[:, c], f3_hbm.at[:, t, 1 - c], w3_s_, w3_r_, 3 + j, sib).wait_recv()
        for cp in sends + w3_sends + passed:
            cp.wait_send()
        for q in range(NQ):
            _remote(w_vm.at[1, q, mine, :], w_vm.at[3, q, mine, :], fw_s, fw_r, q, sib).wait_send()
        for cp in local:
            cp.wait()
        for slot in range(2):
            out_copy(slot, pl.ds(0, TM), abs_col(s, 0)).wait()

    n_loc = 2 + 5 * NQ
    return pl.pallas_call(
        body, name="ag_in_proj",
        in_specs=[VMEM_FULL, ANY, VMEM_FULL, VMEM_FULL, VMEM_FULL, VMEM_FULL, VMEM_FULL], out_specs=[ANY] * 5,
        out_shape=[jax.ShapeDtypeStruct((L, N_SHARD * Wc), BF16),
                   jax.ShapeDtypeStruct((L, Dh), BF16), jax.ShapeDtypeStruct((L, Dh), BF16),
                   jax.ShapeDtypeStruct((D, N_SHARD * Wc), BF16), jax.ShapeDtypeStruct((3, N_SHARD, 2, Do, D), BF16)],
        scratch_shapes=[pltpu.VMEM((N_SHARD, NQ, D, Wq), BF16), pltpu.VMEM((3, 2, Do, D), BF16), pltpu.VMEM((2, TM, Wq), BF16), pltpu.VMEM((TM, Dh), BF16),
                        pltpu.SemaphoreType.DMA((3 * NQ,)), pltpu.SemaphoreType.DMA((3 * NQ,)),
                        pltpu.SemaphoreType.DMA((3 * NQ,)), pltpu.SemaphoreType.DMA((3 * NQ,)),
                        pltpu.SemaphoreType.DMA((6,)), pltpu.SemaphoreType.DMA((6,)),
                        pltpu.SemaphoreType.DMA((NQ,)), pltpu.SemaphoreType.DMA((NQ,)),
                        pltpu.SemaphoreType.DMA((n_loc,)), pltpu.SemaphoreType.DMA((2,)), pltpu.SemaphoreType.DMA],
        compiler_params=_cparams(None, VMEM_LIMIT),
    )(xm, w_in_s, *w3_s, c2, s2)


def _pair_exchange_w3(dw3):
    _, _, _, Do, D = dw3.shape

    def build(ins, outs, send, recv):
        pos = _position()
        return [_remote(ins[0].at[:, :, 1 - pos[2]], outs[0], send, recv, 0, _peer(pos, 1))]

    return _Exchange((dw3,), (jax.ShapeDtypeStruct((3, N_SHARD, Do, D), F32),), 1, build)


def _sum_pair_in(dw_mine, ri):
    Dh, Wf = dw_mine.shape
    Wc = Wf // N_SHARD
    tr = min(256, Dh)

    def body(a_ref, b_ref, o_ref):
        o_ref[...] = (a_ref[...] + b_ref[...]).astype(BF16)

    return pl.pallas_call(
        body, name="sum_pair_in", grid=(Dh // tr, N_SHARD),
        in_specs=[pl.BlockSpec((tr, Wc), lambda i, t: (i, t)), pl.BlockSpec((tr, Wc), lambda i, t: (i, t))],
        out_specs=pl.BlockSpec((None, tr, Wc), lambda i, t: (t, i, 0)),
        out_shape=jax.ShapeDtypeStruct((N_SHARD, Dh, Wc), BF16),
        compiler_params=_cparams(("parallel", "parallel")),
    )(dw_mine, ri)


def _sum_pair_w3(cidx, dw3, r3):
    _, _, _, Do, D = dw3.shape

    def body(c_ref, a_ref, b_ref, o_ref):
        o_ref[...] = (a_ref[...] + b_ref[...]).astype(BF16)

    return pl.pallas_call(
        body, name="sum_pair_w3",
        grid_spec=pltpu.PrefetchScalarGridSpec(
            num_scalar_prefetch=1, grid=(3,),
            in_specs=[pl.BlockSpec((None, N_SHARD, None, Do, D), lambda a, c: (a, 0, c[0], 0, 0)),
                      pl.BlockSpec((None, N_SHARD, Do, D), lambda a, c: (a, 0, 0, 0))],
            out_specs=pl.BlockSpec((None, N_SHARD, Do, D), lambda a, c: (a, 0, 0, 0))),
        out_shape=jax.ShapeDtypeStruct((3, N_SHARD, Do, D), BF16),
        compiler_params=_cparams(("parallel",)),
    )(cidx, dw3, r3)


def _chips_exchange_in(cs_in):
    _, Dh, Wc = cs_in.shape

    def build(ins, outs, send, recv):
        pos = _position()
        return [_remote(ins[0].at[_shard_of(_peer(pos, k))], outs[0].at[j], send, recv, j, _peer(pos, k))
                for j, k in enumerate(CHIP_FLIPS)]

    return _Exchange((cs_in,), (jax.ShapeDtypeStruct((3, Dh, Wc), BF16),), 3, build)


def _chips_exchange_w3(cs_3):
    _, _, Do, D = cs_3.shape

    def build(ins, outs, send, recv):
        pos = _position()
        return [_remote(ins[0].at[:, _shard_of(_peer(pos, k))], outs[0].at[j], send, recv, j, _peer(pos, k))
                for j, k in enumerate(CHIP_FLIPS)]

    return _Exchange((cs_3,), (jax.ShapeDtypeStruct((3, 3, Do, D), BF16),), 3, build)


def _sum_chips_in(csidx, cs_in, rb_in):
    _, Dh, Wc = cs_in.shape
    tr = min(256, Dh)

    def body(s_ref, a_ref, b_ref, o_ref):
        acc = a_ref[...].astype(F32)
        for j in range(3):
            acc = acc + b_ref[j].astype(F32)
        o_ref[...] = acc

    return pl.pallas_call(
        body, name="sum_chips_in",
        grid_spec=pltpu.PrefetchScalarGridSpec(
            num_scalar_prefetch=1, grid=(Dh // tr,),
            in_specs=[pl.BlockSpec((None, tr, Wc), lambda i, s: (s[1], i, 0)),
                      pl.BlockSpec((3, tr, Wc), lambda i, s: (0, i, 0))],
            out_specs=pl.BlockSpec((None, tr, Wc), lambda i, s: (s[0], i, 0))),
        out_shape=jax.ShapeDtypeStruct((2, Dh, Wc), F32),
        compiler_params=_cparams(("parallel",)),
    )(csidx, cs_in, rb_in)


def _sum_chips_w3(csidx, cs_3, rb_3):
    _, _, Do, D = cs_3.shape

    def body(s_ref, a_ref, b_ref, o_ref):
        acc = a_ref[...].astype(F32)
        for j in range(3):
            acc = acc + b_ref[j].astype(F32)
        o_ref[...] = acc

    return pl.pallas_call(
        body, name="sum_chips_w3",
        grid_spec=pltpu.PrefetchScalarGridSpec(
            num_scalar_prefetch=1, grid=(3,),
            in_specs=[pl.BlockSpec((None, None, Do, D), lambda a, s: (a, s[1], 0, 0)),
                      pl.BlockSpec((3, None, Do, D), lambda a, s: (0, a, 0, 0))],
            out_specs=pl.BlockSpec((None, None, Do, D), lambda a, s: (a, s[0], 0, 0))),
        out_shape=jax.ShapeDtypeStruct((3, 2, Do, D), F32),
        compiler_params=_cparams(("parallel",)),
    )(csidx, cs_3, rb_3)


def _adam_math(w, g, m, v):
    m = ADAM_B1 * m + (1.0 - ADAM_B1) * g
    v = ADAM_B2 * v + (1.0 - ADAM_B2) * (g * g)
    m_hat = m / (1.0 - ADAM_B1 ** ADAM_STEP)
    v_hat = v / (1.0 - ADAM_B2 ** ADAM_STEP)
    delta = -ADAM_LR * (m_hat / (jnp.sqrt(v_hat) + ADAM_EPS) + ADAM_WD * w)
    return delta, m, v


def _adamw(w, g, m, v):
    R, C = w.shape
    tr = min(128, R)

    def body(w_ref, g_ref, m_ref, v_ref, d_ref, nm_ref, nv_ref):
        d_ref[...], nm_ref[...], nv_ref[...] = _adam_math(w_ref[...], g_ref[...], m_ref[...], v_ref[...])

    blk = pl.BlockSpec((tr, C), lambda i: (i, 0))
    return pl.pallas_call(
        body, name="adamw_w_in", grid=(R // tr,), in_specs=[blk] * 4, out_specs=[blk] * 3,
        out_shape=[jax.ShapeDtypeStruct((R, C), F32)] * 3,
        compiler_params=_cparams(("parallel",), VMEM_LIMIT),
    )(w, g, m, v)


SMALL_ROWS = ("c_ctx", "norm_w", "conv_b", "gn_w", "final_norm_w")


def _bwd_small(stats, ada_w, Dq, gh_in, gh_3):
    D = stats[0].shape[1]
    Wm = ada_w.shape[1]

    def body(stx, stm, stc, stv, stl, stlc, aw_ref, gi_in, g3_in, tot_ref, dm_sh, gcw, da_ref, gi_ref, g3_ref,
             vec_ref, vbuf, dm, amine, abuf, s_v, r_v, s_a, r_a, s_g, r_g):
        pos = _position()
        me, s = _dev_id(pos), _shard_of(pos)
        c, sib = pos[2], _peer(pos, 1)
        halves = [_remote(gi_in.at[c], gi_ref.at[c], s_g, r_g, 0, sib),
                  _remote(g3_in.at[:, c], g3_ref.at[:, c], s_g, r_g, 1, sib)]
        for cp in halves:
            cp.start()
        vec_ref[...] = jnp.zeros_like(vec_ref)
        vec_ref[0:2, :] = stx[0:2, :]
        vec_ref[2:3, :] = stm[1:2, :]
        vec_ref[3:5, :] = stc[0:2, :]
        vec_ref[5:6, :] = stx[2:3, :] + stc[2:3, :]
        vec_ref[6:7, :] = stv[3:4, :]
        vec_ref[7:8, :] = stm[3:4, :]
        vec_ref[8:9, :] = stm[0:1, :]
        vec_ref[9:12, :] = stv[0:3, :]
        vec_ref[12:14, 0:128] = stl[0:2, :] + stlc[0:2, :]
        vec_ref[14:15, :] = stm[2:3, :]
        vbuf[me] = vec_ref[...]
        sends = [_remote(vec_ref, vbuf.at[me], s_v, r_v, k - 1, _peer(pos, k)) for k in range(1, 8)]
        for cp in sends:
            cp.start()
        for k in range(1, 8):
            _remote(vec_ref, vbuf.at[_dev_id(_peer(pos, k))], s_v, r_v, k - 1, _peer(pos, k)).wait_recv()
        tot = vbuf[0]
        for d in range(1, N_DEV):
            tot = tot + vbuf[d]
        dm[...] = jnp.zeros_like(dm)
        for d in range(N_DEV):
            for r in range(3):
                dm[d:d + 1, r * D:(r + 1) * D] = vbuf[d, r:r + 1, :]
        dm[8:9, 0:D] = tot[3:4, :]
        dm[8:9, D:2 * D] = tot[4:5, :]
        for t in range(N_SHARD):
            @pl.when(s == t)
            def _(t=t):
                dm_sh[...] = dm[:, t * Wm:(t + 1) * Wm]
                gcw[...] = tot[9:12, t * Dq:(t + 1) * Dq]
        tot_ref[...] = tot
        part = lax.dot_general(dm_sh[8:16, :], aw_ref[...], (((1,), (1,)), ((), ())),
                               precision=lax.Precision.HIGHEST, preferred_element_type=F32)
        amine[...] = part
        abuf[s] = part
        asend = [_remote(amine, abuf.at[s], s_a, r_a, j, _peer(pos, k)) for j, k in enumerate(CHIP_FLIPS)]
        for cp in asend:
            cp.start()
        for j, k in enumerate(CHIP_FLIPS):
            _remote(amine, abuf.at[_shard_of(_peer(pos, k))], s_a, r_a, j, _peer(pos, k)).wait_recv()
        da = abuf[0]
        for t in range(1, N_SHARD):
            da = da + abuf[t]
        da_ref[...] = da
        _remote(gi_in.at[1 - c], gi_ref.at[1 - c], s_g, r_g, 0, sib).wait_recv()
        _remote(g3_in.at[:, 1 - c], g3_ref.at[:, 1 - c], s_g, r_g, 1, sib).wait_recv()
        for cp in sends + asend + halves:
            cp.wait_send()

    row = lambda *shape: jax.ShapeDtypeStruct(shape, F32)
    return pl.pallas_call(
        body, name="bwd_small",
        in_specs=[VMEM_FULL] * 7 + [ANY, ANY], out_specs=[VMEM_FULL] * 4 + [ANY, ANY],
        input_output_aliases={7: 4, 8: 5},
        out_shape=[row(16, D), row(16, Wm), row(3, Dq), row(8, D), row(*gh_in.shape), row(*gh_3.shape)],
        scratch_shapes=[pltpu.VMEM((16, D), F32), pltpu.VMEM((N_DEV, 16, D), F32), pltpu.VMEM((16, 3 * D), F32),
                        pltpu.VMEM((8, D), F32), pltpu.VMEM((N_SHARD, 8, D), F32),
                        pltpu.SemaphoreType.DMA((7,)), pltpu.SemaphoreType.DMA((7,)),
                        pltpu.SemaphoreType.DMA((3,)), pltpu.SemaphoreType.DMA((3,)),
                        pltpu.SemaphoreType.DMA((2,)), pltpu.SemaphoreType.DMA((2,))],
        compiler_params=_cparams(None, VMEM_LIMIT),
    )(*stats, ada_w, gh_in, gh_3)


ADAM_SLAB = 16


def _small_update(tot, dm_sh, gcw, da, act, g3, rows, ab, cw, dl, aw, w3):
    D = act.shape[1]
    Wm = dm_sh.shape[1]
    Dq = gcw.shape[1]
    H = dl[0].shape[1]
    params = tuple(rows) + (ab, cw, dl)
    shards = (aw,) + tuple(w3)
    n_p, n_s = len(params), len(shards)
    whole = (slice(None), slice(None))

    def body(tot_ref, dm_ref, gcw_ref, da_ref, act_ref, g3_hbm, *refs):
        wmv = [refs[3 * k:3 * k + 3] for k in range(n_p + n_s)]
        refs = refs[3 * (n_p + n_s):]
        gaw_ref, loss_ref = refs[0:2]
        outs, shard_outs = refs[2:2 + 4 * n_p], refs[2 + 4 * n_p:2 + 4 * n_p + 3 * n_s]
        g3_vm, *bufs, ld_sem, st_sem = refs[2 + 4 * n_p + 3 * n_s:]
        in_vm, out_vm = bufs[0:3 * n_s], bufs[3 * n_s:]
        o_q = [outs[n_p * q:n_p * (q + 1)] for q in range(4)]
        loads = [pltpu.make_async_copy(wmv[n_p + k][j], in_vm[3 * k + j], ld_sem.at[3 * k + j])
                 for k in range(n_s) for j in range(3)]
        load_g3 = pltpu.make_async_copy(g3_hbm, g3_vm, ld_sem.at[3 * n_s])
        for cp in loads + [load_g3]:
            cp.start()
        tot = tot_ref[...]
        gaw_ref[...] = lax.dot_general(act_ref[...], dm_ref[...], (((0,), (0,)), ((), ())),
                                       precision=lax.Precision.HIGHEST, preferred_element_type=F32)
        loss_ref[...] = (0.5 / D) * _sum_all(tot[14:15, :])
        cc = wmv[0][0][...]
        sg = _sigmoid(cc)
        g_cctx = da_ref[0:1, :] * (sg * (1.0 + cc * (1.0 - sg)))

        def emit(k, g, at=whole):
            w_ref, m_ref, v_ref = wmv[k]
            for q, val in enumerate((g,) + _adam_math(w_ref[at], g, m_ref[at], v_ref[at])):
                o_q[q][k][at] = val

        for k, g in enumerate([g_cctx, tot[5:6, :], tot[6:7, :], tot[7:8, :], tot[8:9, :]]):
            emit(k, g)
        for r, g in enumerate([tot[0:1, :] + tot[3:4, :], tot[1:2, :] + tot[4:5, :], tot[2:3, :]]):
            emit(n_p - 3, g, (slice(0, 1), slice(r * D, (r + 1) * D)))
        emit(n_p - 2, gcw_ref[...])
        emit(n_p - 1, tot[12:14, 0:H] * _sigmoid(-wmv[n_p - 1][0][...]))

        def shard_step(k, g_ref):
            w_ref, m_ref, v_ref = in_vm[3 * k:3 * k + 3]
            for cp in loads[3 * k:3 * k + 3]:
                cp.wait()

            def slab(i, carry):
                sl = pl.ds(pl.multiple_of(i * ADAM_SLAB, ADAM_SLAB), ADAM_SLAB)
                res = _adam_math(w_ref[sl, :], g_ref[sl, :], m_ref[sl, :], v_ref[sl, :])
                for o_ref, val in zip(out_vm[3 * k:3 * k + 3], res):
                    o_ref[sl, :] = val
                return carry

            lax.fori_loop(0, w_ref.shape[0] // ADAM_SLAB, slab, 0)
            stores = [pltpu.make_async_copy(out_vm[3 * k + j], shard_outs[3 * k + j], st_sem.at[3 * k + j])
                      for j in range(3)]
            for cp in stores:
                cp.start()
            return stores

        stores = shard_step(0, gaw_ref)
        load_g3.wait()
        for a in range(n_s - 1):
            stores += shard_step(1 + a, g3_vm.at[a])
        for cp in stores:
            cp.wait()

    row = lambda *shape: jax.ShapeDtypeStruct(shape, F32)
    per_q = [row(1, D)] * len(rows) + [row(1, 3 * D), row(3, Dq), row(2, H)]
    per_shard = [row(*s[0].shape) for s in shards for _ in range(3)]
    held = [pltpu.VMEM(s[0].shape, F32) for s in shards for _ in range(3)]
    res = pl.pallas_call(
        body, name="small_update",
        in_specs=[VMEM_FULL] * 5 + [ANY] + [VMEM_FULL] * (3 * n_p) + [ANY] * (3 * n_s),
        out_specs=[VMEM_FULL] * (2 + 4 * n_p) + [ANY] * (3 * n_s),
        out_shape=[row(D, Wm), row(1, 1)] + per_q * 4 + per_shard,
        scratch_shapes=[pltpu.VMEM(g3.shape, F32)] + held + held
                       + [pltpu.SemaphoreType.DMA((3 * n_s + 1,)), pltpu.SemaphoreType.DMA((3 * n_s,))],
        compiler_params=_cparams(None, VMEM_LIMIT),
    )(tot, dm_sh, gcw, da, act, g3, *[a for p in params + shards for a in p])
    k = 2 + 4 * n_p
    return (res[0], res[1], [res[2 + n_p * q:2 + n_p * (q + 1)] for q in range(4)],
            [res[k + 3 * s:k + 3 * s + 3] for s in range(n_s)])


def kernel(x, c, ctx, c_ctx, norm_w, ada_w, ada_b, w_in, conv_w, conv_b, decay_logit, gn_w, w_a, w_b, w_out, final_norm_w, loss_target, m_c_ctx, m_norm_w, m_ada_w, m_ada_b, m_w_in, m_conv_w, m_conv_b, m_decay_logit, m_gn_w, m_w_a, m_w_b, m_w_out, m_final_norm_w, v_c_ctx, v_norm_w, v_ada_w, v_ada_b, v_w_in, v_conv_w, v_conv_b, v_decay_logit, v_gn_w, v_w_a, v_w_b, v_w_out, v_final_norm_w):
    L, D = x.shape[1], x.shape[2]
    Wc = w_in.shape[2]
    Do = D // 8
    pos = _position()
    cidx = jnp.reshape(pos[2], (1,)).astype(jnp.int32)
    sidx = jnp.reshape(_shard_of(pos), (1,)).astype(jnp.int32)

    act, mod_x, mod_c, conv_w8 = _fwd_small(c, c_ctx[None], ada_w[0], ada_b, conv_w[0])
    lg = jax.nn.log_sigmoid(decay_logit[0])

    w3_s = tuple(w[0].reshape(2, Do, D) for w in (w_a, w_b, w_out))

    def project(xm, c2, s2, w_in_bf):
        p, qr, kr, w_in_full, w3_full = _ag_in_proj(xm, w_in_bf, w3_s, c2, s2)
        return p, qr, kr, w_in_full, w3_full.reshape(3, D, D)

    csidx = jnp.concatenate([cidx, sidx])
    groups, dret_c, xmt, cmt, dx1, sc_x, w_in_full, gh_3, sts = _local_step(
        x[0], ctx[0], loss_target[0], mod_x, mod_c, norm_w, conv_w8, conv_b, lg, gn_w, final_norm_w[None],
        project, csidx, w_in[0])
    st_mid, st_conv, st_lg, st_lgc, st_c = sts

    dw_mine, ra_in = _dw_in(xmt, groups, cmt, dret_c, D, True)
    cs_in = _sum_pair_in(dw_mine, ra_in)
    grad_x, st_x, rb_in = _dxm(groups, 0, w_in_full, x[0], norm_w, sc_x, dx1, "dxm_x", _chips_exchange_in(cs_in))
    gh_in = _sum_chips_in(csidx, cs_in, rb_in)

    tot, dm_sh, gcw, da, g_in, g_3 = _bwd_small((st_x, st_mid, st_c, st_conv, st_lg, st_lgc), ada_w[0],
                                                conv_w.shape[2], gh_in, gh_3)
    g_w_in = g_in.reshape(D, Wc)
    g_3 = g_3.reshape(3, D // 4, D)
    rows = ((c_ctx[None], m_c_ctx[None], v_c_ctx[None]), (norm_w, m_norm_w, v_norm_w), (conv_b, m_conv_b, v_conv_b),
            (gn_w, m_gn_w, v_gn_w), (final_norm_w[None], m_final_norm_w[None], v_final_norm_w[None]))
    w3 = ((w_a[0], m_w_a[0], v_w_a[0]), (w_b[0], m_w_b[0], v_w_b[0]), (w_out[0], m_w_out[0], v_w_out[0]))
    g_ada_w, loss, small, (upd_ada, upd_a, upd_b, upd_o) = _small_update(
        tot, dm_sh, gcw, da, act, g_3, rows, (ada_b, m_ada_b, v_ada_b), (conv_w[0], m_conv_w[0], v_conv_w[0]),
        (decay_logit[0], m_decay_logit[0], v_decay_logit[0]), (ada_w[0], m_ada_w[0], v_ada_w[0]), w3)

    upd_in = _adamw(w_in[0], g_w_in, m_w_in[0], v_w_in[0])

    def leaves(q):
        big = lambda g, upd: (g if q == 0 else upd[q - 1])[None]
        r_cctx, r_norm, r_convb, r_gn, r_fnorm, r_ab, r_cw, r_dl = small[q]
        return [r_cctx.reshape(D), r_norm, big(g_ada_w, upd_ada), r_ab, big(g_w_in, upd_in),
                r_cw[None], r_convb, r_dl[None], r_gn,
                big(g_3[0], upd_a), big(g_3[1], upd_b), big(g_3[2], upd_o), r_fnorm.reshape(D)]

    return (loss.reshape(()), grad_x[None], *leaves(0), *leaves(1), *leaves(2), *leaves(3))
```

```python
from typing import Callable, NamedTuple

import jax
import jax.numpy as jnp
import numpy as np
from jax import lax
from jax.experimental import pallas as pl
from jax.experimental.pallas import tpu as pltpu

F32 = jnp.float32
BF16 = jnp.bfloat16
MESH = pl.DeviceIdType.MESH

CHUNK = 128
RET_CPB = 4
DV = 128
DK = 64
GRID_W = 64
ROPE_BASE = 10000.0
EPS = 1e-6
K_SCALE = DK ** -0.5
N_SHARD = 4
N_DEV = 8

ADAM_LR = 0.001
ADAM_B1 = 0.9
ADAM_B2 = 0.999
ADAM_EPS = 1e-08
ADAM_WD = 0.01
ADAM_STEP = 10

VMEM_LIMIT = 56 * 1024 * 1024


def _cparams(sem=None, vmem=None):
    kw = {}
    if sem is not None:
        kw["dimension_semantics"] = sem
    if vmem is not None:
        kw["vmem_limit_bytes"] = vmem
    return pltpu.CompilerParams(**kw)


def _dot(a, b):
    return jnp.dot(a, b, preferred_element_type=F32)


def _dot_nt(a, b):
    return lax.dot_general(a, b, (((1,), (1,)), ((), ())), preferred_element_type=F32)


def _dot_tn(a, b):
    return lax.dot_general(a, b, (((0,), (0,)), ((), ())), preferred_element_type=F32)


def _sigmoid(x):
    return 1.0 / (1.0 + jnp.exp(-x))


def _sum_all(x):
    return jnp.sum(jnp.sum(x, axis=1, keepdims=True), axis=0, keepdims=True)


def _swap_halves(t):
    n = t.shape[1]
    lane = lax.broadcasted_iota(jnp.int32, t.shape, 1)
    low = (lane & 32) == 0
    return jnp.where(low, pltpu.roll(t, n - 32, 1), pltpu.roll(t, 32, 1))


def _vec_spec(d):
    return pl.BlockSpec((1, d), lambda *a: (0, 0))


def _norm_mod(x, nw, sc, sh, name, also_bf16=None):
    L, D = x.shape
    tl = min(256, L)
    nt = L // tl

    def body(x_ref, nw_ref, sc_ref, sh_ref, *rest):
        xm_ref, xmt_ref = rest[-3:-1] if also_bf16 is not None else rest
        xv = x_ref[...]
        r = lax.rsqrt(jnp.mean(xv * xv, axis=-1, keepdims=True) + EPS)
        xm = (xv * r * nw_ref[...]) * (1.0 + sc_ref[...]) + sh_ref[...]
        xm_b = xm.astype(BF16)
        xm_ref[...] = xm_b
        xmt_ref[...] = xm_b.T
        if also_bf16 is not None:
            rest[-1][...] = rest[0][...].astype(BF16)

    in_specs = [pl.BlockSpec((tl, D), lambda i: (i, 0)), _vec_spec(D), _vec_spec(D), _vec_spec(D)]
    out_specs = [pl.BlockSpec((tl, D), lambda i: (i, 0)), pl.BlockSpec((D, tl), lambda i: (0, i))]
    out_shape = [jax.ShapeDtypeStruct((L, D), BF16), jax.ShapeDtypeStruct((D, L), BF16)]
    args = [x, nw, sc, sh]
    if also_bf16 is not None:
        R, C = also_bf16.shape
        slab = pl.BlockSpec((R // nt, C), lambda i: (i, 0))
        in_specs.append(slab)
        out_specs.append(slab)
        out_shape.append(jax.ShapeDtypeStruct((R, C), BF16))
        args.append(also_bf16)
    return pl.pallas_call(
        body, name=name, grid=(nt,), in_specs=in_specs, out_specs=out_specs, out_shape=out_shape,
        compiler_params=_cparams(("parallel",)),
    )(*args)


QK_BLOCK, V_BLOCK = 4, 5


def _in_proj(xm, w, name, first=0, count=None):
    M, D = xm.shape
    count = w.shape[1] // D if count is None else count
    tm = min(1024, M)

    def body(a_ref, b_ref, o_ref, qk_ref):
        acc = _dot(a_ref[...], b_ref[...])
        o_ref[...] = acc.astype(o_ref.dtype)

        @pl.when(pl.program_id(1) == QK_BLOCK - first)
        def _():
            qk_ref[...] = acc

    return pl.pallas_call(
        body, name=name, grid=(M // tm, count),
        in_specs=[pl.BlockSpec((tm, D), lambda i, j: (i, 0)), pl.BlockSpec((D, D), lambda i, j: (0, first + j))],
        out_specs=[pl.BlockSpec((tm, D), lambda i, j: (i, j)), pl.BlockSpec((tm, D), lambda i, j: (i, 0))],
        out_shape=[jax.ShapeDtypeStruct((M, count * D), BF16), jax.ShapeDtypeStruct((M, D), F32)],
        compiler_params=_cparams(("parallel", "arbitrary")),
    )(xm, w)


def _halo_specs(tl, L, D, col):
    hb = tl // 16
    last = L // 16 - 1
    prev = pl.BlockSpec((16, D), lambda i: (jnp.maximum(i * hb - 1, 0), col))
    nxt = pl.BlockSpec((16, D), lambda i: (jnp.minimum((i + 1) * hb, last), col))
    return prev, nxt


def _shift_rows(u, above, below):
    tl = u.shape[0]
    row = lax.broadcasted_iota(jnp.int32, u.shape, 0)
    dn = jnp.where(row == 0, above, pltpu.roll(u, 1, 0))
    up = jnp.where(row == tl - 1, below, pltpu.roll(u, tl - 1, 0))
    return dn, up


def _rope_tables(L):
    pos = np.arange(L)
    row = (pos // GRID_W).astype(np.float32)
    col = (pos % GRID_W).astype(np.float32)
    nf = DK // 4
    inv = np.float32(ROPE_BASE) ** (-np.arange(nf, dtype=np.float32) / np.float32(nf))
    ang = np.concatenate([row[:, None] * inv, col[:, None] * inv], axis=-1).astype(np.float32)
    cos, sin = np.cos(ang), np.sin(ang)
    return (jnp.asarray(np.concatenate([cos, cos, cos, cos], axis=-1), F32),
            jnp.asarray(np.concatenate([-sin, sin, -sin, sin], axis=-1), F32))


def _smem_spec():
    return pl.BlockSpec(memory_space=pltpu.SMEM)


def _pair_select(e0, e1):
    row = lax.broadcasted_iota(jnp.int32, e0.shape, 0)
    return jnp.where(row < DK, e0, e1)


def _head_lane_mask(shape, e):
    lane = lax.broadcasted_iota(jnp.int32, shape, 1)
    return (lane < DK) if e == 0 else (lane >= DK)


def _ctx_states(pc, pqk_c, lg, D):
    Lc = pc.shape[0]
    H = D // DV

    def body(lg_ref, k_ref, v_ref, s_ref):
        m = lax.broadcasted_iota(jnp.int32, (Lc, DV), 0).astype(F32)
        for pr in range(H // 2):
            k2 = k_ref[:, pr * 128:(pr + 1) * 128].astype(F32) * K_SCALE
            res = [[None, None], [None, None]]
            for e in range(2):
                h = 2 * pr + e
                v = v_ref[:, h * DV:(h + 1) * DV]
                dec_f = jnp.exp(lg_ref[0, h] * (Lc - 1.0 - m))
                dec_b = jnp.exp(lg_ref[1, h] * m)
                res[0][e] = _dot_tn((k2 * dec_f).astype(BF16), v)
                res[1][e] = _dot_tn((k2 * dec_b).astype(BF16), v)
            s_ref[0, pr] = _pair_select(res[0][0], res[0][1])
            s_ref[1, pr] = _pair_select(res[1][0], res[1][1])

    return pl.pallas_call(
        body, name="ctx_states", grid=(1,),
        in_specs=[_smem_spec(), pl.BlockSpec((Lc, D // 2), lambda i: (0, 1)), pl.BlockSpec((Lc, D), lambda i: (0, 1))],
        out_specs=pl.BlockSpec((2, H // 2, 128, 128), lambda i: (0, 0, 0, 0)),
        out_shape=jax.ShapeDtypeStruct((2, H // 2, 128, 128), F32),
    )(lg, pqk_c, pc)


T_M, T_MT = 0, 1
T_MF1, T_MB1 = 2, 3
T_QF, T_QB = 4, 5
T_KF, T_KB = 6, 7


def _decay_tables(lg, H):
    def body(lg_ref, t_ref):
        h = pl.program_id(0)
        lgf, lgb = lg_ref[0, h], lg_ref[1, h]
        i = lax.broadcasted_iota(jnp.int32, (CHUNK, CHUNK), 0).astype(F32)
        j = lax.broadcasted_iota(jnp.int32, (CHUNK, CHUNK), 1).astype(F32)
        d = i - j
        mf = jnp.where(d > 0, jnp.exp(lgf * jnp.maximum(d, 0.0)), 0.0)
        mb = jnp.where(d < 0, jnp.exp(lgb * jnp.maximum(-d, 0.0)), 0.0)
        mf_t = jnp.where(d < 0, jnp.exp(lgf * jnp.maximum(-d, 0.0)), 0.0)
        mb_t = jnp.where(d > 0, jnp.exp(lgb * jnp.maximum(d, 0.0)), 0.0)
        diag = jnp.where(d == 0, 2.0, 0.0)
        t_ref[0, T_M] = mf + mb + diag
        t_ref[0, T_MT] = mf_t + mb_t + diag
        t_ref[0, T_MF1] = mf * d
        t_ref[0, T_MB1] = mb * (-d)
        t_ref[0, T_QF] = jnp.exp(lgf * (i + 1.0))
        t_ref[0, T_QB] = jnp.exp(lgb * (CHUNK - i))
        t_ref[0, T_KF] = jnp.exp(lgf * (CHUNK - 1.0 - i))
        t_ref[0, T_KB] = jnp.exp(lgb * i)

    return pl.pallas_call(
        body, name="decay_tables", grid=(H,), in_specs=[_smem_spec()],
        out_specs=pl.BlockSpec((1, 8, CHUNK, CHUNK), lambda h: (h, 0, 0, 0)),
        out_shape=jax.ShapeDtypeStruct((H, 8, CHUNK, CHUNK), F32),
    )(lg)


def _tab_spec(H):
    return pl.BlockSpec((H, 8, CHUNK, CHUNK), lambda n: (0, 0, 0, 0))


def _chunk_decay(tab_ref, h):
    return tab_ref[h, T_QF, CHUNK - 1:CHUNK, :], tab_ref[h, T_QB, 0:1, :]


def _ret_states(kr, p, s0, tab, D):
    L = kr.shape[0]
    H = D // DV
    N = L // CHUNK
    HP = H // 2

    def body(tab_ref, kf_ref, kb_ref, vf_ref, vb_ref, s0_ref, sf_out, sb_out, sf, sb):
        n = pl.program_id(0)

        @pl.when(n == 0)
        def _():
            sf[...] = s0_ref[0]
            sb[...] = s0_ref[1]

        for cc in range(RET_CPB):
            cf_, cb_ = cc, RET_CPB - 1 - cc
            rf, rb = slice(cf_ * CHUNK, (cf_ + 1) * CHUNK), slice(cb_ * CHUNK, (cb_ + 1) * CHUNK)
            sf_out[cf_] = sf[...]
            sb_out[cb_] = sb[...]
            for pr in range(HP):
                kf2 = kf_ref[rf, pr * 128:(pr + 1) * 128].astype(F32)
                kb2 = kb_ref[rb, pr * 128:(pr + 1) * 128].astype(F32)
                inc_f, inc_b, gf, gb = [], [], [], []
                for e in range(2):
                    h = 2 * pr + e
                    inc_f.append(_dot_tn((kf2 * tab_ref[h, T_KF]).astype(BF16), vf_ref[rf, h * DV:(h + 1) * DV]))
                    inc_b.append(_dot_tn((kb2 * tab_ref[h, T_KB]).astype(BF16), vb_ref[rb, h * DV:(h + 1) * DV]))
                    cf, cb = _chunk_decay(tab_ref, h)
                    gf.append(jnp.broadcast_to(cf, (128, 128)))
                    gb.append(jnp.broadcast_to(cb, (128, 128)))
                sf[pr] = _pair_select(gf[0], gf[1]) * sf[pr] + _pair_select(inc_f[0], inc_f[1])
                sb[pr] = _pair_select(gb[0], gb[1]) * sb[pr] + _pair_select(inc_b[0], inc_b[1])

    st = jax.ShapeDtypeStruct((N, HP, 128, 128), F32)
    R = RET_CPB * CHUNK
    NB = N // RET_CPB
    return _riding_call(
        body, None, NB, name="ret_states", args=(tab, kr, kr, p, p, s0),
        in_specs=[_tab_spec(H),
                  pl.BlockSpec((R, D // 2), lambda n: (n, 0)),
                  pl.BlockSpec((R, D // 2), lambda n: (NB - 1 - n, 0)),
                  pl.BlockSpec((R, D), lambda n: (n, 5)),
                  pl.BlockSpec((R, D), lambda n: (NB - 1 - n, 5)),
                  pl.BlockSpec((2, HP, 128, 128), lambda n: (0, 0, 0, 0))],
        out_specs=[pl.BlockSpec((RET_CPB, HP, 128, 128), lambda n: (n, 0, 0, 0)),
                   pl.BlockSpec((RET_CPB, HP, 128, 128), lambda n: (NB - 1 - n, 0, 0, 0))],
        out_shape=[st, st],
        scratch=[pltpu.VMEM((HP, 128, 128), F32), pltpu.VMEM((HP, 128, 128), F32)],
        cparams=_cparams(("arbitrary",)))


def _ret_out(qr, kr, p, sf_prev, sb_prev, gn_w, tab, D):
    L = qr.shape[0]
    H = D // DV
    N = L // CHUNK
    HP = H // 2

    def body(tab_ref, q_ref, k_ref, v_ref, zb_ref, sf_ref, sb_ref, gn_ref, o_ref, yb_ref):
        def chunk(cc, carry):
            rows = pl.ds(pl.multiple_of(cc * CHUNK, CHUNK), CHUNK)
            for pr in range(HP):
                q2 = q_ref[rows, pr * 128:(pr + 1) * 128]
                k2 = k_ref[rows, pr * 128:(pr + 1) * 128]
                sfp = sf_ref[cc, pr].astype(BF16)
                sbp = sb_ref[cc, pr].astype(BF16)
                for e in range(2):
                    h = 2 * pr + e
                    sl = slice(h * DV, (h + 1) * DV)
                    qm = jnp.where(_head_lane_mask(q2.shape, e), q2, jnp.zeros_like(q2))
                    a = (_dot_nt(qm, k2) * tab_ref[h, T_M]).astype(BF16)
                    qf = qm.astype(F32)
                    o = _dot(a, v_ref[rows, sl])
                    o += _dot((qf * tab_ref[h, T_QF]).astype(BF16), sfp)
                    o += _dot((qf * tab_ref[h, T_QB]).astype(BF16), sbp)
                    o_ref[rows, sl] = o
                    mu = jnp.mean(o, axis=-1, keepdims=True)
                    oc = o - mu
                    rstd = lax.rsqrt(jnp.mean(oc * oc, axis=-1, keepdims=True) + EPS)
                    zb = zb_ref[rows, sl].astype(F32)
                    yb_ref[rows, sl] = (zb * _sigmoid(zb) * (oc * rstd * gn_ref[:, sl])).astype(BF16)
            return carry

        lax.fori_loop(0, RET_CPB, chunk, 0)

    R = RET_CPB * CHUNK
    return _riding_call(
        body, None, N // RET_CPB, name="ret_out", args=(tab, qr, kr, p, p, sf_prev, sb_prev, gn_w),
        in_specs=[_tab_spec(H),
                  pl.BlockSpec((R, D // 2), lambda n: (n, 0)),
                  pl.BlockSpec((R, D // 2), lambda n: (n, 0)),
                  pl.BlockSpec((R, D), lambda n: (n, 5)),
                  pl.BlockSpec((R, D), lambda n: (n, 6)),
                  pl.BlockSpec((RET_CPB, HP, 128, 128), lambda n: (n, 0, 0, 0)),
                  pl.BlockSpec((RET_CPB, HP, 128, 128), lambda n: (n, 0, 0, 0)),
                  _vec_spec(D)],
        out_specs=[pl.BlockSpec((R, D), lambda n: (n, 0)), pl.BlockSpec((R, D), lambda n: (n, 0))],
        out_shape=[jax.ShapeDtypeStruct((L, D), F32), jax.ShapeDtypeStruct((L, D), BF16)],
        cparams=_cparams(("arbitrary",)))


def _mid(p, yb, o, x, tgt, w3, g, fw, conv_w, conv_b, gn_w, D):
    L = x.shape[0]
    H = D // DV
    tm = min(256, L)
    nt = L // tm

    def body(h_ref, bg_ref, cg_ref, za_ref, hp_ref, hn_ref, cp_ref, cn_ref, yb_ref, ga_ref, gb_ref, zb_ref, o_ref,
             x_ref, t_ref, w_hbm, g_ref, fw_ref, cw_ref, cb_ref, gn_ref,
             dx1_ref, dya_ref, do_ref, dzb_ref, dgab_ref, dw_hbm, st_ref, w_vm, dw_acc, sem):
        i = pl.program_id(0)

        @pl.when(i == 0)
        def _():
            cp = pltpu.make_async_copy(w_hbm, w_vm, sem)
            cp.start()
            dw_acc[...] = jnp.zeros_like(dw_acc)
            st_ref[...] = jnp.zeros_like(st_ref)
            cp.wait()

        u = cg_ref[...].astype(F32) * h_ref[...].astype(F32)
        above = jnp.where(i == 0, 0.0, cp_ref[15:16, :].astype(F32) * hp_ref[15:16, :].astype(F32))
        below = jnp.where(i == nt - 1, 0.0, cn_ref[0:1, :].astype(F32) * hn_ref[0:1, :].astype(F32))
        dn, up = _shift_rows(u, above, below)
        co = cw_ref[0:1, :] * dn + cw_ref[1:2, :] * u + cw_ref[2:3, :] * up + cb_ref[...]
        za = za_ref[...].astype(F32)
        ya_b = (za * _sigmoid(za) * bg_ref[...].astype(F32) * co).astype(BF16)
        yb_b = yb_ref[...]
        y_a = _dot(ya_b, w_vm[0])
        y_b = _dot(yb_b, w_vm[1])
        sga = _sigmoid(ga_ref[...].astype(F32))
        sgb = _sigmoid(gb_ref[...].astype(F32))
        mix_b = (sga * y_a + sgb * y_b).astype(BF16)
        y_x = _dot(mix_b, w_vm[2])
        gvec, fwv = g_ref[...], fw_ref[...]
        x1 = x_ref[...] + gvec * y_x
        r1 = lax.rsqrt(jnp.mean(x1 * x1, axis=-1, keepdims=True) + EPS)
        xh = x1 * r1
        diff = xh * fwv - t_ref[...]
        dout = diff * (1.0 / D)
        dxh = dout * fwv
        dx1 = r1 * (dxh - xh * jnp.mean(dxh * xh, axis=-1, keepdims=True))
        dx1_ref[...] = dx1
        st_ref[0:1, :] += jnp.sum(dout * xh, axis=0, keepdims=True)
        st_ref[1:2, :] += jnp.sum(dx1 * y_x, axis=0, keepdims=True)
        st_ref[2:3, :] += jnp.sum(diff * diff, axis=0, keepdims=True)
        dyx_b = (dx1 * gvec).astype(BF16)
        dmix = _dot_nt(dyx_b, w_vm[2])
        dw_acc[2] += _dot_tn(mix_b, dyx_b)
        dya_b = (dmix * sga).astype(BF16)
        dyb_b = (dmix * sgb).astype(BF16)
        dgab_ref[:, 0:D] = (dmix * y_a * sga * (1.0 - sga)).astype(BF16)
        dgab_ref[:, D:2 * D] = (dmix * y_b * sgb * (1.0 - sgb)).astype(BF16)
        dya_ref[...] = _dot_nt(dya_b, w_vm[0])
        dyb = _dot_nt(dyb_b, w_vm[1])
        dw_acc[0] += _dot_tn(ya_b, dya_b)
        dw_acc[1] += _dot_tn(yb_b, dyb_b)

        for h in range(H):
            sl = slice(h * DV, (h + 1) * DV)
            ov = o_ref[:, sl]
            oc = ov - jnp.mean(ov, axis=-1, keepdims=True)
            rstd = lax.rsqrt(jnp.mean(oc * oc, axis=-1, keepdims=True) + EPS)
            rn = oc * rstd
            gw = gn_ref[:, sl]
            zb = zb_ref[:, sl].astype(F32)
            sz = _sigmoid(zb)
            dy = dyb[:, sl]
            dzb_ref[:, sl] = (dy * (rn * gw) * (sz * (1.0 + zb * (1.0 - sz)))).astype(BF16)
            dretn = dy * (zb * sz)
            st_ref[3:4, sl] += jnp.sum(dretn * rn, axis=0, keepdims=True)
            drn = dretn * gw
            do_ref[:, sl] = (rstd * (drn - jnp.mean(drn, axis=-1, keepdims=True)
                                     - rn * jnp.mean(drn * rn, axis=-1, keepdims=True))).astype(BF16)

        @pl.when(i == nt - 1)
        def _():
            out = pltpu.make_async_copy(dw_acc, dw_hbm, sem)
            out.start()
            out.wait()

    row = lambda col: pl.BlockSpec((tm, D), lambda i: (i, col))
    any_spec = pl.BlockSpec(memory_space=pl.ANY)
    f32o = jax.ShapeDtypeStruct((L, D), F32)
    bf16o = jax.ShapeDtypeStruct((L, D), BF16)
    hp, hn = _halo_specs(tm, L, D, 0)
    cp, cn = _halo_specs(tm, L, D, 2)
    return pl.pallas_call(
        body, name="mid", grid=(nt,),
        in_specs=[row(0), row(1), row(2), row(3), hp, hn, cp, cn, row(0), row(7), row(8), row(6), row(0),
                  row(0), row(0), any_spec, _vec_spec(D), _vec_spec(D),
                  pl.BlockSpec((8, D), lambda i: (0, 0)), _vec_spec(D), _vec_spec(D)],
        out_specs=[row(0), row(0), row(0), row(0), pl.BlockSpec((tm, 2 * D), lambda i: (i, 0)), any_spec,
                   pl.BlockSpec((8, D), lambda i: (0, 0))],
        out_shape=[f32o, f32o, bf16o, bf16o, jax.ShapeDtypeStruct((L, 2 * D), BF16),
                   jax.ShapeDtypeStruct((3, D, D), F32), jax.ShapeDtypeStruct((8, D), F32)],
        scratch_shapes=[pltpu.VMEM((3, D, D), BF16), pltpu.VMEM((3, D, D), F32), pltpu.SemaphoreType.DMA],
        compiler_params=_cparams(("arbitrary",), VMEM_LIMIT),
    )(p, p, p, p, p, p, p, p, yb, p, p, p, o, x, tgt, w3, g, fw, conv_w, conv_b, gn_w)


def _conv_bwd(dya, p, conv_w, conv_b, D, exchange=None):
    L = p.shape[0]
    tl = min(256, L)
    nt = L // tl

    def body(d_ref, h_ref, bg_ref, cg_ref, za_ref,
             dp_ref, dn_ref, hp_ref, hn_ref, bp_ref, bn_ref, cp_ref, cn_ref, zp_ref, zn_ref,
             w_ref, b_ref, dc_ref, st_ref):
        i = pl.program_id(0)

        @pl.when(i == 0)
        def _():
            st_ref[...] = jnp.zeros_like(st_ref)

        first, last = i == 0, i == nt - 1
        h = h_ref[...].astype(F32)
        cg = cg_ref[...].astype(F32)
        bg = bg_ref[...].astype(F32)
        za = za_ref[...].astype(F32)
        dy = d_ref[...].astype(F32)
        u = cg * h
        u_above = jnp.where(first, 0.0, cp_ref[15:16, :].astype(F32) * hp_ref[15:16, :].astype(F32))
        u_below = jnp.where(last, 0.0, cn_ref[0:1, :].astype(F32) * hn_ref[0:1, :].astype(F32))
        u_dn, u_up = _shift_rows(u, u_above, u_below)
        w0, w1, w2 = w_ref[0:1, :], w_ref[1:2, :], w_ref[2:3, :]
        co = w0 * u_dn + w1 * u + w2 * u_up + b_ref[...]
        sz = _sigmoid(za)
        silu = za * sz
        dc_ref[:, 3 * D:4 * D] = (dy * bg * co * (sz * (1.0 + za * (1.0 - sz)))).astype(BF16)
        dc_ref[:, D:2 * D] = (dy * silu * co).astype(BF16)
        dco = dy * silu * bg

        def edge(dr, zr, br, r):
            z = zr[r:r + 1, :].astype(F32)
            return dr[r:r + 1, :].astype(F32) * (z * _sigmoid(z)) * br[r:r + 1, :].astype(F32)

        dco_above = jnp.where(first, 0.0, edge(dp_ref, zp_ref, bp_ref, 15))
        dco_below = jnp.where(last, 0.0, edge(dn_ref, zn_ref, bn_ref, 0))
        dco_dn, dco_up = _shift_rows(dco, dco_above, dco_below)
        du = w0 * dco_up + w1 * dco + w2 * dco_dn
        dc_ref[:, 2 * D:3 * D] = (du * h).astype(BF16)
        dc_ref[:, 0:D] = (du * cg).astype(BF16)
        st_ref[0:1, :] += jnp.sum(dco * u_dn, axis=0, keepdims=True)
        st_ref[1:2, :] += jnp.sum(dco * u, axis=0, keepdims=True)
        st_ref[2:3, :] += jnp.sum(dco * u_up, axis=0, keepdims=True)
        st_ref[3:4, :] += jnp.sum(dco, axis=0, keepdims=True)

    main = lambda col: pl.BlockSpec((tl, D), lambda i: (i, col))
    halos = []
    for col in (0, 0, 1, 2, 3):
        halos.extend(_halo_specs(tl, L, D, col))
    return _riding_call(
        body, exchange, nt, name="conv_bwd",
        args=(dya, p, p, p, p, dya, dya, p, p, p, p, p, p, p, p, conv_w, conv_b),
        in_specs=[main(0), main(0), main(1), main(2), main(3)] + halos
                 + [pl.BlockSpec((8, D), lambda i: (0, 0)), _vec_spec(D)],
        out_specs=[pl.BlockSpec((tl, 4 * D), lambda i: (i, 0)), pl.BlockSpec((8, D), lambda i: (0, 0))],
        out_shape=[jax.ShapeDtypeStruct((L, 4 * D), BF16), jax.ShapeDtypeStruct((8, D), F32)],
        cparams=_cparams(("arbitrary",)))


def _ret_bwd_states(qr, do, tab, D):
    L = qr.shape[0]
    H = D // DV
    N = L // CHUNK
    HP = H // 2

    def body(tab_ref, qf_ref, qb_ref, dof_ref, dob_ref, dsf_out, dsb_out, ds0_out, dsf, dsb):
        n = pl.program_id(0)

        @pl.when(n == 0)
        def _():
            dsf[...] = jnp.zeros_like(dsf)
            dsb[...] = jnp.zeros_like(dsb)

        for cc in range(RET_CPB):
            cf_, cb_ = RET_CPB - 1 - cc, cc
            rf, rb = slice(cf_ * CHUNK, (cf_ + 1) * CHUNK), slice(cb_ * CHUNK, (cb_ + 1) * CHUNK)
            dsf_out[cf_] = dsf[...]
            dsb_out[cb_] = dsb[...]
            for pr in range(HP):
                qf2 = qf_ref[rf, pr * 128:(pr + 1) * 128].astype(F32)
                qb2 = qb_ref[rb, pr * 128:(pr + 1) * 128].astype(F32)
                inc_f, inc_b, gf, gb = [], [], [], []
                for e in range(2):
                    h = 2 * pr + e
                    inc_f.append(_dot_tn((qf2 * tab_ref[h, T_QF]).astype(BF16), dof_ref[rf, h * DV:(h + 1) * DV]))
                    inc_b.append(_dot_tn((qb2 * tab_ref[h, T_QB]).astype(BF16), dob_ref[rb, h * DV:(h + 1) * DV]))
                    cf, cb = _chunk_decay(tab_ref, h)
                    gf.append(jnp.broadcast_to(cf, (128, 128)))
                    gb.append(jnp.broadcast_to(cb, (128, 128)))
                dsf[pr] = _pair_select(gf[0], gf[1]) * dsf[pr] + _pair_select(inc_f[0], inc_f[1])
                dsb[pr] = _pair_select(gb[0], gb[1]) * dsb[pr] + _pair_select(inc_b[0], inc_b[1])

        @pl.when(n == NB - 1)
        def _():
            ds0_out[0] = dsf[...]
            ds0_out[1] = dsb[...]

    st = jax.ShapeDtypeStruct((N, HP, 128, 128), F32)
    R = RET_CPB * CHUNK
    NB = N // RET_CPB
    return pl.pallas_call(
        body, name="ret_bwd_states", grid=(NB,),
        in_specs=[_tab_spec(H),
                  pl.BlockSpec((R, D // 2), lambda n: (NB - 1 - n, 0)),
                  pl.BlockSpec((R, D // 2), lambda n: (n, 0)),
                  pl.BlockSpec((R, D), lambda n: (NB - 1 - n, 0)),
                  pl.BlockSpec((R, D), lambda n: (n, 0))],
        out_specs=[pl.BlockSpec((RET_CPB, HP, 128, 128), lambda n: (NB - 1 - n, 0, 0, 0)),
                   pl.BlockSpec((RET_CPB, HP, 128, 128), lambda n: (n, 0, 0, 0)),
                   pl.BlockSpec((2, HP, 128, 128), lambda n: (0, 0, 0, 0))],
        out_shape=[st, st, jax.ShapeDtypeStruct((2, HP, 128, 128), F32)],
        scratch_shapes=[pltpu.VMEM((HP, 128, 128), F32), pltpu.VMEM((HP, 128, 128), F32)],
        compiler_params=_cparams(("arbitrary",)),
    )(tab, qr, qr, do, do)


def _ret_bwd_main(qr, kr, p, do, sf_prev, sb_prev, dsf, dsb, c2, s2, tab, D, exchange=None):
    L = qr.shape[0]
    H = D // DV
    N = L // CHUNK
    HP = H // 2
    W = D // 2

    def body(tab_ref, q_ref, k_ref, v_ref, do_ref, sf_ref, sb_ref, dsf_ref, dsb_ref, c_ref, s_ref,
             dr_ref, st_ref, dl_acc):
        @pl.when(pl.program_id(0) == 0)
        def _():
            dl_acc[...] = jnp.zeros_like(dl_acc)

        i = lax.broadcasted_iota(jnp.int32, (CHUNK, 128), 0).astype(F32)
        rowid = lax.broadcasted_iota(jnp.int32, (128, 128), 0)

        def chunk(cc, carry):
            rows = pl.ds(pl.multiple_of(cc * CHUNK, CHUNK), CHUNK)
            c, s = c_ref[rows, :], s_ref[rows, :]
            for pr in range(HP):
                ps = slice(pr * 128, (pr + 1) * 128)
                q2, k2 = q_ref[rows, ps], k_ref[rows, ps]
                sf32, sb32 = sf_ref[cc, pr], sb_ref[cc, pr]
                dsf32, dsb32 = dsf_ref[cc, pr], dsb_ref[cc, pr]
                sfp, sbp = sf32.astype(BF16), sb32.astype(BF16)
                dsfp, dsbp = dsf32.astype(BF16), dsb32.astype(BF16)
                dq2 = jnp.zeros((CHUNK, 128), F32)
                dk2 = jnp.zeros((CHUNK, 128), F32)
                for e in range(2):
                    h = 2 * pr + e
                    sl = slice(h * DV, (h + 1) * DV)
                    hm = _head_lane_mask(q2.shape, e)
                    qm = jnp.where(hm, q2, jnp.zeros_like(q2))
                    km = jnp.where(hm, k2, jnp.zeros_like(k2))
                    qf, kf = qm.astype(F32), km.astype(F32)
                    v, do = v_ref[rows, sl], do_ref[rows, sl]
                    vf, dof = v.astype(F32), do.astype(F32)
                    m_t = tab_ref[h, T_MT]
                    sc = _dot_nt(qm, k2)
                    dpm = _dot_nt(do, v)
                    dsc = (dpm * tab_ref[h, T_M]).astype(BF16)
                    a_t = (_dot_nt(km, q2) * m_t).astype(BF16)
                    dsc_t = (_dot_nt(v, do) * m_t).astype(BF16)
                    dq_f, dq_b = tab_ref[h, T_QF], tab_ref[h, T_QB]
                    dk_f, dk_b = tab_ref[h, T_KF], tab_ref[h, T_KB]
                    dq = _dot(dsc, km)
                    dq += jnp.where(hm, dq_f * _dot_nt(do, sfp) + dq_b * _dot_nt(do, sbp), 0.0)
                    dk = _dot(dsc_t, qm)
                    dk += jnp.where(hm, dk_f * _dot_nt(v, dsfp) + dk_b * _dot_nt(v, dsbp), 0.0)
                    kdf = _dot((kf * dk_f).astype(BF16), dsfp)
                    kdb = _dot((kf * dk_b).astype(BF16), dsbp)
                    dr_ref[rows, D + h * DV:D + (h + 1) * DV] = (_dot(a_t, do) + kdf + kdb).astype(BF16)
                    dq2 += dq
                    dk2 += dk
                    xf = _dot((qf * dq_f).astype(BF16), sfp)
                    xb = _dot((qf * dq_b).astype(BF16), sbp)
                    pair = (rowid < DK) if e == 0 else (rowid >= DK)
                    gcf, gcb = tab_ref[h, T_QF, CHUNK - 1:CHUNK, 0:1], tab_ref[h, T_QB, 0:1, 0:1]
                    scdp = sc * dpm
                    dl_acc[h, 0] += scdp * tab_ref[h, T_MF1] + xf * dof * (i + 1.0) \
                        + kdf * vf * (CHUNK - 1.0 - i) + (CHUNK * gcf) * jnp.where(pair, dsf32 * sf32, 0.0)
                    dl_acc[h, 1] += scdp * tab_ref[h, T_MB1] + xb * dof * (CHUNK - i) \
                        + kdb * vf * i + (CHUNK * gcb) * jnp.where(pair, dsb32 * sb32, 0.0)
                dr_ref[rows, ps] = (dq2 * c - _swap_halves(dq2) * s).astype(BF16)
                dr_ref[rows, W + pr * 128:W + (pr + 1) * 128] = \
                    ((dk2 * c - _swap_halves(dk2) * s) * K_SCALE).astype(BF16)
            return carry

        lax.fori_loop(0, RET_CPB, chunk, 0)

        @pl.when(pl.program_id(0) == N // RET_CPB - 1)
        def _():
            lane = lax.broadcasted_iota(jnp.int32, (1, 128), 1)
            acc = [jnp.zeros((1, 128), F32), jnp.zeros((1, 128), F32)]
            for h in range(H):
                for b in range(2):
                    acc[b] += jnp.where(lane == h, _sum_all(dl_acc[h, b]), 0.0)
            st_ref[...] = jnp.zeros_like(st_ref)
            st_ref[0:1, :] = acc[0]
            st_ref[1:2, :] = acc[1]

    R = RET_CPB * CHUNK
    st_spec = pl.BlockSpec((RET_CPB, HP, 128, 128), lambda n: (n, 0, 0, 0))
    half = pl.BlockSpec((R, W), lambda n: (n, 0))
    rope = pl.BlockSpec((R, 128), lambda n: (n, 0))
    return _riding_call(
        body, exchange, N // RET_CPB, name="ret_bwd_main",
        args=(tab, qr, kr, p, do, sf_prev, sb_prev, dsf, dsb, c2, s2),
        in_specs=[_tab_spec(H), half, half,
                  pl.BlockSpec((R, D), lambda n: (n, 5)),
                  pl.BlockSpec((R, D), lambda n: (n, 0)),
                  st_spec, st_spec, st_spec, st_spec, rope, rope],
        out_specs=[pl.BlockSpec((R, 2 * D), lambda n: (n, 0)),
                   pl.BlockSpec((8, 128), lambda n: (0, 0))],
        out_shape=[jax.ShapeDtypeStruct((L, 2 * D), BF16), jax.ShapeDtypeStruct((8, 128), F32)],
        scratch=[pltpu.VMEM((H, 2, CHUNK, 128), F32)],
        cparams=_cparams(("arbitrary",)))


def _ctx_bwd(pc, pqk_c, ds0, lg, D):
    Lc = pc.shape[0]
    H = D // DV
    HP = H // 2
    W = D // 2

    def body(lg_ref, k_ref, v_ref, ds_ref, dr_ref, st_ref):
        dqk_ref = dr_ref.at[:, 0:D]
        dv_ref = dr_ref.at[:, D:2 * D]
        m = lax.broadcasted_iota(jnp.int32, (Lc, 128), 0).astype(F32)
        lane = lax.broadcasted_iota(jnp.int32, (1, 128), 1)
        acc_f = jnp.zeros((1, 128), F32)
        acc_b = jnp.zeros((1, 128), F32)
        dqk_ref[:, 0:W] = jnp.zeros((Lc, W), BF16)
        for pr in range(HP):
            ps = slice(pr * 128, (pr + 1) * 128)
            k2 = k_ref[:, ps].astype(F32) * K_SCALE
            dsfp, dsbp = ds_ref[0, pr].astype(BF16), ds_ref[1, pr].astype(BF16)
            dk2 = jnp.zeros((Lc, 128), F32)
            for e in range(2):
                h = 2 * pr + e
                sl = slice(h * DV, (h + 1) * DV)
                hm = _head_lane_mask(k2.shape, e)
                km = jnp.where(hm, k2, 0.0)
                v = v_ref[:, sl]
                vf = v.astype(F32)
                dec_f = jnp.exp(lg_ref[0, h] * (Lc - 1.0 - m))
                dec_b = jnp.exp(lg_ref[1, h] * m)
                kdf = _dot((km * dec_f).astype(BF16), dsfp)
                kdb = _dot((km * dec_b).astype(BF16), dsbp)
                dv_ref[:, sl] = (kdf + kdb).astype(BF16)
                dk2 += jnp.where(hm, dec_f * _dot_nt(v, dsfp) + dec_b * _dot_nt(v, dsbp), 0.0)
                acc_f += jnp.where(lane == h, _sum_all(kdf * vf * (Lc - 1.0 - m)), 0.0)
                acc_b += jnp.where(lane == h, _sum_all(kdb * vf * m), 0.0)
            dqk_ref[:, W + pr * 128:W + (pr + 1) * 128] = (dk2 * K_SCALE).astype(BF16)
        st_ref[...] = jnp.zeros_like(st_ref)
        st_ref[0:1, :] = acc_f
        st_ref[1:2, :] = acc_b

    return pl.pallas_call(
        body, name="ctx_bwd", grid=(1,),
        in_specs=[_smem_spec(), pl.BlockSpec((Lc, W), lambda i: (0, 1)), pl.BlockSpec((Lc, D), lambda i: (0, 1)),
                  pl.BlockSpec((2, HP, 128, 128), lambda i: (0, 0, 0, 0))],
        out_specs=[pl.BlockSpec((Lc, 2 * D), lambda i: (0, 0)), pl.BlockSpec((8, 128), lambda i: (0, 0))],
        out_shape=[jax.ShapeDtypeStruct((Lc, 2 * D), BF16), jax.ShapeDtypeStruct((8, 128), F32)],
    )(lg, pqk_c, pc, ds0)


class _Exchange(NamedTuple):
    inputs: tuple
    out_shapes: tuple
    n_copies: int
    build: Callable


def _exchange_parts(exchange):
    if exchange is None:
        return [], [], [], [], []
    n = exchange.n_copies
    return (list(exchange.inputs), [ANY] * len(exchange.inputs), list(exchange.out_shapes),
            [ANY] * len(exchange.out_shapes), [pltpu.SemaphoreType.DMA((n,)), pltpu.SemaphoreType.DMA((n,))])


def _riding_call(body, exchange, n_steps, *, args, in_specs, out_specs, out_shape, name, cparams, scratch=()):
    ex_args, ex_in_specs, ex_shapes, ex_out_specs, ex_scratch = _exchange_parts(exchange)
    n_in, n_out, n_sc = len(args), len(out_shape), len(scratch)

    def riding(*refs):
        k = n_in + len(ex_args)
        ins, ex_in = refs[:n_in], refs[n_in:k]
        outs, ex_out = refs[k:k + n_out], refs[k + n_out:k + n_out + len(ex_shapes)]
        k += n_out + len(ex_shapes)
        own_scratch, ex_sems = refs[k:k + n_sc], refs[k + n_sc:]
        step = pl.program_id(0)
        if exchange is not None:
            @pl.when(step == 0)
            def _():
                for rc in exchange.build(ex_in, ex_out, *ex_sems):
                    rc.start()
        body(*ins, *outs, *own_scratch)
        if exchange is not None:
            @pl.when(step == n_steps - 1)
            def _():
                for rc in exchange.build(ex_in, ex_out, *ex_sems):
                    rc.wait()

    return tuple(pl.pallas_call(
        riding, name=name, grid=(n_steps,),
        in_specs=list(in_specs) + ex_in_specs, out_specs=list(out_specs) + ex_out_specs,
        out_shape=list(out_shape) + ex_shapes, scratch_shapes=list(scratch) + ex_scratch,
        compiler_params=cparams,
    )(*args, *ex_args))


def _dxm(groups, col0, w, x, nw, sc, dx1, name, exchange=None):
    L, D = x.shape
    tm = min(256, L)
    nt = L // tm
    ng = len(groups)
    widths = [g.shape[1] for g in groups]
    wtot = sum(widths)
    with_dx = dx1 is not None
    ex_args, ex_in_specs, ex_shapes, ex_out_specs, ex_scratch = _exchange_parts(exchange)
    n_in = ng + 4 + (1 if with_dx else 0)
    n_out = 2 if with_dx else 1

    def body(*refs):
        group_refs = refs[:ng]
        w_hbm, x_ref, nw_ref, sc_ref = refs[ng:ng + 4]
        ex_in = refs[n_in:n_in + len(ex_args)]
        outs = refs[n_in + len(ex_args):]
        if with_dx:
            dx1_ref, gx_ref, st_ref = refs[ng + 4], outs[0], outs[1]
        else:
            st_ref = outs[0]
        ex_out = outs[n_out:n_out + len(ex_shapes)]
        w_vm, sem = outs[n_out + len(ex_shapes):n_out + len(ex_shapes) + 2]
        ex_sems = outs[n_out + len(ex_shapes) + 2:]
        i = pl.program_id(0)

        @pl.when(i == 0)
        def _():
            cp = pltpu.make_async_copy(w_hbm.at[:, col0 * D:col0 * D + wtot], w_vm, sem)
            cp.start()
            if exchange is not None:
                for rc in exchange.build(ex_in, ex_out, *ex_sems):
                    rc.start()
            st_ref[...] = jnp.zeros_like(st_ref)
            cp.wait()

        dxm, off = None, 0
        for g_ref, wd in zip(group_refs, widths):
            part = _dot_nt(g_ref[...], w_vm[:, off:off + wd])
            dxm = part if dxm is None else dxm + part
            off += wd

        xv = x_ref[...]
        r = lax.rsqrt(jnp.mean(xv * xv, axis=-1, keepdims=True) + EPS)
        xh = xv * r
        nwv = nw_ref[...]
        dxn = dxm * (1.0 + sc_ref[...])
        st_ref[0:1, :] += jnp.sum(dxm, axis=0, keepdims=True)
        st_ref[1:2, :] += jnp.sum(dxm * (xh * nwv), axis=0, keepdims=True)
        st_ref[2:3, :] += jnp.sum(dxn * xh, axis=0, keepdims=True)
        if with_dx:
            dxh = dxn * nwv
            gx_ref[...] = dx1_ref[...] + r * (dxh - xh * jnp.mean(dxh * xh, axis=-1, keepdims=True))

        if exchange is not None:
            @pl.when(i == nt - 1)
            def _():
                for rc in exchange.build(ex_in, ex_out, *ex_sems):
                    rc.wait()

    row = pl.BlockSpec((tm, D), lambda i: (i, 0))
    in_specs = [pl.BlockSpec((tm, wd), lambda i: (i, 0)) for wd in widths] + [ANY, row, _vec_spec(D), _vec_spec(D)]
    out_specs = [pl.BlockSpec((8, D), lambda i: (0, 0))]
    out_shape = [jax.ShapeDtypeStruct((8, D), F32)]
    args = list(groups) + [w, x, nw, sc]
    if with_dx:
        in_specs.append(row)
        out_specs.insert(0, row)
        out_shape.insert(0, jax.ShapeDtypeStruct((L, D), F32))
        args.append(dx1)
    res = pl.pallas_call(
        body, name=name, grid=(nt,),
        in_specs=in_specs + ex_in_specs, out_specs=out_specs + ex_out_specs, out_shape=out_shape + ex_shapes,
        scratch_shapes=[pltpu.VMEM((D, wtot), BF16), pltpu.SemaphoreType.DMA] + ex_scratch,
        compiler_params=_cparams(("arbitrary",), VMEM_LIMIT),
    )(*args, *ex_args)
    gx = res[0] if with_dx else None
    return (gx, res[n_out - 1], *res[n_out:])


DW_TN = 512
DW_RING = 4


def _dw_in(xmt, groups, cmt, dr_c, D, pair):
    L = xmt.shape[1]
    Lc = cmt.shape[1]
    Dh = D // 2
    tn = min(DW_TN, D)
    nblk = [g.shape[1] // tn for g in groups]
    starts = [sum(nblk[:g]) for g in range(len(groups))]
    ng = len(groups)
    nj = sum(nblk)
    rows_out = Dh if pair else D

    def body(*refs):
        xt_hbm = refs[0]
        group_refs = refs[1:1 + ng]
        ct_hbm, drc_ref, o_ref = refs[1 + ng:4 + ng]
        rest = refs[4 + ng:]
        if pair:
            ra_hbm, xt_vm, ct_vm, loc, ring, s_send, s_recv = rest
            pos = _position()
            sib = _peer(pos, 1)
        else:
            xt_vm, ct_vm, loc = rest
        j = pl.program_id(0)

        @pl.when(j == 0)
        def _():
            if pair:
                c = pos[2]
                other = pl.ds(pl.multiple_of((1 - c) * Dh, Dh), Dh)
                mine = pl.ds(pl.multiple_of(c * Dh, Dh), Dh)
                cps = [pltpu.make_async_copy(xt_hbm.at[other, :], xt_vm.at[0:Dh, :], loc.at[0]),
                       pltpu.make_async_copy(xt_hbm.at[mine, :], xt_vm.at[Dh:D, :], loc.at[1]),
                       pltpu.make_async_copy(ct_hbm.at[other, :], ct_vm.at[0:Dh, :], loc.at[2]),
                       pltpu.make_async_copy(ct_hbm.at[mine, :], ct_vm.at[Dh:D, :], loc.at[3])]
            else:
                cps = [pltpu.make_async_copy(xt_hbm, xt_vm, loc.at[0]), pltpu.make_async_copy(ct_hbm, ct_vm, loc.at[1])]
            for cp in cps:
                cp.start()
            for cp in cps:
                cp.wait()

        def send(slot):
            cols = pl.ds(pl.multiple_of(j * tn, 128), tn)
            return pltpu.make_async_remote_copy(src_ref=ring.at[slot], dst_ref=ra_hbm.at[:, cols],
                                                send_sem=s_send.at[slot], recv_sem=s_recv,
                                                device_id=sib, device_id_type=MESH)

        for g in range(ng):
            @pl.when((j >= starts[g]) & (j < starts[g] + nblk[g]))
            def _(g=g):
                acc = _dot(xt_vm[...], group_refs[g][...])
                if g == 1:
                    acc += _dot(ct_vm[...], drc_ref[...])
                if not pair:
                    o_ref[...] = acc
                    return
                o_ref[...] = acc[Dh:, :]
                slot = lax.rem(j, DW_RING)

                @pl.when(j >= DW_RING)
                def _():
                    send(slot).wait_send()

                ring[slot] = acc[0:Dh, :]
                send(slot).start()

        if pair:
            @pl.when(j == nj - 1)
            def _():
                pltpu.make_async_remote_copy(src_ref=ra_hbm, dst_ref=ra_hbm, send_sem=s_send.at[0], recv_sem=s_recv,
                                             device_id=sib, device_id_type=MESH).wait_recv()
                for slot in range(DW_RING):
                    send(slot).wait_send()

    def group_spec(g, rows):
        return pl.BlockSpec((rows, tn), lambda j: (0, jnp.clip(j - starts[g], 0, nblk[g] - 1)))

    out_specs = [pl.BlockSpec((rows_out, tn), lambda j: (0, j))]
    out_shape = [jax.ShapeDtypeStruct((rows_out, nj * tn), F32)]
    scratch = [pltpu.VMEM((D, L), BF16), pltpu.VMEM((D, Lc), BF16), pltpu.SemaphoreType.DMA((4,))]
    if pair:
        out_specs.append(ANY)
        out_shape.append(jax.ShapeDtypeStruct((Dh, nj * tn), F32))
        scratch += [pltpu.VMEM((DW_RING, Dh, tn), F32), pltpu.SemaphoreType.DMA((DW_RING,)), pltpu.SemaphoreType.DMA]
    return tuple(pl.pallas_call(
        body, name="dw_in", grid=(nj,),
        in_specs=[ANY] + [group_spec(g, L) for g in range(ng)] + [ANY, group_spec(1, Lc)],
        out_specs=out_specs, out_shape=out_shape, scratch_shapes=scratch,
        compiler_params=_cparams(("arbitrary",), VMEM_LIMIT),
    )(xmt, *groups, cmt, dr_c))


def _local_step(x, ctx, tgt, mod_x, mod_c, norm_w, conv_w8, conv_b, lg, gn_w, fw, project, csidx=None, w_in_s=None):
    L, D = x.shape
    sh_x, sc_x, g_x = mod_x
    sh_c, sc_c = mod_c
    c2, s2 = _rope_tables(L)
    tab = _decay_tables(lg, D // DV)

    xm, xmt, *w_bf = _norm_mod(x, norm_w, sc_x, sh_x, "norm_mod_x", w_in_s)
    cm, cmt = _norm_mod(ctx, norm_w, sc_c, sh_c, "norm_mod_ctx")
    reduce = csidx is not None
    p, qr, kr, w_in, w3 = project(xm, c2, s2, *w_bf)
    pc, pqk_c = _in_proj(cm, w_in, "in_proj_ctx", QK_BLOCK, 2)
    s0 = _ctx_states(pc, pqk_c, lg, D)
    sf_prev, sb_prev = _ret_states(kr, p, s0, tab, D)
    o, yb = _ret_out(qr, kr, p, sf_prev, sb_prev, gn_w, tab, D)
    dx1, dya, do, dzb, dgab, dw3, st_mid = _mid(p, yb, o, x, tgt, w3, g_x, fw, conv_w8, conv_b, gn_w, D)
    dw3_5 = dw3.reshape(3, N_SHARD, 2, D // 8, D)
    dconv, st_conv, *ra_3 = _conv_bwd(dya, p, conv_w8, conv_b, D, _pair_exchange_w3(dw3_5) if reduce else None)
    dsf, dsb, ds0 = _ret_bwd_states(qr, do, tab, D)
    cs_3 = _sum_pair_w3(csidx[0:1], dw3_5, ra_3[0]) if reduce else None
    dret, st_lg, *rb_3 = _ret_bwd_main(qr, kr, p, do, sf_prev, sb_prev, dsf, dsb, c2, s2, tab, D,
                                       _chips_exchange_w3(cs_3) if reduce else None)
    g_3 = _sum_chips_w3(csidx, cs_3, rb_3[0]) if reduce else dw3
    dret_c, st_lgc = _ctx_bwd(pc, pqk_c, ds0, lg, D)
    groups = (dconv, dret, dzb, dgab)
    _, st_c = _dxm((dret_c,), 4, w_in, ctx, norm_w, sc_c, None, "dxm_ctx")
    return groups, dret_c, xmt, cmt, dx1, sc_x, w_in, g_3, (st_mid, st_conv, st_lg, st_lgc, st_c)


CHIP_FLIPS = (4, 2, 6)
ANY = pl.BlockSpec(memory_space=pl.ANY)
VMEM_FULL = pl.BlockSpec(memory_space=pltpu.VMEM)


def _position():
    return lax.axis_index("x"), lax.axis_index("y"), lax.axis_index("c")


def _peer(pos, k):
    x, y, c = pos
    return (1 - x if k & 4 else x, 1 - y if k & 2 else y, 1 - c if k & 1 else c)


def _dev_id(pos):
    return 4 * pos[0] + 2 * pos[1] + pos[2]


def _shard_of(pos):
    return 2 * pos[0] + pos[1]


def _remote(src, dst, send_sems, recv_sems, idx, to):
    return pltpu.make_async_remote_copy(src_ref=src, dst_ref=dst, send_sem=send_sems.at[idx],
                                        recv_sem=recv_sems.at[idx], device_id=to, device_id_type=MESH)


def _dot_f32(a, b):
    return jnp.dot(a, b, precision=lax.Precision.HIGHEST, preferred_element_type=F32)


def _silu(x):
    return x * _sigmoid(x)


def _fwd_small(c, c_ctx, ada_w, ada_b, conv_w):
    D = c.shape[1]
    Wm = ada_w.shape[1]
    Dq = conv_w.shape[1]

    def body(c_ref, cc_ref, aw_ref, ab_ref, cw_ref, act_ref, shx_ref, scx_ref, gx_ref, shc_ref, scc_ref, cwf_ref,
             mod_ref, cbuf, pmine, pbuf, wbuf, s_c, r_c, s_p, r_p, s_w, r_w):
        pos = _position()
        me, s = _dev_id(pos), _shard_of(pos)
        cbuf[me] = c_ref[...]
        wbuf[s] = cw_ref[...]
        sends = [_remote(c_ref, cbuf.at[me], s_c, r_c, k - 1, _peer(pos, k)) for k in range(1, 8)]
        sends += [_remote(cw_ref, wbuf.at[s], s_w, r_w, j, _peer(pos, k)) for j, k in enumerate(CHIP_FLIPS)]
        for cp in sends:
            cp.start()
        for k in range(1, 8):
            _remote(c_ref, cbuf.at[_dev_id(_peer(pos, k))], s_c, r_c, k - 1, _peer(pos, k)).wait_recv()
        for d in range(N_DEV):
            act_ref[d:d + 1, :] = _silu(cbuf[d])
        act_ref[8:9, :] = _silu(cc_ref[...])
        act_ref[9:16, :] = jnp.zeros((7, D), F32)
        part = _dot_f32(act_ref[...], aw_ref[...])
        pmine[...] = part
        pbuf[s] = part
        psend = [_remote(pmine, pbuf.at[s], s_p, r_p, j, _peer(pos, k)) for j, k in enumerate(CHIP_FLIPS)]
        for cp in psend:
            cp.start()
        for j, k in enumerate(CHIP_FLIPS):
            t = _shard_of(_peer(pos, k))
            _remote(pmine, pbuf.at[t], s_p, r_p, j, _peer(pos, k)).wait_recv()
            _remote(cw_ref, wbuf.at[t], s_w, r_w, j, _peer(pos, k)).wait_recv()
        cwf_ref[...] = jnp.zeros_like(cwf_ref)
        for t in range(N_SHARD):
            mod_ref[:, t * Wm:(t + 1) * Wm] = pbuf[t] + ab_ref[:, t * Wm:(t + 1) * Wm]
            cwf_ref[0:3, t * Dq:(t + 1) * Dq] = wbuf[t]
        for r, o_ref in enumerate((shx_ref, scx_ref, gx_ref)):
            o_ref[...] = mod_ref[pl.ds(me, 1), r * D:(r + 1) * D]
        for r, o_ref in enumerate((shc_ref, scc_ref)):
            o_ref[...] = mod_ref[8:9, r * D:(r + 1) * D]
        for cp in sends + psend:
            cp.wait_send()

    row = jax.ShapeDtypeStruct((1, D), F32)
    res = pl.pallas_call(
        body, name="fwd_small",
        in_specs=[VMEM_FULL] * 5, out_specs=[VMEM_FULL] * 7,
        out_shape=[jax.ShapeDtypeStruct((16, D), F32)] + [row] * 5 + [jax.ShapeDtypeStruct((8, D), F32)],
        scratch_shapes=[pltpu.VMEM((16, 3 * D), F32), pltpu.VMEM((N_DEV, 1, D), F32), pltpu.VMEM((16, Wm), F32),
                        pltpu.VMEM((N_SHARD, 16, Wm), F32), pltpu.VMEM((N_SHARD, 3, Dq), F32),
                        pltpu.SemaphoreType.DMA((7,)), pltpu.SemaphoreType.DMA((7,)),
                        pltpu.SemaphoreType.DMA((3,)), pltpu.SemaphoreType.DMA((3,)),
                        pltpu.SemaphoreType.DMA((3,)), pltpu.SemaphoreType.DMA((3,))],
        compiler_params=_cparams(None, VMEM_LIMIT),
    )(c, c_ctx, ada_w, ada_b, conv_w)
    return res[0], tuple(res[1:4]), tuple(res[4:6]), res[6]


AG_CHUNKS = 3


def _ag_in_proj(xm, w_in_s, w3_s, c2, s2):
    L, D = xm.shape
    Wc = w_in_s.shape[1]
    Wq = Wc // AG_CHUNKS
    Dh = D // 2
    Do = w3_s[0].shape[1]
    TM = min(1024, L // 4)
    NT = L // TM
    NQ = AG_CHUNKS
    order = [(q, j) for q in range(NQ) for j in (0, 1)] + [(q, 2) for q in range(NQ)]

    def body(xm_ref, wi_hbm, wa_ref, wb_ref, wo_ref, c_ref, s_ref, p_hbm, qr_hbm, kr_hbm, fi_hbm, f3_hbm,
             w_vm, s3, stage, qk_stage, ici_s, ici_r, d2d_s, d2d_r, w3_s_, w3_r_, fw_s, fw_r,
             loc, out_sem, qk_sem):
        pos = _position()
        c = pos[2]
        s = _shard_of(pos)
        sib = _peer(pos, 1)
        mine = pl.ds(pl.multiple_of(c * Dh, Dh), Dh)
        other = pl.ds(pl.multiple_of((1 - c) * Dh, Dh), Dh)

        def abs_col(t, q):
            return pl.ds(pl.multiple_of(t * Wc + q * Wq, 128), Wq)

        own = [pltpu.make_async_copy(wi_hbm.at[:, q * Wq:(q + 1) * Wq], w_vm.at[0, q], loc.at[2 + 4 * NQ + q])
               for q in range(NQ)]
        for cp in own:
            cp.start()
        for cp in own:
            cp.wait()
        sends = [_remote(w_vm.at[0, q, mine, :], w_vm.at[1 + j, q, mine, :], ici_s, ici_r, q * 3 + j,
                         _peer(pos, CHIP_FLIPS[j])) for q, j in order if j < 2]
        for cp in sends:
            cp.start()
        for a, w_ref in enumerate((wa_ref, wb_ref, wo_ref)):
            s3[a] = w_ref[...].astype(BF16)
        w3_sends = [_remote(s3.at[:, c], f3_hbm.at[:, s, c], w3_s_, w3_r_, j, _peer(pos, k))
                    for j, k in enumerate(CHIP_FLIPS)]
        local = [pltpu.make_async_copy(s3, f3_hbm.at[:, s], loc.at[1])]
        local += [pltpu.make_async_copy(w_vm.at[0, q], fi_hbm.at[:, abs_col(s, q)], loc.at[2 + q]) for q in range(NQ)]
        for cp in local:
            cp.start()

        def out_copy(slot, rows, cols):
            return pltpu.make_async_copy(stage.at[slot], p_hbm.at[rows, cols], out_sem.at[slot])

        def block(r, q, t, first):
            cols = abs_col(t, q)

            def row_tile(rt, carry):
                rows = pl.ds(pl.multiple_of(rt * TM, TM), TM)
                acc = _dot(xm_ref[rows, :], w_vm[r, q])
                slot = lax.rem(rt, 2)

                @pl.when(rt >= 2 if first else rt >= 0)
                def _():
                    out_copy(slot, rows, cols).wait()

                stage[slot] = acc.astype(BF16)
                out_copy(slot, rows, cols).start()

                def rotary(lo, scale, dst_hbm):
                    c, s = c_ref[rows, :], s_ref[rows, :]
                    for pr in range(Dh // 128):
                        tq = acc[:, lo + pr * 128:lo + (pr + 1) * 128] * scale
                        qk_stage[:, pr * 128:(pr + 1) * 128] = (tq * c + _swap_halves(tq) * s).astype(BF16)
                    cp = pltpu.make_async_copy(qk_stage, dst_hbm.at[rows, :], qk_sem)
                    cp.start()
                    cp.wait()

                if q == NQ - 1:
                    @pl.when(t == 1)
                    def _():
                        rotary(Wq - Dh, 1.0, qr_hbm)
                if q == 0:
                    @pl.when(t == 2)
                    def _():
                        rotary(0, K_SCALE, kr_hbm)
                return carry

            lax.fori_loop(0, NT, row_tile, 0)

        passed = []

        def hand_on(q, j):
            half = w_vm.at[1 + j, q, mine, :]
            if j == 2:
                _remote(half, half, fw_s, fw_r, q, sib).wait_recv()
            else:
                _remote(half, half, ici_s, ici_r, q * 3 + j, sib).wait_recv()

                @pl.when(c == (0 if j == q % 2 else 1))
                def _():
                    _remote(half, w_vm.at[3, q, mine, :], fw_s, fw_r, q, _peer(pos, CHIP_FLIPS[1 - j])).start()
            fwd = _remote(half, half, d2d_s, d2d_r, q * 3 + j, sib)
            fwd.start()
            passed.append(fwd)

        for q in range(NQ):
            if q == NQ - 1:
                hand_on(*order[0])
            block(0, q, s, q == 0)
        for n, (q, j) in enumerate(order):
            r, idx = 1 + j, q * 3 + j
            t = _shard_of(_peer(pos, CHIP_FLIPS[j]))
            if n + 1 < len(order):
                hand_on(*order[n + 1])
            if n + 1 == 2 * NQ - 1:
                for cp in w3_sends:
                    cp.start()
            _remote(w_vm.at[r, q, other, :], w_vm.at[r, q, other, :], d2d_s, d2d_r, idx, sib).wait_recv()
            block(r, q, t, False)
            cp = pltpu.make_async_copy(w_vm.at[r, q], fi_hbm.at[:, abs_col(t, q)], loc.at[2 + NQ + idx])
            cp.start()
            local.append(cp)
        for j, k in enumerate(CHIP_FLIPS):
            t = _shard_of(_peer(pos, k))
            _remote(s3.at[:, c], f3_hbm.at[:, t, c], w3_s_, w3_r_, j, sib).wait_recv()
            fwd = _remote(f3_hbm.at[:, t, c], f3_hbm.at[:, t, c], w3_s_, w3_r_, 3 + j, sib)
            fwd.start()
            passed.append(fwd)
        for j, k in enumerate(CHIP_FLIPS):
            t = _shard_of(_peer(pos, k))
            _remote(s3.at[:, c], f3_hbm.at[:, t, 1 - c], w3_s_, w3_r_, 3 + j, sib).wait_recv()
        for cp in sends + w3_sends + passed:
            cp.wait_send()
        for q in range(NQ):
            _remote(w_vm.at[1, q, mine, :], w_vm.at[3, q, mine, :], fw_s, fw_r, q, sib).wait_send()
        for cp in local:
            cp.wait()
        for slot in range(2):
            out_copy(slot, pl.ds(0, TM), abs_col(s, 0)).wait()

    n_loc = 2 + 5 * NQ
    return pl.pallas_call(
        body, name="ag_in_proj",
        in_specs=[VMEM_FULL, ANY, VMEM_FULL, VMEM_FULL, VMEM_FULL, VMEM_FULL, VMEM_FULL], out_specs=[ANY] * 5,
        out_shape=[jax.ShapeDtypeStruct((L, N_SHARD * Wc), BF16),
                   jax.ShapeDtypeStruct((L, Dh), BF16), jax.ShapeDtypeStruct((L, Dh), BF16),
                   jax.ShapeDtypeStruct((D, N_SHARD * Wc), BF16), jax.ShapeDtypeStruct((3, N_SHARD, 2, Do, D), BF16)],
        scratch_shapes=[pltpu.VMEM((N_SHARD, NQ, D, Wq), BF16), pltpu.VMEM((3, 2, Do, D), BF16), pltpu.VMEM((2, TM, Wq), BF16), pltpu.VMEM((TM, Dh), BF16),
                        pltpu.SemaphoreType.DMA((3 * NQ,)), pltpu.SemaphoreType.DMA((3 * NQ,)),
                        pltpu.SemaphoreType.DMA((3 * NQ,)), pltpu.SemaphoreType.DMA((3 * NQ,)),
                        pltpu.SemaphoreType.DMA((6,)), pltpu.SemaphoreType.DMA((6,)),
                        pltpu.SemaphoreType.DMA((NQ,)), pltpu.SemaphoreType.DMA((NQ,)),
                        pltpu.SemaphoreType.DMA((n_loc,)), pltpu.SemaphoreType.DMA((2,)), pltpu.SemaphoreType.DMA],
        compiler_params=_cparams(None, VMEM_LIMIT),
    )(xm, w_in_s, *w3_s, c2, s2)


def _pair_exchange_w3(dw3):
    _, _, _, Do, D = dw3.shape

    def build(ins, outs, send, recv):
        pos = _position()
        return [_remote(ins[0].at[:, :, 1 - pos[2]], outs[0], send, recv, 0, _peer(pos, 1))]

    return _Exchange((dw3,), (jax.ShapeDtypeStruct((3, N_SHARD, Do, D), F32),), 1, build)


def _sum_pair_in(dw_mine, ri):
    Dh, Wf = dw_mine.shape
    Wc = Wf // N_SHARD
    tr = min(256, Dh)

    def body(a_ref, b_ref, o_ref):
        o_ref[...] = (a_ref[...] + b_ref[...]).astype(BF16)

    return pl.pallas_call(
        body, name="sum_pair_in", grid=(Dh // tr, N_SHARD),
        in_specs=[pl.BlockSpec((tr, Wc), lambda i, t: (i, t)), pl.BlockSpec((tr, Wc), lambda i, t: (i, t))],
        out_specs=pl.BlockSpec((None, tr, Wc), lambda i, t: (t, i, 0)),
        out_shape=jax.ShapeDtypeStruct((N_SHARD, Dh, Wc), BF16),
        compiler_params=_cparams(("parallel", "parallel")),
    )(dw_mine, ri)


def _sum_pair_w3(cidx, dw3, r3):
    _, _, _, Do, D = dw3.shape

    def body(c_ref, a_ref, b_ref, o_ref):
        o_ref[...] = (a_ref[...] + b_ref[...]).astype(BF16)

    return pl.pallas_call(
        body, name="sum_pair_w3",
        grid_spec=pltpu.PrefetchScalarGridSpec(
            num_scalar_prefetch=1, grid=(3,),
            in_specs=[pl.BlockSpec((None, N_SHARD, None, Do, D), lambda a, c: (a, 0, c[0], 0, 0)),
                      pl.BlockSpec((None, N_SHARD, Do, D), lambda a, c: (a, 0, 0, 0))],
            out_specs=pl.BlockSpec((None, N_SHARD, Do, D), lambda a, c: (a, 0, 0, 0))),
        out_shape=jax.ShapeDtypeStruct((3, N_SHARD, Do, D), BF16),
        compiler_params=_cparams(("parallel",)),
    )(cidx, dw3, r3)


def _chips_exchange_in(cs_in):
    _, Dh, Wc = cs_in.shape

    def build(ins, outs, send, recv):
        pos = _position()
        return [_remote(ins[0].at[_shard_of(_peer(pos, k))], outs[0].at[j], send, recv, j, _peer(pos, k))
                for j, k in enumerate(CHIP_FLIPS)]

    return _Exchange((cs_in,), (jax.ShapeDtypeStruct((3, Dh, Wc), BF16),), 3, build)


def _chips_exchange_w3(cs_3):
    _, _, Do, D = cs_3.shape

    def build(ins, outs, send, recv):
        pos = _position()
        return [_remote(ins[0].at[:, _shard_of(_peer(pos, k))], outs[0].at[j], send, recv, j, _peer(pos, k))
                for j, k in enumerate(CHIP_FLIPS)]

    return _Exchange((cs_3,), (jax.ShapeDtypeStruct((3, 3, Do, D), BF16),), 3, build)


def _sum_chips_in(csidx, cs_in, rb_in):
    _, Dh, Wc = cs_in.shape
    tr = min(256, Dh)

    def body(s_ref, a_ref, b_ref, o_ref):
        acc = a_ref[...].astype(F32)
        for j in range(3):
            acc = acc + b_ref[j].astype(F32)
        o_ref[...] = acc

    return pl.pallas_call(
        body, name="sum_chips_in",
        grid_spec=pltpu.PrefetchScalarGridSpec(
            num_scalar_prefetch=1, grid=(Dh // tr,),
            in_specs=[pl.BlockSpec((None, tr, Wc), lambda i, s: (s[1], i, 0)),
                      pl.BlockSpec((3, tr, Wc), lambda i, s: (0, i, 0))],
            out_specs=pl.BlockSpec((None, tr, Wc), lambda i, s: (s[0], i, 0))),
        out_shape=jax.ShapeDtypeStruct((2, Dh, Wc), F32),
        compiler_params=_cparams(("parallel",)),
    )(csidx, cs_in, rb_in)


def _sum_chips_w3(csidx, cs_3, rb_3):
    _, _, Do, D = cs_3.shape

    def body(s_ref, a_ref, b_ref, o_ref):
        acc = a_ref[...].astype(F32)
        for j in range(3):
            acc = acc + b_ref[j].astype(F32)
        o_ref[...] = acc

    return pl.pallas_call(
        body, name="sum_chips_w3",
        grid_spec=pltpu.PrefetchScalarGridSpec(
            num_scalar_prefetch=1, grid=(3,),
            in_specs=[pl.BlockSpec((None, None, Do, D), lambda a, s: (a, s[1], 0, 0)),
                      pl.BlockSpec((3, None, Do, D), lambda a, s: (0, a, 0, 0))],
            out_specs=pl.BlockSpec((None, None, Do, D), lambda a, s: (a, s[0], 0, 0))),
        out_shape=jax.ShapeDtypeStruct((3, 2, Do, D), F32),
        compiler_params=_cparams(("parallel",)),
    )(csidx, cs_3, rb_3)


def _adam_math(w, g, m, v):
    m = ADAM_B1 * m + (1.0 - ADAM_B1) * g
    v = ADAM_B2 * v + (1.0 - ADAM_B2) * (g * g)
    m_hat = m / (1.0 - ADAM_B1 ** ADAM_STEP)
    v_hat = v / (1.0 - ADAM_B2 ** ADAM_STEP)
    delta = -ADAM_LR * (m_hat / (jnp.sqrt(v_hat) + ADAM_EPS) + ADAM_WD * w)
    return delta, m, v


def _adamw(w, g, m, v):
    R, C = w.shape
    tr = min(128, R)

    def body(w_ref, g_ref, m_ref, v_ref, d_ref, nm_ref, nv_ref):
        d_ref[...], nm_ref[...], nv_ref[...] = _adam_math(w_ref[...], g_ref[...], m_ref[...], v_ref[...])

    blk = pl.BlockSpec((tr, C), lambda i: (i, 0))
    return pl.pallas_call(
        body, name="adamw_w_in", grid=(R // tr,), in_specs=[blk] * 4, out_specs=[blk] * 3,
        out_shape=[jax.ShapeDtypeStruct((R, C), F32)] * 3,
        compiler_params=_cparams(("parallel",), VMEM_LIMIT),
    )(w, g, m, v)


SMALL_ROWS = ("c_ctx", "norm_w", "conv_b", "gn_w", "final_norm_w")


def _bwd_small(stats, ada_w, Dq, gh_in, gh_3):
    D = stats[0].shape[1]
    Wm = ada_w.shape[1]

    def body(stx, stm, stc, stv, stl, stlc, aw_ref, gi_in, g3_in, tot_ref, dm_sh, gcw, da_ref, gi_ref, g3_ref,
             vec_ref, vbuf, dm, amine, abuf, s_v, r_v, s_a, r_a, s_g, r_g):
        pos = _position()
        me, s = _dev_id(pos), _shard_of(pos)
        c, sib = pos[2], _peer(pos, 1)
        halves = [_remote(gi_in.at[c], gi_ref.at[c], s_g, r_g, 0, sib),
                  _remote(g3_in.at[:, c], g3_ref.at[:, c], s_g, r_g, 1, sib)]
        for cp in halves:
            cp.start()
        vec_ref[...] = jnp.zeros_like(vec_ref)
        vec_ref[0:2, :] = stx[0:2, :]
        vec_ref[2:3, :] = stm[1:2, :]
        vec_ref[3:5, :] = stc[0:2, :]
        vec_ref[5:6, :] = stx[2:3, :] + stc[2:3, :]
        vec_ref[6:7, :] = stv[3:4, :]
        vec_ref[7:8, :] = stm[3:4, :]
        vec_ref[8:9, :] = stm[0:1, :]
        vec_ref[9:12, :] = stv[0:3, :]
        vec_ref[12:14, 0:128] = stl[0:2, :] + stlc[0:2, :]
        vec_ref[14:15, :] = stm[2:3, :]
        vbuf[me] = vec_ref[...]
        sends = [_remote(vec_ref, vbuf.at[me], s_v, r_v, k - 1, _peer(pos, k)) for k in range(1, 8)]
        for cp in sends:
            cp.start()
        for k in range(1, 8):
            _remote(vec_ref, vbuf.at[_dev_id(_peer(pos, k))], s_v, r_v, k - 1, _peer(pos, k)).wait_recv()
        tot = vbuf[0]
        for d in range(1, N_DEV):
            tot = tot + vbuf[d]
        dm[...] = jnp.zeros_like(dm)
        for d in range(N_DEV):
            for r in range(3):
                dm[d:d + 1, r * D:(r + 1) * D] = vbuf[d, r:r + 1, :]
        dm[8:9, 0:D] = tot[3:4, :]
        dm[8:9, D:2 * D] = tot[4:5, :]
        for t in range(N_SHARD):
            @pl.when(s == t)
            def _(t=t):
                dm_sh[...] = dm[:, t * Wm:(t + 1) * Wm]
                gcw[...] = tot[9:12, t * Dq:(t + 1) * Dq]
        tot_ref[...] = tot
        part = lax.dot_general(dm_sh[8:16, :], aw_ref[...], (((1,), (1,)), ((), ())),
                               precision=lax.Precision.HIGHEST, preferred_element_type=F32)
        amine[...] = part
        abuf[s] = part
        asend = [_remote(amine, abuf.at[s], s_a, r_a, j, _peer(pos, k)) for j, k in enumerate(CHIP_FLIPS)]
        for cp in asend:
            cp.start()
        for j, k in enumerate(CHIP_FLIPS):
            _remote(amine, abuf.at[_shard_of(_peer(pos, k))], s_a, r_a, j, _peer(pos, k)).wait_recv()
        da = abuf[0]
        for t in range(1, N_SHARD):
            da = da + abuf[t]
        da_ref[...] = da
        _remote(gi_in.at[1 - c], gi_ref.at[1 - c], s_g, r_g, 0, sib).wait_recv()
        _remote(g3_in.at[:, 1 - c], g3_ref.at[:, 1 - c], s_g, r_g, 1, sib).wait_recv()
        for cp in sends + asend + halves:
            cp.wait_send()

    row = lambda *shape: jax.ShapeDtypeStruct(shape, F32)
    return pl.pallas_call(
        body, name="bwd_small",
        in_specs=[VMEM_FULL] * 7 + [ANY, ANY], out_specs=[VMEM_FULL] * 4 + [ANY, ANY],
        input_output_aliases={7: 4, 8: 5},
        out_shape=[row(16, D), row(16, Wm), row(3, Dq), row(8, D), row(*gh_in.shape), row(*gh_3.shape)],
        scratch_shapes=[pltpu.VMEM((16, D), F32), pltpu.VMEM((N_DEV, 16, D), F32), pltpu.VMEM((16, 3 * D), F32),
                        pltpu.VMEM((8, D), F32), pltpu.VMEM((N_SHARD, 8, D), F32),
                        pltpu.SemaphoreType.DMA((7,)), pltpu.SemaphoreType.DMA((7,)),
                        pltpu.SemaphoreType.DMA((3,)), pltpu.SemaphoreType.DMA((3,)),
                        pltpu.SemaphoreType.DMA((2,)), pltpu.SemaphoreType.DMA((2,))],
        compiler_params=_cparams(None, VMEM_LIMIT),
    )(*stats, ada_w, gh_in, gh_3)


ADAM_SLAB = 16


def _small_update(tot, dm_sh, gcw, da, act, g3, rows, ab, cw, dl, aw, w3):
    D = act.shape[1]
    Wm = dm_sh.shape[1]
    Dq = gcw.shape[1]
    H = dl[0].shape[1]
    params = tuple(rows) + (ab, cw, dl)
    shards = (aw,) + tuple(w3)
    n_p, n_s = len(params), len(shards)
    whole = (slice(None), slice(None))

    def body(tot_ref, dm_ref, gcw_ref, da_ref, act_ref, g3_hbm, *refs):
        wmv = [refs[3 * k:3 * k + 3] for k in range(n_p + n_s)]
        refs = refs[3 * (n_p + n_s):]
        gaw_ref, loss_ref = refs[0:2]
        outs, shard_outs = refs[2:2 + 4 * n_p], refs[2 + 4 * n_p:2 + 4 * n_p + 3 * n_s]
        g3_vm, *bufs, ld_sem, st_sem = refs[2 + 4 * n_p + 3 * n_s:]
        in_vm, out_vm = bufs[0:3 * n_s], bufs[3 * n_s:]
        o_q = [outs[n_p * q:n_p * (q + 1)] for q in range(4)]
        loads = [pltpu.make_async_copy(wmv[n_p + k][j], in_vm[3 * k + j], ld_sem.at[3 * k + j])
                 for k in range(n_s) for j in range(3)]
        load_g3 = pltpu.make_async_copy(g3_hbm, g3_vm, ld_sem.at[3 * n_s])
        for cp in loads + [load_g3]:
            cp.start()
        tot = tot_ref[...]
        gaw_ref[...] = lax.dot_general(act_ref[...], dm_ref[...], (((0,), (0,)), ((), ())),
                                       precision=lax.Precision.HIGHEST, preferred_element_type=F32)
        loss_ref[...] = (0.5 / D) * _sum_all(tot[14:15, :])
        cc = wmv[0][0][...]
        sg = _sigmoid(cc)
        g_cctx = da_ref[0:1, :] * (sg * (1.0 + cc * (1.0 - sg)))

        def emit(k, g, at=whole):
            w_ref, m_ref, v_ref = wmv[k]
            for q, val in enumerate((g,) + _adam_math(w_ref[at], g, m_ref[at], v_ref[at])):
                o_q[q][k][at] = val

        for k, g in enumerate([g_cctx, tot[5:6, :], tot[6:7, :], tot[7:8, :], tot[8:9, :]]):
            emit(k, g)
        for r, g in enumerate([tot[0:1, :] + tot[3:4, :], tot[1:2, :] + tot[4:5, :], tot[2:3, :]]):
            emit(n_p - 3, g, (slice(0, 1), slice(r * D, (r + 1) * D)))
        for r in range(3):
            emit(n_p - 2, gcw_ref[r:r + 1, :], (r, slice(None), slice(None)))
        emit(n_p - 1, tot[12:14, 0:H] * _sigmoid(-wmv[n_p - 1][0][...]))

        def shard_step(k, g_ref):
            w_ref, m_ref, v_ref = in_vm[3 * k:3 * k + 3]
            for cp in loads[3 * k:3 * k + 3]:
                cp.wait()

            def slab(i, carry):
                sl = pl.ds(pl.multiple_of(i * ADAM_SLAB, ADAM_SLAB), ADAM_SLAB)
                res = _adam_math(w_ref[sl, :], g_ref[sl, :], m_ref[sl, :], v_ref[sl, :])
                for o_ref, val in zip(out_vm[3 * k:3 * k + 3], res):
                    o_ref[sl, :] = val
                return carry

            lax.fori_loop(0, w_ref.shape[0] // ADAM_SLAB, slab, 0)
            stores = [pltpu.make_async_copy(out_vm[3 * k + j], shard_outs[3 * k + j], st_sem.at[3 * k + j])
                      for j in range(3)]
            for cp in stores:
                cp.start()
            return stores

        stores = shard_step(0, gaw_ref)
        load_g3.wait()
        for a in range(n_s - 1):
            stores += shard_step(1 + a, g3_vm.at[a])
        for cp in stores:
            cp.wait()

    row = lambda *shape: jax.ShapeDtypeStruct(shape, F32)
    per_q = [row(1, D)] * len(rows) + [row(1, 3 * D), row(3, 1, Dq), row(2, H)]
    per_shard = [row(*s[0].shape) for s in shards for _ in range(3)]
    held = [pltpu.VMEM(s[0].shape, F32) for s in shards for _ in range(3)]
    res = pl.pallas_call(
        body, name="small_update",
        in_specs=[VMEM_FULL] * 5 + [ANY] + [VMEM_FULL] * (3 * n_p) + [ANY] * (3 * n_s),
        out_specs=[VMEM_FULL] * (2 + 4 * n_p) + [ANY] * (3 * n_s),
        out_shape=[row(D, Wm), row(1, 1)] + per_q * 4 + per_shard,
        scratch_shapes=[pltpu.VMEM(g3.shape, F32)] + held + held
                       + [pltpu.SemaphoreType.DMA((3 * n_s + 1,)), pltpu.SemaphoreType.DMA((3 * n_s,))],
        compiler_params=_cparams(None, VMEM_LIMIT),
    )(tot, dm_sh, gcw, da, act, g3, *[a for p in params + shards for a in p])
    k = 2 + 4 * n_p
    return (res[0], res[1], [res[2 + n_p * q:2 + n_p * (q + 1)] for q in range(4)],
            [res[k + 3 * s:k + 3 * s + 3] for s in range(n_s)])


def kernel(x, c, ctx, c_ctx, norm_w, ada_w, ada_b, w_in, conv_w, conv_b, decay_logit, gn_w, w_a, w_b, w_out, final_norm_w, loss_target, m_c_ctx, m_norm_w, m_ada_w, m_ada_b, m_w_in, m_conv_w, m_conv_b, m_decay_logit, m_gn_w, m_w_a, m_w_b, m_w_out, m_final_norm_w, v_c_ctx, v_norm_w, v_ada_w, v_ada_b, v_w_in, v_conv_w, v_conv_b, v_decay_logit, v_gn_w, v_w_a, v_w_b, v_w_out, v_final_norm_w):
    L, D = x.shape[1], x.shape[2]
    Wc = w_in.shape[2]
    Do = D // 8
    pos = _position()
    cidx = jnp.reshape(pos[2], (1,)).astype(jnp.int32)
    sidx = jnp.reshape(_shard_of(pos), (1,)).astype(jnp.int32)

    act, mod_x, mod_c, conv_w8 = _fwd_small(c, c_ctx[None], ada_w[0], ada_b, conv_w[0])
    lg = jax.nn.log_sigmoid(decay_logit[0])

    w3_s = tuple(w[0].reshape(2, Do, D) for w in (w_a, w_b, w_out))

    def project(xm, c2, s2, w_in_bf):
        p, qr, kr, w_in_full, w3_full = _ag_in_proj(xm, w_in_bf, w3_s, c2, s2)
        return p, qr, kr, w_in_full, w3_full.reshape(3, D, D)

    csidx = jnp.concatenate([cidx, sidx])
    groups, dret_c, xmt, cmt, dx1, sc_x, w_in_full, gh_3, sts = _local_step(
        x[0], ctx[0], loss_target[0], mod_x, mod_c, norm_w, conv_w8, conv_b, lg, gn_w, final_norm_w[None],
        project, csidx, w_in[0])
    st_mid, st_conv, st_lg, st_lgc, st_c = sts

    dw_mine, ra_in = _dw_in(xmt, groups, cmt, dret_c, D, True)
    cs_in = _sum_pair_in(dw_mine, ra_in)
    grad_x, st_x, rb_in = _dxm(groups, 0, w_in_full, x[0], norm_w, sc_x, dx1, "dxm_x", _chips_exchange_in(cs_in))
    gh_in = _sum_chips_in(csidx, cs_in, rb_in)

    tot, dm_sh, gcw, da, g_in, g_3 = _bwd_small((st_x, st_mid, st_c, st_conv, st_lg, st_lgc), ada_w[0],
                                                conv_w.shape[2], gh_in, gh_3)
    g_w_in = g_in.reshape(D, Wc)
    g_3 = g_3.reshape(3, D // 4, D)
    rows = ((c_ctx[None], m_c_ctx[None], v_c_ctx[None]), (norm_w, m_norm_w, v_norm_w), (conv_b, m_conv_b, v_conv_b),
            (gn_w, m_gn_w, v_gn_w), (final_norm_w[None], m_final_norm_w[None], v_final_norm_w[None]))
    w3 = ((w_a[0], m_w_a[0], v_w_a[0]), (w_b[0], m_w_b[0], v_w_b[0]), (w_out[0], m_w_out[0], v_w_out[0]))
    g_ada_w, loss, small, (upd_ada, upd_a, upd_b, upd_o) = _small_update(
        tot, dm_sh, gcw, da, act, g_3, rows, (ada_b, m_ada_b, v_ada_b),
        tuple(jnp.transpose(a, (1, 0, 2)) for a in (conv_w, m_conv_w, v_conv_w)),
        (decay_logit[0], m_decay_logit[0], v_decay_logit[0]), (ada_w[0], m_ada_w[0], v_ada_w[0]), w3)

    upd_in = _adamw(w_in[0], g_w_in, m_w_in[0], v_w_in[0])

    def leaves(q):
        big = lambda g, upd: (g if q == 0 else upd[q - 1])[None]
        r_cctx, r_norm, r_convb, r_gn, r_fnorm, r_ab, r_cw, r_dl = small[q]
        return [r_cctx.reshape(D), r_norm, big(g_ada_w, upd_ada), r_ab, big(g_w_in, upd_in),
                jnp.transpose(r_cw, (1, 0, 2)), r_convb, r_dl[None], r_gn,
                big(g_3[0], upd_a), big(g_3[1], upd_b), big(g_3[2], upd_o), r_fnorm.reshape(D)]

    return (loss.reshape(()), grad_x[None], *leaves(0), *leaves(1), *leaves(2), *leaves(3))
```

```python
from typing import Callable, NamedTuple

import jax
import jax.numpy as jnp
import numpy as np
from jax import lax
from jax.experimental import pallas as pl
from jax.experimental.pallas import tpu as pltpu

F32 = jnp.float32
BF16 = jnp.bfloat16
MESH = pl.DeviceIdType.MESH

CHUNK = 128
RET_CPB = 4
DV = 128
DK = 64
GRID_W = 64
ROPE_BASE = 10000.0
EPS = 1e-6
K_SCALE = DK ** -0.5
N_SHARD = 4
N_DEV = 8

ADAM_LR = 0.001
ADAM_B1 = 0.9
ADAM_B2 = 0.999
ADAM_EPS = 1e-08
ADAM_WD = 0.01
ADAM_STEP = 10

VMEM_LIMIT = 56 * 1024 * 1024


def _cparams(sem=None, vmem=None):
    kw = {}
    if sem is not None:
        kw["dimension_semantics"] = sem
    if vmem is not None:
        kw["vmem_limit_bytes"] = vmem
    return pltpu.CompilerParams(**kw)


def _dot(a, b):
    return jnp.dot(a, b, preferred_element_type=F32)


def _dot_nt(a, b):
    return lax.dot_general(a, b, (((1,), (1,)), ((), ())), preferred_element_type=F32)


def _dot_tn(a, b):
    return lax.dot_general(a, b, (((0,), (0,)), ((), ())), preferred_element_type=F32)


def _sigmoid(x):
    return 1.0 / (1.0 + jnp.exp(-x))


def _sum_all(x):
    return jnp.sum(jnp.sum(x, axis=1, keepdims=True), axis=0, keepdims=True)


def _swap_halves(t):
    n = t.shape[1]
    lane = lax.broadcasted_iota(jnp.int32, t.shape, 1)
    low = (lane & 32) == 0
    return jnp.where(low, pltpu.roll(t, n - 32, 1), pltpu.roll(t, 32, 1))


def _vec_spec(d):
    return pl.BlockSpec((1, d), lambda *a: (0, 0))


def _norm_mod(x, nw, sc, sh, name, also_bf16=None):
    L, D = x.shape
    tl = min(256, L)
    nt = L // tl

    def body(x_ref, nw_ref, sc_ref, sh_ref, *rest):
        xm_ref, xmt_ref = rest[-3:-1] if also_bf16 is not None else rest
        xv = x_ref[...]
        r = lax.rsqrt(jnp.mean(xv * xv, axis=-1, keepdims=True) + EPS)
        xm = (xv * r * nw_ref[...]) * (1.0 + sc_ref[...]) + sh_ref[...]
        xm_b = xm.astype(BF16)
        xm_ref[...] = xm_b
        xmt_ref[...] = xm_b.T
        if also_bf16 is not None:
            rest[-1][...] = rest[0][...].astype(BF16)

    in_specs = [pl.BlockSpec((tl, D), lambda i: (i, 0)), _vec_spec(D), _vec_spec(D), _vec_spec(D)]
    out_specs = [pl.BlockSpec((tl, D), lambda i: (i, 0)), pl.BlockSpec((D, tl), lambda i: (0, i))]
    out_shape = [jax.ShapeDtypeStruct((L, D), BF16), jax.ShapeDtypeStruct((D, L), BF16)]
    args = [x, nw, sc, sh]
    if also_bf16 is not None:
        R, C = also_bf16.shape
        slab = pl.BlockSpec((R // nt, C), lambda i: (i, 0))
        in_specs.append(slab)
        out_specs.append(slab)
        out_shape.append(jax.ShapeDtypeStruct((R, C), BF16))
        args.append(also_bf16)
    return pl.pallas_call(
        body, name=name, grid=(nt,), in_specs=in_specs, out_specs=out_specs, out_shape=out_shape,
        compiler_params=_cparams(("parallel",)),
    )(*args)


QK_BLOCK, V_BLOCK = 4, 5


def _in_proj(xm, w, name, first=0, count=None):
    M, D = xm.shape
    count = w.shape[1] // D if count is None else count
    tm = min(1024, M)

    def body(a_ref, b_ref, o_ref, qk_ref):
        acc = _dot(a_ref[...], b_ref[...])
        o_ref[...] = acc.astype(o_ref.dtype)

        @pl.when(pl.program_id(1) == QK_BLOCK - first)
        def _():
            qk_ref[...] = acc

    return pl.pallas_call(
        body, name=name, grid=(M // tm, count),
        in_specs=[pl.BlockSpec((tm, D), lambda i, j: (i, 0)), pl.BlockSpec((D, D), lambda i, j: (0, first + j))],
        out_specs=[pl.BlockSpec((tm, D), lambda i, j: (i, j)), pl.BlockSpec((tm, D), lambda i, j: (i, 0))],
        out_shape=[jax.ShapeDtypeStruct((M, count * D), BF16), jax.ShapeDtypeStruct((M, D), F32)],
        compiler_params=_cparams(("parallel", "arbitrary")),
    )(xm, w)


def _halo_specs(tl, L, D, col):
    hb = tl // 16
    last = L // 16 - 1
    prev = pl.BlockSpec((16, D), lambda i: (jnp.maximum(i * hb - 1, 0), col))
    nxt = pl.BlockSpec((16, D), lambda i: (jnp.minimum((i + 1) * hb, last), col))
    return prev, nxt


def _shift_rows(u, above, below):
    tl = u.shape[0]
    row = lax.broadcasted_iota(jnp.int32, u.shape, 0)
    dn = jnp.where(row == 0, above, pltpu.roll(u, 1, 0))
    up = jnp.where(row == tl - 1, below, pltpu.roll(u, tl - 1, 0))
    return dn, up


def _rope_tables(L):
    pos = np.arange(L)
    row = (pos // GRID_W).astype(np.float32)
    col = (pos % GRID_W).astype(np.float32)
    nf = DK // 4
    inv = np.float32(ROPE_BASE) ** (-np.arange(nf, dtype=np.float32) / np.float32(nf))
    ang = np.concatenate([row[:, None] * inv, col[:, None] * inv], axis=-1).astype(np.float32)
    cos, sin = np.cos(ang), np.sin(ang)
    return (jnp.asarray(np.concatenate([cos, cos, cos, cos], axis=-1), F32),
            jnp.asarray(np.concatenate([-sin, sin, -sin, sin], axis=-1), F32))


def _smem_spec():
    return pl.BlockSpec(memory_space=pltpu.SMEM)


def _pair_select(e0, e1):
    row = lax.broadcasted_iota(jnp.int32, e0.shape, 0)
    return jnp.where(row < DK, e0, e1)


def _head_lane_mask(shape, e):
    lane = lax.broadcasted_iota(jnp.int32, shape, 1)
    return (lane < DK) if e == 0 else (lane >= DK)


def _ctx_states(pc, pqk_c, lg, D):
    Lc = pc.shape[0]
    H = D // DV

    def body(lg_ref, k_ref, v_ref, s_ref):
        m = lax.broadcasted_iota(jnp.int32, (Lc, DV), 0).astype(F32)
        for pr in range(H // 2):
            k2 = k_ref[:, pr * 128:(pr + 1) * 128].astype(F32) * K_SCALE
            res = [[None, None], [None, None]]
            for e in range(2):
                h = 2 * pr + e
                v = v_ref[:, h * DV:(h + 1) * DV]
                dec_f = jnp.exp(lg_ref[0, h] * (Lc - 1.0 - m))
                dec_b = jnp.exp(lg_ref[1, h] * m)
                res[0][e] = _dot_tn((k2 * dec_f).astype(BF16), v)
                res[1][e] = _dot_tn((k2 * dec_b).astype(BF16), v)
            s_ref[0, pr] = _pair_select(res[0][0], res[0][1])
            s_ref[1, pr] = _pair_select(res[1][0], res[1][1])

    return pl.pallas_call(
        body, name="ctx_states", grid=(1,),
        in_specs=[_smem_spec(), pl.BlockSpec((Lc, D // 2), lambda i: (0, 1)), pl.BlockSpec((Lc, D), lambda i: (0, 1))],
        out_specs=pl.BlockSpec((2, H // 2, 128, 128), lambda i: (0, 0, 0, 0)),
        out_shape=jax.ShapeDtypeStruct((2, H // 2, 128, 128), F32),
    )(lg, pqk_c, pc)


T_M, T_MT = 0, 1
T_MF1, T_MB1 = 2, 3
T_QF, T_QB = 4, 5
T_KF, T_KB = 6, 7


def _decay_tables(lg, H):
    def body(lg_ref, t_ref):
        h = pl.program_id(0)
        lgf, lgb = lg_ref[0, h], lg_ref[1, h]
        i = lax.broadcasted_iota(jnp.int32, (CHUNK, CHUNK), 0).astype(F32)
        j = lax.broadcasted_iota(jnp.int32, (CHUNK, CHUNK), 1).astype(F32)
        d = i - j
        mf = jnp.where(d > 0, jnp.exp(lgf * jnp.maximum(d, 0.0)), 0.0)
        mb = jnp.where(d < 0, jnp.exp(lgb * jnp.maximum(-d, 0.0)), 0.0)
        mf_t = jnp.where(d < 0, jnp.exp(lgf * jnp.maximum(-d, 0.0)), 0.0)
        mb_t = jnp.where(d > 0, jnp.exp(lgb * jnp.maximum(d, 0.0)), 0.0)
        diag = jnp.where(d == 0, 2.0, 0.0)
        t_ref[0, T_M] = mf + mb + diag
        t_ref[0, T_MT] = mf_t + mb_t + diag
        t_ref[0, T_MF1] = mf * d
        t_ref[0, T_MB1] = mb * (-d)
        t_ref[0, T_QF] = jnp.exp(lgf * (i + 1.0))
        t_ref[0, T_QB] = jnp.exp(lgb * (CHUNK - i))
        t_ref[0, T_KF] = jnp.exp(lgf * (CHUNK - 1.0 - i))
        t_ref[0, T_KB] = jnp.exp(lgb * i)

    return pl.pallas_call(
        body, name="decay_tables", grid=(H,), in_specs=[_smem_spec()],
        out_specs=pl.BlockSpec((1, 8, CHUNK, CHUNK), lambda h: (h, 0, 0, 0)),
        out_shape=jax.ShapeDtypeStruct((H, 8, CHUNK, CHUNK), F32),
    )(lg)


def _tab_spec(H):
    return pl.BlockSpec((H, 8, CHUNK, CHUNK), lambda n: (0, 0, 0, 0))


def _chunk_decay(tab_ref, h):
    return tab_ref[h, T_QF, CHUNK - 1:CHUNK, :], tab_ref[h, T_QB, 0:1, :]


def _ret_states(kr, p, s0, tab, D):
    L = kr.shape[0]
    H = D // DV
    N = L // CHUNK
    HP = H // 2

    def body(tab_ref, kf_ref, kb_ref, vf_ref, vb_ref, s0_ref, sf_out, sb_out, sf, sb):
        n = pl.program_id(0)

        @pl.when(n == 0)
        def _():
            sf[...] = s0_ref[0]
            sb[...] = s0_ref[1]

        for cc in range(RET_CPB):
            cf_, cb_ = cc, RET_CPB - 1 - cc
            rf, rb = slice(cf_ * CHUNK, (cf_ + 1) * CHUNK), slice(cb_ * CHUNK, (cb_ + 1) * CHUNK)
            sf_out[cf_] = sf[...]
            sb_out[cb_] = sb[...]
            for pr in range(HP):
                kf2 = kf_ref[rf, pr * 128:(pr + 1) * 128].astype(F32)
                kb2 = kb_ref[rb, pr * 128:(pr + 1) * 128].astype(F32)
                inc_f, inc_b, gf, gb = [], [], [], []
                for e in range(2):
                    h = 2 * pr + e
                    inc_f.append(_dot_tn((kf2 * tab_ref[h, T_KF]).astype(BF16), vf_ref[rf, h * DV:(h + 1) * DV]))
                    inc_b.append(_dot_tn((kb2 * tab_ref[h, T_KB]).astype(BF16), vb_ref[rb, h * DV:(h + 1) * DV]))
                    cf, cb = _chunk_decay(tab_ref, h)
                    gf.append(jnp.broadcast_to(cf, (128, 128)))
                    gb.append(jnp.broadcast_to(cb, (128, 128)))
                sf[pr] = _pair_select(gf[0], gf[1]) * sf[pr] + _pair_select(inc_f[0], inc_f[1])
                sb[pr] = _pair_select(gb[0], gb[1]) * sb[pr] + _pair_select(inc_b[0], inc_b[1])

    st = jax.ShapeDtypeStruct((N, HP, 128, 128), F32)
    R = RET_CPB * CHUNK
    NB = N // RET_CPB
    return _riding_call(
        body, None, NB, name="ret_states", args=(tab, kr, kr, p, p, s0),
        in_specs=[_tab_spec(H),
                  pl.BlockSpec((R, D // 2), lambda n: (n, 0)),
                  pl.BlockSpec((R, D // 2), lambda n: (NB - 1 - n, 0)),
                  pl.BlockSpec((R, D), lambda n: (n, 5)),
                  pl.BlockSpec((R, D), lambda n: (NB - 1 - n, 5)),
                  pl.BlockSpec((2, HP, 128, 128), lambda n: (0, 0, 0, 0))],
        out_specs=[pl.BlockSpec((RET_CPB, HP, 128, 128), lambda n: (n, 0, 0, 0)),
                   pl.BlockSpec((RET_CPB, HP, 128, 128), lambda n: (NB - 1 - n, 0, 0, 0))],
        out_shape=[st, st],
        scratch=[pltpu.VMEM((HP, 128, 128), F32), pltpu.VMEM((HP, 128, 128), F32)],
        cparams=_cparams(("arbitrary",)))


def _ret_out(qr, kr, p, sf_prev, sb_prev, gn_w, tab, D):
    L = qr.shape[0]
    H = D // DV
    N = L // CHUNK
    HP = H // 2

    def body(tab_ref, q_ref, k_ref, v_ref, zb_ref, sf_ref, sb_ref, gn_ref, o_ref, yb_ref):
        def chunk(cc, carry):
            rows = pl.ds(pl.multiple_of(cc * CHUNK, CHUNK), CHUNK)
            for pr in range(HP):
                q2 = q_ref[rows, pr * 128:(pr + 1) * 128]
                k2 = k_ref[rows, pr * 128:(pr + 1) * 128]
                sfp = sf_ref[cc, pr].astype(BF16)
                sbp = sb_ref[cc, pr].astype(BF16)
                for e in range(2):
                    h = 2 * pr + e
                    sl = slice(h * DV, (h + 1) * DV)
                    qm = jnp.where(_head_lane_mask(q2.shape, e), q2, jnp.zeros_like(q2))
                    a = (_dot_nt(qm, k2) * tab_ref[h, T_M]).astype(BF16)
                    qf = qm.astype(F32)
                    o = _dot(a, v_ref[rows, sl])
                    o += _dot((qf * tab_ref[h, T_QF]).astype(BF16), sfp)
                    o += _dot((qf * tab_ref[h, T_QB]).astype(BF16), sbp)
                    o_ref[rows, sl] = o
                    mu = jnp.mean(o, axis=-1, keepdims=True)
                    oc = o - mu
                    rstd = lax.rsqrt(jnp.mean(oc * oc, axis=-1, keepdims=True) + EPS)
                    zb = zb_ref[rows, sl].astype(F32)
                    yb_ref[rows, sl] = (zb * _sigmoid(zb) * (oc * rstd * gn_ref[:, sl])).astype(BF16)
            return carry

        lax.fori_loop(0, RET_CPB, chunk, 0)

    R = RET_CPB * CHUNK
    return _riding_call(
        body, None, N // RET_CPB, name="ret_out", args=(tab, qr, kr, p, p, sf_prev, sb_prev, gn_w),
        in_specs=[_tab_spec(H),
                  pl.BlockSpec((R, D // 2), lambda n: (n, 0)),
                  pl.BlockSpec((R, D // 2), lambda n: (n, 0)),
                  pl.BlockSpec((R, D), lambda n: (n, 5)),
                  pl.BlockSpec((R, D), lambda n: (n, 6)),
                  pl.BlockSpec((RET_CPB, HP, 128, 128), lambda n: (n, 0, 0, 0)),
                  pl.BlockSpec((RET_CPB, HP, 128, 128), lambda n: (n, 0, 0, 0)),
                  _vec_spec(D)],
        out_specs=[pl.BlockSpec((R, D), lambda n: (n, 0)), pl.BlockSpec((R, D), lambda n: (n, 0))],
        out_shape=[jax.ShapeDtypeStruct((L, D), F32), jax.ShapeDtypeStruct((L, D), BF16)],
        cparams=_cparams(("arbitrary",)))


def _mid(p, yb, o, x, tgt, w3, g, fw, conv_w, conv_b, gn_w, D):
    L = x.shape[0]
    H = D // DV
    tm = min(256, L)
    nt = L // tm

    def body(h_ref, bg_ref, cg_ref, za_ref, hp_ref, hn_ref, cp_ref, cn_ref, yb_ref, ga_ref, gb_ref, zb_ref, o_ref,
             x_ref, t_ref, w_hbm, g_ref, fw_ref, cw_ref, cb_ref, gn_ref,
             dx1_ref, dya_ref, do_ref, dzb_ref, dgab_ref, dw_hbm, st_ref, w_vm, dw_acc, sem):
        i = pl.program_id(0)

        @pl.when(i == 0)
        def _():
            cp = pltpu.make_async_copy(w_hbm, w_vm, sem)
            cp.start()
            dw_acc[...] = jnp.zeros_like(dw_acc)
            st_ref[...] = jnp.zeros_like(st_ref)
            cp.wait()

        u = cg_ref[...].astype(F32) * h_ref[...].astype(F32)
        above = jnp.where(i == 0, 0.0, cp_ref[15:16, :].astype(F32) * hp_ref[15:16, :].astype(F32))
        below = jnp.where(i == nt - 1, 0.0, cn_ref[0:1, :].astype(F32) * hn_ref[0:1, :].astype(F32))
        dn, up = _shift_rows(u, above, below)
        co = cw_ref[0:1, :] * dn + cw_ref[1:2, :] * u + cw_ref[2:3, :] * up + cb_ref[...]
        za = za_ref[...].astype(F32)
        ya_b = (za * _sigmoid(za) * bg_ref[...].astype(F32) * co).astype(BF16)
        yb_b = yb_ref[...]
        y_a = _dot(ya_b, w_vm[0])
        y_b = _dot(yb_b, w_vm[1])
        sga = _sigmoid(ga_ref[...].astype(F32))
        sgb = _sigmoid(gb_ref[...].astype(F32))
        mix_b = (sga * y_a + sgb * y_b).astype(BF16)
        y_x = _dot(mix_b, w_vm[2])
        gvec, fwv = g_ref[...], fw_ref[...]
        x1 = x_ref[...] + gvec * y_x
        r1 = lax.rsqrt(jnp.mean(x1 * x1, axis=-1, keepdims=True) + EPS)
        xh = x1 * r1
        diff = xh * fwv - t_ref[...]
        dout = diff * (1.0 / D)
        dxh = dout * fwv
        dx1 = r1 * (dxh - xh * jnp.mean(dxh * xh, axis=-1, keepdims=True))
        dx1_ref[...] = dx1
        st_ref[0:1, :] += jnp.sum(dout * xh, axis=0, keepdims=True)
        st_ref[1:2, :] += jnp.sum(dx1 * y_x, axis=0, keepdims=True)
        st_ref[2:3, :] += jnp.sum(diff * diff, axis=0, keepdims=True)
        dyx_b = (dx1 * gvec).astype(BF16)
        dmix = _dot_nt(dyx_b, w_vm[2])
        dw_acc[2] += _dot_tn(mix_b, dyx_b)
        dya_b = (dmix * sga).astype(BF16)
        dyb_b = (dmix * sgb).astype(BF16)
        dgab_ref[:, 0:D] = (dmix * y_a * sga * (1.0 - sga)).astype(BF16)
        dgab_ref[:, D:2 * D] = (dmix * y_b * sgb * (1.0 - sgb)).astype(BF16)
        dya_ref[...] = _dot_nt(dya_b, w_vm[0])
        dyb = _dot_nt(dyb_b, w_vm[1])
        dw_acc[0] += _dot_tn(ya_b, dya_b)
        dw_acc[1] += _dot_tn(yb_b, dyb_b)

        for h in range(H):
            sl = slice(h * DV, (h + 1) * DV)
            ov = o_ref[:, sl]
            oc = ov - jnp.mean(ov, axis=-1, keepdims=True)
            rstd = lax.rsqrt(jnp.mean(oc * oc, axis=-1, keepdims=True) + EPS)
            rn = oc * rstd
            gw = gn_ref[:, sl]
            zb = zb_ref[:, sl].astype(F32)
            sz = _sigmoid(zb)
            dy = dyb[:, sl]
            dzb_ref[:, sl] = (dy * (rn * gw) * (sz * (1.0 + zb * (1.0 - sz)))).astype(BF16)
            dretn = dy * (zb * sz)
            st_ref[3:4, sl] += jnp.sum(dretn * rn, axis=0, keepdims=True)
            drn = dretn * gw
            do_ref[:, sl] = (rstd * (drn - jnp.mean(drn, axis=-1, keepdims=True)
                                     - rn * jnp.mean(drn * rn, axis=-1, keepdims=True))).astype(BF16)

        @pl.when(i == nt - 1)
        def _():
            out = pltpu.make_async_copy(dw_acc, dw_hbm, sem)
            out.start()
            out.wait()

    row = lambda col: pl.BlockSpec((tm, D), lambda i: (i, col))
    any_spec = pl.BlockSpec(memory_space=pl.ANY)
    f32o = jax.ShapeDtypeStruct((L, D), F32)
    bf16o = jax.ShapeDtypeStruct((L, D), BF16)
    hp, hn = _halo_specs(tm, L, D, 0)
    cp, cn = _halo_specs(tm, L, D, 2)
    return pl.pallas_call(
        body, name="mid", grid=(nt,),
        in_specs=[row(0), row(1), row(2), row(3), hp, hn, cp, cn, row(0), row(7), row(8), row(6), row(0),
                  row(0), row(0), any_spec, _vec_spec(D), _vec_spec(D),
                  pl.BlockSpec((8, D), lambda i: (0, 0)), _vec_spec(D), _vec_spec(D)],
        out_specs=[row(0), row(0), row(0), row(0), pl.BlockSpec((tm, 2 * D), lambda i: (i, 0)), any_spec,
                   pl.BlockSpec((8, D), lambda i: (0, 0))],
        out_shape=[f32o, f32o, bf16o, bf16o, jax.ShapeDtypeStruct((L, 2 * D), BF16),
                   jax.ShapeDtypeStruct((3, D, D), F32), jax.ShapeDtypeStruct((8, D), F32)],
        scratch_shapes=[pltpu.VMEM((3, D, D), BF16), pltpu.VMEM((3, D, D), F32), pltpu.SemaphoreType.DMA],
        compiler_params=_cparams(("arbitrary",), VMEM_LIMIT),
    )(p, p, p, p, p, p, p, p, yb, p, p, p, o, x, tgt, w3, g, fw, conv_w, conv_b, gn_w)


def _conv_bwd(dya, p, conv_w, conv_b, D, exchange=None):
    L = p.shape[0]
    tl = min(256, L)
    nt = L // tl

    def body(d_ref, h_ref, bg_ref, cg_ref, za_ref,
             dp_ref, dn_ref, hp_ref, hn_ref, bp_ref, bn_ref, cp_ref, cn_ref, zp_ref, zn_ref,
             w_ref, b_ref, dc_ref, st_ref):
        i = pl.program_id(0)

        @pl.when(i == 0)
        def _():
            st_ref[...] = jnp.zeros_like(st_ref)

        first, last = i == 0, i == nt - 1
        h = h_ref[...].astype(F32)
        cg = cg_ref[...].astype(F32)
        bg = bg_ref[...].astype(F32)
        za = za_ref[...].astype(F32)
        dy = d_ref[...].astype(F32)
        u = cg * h
        u_above = jnp.where(first, 0.0, cp_ref[15:16, :].astype(F32) * hp_ref[15:16, :].astype(F32))
        u_below = jnp.where(last, 0.0, cn_ref[0:1, :].astype(F32) * hn_ref[0:1, :].astype(F32))
        u_dn, u_up = _shift_rows(u, u_above, u_below)
        w0, w1, w2 = w_ref[0:1, :], w_ref[1:2, :], w_ref[2:3, :]
        co = w0 * u_dn + w1 * u + w2 * u_up + b_ref[...]
        sz = _sigmoid(za)
        silu = za * sz
        dc_ref[:, 3 * D:4 * D] = (dy * bg * co * (sz * (1.0 + za * (1.0 - sz)))).astype(BF16)
        dc_ref[:, D:2 * D] = (dy * silu * co).astype(BF16)
        dco = dy * silu * bg

        def edge(dr, zr, br, r):
            z = zr[r:r + 1, :].astype(F32)
            return dr[r:r + 1, :].astype(F32) * (z * _sigmoid(z)) * br[r:r + 1, :].astype(F32)

        dco_above = jnp.where(first, 0.0, edge(dp_ref, zp_ref, bp_ref, 15))
        dco_below = jnp.where(last, 0.0, edge(dn_ref, zn_ref, bn_ref, 0))
        dco_dn, dco_up = _shift_rows(dco, dco_above, dco_below)
        du = w0 * dco_up + w1 * dco + w2 * dco_dn
        dc_ref[:, 2 * D:3 * D] = (du * h).astype(BF16)
        dc_ref[:, 0:D] = (du * cg).astype(BF16)
        st_ref[0:1, :] += jnp.sum(dco * u_dn, axis=0, keepdims=True)
        st_ref[1:2, :] += jnp.sum(dco * u, axis=0, keepdims=True)
        st_ref[2:3, :] += jnp.sum(dco * u_up, axis=0, keepdims=True)
        st_ref[3:4, :] += jnp.sum(dco, axis=0, keepdims=True)

    main = lambda col: pl.BlockSpec((tl, D), lambda i: (i, col))
    halos = []
    for col in (0, 0, 1, 2, 3):
        halos.extend(_halo_specs(tl, L, D, col))
    return _riding_call(
        body, exchange, nt, name="conv_bwd",
        args=(dya, p, p, p, p, dya, dya, p, p, p, p, p, p, p, p, conv_w, conv_b),
        in_specs=[main(0), main(0), main(1), main(2), main(3)] + halos
                 + [pl.BlockSpec((8, D), lambda i: (0, 0)), _vec_spec(D)],
        out_specs=[pl.BlockSpec((tl, 4 * D), lambda i: (i, 0)), pl.BlockSpec((8, D), lambda i: (0, 0))],
        out_shape=[jax.ShapeDtypeStruct((L, 4 * D), BF16), jax.ShapeDtypeStruct((8, D), F32)],
        cparams=_cparams(("arbitrary",)))


def _ret_bwd_states(qr, do, tab, D):
    L = qr.shape[0]
    H = D // DV
    N = L // CHUNK
    HP = H // 2

    def body(tab_ref, qf_ref, qb_ref, dof_ref, dob_ref, dsf_out, dsb_out, ds0_out, dsf, dsb):
        n = pl.program_id(0)

        @pl.when(n == 0)
        def _():
            dsf[...] = jnp.zeros_like(dsf)
            dsb[...] = jnp.zeros_like(dsb)

        for cc in range(RET_CPB):
            cf_, cb_ = RET_CPB - 1 - cc, cc
            rf, rb = slice(cf_ * CHUNK, (cf_ + 1) * CHUNK), slice(cb_ * CHUNK, (cb_ + 1) * CHUNK)
            dsf_out[cf_] = dsf[...]
            dsb_out[cb_] = dsb[...]
            for pr in range(HP):
                qf2 = qf_ref[rf, pr * 128:(pr + 1) * 128].astype(F32)
                qb2 = qb_ref[rb, pr * 128:(pr + 1) * 128].astype(F32)
                inc_f, inc_b, gf, gb = [], [], [], []
                for e in range(2):
                    h = 2 * pr + e
                    inc_f.append(_dot_tn((qf2 * tab_ref[h, T_QF]).astype(BF16), dof_ref[rf, h * DV:(h + 1) * DV]))
                    inc_b.append(_dot_tn((qb2 * tab_ref[h, T_QB]).astype(BF16), dob_ref[rb, h * DV:(h + 1) * DV]))
                    cf, cb = _chunk_decay(tab_ref, h)
                    gf.append(jnp.broadcast_to(cf, (128, 128)))
                    gb.append(jnp.broadcast_to(cb, (128, 128)))
                dsf[pr] = _pair_select(gf[0], gf[1]) * dsf[pr] + _pair_select(inc_f[0], inc_f[1])
                dsb[pr] = _pair_select(gb[0], gb[1]) * dsb[pr] + _pair_select(inc_b[0], inc_b[1])

        @pl.when(n == NB - 1)
        def _():
            ds0_out[0] = dsf[...]
            ds0_out[1] = dsb[...]

    st = jax.ShapeDtypeStruct((N, HP, 128, 128), F32)
    R = RET_CPB * CHUNK
    NB = N // RET_CPB
    return pl.pallas_call(
        body, name="ret_bwd_states", grid=(NB,),
        in_specs=[_tab_spec(H),
                  pl.BlockSpec((R, D // 2), lambda n: (NB - 1 - n, 0)),
                  pl.BlockSpec((R, D // 2), lambda n: (n, 0)),
                  pl.BlockSpec((R, D), lambda n: (NB - 1 - n, 0)),
                  pl.BlockSpec((R, D), lambda n: (n, 0))],
        out_specs=[pl.BlockSpec((RET_CPB, HP, 128, 128), lambda n: (NB - 1 - n, 0, 0, 0)),
                   pl.BlockSpec((RET_CPB, HP, 128, 128), lambda n: (n, 0, 0, 0)),
                   pl.BlockSpec((2, HP, 128, 128), lambda n: (0, 0, 0, 0))],
        out_shape=[st, st, jax.ShapeDtypeStruct((2, HP, 128, 128), F32)],
        scratch_shapes=[pltpu.VMEM((HP, 128, 128), F32), pltpu.VMEM((HP, 128, 128), F32)],
        compiler_params=_cparams(("arbitrary",)),
    )(tab, qr, qr, do, do)


def _ret_bwd_main(qr, kr, p, do, sf_prev, sb_prev, dsf, dsb, c2, s2, tab, D, exchange=None):
    L = qr.shape[0]
    H = D // DV
    N = L // CHUNK
    HP = H // 2
    W = D // 2

    def body(tab_ref, q_ref, k_ref, v_ref, do_ref, sf_ref, sb_ref, dsf_ref, dsb_ref, c_ref, s_ref,
             dr_ref, st_ref, dl_acc):
        @pl.when(pl.program_id(0) == 0)
        def _():
            dl_acc[...] = jnp.zeros_like(dl_acc)

        i = lax.broadcasted_iota(jnp.int32, (CHUNK, 128), 0).astype(F32)
        rowid = lax.broadcasted_iota(jnp.int32, (128, 128), 0)

        def chunk(cc, carry):
            rows = pl.ds(pl.multiple_of(cc * CHUNK, CHUNK), CHUNK)
            c, s = c_ref[rows, :], s_ref[rows, :]
            for pr in range(HP):
                ps = slice(pr * 128, (pr + 1) * 128)
                q2, k2 = q_ref[rows, ps], k_ref[rows, ps]
                sf32, sb32 = sf_ref[cc, pr], sb_ref[cc, pr]
                dsf32, dsb32 = dsf_ref[cc, pr], dsb_ref[cc, pr]
                sfp, sbp = sf32.astype(BF16), sb32.astype(BF16)
                dsfp, dsbp = dsf32.astype(BF16), dsb32.astype(BF16)
                dq2 = jnp.zeros((CHUNK, 128), F32)
                dk2 = jnp.zeros((CHUNK, 128), F32)
                for e in range(2):
                    h = 2 * pr + e
                    sl = slice(h * DV, (h + 1) * DV)
                    hm = _head_lane_mask(q2.shape, e)
                    qm = jnp.where(hm, q2, jnp.zeros_like(q2))
                    km = jnp.where(hm, k2, jnp.zeros_like(k2))
                    qf, kf = qm.astype(F32), km.astype(F32)
                    v, do = v_ref[rows, sl], do_ref[rows, sl]
                    vf, dof = v.astype(F32), do.astype(F32)
                    m_t = tab_ref[h, T_MT]
                    sc = _dot_nt(qm, k2)
                    dpm = _dot_nt(do, v)
                    dsc = (dpm * tab_ref[h, T_M]).astype(BF16)
                    a_t = (_dot_nt(km, q2) * m_t).astype(BF16)
                    dsc_t = (_dot_nt(v, do) * m_t).astype(BF16)
                    dq_f, dq_b = tab_ref[h, T_QF], tab_ref[h, T_QB]
                    dk_f, dk_b = tab_ref[h, T_KF], tab_ref[h, T_KB]
                    dq = _dot(dsc, km)
                    dq += jnp.where(hm, dq_f * _dot_nt(do, sfp) + dq_b * _dot_nt(do, sbp), 0.0)
                    dk = _dot(dsc_t, qm)
                    dk += jnp.where(hm, dk_f * _dot_nt(v, dsfp) + dk_b * _dot_nt(v, dsbp), 0.0)
                    kdf = _dot((kf * dk_f).astype(BF16), dsfp)
                    kdb = _dot((kf * dk_b).astype(BF16), dsbp)
                    dr_ref[rows, D + h * DV:D + (h + 1) * DV] = (_dot(a_t, do) + kdf + kdb).astype(BF16)
                    dq2 += dq
                    dk2 += dk
                    xf = _dot((qf * dq_f).astype(BF16), sfp)
                    xb = _dot((qf * dq_b).astype(BF16), sbp)
                    pair = (rowid < DK) if e == 0 else (rowid >= DK)
                    gcf, gcb = tab_ref[h, T_QF, CHUNK - 1:CHUNK, 0:1], tab_ref[h, T_QB, 0:1, 0:1]
                    scdp = sc * dpm
                    dl_acc[h, 0] += scdp * tab_ref[h, T_MF1] + xf * dof * (i + 1.0) \
                        + kdf * vf * (CHUNK - 1.0 - i) + (CHUNK * gcf) * jnp.where(pair, dsf32 * sf32, 0.0)
                    dl_acc[h, 1] += scdp * tab_ref[h, T_MB1] + xb * dof * (CHUNK - i) \
                        + kdb * vf * i + (CHUNK * gcb) * jnp.where(pair, dsb32 * sb32, 0.0)
                dr_ref[rows, ps] = (dq2 * c - _swap_halves(dq2) * s).astype(BF16)
                dr_ref[rows, W + pr * 128:W + (pr + 1) * 128] = \
                    ((dk2 * c - _swap_halves(dk2) * s) * K_SCALE).astype(BF16)
            return carry

        lax.fori_loop(0, RET_CPB, chunk, 0)

        @pl.when(pl.program_id(0) == N // RET_CPB - 1)
        def _():
            lane = lax.broadcasted_iota(jnp.int32, (1, 128), 1)
            acc = [jnp.zeros((1, 128), F32), jnp.zeros((1, 128), F32)]
            for h in range(H):
                for b in range(2):
                    acc[b] += jnp.where(lane == h, _sum_all(dl_acc[h, b]), 0.0)
            st_ref[...] = jnp.zeros_like(st_ref)
            st_ref[0:1, :] = acc[0]
            st_ref[1:2, :] = acc[1]

    R = RET_CPB * CHUNK
    st_spec = pl.BlockSpec((RET_CPB, HP, 128, 128), lambda n: (n, 0, 0, 0))
    half = pl.BlockSpec((R, W), lambda n: (n, 0))
    rope = pl.BlockSpec((R, 128), lambda n: (n, 0))
    return _riding_call(
        body, exchange, N // RET_CPB, name="ret_bwd_main",
        args=(tab, qr, kr, p, do, sf_prev, sb_prev, dsf, dsb, c2, s2),
        in_specs=[_tab_spec(H), half, half,
                  pl.BlockSpec((R, D), lambda n: (n, 5)),
                  pl.BlockSpec((R, D), lambda n: (n, 0)),
                  st_spec, st_spec, st_spec, st_spec, rope, rope],
        out_specs=[pl.BlockSpec((R, 2 * D), lambda n: (n, 0)),
                   pl.BlockSpec((8, 128), lambda n: (0, 0))],
        out_shape=[jax.ShapeDtypeStruct((L, 2 * D), BF16), jax.ShapeDtypeStruct((8, 128), F32)],
        scratch=[pltpu.VMEM((H, 2, CHUNK, 128), F32)],
        cparams=_cparams(("arbitrary",)))


def _ctx_bwd(pc, pqk_c, ds0, lg, D):
    Lc = pc.shape[0]
    H = D // DV
    HP = H // 2
    W = D // 2

    def body(lg_ref, k_ref, v_ref, ds_ref, dr_ref, st_ref):
        dqk_ref = dr_ref.at[:, 0:D]
        dv_ref = dr_ref.at[:, D:2 * D]
        m = lax.broadcasted_iota(jnp.int32, (Lc, 128), 0).astype(F32)
        lane = lax.broadcasted_iota(jnp.int32, (1, 128), 1)
        acc_f = jnp.zeros((1, 128), F32)
        acc_b = jnp.zeros((1, 128), F32)
        dqk_ref[:, 0:W] = jnp.zeros((Lc, W), BF16)
        for pr in range(HP):
            ps = slice(pr * 128, (pr + 1) * 128)
            k2 = k_ref[:, ps].astype(F32) * K_SCALE
            dsfp, dsbp = ds_ref[0, pr].astype(BF16), ds_ref[1, pr].astype(BF16)
            dk2 = jnp.zeros((Lc, 128), F32)
            for e in range(2):
                h = 2 * pr + e
                sl = slice(h * DV, (h + 1) * DV)
                hm = _head_lane_mask(k2.shape, e)
                km = jnp.where(hm, k2, 0.0)
                v = v_ref[:, sl]
                vf = v.astype(F32)
                dec_f = jnp.exp(lg_ref[0, h] * (Lc - 1.0 - m))
                dec_b = jnp.exp(lg_ref[1, h] * m)
                kdf = _dot((km * dec_f).astype(BF16), dsfp)
                kdb = _dot((km * dec_b).astype(BF16), dsbp)
                dv_ref[:, sl] = (kdf + kdb).astype(BF16)
                dk2 += jnp.where(hm, dec_f * _dot_nt(v, dsfp) + dec_b * _dot_nt(v, dsbp), 0.0)
                acc_f += jnp.where(lane == h, _sum_all(kdf * vf * (Lc - 1.0 - m)), 0.0)
                acc_b += jnp.where(lane == h, _sum_all(kdb * vf * m), 0.0)
            dqk_ref[:, W + pr * 128:W + (pr + 1) * 128] = (dk2 * K_SCALE).astype(BF16)
        st_ref[...] = jnp.zeros_like(st_ref)
        st_ref[0:1, :] = acc_f
        st_ref[1:2, :] = acc_b

    return pl.pallas_call(
        body, name="ctx_bwd", grid=(1,),
        in_specs=[_smem_spec(), pl.BlockSpec((Lc, W), lambda i: (0, 1)), pl.BlockSpec((Lc, D), lambda i: (0, 1)),
                  pl.BlockSpec((2, HP, 128, 128), lambda i: (0, 0, 0, 0))],
        out_specs=[pl.BlockSpec((Lc, 2 * D), lambda i: (0, 0)), pl.BlockSpec((8, 128), lambda i: (0, 0))],
        out_shape=[jax.ShapeDtypeStruct((Lc, 2 * D), BF16), jax.ShapeDtypeStruct((8, 128), F32)],
    )(lg, pqk_c, pc, ds0)


class _Exchange(NamedTuple):
    inputs: tuple
    out_shapes: tuple
    n_copies: int
    build: Callable


def _exchange_parts(exchange):
    if exchange is None:
        return [], [], [], [], []
    n = exchange.n_copies
    return (list(exchange.inputs), [ANY] * len(exchange.inputs), list(exchange.out_shapes),
            [ANY] * len(exchange.out_shapes), [pltpu.SemaphoreType.DMA((n,)), pltpu.SemaphoreType.DMA((n,))])


def _riding_call(body, exchange, n_steps, *, args, in_specs, out_specs, out_shape, name, cparams, scratch=()):
    ex_args, ex_in_specs, ex_shapes, ex_out_specs, ex_scratch = _exchange_parts(exchange)
    n_in, n_out, n_sc = len(args), len(out_shape), len(scratch)

    def riding(*refs):
        k = n_in + len(ex_args)
        ins, ex_in = refs[:n_in], refs[n_in:k]
        outs, ex_out = refs[k:k + n_out], refs[k + n_out:k + n_out + len(ex_shapes)]
        k += n_out + len(ex_shapes)
        own_scratch, ex_sems = refs[k:k + n_sc], refs[k + n_sc:]
        step = pl.program_id(0)
        if exchange is not None:
            @pl.when(step == 0)
            def _():
                for rc in exchange.build(ex_in, ex_out, *ex_sems):
                    rc.start()
        body(*ins, *outs, *own_scratch)
        if exchange is not None:
            @pl.when(step == n_steps - 1)
            def _():
                for rc in exchange.build(ex_in, ex_out, *ex_sems):
                    rc.wait()

    return tuple(pl.pallas_call(
        riding, name=name, grid=(n_steps,),
        in_specs=list(in_specs) + ex_in_specs, out_specs=list(out_specs) + ex_out_specs,
        out_shape=list(out_shape) + ex_shapes, scratch_shapes=list(scratch) + ex_scratch,
        compiler_params=cparams,
    )(*args, *ex_args))


def _dxm(groups, col0, w, x, nw, sc, dx1, name, exchange=None):
    L, D = x.shape
    tm = min(256, L)
    nt = L // tm
    ng = len(groups)
    widths = [g.shape[1] for g in groups]
    wtot = sum(widths)
    with_dx = dx1 is not None
    ex_args, ex_in_specs, ex_shapes, ex_out_specs, ex_scratch = _exchange_parts(exchange)
    n_in = ng + 4 + (1 if with_dx else 0)
    n_out = 2 if with_dx else 1

    def body(*refs):
        group_refs = refs[:ng]
        w_hbm, x_ref, nw_ref, sc_ref = refs[ng:ng + 4]
        ex_in = refs[n_in:n_in + len(ex_args)]
        outs = refs[n_in + len(ex_args):]
        if with_dx:
            dx1_ref, gx_ref, st_ref = refs[ng + 4], outs[0], outs[1]
        else:
            st_ref = outs[0]
        ex_out = outs[n_out:n_out + len(ex_shapes)]
        w_vm, sem = outs[n_out + len(ex_shapes):n_out + len(ex_shapes) + 2]
        ex_sems = outs[n_out + len(ex_shapes) + 2:]
        i = pl.program_id(0)

        @pl.when(i == 0)
        def _():
            cp = pltpu.make_async_copy(w_hbm.at[:, col0 * D:col0 * D + wtot], w_vm, sem)
            cp.start()
            if exchange is not None:
                for rc in exchange.build(ex_in, ex_out, *ex_sems):
                    rc.start()
            st_ref[...] = jnp.zeros_like(st_ref)
            cp.wait()

        dxm, off = None, 0
        for g_ref, wd in zip(group_refs, widths):
            part = _dot_nt(g_ref[...], w_vm[:, off:off + wd])
            dxm = part if dxm is None else dxm + part
            off += wd

        xv = x_ref[...]
        r = lax.rsqrt(jnp.mean(xv * xv, axis=-1, keepdims=True) + EPS)
        xh = xv * r
        nwv = nw_ref[...]
        dxn = dxm * (1.0 + sc_ref[...])
        st_ref[0:1, :] += jnp.sum(dxm, axis=0, keepdims=True)
        st_ref[1:2, :] += jnp.sum(dxm * (xh * nwv), axis=0, keepdims=True)
        st_ref[2:3, :] += jnp.sum(dxn * xh, axis=0, keepdims=True)
        if with_dx:
            dxh = dxn * nwv
            gx_ref[...] = dx1_ref[...] + r * (dxh - xh * jnp.mean(dxh * xh, axis=-1, keepdims=True))

        if exchange is not None:
            @pl.when(i == nt - 1)
            def _():
                for rc in exchange.build(ex_in, ex_out, *ex_sems):
                    rc.wait()

    row = pl.BlockSpec((tm, D), lambda i: (i, 0))
    in_specs = [pl.BlockSpec((tm, wd), lambda i: (i, 0)) for wd in widths] + [ANY, row, _vec_spec(D), _vec_spec(D)]
    out_specs = [pl.BlockSpec((8, D), lambda i: (0, 0))]
    out_shape = [jax.ShapeDtypeStruct((8, D), F32)]
    args = list(groups) + [w, x, nw, sc]
    if with_dx:
        in_specs.append(row)
        out_specs.insert(0, row)
        out_shape.insert(0, jax.ShapeDtypeStruct((L, D), F32))
        args.append(dx1)
    res = pl.pallas_call(
        body, name=name, grid=(nt,),
        in_specs=in_specs + ex_in_specs, out_specs=out_specs + ex_out_specs, out_shape=out_shape + ex_shapes,
        scratch_shapes=[pltpu.VMEM((D, wtot), BF16), pltpu.SemaphoreType.DMA] + ex_scratch,
        compiler_params=_cparams(("arbitrary",), VMEM_LIMIT),
    )(*args, *ex_args)
    gx = res[0] if with_dx else None
    return (gx, res[n_out - 1], *res[n_out:])


DW_TN = 512
DW_RING = 4


def _dw_in(xmt, groups, cmt, dr_c, D, pair):
    L = xmt.shape[1]
    Lc = cmt.shape[1]
    Dh = D // 2
    tn = min(DW_TN, D)
    nblk = [g.shape[1] // tn for g in groups]
    starts = [sum(nblk[:g]) for g in range(len(groups))]
    ng = len(groups)
    nj = sum(nblk)
    rows_out = Dh if pair else D

    def body(*refs):
        xt_hbm = refs[0]
        group_refs = refs[1:1 + ng]
        ct_hbm, drc_ref, o_ref = refs[1 + ng:4 + ng]
        rest = refs[4 + ng:]
        if pair:
            ra_hbm, xt_vm, ct_vm, loc, ring, s_send, s_recv = rest
            pos = _position()
            sib = _peer(pos, 1)
        else:
            xt_vm, ct_vm, loc = rest
        j = pl.program_id(0)

        @pl.when(j == 0)
        def _():
            if pair:
                c = pos[2]
                other = pl.ds(pl.multiple_of((1 - c) * Dh, Dh), Dh)
                mine = pl.ds(pl.multiple_of(c * Dh, Dh), Dh)
                cps = [pltpu.make_async_copy(xt_hbm.at[other, :], xt_vm.at[0:Dh, :], loc.at[0]),
                       pltpu.make_async_copy(xt_hbm.at[mine, :], xt_vm.at[Dh:D, :], loc.at[1]),
                       pltpu.make_async_copy(ct_hbm.at[other, :], ct_vm.at[0:Dh, :], loc.at[2]),
                       pltpu.make_async_copy(ct_hbm.at[mine, :], ct_vm.at[Dh:D, :], loc.at[3])]
            else:
                cps = [pltpu.make_async_copy(xt_hbm, xt_vm, loc.at[0]), pltpu.make_async_copy(ct_hbm, ct_vm, loc.at[1])]
            for cp in cps:
                cp.start()
            for cp in cps:
                cp.wait()

        def send(slot):
            cols = pl.ds(pl.multiple_of(j * tn, 128), tn)
            return pltpu.make_async_remote_copy(src_ref=ring.at[slot], dst_ref=ra_hbm.at[:, cols],
                                                send_sem=s_send.at[slot], recv_sem=s_recv,
                                                device_id=sib, device_id_type=MESH)

        for g in range(ng):
            @pl.when((j >= starts[g]) & (j < starts[g] + nblk[g]))
            def _(g=g):
                acc = _dot(xt_vm[...], group_refs[g][...])
                if g == 1:
                    acc += _dot(ct_vm[...], drc_ref[...])
                if not pair:
                    o_ref[...] = acc
                    return
                o_ref[...] = acc[Dh:, :]
                slot = lax.rem(j, DW_RING)

                @pl.when(j >= DW_RING)
                def _():
                    send(slot).wait_send()

                ring[slot] = acc[0:Dh, :]
                send(slot).start()

        if pair:
            @pl.when(j == nj - 1)
            def _():
                pltpu.make_async_remote_copy(src_ref=ra_hbm, dst_ref=ra_hbm, send_sem=s_send.at[0], recv_sem=s_recv,
                                             device_id=sib, device_id_type=MESH).wait_recv()
                for slot in range(DW_RING):
                    send(slot).wait_send()

    def group_spec(g, rows):
        return pl.BlockSpec((rows, tn), lambda j: (0, jnp.clip(j - starts[g], 0, nblk[g] - 1)))

    out_specs = [pl.BlockSpec((rows_out, tn), lambda j: (0, j))]
    out_shape = [jax.ShapeDtypeStruct((rows_out, nj * tn), F32)]
    scratch = [pltpu.VMEM((D, L), BF16), pltpu.VMEM((D, Lc), BF16), pltpu.SemaphoreType.DMA((4,))]
    if pair:
        out_specs.append(ANY)
        out_shape.append(jax.ShapeDtypeStruct((Dh, nj * tn), F32))
        scratch += [pltpu.VMEM((DW_RING, Dh, tn), F32), pltpu.SemaphoreType.DMA((DW_RING,)), pltpu.SemaphoreType.DMA]
    return tuple(pl.pallas_call(
        body, name="dw_in", grid=(nj,),
        in_specs=[ANY] + [group_spec(g, L) for g in range(ng)] + [ANY, group_spec(1, Lc)],
        out_specs=out_specs, out_shape=out_shape, scratch_shapes=scratch,
        compiler_params=_cparams(("arbitrary",), VMEM_LIMIT),
    )(xmt, *groups, cmt, dr_c))


def _local_step(x, ctx, tgt, mod_x, mod_c, norm_w, conv_w8, conv_b, lg, gn_w, fw, project, csidx=None, w_in_s=None):
    L, D = x.shape
    sh_x, sc_x, g_x = mod_x
    sh_c, sc_c = mod_c
    c2, s2 = _rope_tables(L)
    tab = _decay_tables(lg, D // DV)

    xm, xmt, *w_bf = _norm_mod(x, norm_w, sc_x, sh_x, "norm_mod_x", w_in_s)
    cm, cmt = _norm_mod(ctx, norm_w, sc_c, sh_c, "norm_mod_ctx")
    reduce = csidx is not None
    p, qr, kr, w_in, w3 = project(xm, c2, s2, *w_bf)
    pc, pqk_c = _in_proj(cm, w_in, "in_proj_ctx", QK_BLOCK, 2)
    s0 = _ctx_states(pc, pqk_c, lg, D)
    sf_prev, sb_prev = _ret_states(kr, p, s0, tab, D)
    o, yb = _ret_out(qr, kr, p, sf_prev, sb_prev, gn_w, tab, D)
    dx1, dya, do, dzb, dgab, dw3, st_mid = _mid(p, yb, o, x, tgt, w3, g_x, fw, conv_w8, conv_b, gn_w, D)
    dw3_5 = dw3.reshape(3, N_SHARD, 2, D // 8, D)
    dconv, st_conv, *ra_3 = _conv_bwd(dya, p, conv_w8, conv_b, D, _pair_exchange_w3(dw3_5) if reduce else None)
    dsf, dsb, ds0 = _ret_bwd_states(qr, do, tab, D)
    cs_3 = _sum_pair_w3(csidx[0:1], dw3_5, ra_3[0]) if reduce else None
    dret, st_lg, *rb_3 = _ret_bwd_main(qr, kr, p, do, sf_prev, sb_prev, dsf, dsb, c2, s2, tab, D,
                                       _chips_exchange_w3(cs_3) if reduce else None)
    g_3 = _sum_chips_w3(csidx, cs_3, rb_3[0]) if reduce else dw3
    dret_c, st_lgc = _ctx_bwd(pc, pqk_c, ds0, lg, D)
    groups = (dconv, dret, dzb, dgab)
    _, st_c = _dxm((dret_c,), 4, w_in, ctx, norm_w, sc_c, None, "dxm_ctx")
    return groups, dret_c, xmt, cmt, dx1, sc_x, w_in, g_3, (st_mid, st_conv, st_lg, st_lgc, st_c)


CHIP_FLIPS = (4, 2, 6)
ANY = pl.BlockSpec(memory_space=pl.ANY)
VMEM_FULL = pl.BlockSpec(memory_space=pltpu.VMEM)


def _position():
    return lax.axis_index("x"), lax.axis_index("y"), lax.axis_index("c")


def _peer(pos, k):
    x, y, c = pos
    return (1 - x if k & 4 else x, 1 - y if k & 2 else y, 1 - c if k & 1 else c)


def _dev_id(pos):
    return 4 * pos[0] + 2 * pos[1] + pos[2]


def _shard_of(pos):
    return 2 * pos[0] + pos[1]


def _remote(src, dst, send_sems, recv_sems, idx, to):
    return pltpu.make_async_remote_copy(src_ref=src, dst_ref=dst, send_sem=send_sems.at[idx],
                                        recv_sem=recv_sems.at[idx], device_id=to, device_id_type=MESH)


def _dot_f32(a, b):
    return jnp.dot(a, b, precision=lax.Precision.HIGHEST, preferred_element_type=F32)


def _silu(x):
    return x * _sigmoid(x)


def _fwd_small(c, c_ctx, ada_w, ada_b, conv_w):
    D = c.shape[1]
    Wm = ada_w.shape[1]
    Dq = conv_w.shape[1]

    def body(c_ref, cc_ref, aw_hbm, ab_ref, cw_ref, act_ref, shx_ref, scx_ref, gx_ref, shc_ref, scc_ref, cwf_ref,
             mod_ref, cbuf, pmine, pbuf, wbuf, aw_ref, s_c, r_c, s_p, r_p, s_w, r_w, s_aw):
        pos = _position()
        me, s = _dev_id(pos), _shard_of(pos)
        cbuf[me] = c_ref[...]
        wbuf[s] = cw_ref[...]
        sends = [_remote(c_ref, cbuf.at[me], s_c, r_c, k - 1, _peer(pos, k)) for k in range(1, 8)]
        sends += [_remote(cw_ref, wbuf.at[s], s_w, r_w, j, _peer(pos, k)) for j, k in enumerate(CHIP_FLIPS)]
        for cp in sends:
            cp.start()
        load_aw = pltpu.make_async_copy(aw_hbm, aw_ref, s_aw)
        load_aw.start()
        for k in range(1, 8):
            _remote(c_ref, cbuf.at[_dev_id(_peer(pos, k))], s_c, r_c, k - 1, _peer(pos, k)).wait_recv()
        for d in range(N_DEV):
            act_ref[d:d + 1, :] = _silu(cbuf[d])
        act_ref[8:9, :] = _silu(cc_ref[...])
        act_ref[9:16, :] = jnp.zeros((7, D), F32)
        load_aw.wait()
        part = _dot_f32(act_ref[...], aw_ref[...])
        pmine[...] = part
        pbuf[s] = part
        psend = [_remote(pmine, pbuf.at[s], s_p, r_p, j, _peer(pos, k)) for j, k in enumerate(CHIP_FLIPS)]
        for cp in psend:
            cp.start()
        for j, k in enumerate(CHIP_FLIPS):
            t = _shard_of(_peer(pos, k))
            _remote(pmine, pbuf.at[t], s_p, r_p, j, _peer(pos, k)).wait_recv()
            _remote(cw_ref, wbuf.at[t], s_w, r_w, j, _peer(pos, k)).wait_recv()
        cwf_ref[...] = jnp.zeros_like(cwf_ref)
        for t in range(N_SHARD):
            mod_ref[:, t * Wm:(t + 1) * Wm] = pbuf[t] + ab_ref[:, t * Wm:(t + 1) * Wm]
            cwf_ref[0:3, t * Dq:(t + 1) * Dq] = wbuf[t]
        for r, o_ref in enumerate((shx_ref, scx_ref, gx_ref)):
            o_ref[...] = mod_ref[pl.ds(me, 1), r * D:(r + 1) * D]
        for r, o_ref in enumerate((shc_ref, scc_ref)):
            o_ref[...] = mod_ref[8:9, r * D:(r + 1) * D]
        for cp in sends + psend:
            cp.wait_send()

    row = jax.ShapeDtypeStruct((1, D), F32)
    res = pl.pallas_call(
        body, name="fwd_small",
        in_specs=[VMEM_FULL, VMEM_FULL, ANY, VMEM_FULL, VMEM_FULL], out_specs=[VMEM_FULL] * 7,
        out_shape=[jax.ShapeDtypeStruct((16, D), F32)] + [row] * 5 + [jax.ShapeDtypeStruct((8, D), F32)],
        scratch_shapes=[pltpu.VMEM((16, 3 * D), F32), pltpu.VMEM((N_DEV, 1, D), F32), pltpu.VMEM((16, Wm), F32),
                        pltpu.VMEM((N_SHARD, 16, Wm), F32), pltpu.VMEM((N_SHARD, 3, Dq), F32),
                        pltpu.VMEM((D, Wm), F32),
                        pltpu.SemaphoreType.DMA((7,)), pltpu.SemaphoreType.DMA((7,)),
                        pltpu.SemaphoreType.DMA((3,)), pltpu.SemaphoreType.DMA((3,)),
                        pltpu.SemaphoreType.DMA((3,)), pltpu.SemaphoreType.DMA((3,)), pltpu.SemaphoreType.DMA],
        compiler_params=_cparams(None, VMEM_LIMIT),
    )(c, c_ctx, ada_w, ada_b, conv_w)
    return res[0], tuple(res[1:4]), tuple(res[4:6]), res[6]


AG_CHUNKS = 3


def _ag_in_proj(xm, w_in_s, w3_s, c2, s2):
    L, D = xm.shape
    Wc = w_in_s.shape[1]
    Wq = Wc // AG_CHUNKS
    Dh = D // 2
    Do = w3_s[0].shape[1]
    TM = min(1024, L // 4)
    NT = L // TM
    NQ = AG_CHUNKS
    order = [(q, j) for q in range(NQ) for j in (0, 1)] + [(q, 2) for q in range(NQ)]

    def body(xm_ref, wi_hbm, wa_ref, wb_ref, wo_ref, c_ref, s_ref, p_hbm, qr_hbm, kr_hbm, fi_hbm, f3_hbm,
             w_vm, s3, stage, qk_stage, ici_s, ici_r, d2d_s, d2d_r, w3_s_, w3_r_, fw_s, fw_r,
             loc, out_sem, qk_sem):
        pos = _position()
        c = pos[2]
        s = _shard_of(pos)
        sib = _peer(pos, 1)
        mine = pl.ds(pl.multiple_of(c * Dh, Dh), Dh)
        other = pl.ds(pl.multiple_of((1 - c) * Dh, Dh), Dh)

        def abs_col(t, q):
            return pl.ds(pl.multiple_of(t * Wc + q * Wq, 128), Wq)

        own = [pltpu.make_async_copy(wi_hbm.at[:, q * Wq:(q + 1) * Wq], w_vm.at[0, q], loc.at[2 + 4 * NQ + q])
               for q in range(NQ)]
        for cp in own:
            cp.start()
        for cp in own:
            cp.wait()
        sends = [_remote(w_vm.at[0, q, mine, :], w_vm.at[1 + j, q, mine, :], ici_s, ici_r, q * 3 + j,
                         _peer(pos, CHIP_FLIPS[j])) for q, j in order if j < 2]
        for cp in sends:
            cp.start()
        for a, w_ref in enumerate((wa_ref, wb_ref, wo_ref)):
            s3[a] = w_ref[...].astype(BF16)
        w3_sends = [_remote(s3.at[:, c], f3_hbm.at[:, s, c], w3_s_, w3_r_, j, _peer(pos, k))
                    for j, k in enumerate(CHIP_FLIPS)]
        local = [pltpu.make_async_copy(s3, f3_hbm.at[:, s], loc.at[1])]
        local += [pltpu.make_async_copy(w_vm.at[0, q], fi_hbm.at[:, abs_col(s, q)], loc.at[2 + q]) for q in range(NQ)]
        for cp in local:
            cp.start()

        def out_copy(slot, rows, cols):
            return pltpu.make_async_copy(stage.at[slot], p_hbm.at[rows, cols], out_sem.at[slot])

        def block(r, q, t, first):
            cols = abs_col(t, q)

            def row_tile(rt, carry):
                rows = pl.ds(pl.multiple_of(rt * TM, TM), TM)
                acc = _dot(xm_ref[rows, :], w_vm[r, q])
                slot = lax.rem(rt, 2)

                @pl.when(rt >= 2 if first else rt >= 0)
                def _():
                    out_copy(slot, rows, cols).wait()

                stage[slot] = acc.astype(BF16)
                out_copy(slot, rows, cols).start()

                def rotary(lo, scale, dst_hbm):
                    c, s = c_ref[rows, :], s_ref[rows, :]
                    for pr in range(Dh // 128):
                        tq = acc[:, lo + pr * 128:lo + (pr + 1) * 128] * scale
                        qk_stage[:, pr * 128:(pr + 1) * 128] = (tq * c + _swap_halves(tq) * s).astype(BF16)
                    cp = pltpu.make_async_copy(qk_stage, dst_hbm.at[rows, :], qk_sem)
                    cp.start()
                    cp.wait()

                if q == NQ - 1:
                    @pl.when(t == 1)
                    def _():
                        rotary(Wq - Dh, 1.0, qr_hbm)
                if q == 0:
                    @pl.when(t == 2)
                    def _():
                        rotary(0, K_SCALE, kr_hbm)
                return carry

            lax.fori_loop(0, NT, row_tile, 0)

        passed = []

        def hand_on(q, j):
            half = w_vm.at[1 + j, q, mine, :]
            if j == 2:
                _remote(half, half, fw_s, fw_r, q, sib).wait_recv()
            else:
                _remote(half, half, ici_s, ici_r, q * 3 + j, sib).wait_recv()

                @pl.when(c == (0 if j == q % 2 else 1))
                def _():
                    _remote(half, w_vm.at[3, q, mine, :], fw_s, fw_r, q, _peer(pos, CHIP_FLIPS[1 - j])).start()
            fwd = _remote(half, half, d2d_s, d2d_r, q * 3 + j, sib)
            fwd.start()
            passed.append(fwd)

        for q in range(NQ):
            if q == NQ - 1:
                hand_on(*order[0])
            block(0, q, s, q == 0)
        for n, (q, j) in enumerate(order):
            r, idx = 1 + j, q * 3 + j
            t = _shard_of(_peer(pos, CHIP_FLIPS[j]))
            if n + 1 < len(order):
                hand_on(*order[n + 1])
            if n + 1 == 2 * NQ - 1:
                for cp in w3_sends:
                    cp.start()
            _remote(w_vm.at[r, q, other, :], w_vm.at[r, q, other, :], d2d_s, d2d_r, idx, sib).wait_recv()
            block(r, q, t, False)
            cp = pltpu.make_async_copy(w_vm.at[r, q], fi_hbm.at[:, abs_col(t, q)], loc.at[2 + NQ + idx])
            cp.start()
            local.append(cp)
        for j, k in enumerate(CHIP_FLIPS):
            t = _shard_of(_peer(pos, k))
            _remote(s3.at[:, c], f3_hbm.at[:, t, c], w3_s_, w3_r_, j, sib).wait_recv()
            fwd = _remote(f3_hbm.at[:, t, c], f3_hbm.at[:, t, c], w3_s_, w3_r_, 3 + j, sib)
            fwd.start()
            passed.append(fwd)
        for j, k in enumerate(CHIP_FLIPS):
            t = _shard_of(_peer(pos, k))
            _remote(s3.at[:, c], f3_hbm.at[:, t, 1 - c], w3_s_, w3_r_, 3 + j, sib).wait_recv()
        for cp in sends + w3_sends + passed:
            cp.wait_send()
        for q in range(NQ):
            _remote(w_vm.at[1, q, mine, :], w_vm.at[3, q, mine, :], fw_s, fw_r, q, sib).wait_send()
        for cp in local:
            cp.wait()
        for slot in range(2):
            out_copy(slot, pl.ds(0, TM), abs_col(s, 0)).wait()

    n_loc = 2 + 5 * NQ
    return pl.pallas_call(
        body, name="ag_in_proj",
        in_specs=[VMEM_FULL, ANY, VMEM_FULL, VMEM_FULL, VMEM_FULL, VMEM_FULL, VMEM_FULL], out_specs=[ANY] * 5,
        out_shape=[jax.ShapeDtypeStruct((L, N_SHARD * Wc), BF16),
                   jax.ShapeDtypeStruct((L, Dh), BF16), jax.ShapeDtypeStruct((L, Dh), BF16),
                   jax.ShapeDtypeStruct((D, N_SHARD * Wc), BF16), jax.ShapeDtypeStruct((3, N_SHARD, 2, Do, D), BF16)],
        scratch_shapes=[pltpu.VMEM((N_SHARD, NQ, D, Wq), BF16), pltpu.VMEM((3, 2, Do, D), BF16), pltpu.VMEM((2, TM, Wq), BF16), pltpu.VMEM((TM, Dh), BF16),
                        pltpu.SemaphoreType.DMA((3 * NQ,)), pltpu.SemaphoreType.DMA((3 * NQ,)),
                        pltpu.SemaphoreType.DMA((3 * NQ,)), pltpu.SemaphoreType.DMA((3 * NQ,)),
                        pltpu.SemaphoreType.DMA((6,)), pltpu.SemaphoreType.DMA((6,)),
                        pltpu.SemaphoreType.DMA((NQ,)), pltpu.SemaphoreType.DMA((NQ,)),
                        pltpu.SemaphoreType.DMA((n_loc,)), pltpu.SemaphoreType.DMA((2,)), pltpu.SemaphoreType.DMA],
        compiler_params=_cparams(None, VMEM_LIMIT),
    )(xm, w_in_s, *w3_s, c2, s2)


def _pair_exchange_w3(dw3):
    _, _, _, Do, D = dw3.shape

    def build(ins, outs, send, recv):
        pos = _position()
        return [_remote(ins[0].at[:, :, 1 - pos[2]], outs[0], send, recv, 0, _peer(pos, 1))]

    return _Exchange((dw3,), (jax.ShapeDtypeStruct((3, N_SHARD, Do, D), F32),), 1, build)


def _sum_pair_in(dw_mine, ri):
    Dh, Wf = dw_mine.shape
    Wc = Wf // N_SHARD
    tr = min(256, Dh)

    def body(a_ref, b_ref, o_ref):
        o_ref[...] = (a_ref[...] + b_ref[...]).astype(BF16)

    return pl.pallas_call(
        body, name="sum_pair_in", grid=(Dh // tr, N_SHARD),
        in_specs=[pl.BlockSpec((tr, Wc), lambda i, t: (i, t)), pl.BlockSpec((tr, Wc), lambda i, t: (i, t))],
        out_specs=pl.BlockSpec((None, tr, Wc), lambda i, t: (t, i, 0)),
        out_shape=jax.ShapeDtypeStruct((N_SHARD, Dh, Wc), BF16),
        compiler_params=_cparams(("parallel", "parallel")),
    )(dw_mine, ri)


def _sum_pair_w3(cidx, dw3, r3):
    _, _, _, Do, D = dw3.shape

    def body(c_ref, a_ref, b_ref, o_ref):
        o_ref[...] = (a_ref[...] + b_ref[...]).astype(BF16)

    return pl.pallas_call(
        body, name="sum_pair_w3",
        grid_spec=pltpu.PrefetchScalarGridSpec(
            num_scalar_prefetch=1, grid=(3,),
            in_specs=[pl.BlockSpec((None, N_SHARD, None, Do, D), lambda a, c: (a, 0, c[0], 0, 0)),
                      pl.BlockSpec((None, N_SHARD, Do, D), lambda a, c: (a, 0, 0, 0))],
            out_specs=pl.BlockSpec((None, N_SHARD, Do, D), lambda a, c: (a, 0, 0, 0))),
        out_shape=jax.ShapeDtypeStruct((3, N_SHARD, Do, D), BF16),
        compiler_params=_cparams(("parallel",)),
    )(cidx, dw3, r3)


def _chips_exchange_in(cs_in):
    _, Dh, Wc = cs_in.shape

    def build(ins, outs, send, recv):
        pos = _position()
        return [_remote(ins[0].at[_shard_of(_peer(pos, k))], outs[0].at[j], send, recv, j, _peer(pos, k))
                for j, k in enumerate(CHIP_FLIPS)]

    return _Exchange((cs_in,), (jax.ShapeDtypeStruct((3, Dh, Wc), BF16),), 3, build)


def _chips_exchange_w3(cs_3):
    _, _, Do, D = cs_3.shape

    def build(ins, outs, send, recv):
        pos = _position()
        return [_remote(ins[0].at[:, _shard_of(_peer(pos, k))], outs[0].at[j], send, recv, j, _peer(pos, k))
                for j, k in enumerate(CHIP_FLIPS)]

    return _Exchange((cs_3,), (jax.ShapeDtypeStruct((3, 3, Do, D), BF16),), 3, build)


def _sum_chips_in(csidx, cs_in, rb_in):
    _, Dh, Wc = cs_in.shape
    tr = min(256, Dh)

    def body(s_ref, a_ref, b_ref, o_ref):
        acc = a_ref[...].astype(F32)
        for j in range(3):
            acc = acc + b_ref[j].astype(F32)
        o_ref[...] = acc

    return pl.pallas_call(
        body, name="sum_chips_in",
        grid_spec=pltpu.PrefetchScalarGridSpec(
            num_scalar_prefetch=1, grid=(Dh // tr,),
            in_specs=[pl.BlockSpec((None, tr, Wc), lambda i, s: (s[1], i, 0)),
                      pl.BlockSpec((3, tr, Wc), lambda i, s: (0, i, 0))],
            out_specs=pl.BlockSpec((None, tr, Wc), lambda i, s: (s[0], i, 0))),
        out_shape=jax.ShapeDtypeStruct((2, Dh, Wc), F32),
        compiler_params=_cparams(("parallel",)),
    )(csidx, cs_in, rb_in)


def _sum_chips_w3(csidx, cs_3, rb_3):
    _, _, Do, D = cs_3.shape

    def body(s_ref, a_ref, b_ref, o_ref):
        acc = a_ref[...].astype(F32)
        for j in range(3):
            acc = acc + b_ref[j].astype(F32)
        o_ref[...] = acc

    return pl.pallas_call(
        body, name="sum_chips_w3",
        grid_spec=pltpu.PrefetchScalarGridSpec(
            num_scalar_prefetch=1, grid=(3,),
            in_specs=[pl.BlockSpec((None, None, Do, D), lambda a, s: (a, s[1], 0, 0)),
                      pl.BlockSpec((3, None, Do, D), lambda a, s: (0, a, 0, 0))],
            out_specs=pl.BlockSpec((None, None, Do, D), lambda a, s: (a, s[0], 0, 0))),
        out_shape=jax.ShapeDtypeStruct((3, 2, Do, D), F32),
        compiler_params=_cparams(("parallel",)),
    )(csidx, cs_3, rb_3)


def _adam_math(w, g, m, v):
    m = ADAM_B1 * m + (1.0 - ADAM_B1) * g
    v = ADAM_B2 * v + (1.0 - ADAM_B2) * (g * g)
    m_hat = m / (1.0 - ADAM_B1 ** ADAM_STEP)
    v_hat = v / (1.0 - ADAM_B2 ** ADAM_STEP)
    delta = -ADAM_LR * (m_hat / (jnp.sqrt(v_hat) + ADAM_EPS) + ADAM_WD * w)
    return delta, m, v


def _adamw(w, g, m, v):
    R, C = w.shape
    tr = min(128, R)

    def body(w_ref, g_ref, m_ref, v_ref, d_ref, nm_ref, nv_ref):
        d_ref[...], nm_ref[...], nv_ref[...] = _adam_math(w_ref[...], g_ref[...], m_ref[...], v_ref[...])

    blk = pl.BlockSpec((tr, C), lambda i: (i, 0))
    return pl.pallas_call(
        body, name="adamw_w_in", grid=(R // tr,), in_specs=[blk] * 4, out_specs=[blk] * 3,
        out_shape=[jax.ShapeDtypeStruct((R, C), F32)] * 3,
        compiler_params=_cparams(("parallel",), VMEM_LIMIT),
    )(w, g, m, v)


SMALL_ROWS = ("c_ctx", "norm_w", "conv_b", "gn_w", "final_norm_w")


def _bwd_small(stats, ada_w, Dq, gh_in, gh_3):
    D = stats[0].shape[1]
    Wm = ada_w.shape[1]

    def body(stx, stm, stc, stv, stl, stlc, aw_hbm, gi_in, g3_in, tot_ref, dm_sh, gcw, da_ref, gi_ref, g3_ref,
             vec_ref, vbuf, dm, amine, abuf, aw_ref, s_v, r_v, s_a, r_a, s_g, r_g, s_aw):
        pos = _position()
        me, s = _dev_id(pos), _shard_of(pos)
        c, sib = pos[2], _peer(pos, 1)
        halves = [_remote(gi_in.at[c], gi_ref.at[c], s_g, r_g, 0, sib),
                  _remote(g3_in.at[:, c], g3_ref.at[:, c], s_g, r_g, 1, sib)]
        for cp in halves:
            cp.start()
        vec_ref[...] = jnp.zeros_like(vec_ref)
        vec_ref[0:2, :] = stx[0:2, :]
        vec_ref[2:3, :] = stm[1:2, :]
        vec_ref[3:5, :] = stc[0:2, :]
        vec_ref[5:6, :] = stx[2:3, :] + stc[2:3, :]
        vec_ref[6:7, :] = stv[3:4, :]
        vec_ref[7:8, :] = stm[3:4, :]
        vec_ref[8:9, :] = stm[0:1, :]
        vec_ref[9:12, :] = stv[0:3, :]
        vec_ref[12:14, 0:128] = stl[0:2, :] + stlc[0:2, :]
        vec_ref[14:15, :] = stm[2:3, :]
        vbuf[me] = vec_ref[...]
        sends = [_remote(vec_ref, vbuf.at[me], s_v, r_v, k - 1, _peer(pos, k)) for k in range(1, 8)]
        for cp in sends:
            cp.start()
        load_aw = pltpu.make_async_copy(aw_hbm, aw_ref, s_aw)
        load_aw.start()
        for k in range(1, 8):
            _remote(vec_ref, vbuf.at[_dev_id(_peer(pos, k))], s_v, r_v, k - 1, _peer(pos, k)).wait_recv()
        tot = vbuf[0]
        for d in range(1, N_DEV):
            tot = tot + vbuf[d]
        dm[...] = jnp.zeros_like(dm)
        for d in range(N_DEV):
            for r in range(3):
                dm[d:d + 1, r * D:(r + 1) * D] = vbuf[d, r:r + 1, :]
        dm[8:9, 0:D] = tot[3:4, :]
        dm[8:9, D:2 * D] = tot[4:5, :]
        for t in range(N_SHARD):
            @pl.when(s == t)
            def _(t=t):
                dm_sh[...] = dm[:, t * Wm:(t + 1) * Wm]
                gcw[...] = tot[9:12, t * Dq:(t + 1) * Dq]
        tot_ref[...] = tot
        load_aw.wait()
        part = lax.dot_general(dm_sh[8:16, :], aw_ref[...], (((1,), (1,)), ((), ())),
                               precision=lax.Precision.HIGHEST, preferred_element_type=F32)
        amine[...] = part
        abuf[s] = part
        asend = [_remote(amine, abuf.at[s], s_a, r_a, j, _peer(pos, k)) for j, k in enumerate(CHIP_FLIPS)]
        for cp in asend:
            cp.start()
        for j, k in enumerate(CHIP_FLIPS):
            _remote(amine, abuf.at[_shard_of(_peer(pos, k))], s_a, r_a, j, _peer(pos, k)).wait_recv()
        da = abuf[0]
        for t in range(1, N_SHARD):
            da = da + abuf[t]
        da_ref[...] = da
        _remote(gi_in.at[1 - c], gi_ref.at[1 - c], s_g, r_g, 0, sib).wait_recv()
        _remote(g3_in.at[:, 1 - c], g3_ref.at[:, 1 - c], s_g, r_g, 1, sib).wait_recv()
        for cp in sends + asend + halves:
            cp.wait_send()

    row = lambda *shape: jax.ShapeDtypeStruct(shape, F32)
    return pl.pallas_call(
        body, name="bwd_small",
        in_specs=[VMEM_FULL] * 6 + [ANY, ANY, ANY], out_specs=[VMEM_FULL] * 4 + [ANY, ANY],
        input_output_aliases={7: 4, 8: 5},
        out_shape=[row(16, D), row(16, Wm), row(3, Dq), row(8, D), row(*gh_in.shape), row(*gh_3.shape)],
        scratch_shapes=[pltpu.VMEM((16, D), F32), pltpu.VMEM((N_DEV, 16, D), F32), pltpu.VMEM((16, 3 * D), F32),
                        pltpu.VMEM((8, D), F32), pltpu.VMEM((N_SHARD, 8, D), F32), pltpu.VMEM((D, Wm), F32),
                        pltpu.SemaphoreType.DMA((7,)), pltpu.SemaphoreType.DMA((7,)),
                        pltpu.SemaphoreType.DMA((3,)), pltpu.SemaphoreType.DMA((3,)),
                        pltpu.SemaphoreType.DMA((2,)), pltpu.SemaphoreType.DMA((2,)), pltpu.SemaphoreType.DMA],
        compiler_params=_cparams(None, VMEM_LIMIT),
    )(*stats, ada_w, gh_in, gh_3)


ADAM_SLAB = 16


def _small_update(tot, dm_sh, gcw, da, act, g3, rows, ab, cw, dl, aw, w3):
    D = act.shape[1]
    Wm = dm_sh.shape[1]
    Dq = gcw.shape[1]
    H = dl[0].shape[1]
    params = tuple(rows) + (ab, cw, dl)
    shards = (aw,) + tuple(w3)
    n_p, n_s = len(params), len(shards)
    whole = (slice(None), slice(None))

    def body(tot_ref, dm_ref, gcw_ref, da_ref, act_ref, g3_hbm, *refs):
        wmv = [refs[3 * k:3 * k + 3] for k in range(n_p + n_s)]
        refs = refs[3 * (n_p + n_s):]
        gaw_ref, loss_ref = refs[0:2]
        outs, shard_outs = refs[2:2 + 4 * n_p], refs[2 + 4 * n_p:2 + 4 * n_p + 3 * n_s]
        g3_vm, *bufs, ld_sem, st_sem = refs[2 + 4 * n_p + 3 * n_s:]
        in_vm, out_vm = bufs[0:3 * n_s], bufs[3 * n_s:]
        o_q = [outs[n_p * q:n_p * (q + 1)] for q in range(4)]
        loads = [pltpu.make_async_copy(wmv[n_p + k][j], in_vm[3 * k + j], ld_sem.at[3 * k + j])
                 for k in range(n_s) for j in range(3)]
        load_g3 = pltpu.make_async_copy(g3_hbm, g3_vm, ld_sem.at[3 * n_s])
        for cp in loads + [load_g3]:
            cp.start()
        tot = tot_ref[...]
        gaw_ref[...] = lax.dot_general(act_ref[...], dm_ref[...], (((0,), (0,)), ((), ())),
                                       precision=lax.Precision.HIGHEST, preferred_element_type=F32)
        loss_ref[...] = (0.5 / D) * _sum_all(tot[14:15, :])
        cc = wmv[0][0][...]
        sg = _sigmoid(cc)
        g_cctx = da_ref[0:1, :] * (sg * (1.0 + cc * (1.0 - sg)))

        def emit(k, g, at=whole):
            w_ref, m_ref, v_ref = wmv[k]
            for q, val in enumerate((g,) + _adam_math(w_ref[at], g, m_ref[at], v_ref[at])):
                o_q[q][k][at] = val

        for k, g in enumerate([g_cctx, tot[5:6, :], tot[6:7, :], tot[7:8, :], tot[8:9, :]]):
            emit(k, g)
        for r, g in enumerate([tot[0:1, :] + tot[3:4, :], tot[1:2, :] + tot[4:5, :], tot[2:3, :]]):
            emit(n_p - 3, g, (slice(0, 1), slice(r * D, (r + 1) * D)))
        for r in range(3):
            emit(n_p - 2, gcw_ref[r:r + 1, :], (r, slice(None), slice(None)))
        emit(n_p - 1, tot[12:14, 0:H] * _sigmoid(-wmv[n_p - 1][0][...]))

        def shard_step(k, g_ref):
            w_ref, m_ref, v_ref = in_vm[3 * k:3 * k + 3]
            for cp in loads[3 * k:3 * k + 3]:
                cp.wait()

            def slab(i, carry):
                sl = pl.ds(pl.multiple_of(i * ADAM_SLAB, ADAM_SLAB), ADAM_SLAB)
                res = _adam_math(w_ref[sl, :], g_ref[sl, :], m_ref[sl, :], v_ref[sl, :])
                for o_ref, val in zip(out_vm[3 * k:3 * k + 3], res):
                    o_ref[sl, :] = val
                return carry

            lax.fori_loop(0, w_ref.shape[0] // ADAM_SLAB, slab, 0)
            stores = [pltpu.make_async_copy(out_vm[3 * k + j], shard_outs[3 * k + j], st_sem.at[3 * k + j])
                      for j in range(3)]
            for cp in stores:
                cp.start()
            return stores

        stores = shard_step(0, gaw_ref)
        load_g3.wait()
        for a in range(n_s - 1):
            stores += shard_step(1 + a, g3_vm.at[a])
        for cp in stores:
            cp.wait()

    row = lambda *shape: jax.ShapeDtypeStruct(shape, F32)
    per_q = [row(1, D)] * len(rows) + [row(1, 3 * D), row(3, 1, Dq), row(2, H)]
    per_shard = [row(*s[0].shape) for s in shards for _ in range(3)]
    held = [pltpu.VMEM(s[0].shape, F32) for s in shards for _ in range(3)]
    res = pl.pallas_call(
        body, name="small_update",
        in_specs=[VMEM_FULL] * 5 + [ANY] + [VMEM_FULL] * (3 * n_p) + [ANY] * (3 * n_s),
        out_specs=[VMEM_FULL] * (2 + 4 * n_p) + [ANY] * (3 * n_s),
        out_shape=[row(D, Wm), row(1, 1)] + per_q * 4 + per_shard,
        scratch_shapes=[pltpu.VMEM(g3.shape, F32)] + held + held
                       + [pltpu.SemaphoreType.DMA((3 * n_s + 1,)), pltpu.SemaphoreType.DMA((3 * n_s,))],
        compiler_params=_cparams(None, VMEM_LIMIT),
    )(tot, dm_sh, gcw, da, act, g3, *[a for p in params + shards for a in p])
    k = 2 + 4 * n_p
    return (res[0], res[1], [res[2 + n_p * q:2 + n_p * (q + 1)] for q in range(4)],
            [res[k + 3 * s:k + 3 * s + 3] for s in range(n_s)])


def kernel(x, c, ctx, c_ctx, norm_w, ada_w, ada_b, w_in, conv_w, conv_b, decay_logit, gn_w, w_a, w_b, w_out, final_norm_w, loss_target, m_c_ctx, m_norm_w, m_ada_w, m_ada_b, m_w_in, m_conv_w, m_conv_b, m_decay_logit, m_gn_w, m_w_a, m_w_b, m_w_out, m_final_norm_w, v_c_ctx, v_norm_w, v_ada_w, v_ada_b, v_w_in, v_conv_w, v_conv_b, v_decay_logit, v_gn_w, v_w_a, v_w_b, v_w_out, v_final_norm_w):
    L, D = x.shape[1], x.shape[2]
    Wc = w_in.shape[2]
    Do = D // 8
    pos = _position()
    cidx = jnp.reshape(pos[2], (1,)).astype(jnp.int32)
    sidx = jnp.reshape(_shard_of(pos), (1,)).astype(jnp.int32)

    act, mod_x, mod_c, conv_w8 = _fwd_small(c, c_ctx[None], ada_w[0], ada_b, conv_w[0])
    lg = jax.nn.log_sigmoid(decay_logit[0])

    w3_s = tuple(w[0].reshape(2, Do, D) for w in (w_a, w_b, w_out))

    def project(xm, c2, s2, w_in_bf):
        p, qr, kr, w_in_full, w3_full = _ag_in_proj(xm, w_in_bf, w3_s, c2, s2)
        return p, qr, kr, w_in_full, w3_full.reshape(3, D, D)

    csidx = jnp.concatenate([cidx, sidx])
    groups, dret_c, xmt, cmt, dx1, sc_x, w_in_full, gh_3, sts = _local_step(
        x[0], ctx[0], loss_target[0], mod_x, mod_c, norm_w, conv_w8, conv_b, lg, gn_w, final_norm_w[None],
        project, csidx, w_in[0])
    st_mid, st_conv, st_lg, st_lgc, st_c = sts

    dw_mine, ra_in = _dw_in(xmt, groups, cmt, dret_c, D, True)
    cs_in = _sum_pair_in(dw_mine, ra_in)
    grad_x, st_x, rb_in = _dxm(groups, 0, w_in_full, x[0], norm_w, sc_x, dx1, "dxm_x", _chips_exchange_in(cs_in))
    gh_in = _sum_chips_in(csidx, cs_in, rb_in)

    tot, dm_sh, gcw, da, g_in, g_3 = _bwd_small((st_x, st_mid, st_c, st_conv, st_lg, st_lgc), ada_w[0],
                                                conv_w.shape[2], gh_in, gh_3)
    g_w_in = g_in.reshape(D, Wc)
    g_3 = g_3.reshape(3, D // 4, D)
    rows = ((c_ctx[None], m_c_ctx[None], v_c_ctx[None]), (norm_w, m_norm_w, v_norm_w), (conv_b, m_conv_b, v_conv_b),
            (gn_w, m_gn_w, v_gn_w), (final_norm_w[None], m_final_norm_w[None], v_final_norm_w[None]))
    w3 = ((w_a[0], m_w_a[0], v_w_a[0]), (w_b[0], m_w_b[0], v_w_b[0]), (w_out[0], m_w_out[0], v_w_out[0]))
    g_ada_w, loss, small, (upd_ada, upd_a, upd_b, upd_o) = _small_update(
        tot, dm_sh, gcw, da, act, g_3, rows, (ada_b, m_ada_b, v_ada_b),
        tuple(jnp.transpose(a, (1, 0, 2)) for a in (conv_w, m_conv_w, v_conv_w)),
        (decay_logit[0], m_decay_logit[0], v_decay_logit[0]), (ada_w[0], m_ada_w[0], v_ada_w[0]), w3)

    upd_in = _adamw(w_in[0], g_w_in, m_w_in[0], v_w_in[0])

    def leaves(q):
        big = lambda g, upd: (g if q == 0 else upd[q - 1])[None]
        r_cctx, r_norm, r_convb, r_gn, r_fnorm, r_ab, r_cw, r_dl = small[q]
        return [r_cctx.reshape(D), r_norm, big(g_ada_w, upd_ada), r_ab, big(g_w_in, upd_in),
                jnp.transpose(r_cw, (1, 0, 2)), r_convb, r_dl[None], r_gn,
                big(g_3[0], upd_a), big(g_3[1], upd_b), big(g_3[2], upd_o), r_fnorm.reshape(D)]

    return (loss.reshape(()), grad_x[None], *leaves(0), *leaves(1), *leaves(2), *leaves(3))
```
